```python
import jax, jax.numpy as jnp
from jax import lax
import numpy as np

D_MODEL = 1024
BATCH = 8
SEQ = 2048
DEPTH = 1

MEM_LEN = 256
CONV_W = D_MODEL
CONV_K = 3
SB_HEADS = 8
SB_HEAD_DIM = 128
SB_W = SB_HEADS * SB_HEAD_DIM
Q_BLOCK = 128
X_HEADS = 4
X_HEAD_DIM = D_MODEL // X_HEADS
D_FF = 2816
RMS_EPS = 1e-6
PROJ_SIZES = (CONV_W, CONV_W, CONV_W, SB_W, SB_W, SB_W, D_MODEL, D_MODEL)

kernel_name = 'hybrid_conv_stickbreak_macaron_layer'


def rms_norm(x, g):
    xf = x.astype(jnp.float32)
    y = xf * lax.rsqrt(jnp.mean(xf * xf, axis=-1, keepdims=True) + RMS_EPS)
    return (y * g.astype(jnp.float32)).astype(x.dtype)


def swiglu(x, w_gu, w_down):
    gate, up = jnp.split(x @ w_gu, 2, axis=-1)
    return (jax.nn.silu(gate) * up) @ w_down


def short_conv(x, w):
    c = x.shape[-1]
    return lax.conv_general_dilated(
        x, w[:, None, :].astype(x.dtype), window_strides=(1,), padding=[(CONV_K - 1, 0)],
        dimension_numbers=('NWC', 'WIO', 'NWC'), feature_group_count=c)


def stick_breaking_attention(q, k, v):
    seq = q.shape[2]
    scale = SB_HEAD_DIM ** -0.5
    outs = []
    for blk in range(seq // Q_BLOCK):
        start = blk * Q_BLOCK
        end = start + Q_BLOCK
        z = jnp.einsum('bhqd,bhkd->bhqk', q[:, :, start:end], k[:, :, :end]).astype(jnp.float32) * scale
        t_pos = start + jnp.arange(Q_BLOCK)[:, None]
        s_pos = jnp.arange(end)[None, :]
        causal = s_pos < t_pos
        log_1m_beta = jnp.where(causal, jax.nn.log_sigmoid(-z), 0.0)
        after = lax.cumsum(log_1m_beta, axis=3, reverse=True) - log_1m_beta
        a = jnp.where(causal, jnp.exp(jax.nn.log_sigmoid(z) + after), 0.0)
        outs.append(jnp.einsum('bhqk,bhkd->bhqd', a, v[:, :, :end].astype(jnp.float32)))
    return jnp.concatenate(outs, axis=2).astype(q.dtype)


def memory_cross_attention(hn, mn, w_cq, w_ckv, w_co):
    b, s, _ = hn.shape
    m = mn.shape[1]
    q = (hn @ w_cq).reshape(b, s, X_HEADS, X_HEAD_DIM)
    k, v = jnp.split(mn @ w_ckv, 2, axis=-1)
    k = k.reshape(b, m, X_HEADS, X_HEAD_DIM)
    v = v.reshape(b, m, X_HEADS, X_HEAD_DIM)
    scores = jnp.einsum('bshd,bmhd->bhsm', q, k).astype(jnp.float32) * (X_HEAD_DIM ** -0.5)
    p = jax.nn.softmax(scores, axis=-1)
    o = jnp.einsum('bhsm,bmhd->bshd', p, v.astype(jnp.float32)).astype(hn.dtype)
    return o.reshape(b, s, D_MODEL) @ w_co


def hybrid_mixer(u, w_in, b_gate, conv_w, w_conv_out, w_attn_out, w_o):
    b, s, _ = u.shape
    split_at = np.cumsum(PROJ_SIZES)[:6].tolist()
    cb, cc, cx, q, k, v, gates = jnp.split(u @ w_in, split_at, axis=-1)
    gate_pre = gates + b_gate
    g_conv, g_sb = jnp.split(jax.nn.sigmoid(gate_pre), 2, axis=-1)
    y_conv = cb * short_conv(cc * cx, conv_w)
    to_heads = lambda t: t.reshape(b, s, SB_HEADS, SB_HEAD_DIM).transpose(0, 2, 1, 3)
    y_sb = stick_breaking_attention(to_heads(q), to_heads(k), to_heads(v))
    y_sb = y_sb.transpose(0, 2, 1, 3).reshape(b, s, SB_W)
    merged = g_conv * (y_conv @ w_conv_out) + g_sb * (y_sb @ w_attn_out)
    return merged @ w_o


def _fwd_setup_inputs(seed: int = 0) -> dict:
    key = jax.random.key(seed)
    ks = jax.random.split(key, 21)
    f32 = jnp.float32

    def dense(k, shape):
        return jax.random.normal(k, shape, f32) * (shape[0] ** -0.5)

    def gain(k):
        return 1.0 + 0.01 * jax.random.normal(k, (D_MODEL,), f32)

    return {
        'x': jax.random.normal(ks[0], (BATCH, SEQ, D_MODEL), f32),
        'mem': jax.random.normal(ks[1], (BATCH, MEM_LEN, D_MODEL), f32),
        'g_ffn1': gain(ks[2]),
        'w_ffn1_gu': dense(ks[3], (D_MODEL, 2 * D_FF)),
        'w_ffn1_down': dense(ks[4], (D_FF, D_MODEL)),
        'g_mix': gain(ks[5]),
        'w_in': dense(ks[6], (D_MODEL, sum(PROJ_SIZES))),
        'b_gate': 0.01 * jax.random.normal(ks[7], (2 * D_MODEL,), f32),
        'conv_w': jax.random.normal(ks[8], (CONV_K, CONV_W), f32) * (CONV_K ** -0.5),
        'w_conv_out': dense(ks[9], (CONV_W, D_MODEL)),
        'w_attn_out': dense(ks[10], (SB_W, D_MODEL)),
        'w_o': dense(ks[11], (D_MODEL, D_MODEL)),
        'g_cross': gain(ks[12]),
        'g_mem': gain(ks[13]),
        'w_cq': dense(ks[14], (D_MODEL, D_MODEL)),
        'w_ckv': dense(ks[15], (D_MODEL, 2 * D_MODEL)),
        'w_co': dense(ks[16], (D_MODEL, D_MODEL)),
        'g_ffn2': gain(ks[17]),
        'w_ffn2_gu': dense(ks[18], (D_MODEL, 2 * D_FF)),
        'w_ffn2_down': dense(ks[19], (D_FF, D_MODEL)),
        'g_final': gain(ks[20]),
    }


def _fwd_reference(x, mem, g_ffn1, w_ffn1_gu, w_ffn1_down, g_mix, w_in, b_gate, conv_w,
              w_conv_out, w_attn_out, w_o, g_cross, g_mem, w_cq, w_ckv, w_co,
              g_ffn2, w_ffn2_gu, w_ffn2_down, g_final):
    h = x
    for _ in range(DEPTH):
        h = h + 0.5 * swiglu(rms_norm(h, g_ffn1), w_ffn1_gu, w_ffn1_down)
        h = h + hybrid_mixer(rms_norm(h, g_mix), w_in, b_gate, conv_w, w_conv_out, w_attn_out, w_o)
        h = h + memory_cross_attention(rms_norm(h, g_cross), rms_norm(mem, g_mem), w_cq, w_ckv, w_co)
        h = h + 0.5 * swiglu(rms_norm(h, g_ffn2), w_ffn2_gu, w_ffn2_down)
    return rms_norm(h, g_final)


import jax as _jax
import jax.numpy as _jnp

TWIN_FORMAT = 'train_step'
FWD_PARAMS = ['x', 'mem', 'g_ffn1', 'w_ffn1_gu', 'w_ffn1_down', 'g_mix', 'w_in', 'b_gate', 'conv_w', 'w_conv_out', 'w_attn_out', 'w_o', 'g_cross', 'g_mem', 'w_cq', 'w_ckv', 'w_co', 'g_ffn2', 'w_ffn2_gu', 'w_ffn2_down', 'g_final']
TWIN_WEIGHTS = ['g_ffn1', 'w_ffn1_gu', 'w_ffn1_down', 'g_mix', 'w_in', 'b_gate', 'conv_w', 'w_conv_out', 'w_attn_out', 'w_o', 'g_cross', 'g_mem', 'w_cq', 'w_ckv', 'w_co', 'g_ffn2', 'w_ffn2_gu', 'w_ffn2_down', 'g_final']
TWIN_DIFF_INPUT = 'x'
TWIN_INPUTS = ['x', 'mem', 'g_ffn1', 'w_ffn1_gu', 'w_ffn1_down', 'g_mix', 'w_in', 'b_gate', 'conv_w', 'w_conv_out', 'w_attn_out', 'w_o', 'g_cross', 'g_mem', 'w_cq', 'w_ckv', 'w_co', 'g_ffn2', 'w_ffn2_gu', 'w_ffn2_down', 'g_final', 'loss_target', 'm_g_ffn1', 'm_w_ffn1_gu', 'm_w_ffn1_down', 'm_g_mix', 'm_w_in', 'm_b_gate', 'm_conv_w', 'm_w_conv_out', 'm_w_attn_out', 'm_w_o', 'm_g_cross', 'm_g_mem', 'm_w_cq', 'm_w_ckv', 'm_w_co', 'm_g_ffn2', 'm_w_ffn2_gu', 'm_w_ffn2_down', 'm_g_final', 'v_g_ffn1', 'v_w_ffn1_gu', 'v_w_ffn1_down', 'v_g_mix', 'v_w_in', 'v_b_gate', 'v_conv_w', 'v_w_conv_out', 'v_w_attn_out', 'v_w_o', 'v_g_cross', 'v_g_mem', 'v_w_cq', 'v_w_ckv', 'v_w_co', 'v_g_ffn2', 'v_w_ffn2_gu', 'v_w_ffn2_down', 'v_g_final']
TWIN_OUTPUTS = ['loss', 'grad_x', 'grad_g_ffn1', 'grad_w_ffn1_gu', 'grad_w_ffn1_down', 'grad_g_mix', 'grad_w_in', 'grad_b_gate', 'grad_conv_w', 'grad_w_conv_out', 'grad_w_attn_out', 'grad_w_o', 'grad_g_cross', 'grad_g_mem', 'grad_w_cq', 'grad_w_ckv', 'grad_w_co', 'grad_g_ffn2', 'grad_w_ffn2_gu', 'grad_w_ffn2_down', 'grad_g_final', 'delta_g_ffn1', 'delta_w_ffn1_gu', 'delta_w_ffn1_down', 'delta_g_mix', 'delta_w_in', 'delta_b_gate', 'delta_conv_w', 'delta_w_conv_out', 'delta_w_attn_out', 'delta_w_o', 'delta_g_cross', 'delta_g_mem', 'delta_w_cq', 'delta_w_ckv', 'delta_w_co', 'delta_g_ffn2', 'delta_w_ffn2_gu', 'delta_w_ffn2_down', 'delta_g_final', 'new_m_g_ffn1', 'new_m_w_ffn1_gu', 'new_m_w_ffn1_down', 'new_m_g_mix', 'new_m_w_in', 'new_m_b_gate', 'new_m_conv_w', 'new_m_w_conv_out', 'new_m_w_attn_out', 'new_m_w_o', 'new_m_g_cross', 'new_m_g_mem', 'new_m_w_cq', 'new_m_w_ckv', 'new_m_w_co', 'new_m_g_ffn2', 'new_m_w_ffn2_gu', 'new_m_w_ffn2_down', 'new_m_g_final', 'new_v_g_ffn1', 'new_v_w_ffn1_gu', 'new_v_w_ffn1_down', 'new_v_g_mix', 'new_v_w_in', 'new_v_b_gate', 'new_v_conv_w', 'new_v_w_conv_out', 'new_v_w_attn_out', 'new_v_w_o', 'new_v_g_cross', 'new_v_g_mem', 'new_v_w_cq', 'new_v_w_ckv', 'new_v_w_co', 'new_v_g_ffn2', 'new_v_w_ffn2_gu', 'new_v_w_ffn2_down', 'new_v_g_final']
TWIN_LEAF_KINDS = {'loss': 'loss', 'grad_x': 'grad_x', 'grad_g_ffn1': 'grad_w', 'grad_w_ffn1_gu': 'grad_w', 'grad_w_ffn1_down': 'grad_w', 'grad_g_mix': 'grad_w', 'grad_w_in': 'grad_w', 'grad_b_gate': 'grad_w', 'grad_conv_w': 'grad_w', 'grad_w_conv_out': 'grad_w', 'grad_w_attn_out': 'grad_w', 'grad_w_o': 'grad_w', 'grad_g_cross': 'grad_w', 'grad_g_mem': 'grad_w', 'grad_w_cq': 'grad_w', 'grad_w_ckv': 'grad_w', 'grad_w_co': 'grad_w', 'grad_g_ffn2': 'grad_w', 'grad_w_ffn2_gu': 'grad_w', 'grad_w_ffn2_down': 'grad_w', 'grad_g_final': 'grad_w', 'delta_g_ffn1': 'delta_w', 'delta_w_ffn1_gu': 'delta_w', 'delta_w_ffn1_down': 'delta_w', 'delta_g_mix': 'delta_w', 'delta_w_in': 'delta_w', 'delta_b_gate': 'delta_w', 'delta_conv_w': 'delta_w', 'delta_w_conv_out': 'delta_w', 'delta_w_attn_out': 'delta_w', 'delta_w_o': 'delta_w', 'delta_g_cross': 'delta_w', 'delta_g_mem': 'delta_w', 'delta_w_cq': 'delta_w', 'delta_w_ckv': 'delta_w', 'delta_w_co': 'delta_w', 'delta_g_ffn2': 'delta_w', 'delta_w_ffn2_gu': 'delta_w', 'delta_w_ffn2_down': 'delta_w', 'delta_g_final': 'delta_w', 'new_m_g_ffn1': 'new_m', 'new_m_w_ffn1_gu': 'new_m', 'new_m_w_ffn1_down': 'new_m', 'new_m_g_mix': 'new_m', 'new_m_w_in': 'new_m', 'new_m_b_gate': 'new_m', 'new_m_conv_w': 'new_m', 'new_m_w_conv_out': 'new_m', 'new_m_w_attn_out': 'new_m', 'new_m_w_o': 'new_m', 'new_m_g_cross': 'new_m', 'new_m_g_mem': 'new_m', 'new_m_w_cq': 'new_m', 'new_m_w_ckv': 'new_m', 'new_m_w_co': 'new_m', 'new_m_g_ffn2': 'new_m', 'new_m_w_ffn2_gu': 'new_m', 'new_m_w_ffn2_down': 'new_m', 'new_m_g_final': 'new_m', 'new_v_g_ffn1': 'new_v', 'new_v_w_ffn1_gu': 'new_v', 'new_v_w_ffn1_down': 'new_v', 'new_v_g_mix': 'new_v', 'new_v_w_in': 'new_v', 'new_v_b_gate': 'new_v', 'new_v_conv_w': 'new_v', 'new_v_w_conv_out': 'new_v', 'new_v_w_attn_out': 'new_v', 'new_v_w_o': 'new_v', 'new_v_g_cross': 'new_v', 'new_v_g_mem': 'new_v', 'new_v_w_cq': 'new_v', 'new_v_w_ckv': 'new_v', 'new_v_w_co': 'new_v', 'new_v_g_ffn2': 'new_v', 'new_v_w_ffn2_gu': 'new_v', 'new_v_w_ffn2_down': 'new_v', 'new_v_g_final': 'new_v'}


def _forward(args):
    return _fwd_reference(*[args[k] for k in FWD_PARAMS])


def _output_shape():
    out = _jax.eval_shape(lambda: _forward(_fwd_setup_inputs(0)))
    return out.shape, out.dtype

N_MICROBATCH = 1
ADAM_LR = 0.001
ADAM_B1 = 0.9
ADAM_B2 = 0.999
ADAM_EPS = 1e-08
ADAM_WD = 0.01
ADAM_STEP = 10
PER_EXAMPLE_BATCH_AXIS = {'x': 0, 'mem': 0, 'loss_target': 0}
SHARED_INPUTS = []
_WEIGHT_DTYPES = {'g_ffn1': _jnp.float32, 'w_ffn1_gu': _jnp.float32, 'w_ffn1_down': _jnp.float32, 'g_mix': _jnp.float32, 'w_in': _jnp.float32, 'b_gate': _jnp.float32, 'conv_w': _jnp.float32, 'w_conv_out': _jnp.float32, 'w_attn_out': _jnp.float32, 'w_o': _jnp.float32, 'g_cross': _jnp.float32, 'g_mem': _jnp.float32, 'w_cq': _jnp.float32, 'w_ckv': _jnp.float32, 'w_co': _jnp.float32, 'g_ffn2': _jnp.float32, 'w_ffn2_gu': _jnp.float32, 'w_ffn2_down': _jnp.float32, 'g_final': _jnp.float32}
MOMENT_SCALE = {'g_ffn1': 6.581798e-02, 'w_ffn1_gu': 2.727555e-02, 'w_ffn1_down': 4.452713e-02, 'g_mix': 1.087338e-01, 'w_in': 3.901860e-02, 'b_gate': 1.840513e-02, 'conv_w': 5.718026e-02, 'w_conv_out': 5.631519e-02, 'w_attn_out': 3.709324e-02, 'w_o': 6.767077e-02, 'g_cross': 1.134422e-02, 'g_mem': 1.674202e-02, 'w_cq': 1.098908e-02, 'w_ckv': 1.103125e-02, 'w_co': 1.117366e-02, 'g_ffn2': 4.278778e-02, 'w_ffn2_gu': 1.816592e-02, 'w_ffn2_down': 2.964744e-02, 'g_final': 1.599616e+01}


def _to_microbatches(a, axis):
    t = _jnp.moveaxis(a, axis, 0)
    t = t.reshape((N_MICROBATCH, t.shape[0] // N_MICROBATCH) + t.shape[1:])
    return _jnp.moveaxis(t, 1, axis + 1)


def setup_inputs(seed: int = 0) -> dict:
    inp = _fwd_setup_inputs(seed)
    key = _jax.random.fold_in(_jax.random.key(seed), 7919)
    shape, _ = _output_shape()
    out = dict(inp)
    out["loss_target"] = _jax.random.normal(_jax.random.fold_in(key, 0), shape, _jnp.float32)
    for i, name in enumerate(TWIN_WEIGHTS):
        w = inp[name].astype(_jnp.float32)
        if MOMENT_SCALE is None:
            s = _jnp.sqrt(_jnp.mean(_jnp.square(w)) + 1e-30)
        else:
            s = MOMENT_SCALE[name]
        km, kv = _jax.random.split(_jax.random.fold_in(key, i + 1))
        out[name] = w
        out["m_" + name] = s * _jax.random.normal(km, w.shape, _jnp.float32)
        out["v_" + name] = (s * s) * _jax.random.uniform(kv, w.shape, _jnp.float32, 0.5, 1.5)
    if N_MICROBATCH > 1:
        for name, axis in PER_EXAMPLE_BATCH_AXIS.items():
            out[name] = _to_microbatches(out[name], axis)
    return {'x': out['x'], 'mem': out['mem'], 'g_ffn1': out['g_ffn1'], 'w_ffn1_gu': out['w_ffn1_gu'], 'w_ffn1_down': out['w_ffn1_down'], 'g_mix': out['g_mix'], 'w_in': out['w_in'], 'b_gate': out['b_gate'], 'conv_w': out['conv_w'], 'w_conv_out': out['w_conv_out'], 'w_attn_out': out['w_attn_out'], 'w_o': out['w_o'], 'g_cross': out['g_cross'], 'g_mem': out['g_mem'], 'w_cq': out['w_cq'], 'w_ckv': out['w_ckv'], 'w_co': out['w_co'], 'g_ffn2': out['g_ffn2'], 'w_ffn2_gu': out['w_ffn2_gu'], 'w_ffn2_down': out['w_ffn2_down'], 'g_final': out['g_final'], 'loss_target': out['loss_target'], 'm_g_ffn1': out['m_g_ffn1'], 'm_w_ffn1_gu': out['m_w_ffn1_gu'], 'm_w_ffn1_down': out['m_w_ffn1_down'], 'm_g_mix': out['m_g_mix'], 'm_w_in': out['m_w_in'], 'm_b_gate': out['m_b_gate'], 'm_conv_w': out['m_conv_w'], 'm_w_conv_out': out['m_w_conv_out'], 'm_w_attn_out': out['m_w_attn_out'], 'm_w_o': out['m_w_o'], 'm_g_cross': out['m_g_cross'], 'm_g_mem': out['m_g_mem'], 'm_w_cq': out['m_w_cq'], 'm_w_ckv': out['m_w_ckv'], 'm_w_co': out['m_w_co'], 'm_g_ffn2': out['m_g_ffn2'], 'm_w_ffn2_gu': out['m_w_ffn2_gu'], 'm_w_ffn2_down': out['m_w_ffn2_down'], 'm_g_final': out['m_g_final'], 'v_g_ffn1': out['v_g_ffn1'], 'v_w_ffn1_gu': out['v_w_ffn1_gu'], 'v_w_ffn1_down': out['v_w_ffn1_down'], 'v_g_mix': out['v_g_mix'], 'v_w_in': out['v_w_in'], 'v_b_gate': out['v_b_gate'], 'v_conv_w': out['v_conv_w'], 'v_w_conv_out': out['v_w_conv_out'], 'v_w_attn_out': out['v_w_attn_out'], 'v_w_o': out['v_w_o'], 'v_g_cross': out['v_g_cross'], 'v_g_mem': out['v_g_mem'], 'v_w_cq': out['v_w_cq'], 'v_w_ckv': out['v_w_ckv'], 'v_w_co': out['v_w_co'], 'v_g_ffn2': out['v_g_ffn2'], 'v_w_ffn2_gu': out['v_w_ffn2_gu'], 'v_w_ffn2_down': out['v_w_ffn2_down'], 'v_g_final': out['v_g_final']}


def _loss(weights, diff, rest, loss_target):
    with _jax.named_scope("forward"):
        args = {**rest, TWIN_DIFF_INPUT: diff, **{k: w.astype(_WEIGHT_DTYPES[k]) for k, w in weights.items()}}
        y = _forward(args)
    with _jax.named_scope("loss_head"):
        err = _jnp.square(y.astype(_jnp.float32) - loss_target)
        return 0.5 * _jnp.sum(_jnp.mean(err, axis=-1)) if err.ndim else 0.5 * err


def _adamw(w, g, m, v):
    m = ADAM_B1 * m + (1.0 - ADAM_B1) * g
    v = ADAM_B2 * v + (1.0 - ADAM_B2) * _jnp.square(g)
    m_hat = m / (1.0 - ADAM_B1 ** ADAM_STEP)
    v_hat = v / (1.0 - ADAM_B2 ** ADAM_STEP)
    delta = -ADAM_LR * (m_hat / (_jnp.sqrt(v_hat) + ADAM_EPS) + ADAM_WD * w)
    return delta, m, v


def reference(x, mem, g_ffn1, w_ffn1_gu, w_ffn1_down, g_mix, w_in, b_gate, conv_w, w_conv_out, w_attn_out, w_o, g_cross, g_mem, w_cq, w_ckv, w_co, g_ffn2, w_ffn2_gu, w_ffn2_down, g_final, loss_target, m_g_ffn1, m_w_ffn1_gu, m_w_ffn1_down, m_g_mix, m_w_in, m_b_gate, m_conv_w, m_w_conv_out, m_w_attn_out, m_w_o, m_g_cross, m_g_mem, m_w_cq, m_w_ckv, m_w_co, m_g_ffn2, m_w_ffn2_gu, m_w_ffn2_down, m_g_final, v_g_ffn1, v_w_ffn1_gu, v_w_ffn1_down, v_g_mix, v_w_in, v_b_gate, v_conv_w, v_w_conv_out, v_w_attn_out, v_w_o, v_g_cross, v_g_mem, v_w_cq, v_w_ckv, v_w_co, v_g_ffn2, v_w_ffn2_gu, v_w_ffn2_down, v_g_final):
    given = dict(x=x, mem=mem, g_ffn1=g_ffn1, w_ffn1_gu=w_ffn1_gu, w_ffn1_down=w_ffn1_down, g_mix=g_mix, w_in=w_in, b_gate=b_gate, conv_w=conv_w, w_conv_out=w_conv_out, w_attn_out=w_attn_out, w_o=w_o, g_cross=g_cross, g_mem=g_mem, w_cq=w_cq, w_ckv=w_ckv, w_co=w_co, g_ffn2=g_ffn2, w_ffn2_gu=w_ffn2_gu, w_ffn2_down=w_ffn2_down, g_final=g_final, loss_target=loss_target, m_g_ffn1=m_g_ffn1, m_w_ffn1_gu=m_w_ffn1_gu, m_w_ffn1_down=m_w_ffn1_down, m_g_mix=m_g_mix, m_w_in=m_w_in, m_b_gate=m_b_gate, m_conv_w=m_conv_w, m_w_conv_out=m_w_conv_out, m_w_attn_out=m_w_attn_out, m_w_o=m_w_o, m_g_cross=m_g_cross, m_g_mem=m_g_mem, m_w_cq=m_w_cq, m_w_ckv=m_w_ckv, m_w_co=m_w_co, m_g_ffn2=m_g_ffn2, m_w_ffn2_gu=m_w_ffn2_gu, m_w_ffn2_down=m_w_ffn2_down, m_g_final=m_g_final, v_g_ffn1=v_g_ffn1, v_w_ffn1_gu=v_w_ffn1_gu, v_w_ffn1_down=v_w_ffn1_down, v_g_mix=v_g_mix, v_w_in=v_w_in, v_b_gate=v_b_gate, v_conv_w=v_conv_w, v_w_conv_out=v_w_conv_out, v_w_attn_out=v_w_attn_out, v_w_o=v_w_o, v_g_cross=v_g_cross, v_g_mem=v_g_mem, v_w_cq=v_w_cq, v_w_ckv=v_w_ckv, v_w_co=v_w_co, v_g_ffn2=v_g_ffn2, v_w_ffn2_gu=v_w_ffn2_gu, v_w_ffn2_down=v_w_ffn2_down, v_g_final=v_g_final)
    weights = {n: given[n] for n in TWIN_WEIGHTS}
    shared = {n: given[n] for n in SHARED_INPUTS}
    per_example = {n: given[n] for n in ['x', 'mem']}
    grad_fn = _jax.value_and_grad(_loss, argnums=(0, 1))

    def one_microbatch(ex, loss_target):
        ex = dict(ex)
        diff = ex.pop(TWIN_DIFF_INPUT)
        return grad_fn(weights, diff, {**shared, **ex}, loss_target)

    if N_MICROBATCH == 1:
        loss, (grad_w, grad_x) = one_microbatch(per_example, given["loss_target"])
    else:
        def body(carry, xs):
            loss_sum, grad_sum = carry
            l_k, (gw_k, gx_k) = one_microbatch(xs[0], xs[1])
            with _jax.named_scope("update"):
                return (loss_sum + l_k, _jax.tree.map(_jnp.add, grad_sum, gw_k)), gx_k

        init = (_jnp.zeros((), _jnp.float32), _jax.tree.map(_jnp.zeros_like, weights))
        (loss, grad_w), grad_x = _jax.lax.scan(body, init, (per_example, given["loss_target"]))
    with _jax.named_scope("update"):
        delta_w, new_m, new_v = {}, {}, {}
        for n in TWIN_WEIGHTS:
            delta_w[n], new_m[n], new_v[n] = _adamw(weights[n], grad_w[n], given["m_" + n], given["v_" + n])
    return (loss, grad_x, *[grad_w[n] for n in TWIN_WEIGHTS], *[delta_w[n] for n in TWIN_WEIGHTS],
            *[new_m[n] for n in TWIN_WEIGHTS], *[new_v[n] for n in TWIN_WEIGHTS])
```

```python
import functools

import jax
import jax.numpy as jnp
from jax import lax
from jax.experimental import pallas as pl
from jax.experimental.pallas import tpu as pltpu

F32 = jnp.float32
BF16 = jnp.bfloat16

D = 1024
DFF = 2816
NPROJ = 8192
SB_H = 8
SB_DH = 128
X_H = 4
X_DH = 256
CONV_K = 3
RMS_EPS = 1e-6
N_DEV = 8

ADAM_LR = 0.001
ADAM_B1 = 0.9
ADAM_B2 = 0.999
ADAM_EPS = 1e-08
ADAM_WD = 0.01
ADAM_STEP = 10

TM = 256
TQ = 256
FF_CHUNK = 256
VMEM_LIMIT = 56 << 20

MESH_AXES = ("x", "y", "c")
_ANY = pl.BlockSpec(memory_space=pl.ANY)

BIG = (
    ("w_ffn1_gu", (D, 2 * DFF), 1),
    ("w_ffn1_down", (DFF, D), 0),
    ("w_in", (D, NPROJ), 1),
    ("w_conv_out", (D, D), 0),
    ("w_attn_out", (D, D), 0),
    ("w_o", (D, D), 0),
    ("w_cq", (D, D), 0),
    ("w_ckv", (D, 2 * D), 1),
    ("w_co", (D, D), 0),
    ("w_ffn2_gu", (D, 2 * DFF), 1),
    ("w_ffn2_down", (DFF, D), 0),
)
PACK_W = 1024


def _shard_shape(shape, axis):
    s = list(shape)
    s[axis] //= N_DEV
    return tuple(s)


def _pack_rows(shape, axis):
    a, b = _shard_shape(shape, axis)
    return (a * b) // PACK_W


PACK_R = sum(_pack_rows(s, ax) for _, s, ax in BIG)


def _cparams(n_axes=1):
    return pltpu.CompilerParams(
        dimension_semantics=("arbitrary",) * n_axes, vmem_limit_bytes=VMEM_LIMIT)


def _row_spec(tm, n):
    return pl.BlockSpec((tm, n), lambda i: (i, 0))


def _const_spec(shape):
    zeros = (0,) * len(shape)
    return pl.BlockSpec(shape, lambda i: zeros)


def _dot(a, b):
    return jnp.dot(a, b, preferred_element_type=F32)


def _dot_nt(a, b):
    return lax.dot_general(a, b, (((1,), (1,)), ((), ())), preferred_element_type=F32)


def _dot_tn(a, b):
    return lax.dot_general(a, b, (((0,), (0,)), ((), ())), preferred_element_type=F32)


def _sigmoid(x):
    return 1.0 / (1.0 + jnp.exp(-x))


def _load_resident(step, pairs, sems):
    @pl.when(step == 0)
    def _():
        copies = [pltpu.make_async_copy(src, dst, sems.at[k]) for k, (src, dst) in enumerate(pairs)]
        for cp in copies:
            cp.start()
        for cp in copies:
            cp.wait()


def _rms_fwd_tile(xt, g):
    r = lax.rsqrt(jnp.mean(xt * xt, axis=-1, keepdims=True) + RMS_EPS)
    return (xt * r) * g


def _rms_bwd_tile(xt, g, dn):
    r = lax.rsqrt(jnp.mean(xt * xt, axis=-1, keepdims=True) + RMS_EPS)
    xhat = xt * r
    dxhat = dn * g
    dx = r * (dxhat - xhat * jnp.mean(dxhat * xhat, axis=-1, keepdims=True))
    dg = jnp.sum(dn * xhat, axis=0, keepdims=True)
    return dx, dg


def _accumulate(ref, step, value):
    @pl.when(step == 0)
    def _():
        ref[...] = value

    @pl.when(step != 0)
    def _():
        ref[...] = ref[...] + value


def _ffn_fwd(x, g, wgu, wd, name):
    t = x.shape[0]
    n_chunk = DFF // FF_CHUNK

    def body(x_ref, g_ref, wgu_hbm, wd_hbm, n_ref, gate_ref, up_ref, act_ref, h_ref, wgu_v, wd_v, sems):
        _load_resident(pl.program_id(0), [(wgu_hbm, wgu_v), (wd_hbm, wd_v)], sems)
        xt = x_ref[...]
        n = _rms_fwd_tile(xt, g_ref[...]).astype(BF16)
        n_ref[...] = n
        acc = jnp.zeros((TM, D), F32)
        for c in range(n_chunk):
            lo, hi = c * FF_CHUNK, (c + 1) * FF_CHUNK
            gt = _dot(n, wgu_v[:, lo:hi])
            ut = _dot(n, wgu_v[:, DFF + lo:DFF + hi])
            act = ((gt * _sigmoid(gt)) * ut).astype(BF16)
            gate_ref[:, lo:hi] = gt.astype(BF16)
            up_ref[:, lo:hi] = ut.astype(BF16)
            act_ref[:, lo:hi] = act
            acc = acc + _dot(act, wd_v[lo:hi, :])
        h_ref[...] = xt + 0.5 * acc

    return pl.pallas_call(
        body, grid=(t // TM,), name=name,
        in_specs=[_row_spec(TM, D), _const_spec((1, D)), _ANY, _ANY],
        out_specs=[_row_spec(TM, D), _row_spec(TM, DFF), _row_spec(TM, DFF), _row_spec(TM, DFF), _row_spec(TM, D)],
        out_shape=[jax.ShapeDtypeStruct((t, D), BF16), jax.ShapeDtypeStruct((t, DFF), BF16),
                   jax.ShapeDtypeStruct((t, DFF), BF16), jax.ShapeDtypeStruct((t, DFF), BF16),
                   jax.ShapeDtypeStruct((t, D), F32)],
        scratch_shapes=[pltpu.VMEM(wgu.shape, BF16), pltpu.VMEM(wd.shape, BF16), pltpu.SemaphoreType.DMA((2,))],
        compiler_params=_cparams(),
    )(x, g, wgu, wd)


def _ffn_bwd(dh, xin, g, gate, up, wgu, wd, name):
    t = dh.shape[0]
    n_chunk = DFF // FF_CHUNK

    def body(dh_ref, x_ref, g_ref, gate_ref, up_ref, wgu_hbm, wd_hbm,
             dgu_ref, dhb_ref, dx_ref, dg_ref, wgu_v, wd_v, sems):
        step = pl.program_id(0)
        _load_resident(step, [(wgu_hbm, wgu_v), (wd_hbm, wd_v)], sems)
        dht = dh_ref[...]
        dhb = (0.5 * dht).astype(BF16)
        dhb_ref[...] = dhb
        dn = jnp.zeros((TM, D), F32)
        for c in range(n_chunk):
            lo, hi = c * FF_CHUNK, (c + 1) * FF_CHUNK
            da = _dot_nt(dhb, wd_v[lo:hi, :])
            gt = gate_ref[:, lo:hi].astype(F32)
            ut = up_ref[:, lo:hi].astype(F32)
            sg = _sigmoid(gt)
            dgt = (da * ut * (sg * (1.0 + gt * (1.0 - sg)))).astype(BF16)
            dut = (da * (gt * sg)).astype(BF16)
            dgu_ref[:, lo:hi] = dgt
            dgu_ref[:, DFF + lo:DFF + hi] = dut
            dn = dn + _dot_nt(dgt, wgu_v[:, lo:hi]) + _dot_nt(dut, wgu_v[:, DFF + lo:DFF + hi])
        dx, dg = _rms_bwd_tile(x_ref[...], g_ref[...], dn)
        dx_ref[...] = dht + dx
        _accumulate(dg_ref, step, dg)

    return pl.pallas_call(
        body, grid=(t // TM,), name=name,
        in_specs=[_row_spec(TM, D), _row_spec(TM, D), _const_spec((1, D)), _row_spec(TM, DFF), _row_spec(TM, DFF),
                  _ANY, _ANY],
        out_specs=[_row_spec(TM, 2 * DFF), _row_spec(TM, D), _row_spec(TM, D), _const_spec((1, D))],
        out_shape=[jax.ShapeDtypeStruct((t, 2 * DFF), BF16), jax.ShapeDtypeStruct((t, D), BF16),
                   jax.ShapeDtypeStruct((t, D), F32), jax.ShapeDtypeStruct((1, D), F32)],
        scratch_shapes=[pltpu.VMEM(wgu.shape, BF16), pltpu.VMEM(wd.shape, BF16), pltpu.SemaphoreType.DMA((2,))],
        compiler_params=_cparams(),
    )(dh, xin, g, gate, up, wgu, wd)


def _pick_tile(n, options=(512, 256, 128)):
    for o in options:
        if n % o == 0:
            return o
    return n


def _mm_tn(a, b, name, out_dtype=BF16):
    k, m = a.shape
    _, n = b.shape
    tm, tn = _pick_tile(m), _pick_tile(n)

    def body(a_ref, b_ref, o_ref):
        o_ref[...] = _dot_tn(a_ref[...].astype(BF16), b_ref[...].astype(BF16)).astype(out_dtype)

    return pl.pallas_call(
        body, grid=(m // tm, n // tn), name=name,
        in_specs=[pl.BlockSpec((k, tm), lambda i, j: (0, i)), pl.BlockSpec((k, tn), lambda i, j: (0, j))],
        out_specs=pl.BlockSpec((tm, tn), lambda i, j: (i, j)),
        out_shape=jax.ShapeDtypeStruct((m, n), out_dtype),
        compiler_params=_cparams(2),
    )(a, b)


PCG_W = 5 * D
QKV_W = 3 * D
PROJ_CHUNK = 512


def _inproj_fwd(h, g, w_in, name):
    t = h.shape[0]

    def body(h_ref, g_ref, w_hbm, u_ref, pcg_ref, qkv_ref, w_v, sems):
        _load_resident(pl.program_id(0), [(w_hbm, w_v)], sems)
        u = _rms_fwd_tile(h_ref[...], g_ref[...]).astype(BF16)
        u_ref[...] = u
        for c in range(NPROJ // PROJ_CHUNK):
            lo, hi = c * PROJ_CHUNK, (c + 1) * PROJ_CHUNK
            p = _dot(u, w_v[:, lo:hi])
            if lo < 3 * D:
                pcg_ref[:, lo:hi] = p
            elif lo < 6 * D:
                qkv_ref[:, lo - 3 * D:hi - 3 * D] = p.astype(BF16)
            else:
                pcg_ref[:, lo - 3 * D:hi - 3 * D] = p

    return pl.pallas_call(
        body, grid=(t // TM,), name=name,
        in_specs=[_row_spec(TM, D), _const_spec((1, D)), _ANY],
        out_specs=[_row_spec(TM, D), _row_spec(TM, PCG_W), _row_spec(TM, QKV_W)],
        out_shape=[jax.ShapeDtypeStruct((t, D), BF16), jax.ShapeDtypeStruct((t, PCG_W), F32),
                   jax.ShapeDtypeStruct((t, QKV_W), BF16)],
        scratch_shapes=[pltpu.VMEM(w_in.shape, BF16), pltpu.SemaphoreType.DMA((1,))],
        compiler_params=_cparams(),
    )(h, g, w_in)


CONV_CW = 256


def _shift_down(v, k, rows):
    return jnp.where(rows >= k, pltpu.roll(v, k, 0), 0.0)


def _shift_up(v, k, rows, t):
    return jnp.where(rows < t - k, pltpu.roll(v, t - k, 0), 0.0)


def _col_spec(t, cw, off):
    return pl.BlockSpec((t, cw), lambda j: (0, j + off))


def _conv_fwd(pcg, conv_w, name):
    t = pcg.shape[0]
    nb = D // CONV_CW

    def body(cb_ref, cc_ref, cx_ref, w_ref, y_ref):
        rows = lax.broadcasted_iota(jnp.int32, (t, CONV_CW), 0)
        xc = cc_ref[...] * cx_ref[...]
        conv = (w_ref[0:1, :] * _shift_down(xc, 2, rows) + w_ref[1:2, :] * _shift_down(xc, 1, rows)
                + w_ref[2:3, :] * xc)
        y_ref[...] = (cb_ref[...] * conv).astype(BF16)

    return pl.pallas_call(
        body, grid=(nb,), name=name,
        in_specs=[_col_spec(t, CONV_CW, 0), _col_spec(t, CONV_CW, nb), _col_spec(t, CONV_CW, 2 * nb),
                  pl.BlockSpec((CONV_K, CONV_CW), lambda j: (0, j))],
        out_specs=_col_spec(t, CONV_CW, 0),
        out_shape=jax.ShapeDtypeStruct((t, D), BF16),
        compiler_params=_cparams(),
    )(pcg, pcg, pcg, conv_w)


def _conv_bwd(pcg, conv_w, dyc, name):
    t = pcg.shape[0]
    nb = D // CONV_CW

    def body(cb_ref, cc_ref, cx_ref, w_ref, dy_ref, dcb_ref, dcc_ref, dcx_ref, dw_ref):
        rows = lax.broadcasted_iota(jnp.int32, (t, CONV_CW), 0)
        cc, cx = cc_ref[...], cx_ref[...]
        xc = cc * cx
        x1 = _shift_down(xc, 1, rows)
        x2 = _shift_down(xc, 2, rows)
        w0, w1, w2 = w_ref[0:1, :], w_ref[1:2, :], w_ref[2:3, :]
        conv = w0 * x2 + w1 * x1 + w2 * xc
        dy = dy_ref[...]
        dcb_ref[...] = (dy * conv).astype(BF16)
        dconv = dy * cb_ref[...]
        dw_ref[...] = jnp.zeros((8, CONV_CW), F32)
        dw_ref[0:1, :] = jnp.sum(dconv * x2, axis=0, keepdims=True)
        dw_ref[1:2, :] = jnp.sum(dconv * x1, axis=0, keepdims=True)
        dw_ref[2:3, :] = jnp.sum(dconv * xc, axis=0, keepdims=True)
        dxc = w2 * dconv + w1 * _shift_up(dconv, 1, rows, t) + w0 * _shift_up(dconv, 2, rows, t)
        dcc_ref[...] = (dxc * cx).astype(BF16)
        dcx_ref[...] = (dxc * cc).astype(BF16)

    return pl.pallas_call(
        body, grid=(nb,), name=name,
        in_specs=[_col_spec(t, CONV_CW, 0), _col_spec(t, CONV_CW, nb), _col_spec(t, CONV_CW, 2 * nb),
                  pl.BlockSpec((CONV_K, CONV_CW), lambda j: (0, j)), _col_spec(t, CONV_CW, 0)],
        out_specs=[_col_spec(t, CONV_CW, 0), _col_spec(t, CONV_CW, 0), _col_spec(t, CONV_CW, 0),
                   pl.BlockSpec((8, CONV_CW), lambda j: (0, j))],
        out_shape=[jax.ShapeDtypeStruct((t, D), BF16)] * 3 + [jax.ShapeDtypeStruct((8, D), F32)],
        compiler_params=_cparams(),
    )(pcg, pcg, pcg, conv_w, dyc)


def _split3(v):
    hi = v.astype(BF16)
    r1 = v - hi.astype(F32)
    mid = r1.astype(BF16)
    lo = (r1 - mid.astype(F32)).astype(BF16)
    return hi, mid, lo


def _cumdot(v, tri):
    hi, mid, lo = _split3(v)
    return _dot(hi, tri) + _dot(mid, tri) + _dot(lo, tri)


def _log_sigmoids(z):
    sp = jnp.log(1.0 + jnp.exp(-jnp.abs(z)))
    return -(jnp.maximum(z, 0.0) + sp), -(jnp.maximum(-z, 0.0) + sp)


def _sb_fwd(qkv, name):
    t = qkv.shape[0]
    nq = t // TQ
    scale = SB_DH ** -0.5

    def body(q_ref, k_ref, v_ref, y_ref, ct_ref):
        i = pl.program_id(1)
        q = q_ref[...]
        rr = lax.broadcasted_iota(jnp.int32, (TQ, TQ), 0)
        cc = lax.broadcasted_iota(jnp.int32, (TQ, TQ), 1)
        later = (rr > cc).astype(BF16)
        causal = cc < rr

        def block(j, acc, c_sum, masked):
            off = pl.multiple_of(j * TQ, TQ)
            kj = k_ref[pl.ds(off, TQ), :]
            vj = v_ref[pl.ds(off, TQ), :]
            z = _dot_nt(q, kj) * scale
            m, l = _log_sigmoids(z)
            if masked:
                m = jnp.where(causal, m, 0.0)
            a = jnp.exp(l + c_sum + _cumdot(m, later))
            if masked:
                a = jnp.where(causal, a, 0.0)
            acc = acc + _dot(a.astype(BF16), vj)
            return acc, c_sum + jnp.sum(m, axis=1, keepdims=True)

        acc, c_sum = block(i, jnp.zeros((TQ, SB_DH), F32), jnp.zeros((TQ, 1), F32), True)
        acc, c_sum = lax.fori_loop(0, i, lambda jj, c: block(i - 1 - jj, c[0], c[1], False), (acc, c_sum))
        y_ref[...] = acc.astype(BF16)
        ct_ref[0] = c_sum

    return pl.pallas_call(
        body, grid=(SB_H, nq), name=name,
        in_specs=[pl.BlockSpec((TQ, SB_DH), lambda h, i: (i, h)),
                  pl.BlockSpec((t, SB_DH), lambda h, i: (0, SB_H + h)),
                  pl.BlockSpec((t, SB_DH), lambda h, i: (0, 2 * SB_H + h))],
        out_specs=[pl.BlockSpec((TQ, SB_DH), lambda h, i: (i, h)),
                   pl.BlockSpec((1, TQ, 1), lambda h, i: (h, i, 0))],
        out_shape=[jax.ShapeDtypeStruct((t, D), BF16), jax.ShapeDtypeStruct((SB_H, t, 1), F32)],
        compiler_params=_cparams(2),
    )(qkv, qkv, qkv)


def _sb_bwd(qkv, dy, ctot, name):
    t = qkv.shape[0]
    nq = t // TQ
    scale = SB_DH ** -0.5

    def body(q_ref, k_ref, v_ref, dy_ref, ct_ref, dq_ref, dk_ref, dv_ref):
        i = pl.program_id(1)

        @pl.when(i == 0)
        def _():
            dk_ref[...] = jnp.zeros_like(dk_ref)
            dv_ref[...] = jnp.zeros_like(dv_ref)

        q = q_ref[...]
        dy_t = dy_ref[...]
        c_tot = ct_ref[0]
        rr = lax.broadcasted_iota(jnp.int32, (TQ, TQ), 0)
        cc = lax.broadcasted_iota(jnp.int32, (TQ, TQ), 1)
        upto = (rr <= cc).astype(BF16)
        before = (rr < cc).astype(BF16)
        causal = cc < rr

        def block(j, dq, p_sum, e_sum, masked):
            off = pl.multiple_of(j * TQ, TQ)
            kj = k_ref[pl.ds(off, TQ), :]
            vj = v_ref[pl.ds(off, TQ), :]
            z = _dot_nt(q, kj) * scale
            m, l = _log_sigmoids(z)
            if masked:
                m = jnp.where(causal, m, 0.0)
            a = jnp.exp(l + ((c_tot - p_sum) - _cumdot(m, upto)))
            if masked:
                a = jnp.where(causal, a, 0.0)
            e = _dot_nt(dy_t, vj) * a
            e_before = e_sum + _cumdot(e, before)
            beta = jnp.exp(l)
            dz = e * (1.0 - beta) - e_before * beta
            if masked:
                dz = jnp.where(causal, dz, 0.0)
            dzs = (dz * scale).astype(BF16)
            dq = dq + _dot(dzs, kj)
            dk_ref[pl.ds(off, TQ), :] += _dot_tn(dzs, q)
            dv_ref[pl.ds(off, TQ), :] += _dot_tn(a.astype(BF16), dy_t)
            return dq, p_sum + jnp.sum(m, axis=1, keepdims=True), e_sum + jnp.sum(e, axis=1, keepdims=True)

        zero = jnp.zeros((TQ, 1), F32)
        carry = lax.fori_loop(0, i, lambda j, c: block(j, c[0], c[1], c[2], False),
                              (jnp.zeros((TQ, SB_DH), F32), zero, zero))
        dq, _, _ = block(i, carry[0], carry[1], carry[2], True)
        dq_ref[...] = dq.astype(BF16)

    return pl.pallas_call(
        body, grid=(SB_H, nq), name=name,
        in_specs=[pl.BlockSpec((TQ, SB_DH), lambda h, i: (i, h)),
                  pl.BlockSpec((t, SB_DH), lambda h, i: (0, SB_H + h)),
                  pl.BlockSpec((t, SB_DH), lambda h, i: (0, 2 * SB_H + h)),
                  pl.BlockSpec((TQ, SB_DH), lambda h, i: (i, h)),
                  pl.BlockSpec((1, TQ, 1), lambda h, i: (h, i, 0))],
        out_specs=[pl.BlockSpec((TQ, SB_DH), lambda h, i: (i, h)),
                   pl.BlockSpec((t, SB_DH), lambda h, i: (0, h)),
                   pl.BlockSpec((t, SB_DH), lambda h, i: (0, h))],
        out_shape=[jax.ShapeDtypeStruct((t, D), BF16), jax.ShapeDtypeStruct((t, D), F32),
                   jax.ShapeDtypeStruct((t, D), F32)],
        compiler_params=_cparams(2),
    )(qkv, qkv, qkv, dy, ctot)


def _gate_specs():
    return [pl.BlockSpec((TM, D), lambda i: (i, 3)), pl.BlockSpec((TM, D), lambda i: (i, 4))]


def _mix_out_fwd(yc, ysb, pcg, b_gate, h, w_conv_out, w_attn_out, w_o, name):
    t = h.shape[0]

    def body(yc_ref, ysb_ref, gc_ref, gs_ref, b_ref, h_ref, wc_hbm, wa_hbm, wo_hbm,
             a_ref, b_out_ref, mg_ref, h2_ref, wc_v, wa_v, wo_v, sems):
        _load_resident(pl.program_id(0), [(wc_hbm, wc_v), (wa_hbm, wa_v), (wo_hbm, wo_v)], sems)
        a = _dot(yc_ref[...], wc_v[...])
        b = _dot(ysb_ref[...], wa_v[...])
        merged = (_sigmoid(gc_ref[...] + b_ref[:, :D]) * a + _sigmoid(gs_ref[...] + b_ref[:, D:]) * b).astype(BF16)
        a_ref[...] = a
        b_out_ref[...] = b
        mg_ref[...] = merged
        h2_ref[...] = h_ref[...] + _dot(merged, wo_v[...])

    return pl.pallas_call(
        body, grid=(t // TM,), name=name,
        in_specs=[_row_spec(TM, D), _row_spec(TM, D)] + _gate_specs()
                 + [_const_spec((1, 2 * D)), _row_spec(TM, D), _ANY, _ANY, _ANY],
        out_specs=[_row_spec(TM, D)] * 4,
        out_shape=[jax.ShapeDtypeStruct((t, D), F32), jax.ShapeDtypeStruct((t, D), F32),
                   jax.ShapeDtypeStruct((t, D), BF16), jax.ShapeDtypeStruct((t, D), F32)],
        scratch_shapes=[pltpu.VMEM((D, D), BF16)] * 3 + [pltpu.SemaphoreType.DMA((3,))],
        compiler_params=_cparams(),
    )(yc, ysb, pcg, pcg, b_gate, h, w_conv_out, w_attn_out, w_o)


def _mix_out_bwd(dh2, a, b, pcg, b_gate, w_conv_out, w_attn_out, w_o, name):
    t = dh2.shape[0]

    def body(dh_ref, a_ref, b_ref, gc_ref, gs_ref, bias_ref, wc_hbm, wa_hbm, wo_hbm,
             dhb_ref, da_ref, db_ref, dgp_ref, dyc_ref, dysb_ref, dbias_ref, wc_v, wa_v, wo_v, sems):
        step = pl.program_id(0)
        _load_resident(step, [(wc_hbm, wc_v), (wa_hbm, wa_v), (wo_hbm, wo_v)], sems)
        dhb = dh_ref[...].astype(BF16)
        dhb_ref[...] = dhb
        dm = _dot_nt(dhb, wo_v[...])
        gc = _sigmoid(gc_ref[...] + bias_ref[:, :D])
        gs = _sigmoid(gs_ref[...] + bias_ref[:, D:])
        da = (dm * gc).astype(BF16)
        db = (dm * gs).astype(BF16)
        da_ref[...] = da
        db_ref[...] = db
        dgc = dm * a_ref[...] * (gc * (1.0 - gc))
        dgs = dm * b_ref[...] * (gs * (1.0 - gs))
        dgp_ref[:, :D] = dgc.astype(BF16)
        dgp_ref[:, D:] = dgs.astype(BF16)
        _accumulate(dbias_ref.at[:, :D], step, jnp.sum(dgc, axis=0, keepdims=True))
        _accumulate(dbias_ref.at[:, D:], step, jnp.sum(dgs, axis=0, keepdims=True))
        dyc_ref[...] = _dot_nt(da, wc_v[...])
        dysb_ref[...] = _dot_nt(db, wa_v[...]).astype(BF16)

    return pl.pallas_call(
        body, grid=(t // TM,), name=name,
        in_specs=[_row_spec(TM, D)] * 3 + _gate_specs() + [_const_spec((1, 2 * D)), _ANY, _ANY, _ANY],
        out_specs=[_row_spec(TM, D)] * 3 + [_row_spec(TM, 2 * D), _row_spec(TM, D), _row_spec(TM, D),
                                            _const_spec((1, 2 * D))],
        out_shape=[jax.ShapeDtypeStruct((t, D), BF16)] * 3
                  + [jax.ShapeDtypeStruct((t, 2 * D), BF16), jax.ShapeDtypeStruct((t, D), F32),
                     jax.ShapeDtypeStruct((t, D), BF16), jax.ShapeDtypeStruct((1, 2 * D), F32)],
        scratch_shapes=[pltpu.VMEM((D, D), BF16)] * 3 + [pltpu.SemaphoreType.DMA((3,))],
        compiler_params=_cparams(),
    )(dh2, a, b, pcg, pcg, b_gate, w_conv_out, w_attn_out, w_o)


def _inproj_bwd(pieces, dgp, w_in, h, g, dh_res, name):
    t = h.shape[0]
    n_piece = len(pieces)

    def body(*refs):
        piece_refs = refs[:n_piece]
        dgp_ref, w_hbm, h_ref, g_ref, dres_ref, dh_ref, dg_ref, w_v, sems = refs[n_piece:]
        step = pl.program_id(0)
        _load_resident(step, [(w_hbm, w_v)], sems)
        du = jnp.zeros((TM, D), F32)
        for p, ref in enumerate(piece_refs):
            du = du + _dot_nt(ref[...].astype(BF16), w_v[:, p * D:(p + 1) * D])
        du = du + _dot_nt(dgp_ref[...], w_v[:, n_piece * D:])
        dx, dg = _rms_bwd_tile(h_ref[...], g_ref[...], du)
        dh_ref[...] = dres_ref[...] + dx
        _accumulate(dg_ref, step, dg)

    return pl.pallas_call(
        body, grid=(t // TM,), name=name,
        in_specs=[_row_spec(TM, D)] * n_piece + [_row_spec(TM, 2 * D), _ANY, _row_spec(TM, D), _const_spec((1, D)),
                                                  _row_spec(TM, D)],
        out_specs=[_row_spec(TM, D), _const_spec((1, D))],
        out_shape=[jax.ShapeDtypeStruct((t, D), F32), jax.ShapeDtypeStruct((1, D), F32)],
        scratch_shapes=[pltpu.VMEM(w_in.shape, BF16), pltpu.SemaphoreType.DMA((1,))],
        compiler_params=_cparams(),
    )(*pieces, dgp, w_in, h, g, dh_res)


def _memkv_fwd(mem, g, w_ckv, name):
    m = mem.shape[0]

    def body(mem_ref, g_ref, w_ref, mn_ref, kv_ref):
        mn = _rms_fwd_tile(mem_ref[...], g_ref[...]).astype(BF16)
        mn_ref[...] = mn
        kv_ref[...] = _dot(mn, w_ref[...]).astype(BF16)

    return pl.pallas_call(
        body, grid=(1,), name=name,
        in_specs=[_const_spec((m, D)), _const_spec((1, D)), _const_spec((D, 2 * D))],
        out_specs=[_const_spec((m, D)), _const_spec((m, 2 * D))],
        out_shape=[jax.ShapeDtypeStruct((m, D), BF16), jax.ShapeDtypeStruct((m, 2 * D), BF16)],
        compiler_params=_cparams(),
    )(mem, g, w_ckv)


def _memkv_bwd(dkv, mem, g, w_ckv, name):
    m = mem.shape[0]

    def body(dkv_ref, mem_ref, g_ref, w_ref, dg_ref):
        dmn = _dot_nt(dkv_ref[...].astype(BF16), w_ref[...])
        _, dg = _rms_bwd_tile(mem_ref[...], g_ref[...], dmn)
        dg_ref[...] = dg

    return pl.pallas_call(
        body, grid=(1,), name=name,
        in_specs=[_const_spec((m, 2 * D)), _const_spec((m, D)), _const_spec((1, D)), _const_spec((D, 2 * D))],
        out_specs=_const_spec((1, D)),
        out_shape=jax.ShapeDtypeStruct((1, D), F32),
        compiler_params=_cparams(),
    )(dkv, mem, g, w_ckv)


def _softmax_rows(s):
    e = jnp.exp(s - jnp.max(s, axis=-1, keepdims=True))
    return e / jnp.sum(e, axis=-1, keepdims=True)


def _cross_fwd(h, g, kv, w_cq, w_co, name):
    t = h.shape[0]
    m = kv.shape[0]
    scale = X_DH ** -0.5

    def body(h_ref, g_ref, kv_ref, wq_hbm, wo_hbm, hn_ref, qx_ref, o_ref, h3_ref, wq_v, wo_v, sems):
        _load_resident(pl.program_id(0), [(wq_hbm, wq_v), (wo_hbm, wo_v)], sems)
        ht = h_ref[...]
        hn = _rms_fwd_tile(ht, g_ref[...]).astype(BF16)
        hn_ref[...] = hn
        qx = _dot(hn, wq_v[...]).astype(BF16)
        qx_ref[...] = qx
        for hd in range(X_H):
            lo, hi = hd * X_DH, (hd + 1) * X_DH
            p = _softmax_rows(_dot_nt(qx[:, lo:hi], kv_ref[:, lo:hi]) * scale)
            o_ref[:, lo:hi] = _dot(p.astype(BF16), kv_ref[:, D + lo:D + hi]).astype(BF16)
        h3_ref[...] = ht + _dot(o_ref[...], wo_v[...])

    return pl.pallas_call(
        body, grid=(t // TM,), name=name,
        in_specs=[_row_spec(TM, D), _const_spec((1, D)), _const_spec((m, 2 * D)), _ANY, _ANY],
        out_specs=[_row_spec(TM, D)] * 4,
        out_shape=[jax.ShapeDtypeStruct((t, D), BF16)] * 3 + [jax.ShapeDtypeStruct((t, D), F32)],
        scratch_shapes=[pltpu.VMEM((D, D), BF16)] * 2 + [pltpu.SemaphoreType.DMA((2,))],
        compiler_params=_cparams(),
    )(h, g, kv, w_cq, w_co)


def _cross_bwd(dh3, h, g, qx, kv, w_cq, w_co, name):
    t = h.shape[0]
    m = kv.shape[0]
    scale = X_DH ** -0.5

    def body(dh_ref, h_ref, g_ref, qx_ref, kv_ref, wq_hbm, wo_hbm,
             dhb_ref, dqx_ref, dkv_ref, dh2_ref, dg_ref, wq_v, wo_v, sems):
        step = pl.program_id(0)
        _load_resident(step, [(wq_hbm, wq_v), (wo_hbm, wo_v)], sems)

        @pl.when(step == 0)
        def _():
            dkv_ref[...] = jnp.zeros_like(dkv_ref)

        dht = dh_ref[...]
        dhb = dht.astype(BF16)
        dhb_ref[...] = dhb
        do = _dot_nt(dhb, wo_v[...]).astype(BF16)
        for hd in range(X_H):
            lo, hi = hd * X_DH, (hd + 1) * X_DH
            qh = qx_ref[:, lo:hi]
            kh = kv_ref[:, lo:hi]
            p = _softmax_rows(_dot_nt(qh, kh) * scale)
            doh = do[:, lo:hi]
            dp = _dot_nt(doh, kv_ref[:, D + lo:D + hi])
            ds = (p * (dp - jnp.sum(dp * p, axis=-1, keepdims=True)) * scale).astype(BF16)
            dqx_ref[:, lo:hi] = _dot(ds, kh).astype(BF16)
            dkv_ref[:, lo:hi] += _dot_tn(ds, qh)
            dkv_ref[:, D + lo:D + hi] += _dot_tn(p.astype(BF16), doh)
        dhn = _dot_nt(dqx_ref[...], wq_v[...])
        dx, dg = _rms_bwd_tile(h_ref[...], g_ref[...], dhn)
        dh2_ref[...] = dht + dx
        _accumulate(dg_ref, step, dg)

    return pl.pallas_call(
        body, grid=(t // TM,), name=name,
        in_specs=[_row_spec(TM, D), _row_spec(TM, D), _const_spec((1, D)), _row_spec(TM, D), _const_spec((m, 2 * D)),
                  _ANY, _ANY],
        out_specs=[_row_spec(TM, D), _row_spec(TM, D), _const_spec((m, 2 * D)), _row_spec(TM, D), _const_spec((1, D))],
        out_shape=[jax.ShapeDtypeStruct((t, D), BF16), jax.ShapeDtypeStruct((t, D), BF16),
                   jax.ShapeDtypeStruct((m, 2 * D), F32), jax.ShapeDtypeStruct((t, D), F32),
                   jax.ShapeDtypeStruct((1, D), F32)],
        scratch_shapes=[pltpu.VMEM((D, D), BF16)] * 2 + [pltpu.SemaphoreType.DMA((2,))],
        compiler_params=_cparams(),
    )(dh3, h, g, qx, kv, w_cq, w_co)


def _loss_bwd(h, g, target, name):
    t = h.shape[0]

    def body(h_ref, g_ref, t_ref, loss_ref, dh_ref, dg_ref):
        step = pl.program_id(0)
        ht = h_ref[...]
        gain = g_ref[...]
        diff = _rms_fwd_tile(ht, gain) - t_ref[...]
        part = 0.5 * jnp.sum(jnp.sum(diff * diff, axis=-1, keepdims=True) / D, axis=0, keepdims=True)
        dx, dg = _rms_bwd_tile(ht, gain, diff / D)
        dh_ref[...] = dx
        _accumulate(loss_ref, step, jnp.broadcast_to(part, (8, 128)))
        _accumulate(dg_ref, step, dg)

    return pl.pallas_call(
        body, grid=(t // TM,), name=name,
        in_specs=[_row_spec(TM, D), _const_spec((1, D)), _row_spec(TM, D)],
        out_specs=[_const_spec((8, 128)), _row_spec(TM, D), _const_spec((1, D))],
        out_shape=[jax.ShapeDtypeStruct((8, 128), F32), jax.ShapeDtypeStruct((t, D), F32),
                   jax.ShapeDtypeStruct((1, D), F32)],
        compiler_params=_cparams(),
    )(h, g, target)


def _adamw(w, gr, m, v, name):
    r, c = w.shape
    tr = _pick_tile(r, (256, 128)) if r > 512 else r

    def body(w_ref, g_ref, m_ref, v_ref, d_ref, nm_ref, nv_ref):
        gt = g_ref[...]
        nm = ADAM_B1 * m_ref[...] + (1.0 - ADAM_B1) * gt
        nv = ADAM_B2 * v_ref[...] + (1.0 - ADAM_B2) * jnp.square(gt)
        m_hat = nm / (1.0 - ADAM_B1 ** ADAM_STEP)
        v_hat = nv / (1.0 - ADAM_B2 ** ADAM_STEP)
        d_ref[...] = -ADAM_LR * (m_hat / (jnp.sqrt(v_hat) + ADAM_EPS) + ADAM_WD * w_ref[...])
        nm_ref[...] = nm
        nv_ref[...] = nv

    spec = _row_spec(tr, c)
    return pl.pallas_call(
        body, grid=(r // tr,), name=name,
        in_specs=[spec] * 4, out_specs=[spec] * 3,
        out_shape=[jax.ShapeDtypeStruct((r, c), F32)] * 3,
        compiler_params=_cparams(),
    )(w, gr, m, v)


def _mesh_pos():
    return lax.axis_index("x"), lax.axis_index("y"), lax.axis_index("c")


def _all_gather(shard, name):
    r, c_dim = shard.shape

    def body(x_ref, out_ref, send_sems, recv_sems, local_sem):
        x, y, c = _mesh_pos()
        me, sibling = (x, y, c), (x, y, 1 - c)
        chips = [(1 - x, y), (x, 1 - y), (1 - x, 1 - y)]

        def slot(px, py, pc):
            return out_ref.at[4 * px + 2 * py + pc]

        def copy(k, block, to, src=None):
            return pltpu.make_async_remote_copy(
                src_ref=slot(*block) if src is None else src, dst_ref=slot(*block),
                send_sem=send_sems.at[k], recv_sem=recv_sems.at[k],
                device_id=to, device_id_type=pl.DeviceIdType.MESH)

        mine = pltpu.make_async_copy(x_ref, slot(*me), local_sem)
        mine.start()
        first = [copy(0, me, sibling, src=x_ref)]
        first += [copy(1 + j, me, (*chip, c), src=x_ref) for j, chip in enumerate(chips)]
        for cp in first:
            cp.start()
        passed = [copy(4 + j, (*chip, c), sibling) for j, chip in enumerate(chips)]
        for j, chip in enumerate(chips):
            copy(1 + j, (*chip, c), me).wait_recv()
            passed[j].start()
        copy(0, sibling, me).wait_recv()
        for j, chip in enumerate(chips):
            copy(4 + j, (*chip, 1 - c), me).wait_recv()
        for cp in first + passed:
            cp.wait_send()
        mine.wait()

    return pl.pallas_call(
        body, name=name,
        out_shape=jax.ShapeDtypeStruct((N_DEV, r, c_dim), shard.dtype),
        in_specs=[_ANY], out_specs=_ANY,
        scratch_shapes=[pltpu.SemaphoreType.DMA((7,)), pltpu.SemaphoreType.DMA((7,)), pltpu.SemaphoreType.DMA],
    )(shard)


def _pair_exchange(g2, name):
    _, n_chip, r, c_dim = g2.shape

    def body(g_ref, land_ref, send_sem, recv_sem):
        x, y, c = _mesh_pos()
        cp = pltpu.make_async_remote_copy(
            src_ref=g_ref.at[1 - c], dst_ref=land_ref, send_sem=send_sem, recv_sem=recv_sem,
            device_id=(x, y, 1 - c), device_id_type=pl.DeviceIdType.MESH)
        cp.start()
        cp.wait()

    return pl.pallas_call(
        body, name=name,
        out_shape=jax.ShapeDtypeStruct((n_chip, r, c_dim), g2.dtype),
        in_specs=[_ANY], out_specs=_ANY,
        scratch_shapes=[pltpu.SemaphoreType.DMA, pltpu.SemaphoreType.DMA],
    )(g2)


def _pair_sum(g2, landed, core, name):
    _, n_chip, r, c_dim = g2.shape
    tr = r // 9

    def body(core_ref, mine_ref, theirs_ref, o_ref):
        o_ref[0] = (mine_ref[0, 0].astype(F32) + theirs_ref[0].astype(F32)).astype(o_ref.dtype)

    return pl.pallas_call(
        body, name=name,
        grid_spec=pltpu.PrefetchScalarGridSpec(
            num_scalar_prefetch=1, grid=(n_chip, r // tr),
            in_specs=[pl.BlockSpec((1, 1, tr, c_dim), lambda k, i, core_ref: (core_ref[0], k, i, 0)),
                      pl.BlockSpec((1, tr, c_dim), lambda k, i, core_ref: (k, i, 0))],
            out_specs=pl.BlockSpec((1, tr, c_dim), lambda k, i, core_ref: (k, i, 0))),
        out_shape=jax.ShapeDtypeStruct((n_chip, r, c_dim), g2.dtype),
        compiler_params=_cparams(2),
    )(core, g2, landed)


def _chip_exchange(p, name):
    n_chip, r, c_dim = p.shape

    def body(p_ref, land_ref, send_sems, recv_sems, local_sem):
        x, y, c = _mesh_pos()
        my_chip = 2 * x + y
        chips = [(1 - x, y), (x, 1 - y), (1 - x, 1 - y)]
        local = pltpu.make_async_copy(p_ref.at[my_chip], land_ref.at[my_chip], local_sem)
        local.start()

        def copy(k, px, py):
            return pltpu.make_async_remote_copy(
                src_ref=p_ref.at[2 * px + py], dst_ref=land_ref.at[my_chip],
                send_sem=send_sems.at[k], recv_sem=recv_sems.at[k],
                device_id=(px, py, c), device_id_type=pl.DeviceIdType.MESH)

        def arrival(k, px, py):
            return pltpu.make_async_remote_copy(
                src_ref=p_ref.at[my_chip], dst_ref=land_ref.at[2 * px + py],
                send_sem=send_sems.at[k], recv_sem=recv_sems.at[k],
                device_id=(px, py, c), device_id_type=pl.DeviceIdType.MESH)

        sends = [copy(k, px, py) for k, (px, py) in enumerate(chips)]
        for cp in sends:
            cp.start()
        for k, (px, py) in enumerate(chips):
            arrival(k, px, py).wait_recv()
        for cp in sends:
            cp.wait_send()
        local.wait()

    return pl.pallas_call(
        body, name=name,
        out_shape=jax.ShapeDtypeStruct((n_chip, r, c_dim), p.dtype),
        in_specs=[_ANY], out_specs=_ANY,
        scratch_shapes=[pltpu.SemaphoreType.DMA((3,)), pltpu.SemaphoreType.DMA((3,)), pltpu.SemaphoreType.DMA],
    )(p)


def _sum_slots(parts, name):
    n, r, c_dim = parts.shape
    tr = r // 9 if r % 9 == 0 and r > 512 else r

    def body(p_ref, o_ref):
        acc = p_ref[0].astype(F32)
        for k in range(1, n):
            acc = acc + p_ref[k].astype(F32)
        o_ref[...] = acc

    return pl.pallas_call(
        body, grid=(r // tr,), name=name,
        in_specs=[pl.BlockSpec((n, tr, c_dim), lambda i: (0, i, 0))],
        out_specs=_row_spec(tr, c_dim),
        out_shape=jax.ShapeDtypeStruct((r, c_dim), F32),
        compiler_params=_cparams(),
    )(parts)


def _pack_shards(shards):
    return jnp.concatenate([shards[name].astype(BF16).reshape(-1, PACK_W) for name, _, _ in BIG], axis=0)


def _unpack_gathered(gathered):
    full, row = {}, 0
    for name, shape, axis in BIG:
        a, b = _shard_shape(shape, axis)
        rows = _pack_rows(shape, axis)
        blk = gathered[:, row:row + rows, :].reshape(N_DEV, a, b)
        row += rows
        full[name] = blk.reshape(shape) if axis == 0 else blk.transpose(1, 0, 2).reshape(shape)
    return full


def _pack_grads(grads):
    parts = []
    for name, shape, axis in BIG:
        a, b = _shard_shape(shape, axis)
        gfull = grads[name]
        blk = gfull.reshape(N_DEV, a, b) if axis == 0 else gfull.reshape(a, N_DEV, b).transpose(1, 0, 2)
        parts.append(blk.reshape(N_DEV, -1, PACK_W))
    packed = jnp.concatenate(parts, axis=1)
    return packed.reshape(4, 2, PACK_R, PACK_W).transpose(1, 0, 2, 3)


def _unpack_local(flat):
    out, row = {}, 0
    for name, shape, axis in BIG:
        rows = _pack_rows(shape, axis)
        out[name] = flat[row:row + rows, :].reshape(_shard_shape(shape, axis))
        row += rows
    return out


def _local_step(x, mem, target, w, gains, b_gate, conv_w):
    n1, gate1, up1, act1, h1 = _ffn_fwd(x, gains["g_ffn1"], w["w_ffn1_gu"], w["w_ffn1_down"], "ffn1_fwd")
    u, pcg, qkv = _inproj_fwd(h1, gains["g_mix"], w["w_in"], "inproj_fwd")
    yc = _conv_fwd(pcg, conv_w, "conv_fwd")
    ysb, ctot = _sb_fwd(qkv, "sb_fwd")
    a_mix, b_mix, merged, h2 = _mix_out_fwd(yc, ysb, pcg, b_gate, h1, w["w_conv_out"], w["w_attn_out"], w["w_o"],
                                            "mix_out_fwd")
    mn, kv = _memkv_fwd(mem, gains["g_mem"], w["w_ckv"], "memkv_fwd")
    hn, qx, o_x, h3 = _cross_fwd(h2, gains["g_cross"], kv, w["w_cq"], w["w_co"], "cross_fwd")
    n4, gate2, up2, act2, h4 = _ffn_fwd(h3, gains["g_ffn2"], w["w_ffn2_gu"], w["w_ffn2_down"], "ffn2_fwd")
    loss, dh4, dg_final = _loss_bwd(h4, gains["g_final"], target, "loss_bwd")

    gw, gs = {}, {"g_final": dg_final}
    dgu2, dh4b, dh3, gs["g_ffn2"] = _ffn_bwd(dh4, h3, gains["g_ffn2"], gate2, up2, w["w_ffn2_gu"], w["w_ffn2_down"],
                                             "ffn2_bwd")
    gw["w_ffn2_down"] = _mm_tn(act2, dh4b, "dw_ffn2_down")
    gw["w_ffn2_gu"] = _mm_tn(n4, dgu2, "dw_ffn2_gu")
    dh3b, dqx, dkv, dh2, gs["g_cross"] = _cross_bwd(dh3, h2, gains["g_cross"], qx, kv, w["w_cq"], w["w_co"],
                                                    "cross_bwd")
    gw["w_co"] = _mm_tn(o_x, dh3b, "dw_co")
    gw["w_cq"] = _mm_tn(hn, dqx, "dw_cq")
    gw["w_ckv"] = _mm_tn(mn, dkv, "dw_ckv")
    gs["g_mem"] = _memkv_bwd(dkv, mem, gains["g_mem"], w["w_ckv"], "memkv_bwd")
    dh2b, da_mix, db_mix, dgp, dyc, dysb, gs["b_gate"] = _mix_out_bwd(
        dh2, a_mix, b_mix, pcg, b_gate, w["w_conv_out"], w["w_attn_out"], w["w_o"], "mix_out_bwd")
    gw["w_o"] = _mm_tn(merged, dh2b, "dw_o")
    gw["w_conv_out"] = _mm_tn(yc, da_mix, "dw_conv_out")
    gw["w_attn_out"] = _mm_tn(ysb, db_mix, "dw_attn_out")
    dq, dk, dv = _sb_bwd(qkv, dysb, ctot, "sb_bwd")
    dcb, dcc, dcx, gs["conv_w"] = _conv_bwd(pcg, conv_w, dyc, "conv_bwd")
    pieces = [dcb, dcc, dcx, dq, dk, dv]
    gw["w_in"] = jnp.concatenate(
        [_mm_tn(u, p, "dw_in_%d" % k) for k, p in enumerate(pieces)] + [_mm_tn(u, dgp, "dw_in_gates")], axis=1)
    dh1, gs["g_mix"] = _inproj_bwd(pieces, dgp, w["w_in"], h1, gains["g_mix"], dh2, "inproj_bwd")
    dgu1, dh1b, dx, gs["g_ffn1"] = _ffn_bwd(dh1, x, gains["g_ffn1"], gate1, up1, w["w_ffn1_gu"], w["w_ffn1_down"],
                                            "ffn1_bwd")
    gw["w_ffn1_down"] = _mm_tn(act1, dh1b, "dw_ffn1_down")
    gw["w_ffn1_gu"] = _mm_tn(n1, dgu1, "dw_ffn1_gu")
    return loss, dx, gw, gs


SMALL_ROWS = {"g_ffn1": 0, "g_mix": 1, "g_cross": 2, "g_mem": 3, "g_ffn2": 4, "g_final": 5,
              "b_gate": 6, "conv_w": 8}
SMALL_R = 16
GAINS = ("g_ffn1", "g_mix", "g_cross", "g_mem", "g_ffn2", "g_final")
WEIGHT_ORDER = ("g_ffn1", "w_ffn1_gu", "w_ffn1_down", "g_mix", "w_in", "b_gate", "conv_w", "w_conv_out",
                "w_attn_out", "w_o", "g_cross", "g_mem", "w_cq", "w_ckv", "w_co", "g_ffn2", "w_ffn2_gu",
                "w_ffn2_down", "g_final")


def _pack_small(vals, conv_full):
    rows = [vals[n].reshape(1, D) for n in GAINS] + [vals["b_gate"].reshape(2, D), conv_full.reshape(CONV_K, D)]
    used = len(GAINS) + 2 + CONV_K
    return jnp.concatenate(rows + [jnp.zeros((SMALL_R - used, D), F32)], axis=0)


def _unpack_small(buf):
    out = {n: buf[SMALL_ROWS[n]] for n in GAINS}
    out["b_gate"] = buf[6:8].reshape(2 * D)
    out["conv_w"] = buf[8:8 + CONV_K]
    return out


def kernel(x, mem, g_ffn1, w_ffn1_gu, w_ffn1_down, g_mix, w_in, b_gate, conv_w, w_conv_out, w_attn_out, w_o, g_cross, g_mem, w_cq, w_ckv, w_co, g_ffn2, w_ffn2_gu, w_ffn2_down, g_final, loss_target, m_g_ffn1, m_w_ffn1_gu, m_w_ffn1_down, m_g_mix, m_w_in, m_b_gate, m_conv_w, m_w_conv_out, m_w_attn_out, m_w_o, m_g_cross, m_g_mem, m_w_cq, m_w_ckv, m_w_co, m_g_ffn2, m_w_ffn2_gu, m_w_ffn2_down, m_g_final, v_g_ffn1, v_w_ffn1_gu, v_w_ffn1_down, v_g_mix, v_w_in, v_b_gate, v_conv_w, v_w_conv_out, v_w_attn_out, v_w_o, v_g_cross, v_g_mem, v_w_cq, v_w_ckv, v_w_co, v_g_ffn2, v_w_ffn2_gu, v_w_ffn2_down, v_g_final):
    args = locals()
    wts = {n: args[n] for n in WEIGHT_ORDER}
    mom1 = {n: args["m_" + n] for n in WEIGHT_ORDER}
    mom2 = {n: args["v_" + n] for n in WEIGHT_ORDER}
    cx, cy, cc = _mesh_pos()
    dev = 4 * cx + 2 * cy + cc
    conv_cols = D // N_DEV

    gathered = _all_gather(_pack_shards(wts), "gather_weights")
    w_full = _unpack_gathered(gathered)
    conv_pad = jnp.concatenate([conv_w, jnp.zeros((8 - CONV_K, conv_cols), F32)], axis=0)
    conv_all = _all_gather(conv_pad, "gather_conv")
    conv_full = conv_all[:, :CONV_K, :].transpose(1, 0, 2).reshape(CONV_K, D)

    gains = {n: wts[n].reshape(1, D) for n in GAINS}
    loss8, dx, gw, gs = _local_step(x[0], mem[0], loss_target[0], w_full, gains, b_gate.reshape(1, 2 * D), conv_full)
    loss = lax.psum(loss8[0, 0], MESH_AXES)

    g2 = _pack_grads(gw)
    landed = _pair_exchange(g2, "grads_to_sibling")
    chip_sum = _pair_sum(g2, landed, cc.reshape(1).astype(jnp.int32), "grads_pair_sum")
    parts = _chip_exchange(chip_sum, "grads_to_chips")
    grad_big = _unpack_local(_sum_slots(parts, "grads_sum"))

    gs_flat = dict(gs)
    gs_flat["conv_w"] = gs["conv_w"][:CONV_K]
    small_all = _all_gather(_pack_small(gs_flat, gs_flat["conv_w"]), "gather_small_grads")
    grad_small = _unpack_small(_sum_slots(small_all, "small_grads_sum"))
    grad_small["conv_w"] = lax.dynamic_slice_in_dim(grad_small["conv_w"], dev * conv_cols, conv_cols, axis=1)

    grads, delta, new_m, new_v = {}, {}, {}, {}
    for n in WEIGHT_ORDER:
        if n in grad_big:
            grads[n] = grad_big[n]
            delta[n], new_m[n], new_v[n] = _adamw(wts[n], grads[n], mom1[n], mom2[n], "adamw_" + n)
    def small_buf(vals):
        conv_rows = jnp.concatenate([vals["conv_w"], jnp.zeros((CONV_K, D - conv_cols), F32)], axis=1)
        return _pack_small(vals, conv_rows)
    small_w = {n: wts[n] for n in GAINS + ("b_gate", "conv_w")}
    small_m = {n: mom1[n] for n in GAINS + ("b_gate", "conv_w")}
    small_v = {n: mom2[n] for n in GAINS + ("b_gate", "conv_w")}
    d_s, m_s, v_s = _adamw(small_buf(small_w), small_buf(grad_small), small_buf(small_m), small_buf(small_v),
                           "adamw_small")
    for res, buf in ((delta, d_s), (new_m, m_s), (new_v, v_s)):
        un = _unpack_small(buf)
        for n in GAINS + ("b_gate",):
            res[n] = un[n]
        res["conv_w"] = un["conv_w"][:, :conv_cols]
    for n in GAINS + ("b_gate", "conv_w"):
        grads[n] = grad_small[n]

    return (loss, dx[None], *[grads[n] for n in WEIGHT_ORDER], *[delta[n] for n in WEIGHT_ORDER],
            *[new_m[n] for n in WEIGHT_ORDER], *[new_v[n] for n in WEIGHT_ORDER])
```

```python
import jax
import jax.numpy as jnp
from jax import lax
from jax.experimental import pallas as pl
from jax.experimental.pallas import tpu as pltpu

F32 = jnp.float32
BF16 = jnp.bfloat16

D = 1024
DFF = 2816
SB_H = 8
SB_DH = 128
X_H = 4
X_DH = 256
CONV_K = 3
RMS_EPS = 1e-6
N_DEV = 8
N_CHIP = 4

ADAM_LR = 0.001
ADAM_B1 = 0.9
ADAM_B2 = 0.999
ADAM_EPS = 1e-08
ADAM_WD = 0.01
ADAM_STEP = 10

TM = 256
TQ = 256
SB_HPS = 2
VMEM_LIMIT = 56 << 20

FF_BLK = DFF // 4
FF_PAD = 768
FF_SUB = 256
DOWN_ROWS = DFF // N_DEV

P_ROWS = (("w_ffn1_down", DOWN_ROWS), ("w_in", D), ("w_conv_out", D // N_DEV), ("w_attn_out", D // N_DEV),
          ("w_o", D // N_DEV), ("w_cq", D // N_DEV), ("w_co", D // N_DEV), ("w_ffn2_down", DOWN_ROWS))
P_OFF = {}
_row = 0
for _name, _rows in P_ROWS:
    P_OFF[_name] = _row
    _row += _rows
P_R = _row

MESH_AXES = ("x", "y", "c")
_ANY = pl.BlockSpec(memory_space=pl.ANY)


def _cparams(n_axes=1):
    return pltpu.CompilerParams(
        dimension_semantics=("arbitrary",) * n_axes, vmem_limit_bytes=VMEM_LIMIT)


def _row_spec(tm, n):
    return pl.BlockSpec((tm, n), lambda i: (i, 0))


def _blk_row_spec(nb, tm, n):
    return pl.BlockSpec((nb, tm, n), lambda i: (0, i, 0))


def _const_spec(shape):
    zeros = (0,) * len(shape)
    return pl.BlockSpec(shape, lambda i: zeros)


def _dot(a, b):
    return jnp.dot(a, b, preferred_element_type=F32)


def _dot_nt(a, b):
    return lax.dot_general(a, b, (((1,), (1,)), ((), ())), preferred_element_type=F32)


def _dot_tn(a, b):
    return lax.dot_general(a, b, (((0,), (0,)), ((), ())), preferred_element_type=F32)


def _sigmoid(x):
    return 1.0 / (1.0 + jnp.exp(-x))


def _load_resident(step, pairs, sems):
    @pl.when(step == 0)
    def _():
        copies = [pltpu.make_async_copy(src, dst, sems.at[k]) for k, (src, dst) in enumerate(pairs)]
        for cp in copies:
            cp.start()
        for cp in copies:
            cp.wait()


def _square_pairs(p_hbm, name, dst):
    rows = D // N_DEV
    off = P_OFF[name]
    return [(p_hbm.at[d, off:off + rows, :], dst.at[d * rows:(d + 1) * rows, :]) for d in range(N_DEV)]


def _down_pairs(p_hbm, name, dst):
    off = P_OFF[name]
    return [(p_hbm.at[d, off:off + DOWN_ROWS, :],
             dst.at[d // 2, (d % 2) * DOWN_ROWS:(d % 2 + 1) * DOWN_ROWS, :]) for d in range(N_DEV)]


def _zero_down_pad(step, dst):
    @pl.when(step == 0)
    def _():
        dst[:, FF_BLK:, :] = jnp.zeros((4, FF_PAD - FF_BLK, D), BF16)


def _rms_fwd_tile(xt, g):
    r = lax.rsqrt(jnp.mean(xt * xt, axis=-1, keepdims=True) + RMS_EPS)
    return (xt * r) * g


def _rms_bwd_tile(xt, g, dn):
    r = lax.rsqrt(jnp.mean(xt * xt, axis=-1, keepdims=True) + RMS_EPS)
    xhat = xt * r
    dxhat = dn * g
    dx = r * (dxhat - xhat * jnp.mean(dxhat * xhat, axis=-1, keepdims=True))
    dg = jnp.sum(dn * xhat, axis=0, keepdims=True)
    return dx, dg


def _accumulate(ref, step, value):
    @pl.when(step == 0)
    def _():
        ref[...] = value

    @pl.when(step != 0)
    def _():
        ref[...] = ref[...] + value


def _ffn_fwd(x, g, wgu, p_all, down_name, name):
    t = x.shape[0]

    def body(x_ref, g_ref, wgu_hbm, p_hbm, n_ref, gate_ref, up_ref, act_ref, h_ref, wgu_v, wd_v, sems):
        step = pl.program_id(0)
        _zero_down_pad(step, wd_v)
        _load_resident(step, [(wgu_hbm, wgu_v)] + _down_pairs(p_hbm, down_name, wd_v), sems)
        xt = x_ref[...]
        n = _rms_fwd_tile(xt, g_ref[...]).astype(BF16)
        n_ref[...] = n
        acc = jnp.zeros((TM, D), F32)
        for j in range(4):
            for s in range(FF_PAD // FF_SUB):
                lo, hi = s * FF_SUB, (s + 1) * FF_SUB
                gt = _dot(n, wgu_v[j, :, lo:hi])
                ut = _dot(n, wgu_v[4 + j, :, lo:hi])
                act = ((gt * _sigmoid(gt)) * ut).astype(BF16)
                gate_ref[j, :, lo:hi] = gt.astype(BF16)
                up_ref[j, :, lo:hi] = ut.astype(BF16)
                act_ref[j, :, lo:hi] = act
                acc = acc + _dot(act, wd_v[j, lo:hi, :])
        h_ref[...] = xt + 0.5 * acc

    ff = jax.ShapeDtypeStruct((4, t, FF_PAD), BF16)
    return pl.pallas_call(
        body, grid=(t // TM,), name=name,
        in_specs=[_row_spec(TM, D), _const_spec((1, D)), _ANY, _ANY],
        out_specs=[_row_spec(TM, D)] + [_blk_row_spec(4, TM, FF_PAD)] * 3 + [_row_spec(TM, D)],
        out_shape=[jax.ShapeDtypeStruct((t, D), BF16), ff, ff, ff, jax.ShapeDtypeStruct((t, D), F32)],
        scratch_shapes=[pltpu.VMEM((N_DEV, D, FF_PAD), BF16), pltpu.VMEM((4, FF_PAD, D), BF16),
                        pltpu.SemaphoreType.DMA((1 + N_DEV,))],
        compiler_params=_cparams(),
    )(x, g, wgu, p_all)


def _ffn_bwd(dh, xin, g, gate, up, wgu, p_all, down_name, name):
    t = dh.shape[0]

    def body(dh_ref, x_ref, g_ref, gate_ref, up_ref, wgu_hbm, p_hbm,
             dgu_ref, dhb_ref, dx_ref, dg_ref, wgu_v, wd_v, sems):
        step = pl.program_id(0)
        _zero_down_pad(step, wd_v)
        _load_resident(step, [(wgu_hbm, wgu_v)] + _down_pairs(p_hbm, down_name, wd_v), sems)
        dht = dh_ref[...]
        dhb = (0.5 * dht).astype(BF16)
        dhb_ref[...] = dhb
        dn = jnp.zeros((TM, D), F32)
        for j in range(4):
            for s in range(FF_PAD // FF_SUB):
                lo, hi = s * FF_SUB, (s + 1) * FF_SUB
                da = _dot_nt(dhb, wd_v[j, lo:hi, :])
                gt = gate_ref[j, :, lo:hi].astype(F32)
                ut = up_ref[j, :, lo:hi].astype(F32)
                sg = _sigmoid(gt)
                dgt = (da * ut * (sg * (1.0 + gt * (1.0 - sg)))).astype(BF16)
                dut = (da * (gt * sg)).astype(BF16)
                dgu_ref[j, :, lo:hi] = dgt
                dgu_ref[4 + j, :, lo:hi] = dut
                dn = dn + _dot_nt(dgt, wgu_v[j, :, lo:hi]) + _dot_nt(dut, wgu_v[4 + j, :, lo:hi])
        dx, dg = _rms_bwd_tile(x_ref[...], g_ref[...], dn)
        dx_ref[...] = dht + dx
        _accumulate(dg_ref, step, dg)

    return pl.pallas_call(
        body, grid=(t // TM,), name=name,
        in_specs=[_row_spec(TM, D), _row_spec(TM, D), _const_spec((1, D)), _blk_row_spec(4, TM, FF_PAD),
                  _blk_row_spec(4, TM, FF_PAD), _ANY, _ANY],
        out_specs=[_blk_row_spec(N_DEV, TM, FF_PAD), _row_spec(TM, D), _row_spec(TM, D), _const_spec((1, D))],
        out_shape=[jax.ShapeDtypeStruct((N_DEV, t, FF_PAD), BF16), jax.ShapeDtypeStruct((t, D), BF16),
                   jax.ShapeDtypeStruct((t, D), F32), jax.ShapeDtypeStruct((1, D), F32)],
        scratch_shapes=[pltpu.VMEM((N_DEV, D, FF_PAD), BF16), pltpu.VMEM((4, FF_PAD, D), BF16),
                        pltpu.SemaphoreType.DMA((1 + N_DEV,))],
        compiler_params=_cparams(),
    )(dh, xin, g, gate, up, wgu, p_all)


def _pick_tile(n, options=(512, 256, 128)):
    for o in options:
        if n % o == 0:
            return o
    return n


def _mm_tn(a, b, name):
    k, m = a.shape
    _, n = b.shape
    tm, tn = _pick_tile(m), _pick_tile(n)

    def body(a_ref, b_ref, o_ref):
        o_ref[...] = _dot_tn(a_ref[...].astype(BF16), b_ref[...].astype(BF16)).astype(BF16)

    return pl.pallas_call(
        body, grid=(m // tm, n // tn), name=name,
        in_specs=[pl.BlockSpec((k, tm), lambda i, j: (0, i)), pl.BlockSpec((k, tn), lambda i, j: (0, j))],
        out_specs=pl.BlockSpec((tm, tn), lambda i, j: (i, j)),
        out_shape=jax.ShapeDtypeStruct((m, n), BF16),
        compiler_params=_cparams(2),
    )(a, b)


def _mm_tn_cols(a, b, name):
    k, m = a.shape
    nb, _, n = b.shape
    tm = _pick_tile(m)

    def body(a_ref, b_ref, o_ref):
        o_ref[0] = _dot_tn(a_ref[...].astype(BF16), b_ref[0].astype(BF16)).astype(BF16)

    return pl.pallas_call(
        body, grid=(nb, m // tm), name=name,
        in_specs=[pl.BlockSpec((k, tm), lambda j, i: (0, i)), pl.BlockSpec((1, k, n), lambda j, i: (j, 0, 0))],
        out_specs=pl.BlockSpec((1, tm, n), lambda j, i: (j, i, 0)),
        out_shape=jax.ShapeDtypeStruct((nb, m, n), BF16),
        compiler_params=_cparams(2),
    )(a, b)


def _mm_tn_rows(a, b, keep, name):
    nb, k, m = a.shape
    _, n = b.shape
    tn = _pick_tile(n)

    def body(a_ref, b_ref, o_ref):
        o_ref[0] = _dot_tn(a_ref[0], b_ref[...])[:keep].astype(BF16)

    return pl.pallas_call(
        body, grid=(nb, n // tn), name=name,
        in_specs=[pl.BlockSpec((1, k, m), lambda j, i: (j, 0, 0)), pl.BlockSpec((k, tn), lambda j, i: (0, i))],
        out_specs=pl.BlockSpec((1, keep, tn), lambda j, i: (j, 0, i)),
        out_shape=jax.ShapeDtypeStruct((nb, keep, n), BF16),
        compiler_params=_cparams(2),
    )(a, b)


PCG_W = 5 * D
QKV_W = 3 * D
PROJ_SUB = 512


def _w_in_pair(p_hbm, dst):
    off = P_OFF["w_in"]
    return (p_hbm.at[:, off:off + D, :], dst)


def _inproj_fwd(h, g, p_all, name):
    t = h.shape[0]

    def body(h_ref, g_ref, p_hbm, u_ref, pcg_ref, qkv_ref, w_v, sems):
        _load_resident(pl.program_id(0), [_w_in_pair(p_hbm, w_v)], sems)
        u = _rms_fwd_tile(h_ref[...], g_ref[...]).astype(BF16)
        u_ref[...] = u
        for blk in range(N_DEV):
            for s in range(D // PROJ_SUB):
                lo, hi = s * PROJ_SUB, (s + 1) * PROJ_SUB
                p = _dot(u, w_v[blk, :, lo:hi])
                if blk < 3:
                    pcg_ref[:, blk * D + lo:blk * D + hi] = p
                elif blk < 6:
                    qkv_ref[:, (blk - 3) * D + lo:(blk - 3) * D + hi] = p.astype(BF16)
                else:
                    pcg_ref[:, (blk - 3) * D + lo:(blk - 3) * D + hi] = p

    return pl.pallas_call(
        body, grid=(t // TM,), name=name,
        in_specs=[_row_spec(TM, D), _const_spec((1, D)), _ANY],
        out_specs=[_row_spec(TM, D), _row_spec(TM, PCG_W), _row_spec(TM, QKV_W)],
        out_shape=[jax.ShapeDtypeStruct((t, D), BF16), jax.ShapeDtypeStruct((t, PCG_W), F32),
                   jax.ShapeDtypeStruct((t, QKV_W), BF16)],
        scratch_shapes=[pltpu.VMEM((N_DEV, D, D), BF16), pltpu.SemaphoreType.DMA((1,))],
        compiler_params=_cparams(),
    )(h, g, p_all)


CONV_CW = 256


def _shift_down(v, k, rows):
    return jnp.where(rows >= k, pltpu.roll(v, k, 0), 0.0)


def _shift_up(v, k, rows, t):
    return jnp.where(rows < t - k, pltpu.roll(v, t - k, 0), 0.0)


def _col_spec(t, cw, off):
    return pl.BlockSpec((t, cw), lambda j: (0, j + off))


def _conv_fwd(pcg, conv_w, name):
    t = pcg.shape[0]
    nb = D // CONV_CW

    def body(cb_ref, cc_ref, cx_ref, w_ref, y_ref):
        rows = lax.broadcasted_iota(jnp.int32, (t, CONV_CW), 0)
        xc = cc_ref[...] * cx_ref[...]
        conv = (w_ref[0:1, :] * _shift_down(xc, 2, rows) + w_ref[1:2, :] * _shift_down(xc, 1, rows)
                + w_ref[2:3, :] * xc)
        y_ref[...] = (cb_ref[...] * conv).astype(BF16)

    return pl.pallas_call(
        body, grid=(nb,), name=name,
        in_specs=[_col_spec(t, CONV_CW, 0), _col_spec(t, CONV_CW, nb), _col_spec(t, CONV_CW, 2 * nb),
                  pl.BlockSpec((CONV_K, CONV_CW), lambda j: (0, j))],
        out_specs=_col_spec(t, CONV_CW, 0),
        out_shape=jax.ShapeDtypeStruct((t, D), BF16),
        compiler_params=_cparams(),
    )(pcg, pcg, pcg, conv_w)


def _conv_bwd(pcg, conv_w, dyc, name):
    t = pcg.shape[0]
    nb = D // CONV_CW

    def body(cb_ref, cc_ref, cx_ref, w_ref, dy_ref, dcb_ref, dcc_ref, dcx_ref, dw_ref):
        rows = lax.broadcasted_iota(jnp.int32, (t, CONV_CW), 0)
        cc, cx = cc_ref[...], cx_ref[...]
        xc = cc * cx
        x1 = _shift_down(xc, 1, rows)
        x2 = _shift_down(xc, 2, rows)
        w0, w1, w2 = w_ref[0:1, :], w_ref[1:2, :], w_ref[2:3, :]
        conv = w0 * x2 + w1 * x1 + w2 * xc
        dy = dy_ref[...]
        dcb_ref[...] = (dy * conv).astype(BF16)
        dconv = dy * cb_ref[...]
        dw_ref[...] = jnp.zeros((8, CONV_CW), F32)
        dw_ref[0:1, :] = jnp.sum(dconv * x2, axis=0, keepdims=True)
        dw_ref[1:2, :] = jnp.sum(dconv * x1, axis=0, keepdims=True)
        dw_ref[2:3, :] = jnp.sum(dconv * xc, axis=0, keepdims=True)
        dxc = w2 * dconv + w1 * _shift_up(dconv, 1, rows, t) + w0 * _shift_up(dconv, 2, rows, t)
        dcc_ref[...] = (dxc * cx).astype(BF16)
        dcx_ref[...] = (dxc * cc).astype(BF16)

    return pl.pallas_call(
        body, grid=(nb,), name=name,
        in_specs=[_col_spec(t, CONV_CW, 0), _col_spec(t, CONV_CW, nb), _col_spec(t, CONV_CW, 2 * nb),
                  pl.BlockSpec((CONV_K, CONV_CW), lambda j: (0, j)), _col_spec(t, CONV_CW, 0)],
        out_specs=[_col_spec(t, CONV_CW, 0), _col_spec(t, CONV_CW, 0), _col_spec(t, CONV_CW, 0),
                   pl.BlockSpec((8, CONV_CW), lambda j: (0, j))],
        out_shape=[jax.ShapeDtypeStruct((t, D), BF16)] * 3 + [jax.ShapeDtypeStruct((8, D), F32)],
        compiler_params=_cparams(),
    )(pcg, pcg, pcg, conv_w, dyc)


def _tri2(cond):
    rr = lax.broadcasted_iota(jnp.int32, (2 * TQ, TQ), 0) & (TQ - 1)
    cc = lax.broadcasted_iota(jnp.int32, (2 * TQ, TQ), 1)
    return cond(rr, cc).astype(BF16)


def _cumdot(v, tri2):
    hi = v.astype(BF16)
    lo = (v - hi.astype(F32)).astype(BF16)
    return _dot(jnp.concatenate([hi, lo], axis=1), tri2)


def _log_1m_beta(z):
    return -(jnp.maximum(z, 0.0) + jnp.log(1.0 + jnp.exp(-jnp.abs(z))))


def _sb_specs(t):
    g = SB_H // SB_HPS
    w = SB_HPS * SB_DH
    q_spec = pl.BlockSpec((TQ, w), lambda h, i: (i, h))
    k_spec = pl.BlockSpec((t, w), lambda h, i: (0, g + h))
    v_spec = pl.BlockSpec((t, w), lambda h, i: (0, 2 * g + h))
    ct_spec = pl.BlockSpec((SB_HPS, TQ, 1), lambda h, i: (h, i, 0))
    return g, w, q_spec, k_spec, v_spec, ct_spec


def _sb_fwd(qkv, name):
    t = qkv.shape[0]
    scale = SB_DH ** -0.5
    g, w, q_spec, k_spec, v_spec, ct_spec = _sb_specs(t)

    def body(q_ref, k_ref, v_ref, y_ref, ct_ref):
        i = pl.program_id(1)
        later = _tri2(lambda j, s: j > s)
        rr = lax.broadcasted_iota(jnp.int32, (TQ, TQ), 0)
        cc = lax.broadcasted_iota(jnp.int32, (TQ, TQ), 1)
        causal = cc < rr

        def block(j, carry, masked):
            off = pl.multiple_of(j * TQ, TQ)
            out = []
            for hd in range(SB_HPS):
                acc, c_sum = carry[hd]
                cols = slice(hd * SB_DH, (hd + 1) * SB_DH)
                kj = k_ref[pl.ds(off, TQ), cols]
                vj = v_ref[pl.ds(off, TQ), cols]
                z = _dot_nt(q_ref[:, cols], kj) * scale
                m = _log_1m_beta(z)
                if masked:
                    m = jnp.where(causal, m, 0.0)
                a = jnp.exp((m + z) + (c_sum + _cumdot(m, later)))
                if masked:
                    a = jnp.where(causal, a, 0.0)
                out.append((acc + _dot(a.astype(BF16), vj), c_sum + jnp.sum(m, axis=1, keepdims=True)))
            return tuple(out)

        init = tuple((jnp.zeros((TQ, SB_DH), F32), jnp.zeros((TQ, 1), F32)) for _ in range(SB_HPS))
        carry = block(i, init, True)
        carry = lax.fori_loop(0, i, lambda jj, c: block(i - 1 - jj, c, False), carry)
        for hd in range(SB_HPS):
            y_ref[:, hd * SB_DH:(hd + 1) * SB_DH] = carry[hd][0].astype(BF16)
            ct_ref[hd] = carry[hd][1]

    return pl.pallas_call(
        body, grid=(g, t // TQ), name=name,
        in_specs=[q_spec, k_spec, v_spec],
        out_specs=[q_spec, ct_spec],
        out_shape=[jax.ShapeDtypeStruct((t, D), BF16), jax.ShapeDtypeStruct((SB_H, t, 1), F32)],
        compiler_params=_cparams(2),
    )(qkv, qkv, qkv)


def _sb_bwd(qkv, dy, ctot, name):
    t = qkv.shape[0]
    scale = SB_DH ** -0.5
    g, w, q_spec, k_spec, v_spec, ct_spec = _sb_specs(t)
    acc_spec = pl.BlockSpec((t, w), lambda h, i: (0, h))

    def body(q_ref, k_ref, v_ref, dy_ref, ct_ref, dq_ref, dk_ref, dv_ref):
        i = pl.program_id(1)

        @pl.when(i == 0)
        def _():
            dk_ref[...] = jnp.zeros_like(dk_ref)
            dv_ref[...] = jnp.zeros_like(dv_ref)

        upto = _tri2(lambda j, s: j <= s)
        before = _tri2(lambda j, s: j < s)
        rr = lax.broadcasted_iota(jnp.int32, (TQ, TQ), 0)
        cc = lax.broadcasted_iota(jnp.int32, (TQ, TQ), 1)
        causal = cc < rr

        def block(j, carry, masked):
            off = pl.multiple_of(j * TQ, TQ)
            out = []
            for hd in range(SB_HPS):
                dq, p_sum, e_sum = carry[hd]
                cols = slice(hd * SB_DH, (hd + 1) * SB_DH)
                q = q_ref[:, cols]
                dy_t = dy_ref[:, cols]
                kj = k_ref[pl.ds(off, TQ), cols]
                vj = v_ref[pl.ds(off, TQ), cols]
                z = _dot_nt(q, kj) * scale
                m = _log_1m_beta(z)
                if masked:
                    m = jnp.where(causal, m, 0.0)
                l = m + z
                a = jnp.exp(l + ((ct_ref[hd] - p_sum) - _cumdot(m, upto)))
                if masked:
                    a = jnp.where(causal, a, 0.0)
                e = _dot_nt(dy_t, vj) * a
                e_before = e_sum + _cumdot(e, before)
                beta = jnp.exp(l)
                dz = e * (1.0 - beta) - e_before * beta
                if masked:
                    dz = jnp.where(causal, dz, 0.0)
                dzs = (dz * scale).astype(BF16)
                dk_ref[pl.ds(off, TQ), cols] += _dot_tn(dzs, q)
                dv_ref[pl.ds(off, TQ), cols] += _dot_tn(a.astype(BF16), dy_t)
                out.append((dq + _dot(dzs, kj), p_sum + jnp.sum(m, axis=1, keepdims=True),
                            e_sum + jnp.sum(e, axis=1, keepdims=True)))
            return tuple(out)

        zero = jnp.zeros((TQ, 1), F32)
        init = tuple((jnp.zeros((TQ, SB_DH), F32), zero, zero) for _ in range(SB_HPS))
        carry = lax.fori_loop(0, i, lambda j, c: block(j, c, False), init)
        carry = block(i, carry, True)
        for hd in range(SB_HPS):
            dq_ref[:, hd * SB_DH:(hd + 1) * SB_DH] = carry[hd][0].astype(BF16)

    return pl.pallas_call(
        body, grid=(g, t // TQ), name=name,
        in_specs=[q_spec, k_spec, v_spec, q_spec, ct_spec],
        out_specs=[q_spec, acc_spec, acc_spec],
        out_shape=[jax.ShapeDtypeStruct((t, D), BF16), jax.ShapeDtypeStruct((t, D), F32),
                   jax.ShapeDtypeStruct((t, D), F32)],
        compiler_params=_cparams(2),
    )(qkv, qkv, qkv, dy, ctot)


def _gate_specs():
    return [pl.BlockSpec((TM, D), lambda i: (i, 3)), pl.BlockSpec((TM, D), lambda i: (i, 4))]


_MIX_MATS = ("w_conv_out", "w_attn_out", "w_o")


def _mix_pairs(p_hbm, dsts):
    pairs = []
    for nm, dst in zip(_MIX_MATS, dsts):
        pairs += _square_pairs(p_hbm, nm, dst)
    return pairs


def _mix_out_fwd(yc, ysb, pcg, b_gate, h, p_all, name):
    t = h.shape[0]

    def body(yc_ref, ysb_ref, gc_ref, gs_ref, b_ref, h_ref, p_hbm,
             a_ref, b_out_ref, mg_ref, h2_ref, wc_v, wa_v, wo_v, sems):
        _load_resident(pl.program_id(0), _mix_pairs(p_hbm, (wc_v, wa_v, wo_v)), sems)
        a = _dot(yc_ref[...], wc_v[...])
        b = _dot(ysb_ref[...], wa_v[...])
        merged = (_sigmoid(gc_ref[...] + b_ref[:, :D]) * a + _sigmoid(gs_ref[...] + b_ref[:, D:]) * b).astype(BF16)
        a_ref[...] = a
        b_out_ref[...] = b
        mg_ref[...] = merged
        h2_ref[...] = h_ref[...] + _dot(merged, wo_v[...])

    return pl.pallas_call(
        body, grid=(t // TM,), name=name,
        in_specs=[_row_spec(TM, D), _row_spec(TM, D)] + _gate_specs()
                 + [_const_spec((1, 2 * D)), _row_spec(TM, D), _ANY],
        out_specs=[_row_spec(TM, D)] * 4,
        out_shape=[jax.ShapeDtypeStruct((t, D), F32), jax.ShapeDtypeStruct((t, D), F32),
                   jax.ShapeDtypeStruct((t, D), BF16), jax.ShapeDtypeStruct((t, D), F32)],
        scratch_shapes=[pltpu.VMEM((D, D), BF16)] * 3 + [pltpu.SemaphoreType.DMA((3 * N_DEV,))],
        compiler_params=_cparams(),
    )(yc, ysb, pcg, pcg, b_gate, h, p_all)


def _mix_out_bwd(dh2, a, b, pcg, b_gate, p_all, name):
    t = dh2.shape[0]

    def body(dh_ref, a_ref, b_ref, gc_ref, gs_ref, bias_ref, p_hbm,
             dhb_ref, da_ref, db_ref, dgp_ref, dyc_ref, dysb_ref, dbias_ref, wc_v, wa_v, wo_v, sems):
        step = pl.program_id(0)
        _load_resident(step, _mix_pairs(p_hbm, (wc_v, wa_v, wo_v)), sems)
        dhb = dh_ref[...].astype(BF16)
        dhb_ref[...] = dhb
        dm = _dot_nt(dhb, wo_v[...])
        gc = _sigmoid(gc_ref[...] + bias_ref[:, :D])
        gs = _sigmoid(gs_ref[...] + bias_ref[:, D:])
        da = (dm * gc).astype(BF16)
        db = (dm * gs).astype(BF16)
        da_ref[...] = da
        db_ref[...] = db
        dgc = dm * a_ref[...] * (gc * (1.0 - gc))
        dgs = dm * b_ref[...] * (gs * (1.0 - gs))
        dgp_ref[0] = dgc.astype(BF16)
        dgp_ref[1] = dgs.astype(BF16)
        _accumulate(dbias_ref.at[:, :D], step, jnp.sum(dgc, axis=0, keepdims=True))
        _accumulate(dbias_ref.at[:, D:], step, jnp.sum(dgs, axis=0, keepdims=True))
        dyc_ref[...] = _dot_nt(da, wc_v[...])
        dysb_ref[...] = _dot_nt(db, wa_v[...]).astype(BF16)

    return pl.pallas_call(
        body, grid=(t // TM,), name=name,
        in_specs=[_row_spec(TM, D)] * 3 + _gate_specs() + [_const_spec((1, 2 * D)), _ANY],
        out_specs=[_row_spec(TM, D)] * 3 + [_blk_row_spec(2, TM, D), _row_spec(TM, D), _row_spec(TM, D),
                                            _const_spec((1, 2 * D))],
        out_shape=[jax.ShapeDtypeStruct((t, D), BF16)] * 3
                  + [jax.ShapeDtypeStruct((2, t, D), BF16), jax.ShapeDtypeStruct((t, D), F32),
                     jax.ShapeDtypeStruct((t, D), BF16), jax.ShapeDtypeStruct((1, 2 * D), F32)],
        scratch_shapes=[pltpu.VMEM((D, D), BF16)] * 3 + [pltpu.SemaphoreType.DMA((3 * N_DEV,))],
        compiler_params=_cparams(),
    )(dh2, a, b, pcg, pcg, b_gate, p_all)


def _inproj_bwd(pieces, dgp, p_all, h, g, dh_res, name):
    t = h.shape[0]
    n_piece = len(pieces)

    def body(*refs):
        piece_refs = refs[:n_piece]
        dgp_ref, p_hbm, h_ref, g_ref, dres_ref, dh_ref, dg_ref, w_v, sems = refs[n_piece:]
        step = pl.program_id(0)
        _load_resident(step, [_w_in_pair(p_hbm, w_v)], sems)
        du = jnp.zeros((TM, D), F32)
        for p, ref in enumerate(piece_refs):
            du = du + _dot_nt(ref[...].astype(BF16), w_v[p])
        for k in range(2):
            du = du + _dot_nt(dgp_ref[k], w_v[n_piece + k])
        dx, dg = _rms_bwd_tile(h_ref[...], g_ref[...], du)
        dh_ref[...] = dres_ref[...] + dx
        _accumulate(dg_ref, step, dg)

    return pl.pallas_call(
        body, grid=(t // TM,), name=name,
        in_specs=[_row_spec(TM, D)] * n_piece + [_blk_row_spec(2, TM, D), _ANY, _row_spec(TM, D),
                                                  _const_spec((1, D)), _row_spec(TM, D)],
        out_specs=[_row_spec(TM, D), _const_spec((1, D))],
        out_shape=[jax.ShapeDtypeStruct((t, D), F32), jax.ShapeDtypeStruct((1, D), F32)],
        scratch_shapes=[pltpu.VMEM((N_DEV, D, D), BF16), pltpu.SemaphoreType.DMA((1,))],
        compiler_params=_cparams(),
    )(*pieces, dgp, p_all, h, g, dh_res)


def _memkv_fwd(mem, g, w_ckv, name):
    m = mem.shape[0]

    def body(mem_ref, g_ref, w_ref, mn_ref, kv_ref):
        mn = _rms_fwd_tile(mem_ref[...], g_ref[...]).astype(BF16)
        mn_ref[...] = mn
        for j in range(N_DEV):
            kv_ref[j] = _dot(mn, w_ref[j]).astype(BF16)

    return pl.pallas_call(
        body, grid=(1,), name=name,
        in_specs=[_const_spec((m, D)), _const_spec((1, D)), _const_spec((N_DEV, D, X_DH))],
        out_specs=[_const_spec((m, D)), _const_spec((N_DEV, m, X_DH))],
        out_shape=[jax.ShapeDtypeStruct((m, D), BF16), jax.ShapeDtypeStruct((N_DEV, m, X_DH), BF16)],
        compiler_params=_cparams(),
    )(mem, g, w_ckv)


def _memkv_bwd(dkv, mem, g, w_ckv, name):
    m = mem.shape[0]

    def body(dkv_ref, mem_ref, g_ref, w_ref, dg_ref):
        dmn = jnp.zeros((m, D), F32)
        for j in range(N_DEV):
            dmn = dmn + _dot_nt(dkv_ref[j].astype(BF16), w_ref[j])
        _, dg = _rms_bwd_tile(mem_ref[...], g_ref[...], dmn)
        dg_ref[...] = dg

    return pl.pallas_call(
        body, grid=(1,), name=name,
        in_specs=[_const_spec((N_DEV, m, X_DH)), _const_spec((m, D)), _const_spec((1, D)),
                  _const_spec((N_DEV, D, X_DH))],
        out_specs=_const_spec((1, D)),
        out_shape=jax.ShapeDtypeStruct((1, D), F32),
        compiler_params=_cparams(),
    )(dkv, mem, g, w_ckv)


def _softmax_rows(s):
    e = jnp.exp(s - jnp.max(s, axis=-1, keepdims=True))
    return e / jnp.sum(e, axis=-1, keepdims=True)


def _cross_pairs(p_hbm, wq_v, wo_v):
    return _square_pairs(p_hbm, "w_cq", wq_v) + _square_pairs(p_hbm, "w_co", wo_v)


def _cross_fwd(h, g, kv, p_all, name):
    t = h.shape[0]
    m = kv.shape[1]
    scale = X_DH ** -0.5

    def body(h_ref, g_ref, kv_ref, p_hbm, hn_ref, qx_ref, o_ref, h3_ref, wq_v, wo_v, sems):
        _load_resident(pl.program_id(0), _cross_pairs(p_hbm, wq_v, wo_v), sems)
        ht = h_ref[...]
        hn = _rms_fwd_tile(ht, g_ref[...]).astype(BF16)
        hn_ref[...] = hn
        qx = _dot(hn, wq_v[...]).astype(BF16)
        qx_ref[...] = qx
        for hd in range(X_H):
            lo, hi = hd * X_DH, (hd + 1) * X_DH
            p = _softmax_rows(_dot_nt(qx[:, lo:hi], kv_ref[hd]) * scale)
            o_ref[:, lo:hi] = _dot(p.astype(BF16), kv_ref[X_H + hd]).astype(BF16)
        h3_ref[...] = ht + _dot(o_ref[...], wo_v[...])

    return pl.pallas_call(
        body, grid=(t // TM,), name=name,
        in_specs=[_row_spec(TM, D), _const_spec((1, D)), _const_spec((N_DEV, m, X_DH)), _ANY],
        out_specs=[_row_spec(TM, D)] * 4,
        out_shape=[jax.ShapeDtypeStruct((t, D), BF16)] * 3 + [jax.ShapeDtypeStruct((t, D), F32)],
        scratch_shapes=[pltpu.VMEM((D, D), BF16)] * 2 + [pltpu.SemaphoreType.DMA((2 * N_DEV,))],
        compiler_params=_cparams(),
    )(h, g, kv, p_all)


def _cross_bwd(dh3, h, g, qx, kv, p_all, name):
    t = h.shape[0]
    m = kv.shape[1]
    scale = X_DH ** -0.5

    def body(dh_ref, h_ref, g_ref, qx_ref, kv_ref, p_hbm,
             dhb_ref, dqx_ref, dkv_ref, dh2_ref, dg_ref, wq_v, wo_v, sems):
        step = pl.program_id(0)
        _load_resident(step, _cross_pairs(p_hbm, wq_v, wo_v), sems)

        @pl.when(step == 0)
        def _():
            dkv_ref[...] = jnp.zeros_like(dkv_ref)

        dht = dh_ref[...]
        dhb = dht.astype(BF16)
        dhb_ref[...] = dhb
        do = _dot_nt(dhb, wo_v[...]).astype(BF16)
        for hd in range(X_H):
            lo, hi = hd * X_DH, (hd + 1) * X_DH
            qh = qx_ref[:, lo:hi]
            kh = kv_ref[hd]
            p = _softmax_rows(_dot_nt(qh, kh) * scale)
            doh = do[:, lo:hi]
            dp = _dot_nt(doh, kv_ref[X_H + hd])
            ds = (p * (dp - jnp.sum(dp * p, axis=-1, keepdims=True)) * scale).astype(BF16)
            dqx_ref[:, lo:hi] = _dot(ds, kh).astype(BF16)
            dkv_ref[hd] += _dot_tn(ds, qh)
            dkv_ref[X_H + hd] += _dot_tn(p.astype(BF16), doh)
        dhn = _dot_nt(dqx_ref[...], wq_v[...])
        dx, dg = _rms_bwd_tile(h_ref[...], g_ref[...], dhn)
        dh2_ref[...] = dht + dx
        _accumulate(dg_ref, step, dg)

    return pl.pallas_call(
        body, grid=(t // TM,), name=name,
        in_specs=[_row_spec(TM, D), _row_spec(TM, D), _const_spec((1, D)), _row_spec(TM, D),
                  _const_spec((N_DEV, m, X_DH)), _ANY],
        out_specs=[_row_spec(TM, D), _row_spec(TM, D), _const_spec((N_DEV, m, X_DH)), _row_spec(TM, D),
                   _const_spec((1, D))],
        out_shape=[jax.ShapeDtypeStruct((t, D), BF16), jax.ShapeDtypeStruct((t, D), BF16),
                   jax.ShapeDtypeStruct((N_DEV, m, X_DH), F32), jax.ShapeDtypeStruct((t, D), F32),
                   jax.ShapeDtypeStruct((1, D), F32)],
        scratch_shapes=[pltpu.VMEM((D, D), BF16)] * 2 + [pltpu.SemaphoreType.DMA((2 * N_DEV,))],
        compiler_params=_cparams(),
    )(dh3, h, g, qx, kv, p_all)


def _loss_bwd(h, g, target, name):
    t = h.shape[0]

    def body(h_ref, g_ref, t_ref, loss_ref, dh_ref, dg_ref):
        step = pl.program_id(0)
        ht = h_ref[...]
        gain = g_ref[...]
        diff = _rms_fwd_tile(ht, gain) - t_ref[...]
        part = 0.5 * jnp.sum(jnp.sum(diff * diff, axis=-1, keepdims=True) / D, axis=0, keepdims=True)
        dx, dg = _rms_bwd_tile(ht, gain, diff / D)
        dh_ref[...] = dx
        _accumulate(loss_ref, step, jnp.broadcast_to(part, (8, 128)))
        _accumulate(dg_ref, step, dg)

    return pl.pallas_call(
        body, grid=(t // TM,), name=name,
        in_specs=[_row_spec(TM, D), _const_spec((1, D)), _row_spec(TM, D)],
        out_specs=[_const_spec((8, 128)), _row_spec(TM, D), _const_spec((1, D))],
        out_shape=[jax.ShapeDtypeStruct((8, 128), F32), jax.ShapeDtypeStruct((t, D), F32),
                   jax.ShapeDtypeStruct((1, D), F32)],
        compiler_params=_cparams(),
    )(h, g, target)


def _adamw(w, gr, m, v, name):
    r, c = w.shape
    tr = _pick_tile(r, (256, 128)) if r > 512 else r

    def body(w_ref, g_ref, m_ref, v_ref, d_ref, nm_ref, nv_ref):
        gt = g_ref[...]
        nm = ADAM_B1 * m_ref[...] + (1.0 - ADAM_B1) * gt
        nv = ADAM_B2 * v_ref[...] + (1.0 - ADAM_B2) * jnp.square(gt)
        m_hat = nm / (1.0 - ADAM_B1 ** ADAM_STEP)
        v_hat = nv / (1.0 - ADAM_B2 ** ADAM_STEP)
        d_ref[...] = -ADAM_LR * (m_hat / (jnp.sqrt(v_hat) + ADAM_EPS) + ADAM_WD * w_ref[...])
        nm_ref[...] = nm
        nv_ref[...] = nv

    spec = _row_spec(tr, c)
    return pl.pallas_call(
        body, grid=(r // tr,), name=name,
        in_specs=[spec] * 4, out_specs=[spec] * 3,
        out_shape=[jax.ShapeDtypeStruct((r, c), F32)] * 3,
        compiler_params=_cparams(),
    )(w, gr, m, v)


def _mesh_pos():
    return lax.axis_index("x"), lax.axis_index("y"), lax.axis_index("c")


def _all_gather(shards, name):
    n_arr = len(shards)

    def body(*refs):
        x_refs, out_refs = refs[:n_arr], refs[n_arr:2 * n_arr]
        send_sems, recv_sems, local_sems = refs[2 * n_arr:]
        x, y, c = _mesh_pos()
        me, sibling = (x, y, c), (x, y, 1 - c)
        chips = [(1 - x, y), (x, 1 - y), (1 - x, 1 - y)]

        def slot(a, px, py, pc):
            return out_refs[a].at[4 * px + 2 * py + pc]

        def copy(a, k, block, to, src=None):
            return pltpu.make_async_remote_copy(
                src_ref=slot(a, *block) if src is None else src, dst_ref=slot(a, *block),
                send_sem=send_sems.at[a, k], recv_sem=recv_sems.at[a, k],
                device_id=to, device_id_type=pl.DeviceIdType.MESH)

        mine = [pltpu.make_async_copy(x_refs[a], slot(a, *me), local_sems.at[a]) for a in range(n_arr)]
        for cp in mine:
            cp.start()
        first = []
        for a in range(n_arr):
            first.append(copy(a, 0, me, sibling, src=x_refs[a]))
            first += [copy(a, 1 + j, me, (*chip, c), src=x_refs[a]) for j, chip in enumerate(chips)]
        for cp in first:
            cp.start()
        passed = []
        for a in range(n_arr):
            for j, chip in enumerate(chips):
                copy(a, 1 + j, (*chip, c), me).wait_recv()
                fwd = copy(a, 4 + j, (*chip, c), sibling)
                fwd.start()
                passed.append(fwd)
        for a in range(n_arr):
            copy(a, 0, sibling, me).wait_recv()
            for j, chip in enumerate(chips):
                copy(a, 4 + j, (*chip, 1 - c), me).wait_recv()
        for cp in first + passed:
            cp.wait_send()
        for cp in mine:
            cp.wait()

    return pl.pallas_call(
        body, name=name,
        out_shape=[jax.ShapeDtypeStruct((N_DEV,) + s.shape, s.dtype) for s in shards],
        in_specs=[_ANY] * n_arr, out_specs=[_ANY] * n_arr,
        scratch_shapes=[pltpu.SemaphoreType.DMA((n_arr, 7)), pltpu.SemaphoreType.DMA((n_arr, 7)),
                        pltpu.SemaphoreType.DMA((n_arr,))],
    )(*shards)


def _pair_exchange(grads, name):
    n_arr = len(grads)

    def body(*refs):
        g_refs, land_refs = refs[:n_arr], refs[n_arr:2 * n_arr]
        send_sems, recv_sems = refs[2 * n_arr:]
        x, y, c = _mesh_pos()
        copies = []
        for a in range(n_arr):
            for k in range(N_CHIP):
                copies.append(pltpu.make_async_remote_copy(
                    src_ref=g_refs[a].at[2 * k + 1 - c], dst_ref=land_refs[a].at[k],
                    send_sem=send_sems.at[a, k], recv_sem=recv_sems.at[a, k],
                    device_id=(x, y, 1 - c), device_id_type=pl.DeviceIdType.MESH))
        for cp in copies:
            cp.start()
        for cp in copies:
            cp.wait()

    return pl.pallas_call(
        body, name=name,
        out_shape=[jax.ShapeDtypeStruct((N_CHIP,) + g.shape[1:], g.dtype) for g in grads],
        in_specs=[_ANY] * n_arr, out_specs=[_ANY] * n_arr,
        scratch_shapes=[pltpu.SemaphoreType.DMA((n_arr, N_CHIP)), pltpu.SemaphoreType.DMA((n_arr, N_CHIP))],
    )(*grads)


def _row_tile(r, cap=640):
    best = None
    for cand in range(16, min(r, cap) + 1, 16):
        if r % cand == 0:
            best = cand
    return best if best is not None else r


def _pair_sum(g, landed, core, name):
    _, r, c_dim = g.shape
    tr = _row_tile(r)

    def body(core_ref, mine_ref, theirs_ref, o_ref):
        o_ref[0] = (mine_ref[0].astype(F32) + theirs_ref[0].astype(F32)).astype(o_ref.dtype)

    return pl.pallas_call(
        body, name=name,
        grid_spec=pltpu.PrefetchScalarGridSpec(
            num_scalar_prefetch=1, grid=(N_CHIP, r // tr),
            in_specs=[pl.BlockSpec((1, tr, c_dim), lambda k, i, core_ref: (2 * k + core_ref[0], i, 0)),
                      pl.BlockSpec((1, tr, c_dim), lambda k, i, core_ref: (k, i, 0))],
            out_specs=pl.BlockSpec((1, tr, c_dim), lambda k, i, core_ref: (k, i, 0))),
        out_shape=jax.ShapeDtypeStruct((N_CHIP, r, c_dim), g.dtype),
        compiler_params=_cparams(2),
    )(core, g, landed)


def _chip_exchange(parts, name):
    n_arr = len(parts)

    def body(*refs):
        p_refs, land_refs = refs[:n_arr], refs[n_arr:2 * n_arr]
        send_sems, recv_sems, local_sems = refs[2 * n_arr:]
        x, y, c = _mesh_pos()
        my_chip = 2 * x + y
        chips = [(1 - x, y), (x, 1 - y), (1 - x, 1 - y)]
        local = [pltpu.make_async_copy(p_refs[a].at[my_chip], land_refs[a].at[my_chip], local_sems.at[a])
                 for a in range(n_arr)]
        for cp in local:
            cp.start()

        def copy(a, k, src_slot, dst_slot, px, py):
            return pltpu.make_async_remote_copy(
                src_ref=p_refs[a].at[src_slot], dst_ref=land_refs[a].at[dst_slot],
                send_sem=send_sems.at[a, k], recv_sem=recv_sems.at[a, k],
                device_id=(px, py, c), device_id_type=pl.DeviceIdType.MESH)

        sends = [copy(a, k, 2 * px + py, my_chip, px, py) for a in range(n_arr) for k, (px, py) in enumerate(chips)]
        for cp in sends:
            cp.start()
        for a in range(n_arr):
            for k, (px, py) in enumerate(chips):
                copy(a, k, my_chip, 2 * px + py, px, py).wait_recv()
        for cp in sends:
            cp.wait_send()
        for cp in local:
            cp.wait()

    return pl.pallas_call(
        body, name=name,
        out_shape=[jax.ShapeDtypeStruct(p.shape, p.dtype) for p in parts],
        in_specs=[_ANY] * n_arr, out_specs=[_ANY] * n_arr,
        scratch_shapes=[pltpu.SemaphoreType.DMA((n_arr, 3)), pltpu.SemaphoreType.DMA((n_arr, 3)),
                        pltpu.SemaphoreType.DMA((n_arr,))],
    )(*parts)


def _sum_slots(parts, name):
    n, r, c_dim = parts.shape
    tr = _row_tile(r)

    def body(p_ref, o_ref):
        acc = p_ref[0].astype(F32)
        for k in range(1, n):
            acc = acc + p_ref[k].astype(F32)
        o_ref[...] = acc

    return pl.pallas_call(
        body, grid=(r // tr,), name=name,
        in_specs=[pl.BlockSpec((n, tr, c_dim), lambda i: (0, i, 0))],
        out_specs=_row_spec(tr, c_dim),
        out_shape=jax.ShapeDtypeStruct((r, c_dim), F32),
        compiler_params=_cparams(),
    )(parts)


GU_NAMES = ("w_ffn1_gu", "w_ffn2_gu")


def _local_step(x, mem, target, wg, gains, b_gate, conv_w):
    p_all = wg["p"]
    n1, gate1, up1, act1, h1 = _ffn_fwd(x, gains["g_ffn1"], wg["w_ffn1_gu"], p_all, "w_ffn1_down", "ffn1_fwd")
    u, pcg, qkv = _inproj_fwd(h1, gains["g_mix"], p_all, "inproj_fwd")
    yc = _conv_fwd(pcg, conv_w, "conv_fwd")
    ysb, ctot = _sb_fwd(qkv, "sb_fwd")
    a_mix, b_mix, merged, h2 = _mix_out_fwd(yc, ysb, pcg, b_gate, h1, p_all, "mix_out_fwd")
    mn, kv = _memkv_fwd(mem, gains["g_mem"], wg["w_ckv"], "memkv_fwd")
    hn, qx, o_x, h3 = _cross_fwd(h2, gains["g_cross"], kv, p_all, "cross_fwd")
    n4, gate2, up2, act2, h4 = _ffn_fwd(h3, gains["g_ffn2"], wg["w_ffn2_gu"], p_all, "w_ffn2_down", "ffn2_fwd")
    loss, dh4, dg_final = _loss_bwd(h4, gains["g_final"], target, "loss_bwd")

    gw, gp, gs = {}, {}, {"g_final": dg_final}
    dgu2, dh4b, dh3, gs["g_ffn2"] = _ffn_bwd(dh4, h3, gains["g_ffn2"], gate2, up2, wg["w_ffn2_gu"], p_all,
                                             "w_ffn2_down", "ffn2_bwd")
    gp["w_ffn2_down"] = _mm_tn_rows(act2, dh4b, FF_BLK, "dw_ffn2_down").reshape(N_DEV, DOWN_ROWS, D)
    gw["w_ffn2_gu"] = _mm_tn_cols(n4, dgu2, "dw_ffn2_gu")
    dh3b, dqx, dkv, dh2, gs["g_cross"] = _cross_bwd(dh3, h2, gains["g_cross"], qx, kv, p_all, "cross_bwd")
    gp["w_co"] = _mm_tn(o_x, dh3b, "dw_co").reshape(N_DEV, D // N_DEV, D)
    gp["w_cq"] = _mm_tn(hn, dqx, "dw_cq").reshape(N_DEV, D // N_DEV, D)
    gw["w_ckv"] = _mm_tn_cols(mn, dkv, "dw_ckv")
    gs["g_mem"] = _memkv_bwd(dkv, mem, gains["g_mem"], wg["w_ckv"], "memkv_bwd")
    dh2b, da_mix, db_mix, dgp, dyc, dysb, gs["b_gate"] = _mix_out_bwd(dh2, a_mix, b_mix, pcg, b_gate, p_all,
                                                                      "mix_out_bwd")
    gp["w_o"] = _mm_tn(merged, dh2b, "dw_o").reshape(N_DEV, D // N_DEV, D)
    gp["w_conv_out"] = _mm_tn(yc, da_mix, "dw_conv_out").reshape(N_DEV, D // N_DEV, D)
    gp["w_attn_out"] = _mm_tn(ysb, db_mix, "dw_attn_out").reshape(N_DEV, D // N_DEV, D)
    dq, dk, dv = _sb_bwd(qkv, dysb, ctot, "sb_bwd")
    dcb, dcc, dcx, gs["conv_w"] = _conv_bwd(pcg, conv_w, dyc, "conv_bwd")
    pieces = [dcb, dcc, dcx, dq, dk, dv]
    gp["w_in"] = jnp.concatenate(
        [_mm_tn(u, p, "dw_in_%d" % k)[None] for k, p in enumerate(pieces)] + [_mm_tn_cols(u, dgp, "dw_in_gates")],
        axis=0)
    dh1, gs["g_mix"] = _inproj_bwd(pieces, dgp, p_all, h1, gains["g_mix"], dh2, "inproj_bwd")
    dgu1, dh1b, dx, gs["g_ffn1"] = _ffn_bwd(dh1, x, gains["g_ffn1"], gate1, up1, wg["w_ffn1_gu"], p_all,
                                            "w_ffn1_down", "ffn1_bwd")
    gp["w_ffn1_down"] = _mm_tn_rows(act1, dh1b, FF_BLK, "dw_ffn1_down").reshape(N_DEV, DOWN_ROWS, D)
    gw["w_ffn1_gu"] = _mm_tn_cols(n1, dgu1, "dw_ffn1_gu")
    gw["p"] = jnp.concatenate([gp[nm] for nm, _ in P_ROWS], axis=1)
    return loss, dx, gw, gs


GAINS = ("g_ffn1", "g_mix", "g_cross", "g_mem", "g_ffn2", "g_final")
SMALL = GAINS + ("b_gate", "conv_w")
SMALL_R = 16
WEIGHT_ORDER = ("g_ffn1", "w_ffn1_gu", "w_ffn1_down", "g_mix", "w_in", "b_gate", "conv_w", "w_conv_out",
                "w_attn_out", "w_o", "g_cross", "g_mem", "w_cq", "w_ckv", "w_co", "g_ffn2", "w_ffn2_gu",
                "w_ffn2_down", "g_final")
EXCHANGED = GU_NAMES + ("w_ckv", "p")


def _pack_small(vals, conv_rows):
    rows = [vals[n].reshape(1, D) for n in GAINS] + [vals["b_gate"].reshape(2, D), conv_rows.reshape(CONV_K, D)]
    used = len(GAINS) + 2 + CONV_K
    return jnp.concatenate(rows + [jnp.zeros((SMALL_R - used, D), F32)], axis=0)


def _unpack_small(buf):
    out = {n: buf[k] for k, n in enumerate(GAINS)}
    out["b_gate"] = buf[6:8].reshape(2 * D)
    out["conv_w"] = buf[8:8 + CONV_K]
    return out


def _pack_shards(wts):
    out = {n: jnp.pad(wts[n].astype(BF16), ((0, 0), (0, FF_PAD - FF_BLK))) for n in GU_NAMES}
    out["w_ckv"] = wts["w_ckv"].astype(BF16)
    out["p"] = jnp.concatenate([wts[nm].astype(BF16) for nm, _ in P_ROWS], axis=0)
    return out


def _unpack_grads(sums):
    out = {n: sums[n][:, :FF_BLK] for n in GU_NAMES}
    out["w_ckv"] = sums["w_ckv"]
    for nm, rows in P_ROWS:
        out[nm] = sums["p"][P_OFF[nm]:P_OFF[nm] + rows]
    return out


def kernel(x, mem, g_ffn1, w_ffn1_gu, w_ffn1_down, g_mix, w_in, b_gate, conv_w, w_conv_out, w_attn_out, w_o, g_cross, g_mem, w_cq, w_ckv, w_co, g_ffn2, w_ffn2_gu, w_ffn2_down, g_final, loss_target, m_g_ffn1, m_w_ffn1_gu, m_w_ffn1_down, m_g_mix, m_w_in, m_b_gate, m_conv_w, m_w_conv_out, m_w_attn_out, m_w_o, m_g_cross, m_g_mem, m_w_cq, m_w_ckv, m_w_co, m_g_ffn2, m_w_ffn2_gu, m_w_ffn2_down, m_g_final, v_g_ffn1, v_w_ffn1_gu, v_w_ffn1_down, v_g_mix, v_w_in, v_b_gate, v_conv_w, v_w_conv_out, v_w_attn_out, v_w_o, v_g_cross, v_g_mem, v_w_cq, v_w_ckv, v_w_co, v_g_ffn2, v_w_ffn2_gu, v_w_ffn2_down, v_g_final):
    args = locals()
    wts = {n: args[n] for n in WEIGHT_ORDER}
    mom1 = {n: args["m_" + n] for n in WEIGHT_ORDER}
    mom2 = {n: args["v_" + n] for n in WEIGHT_ORDER}
    cx, cy, cc = _mesh_pos()
    dev = 4 * cx + 2 * cy + cc
    conv_cols = D // N_DEV

    shards = _pack_shards(wts)
    conv_pad = jnp.concatenate([conv_w, jnp.zeros((8 - CONV_K, conv_cols), F32)], axis=0)
    gathered = _all_gather([shards[n] for n in EXCHANGED] + [conv_pad], "gather_weights")
    wg = dict(zip(EXCHANGED, gathered[:-1]))
    conv_full = gathered[-1][:, :CONV_K, :].transpose(1, 0, 2).reshape(CONV_K, D)

    gains = {n: wts[n].reshape(1, D) for n in GAINS}
    loss8, dx, gw, gs = _local_step(x[0], mem[0], loss_target[0], wg, gains, b_gate.reshape(1, 2 * D), conv_full)
    loss = lax.psum(loss8[0, 0], MESH_AXES)

    grads_out = [gw[n] for n in EXCHANGED]
    landed = _pair_exchange(grads_out, "grads_to_sibling")
    core = cc.reshape(1).astype(jnp.int32)
    chip_sums = [_pair_sum(g, l, core, "grads_pair_sum_" + n) for g, l, n in zip(grads_out, landed, EXCHANGED)]
    parts = _chip_exchange(chip_sums, "grads_to_chips")
    sums = {n: _sum_slots(p, "grads_sum_" + n) for n, p in zip(EXCHANGED, parts)}
    grads = _unpack_grads(sums)

    gs_rows = {n: gs[n] for n in GAINS + ("b_gate",)}
    small_all = _all_gather([_pack_small(gs_rows, gs["conv_w"][:CONV_K])], "gather_small_grads")[0]
    grad_small = _unpack_small(_sum_slots(small_all, "small_grads_sum"))
    grad_small["conv_w"] = lax.dynamic_slice_in_dim(grad_small["conv_w"], dev * conv_cols, conv_cols, axis=1)
    grads.update(grad_small)

    delta, new_m, new_v = {}, {}, {}
    for n in WEIGHT_ORDER:
        if n not in SMALL:
            delta[n], new_m[n], new_v[n] = _adamw(wts[n], grads[n], mom1[n], mom2[n], "adamw_" + n)

    def small_buf(vals):
        return _pack_small(vals, jnp.concatenate([vals["conv_w"], jnp.zeros((CONV_K, D - conv_cols), F32)], axis=1))

    d_s, m_s, v_s = _adamw(small_buf(wts), small_buf(grads), small_buf(mom1), small_buf(mom2), "adamw_small")
    for res, buf in ((delta, d_s), (new_m, m_s), (new_v, v_s)):
        un = _unpack_small(buf)
        for n in GAINS + ("b_gate",):
            res[n] = un[n]
        res["conv_w"] = un["conv_w"][:, :conv_cols]

    return (loss, dx[None], *[grads[n] for n in WEIGHT_ORDER], *[delta[n] for n in WEIGHT_ORDER],
            *[new_m[n] for n in WEIGHT_ORDER], *[new_v[n] for n in WEIGHT_ORDER])
```

```python
import types

import jax
import jax.numpy as jnp
from jax import lax
from jax.experimental import pallas as pl
from jax.experimental.pallas import tpu as pltpu

F32 = jnp.float32
BF16 = jnp.bfloat16

D = 1024
DFF = 2816
SB_H = 8
SB_DH = 128
X_H = 4
X_DH = 256
CONV_K = 3
RMS_EPS = 1e-6
N_DEV = 8
N_CHIP = 4
SQ_ROWS = D // N_DEV

ADAM_LR = 0.001
ADAM_B1 = 0.9
ADAM_B2 = 0.999
ADAM_EPS = 1e-08
ADAM_WD = 0.01
ADAM_STEP = 10

TM = 256
TQ = 256
SB_HPS = 2
VMEM_LIMIT = 56 << 20

FF_BLK = DFF // 4
FF_PAD = 768
FF_SUB = 256
DOWN_ROWS = DFF // N_DEV

MIX_MATS = ("w_conv_out", "w_attn_out", "w_o")
CROSS_MATS = ("w_cq", "w_co")

MESH_AXES = ("x", "y", "c")
_ANY = pl.BlockSpec(memory_space=pl.ANY)


def _cparams(n_axes=1):
    return pltpu.CompilerParams(
        dimension_semantics=("arbitrary",) * n_axes, vmem_limit_bytes=VMEM_LIMIT)


def _row_spec(tm, n):
    return pl.BlockSpec((tm, n), lambda i: (i, 0))


def _blk_row_spec(nb, tm, n):
    return pl.BlockSpec((nb, tm, n), lambda i: (0, i, 0))


def _const_spec(shape):
    zeros = (0,) * len(shape)
    return pl.BlockSpec(shape, lambda i: zeros)


def _dot(a, b):
    return jnp.dot(a, b, preferred_element_type=F32)


def _dot_nt(a, b):
    return lax.dot_general(a, b, (((1,), (1,)), ((), ())), preferred_element_type=F32)


def _dot_tn(a, b):
    return lax.dot_general(a, b, (((0,), (0,)), ((), ())), preferred_element_type=F32)


def _sigmoid(x):
    return 1.0 / (1.0 + jnp.exp(-x))


def _call(body, operands, *, grid, in_specs, out_specs, out_shape, scratch_shapes, name, comm=None):
    n_in, n_out, n_sc = len(in_specs), len(out_specs), len(scratch_shapes)
    if comm is None:
        outs = pl.pallas_call(
            body, grid=grid, name=name, in_specs=in_specs, out_specs=out_specs, out_shape=out_shape,
            scratch_shapes=scratch_shapes, compiler_params=_cparams(len(grid)))(*operands)
        return list(outs), []
    c_in, c_out, c_sem = len(comm.inputs), len(comm.out_shapes), len(comm.sem_shapes)

    def hosted(*refs):
        bounds = [0, n_in, c_in, n_out, c_out, n_sc, c_sem]
        parts, pos = [], 0
        for k in bounds[1:]:
            parts.append(refs[pos:pos + k])
            pos += k
        ins, cins, outs, couts, scr, sems = parts
        first = pl.program_id(0) == 0
        last = pl.program_id(0) == grid[0] - 1
        for ax in range(1, len(grid)):
            first = jnp.logical_and(first, pl.program_id(ax) == 0)
            last = jnp.logical_and(last, pl.program_id(ax) == grid[ax] - 1)

        @pl.when(first)
        def _():
            comm.start(cins, couts, sems)

        body(*ins, *outs, *scr)

        @pl.when(last)
        def _():
            comm.finish(cins, couts, sems)

    res = pl.pallas_call(
        hosted, grid=grid, name=name, in_specs=list(in_specs) + [_ANY] * c_in,
        out_specs=list(out_specs) + [_ANY] * c_out, out_shape=list(out_shape) + list(comm.out_shapes),
        scratch_shapes=list(scratch_shapes) + list(comm.sem_shapes),
        compiler_params=_cparams(len(grid)))(*operands, *comm.inputs)
    return list(res[:n_out]), list(res[n_out:])


def _load_resident(step, pairs, sems):
    @pl.when(step == 0)
    def _():
        copies = [pltpu.make_async_copy(src, dst, sems.at[k]) for k, (src, dst) in enumerate(pairs)]
        for cp in copies:
            cp.start()
        for cp in copies:
            cp.wait()


def _square_pairs(buf_hbm, index, dst):
    off = index * SQ_ROWS
    return [(buf_hbm.at[d, off:off + SQ_ROWS, :], dst.at[d * SQ_ROWS:(d + 1) * SQ_ROWS, :]) for d in range(N_DEV)]


def _down_pairs(wd_hbm, dst):
    return [(wd_hbm.at[d], dst.at[d // 2, (d % 2) * DOWN_ROWS:(d % 2 + 1) * DOWN_ROWS, :]) for d in range(N_DEV)]


def _zero_down_pad(step, dst):
    @pl.when(step == 0)
    def _():
        dst[:, FF_BLK:, :] = jnp.zeros((4, FF_PAD - FF_BLK, D), BF16)


def _rms_fwd_tile(xt, g):
    r = lax.rsqrt(jnp.mean(xt * xt, axis=-1, keepdims=True) + RMS_EPS)
    return (xt * r) * g


def _rms_bwd_tile(xt, g, dn):
    r = lax.rsqrt(jnp.mean(xt * xt, axis=-1, keepdims=True) + RMS_EPS)
    xhat = xt * r
    dxhat = dn * g
    dx = r * (dxhat - xhat * jnp.mean(dxhat * xhat, axis=-1, keepdims=True))
    dg = jnp.sum(dn * xhat, axis=0, keepdims=True)
    return dx, dg


def _accumulate(ref, step, value):
    @pl.when(step == 0)
    def _():
        ref[...] = value

    @pl.when(step != 0)
    def _():
        ref[...] = ref[...] + value


def _ffn_fwd(x, g, wgu, wd, name, comm=None):
    t = x.shape[0]

    def body(x_ref, g_ref, wgu_hbm, wd_hbm, n_ref, gate_ref, up_ref, act_ref, h_ref, wgu_v, wd_v, sems):
        step = pl.program_id(0)
        _zero_down_pad(step, wd_v)
        _load_resident(step, [(wgu_hbm, wgu_v)] + _down_pairs(wd_hbm, wd_v), sems)
        xt = x_ref[...]
        n = _rms_fwd_tile(xt, g_ref[...]).astype(BF16)
        n_ref[...] = n
        acc = jnp.zeros((TM, D), F32)
        for j in range(4):
            for s in range(FF_PAD // FF_SUB):
                lo, hi = s * FF_SUB, (s + 1) * FF_SUB
                gt = _dot(n, wgu_v[j, :, lo:hi])
                ut = _dot(n, wgu_v[4 + j, :, lo:hi])
                act = ((gt * _sigmoid(gt)) * ut).astype(BF16)
                gate_ref[j, :, lo:hi] = gt.astype(BF16)
                up_ref[j, :, lo:hi] = ut.astype(BF16)
                act_ref[j, :, lo:hi] = act
                acc = acc + _dot(act, wd_v[j, lo:hi, :])
        h_ref[...] = xt + 0.5 * acc

    ff = jax.ShapeDtypeStruct((4, t, FF_PAD), BF16)
    return _call(
        body, (x, g, wgu, wd), grid=(t // TM,), name=name, comm=comm,
        in_specs=[_row_spec(TM, D), _const_spec((1, D)), _ANY, _ANY],
        out_specs=[_row_spec(TM, D)] + [_blk_row_spec(4, TM, FF_PAD)] * 3 + [_row_spec(TM, D)],
        out_shape=[jax.ShapeDtypeStruct((t, D), BF16), ff, ff, ff, jax.ShapeDtypeStruct((t, D), F32)],
        scratch_shapes=[pltpu.VMEM((N_DEV, D, FF_PAD), BF16), pltpu.VMEM((4, FF_PAD, D), BF16),
                        pltpu.SemaphoreType.DMA((1 + N_DEV,))])


def _ffn_bwd(dh, xin, g, gate, up, wgu, wd, name, comm=None):
    t = dh.shape[0]

    def body(dh_ref, x_ref, g_ref, gate_ref, up_ref, wgu_hbm, wd_hbm,
             dgu_ref, dhb_ref, dx_ref, dg_ref, wgu_v, wd_v, sems):
        step = pl.program_id(0)
        _zero_down_pad(step, wd_v)
        _load_resident(step, [(wgu_hbm, wgu_v)] + _down_pairs(wd_hbm, wd_v), sems)
        dht = dh_ref[...]
        dhb = (0.5 * dht).astype(BF16)
        dhb_ref[...] = dhb
        dn = jnp.zeros((TM, D), F32)
        for j in range(4):
            for s in range(FF_PAD // FF_SUB):
                lo, hi = s * FF_SUB, (s + 1) * FF_SUB
                da = _dot_nt(dhb, wd_v[j, lo:hi, :])
                gt = gate_ref[j, :, lo:hi].astype(F32)
                ut = up_ref[j, :, lo:hi].astype(F32)
                sg = _sigmoid(gt)
                dgt = (da * ut * (sg * (1.0 + gt * (1.0 - sg)))).astype(BF16)
                dut = (da * (gt * sg)).astype(BF16)
                dgu_ref[j, :, lo:hi] = dgt
                dgu_ref[4 + j, :, lo:hi] = dut
                dn = dn + _dot_nt(dgt, wgu_v[j, :, lo:hi]) + _dot_nt(dut, wgu_v[4 + j, :, lo:hi])
        dx, dg = _rms_bwd_tile(x_ref[...], g_ref[...], dn)
        dx_ref[...] = dht + dx
        _accumulate(dg_ref, step, dg)

    return _call(
        body, (dh, xin, g, gate, up, wgu, wd), grid=(t // TM,), name=name, comm=comm,
        in_specs=[_row_spec(TM, D), _row_spec(TM, D), _const_spec((1, D)), _blk_row_spec(4, TM, FF_PAD),
                  _blk_row_spec(4, TM, FF_PAD), _ANY, _ANY],
        out_specs=[_blk_row_spec(N_DEV, TM, FF_PAD), _row_spec(TM, D), _row_spec(TM, D), _const_spec((1, D))],
        out_shape=[jax.ShapeDtypeStruct((N_DEV, t, FF_PAD), BF16), jax.ShapeDtypeStruct((t, D), BF16),
                   jax.ShapeDtypeStruct((t, D), F32), jax.ShapeDtypeStruct((1, D), F32)],
        scratch_shapes=[pltpu.VMEM((N_DEV, D, FF_PAD), BF16), pltpu.VMEM((4, FF_PAD, D), BF16),
                        pltpu.SemaphoreType.DMA((1 + N_DEV,))])


def _pick_tile(n, options=(512, 256, 128)):
    for o in options:
        if n % o == 0:
            return o
    return n


def _mm_tn(a, b, name):
    k, m = a.shape
    _, n = b.shape
    tm, tn = _pick_tile(m), _pick_tile(n)

    def body(a_ref, b_ref, o_ref):
        o_ref[...] = _dot_tn(a_ref[...].astype(BF16), b_ref[...].astype(BF16)).astype(BF16)

    return pl.pallas_call(
        body, grid=(m // tm, n // tn), name=name,
        in_specs=[pl.BlockSpec((k, tm), lambda i, j: (0, i)), pl.BlockSpec((k, tn), lambda i, j: (0, j))],
        out_specs=pl.BlockSpec((tm, tn), lambda i, j: (i, j)),
        out_shape=jax.ShapeDtypeStruct((m, n), BF16),
        compiler_params=_cparams(2),
    )(a, b)


def _mm_tn_cols(a, b, name):
    k, m = a.shape
    nb, _, n = b.shape
    tm = _pick_tile(m)

    def body(a_ref, b_ref, o_ref):
        o_ref[0] = _dot_tn(a_ref[...].astype(BF16), b_ref[0].astype(BF16)).astype(BF16)

    return pl.pallas_call(
        body, grid=(nb, m // tm), name=name,
        in_specs=[pl.BlockSpec((k, tm), lambda j, i: (0, i)), pl.BlockSpec((1, k, n), lambda j, i: (j, 0, 0))],
        out_specs=pl.BlockSpec((1, tm, n), lambda j, i: (j, i, 0)),
        out_shape=jax.ShapeDtypeStruct((nb, m, n), BF16),
        compiler_params=_cparams(2),
    )(a, b)


def _mm_tn_rows(a, b, keep, name):
    nb, k, m = a.shape
    _, n = b.shape
    tn = _pick_tile(n)

    def body(a_ref, b_ref, o_ref):
        o_ref[0] = _dot_tn(a_ref[0], b_ref[...])[:keep].astype(BF16)

    return pl.pallas_call(
        body, grid=(nb, n // tn), name=name,
        in_specs=[pl.BlockSpec((1, k, m), lambda j, i: (j, 0, 0)), pl.BlockSpec((k, tn), lambda j, i: (0, i))],
        out_specs=pl.BlockSpec((1, keep, tn), lambda j, i: (j, 0, i)),
        out_shape=jax.ShapeDtypeStruct((nb, keep, n), BF16),
        compiler_params=_cparams(2),
    )(a, b)


PCG_W = 5 * D
QKV_W = 3 * D
PROJ_SUB = 512


def _inproj_fwd(h, g, w_in, name, comm=None):
    t = h.shape[0]

    def body(h_ref, g_ref, w_hbm, u_ref, pcg_ref, qkv_ref, w_v, sems):
        _load_resident(pl.program_id(0), [(w_hbm, w_v)], sems)
        u = _rms_fwd_tile(h_ref[...], g_ref[...]).astype(BF16)
        u_ref[...] = u
        for blk in range(N_DEV):
            for s in range(D // PROJ_SUB):
                lo, hi = s * PROJ_SUB, (s + 1) * PROJ_SUB
                p = _dot(u, w_v[blk, :, lo:hi])
                if blk < 3:
                    pcg_ref[:, blk * D + lo:blk * D + hi] = p
                elif blk < 6:
                    qkv_ref[:, (blk - 3) * D + lo:(blk - 3) * D + hi] = p.astype(BF16)
                else:
                    pcg_ref[:, (blk - 3) * D + lo:(blk - 3) * D + hi] = p

    return _call(
        body, (h, g, w_in), grid=(t // TM,), name=name, comm=comm,
        in_specs=[_row_spec(TM, D), _const_spec((1, D)), _ANY],
        out_specs=[_row_spec(TM, D), _row_spec(TM, PCG_W), _row_spec(TM, QKV_W)],
        out_shape=[jax.ShapeDtypeStruct((t, D), BF16), jax.ShapeDtypeStruct((t, PCG_W), F32),
                   jax.ShapeDtypeStruct((t, QKV_W), BF16)],
        scratch_shapes=[pltpu.VMEM((N_DEV, D, D), BF16), pltpu.SemaphoreType.DMA((1,))])


CONV_CW = 256


def _shift_down(v, k, rows):
    return jnp.where(rows >= k, pltpu.roll(v, k, 0), 0.0)


def _shift_up(v, k, rows, t):
    return jnp.where(rows < t - k, pltpu.roll(v, t - k, 0), 0.0)


def _col_spec(t, cw, off):
    return pl.BlockSpec((t, cw), lambda j: (0, j + off))


def _conv_fwd(pcg, conv_w, name):
    t = pcg.shape[0]
    nb = D // CONV_CW

    def body(cb_ref, cc_ref, cx_ref, w_ref, y_ref):
        rows = lax.broadcasted_iota(jnp.int32, (t, CONV_CW), 0)
        xc = cc_ref[...] * cx_ref[...]
        conv = (w_ref[0:1, :] * _shift_down(xc, 2, rows) + w_ref[1:2, :] * _shift_down(xc, 1, rows)
                + w_ref[2:3, :] * xc)
        y_ref[...] = (cb_ref[...] * conv).astype(BF16)

    return pl.pallas_call(
        body, grid=(nb,), name=name,
        in_specs=[_col_spec(t, CONV_CW, 0), _col_spec(t, CONV_CW, nb), _col_spec(t, CONV_CW, 2 * nb),
                  pl.BlockSpec((CONV_K, CONV_CW), lambda j: (0, j))],
        out_specs=_col_spec(t, CONV_CW, 0),
        out_shape=jax.ShapeDtypeStruct((t, D), BF16),
        compiler_params=_cparams(),
    )(pcg, pcg, pcg, conv_w)


def _conv_bwd(pcg, conv_w, dyc, name):
    t = pcg.shape[0]
    nb = D // CONV_CW

    def body(cb_ref, cc_ref, cx_ref, w_ref, dy_ref, dcb_ref, dcc_ref, dcx_ref, dw_ref):
        rows = lax.broadcasted_iota(jnp.int32, (t, CONV_CW), 0)
        cc, cx = cc_ref[...], cx_ref[...]
        xc = cc * cx
        x1 = _shift_down(xc, 1, rows)
        x2 = _shift_down(xc, 2, rows)
        w0, w1, w2 = w_ref[0:1, :], w_ref[1:2, :], w_ref[2:3, :]
        conv = w0 * x2 + w1 * x1 + w2 * xc
        dy = dy_ref[...]
        dcb_ref[...] = (dy * conv).astype(BF16)
        dconv = dy * cb_ref[...]
        dw_ref[...] = jnp.zeros((8, CONV_CW), F32)
        dw_ref[0:1, :] = jnp.sum(dconv * x2, axis=0, keepdims=True)
        dw_ref[1:2, :] = jnp.sum(dconv * x1, axis=0, keepdims=True)
        dw_ref[2:3, :] = jnp.sum(dconv * xc, axis=0, keepdims=True)
        dxc = w2 * dconv + w1 * _shift_up(dconv, 1, rows, t) + w0 * _shift_up(dconv, 2, rows, t)
        dcc_ref[...] = (dxc * cx).astype(BF16)
        dcx_ref[...] = (dxc * cc).astype(BF16)

    return pl.pallas_call(
        body, grid=(nb,), name=name,
        in_specs=[_col_spec(t, CONV_CW, 0), _col_spec(t, CONV_CW, nb), _col_spec(t, CONV_CW, 2 * nb),
                  pl.BlockSpec((CONV_K, CONV_CW), lambda j: (0, j)), _col_spec(t, CONV_CW, 0)],
        out_specs=[_col_spec(t, CONV_CW, 0), _col_spec(t, CONV_CW, 0), _col_spec(t, CONV_CW, 0),
                   pl.BlockSpec((8, CONV_CW), lambda j: (0, j))],
        out_shape=[jax.ShapeDtypeStruct((t, D), BF16)] * 3 + [jax.ShapeDtypeStruct((8, D), F32)],
        compiler_params=_cparams(),
    )(pcg, pcg, pcg, conv_w, dyc)


def _tri2(cond):
    rr = lax.broadcasted_iota(jnp.int32, (2 * TQ, TQ), 0) & (TQ - 1)
    cc = lax.broadcasted_iota(jnp.int32, (2 * TQ, TQ), 1)
    return cond(rr, cc).astype(BF16)


def _cumdot(v, tri2):
    hi = v.astype(BF16)
    lo = (v - hi.astype(F32)).astype(BF16)
    return _dot(jnp.concatenate([hi, lo], axis=1), tri2)


def _log_1m_beta(z):
    return -(jnp.maximum(z, 0.0) + jnp.log(1.0 + jnp.exp(-jnp.abs(z))))


def _sb_specs(t):
    g = SB_H // SB_HPS
    w = SB_HPS * SB_DH
    q_spec = pl.BlockSpec((TQ, w), lambda h, i: (i, h))
    k_spec = pl.BlockSpec((t, w), lambda h, i: (0, g + h))
    v_spec = pl.BlockSpec((t, w), lambda h, i: (0, 2 * g + h))
    ct_spec = pl.BlockSpec((SB_HPS, TQ, 1), lambda h, i: (h, i, 0))
    return g, w, q_spec, k_spec, v_spec, ct_spec


def _sb_fwd(qkv, name, comm=None):
    t = qkv.shape[0]
    scale = SB_DH ** -0.5
    g, w, q_spec, k_spec, v_spec, ct_spec = _sb_specs(t)

    def body(q_ref, k_ref, v_ref, y_ref, ct_ref):
        i = pl.program_id(1)
        later = _tri2(lambda j, s: j > s)
        rr = lax.broadcasted_iota(jnp.int32, (TQ, TQ), 0)
        cc = lax.broadcasted_iota(jnp.int32, (TQ, TQ), 1)
        causal = cc < rr

        def block(j, carry, masked):
            off = pl.multiple_of(j * TQ, TQ)
            out = []
            for hd in range(SB_HPS):
                acc, c_sum = carry[hd]
                cols = slice(hd * SB_DH, (hd + 1) * SB_DH)
                kj = k_ref[pl.ds(off, TQ), cols]
                vj = v_ref[pl.ds(off, TQ), cols]
                z = _dot_nt(q_ref[:, cols], kj) * scale
                m = _log_1m_beta(z)
                if masked:
                    m = jnp.where(causal, m, 0.0)
                a = jnp.exp((m + z) + (c_sum + _cumdot(m, later)))
                if masked:
                    a = jnp.where(causal, a, 0.0)
                out.append((acc + _dot(a.astype(BF16), vj), c_sum + jnp.sum(m, axis=1, keepdims=True)))
            return tuple(out)

        init = tuple((jnp.zeros((TQ, SB_DH), F32), jnp.zeros((TQ, 1), F32)) for _ in range(SB_HPS))
        carry = block(i, init, True)
        carry = lax.fori_loop(0, i, lambda jj, c: block(i - 1 - jj, c, False), carry)
        for hd in range(SB_HPS):
            y_ref[:, hd * SB_DH:(hd + 1) * SB_DH] = carry[hd][0].astype(BF16)
            ct_ref[hd] = carry[hd][1]

    return _call(
        body, (qkv, qkv, qkv), grid=(g, t // TQ), name=name, comm=comm,
        in_specs=[q_spec, k_spec, v_spec],
        out_specs=[q_spec, ct_spec],
        out_shape=[jax.ShapeDtypeStruct((t, D), BF16), jax.ShapeDtypeStruct((SB_H, t, 1), F32)],
        scratch_shapes=[])


def _sb_bwd(qkv, dy, ctot, name, comm=None):
    t = qkv.shape[0]
    scale = SB_DH ** -0.5
    g, w, q_spec, k_spec, v_spec, ct_spec = _sb_specs(t)
    acc_spec = pl.BlockSpec((t, w), lambda h, i: (0, h))

    def body(q_ref, k_ref, v_ref, dy_ref, ct_ref, dq_ref, dk_ref, dv_ref):
        i = pl.program_id(1)

        @pl.when(i == 0)
        def _():
            dk_ref[...] = jnp.zeros_like(dk_ref)
            dv_ref[...] = jnp.zeros_like(dv_ref)

        upto = _tri2(lambda j, s: j <= s)
        before = _tri2(lambda j, s: j < s)
        rr = lax.broadcasted_iota(jnp.int32, (TQ, TQ), 0)
        cc = lax.broadcasted_iota(jnp.int32, (TQ, TQ), 1)
        causal = cc < rr

        def block(j, carry, masked):
            off = pl.multiple_of(j * TQ, TQ)
            out = []
            for hd in range(SB_HPS):
                dq, p_sum, e_sum = carry[hd]
                cols = slice(hd * SB_DH, (hd + 1) * SB_DH)
                q = q_ref[:, cols]
                dy_t = dy_ref[:, cols]
                kj = k_ref[pl.ds(off, TQ), cols]
                vj = v_ref[pl.ds(off, TQ), cols]
                z = _dot_nt(q, kj) * scale
                m = _log_1m_beta(z)
                if masked:
                    m = jnp.where(causal, m, 0.0)
                l = m + z
                a = jnp.exp(l + ((ct_ref[hd] - p_sum) - _cumdot(m, upto)))
                if masked:
                    a = jnp.where(causal, a, 0.0)
                e = _dot_nt(dy_t, vj) * a
                e_before = e_sum + _cumdot(e, before)
                beta = jnp.exp(l)
                dz = e * (1.0 - beta) - e_before * beta
                if masked:
                    dz = jnp.where(causal, dz, 0.0)
                dzs = (dz * scale).astype(BF16)
                dk_ref[pl.ds(off, TQ), cols] += _dot_tn(dzs, q)
                dv_ref[pl.ds(off, TQ), cols] += _dot_tn(a.astype(BF16), dy_t)
                out.append((dq + _dot(dzs, kj), p_sum + jnp.sum(m, axis=1, keepdims=True),
                            e_sum + jnp.sum(e, axis=1, keepdims=True)))
            return tuple(out)

        zero = jnp.zeros((TQ, 1), F32)
        init = tuple((jnp.zeros((TQ, SB_DH), F32), zero, zero) for _ in range(SB_HPS))
        carry = lax.fori_loop(0, i, lambda j, c: block(j, c, False), init)
        carry = block(i, carry, True)
        for hd in range(SB_HPS):
            dq_ref[:, hd * SB_DH:(hd + 1) * SB_DH] = carry[hd][0].astype(BF16)

    return _call(
        body, (qkv, qkv, qkv, dy, ctot), grid=(g, t // TQ), name=name, comm=comm,
        in_specs=[q_spec, k_spec, v_spec, q_spec, ct_spec],
        out_specs=[q_spec, acc_spec, acc_spec],
        out_shape=[jax.ShapeDtypeStruct((t, D), BF16), jax.ShapeDtypeStruct((t, D), F32),
                   jax.ShapeDtypeStruct((t, D), F32)],
        scratch_shapes=[])


def _gate_specs():
    return [pl.BlockSpec((TM, D), lambda i: (i, 3)), pl.BlockSpec((TM, D), lambda i: (i, 4))]


def _mix_pairs(mix_hbm, dsts):
    pairs = []
    for index, dst in enumerate(dsts):
        pairs += _square_pairs(mix_hbm, index, dst)
    return pairs


def _mix_out_fwd(yc, ysb, pcg, b_gate, h, w_mix, name):
    t = h.shape[0]

    def body(yc_ref, ysb_ref, gc_ref, gs_ref, b_ref, h_ref, mix_hbm,
             a_ref, b_out_ref, mg_ref, h2_ref, wc_v, wa_v, wo_v, sems):
        _load_resident(pl.program_id(0), _mix_pairs(mix_hbm, (wc_v, wa_v, wo_v)), sems)
        a = _dot(yc_ref[...], wc_v[...])
        b = _dot(ysb_ref[...], wa_v[...])
        merged = (_sigmoid(gc_ref[...] + b_ref[:, :D]) * a + _sigmoid(gs_ref[...] + b_ref[:, D:]) * b).astype(BF16)
        a_ref[...] = a
        b_out_ref[...] = b
        mg_ref[...] = merged
        h2_ref[...] = h_ref[...] + _dot(merged, wo_v[...])

    return pl.pallas_call(
        body, grid=(t // TM,), name=name,
        in_specs=[_row_spec(TM, D), _row_spec(TM, D)] + _gate_specs()
                 + [_const_spec((1, 2 * D)), _row_spec(TM, D), _ANY],
        out_specs=[_row_spec(TM, D)] * 4,
        out_shape=[jax.ShapeDtypeStruct((t, D), F32), jax.ShapeDtypeStruct((t, D), F32),
                   jax.ShapeDtypeStruct((t, D), BF16), jax.ShapeDtypeStruct((t, D), F32)],
        scratch_shapes=[pltpu.VMEM((D, D), BF16)] * 3 + [pltpu.SemaphoreType.DMA((3 * N_DEV,))],
        compiler_params=_cparams(),
    )(yc, ysb, pcg, pcg, b_gate, h, w_mix)


def _mix_out_bwd(dh2, a, b, pcg, b_gate, w_mix, name, comm=None):
    t = dh2.shape[0]

    def body(dh_ref, a_ref, b_ref, gc_ref, gs_ref, bias_ref, mix_hbm,
             dhb_ref, da_ref, db_ref, dgp_ref, dyc_ref, dysb_ref, dbias_ref, wc_v, wa_v, wo_v, sems):
        step = pl.program_id(0)
        _load_resident(step, _mix_pairs(mix_hbm, (wc_v, wa_v, wo_v)), sems)
        dhb = dh_ref[...].astype(BF16)
        dhb_ref[...] = dhb
        dm = _dot_nt(dhb, wo_v[...])
        gc = _sigmoid(gc_ref[...] + bias_ref[:, :D])
        gs = _sigmoid(gs_ref[...] + bias_ref[:, D:])
        da = (dm * gc).astype(BF16)
        db = (dm * gs).astype(BF16)
        da_ref[...] = da
        db_ref[...] = db
        dgc = dm * a_ref[...] * (gc * (1.0 - gc))
        dgs = dm * b_ref[...] * (gs * (1.0 - gs))
        dgp_ref[0] = dgc.astype(BF16)
        dgp_ref[1] = dgs.astype(BF16)
        _accumulate(dbias_ref.at[:, :D], step, jnp.sum(dgc, axis=0, keepdims=True))
        _accumulate(dbias_ref.at[:, D:], step, jnp.sum(dgs, axis=0, keepdims=True))
        dyc_ref[...] = _dot_nt(da, wc_v[...])
        dysb_ref[...] = _dot_nt(db, wa_v[...]).astype(BF16)

    return _call(
        body, (dh2, a, b, pcg, pcg, b_gate, w_mix), grid=(t // TM,), name=name, comm=comm,
        in_specs=[_row_spec(TM, D)] * 3 + _gate_specs() + [_const_spec((1, 2 * D)), _ANY],
        out_specs=[_row_spec(TM, D)] * 3 + [_blk_row_spec(2, TM, D), _row_spec(TM, D), _row_spec(TM, D),
                                            _const_spec((1, 2 * D))],
        out_shape=[jax.ShapeDtypeStruct((t, D), BF16)] * 3
                  + [jax.ShapeDtypeStruct((2, t, D), BF16), jax.ShapeDtypeStruct((t, D), F32),
                     jax.ShapeDtypeStruct((t, D), BF16), jax.ShapeDtypeStruct((1, 2 * D), F32)],
        scratch_shapes=[pltpu.VMEM((D, D), BF16)] * 3 + [pltpu.SemaphoreType.DMA((3 * N_DEV,))])


def _inproj_bwd(pieces, dgp, w_in, h, g, dh_res, name, comm=None):
    t = h.shape[0]
    n_piece = len(pieces)

    def body(*refs):
        piece_refs = refs[:n_piece]
        dgp_ref, w_hbm, h_ref, g_ref, dres_ref, dh_ref, dg_ref, w_v, sems = refs[n_piece:]
        step = pl.program_id(0)
        _load_resident(step, [(w_hbm, w_v)], sems)
        du = jnp.zeros((TM, D), F32)
        for p, ref in enumerate(piece_refs):
            du = du + _dot_nt(ref[...].astype(BF16), w_v[p])
        for k in range(2):
            du = du + _dot_nt(dgp_ref[k], w_v[n_piece + k])
        dx, dg = _rms_bwd_tile(h_ref[...], g_ref[...], du)
        dh_ref[...] = dres_ref[...] + dx
        _accumulate(dg_ref, step, dg)

    return _call(
        body, (*pieces, dgp, w_in, h, g, dh_res), grid=(t // TM,), name=name, comm=comm,
        in_specs=[_row_spec(TM, D)] * n_piece + [_blk_row_spec(2, TM, D), _ANY, _row_spec(TM, D),
                                                  _const_spec((1, D)), _row_spec(TM, D)],
        out_specs=[_row_spec(TM, D), _const_spec((1, D))],
        out_shape=[jax.ShapeDtypeStruct((t, D), F32), jax.ShapeDtypeStruct((1, D), F32)],
        scratch_shapes=[pltpu.VMEM((N_DEV, D, D), BF16), pltpu.SemaphoreType.DMA((1,))])


def _memkv_fwd(mem, g, w_ckv, name):
    m = mem.shape[0]

    def body(mem_ref, g_ref, w_ref, mn_ref, kv_ref):
        mn = _rms_fwd_tile(mem_ref[...], g_ref[...]).astype(BF16)
        mn_ref[...] = mn
        for j in range(N_DEV):
            kv_ref[j] = _dot(mn, w_ref[j]).astype(BF16)

    return pl.pallas_call(
        body, grid=(1,), name=name,
        in_specs=[_const_spec((m, D)), _const_spec((1, D)), _const_spec((N_DEV, D, X_DH))],
        out_specs=[_const_spec((m, D)), _const_spec((N_DEV, m, X_DH))],
        out_shape=[jax.ShapeDtypeStruct((m, D), BF16), jax.ShapeDtypeStruct((N_DEV, m, X_DH), BF16)],
        compiler_params=_cparams(),
    )(mem, g, w_ckv)


def _memkv_bwd(dkv, mem, g, w_ckv, name):
    m = mem.shape[0]

    def body(dkv_ref, mem_ref, g_ref, w_ref, dg_ref):
        dmn = jnp.zeros((m, D), F32)
        for j in range(N_DEV):
            dmn = dmn + _dot_nt(dkv_ref[j].astype(BF16), w_ref[j])
        _, dg = _rms_bwd_tile(mem_ref[...], g_ref[...], dmn)
        dg_ref[...] = dg

    return pl.pallas_call(
        body, grid=(1,), name=name,
        in_specs=[_const_spec((N_DEV, m, X_DH)), _const_spec((m, D)), _const_spec((1, D)),
                  _const_spec((N_DEV, D, X_DH))],
        out_specs=_const_spec((1, D)),
        out_shape=jax.ShapeDtypeStruct((1, D), F32),
        compiler_params=_cparams(),
    )(dkv, mem, g, w_ckv)


def _softmax_rows(s):
    e = jnp.exp(s - jnp.max(s, axis=-1, keepdims=True))
    return e / jnp.sum(e, axis=-1, keepdims=True)


def _cross_pairs(cross_hbm, wq_v, wo_v):
    return _square_pairs(cross_hbm, 0, wq_v) + _square_pairs(cross_hbm, 1, wo_v)


def _cross_fwd(h, g, kv, w_cross, name):
    t = h.shape[0]
    m = kv.shape[1]
    scale = X_DH ** -0.5

    def body(h_ref, g_ref, kv_ref, cross_hbm, hn_ref, qx_ref, o_ref, h3_ref, wq_v, wo_v, sems):
        _load_resident(pl.program_id(0), _cross_pairs(cross_hbm, wq_v, wo_v), sems)
        ht = h_ref[...]
        hn = _rms_fwd_tile(ht, g_ref[...]).astype(BF16)
        hn_ref[...] = hn
        qx = _dot(hn, wq_v[...]).astype(BF16)
        qx_ref[...] = qx
        for hd in range(X_H):
            lo, hi = hd * X_DH, (hd + 1) * X_DH
            p = _softmax_rows(_dot_nt(qx[:, lo:hi], kv_ref[hd]) * scale)
            o_ref[:, lo:hi] = _dot(p.astype(BF16), kv_ref[X_H + hd]).astype(BF16)
        h3_ref[...] = ht + _dot(o_ref[...], wo_v[...])

    return pl.pallas_call(
        body, grid=(t // TM,), name=name,
        in_specs=[_row_spec(TM, D), _const_spec((1, D)), _const_spec((N_DEV, m, X_DH)), _ANY],
        out_specs=[_row_spec(TM, D)] * 4,
        out_shape=[jax.ShapeDtypeStruct((t, D), BF16)] * 3 + [jax.ShapeDtypeStruct((t, D), F32)],
        scratch_shapes=[pltpu.VMEM((D, D), BF16)] * 2 + [pltpu.SemaphoreType.DMA((2 * N_DEV,))],
        compiler_params=_cparams(),
    )(h, g, kv, w_cross)


def _cross_bwd(dh3, h, g, qx, kv, w_cross, name, comm=None):
    t = h.shape[0]
    m = kv.shape[1]
    scale = X_DH ** -0.5

    def body(dh_ref, h_ref, g_ref, qx_ref, kv_ref, cross_hbm,
             dhb_ref, dqx_ref, dkv_ref, dh2_ref, dg_ref, wq_v, wo_v, sems):
        step = pl.program_id(0)
        _load_resident(step, _cross_pairs(cross_hbm, wq_v, wo_v), sems)

        @pl.when(step == 0)
        def _():
            dkv_ref[...] = jnp.zeros_like(dkv_ref)

        dht = dh_ref[...]
        dhb = dht.astype(BF16)
        dhb_ref[...] = dhb
        do = _dot_nt(dhb, wo_v[...]).astype(BF16)
        for hd in range(X_H):
            lo, hi = hd * X_DH, (hd + 1) * X_DH
            qh = qx_ref[:, lo:hi]
            kh = kv_ref[hd]
            p = _softmax_rows(_dot_nt(qh, kh) * scale)
            doh = do[:, lo:hi]
            dp = _dot_nt(doh, kv_ref[X_H + hd])
            ds = (p * (dp - jnp.sum(dp * p, axis=-1, keepdims=True)) * scale).astype(BF16)
            dqx_ref[:, lo:hi] = _dot(ds, kh).astype(BF16)
            dkv_ref[hd] += _dot_tn(ds, qh)
            dkv_ref[X_H + hd] += _dot_tn(p.astype(BF16), doh)
        dhn = _dot_nt(dqx_ref[...], wq_v[...])
        dx, dg = _rms_bwd_tile(h_ref[...], g_ref[...], dhn)
        dh2_ref[...] = dht + dx
        _accumulate(dg_ref, step, dg)

    return _call(
        body, (dh3, h, g, qx, kv, w_cross), grid=(t // TM,), name=name, comm=comm,
        in_specs=[_row_spec(TM, D), _row_spec(TM, D), _const_spec((1, D)), _row_spec(TM, D),
                  _const_spec((N_DEV, m, X_DH)), _ANY],
        out_specs=[_row_spec(TM, D), _row_spec(TM, D), _const_spec((N_DEV, m, X_DH)), _row_spec(TM, D),
                   _const_spec((1, D))],
        out_shape=[jax.ShapeDtypeStruct((t, D), BF16), jax.ShapeDtypeStruct((t, D), BF16),
                   jax.ShapeDtypeStruct((N_DEV, m, X_DH), F32), jax.ShapeDtypeStruct((t, D), F32),
                   jax.ShapeDtypeStruct((1, D), F32)],
        scratch_shapes=[pltpu.VMEM((D, D), BF16)] * 2 + [pltpu.SemaphoreType.DMA((2 * N_DEV,))])


def _loss_bwd(h, g, target, name):
    t = h.shape[0]

    def body(h_ref, g_ref, t_ref, loss_ref, dh_ref, dg_ref):
        step = pl.program_id(0)
        ht = h_ref[...]
        gain = g_ref[...]
        diff = _rms_fwd_tile(ht, gain) - t_ref[...]
        part = 0.5 * jnp.sum(jnp.sum(diff * diff, axis=-1, keepdims=True) / D, axis=0, keepdims=True)
        dx, dg = _rms_bwd_tile(ht, gain, diff / D)
        dh_ref[...] = dx
        _accumulate(loss_ref, step, jnp.broadcast_to(part, (8, 128)))
        _accumulate(dg_ref, step, dg)

    return pl.pallas_call(
        body, grid=(t // TM,), name=name,
        in_specs=[_row_spec(TM, D), _const_spec((1, D)), _row_spec(TM, D)],
        out_specs=[_const_spec((8, 128)), _row_spec(TM, D), _const_spec((1, D))],
        out_shape=[jax.ShapeDtypeStruct((8, 128), F32), jax.ShapeDtypeStruct((t, D), F32),
                   jax.ShapeDtypeStruct((1, D), F32)],
        compiler_params=_cparams(),
    )(h, g, target)


def _adamw(w, gr, m, v, name):
    r, c = w.shape
    tr = _pick_tile(r, (256, 128)) if r > 512 else r

    def body(w_ref, g_ref, m_ref, v_ref, d_ref, nm_ref, nv_ref):
        gt = g_ref[...]
        nm = ADAM_B1 * m_ref[...] + (1.0 - ADAM_B1) * gt
        nv = ADAM_B2 * v_ref[...] + (1.0 - ADAM_B2) * jnp.square(gt)
        m_hat = nm / (1.0 - ADAM_B1 ** ADAM_STEP)
        v_hat = nv / (1.0 - ADAM_B2 ** ADAM_STEP)
        d_ref[...] = -ADAM_LR * (m_hat / (jnp.sqrt(v_hat) + ADAM_EPS) + ADAM_WD * w_ref[...])
        nm_ref[...] = nm
        nv_ref[...] = nv

    spec = _row_spec(tr, c)
    return pl.pallas_call(
        body, grid=(r // tr,), name=name,
        in_specs=[spec] * 4, out_specs=[spec] * 3,
        out_shape=[jax.ShapeDtypeStruct((r, c), F32)] * 3,
        compiler_params=_cparams(),
    )(w, gr, m, v)


def _mesh_pos():
    return lax.axis_index("x"), lax.axis_index("y"), lax.axis_index("c")


def _run_exchange(comm, name):
    c_in, c_out = len(comm.inputs), len(comm.out_shapes)

    def body(*refs):
        cins, couts, sems = refs[:c_in], refs[c_in:c_in + c_out], refs[c_in + c_out:]
        comm.start(cins, couts, sems)
        comm.finish(cins, couts, sems)

    return list(pl.pallas_call(
        body, name=name, out_shape=list(comm.out_shapes),
        in_specs=[_ANY] * c_in, out_specs=[_ANY] * c_out, scratch_shapes=list(comm.sem_shapes),
    )(*comm.inputs))


def _gather_exchange(shards):
    n_arr = len(shards)

    def plan(x_refs, out_refs, sems):
        send_sems, recv_sems, local_sems = sems
        x, y, c = _mesh_pos()
        me, sibling = (x, y, c), (x, y, 1 - c)
        chips = [(1 - x, y), (x, 1 - y), (1 - x, 1 - y)]

        def slot(a, px, py, pc):
            return out_refs[a].at[4 * px + 2 * py + pc]

        def copy(a, k, block, to, src=None):
            return pltpu.make_async_remote_copy(
                src_ref=slot(a, *block) if src is None else src, dst_ref=slot(a, *block),
                send_sem=send_sems.at[a, k], recv_sem=recv_sems.at[a, k],
                device_id=to, device_id_type=pl.DeviceIdType.MESH)

        mine = [pltpu.make_async_copy(x_refs[a], slot(a, *me), local_sems.at[a]) for a in range(n_arr)]
        first = []
        for a in range(n_arr):
            first.append(copy(a, 0, me, sibling, src=x_refs[a]))
            first += [copy(a, 1 + j, me, (*chip, c), src=x_refs[a]) for j, chip in enumerate(chips)]
        return me, sibling, chips, c, copy, mine, first

    def start(x_refs, out_refs, sems):
        _, _, _, _, _, mine, first = plan(x_refs, out_refs, sems)
        for cp in mine + first:
            cp.start()

    def finish(x_refs, out_refs, sems):
        me, sibling, chips, c, copy, mine, first = plan(x_refs, out_refs, sems)
        passed = []
        for a in range(n_arr):
            for j, chip in enumerate(chips):
                copy(a, 1 + j, (*chip, c), me).wait_recv()
                fwd = copy(a, 4 + j, (*chip, c), sibling)
                fwd.start()
                passed.append(fwd)
        for a in range(n_arr):
            copy(a, 0, sibling, me).wait_recv()
            for j, chip in enumerate(chips):
                copy(a, 4 + j, (*chip, 1 - c), me).wait_recv()
        for cp in first + passed:
            cp.wait_send()
        for cp in mine:
            cp.wait()

    return types.SimpleNamespace(
        inputs=list(shards), start=start, finish=finish,
        out_shapes=[jax.ShapeDtypeStruct((N_DEV,) + s.shape, s.dtype) for s in shards],
        sem_shapes=[pltpu.SemaphoreType.DMA((n_arr, 7)), pltpu.SemaphoreType.DMA((n_arr, 7)),
                    pltpu.SemaphoreType.DMA((n_arr,))])


def _pair_exchange(grads):
    n_arr = len(grads)

    def plan(g_refs, land_refs, sems):
        send_sems, recv_sems = sems
        x, y, c = _mesh_pos()
        return [pltpu.make_async_remote_copy(
            src_ref=g_refs[a].at[2 * k + 1 - c], dst_ref=land_refs[a].at[k],
            send_sem=send_sems.at[a, k], recv_sem=recv_sems.at[a, k],
            device_id=(x, y, 1 - c), device_id_type=pl.DeviceIdType.MESH)
            for a in range(n_arr) for k in range(N_CHIP)]

    def start(g_refs, land_refs, sems):
        for cp in plan(g_refs, land_refs, sems):
            cp.start()

    def finish(g_refs, land_refs, sems):
        for cp in plan(g_refs, land_refs, sems):
            cp.wait()

    return types.SimpleNamespace(
        inputs=list(grads), start=start, finish=finish,
        out_shapes=[jax.ShapeDtypeStruct((N_CHIP,) + g.shape[1:], g.dtype) for g in grads],
        sem_shapes=[pltpu.SemaphoreType.DMA((n_arr, N_CHIP)), pltpu.SemaphoreType.DMA((n_arr, N_CHIP))])


def _chip_exchange(parts):
    n_arr = len(parts)

    def plan(p_refs, land_refs, sems):
        send_sems, recv_sems, local_sems = sems
        x, y, c = _mesh_pos()
        my_chip = 2 * x + y
        chips = [(1 - x, y), (x, 1 - y), (1 - x, 1 - y)]
        local = [pltpu.make_async_copy(p_refs[a].at[my_chip], land_refs[a].at[my_chip], local_sems.at[a])
                 for a in range(n_arr)]

        def copy(a, k, src_slot, dst_slot, px, py):
            return pltpu.make_async_remote_copy(
                src_ref=p_refs[a].at[src_slot], dst_ref=land_refs[a].at[dst_slot],
                send_sem=send_sems.at[a, k], recv_sem=recv_sems.at[a, k],
                device_id=(px, py, c), device_id_type=pl.DeviceIdType.MESH)

        sends = [copy(a, k, 2 * px + py, my_chip, px, py) for a in range(n_arr) for k, (px, py) in enumerate(chips)]
        arrivals = [copy(a, k, my_chip, 2 * px + py, px, py) for a in range(n_arr)
                    for k, (px, py) in enumerate(chips)]
        return local, sends, arrivals

    def start(p_refs, land_refs, sems):
        local, sends, _ = plan(p_refs, land_refs, sems)
        for cp in local + sends:
            cp.start()

    def finish(p_refs, land_refs, sems):
        local, sends, arrivals = plan(p_refs, land_refs, sems)
        for cp in arrivals:
            cp.wait_recv()
        for cp in sends:
            cp.wait_send()
        for cp in local:
            cp.wait()

    return types.SimpleNamespace(
        inputs=list(parts), start=start, finish=finish,
        out_shapes=[jax.ShapeDtypeStruct(p.shape, p.dtype) for p in parts],
        sem_shapes=[pltpu.SemaphoreType.DMA((n_arr, 3)), pltpu.SemaphoreType.DMA((n_arr, 3)),
                    pltpu.SemaphoreType.DMA((n_arr,))])


def _row_tile(r, cap=640):
    best = None
    for cand in range(16, min(r, cap) + 1, 16):
        if r % cand == 0:
            best = cand
    return best if best is not None else r


def _pair_sum(g, landed, core, name):
    _, r, c_dim = g.shape
    tr = _row_tile(r)

    def body(core_ref, mine_ref, theirs_ref, o_ref):
        o_ref[0] = (mine_ref[0].astype(F32) + theirs_ref[0].astype(F32)).astype(o_ref.dtype)

    return pl.pallas_call(
        body, name=name,
        grid_spec=pltpu.PrefetchScalarGridSpec(
            num_scalar_prefetch=1, grid=(N_CHIP, r // tr),
            in_specs=[pl.BlockSpec((1, tr, c_dim), lambda k, i, core_ref: (2 * k + core_ref[0], i, 0)),
                      pl.BlockSpec((1, tr, c_dim), lambda k, i, core_ref: (k, i, 0))],
            out_specs=pl.BlockSpec((1, tr, c_dim), lambda k, i, core_ref: (k, i, 0))),
        out_shape=jax.ShapeDtypeStruct((N_CHIP, r, c_dim), g.dtype),
        compiler_params=_cparams(2),
    )(core, g, landed)


def _sum_slots(parts, name):
    n, r, c_dim = parts.shape
    tr = _row_tile(r)

    def body(p_ref, o_ref):
        acc = p_ref[0].astype(F32)
        for k in range(1, n):
            acc = acc + p_ref[k].astype(F32)
        o_ref[...] = acc

    return pl.pallas_call(
        body, grid=(r // tr,), name=name,
        in_specs=[pl.BlockSpec((n, tr, c_dim), lambda i: (0, i, 0))],
        out_specs=_row_spec(tr, c_dim),
        out_shape=jax.ShapeDtypeStruct((r, c_dim), F32),
        compiler_params=_cparams(),
    )(parts)


GAINS = ("g_ffn1", "g_mix", "g_cross", "g_mem", "g_ffn2", "g_final")
SMALL = GAINS + ("b_gate", "conv_w")
SMALL_R = 16
WEIGHT_ORDER = ("g_ffn1", "w_ffn1_gu", "w_ffn1_down", "g_mix", "w_in", "b_gate", "conv_w", "w_conv_out",
                "w_attn_out", "w_o", "g_cross", "g_mem", "w_cq", "w_ckv", "w_co", "g_ffn2", "w_ffn2_gu",
                "w_ffn2_down", "g_final")
GU_NAMES = ("w_ffn1_gu", "w_ffn2_gu")


def _pack_small(vals, conv_rows):
    rows = [vals[n].reshape(1, D) for n in GAINS] + [vals["b_gate"].reshape(2, D), conv_rows.reshape(CONV_K, D)]
    used = len(GAINS) + 2 + CONV_K
    return jnp.concatenate(rows + [jnp.zeros((SMALL_R - used, D), F32)], axis=0)


def _unpack_small(buf):
    out = {n: buf[k] for k, n in enumerate(GAINS)}
    out["b_gate"] = buf[6:8].reshape(2 * D)
    out["conv_w"] = buf[8:8 + CONV_K]
    return out


def _exchange_shards(wts):
    out = {n: jnp.pad(wts[n].astype(BF16), ((0, 0), (0, FF_PAD - FF_BLK))) for n in GU_NAMES}
    for n in ("w_ckv", "w_in", "w_ffn1_down", "w_ffn2_down"):
        out[n] = wts[n].astype(BF16)
    out["mix"] = jnp.concatenate([wts[n].astype(BF16) for n in MIX_MATS], axis=0)
    out["cross"] = jnp.concatenate([wts[n].astype(BF16) for n in CROSS_MATS], axis=0)
    return out


def _by_device(dw):
    return dw.reshape(N_DEV, SQ_ROWS, D)


def _reduce_group(grads, landed, core, names):
    return [_pair_sum(g, l, core, "grads_pair_sum_" + n) for g, l, n in zip(grads, landed, names)]


def _step(x, mem, target, sh, conv_pad, gains, b_gate, core):
    wg1, wd1, conv_all = _run_exchange(_gather_exchange([sh["w_ffn1_gu"], sh["w_ffn1_down"], conv_pad]), "gather_ffn1")
    conv_w = conv_all[:, :CONV_K, :].transpose(1, 0, 2).reshape(CONV_K, D)
    (n1, gate1, up1, act1, h1), (w_in,) = _ffn_fwd(
        x, gains["g_ffn1"], wg1, wd1, "ffn1_fwd", comm=_gather_exchange([sh["w_in"]]))
    (u, pcg, qkv), (w_mix,) = _inproj_fwd(h1, gains["g_mix"], w_in, "inproj_fwd", comm=_gather_exchange([sh["mix"]]))
    yc = _conv_fwd(pcg, conv_w, "conv_fwd")
    (ysb, ctot), (w_cross, w_ckv, wg2, wd2) = _sb_fwd(
        qkv, "sb_fwd", comm=_gather_exchange([sh["cross"], sh["w_ckv"], sh["w_ffn2_gu"], sh["w_ffn2_down"]]))
    a_mix, b_mix, merged, h2 = _mix_out_fwd(yc, ysb, pcg, b_gate, h1, w_mix, "mix_out_fwd")
    mn, kv = _memkv_fwd(mem, gains["g_mem"], w_ckv, "memkv_fwd")
    hn, qx, o_x, h3 = _cross_fwd(h2, gains["g_cross"], kv, w_cross, "cross_fwd")
    (n4, gate2, up2, act2, h4), _ = _ffn_fwd(h3, gains["g_ffn2"], wg2, wd2, "ffn2_fwd")
    loss, dh4, dg_final = _loss_bwd(h4, gains["g_final"], target, "loss_bwd")

    gs = {"g_final": dg_final}
    (dgu2, dh4b, dh3, gs["g_ffn2"]), _ = _ffn_bwd(dh4, h3, gains["g_ffn2"], gate2, up2, wg2, wd2, "ffn2_bwd")
    grads_a = [_mm_tn_cols(n4, dgu2, "dw_ffn2_gu"),
               _mm_tn_rows(act2, dh4b, FF_BLK, "dw_ffn2_down").reshape(N_DEV, DOWN_ROWS, D)]
    names_a = ["w_ffn2_gu", "w_ffn2_down"]
    (dh3b, dqx, dkv, dh2, gs["g_cross"]), landed_a = _cross_bwd(
        dh3, h2, gains["g_cross"], qx, kv, w_cross, "cross_bwd", comm=_pair_exchange(grads_a))
    sums_a = _reduce_group(grads_a, landed_a, core, names_a)
    grads_b = [_mm_tn_cols(mn, dkv, "dw_ckv"),
               jnp.concatenate([_by_device(_mm_tn(hn, dqx, "dw_cq")), _by_device(_mm_tn(o_x, dh3b, "dw_co"))], axis=1)]
    names_b = ["w_ckv", "cross"]
    gs["g_mem"] = _memkv_bwd(dkv, mem, gains["g_mem"], w_ckv, "memkv_bwd")
    (dh2b, da_mix, db_mix, dgp, dyc, dysb, gs["b_gate"]), landed_b = _mix_out_bwd(
        dh2, a_mix, b_mix, pcg, b_gate, w_mix, "mix_out_bwd", comm=_pair_exchange(grads_b))
    sums_b = _reduce_group(grads_b, landed_b, core, names_b)
    grads_c = [jnp.concatenate([_by_device(_mm_tn(yc, da_mix, "dw_conv_out")),
                                _by_device(_mm_tn(ysb, db_mix, "dw_attn_out")),
                                _by_device(_mm_tn(merged, dh2b, "dw_o"))], axis=1)]
    landed_c = _run_exchange(_pair_exchange(grads_c), "grads_to_sibling_mix")
    sums_c = _reduce_group(grads_c, landed_c, core, ["mix"])
    (dq, dk, dv), parts_abc = _sb_bwd(qkv, dysb, ctot, "sb_bwd", comm=_chip_exchange(sums_a + sums_b + sums_c))
    dcb, dcc, dcx, gs["conv_w"] = _conv_bwd(pcg, conv_w, dyc, "conv_bwd")
    pieces = [dcb, dcc, dcx, dq, dk, dv]
    grads_d = [jnp.concatenate(
        [_mm_tn(u, p, "dw_in_%d" % k)[None] for k, p in enumerate(pieces)] + [_mm_tn_cols(u, dgp, "dw_in_gates")],
        axis=0)]
    (dh1, gs["g_mix"]), landed_d = _inproj_bwd(pieces, dgp, w_in, h1, gains["g_mix"], dh2, "inproj_bwd",
                                               comm=_pair_exchange(grads_d))
    sums_d = _reduce_group(grads_d, landed_d, core, ["w_in"])
    (dgu1, dh1b, dx, gs["g_ffn1"]), parts_d = _ffn_bwd(dh1, x, gains["g_ffn1"], gate1, up1, wg1, wd1, "ffn1_bwd",
                                                       comm=_chip_exchange(sums_d))
    grads_e = [_mm_tn_cols(n1, dgu1, "dw_ffn1_gu"),
               _mm_tn_rows(act1, dh1b, FF_BLK, "dw_ffn1_down").reshape(N_DEV, DOWN_ROWS, D)]
    names_e = ["w_ffn1_gu", "w_ffn1_down"]
    landed_e = _run_exchange(_pair_exchange(grads_e), "grads_to_sibling_ffn1")
    parts_e = _run_exchange(_chip_exchange(_reduce_group(grads_e, landed_e, core, names_e)), "grads_to_chips_ffn1")

    names = names_a + names_b + ["mix"] + ["w_in"] + names_e
    sums = {n: _sum_slots(p, "grads_sum_" + n) for n, p in zip(names, parts_abc + parts_d + parts_e)}
    return loss, dx, sums, gs


def _unpack_grads(sums):
    out = {n: sums[n][:, :FF_BLK] for n in GU_NAMES}
    for n in ("w_ckv", "w_in", "w_ffn1_down", "w_ffn2_down"):
        out[n] = sums[n]
    for k, n in enumerate(MIX_MATS):
        out[n] = sums["mix"][k * SQ_ROWS:(k + 1) * SQ_ROWS]
    for k, n in enumerate(CROSS_MATS):
        out[n] = sums["cross"][k * SQ_ROWS:(k + 1) * SQ_ROWS]
    return out


def kernel(x, mem, g_ffn1, w_ffn1_gu, w_ffn1_down, g_mix, w_in, b_gate, conv_w, w_conv_out, w_attn_out, w_o, g_cross, g_mem, w_cq, w_ckv, w_co, g_ffn2, w_ffn2_gu, w_ffn2_down, g_final, loss_target, m_g_ffn1, m_w_ffn1_gu, m_w_ffn1_down, m_g_mix, m_w_in, m_b_gate, m_conv_w, m_w_conv_out, m_w_attn_out, m_w_o, m_g_cross, m_g_mem, m_w_cq, m_w_ckv, m_w_co, m_g_ffn2, m_w_ffn2_gu, m_w_ffn2_down, m_g_final, v_g_ffn1, v_w_ffn1_gu, v_w_ffn1_down, v_g_mix, v_w_in, v_b_gate, v_conv_w, v_w_conv_out, v_w_attn_out, v_w_o, v_g_cross, v_g_mem, v_w_cq, v_w_ckv, v_w_co, v_g_ffn2, v_w_ffn2_gu, v_w_ffn2_down, v_g_final):
    args = locals()
    wts = {n: args[n] for n in WEIGHT_ORDER}
    mom1 = {n: args["m_" + n] for n in WEIGHT_ORDER}
    mom2 = {n: args["v_" + n] for n in WEIGHT_ORDER}
    cx, cy, cc = _mesh_pos()
    dev = 4 * cx + 2 * cy + cc
    conv_cols = D // N_DEV

    conv_pad = jnp.concatenate([conv_w, jnp.zeros((8 - CONV_K, conv_cols), F32)], axis=0)
    gains = {n: wts[n].reshape(1, D) for n in GAINS}
    loss8, dx, sums, gs = _step(x[0], mem[0], loss_target[0], _exchange_shards(wts), conv_pad, gains,
                                b_gate.reshape(1, 2 * D), cc.reshape(1).astype(jnp.int32))
    loss = lax.psum(loss8[0, 0], MESH_AXES)
    grads = _unpack_grads(sums)

    gs_rows = {n: gs[n] for n in GAINS + ("b_gate",)}
    small_all = _run_exchange(_gather_exchange([_pack_small(gs_rows, gs["conv_w"][:CONV_K])]), "gather_small_grads")[0]
    grad_small = _unpack_small(_sum_slots(small_all, "small_grads_sum"))
    grad_small["conv_w"] = lax.dynamic_slice_in_dim(grad_small["conv_w"], dev * conv_cols, conv_cols, axis=1)
    grads.update(grad_small)

    delta, new_m, new_v = {}, {}, {}
    for n in WEIGHT_ORDER:
        if n not in SMALL:
            delta[n], new_m[n], new_v[n] = _adamw(wts[n], grads[n], mom1[n], mom2[n], "adamw_" + n)

    def small_buf(vals):
        return _pack_small(vals, jnp.concatenate([vals["conv_w"], jnp.zeros((CONV_K, D - conv_cols), F32)], axis=1))

    d_s, m_s, v_s = _adamw(small_buf(wts), small_buf(grads), small_buf(mom1), small_buf(mom2), "adamw_small")
    for res, buf in ((delta, d_s), (new_m, m_s), (new_v, v_s)):
        un = _unpack_small(buf)
        for n in GAINS + ("b_gate",):
            res[n] = un[n]
        res["conv_w"] = un["conv_w"][:, :conv_cols]

    return (loss, dx[None], *[grads[n] for n in WEIGHT_ORDER], *[delta[n] for n in WEIGHT_ORDER],
            *[new_m[n] for n in WEIGHT_ORDER], *[new_v[n] for n in WEIGHT_ORDER])
```

```python
import types

import jax
import jax.numpy as jnp
from jax import lax
from jax.experimental import pallas as pl
from jax.experimental.pallas import tpu as pltpu

F32 = jnp.float32
BF16 = jnp.bfloat16

D = 1024
DFF = 2816
SB_H = 8
SB_DH = 128
X_H = 4
X_DH = 256
CONV_K = 3
RMS_EPS = 1e-6
N_DEV = 8
N_CHIP = 4
SQ_ROWS = D // N_DEV

ADAM_LR = 0.001
ADAM_B1 = 0.9
ADAM_B2 = 0.999
ADAM_EPS = 1e-08
ADAM_WD = 0.01
ADAM_STEP = 10

TM = 256
TQ = 512
TK = 256
SB_HPS = 2
VMEM_LIMIT = 56 << 20

FF_BLK = DFF // 4
FF_PAD = 768
FF_SUB = 256
DOWN_ROWS = DFF // N_DEV

MIX_MATS = ("w_conv_out", "w_attn_out", "w_o")
CROSS_MATS = ("w_cq", "w_co")

MESH_AXES = ("x", "y", "c")
_ANY = pl.BlockSpec(memory_space=pl.ANY)


def _cparams(n_axes=1):
    return pltpu.CompilerParams(
        dimension_semantics=("arbitrary",) * n_axes, vmem_limit_bytes=VMEM_LIMIT)


def _row_spec(tm, n):
    return pl.BlockSpec((tm, n), lambda i: (i, 0))


def _blk_row_spec(nb, tm, n):
    return pl.BlockSpec((nb, tm, n), lambda i: (0, i, 0))


def _const_spec(shape):
    zeros = (0,) * len(shape)
    return pl.BlockSpec(shape, lambda i: zeros)


def _dot(a, b):
    return jnp.dot(a, b, preferred_element_type=F32)


def _dot_nt(a, b):
    return lax.dot_general(a, b, (((1,), (1,)), ((), ())), preferred_element_type=F32)


def _dot_tn(a, b):
    return lax.dot_general(a, b, (((0,), (0,)), ((), ())), preferred_element_type=F32)


def _sigmoid(x):
    return 1.0 / (1.0 + jnp.exp(-x))


def _call(body, operands, *, grid, in_specs, out_specs, out_shape, scratch_shapes, name, comm=None):
    n_in, n_out, n_sc = len(in_specs), len(out_specs), len(scratch_shapes)
    if comm is None:
        outs = pl.pallas_call(
            body, grid=grid, name=name, in_specs=in_specs, out_specs=out_specs, out_shape=out_shape,
            scratch_shapes=scratch_shapes, compiler_params=_cparams(len(grid)))(*operands)
        return list(outs), []
    c_in, c_out, c_sem = len(comm.inputs), len(comm.out_shapes), len(comm.sem_shapes)

    def hosted(*refs):
        bounds = [0, n_in, c_in, n_out, c_out, n_sc, c_sem]
        parts, pos = [], 0
        for k in bounds[1:]:
            parts.append(refs[pos:pos + k])
            pos += k
        ins, cins, outs, couts, scr, sems = parts
        first = pl.program_id(0) == 0
        last = pl.program_id(0) == grid[0] - 1
        for ax in range(1, len(grid)):
            first = jnp.logical_and(first, pl.program_id(ax) == 0)
            last = jnp.logical_and(last, pl.program_id(ax) == grid[ax] - 1)

        @pl.when(first)
        def _():
            comm.start(cins, couts, sems)

        body(*ins, *outs, *scr)

        @pl.when(last)
        def _():
            comm.finish(cins, couts, sems)

    res = pl.pallas_call(
        hosted, grid=grid, name=name, in_specs=list(in_specs) + [_ANY] * c_in,
        out_specs=list(out_specs) + [_ANY] * c_out, out_shape=list(out_shape) + list(comm.out_shapes),
        scratch_shapes=list(scratch_shapes) + list(comm.sem_shapes),
        compiler_params=_cparams(len(grid)))(*operands, *comm.inputs)
    return list(res[:n_out]), list(res[n_out:])


def _load_resident(step, pairs, sems):
    @pl.when(step == 0)
    def _():
        copies = [pltpu.make_async_copy(src, dst, sems.at[k]) for k, (src, dst) in enumerate(pairs)]
        for cp in copies:
            cp.start()
        for cp in copies:
            cp.wait()


def _square_pairs(buf_hbm, index, dst):
    off = index * SQ_ROWS
    return [(buf_hbm.at[d, off:off + SQ_ROWS, :], dst.at[d * SQ_ROWS:(d + 1) * SQ_ROWS, :]) for d in range(N_DEV)]


def _down_pairs(wd_hbm, dst):
    return [(wd_hbm.at[d], dst.at[d // 2, (d % 2) * DOWN_ROWS:(d % 2 + 1) * DOWN_ROWS, :]) for d in range(N_DEV)]


def _zero_down_pad(step, dst):
    @pl.when(step == 0)
    def _():
        dst[:, FF_BLK:, :] = jnp.zeros((4, FF_PAD - FF_BLK, D), BF16)


def _rms_fwd_tile(xt, g):
    r = lax.rsqrt(jnp.mean(xt * xt, axis=-1, keepdims=True) + RMS_EPS)
    return (xt * r) * g


def _rms_bwd_tile(xt, g, dn):
    r = lax.rsqrt(jnp.mean(xt * xt, axis=-1, keepdims=True) + RMS_EPS)
    xhat = xt * r
    dxhat = dn * g
    dx = r * (dxhat - xhat * jnp.mean(dxhat * xhat, axis=-1, keepdims=True))
    dg = jnp.sum(dn * xhat, axis=0, keepdims=True)
    return dx, dg


def _accumulate(ref, step, value):
    @pl.when(step == 0)
    def _():
        ref[...] = value

    @pl.when(step != 0)
    def _():
        ref[...] = ref[...] + value


def _ffn_fwd(x, g, wgu, wd, name, comm=None):
    t = x.shape[0]

    def body(x_ref, g_ref, wgu_hbm, wd_hbm, n_ref, gate_ref, up_ref, act_ref, h_ref, wgu_v, wd_v, sems):
        step = pl.program_id(0)
        _zero_down_pad(step, wd_v)
        _load_resident(step, [(wgu_hbm, wgu_v)] + _down_pairs(wd_hbm, wd_v), sems)
        xt = x_ref[...]
        n = _rms_fwd_tile(xt, g_ref[...]).astype(BF16)
        n_ref[...] = n
        acc = jnp.zeros((TM, D), F32)
        for j in range(4):
            for s in range(FF_PAD // FF_SUB):
                lo, hi = s * FF_SUB, (s + 1) * FF_SUB
                gt = _dot(n, wgu_v[j, :, lo:hi])
                ut = _dot(n, wgu_v[4 + j, :, lo:hi])
                act = ((gt * _sigmoid(gt)) * ut).astype(BF16)
                gate_ref[j, :, lo:hi] = gt.astype(BF16)
                up_ref[j, :, lo:hi] = ut.astype(BF16)
                act_ref[j, :, lo:hi] = act
                acc = acc + _dot(act, wd_v[j, lo:hi, :])
        h_ref[...] = xt + 0.5 * acc

    ff = jax.ShapeDtypeStruct((4, t, FF_PAD), BF16)
    return _call(
        body, (x, g, wgu, wd), grid=(t // TM,), name=name, comm=comm,
        in_specs=[_row_spec(TM, D), _const_spec((1, D)), _ANY, _ANY],
        out_specs=[_row_spec(TM, D)] + [_blk_row_spec(4, TM, FF_PAD)] * 3 + [_row_spec(TM, D)],
        out_shape=[jax.ShapeDtypeStruct((t, D), BF16), ff, ff, ff, jax.ShapeDtypeStruct((t, D), F32)],
        scratch_shapes=[pltpu.VMEM((N_DEV, D, FF_PAD), BF16), pltpu.VMEM((4, FF_PAD, D), BF16),
                        pltpu.SemaphoreType.DMA((1 + N_DEV,))])


def _ffn_bwd(dh, xin, g, gate, up, wgu, wd, name, comm=None):
    t = dh.shape[0]

    def body(dh_ref, x_ref, g_ref, gate_ref, up_ref, wgu_hbm, wd_hbm,
             dgu_ref, dhb_ref, dx_ref, dg_ref, wgu_v, wd_v, sems):
        step = pl.program_id(0)
        _zero_down_pad(step, wd_v)
        _load_resident(step, [(wgu_hbm, wgu_v)] + _down_pairs(wd_hbm, wd_v), sems)
        dht = dh_ref[...]
        dhb = (0.5 * dht).astype(BF16)
        dhb_ref[...] = dhb
        dn = jnp.zeros((TM, D), F32)
        for j in range(4):
            for s in range(FF_PAD // FF_SUB):
                lo, hi = s * FF_SUB, (s + 1) * FF_SUB
                da = _dot_nt(dhb, wd_v[j, lo:hi, :])
                gt = gate_ref[j, :, lo:hi].astype(F32)
                ut = up_ref[j, :, lo:hi].astype(F32)
                sg = _sigmoid(gt)
                dgt = (da * ut * (sg * (1.0 + gt * (1.0 - sg)))).astype(BF16)
                dut = (da * (gt * sg)).astype(BF16)
                dgu_ref[j, :, lo:hi] = dgt
                dgu_ref[4 + j, :, lo:hi] = dut
                dn = dn + _dot_nt(dgt, wgu_v[j, :, lo:hi]) + _dot_nt(dut, wgu_v[4 + j, :, lo:hi])
        dx, dg = _rms_bwd_tile(x_ref[...], g_ref[...], dn)
        dx_ref[...] = dht + dx
        _accumulate(dg_ref, step, dg)

    return _call(
        body, (dh, xin, g, gate, up, wgu, wd), grid=(t // TM,), name=name, comm=comm,
        in_specs=[_row_spec(TM, D), _row_spec(TM, D), _const_spec((1, D)), _blk_row_spec(4, TM, FF_PAD),
                  _blk_row_spec(4, TM, FF_PAD), _ANY, _ANY],
        out_specs=[_blk_row_spec(N_DEV, TM, FF_PAD), _row_spec(TM, D), _row_spec(TM, D), _const_spec((1, D))],
        out_shape=[jax.ShapeDtypeStruct((N_DEV, t, FF_PAD), BF16), jax.ShapeDtypeStruct((t, D), BF16),
                   jax.ShapeDtypeStruct((t, D), F32), jax.ShapeDtypeStruct((1, D), F32)],
        scratch_shapes=[pltpu.VMEM((N_DEV, D, FF_PAD), BF16), pltpu.VMEM((4, FF_PAD, D), BF16),
                        pltpu.SemaphoreType.DMA((1 + N_DEV,))])


def _pick_tile(n, options=(512, 256, 128)):
    for o in options:
        if n % o == 0:
            return o
    return n


def _mm_tn(a, b, name):
    k, m = a.shape
    _, n = b.shape
    tm, tn = _pick_tile(m), _pick_tile(n)

    def body(a_ref, b_ref, o_ref):
        o_ref[...] = _dot_tn(a_ref[...].astype(BF16), b_ref[...].astype(BF16)).astype(BF16)

    return pl.pallas_call(
        body, grid=(m // tm, n // tn), name=name,
        in_specs=[pl.BlockSpec((k, tm), lambda i, j: (0, i)), pl.BlockSpec((k, tn), lambda i, j: (0, j))],
        out_specs=pl.BlockSpec((tm, tn), lambda i, j: (i, j)),
        out_shape=jax.ShapeDtypeStruct((m, n), BF16),
        compiler_params=_cparams(2),
    )(a, b)


def _mm_tn_cols(a, b, name):
    k, m = a.shape
    nb, _, n = b.shape
    tm = _pick_tile(m)

    def body(a_ref, b_ref, o_ref):
        o_ref[0] = _dot_tn(a_ref[...].astype(BF16), b_ref[0].astype(BF16)).astype(BF16)

    return pl.pallas_call(
        body, grid=(nb, m // tm), name=name,
        in_specs=[pl.BlockSpec((k, tm), lambda j, i: (0, i)), pl.BlockSpec((1, k, n), lambda j, i: (j, 0, 0))],
        out_specs=pl.BlockSpec((1, tm, n), lambda j, i: (j, i, 0)),
        out_shape=jax.ShapeDtypeStruct((nb, m, n), BF16),
        compiler_params=_cparams(2),
    )(a, b)


def _mm_tn_rows(a, b, keep, name):
    nb, k, m = a.shape
    _, n = b.shape
    tn = _pick_tile(n)

    def body(a_ref, b_ref, o_ref):
        o_ref[0] = _dot_tn(a_ref[0], b_ref[...])[:keep].astype(BF16)

    return pl.pallas_call(
        body, grid=(nb, n // tn), name=name,
        in_specs=[pl.BlockSpec((1, k, m), lambda j, i: (j, 0, 0)), pl.BlockSpec((k, tn), lambda j, i: (0, i))],
        out_specs=pl.BlockSpec((1, keep, tn), lambda j, i: (j, 0, i)),
        out_shape=jax.ShapeDtypeStruct((nb, keep, n), BF16),
        compiler_params=_cparams(2),
    )(a, b)


PCG_W = 5 * D
QKV_W = 3 * D
PROJ_SUB = 512


def _inproj_fwd(h, g, w_in, name, comm=None):
    t = h.shape[0]

    def body(h_ref, g_ref, w_hbm, u_ref, pcg_ref, qkv_ref, w_v, sems):
        _load_resident(pl.program_id(0), [(w_hbm, w_v)], sems)
        u = _rms_fwd_tile(h_ref[...], g_ref[...]).astype(BF16)
        u_ref[...] = u
        for blk in range(N_DEV):
            for s in range(D // PROJ_SUB):
                lo, hi = s * PROJ_SUB, (s + 1) * PROJ_SUB
                p = _dot(u, w_v[blk, :, lo:hi])
                if blk < 3:
                    pcg_ref[:, blk * D + lo:blk * D + hi] = p
                elif blk < 6:
                    qkv_ref[:, (blk - 3) * D + lo:(blk - 3) * D + hi] = p.astype(BF16)
                else:
                    pcg_ref[:, (blk - 3) * D + lo:(blk - 3) * D + hi] = p

    return _call(
        body, (h, g, w_in), grid=(t // TM,), name=name, comm=comm,
        in_specs=[_row_spec(TM, D), _const_spec((1, D)), _ANY],
        out_specs=[_row_spec(TM, D), _row_spec(TM, PCG_W), _row_spec(TM, QKV_W)],
        out_shape=[jax.ShapeDtypeStruct((t, D), BF16), jax.ShapeDtypeStruct((t, PCG_W), F32),
                   jax.ShapeDtypeStruct((t, QKV_W), BF16)],
        scratch_shapes=[pltpu.VMEM((N_DEV, D, D), BF16), pltpu.SemaphoreType.DMA((1,))])


CONV_CW = 256


def _shift_down(v, k, rows):
    return jnp.where(rows >= k, pltpu.roll(v, k, 0), 0.0)


def _shift_up(v, k, rows, t):
    return jnp.where(rows < t - k, pltpu.roll(v, t - k, 0), 0.0)


def _col_spec(t, cw, off):
    return pl.BlockSpec((t, cw), lambda j: (0, j + off))


def _conv_fwd(pcg, conv_w, name):
    t = pcg.shape[0]
    nb = D // CONV_CW

    def body(cb_ref, cc_ref, cx_ref, w_ref, y_ref):
        rows = lax.broadcasted_iota(jnp.int32, (t, CONV_CW), 0)
        xc = cc_ref[...] * cx_ref[...]
        conv = (w_ref[0:1, :] * _shift_down(xc, 2, rows) + w_ref[1:2, :] * _shift_down(xc, 1, rows)
                + w_ref[2:3, :] * xc)
        y_ref[...] = (cb_ref[...] * conv).astype(BF16)

    return pl.pallas_call(
        body, grid=(nb,), name=name,
        in_specs=[_col_spec(t, CONV_CW, 0), _col_spec(t, CONV_CW, nb), _col_spec(t, CONV_CW, 2 * nb),
                  pl.BlockSpec((CONV_K, CONV_CW), lambda j: (0, j))],
        out_specs=_col_spec(t, CONV_CW, 0),
        out_shape=jax.ShapeDtypeStruct((t, D), BF16),
        compiler_params=_cparams(),
    )(pcg, pcg, pcg, conv_w)


def _conv_bwd(pcg, conv_w, dyc, name):
    t = pcg.shape[0]
    nb = D // CONV_CW

    def body(cb_ref, cc_ref, cx_ref, w_ref, dy_ref, dcb_ref, dcc_ref, dcx_ref, dw_ref):
        rows = lax.broadcasted_iota(jnp.int32, (t, CONV_CW), 0)
        cc, cx = cc_ref[...], cx_ref[...]
        xc = cc * cx
        x1 = _shift_down(xc, 1, rows)
        x2 = _shift_down(xc, 2, rows)
        w0, w1, w2 = w_ref[0:1, :], w_ref[1:2, :], w_ref[2:3, :]
        conv = w0 * x2 + w1 * x1 + w2 * xc
        dy = dy_ref[...]
        dcb_ref[...] = (dy * conv).astype(BF16)
        dconv = dy * cb_ref[...]
        dw_ref[...] = jnp.zeros((8, CONV_CW), F32)
        dw_ref[0:1, :] = jnp.sum(dconv * x2, axis=0, keepdims=True)
        dw_ref[1:2, :] = jnp.sum(dconv * x1, axis=0, keepdims=True)
        dw_ref[2:3, :] = jnp.sum(dconv * xc, axis=0, keepdims=True)
        dxc = w2 * dconv + w1 * _shift_up(dconv, 1, rows, t) + w0 * _shift_up(dconv, 2, rows, t)
        dcc_ref[...] = (dxc * cx).astype(BF16)
        dcx_ref[...] = (dxc * cc).astype(BF16)

    return pl.pallas_call(
        body, grid=(nb,), name=name,
        in_specs=[_col_spec(t, CONV_CW, 0), _col_spec(t, CONV_CW, nb), _col_spec(t, CONV_CW, 2 * nb),
                  pl.BlockSpec((CONV_K, CONV_CW), lambda j: (0, j)), _col_spec(t, CONV_CW, 0)],
        out_specs=[_col_spec(t, CONV_CW, 0), _col_spec(t, CONV_CW, 0), _col_spec(t, CONV_CW, 0),
                   pl.BlockSpec((8, CONV_CW), lambda j: (0, j))],
        out_shape=[jax.ShapeDtypeStruct((t, D), BF16)] * 3 + [jax.ShapeDtypeStruct((8, D), F32)],
        compiler_params=_cparams(),
    )(pcg, pcg, pcg, conv_w, dyc)


def _tri2(cond):
    rr = lax.broadcasted_iota(jnp.int32, (2 * TK, TK), 0) & (TK - 1)
    cc = lax.broadcasted_iota(jnp.int32, (2 * TK, TK), 1)
    return cond(rr, cc).astype(BF16)


def _causal(shift):
    rr = lax.broadcasted_iota(jnp.int32, (TQ, TK), 0)
    cc = lax.broadcasted_iota(jnp.int32, (TQ, TK), 1)
    return cc + shift < rr


def _cumdot(v, tri2):
    hi = v.astype(BF16)
    lo = (v - hi.astype(F32)).astype(BF16)
    return _dot(jnp.concatenate([hi, lo], axis=1), tri2)


def _log_1m_beta(z):
    return -(jnp.maximum(z, 0.0) + jnp.log(1.0 + jnp.exp(-jnp.abs(z))))


def _sb_specs(t):
    g = SB_H // SB_HPS
    w = SB_HPS * SB_DH
    q_spec = pl.BlockSpec((TQ, w), lambda h, i: (i, h))
    k_spec = pl.BlockSpec((t, w), lambda h, i: (0, g + h))
    v_spec = pl.BlockSpec((t, w), lambda h, i: (0, 2 * g + h))
    ct_spec = pl.BlockSpec((SB_HPS, TQ, 1), lambda h, i: (h, i, 0))
    return g, w, q_spec, k_spec, v_spec, ct_spec


def _sb_fwd(qkv, name, comm=None):
    t = qkv.shape[0]
    scale = SB_DH ** -0.5
    g, w, q_spec, k_spec, v_spec, ct_spec = _sb_specs(t)

    def body(q_ref, k_ref, v_ref, y_ref, ct_ref):
        i = pl.program_id(1)
        later = _tri2(lambda j, s: j > s)
        n_diag = TQ // TK

        def block(j, carry, shift):
            off = pl.multiple_of(j * TK, TK)
            zs, ms = [], []
            for hd in range(SB_HPS):
                cols = slice(hd * SB_DH, (hd + 1) * SB_DH)
                z = _dot_nt(q_ref[:, cols], k_ref[pl.ds(off, TK), cols]) * scale
                m = _log_1m_beta(z)
                if shift is not None:
                    m = jnp.where(_causal(shift), m, 0.0)
                zs.append(z)
                ms.append(m)
            after = _cumdot(jnp.concatenate(ms, axis=0), later)
            out = []
            for hd in range(SB_HPS):
                acc, c_sum = carry[hd]
                cols = slice(hd * SB_DH, (hd + 1) * SB_DH)
                a = jnp.exp((ms[hd] + zs[hd]) + (c_sum + after[hd * TQ:(hd + 1) * TQ]))
                if shift is not None:
                    a = jnp.where(_causal(shift), a, 0.0)
                out.append((acc + _dot(a.astype(BF16), v_ref[pl.ds(off, TK), cols]),
                            c_sum + jnp.sum(ms[hd], axis=1, keepdims=True)))
            return tuple(out)

        carry = tuple((jnp.zeros((TQ, SB_DH), F32), jnp.zeros((TQ, 1), F32)) for _ in range(SB_HPS))
        for d in reversed(range(n_diag)):
            carry = block(i * n_diag + d, carry, d * TK)
        carry = lax.fori_loop(0, i * n_diag, lambda jj, c: block(i * n_diag - 1 - jj, c, None), carry)
        for hd in range(SB_HPS):
            y_ref[:, hd * SB_DH:(hd + 1) * SB_DH] = carry[hd][0].astype(BF16)
            ct_ref[hd] = carry[hd][1]

    return _call(
        body, (qkv, qkv, qkv), grid=(g, t // TQ), name=name, comm=comm,
        in_specs=[q_spec, k_spec, v_spec],
        out_specs=[q_spec, ct_spec],
        out_shape=[jax.ShapeDtypeStruct((t, D), BF16), jax.ShapeDtypeStruct((SB_H, t, 1), F32)],
        scratch_shapes=[])


def _sb_bwd(qkv, dy, ctot, name, comm=None):
    t = qkv.shape[0]
    scale = SB_DH ** -0.5
    g, w, q_spec, k_spec, v_spec, ct_spec = _sb_specs(t)
    acc_spec = pl.BlockSpec((t, w), lambda h, i: (0, h))

    def body(q_ref, k_ref, v_ref, dy_ref, ct_ref, dq_ref, dk_ref, dv_ref):
        i = pl.program_id(1)

        @pl.when(i == 0)
        def _():
            dk_ref[...] = jnp.zeros_like(dk_ref)
            dv_ref[...] = jnp.zeros_like(dv_ref)

        upto = _tri2(lambda j, s: j <= s)
        n_diag = TQ // TK

        def block(j, carry, shift):
            off = pl.multiple_of(j * TK, TK)
            zs, ms = [], []
            for hd in range(SB_HPS):
                cols = slice(hd * SB_DH, (hd + 1) * SB_DH)
                z = _dot_nt(q_ref[:, cols], k_ref[pl.ds(off, TK), cols]) * scale
                m = _log_1m_beta(z)
                if shift is not None:
                    m = jnp.where(_causal(shift), m, 0.0)
                zs.append(z)
                ms.append(m)
            m_upto = _cumdot(jnp.concatenate(ms, axis=0), upto)
            ls, a_s, es = [], [], []
            for hd in range(SB_HPS):
                cols = slice(hd * SB_DH, (hd + 1) * SB_DH)
                l = ms[hd] + zs[hd]
                a = jnp.exp(l + ((ct_ref[hd] - carry[hd][1]) - m_upto[hd * TQ:(hd + 1) * TQ]))
                if shift is not None:
                    a = jnp.where(_causal(shift), a, 0.0)
                ls.append(l)
                a_s.append(a)
                es.append(_dot_nt(dy_ref[:, cols], v_ref[pl.ds(off, TK), cols]) * a)
            e_upto = _cumdot(jnp.concatenate(es, axis=0), upto)
            out = []
            for hd in range(SB_HPS):
                dq, p_sum, e_sum = carry[hd]
                cols = slice(hd * SB_DH, (hd + 1) * SB_DH)
                e = es[hd]
                e_before = e_sum + (e_upto[hd * TQ:(hd + 1) * TQ] - e)
                beta = jnp.exp(ls[hd])
                dz = e * (1.0 - beta) - e_before * beta
                if shift is not None:
                    dz = jnp.where(_causal(shift), dz, 0.0)
                dzs = (dz * scale).astype(BF16)
                dk_ref[pl.ds(off, TK), cols] += _dot_tn(dzs, q_ref[:, cols])
                dv_ref[pl.ds(off, TK), cols] += _dot_tn(a_s[hd].astype(BF16), dy_ref[:, cols])
                out.append((dq + _dot(dzs, k_ref[pl.ds(off, TK), cols]),
                            p_sum + jnp.sum(ms[hd], axis=1, keepdims=True),
                            e_sum + jnp.sum(e, axis=1, keepdims=True)))
            return tuple(out)

        zero = jnp.zeros((TQ, 1), F32)
        init = tuple((jnp.zeros((TQ, SB_DH), F32), zero, zero) for _ in range(SB_HPS))
        carry = lax.fori_loop(0, i * n_diag, lambda j, c: block(j, c, None), init)
        for d in range(n_diag):
            carry = block(i * n_diag + d, carry, d * TK)
        for hd in range(SB_HPS):
            dq_ref[:, hd * SB_DH:(hd + 1) * SB_DH] = carry[hd][0].astype(BF16)

    return _call(
        body, (qkv, qkv, qkv, dy, ctot), grid=(g, t // TQ), name=name, comm=comm,
        in_specs=[q_spec, k_spec, v_spec, q_spec, ct_spec],
        out_specs=[q_spec, acc_spec, acc_spec],
        out_shape=[jax.ShapeDtypeStruct((t, D), BF16), jax.ShapeDtypeStruct((t, D), F32),
                   jax.ShapeDtypeStruct((t, D), F32)],
        scratch_shapes=[])


def _gate_specs():
    return [pl.BlockSpec((TM, D), lambda i: (i, 3)), pl.BlockSpec((TM, D), lambda i: (i, 4))]


def _mix_pairs(mix_hbm, dsts):
    pairs = []
    for index, dst in enumerate(dsts):
        pairs += _square_pairs(mix_hbm, index, dst)
    return pairs


def _mix_out_fwd(yc, ysb, pcg, b_gate, h, w_mix, name):
    t = h.shape[0]

    def body(yc_ref, ysb_ref, gc_ref, gs_ref, b_ref, h_ref, mix_hbm,
             a_ref, b_out_ref, mg_ref, h2_ref, wc_v, wa_v, wo_v, sems):
        _load_resident(pl.program_id(0), _mix_pairs(mix_hbm, (wc_v, wa_v, wo_v)), sems)
        a = _dot(yc_ref[...], wc_v[...])
        b = _dot(ysb_ref[...], wa_v[...])
        merged = (_sigmoid(gc_ref[...] + b_ref[:, :D]) * a + _sigmoid(gs_ref[...] + b_ref[:, D:]) * b).astype(BF16)
        a_ref[...] = a
        b_out_ref[...] = b
        mg_ref[...] = merged
        h2_ref[...] = h_ref[...] + _dot(merged, wo_v[...])

    return pl.pallas_call(
        body, grid=(t // TM,), name=name,
        in_specs=[_row_spec(TM, D), _row_spec(TM, D)] + _gate_specs()
                 + [_const_spec((1, 2 * D)), _row_spec(TM, D), _ANY],
        out_specs=[_row_spec(TM, D)] * 4,
        out_shape=[jax.ShapeDtypeStruct((t, D), F32), jax.ShapeDtypeStruct((t, D), F32),
                   jax.ShapeDtypeStruct((t, D), BF16), jax.ShapeDtypeStruct((t, D), F32)],
        scratch_shapes=[pltpu.VMEM((D, D), BF16)] * 3 + [pltpu.SemaphoreType.DMA((3 * N_DEV,))],
        compiler_params=_cparams(),
    )(yc, ysb, pcg, pcg, b_gate, h, w_mix)


def _mix_out_bwd(dh2, a, b, pcg, b_gate, w_mix, name, comm=None):
    t = dh2.shape[0]

    def body(dh_ref, a_ref, b_ref, gc_ref, gs_ref, bias_ref, mix_hbm,
             dhb_ref, da_ref, db_ref, dgp_ref, dyc_ref, dysb_ref, dbias_ref, wc_v, wa_v, wo_v, sems):
        step = pl.program_id(0)
        _load_resident(step, _mix_pairs(mix_hbm, (wc_v, wa_v, wo_v)), sems)
        dhb = dh_ref[...].astype(BF16)
        dhb_ref[...] = dhb
        dm = _dot_nt(dhb, wo_v[...])
        gc = _sigmoid(gc_ref[...] + bias_ref[:, :D])
        gs = _sigmoid(gs_ref[...] + bias_ref[:, D:])
        da = (dm * gc).astype(BF16)
        db = (dm * gs).astype(BF16)
        da_ref[...] = da
        db_ref[...] = db
        dgc = dm * a_ref[...] * (gc * (1.0 - gc))
        dgs = dm * b_ref[...] * (gs * (1.0 - gs))
        dgp_ref[0] = dgc.astype(BF16)
        dgp_ref[1] = dgs.astype(BF16)
        _accumulate(dbias_ref.at[:, :D], step, jnp.sum(dgc, axis=0, keepdims=True))
        _accumulate(dbias_ref.at[:, D:], step, jnp.sum(dgs, axis=0, keepdims=True))
        dyc_ref[...] = _dot_nt(da, wc_v[...])
        dysb_ref[...] = _dot_nt(db, wa_v[...]).astype(BF16)

    return _call(
        body, (dh2, a, b, pcg, pcg, b_gate, w_mix), grid=(t // TM,), name=name, comm=comm,
        in_specs=[_row_spec(TM, D)] * 3 + _gate_specs() + [_const_spec((1, 2 * D)), _ANY],
        out_specs=[_row_spec(TM, D)] * 3 + [_blk_row_spec(2, TM, D), _row_spec(TM, D), _row_spec(TM, D),
                                            _const_spec((1, 2 * D))],
        out_shape=[jax.ShapeDtypeStruct((t, D), BF16)] * 3
                  + [jax.ShapeDtypeStruct((2, t, D), BF16), jax.ShapeDtypeStruct((t, D), F32),
                     jax.ShapeDtypeStruct((t, D), BF16), jax.ShapeDtypeStruct((1, 2 * D), F32)],
        scratch_shapes=[pltpu.VMEM((D, D), BF16)] * 3 + [pltpu.SemaphoreType.DMA((3 * N_DEV,))])


def _inproj_bwd(pieces, dgp, w_in, h, g, dh_res, name, comm=None):
    t = h.shape[0]
    n_piece = len(pieces)

    def body(*refs):
        piece_refs = refs[:n_piece]
        dgp_ref, w_hbm, h_ref, g_ref, dres_ref, dh_ref, dg_ref, w_v, sems = refs[n_piece:]
        step = pl.program_id(0)
        _load_resident(step, [(w_hbm, w_v)], sems)
        du = jnp.zeros((TM, D), F32)
        for p, ref in enumerate(piece_refs):
            du = du + _dot_nt(ref[...].astype(BF16), w_v[p])
        for k in range(2):
            du = du + _dot_nt(dgp_ref[k], w_v[n_piece + k])
        dx, dg = _rms_bwd_tile(h_ref[...], g_ref[...], du)
        dh_ref[...] = dres_ref[...] + dx
        _accumulate(dg_ref, step, dg)

    return _call(
        body, (*pieces, dgp, w_in, h, g, dh_res), grid=(t // TM,), name=name, comm=comm,
        in_specs=[_row_spec(TM, D)] * n_piece + [_blk_row_spec(2, TM, D), _ANY, _row_spec(TM, D),
                                                  _const_spec((1, D)), _row_spec(TM, D)],
        out_specs=[_row_spec(TM, D), _const_spec((1, D))],
        out_shape=[jax.ShapeDtypeStruct((t, D), F32), jax.ShapeDtypeStruct((1, D), F32)],
        scratch_shapes=[pltpu.VMEM((N_DEV, D, D), BF16), pltpu.SemaphoreType.DMA((1,))])


def _memkv_fwd(mem, g, w_ckv, name):
    m = mem.shape[0]

    def body(mem_ref, g_ref, w_ref, mn_ref, kv_ref):
        mn = _rms_fwd_tile(mem_ref[...], g_ref[...]).astype(BF16)
        mn_ref[...] = mn
        for j in range(N_DEV):
            kv_ref[j] = _dot(mn, w_ref[j]).astype(BF16)

    return pl.pallas_call(
        body, grid=(1,), name=name,
        in_specs=[_const_spec((m, D)), _const_spec((1, D)), _const_spec((N_DEV, D, X_DH))],
        out_specs=[_const_spec((m, D)), _const_spec((N_DEV, m, X_DH))],
        out_shape=[jax.ShapeDtypeStruct((m, D), BF16), jax.ShapeDtypeStruct((N_DEV, m, X_DH), BF16)],
        compiler_params=_cparams(),
    )(mem, g, w_ckv)


def _memkv_bwd(dkv, mem, g, w_ckv, name):
    m = mem.shape[0]

    def body(dkv_ref, mem_ref, g_ref, w_ref, dg_ref):
        dmn = jnp.zeros((m, D), F32)
        for j in range(N_DEV):
            dmn = dmn + _dot_nt(dkv_ref[j].astype(BF16), w_ref[j])
        _, dg = _rms_bwd_tile(mem_ref[...], g_ref[...], dmn)
        dg_ref[...] = dg

    return pl.pallas_call(
        body, grid=(1,), name=name,
        in_specs=[_const_spec((N_DEV, m, X_DH)), _const_spec((m, D)), _const_spec((1, D)),
                  _const_spec((N_DEV, D, X_DH))],
        out_specs=_const_spec((1, D)),
        out_shape=jax.ShapeDtypeStruct((1, D), F32),
        compiler_params=_cparams(),
    )(dkv, mem, g, w_ckv)


def _softmax_rows(s):
    e = jnp.exp(s - jnp.max(s, axis=-1, keepdims=True))
    return e / jnp.sum(e, axis=-1, keepdims=True)


def _cross_pairs(cross_hbm, wq_v, wo_v):
    return _square_pairs(cross_hbm, 0, wq_v) + _square_pairs(cross_hbm, 1, wo_v)


def _cross_fwd(h, g, kv, w_cross, name):
    t = h.shape[0]
    m = kv.shape[1]
    scale = X_DH ** -0.5

    def body(h_ref, g_ref, kv_ref, cross_hbm, hn_ref, qx_ref, o_ref, h3_ref, wq_v, wo_v, sems):
        _load_resident(pl.program_id(0), _cross_pairs(cross_hbm, wq_v, wo_v), sems)
        ht = h_ref[...]
        hn = _rms_fwd_tile(ht, g_ref[...]).astype(BF16)
        hn_ref[...] = hn
        qx = _dot(hn, wq_v[...]).astype(BF16)
        qx_ref[...] = qx
        for hd in range(X_H):
            lo, hi = hd * X_DH, (hd + 1) * X_DH
            p = _softmax_rows(_dot_nt(qx[:, lo:hi], kv_ref[hd]) * scale)
            o_ref[:, lo:hi] = _dot(p.astype(BF16), kv_ref[X_H + hd]).astype(BF16)
        h3_ref[...] = ht + _dot(o_ref[...], wo_v[...])

    return pl.pallas_call(
        body, grid=(t // TM,), name=name,
        in_specs=[_row_spec(TM, D), _const_spec((1, D)), _const_spec((N_DEV, m, X_DH)), _ANY],
        out_specs=[_row_spec(TM, D)] * 4,
        out_shape=[jax.ShapeDtypeStruct((t, D), BF16)] * 3 + [jax.ShapeDtypeStruct((t, D), F32)],
        scratch_shapes=[pltpu.VMEM((D, D), BF16)] * 2 + [pltpu.SemaphoreType.DMA((2 * N_DEV,))],
        compiler_params=_cparams(),
    )(h, g, kv, w_cross)


def _cross_bwd(dh3, h, g, qx, kv, w_cross, name, comm=None):
    t = h.shape[0]
    m = kv.shape[1]
    scale = X_DH ** -0.5

    def body(dh_ref, h_ref, g_ref, qx_ref, kv_ref, cross_hbm,
             dhb_ref, dqx_ref, dkv_ref, dh2_ref, dg_ref, wq_v, wo_v, sems):
        step = pl.program_id(0)
        _load_resident(step, _cross_pairs(cross_hbm, wq_v, wo_v), sems)

        @pl.when(step == 0)
        def _():
            dkv_ref[...] = jnp.zeros_like(dkv_ref)

        dht = dh_ref[...]
        dhb = dht.astype(BF16)
        dhb_ref[...] = dhb
        do = _dot_nt(dhb, wo_v[...]).astype(BF16)
        for hd in range(X_H):
            lo, hi = hd * X_DH, (hd + 1) * X_DH
            qh = qx_ref[:, lo:hi]
            kh = kv_ref[hd]
            p = _softmax_rows(_dot_nt(qh, kh) * scale)
            doh = do[:, lo:hi]
            dp = _dot_nt(doh, kv_ref[X_H + hd])
            ds = (p * (dp - jnp.sum(dp * p, axis=-1, keepdims=True)) * scale).astype(BF16)
            dqx_ref[:, lo:hi] = _dot(ds, kh).astype(BF16)
            dkv_ref[hd] += _dot_tn(ds, qh)
            dkv_ref[X_H + hd] += _dot_tn(p.astype(BF16), doh)
        dhn = _dot_nt(dqx_ref[...], wq_v[...])
        dx, dg = _rms_bwd_tile(h_ref[...], g_ref[...], dhn)
        dh2_ref[...] = dht + dx
        _accumulate(dg_ref, step, dg)

    return _call(
        body, (dh3, h, g, qx, kv, w_cross), grid=(t // TM,), name=name, comm=comm,
        in_specs=[_row_spec(TM, D), _row_spec(TM, D), _const_spec((1, D)), _row_spec(TM, D),
                  _const_spec((N_DEV, m, X_DH)), _ANY],
        out_specs=[_row_spec(TM, D), _row_spec(TM, D), _const_spec((N_DEV, m, X_DH)), _row_spec(TM, D),
                   _const_spec((1, D))],
        out_shape=[jax.ShapeDtypeStruct((t, D), BF16), jax.ShapeDtypeStruct((t, D), BF16),
                   jax.ShapeDtypeStruct((N_DEV, m, X_DH), F32), jax.ShapeDtypeStruct((t, D), F32),
                   jax.ShapeDtypeStruct((1, D), F32)],
        scratch_shapes=[pltpu.VMEM((D, D), BF16)] * 2 + [pltpu.SemaphoreType.DMA((2 * N_DEV,))])


def _loss_bwd(h, g, target, name):
    t = h.shape[0]

    def body(h_ref, g_ref, t_ref, loss_ref, dh_ref, dg_ref):
        step = pl.program_id(0)
        ht = h_ref[...]
        gain = g_ref[...]
        diff = _rms_fwd_tile(ht, gain) - t_ref[...]
        part = 0.5 * jnp.sum(jnp.sum(diff * diff, axis=-1, keepdims=True) / D, axis=0, keepdims=True)
        dx, dg = _rms_bwd_tile(ht, gain, diff / D)
        dh_ref[...] = dx
        _accumulate(loss_ref, step, jnp.broadcast_to(part, (8, 128)))
        _accumulate(dg_ref, step, dg)

    return pl.pallas_call(
        body, grid=(t // TM,), name=name,
        in_specs=[_row_spec(TM, D), _const_spec((1, D)), _row_spec(TM, D)],
        out_specs=[_const_spec((8, 128)), _row_spec(TM, D), _const_spec((1, D))],
        out_shape=[jax.ShapeDtypeStruct((8, 128), F32), jax.ShapeDtypeStruct((t, D), F32),
                   jax.ShapeDtypeStruct((1, D), F32)],
        compiler_params=_cparams(),
    )(h, g, target)


def _adamw(w, gr, m, v, name):
    r, c = w.shape
    tr = _pick_tile(r, (256, 128)) if r > 512 else r

    def body(w_ref, g_ref, m_ref, v_ref, d_ref, nm_ref, nv_ref):
        gt = g_ref[...]
        nm = ADAM_B1 * m_ref[...] + (1.0 - ADAM_B1) * gt
        nv = ADAM_B2 * v_ref[...] + (1.0 - ADAM_B2) * jnp.square(gt)
        m_hat = nm / (1.0 - ADAM_B1 ** ADAM_STEP)
        v_hat = nv / (1.0 - ADAM_B2 ** ADAM_STEP)
        d_ref[...] = -ADAM_LR * (m_hat / (jnp.sqrt(v_hat) + ADAM_EPS) + ADAM_WD * w_ref[...])
        nm_ref[...] = nm
        nv_ref[...] = nv

    spec = _row_spec(tr, c)
    return pl.pallas_call(
        body, grid=(r // tr,), name=name,
        in_specs=[spec] * 4, out_specs=[spec] * 3,
        out_shape=[jax.ShapeDtypeStruct((r, c), F32)] * 3,
        compiler_params=_cparams(),
    )(w, gr, m, v)


def _mesh_pos():
    return lax.axis_index("x"), lax.axis_index("y"), lax.axis_index("c")


def _run_exchange(comm, name):
    c_in, c_out = len(comm.inputs), len(comm.out_shapes)

    def body(*refs):
        cins, couts, sems = refs[:c_in], refs[c_in:c_in + c_out], refs[c_in + c_out:]
        comm.start(cins, couts, sems)
        comm.finish(cins, couts, sems)

    return list(pl.pallas_call(
        body, name=name, out_shape=list(comm.out_shapes),
        in_specs=[_ANY] * c_in, out_specs=[_ANY] * c_out, scratch_shapes=list(comm.sem_shapes),
    )(*comm.inputs))


def _gather_exchange(shards):
    n_arr = len(shards)

    def plan(x_refs, out_refs, sems):
        send_sems, recv_sems, local_sems = sems
        x, y, c = _mesh_pos()
        me, sibling = (x, y, c), (x, y, 1 - c)
        chips = [(1 - x, y), (x, 1 - y), (1 - x, 1 - y)]

        def slot(a, px, py, pc):
            return out_refs[a].at[4 * px + 2 * py + pc]

        def copy(a, k, block, to, src=None):
            return pltpu.make_async_remote_copy(
                src_ref=slot(a, *block) if src is None else src, dst_ref=slot(a, *block),
                send_sem=send_sems.at[a, k], recv_sem=recv_sems.at[a, k],
                device_id=to, device_id_type=pl.DeviceIdType.MESH)

        mine = [pltpu.make_async_copy(x_refs[a], slot(a, *me), local_sems.at[a]) for a in range(n_arr)]
        first = []
        for a in range(n_arr):
            first.append(copy(a, 0, me, sibling, src=x_refs[a]))
            first += [copy(a, 1 + j, me, (*chip, c), src=x_refs[a]) for j, chip in enumerate(chips)]
        return me, sibling, chips, c, copy, mine, first

    def start(x_refs, out_refs, sems):
        _, _, _, _, _, mine, first = plan(x_refs, out_refs, sems)
        for cp in mine + first:
            cp.start()

    def finish(x_refs, out_refs, sems):
        me, sibling, chips, c, copy, mine, first = plan(x_refs, out_refs, sems)
        passed = []
        for a in range(n_arr):
            for j, chip in enumerate(chips):
                copy(a, 1 + j, (*chip, c), me).wait_recv()
                fwd = copy(a, 4 + j, (*chip, c), sibling)
                fwd.start()
                passed.append(fwd)
        for a in range(n_arr):
            copy(a, 0, sibling, me).wait_recv()
            for j, chip in enumerate(chips):
                copy(a, 4 + j, (*chip, 1 - c), me).wait_recv()
        for cp in first + passed:
            cp.wait_send()
        for cp in mine:
            cp.wait()

    return types.SimpleNamespace(
        inputs=list(shards), start=start, finish=finish,
        out_shapes=[jax.ShapeDtypeStruct((N_DEV,) + s.shape, s.dtype) for s in shards],
        sem_shapes=[pltpu.SemaphoreType.DMA((n_arr, 7)), pltpu.SemaphoreType.DMA((n_arr, 7)),
                    pltpu.SemaphoreType.DMA((n_arr,))])


def _pair_exchange(grads):
    n_arr = len(grads)

    def plan(g_refs, land_refs, sems):
        send_sems, recv_sems = sems
        x, y, c = _mesh_pos()
        return [pltpu.make_async_remote_copy(
            src_ref=g_refs[a].at[2 * k + 1 - c], dst_ref=land_refs[a].at[k],
            send_sem=send_sems.at[a, k], recv_sem=recv_sems.at[a, k],
            device_id=(x, y, 1 - c), device_id_type=pl.DeviceIdType.MESH)
            for a in range(n_arr) for k in range(N_CHIP)]

    def start(g_refs, land_refs, sems):
        for cp in plan(g_refs, land_refs, sems):
            cp.start()

    def finish(g_refs, land_refs, sems):
        for cp in plan(g_refs, land_refs, sems):
            cp.wait()

    return types.SimpleNamespace(
        inputs=list(grads), start=start, finish=finish,
        out_shapes=[jax.ShapeDtypeStruct((N_CHIP,) + g.shape[1:], g.dtype) for g in grads],
        sem_shapes=[pltpu.SemaphoreType.DMA((n_arr, N_CHIP)), pltpu.SemaphoreType.DMA((n_arr, N_CHIP))])


def _chip_exchange(parts):
    n_arr = len(parts)

    def plan(p_refs, land_refs, sems):
        send_sems, recv_sems, local_sems = sems
        x, y, c = _mesh_pos()
        my_chip = 2 * x + y
        chips = [(1 - x, y), (x, 1 - y), (1 - x, 1 - y)]
        local = [pltpu.make_async_copy(p_refs[a].at[my_chip], land_refs[a].at[my_chip], local_sems.at[a])
                 for a in range(n_arr)]

        def copy(a, k, src_slot, dst_slot, px, py):
            return pltpu.make_async_remote_copy(
                src_ref=p_refs[a].at[src_slot], dst_ref=land_refs[a].at[dst_slot],
                send_sem=send_sems.at[a, k], recv_sem=recv_sems.at[a, k],
                device_id=(px, py, c), device_id_type=pl.DeviceIdType.MESH)

        sends = [copy(a, k, 2 * px + py, my_chip, px, py) for a in range(n_arr) for k, (px, py) in enumerate(chips)]
        arrivals = [copy(a, k, my_chip, 2 * px + py, px, py) for a in range(n_arr)
                    for k, (px, py) in enumerate(chips)]
        return local, sends, arrivals

    def start(p_refs, land_refs, sems):
        local, sends, _ = plan(p_refs, land_refs, sems)
        for cp in local + sends:
            cp.start()

    def finish(p_refs, land_refs, sems):
        local, sends, arrivals = plan(p_refs, land_refs, sems)
        for cp in arrivals:
            cp.wait_recv()
        for cp in sends:
            cp.wait_send()
        for cp in local:
            cp.wait()

    return types.SimpleNamespace(
        inputs=list(parts), start=start, finish=finish,
        out_shapes=[jax.ShapeDtypeStruct(p.shape, p.dtype) for p in parts],
        sem_shapes=[pltpu.SemaphoreType.DMA((n_arr, 3)), pltpu.SemaphoreType.DMA((n_arr, 3)),
                    pltpu.SemaphoreType.DMA((n_arr,))])


def _row_tile(r, cap=640):
    best = None
    for cand in range(16, min(r, cap) + 1, 16):
        if r % cand == 0:
            best = cand
    return best if best is not None else r


def _pair_sum(g, landed, core, name):
    _, r, c_dim = g.shape
    tr = _row_tile(r)

    def body(core_ref, mine_ref, theirs_ref, o_ref):
        o_ref[0] = (mine_ref[0].astype(F32) + theirs_ref[0].astype(F32)).astype(o_ref.dtype)

    return pl.pallas_call(
        body, name=name,
        grid_spec=pltpu.PrefetchScalarGridSpec(
            num_scalar_prefetch=1, grid=(N_CHIP, r // tr),
            in_specs=[pl.BlockSpec((1, tr, c_dim), lambda k, i, core_ref: (2 * k + core_ref[0], i, 0)),
                      pl.BlockSpec((1, tr, c_dim), lambda k, i, core_ref: (k, i, 0))],
            out_specs=pl.BlockSpec((1, tr, c_dim), lambda k, i, core_ref: (k, i, 0))),
        out_shape=jax.ShapeDtypeStruct((N_CHIP, r, c_dim), g.dtype),
        compiler_params=_cparams(2),
    )(core, g, landed)


def _sum_slots(parts, name):
    n, r, c_dim = parts.shape
    tr = _row_tile(r)

    def body(p_ref, o_ref):
        acc = p_ref[0].astype(F32)
        for k in range(1, n):
            acc = acc + p_ref[k].astype(F32)
        o_ref[...] = acc

    return pl.pallas_call(
        body, grid=(r // tr,), name=name,
        in_specs=[pl.BlockSpec((n, tr, c_dim), lambda i: (0, i, 0))],
        out_specs=_row_spec(tr, c_dim),
        out_shape=jax.ShapeDtypeStruct((r, c_dim), F32),
        compiler_params=_cparams(),
    )(parts)


GAINS = ("g_ffn1", "g_mix", "g_cross", "g_mem", "g_ffn2", "g_final")
SMALL = GAINS + ("b_gate", "conv_w")
SMALL_R = 16
WEIGHT_ORDER = ("g_ffn1", "w_ffn1_gu", "w_ffn1_down", "g_mix", "w_in", "b_gate", "conv_w", "w_conv_out",
                "w_attn_out", "w_o", "g_cross", "g_mem", "w_cq", "w_ckv", "w_co", "g_ffn2", "w_ffn2_gu",
                "w_ffn2_down", "g_final")
GU_NAMES = ("w_ffn1_gu", "w_ffn2_gu")


def _pack_small(vals, conv_rows):
    rows = [vals[n].reshape(1, D) for n in GAINS] + [vals["b_gate"].reshape(2, D), conv_rows.reshape(CONV_K, D)]
    used = len(GAINS) + 2 + CONV_K
    return jnp.concatenate(rows + [jnp.zeros((SMALL_R - used, D), F32)], axis=0)


def _unpack_small(buf):
    out = {n: buf[k] for k, n in enumerate(GAINS)}
    out["b_gate"] = buf[6:8].reshape(2 * D)
    out["conv_w"] = buf[8:8 + CONV_K]
    return out


def _exchange_shards(wts):
    out = {n: jnp.pad(wts[n].astype(BF16), ((0, 0), (0, FF_PAD - FF_BLK))) for n in GU_NAMES}
    for n in ("w_ckv", "w_in", "w_ffn1_down", "w_ffn2_down"):
        out[n] = wts[n].astype(BF16)
    out["mix"] = jnp.concatenate([wts[n].astype(BF16) for n in MIX_MATS], axis=0)
    out["cross"] = jnp.concatenate([wts[n].astype(BF16) for n in CROSS_MATS], axis=0)
    return out


def _by_device(dw):
    return dw.reshape(N_DEV, SQ_ROWS, D)


def _reduce_group(grads, landed, core, names):
    return [_pair_sum(g, l, core, "grads_pair_sum_" + n) for g, l, n in zip(grads, landed, names)]


def _step(x, mem, target, sh, conv_pad, gains, b_gate, core):
    wg1, wd1, conv_all = _run_exchange(_gather_exchange([sh["w_ffn1_gu"], sh["w_ffn1_down"], conv_pad]), "gather_ffn1")
    conv_w = conv_all[:, :CONV_K, :].transpose(1, 0, 2).reshape(CONV_K, D)
    (n1, gate1, up1, act1, h1), (w_in,) = _ffn_fwd(
        x, gains["g_ffn1"], wg1, wd1, "ffn1_fwd", comm=_gather_exchange([sh["w_in"]]))
    (u, pcg, qkv), (w_mix, w_cross, w_ckv) = _inproj_fwd(
        h1, gains["g_mix"], w_in, "inproj_fwd", comm=_gather_exchange([sh["mix"], sh["cross"], sh["w_ckv"]]))
    yc = _conv_fwd(pcg, conv_w, "conv_fwd")
    (ysb, ctot), (wg2, wd2) = _sb_fwd(
        qkv, "sb_fwd", comm=_gather_exchange([sh["w_ffn2_gu"], sh["w_ffn2_down"]]))
    a_mix, b_mix, merged, h2 = _mix_out_fwd(yc, ysb, pcg, b_gate, h1, w_mix, "mix_out_fwd")
    mn, kv = _memkv_fwd(mem, gains["g_mem"], w_ckv, "memkv_fwd")
    hn, qx, o_x, h3 = _cross_fwd(h2, gains["g_cross"], kv, w_cross, "cross_fwd")
    (n4, gate2, up2, act2, h4), _ = _ffn_fwd(h3, gains["g_ffn2"], wg2, wd2, "ffn2_fwd")
    loss, dh4, dg_final = _loss_bwd(h4, gains["g_final"], target, "loss_bwd")

    gs = {"g_final": dg_final}
    (dgu2, dh4b, dh3, gs["g_ffn2"]), _ = _ffn_bwd(dh4, h3, gains["g_ffn2"], gate2, up2, wg2, wd2, "ffn2_bwd")
    grads_a = [_mm_tn_cols(n4, dgu2, "dw_ffn2_gu"),
               _mm_tn_rows(act2, dh4b, FF_BLK, "dw_ffn2_down").reshape(N_DEV, DOWN_ROWS, D)]
    names_a = ["w_ffn2_gu", "w_ffn2_down"]
    (dh3b, dqx, dkv, dh2, gs["g_cross"]), landed_a = _cross_bwd(
        dh3, h2, gains["g_cross"], qx, kv, w_cross, "cross_bwd", comm=_pair_exchange(grads_a))
    sums_a = _reduce_group(grads_a, landed_a, core, names_a)
    grads_b = [_mm_tn_cols(mn, dkv, "dw_ckv"),
               jnp.concatenate([_by_device(_mm_tn(hn, dqx, "dw_cq")), _by_device(_mm_tn(o_x, dh3b, "dw_co"))], axis=1)]
    names_b = ["w_ckv", "cross"]
    gs["g_mem"] = _memkv_bwd(dkv, mem, gains["g_mem"], w_ckv, "memkv_bwd")
    (dh2b, da_mix, db_mix, dgp, dyc, dysb, gs["b_gate"]), landed_b = _mix_out_bwd(
        dh2, a_mix, b_mix, pcg, b_gate, w_mix, "mix_out_bwd", comm=_pair_exchange(grads_b))
    sums_b = _reduce_group(grads_b, landed_b, core, names_b)
    grads_c = [jnp.concatenate([_by_device(_mm_tn(yc, da_mix, "dw_conv_out")),
                                _by_device(_mm_tn(ysb, db_mix, "dw_attn_out")),
                                _by_device(_mm_tn(merged, dh2b, "dw_o"))], axis=1)]
    landed_c = _run_exchange(_pair_exchange(grads_c), "grads_to_sibling_mix")
    sums_c = _reduce_group(grads_c, landed_c, core, ["mix"])
    (dq, dk, dv), parts_abc = _sb_bwd(qkv, dysb, ctot, "sb_bwd", comm=_chip_exchange(sums_a + sums_b + sums_c))
    dcb, dcc, dcx, gs["conv_w"] = _conv_bwd(pcg, conv_w, dyc, "conv_bwd")
    pieces = [dcb, dcc, dcx, dq, dk, dv]
    grads_d = [jnp.concatenate(
        [_mm_tn(u, p, "dw_in_%d" % k)[None] for k, p in enumerate(pieces)] + [_mm_tn_cols(u, dgp, "dw_in_gates")],
        axis=0)]
    (dh1, gs["g_mix"]), landed_d = _inproj_bwd(pieces, dgp, w_in, h1, gains["g_mix"], dh2, "inproj_bwd",
                                               comm=_pair_exchange(grads_d))
    sums_d = _reduce_group(grads_d, landed_d, core, ["w_in"])
    (dgu1, dh1b, dx, gs["g_ffn1"]), parts_d = _ffn_bwd(dh1, x, gains["g_ffn1"], gate1, up1, wg1, wd1, "ffn1_bwd",
                                                       comm=_chip_exchange(sums_d))
    grads_e = [_mm_tn_cols(n1, dgu1, "dw_ffn1_gu"),
               _mm_tn_rows(act1, dh1b, FF_BLK, "dw_ffn1_down").reshape(N_DEV, DOWN_ROWS, D)]
    names_e = ["w_ffn1_gu", "w_ffn1_down"]
    landed_e = _run_exchange(_pair_exchange(grads_e), "grads_to_sibling_ffn1")
    parts_e = _run_exchange(_chip_exchange(_reduce_group(grads_e, landed_e, core, names_e)), "grads_to_chips_ffn1")

    names = names_a + names_b + ["mix"] + ["w_in"] + names_e
    sums = {n: _sum_slots(p, "grads_sum_" + n) for n, p in zip(names, parts_abc + parts_d + parts_e)}
    return loss, dx, sums, gs


def _unpack_grads(sums):
    out = {n: sums[n][:, :FF_BLK] for n in GU_NAMES}
    for n in ("w_ckv", "w_in", "w_ffn1_down", "w_ffn2_down"):
        out[n] = sums[n]
    for k, n in enumerate(MIX_MATS):
        out[n] = sums["mix"][k * SQ_ROWS:(k + 1) * SQ_ROWS]
    for k, n in enumerate(CROSS_MATS):
        out[n] = sums["cross"][k * SQ_ROWS:(k + 1) * SQ_ROWS]
    return out


def kernel(x, mem, g_ffn1, w_ffn1_gu, w_ffn1_down, g_mix, w_in, b_gate, conv_w, w_conv_out, w_attn_out, w_o, g_cross, g_mem, w_cq, w_ckv, w_co, g_ffn2, w_ffn2_gu, w_ffn2_down, g_final, loss_target, m_g_ffn1, m_w_ffn1_gu, m_w_ffn1_down, m_g_mix, m_w_in, m_b_gate, m_conv_w, m_w_conv_out, m_w_attn_out, m_w_o, m_g_cross, m_g_mem, m_w_cq, m_w_ckv, m_w_co, m_g_ffn2, m_w_ffn2_gu, m_w_ffn2_down, m_g_final, v_g_ffn1, v_w_ffn1_gu, v_w_ffn1_down, v_g_mix, v_w_in, v_b_gate, v_conv_w, v_w_conv_out, v_w_attn_out, v_w_o, v_g_cross, v_g_mem, v_w_cq, v_w_ckv, v_w_co, v_g_ffn2, v_w_ffn2_gu, v_w_ffn2_down, v_g_final):
    args = locals()
    wts = {n: args[n] for n in WEIGHT_ORDER}
    mom1 = {n: args["m_" + n] for n in WEIGHT_ORDER}
    mom2 = {n: args["v_" + n] for n in WEIGHT_ORDER}
    cx, cy, cc = _mesh_pos()
    dev = 4 * cx + 2 * cy + cc
    conv_cols = D // N_DEV

    conv_pad = jnp.concatenate([conv_w, jnp.zeros((8 - CONV_K, conv_cols), F32)], axis=0)
    gains = {n: wts[n].reshape(1, D) for n in GAINS}
    loss8, dx, sums, gs = _step(x[0], mem[0], loss_target[0], _exchange_shards(wts), conv_pad, gains,
                                b_gate.reshape(1, 2 * D), cc.reshape(1).astype(jnp.int32))
    loss = lax.psum(loss8[0, 0], MESH_AXES)
    grads = _unpack_grads(sums)

    gs_rows = {n: gs[n] for n in GAINS + ("b_gate",)}
    small_all = _run_exchange(_gather_exchange([_pack_small(gs_rows, gs["conv_w"][:CONV_K])]), "gather_small_grads")[0]
    grad_small = _unpack_small(_sum_slots(small_all, "small_grads_sum"))
    grad_small["conv_w"] = lax.dynamic_slice_in_dim(grad_small["conv_w"], dev * conv_cols, conv_cols, axis=1)
    grads.update(grad_small)

    delta, new_m, new_v = {}, {}, {}
    for n in WEIGHT_ORDER:
        if n not in SMALL:
            delta[n], new_m[n], new_v[n] = _adamw(wts[n], grads[n], mom1[n], mom2[n], "adamw_" + n)

    def small_buf(vals):
        return _pack_small(vals, jnp.concatenate([vals["conv_w"], jnp.zeros((CONV_K, D - conv_cols), F32)], axis=1))

    d_s, m_s, v_s = _adamw(small_buf(wts), small_buf(grads), small_buf(mom1), small_buf(mom2), "adamw_small")
    for res, buf in ((delta, d_s), (new_m, m_s), (new_v, v_s)):
        un = _unpack_small(buf)
        for n in GAINS + ("b_gate",):
            res[n] = un[n]
        res["conv_w"] = un["conv_w"][:, :conv_cols]

    return (loss, dx[None], *[grads[n] for n in WEIGHT_ORDER], *[delta[n] for n in WEIGHT_ORDER],
            *[new_m[n] for n in WEIGHT_ORDER], *[new_v[n] for n in WEIGHT_ORDER])
```

```python
import types

import jax
import jax.numpy as jnp
from jax import lax
from jax.experimental import pallas as pl
from jax.experimental.pallas import tpu as pltpu

F32 = jnp.float32
BF16 = jnp.bfloat16

D = 1024
DFF = 2816
SB_H = 8
SB_DH = 128
X_H = 4
X_DH = 256
CONV_K = 3
RMS_EPS = 1e-6
N_DEV = 8
N_CHIP = 4
SQ_ROWS = D // N_DEV

ADAM_LR = 0.001
ADAM_B1 = 0.9
ADAM_B2 = 0.999
ADAM_EPS = 1e-08
ADAM_WD = 0.01
ADAM_STEP = 10

TM = 256
TQ = 512
TK = 256
SB_HPS = 2
VMEM_LIMIT = 56 << 20

FF_BLK = DFF // 4
FF_PAD = 768
FF_SUB = 256
DOWN_ROWS = DFF // N_DEV

MIX_MATS = ("w_conv_out", "w_attn_out", "w_o")
CROSS_MATS = ("w_cq", "w_co")

MESH_AXES = ("x", "y", "c")
_ANY = pl.BlockSpec(memory_space=pl.ANY)


def _cparams(n_axes=1):
    return pltpu.CompilerParams(
        dimension_semantics=("arbitrary",) * n_axes, vmem_limit_bytes=VMEM_LIMIT)


def _row_spec(tm, n):
    return pl.BlockSpec((tm, n), lambda i: (i, 0))


def _blk_row_spec(nb, tm, n):
    return pl.BlockSpec((nb, tm, n), lambda i: (0, i, 0))


def _const_spec(shape):
    zeros = (0,) * len(shape)
    return pl.BlockSpec(shape, lambda i: zeros)


def _dot(a, b):
    return jnp.dot(a, b, preferred_element_type=F32)


def _dot_nt(a, b):
    return lax.dot_general(a, b, (((1,), (1,)), ((), ())), preferred_element_type=F32)


def _dot_tn(a, b):
    return lax.dot_general(a, b, (((0,), (0,)), ((), ())), preferred_element_type=F32)


def _sigmoid(x):
    return 1.0 / (1.0 + jnp.exp(-x))


def _call(body, operands, *, grid, in_specs, out_specs, out_shape, scratch_shapes, name, comm=None):
    n_in, n_out, n_sc = len(in_specs), len(out_specs), len(scratch_shapes)
    if comm is None:
        outs = pl.pallas_call(
            body, grid=grid, name=name, in_specs=in_specs, out_specs=out_specs, out_shape=out_shape,
            scratch_shapes=scratch_shapes, compiler_params=_cparams(len(grid)))(*operands)
        return list(outs), []
    c_in, c_out, c_sem = len(comm.inputs), len(comm.out_shapes), len(comm.sem_shapes)

    def hosted(*refs):
        bounds = [0, n_in, c_in, n_out, c_out, n_sc, c_sem]
        parts, pos = [], 0
        for k in bounds[1:]:
            parts.append(refs[pos:pos + k])
            pos += k
        ins, cins, outs, couts, scr, sems = parts
        first = pl.program_id(0) == 0
        last = pl.program_id(0) == grid[0] - 1
        for ax in range(1, len(grid)):
            first = jnp.logical_and(first, pl.program_id(ax) == 0)
            last = jnp.logical_and(last, pl.program_id(ax) == grid[ax] - 1)

        @pl.when(first)
        def _():
            comm.start(cins, couts, sems)

        body(*ins, *outs, *scr)

        @pl.when(last)
        def _():
            comm.finish(cins, couts, sems)

    res = pl.pallas_call(
        hosted, grid=grid, name=name, in_specs=list(in_specs) + [_ANY] * c_in,
        out_specs=list(out_specs) + [_ANY] * c_out, out_shape=list(out_shape) + list(comm.out_shapes),
        scratch_shapes=list(scratch_shapes) + list(comm.sem_shapes),
        compiler_params=_cparams(len(grid)))(*operands, *comm.inputs)
    return list(res[:n_out]), list(res[n_out:])


def _load_resident(step, pairs, sems):
    @pl.when(step == 0)
    def _():
        copies = [pltpu.make_async_copy(src, dst, sems.at[k]) for k, (src, dst) in enumerate(pairs)]
        for cp in copies:
            cp.start()
        for cp in copies:
            cp.wait()


def _square_pairs(buf_hbm, index, dst):
    off = index * SQ_ROWS
    return [(buf_hbm.at[d, off:off + SQ_ROWS, :], dst.at[d * SQ_ROWS:(d + 1) * SQ_ROWS, :]) for d in range(N_DEV)]


def _down_pairs(wd_hbm, dst):
    return [(wd_hbm.at[d], dst.at[d // 2, (d % 2) * DOWN_ROWS:(d % 2 + 1) * DOWN_ROWS, :]) for d in range(N_DEV)]


def _zero_down_pad(step, dst):
    @pl.when(step == 0)
    def _():
        dst[:, FF_BLK:, :] = jnp.zeros((4, FF_PAD - FF_BLK, D), BF16)


def _rms_fwd_tile(xt, g):
    r = lax.rsqrt(jnp.mean(xt * xt, axis=-1, keepdims=True) + RMS_EPS)
    return (xt * r) * g


def _rms_bwd_tile(xt, g, dn):
    r = lax.rsqrt(jnp.mean(xt * xt, axis=-1, keepdims=True) + RMS_EPS)
    xhat = xt * r
    dxhat = dn * g
    dx = r * (dxhat - xhat * jnp.mean(dxhat * xhat, axis=-1, keepdims=True))
    dg = jnp.sum(dn * xhat, axis=0, keepdims=True)
    return dx, dg


def _accumulate(ref, step, value):
    @pl.when(step == 0)
    def _():
        ref[...] = value

    @pl.when(step != 0)
    def _():
        ref[...] = ref[...] + value


def _ffn_fwd(x, g, wgu, wd, name, comm=None):
    t = x.shape[0]

    def body(x_ref, g_ref, wgu_hbm, wd_hbm, n_ref, gate_ref, up_ref, act_ref, h_ref, wgu_v, wd_v, sems):
        step = pl.program_id(0)
        _zero_down_pad(step, wd_v)
        _load_resident(step, [(wgu_hbm, wgu_v)] + _down_pairs(wd_hbm, wd_v), sems)
        xt = x_ref[...]
        n = _rms_fwd_tile(xt, g_ref[...]).astype(BF16)
        n_ref[...] = n
        acc = jnp.zeros((TM, D), F32)
        for j in range(4):
            for s in range(FF_PAD // FF_SUB):
                lo, hi = s * FF_SUB, (s + 1) * FF_SUB
                gt = _dot_nt(n, wgu_v[j, lo:hi, :])
                ut = _dot_nt(n, wgu_v[4 + j, lo:hi, :])
                gate_ref[j, :, lo:hi] = gt.astype(BF16)
                up_ref[j, :, lo:hi] = ut.astype(BF16)
                act_ref[j, :, lo:hi] = ((gt * _sigmoid(gt)) * ut).astype(BF16)
            acc = acc + _dot(act_ref[j], wd_v[j])
        h_ref[...] = xt + 0.5 * acc

    ff = jax.ShapeDtypeStruct((4, t, FF_PAD), BF16)
    return _call(
        body, (x, g, wgu, wd), grid=(t // TM,), name=name, comm=comm,
        in_specs=[_row_spec(TM, D), _const_spec((1, D)), _ANY, _ANY],
        out_specs=[_row_spec(TM, D)] + [_blk_row_spec(4, TM, FF_PAD)] * 3 + [_row_spec(TM, D)],
        out_shape=[jax.ShapeDtypeStruct((t, D), BF16), ff, ff, ff, jax.ShapeDtypeStruct((t, D), F32)],
        scratch_shapes=[pltpu.VMEM((N_DEV, FF_PAD, D), BF16), pltpu.VMEM((4, FF_PAD, D), BF16),
                        pltpu.SemaphoreType.DMA((1 + N_DEV,))])


def _ffn_bwd(dh, xin, g, gate, up, wgu, wd, name, comm=None):
    t = dh.shape[0]

    def body(dh_ref, x_ref, g_ref, gate_ref, up_ref, wgu_hbm, wd_hbm,
             dgu_ref, dhb_ref, dx_ref, dg_ref, wgu_v, wd_v, sems):
        step = pl.program_id(0)
        _zero_down_pad(step, wd_v)
        _load_resident(step, [(wgu_hbm, wgu_v)] + _down_pairs(wd_hbm, wd_v), sems)
        dht = dh_ref[...]
        dhb = (0.5 * dht).astype(BF16)
        dhb_ref[...] = dhb
        dn = jnp.zeros((TM, D), F32)
        for j in range(4):
            for s in range(FF_PAD // FF_SUB):
                lo, hi = s * FF_SUB, (s + 1) * FF_SUB
                da = _dot_nt(dhb, wd_v[j, lo:hi, :])
                gt = gate_ref[j, :, lo:hi].astype(F32)
                ut = up_ref[j, :, lo:hi].astype(F32)
                sg = _sigmoid(gt)
                dgt = (da * ut * (sg * (1.0 + gt * (1.0 - sg)))).astype(BF16)
                dut = (da * (gt * sg)).astype(BF16)
                dgu_ref[j, :, lo:hi] = dgt
                dgu_ref[4 + j, :, lo:hi] = dut
            dn = dn + _dot(dgu_ref[j], wgu_v[j]) + _dot(dgu_ref[4 + j], wgu_v[4 + j])
        dx, dg = _rms_bwd_tile(x_ref[...], g_ref[...], dn)
        dx_ref[...] = dht + dx
        _accumulate(dg_ref, step, dg)

    return _call(
        body, (dh, xin, g, gate, up, wgu, wd), grid=(t // TM,), name=name, comm=comm,
        in_specs=[_row_spec(TM, D), _row_spec(TM, D), _const_spec((1, D)), _blk_row_spec(4, TM, FF_PAD),
                  _blk_row_spec(4, TM, FF_PAD), _ANY, _ANY],
        out_specs=[_blk_row_spec(N_DEV, TM, FF_PAD), _row_spec(TM, D), _row_spec(TM, D), _const_spec((1, D))],
        out_shape=[jax.ShapeDtypeStruct((N_DEV, t, FF_PAD), BF16), jax.ShapeDtypeStruct((t, D), BF16),
                   jax.ShapeDtypeStruct((t, D), F32), jax.ShapeDtypeStruct((1, D), F32)],
        scratch_shapes=[pltpu.VMEM((N_DEV, FF_PAD, D), BF16), pltpu.VMEM((4, FF_PAD, D), BF16),
                        pltpu.SemaphoreType.DMA((1 + N_DEV,))])


def _pick_tile(n, options=(512, 256, 128)):
    for o in options:
        if n % o == 0:
            return o
    return n


def _mm_tn(a, b, name):
    k, m = a.shape
    _, n = b.shape
    tm, tn = _pick_tile(m), _pick_tile(n)

    def body(a_ref, b_ref, o_ref):
        o_ref[...] = _dot_tn(a_ref[...].astype(BF16), b_ref[...].astype(BF16)).astype(BF16)

    return pl.pallas_call(
        body, grid=(m // tm, n // tn), name=name,
        in_specs=[pl.BlockSpec((k, tm), lambda i, j: (0, i)), pl.BlockSpec((k, tn), lambda i, j: (0, j))],
        out_specs=pl.BlockSpec((tm, tn), lambda i, j: (i, j)),
        out_shape=jax.ShapeDtypeStruct((m, n), BF16),
        compiler_params=_cparams(2),
    )(a, b)


def _mm_tn_cols(a, b, name):
    k, m = a.shape
    nb, _, n = b.shape
    tm = _pick_tile(m)

    def body(a_ref, b_ref, o_ref):
        o_ref[0] = _dot_tn(a_ref[...].astype(BF16), b_ref[0].astype(BF16)).astype(BF16)

    return pl.pallas_call(
        body, grid=(nb, m // tm), name=name,
        in_specs=[pl.BlockSpec((k, tm), lambda j, i: (0, i)), pl.BlockSpec((1, k, n), lambda j, i: (j, 0, 0))],
        out_specs=pl.BlockSpec((1, tm, n), lambda j, i: (j, i, 0)),
        out_shape=jax.ShapeDtypeStruct((nb, m, n), BF16),
        compiler_params=_cparams(2),
    )(a, b)


def _mm_tn_rows(a, b, keep, name):
    nb, k, m = a.shape
    _, n = b.shape
    tn = _pick_tile(n)

    def body(a_ref, b_ref, o_ref):
        o_ref[0] = _dot_tn(a_ref[0], b_ref[...])[:keep].astype(BF16)

    return pl.pallas_call(
        body, grid=(nb, n // tn), name=name,
        in_specs=[pl.BlockSpec((1, k, m), lambda j, i: (j, 0, 0)), pl.BlockSpec((k, tn), lambda j, i: (0, i))],
        out_specs=pl.BlockSpec((1, keep, tn), lambda j, i: (j, 0, i)),
        out_shape=jax.ShapeDtypeStruct((nb, keep, n), BF16),
        compiler_params=_cparams(2),
    )(a, b)


PCG_W = 5 * D
QKV_W = 3 * D
PROJ_SUB = 512


def _inproj_fwd(h, g, w_in, name, comm=None):
    t = h.shape[0]

    def body(h_ref, g_ref, w_hbm, u_ref, pcg_ref, qkv_ref, w_v, sems):
        _load_resident(pl.program_id(0), [(w_hbm, w_v)], sems)
        u = _rms_fwd_tile(h_ref[...], g_ref[...]).astype(BF16)
        u_ref[...] = u
        for blk in range(N_DEV):
            for s in range(D // PROJ_SUB):
                lo, hi = s * PROJ_SUB, (s + 1) * PROJ_SUB
                p = _dot(u, w_v[blk, :, lo:hi])
                if blk < 3:
                    pcg_ref[:, blk * D + lo:blk * D + hi] = p
                elif blk < 6:
                    qkv_ref[:, (blk - 3) * D + lo:(blk - 3) * D + hi] = p.astype(BF16)
                else:
                    pcg_ref[:, (blk - 3) * D + lo:(blk - 3) * D + hi] = p

    return _call(
        body, (h, g, w_in), grid=(t // TM,), name=name, comm=comm,
        in_specs=[_row_spec(TM, D), _const_spec((1, D)), _ANY],
        out_specs=[_row_spec(TM, D), _row_spec(TM, PCG_W), _row_spec(TM, QKV_W)],
        out_shape=[jax.ShapeDtypeStruct((t, D), BF16), jax.ShapeDtypeStruct((t, PCG_W), F32),
                   jax.ShapeDtypeStruct((t, QKV_W), BF16)],
        scratch_shapes=[pltpu.VMEM((N_DEV, D, D), BF16), pltpu.SemaphoreType.DMA((1,))])


CONV_CW = 256


def _shift_down(v, k, rows):
    return jnp.where(rows >= k, pltpu.roll(v, k, 0), 0.0)


def _shift_up(v, k, rows, t):
    return jnp.where(rows < t - k, pltpu.roll(v, t - k, 0), 0.0)


def _col_spec(t, cw, off):
    return pl.BlockSpec((t, cw), lambda j: (0, j + off))


def _conv_fwd(pcg, conv_w, name):
    t = pcg.shape[0]
    nb = D // CONV_CW

    def body(cb_ref, cc_ref, cx_ref, w_ref, y_ref):
        rows = lax.broadcasted_iota(jnp.int32, (t, CONV_CW), 0)
        xc = cc_ref[...] * cx_ref[...]
        conv = (w_ref[0:1, :] * _shift_down(xc, 2, rows) + w_ref[1:2, :] * _shift_down(xc, 1, rows)
                + w_ref[2:3, :] * xc)
        y_ref[...] = (cb_ref[...] * conv).astype(BF16)

    return pl.pallas_call(
        body, grid=(nb,), name=name,
        in_specs=[_col_spec(t, CONV_CW, 0), _col_spec(t, CONV_CW, nb), _col_spec(t, CONV_CW, 2 * nb),
                  pl.BlockSpec((CONV_K, CONV_CW), lambda j: (0, j))],
        out_specs=_col_spec(t, CONV_CW, 0),
        out_shape=jax.ShapeDtypeStruct((t, D), BF16),
        compiler_params=_cparams(),
    )(pcg, pcg, pcg, conv_w)


def _conv_bwd(pcg, conv_w, dyc, name):
    t = pcg.shape[0]
    nb = D // CONV_CW

    def body(cb_ref, cc_ref, cx_ref, w_ref, dy_ref, dc_ref, dw_ref):
        rows = lax.broadcasted_iota(jnp.int32, (t, CONV_CW), 0)
        cc, cx = cc_ref[...], cx_ref[...]
        xc = cc * cx
        x1 = _shift_down(xc, 1, rows)
        x2 = _shift_down(xc, 2, rows)
        w0, w1, w2 = w_ref[0:1, :], w_ref[1:2, :], w_ref[2:3, :]
        conv = w0 * x2 + w1 * x1 + w2 * xc
        dy = dy_ref[...]
        dc_ref[0] = (dy * conv).astype(BF16)
        dconv = dy * cb_ref[...]
        dw_ref[...] = jnp.zeros((8, CONV_CW), F32)
        dw_ref[0:1, :] = jnp.sum(dconv * x2, axis=0, keepdims=True)
        dw_ref[1:2, :] = jnp.sum(dconv * x1, axis=0, keepdims=True)
        dw_ref[2:3, :] = jnp.sum(dconv * xc, axis=0, keepdims=True)
        dxc = w2 * dconv + w1 * _shift_up(dconv, 1, rows, t) + w0 * _shift_up(dconv, 2, rows, t)
        dc_ref[1] = (dxc * cx).astype(BF16)
        dc_ref[2] = (dxc * cc).astype(BF16)

    return pl.pallas_call(
        body, grid=(nb,), name=name,
        in_specs=[_col_spec(t, CONV_CW, 0), _col_spec(t, CONV_CW, nb), _col_spec(t, CONV_CW, 2 * nb),
                  pl.BlockSpec((CONV_K, CONV_CW), lambda j: (0, j)), _col_spec(t, CONV_CW, 0)],
        out_specs=[pl.BlockSpec((3, t, CONV_CW), lambda j: (0, 0, j)), pl.BlockSpec((8, CONV_CW), lambda j: (0, j))],
        out_shape=[jax.ShapeDtypeStruct((3, t, D), BF16), jax.ShapeDtypeStruct((8, D), F32)],
        compiler_params=_cparams(),
    )(pcg, pcg, pcg, conv_w, dyc)


def _tri2(cond):
    rr = lax.broadcasted_iota(jnp.int32, (2 * TK, TK), 0) & (TK - 1)
    cc = lax.broadcasted_iota(jnp.int32, (2 * TK, TK), 1)
    return cond(rr, cc).astype(BF16)


def _causal(shift):
    rr = lax.broadcasted_iota(jnp.int32, (TQ, TK), 0)
    cc = lax.broadcasted_iota(jnp.int32, (TQ, TK), 1)
    return cc + shift < rr


def _cumdot(v, tri2):
    hi = v.astype(BF16)
    lo = (v - hi.astype(F32)).astype(BF16)
    return _dot(jnp.concatenate([hi, lo], axis=1), tri2)


def _log_1m_beta(z):
    return -(jnp.maximum(z, 0.0) + jnp.log(1.0 + jnp.exp(-jnp.abs(z))))


def _sb_specs(t):
    g = SB_H // SB_HPS
    w = SB_HPS * SB_DH
    q_spec = pl.BlockSpec((TQ, w), lambda h, i: (i, h))
    k_spec = pl.BlockSpec((t, w), lambda h, i: (0, g + h))
    v_spec = pl.BlockSpec((t, w), lambda h, i: (0, 2 * g + h))
    ct_spec = pl.BlockSpec((SB_HPS, TQ, 1), lambda h, i: (h, i, 0))
    return g, w, q_spec, k_spec, v_spec, ct_spec


def _sb_fwd(qkv, name, comm=None):
    t = qkv.shape[0]
    scale = SB_DH ** -0.5
    g, w, q_spec, k_spec, v_spec, ct_spec = _sb_specs(t)

    def body(q_ref, k_ref, v_ref, y_ref, ct_ref):
        i = pl.program_id(1)
        later = _tri2(lambda j, s: j > s)
        n_diag = TQ // TK

        def block(j, carry, shift):
            off = pl.multiple_of(j * TK, TK)
            zs, ms = [], []
            for hd in range(SB_HPS):
                cols = slice(hd * SB_DH, (hd + 1) * SB_DH)
                z = _dot_nt(q_ref[:, cols], k_ref[pl.ds(off, TK), cols]) * scale
                m = _log_1m_beta(z)
                if shift is not None:
                    m = jnp.where(_causal(shift), m, 0.0)
                zs.append(z)
                ms.append(m)
            after = _cumdot(jnp.concatenate(ms, axis=0), later)
            out = []
            for hd in range(SB_HPS):
                acc, c_sum = carry[hd]
                cols = slice(hd * SB_DH, (hd + 1) * SB_DH)
                a = jnp.exp((ms[hd] + zs[hd]) + (c_sum + after[hd * TQ:(hd + 1) * TQ]))
                if shift is not None:
                    a = jnp.where(_causal(shift), a, 0.0)
                out.append((acc + _dot(a.astype(BF16), v_ref[pl.ds(off, TK), cols]),
                            c_sum + jnp.sum(ms[hd], axis=1, keepdims=True)))
            return tuple(out)

        carry = tuple((jnp.zeros((TQ, SB_DH), F32), jnp.zeros((TQ, 1), F32)) for _ in range(SB_HPS))
        for d in reversed(range(n_diag)):
            carry = block(i * n_diag + d, carry, d * TK)
        carry = lax.fori_loop(0, i * n_diag, lambda jj, c: block(i * n_diag - 1 - jj, c, None), carry)
        for hd in range(SB_HPS):
            y_ref[:, hd * SB_DH:(hd + 1) * SB_DH] = carry[hd][0].astype(BF16)
            ct_ref[hd] = carry[hd][1]

    return _call(
        body, (qkv, qkv, qkv), grid=(g, t // TQ), name=name, comm=comm,
        in_specs=[q_spec, k_spec, v_spec],
        out_specs=[q_spec, ct_spec],
        out_shape=[jax.ShapeDtypeStruct((t, D), BF16), jax.ShapeDtypeStruct((SB_H, t, 1), F32)],
        scratch_shapes=[])


def _sb_bwd(qkv, dy, ctot, name, comm=None):
    t = qkv.shape[0]
    scale = SB_DH ** -0.5
    g, w, q_spec, k_spec, v_spec, ct_spec = _sb_specs(t)
    acc_spec = pl.BlockSpec((2, t, w), lambda h, i: (0, 0, h))

    def body(q_ref, k_ref, v_ref, dy_ref, ct_ref, dq_ref, dkv_ref):
        i = pl.program_id(1)

        @pl.when(i == 0)
        def _():
            dkv_ref[...] = jnp.zeros_like(dkv_ref)

        upto = _tri2(lambda j, s: j <= s)
        n_diag = TQ // TK

        def block(j, carry, shift):
            off = pl.multiple_of(j * TK, TK)
            zs, ms = [], []
            for hd in range(SB_HPS):
                cols = slice(hd * SB_DH, (hd + 1) * SB_DH)
                z = _dot_nt(q_ref[:, cols], k_ref[pl.ds(off, TK), cols]) * scale
                m = _log_1m_beta(z)
                if shift is not None:
                    m = jnp.where(_causal(shift), m, 0.0)
                zs.append(z)
                ms.append(m)
            m_upto = _cumdot(jnp.concatenate(ms, axis=0), upto)
            ls, a_s, es = [], [], []
            for hd in range(SB_HPS):
                cols = slice(hd * SB_DH, (hd + 1) * SB_DH)
                l = ms[hd] + zs[hd]
                a = jnp.exp(l + ((ct_ref[hd] - carry[hd][1]) - m_upto[hd * TQ:(hd + 1) * TQ]))
                if shift is not None:
                    a = jnp.where(_causal(shift), a, 0.0)
                ls.append(l)
                a_s.append(a)
                es.append(_dot_nt(dy_ref[:, cols], v_ref[pl.ds(off, TK), cols]) * a)
            e_upto = _cumdot(jnp.concatenate(es, axis=0), upto)
            out = []
            for hd in range(SB_HPS):
                dq, p_sum, e_sum = carry[hd]
                cols = slice(hd * SB_DH, (hd + 1) * SB_DH)
                e = es[hd]
                e_before = e_sum + (e_upto[hd * TQ:(hd + 1) * TQ] - e)
                beta = jnp.exp(ls[hd])
                dz = e * (1.0 - beta) - e_before * beta
                if shift is not None:
                    dz = jnp.where(_causal(shift), dz, 0.0)
                dzs = (dz * scale).astype(BF16)
                dkv_ref[0, pl.ds(off, TK), cols] += _dot_tn(dzs, q_ref[:, cols])
                dkv_ref[1, pl.ds(off, TK), cols] += _dot_tn(a_s[hd].astype(BF16), dy_ref[:, cols])
                out.append((dq + _dot(dzs, k_ref[pl.ds(off, TK), cols]),
                            p_sum + jnp.sum(ms[hd], axis=1, keepdims=True),
                            e_sum + jnp.sum(e, axis=1, keepdims=True)))
            return tuple(out)

        zero = jnp.zeros((TQ, 1), F32)
        init = tuple((jnp.zeros((TQ, SB_DH), F32), zero, zero) for _ in range(SB_HPS))
        carry = lax.fori_loop(0, i * n_diag, lambda j, c: block(j, c, None), init)
        for d in range(n_diag):
            carry = block(i * n_diag + d, carry, d * TK)
        for hd in range(SB_HPS):
            dq_ref[:, hd * SB_DH:(hd + 1) * SB_DH] = carry[hd][0].astype(BF16)

    return _call(
        body, (qkv, qkv, qkv, dy, ctot), grid=(g, t // TQ), name=name, comm=comm,
        in_specs=[q_spec, k_spec, v_spec, q_spec, ct_spec],
        out_specs=[q_spec, acc_spec],
        out_shape=[jax.ShapeDtypeStruct((t, D), BF16), jax.ShapeDtypeStruct((2, t, D), F32)],
        scratch_shapes=[])


def _gate_specs():
    return [pl.BlockSpec((TM, D), lambda i: (i, 3)), pl.BlockSpec((TM, D), lambda i: (i, 4))]


def _mix_pairs(mix_hbm, dsts):
    pairs = []
    for index, dst in enumerate(dsts):
        pairs += _square_pairs(mix_hbm, index, dst)
    return pairs


def _mix_out_fwd(yc, ysb, pcg, b_gate, h, w_mix, name):
    t = h.shape[0]

    def body(yc_ref, ysb_ref, gc_ref, gs_ref, b_ref, h_ref, mix_hbm,
             a_ref, b_out_ref, mg_ref, h2_ref, wc_v, wa_v, wo_v, sems):
        _load_resident(pl.program_id(0), _mix_pairs(mix_hbm, (wc_v, wa_v, wo_v)), sems)
        a = _dot(yc_ref[...], wc_v[...])
        b = _dot(ysb_ref[...], wa_v[...])
        merged = (_sigmoid(gc_ref[...] + b_ref[:, :D]) * a + _sigmoid(gs_ref[...] + b_ref[:, D:]) * b).astype(BF16)
        a_ref[...] = a
        b_out_ref[...] = b
        mg_ref[...] = merged
        h2_ref[...] = h_ref[...] + _dot(merged, wo_v[...])

    return pl.pallas_call(
        body, grid=(t // TM,), name=name,
        in_specs=[_row_spec(TM, D), _row_spec(TM, D)] + _gate_specs()
                 + [_const_spec((1, 2 * D)), _row_spec(TM, D), _ANY],
        out_specs=[_row_spec(TM, D)] * 4,
        out_shape=[jax.ShapeDtypeStruct((t, D), F32), jax.ShapeDtypeStruct((t, D), F32),
                   jax.ShapeDtypeStruct((t, D), BF16), jax.ShapeDtypeStruct((t, D), F32)],
        scratch_shapes=[pltpu.VMEM((D, D), BF16)] * 3 + [pltpu.SemaphoreType.DMA((3 * N_DEV,))],
        compiler_params=_cparams(),
    )(yc, ysb, pcg, pcg, b_gate, h, w_mix)


def _mix_out_bwd(dh2, a, b, pcg, b_gate, w_mix, name, comm=None):
    t = dh2.shape[0]

    def body(dh_ref, a_ref, b_ref, gc_ref, gs_ref, bias_ref, mix_hbm,
             dhb_ref, da_ref, db_ref, dgp_ref, dyc_ref, dysb_ref, dbias_ref, wc_v, wa_v, wo_v, sems):
        step = pl.program_id(0)
        _load_resident(step, _mix_pairs(mix_hbm, (wc_v, wa_v, wo_v)), sems)
        dhb = dh_ref[...].astype(BF16)
        dhb_ref[...] = dhb
        dm = _dot_nt(dhb, wo_v[...])
        gc = _sigmoid(gc_ref[...] + bias_ref[:, :D])
        gs = _sigmoid(gs_ref[...] + bias_ref[:, D:])
        da = (dm * gc).astype(BF16)
        db = (dm * gs).astype(BF16)
        da_ref[...] = da
        db_ref[...] = db
        dgc = dm * a_ref[...] * (gc * (1.0 - gc))
        dgs = dm * b_ref[...] * (gs * (1.0 - gs))
        dgp_ref[0] = dgc.astype(BF16)
        dgp_ref[1] = dgs.astype(BF16)
        _accumulate(dbias_ref.at[:, :D], step, jnp.sum(dgc, axis=0, keepdims=True))
        _accumulate(dbias_ref.at[:, D:], step, jnp.sum(dgs, axis=0, keepdims=True))
        dyc_ref[...] = _dot_nt(da, wc_v[...])
        dysb_ref[...] = _dot_nt(db, wa_v[...]).astype(BF16)

    return _call(
        body, (dh2, a, b, pcg, pcg, b_gate, w_mix), grid=(t // TM,), name=name, comm=comm,
        in_specs=[_row_spec(TM, D)] * 3 + _gate_specs() + [_const_spec((1, 2 * D)), _ANY],
        out_specs=[_row_spec(TM, D)] * 3 + [_blk_row_spec(2, TM, D), _row_spec(TM, D), _row_spec(TM, D),
                                            _const_spec((1, 2 * D))],
        out_shape=[jax.ShapeDtypeStruct((t, D), BF16)] * 3
                  + [jax.ShapeDtypeStruct((2, t, D), BF16), jax.ShapeDtypeStruct((t, D), F32),
                     jax.ShapeDtypeStruct((t, D), BF16), jax.ShapeDtypeStruct((1, 2 * D), F32)],
        scratch_shapes=[pltpu.VMEM((D, D), BF16)] * 3 + [pltpu.SemaphoreType.DMA((3 * N_DEV,))])


def _inproj_bwd(dconv, dq, dkv, dgp, w_in, h, g, dh_res, name, comm=None):
    t = h.shape[0]

    def body(dc_ref, dq_ref, dkv_ref, dgp_ref, w_hbm, h_ref, g_ref, dres_ref, dh_ref, dg_ref, w_v, sems):
        step = pl.program_id(0)
        _load_resident(step, [(w_hbm, w_v)], sems)
        du = _dot_nt(dq_ref[...], w_v[3])
        for k in range(3):
            du = du + _dot_nt(dc_ref[k], w_v[k])
        for k in range(2):
            du = du + _dot_nt(dkv_ref[k].astype(BF16), w_v[4 + k]) + _dot_nt(dgp_ref[k], w_v[6 + k])
        dx, dg = _rms_bwd_tile(h_ref[...], g_ref[...], du)
        dh_ref[...] = dres_ref[...] + dx
        _accumulate(dg_ref, step, dg)

    return _call(
        body, (dconv, dq, dkv, dgp, w_in, h, g, dh_res), grid=(t // TM,), name=name, comm=comm,
        in_specs=[_blk_row_spec(3, TM, D), _row_spec(TM, D), _blk_row_spec(2, TM, D), _blk_row_spec(2, TM, D), _ANY,
                  _row_spec(TM, D), _const_spec((1, D)), _row_spec(TM, D)],
        out_specs=[_row_spec(TM, D), _const_spec((1, D))],
        out_shape=[jax.ShapeDtypeStruct((t, D), F32), jax.ShapeDtypeStruct((1, D), F32)],
        scratch_shapes=[pltpu.VMEM((N_DEV, D, D), BF16), pltpu.SemaphoreType.DMA((1,))])


def _memkv_fwd(mem, g, w_ckv, name):
    m = mem.shape[0]

    def body(mem_ref, g_ref, w_ref, mn_ref, kv_ref):
        mn = _rms_fwd_tile(mem_ref[...], g_ref[...]).astype(BF16)
        mn_ref[...] = mn
        for j in range(N_DEV):
            kv_ref[j] = _dot(mn, w_ref[j]).astype(BF16)

    return pl.pallas_call(
        body, grid=(1,), name=name,
        in_specs=[_const_spec((m, D)), _const_spec((1, D)), _const_spec((N_DEV, D, X_DH))],
        out_specs=[_const_spec((m, D)), _const_spec((N_DEV, m, X_DH))],
        out_shape=[jax.ShapeDtypeStruct((m, D), BF16), jax.ShapeDtypeStruct((N_DEV, m, X_DH), BF16)],
        compiler_params=_cparams(),
    )(mem, g, w_ckv)


def _memkv_bwd(dkv, mem, g, w_ckv, name):
    m = mem.shape[0]

    def body(dkv_ref, mem_ref, g_ref, w_ref, dg_ref):
        dmn = jnp.zeros((m, D), F32)
        for j in range(N_DEV):
            dmn = dmn + _dot_nt(dkv_ref[j].astype(BF16), w_ref[j])
        _, dg = _rms_bwd_tile(mem_ref[...], g_ref[...], dmn)
        dg_ref[...] = dg

    return pl.pallas_call(
        body, grid=(1,), name=name,
        in_specs=[_const_spec((N_DEV, m, X_DH)), _const_spec((m, D)), _const_spec((1, D)),
                  _const_spec((N_DEV, D, X_DH))],
        out_specs=_const_spec((1, D)),
        out_shape=jax.ShapeDtypeStruct((1, D), F32),
        compiler_params=_cparams(),
    )(dkv, mem, g, w_ckv)


def _softmax_rows(s):
    e = jnp.exp(s - jnp.max(s, axis=-1, keepdims=True))
    return e / jnp.sum(e, axis=-1, keepdims=True)


def _cross_pairs(cross_hbm, wq_v, wo_v):
    return _square_pairs(cross_hbm, 0, wq_v) + _square_pairs(cross_hbm, 1, wo_v)


def _cross_fwd(h, g, kv, w_cross, name):
    t = h.shape[0]
    m = kv.shape[1]
    scale = X_DH ** -0.5

    def body(h_ref, g_ref, kv_ref, cross_hbm, hn_ref, qx_ref, o_ref, h3_ref, wq_v, wo_v, sems):
        _load_resident(pl.program_id(0), _cross_pairs(cross_hbm, wq_v, wo_v), sems)
        ht = h_ref[...]
        hn = _rms_fwd_tile(ht, g_ref[...]).astype(BF16)
        hn_ref[...] = hn
        qx = _dot(hn, wq_v[...]).astype(BF16)
        qx_ref[...] = qx
        for hd in range(X_H):
            lo, hi = hd * X_DH, (hd + 1) * X_DH
            p = _softmax_rows(_dot_nt(qx[:, lo:hi], kv_ref[hd]) * scale)
            o_ref[:, lo:hi] = _dot(p.astype(BF16), kv_ref[X_H + hd]).astype(BF16)
        h3_ref[...] = ht + _dot(o_ref[...], wo_v[...])

    return pl.pallas_call(
        body, grid=(t // TM,), name=name,
        in_specs=[_row_spec(TM, D), _const_spec((1, D)), _const_spec((N_DEV, m, X_DH)), _ANY],
        out_specs=[_row_spec(TM, D)] * 4,
        out_shape=[jax.ShapeDtypeStruct((t, D), BF16)] * 3 + [jax.ShapeDtypeStruct((t, D), F32)],
        scratch_shapes=[pltpu.VMEM((D, D), BF16)] * 2 + [pltpu.SemaphoreType.DMA((2 * N_DEV,))],
        compiler_params=_cparams(),
    )(h, g, kv, w_cross)


def _cross_bwd(dh3, h, g, qx, kv, w_cross, name, comm=None):
    t = h.shape[0]
    m = kv.shape[1]
    scale = X_DH ** -0.5

    def body(dh_ref, h_ref, g_ref, qx_ref, kv_ref, cross_hbm,
             dhb_ref, dqx_ref, dkv_ref, dh2_ref, dg_ref, wq_v, wo_v, sems):
        step = pl.program_id(0)
        _load_resident(step, _cross_pairs(cross_hbm, wq_v, wo_v), sems)

        @pl.when(step == 0)
        def _():
            dkv_ref[...] = jnp.zeros_like(dkv_ref)

        dht = dh_ref[...]
        dhb = dht.astype(BF16)
        dhb_ref[...] = dhb
        do = _dot_nt(dhb, wo_v[...]).astype(BF16)
        for hd in range(X_H):
            lo, hi = hd * X_DH, (hd + 1) * X_DH
            qh = qx_ref[:, lo:hi]
            kh = kv_ref[hd]
            p = _softmax_rows(_dot_nt(qh, kh) * scale)
            doh = do[:, lo:hi]
            dp = _dot_nt(doh, kv_ref[X_H + hd])
            ds = (p * (dp - jnp.sum(dp * p, axis=-1, keepdims=True)) * scale).astype(BF16)
            dqx_ref[:, lo:hi] = _dot(ds, kh).astype(BF16)
            dkv_ref[hd] += _dot_tn(ds, qh)
            dkv_ref[X_H + hd] += _dot_tn(p.astype(BF16), doh)
        dhn = _dot_nt(dqx_ref[...], wq_v[...])
        dx, dg = _rms_bwd_tile(h_ref[...], g_ref[...], dhn)
        dh2_ref[...] = dht + dx
        _accumulate(dg_ref, step, dg)

    return _call(
        body, (dh3, h, g, qx, kv, w_cross), grid=(t // TM,), name=name, comm=comm,
        in_specs=[_row_spec(TM, D), _row_spec(TM, D), _const_spec((1, D)), _row_spec(TM, D),
                  _const_spec((N_DEV, m, X_DH)), _ANY],
        out_specs=[_row_spec(TM, D), _row_spec(TM, D), _const_spec((N_DEV, m, X_DH)), _row_spec(TM, D),
                   _const_spec((1, D))],
        out_shape=[jax.ShapeDtypeStruct((t, D), BF16), jax.ShapeDtypeStruct((t, D), BF16),
                   jax.ShapeDtypeStruct((N_DEV, m, X_DH), F32), jax.ShapeDtypeStruct((t, D), F32),
                   jax.ShapeDtypeStruct((1, D), F32)],
        scratch_shapes=[pltpu.VMEM((D, D), BF16)] * 2 + [pltpu.SemaphoreType.DMA((2 * N_DEV,))])


def _loss_bwd(h, g, target, name):
    t = h.shape[0]

    def body(h_ref, g_ref, t_ref, loss_ref, dh_ref, dg_ref):
        step = pl.program_id(0)
        ht = h_ref[...]
        gain = g_ref[...]
        diff = _rms_fwd_tile(ht, gain) - t_ref[...]
        part = 0.5 * jnp.sum(jnp.sum(diff * diff, axis=-1, keepdims=True) / D, axis=0, keepdims=True)
        dx, dg = _rms_bwd_tile(ht, gain, diff / D)
        dh_ref[...] = dx
        _accumulate(loss_ref, step, jnp.broadcast_to(part, (8, 128)))
        _accumulate(dg_ref, step, dg)

    return pl.pallas_call(
        body, grid=(t // TM,), name=name,
        in_specs=[_row_spec(TM, D), _const_spec((1, D)), _row_spec(TM, D)],
        out_specs=[_const_spec((8, 128)), _row_spec(TM, D), _const_spec((1, D))],
        out_shape=[jax.ShapeDtypeStruct((8, 128), F32), jax.ShapeDtypeStruct((t, D), F32),
                   jax.ShapeDtypeStruct((1, D), F32)],
        compiler_params=_cparams(),
    )(h, g, target)


def _adamw(w, parts, m, v, name, row_block=0):
    r, c = w.shape
    n = parts.shape[0]
    tr = _pick_tile(r, (256, 352, 128))
    off = row_block * (r // tr)

    def body(w_ref, p_ref, m_ref, v_ref, g_ref, d_ref, nm_ref, nv_ref):
        gt = p_ref[0].astype(F32)
        for k in range(1, n):
            gt = gt + p_ref[k].astype(F32)
        g_ref[...] = gt
        nm = ADAM_B1 * m_ref[...] + (1.0 - ADAM_B1) * gt
        nv = ADAM_B2 * v_ref[...] + (1.0 - ADAM_B2) * jnp.square(gt)
        m_hat = nm / (1.0 - ADAM_B1 ** ADAM_STEP)
        v_hat = nv / (1.0 - ADAM_B2 ** ADAM_STEP)
        d_ref[...] = -ADAM_LR * (m_hat / (jnp.sqrt(v_hat) + ADAM_EPS) + ADAM_WD * w_ref[...])
        nm_ref[...] = nm
        nv_ref[...] = nv

    spec = _row_spec(tr, c)
    return pl.pallas_call(
        body, grid=(r // tr,), name=name,
        in_specs=[spec, pl.BlockSpec((n, tr, c), lambda i: (0, i + off, 0)), spec, spec], out_specs=[spec] * 4,
        out_shape=[jax.ShapeDtypeStruct((r, c), F32)] * 4,
        compiler_params=_cparams(),
    )(w, parts, m, v)


def _mesh_pos():
    return lax.axis_index("x"), lax.axis_index("y"), lax.axis_index("c")


def _run_exchange(comm, name):
    c_in, c_out = len(comm.inputs), len(comm.out_shapes)

    def body(*refs):
        cins, couts, sems = refs[:c_in], refs[c_in:c_in + c_out], refs[c_in + c_out:]
        comm.start(cins, couts, sems)
        comm.finish(cins, couts, sems)

    return list(pl.pallas_call(
        body, name=name, out_shape=list(comm.out_shapes),
        in_specs=[_ANY] * c_in, out_specs=[_ANY] * c_out, scratch_shapes=list(comm.sem_shapes),
    )(*comm.inputs))


def _gather_exchange(shards):
    n_arr = len(shards)

    def plan(x_refs, out_refs, sems):
        send_sems, recv_sems, local_sems = sems
        x, y, c = _mesh_pos()
        me, sibling = (x, y, c), (x, y, 1 - c)
        chips = [(1 - x, y), (x, 1 - y), (1 - x, 1 - y)]

        def slot(a, px, py, pc):
            return out_refs[a].at[4 * px + 2 * py + pc]

        def copy(a, k, block, to, src=None):
            return pltpu.make_async_remote_copy(
                src_ref=slot(a, *block) if src is None else src, dst_ref=slot(a, *block),
                send_sem=send_sems.at[a, k], recv_sem=recv_sems.at[a, k],
                device_id=to, device_id_type=pl.DeviceIdType.MESH)

        mine = [pltpu.make_async_copy(x_refs[a], slot(a, *me), local_sems.at[a]) for a in range(n_arr)]
        first = []
        for a in range(n_arr):
            first.append(copy(a, 0, me, sibling, src=x_refs[a]))
            first += [copy(a, 1 + j, me, (*chip, c), src=x_refs[a]) for j, chip in enumerate(chips)]
        return me, sibling, chips, c, copy, mine, first

    def start(x_refs, out_refs, sems):
        _, _, _, _, _, mine, first = plan(x_refs, out_refs, sems)
        for cp in mine + first:
            cp.start()

    def finish(x_refs, out_refs, sems):
        me, sibling, chips, c, copy, mine, first = plan(x_refs, out_refs, sems)
        passed = []
        for a in range(n_arr):
            for j, chip in enumerate(chips):
                copy(a, 1 + j, (*chip, c), me).wait_recv()
                fwd = copy(a, 4 + j, (*chip, c), sibling)
                fwd.start()
                passed.append(fwd)
        for a in range(n_arr):
            copy(a, 0, sibling, me).wait_recv()
            for j, chip in enumerate(chips):
                copy(a, 4 + j, (*chip, 1 - c), me).wait_recv()
        for cp in first + passed:
            cp.wait_send()
        for cp in mine:
            cp.wait()

    return types.SimpleNamespace(
        inputs=list(shards), start=start, finish=finish,
        out_shapes=[jax.ShapeDtypeStruct((N_DEV,) + s.shape, s.dtype) for s in shards],
        sem_shapes=[pltpu.SemaphoreType.DMA((n_arr, 7)), pltpu.SemaphoreType.DMA((n_arr, 7)),
                    pltpu.SemaphoreType.DMA((n_arr,))])


def _pair_exchange(grads):
    n_arr = len(grads)

    def plan(g_refs, land_refs, sems):
        send_sems, recv_sems = sems
        x, y, c = _mesh_pos()
        return [pltpu.make_async_remote_copy(
            src_ref=g_refs[a].at[2 * k + 1 - c], dst_ref=land_refs[a].at[k],
            send_sem=send_sems.at[a, k], recv_sem=recv_sems.at[a, k],
            device_id=(x, y, 1 - c), device_id_type=pl.DeviceIdType.MESH)
            for a in range(n_arr) for k in range(N_CHIP)]

    def start(g_refs, land_refs, sems):
        for cp in plan(g_refs, land_refs, sems):
            cp.start()

    def finish(g_refs, land_refs, sems):
        for cp in plan(g_refs, land_refs, sems):
            cp.wait()

    return types.SimpleNamespace(
        inputs=list(grads), start=start, finish=finish,
        out_shapes=[jax.ShapeDtypeStruct((N_CHIP,) + g.shape[1:], g.dtype) for g in grads],
        sem_shapes=[pltpu.SemaphoreType.DMA((n_arr, N_CHIP)), pltpu.SemaphoreType.DMA((n_arr, N_CHIP))])


def _chip_exchange(parts):
    n_arr = len(parts)

    def plan(p_refs, land_refs, sems):
        send_sems, recv_sems, local_sems = sems
        x, y, c = _mesh_pos()
        my_chip = 2 * x + y
        chips = [(1 - x, y), (x, 1 - y), (1 - x, 1 - y)]
        local = [pltpu.make_async_copy(p_refs[a].at[my_chip], land_refs[a].at[my_chip], local_sems.at[a])
                 for a in range(n_arr)]

        def copy(a, k, src_slot, dst_slot, px, py):
            return pltpu.make_async_remote_copy(
                src_ref=p_refs[a].at[src_slot], dst_ref=land_refs[a].at[dst_slot],
                send_sem=send_sems.at[a, k], recv_sem=recv_sems.at[a, k],
                device_id=(px, py, c), device_id_type=pl.DeviceIdType.MESH)

        sends = [copy(a, k, 2 * px + py, my_chip, px, py) for a in range(n_arr) for k, (px, py) in enumerate(chips)]
        arrivals = [copy(a, k, my_chip, 2 * px + py, px, py) for a in range(n_arr)
                    for k, (px, py) in enumerate(chips)]
        return local, sends, arrivals

    def start(p_refs, land_refs, sems):
        local, sends, _ = plan(p_refs, land_refs, sems)
        for cp in local + sends:
            cp.start()

    def finish(p_refs, land_refs, sems):
        local, sends, arrivals = plan(p_refs, land_refs, sems)
        for cp in arrivals:
            cp.wait_recv()
        for cp in sends:
            cp.wait_send()
        for cp in local:
            cp.wait()

    return types.SimpleNamespace(
        inputs=list(parts), start=start, finish=finish,
        out_shapes=[jax.ShapeDtypeStruct(p.shape, p.dtype) for p in parts],
        sem_shapes=[pltpu.SemaphoreType.DMA((n_arr, 3)), pltpu.SemaphoreType.DMA((n_arr, 3)),
                    pltpu.SemaphoreType.DMA((n_arr,))])


def _row_tile(r, cap=640):
    best = None
    for cand in range(16, min(r, cap) + 1, 16):
        if r % cand == 0:
            best = cand
    return best if best is not None else r


def _pair_sum(g, landed, core, name):
    _, r, c_dim = g.shape
    tr = _row_tile(r)

    def body(core_ref, mine_ref, theirs_ref, o_ref):
        o_ref[0] = (mine_ref[0].astype(F32) + theirs_ref[0].astype(F32)).astype(o_ref.dtype)

    return pl.pallas_call(
        body, name=name,
        grid_spec=pltpu.PrefetchScalarGridSpec(
            num_scalar_prefetch=1, grid=(N_CHIP, r // tr),
            in_specs=[pl.BlockSpec((1, tr, c_dim), lambda k, i, core_ref: (2 * k + core_ref[0], i, 0)),
                      pl.BlockSpec((1, tr, c_dim), lambda k, i, core_ref: (k, i, 0))],
            out_specs=pl.BlockSpec((1, tr, c_dim), lambda k, i, core_ref: (k, i, 0))),
        out_shape=jax.ShapeDtypeStruct((N_CHIP, r, c_dim), g.dtype),
        compiler_params=_cparams(2),
    )(core, g, landed)


def _sum_slots(parts, name):
    n, r, c_dim = parts.shape
    tr = _row_tile(r)

    def body(p_ref, o_ref):
        acc = p_ref[0].astype(F32)
        for k in range(1, n):
            acc = acc + p_ref[k].astype(F32)
        o_ref[...] = acc

    return pl.pallas_call(
        body, grid=(r // tr,), name=name,
        in_specs=[pl.BlockSpec((n, tr, c_dim), lambda i: (0, i, 0))],
        out_specs=_row_spec(tr, c_dim),
        out_shape=jax.ShapeDtypeStruct((r, c_dim), F32),
        compiler_params=_cparams(),
    )(parts)


GAINS = ("g_ffn1", "g_mix", "g_cross", "g_mem", "g_ffn2", "g_final")
SMALL = GAINS + ("b_gate", "conv_w")
SMALL_R = 16
WEIGHT_ORDER = ("g_ffn1", "w_ffn1_gu", "w_ffn1_down", "g_mix", "w_in", "b_gate", "conv_w", "w_conv_out",
                "w_attn_out", "w_o", "g_cross", "g_mem", "w_cq", "w_ckv", "w_co", "g_ffn2", "w_ffn2_gu",
                "w_ffn2_down", "g_final")
GU_NAMES = ("w_ffn1_gu", "w_ffn2_gu")


def _pack_small(vals, conv_rows):
    rows = [vals[n].reshape(1, D) for n in GAINS] + [vals["b_gate"].reshape(2, D), conv_rows.reshape(CONV_K, D)]
    used = len(GAINS) + 2 + CONV_K
    return jnp.concatenate(rows + [jnp.zeros((SMALL_R - used, D), F32)], axis=0)


def _unpack_small(buf):
    out = {n: buf[k] for k, n in enumerate(GAINS)}
    out["b_gate"] = buf[6:8].reshape(2 * D)
    out["conv_w"] = buf[8:8 + CONV_K]
    return out


def _exchange_shards(wts):
    out = {n: jnp.pad(wts[n].T.astype(BF16), ((0, FF_PAD - FF_BLK), (0, 0))) for n in GU_NAMES}
    for n in ("w_ckv", "w_in", "w_ffn1_down", "w_ffn2_down"):
        out[n] = wts[n].astype(BF16)
    out["mix"] = jnp.concatenate([wts[n].astype(BF16) for n in MIX_MATS], axis=0)
    out["cross"] = jnp.concatenate([wts[n].astype(BF16) for n in CROSS_MATS], axis=0)
    return out


def _by_device(dw):
    return dw.reshape(N_DEV, SQ_ROWS, D)


def _reduce_group(grads, landed, core, names):
    return [_pair_sum(g, l, core, "grads_pair_sum_" + n) for g, l, n in zip(grads, landed, names)]


def _step(x, mem, target, sh, conv_pad, gains, b_gate, core):
    wg1, wd1, conv_all = _run_exchange(_gather_exchange([sh["w_ffn1_gu"], sh["w_ffn1_down"], conv_pad]), "gather_ffn1")
    conv_w = conv_all[:, :CONV_K, :].transpose(1, 0, 2).reshape(CONV_K, D)
    (n1, gate1, up1, act1, h1), (w_in,) = _ffn_fwd(
        x, gains["g_ffn1"], wg1, wd1, "ffn1_fwd", comm=_gather_exchange([sh["w_in"]]))
    (u, pcg, qkv), (w_mix, w_cross, w_ckv) = _inproj_fwd(
        h1, gains["g_mix"], w_in, "inproj_fwd", comm=_gather_exchange([sh["mix"], sh["cross"], sh["w_ckv"]]))
    yc = _conv_fwd(pcg, conv_w, "conv_fwd")
    (ysb, ctot), (wg2, wd2) = _sb_fwd(
        qkv, "sb_fwd", comm=_gather_exchange([sh["w_ffn2_gu"], sh["w_ffn2_down"]]))
    a_mix, b_mix, merged, h2 = _mix_out_fwd(yc, ysb, pcg, b_gate, h1, w_mix, "mix_out_fwd")
    mn, kv = _memkv_fwd(mem, gains["g_mem"], w_ckv, "memkv_fwd")
    hn, qx, o_x, h3 = _cross_fwd(h2, gains["g_cross"], kv, w_cross, "cross_fwd")
    (n4, gate2, up2, act2, h4), _ = _ffn_fwd(h3, gains["g_ffn2"], wg2, wd2, "ffn2_fwd")
    loss, dh4, dg_final = _loss_bwd(h4, gains["g_final"], target, "loss_bwd")

    gs = {"g_final": dg_final}
    (dgu2, dh4b, dh3, gs["g_ffn2"]), _ = _ffn_bwd(dh4, h3, gains["g_ffn2"], gate2, up2, wg2, wd2, "ffn2_bwd")
    grads_a = [_mm_tn_rows(dgu2, n4, FF_PAD, "dw_ffn2_gu"),
               _mm_tn_rows(act2, dh4b, FF_BLK, "dw_ffn2_down").reshape(N_DEV, DOWN_ROWS, D)]
    names_a = ["w_ffn2_gu", "w_ffn2_down"]
    (dh3b, dqx, dkv, dh2, gs["g_cross"]), landed_a = _cross_bwd(
        dh3, h2, gains["g_cross"], qx, kv, w_cross, "cross_bwd", comm=_pair_exchange(grads_a))
    sums_a = _reduce_group(grads_a, landed_a, core, names_a)
    grads_b = [_mm_tn_cols(mn, dkv, "dw_ckv"),
               jnp.concatenate([_by_device(_mm_tn(hn, dqx, "dw_cq")), _by_device(_mm_tn(o_x, dh3b, "dw_co"))], axis=1)]
    names_b = ["w_ckv", "cross"]
    gs["g_mem"] = _memkv_bwd(dkv, mem, gains["g_mem"], w_ckv, "memkv_bwd")
    (dh2b, da_mix, db_mix, dgp, dyc, dysb, gs["b_gate"]), landed_b = _mix_out_bwd(
        dh2, a_mix, b_mix, pcg, b_gate, w_mix, "mix_out_bwd", comm=_pair_exchange(grads_b))
    sums_b = _reduce_group(grads_b, landed_b, core, names_b)
    grads_c = [jnp.concatenate([_by_device(_mm_tn(yc, da_mix, "dw_conv_out")),
                                _by_device(_mm_tn(ysb, db_mix, "dw_attn_out")),
                                _by_device(_mm_tn(merged, dh2b, "dw_o"))], axis=1)]
    landed_c = _run_exchange(_pair_exchange(grads_c), "grads_to_sibling_mix")
    sums_c = _reduce_group(grads_c, landed_c, core, ["mix"])
    (dq, dkv_sb), parts_abc = _sb_bwd(qkv, dysb, ctot, "sb_bwd", comm=_chip_exchange(sums_a + sums_b + sums_c))
    dconv, gs["conv_w"] = _conv_bwd(pcg, conv_w, dyc, "conv_bwd")
    grads_d = [jnp.concatenate(
        [_mm_tn_cols(u, dconv, "dw_in_conv"), _mm_tn(u, dq, "dw_in_q")[None], _mm_tn_cols(u, dkv_sb, "dw_in_kv"),
         _mm_tn_cols(u, dgp, "dw_in_gates")], axis=0)]
    (dh1, gs["g_mix"]), landed_d = _inproj_bwd(dconv, dq, dkv_sb, dgp, w_in, h1, gains["g_mix"], dh2, "inproj_bwd",
                                               comm=_pair_exchange(grads_d))
    sums_d = _reduce_group(grads_d, landed_d, core, ["w_in"])
    (dgu1, dh1b, dx, gs["g_ffn1"]), parts_d = _ffn_bwd(dh1, x, gains["g_ffn1"], gate1, up1, wg1, wd1, "ffn1_bwd",
                                                       comm=_chip_exchange(sums_d))
    grads_e = [_mm_tn_rows(dgu1, n1, FF_PAD, "dw_ffn1_gu"),
               _mm_tn_rows(act1, dh1b, FF_BLK, "dw_ffn1_down").reshape(N_DEV, DOWN_ROWS, D)]
    names_e = ["w_ffn1_gu", "w_ffn1_down"]
    landed_e = _run_exchange(_pair_exchange(grads_e), "grads_to_sibling_ffn1")
    parts_e = _run_exchange(_chip_exchange(_reduce_group(grads_e, landed_e, core, names_e)), "grads_to_chips_ffn1")

    names = names_a + names_b + ["mix"] + ["w_in"] + names_e
    return loss, dx, dict(zip(names, parts_abc + parts_d + parts_e)), gs


def kernel(x, mem, g_ffn1, w_ffn1_gu, w_ffn1_down, g_mix, w_in, b_gate, conv_w, w_conv_out, w_attn_out, w_o, g_cross, g_mem, w_cq, w_ckv, w_co, g_ffn2, w_ffn2_gu, w_ffn2_down, g_final, loss_target, m_g_ffn1, m_w_ffn1_gu, m_w_ffn1_down, m_g_mix, m_w_in, m_b_gate, m_conv_w, m_w_conv_out, m_w_attn_out, m_w_o, m_g_cross, m_g_mem, m_w_cq, m_w_ckv, m_w_co, m_g_ffn2, m_w_ffn2_gu, m_w_ffn2_down, m_g_final, v_g_ffn1, v_w_ffn1_gu, v_w_ffn1_down, v_g_mix, v_w_in, v_b_gate, v_conv_w, v_w_conv_out, v_w_attn_out, v_w_o, v_g_cross, v_g_mem, v_w_cq, v_w_ckv, v_w_co, v_g_ffn2, v_w_ffn2_gu, v_w_ffn2_down, v_g_final):
    args = locals()
    wts = {n: args[n] for n in WEIGHT_ORDER}
    mom1 = {n: args["m_" + n] for n in WEIGHT_ORDER}
    mom2 = {n: args["v_" + n] for n in WEIGHT_ORDER}
    cx, cy, cc = _mesh_pos()
    dev = 4 * cx + 2 * cy + cc
    conv_cols = D // N_DEV

    conv_pad = jnp.concatenate([conv_w, jnp.zeros((8 - CONV_K, conv_cols), F32)], axis=0)
    gains = {n: wts[n].reshape(1, D) for n in GAINS}
    loss8, dx, parts, gs = _step(x[0], mem[0], loss_target[0], _exchange_shards(wts), conv_pad, gains,
                                 b_gate.reshape(1, 2 * D), cc.reshape(1).astype(jnp.int32))
    loss = lax.psum(loss8[0, 0], MESH_AXES)

    grads, delta, new_m, new_v = {}, {}, {}, {}

    def update(n, buf, row_block=0, transposed=False):
        w, m1, m2 = wts[n], mom1[n], mom2[n]
        if transposed:
            w, m1, m2 = w.T, m1.T, m2.T
        res = _adamw(w, parts[buf], m1, m2, "adamw_" + n, row_block)
        if transposed:
            res = [r.T for r in res]
        grads[n], delta[n], new_m[n], new_v[n] = res

    for n in GU_NAMES:
        update(n, n, transposed=True)
    for n in ("w_ckv", "w_in", "w_ffn1_down", "w_ffn2_down"):
        update(n, n)
    for k, n in enumerate(MIX_MATS):
        update(n, "mix", k)
    for k, n in enumerate(CROSS_MATS):
        update(n, "cross", k)

    gs_rows = {n: gs[n] for n in GAINS + ("b_gate",)}
    small_all = _run_exchange(_gather_exchange([_pack_small(gs_rows, gs["conv_w"][:CONV_K])]), "gather_small_grads")[0]
    grad_small = _unpack_small(_sum_slots(small_all, "small_grads_sum"))
    grad_small["conv_w"] = lax.dynamic_slice_in_dim(grad_small["conv_w"], dev * conv_cols, conv_cols, axis=1)
    grads.update(grad_small)

    def small_buf(vals):
        return _pack_small(vals, jnp.concatenate([vals["conv_w"], jnp.zeros((CONV_K, D - conv_cols), F32)], axis=1))

    _, d_s, m_s, v_s = _adamw(small_buf(wts), small_buf(grads)[None], small_buf(mom1), small_buf(mom2), "adamw_small")
    for res, buf in ((delta, d_s), (new_m, m_s), (new_v, v_s)):
        un = _unpack_small(buf)
        for n in GAINS + ("b_gate",):
            res[n] = un[n]
        res["conv_w"] = un["conv_w"][:, :conv_cols]

    return (loss, dx[None], *[grads[n] for n in WEIGHT_ORDER], *[delta[n] for n in WEIGHT_ORDER],
            *[new_m[n] for n in WEIGHT_ORDER], *[new_v[n] for n in WEIGHT_ORDER])
```

```python
import types

import jax
import jax.numpy as jnp
from jax import lax
from jax.experimental import pallas as pl
from jax.experimental.pallas import tpu as pltpu

F32 = jnp.float32
BF16 = jnp.bfloat16

D = 1024
DFF = 2816
SB_H = 8
SB_DH = 128
X_H = 4
X_DH = 256
CONV_K = 3
RMS_EPS = 1e-6
N_DEV = 8
N_CHIP = 4
SQ_ROWS = D // N_DEV

ADAM_LR = 0.001
ADAM_B1 = 0.9
ADAM_B2 = 0.999
ADAM_EPS = 1e-08
ADAM_WD = 0.01
ADAM_STEP = 10

TM = 256
TQ = 512
TK = 256
SB_HPS = 2
VMEM_LIMIT = 56 << 20

FF_BLK = DFF // 4
FF_PAD = 768
FF_SUB = 256
DOWN_ROWS = DFF // N_DEV

MIX_MATS = ("w_conv_out", "w_attn_out", "w_o")
CROSS_MATS = ("w_cq", "w_co")

MESH_AXES = ("x", "y", "c")
_ANY = pl.BlockSpec(memory_space=pl.ANY)


def _cparams(n_axes=1):
    return pltpu.CompilerParams(
        dimension_semantics=("arbitrary",) * n_axes, vmem_limit_bytes=VMEM_LIMIT)


def _row_spec(tm, n):
    return pl.BlockSpec((tm, n), lambda i: (i, 0))


def _blk_row_spec(nb, tm, n):
    return pl.BlockSpec((nb, tm, n), lambda i: (0, i, 0))


def _const_spec(shape):
    zeros = (0,) * len(shape)
    return pl.BlockSpec(shape, lambda i: zeros)


def _dot(a, b):
    return jnp.dot(a, b, preferred_element_type=F32)


def _dot_nt(a, b):
    return lax.dot_general(a, b, (((1,), (1,)), ((), ())), preferred_element_type=F32)


def _dot_tn(a, b):
    return lax.dot_general(a, b, (((0,), (0,)), ((), ())), preferred_element_type=F32)


def _sigmoid(x):
    return 1.0 / (1.0 + jnp.exp(-x))


def _call(body, operands, *, grid, in_specs, out_specs, out_shape, scratch_shapes, name, comm=None):
    n_in, n_out, n_sc = len(in_specs), len(out_specs), len(scratch_shapes)
    if comm is None:
        outs = pl.pallas_call(
            body, grid=grid, name=name, in_specs=in_specs, out_specs=out_specs, out_shape=out_shape,
            scratch_shapes=scratch_shapes, compiler_params=_cparams(len(grid)))(*operands)
        return list(outs), []
    c_in, c_out, c_sem = len(comm.inputs), len(comm.out_shapes), len(comm.sem_shapes)

    def hosted(*refs):
        bounds = [0, n_in, c_in, n_out, c_out, n_sc, c_sem]
        parts, pos = [], 0
        for k in bounds[1:]:
            parts.append(refs[pos:pos + k])
            pos += k
        ins, cins, outs, couts, scr, sems = parts
        step, n_steps = pl.program_id(0), grid[0]
        for ax in range(1, len(grid)):
            step, n_steps = step * grid[ax] + pl.program_id(ax), n_steps * grid[ax]

        @pl.when(step == 0)
        def _():
            comm.start(cins, couts, sems)

        @pl.when(step == (2 * n_steps) // 3)
        def _():
            comm.middle(cins, couts, sems)

        body(*ins, *outs, *scr)

        @pl.when(step == n_steps - 1)
        def _():
            comm.finish(cins, couts, sems)

    res = pl.pallas_call(
        hosted, grid=grid, name=name, in_specs=list(in_specs) + [_ANY] * c_in,
        out_specs=list(out_specs) + [_ANY] * c_out, out_shape=list(out_shape) + list(comm.out_shapes),
        scratch_shapes=list(scratch_shapes) + list(comm.sem_shapes),
        compiler_params=_cparams(len(grid)))(*operands, *comm.inputs)
    return list(res[:n_out]), list(res[n_out:])


def _load_resident(step, pairs, sems):
    @pl.when(step == 0)
    def _():
        copies = [pltpu.make_async_copy(src, dst, sems.at[k]) for k, (src, dst) in enumerate(pairs)]
        for cp in copies:
            cp.start()
        for cp in copies:
            cp.wait()


def _square_pairs(buf_hbm, index, dst):
    off = index * SQ_ROWS
    return [(buf_hbm.at[d, off:off + SQ_ROWS, :], dst.at[d * SQ_ROWS:(d + 1) * SQ_ROWS, :]) for d in range(N_DEV)]


def _down_pairs(wd_hbm, dst):
    return [(wd_hbm.at[d], dst.at[d // 2, (d % 2) * DOWN_ROWS:(d % 2 + 1) * DOWN_ROWS, :]) for d in range(N_DEV)]


def _zero_down_pad(step, dst):
    @pl.when(step == 0)
    def _():
        dst[:, FF_BLK:, :] = jnp.zeros((4, FF_PAD - FF_BLK, D), BF16)


def _rms_fwd_tile(xt, g):
    r = lax.rsqrt(jnp.mean(xt * xt, axis=-1, keepdims=True) + RMS_EPS)
    return (xt * r) * g


def _rms_bwd_tile(xt, g, dn):
    r = lax.rsqrt(jnp.mean(xt * xt, axis=-1, keepdims=True) + RMS_EPS)
    xhat = xt * r
    dxhat = dn * g
    dx = r * (dxhat - xhat * jnp.mean(dxhat * xhat, axis=-1, keepdims=True))
    dg = jnp.sum(dn * xhat, axis=0, keepdims=True)
    return dx, dg


def _accumulate(ref, step, value):
    @pl.when(step == 0)
    def _():
        ref[...] = value

    @pl.when(step != 0)
    def _():
        ref[...] = ref[...] + value


def _ffn_fwd(x, g, wgu, wd, name, comm=None):
    t = x.shape[0]

    def body(x_ref, g_ref, wgu_hbm, wd_hbm, n_ref, gate_ref, up_ref, act_ref, h_ref, wgu_v, wd_v, sems):
        step = pl.program_id(0)
        _zero_down_pad(step, wd_v)
        _load_resident(step, [(wgu_hbm, wgu_v)] + _down_pairs(wd_hbm, wd_v), sems)
        xt = x_ref[...]
        n = _rms_fwd_tile(xt, g_ref[...]).astype(BF16)
        n_ref[...] = n
        acc = jnp.zeros((TM, D), F32)
        for j in range(4):
            for s in range(FF_PAD // FF_SUB):
                lo, hi = s * FF_SUB, (s + 1) * FF_SUB
                gt = _dot_nt(n, wgu_v[j, lo:hi, :])
                ut = _dot_nt(n, wgu_v[4 + j, lo:hi, :])
                gate_ref[j, :, lo:hi] = gt.astype(BF16)
                up_ref[j, :, lo:hi] = ut.astype(BF16)
                act_ref[j, :, lo:hi] = ((gt * _sigmoid(gt)) * ut).astype(BF16)
            acc = acc + _dot(act_ref[j], wd_v[j])
        h_ref[...] = xt + 0.5 * acc

    ff = jax.ShapeDtypeStruct((4, t, FF_PAD), BF16)
    return _call(
        body, (x, g, wgu, wd), grid=(t // TM,), name=name, comm=comm,
        in_specs=[_row_spec(TM, D), _const_spec((1, D)), _ANY, _ANY],
        out_specs=[_row_spec(TM, D)] + [_blk_row_spec(4, TM, FF_PAD)] * 3 + [_row_spec(TM, D)],
        out_shape=[jax.ShapeDtypeStruct((t, D), BF16), ff, ff, ff, jax.ShapeDtypeStruct((t, D), F32)],
        scratch_shapes=[pltpu.VMEM((N_DEV, FF_PAD, D), BF16), pltpu.VMEM((4, FF_PAD, D), BF16),
                        pltpu.SemaphoreType.DMA((1 + N_DEV,))])


def _ffn_bwd(dh, xin, g, gate, up, wgu, wd, name, comm=None):
    t = dh.shape[0]

    def body(dh_ref, x_ref, g_ref, gate_ref, up_ref, wgu_hbm, wd_hbm,
             dgu_ref, dhb_ref, dx_ref, dg_ref, wgu_v, wd_v, sems):
        step = pl.program_id(0)
        _zero_down_pad(step, wd_v)
        _load_resident(step, [(wgu_hbm, wgu_v)] + _down_pairs(wd_hbm, wd_v), sems)
        dht = dh_ref[...]
        dhb = (0.5 * dht).astype(BF16)
        dhb_ref[...] = dhb
        dn = jnp.zeros((TM, D), F32)
        for j in range(4):
            for s in range(FF_PAD // FF_SUB):
                lo, hi = s * FF_SUB, (s + 1) * FF_SUB
                da = _dot_nt(dhb, wd_v[j, lo:hi, :])
                gt = gate_ref[j, :, lo:hi].astype(F32)
                ut = up_ref[j, :, lo:hi].astype(F32)
                sg = _sigmoid(gt)
                dgt = (da * ut * (sg * (1.0 + gt * (1.0 - sg)))).astype(BF16)
                dut = (da * (gt * sg)).astype(BF16)
                dgu_ref[j, :, lo:hi] = dgt
                dgu_ref[4 + j, :, lo:hi] = dut
            dn = dn + _dot(dgu_ref[j], wgu_v[j]) + _dot(dgu_ref[4 + j], wgu_v[4 + j])
        dx, dg = _rms_bwd_tile(x_ref[...], g_ref[...], dn)
        dx_ref[...] = dht + dx
        _accumulate(dg_ref, step, dg)

    return _call(
        body, (dh, xin, g, gate, up, wgu, wd), grid=(t // TM,), name=name, comm=comm,
        in_specs=[_row_spec(TM, D), _row_spec(TM, D), _const_spec((1, D)), _blk_row_spec(4, TM, FF_PAD),
                  _blk_row_spec(4, TM, FF_PAD), _ANY, _ANY],
        out_specs=[_blk_row_spec(N_DEV, TM, FF_PAD), _row_spec(TM, D), _row_spec(TM, D), _const_spec((1, D))],
        out_shape=[jax.ShapeDtypeStruct((N_DEV, t, FF_PAD), BF16), jax.ShapeDtypeStruct((t, D), BF16),
                   jax.ShapeDtypeStruct((t, D), F32), jax.ShapeDtypeStruct((1, D), F32)],
        scratch_shapes=[pltpu.VMEM((N_DEV, FF_PAD, D), BF16), pltpu.VMEM((4, FF_PAD, D), BF16),
                        pltpu.SemaphoreType.DMA((1 + N_DEV,))])


WIDE_TILES = (1024, 512, 256, 128)


def _pick_tile(n, options=(512, 256, 128)):
    for o in options:
        if n % o == 0:
            return o
    return n


def _mm_tn(a, b, name):
    k, m = a.shape
    _, n = b.shape
    tm, tn = _pick_tile(m, WIDE_TILES), _pick_tile(n)

    def body(a_ref, b_ref, o_ref):
        o_ref[...] = _dot_tn(a_ref[...].astype(BF16), b_ref[...].astype(BF16)).astype(BF16)

    return pl.pallas_call(
        body, grid=(m // tm, n // tn), name=name,
        in_specs=[pl.BlockSpec((k, tm), lambda i, j: (0, i)), pl.BlockSpec((k, tn), lambda i, j: (0, j))],
        out_specs=pl.BlockSpec((tm, tn), lambda i, j: (i, j)),
        out_shape=jax.ShapeDtypeStruct((m, n), BF16),
        compiler_params=_cparams(2),
    )(a, b)


def _mm_tn_cols(a, b, name):
    k, m = a.shape
    nb, _, n = b.shape
    tm = _pick_tile(m, WIDE_TILES)

    def body(a_ref, b_ref, o_ref):
        o_ref[0] = _dot_tn(a_ref[...].astype(BF16), b_ref[0].astype(BF16)).astype(BF16)

    return pl.pallas_call(
        body, grid=(nb, m // tm), name=name,
        in_specs=[pl.BlockSpec((k, tm), lambda j, i: (0, i)), pl.BlockSpec((1, k, n), lambda j, i: (j, 0, 0))],
        out_specs=pl.BlockSpec((1, tm, n), lambda j, i: (j, i, 0)),
        out_shape=jax.ShapeDtypeStruct((nb, m, n), BF16),
        compiler_params=_cparams(2),
    )(a, b)


def _mm_tn_rows(a, b, keep, name):
    nb, k, m = a.shape
    _, n = b.shape
    tn = _pick_tile(n, WIDE_TILES)

    def body(a_ref, b_ref, o_ref):
        o_ref[0] = _dot_tn(a_ref[0], b_ref[...])[:keep].astype(BF16)

    return pl.pallas_call(
        body, grid=(nb, n // tn), name=name,
        in_specs=[pl.BlockSpec((1, k, m), lambda j, i: (j, 0, 0)), pl.BlockSpec((k, tn), lambda j, i: (0, i))],
        out_specs=pl.BlockSpec((1, keep, tn), lambda j, i: (j, 0, i)),
        out_shape=jax.ShapeDtypeStruct((nb, keep, n), BF16),
        compiler_params=_cparams(2),
    )(a, b)


PCG_W = 5 * D
QKV_W = 3 * D
PROJ_SUB = 512


def _inproj_fwd(h, g, w_in, name, comm=None):
    t = h.shape[0]

    def body(h_ref, g_ref, w_hbm, u_ref, pcg_ref, qkv_ref, w_v, sems):
        _load_resident(pl.program_id(0), [(w_hbm, w_v)], sems)
        u = _rms_fwd_tile(h_ref[...], g_ref[...]).astype(BF16)
        u_ref[...] = u
        for blk in range(N_DEV):
            for s in range(D // PROJ_SUB):
                lo, hi = s * PROJ_SUB, (s + 1) * PROJ_SUB
                p = _dot(u, w_v[blk, :, lo:hi])
                if blk < 3:
                    pcg_ref[:, blk * D + lo:blk * D + hi] = p
                elif blk < 6:
                    qkv_ref[:, (blk - 3) * D + lo:(blk - 3) * D + hi] = p.astype(BF16)
                else:
                    pcg_ref[:, (blk - 3) * D + lo:(blk - 3) * D + hi] = p

    return _call(
        body, (h, g, w_in), grid=(t // TM,), name=name, comm=comm,
        in_specs=[_row_spec(TM, D), _const_spec((1, D)), _ANY],
        out_specs=[_row_spec(TM, D), _row_spec(TM, PCG_W), _row_spec(TM, QKV_W)],
        out_shape=[jax.ShapeDtypeStruct((t, D), BF16), jax.ShapeDtypeStruct((t, PCG_W), F32),
                   jax.ShapeDtypeStruct((t, QKV_W), BF16)],
        scratch_shapes=[pltpu.VMEM((N_DEV, D, D), BF16), pltpu.SemaphoreType.DMA((1,))])


CONV_CW = 256


def _shift_down(v, k, rows):
    return jnp.where(rows >= k, pltpu.roll(v, k, 0), 0.0)


def _shift_up(v, k, rows, t):
    return jnp.where(rows < t - k, pltpu.roll(v, t - k, 0), 0.0)


def _col_spec(t, cw, off):
    return pl.BlockSpec((t, cw), lambda j: (0, j + off))


def _conv_fwd(pcg, conv_w, name):
    t = pcg.shape[0]
    nb = D // CONV_CW

    def body(cb_ref, cc_ref, cx_ref, w_ref, y_ref):
        rows = lax.broadcasted_iota(jnp.int32, (t, CONV_CW), 0)
        xc = cc_ref[...] * cx_ref[...]
        conv = (w_ref[0:1, :] * _shift_down(xc, 2, rows) + w_ref[1:2, :] * _shift_down(xc, 1, rows)
                + w_ref[2:3, :] * xc)
        y_ref[...] = (cb_ref[...] * conv).astype(BF16)

    return pl.pallas_call(
        body, grid=(nb,), name=name,
        in_specs=[_col_spec(t, CONV_CW, 0), _col_spec(t, CONV_CW, nb), _col_spec(t, CONV_CW, 2 * nb),
                  pl.BlockSpec((CONV_K, CONV_CW), lambda j: (0, j))],
        out_specs=_col_spec(t, CONV_CW, 0),
        out_shape=jax.ShapeDtypeStruct((t, D), BF16),
        compiler_params=_cparams(),
    )(pcg, pcg, pcg, conv_w)


def _conv_bwd(pcg, conv_w, dyc, name):
    t = pcg.shape[0]
    nb = D // CONV_CW

    def body(cb_ref, cc_ref, cx_ref, w_ref, dy_ref, dc_ref, dw_ref):
        rows = lax.broadcasted_iota(jnp.int32, (t, CONV_CW), 0)
        cc, cx = cc_ref[...], cx_ref[...]
        xc = cc * cx
        x1 = _shift_down(xc, 1, rows)
        x2 = _shift_down(xc, 2, rows)
        w0, w1, w2 = w_ref[0:1, :], w_ref[1:2, :], w_ref[2:3, :]
        conv = w0 * x2 + w1 * x1 + w2 * xc
        dy = dy_ref[...]
        dc_ref[0] = (dy * conv).astype(BF16)
        dconv = dy * cb_ref[...]
        dw_ref[...] = jnp.zeros((8, CONV_CW), F32)
        dw_ref[0:1, :] = jnp.sum(dconv * x2, axis=0, keepdims=True)
        dw_ref[1:2, :] = jnp.sum(dconv * x1, axis=0, keepdims=True)
        dw_ref[2:3, :] = jnp.sum(dconv * xc, axis=0, keepdims=True)
        dxc = w2 * dconv + w1 * _shift_up(dconv, 1, rows, t) + w0 * _shift_up(dconv, 2, rows, t)
        dc_ref[1] = (dxc * cx).astype(BF16)
        dc_ref[2] = (dxc * cc).astype(BF16)

    return pl.pallas_call(
        body, grid=(nb,), name=name,
        in_specs=[_col_spec(t, CONV_CW, 0), _col_spec(t, CONV_CW, nb), _col_spec(t, CONV_CW, 2 * nb),
                  pl.BlockSpec((CONV_K, CONV_CW), lambda j: (0, j)), _col_spec(t, CONV_CW, 0)],
        out_specs=[pl.BlockSpec((3, t, CONV_CW), lambda j: (0, 0, j)), pl.BlockSpec((8, CONV_CW), lambda j: (0, j))],
        out_shape=[jax.ShapeDtypeStruct((3, t, D), BF16), jax.ShapeDtypeStruct((8, D), F32)],
        compiler_params=_cparams(),
    )(pcg, pcg, pcg, conv_w, dyc)


def _tri2(cond):
    rr = lax.broadcasted_iota(jnp.int32, (2 * TK, TK), 0) & (TK - 1)
    cc = lax.broadcasted_iota(jnp.int32, (2 * TK, TK), 1)
    return cond(rr, cc).astype(BF16)


def _causal(shift):
    rr = lax.broadcasted_iota(jnp.int32, (TQ, TK), 0)
    cc = lax.broadcasted_iota(jnp.int32, (TQ, TK), 1)
    return cc + shift < rr


def _cumdot(v, tri2):
    hi = v.astype(BF16)
    lo = (v - hi.astype(F32)).astype(BF16)
    return _dot(jnp.concatenate([hi, lo], axis=1), tri2)


def _log_1m_beta(z):
    return -(jnp.maximum(z, 0.0) + jnp.log(1.0 + jnp.exp(-jnp.abs(z))))


def _sb_specs(t):
    g = SB_H // SB_HPS
    w = SB_HPS * SB_DH
    q_spec = pl.BlockSpec((TQ, w), lambda h, i: (i, h))
    k_spec = pl.BlockSpec((t, w), lambda h, i: (0, g + h))
    v_spec = pl.BlockSpec((t, w), lambda h, i: (0, 2 * g + h))
    ct_spec = pl.BlockSpec((SB_HPS, TQ, 1), lambda h, i: (h, i, 0))
    return g, w, q_spec, k_spec, v_spec, ct_spec


def _sb_fwd(qkv, name, comm=None):
    t = qkv.shape[0]
    scale = SB_DH ** -0.5
    g, w, q_spec, k_spec, v_spec, ct_spec = _sb_specs(t)

    def body(q_ref, k_ref, v_ref, y_ref, ct_ref):
        i = pl.program_id(1)
        later = _tri2(lambda j, s: j > s)
        n_diag = TQ // TK

        def block(j, carry, shift):
            off = pl.multiple_of(j * TK, TK)
            zs, ms = [], []
            for hd in range(SB_HPS):
                cols = slice(hd * SB_DH, (hd + 1) * SB_DH)
                z = _dot_nt(q_ref[:, cols], k_ref[pl.ds(off, TK), cols]) * scale
                m = _log_1m_beta(z)
                if shift is not None:
                    m = jnp.where(_causal(shift), m, 0.0)
                zs.append(z)
                ms.append(m)
            after = _cumdot(jnp.concatenate(ms, axis=0), later)
            out = []
            for hd in range(SB_HPS):
                acc, c_sum = carry[hd]
                cols = slice(hd * SB_DH, (hd + 1) * SB_DH)
                a = jnp.exp((ms[hd] + zs[hd]) + (c_sum + after[hd * TQ:(hd + 1) * TQ]))
                if shift is not None:
                    a = jnp.where(_causal(shift), a, 0.0)
                out.append((acc + _dot(a.astype(BF16), v_ref[pl.ds(off, TK), cols]),
                            c_sum + jnp.sum(ms[hd], axis=1, keepdims=True)))
            return tuple(out)

        carry = tuple((jnp.zeros((TQ, SB_DH), F32), jnp.zeros((TQ, 1), F32)) for _ in range(SB_HPS))
        for d in reversed(range(n_diag)):
            carry = block(i * n_diag + d, carry, d * TK)
        carry = lax.fori_loop(0, i * n_diag, lambda jj, c: block(i * n_diag - 1 - jj, c, None), carry)
        for hd in range(SB_HPS):
            y_ref[:, hd * SB_DH:(hd + 1) * SB_DH] = carry[hd][0].astype(BF16)
            ct_ref[hd] = carry[hd][1]

    return _call(
        body, (qkv, qkv, qkv), grid=(g, t // TQ), name=name, comm=comm,
        in_specs=[q_spec, k_spec, v_spec],
        out_specs=[q_spec, ct_spec],
        out_shape=[jax.ShapeDtypeStruct((t, D), BF16), jax.ShapeDtypeStruct((SB_H, t, 1), F32)],
        scratch_shapes=[])


def _sb_bwd(qkv, dy, ctot, name, comm=None):
    t = qkv.shape[0]
    scale = SB_DH ** -0.5
    g, w, q_spec, k_spec, v_spec, ct_spec = _sb_specs(t)
    acc_spec = pl.BlockSpec((2, t, w), lambda h, i: (0, 0, h))

    def body(q_ref, k_ref, v_ref, dy_ref, ct_ref, dq_ref, dkv_ref):
        i = pl.program_id(1)

        @pl.when(i == 0)
        def _():
            dkv_ref[...] = jnp.zeros_like(dkv_ref)

        upto = _tri2(lambda j, s: j <= s)
        n_diag = TQ // TK

        def block(j, carry, shift):
            off = pl.multiple_of(j * TK, TK)
            zs, ms = [], []
            for hd in range(SB_HPS):
                cols = slice(hd * SB_DH, (hd + 1) * SB_DH)
                z = _dot_nt(q_ref[:, cols], k_ref[pl.ds(off, TK), cols]) * scale
                m = _log_1m_beta(z)
                if shift is not None:
                    m = jnp.where(_causal(shift), m, 0.0)
                zs.append(z)
                ms.append(m)
            m_upto = _cumdot(jnp.concatenate(ms, axis=0), upto)
            ls, a_s, es = [], [], []
            for hd in range(SB_HPS):
                cols = slice(hd * SB_DH, (hd + 1) * SB_DH)
                l = ms[hd] + zs[hd]
                a = jnp.exp(l + ((ct_ref[hd] - carry[hd][1]) - m_upto[hd * TQ:(hd + 1) * TQ]))
                if shift is not None:
                    a = jnp.where(_causal(shift), a, 0.0)
                ls.append(l)
                a_s.append(a)
                es.append(_dot_nt(dy_ref[:, cols], v_ref[pl.ds(off, TK), cols]) * a)
            e_upto = _dot(jnp.concatenate(es, axis=0).astype(BF16), upto[:TK])
            out = []
            for hd in range(SB_HPS):
                dq, p_sum, e_sum = carry[hd]
                cols = slice(hd * SB_DH, (hd + 1) * SB_DH)
                e = es[hd]
                dz = e - jnp.exp(ls[hd]) * (e_sum + e_upto[hd * TQ:(hd + 1) * TQ])
                if shift is not None:
                    dz = jnp.where(_causal(shift), dz, 0.0)
                dzs = (dz * scale).astype(BF16)
                dkv_ref[0, pl.ds(off, TK), cols] += _dot_tn(dzs, q_ref[:, cols])
                dkv_ref[1, pl.ds(off, TK), cols] += _dot_tn(a_s[hd].astype(BF16), dy_ref[:, cols])
                out.append((dq + _dot(dzs, k_ref[pl.ds(off, TK), cols]),
                            p_sum + jnp.sum(ms[hd], axis=1, keepdims=True),
                            e_sum + jnp.sum(e, axis=1, keepdims=True)))
            return tuple(out)

        zero = jnp.zeros((TQ, 1), F32)
        init = tuple((jnp.zeros((TQ, SB_DH), F32), zero, zero) for _ in range(SB_HPS))
        carry = lax.fori_loop(0, i * n_diag, lambda j, c: block(j, c, None), init)
        for d in range(n_diag):
            carry = block(i * n_diag + d, carry, d * TK)
        for hd in range(SB_HPS):
            dq_ref[:, hd * SB_DH:(hd + 1) * SB_DH] = carry[hd][0].astype(BF16)

    return _call(
        body, (qkv, qkv, qkv, dy, ctot), grid=(g, t // TQ), name=name, comm=comm,
        in_specs=[q_spec, k_spec, v_spec, q_spec, ct_spec],
        out_specs=[q_spec, acc_spec],
        out_shape=[jax.ShapeDtypeStruct((t, D), BF16), jax.ShapeDtypeStruct((2, t, D), F32)],
        scratch_shapes=[])


def _gate_specs():
    return [pl.BlockSpec((TM, D), lambda i: (i, 3)), pl.BlockSpec((TM, D), lambda i: (i, 4))]


def _mix_pairs(mix_hbm, dsts):
    pairs = []
    for index, dst in enumerate(dsts):
        pairs += _square_pairs(mix_hbm, index, dst)
    return pairs


def _mix_out_fwd(yc, ysb, pcg, b_gate, h, w_mix, name):
    t = h.shape[0]

    def body(yc_ref, ysb_ref, gc_ref, gs_ref, b_ref, h_ref, mix_hbm,
             a_ref, b_out_ref, mg_ref, h2_ref, wc_v, wa_v, wo_v, sems):
        _load_resident(pl.program_id(0), _mix_pairs(mix_hbm, (wc_v, wa_v, wo_v)), sems)
        a = _dot(yc_ref[...], wc_v[...])
        b = _dot(ysb_ref[...], wa_v[...])
        merged = (_sigmoid(gc_ref[...] + b_ref[:, :D]) * a + _sigmoid(gs_ref[...] + b_ref[:, D:]) * b).astype(BF16)
        a_ref[...] = a
        b_out_ref[...] = b
        mg_ref[...] = merged
        h2_ref[...] = h_ref[...] + _dot(merged, wo_v[...])

    return pl.pallas_call(
        body, grid=(t // TM,), name=name,
        in_specs=[_row_spec(TM, D), _row_spec(TM, D)] + _gate_specs()
                 + [_const_spec((1, 2 * D)), _row_spec(TM, D), _ANY],
        out_specs=[_row_spec(TM, D)] * 4,
        out_shape=[jax.ShapeDtypeStruct((t, D), F32), jax.ShapeDtypeStruct((t, D), F32),
                   jax.ShapeDtypeStruct((t, D), BF16), jax.ShapeDtypeStruct((t, D), F32)],
        scratch_shapes=[pltpu.VMEM((D, D), BF16)] * 3 + [pltpu.SemaphoreType.DMA((3 * N_DEV,))],
        compiler_params=_cparams(),
    )(yc, ysb, pcg, pcg, b_gate, h, w_mix)


def _mix_out_bwd(dh2, a, b, pcg, b_gate, w_mix, name, comm=None):
    t = dh2.shape[0]

    def body(dh_ref, a_ref, b_ref, gc_ref, gs_ref, bias_ref, mix_hbm,
             dhb_ref, da_ref, db_ref, dgp_ref, dyc_ref, dysb_ref, dbias_ref, wc_v, wa_v, wo_v, sems):
        step = pl.program_id(0)
        _load_resident(step, _mix_pairs(mix_hbm, (wc_v, wa_v, wo_v)), sems)
        dhb = dh_ref[...].astype(BF16)
        dhb_ref[...] = dhb
        dm = _dot_nt(dhb, wo_v[...])
        gc = _sigmoid(gc_ref[...] + bias_ref[:, :D])
        gs = _sigmoid(gs_ref[...] + bias_ref[:, D:])
        da = (dm * gc).astype(BF16)
        db = (dm * gs).astype(BF16)
        da_ref[...] = da
        db_ref[...] = db
        dgc = dm * a_ref[...] * (gc * (1.0 - gc))
        dgs = dm * b_ref[...] * (gs * (1.0 - gs))
        dgp_ref[0] = dgc.astype(BF16)
        dgp_ref[1] = dgs.astype(BF16)
        _accumulate(dbias_ref.at[:, :D], step, jnp.sum(dgc, axis=0, keepdims=True))
        _accumulate(dbias_ref.at[:, D:], step, jnp.sum(dgs, axis=0, keepdims=True))
        dyc_ref[...] = _dot_nt(da, wc_v[...])
        dysb_ref[...] = _dot_nt(db, wa_v[...]).astype(BF16)

    return _call(
        body, (dh2, a, b, pcg, pcg, b_gate, w_mix), grid=(t // TM,), name=name, comm=comm,
        in_specs=[_row_spec(TM, D)] * 3 + _gate_specs() + [_const_spec((1, 2 * D)), _ANY],
        out_specs=[_row_spec(TM, D)] * 3 + [_blk_row_spec(2, TM, D), _row_spec(TM, D), _row_spec(TM, D),
                                            _const_spec((1, 2 * D))],
        out_shape=[jax.ShapeDtypeStruct((t, D), BF16)] * 3
                  + [jax.ShapeDtypeStruct((2, t, D), BF16), jax.ShapeDtypeStruct((t, D), F32),
                     jax.ShapeDtypeStruct((t, D), BF16), jax.ShapeDtypeStruct((1, 2 * D), F32)],
        scratch_shapes=[pltpu.VMEM((D, D), BF16)] * 3 + [pltpu.SemaphoreType.DMA((3 * N_DEV,))])


def _inproj_bwd(dconv, dq, dkv, dgp, w_in, h, g, dh_res, name, comm=None):
    t = h.shape[0]

    def body(dc_ref, dq_ref, dkv_ref, dgp_ref, w_hbm, h_ref, g_ref, dres_ref, dh_ref, dg_ref, w_v, sems):
        step = pl.program_id(0)
        _load_resident(step, [(w_hbm, w_v)], sems)
        du = _dot_nt(dq_ref[...], w_v[3])
        for k in range(3):
            du = du + _dot_nt(dc_ref[k], w_v[k])
        for k in range(2):
            du = du + _dot_nt(dkv_ref[k].astype(BF16), w_v[4 + k]) + _dot_nt(dgp_ref[k], w_v[6 + k])
        dx, dg = _rms_bwd_tile(h_ref[...], g_ref[...], du)
        dh_ref[...] = dres_ref[...] + dx
        _accumulate(dg_ref, step, dg)

    return _call(
        body, (dconv, dq, dkv, dgp, w_in, h, g, dh_res), grid=(t // TM,), name=name, comm=comm,
        in_specs=[_blk_row_spec(3, TM, D), _row_spec(TM, D), _blk_row_spec(2, TM, D), _blk_row_spec(2, TM, D), _ANY,
                  _row_spec(TM, D), _const_spec((1, D)), _row_spec(TM, D)],
        out_specs=[_row_spec(TM, D), _const_spec((1, D))],
        out_shape=[jax.ShapeDtypeStruct((t, D), F32), jax.ShapeDtypeStruct((1, D), F32)],
        scratch_shapes=[pltpu.VMEM((N_DEV, D, D), BF16), pltpu.SemaphoreType.DMA((1,))])


def _memkv_fwd(mem, g, w_ckv, name):
    m = mem.shape[0]

    def body(mem_ref, g_ref, w_ref, mn_ref, kv_ref):
        mn = _rms_fwd_tile(mem_ref[...], g_ref[...]).astype(BF16)
        mn_ref[...] = mn
        for j in range(N_DEV):
            kv_ref[j] = _dot(mn, w_ref[j]).astype(BF16)

    return pl.pallas_call(
        body, grid=(1,), name=name,
        in_specs=[_const_spec((m, D)), _const_spec((1, D)), _const_spec((N_DEV, D, X_DH))],
        out_specs=[_const_spec((m, D)), _const_spec((N_DEV, m, X_DH))],
        out_shape=[jax.ShapeDtypeStruct((m, D), BF16), jax.ShapeDtypeStruct((N_DEV, m, X_DH), BF16)],
        compiler_params=_cparams(),
    )(mem, g, w_ckv)


def _memkv_bwd(dkv, mem, g, w_ckv, name):
    m = mem.shape[0]

    def body(dkv_ref, mem_ref, g_ref, w_ref, dg_ref):
        dmn = jnp.zeros((m, D), F32)
        for j in range(N_DEV):
            dmn = dmn + _dot_nt(dkv_ref[j].astype(BF16), w_ref[j])
        _, dg = _rms_bwd_tile(mem_ref[...], g_ref[...], dmn)
        dg_ref[...] = dg

    return pl.pallas_call(
        body, grid=(1,), name=name,
        in_specs=[_const_spec((N_DEV, m, X_DH)), _const_spec((m, D)), _const_spec((1, D)),
                  _const_spec((N_DEV, D, X_DH))],
        out_specs=_const_spec((1, D)),
        out_shape=jax.ShapeDtypeStruct((1, D), F32),
        compiler_params=_cparams(),
    )(dkv, mem, g, w_ckv)


def _softmax_rows(s):
    e = jnp.exp(s - jnp.max(s, axis=-1, keepdims=True))
    return e / jnp.sum(e, axis=-1, keepdims=True)


def _cross_pairs(cross_hbm, wq_v, wo_v):
    return _square_pairs(cross_hbm, 0, wq_v) + _square_pairs(cross_hbm, 1, wo_v)


def _cross_fwd(h, g, kv, w_cross, name):
    t = h.shape[0]
    m = kv.shape[1]
    scale = X_DH ** -0.5

    def body(h_ref, g_ref, kv_ref, cross_hbm, hn_ref, qx_ref, o_ref, h3_ref, wq_v, wo_v, sems):
        _load_resident(pl.program_id(0), _cross_pairs(cross_hbm, wq_v, wo_v), sems)
        ht = h_ref[...]
        hn = _rms_fwd_tile(ht, g_ref[...]).astype(BF16)
        hn_ref[...] = hn
        qx = _dot(hn, wq_v[...]).astype(BF16)
        qx_ref[...] = qx
        for hd in range(X_H):
            lo, hi = hd * X_DH, (hd + 1) * X_DH
            p = _softmax_rows(_dot_nt(qx[:, lo:hi], kv_ref[hd]) * scale)
            o_ref[:, lo:hi] = _dot(p.astype(BF16), kv_ref[X_H + hd]).astype(BF16)
        h3_ref[...] = ht + _dot(o_ref[...], wo_v[...])

    return pl.pallas_call(
        body, grid=(t // TM,), name=name,
        in_specs=[_row_spec(TM, D), _const_spec((1, D)), _const_spec((N_DEV, m, X_DH)), _ANY],
        out_specs=[_row_spec(TM, D)] * 4,
        out_shape=[jax.ShapeDtypeStruct((t, D), BF16)] * 3 + [jax.ShapeDtypeStruct((t, D), F32)],
        scratch_shapes=[pltpu.VMEM((D, D), BF16)] * 2 + [pltpu.SemaphoreType.DMA((2 * N_DEV,))],
        compiler_params=_cparams(),
    )(h, g, kv, w_cross)


def _cross_bwd(dh3, h, g, qx, kv, w_cross, name, comm=None):
    t = h.shape[0]
    m = kv.shape[1]
    scale = X_DH ** -0.5

    def body(dh_ref, h_ref, g_ref, qx_ref, kv_ref, cross_hbm,
             dhb_ref, dqx_ref, dkv_ref, dh2_ref, dg_ref, wq_v, wo_v, sems):
        step = pl.program_id(0)
        _load_resident(step, _cross_pairs(cross_hbm, wq_v, wo_v), sems)

        @pl.when(step == 0)
        def _():
            dkv_ref[...] = jnp.zeros_like(dkv_ref)

        dht = dh_ref[...]
        dhb = dht.astype(BF16)
        dhb_ref[...] = dhb
        do = _dot_nt(dhb, wo_v[...]).astype(BF16)
        for hd in range(X_H):
            lo, hi = hd * X_DH, (hd + 1) * X_DH
            qh = qx_ref[:, lo:hi]
            kh = kv_ref[hd]
            p = _softmax_rows(_dot_nt(qh, kh) * scale)
            doh = do[:, lo:hi]
            dp = _dot_nt(doh, kv_ref[X_H + hd])
            ds = (p * (dp - jnp.sum(dp * p, axis=-1, keepdims=True)) * scale).astype(BF16)
            dqx_ref[:, lo:hi] = _dot(ds, kh).astype(BF16)
            dkv_ref[hd] += _dot_tn(ds, qh)
            dkv_ref[X_H + hd] += _dot_tn(p.astype(BF16), doh)
        dhn = _dot_nt(dqx_ref[...], wq_v[...])
        dx, dg = _rms_bwd_tile(h_ref[...], g_ref[...], dhn)
        dh2_ref[...] = dht + dx
        _accumulate(dg_ref, step, dg)

    return _call(
        body, (dh3, h, g, qx, kv, w_cross), grid=(t // TM,), name=name, comm=comm,
        in_specs=[_row_spec(TM, D), _row_spec(TM, D), _const_spec((1, D)), _row_spec(TM, D),
                  _const_spec((N_DEV, m, X_DH)), _ANY],
        out_specs=[_row_spec(TM, D), _row_spec(TM, D), _const_spec((N_DEV, m, X_DH)), _row_spec(TM, D),
                   _const_spec((1, D))],
        out_shape=[jax.ShapeDtypeStruct((t, D), BF16), jax.ShapeDtypeStruct((t, D), BF16),
                   jax.ShapeDtypeStruct((N_DEV, m, X_DH), F32), jax.ShapeDtypeStruct((t, D), F32),
                   jax.ShapeDtypeStruct((1, D), F32)],
        scratch_shapes=[pltpu.VMEM((D, D), BF16)] * 2 + [pltpu.SemaphoreType.DMA((2 * N_DEV,))])


def _loss_bwd(h, g, target, name):
    t = h.shape[0]

    def body(h_ref, g_ref, t_ref, loss_ref, dh_ref, dg_ref):
        step = pl.program_id(0)
        ht = h_ref[...]
        gain = g_ref[...]
        diff = _rms_fwd_tile(ht, gain) - t_ref[...]
        part = 0.5 * jnp.sum(jnp.sum(diff * diff, axis=-1, keepdims=True) / D, axis=0, keepdims=True)
        dx, dg = _rms_bwd_tile(ht, gain, diff / D)
        dh_ref[...] = dx
        _accumulate(loss_ref, step, jnp.broadcast_to(part, (8, 128)))
        _accumulate(dg_ref, step, dg)

    return pl.pallas_call(
        body, grid=(t // TM,), name=name,
        in_specs=[_row_spec(TM, D), _const_spec((1, D)), _row_spec(TM, D)],
        out_specs=[_const_spec((8, 128)), _row_spec(TM, D), _const_spec((1, D))],
        out_shape=[jax.ShapeDtypeStruct((8, 128), F32), jax.ShapeDtypeStruct((t, D), F32),
                   jax.ShapeDtypeStruct((1, D), F32)],
        compiler_params=_cparams(),
    )(h, g, target)


def _adamw(w, parts, m, v, name, row_block=0):
    r, c = w.shape
    n = parts.shape[0]
    tr = _pick_tile(r, (256, 352, 128))
    off = row_block * (r // tr)

    def body(w_ref, p_ref, m_ref, v_ref, g_ref, d_ref, nm_ref, nv_ref):
        gt = p_ref[0].astype(F32)
        for k in range(1, n):
            gt = gt + p_ref[k].astype(F32)
        g_ref[...] = gt
        nm = ADAM_B1 * m_ref[...] + (1.0 - ADAM_B1) * gt
        nv = ADAM_B2 * v_ref[...] + (1.0 - ADAM_B2) * jnp.square(gt)
        m_hat = nm / (1.0 - ADAM_B1 ** ADAM_STEP)
        v_hat = nv / (1.0 - ADAM_B2 ** ADAM_STEP)
        d_ref[...] = -ADAM_LR * (m_hat / (jnp.sqrt(v_hat) + ADAM_EPS) + ADAM_WD * w_ref[...])
        nm_ref[...] = nm
        nv_ref[...] = nv

    spec = _row_spec(tr, c)
    return pl.pallas_call(
        body, grid=(r // tr,), name=name,
        in_specs=[spec, pl.BlockSpec((n, tr, c), lambda i: (0, i + off, 0)), spec, spec], out_specs=[spec] * 4,
        out_shape=[jax.ShapeDtypeStruct((r, c), F32)] * 4,
        compiler_params=_cparams(),
    )(w, parts, m, v)


def _mesh_pos():
    return lax.axis_index("x"), lax.axis_index("y"), lax.axis_index("c")


def _no_round(in_refs, out_refs, sems):
    pass


def _run_exchange(comm, name):
    c_in, c_out = len(comm.inputs), len(comm.out_shapes)

    def body(*refs):
        cins, couts, sems = refs[:c_in], refs[c_in:c_in + c_out], refs[c_in + c_out:]
        comm.start(cins, couts, sems)
        comm.middle(cins, couts, sems)
        comm.finish(cins, couts, sems)

    return list(pl.pallas_call(
        body, name=name, out_shape=list(comm.out_shapes),
        in_specs=[_ANY] * c_in, out_specs=[_ANY] * c_out, scratch_shapes=list(comm.sem_shapes),
    )(*comm.inputs))


def _gather_exchange(shards):
    n_arr = len(shards)

    def plan(x_refs, out_refs, sems):
        send_sems, recv_sems, local_sems = sems
        x, y, c = _mesh_pos()
        me, sibling = (x, y, c), (x, y, 1 - c)
        xn, yn, diag = (1 - x, y), (x, 1 - y), (1 - x, 1 - y)

        def slot(a, px, py, pc, half=None):
            ref = out_refs[a].at[4 * px + 2 * py + pc]
            if half is None:
                return ref
            rows = shards[a].shape[0] // 2
            return ref.at[half * rows:(half + 1) * rows]

        def copy(a, k, block, to, half=None, src=None):
            dst = slot(a, *block, half)
            return pltpu.make_async_remote_copy(
                src_ref=dst if src is None else src, dst_ref=dst,
                send_sem=send_sems.at[a, k], recv_sem=recv_sems.at[a, k],
                device_id=to, device_id_type=pl.DeviceIdType.MESH)

        return types.SimpleNamespace(
            me=me, sibling=sibling, xn=xn, yn=yn, diag=diag, c=c, copy=copy,
            mine=[pltpu.make_async_copy(x_refs[a], slot(a, *me), local_sems.at[a]) for a in range(n_arr)],
            first=[cp for a in range(n_arr) for cp in (
                copy(a, 0, me, sibling, src=x_refs[a]), copy(a, 1, me, (*xn, c), src=x_refs[a]),
                copy(a, 2, me, (*yn, c), src=x_refs[a]))],
            second=lambda a: (copy(a, 3, (*xn, c), (*yn, c), half=0), copy(a, 5, (*xn, c), sibling),
                              copy(a, 4, (*yn, c), (*xn, c), half=1), copy(a, 6, (*yn, c), sibling)),
            third=lambda a: (copy(a, 7, (*diag, c), sibling, half=0), copy(a, 8, (*diag, c), sibling, half=1)))

    def start(x_refs, out_refs, sems):
        p = plan(x_refs, out_refs, sems)
        for cp in p.mine + p.first:
            cp.start()

    def middle(x_refs, out_refs, sems):
        p = plan(x_refs, out_refs, sems)
        for a in range(n_arr):
            to_yn, x_to_sib, to_xn, y_to_sib = p.second(a)
            p.copy(a, 1, (*p.xn, p.c), p.me).wait_recv()
            to_yn.start()
            x_to_sib.start()
            p.copy(a, 2, (*p.yn, p.c), p.me).wait_recv()
            to_xn.start()
            y_to_sib.start()

    def finish(x_refs, out_refs, sems):
        p = plan(x_refs, out_refs, sems)
        for a in range(n_arr):
            half0_to_sib, half1_to_sib = p.third(a)
            p.copy(a, 3, (*p.diag, p.c), p.me, half=0).wait_recv()
            half0_to_sib.start()
            p.copy(a, 4, (*p.diag, p.c), p.me, half=1).wait_recv()
            half1_to_sib.start()
        other = 1 - p.c
        for a in range(n_arr):
            p.copy(a, 0, p.sibling, p.me).wait_recv()
            p.copy(a, 5, (*p.xn, other), p.me).wait_recv()
            p.copy(a, 6, (*p.yn, other), p.me).wait_recv()
            p.copy(a, 7, (*p.diag, other), p.me, half=0).wait_recv()
            p.copy(a, 8, (*p.diag, other), p.me, half=1).wait_recv()
        for cp in p.first:
            cp.wait_send()
        for a in range(n_arr):
            for cp in p.second(a) + p.third(a):
                cp.wait_send()
        for cp in p.mine:
            cp.wait()

    return types.SimpleNamespace(
        inputs=list(shards), start=start, middle=middle, finish=finish,
        out_shapes=[jax.ShapeDtypeStruct((N_DEV,) + s.shape, s.dtype) for s in shards],
        sem_shapes=[pltpu.SemaphoreType.DMA((n_arr, 9)), pltpu.SemaphoreType.DMA((n_arr, 9)),
                    pltpu.SemaphoreType.DMA((n_arr,))])


def _pair_exchange(grads):
    n_arr = len(grads)

    def plan(g_refs, land_refs, sems):
        send_sems, recv_sems = sems
        x, y, c = _mesh_pos()
        return [pltpu.make_async_remote_copy(
            src_ref=g_refs[a].at[2 * k + 1 - c], dst_ref=land_refs[a].at[k],
            send_sem=send_sems.at[a, k], recv_sem=recv_sems.at[a, k],
            device_id=(x, y, 1 - c), device_id_type=pl.DeviceIdType.MESH)
            for a in range(n_arr) for k in range(N_CHIP)]

    def start(g_refs, land_refs, sems):
        for cp in plan(g_refs, land_refs, sems):
            cp.start()

    def finish(g_refs, land_refs, sems):
        for cp in plan(g_refs, land_refs, sems):
            cp.wait()

    return types.SimpleNamespace(
        inputs=list(grads), start=start, middle=_no_round, finish=finish,
        out_shapes=[jax.ShapeDtypeStruct((N_CHIP,) + g.shape[1:], g.dtype) for g in grads],
        sem_shapes=[pltpu.SemaphoreType.DMA((n_arr, N_CHIP)), pltpu.SemaphoreType.DMA((n_arr, N_CHIP))])


def _chip_exchange(parts):
    n_arr = len(parts)

    def plan(p_refs, land_refs, sems):
        send_sems, recv_sems, local_sems = sems
        x, y, c = _mesh_pos()
        my_chip = 2 * x + y
        chips = [(1 - x, y), (x, 1 - y), (1 - x, 1 - y)]
        local = [pltpu.make_async_copy(p_refs[a].at[my_chip], land_refs[a].at[my_chip], local_sems.at[a])
                 for a in range(n_arr)]

        def copy(a, k, src_slot, dst_slot, px, py):
            return pltpu.make_async_remote_copy(
                src_ref=p_refs[a].at[src_slot], dst_ref=land_refs[a].at[dst_slot],
                send_sem=send_sems.at[a, k], recv_sem=recv_sems.at[a, k],
                device_id=(px, py, c), device_id_type=pl.DeviceIdType.MESH)

        sends = [copy(a, k, 2 * px + py, my_chip, px, py) for a in range(n_arr) for k, (px, py) in enumerate(chips)]
        arrivals = [copy(a, k, my_chip, 2 * px + py, px, py) for a in range(n_arr)
                    for k, (px, py) in enumerate(chips)]
        return local, sends, arrivals

    def start(p_refs, land_refs, sems):
        local, sends, _ = plan(p_refs, land_refs, sems)
        for cp in local + sends:
            cp.start()

    def finish(p_refs, land_refs, sems):
        local, sends, arrivals = plan(p_refs, land_refs, sems)
        for cp in arrivals:
            cp.wait_recv()
        for cp in sends:
            cp.wait_send()
        for cp in local:
            cp.wait()

    return types.SimpleNamespace(
        inputs=list(parts), start=start, middle=_no_round, finish=finish,
        out_shapes=[jax.ShapeDtypeStruct(p.shape, p.dtype) for p in parts],
        sem_shapes=[pltpu.SemaphoreType.DMA((n_arr, 3)), pltpu.SemaphoreType.DMA((n_arr, 3)),
                    pltpu.SemaphoreType.DMA((n_arr,))])


def _row_tile(r, cap=640):
    best = None
    for cand in range(16, min(r, cap) + 1, 16):
        if r % cand == 0:
            best = cand
    return best if best is not None else r


def _pair_sum(g, landed, core, name):
    _, r, c_dim = g.shape
    tr = _row_tile(r)

    def body(core_ref, mine_ref, theirs_ref, o_ref):
        o_ref[0] = (mine_ref[0].astype(F32) + theirs_ref[0].astype(F32)).astype(o_ref.dtype)

    return pl.pallas_call(
        body, name=name,
        grid_spec=pltpu.PrefetchScalarGridSpec(
            num_scalar_prefetch=1, grid=(N_CHIP, r // tr),
            in_specs=[pl.BlockSpec((1, tr, c_dim), lambda k, i, core_ref: (2 * k + core_ref[0], i, 0)),
                      pl.BlockSpec((1, tr, c_dim), lambda k, i, core_ref: (k, i, 0))],
            out_specs=pl.BlockSpec((1, tr, c_dim), lambda k, i, core_ref: (k, i, 0))),
        out_shape=jax.ShapeDtypeStruct((N_CHIP, r, c_dim), g.dtype),
        compiler_params=_cparams(2),
    )(core, g, landed)


def _sum_slots(parts, name):
    n, r, c_dim = parts.shape
    tr = _row_tile(r)

    def body(p_ref, o_ref):
        acc = p_ref[0].astype(F32)
        for k in range(1, n):
            acc = acc + p_ref[k].astype(F32)
        o_ref[...] = acc

    return pl.pallas_call(
        body, grid=(r // tr,), name=name,
        in_specs=[pl.BlockSpec((n, tr, c_dim), lambda i: (0, i, 0))],
        out_specs=_row_spec(tr, c_dim),
        out_shape=jax.ShapeDtypeStruct((r, c_dim), F32),
        compiler_params=_cparams(),
    )(parts)


GAINS = ("g_ffn1", "g_mix", "g_cross", "g_mem", "g_ffn2", "g_final")
SMALL = GAINS + ("b_gate", "conv_w")
SMALL_R = 16
WEIGHT_ORDER = ("g_ffn1", "w_ffn1_gu", "w_ffn1_down", "g_mix", "w_in", "b_gate", "conv_w", "w_conv_out",
                "w_attn_out", "w_o", "g_cross", "g_mem", "w_cq", "w_ckv", "w_co", "g_ffn2", "w_ffn2_gu",
                "w_ffn2_down", "g_final")
GU_NAMES = ("w_ffn1_gu", "w_ffn2_gu")


def _pack_small(vals, conv_rows):
    rows = [vals[n].reshape(1, D) for n in GAINS] + [vals["b_gate"].reshape(2, D), conv_rows.reshape(CONV_K, D)]
    used = len(GAINS) + 2 + CONV_K
    return jnp.concatenate(rows + [jnp.zeros((SMALL_R - used, D), F32)], axis=0)


def _unpack_small(buf):
    out = {n: buf[k] for k, n in enumerate(GAINS)}
    out["b_gate"] = buf[6:8].reshape(2 * D)
    out["conv_w"] = buf[8:8 + CONV_K]
    return out


def _exchange_shards(wts):
    out = {n: jnp.pad(wts[n].T.astype(BF16), ((0, FF_PAD - FF_BLK), (0, 0))) for n in GU_NAMES}
    for n in ("w_ckv", "w_in", "w_ffn1_down", "w_ffn2_down"):
        out[n] = wts[n].astype(BF16)
    out["mix"] = jnp.concatenate([wts[n].astype(BF16) for n in MIX_MATS], axis=0)
    out["cross"] = jnp.concatenate([wts[n].astype(BF16) for n in CROSS_MATS], axis=0)
    return out


def _by_device(dw):
    return dw.reshape(N_DEV, SQ_ROWS, D)


def _reduce_group(grads, landed, core, names):
    return [_pair_sum(g, l, core, "grads_pair_sum_" + n) for g, l, n in zip(grads, landed, names)]


def _step(x, mem, target, sh, conv_pad, gains, b_gate, core):
    wg1, wd1, conv_all = _run_exchange(_gather_exchange([sh["w_ffn1_gu"], sh["w_ffn1_down"], conv_pad]), "gather_ffn1")
    conv_w = conv_all[:, :CONV_K, :].transpose(1, 0, 2).reshape(CONV_K, D)
    (n1, gate1, up1, act1, h1), (w_in,) = _ffn_fwd(
        x, gains["g_ffn1"], wg1, wd1, "ffn1_fwd", comm=_gather_exchange([sh["w_in"]]))
    (u, pcg, qkv), (w_mix, w_cross, w_ckv) = _inproj_fwd(
        h1, gains["g_mix"], w_in, "inproj_fwd", comm=_gather_exchange([sh["mix"], sh["cross"], sh["w_ckv"]]))
    yc = _conv_fwd(pcg, conv_w, "conv_fwd")
    (ysb, ctot), (wg2, wd2) = _sb_fwd(
        qkv, "sb_fwd", comm=_gather_exchange([sh["w_ffn2_gu"], sh["w_ffn2_down"]]))
    a_mix, b_mix, merged, h2 = _mix_out_fwd(yc, ysb, pcg, b_gate, h1, w_mix, "mix_out_fwd")
    mn, kv = _memkv_fwd(mem, gains["g_mem"], w_ckv, "memkv_fwd")
    hn, qx, o_x, h3 = _cross_fwd(h2, gains["g_cross"], kv, w_cross, "cross_fwd")
    (n4, gate2, up2, act2, h4), _ = _ffn_fwd(h3, gains["g_ffn2"], wg2, wd2, "ffn2_fwd")
    loss, dh4, dg_final = _loss_bwd(h4, gains["g_final"], target, "loss_bwd")

    gs = {"g_final": dg_final}
    (dgu2, dh4b, dh3, gs["g_ffn2"]), _ = _ffn_bwd(dh4, h3, gains["g_ffn2"], gate2, up2, wg2, wd2, "ffn2_bwd")
    grads_a = [_mm_tn_rows(dgu2, n4, FF_PAD, "dw_ffn2_gu"),
               _mm_tn_rows(act2, dh4b, FF_BLK, "dw_ffn2_down").reshape(N_DEV, DOWN_ROWS, D)]
    names_a = ["w_ffn2_gu", "w_ffn2_down"]
    (dh3b, dqx, dkv, dh2, gs["g_cross"]), landed_a = _cross_bwd(
        dh3, h2, gains["g_cross"], qx, kv, w_cross, "cross_bwd", comm=_pair_exchange(grads_a))
    sums_a = _reduce_group(grads_a, landed_a, core, names_a)
    grads_b = [_mm_tn_cols(mn, dkv, "dw_ckv"),
               jnp.concatenate([_by_device(_mm_tn(hn, dqx, "dw_cq")), _by_device(_mm_tn(o_x, dh3b, "dw_co"))], axis=1)]
    names_b = ["w_ckv", "cross"]
    gs["g_mem"] = _memkv_bwd(dkv, mem, gains["g_mem"], w_ckv, "memkv_bwd")
    (dh2b, da_mix, db_mix, dgp, dyc, dysb, gs["b_gate"]), landed_b = _mix_out_bwd(
        dh2, a_mix, b_mix, pcg, b_gate, w_mix, "mix_out_bwd", comm=_pair_exchange(grads_b))
    sums_b = _reduce_group(grads_b, landed_b, core, names_b)
    grads_c = [jnp.concatenate([_by_device(_mm_tn(yc, da_mix, "dw_conv_out")),
                                _by_device(_mm_tn(ysb, db_mix, "dw_attn_out")),
                                _by_device(_mm_tn(merged, dh2b, "dw_o"))], axis=1)]
    landed_c = _run_exchange(_pair_exchange(grads_c), "grads_to_sibling_mix")
    sums_c = _reduce_group(grads_c, landed_c, core, ["mix"])
    (dq, dkv_sb), parts_abc = _sb_bwd(qkv, dysb, ctot, "sb_bwd", comm=_chip_exchange(sums_a + sums_b + sums_c))
    dconv, gs["conv_w"] = _conv_bwd(pcg, conv_w, dyc, "conv_bwd")
    grads_d = [jnp.concatenate(
        [_mm_tn_cols(u, dconv, "dw_in_conv"), _mm_tn(u, dq, "dw_in_q")[None], _mm_tn_cols(u, dkv_sb, "dw_in_kv"),
         _mm_tn_cols(u, dgp, "dw_in_gates")], axis=0)]
    (dh1, gs["g_mix"]), landed_d = _inproj_bwd(dconv, dq, dkv_sb, dgp, w_in, h1, gains["g_mix"], dh2, "inproj_bwd",
                                               comm=_pair_exchange(grads_d))
    sums_d = _reduce_group(grads_d, landed_d, core, ["w_in"])
    (dgu1, dh1b, dx, gs["g_ffn1"]), parts_d = _ffn_bwd(dh1, x, gains["g_ffn1"], gate1, up1, wg1, wd1, "ffn1_bwd",
                                                       comm=_chip_exchange(sums_d))
    grads_e = [_mm_tn_rows(dgu1, n1, FF_PAD, "dw_ffn1_gu"),
               _mm_tn_rows(act1, dh1b, FF_BLK, "dw_ffn1_down").reshape(N_DEV, DOWN_ROWS, D)]
    names_e = ["w_ffn1_gu", "w_ffn1_down"]
    landed_e = _run_exchange(_pair_exchange(grads_e), "grads_to_sibling_ffn1")
    parts_e = _run_exchange(_chip_exchange(_reduce_group(grads_e, landed_e, core, names_e)), "grads_to_chips_ffn1")

    names = names_a + names_b + ["mix"] + ["w_in"] + names_e
    return loss, dx, dict(zip(names, parts_abc + parts_d + parts_e)), gs


def kernel(x, mem, g_ffn1, w_ffn1_gu, w_ffn1_down, g_mix, w_in, b_gate, conv_w, w_conv_out, w_attn_out, w_o, g_cross, g_mem, w_cq, w_ckv, w_co, g_ffn2, w_ffn2_gu, w_ffn2_down, g_final, loss_target, m_g_ffn1, m_w_ffn1_gu, m_w_ffn1_down, m_g_mix, m_w_in, m_b_gate, m_conv_w, m_w_conv_out, m_w_attn_out, m_w_o, m_g_cross, m_g_mem, m_w_cq, m_w_ckv, m_w_co, m_g_ffn2, m_w_ffn2_gu, m_w_ffn2_down, m_g_final, v_g_ffn1, v_w_ffn1_gu, v_w_ffn1_down, v_g_mix, v_w_in, v_b_gate, v_conv_w, v_w_conv_out, v_w_attn_out, v_w_o, v_g_cross, v_g_mem, v_w_cq, v_w_ckv, v_w_co, v_g_ffn2, v_w_ffn2_gu, v_w_ffn2_down, v_g_final):
    args = locals()
    wts = {n: args[n] for n in WEIGHT_ORDER}
    mom1 = {n: args["m_" + n] for n in WEIGHT_ORDER}
    mom2 = {n: args["v_" + n] for n in WEIGHT_ORDER}
    cx, cy, cc = _mesh_pos()
    dev = 4 * cx + 2 * cy + cc
    conv_cols = D // N_DEV

    conv_pad = jnp.concatenate([conv_w, jnp.zeros((SMALL_R - CONV_K, conv_cols), F32)], axis=0)
    gains = {n: wts[n].reshape(1, D) for n in GAINS}
    loss8, dx, parts, gs = _step(x[0], mem[0], loss_target[0], _exchange_shards(wts), conv_pad, gains,
                                 b_gate.reshape(1, 2 * D), cc.reshape(1).astype(jnp.int32))
    loss = lax.psum(loss8[0, 0], MESH_AXES)

    grads, delta, new_m, new_v = {}, {}, {}, {}

    def update(n, buf, row_block=0, transposed=False):
        w, m1, m2 = wts[n], mom1[n], mom2[n]
        if transposed:
            w, m1, m2 = w.T, m1.T, m2.T
        res = _adamw(w, parts[buf], m1, m2, "adamw_" + n, row_block)
        if transposed:
            res = [r.T for r in res]
        grads[n], delta[n], new_m[n], new_v[n] = res

    for n in GU_NAMES:
        update(n, n, transposed=True)
    for n in ("w_ckv", "w_in", "w_ffn1_down", "w_ffn2_down"):
        update(n, n)
    for k, n in enumerate(MIX_MATS):
        update(n, "mix", k)
    for k, n in enumerate(CROSS_MATS):
        update(n, "cross", k)

    gs_rows = {n: gs[n] for n in GAINS + ("b_gate",)}
    small_all = _run_exchange(_gather_exchange([_pack_small(gs_rows, gs["conv_w"][:CONV_K])]), "gather_small_grads")[0]
    grad_small = _unpack_small(_sum_slots(small_all, "small_grads_sum"))
    grad_small["conv_w"] = lax.dynamic_slice_in_dim(grad_small["conv_w"], dev * conv_cols, conv_cols, axis=1)
    grads.update(grad_small)

    def small_buf(vals):
        return _pack_small(vals, jnp.concatenate([vals["conv_w"], jnp.zeros((CONV_K, D - conv_cols), F32)], axis=1))

    _, d_s, m_s, v_s = _adamw(small_buf(wts), small_buf(grads)[None], small_buf(mom1), small_buf(mom2), "adamw_small")
    for res, buf in ((delta, d_s), (new_m, m_s), (new_v, v_s)):
        un = _unpack_small(buf)
        for n in GAINS + ("b_gate",):
            res[n] = un[n]
        res["conv_w"] = un["conv_w"][:, :conv_cols]

    return (loss, dx[None], *[grads[n] for n in WEIGHT_ORDER], *[delta[n] for n in WEIGHT_ORDER],
            *[new_m[n] for n in WEIGHT_ORDER], *[new_v[n] for n in WEIGHT_ORDER])
```

```python
import types

import jax
import jax.numpy as jnp
from jax import lax
from jax.experimental import pallas as pl
from jax.experimental.pallas import tpu as pltpu

F32 = jnp.float32
BF16 = jnp.bfloat16

D = 1024
DFF = 2816
SB_H = 8
SB_DH = 128
X_H = 4
X_DH = 256
CONV_K = 3
RMS_EPS = 1e-6
N_DEV = 8
N_CHIP = 4
SQ_ROWS = D // N_DEV

ADAM_LR = 0.001
ADAM_B1 = 0.9
ADAM_B2 = 0.999
ADAM_EPS = 1e-08
ADAM_WD = 0.01
ADAM_STEP = 10

TM = 256
TQ = 512
TK = 256
SB_HPS = 2
VMEM_LIMIT = 56 << 20

FF_BLK = DFF // 4
FF_PAD = 768
FF_SUB = 256
DOWN_ROWS = DFF // N_DEV

MIX_MATS = ("w_conv_out", "w_attn_out", "w_o")
CROSS_MATS = ("w_cq", "w_co")

MESH_AXES = ("x", "y", "c")
_ANY = pl.BlockSpec(memory_space=pl.ANY)


def _cparams(n_axes=1):
    return pltpu.CompilerParams(
        dimension_semantics=("arbitrary",) * n_axes, vmem_limit_bytes=VMEM_LIMIT)


def _row_spec(tm, n):
    return pl.BlockSpec((tm, n), lambda i: (i, 0))


def _blk_row_spec(nb, tm, n):
    return pl.BlockSpec((nb, tm, n), lambda i: (0, i, 0))


def _const_spec(shape):
    zeros = (0,) * len(shape)
    return pl.BlockSpec(shape, lambda i: zeros)


def _dot(a, b):
    return jnp.dot(a, b, preferred_element_type=F32)


def _dot_nt(a, b):
    return lax.dot_general(a, b, (((1,), (1,)), ((), ())), preferred_element_type=F32)


def _dot_tn(a, b):
    return lax.dot_general(a, b, (((0,), (0,)), ((), ())), preferred_element_type=F32)


def _sigmoid(x):
    return 1.0 / (1.0 + jnp.exp(-x))


def _call(body, operands, *, grid, in_specs, out_specs, out_shape, scratch_shapes, name, comm=None):
    n_in, n_out, n_sc = len(in_specs), len(out_specs), len(scratch_shapes)
    if comm is None:
        outs = pl.pallas_call(
            body, grid=grid, name=name, in_specs=in_specs, out_specs=out_specs, out_shape=out_shape,
            scratch_shapes=scratch_shapes, compiler_params=_cparams(len(grid)))(*operands)
        return list(outs), []
    c_in, c_out, c_sem = len(comm.inputs), len(comm.out_shapes), len(comm.sem_shapes)

    def hosted(*refs):
        bounds = [0, n_in, c_in, n_out, c_out, n_sc, c_sem]
        parts, pos = [], 0
        for k in bounds[1:]:
            parts.append(refs[pos:pos + k])
            pos += k
        ins, cins, outs, couts, scr, sems = parts
        step, n_steps = pl.program_id(0), grid[0]
        for ax in range(1, len(grid)):
            step, n_steps = step * grid[ax] + pl.program_id(ax), n_steps * grid[ax]

        @pl.when(step == 0)
        def _():
            comm.start(cins, couts, sems)

        @pl.when(step == (2 * n_steps) // 3)
        def _():
            comm.middle(cins, couts, sems)

        body(*ins, *outs, *scr)

        @pl.when(step == n_steps - 1)
        def _():
            comm.finish(cins, couts, sems)

    res = pl.pallas_call(
        hosted, grid=grid, name=name, in_specs=list(in_specs) + [_ANY] * c_in,
        out_specs=list(out_specs) + [_ANY] * c_out, out_shape=list(out_shape) + list(comm.out_shapes),
        scratch_shapes=list(scratch_shapes) + list(comm.sem_shapes),
        compiler_params=_cparams(len(grid)))(*operands, *comm.inputs)
    return list(res[:n_out]), list(res[n_out:])


def _load_resident(step, pairs, sems):
    @pl.when(step == 0)
    def _():
        copies = [pltpu.make_async_copy(src, dst, sems.at[k]) for k, (src, dst) in enumerate(pairs)]
        for cp in copies:
            cp.start()
        for cp in copies:
            cp.wait()


def _square_pairs(buf_hbm, index, dst):
    off = index * SQ_ROWS
    return [(buf_hbm.at[d, off:off + SQ_ROWS, :], dst.at[d * SQ_ROWS:(d + 1) * SQ_ROWS, :]) for d in range(N_DEV)]


def _down_pairs(wd_hbm, dst):
    return [(wd_hbm.at[d], dst.at[d // 2, (d % 2) * DOWN_ROWS:(d % 2 + 1) * DOWN_ROWS, :]) for d in range(N_DEV)]


def _zero_down_pad(step, dst):
    @pl.when(step == 0)
    def _():
        dst[:, FF_BLK:, :] = jnp.zeros((4, FF_PAD - FF_BLK, D), BF16)


def _rms_fwd_tile(xt, g):
    r = lax.rsqrt(jnp.mean(xt * xt, axis=-1, keepdims=True) + RMS_EPS)
    return (xt * r) * g


def _rms_bwd_tile(xt, g, dn):
    r = lax.rsqrt(jnp.mean(xt * xt, axis=-1, keepdims=True) + RMS_EPS)
    xhat = xt * r
    dxhat = dn * g
    dx = r * (dxhat - xhat * jnp.mean(dxhat * xhat, axis=-1, keepdims=True))
    dg = jnp.sum(dn * xhat, axis=0, keepdims=True)
    return dx, dg


def _accumulate(ref, step, value):
    @pl.when(step == 0)
    def _():
        ref[...] = value

    @pl.when(step != 0)
    def _():
        ref[...] = ref[...] + value


def _ffn_fwd(x, g, wgu, wd, name, comm=None):
    t = x.shape[0]

    def body(x_ref, g_ref, wgu_hbm, wd_hbm, n_ref, gate_ref, up_ref, act_ref, h_ref, wgu_v, wd_v, sems):
        step = pl.program_id(0)
        _zero_down_pad(step, wd_v)
        _load_resident(step, [(wgu_hbm, wgu_v)] + _down_pairs(wd_hbm, wd_v), sems)
        xt = x_ref[...]
        n = _rms_fwd_tile(xt, g_ref[...]).astype(BF16)
        n_ref[...] = n
        acc = jnp.zeros((TM, D), F32)
        for j in range(4):
            for s in range(FF_PAD // FF_SUB):
                lo, hi = s * FF_SUB, (s + 1) * FF_SUB
                gt = _dot_nt(n, wgu_v[j, lo:hi, :])
                ut = _dot_nt(n, wgu_v[4 + j, lo:hi, :])
                gate_ref[j, :, lo:hi] = gt.astype(BF16)
                up_ref[j, :, lo:hi] = ut.astype(BF16)
                act_ref[j, :, lo:hi] = ((gt * _sigmoid(gt)) * ut).astype(BF16)
            acc = acc + _dot(act_ref[j], wd_v[j])
        h_ref[...] = xt + 0.5 * acc

    ff = jax.ShapeDtypeStruct((4, t, FF_PAD), BF16)
    return _call(
        body, (x, g, wgu, wd), grid=(t // TM,), name=name, comm=comm,
        in_specs=[_row_spec(TM, D), _const_spec((1, D)), _ANY, _ANY],
        out_specs=[_row_spec(TM, D)] + [_blk_row_spec(4, TM, FF_PAD)] * 3 + [_row_spec(TM, D)],
        out_shape=[jax.ShapeDtypeStruct((t, D), BF16), ff, ff, ff, jax.ShapeDtypeStruct((t, D), F32)],
        scratch_shapes=[pltpu.VMEM((N_DEV, FF_PAD, D), BF16), pltpu.VMEM((4, FF_PAD, D), BF16),
                        pltpu.SemaphoreType.DMA((1 + N_DEV,))])


def _ffn_bwd(dh, xin, g, gate, up, wgu, wd, name, comm=None):
    t = dh.shape[0]

    def body(dh_ref, x_ref, g_ref, gate_ref, up_ref, wgu_hbm, wd_hbm,
             dgu_ref, dhb_ref, dx_ref, dg_ref, wgu_v, wd_v, sems):
        step = pl.program_id(0)
        _zero_down_pad(step, wd_v)
        _load_resident(step, [(wgu_hbm, wgu_v)] + _down_pairs(wd_hbm, wd_v), sems)
        dht = dh_ref[...]
        dhb = (0.5 * dht).astype(BF16)
        dhb_ref[...] = dhb
        dn = jnp.zeros((TM, D), F32)
        for j in range(4):
            for s in range(FF_PAD // FF_SUB):
                lo, hi = s * FF_SUB, (s + 1) * FF_SUB
                da = _dot_nt(dhb, wd_v[j, lo:hi, :])
                gt = gate_ref[j, :, lo:hi].astype(F32)
                ut = up_ref[j, :, lo:hi].astype(F32)
                sg = _sigmoid(gt)
                dgt = (da * ut * (sg * (1.0 + gt * (1.0 - sg)))).astype(BF16)
                dut = (da * (gt * sg)).astype(BF16)
                dgu_ref[j, :, lo:hi] = dgt
                dgu_ref[4 + j, :, lo:hi] = dut
            dn = dn + _dot(dgu_ref[j], wgu_v[j]) + _dot(dgu_ref[4 + j], wgu_v[4 + j])
        dx, dg = _rms_bwd_tile(x_ref[...], g_ref[...], dn)
        dx_ref[...] = dht + dx
        _accumulate(dg_ref, step, dg)

    return _call(
        body, (dh, xin, g, gate, up, wgu, wd), grid=(t // TM,), name=name, comm=comm,
        in_specs=[_row_spec(TM, D), _row_spec(TM, D), _const_spec((1, D)), _blk_row_spec(4, TM, FF_PAD),
                  _blk_row_spec(4, TM, FF_PAD), _ANY, _ANY],
        out_specs=[_blk_row_spec(N_DEV, TM, FF_PAD), _row_spec(TM, D), _row_spec(TM, D), _const_spec((1, D))],
        out_shape=[jax.ShapeDtypeStruct((N_DEV, t, FF_PAD), BF16), jax.ShapeDtypeStruct((t, D), BF16),
                   jax.ShapeDtypeStruct((t, D), F32), jax.ShapeDtypeStruct((1, D), F32)],
        scratch_shapes=[pltpu.VMEM((N_DEV, FF_PAD, D), BF16), pltpu.VMEM((4, FF_PAD, D), BF16),
                        pltpu.SemaphoreType.DMA((1 + N_DEV,))])


WIDE_TILES = (1024, 512, 256, 128)


def _pick_tile(n, options=(512, 256, 128)):
    for o in options:
        if n % o == 0:
            return o
    return n


def _mm_tn(a, b, name):
    k, m = a.shape
    _, n = b.shape
    tm, tn = _pick_tile(m, WIDE_TILES), _pick_tile(n)

    def body(a_ref, b_ref, o_ref):
        o_ref[...] = _dot_tn(a_ref[...].astype(BF16), b_ref[...].astype(BF16)).astype(BF16)

    return pl.pallas_call(
        body, grid=(m // tm, n // tn), name=name,
        in_specs=[pl.BlockSpec((k, tm), lambda i, j: (0, i)), pl.BlockSpec((k, tn), lambda i, j: (0, j))],
        out_specs=pl.BlockSpec((tm, tn), lambda i, j: (i, j)),
        out_shape=jax.ShapeDtypeStruct((m, n), BF16),
        compiler_params=_cparams(2),
    )(a, b)


def _mm_tn_cols(a, b, name):
    k, m = a.shape
    nb, _, n = b.shape
    tm = _pick_tile(m, WIDE_TILES)

    def body(a_ref, b_ref, o_ref):
        o_ref[0] = _dot_tn(a_ref[...].astype(BF16), b_ref[0].astype(BF16)).astype(BF16)

    return pl.pallas_call(
        body, grid=(nb, m // tm), name=name,
        in_specs=[pl.BlockSpec((k, tm), lambda j, i: (0, i)), pl.BlockSpec((1, k, n), lambda j, i: (j, 0, 0))],
        out_specs=pl.BlockSpec((1, tm, n), lambda j, i: (j, i, 0)),
        out_shape=jax.ShapeDtypeStruct((nb, m, n), BF16),
        compiler_params=_cparams(2),
    )(a, b)


def _mm_tn_rows(a, b, keep, name):
    nb, k, m = a.shape
    _, n = b.shape
    tn = _pick_tile(n, WIDE_TILES)

    def body(a_ref, b_ref, o_ref):
        o_ref[0] = _dot_tn(a_ref[0], b_ref[...])[:keep].astype(BF16)

    return pl.pallas_call(
        body, grid=(nb, n // tn), name=name,
        in_specs=[pl.BlockSpec((1, k, m), lambda j, i: (j, 0, 0)), pl.BlockSpec((k, tn), lambda j, i: (0, i))],
        out_specs=pl.BlockSpec((1, keep, tn), lambda j, i: (j, 0, i)),
        out_shape=jax.ShapeDtypeStruct((nb, keep, n), BF16),
        compiler_params=_cparams(2),
    )(a, b)


PCG_W = 5 * D
QKV_W = 3 * D
PROJ_SUB = 512


def _inproj_fwd(h, g, w_in, name, comm=None):
    t = h.shape[0]

    def body(h_ref, g_ref, w_hbm, u_ref, pcg_ref, qkv_ref, w_v, sems):
        _load_resident(pl.program_id(0), [(w_hbm, w_v)], sems)
        u = _rms_fwd_tile(h_ref[...], g_ref[...]).astype(BF16)
        u_ref[...] = u
        for blk in range(N_DEV):
            for s in range(D // PROJ_SUB):
                lo, hi = s * PROJ_SUB, (s + 1) * PROJ_SUB
                p = _dot(u, w_v[blk, :, lo:hi])
                if blk < 3:
                    pcg_ref[:, blk * D + lo:blk * D + hi] = p
                elif blk < 6:
                    qkv_ref[:, (blk - 3) * D + lo:(blk - 3) * D + hi] = p.astype(BF16)
                else:
                    pcg_ref[:, (blk - 3) * D + lo:(blk - 3) * D + hi] = p

    return _call(
        body, (h, g, w_in), grid=(t // TM,), name=name, comm=comm,
        in_specs=[_row_spec(TM, D), _const_spec((1, D)), _ANY],
        out_specs=[_row_spec(TM, D), _row_spec(TM, PCG_W), _row_spec(TM, QKV_W)],
        out_shape=[jax.ShapeDtypeStruct((t, D), BF16), jax.ShapeDtypeStruct((t, PCG_W), F32),
                   jax.ShapeDtypeStruct((t, QKV_W), BF16)],
        scratch_shapes=[pltpu.VMEM((N_DEV, D, D), BF16), pltpu.SemaphoreType.DMA((1,))])


CONV_CW = 256


def _shift_down(v, k, rows):
    return jnp.where(rows >= k, pltpu.roll(v, k, 0), 0.0)


def _shift_up(v, k, rows, t):
    return jnp.where(rows < t - k, pltpu.roll(v, t - k, 0), 0.0)


def _col_spec(t, cw, off):
    return pl.BlockSpec((t, cw), lambda j: (0, j + off))


def _conv_fwd(pcg, conv_w, name):
    t = pcg.shape[0]
    nb = D // CONV_CW

    def body(cb_ref, cc_ref, cx_ref, w_ref, y_ref):
        rows = lax.broadcasted_iota(jnp.int32, (t, CONV_CW), 0)
        xc = cc_ref[...] * cx_ref[...]
        conv = (w_ref[0:1, :] * _shift_down(xc, 2, rows) + w_ref[1:2, :] * _shift_down(xc, 1, rows)
                + w_ref[2:3, :] * xc)
        y_ref[...] = (cb_ref[...] * conv).astype(BF16)

    return pl.pallas_call(
        body, grid=(nb,), name=name,
        in_specs=[_col_spec(t, CONV_CW, 0), _col_spec(t, CONV_CW, nb), _col_spec(t, CONV_CW, 2 * nb),
                  pl.BlockSpec((CONV_K, CONV_CW), lambda j: (0, j))],
        out_specs=_col_spec(t, CONV_CW, 0),
        out_shape=jax.ShapeDtypeStruct((t, D), BF16),
        compiler_params=_cparams(),
    )(pcg, pcg, pcg, conv_w)


def _conv_bwd(pcg, conv_w, dyc, name):
    t = pcg.shape[0]
    nb = D // CONV_CW

    def body(cb_ref, cc_ref, cx_ref, w_ref, dy_ref, dc_ref, dw_ref):
        rows = lax.broadcasted_iota(jnp.int32, (t, CONV_CW), 0)
        cc, cx = cc_ref[...], cx_ref[...]
        xc = cc * cx
        x1 = _shift_down(xc, 1, rows)
        x2 = _shift_down(xc, 2, rows)
        w0, w1, w2 = w_ref[0:1, :], w_ref[1:2, :], w_ref[2:3, :]
        conv = w0 * x2 + w1 * x1 + w2 * xc
        dy = dy_ref[...]
        dc_ref[0] = (dy * conv).astype(BF16)
        dconv = dy * cb_ref[...]
        dw_ref[...] = jnp.zeros((8, CONV_CW), F32)
        dw_ref[0:1, :] = jnp.sum(dconv * x2, axis=0, keepdims=True)
        dw_ref[1:2, :] = jnp.sum(dconv * x1, axis=0, keepdims=True)
        dw_ref[2:3, :] = jnp.sum(dconv * xc, axis=0, keepdims=True)
        dxc = w2 * dconv + w1 * _shift_up(dconv, 1, rows, t) + w0 * _shift_up(dconv, 2, rows, t)
        dc_ref[1] = (dxc * cx).astype(BF16)
        dc_ref[2] = (dxc * cc).astype(BF16)

    return pl.pallas_call(
        body, grid=(nb,), name=name,
        in_specs=[_col_spec(t, CONV_CW, 0), _col_spec(t, CONV_CW, nb), _col_spec(t, CONV_CW, 2 * nb),
                  pl.BlockSpec((CONV_K, CONV_CW), lambda j: (0, j)), _col_spec(t, CONV_CW, 0)],
        out_specs=[pl.BlockSpec((3, t, CONV_CW), lambda j: (0, 0, j)), pl.BlockSpec((8, CONV_CW), lambda j: (0, j))],
        out_shape=[jax.ShapeDtypeStruct((3, t, D), BF16), jax.ShapeDtypeStruct((8, D), F32)],
        compiler_params=_cparams(),
    )(pcg, pcg, pcg, conv_w, dyc)


def _tri2(cond):
    rr = lax.broadcasted_iota(jnp.int32, (2 * TK, TK), 0) & (TK - 1)
    cc = lax.broadcasted_iota(jnp.int32, (2 * TK, TK), 1)
    return cond(rr, cc).astype(BF16)


def _causal(shift, row0=0):
    rr = lax.broadcasted_iota(jnp.int32, (TQ - row0, TK), 0) + row0
    cc = lax.broadcasted_iota(jnp.int32, (TQ - row0, TK), 1)
    return cc + shift < rr


def _cumdot(v, tri2):
    hi = v.astype(BF16)
    lo = (v - hi.astype(F32)).astype(BF16)
    return _dot(jnp.concatenate([hi, lo], axis=1), tri2)


def _log_1m_beta(z):
    return -(jnp.maximum(z, 0.0) + jnp.log(1.0 + jnp.exp(-jnp.abs(z))))


def _sb_specs(t):
    g = SB_H // SB_HPS
    w = SB_HPS * SB_DH
    q_spec = pl.BlockSpec((TQ, w), lambda h, i: (i, h))
    k_spec = pl.BlockSpec((t, w), lambda h, i: (0, g + h))
    v_spec = pl.BlockSpec((t, w), lambda h, i: (0, 2 * g + h))
    ct_spec = pl.BlockSpec((SB_HPS, TQ, 1), lambda h, i: (h, i, 0))
    return g, w, q_spec, k_spec, v_spec, ct_spec


def _sb_fwd(qkv, name, comm=None):
    t = qkv.shape[0]
    scale = SB_DH ** -0.5
    g, w, q_spec, k_spec, v_spec, ct_spec = _sb_specs(t)

    def body(q_ref, k_ref, v_ref, y_ref, ct_ref):
        i = pl.program_id(1)
        later = _tri2(lambda j, s: j > s)
        n_diag = TQ // TK

        def block(j, carry, shift):
            off = pl.multiple_of(j * TK, TK)
            zs, ms = [], []
            for hd in range(SB_HPS):
                cols = slice(hd * SB_DH, (hd + 1) * SB_DH)
                z = _dot_nt(q_ref[:, cols], k_ref[pl.ds(off, TK), cols]) * scale
                m = _log_1m_beta(z)
                if shift is not None:
                    m = jnp.where(_causal(shift), m, 0.0)
                zs.append(z)
                ms.append(m)
            after = _cumdot(jnp.concatenate(ms, axis=0), later)
            out = []
            for hd in range(SB_HPS):
                acc, c_sum = carry[hd]
                cols = slice(hd * SB_DH, (hd + 1) * SB_DH)
                a = jnp.exp((ms[hd] + zs[hd]) + (c_sum + after[hd * TQ:(hd + 1) * TQ]))
                if shift is not None:
                    a = jnp.where(_causal(shift), a, 0.0)
                out.append((acc + _dot(a.astype(BF16), v_ref[pl.ds(off, TK), cols]),
                            c_sum + jnp.sum(ms[hd], axis=1, keepdims=True)))
            return tuple(out)

        carry = tuple((jnp.zeros((TQ, SB_DH), F32), jnp.zeros((TQ, 1), F32)) for _ in range(SB_HPS))
        for d in reversed(range(n_diag)):
            carry = block(i * n_diag + d, carry, d * TK)
        carry = lax.fori_loop(0, i * n_diag, lambda jj, c: block(i * n_diag - 1 - jj, c, None), carry)
        for hd in range(SB_HPS):
            y_ref[:, hd * SB_DH:(hd + 1) * SB_DH] = carry[hd][0].astype(BF16)
            ct_ref[hd] = carry[hd][1]

    return _call(
        body, (qkv, qkv, qkv), grid=(g, t // TQ), name=name, comm=comm,
        in_specs=[q_spec, k_spec, v_spec],
        out_specs=[q_spec, ct_spec],
        out_shape=[jax.ShapeDtypeStruct((t, D), BF16), jax.ShapeDtypeStruct((SB_H, t, 1), F32)],
        scratch_shapes=[])


def _sb_bwd(qkv, dy, ctot, name, comm=None):
    t = qkv.shape[0]
    scale = SB_DH ** -0.5
    g, w, q_spec, k_spec, v_spec, ct_spec = _sb_specs(t)
    acc_spec = pl.BlockSpec((2, t, w), lambda h, i: (0, 0, h))

    def body(q_ref, k_ref, v_ref, dy_ref, ct_ref, dq_ref, dkv_ref):
        i = pl.program_id(1)

        @pl.when(i == 0)
        def _():
            dkv_ref[...] = jnp.zeros_like(dkv_ref)

        upto = _tri2(lambda j, s: j <= s)
        n_diag = TQ // TK

        def block(j, carry, shift):
            off = pl.multiple_of(j * TK, TK)
            r0 = 0 if shift is None else shift
            nr = TQ - r0
            causal = None if shift is None else _causal(shift, r0)

            def grow(old, delta):
                return old + delta if r0 == 0 else jnp.concatenate([old[:r0], old[r0:] + delta], axis=0)

            zs, ms = [], []
            for hd in range(SB_HPS):
                cols = slice(hd * SB_DH, (hd + 1) * SB_DH)
                z = _dot_nt(q_ref[r0:, cols], k_ref[pl.ds(off, TK), cols]) * scale
                m = _log_1m_beta(z)
                if causal is not None:
                    m = jnp.where(causal, m, 0.0)
                zs.append(z)
                ms.append(m)
            m_upto = _cumdot(jnp.concatenate(ms, axis=0), upto)
            ls, a_s, es = [], [], []
            for hd in range(SB_HPS):
                cols = slice(hd * SB_DH, (hd + 1) * SB_DH)
                l = ms[hd] + zs[hd]
                a = jnp.exp(l + ((ct_ref[hd, r0:] - carry[hd][1][r0:]) - m_upto[hd * nr:(hd + 1) * nr]))
                if causal is not None:
                    a = jnp.where(causal, a, 0.0)
                ls.append(l)
                a_s.append(a)
                es.append(_dot_nt(dy_ref[r0:, cols], v_ref[pl.ds(off, TK), cols]) * a)
            e_upto = _dot(jnp.concatenate(es, axis=0).astype(BF16), upto[:TK])
            out = []
            for hd in range(SB_HPS):
                dq, p_sum, e_sum = carry[hd]
                cols = slice(hd * SB_DH, (hd + 1) * SB_DH)
                e = es[hd]
                dz = e - jnp.exp(ls[hd]) * (e_sum[r0:] + e_upto[hd * nr:(hd + 1) * nr])
                if causal is not None:
                    dz = jnp.where(causal, dz, 0.0)
                dzs = (dz * scale).astype(BF16)
                dkv_ref[0, pl.ds(off, TK), cols] += _dot_tn(dzs, q_ref[r0:, cols])
                dkv_ref[1, pl.ds(off, TK), cols] += _dot_tn(a_s[hd].astype(BF16), dy_ref[r0:, cols])
                out.append((grow(dq, _dot(dzs, k_ref[pl.ds(off, TK), cols])),
                            grow(p_sum, jnp.sum(ms[hd], axis=1, keepdims=True)),
                            grow(e_sum, jnp.sum(e, axis=1, keepdims=True))))
            return tuple(out)

        zero = jnp.zeros((TQ, 1), F32)
        init = tuple((jnp.zeros((TQ, SB_DH), F32), zero, zero) for _ in range(SB_HPS))
        carry = lax.fori_loop(0, i * n_diag, lambda j, c: block(j, c, None), init)
        for d in range(n_diag):
            carry = block(i * n_diag + d, carry, d * TK)
        for hd in range(SB_HPS):
            dq_ref[:, hd * SB_DH:(hd + 1) * SB_DH] = carry[hd][0].astype(BF16)

    return _call(
        body, (qkv, qkv, qkv, dy, ctot), grid=(g, t // TQ), name=name, comm=comm,
        in_specs=[q_spec, k_spec, v_spec, q_spec, ct_spec],
        out_specs=[q_spec, acc_spec],
        out_shape=[jax.ShapeDtypeStruct((t, D), BF16), jax.ShapeDtypeStruct((2, t, D), F32)],
        scratch_shapes=[])


def _gate_specs():
    return [pl.BlockSpec((TM, D), lambda i: (i, 3)), pl.BlockSpec((TM, D), lambda i: (i, 4))]


def _mix_pairs(mix_hbm, dsts):
    pairs = []
    for index, dst in enumerate(dsts):
        pairs += _square_pairs(mix_hbm, index, dst)
    return pairs


def _mix_out_fwd(yc, ysb, pcg, b_gate, h, w_mix, name):
    t = h.shape[0]

    def body(yc_ref, ysb_ref, gc_ref, gs_ref, b_ref, h_ref, mix_hbm,
             a_ref, b_out_ref, mg_ref, h2_ref, wc_v, wa_v, wo_v, sems):
        _load_resident(pl.program_id(0), _mix_pairs(mix_hbm, (wc_v, wa_v, wo_v)), sems)
        a = _dot(yc_ref[...], wc_v[...])
        b = _dot(ysb_ref[...], wa_v[...])
        merged = (_sigmoid(gc_ref[...] + b_ref[:, :D]) * a + _sigmoid(gs_ref[...] + b_ref[:, D:]) * b).astype(BF16)
        a_ref[...] = a
        b_out_ref[...] = b
        mg_ref[...] = merged
        h2_ref[...] = h_ref[...] + _dot(merged, wo_v[...])

    return pl.pallas_call(
        body, grid=(t // TM,), name=name,
        in_specs=[_row_spec(TM, D), _row_spec(TM, D)] + _gate_specs()
                 + [_const_spec((1, 2 * D)), _row_spec(TM, D), _ANY],
        out_specs=[_row_spec(TM, D)] * 4,
        out_shape=[jax.ShapeDtypeStruct((t, D), F32), jax.ShapeDtypeStruct((t, D), F32),
                   jax.ShapeDtypeStruct((t, D), BF16), jax.ShapeDtypeStruct((t, D), F32)],
        scratch_shapes=[pltpu.VMEM((D, D), BF16)] * 3 + [pltpu.SemaphoreType.DMA((3 * N_DEV,))],
        compiler_params=_cparams(),
    )(yc, ysb, pcg, pcg, b_gate, h, w_mix)


def _mix_out_bwd(dh2, a, b, pcg, b_gate, w_mix, name, comm=None):
    t = dh2.shape[0]

    def body(dh_ref, a_ref, b_ref, gc_ref, gs_ref, bias_ref, mix_hbm,
             dhb_ref, da_ref, db_ref, dgp_ref, dyc_ref, dysb_ref, dbias_ref, wc_v, wa_v, wo_v, sems):
        step = pl.program_id(0)
        _load_resident(step, _mix_pairs(mix_hbm, (wc_v, wa_v, wo_v)), sems)
        dhb = dh_ref[...].astype(BF16)
        dhb_ref[...] = dhb
        dm = _dot_nt(dhb, wo_v[...])
        gc = _sigmoid(gc_ref[...] + bias_ref[:, :D])
        gs = _sigmoid(gs_ref[...] + bias_ref[:, D:])
        da = (dm * gc).astype(BF16)
        db = (dm * gs).astype(BF16)
        da_ref[...] = da
        db_ref[...] = db
        dgc = dm * a_ref[...] * (gc * (1.0 - gc))
        dgs = dm * b_ref[...] * (gs * (1.0 - gs))
        dgp_ref[0] = dgc.astype(BF16)
        dgp_ref[1] = dgs.astype(BF16)
        _accumulate(dbias_ref.at[:, :D], step, jnp.sum(dgc, axis=0, keepdims=True))
        _accumulate(dbias_ref.at[:, D:], step, jnp.sum(dgs, axis=0, keepdims=True))
        dyc_ref[...] = _dot_nt(da, wc_v[...])
        dysb_ref[...] = _dot_nt(db, wa_v[...]).astype(BF16)

    return _call(
        body, (dh2, a, b, pcg, pcg, b_gate, w_mix), grid=(t // TM,), name=name, comm=comm,
        in_specs=[_row_spec(TM, D)] * 3 + _gate_specs() + [_const_spec((1, 2 * D)), _ANY],
        out_specs=[_row_spec(TM, D)] * 3 + [_blk_row_spec(2, TM, D), _row_spec(TM, D), _row_spec(TM, D),
                                            _const_spec((1, 2 * D))],
        out_shape=[jax.ShapeDtypeStruct((t, D), BF16)] * 3
                  + [jax.ShapeDtypeStruct((2, t, D), BF16), jax.ShapeDtypeStruct((t, D), F32),
                     jax.ShapeDtypeStruct((t, D), BF16), jax.ShapeDtypeStruct((1, 2 * D), F32)],
        scratch_shapes=[pltpu.VMEM((D, D), BF16)] * 3 + [pltpu.SemaphoreType.DMA((3 * N_DEV,))])


def _inproj_bwd(dconv, dq, dkv, dgp, w_in, h, g, dh_res, name, comm=None):
    t = h.shape[0]

    def body(dc_ref, dq_ref, dkv_ref, dgp_ref, w_hbm, h_ref, g_ref, dres_ref, dh_ref, dg_ref, w_v, sems):
        step = pl.program_id(0)
        _load_resident(step, [(w_hbm, w_v)], sems)
        du = _dot_nt(dq_ref[...], w_v[3])
        for k in range(3):
            du = du + _dot_nt(dc_ref[k], w_v[k])
        for k in range(2):
            du = du + _dot_nt(dkv_ref[k].astype(BF16), w_v[4 + k]) + _dot_nt(dgp_ref[k], w_v[6 + k])
        dx, dg = _rms_bwd_tile(h_ref[...], g_ref[...], du)
        dh_ref[...] = dres_ref[...] + dx
        _accumulate(dg_ref, step, dg)

    return _call(
        body, (dconv, dq, dkv, dgp, w_in, h, g, dh_res), grid=(t // TM,), name=name, comm=comm,
        in_specs=[_blk_row_spec(3, TM, D), _row_spec(TM, D), _blk_row_spec(2, TM, D), _blk_row_spec(2, TM, D), _ANY,
                  _row_spec(TM, D), _const_spec((1, D)), _row_spec(TM, D)],
        out_specs=[_row_spec(TM, D), _const_spec((1, D))],
        out_shape=[jax.ShapeDtypeStruct((t, D), F32), jax.ShapeDtypeStruct((1, D), F32)],
        scratch_shapes=[pltpu.VMEM((N_DEV, D, D), BF16), pltpu.SemaphoreType.DMA((1,))])


def _memkv_fwd(mem, g, w_ckv, name):
    m = mem.shape[0]

    def body(mem_ref, g_ref, w_ref, mn_ref, kv_ref):
        mn = _rms_fwd_tile(mem_ref[...], g_ref[...]).astype(BF16)
        mn_ref[...] = mn
        for j in range(N_DEV):
            kv_ref[j] = _dot(mn, w_ref[j]).astype(BF16)

    return pl.pallas_call(
        body, grid=(1,), name=name,
        in_specs=[_const_spec((m, D)), _const_spec((1, D)), _const_spec((N_DEV, D, X_DH))],
        out_specs=[_const_spec((m, D)), _const_spec((N_DEV, m, X_DH))],
        out_shape=[jax.ShapeDtypeStruct((m, D), BF16), jax.ShapeDtypeStruct((N_DEV, m, X_DH), BF16)],
        compiler_params=_cparams(),
    )(mem, g, w_ckv)


def _memkv_bwd(dkv, mem, g, w_ckv, name):
    m = mem.shape[0]

    def body(dkv_ref, mem_ref, g_ref, w_ref, dg_ref):
        dmn = jnp.zeros((m, D), F32)
        for j in range(N_DEV):
            dmn = dmn + _dot_nt(dkv_ref[j].astype(BF16), w_ref[j])
        _, dg = _rms_bwd_tile(mem_ref[...], g_ref[...], dmn)
        dg_ref[...] = dg

    return pl.pallas_call(
        body, grid=(1,), name=name,
        in_specs=[_const_spec((N_DEV, m, X_DH)), _const_spec((m, D)), _const_spec((1, D)),
                  _const_spec((N_DEV, D, X_DH))],
        out_specs=_const_spec((1, D)),
        out_shape=jax.ShapeDtypeStruct((1, D), F32),
        compiler_params=_cparams(),
    )(dkv, mem, g, w_ckv)


def _softmax_rows(s):
    e = jnp.exp(s - jnp.max(s, axis=-1, keepdims=True))
    return e / jnp.sum(e, axis=-1, keepdims=True)


def _cross_pairs(cross_hbm, wq_v, wo_v):
    return _square_pairs(cross_hbm, 0, wq_v) + _square_pairs(cross_hbm, 1, wo_v)


def _cross_fwd(h, g, kv, w_cross, name):
    t = h.shape[0]
    m = kv.shape[1]
    scale = X_DH ** -0.5

    def body(h_ref, g_ref, kv_ref, cross_hbm, hn_ref, qx_ref, o_ref, h3_ref, wq_v, wo_v, sems):
        _load_resident(pl.program_id(0), _cross_pairs(cross_hbm, wq_v, wo_v), sems)
        ht = h_ref[...]
        hn = _rms_fwd_tile(ht, g_ref[...]).astype(BF16)
        hn_ref[...] = hn
        qx = _dot(hn, wq_v[...]).astype(BF16)
        qx_ref[...] = qx
        for hd in range(X_H):
            lo, hi = hd * X_DH, (hd + 1) * X_DH
            p = _softmax_rows(_dot_nt(qx[:, lo:hi], kv_ref[hd]) * scale)
            o_ref[:, lo:hi] = _dot(p.astype(BF16), kv_ref[X_H + hd]).astype(BF16)
        h3_ref[...] = ht + _dot(o_ref[...], wo_v[...])

    return pl.pallas_call(
        body, grid=(t // TM,), name=name,
        in_specs=[_row_spec(TM, D), _const_spec((1, D)), _const_spec((N_DEV, m, X_DH)), _ANY],
        out_specs=[_row_spec(TM, D)] * 4,
        out_shape=[jax.ShapeDtypeStruct((t, D), BF16)] * 3 + [jax.ShapeDtypeStruct((t, D), F32)],
        scratch_shapes=[pltpu.VMEM((D, D), BF16)] * 2 + [pltpu.SemaphoreType.DMA((2 * N_DEV,))],
        compiler_params=_cparams(),
    )(h, g, kv, w_cross)


def _cross_bwd(dh3, h, g, qx, kv, w_cross, name, comm=None):
    t = h.shape[0]
    m = kv.shape[1]
    scale = X_DH ** -0.5

    def body(dh_ref, h_ref, g_ref, qx_ref, kv_ref, cross_hbm,
             dhb_ref, dqx_ref, dkv_ref, dh2_ref, dg_ref, wq_v, wo_v, sems):
        step = pl.program_id(0)
        _load_resident(step, _cross_pairs(cross_hbm, wq_v, wo_v), sems)

        @pl.when(step == 0)
        def _():
            dkv_ref[...] = jnp.zeros_like(dkv_ref)

        dht = dh_ref[...]
        dhb = dht.astype(BF16)
        dhb_ref[...] = dhb
        do = _dot_nt(dhb, wo_v[...]).astype(BF16)
        for hd in range(X_H):
            lo, hi = hd * X_DH, (hd + 1) * X_DH
            qh = qx_ref[:, lo:hi]
            kh = kv_ref[hd]
            p = _softmax_rows(_dot_nt(qh, kh) * scale)
            doh = do[:, lo:hi]
            dp = _dot_nt(doh, kv_ref[X_H + hd])
            ds = (p * (dp - jnp.sum(dp * p, axis=-1, keepdims=True)) * scale).astype(BF16)
            dqx_ref[:, lo:hi] = _dot(ds, kh).astype(BF16)
            dkv_ref[hd] += _dot_tn(ds, qh)
            dkv_ref[X_H + hd] += _dot_tn(p.astype(BF16), doh)
        dhn = _dot_nt(dqx_ref[...], wq_v[...])
        dx, dg = _rms_bwd_tile(h_ref[...], g_ref[...], dhn)
        dh2_ref[...] = dht + dx
        _accumulate(dg_ref, step, dg)

    return _call(
        body, (dh3, h, g, qx, kv, w_cross), grid=(t // TM,), name=name, comm=comm,
        in_specs=[_row_spec(TM, D), _row_spec(TM, D), _const_spec((1, D)), _row_spec(TM, D),
                  _const_spec((N_DEV, m, X_DH)), _ANY],
        out_specs=[_row_spec(TM, D), _row_spec(TM, D), _const_spec((N_DEV, m, X_DH)), _row_spec(TM, D),
                   _const_spec((1, D))],
        out_shape=[jax.ShapeDtypeStruct((t, D), BF16), jax.ShapeDtypeStruct((t, D), BF16),
                   jax.ShapeDtypeStruct((N_DEV, m, X_DH), F32), jax.ShapeDtypeStruct((t, D), F32),
                   jax.ShapeDtypeStruct((1, D), F32)],
        scratch_shapes=[pltpu.VMEM((D, D), BF16)] * 2 + [pltpu.SemaphoreType.DMA((2 * N_DEV,))])


def _loss_bwd(h, g, target, name):
    t = h.shape[0]

    def body(h_ref, g_ref, t_ref, loss_ref, dh_ref, dg_ref):
        step = pl.program_id(0)
        ht = h_ref[...]
        gain = g_ref[...]
        diff = _rms_fwd_tile(ht, gain) - t_ref[...]
        part = 0.5 * jnp.sum(jnp.sum(diff * diff, axis=-1, keepdims=True) / D, axis=0, keepdims=True)
        dx, dg = _rms_bwd_tile(ht, gain, diff / D)
        dh_ref[...] = dx
        _accumulate(loss_ref, step, jnp.broadcast_to(part, (8, 128)))
        _accumulate(dg_ref, step, dg)

    return pl.pallas_call(
        body, grid=(t // TM,), name=name,
        in_specs=[_row_spec(TM, D), _const_spec((1, D)), _row_spec(TM, D)],
        out_specs=[_const_spec((8, 128)), _row_spec(TM, D), _const_spec((1, D))],
        out_shape=[jax.ShapeDtypeStruct((8, 128), F32), jax.ShapeDtypeStruct((t, D), F32),
                   jax.ShapeDtypeStruct((1, D), F32)],
        compiler_params=_cparams(),
    )(h, g, target)


def _adamw(w, parts, m, v, name, row_block=0):
    r, c = w.shape
    n = parts.shape[0]
    tr = _pick_tile(r, (256, 352, 128))
    off = row_block * (r // tr)

    def body(w_ref, p_ref, m_ref, v_ref, g_ref, d_ref, nm_ref, nv_ref):
        gt = p_ref[0].astype(F32)
        for k in range(1, n):
            gt = gt + p_ref[k].astype(F32)
        g_ref[...] = gt
        nm = ADAM_B1 * m_ref[...] + (1.0 - ADAM_B1) * gt
        nv = ADAM_B2 * v_ref[...] + (1.0 - ADAM_B2) * jnp.square(gt)
        m_hat = nm / (1.0 - ADAM_B1 ** ADAM_STEP)
        v_hat = nv / (1.0 - ADAM_B2 ** ADAM_STEP)
        d_ref[...] = -ADAM_LR * (m_hat / (jnp.sqrt(v_hat) + ADAM_EPS) + ADAM_WD * w_ref[...])
        nm_ref[...] = nm
        nv_ref[...] = nv

    spec = _row_spec(tr, c)
    return pl.pallas_call(
        body, grid=(r // tr,), name=name,
        in_specs=[spec, pl.BlockSpec((n, tr, c), lambda i: (0, i + off, 0)), spec, spec], out_specs=[spec] * 4,
        out_shape=[jax.ShapeDtypeStruct((r, c), F32)] * 4,
        compiler_params=_cparams(),
    )(w, parts, m, v)


def _mesh_pos():
    return lax.axis_index("x"), lax.axis_index("y"), lax.axis_index("c")


def _no_round(in_refs, out_refs, sems):
    pass


def _run_exchange(comm, name):
    c_in, c_out = len(comm.inputs), len(comm.out_shapes)

    def body(*refs):
        cins, couts, sems = refs[:c_in], refs[c_in:c_in + c_out], refs[c_in + c_out:]
        comm.start(cins, couts, sems)
        comm.middle(cins, couts, sems)
        comm.finish(cins, couts, sems)

    return list(pl.pallas_call(
        body, name=name, out_shape=list(comm.out_shapes),
        in_specs=[_ANY] * c_in, out_specs=[_ANY] * c_out, scratch_shapes=list(comm.sem_shapes),
    )(*comm.inputs))


def _gather_exchange(shards):
    n_arr = len(shards)

    def plan(x_refs, out_refs, sems):
        send_sems, recv_sems, local_sems = sems
        x, y, c = _mesh_pos()
        me, sibling = (x, y, c), (x, y, 1 - c)
        xn, yn, diag = (1 - x, y), (x, 1 - y), (1 - x, 1 - y)

        def slot(a, px, py, pc, half=None):
            ref = out_refs[a].at[4 * px + 2 * py + pc]
            if half is None:
                return ref
            rows = shards[a].shape[0] // 2
            return ref.at[half * rows:(half + 1) * rows]

        def copy(a, k, block, to, half=None, src=None):
            dst = slot(a, *block, half)
            return pltpu.make_async_remote_copy(
                src_ref=dst if src is None else src, dst_ref=dst,
                send_sem=send_sems.at[a, k], recv_sem=recv_sems.at[a, k],
                device_id=to, device_id_type=pl.DeviceIdType.MESH)

        return types.SimpleNamespace(
            me=me, sibling=sibling, xn=xn, yn=yn, diag=diag, c=c, copy=copy,
            mine=[pltpu.make_async_copy(x_refs[a], slot(a, *me), local_sems.at[a]) for a in range(n_arr)],
            first=[cp for a in range(n_arr) for cp in (
                copy(a, 0, me, sibling, src=x_refs[a]), copy(a, 1, me, (*xn, c), src=x_refs[a]),
                copy(a, 2, me, (*yn, c), src=x_refs[a]))],
            second=lambda a: (copy(a, 3, (*xn, c), (*yn, c), half=0), copy(a, 5, (*xn, c), sibling),
                              copy(a, 4, (*yn, c), (*xn, c), half=1), copy(a, 6, (*yn, c), sibling)),
            third=lambda a: (copy(a, 7, (*diag, c), sibling, half=0), copy(a, 8, (*diag, c), sibling, half=1)))

    def start(x_refs, out_refs, sems):
        p = plan(x_refs, out_refs, sems)
        for cp in p.mine + p.first:
            cp.start()

    def middle(x_refs, out_refs, sems):
        p = plan(x_refs, out_refs, sems)
        for a in range(n_arr):
            to_yn, x_to_sib, to_xn, y_to_sib = p.second(a)
            p.copy(a, 1, (*p.xn, p.c), p.me).wait_recv()
            to_yn.start()
            x_to_sib.start()
            p.copy(a, 2, (*p.yn, p.c), p.me).wait_recv()
            to_xn.start()
            y_to_sib.start()

    def finish(x_refs, out_refs, sems):
        p = plan(x_refs, out_refs, sems)
        for a in range(n_arr):
            half0_to_sib, half1_to_sib = p.third(a)
            p.copy(a, 3, (*p.diag, p.c), p.me, half=0).wait_recv()
            half0_to_sib.start()
            p.copy(a, 4, (*p.diag, p.c), p.me, half=1).wait_recv()
            half1_to_sib.start()
        other = 1 - p.c
        for a in range(n_arr):
            p.copy(a, 0, p.sibling, p.me).wait_recv()
            p.copy(a, 5, (*p.xn, other), p.me).wait_recv()
            p.copy(a, 6, (*p.yn, other), p.me).wait_recv()
            p.copy(a, 7, (*p.diag, other), p.me, half=0).wait_recv()
            p.copy(a, 8, (*p.diag, other), p.me, half=1).wait_recv()
        for cp in p.first:
            cp.wait_send()
        for a in range(n_arr):
            for cp in p.second(a) + p.third(a):
                cp.wait_send()
        for cp in p.mine:
            cp.wait()

    return types.SimpleNamespace(
        inputs=list(shards), start=start, middle=middle, finish=finish,
        out_shapes=[jax.ShapeDtypeStruct((N_DEV,) + s.shape, s.dtype) for s in shards],
        sem_shapes=[pltpu.SemaphoreType.DMA((n_arr, 9)), pltpu.SemaphoreType.DMA((n_arr, 9)),
                    pltpu.SemaphoreType.DMA((n_arr,))])


def _pair_exchange(grads):
    n_arr = len(grads)

    def plan(g_refs, land_refs, sems):
        send_sems, recv_sems = sems
        x, y, c = _mesh_pos()
        return [pltpu.make_async_remote_copy(
            src_ref=g_refs[a].at[2 * k + 1 - c], dst_ref=land_refs[a].at[k],
            send_sem=send_sems.at[a, k], recv_sem=recv_sems.at[a, k],
            device_id=(x, y, 1 - c), device_id_type=pl.DeviceIdType.MESH)
            for a in range(n_arr) for k in range(N_CHIP)]

    def start(g_refs, land_refs, sems):
        for cp in plan(g_refs, land_refs, sems):
            cp.start()

    def finish(g_refs, land_refs, sems):
        for cp in plan(g_refs, land_refs, sems):
            cp.wait()

    return types.SimpleNamespace(
        inputs=list(grads), start=start, middle=_no_round, finish=finish,
        out_shapes=[jax.ShapeDtypeStruct((N_CHIP,) + g.shape[1:], g.dtype) for g in grads],
        sem_shapes=[pltpu.SemaphoreType.DMA((n_arr, N_CHIP)), pltpu.SemaphoreType.DMA((n_arr, N_CHIP))])


def _chip_exchange(parts):
    n_arr = len(parts)

    def plan(p_refs, land_refs, sems):
        send_sems, recv_sems, local_sems = sems
        x, y, c = _mesh_pos()
        my_chip = 2 * x + y
        chips = [(1 - x, y), (x, 1 - y), (1 - x, 1 - y)]
        local = [pltpu.make_async_copy(p_refs[a].at[my_chip], land_refs[a].at[my_chip], local_sems.at[a])
                 for a in range(n_arr)]

        def copy(a, k, src_slot, dst_slot, px, py):
            return pltpu.make_async_remote_copy(
                src_ref=p_refs[a].at[src_slot], dst_ref=land_refs[a].at[dst_slot],
                send_sem=send_sems.at[a, k], recv_sem=recv_sems.at[a, k],
                device_id=(px, py, c), device_id_type=pl.DeviceIdType.MESH)

        sends = [copy(a, k, 2 * px + py, my_chip, px, py) for a in range(n_arr) for k, (px, py) in enumerate(chips)]
        arrivals = [copy(a, k, my_chip, 2 * px + py, px, py) for a in range(n_arr)
                    for k, (px, py) in enumerate(chips)]
        return local, sends, arrivals

    def start(p_refs, land_refs, sems):
        local, sends, _ = plan(p_refs, land_refs, sems)
        for cp in local + sends:
            cp.start()

    def finish(p_refs, land_refs, sems):
        local, sends, arrivals = plan(p_refs, land_refs, sems)
        for cp in arrivals:
            cp.wait_recv()
        for cp in sends:
            cp.wait_send()
        for cp in local:
            cp.wait()

    return types.SimpleNamespace(
        inputs=list(parts), start=start, middle=_no_round, finish=finish,
        out_shapes=[jax.ShapeDtypeStruct(p.shape, p.dtype) for p in parts],
        sem_shapes=[pltpu.SemaphoreType.DMA((n_arr, 3)), pltpu.SemaphoreType.DMA((n_arr, 3)),
                    pltpu.SemaphoreType.DMA((n_arr,))])


def _row_tile(r, cap=640):
    best = None
    for cand in range(16, min(r, cap) + 1, 16):
        if r % cand == 0:
            best = cand
    return best if best is not None else r


def _pair_sum(g, landed, core, name):
    _, r, c_dim = g.shape
    tr = _row_tile(r)

    def body(core_ref, mine_ref, theirs_ref, o_ref):
        o_ref[0] = (mine_ref[0].astype(F32) + theirs_ref[0].astype(F32)).astype(o_ref.dtype)

    return pl.pallas_call(
        body, name=name,
        grid_spec=pltpu.PrefetchScalarGridSpec(
            num_scalar_prefetch=1, grid=(N_CHIP, r // tr),
            in_specs=[pl.BlockSpec((1, tr, c_dim), lambda k, i, core_ref: (2 * k + core_ref[0], i, 0)),
                      pl.BlockSpec((1, tr, c_dim), lambda k, i, core_ref: (k, i, 0))],
            out_specs=pl.BlockSpec((1, tr, c_dim), lambda k, i, core_ref: (k, i, 0))),
        out_shape=jax.ShapeDtypeStruct((N_CHIP, r, c_dim), g.dtype),
        compiler_params=_cparams(2),
    )(core, g, landed)


def _sum_slots(parts, name):
    n, r, c_dim = parts.shape
    tr = _row_tile(r)

    def body(p_ref, o_ref):
        acc = p_ref[0].astype(F32)
        for k in range(1, n):
            acc = acc + p_ref[k].astype(F32)
        o_ref[...] = acc

    return pl.pallas_call(
        body, grid=(r // tr,), name=name,
        in_specs=[pl.BlockSpec((n, tr, c_dim), lambda i: (0, i, 0))],
        out_specs=_row_spec(tr, c_dim),
        out_shape=jax.ShapeDtypeStruct((r, c_dim), F32),
        compiler_params=_cparams(),
    )(parts)


GAINS = ("g_ffn1", "g_mix", "g_cross", "g_mem", "g_ffn2", "g_final")
SMALL = GAINS + ("b_gate", "conv_w")
SMALL_R = 16
WEIGHT_ORDER = ("g_ffn1", "w_ffn1_gu", "w_ffn1_down", "g_mix", "w_in", "b_gate", "conv_w", "w_conv_out",
                "w_attn_out", "w_o", "g_cross", "g_mem", "w_cq", "w_ckv", "w_co", "g_ffn2", "w_ffn2_gu",
                "w_ffn2_down", "g_final")
GU_NAMES = ("w_ffn1_gu", "w_ffn2_gu")


def _pack_small(vals, conv_rows):
    rows = [vals[n].reshape(1, D) for n in GAINS] + [vals["b_gate"].reshape(2, D), conv_rows.reshape(CONV_K, D)]
    used = len(GAINS) + 2 + CONV_K
    return jnp.concatenate(rows + [jnp.zeros((SMALL_R - used, D), F32)], axis=0)


def _unpack_small(buf):
    out = {n: buf[k] for k, n in enumerate(GAINS)}
    out["b_gate"] = buf[6:8].reshape(2 * D)
    out["conv_w"] = buf[8:8 + CONV_K]
    return out


def _exchange_shards(wts):
    out = {n: jnp.pad(wts[n].T.astype(BF16), ((0, FF_PAD - FF_BLK), (0, 0))) for n in GU_NAMES}
    for n in ("w_ckv", "w_in", "w_ffn1_down", "w_ffn2_down"):
        out[n] = wts[n].astype(BF16)
    out["mix"] = jnp.concatenate([wts[n].astype(BF16) for n in MIX_MATS], axis=0)
    out["cross"] = jnp.concatenate([wts[n].astype(BF16) for n in CROSS_MATS], axis=0)
    return out


def _by_device(dw):
    return dw.reshape(N_DEV, SQ_ROWS, D)


def _reduce_group(grads, landed, core, names):
    return [_pair_sum(g, l, core, "grads_pair_sum_" + n) for g, l, n in zip(grads, landed, names)]


def _step(x, mem, target, sh, conv_pad, gains, b_gate, core):
    wg1, wd1, conv_all = _run_exchange(_gather_exchange([sh["w_ffn1_gu"], sh["w_ffn1_down"], conv_pad]), "gather_ffn1")
    conv_w = conv_all[:, :CONV_K, :].transpose(1, 0, 2).reshape(CONV_K, D)
    (n1, gate1, up1, act1, h1), (w_in,) = _ffn_fwd(
        x, gains["g_ffn1"], wg1, wd1, "ffn1_fwd", comm=_gather_exchange([sh["w_in"]]))
    (u, pcg, qkv), (w_mix,) = _inproj_fwd(h1, gains["g_mix"], w_in, "inproj_fwd", comm=_gather_exchange([sh["mix"]]))
    yc = _conv_fwd(pcg, conv_w, "conv_fwd")
    (ysb, ctot), (w_cross, w_ckv, wg2, wd2) = _sb_fwd(
        qkv, "sb_fwd", comm=_gather_exchange([sh["cross"], sh["w_ckv"], sh["w_ffn2_gu"], sh["w_ffn2_down"]]))
    a_mix, b_mix, merged, h2 = _mix_out_fwd(yc, ysb, pcg, b_gate, h1, w_mix, "mix_out_fwd")
    mn, kv = _memkv_fwd(mem, gains["g_mem"], w_ckv, "memkv_fwd")
    hn, qx, o_x, h3 = _cross_fwd(h2, gains["g_cross"], kv, w_cross, "cross_fwd")
    (n4, gate2, up2, act2, h4), _ = _ffn_fwd(h3, gains["g_ffn2"], wg2, wd2, "ffn2_fwd")
    loss, dh4, dg_final = _loss_bwd(h4, gains["g_final"], target, "loss_bwd")

    gs = {"g_final": dg_final}
    (dgu2, dh4b, dh3, gs["g_ffn2"]), _ = _ffn_bwd(dh4, h3, gains["g_ffn2"], gate2, up2, wg2, wd2, "ffn2_bwd")
    grads_a = [_mm_tn_rows(dgu2, n4, FF_PAD, "dw_ffn2_gu"),
               _mm_tn_rows(act2, dh4b, FF_BLK, "dw_ffn2_down").reshape(N_DEV, DOWN_ROWS, D)]
    names_a = ["w_ffn2_gu", "w_ffn2_down"]
    (dh3b, dqx, dkv, dh2, gs["g_cross"]), landed_a = _cross_bwd(
        dh3, h2, gains["g_cross"], qx, kv, w_cross, "cross_bwd", comm=_pair_exchange(grads_a))
    sums_a = _reduce_group(grads_a, landed_a, core, names_a)
    grads_b = [_mm_tn_cols(mn, dkv, "dw_ckv"),
               jnp.concatenate([_by_device(_mm_tn(hn, dqx, "dw_cq")), _by_device(_mm_tn(o_x, dh3b, "dw_co"))], axis=1)]
    names_b = ["w_ckv", "cross"]
    gs["g_mem"] = _memkv_bwd(dkv, mem, gains["g_mem"], w_ckv, "memkv_bwd")
    (dh2b, da_mix, db_mix, dgp, dyc, dysb, gs["b_gate"]), landed_b = _mix_out_bwd(
        dh2, a_mix, b_mix, pcg, b_gate, w_mix, "mix_out_bwd", comm=_pair_exchange(grads_b))
    sums_b = _reduce_group(grads_b, landed_b, core, names_b)
    grads_c = [jnp.concatenate([_by_device(_mm_tn(yc, da_mix, "dw_conv_out")),
                                _by_device(_mm_tn(ysb, db_mix, "dw_attn_out")),
                                _by_device(_mm_tn(merged, dh2b, "dw_o"))], axis=1)]
    landed_c = _run_exchange(_pair_exchange(grads_c), "grads_to_sibling_mix")
    sums_c = _reduce_group(grads_c, landed_c, core, ["mix"])
    (dq, dkv_sb), parts_abc = _sb_bwd(qkv, dysb, ctot, "sb_bwd", comm=_chip_exchange(sums_a + sums_b + sums_c))
    dconv, gs["conv_w"] = _conv_bwd(pcg, conv_w, dyc, "conv_bwd")
    grads_d = [jnp.concatenate(
        [_mm_tn_cols(u, dconv, "dw_in_conv"), _mm_tn(u, dq, "dw_in_q")[None], _mm_tn_cols(u, dkv_sb, "dw_in_kv"),
         _mm_tn_cols(u, dgp, "dw_in_gates")], axis=0)]
    (dh1, gs["g_mix"]), landed_d = _inproj_bwd(dconv, dq, dkv_sb, dgp, w_in, h1, gains["g_mix"], dh2, "inproj_bwd",
                                               comm=_pair_exchange(grads_d))
    sums_d = _reduce_group(grads_d, landed_d, core, ["w_in"])
    (dgu1, dh1b, dx, gs["g_ffn1"]), parts_d = _ffn_bwd(dh1, x, gains["g_ffn1"], gate1, up1, wg1, wd1, "ffn1_bwd",
                                                       comm=_chip_exchange(sums_d))
    grads_e = [_mm_tn_rows(dgu1, n1, FF_PAD, "dw_ffn1_gu"),
               _mm_tn_rows(act1, dh1b, FF_BLK, "dw_ffn1_down").reshape(N_DEV, DOWN_ROWS, D)]
    names_e = ["w_ffn1_gu", "w_ffn1_down"]
    landed_e = _run_exchange(_pair_exchange(grads_e), "grads_to_sibling_ffn1")
    parts_e = _run_exchange(_chip_exchange(_reduce_group(grads_e, landed_e, core, names_e)), "grads_to_chips_ffn1")

    names = names_a + names_b + ["mix"] + ["w_in"] + names_e
    return loss, dx, dict(zip(names, parts_abc + parts_d + parts_e)), gs


def kernel(x, mem, g_ffn1, w_ffn1_gu, w_ffn1_down, g_mix, w_in, b_gate, conv_w, w_conv_out, w_attn_out, w_o, g_cross, g_mem, w_cq, w_ckv, w_co, g_ffn2, w_ffn2_gu, w_ffn2_down, g_final, loss_target, m_g_ffn1, m_w_ffn1_gu, m_w_ffn1_down, m_g_mix, m_w_in, m_b_gate, m_conv_w, m_w_conv_out, m_w_attn_out, m_w_o, m_g_cross, m_g_mem, m_w_cq, m_w_ckv, m_w_co, m_g_ffn2, m_w_ffn2_gu, m_w_ffn2_down, m_g_final, v_g_ffn1, v_w_ffn1_gu, v_w_ffn1_down, v_g_mix, v_w_in, v_b_gate, v_conv_w, v_w_conv_out, v_w_attn_out, v_w_o, v_g_cross, v_g_mem, v_w_cq, v_w_ckv, v_w_co, v_g_ffn2, v_w_ffn2_gu, v_w_ffn2_down, v_g_final):
    args = locals()
    wts = {n: args[n] for n in WEIGHT_ORDER}
    mom1 = {n: args["m_" + n] for n in WEIGHT_ORDER}
    mom2 = {n: args["v_" + n] for n in WEIGHT_ORDER}
    cx, cy, cc = _mesh_pos()
    dev = 4 * cx + 2 * cy + cc
    conv_cols = D // N_DEV

    conv_pad = jnp.concatenate([conv_w, jnp.zeros((SMALL_R - CONV_K, conv_cols), F32)], axis=0)
    gains = {n: wts[n].reshape(1, D) for n in GAINS}
    loss8, dx, parts, gs = _step(x[0], mem[0], loss_target[0], _exchange_shards(wts), conv_pad, gains,
                                 b_gate.reshape(1, 2 * D), cc.reshape(1).astype(jnp.int32))
    loss = lax.psum(loss8[0, 0], MESH_AXES)

    grads, delta, new_m, new_v = {}, {}, {}, {}

    def update(n, buf, row_block=0, transposed=False):
        w, m1, m2 = wts[n], mom1[n], mom2[n]
        if transposed:
            w, m1, m2 = w.T, m1.T, m2.T
        res = _adamw(w, parts[buf], m1, m2, "adamw_" + n, row_block)
        if transposed:
            res = [r.T for r in res]
        grads[n], delta[n], new_m[n], new_v[n] = res

    for n in GU_NAMES:
        update(n, n, transposed=True)
    for n in ("w_ckv", "w_in", "w_ffn1_down", "w_ffn2_down"):
        update(n, n)
    for k, n in enumerate(MIX_MATS):
        update(n, "mix", k)
    for k, n in enumerate(CROSS_MATS):
        update(n, "cross", k)

    gs_rows = {n: gs[n] for n in GAINS + ("b_gate",)}
    small_all = _run_exchange(_gather_exchange([_pack_small(gs_rows, gs["conv_w"][:CONV_K])]), "gather_small_grads")[0]
    grad_small = _unpack_small(_sum_slots(small_all, "small_grads_sum"))
    grad_small["conv_w"] = lax.dynamic_slice_in_dim(grad_small["conv_w"], dev * conv_cols, conv_cols, axis=1)
    grads.update(grad_small)

    def small_buf(vals):
        return _pack_small(vals, jnp.concatenate([vals["conv_w"], jnp.zeros((CONV_K, D - conv_cols), F32)], axis=1))

    _, d_s, m_s, v_s = _adamw(small_buf(wts), small_buf(grads)[None], small_buf(mom1), small_buf(mom2), "adamw_small")
    for res, buf in ((delta, d_s), (new_m, m_s), (new_v, v_s)):
        un = _unpack_small(buf)
        for n in GAINS + ("b_gate",):
            res[n] = un[n]
        res["conv_w"] = un["conv_w"][:, :conv_cols]

    return (loss, dx[None], *[grads[n] for n in WEIGHT_ORDER], *[delta[n] for n in WEIGHT_ORDER],
            *[new_m[n] for n in WEIGHT_ORDER], *[new_v[n] for n in WEIGHT_ORDER])
```

```python
import types

import jax
import jax.numpy as jnp
from jax import lax
from jax.experimental import pallas as pl
from jax.experimental.pallas import tpu as pltpu

F32 = jnp.float32
BF16 = jnp.bfloat16

D = 1024
DFF = 2816
SB_H = 8
SB_DH = 128
X_H = 4
X_DH = 256
CONV_K = 3
RMS_EPS = 1e-6
N_DEV = 8
N_CHIP = 4
SQ_ROWS = D // N_DEV

ADAM_LR = 0.001
ADAM_B1 = 0.9
ADAM_B2 = 0.999
ADAM_EPS = 1e-08
ADAM_WD = 0.01
ADAM_STEP = 10

TM = 256
TQ = 512
TK = 256
SB_HPS = 2
VMEM_LIMIT = 56 << 20

FF_BLK = DFF // 4
FF_PAD = 768
FF_SUB = 256
DOWN_ROWS = DFF // N_DEV

MIX_MATS = ("w_conv_out", "w_attn_out", "w_o")
CROSS_MATS = ("w_cq", "w_co")

MESH_AXES = ("x", "y", "c")
_ANY = pl.BlockSpec(memory_space=pl.ANY)


def _cparams(n_axes=1):
    return pltpu.CompilerParams(
        dimension_semantics=("arbitrary",) * n_axes, vmem_limit_bytes=VMEM_LIMIT)


def _row_spec(tm, n):
    return pl.BlockSpec((tm, n), lambda i: (i, 0))


def _blk_row_spec(nb, tm, n):
    return pl.BlockSpec((nb, tm, n), lambda i: (0, i, 0))


def _const_spec(shape):
    zeros = (0,) * len(shape)
    return pl.BlockSpec(shape, lambda i: zeros)


def _dot(a, b):
    return jnp.dot(a, b, preferred_element_type=F32)


def _dot_nt(a, b):
    return lax.dot_general(a, b, (((1,), (1,)), ((), ())), preferred_element_type=F32)


def _dot_tn(a, b):
    return lax.dot_general(a, b, (((0,), (0,)), ((), ())), preferred_element_type=F32)


def _sigmoid(x):
    return 1.0 / (1.0 + jnp.exp(-x))


def _call(body, operands, *, grid, in_specs, out_specs, out_shape, scratch_shapes, name, comm=None):
    n_in, n_out, n_sc = len(in_specs), len(out_specs), len(scratch_shapes)
    if comm is None:
        outs = pl.pallas_call(
            body, grid=grid, name=name, in_specs=in_specs, out_specs=out_specs, out_shape=out_shape,
            scratch_shapes=scratch_shapes, compiler_params=_cparams(len(grid)))(*operands)
        return list(outs), []
    c_in, c_out, c_sem = len(comm.inputs), len(comm.out_shapes), len(comm.sem_shapes)

    def hosted(*refs):
        bounds = [0, n_in, c_in, n_out, c_out, n_sc, c_sem]
        parts, pos = [], 0
        for k in bounds[1:]:
            parts.append(refs[pos:pos + k])
            pos += k
        ins, cins, outs, couts, scr, sems = parts
        step, n_steps = pl.program_id(0), grid[0]
        for ax in range(1, len(grid)):
            step, n_steps = step * grid[ax] + pl.program_id(ax), n_steps * grid[ax]

        @pl.when(step == 0)
        def _():
            comm.start(cins, couts, sems)

        @pl.when(step == (2 * n_steps) // 3)
        def _():
            comm.middle(cins, couts, sems)

        body(*ins, *outs, *scr)

        @pl.when(step == n_steps - 1)
        def _():
            comm.finish(cins, couts, sems)

    res = pl.pallas_call(
        hosted, grid=grid, name=name, in_specs=list(in_specs) + [_ANY] * c_in,
        out_specs=list(out_specs) + [_ANY] * c_out, out_shape=list(out_shape) + list(comm.out_shapes),
        scratch_shapes=list(scratch_shapes) + list(comm.sem_shapes),
        compiler_params=_cparams(len(grid)))(*operands, *comm.inputs)
    return list(res[:n_out]), list(res[n_out:])


def _load_resident(step, pairs, sems):
    @pl.when(step == 0)
    def _():
        copies = [pltpu.make_async_copy(src, dst, sems.at[k]) for k, (src, dst) in enumerate(pairs)]
        for cp in copies:
            cp.start()
        for cp in copies:
            cp.wait()


def _square_pairs(buf_hbm, index, dst):
    off = index * SQ_ROWS
    return [(buf_hbm.at[d, off:off + SQ_ROWS, :], dst.at[d * SQ_ROWS:(d + 1) * SQ_ROWS, :]) for d in range(N_DEV)]


def _down_pairs(wd_hbm, dst):
    return [(wd_hbm.at[d], dst.at[d // 2, (d % 2) * DOWN_ROWS:(d % 2 + 1) * DOWN_ROWS, :]) for d in range(N_DEV)]


def _zero_down_pad(step, dst):
    @pl.when(step == 0)
    def _():
        dst[:, FF_BLK:, :] = jnp.zeros((4, FF_PAD - FF_BLK, D), BF16)


def _rms_fwd_tile(xt, g):
    r = lax.rsqrt(jnp.mean(xt * xt, axis=-1, keepdims=True) + RMS_EPS)
    return (xt * r) * g


def _rms_bwd_tile(xt, g, dn):
    r = lax.rsqrt(jnp.mean(xt * xt, axis=-1, keepdims=True) + RMS_EPS)
    xhat = xt * r
    dxhat = dn * g
    dx = r * (dxhat - xhat * jnp.mean(dxhat * xhat, axis=-1, keepdims=True))
    dg = jnp.sum(dn * xhat, axis=0, keepdims=True)
    return dx, dg


def _accumulate(ref, step, value):
    @pl.when(step == 0)
    def _():
        ref[...] = value

    @pl.when(step != 0)
    def _():
        ref[...] = ref[...] + value


def _ffn_fwd(x, g, wgu, wd, name, comm=None):
    t = x.shape[0]

    def body(x_ref, g_ref, wgu_hbm, wd_hbm, n_ref, gate_ref, up_ref, act_ref, h_ref, wgu_v, wd_v, sems):
        step = pl.program_id(0)
        _zero_down_pad(step, wd_v)
        _load_resident(step, [(wgu_hbm, wgu_v)] + _down_pairs(wd_hbm, wd_v), sems)
        xt = x_ref[...]
        n = _rms_fwd_tile(xt, g_ref[...]).astype(BF16)
        n_ref[...] = n
        acc = jnp.zeros((TM, D), F32)
        for j in range(4):
            for s in range(FF_PAD // FF_SUB):
                lo, hi = s * FF_SUB, (s + 1) * FF_SUB
                gt = _dot_nt(n, wgu_v[j, lo:hi, :])
                ut = _dot_nt(n, wgu_v[4 + j, lo:hi, :])
                gate_ref[j, :, lo:hi] = gt.astype(BF16)
                up_ref[j, :, lo:hi] = ut.astype(BF16)
                act_ref[j, :, lo:hi] = ((gt * _sigmoid(gt)) * ut).astype(BF16)
            acc = acc + _dot(act_ref[j], wd_v[j])
        h_ref[...] = xt + 0.5 * acc

    ff = jax.ShapeDtypeStruct((4, t, FF_PAD), BF16)
    return _call(
        body, (x, g, wgu, wd), grid=(t // TM,), name=name, comm=comm,
        in_specs=[_row_spec(TM, D), _const_spec((1, D)), _ANY, _ANY],
        out_specs=[_row_spec(TM, D)] + [_blk_row_spec(4, TM, FF_PAD)] * 3 + [_row_spec(TM, D)],
        out_shape=[jax.ShapeDtypeStruct((t, D), BF16), ff, ff, ff, jax.ShapeDtypeStruct((t, D), F32)],
        scratch_shapes=[pltpu.VMEM((N_DEV, FF_PAD, D), BF16), pltpu.VMEM((4, FF_PAD, D), BF16),
                        pltpu.SemaphoreType.DMA((1 + N_DEV,))])


def _ffn_bwd(dh, xin, g, gate, up, wgu, wd, name, comm=None):
    t = dh.shape[0]

    def body(dh_ref, x_ref, g_ref, gate_ref, up_ref, wgu_hbm, wd_hbm,
             dgu_ref, dhb_ref, dx_ref, dg_ref, wgu_v, wd_v, sems):
        step = pl.program_id(0)
        _zero_down_pad(step, wd_v)
        _load_resident(step, [(wgu_hbm, wgu_v)] + _down_pairs(wd_hbm, wd_v), sems)
        dht = dh_ref[...]
        dhb = (0.5 * dht).astype(BF16)
        dhb_ref[...] = dhb
        dn = jnp.zeros((TM, D), F32)
        for j in range(4):
            for s in range(FF_PAD // FF_SUB):
                lo, hi = s * FF_SUB, (s + 1) * FF_SUB
                da = _dot_nt(dhb, wd_v[j, lo:hi, :])
                gt = gate_ref[j, :, lo:hi].astype(F32)
                ut = up_ref[j, :, lo:hi].astype(F32)
                sg = _sigmoid(gt)
                dgt = (da * ut * (sg * (1.0 + gt * (1.0 - sg)))).astype(BF16)
                dut = (da * (gt * sg)).astype(BF16)
                dgu_ref[j, :, lo:hi] = dgt
                dgu_ref[4 + j, :, lo:hi] = dut
            dn = dn + _dot(dgu_ref[j], wgu_v[j]) + _dot(dgu_ref[4 + j], wgu_v[4 + j])
        dx, dg = _rms_bwd_tile(x_ref[...], g_ref[...], dn)
        dx_ref[...] = dht + dx
        _accumulate(dg_ref, step, dg)

    return _call(
        body, (dh, xin, g, gate, up, wgu, wd), grid=(t // TM,), name=name, comm=comm,
        in_specs=[_row_spec(TM, D), _row_spec(TM, D), _const_spec((1, D)), _blk_row_spec(4, TM, FF_PAD),
                  _blk_row_spec(4, TM, FF_PAD), _ANY, _ANY],
        out_specs=[_blk_row_spec(N_DEV, TM, FF_PAD), _row_spec(TM, D), _row_spec(TM, D), _const_spec((1, D))],
        out_shape=[jax.ShapeDtypeStruct((N_DEV, t, FF_PAD), BF16), jax.ShapeDtypeStruct((t, D), BF16),
                   jax.ShapeDtypeStruct((t, D), F32), jax.ShapeDtypeStruct((1, D), F32)],
        scratch_shapes=[pltpu.VMEM((N_DEV, FF_PAD, D), BF16), pltpu.VMEM((4, FF_PAD, D), BF16),
                        pltpu.SemaphoreType.DMA((1 + N_DEV,))])


WIDE_TILES = (1024, 512, 256, 128)


def _pick_tile(n, options=(512, 256, 128)):
    for o in options:
        if n % o == 0:
            return o
    return n


def _mm_tn(a, b, name):
    k, m = a.shape
    _, n = b.shape
    tm, tn = _pick_tile(m, WIDE_TILES), _pick_tile(n)

    def body(a_ref, b_ref, o_ref):
        o_ref[...] = _dot_tn(a_ref[...].astype(BF16), b_ref[...].astype(BF16)).astype(BF16)

    return pl.pallas_call(
        body, grid=(m // tm, n // tn), name=name,
        in_specs=[pl.BlockSpec((k, tm), lambda i, j: (0, i)), pl.BlockSpec((k, tn), lambda i, j: (0, j))],
        out_specs=pl.BlockSpec((tm, tn), lambda i, j: (i, j)),
        out_shape=jax.ShapeDtypeStruct((m, n), BF16),
        compiler_params=_cparams(2),
    )(a, b)


def _mm_tn_cols(a, b, name):
    k, m = a.shape
    nb, _, n = b.shape
    tm = _pick_tile(m, WIDE_TILES)

    def body(a_ref, b_ref, o_ref):
        o_ref[0] = _dot_tn(a_ref[...].astype(BF16), b_ref[0].astype(BF16)).astype(BF16)

    return pl.pallas_call(
        body, grid=(nb, m // tm), name=name,
        in_specs=[pl.BlockSpec((k, tm), lambda j, i: (0, i)), pl.BlockSpec((1, k, n), lambda j, i: (j, 0, 0))],
        out_specs=pl.BlockSpec((1, tm, n), lambda j, i: (j, i, 0)),
        out_shape=jax.ShapeDtypeStruct((nb, m, n), BF16),
        compiler_params=_cparams(2),
    )(a, b)


def _mm_tn_rows(a, b, keep, name):
    nb, k, m = a.shape
    _, n = b.shape
    tn = _pick_tile(n, WIDE_TILES)

    def body(a_ref, b_ref, o_ref):
        o_ref[0] = _dot_tn(a_ref[0], b_ref[...])[:keep].astype(BF16)

    return pl.pallas_call(
        body, grid=(nb, n // tn), name=name,
        in_specs=[pl.BlockSpec((1, k, m), lambda j, i: (j, 0, 0)), pl.BlockSpec((k, tn), lambda j, i: (0, i))],
        out_specs=pl.BlockSpec((1, keep, tn), lambda j, i: (j, 0, i)),
        out_shape=jax.ShapeDtypeStruct((nb, keep, n), BF16),
        compiler_params=_cparams(2),
    )(a, b)


PCG_W = 5 * D
QKV_W = 3 * D
PROJ_SUB = 512


def _inproj_fwd(h, g, w_in, name, comm=None):
    t = h.shape[0]

    def body(h_ref, g_ref, w_hbm, u_ref, pcg_ref, qkv_ref, w_v, sems):
        _load_resident(pl.program_id(0), [(w_hbm, w_v)], sems)
        u = _rms_fwd_tile(h_ref[...], g_ref[...]).astype(BF16)
        u_ref[...] = u
        for blk in range(N_DEV):
            for s in range(D // PROJ_SUB):
                lo, hi = s * PROJ_SUB, (s + 1) * PROJ_SUB
                p = _dot(u, w_v[blk, :, lo:hi])
                if blk < 3:
                    pcg_ref[:, blk * D + lo:blk * D + hi] = p
                elif blk < 6:
                    qkv_ref[:, (blk - 3) * D + lo:(blk - 3) * D + hi] = p.astype(BF16)
                else:
                    pcg_ref[:, (blk - 3) * D + lo:(blk - 3) * D + hi] = p

    return _call(
        body, (h, g, w_in), grid=(t // TM,), name=name, comm=comm,
        in_specs=[_row_spec(TM, D), _const_spec((1, D)), _ANY],
        out_specs=[_row_spec(TM, D), _row_spec(TM, PCG_W), _row_spec(TM, QKV_W)],
        out_shape=[jax.ShapeDtypeStruct((t, D), BF16), jax.ShapeDtypeStruct((t, PCG_W), F32),
                   jax.ShapeDtypeStruct((t, QKV_W), BF16)],
        scratch_shapes=[pltpu.VMEM((N_DEV, D, D), BF16), pltpu.SemaphoreType.DMA((1,))])


CONV_CW = 256


def _shift_down(v, k, rows):
    return jnp.where(rows >= k, pltpu.roll(v, k, 0), 0.0)


def _shift_up(v, k, rows, t):
    return jnp.where(rows < t - k, pltpu.roll(v, t - k, 0), 0.0)


def _col_spec(t, cw, off):
    return pl.BlockSpec((t, cw), lambda j: (0, j + off))


def _conv_fwd(pcg, conv_w, name):
    t = pcg.shape[0]
    nb = D // CONV_CW

    def body(cb_ref, cc_ref, cx_ref, w_ref, y_ref):
        rows = lax.broadcasted_iota(jnp.int32, (t, CONV_CW), 0)
        xc = cc_ref[...] * cx_ref[...]
        conv = (w_ref[0:1, :] * _shift_down(xc, 2, rows) + w_ref[1:2, :] * _shift_down(xc, 1, rows)
                + w_ref[2:3, :] * xc)
        y_ref[...] = (cb_ref[...] * conv).astype(BF16)

    return pl.pallas_call(
        body, grid=(nb,), name=name,
        in_specs=[_col_spec(t, CONV_CW, 0), _col_spec(t, CONV_CW, nb), _col_spec(t, CONV_CW, 2 * nb),
                  pl.BlockSpec((CONV_K, CONV_CW), lambda j: (0, j))],
        out_specs=_col_spec(t, CONV_CW, 0),
        out_shape=jax.ShapeDtypeStruct((t, D), BF16),
        compiler_params=_cparams(),
    )(pcg, pcg, pcg, conv_w)


def _conv_bwd(pcg, conv_w, dyc, name):
    t = pcg.shape[0]
    nb = D // CONV_CW

    def body(cb_ref, cc_ref, cx_ref, w_ref, dy_ref, dc_ref, dw_ref):
        rows = lax.broadcasted_iota(jnp.int32, (t, CONV_CW), 0)
        cc, cx = cc_ref[...], cx_ref[...]
        xc = cc * cx
        x1 = _shift_down(xc, 1, rows)
        x2 = _shift_down(xc, 2, rows)
        w0, w1, w2 = w_ref[0:1, :], w_ref[1:2, :], w_ref[2:3, :]
        conv = w0 * x2 + w1 * x1 + w2 * xc
        dy = dy_ref[...]
        dc_ref[0] = (dy * conv).astype(BF16)
        dconv = dy * cb_ref[...]
        dw_ref[...] = jnp.zeros((8, CONV_CW), F32)
        dw_ref[0:1, :] = jnp.sum(dconv * x2, axis=0, keepdims=True)
        dw_ref[1:2, :] = jnp.sum(dconv * x1, axis=0, keepdims=True)
        dw_ref[2:3, :] = jnp.sum(dconv * xc, axis=0, keepdims=True)
        dxc = w2 * dconv + w1 * _shift_up(dconv, 1, rows, t) + w0 * _shift_up(dconv, 2, rows, t)
        dc_ref[1] = (dxc * cx).astype(BF16)
        dc_ref[2] = (dxc * cc).astype(BF16)

    return pl.pallas_call(
        body, grid=(nb,), name=name,
        in_specs=[_col_spec(t, CONV_CW, 0), _col_spec(t, CONV_CW, nb), _col_spec(t, CONV_CW, 2 * nb),
                  pl.BlockSpec((CONV_K, CONV_CW), lambda j: (0, j)), _col_spec(t, CONV_CW, 0)],
        out_specs=[pl.BlockSpec((3, t, CONV_CW), lambda j: (0, 0, j)), pl.BlockSpec((8, CONV_CW), lambda j: (0, j))],
        out_shape=[jax.ShapeDtypeStruct((3, t, D), BF16), jax.ShapeDtypeStruct((8, D), F32)],
        compiler_params=_cparams(),
    )(pcg, pcg, pcg, conv_w, dyc)


def _tri2(cond):
    rr = lax.broadcasted_iota(jnp.int32, (2 * TK, TK), 0) & (TK - 1)
    cc = lax.broadcasted_iota(jnp.int32, (2 * TK, TK), 1)
    return cond(rr, cc).astype(BF16)


def _causal(shift, row0=0):
    rr = lax.broadcasted_iota(jnp.int32, (TQ - row0, TK), 0) + row0
    cc = lax.broadcasted_iota(jnp.int32, (TQ - row0, TK), 1)
    return cc + shift < rr


def _cumdot(v, tri2):
    hi = v.astype(BF16)
    lo = (v - hi.astype(F32)).astype(BF16)
    return _dot(jnp.concatenate([hi, lo], axis=1), tri2)


def _log_1m_beta(z):
    return -(jnp.maximum(z, 0.0) + jnp.log(1.0 + jnp.exp(-jnp.abs(z))))


def _sb_specs(t):
    g = SB_H // SB_HPS
    w = SB_HPS * SB_DH
    q_spec = pl.BlockSpec((TQ, w), lambda h, i: (i, h))
    k_spec = pl.BlockSpec((t, w), lambda h, i: (0, g + h))
    v_spec = pl.BlockSpec((t, w), lambda h, i: (0, 2 * g + h))
    ct_spec = pl.BlockSpec((SB_HPS, TQ, 1), lambda h, i: (h, i, 0))
    return g, w, q_spec, k_spec, v_spec, ct_spec


def _sb_fwd(qkv, name, comm=None):
    t = qkv.shape[0]
    scale = SB_DH ** -0.5
    g, w, q_spec, k_spec, v_spec, ct_spec = _sb_specs(t)

    def body(q_ref, k_ref, v_ref, y_ref, ct_ref):
        i = pl.program_id(1)
        later = _tri2(lambda j, s: j > s)
        n_diag = TQ // TK

        def block(j, carry, shift):
            off = pl.multiple_of(j * TK, TK)
            zs, ms = [], []
            for hd in range(SB_HPS):
                cols = slice(hd * SB_DH, (hd + 1) * SB_DH)
                z = _dot_nt(q_ref[:, cols], k_ref[pl.ds(off, TK), cols]) * scale
                m = _log_1m_beta(z)
                if shift is not None:
                    m = jnp.where(_causal(shift), m, 0.0)
                zs.append(z)
                ms.append(m)
            after = _cumdot(jnp.concatenate(ms, axis=0), later)
            out = []
            for hd in range(SB_HPS):
                acc, c_sum = carry[hd]
                cols = slice(hd * SB_DH, (hd + 1) * SB_DH)
                a = jnp.exp((ms[hd] + zs[hd]) + (c_sum + after[hd * TQ:(hd + 1) * TQ]))
                if shift is not None:
                    a = jnp.where(_causal(shift), a, 0.0)
                out.append((acc + _dot(a.astype(BF16), v_ref[pl.ds(off, TK), cols]),
                            c_sum + jnp.sum(ms[hd], axis=1, keepdims=True)))
            return tuple(out)

        carry = tuple((jnp.zeros((TQ, SB_DH), F32), jnp.zeros((TQ, 1), F32)) for _ in range(SB_HPS))
        for d in reversed(range(n_diag)):
            carry = block(i * n_diag + d, carry, d * TK)
        carry = lax.fori_loop(0, i * n_diag, lambda jj, c: block(i * n_diag - 1 - jj, c, None), carry)
        for hd in range(SB_HPS):
            y_ref[:, hd * SB_DH:(hd + 1) * SB_DH] = carry[hd][0].astype(BF16)
            ct_ref[hd] = carry[hd][1]

    return _call(
        body, (qkv, qkv, qkv), grid=(g, t // TQ), name=name, comm=comm,
        in_specs=[q_spec, k_spec, v_spec],
        out_specs=[q_spec, ct_spec],
        out_shape=[jax.ShapeDtypeStruct((t, D), BF16), jax.ShapeDtypeStruct((SB_H, t, 1), F32)],
        scratch_shapes=[])


def _sb_bwd(qkv, dy, ctot, name, comm=None):
    t = qkv.shape[0]
    scale = SB_DH ** -0.5
    g, w, q_spec, k_spec, v_spec, ct_spec = _sb_specs(t)
    acc_spec = pl.BlockSpec((2, t, w), lambda h, i: (0, 0, h))

    def body(q_ref, k_ref, v_ref, dy_ref, ct_ref, dq_ref, dkv_ref):
        i = pl.program_id(1)

        @pl.when(i == 0)
        def _():
            dkv_ref[...] = jnp.zeros_like(dkv_ref)

        upto = _tri2(lambda j, s: j <= s)
        n_diag = TQ // TK

        def block(j, carry, shift):
            off = pl.multiple_of(j * TK, TK)
            r0 = 0 if shift is None else shift
            nr = TQ - r0
            causal = None if shift is None else _causal(shift, r0)

            def grow(old, delta):
                return old + delta if r0 == 0 else jnp.concatenate([old[:r0], old[r0:] + delta], axis=0)

            zs, ms = [], []
            for hd in range(SB_HPS):
                cols = slice(hd * SB_DH, (hd + 1) * SB_DH)
                z = _dot_nt(q_ref[r0:, cols], k_ref[pl.ds(off, TK), cols]) * scale
                m = _log_1m_beta(z)
                if causal is not None:
                    m = jnp.where(causal, m, 0.0)
                zs.append(z)
                ms.append(m)
            m_upto = _cumdot(jnp.concatenate(ms, axis=0), upto)
            ls, a_s, es = [], [], []
            for hd in range(SB_HPS):
                cols = slice(hd * SB_DH, (hd + 1) * SB_DH)
                l = ms[hd] + zs[hd]
                a = jnp.exp(l + ((ct_ref[hd, r0:] - carry[hd][1][r0:]) - m_upto[hd * nr:(hd + 1) * nr]))
                if causal is not None:
                    a = jnp.where(causal, a, 0.0)
                ls.append(l)
                a_s.append(a)
                es.append(_dot_nt(dy_ref[r0:, cols], v_ref[pl.ds(off, TK), cols]) * a)
            e_upto = _dot(jnp.concatenate(es, axis=0).astype(BF16), upto[:TK])
            out = []
            for hd in range(SB_HPS):
                dq, p_sum, e_sum = carry[hd]
                cols = slice(hd * SB_DH, (hd + 1) * SB_DH)
                e = es[hd]
                dz = e - jnp.exp(ls[hd]) * (e_sum[r0:] + e_upto[hd * nr:(hd + 1) * nr])
                if causal is not None:
                    dz = jnp.where(causal, dz, 0.0)
                dzs = (dz * scale).astype(BF16)
                dkv_ref[0, pl.ds(off, TK), cols] += _dot_tn(dzs, q_ref[r0:, cols])
                dkv_ref[1, pl.ds(off, TK), cols] += _dot_tn(a_s[hd].astype(BF16), dy_ref[r0:, cols])
                out.append((grow(dq, _dot(dzs, k_ref[pl.ds(off, TK), cols])),
                            grow(p_sum, jnp.sum(ms[hd], axis=1, keepdims=True)),
                            grow(e_sum, jnp.sum(e, axis=1, keepdims=True))))
            return tuple(out)

        zero = jnp.zeros((TQ, 1), F32)
        init = tuple((jnp.zeros((TQ, SB_DH), F32), zero, zero) for _ in range(SB_HPS))
        carry = lax.fori_loop(0, i * n_diag, lambda j, c: block(j, c, None), init)
        for d in range(n_diag):
            carry = block(i * n_diag + d, carry, d * TK)
        for hd in range(SB_HPS):
            dq_ref[:, hd * SB_DH:(hd + 1) * SB_DH] = carry[hd][0].astype(BF16)

    return _call(
        body, (qkv, qkv, qkv, dy, ctot), grid=(g, t // TQ), name=name, comm=comm,
        in_specs=[q_spec, k_spec, v_spec, q_spec, ct_spec],
        out_specs=[q_spec, acc_spec],
        out_shape=[jax.ShapeDtypeStruct((t, D), BF16), jax.ShapeDtypeStruct((2, t, D), F32)],
        scratch_shapes=[])


def _gate_specs():
    return [pl.BlockSpec((TM, D), lambda i: (i, 3)), pl.BlockSpec((TM, D), lambda i: (i, 4))]


def _mix_pairs(mix_hbm, dsts):
    pairs = []
    for index, dst in enumerate(dsts):
        pairs += _square_pairs(mix_hbm, index, dst)
    return pairs


def _mix_out_fwd(yc, ysb, pcg, b_gate, h, w_mix, name):
    t = h.shape[0]

    def body(yc_ref, ysb_ref, gc_ref, gs_ref, b_ref, h_ref, mix_hbm,
             a_ref, b_out_ref, mg_ref, h2_ref, wc_v, wa_v, wo_v, sems):
        _load_resident(pl.program_id(0), _mix_pairs(mix_hbm, (wc_v, wa_v, wo_v)), sems)
        a = _dot(yc_ref[...], wc_v[...])
        b = _dot(ysb_ref[...], wa_v[...])
        merged = (_sigmoid(gc_ref[...] + b_ref[:, :D]) * a + _sigmoid(gs_ref[...] + b_ref[:, D:]) * b).astype(BF16)
        a_ref[...] = a
        b_out_ref[...] = b
        mg_ref[...] = merged
        h2_ref[...] = h_ref[...] + _dot(merged, wo_v[...])

    return pl.pallas_call(
        body, grid=(t // TM,), name=name,
        in_specs=[_row_spec(TM, D), _row_spec(TM, D)] + _gate_specs()
                 + [_const_spec((1, 2 * D)), _row_spec(TM, D), _ANY],
        out_specs=[_row_spec(TM, D)] * 4,
        out_shape=[jax.ShapeDtypeStruct((t, D), F32), jax.ShapeDtypeStruct((t, D), F32),
                   jax.ShapeDtypeStruct((t, D), BF16), jax.ShapeDtypeStruct((t, D), F32)],
        scratch_shapes=[pltpu.VMEM((D, D), BF16)] * 3 + [pltpu.SemaphoreType.DMA((3 * N_DEV,))],
        compiler_params=_cparams(),
    )(yc, ysb, pcg, pcg, b_gate, h, w_mix)


def _mix_out_bwd(dh2, a, b, pcg, b_gate, w_mix, name, comm=None):
    t = dh2.shape[0]

    def body(dh_ref, a_ref, b_ref, gc_ref, gs_ref, bias_ref, mix_hbm,
             dhb_ref, da_ref, db_ref, dgp_ref, dyc_ref, dysb_ref, dbias_ref, wc_v, wa_v, wo_v, sems):
        step = pl.program_id(0)
        _load_resident(step, _mix_pairs(mix_hbm, (wc_v, wa_v, wo_v)), sems)
        dhb = dh_ref[...].astype(BF16)
        dhb_ref[...] = dhb
        dm = _dot_nt(dhb, wo_v[...])
        gc = _sigmoid(gc_ref[...] + bias_ref[:, :D])
        gs = _sigmoid(gs_ref[...] + bias_ref[:, D:])
        da = (dm * gc).astype(BF16)
        db = (dm * gs).astype(BF16)
        da_ref[...] = da
        db_ref[...] = db
        dgc = dm * a_ref[...] * (gc * (1.0 - gc))
        dgs = dm * b_ref[...] * (gs * (1.0 - gs))
        dgp_ref[0] = dgc.astype(BF16)
        dgp_ref[1] = dgs.astype(BF16)
        _accumulate(dbias_ref.at[:, :D], step, jnp.sum(dgc, axis=0, keepdims=True))
        _accumulate(dbias_ref.at[:, D:], step, jnp.sum(dgs, axis=0, keepdims=True))
        dyc_ref[...] = _dot_nt(da, wc_v[...])
        dysb_ref[...] = _dot_nt(db, wa_v[...]).astype(BF16)

    return _call(
        body, (dh2, a, b, pcg, pcg, b_gate, w_mix), grid=(t // TM,), name=name, comm=comm,
        in_specs=[_row_spec(TM, D)] * 3 + _gate_specs() + [_const_spec((1, 2 * D)), _ANY],
        out_specs=[_row_spec(TM, D)] * 3 + [_blk_row_spec(2, TM, D), _row_spec(TM, D), _row_spec(TM, D),
                                            _const_spec((1, 2 * D))],
        out_shape=[jax.ShapeDtypeStruct((t, D), BF16)] * 3
                  + [jax.ShapeDtypeStruct((2, t, D), BF16), jax.ShapeDtypeStruct((t, D), F32),
                     jax.ShapeDtypeStruct((t, D), BF16), jax.ShapeDtypeStruct((1, 2 * D), F32)],
        scratch_shapes=[pltpu.VMEM((D, D), BF16)] * 3 + [pltpu.SemaphoreType.DMA((3 * N_DEV,))])


def _inproj_bwd(dconv, dq, dkv, dgp, w_in, h, g, dh_res, name, comm=None):
    t = h.shape[0]

    def body(dc_ref, dq_ref, dkv_ref, dgp_ref, w_hbm, h_ref, g_ref, dres_ref, dh_ref, dg_ref, w_v, sems):
        step = pl.program_id(0)
        _load_resident(step, [(w_hbm, w_v)], sems)
        du = _dot_nt(dq_ref[...], w_v[3])
        for k in range(3):
            du = du + _dot_nt(dc_ref[k], w_v[k])
        for k in range(2):
            du = du + _dot_nt(dkv_ref[k].astype(BF16), w_v[4 + k]) + _dot_nt(dgp_ref[k], w_v[6 + k])
        dx, dg = _rms_bwd_tile(h_ref[...], g_ref[...], du)
        dh_ref[...] = dres_ref[...] + dx
        _accumulate(dg_ref, step, dg)

    return _call(
        body, (dconv, dq, dkv, dgp, w_in, h, g, dh_res), grid=(t // TM,), name=name, comm=comm,
        in_specs=[_blk_row_spec(3, TM, D), _row_spec(TM, D), _blk_row_spec(2, TM, D), _blk_row_spec(2, TM, D), _ANY,
                  _row_spec(TM, D), _const_spec((1, D)), _row_spec(TM, D)],
        out_specs=[_row_spec(TM, D), _const_spec((1, D))],
        out_shape=[jax.ShapeDtypeStruct((t, D), F32), jax.ShapeDtypeStruct((1, D), F32)],
        scratch_shapes=[pltpu.VMEM((N_DEV, D, D), BF16), pltpu.SemaphoreType.DMA((1,))])


def _memkv_fwd(mem, g, w_ckv, name):
    m = mem.shape[0]

    def body(mem_ref, g_ref, w_ref, mn_ref, kv_ref):
        mn = _rms_fwd_tile(mem_ref[...], g_ref[...]).astype(BF16)
        mn_ref[...] = mn
        for j in range(N_DEV):
            kv_ref[j] = _dot(mn, w_ref[j]).astype(BF16)

    return pl.pallas_call(
        body, grid=(1,), name=name,
        in_specs=[_const_spec((m, D)), _const_spec((1, D)), _const_spec((N_DEV, D, X_DH))],
        out_specs=[_const_spec((m, D)), _const_spec((N_DEV, m, X_DH))],
        out_shape=[jax.ShapeDtypeStruct((m, D), BF16), jax.ShapeDtypeStruct((N_DEV, m, X_DH), BF16)],
        compiler_params=_cparams(),
    )(mem, g, w_ckv)


def _memkv_bwd(dkv, mem, g, w_ckv, name):
    m = mem.shape[0]

    def body(dkv_ref, mem_ref, g_ref, w_ref, dg_ref):
        dmn = jnp.zeros((m, D), F32)
        for j in range(N_DEV):
            dmn = dmn + _dot_nt(dkv_ref[j].astype(BF16), w_ref[j])
        _, dg = _rms_bwd_tile(mem_ref[...], g_ref[...], dmn)
        dg_ref[...] = dg

    return pl.pallas_call(
        body, grid=(1,), name=name,
        in_specs=[_const_spec((N_DEV, m, X_DH)), _const_spec((m, D)), _const_spec((1, D)),
                  _const_spec((N_DEV, D, X_DH))],
        out_specs=_const_spec((1, D)),
        out_shape=jax.ShapeDtypeStruct((1, D), F32),
        compiler_params=_cparams(),
    )(dkv, mem, g, w_ckv)


def _softmax_rows(s):
    e = jnp.exp(s - jnp.max(s, axis=-1, keepdims=True))
    return e / jnp.sum(e, axis=-1, keepdims=True)


def _cross_pairs(cross_hbm, wq_v, wo_v):
    return _square_pairs(cross_hbm, 0, wq_v) + _square_pairs(cross_hbm, 1, wo_v)


def _cross_fwd(h, g, kv, w_cross, name):
    t = h.shape[0]
    m = kv.shape[1]
    scale = X_DH ** -0.5

    def body(h_ref, g_ref, kv_ref, cross_hbm, hn_ref, qx_ref, o_ref, h3_ref, wq_v, wo_v, sems):
        _load_resident(pl.program_id(0), _cross_pairs(cross_hbm, wq_v, wo_v), sems)
        ht = h_ref[...]
        hn = _rms_fwd_tile(ht, g_ref[...]).astype(BF16)
        hn_ref[...] = hn
        qx = _dot(hn, wq_v[...]).astype(BF16)
        qx_ref[...] = qx
        for hd in range(X_H):
            lo, hi = hd * X_DH, (hd + 1) * X_DH
            p = _softmax_rows(_dot_nt(qx[:, lo:hi], kv_ref[hd]) * scale)
            o_ref[:, lo:hi] = _dot(p.astype(BF16), kv_ref[X_H + hd]).astype(BF16)
        h3_ref[...] = ht + _dot(o_ref[...], wo_v[...])

    return pl.pallas_call(
        body, grid=(t // TM,), name=name,
        in_specs=[_row_spec(TM, D), _const_spec((1, D)), _const_spec((N_DEV, m, X_DH)), _ANY],
        out_specs=[_row_spec(TM, D)] * 4,
        out_shape=[jax.ShapeDtypeStruct((t, D), BF16)] * 3 + [jax.ShapeDtypeStruct((t, D), F32)],
        scratch_shapes=[pltpu.VMEM((D, D), BF16)] * 2 + [pltpu.SemaphoreType.DMA((2 * N_DEV,))],
        compiler_params=_cparams(),
    )(h, g, kv, w_cross)


def _cross_bwd(dh3, h, g, qx, kv, w_cross, name, comm=None):
    t = h.shape[0]
    m = kv.shape[1]
    scale = X_DH ** -0.5

    def body(dh_ref, h_ref, g_ref, qx_ref, kv_ref, cross_hbm,
             dhb_ref, dqx_ref, dkv_ref, dh2_ref, dg_ref, wq_v, wo_v, sems):
        step = pl.program_id(0)
        _load_resident(step, _cross_pairs(cross_hbm, wq_v, wo_v), sems)

        @pl.when(step == 0)
        def _():
            dkv_ref[...] = jnp.zeros_like(dkv_ref)

        dht = dh_ref[...]
        dhb = dht.astype(BF16)
        dhb_ref[...] = dhb
        do = _dot_nt(dhb, wo_v[...]).astype(BF16)
        for hd in range(X_H):
            lo, hi = hd * X_DH, (hd + 1) * X_DH
            qh = qx_ref[:, lo:hi]
            kh = kv_ref[hd]
            p = _softmax_rows(_dot_nt(qh, kh) * scale)
            doh = do[:, lo:hi]
            dp = _dot_nt(doh, kv_ref[X_H + hd])
            ds = (p * (dp - jnp.sum(dp * p, axis=-1, keepdims=True)) * scale).astype(BF16)
            dqx_ref[:, lo:hi] = _dot(ds, kh).astype(BF16)
            dkv_ref[hd] += _dot_tn(ds, qh)
            dkv_ref[X_H + hd] += _dot_tn(p.astype(BF16), doh)
        dhn = _dot_nt(dqx_ref[...], wq_v[...])
        dx, dg = _rms_bwd_tile(h_ref[...], g_ref[...], dhn)
        dh2_ref[...] = dht + dx
        _accumulate(dg_ref, step, dg)

    return _call(
        body, (dh3, h, g, qx, kv, w_cross), grid=(t // TM,), name=name, comm=comm,
        in_specs=[_row_spec(TM, D), _row_spec(TM, D), _const_spec((1, D)), _row_spec(TM, D),
                  _const_spec((N_DEV, m, X_DH)), _ANY],
        out_specs=[_row_spec(TM, D), _row_spec(TM, D), _const_spec((N_DEV, m, X_DH)), _row_spec(TM, D),
                   _const_spec((1, D))],
        out_shape=[jax.ShapeDtypeStruct((t, D), BF16), jax.ShapeDtypeStruct((t, D), BF16),
                   jax.ShapeDtypeStruct((N_DEV, m, X_DH), F32), jax.ShapeDtypeStruct((t, D), F32),
                   jax.ShapeDtypeStruct((1, D), F32)],
        scratch_shapes=[pltpu.VMEM((D, D), BF16)] * 2 + [pltpu.SemaphoreType.DMA((2 * N_DEV,))])


def _loss_bwd(h, g, target, name):
    t = h.shape[0]

    def body(h_ref, g_ref, t_ref, loss_ref, dh_ref, dg_ref):
        step = pl.program_id(0)
        ht = h_ref[...]
        gain = g_ref[...]
        diff = _rms_fwd_tile(ht, gain) - t_ref[...]
        part = 0.5 * jnp.sum(jnp.sum(diff * diff, axis=-1, keepdims=True) / D, axis=0, keepdims=True)
        dx, dg = _rms_bwd_tile(ht, gain, diff / D)
        dh_ref[...] = dx
        _accumulate(loss_ref, step, jnp.broadcast_to(part, (8, 128)))
        _accumulate(dg_ref, step, dg)

    return pl.pallas_call(
        body, grid=(t // TM,), name=name,
        in_specs=[_row_spec(TM, D), _const_spec((1, D)), _row_spec(TM, D)],
        out_specs=[_const_spec((8, 128)), _row_spec(TM, D), _const_spec((1, D))],
        out_shape=[jax.ShapeDtypeStruct((8, 128), F32), jax.ShapeDtypeStruct((t, D), F32),
                   jax.ShapeDtypeStruct((1, D), F32)],
        compiler_params=_cparams(),
    )(h, g, target)


def _adamw(w, parts, m, v, name, row_block=0):
    r, c = w.shape
    n = parts.shape[0]
    tr = _pick_tile(r, (256, 352, 128))
    off = row_block * (r // tr)

    def body(*refs):
        _adamw_update(*refs)

    spec = _row_spec(tr, c)
    return pl.pallas_call(
        body, grid=(r // tr,), name=name,
        in_specs=[spec, pl.BlockSpec((n, tr, c), lambda i: (0, i + off, 0)), spec, spec], out_specs=[spec] * 4,
        out_shape=[jax.ShapeDtypeStruct((r, c), F32)] * 4,
        compiler_params=_cparams(),
    )(w, parts, m, v)


def _adamw_update(w_ref, p_ref, m_ref, v_ref, g_ref, d_ref, nm_ref, nv_ref):
    gt = p_ref[0].astype(F32)
    for k in range(1, p_ref.shape[0]):
        gt = gt + p_ref[k].astype(F32)
    g_ref[...] = gt
    nm = ADAM_B1 * m_ref[...] + (1.0 - ADAM_B1) * gt
    nv = ADAM_B2 * v_ref[...] + (1.0 - ADAM_B2) * jnp.square(gt)
    m_hat = nm / (1.0 - ADAM_B1 ** ADAM_STEP)
    v_hat = nv / (1.0 - ADAM_B2 ** ADAM_STEP)
    d_ref[...] = -ADAM_LR * (m_hat / (jnp.sqrt(v_hat) + ADAM_EPS) + ADAM_WD * w_ref[...])
    nm_ref[...] = nm
    nv_ref[...] = nv


MANY_TR = 32


def _adamw_many(jobs, name, comm=None):
    starts, n_steps = [], 0
    for w, _, _, _, _ in jobs:
        starts.append(n_steps)
        n_steps += w.shape[0] // MANY_TR

    def tile_of(start, count):
        return lambda i: jnp.clip(i - start, 0, count - 1)

    in_specs, out_specs, out_shape, operands = [], [], [], []
    for (w, parts, m, v, row_block), start in zip(jobs, starts):
        r, c = w.shape
        count = r // MANY_TR
        tile = tile_of(start, count)
        spec = pl.BlockSpec((MANY_TR, c), lambda i, tile=tile: (tile(i), 0))
        p_spec = pl.BlockSpec((parts.shape[0], MANY_TR, c),
                              lambda i, tile=tile, off=row_block * count: (0, tile(i) + off, 0))
        in_specs += [spec, p_spec, spec, spec]
        out_specs += [spec] * 4
        out_shape += [jax.ShapeDtypeStruct((r, c), F32)] * 4
        operands += [w, parts, m, v]
    n_job = len(jobs)

    def body(*refs):
        i = pl.program_id(0)
        for k, start in enumerate(starts):
            count = jobs[k][0].shape[0] // MANY_TR

            @pl.when(jnp.logical_and(i >= start, i < start + count))
            def _():
                _adamw_update(*refs[4 * k:4 * k + 4], *refs[4 * (n_job + k):4 * (n_job + k) + 4])

    outs, couts = _call(body, tuple(operands), grid=(n_steps,), name=name, comm=comm, in_specs=in_specs,
                        out_specs=out_specs, out_shape=out_shape, scratch_shapes=[])
    return [outs[4 * k:4 * k + 4] for k in range(n_job)], couts


def _mesh_pos():
    return lax.axis_index("x"), lax.axis_index("y"), lax.axis_index("c")


def _no_round(in_refs, out_refs, sems):
    pass


def _run_exchange(comm, name):
    c_in, c_out = len(comm.inputs), len(comm.out_shapes)

    def body(*refs):
        cins, couts, sems = refs[:c_in], refs[c_in:c_in + c_out], refs[c_in + c_out:]
        comm.start(cins, couts, sems)
        comm.middle(cins, couts, sems)
        comm.finish(cins, couts, sems)

    return list(pl.pallas_call(
        body, name=name, out_shape=list(comm.out_shapes),
        in_specs=[_ANY] * c_in, out_specs=[_ANY] * c_out, scratch_shapes=list(comm.sem_shapes),
    )(*comm.inputs))


def _gather_exchange(shards):
    n_arr = len(shards)

    def plan(x_refs, out_refs, sems):
        send_sems, recv_sems, local_sems = sems
        x, y, c = _mesh_pos()
        me, sibling = (x, y, c), (x, y, 1 - c)
        xn, yn, diag = (1 - x, y), (x, 1 - y), (1 - x, 1 - y)

        def slot(a, px, py, pc, half=None):
            ref = out_refs[a].at[4 * px + 2 * py + pc]
            if half is None:
                return ref
            rows = shards[a].shape[0] // 2
            return ref.at[half * rows:(half + 1) * rows]

        def copy(a, k, block, to, half=None, src=None):
            dst = slot(a, *block, half)
            return pltpu.make_async_remote_copy(
                src_ref=dst if src is None else src, dst_ref=dst,
                send_sem=send_sems.at[a, k], recv_sem=recv_sems.at[a, k],
                device_id=to, device_id_type=pl.DeviceIdType.MESH)

        return types.SimpleNamespace(
            me=me, sibling=sibling, xn=xn, yn=yn, diag=diag, c=c, copy=copy,
            mine=[pltpu.make_async_copy(x_refs[a], slot(a, *me), local_sems.at[a]) for a in range(n_arr)],
            first=[cp for a in range(n_arr) for cp in (
                copy(a, 0, me, sibling, src=x_refs[a]), copy(a, 1, me, (*xn, c), src=x_refs[a]),
                copy(a, 2, me, (*yn, c), src=x_refs[a]))],
            second=lambda a: (copy(a, 3, (*xn, c), (*yn, c), half=0), copy(a, 5, (*xn, c), sibling),
                              copy(a, 4, (*yn, c), (*xn, c), half=1), copy(a, 6, (*yn, c), sibling)),
            third=lambda a: (copy(a, 7, (*diag, c), sibling, half=0), copy(a, 8, (*diag, c), sibling, half=1)))

    def start(x_refs, out_refs, sems):
        p = plan(x_refs, out_refs, sems)
        for cp in p.mine + p.first:
            cp.start()

    def middle(x_refs, out_refs, sems):
        p = plan(x_refs, out_refs, sems)
        for a in range(n_arr):
            to_yn, x_to_sib, to_xn, y_to_sib = p.second(a)
            p.copy(a, 1, (*p.xn, p.c), p.me).wait_recv()
            to_yn.start()
            x_to_sib.start()
            p.copy(a, 2, (*p.yn, p.c), p.me).wait_recv()
            to_xn.start()
            y_to_sib.start()

    def finish(x_refs, out_refs, sems):
        p = plan(x_refs, out_refs, sems)
        for a in range(n_arr):
            half0_to_sib, half1_to_sib = p.third(a)
            p.copy(a, 3, (*p.diag, p.c), p.me, half=0).wait_recv()
            half0_to_sib.start()
            p.copy(a, 4, (*p.diag, p.c), p.me, half=1).wait_recv()
            half1_to_sib.start()
        other = 1 - p.c
        for a in range(n_arr):
            p.copy(a, 0, p.sibling, p.me).wait_recv()
            p.copy(a, 5, (*p.xn, other), p.me).wait_recv()
            p.copy(a, 6, (*p.yn, other), p.me).wait_recv()
            p.copy(a, 7, (*p.diag, other), p.me, half=0).wait_recv()
            p.copy(a, 8, (*p.diag, other), p.me, half=1).wait_recv()
        for cp in p.first:
            cp.wait_send()
        for a in range(n_arr):
            for cp in p.second(a) + p.third(a):
                cp.wait_send()
        for cp in p.mine:
            cp.wait()

    return types.SimpleNamespace(
        inputs=list(shards), start=start, middle=middle, finish=finish,
        out_shapes=[jax.ShapeDtypeStruct((N_DEV,) + s.shape, s.dtype) for s in shards],
        sem_shapes=[pltpu.SemaphoreType.DMA((n_arr, 9)), pltpu.SemaphoreType.DMA((n_arr, 9)),
                    pltpu.SemaphoreType.DMA((n_arr,))])


def _pair_exchange(grads):
    n_arr = len(grads)

    def plan(g_refs, land_refs, sems):
        send_sems, recv_sems = sems
        x, y, c = _mesh_pos()
        return [pltpu.make_async_remote_copy(
            src_ref=g_refs[a].at[2 * k + 1 - c], dst_ref=land_refs[a].at[k],
            send_sem=send_sems.at[a, k], recv_sem=recv_sems.at[a, k],
            device_id=(x, y, 1 - c), device_id_type=pl.DeviceIdType.MESH)
            for a in range(n_arr) for k in range(N_CHIP)]

    def start(g_refs, land_refs, sems):
        for cp in plan(g_refs, land_refs, sems):
            cp.start()

    def finish(g_refs, land_refs, sems):
        for cp in plan(g_refs, land_refs, sems):
            cp.wait()

    return types.SimpleNamespace(
        inputs=list(grads), start=start, middle=_no_round, finish=finish,
        out_shapes=[jax.ShapeDtypeStruct((N_CHIP,) + g.shape[1:], g.dtype) for g in grads],
        sem_shapes=[pltpu.SemaphoreType.DMA((n_arr, N_CHIP)), pltpu.SemaphoreType.DMA((n_arr, N_CHIP))])


def _chip_exchange(parts):
    n_arr = len(parts)

    def plan(p_refs, land_refs, sems):
        send_sems, recv_sems, local_sems = sems
        x, y, c = _mesh_pos()
        my_chip = 2 * x + y
        chips = [(1 - x, y), (x, 1 - y), (1 - x, 1 - y)]
        local = [pltpu.make_async_copy(p_refs[a].at[my_chip], land_refs[a].at[my_chip], local_sems.at[a])
                 for a in range(n_arr)]

        def copy(a, k, src_slot, dst_slot, px, py):
            return pltpu.make_async_remote_copy(
                src_ref=p_refs[a].at[src_slot], dst_ref=land_refs[a].at[dst_slot],
                send_sem=send_sems.at[a, k], recv_sem=recv_sems.at[a, k],
                device_id=(px, py, c), device_id_type=pl.DeviceIdType.MESH)

        sends = [copy(a, k, 2 * px + py, my_chip, px, py) for a in range(n_arr) for k, (px, py) in enumerate(chips)]
        arrivals = [copy(a, k, my_chip, 2 * px + py, px, py) for a in range(n_arr)
                    for k, (px, py) in enumerate(chips)]
        return local, sends, arrivals

    def start(p_refs, land_refs, sems):
        local, sends, _ = plan(p_refs, land_refs, sems)
        for cp in local + sends:
            cp.start()

    def finish(p_refs, land_refs, sems):
        local, sends, arrivals = plan(p_refs, land_refs, sems)
        for cp in arrivals:
            cp.wait_recv()
        for cp in sends:
            cp.wait_send()
        for cp in local:
            cp.wait()

    return types.SimpleNamespace(
        inputs=list(parts), start=start, middle=_no_round, finish=finish,
        out_shapes=[jax.ShapeDtypeStruct(p.shape, p.dtype) for p in parts],
        sem_shapes=[pltpu.SemaphoreType.DMA((n_arr, 3)), pltpu.SemaphoreType.DMA((n_arr, 3)),
                    pltpu.SemaphoreType.DMA((n_arr,))])


def _row_tile(r, cap=640):
    best = None
    for cand in range(16, min(r, cap) + 1, 16):
        if r % cand == 0:
            best = cand
    return best if best is not None else r


def _pair_sum(g, landed, core, name):
    _, r, c_dim = g.shape
    tr = _row_tile(r)

    def body(core_ref, mine_ref, theirs_ref, o_ref):
        o_ref[0] = (mine_ref[0].astype(F32) + theirs_ref[0].astype(F32)).astype(o_ref.dtype)

    return pl.pallas_call(
        body, name=name,
        grid_spec=pltpu.PrefetchScalarGridSpec(
            num_scalar_prefetch=1, grid=(N_CHIP, r // tr),
            in_specs=[pl.BlockSpec((1, tr, c_dim), lambda k, i, core_ref: (2 * k + core_ref[0], i, 0)),
                      pl.BlockSpec((1, tr, c_dim), lambda k, i, core_ref: (k, i, 0))],
            out_specs=pl.BlockSpec((1, tr, c_dim), lambda k, i, core_ref: (k, i, 0))),
        out_shape=jax.ShapeDtypeStruct((N_CHIP, r, c_dim), g.dtype),
        compiler_params=_cparams(2),
    )(core, g, landed)


def _sum_slots(parts, name):
    n, r, c_dim = parts.shape
    tr = _row_tile(r)

    def body(p_ref, o_ref):
        acc = p_ref[0].astype(F32)
        for k in range(1, n):
            acc = acc + p_ref[k].astype(F32)
        o_ref[...] = acc

    return pl.pallas_call(
        body, grid=(r // tr,), name=name,
        in_specs=[pl.BlockSpec((n, tr, c_dim), lambda i: (0, i, 0))],
        out_specs=_row_spec(tr, c_dim),
        out_shape=jax.ShapeDtypeStruct((r, c_dim), F32),
        compiler_params=_cparams(),
    )(parts)


GAINS = ("g_ffn1", "g_mix", "g_cross", "g_mem", "g_ffn2", "g_final")
SMALL = GAINS + ("b_gate", "conv_w")
SMALL_R = 16
WEIGHT_ORDER = ("g_ffn1", "w_ffn1_gu", "w_ffn1_down", "g_mix", "w_in", "b_gate", "conv_w", "w_conv_out",
                "w_attn_out", "w_o", "g_cross", "g_mem", "w_cq", "w_ckv", "w_co", "g_ffn2", "w_ffn2_gu",
                "w_ffn2_down", "g_final")
GU_NAMES = ("w_ffn1_gu", "w_ffn2_gu")


def _pack_small(vals, conv_rows):
    rows = [vals[n].reshape(1, D) for n in GAINS] + [vals["b_gate"].reshape(2, D), conv_rows.reshape(CONV_K, D)]
    used = len(GAINS) + 2 + CONV_K
    return jnp.concatenate(rows + [jnp.zeros((SMALL_R - used, D), F32)], axis=0)


def _unpack_small(buf):
    out = {n: buf[k] for k, n in enumerate(GAINS)}
    out["b_gate"] = buf[6:8].reshape(2 * D)
    out["conv_w"] = buf[8:8 + CONV_K]
    return out


def _exchange_shards(wts):
    out = {n: jnp.pad(wts[n].T.astype(BF16), ((0, FF_PAD - FF_BLK), (0, 0))) for n in GU_NAMES}
    for n in ("w_ckv", "w_in", "w_ffn1_down", "w_ffn2_down"):
        out[n] = wts[n].astype(BF16)
    out["mix"] = jnp.concatenate([wts[n].astype(BF16) for n in MIX_MATS], axis=0)
    out["cross"] = jnp.concatenate([wts[n].astype(BF16) for n in CROSS_MATS], axis=0)
    return out


def _by_device(dw):
    return dw.reshape(N_DEV, SQ_ROWS, D)


def _reduce_group(grads, landed, core, names):
    return [_pair_sum(g, l, core, "grads_pair_sum_" + n) for g, l, n in zip(grads, landed, names)]


def _step(x, mem, target, sh, conv_pad, gains, b_gate, core):
    wg1, wd1, conv_all = _run_exchange(_gather_exchange([sh["w_ffn1_gu"], sh["w_ffn1_down"], conv_pad]), "gather_ffn1")
    conv_w = conv_all[:, :CONV_K, :].transpose(1, 0, 2).reshape(CONV_K, D)
    (n1, gate1, up1, act1, h1), (w_in,) = _ffn_fwd(
        x, gains["g_ffn1"], wg1, wd1, "ffn1_fwd", comm=_gather_exchange([sh["w_in"]]))
    (u, pcg, qkv), (w_mix,) = _inproj_fwd(h1, gains["g_mix"], w_in, "inproj_fwd", comm=_gather_exchange([sh["mix"]]))
    yc = _conv_fwd(pcg, conv_w, "conv_fwd")
    (ysb, ctot), (w_cross, w_ckv, wg2, wd2) = _sb_fwd(
        qkv, "sb_fwd", comm=_gather_exchange([sh["cross"], sh["w_ckv"], sh["w_ffn2_gu"], sh["w_ffn2_down"]]))
    a_mix, b_mix, merged, h2 = _mix_out_fwd(yc, ysb, pcg, b_gate, h1, w_mix, "mix_out_fwd")
    mn, kv = _memkv_fwd(mem, gains["g_mem"], w_ckv, "memkv_fwd")
    hn, qx, o_x, h3 = _cross_fwd(h2, gains["g_cross"], kv, w_cross, "cross_fwd")
    (n4, gate2, up2, act2, h4), _ = _ffn_fwd(h3, gains["g_ffn2"], wg2, wd2, "ffn2_fwd")
    loss, dh4, dg_final = _loss_bwd(h4, gains["g_final"], target, "loss_bwd")

    gs = {"g_final": dg_final}
    (dgu2, dh4b, dh3, gs["g_ffn2"]), _ = _ffn_bwd(dh4, h3, gains["g_ffn2"], gate2, up2, wg2, wd2, "ffn2_bwd")
    grads_a = [_mm_tn_rows(dgu2, n4, FF_PAD, "dw_ffn2_gu"),
               _mm_tn_rows(act2, dh4b, FF_BLK, "dw_ffn2_down").reshape(N_DEV, DOWN_ROWS, D)]
    names_a = ["w_ffn2_gu", "w_ffn2_down"]
    (dh3b, dqx, dkv, dh2, gs["g_cross"]), landed_a = _cross_bwd(
        dh3, h2, gains["g_cross"], qx, kv, w_cross, "cross_bwd", comm=_pair_exchange(grads_a))
    sums_a = _reduce_group(grads_a, landed_a, core, names_a)
    grads_b = [_mm_tn_cols(mn, dkv, "dw_ckv"),
               jnp.concatenate([_by_device(_mm_tn(hn, dqx, "dw_cq")), _by_device(_mm_tn(o_x, dh3b, "dw_co"))], axis=1)]
    names_b = ["w_ckv", "cross"]
    gs["g_mem"] = _memkv_bwd(dkv, mem, gains["g_mem"], w_ckv, "memkv_bwd")
    (dh2b, da_mix, db_mix, dgp, dyc, dysb, gs["b_gate"]), landed_b = _mix_out_bwd(
        dh2, a_mix, b_mix, pcg, b_gate, w_mix, "mix_out_bwd", comm=_pair_exchange(grads_b))
    sums_b = _reduce_group(grads_b, landed_b, core, names_b)
    grads_c = [jnp.concatenate([_by_device(_mm_tn(yc, da_mix, "dw_conv_out")),
                                _by_device(_mm_tn(ysb, db_mix, "dw_attn_out")),
                                _by_device(_mm_tn(merged, dh2b, "dw_o"))], axis=1)]
    landed_c = _run_exchange(_pair_exchange(grads_c), "grads_to_sibling_mix")
    sums_c = _reduce_group(grads_c, landed_c, core, ["mix"])
    (dq, dkv_sb), parts_abc = _sb_bwd(qkv, dysb, ctot, "sb_bwd", comm=_chip_exchange(sums_a + sums_b + sums_c))
    dconv, gs["conv_w"] = _conv_bwd(pcg, conv_w, dyc, "conv_bwd")
    grads_d = [jnp.concatenate(
        [_mm_tn_cols(u, dconv, "dw_in_conv"), _mm_tn(u, dq, "dw_in_q")[None], _mm_tn_cols(u, dkv_sb, "dw_in_kv"),
         _mm_tn_cols(u, dgp, "dw_in_gates")], axis=0)]
    (dh1, gs["g_mix"]), landed_d = _inproj_bwd(dconv, dq, dkv_sb, dgp, w_in, h1, gains["g_mix"], dh2, "inproj_bwd",
                                               comm=_pair_exchange(grads_d))
    sums_d = _reduce_group(grads_d, landed_d, core, ["w_in"])
    (dgu1, dh1b, dx, gs["g_ffn1"]), parts_d = _ffn_bwd(dh1, x, gains["g_ffn1"], gate1, up1, wg1, wd1, "ffn1_bwd",
                                                       comm=_chip_exchange(sums_d))
    grads_e = [_mm_tn_rows(dgu1, n1, FF_PAD, "dw_ffn1_gu"),
               _mm_tn_rows(act1, dh1b, FF_BLK, "dw_ffn1_down").reshape(N_DEV, DOWN_ROWS, D)]
    names_e = ["w_ffn1_gu", "w_ffn1_down"]
    landed_e = _run_exchange(_pair_exchange(grads_e), "grads_to_sibling_ffn1")
    sums_e = dict(zip(names_e, _reduce_group(grads_e, landed_e, core, names_e)))

    names = names_a + names_b + ["mix"] + ["w_in"]
    return loss, dx, dict(zip(names, parts_abc + parts_d)), sums_e, gs


def kernel(x, mem, g_ffn1, w_ffn1_gu, w_ffn1_down, g_mix, w_in, b_gate, conv_w, w_conv_out, w_attn_out, w_o, g_cross, g_mem, w_cq, w_ckv, w_co, g_ffn2, w_ffn2_gu, w_ffn2_down, g_final, loss_target, m_g_ffn1, m_w_ffn1_gu, m_w_ffn1_down, m_g_mix, m_w_in, m_b_gate, m_conv_w, m_w_conv_out, m_w_attn_out, m_w_o, m_g_cross, m_g_mem, m_w_cq, m_w_ckv, m_w_co, m_g_ffn2, m_w_ffn2_gu, m_w_ffn2_down, m_g_final, v_g_ffn1, v_w_ffn1_gu, v_w_ffn1_down, v_g_mix, v_w_in, v_b_gate, v_conv_w, v_w_conv_out, v_w_attn_out, v_w_o, v_g_cross, v_g_mem, v_w_cq, v_w_ckv, v_w_co, v_g_ffn2, v_w_ffn2_gu, v_w_ffn2_down, v_g_final):
    args = locals()
    wts = {n: args[n] for n in WEIGHT_ORDER}
    mom1 = {n: args["m_" + n] for n in WEIGHT_ORDER}
    mom2 = {n: args["v_" + n] for n in WEIGHT_ORDER}
    cx, cy, cc = _mesh_pos()
    dev = 4 * cx + 2 * cy + cc
    conv_cols = D // N_DEV

    conv_pad = jnp.concatenate([conv_w, jnp.zeros((SMALL_R - CONV_K, conv_cols), F32)], axis=0)
    gains = {n: wts[n].reshape(1, D) for n in GAINS}
    loss8, dx, parts, sums_e, gs = _step(x[0], mem[0], loss_target[0], _exchange_shards(wts), conv_pad, gains,
                                 b_gate.reshape(1, 2 * D), cc.reshape(1).astype(jnp.int32))
    loss = lax.psum(loss8[0, 0], MESH_AXES)

    grads, delta, new_m, new_v = {}, {}, {}, {}

    def operands(n, transposed):
        trio = (wts[n], mom1[n], mom2[n])
        return tuple(a.T for a in trio) if transposed else trio

    def record(n, res, transposed):
        grads[n], delta[n], new_m[n], new_v[n] = [r.T for r in res] if transposed else res

    early = [("w_ffn2_gu", "w_ffn2_gu", 0, True), ("w_ffn2_down", "w_ffn2_down", 0, False),
             ("w_ckv", "w_ckv", 0, False), ("w_in", "w_in", 0, False)]
    early += [(n, "mix", k, False) for k, n in enumerate(MIX_MATS)]
    early += [(n, "cross", k, False) for k, n in enumerate(CROSS_MATS)]
    jobs = []
    for n, buf, row_block, transposed in early:
        w, m1, m2 = operands(n, transposed)
        jobs.append((w, parts[buf], m1, m2, row_block))
    last_names = ["w_ffn1_gu", "w_ffn1_down"]
    results, last_parts = _adamw_many(jobs, "adamw_early", comm=_chip_exchange([sums_e[n] for n in last_names]))
    for (n, _, _, transposed), res in zip(early, results):
        record(n, res, transposed)
    for n, p, transposed in zip(last_names, last_parts, (True, False)):
        w, m1, m2 = operands(n, transposed)
        record(n, _adamw(w, p, m1, m2, "adamw_" + n), transposed)

    gs_rows = {n: gs[n] for n in GAINS + ("b_gate",)}
    small_all = _run_exchange(_gather_exchange([_pack_small(gs_rows, gs["conv_w"][:CONV_K])]), "gather_small_grads")[0]
    grad_small = _unpack_small(_sum_slots(small_all, "small_grads_sum"))
    grad_small["conv_w"] = lax.dynamic_slice_in_dim(grad_small["conv_w"], dev * conv_cols, conv_cols, axis=1)
    grads.update(grad_small)

    def small_buf(vals):
        return _pack_small(vals, jnp.concatenate([vals["conv_w"], jnp.zeros((CONV_K, D - conv_cols), F32)], axis=1))

    _, d_s, m_s, v_s = _adamw(small_buf(wts), small_buf(grads)[None], small_buf(mom1), small_buf(mom2), "adamw_small")
    for res, buf in ((delta, d_s), (new_m, m_s), (new_v, v_s)):
        un = _unpack_small(buf)
        for n in GAINS + ("b_gate",):
            res[n] = un[n]
        res["conv_w"] = un["conv_w"][:, :conv_cols]

    return (loss, dx[None], *[grads[n] for n in WEIGHT_ORDER], *[delta[n] for n in WEIGHT_ORDER],
            *[new_m[n] for n in WEIGHT_ORDER], *[new_v[n] for n in WEIGHT_ORDER])
```

```python
import types

import jax
import jax.numpy as jnp
from jax import lax
from jax.experimental import pallas as pl
from jax.experimental.pallas import tpu as pltpu

F32 = jnp.float32
BF16 = jnp.bfloat16

D = 1024
DFF = 2816
SB_H = 8
SB_DH = 128
X_H = 4
X_DH = 256
CONV_K = 3
RMS_EPS = 1e-6
N_DEV = 8
N_CHIP = 4
SQ_ROWS = D // N_DEV

ADAM_LR = 0.001
ADAM_B1 = 0.9
ADAM_B2 = 0.999
ADAM_EPS = 1e-08
ADAM_WD = 0.01
ADAM_STEP = 10

TM = 256
TQ = 512
TK = 256
SB_HPS = 2
VMEM_LIMIT = 56 << 20

FF_BLK = DFF // 4
FF_PAD = 768
FF_SUB = 256
DOWN_ROWS = DFF // N_DEV

MIX_MATS = ("w_conv_out", "w_attn_out", "w_o")
CROSS_MATS = ("w_cq", "w_co")

MESH_AXES = ("x", "y", "c")
_ANY = pl.BlockSpec(memory_space=pl.ANY)


def _cparams(n_axes=1):
    return pltpu.CompilerParams(
        dimension_semantics=("arbitrary",) * n_axes, vmem_limit_bytes=VMEM_LIMIT)


def _row_spec(tm, n):
    return pl.BlockSpec((tm, n), lambda i: (i, 0))


def _blk_row_spec(nb, tm, n):
    return pl.BlockSpec((nb, tm, n), lambda i: (0, i, 0))


def _const_spec(shape):
    zeros = (0,) * len(shape)
    return pl.BlockSpec(shape, lambda i: zeros)


def _dot(a, b):
    return jnp.dot(a, b, preferred_element_type=F32)


def _dot_nt(a, b):
    return lax.dot_general(a, b, (((1,), (1,)), ((), ())), preferred_element_type=F32)


def _dot_tn(a, b):
    return lax.dot_general(a, b, (((0,), (0,)), ((), ())), preferred_element_type=F32)


def _sigmoid(x):
    return 1.0 / (1.0 + jnp.exp(-x))


def _call(body, operands, *, grid, in_specs, out_specs, out_shape, scratch_shapes, name, comm=None):
    n_in, n_out, n_sc = len(in_specs), len(out_specs), len(scratch_shapes)
    if comm is None:
        outs = pl.pallas_call(
            body, grid=grid, name=name, in_specs=in_specs, out_specs=out_specs, out_shape=out_shape,
            scratch_shapes=scratch_shapes, compiler_params=_cparams(len(grid)))(*operands)
        return list(outs), []
    c_in, c_out, c_sem = len(comm.inputs), len(comm.out_shapes), len(comm.sem_shapes)

    def hosted(*refs):
        bounds = [0, n_in, c_in, n_out, c_out, n_sc, c_sem]
        parts, pos = [], 0
        for k in bounds[1:]:
            parts.append(refs[pos:pos + k])
            pos += k
        ins, cins, outs, couts, scr, sems = parts
        step, n_steps = pl.program_id(0), grid[0]
        for ax in range(1, len(grid)):
            step, n_steps = step * grid[ax] + pl.program_id(ax), n_steps * grid[ax]

        @pl.when(step == 0)
        def _():
            comm.start(cins, couts, sems)

        @pl.when(step == (2 * n_steps) // 3)
        def _():
            comm.middle(cins, couts, sems)

        body(*ins, *outs, *scr)

        @pl.when(step == n_steps - 1)
        def _():
            comm.finish(cins, couts, sems)

    res = pl.pallas_call(
        hosted, grid=grid, name=name, in_specs=list(in_specs) + [_ANY] * c_in,
        out_specs=list(out_specs) + [_ANY] * c_out, out_shape=list(out_shape) + list(comm.out_shapes),
        scratch_shapes=list(scratch_shapes) + list(comm.sem_shapes),
        compiler_params=_cparams(len(grid)))(*operands, *comm.inputs)
    return list(res[:n_out]), list(res[n_out:])


def _load_resident(step, pairs, sems):
    @pl.when(step == 0)
    def _():
        copies = [pltpu.make_async_copy(src, dst, sems.at[k]) for k, (src, dst) in enumerate(pairs)]
        for cp in copies:
            cp.start()
        for cp in copies:
            cp.wait()


def _square_pairs(buf_hbm, index, dst):
    off = index * SQ_ROWS
    return [(buf_hbm.at[d, off:off + SQ_ROWS, :], dst.at[d * SQ_ROWS:(d + 1) * SQ_ROWS, :]) for d in range(N_DEV)]


def _down_pairs(wd_hbm, dst):
    return [(wd_hbm.at[d], dst.at[d // 2, (d % 2) * DOWN_ROWS:(d % 2 + 1) * DOWN_ROWS, :]) for d in range(N_DEV)]


def _zero_down_pad(step, dst):
    @pl.when(step == 0)
    def _():
        dst[:, FF_BLK:, :] = jnp.zeros((4, FF_PAD - FF_BLK, D), BF16)


def _rms_fwd_tile(xt, g):
    r = lax.rsqrt(jnp.mean(xt * xt, axis=-1, keepdims=True) + RMS_EPS)
    return (xt * r) * g


def _rms_bwd_tile(xt, g, dn):
    r = lax.rsqrt(jnp.mean(xt * xt, axis=-1, keepdims=True) + RMS_EPS)
    xhat = xt * r
    dxhat = dn * g
    dx = r * (dxhat - xhat * jnp.mean(dxhat * xhat, axis=-1, keepdims=True))
    dg = jnp.sum(dn * xhat, axis=0, keepdims=True)
    return dx, dg


def _accumulate(ref, step, value):
    @pl.when(step == 0)
    def _():
        ref[...] = value

    @pl.when(step != 0)
    def _():
        ref[...] = ref[...] + value


def _ffn_fwd(x, g, wgu, wd, name, comm=None):
    t = x.shape[0]

    def body(x_ref, g_ref, wgu_hbm, wd_hbm, n_ref, gate_ref, up_ref, act_ref, h_ref, wgu_v, wd_v, sems):
        step = pl.program_id(0)
        _zero_down_pad(step, wd_v)
        _load_resident(step, [(wgu_hbm, wgu_v)] + _down_pairs(wd_hbm, wd_v), sems)
        xt = x_ref[...]
        n = _rms_fwd_tile(xt, g_ref[...]).astype(BF16)
        n_ref[...] = n
        acc = jnp.zeros((TM, D), F32)
        for j in range(4):
            for s in range(FF_PAD // FF_SUB):
                lo, hi = s * FF_SUB, (s + 1) * FF_SUB
                gt = _dot_nt(n, wgu_v[j, lo:hi, :])
                ut = _dot_nt(n, wgu_v[4 + j, lo:hi, :])
                gate_ref[j, :, lo:hi] = gt.astype(BF16)
                up_ref[j, :, lo:hi] = ut.astype(BF16)
                act_ref[j, :, lo:hi] = ((gt * _sigmoid(gt)) * ut).astype(BF16)
            acc = acc + _dot(act_ref[j], wd_v[j])
        h_ref[...] = xt + 0.5 * acc

    ff = jax.ShapeDtypeStruct((4, t, FF_PAD), BF16)
    return _call(
        body, (x, g, wgu, wd), grid=(t // TM,), name=name, comm=comm,
        in_specs=[_row_spec(TM, D), _const_spec((1, D)), _ANY, _ANY],
        out_specs=[_row_spec(TM, D)] + [_blk_row_spec(4, TM, FF_PAD)] * 3 + [_row_spec(TM, D)],
        out_shape=[jax.ShapeDtypeStruct((t, D), BF16), ff, ff, ff, jax.ShapeDtypeStruct((t, D), F32)],
        scratch_shapes=[pltpu.VMEM((N_DEV, FF_PAD, D), BF16), pltpu.VMEM((4, FF_PAD, D), BF16),
                        pltpu.SemaphoreType.DMA((1 + N_DEV,))])


def _ffn_bwd(dh, xin, g, gate, up, wgu, wd, name, comm=None):
    t = dh.shape[0]

    def body(dh_ref, x_ref, g_ref, gate_ref, up_ref, wgu_hbm, wd_hbm,
             dgu_ref, dhb_ref, dx_ref, dg_ref, wgu_v, wd_v, sems):
        step = pl.program_id(0)
        _zero_down_pad(step, wd_v)
        _load_resident(step, [(wgu_hbm, wgu_v)] + _down_pairs(wd_hbm, wd_v), sems)
        dht = dh_ref[...]
        dhb = (0.5 * dht).astype(BF16)
        dhb_ref[...] = dhb
        dn = jnp.zeros((TM, D), F32)
        for j in range(4):
            for s in range(FF_PAD // FF_SUB):
                lo, hi = s * FF_SUB, (s + 1) * FF_SUB
                da = _dot_nt(dhb, wd_v[j, lo:hi, :])
                gt = gate_ref[j, :, lo:hi].astype(F32)
                ut = up_ref[j, :, lo:hi].astype(F32)
                sg = _sigmoid(gt)
                dgt = (da * ut * (sg * (1.0 + gt * (1.0 - sg)))).astype(BF16)
                dut = (da * (gt * sg)).astype(BF16)
                dgu_ref[j, :, lo:hi] = dgt
                dgu_ref[4 + j, :, lo:hi] = dut
            dn = dn + _dot(dgu_ref[j], wgu_v[j]) + _dot(dgu_ref[4 + j], wgu_v[4 + j])
        dx, dg = _rms_bwd_tile(x_ref[...], g_ref[...], dn)
        dx_ref[...] = dht + dx
        _accumulate(dg_ref, step, dg)

    return _call(
        body, (dh, xin, g, gate, up, wgu, wd), grid=(t // TM,), name=name, comm=comm,
        in_specs=[_row_spec(TM, D), _row_spec(TM, D), _const_spec((1, D)), _blk_row_spec(4, TM, FF_PAD),
                  _blk_row_spec(4, TM, FF_PAD), _ANY, _ANY],
        out_specs=[_blk_row_spec(N_DEV, TM, FF_PAD), _row_spec(TM, D), _row_spec(TM, D), _const_spec((1, D))],
        out_shape=[jax.ShapeDtypeStruct((N_DEV, t, FF_PAD), BF16), jax.ShapeDtypeStruct((t, D), BF16),
                   jax.ShapeDtypeStruct((t, D), F32), jax.ShapeDtypeStruct((1, D), F32)],
        scratch_shapes=[pltpu.VMEM((N_DEV, FF_PAD, D), BF16), pltpu.VMEM((4, FF_PAD, D), BF16),
                        pltpu.SemaphoreType.DMA((1 + N_DEV,))])


WIDE_TILES = (1024, 512, 256, 128)


def _pick_tile(n, options=(512, 256, 128)):
    for o in options:
        if n % o == 0:
            return o
    return n


def _mm_tn(a, b, name):
    k, m = a.shape
    _, n = b.shape
    tm, tn = _pick_tile(m, WIDE_TILES), _pick_tile(n)

    def body(a_ref, b_ref, o_ref):
        o_ref[...] = _dot_tn(a_ref[...].astype(BF16), b_ref[...].astype(BF16)).astype(BF16)

    return pl.pallas_call(
        body, grid=(m // tm, n // tn), name=name,
        in_specs=[pl.BlockSpec((k, tm), lambda i, j: (0, i)), pl.BlockSpec((k, tn), lambda i, j: (0, j))],
        out_specs=pl.BlockSpec((tm, tn), lambda i, j: (i, j)),
        out_shape=jax.ShapeDtypeStruct((m, n), BF16),
        compiler_params=_cparams(2),
    )(a, b)


def _mm_tn_cols(a, b, name):
    k, m = a.shape
    nb, _, n = b.shape
    tm = _pick_tile(m, WIDE_TILES)

    def body(a_ref, b_ref, o_ref):
        o_ref[0] = _dot_tn(a_ref[...].astype(BF16), b_ref[0].astype(BF16)).astype(BF16)

    return pl.pallas_call(
        body, grid=(nb, m // tm), name=name,
        in_specs=[pl.BlockSpec((k, tm), lambda j, i: (0, i)), pl.BlockSpec((1, k, n), lambda j, i: (j, 0, 0))],
        out_specs=pl.BlockSpec((1, tm, n), lambda j, i: (j, i, 0)),
        out_shape=jax.ShapeDtypeStruct((nb, m, n), BF16),
        compiler_params=_cparams(2),
    )(a, b)


def _mm_tn_rows(a, b, keep, name):
    nb, k, m = a.shape
    _, n = b.shape
    tn = _pick_tile(n, WIDE_TILES)

    def body(a_ref, b_ref, o_ref):
        o_ref[0] = _dot_tn(a_ref[0], b_ref[...])[:keep].astype(BF16)

    return pl.pallas_call(
        body, grid=(nb, n // tn), name=name,
        in_specs=[pl.BlockSpec((1, k, m), lambda j, i: (j, 0, 0)), pl.BlockSpec((k, tn), lambda j, i: (0, i))],
        out_specs=pl.BlockSpec((1, keep, tn), lambda j, i: (j, 0, i)),
        out_shape=jax.ShapeDtypeStruct((nb, keep, n), BF16),
        compiler_params=_cparams(2),
    )(a, b)


PCG_W = 5 * D
QKV_W = 3 * D
PROJ_SUB = 512


def _inproj_fwd(h, g, w_in, name, comm=None):
    t = h.shape[0]

    def body(h_ref, g_ref, w_hbm, u_ref, pcg_ref, qkv_ref, w_v, sems):
        _load_resident(pl.program_id(0), [(w_hbm, w_v)], sems)
        u = _rms_fwd_tile(h_ref[...], g_ref[...]).astype(BF16)
        u_ref[...] = u
        for blk in range(N_DEV):
            for s in range(D // PROJ_SUB):
                lo, hi = s * PROJ_SUB, (s + 1) * PROJ_SUB
                p = _dot(u, w_v[blk, :, lo:hi])
                if blk < 3:
                    pcg_ref[:, blk * D + lo:blk * D + hi] = p
                elif blk < 6:
                    qkv_ref[:, (blk - 3) * D + lo:(blk - 3) * D + hi] = p.astype(BF16)
                else:
                    pcg_ref[:, (blk - 3) * D + lo:(blk - 3) * D + hi] = p

    return _call(
        body, (h, g, w_in), grid=(t // TM,), name=name, comm=comm,
        in_specs=[_row_spec(TM, D), _const_spec((1, D)), _ANY],
        out_specs=[_row_spec(TM, D), _row_spec(TM, PCG_W), _row_spec(TM, QKV_W)],
        out_shape=[jax.ShapeDtypeStruct((t, D), BF16), jax.ShapeDtypeStruct((t, PCG_W), F32),
                   jax.ShapeDtypeStruct((t, QKV_W), BF16)],
        scratch_shapes=[pltpu.VMEM((N_DEV, D, D), BF16), pltpu.SemaphoreType.DMA((1,))])


CONV_CW = 256


def _shift_down(v, k, rows):
    return jnp.where(rows >= k, pltpu.roll(v, k, 0), 0.0)


def _shift_up(v, k, rows, t):
    return jnp.where(rows < t - k, pltpu.roll(v, t - k, 0), 0.0)


def _col_spec(t, cw, off):
    return pl.BlockSpec((t, cw), lambda j: (0, j + off))


def _conv_fwd(pcg, conv_w, name):
    t = pcg.shape[0]
    nb = D // CONV_CW

    def body(cb_ref, cc_ref, cx_ref, w_ref, y_ref):
        rows = lax.broadcasted_iota(jnp.int32, (t, CONV_CW), 0)
        xc = cc_ref[...] * cx_ref[...]
        conv = (w_ref[0:1, :] * _shift_down(xc, 2, rows) + w_ref[1:2, :] * _shift_down(xc, 1, rows)
                + w_ref[2:3, :] * xc)
        y_ref[...] = (cb_ref[...] * conv).astype(BF16)

    return pl.pallas_call(
        body, grid=(nb,), name=name,
        in_specs=[_col_spec(t, CONV_CW, 0), _col_spec(t, CONV_CW, nb), _col_spec(t, CONV_CW, 2 * nb),
                  pl.BlockSpec((CONV_K, CONV_CW), lambda j: (0, j))],
        out_specs=_col_spec(t, CONV_CW, 0),
        out_shape=jax.ShapeDtypeStruct((t, D), BF16),
        compiler_params=_cparams(),
    )(pcg, pcg, pcg, conv_w)


def _conv_bwd(pcg, conv_w, dyc, name):
    t = pcg.shape[0]
    nb = D // CONV_CW

    def body(cb_ref, cc_ref, cx_ref, w_ref, dy_ref, dc_ref, dw_ref):
        rows = lax.broadcasted_iota(jnp.int32, (t, CONV_CW), 0)
        cc, cx = cc_ref[...], cx_ref[...]
        xc = cc * cx
        x1 = _shift_down(xc, 1, rows)
        x2 = _shift_down(xc, 2, rows)
        w0, w1, w2 = w_ref[0:1, :], w_ref[1:2, :], w_ref[2:3, :]
        conv = w0 * x2 + w1 * x1 + w2 * xc
        dy = dy_ref[...]
        dc_ref[0] = (dy * conv).astype(BF16)
        dconv = dy * cb_ref[...]
        dw_ref[...] = jnp.zeros((8, CONV_CW), F32)
        dw_ref[0:1, :] = jnp.sum(dconv * x2, axis=0, keepdims=True)
        dw_ref[1:2, :] = jnp.sum(dconv * x1, axis=0, keepdims=True)
        dw_ref[2:3, :] = jnp.sum(dconv * xc, axis=0, keepdims=True)
        dxc = w2 * dconv + w1 * _shift_up(dconv, 1, rows, t) + w0 * _shift_up(dconv, 2, rows, t)
        dc_ref[1] = (dxc * cx).astype(BF16)
        dc_ref[2] = (dxc * cc).astype(BF16)

    return pl.pallas_call(
        body, grid=(nb,), name=name,
        in_specs=[_col_spec(t, CONV_CW, 0), _col_spec(t, CONV_CW, nb), _col_spec(t, CONV_CW, 2 * nb),
                  pl.BlockSpec((CONV_K, CONV_CW), lambda j: (0, j)), _col_spec(t, CONV_CW, 0)],
        out_specs=[pl.BlockSpec((3, t, CONV_CW), lambda j: (0, 0, j)), pl.BlockSpec((8, CONV_CW), lambda j: (0, j))],
        out_shape=[jax.ShapeDtypeStruct((3, t, D), BF16), jax.ShapeDtypeStruct((8, D), F32)],
        compiler_params=_cparams(),
    )(pcg, pcg, pcg, conv_w, dyc)


def _tri2(cond):
    rr = lax.broadcasted_iota(jnp.int32, (2 * TK, TK), 0) & (TK - 1)
    cc = lax.broadcasted_iota(jnp.int32, (2 * TK, TK), 1)
    return cond(rr, cc).astype(BF16)


def _causal(shift, row0=0):
    rr = lax.broadcasted_iota(jnp.int32, (TQ - row0, TK), 0) + row0
    cc = lax.broadcasted_iota(jnp.int32, (TQ - row0, TK), 1)
    return cc + shift < rr


def _cumdot(v, tri2):
    hi = v.astype(BF16)
    lo = (v - hi.astype(F32)).astype(BF16)
    return _dot(jnp.concatenate([hi, lo], axis=1), tri2)


def _log_1m_beta(z):
    return -(jnp.maximum(z, 0.0) + jnp.log(1.0 + jnp.exp(-jnp.abs(z))))


def _sb_specs(t):
    g = SB_H // SB_HPS
    w = SB_HPS * SB_DH
    q_spec = pl.BlockSpec((TQ, w), lambda h, i: (i, h))
    k_spec = pl.BlockSpec((t, w), lambda h, i: (0, g + h))
    v_spec = pl.BlockSpec((t, w), lambda h, i: (0, 2 * g + h))
    ct_spec = pl.BlockSpec((SB_HPS, TQ, 1), lambda h, i: (h, i, 0))
    return g, w, q_spec, k_spec, v_spec, ct_spec


def _sb_fwd(qkv, name, comm=None):
    t = qkv.shape[0]
    scale = SB_DH ** -0.5
    g, w, q_spec, k_spec, v_spec, ct_spec = _sb_specs(t)

    def body(q_ref, k_ref, v_ref, y_ref, ct_ref):
        i = pl.program_id(1)
        later = _tri2(lambda j, s: j > s)
        n_diag = TQ // TK

        def block(j, carry, shift):
            off = pl.multiple_of(j * TK, TK)
            zs, ms = [], []
            for hd in range(SB_HPS):
                cols = slice(hd * SB_DH, (hd + 1) * SB_DH)
                z = _dot_nt(q_ref[:, cols], k_ref[pl.ds(off, TK), cols]) * scale
                m = _log_1m_beta(z)
                if shift is not None:
                    m = jnp.where(_causal(shift), m, 0.0)
                zs.append(z)
                ms.append(m)
            after = _cumdot(jnp.concatenate(ms, axis=0), later)
            out = []
            for hd in range(SB_HPS):
                acc, c_sum = carry[hd]
                cols = slice(hd * SB_DH, (hd + 1) * SB_DH)
                a = jnp.exp((ms[hd] + zs[hd]) + (c_sum + after[hd * TQ:(hd + 1) * TQ]))
                if shift is not None:
                    a = jnp.where(_causal(shift), a, 0.0)
                out.append((acc + _dot(a.astype(BF16), v_ref[pl.ds(off, TK), cols]),
                            c_sum + jnp.sum(ms[hd], axis=1, keepdims=True)))
            return tuple(out)

        carry = tuple((jnp.zeros((TQ, SB_DH), F32), jnp.zeros((TQ, 1), F32)) for _ in range(SB_HPS))
        for d in reversed(range(n_diag)):
            carry = block(i * n_diag + d, carry, d * TK)
        carry = lax.fori_loop(0, i * n_diag, lambda jj, c: block(i * n_diag - 1 - jj, c, None), carry)
        for hd in range(SB_HPS):
            y_ref[:, hd * SB_DH:(hd + 1) * SB_DH] = carry[hd][0].astype(BF16)
            ct_ref[hd] = carry[hd][1]

    return _call(
        body, (qkv, qkv, qkv), grid=(g, t // TQ), name=name, comm=comm,
        in_specs=[q_spec, k_spec, v_spec],
        out_specs=[q_spec, ct_spec],
        out_shape=[jax.ShapeDtypeStruct((t, D), BF16), jax.ShapeDtypeStruct((SB_H, t, 1), F32)],
        scratch_shapes=[])


def _sb_bwd(qkv, dy, ctot, name, comm=None):
    t = qkv.shape[0]
    scale = SB_DH ** -0.5
    g, w, q_spec, k_spec, v_spec, ct_spec = _sb_specs(t)
    acc_spec = pl.BlockSpec((2, t, w), lambda h, i: (0, 0, h))

    def body(q_ref, k_ref, v_ref, dy_ref, ct_ref, dq_ref, dkv_ref):
        i = pl.program_id(1)

        @pl.when(i == 0)
        def _():
            dkv_ref[...] = jnp.zeros_like(dkv_ref)

        upto = _tri2(lambda j, s: j <= s)
        n_diag = TQ // TK

        def block(j, carry, shift):
            off = pl.multiple_of(j * TK, TK)
            r0 = 0 if shift is None else shift
            nr = TQ - r0
            causal = None if shift is None else _causal(shift, r0)

            def grow(old, delta):
                return old + delta if r0 == 0 else jnp.concatenate([old[:r0], old[r0:] + delta], axis=0)

            zs, ms = [], []
            for hd in range(SB_HPS):
                cols = slice(hd * SB_DH, (hd + 1) * SB_DH)
                z = _dot_nt(q_ref[r0:, cols], k_ref[pl.ds(off, TK), cols]) * scale
                m = _log_1m_beta(z)
                if causal is not None:
                    m = jnp.where(causal, m, 0.0)
                zs.append(z)
                ms.append(m)
            m_upto = _cumdot(jnp.concatenate(ms, axis=0), upto)
            ls, a_s, es = [], [], []
            for hd in range(SB_HPS):
                cols = slice(hd * SB_DH, (hd + 1) * SB_DH)
                l = ms[hd] + zs[hd]
                a = jnp.exp(l + ((ct_ref[hd, r0:] - carry[hd][1][r0:]) - m_upto[hd * nr:(hd + 1) * nr]))
                if causal is not None:
                    a = jnp.where(causal, a, 0.0)
                ls.append(l)
                a_s.append(a)
                es.append(_dot_nt(dy_ref[r0:, cols], v_ref[pl.ds(off, TK), cols]) * a)
            e_upto = _dot(jnp.concatenate(es, axis=0).astype(BF16), upto[:TK])
            out = []
            for hd in range(SB_HPS):
                dq, p_sum, e_sum = carry[hd]
                cols = slice(hd * SB_DH, (hd + 1) * SB_DH)
                e = es[hd]
                dz = e - jnp.exp(ls[hd]) * (e_sum[r0:] + e_upto[hd * nr:(hd + 1) * nr])
                if causal is not None:
                    dz = jnp.where(causal, dz, 0.0)
                dzs = (dz * scale).astype(BF16)
                dkv_ref[0, pl.ds(off, TK), cols] += _dot_tn(dzs, q_ref[r0:, cols])
                dkv_ref[1, pl.ds(off, TK), cols] += _dot_tn(a_s[hd].astype(BF16), dy_ref[r0:, cols])
                out.append((grow(dq, _dot(dzs, k_ref[pl.ds(off, TK), cols])),
                            grow(p_sum, jnp.sum(ms[hd], axis=1, keepdims=True)),
                            grow(e_sum, jnp.sum(e, axis=1, keepdims=True))))
            return tuple(out)

        zero = jnp.zeros((TQ, 1), F32)
        init = tuple((jnp.zeros((TQ, SB_DH), F32), zero, zero) for _ in range(SB_HPS))
        carry = lax.fori_loop(0, i * n_diag, lambda j, c: block(j, c, None), init)
        for d in range(n_diag):
            carry = block(i * n_diag + d, carry, d * TK)
        for hd in range(SB_HPS):
            dq_ref[:, hd * SB_DH:(hd + 1) * SB_DH] = carry[hd][0].astype(BF16)

    return _call(
        body, (qkv, qkv, qkv, dy, ctot), grid=(g, t // TQ), name=name, comm=comm,
        in_specs=[q_spec, k_spec, v_spec, q_spec, ct_spec],
        out_specs=[q_spec, acc_spec],
        out_shape=[jax.ShapeDtypeStruct((t, D), BF16), jax.ShapeDtypeStruct((2, t, D), F32)],
        scratch_shapes=[])


def _gate_specs():
    return [pl.BlockSpec((TM, D), lambda i: (i, 3)), pl.BlockSpec((TM, D), lambda i: (i, 4))]


def _mix_pairs(mix_hbm, dsts):
    pairs = []
    for index, dst in enumerate(dsts):
        pairs += _square_pairs(mix_hbm, index, dst)
    return pairs


def _mix_out_fwd(yc, ysb, pcg, b_gate, h, w_mix, name):
    t = h.shape[0]

    def body(yc_ref, ysb_ref, gc_ref, gs_ref, b_ref, h_ref, mix_hbm,
             a_ref, b_out_ref, mg_ref, h2_ref, wc_v, wa_v, wo_v, sems):
        _load_resident(pl.program_id(0), _mix_pairs(mix_hbm, (wc_v, wa_v, wo_v)), sems)
        a = _dot(yc_ref[...], wc_v[...])
        b = _dot(ysb_ref[...], wa_v[...])
        merged = (_sigmoid(gc_ref[...] + b_ref[:, :D]) * a + _sigmoid(gs_ref[...] + b_ref[:, D:]) * b).astype(BF16)
        a_ref[...] = a
        b_out_ref[...] = b
        mg_ref[...] = merged
        h2_ref[...] = h_ref[...] + _dot(merged, wo_v[...])

    return pl.pallas_call(
        body, grid=(t // TM,), name=name,
        in_specs=[_row_spec(TM, D), _row_spec(TM, D)] + _gate_specs()
                 + [_const_spec((1, 2 * D)), _row_spec(TM, D), _ANY],
        out_specs=[_row_spec(TM, D)] * 4,
        out_shape=[jax.ShapeDtypeStruct((t, D), F32), jax.ShapeDtypeStruct((t, D), F32),
                   jax.ShapeDtypeStruct((t, D), BF16), jax.ShapeDtypeStruct((t, D), F32)],
        scratch_shapes=[pltpu.VMEM((D, D), BF16)] * 3 + [pltpu.SemaphoreType.DMA((3 * N_DEV,))],
        compiler_params=_cparams(),
    )(yc, ysb, pcg, pcg, b_gate, h, w_mix)


def _mix_out_bwd(dh2, a, b, pcg, b_gate, w_mix, name, comm=None):
    t = dh2.shape[0]

    def body(dh_ref, a_ref, b_ref, gc_ref, gs_ref, bias_ref, mix_hbm,
             dhb_ref, da_ref, db_ref, dgp_ref, dyc_ref, dysb_ref, dbias_ref, wc_v, wa_v, wo_v, sems):
        step = pl.program_id(0)
        _load_resident(step, _mix_pairs(mix_hbm, (wc_v, wa_v, wo_v)), sems)
        dhb = dh_ref[...].astype(BF16)
        dhb_ref[...] = dhb
        dm = _dot_nt(dhb, wo_v[...])
        gc = _sigmoid(gc_ref[...] + bias_ref[:, :D])
        gs = _sigmoid(gs_ref[...] + bias_ref[:, D:])
        da = (dm * gc).astype(BF16)
        db = (dm * gs).astype(BF16)
        da_ref[...] = da
        db_ref[...] = db
        dgc = dm * a_ref[...] * (gc * (1.0 - gc))
        dgs = dm * b_ref[...] * (gs * (1.0 - gs))
        dgp_ref[0] = dgc.astype(BF16)
        dgp_ref[1] = dgs.astype(BF16)
        _accumulate(dbias_ref.at[:, :D], step, jnp.sum(dgc, axis=0, keepdims=True))
        _accumulate(dbias_ref.at[:, D:], step, jnp.sum(dgs, axis=0, keepdims=True))
        dyc_ref[...] = _dot_nt(da, wc_v[...])
        dysb_ref[...] = _dot_nt(db, wa_v[...]).astype(BF16)

    return _call(
        body, (dh2, a, b, pcg, pcg, b_gate, w_mix), grid=(t // TM,), name=name, comm=comm,
        in_specs=[_row_spec(TM, D)] * 3 + _gate_specs() + [_const_spec((1, 2 * D)), _ANY],
        out_specs=[_row_spec(TM, D)] * 3 + [_blk_row_spec(2, TM, D), _row_spec(TM, D), _row_spec(TM, D),
                                            _const_spec((1, 2 * D))],
        out_shape=[jax.ShapeDtypeStruct((t, D), BF16)] * 3
                  + [jax.ShapeDtypeStruct((2, t, D), BF16), jax.ShapeDtypeStruct((t, D), F32),
                     jax.ShapeDtypeStruct((t, D), BF16), jax.ShapeDtypeStruct((1, 2 * D), F32)],
        scratch_shapes=[pltpu.VMEM((D, D), BF16)] * 3 + [pltpu.SemaphoreType.DMA((3 * N_DEV,))])


def _inproj_bwd(dconv, dq, dkv, dgp, w_in, h, g, dh_res, name, comm=None):
    t = h.shape[0]

    def body(dc_ref, dq_ref, dkv_ref, dgp_ref, w_hbm, h_ref, g_ref, dres_ref, dh_ref, dg_ref, w_v, sems):
        step = pl.program_id(0)
        _load_resident(step, [(w_hbm, w_v)], sems)
        du = _dot_nt(dq_ref[...], w_v[3])
        for k in range(3):
            du = du + _dot_nt(dc_ref[k], w_v[k])
        for k in range(2):
            du = du + _dot_nt(dkv_ref[k].astype(BF16), w_v[4 + k]) + _dot_nt(dgp_ref[k], w_v[6 + k])
        dx, dg = _rms_bwd_tile(h_ref[...], g_ref[...], du)
        dh_ref[...] = dres_ref[...] + dx
        _accumulate(dg_ref, step, dg)

    return _call(
        body, (dconv, dq, dkv, dgp, w_in, h, g, dh_res), grid=(t // TM,), name=name, comm=comm,
        in_specs=[_blk_row_spec(3, TM, D), _row_spec(TM, D), _blk_row_spec(2, TM, D), _blk_row_spec(2, TM, D), _ANY,
                  _row_spec(TM, D), _const_spec((1, D)), _row_spec(TM, D)],
        out_specs=[_row_spec(TM, D), _const_spec((1, D))],
        out_shape=[jax.ShapeDtypeStruct((t, D), F32), jax.ShapeDtypeStruct((1, D), F32)],
        scratch_shapes=[pltpu.VMEM((N_DEV, D, D), BF16), pltpu.SemaphoreType.DMA((1,))])


def _memkv_fwd(mem, g, w_ckv, name):
    m = mem.shape[0]

    def body(mem_ref, g_ref, w_ref, mn_ref, kv_ref):
        mn = _rms_fwd_tile(mem_ref[...], g_ref[...]).astype(BF16)
        mn_ref[...] = mn
        for j in range(N_DEV):
            kv_ref[j] = _dot(mn, w_ref[j]).astype(BF16)

    return pl.pallas_call(
        body, grid=(1,), name=name,
        in_specs=[_const_spec((m, D)), _const_spec((1, D)), _const_spec((N_DEV, D, X_DH))],
        out_specs=[_const_spec((m, D)), _const_spec((N_DEV, m, X_DH))],
        out_shape=[jax.ShapeDtypeStruct((m, D), BF16), jax.ShapeDtypeStruct((N_DEV, m, X_DH), BF16)],
        compiler_params=_cparams(),
    )(mem, g, w_ckv)


def _memkv_bwd(dkv, mem, g, w_ckv, name):
    m = mem.shape[0]

    def body(dkv_ref, mem_ref, g_ref, w_ref, dg_ref):
        dmn = jnp.zeros((m, D), F32)
        for j in range(N_DEV):
            dmn = dmn + _dot_nt(dkv_ref[j].astype(BF16), w_ref[j])
        _, dg = _rms_bwd_tile(mem_ref[...], g_ref[...], dmn)
        dg_ref[...] = dg

    return pl.pallas_call(
        body, grid=(1,), name=name,
        in_specs=[_const_spec((N_DEV, m, X_DH)), _const_spec((m, D)), _const_spec((1, D)),
                  _const_spec((N_DEV, D, X_DH))],
        out_specs=_const_spec((1, D)),
        out_shape=jax.ShapeDtypeStruct((1, D), F32),
        compiler_params=_cparams(),
    )(dkv, mem, g, w_ckv)


def _softmax_rows(s):
    e = jnp.exp(s - jnp.max(s, axis=-1, keepdims=True))
    return e / jnp.sum(e, axis=-1, keepdims=True)


def _cross_pairs(cross_hbm, wq_v, wo_v):
    return _square_pairs(cross_hbm, 0, wq_v) + _square_pairs(cross_hbm, 1, wo_v)


def _cross_fwd(h, g, kv, w_cross, name):
    t = h.shape[0]
    m = kv.shape[1]
    scale = X_DH ** -0.5

    def body(h_ref, g_ref, kv_ref, cross_hbm, hn_ref, qx_ref, o_ref, h3_ref, wq_v, wo_v, sems):
        _load_resident(pl.program_id(0), _cross_pairs(cross_hbm, wq_v, wo_v), sems)
        ht = h_ref[...]
        hn = _rms_fwd_tile(ht, g_ref[...]).astype(BF16)
        hn_ref[...] = hn
        qx = _dot(hn, wq_v[...]).astype(BF16)
        qx_ref[...] = qx
        for hd in range(X_H):
            lo, hi = hd * X_DH, (hd + 1) * X_DH
            p = _softmax_rows(_dot_nt(qx[:, lo:hi], kv_ref[hd]) * scale)
            o_ref[:, lo:hi] = _dot(p.astype(BF16), kv_ref[X_H + hd]).astype(BF16)
        h3_ref[...] = ht + _dot(o_ref[...], wo_v[...])

    return pl.pallas_call(
        body, grid=(t // TM,), name=name,
        in_specs=[_row_spec(TM, D), _const_spec((1, D)), _const_spec((N_DEV, m, X_DH)), _ANY],
        out_specs=[_row_spec(TM, D)] * 4,
        out_shape=[jax.ShapeDtypeStruct((t, D), BF16)] * 3 + [jax.ShapeDtypeStruct((t, D), F32)],
        scratch_shapes=[pltpu.VMEM((D, D), BF16)] * 2 + [pltpu.SemaphoreType.DMA((2 * N_DEV,))],
        compiler_params=_cparams(),
    )(h, g, kv, w_cross)


def _cross_bwd(dh3, h, g, qx, kv, w_cross, name, comm=None):
    t = h.shape[0]
    m = kv.shape[1]
    scale = X_DH ** -0.5

    def body(dh_ref, h_ref, g_ref, qx_ref, kv_ref, cross_hbm,
             dhb_ref, dqx_ref, dkv_ref, dh2_ref, dg_ref, wq_v, wo_v, sems):
        step = pl.program_id(0)
        _load_resident(step, _cross_pairs(cross_hbm, wq_v, wo_v), sems)

        @pl.when(step == 0)
        def _():
            dkv_ref[...] = jnp.zeros_like(dkv_ref)

        dht = dh_ref[...]
        dhb = dht.astype(BF16)
        dhb_ref[...] = dhb
        do = _dot_nt(dhb, wo_v[...]).astype(BF16)
        for hd in range(X_H):
            lo, hi = hd * X_DH, (hd + 1) * X_DH
            qh = qx_ref[:, lo:hi]
            kh = kv_ref[hd]
            p = _softmax_rows(_dot_nt(qh, kh) * scale)
            doh = do[:, lo:hi]
            dp = _dot_nt(doh, kv_ref[X_H + hd])
            ds = (p * (dp - jnp.sum(dp * p, axis=-1, keepdims=True)) * scale).astype(BF16)
            dqx_ref[:, lo:hi] = _dot(ds, kh).astype(BF16)
            dkv_ref[hd] += _dot_tn(ds, qh)
            dkv_ref[X_H + hd] += _dot_tn(p.astype(BF16), doh)
        dhn = _dot_nt(dqx_ref[...], wq_v[...])
        dx, dg = _rms_bwd_tile(h_ref[...], g_ref[...], dhn)
        dh2_ref[...] = dht + dx
        _accumulate(dg_ref, step, dg)

    return _call(
        body, (dh3, h, g, qx, kv, w_cross), grid=(t // TM,), name=name, comm=comm,
        in_specs=[_row_spec(TM, D), _row_spec(TM, D), _const_spec((1, D)), _row_spec(TM, D),
                  _const_spec((N_DEV, m, X_DH)), _ANY],
        out_specs=[_row_spec(TM, D), _row_spec(TM, D), _const_spec((N_DEV, m, X_DH)), _row_spec(TM, D),
                   _const_spec((1, D))],
        out_shape=[jax.ShapeDtypeStruct((t, D), BF16), jax.ShapeDtypeStruct((t, D), BF16),
                   jax.ShapeDtypeStruct((N_DEV, m, X_DH), F32), jax.ShapeDtypeStruct((t, D), F32),
                   jax.ShapeDtypeStruct((1, D), F32)],
        scratch_shapes=[pltpu.VMEM((D, D), BF16)] * 2 + [pltpu.SemaphoreType.DMA((2 * N_DEV,))])


def _loss_bwd(h, g, target, name):
    t = h.shape[0]

    def body(h_ref, g_ref, t_ref, loss_ref, dh_ref, dg_ref):
        step = pl.program_id(0)
        ht = h_ref[...]
        gain = g_ref[...]
        diff = _rms_fwd_tile(ht, gain) - t_ref[...]
        part = 0.5 * jnp.sum(jnp.sum(diff * diff, axis=-1, keepdims=True) / D, axis=0, keepdims=True)
        dx, dg = _rms_bwd_tile(ht, gain, diff / D)
        dh_ref[...] = dx
        _accumulate(loss_ref, step, jnp.broadcast_to(part, (8, 128)))
        _accumulate(dg_ref, step, dg)

    return pl.pallas_call(
        body, grid=(t // TM,), name=name,
        in_specs=[_row_spec(TM, D), _const_spec((1, D)), _row_spec(TM, D)],
        out_specs=[_const_spec((8, 128)), _row_spec(TM, D), _const_spec((1, D))],
        out_shape=[jax.ShapeDtypeStruct((8, 128), F32), jax.ShapeDtypeStruct((t, D), F32),
                   jax.ShapeDtypeStruct((1, D), F32)],
        compiler_params=_cparams(),
    )(h, g, target)


def _adamw(w, parts, m, v, name, row_block=0, token=None):
    r, c = w.shape
    n = parts.shape[0]
    tr = _pick_tile(r, (256, 352, 128))
    off = row_block * (r // tr)

    def body(*refs):
        if token is None:
            _adamw_update(None, *refs)
        else:
            _adamw_update(refs[4], *refs[:4], *refs[5:])

    spec = _row_spec(tr, c)
    in_specs = [spec, pl.BlockSpec((n, tr, c), lambda i: (0, i + off, 0)), spec, spec]
    operands = (w, parts, m, v)
    if token is not None:
        in_specs.append(_const_spec(token.shape))
        operands += (token,)
    return pl.pallas_call(
        body, grid=(r // tr,), name=name, in_specs=in_specs, out_specs=[spec] * 4,
        out_shape=[jax.ShapeDtypeStruct((r, c), F32)] * 4,
        compiler_params=_cparams(),
    )(*operands)


def _adamw_update(tok_ref, w_ref, p_ref, m_ref, v_ref, g_ref, d_ref, nm_ref, nv_ref):
    gt = p_ref[0].astype(F32)
    for k in range(1, p_ref.shape[0]):
        gt = gt + p_ref[k].astype(F32)
    if tok_ref is not None:
        gt = gt + tok_ref[0:1, 0:1]
    g_ref[...] = gt
    nm = ADAM_B1 * m_ref[...] + (1.0 - ADAM_B1) * gt
    nv = ADAM_B2 * v_ref[...] + (1.0 - ADAM_B2) * jnp.square(gt)
    m_hat = nm / (1.0 - ADAM_B1 ** ADAM_STEP)
    v_hat = nv / (1.0 - ADAM_B2 ** ADAM_STEP)
    d_ref[...] = -ADAM_LR * (m_hat / (jnp.sqrt(v_hat) + ADAM_EPS) + ADAM_WD * w_ref[...])
    nm_ref[...] = nm
    nv_ref[...] = nv


def _mesh_pos():
    return lax.axis_index("x"), lax.axis_index("y"), lax.axis_index("c")


def _no_round(in_refs, out_refs, sems):
    pass


def _run_exchange(comm, name):
    c_in, c_out = len(comm.inputs), len(comm.out_shapes)

    def body(*refs):
        cins, couts, sems = refs[:c_in], refs[c_in:c_in + c_out], refs[c_in + c_out:]
        comm.start(cins, couts, sems)
        comm.middle(cins, couts, sems)
        comm.finish(cins, couts, sems)

    return list(pl.pallas_call(
        body, name=name, out_shape=list(comm.out_shapes),
        in_specs=[_ANY] * c_in, out_specs=[_ANY] * c_out, scratch_shapes=list(comm.sem_shapes),
    )(*comm.inputs))


def _gather_exchange(shards):
    n_arr = len(shards)

    def plan(x_refs, out_refs, sems):
        send_sems, recv_sems, local_sems = sems
        x, y, c = _mesh_pos()
        me, sibling = (x, y, c), (x, y, 1 - c)
        xn, yn, diag = (1 - x, y), (x, 1 - y), (1 - x, 1 - y)

        def slot(a, px, py, pc, half=None):
            ref = out_refs[a].at[4 * px + 2 * py + pc]
            if half is None:
                return ref
            rows = shards[a].shape[0] // 2
            return ref.at[half * rows:(half + 1) * rows]

        def copy(a, k, block, to, half=None, src=None):
            dst = slot(a, *block, half)
            return pltpu.make_async_remote_copy(
                src_ref=dst if src is None else src, dst_ref=dst,
                send_sem=send_sems.at[a, k], recv_sem=recv_sems.at[a, k],
                device_id=to, device_id_type=pl.DeviceIdType.MESH)

        return types.SimpleNamespace(
            me=me, sibling=sibling, xn=xn, yn=yn, diag=diag, c=c, copy=copy,
            mine=[pltpu.make_async_copy(x_refs[a], slot(a, *me), local_sems.at[a]) for a in range(n_arr)],
            first=[cp for a in range(n_arr) for cp in (
                copy(a, 0, me, sibling, src=x_refs[a]), copy(a, 1, me, (*xn, c), src=x_refs[a]),
                copy(a, 2, me, (*yn, c), src=x_refs[a]))],
            second=lambda a: (copy(a, 3, (*xn, c), (*yn, c), half=0), copy(a, 5, (*xn, c), sibling),
                              copy(a, 4, (*yn, c), (*xn, c), half=1), copy(a, 6, (*yn, c), sibling)),
            third=lambda a: (copy(a, 7, (*diag, c), sibling, half=0), copy(a, 8, (*diag, c), sibling, half=1)))

    def start(x_refs, out_refs, sems):
        p = plan(x_refs, out_refs, sems)
        for cp in p.mine + p.first:
            cp.start()

    def middle(x_refs, out_refs, sems):
        p = plan(x_refs, out_refs, sems)
        for a in range(n_arr):
            to_yn, x_to_sib, to_xn, y_to_sib = p.second(a)
            p.copy(a, 1, (*p.xn, p.c), p.me).wait_recv()
            to_yn.start()
            x_to_sib.start()
            p.copy(a, 2, (*p.yn, p.c), p.me).wait_recv()
            to_xn.start()
            y_to_sib.start()

    def finish(x_refs, out_refs, sems):
        p = plan(x_refs, out_refs, sems)
        for a in range(n_arr):
            half0_to_sib, half1_to_sib = p.third(a)
            p.copy(a, 3, (*p.diag, p.c), p.me, half=0).wait_recv()
            half0_to_sib.start()
            p.copy(a, 4, (*p.diag, p.c), p.me, half=1).wait_recv()
            half1_to_sib.start()
        other = 1 - p.c
        for a in range(n_arr):
            p.copy(a, 0, p.sibling, p.me).wait_recv()
            p.copy(a, 5, (*p.xn, other), p.me).wait_recv()
            p.copy(a, 6, (*p.yn, other), p.me).wait_recv()
            p.copy(a, 7, (*p.diag, other), p.me, half=0).wait_recv()
            p.copy(a, 8, (*p.diag, other), p.me, half=1).wait_recv()
        for cp in p.first:
            cp.wait_send()
        for a in range(n_arr):
            for cp in p.second(a) + p.third(a):
                cp.wait_send()
        for cp in p.mine:
            cp.wait()

    return types.SimpleNamespace(
        inputs=list(shards), start=start, middle=middle, finish=finish,
        out_shapes=[jax.ShapeDtypeStruct((N_DEV,) + s.shape, s.dtype) for s in shards],
        sem_shapes=[pltpu.SemaphoreType.DMA((n_arr, 9)), pltpu.SemaphoreType.DMA((n_arr, 9)),
                    pltpu.SemaphoreType.DMA((n_arr,))])


def _pair_exchange(grads):
    n_arr = len(grads)

    def plan(g_refs, land_refs, sems):
        send_sems, recv_sems = sems
        x, y, c = _mesh_pos()
        return [pltpu.make_async_remote_copy(
            src_ref=g_refs[a].at[2 * k + 1 - c], dst_ref=land_refs[a].at[k],
            send_sem=send_sems.at[a, k], recv_sem=recv_sems.at[a, k],
            device_id=(x, y, 1 - c), device_id_type=pl.DeviceIdType.MESH)
            for a in range(n_arr) for k in range(N_CHIP)]

    def start(g_refs, land_refs, sems):
        for cp in plan(g_refs, land_refs, sems):
            cp.start()

    def finish(g_refs, land_refs, sems):
        for cp in plan(g_refs, land_refs, sems):
            cp.wait()

    return types.SimpleNamespace(
        inputs=list(grads), start=start, middle=_no_round, finish=finish,
        out_shapes=[jax.ShapeDtypeStruct((N_CHIP,) + g.shape[1:], g.dtype) for g in grads],
        sem_shapes=[pltpu.SemaphoreType.DMA((n_arr, N_CHIP)), pltpu.SemaphoreType.DMA((n_arr, N_CHIP))])


def _chip_exchange(parts):
    n_arr = len(parts)

    def plan(p_refs, land_refs, sems):
        send_sems, recv_sems, local_sems = sems
        x, y, c = _mesh_pos()
        my_chip = 2 * x + y
        chips = [(1 - x, y), (x, 1 - y), (1 - x, 1 - y)]
        local = [pltpu.make_async_copy(p_refs[a].at[my_chip], land_refs[a].at[my_chip], local_sems.at[a])
                 for a in range(n_arr)]

        def copy(a, k, src_slot, dst_slot, px, py):
            return pltpu.make_async_remote_copy(
                src_ref=p_refs[a].at[src_slot], dst_ref=land_refs[a].at[dst_slot],
                send_sem=send_sems.at[a, k], recv_sem=recv_sems.at[a, k],
                device_id=(px, py, c), device_id_type=pl.DeviceIdType.MESH)

        sends = [copy(a, k, 2 * px + py, my_chip, px, py) for a in range(n_arr) for k, (px, py) in enumerate(chips)]
        arrivals = [copy(a, k, my_chip, 2 * px + py, px, py) for a in range(n_arr)
                    for k, (px, py) in enumerate(chips)]
        return local, sends, arrivals

    def start(p_refs, land_refs, sems):
        local, sends, _ = plan(p_refs, land_refs, sems)
        for cp in local + sends:
            cp.start()

    def finish(p_refs, land_refs, sems):
        local, sends, arrivals = plan(p_refs, land_refs, sems)
        for cp in arrivals:
            cp.wait_recv()
        for cp in sends:
            cp.wait_send()
        for cp in local:
            cp.wait()

    return types.SimpleNamespace(
        inputs=list(parts), start=start, middle=_no_round, finish=finish,
        out_shapes=[jax.ShapeDtypeStruct(p.shape, p.dtype) for p in parts],
        sem_shapes=[pltpu.SemaphoreType.DMA((n_arr, 3)), pltpu.SemaphoreType.DMA((n_arr, 3)),
                    pltpu.SemaphoreType.DMA((n_arr,))])


_HBM = pl.BlockSpec(memory_space=pltpu.HBM)
_SEM = pl.BlockSpec(memory_space=pltpu.SEMAPHORE)
_DATAFLOW = pltpu.SideEffectType.DATAFLOW_SIDE_EFFECTING


def _own_chip_slots(parts, name):
    n_arr = len(parts)

    def body(*refs):
        p_refs, land_refs, sems = refs[:n_arr], refs[n_arr:2 * n_arr], refs[2 * n_arr]
        x, y, _ = _mesh_pos()
        my_chip = 2 * x + y
        copies = [pltpu.make_async_copy(p_refs[a].at[my_chip], land_refs[a].at[my_chip], sems.at[a])
                  for a in range(n_arr)]
        for cp in copies:
            cp.start()
        for cp in copies:
            cp.wait()

    return list(pl.pallas_call(
        body, name=name, out_shape=[jax.ShapeDtypeStruct(p.shape, p.dtype) for p in parts],
        in_specs=[_ANY] * n_arr, out_specs=[_ANY] * n_arr, scratch_shapes=[pltpu.SemaphoreType.DMA((n_arr,))],
    )(*parts))


def _chip_copies(p_refs, land_refs, send_sems, recv_sems):
    x, y, c = _mesh_pos()
    my_chip = 2 * x + y
    chips = [(1 - x, y), (x, 1 - y), (1 - x, 1 - y)]
    return [pltpu.make_async_remote_copy(
        src_ref=p_refs[a].at[2 * px + py], dst_ref=land_refs[a].at[my_chip],
        send_sem=send_sems[3 * a + k], recv_sem=recv_sems[3 * a + k],
        device_id=(px, py, c), device_id_type=pl.DeviceIdType.MESH)
        for a in range(len(p_refs)) for k, (px, py) in enumerate(chips)]


def _chip_exchange_begin(parts, lands, name):
    n_arr = len(parts)
    n_buf, n_copy = 2 * n_arr, 3 * n_arr

    def body(*refs):
        p_refs, land_refs = refs[:n_arr], refs[n_arr:n_buf]
        send_sems, recv_sems, token = refs[n_buf:n_buf + n_copy], refs[n_buf + n_copy:n_buf + 2 * n_copy], refs[-1]
        for cp in _chip_copies(p_refs, land_refs, send_sems, recv_sems):
            cp.start()
        token[...] = jnp.zeros_like(token)

    bufs = list(parts) + list(lands)
    outs = pl.pallas_call(
        body, name=name,
        out_shape=(*[pltpu.SemaphoreType.DMA(())] * (2 * n_copy), *[pltpu.HBM(b.shape, b.dtype) for b in bufs],
                   jax.ShapeDtypeStruct((8, 128), F32)),
        in_specs=[_HBM] * n_buf,
        out_specs=(*[_SEM] * (2 * n_copy), *[_HBM] * n_buf, pl.BlockSpec(memory_space=pltpu.VMEM)),
        input_output_aliases={i: 2 * n_copy + i for i in range(n_buf)},
        compiler_params=pltpu.CompilerParams(has_side_effects=_DATAFLOW),
    )(*[pltpu.with_memory_space_constraint(b, pltpu.HBM) for b in bufs])
    sems = list(outs[:2 * n_copy])
    thru = list(outs[2 * n_copy:2 * n_copy + n_buf])
    return sems[:n_copy], sems[n_copy:], thru[:n_arr], thru[n_arr:], outs[-1]


def _chip_exchange_end(send_sems, recv_sems, parts, lands, after, name):
    n_arr = len(parts)
    n_buf, n_copy = 2 * n_arr, 3 * n_arr

    def body(*refs):
        p_refs, land_refs = refs[:n_arr], refs[n_arr:n_buf]
        sems = refs[n_buf:n_buf + 2 * n_copy]
        for cp in _chip_copies(p_refs, land_refs, sems[:n_copy], sems[n_copy:]):
            cp.wait_send()
            cp.wait_recv()

    bufs = list(parts) + list(lands)
    outs = pl.pallas_call(
        body, name=name, out_shape=tuple(pltpu.HBM(b.shape, b.dtype) for b in bufs),
        in_specs=[_HBM] * n_buf + [_SEM] * (2 * n_copy) + [_ANY], out_specs=tuple([_HBM] * n_buf),
        input_output_aliases={i: i for i in range(n_buf)},
        compiler_params=pltpu.CompilerParams(has_side_effects=_DATAFLOW),
    )(*bufs, *send_sems, *recv_sems, after)
    return list(outs[n_arr:])


def _row_tile(r, cap=640):
    best = None
    for cand in range(16, min(r, cap) + 1, 16):
        if r % cand == 0:
            best = cand
    return best if best is not None else r


def _pair_sum(g, landed, core, name):
    _, r, c_dim = g.shape
    tr = _row_tile(r)

    def body(core_ref, mine_ref, theirs_ref, o_ref):
        o_ref[0] = (mine_ref[0].astype(F32) + theirs_ref[0].astype(F32)).astype(o_ref.dtype)

    return pl.pallas_call(
        body, name=name,
        grid_spec=pltpu.PrefetchScalarGridSpec(
            num_scalar_prefetch=1, grid=(N_CHIP, r // tr),
            in_specs=[pl.BlockSpec((1, tr, c_dim), lambda k, i, core_ref: (2 * k + core_ref[0], i, 0)),
                      pl.BlockSpec((1, tr, c_dim), lambda k, i, core_ref: (k, i, 0))],
            out_specs=pl.BlockSpec((1, tr, c_dim), lambda k, i, core_ref: (k, i, 0))),
        out_shape=jax.ShapeDtypeStruct((N_CHIP, r, c_dim), g.dtype),
        compiler_params=_cparams(2),
    )(core, g, landed)


def _sum_slots(parts, name):
    n, r, c_dim = parts.shape
    tr = _row_tile(r)

    def body(p_ref, o_ref):
        acc = p_ref[0].astype(F32)
        for k in range(1, n):
            acc = acc + p_ref[k].astype(F32)
        o_ref[...] = acc

    return pl.pallas_call(
        body, grid=(r // tr,), name=name,
        in_specs=[pl.BlockSpec((n, tr, c_dim), lambda i: (0, i, 0))],
        out_specs=_row_spec(tr, c_dim),
        out_shape=jax.ShapeDtypeStruct((r, c_dim), F32),
        compiler_params=_cparams(),
    )(parts)


GAINS = ("g_ffn1", "g_mix", "g_cross", "g_mem", "g_ffn2", "g_final")
SMALL = GAINS + ("b_gate", "conv_w")
SMALL_R = 16
WEIGHT_ORDER = ("g_ffn1", "w_ffn1_gu", "w_ffn1_down", "g_mix", "w_in", "b_gate", "conv_w", "w_conv_out",
                "w_attn_out", "w_o", "g_cross", "g_mem", "w_cq", "w_ckv", "w_co", "g_ffn2", "w_ffn2_gu",
                "w_ffn2_down", "g_final")
GU_NAMES = ("w_ffn1_gu", "w_ffn2_gu")


def _pack_small(vals, conv_rows):
    rows = [vals[n].reshape(1, D) for n in GAINS] + [vals["b_gate"].reshape(2, D), conv_rows.reshape(CONV_K, D)]
    used = len(GAINS) + 2 + CONV_K
    return jnp.concatenate(rows + [jnp.zeros((SMALL_R - used, D), F32)], axis=0)


def _unpack_small(buf):
    out = {n: buf[k] for k, n in enumerate(GAINS)}
    out["b_gate"] = buf[6:8].reshape(2 * D)
    out["conv_w"] = buf[8:8 + CONV_K]
    return out


def _exchange_shards(wts):
    out = {n: jnp.pad(wts[n].T.astype(BF16), ((0, FF_PAD - FF_BLK), (0, 0))) for n in GU_NAMES}
    for n in ("w_ckv", "w_in", "w_ffn1_down", "w_ffn2_down"):
        out[n] = wts[n].astype(BF16)
    out["mix"] = jnp.concatenate([wts[n].astype(BF16) for n in MIX_MATS], axis=0)
    out["cross"] = jnp.concatenate([wts[n].astype(BF16) for n in CROSS_MATS], axis=0)
    return out


def _by_device(dw):
    return dw.reshape(N_DEV, SQ_ROWS, D)


def _reduce_group(grads, landed, core, names):
    return [_pair_sum(g, l, core, "grads_pair_sum_" + n) for g, l, n in zip(grads, landed, names)]


def _step(x, mem, target, sh, conv_pad, gains, b_gate, core):
    wg1, wd1, conv_all = _run_exchange(_gather_exchange([sh["w_ffn1_gu"], sh["w_ffn1_down"], conv_pad]), "gather_ffn1")
    conv_w = conv_all[:, :CONV_K, :].transpose(1, 0, 2).reshape(CONV_K, D)
    (n1, gate1, up1, act1, h1), (w_in,) = _ffn_fwd(
        x, gains["g_ffn1"], wg1, wd1, "ffn1_fwd", comm=_gather_exchange([sh["w_in"]]))
    (u, pcg, qkv), (w_mix,) = _inproj_fwd(h1, gains["g_mix"], w_in, "inproj_fwd", comm=_gather_exchange([sh["mix"]]))
    yc = _conv_fwd(pcg, conv_w, "conv_fwd")
    (ysb, ctot), (w_cross, w_ckv, wg2, wd2) = _sb_fwd(
        qkv, "sb_fwd", comm=_gather_exchange([sh["cross"], sh["w_ckv"], sh["w_ffn2_gu"], sh["w_ffn2_down"]]))
    a_mix, b_mix, merged, h2 = _mix_out_fwd(yc, ysb, pcg, b_gate, h1, w_mix, "mix_out_fwd")
    mn, kv = _memkv_fwd(mem, gains["g_mem"], w_ckv, "memkv_fwd")
    hn, qx, o_x, h3 = _cross_fwd(h2, gains["g_cross"], kv, w_cross, "cross_fwd")
    (n4, gate2, up2, act2, h4), _ = _ffn_fwd(h3, gains["g_ffn2"], wg2, wd2, "ffn2_fwd")
    loss, dh4, dg_final = _loss_bwd(h4, gains["g_final"], target, "loss_bwd")

    gs = {"g_final": dg_final}
    (dgu2, dh4b, dh3, gs["g_ffn2"]), _ = _ffn_bwd(dh4, h3, gains["g_ffn2"], gate2, up2, wg2, wd2, "ffn2_bwd")
    grads_a = [_mm_tn_rows(dgu2, n4, FF_PAD, "dw_ffn2_gu"),
               _mm_tn_rows(act2, dh4b, FF_BLK, "dw_ffn2_down").reshape(N_DEV, DOWN_ROWS, D)]
    names_a = ["w_ffn2_gu", "w_ffn2_down"]
    (dh3b, dqx, dkv, dh2, gs["g_cross"]), landed_a = _cross_bwd(
        dh3, h2, gains["g_cross"], qx, kv, w_cross, "cross_bwd", comm=_pair_exchange(grads_a))
    sums_a = _reduce_group(grads_a, landed_a, core, names_a)
    grads_b = [_mm_tn_cols(mn, dkv, "dw_ckv"),
               jnp.concatenate([_by_device(_mm_tn(hn, dqx, "dw_cq")), _by_device(_mm_tn(o_x, dh3b, "dw_co"))], axis=1)]
    names_b = ["w_ckv", "cross"]
    gs["g_mem"] = _memkv_bwd(dkv, mem, gains["g_mem"], w_ckv, "memkv_bwd")
    (dh2b, da_mix, db_mix, dgp, dyc, dysb, gs["b_gate"]), landed_b = _mix_out_bwd(
        dh2, a_mix, b_mix, pcg, b_gate, w_mix, "mix_out_bwd", comm=_pair_exchange(grads_b))
    sums_b = _reduce_group(grads_b, landed_b, core, names_b)
    grads_c = [jnp.concatenate([_by_device(_mm_tn(yc, da_mix, "dw_conv_out")),
                                _by_device(_mm_tn(ysb, db_mix, "dw_attn_out")),
                                _by_device(_mm_tn(merged, dh2b, "dw_o"))], axis=1)]
    landed_c = _run_exchange(_pair_exchange(grads_c), "grads_to_sibling_mix")
    sums_c = _reduce_group(grads_c, landed_c, core, ["mix"])
    (dq, dkv_sb), parts_abc = _sb_bwd(qkv, dysb, ctot, "sb_bwd", comm=_chip_exchange(sums_a + sums_b + sums_c))
    dconv, gs["conv_w"] = _conv_bwd(pcg, conv_w, dyc, "conv_bwd")
    grads_d = [jnp.concatenate(
        [_mm_tn_cols(u, dconv, "dw_in_conv"), _mm_tn(u, dq, "dw_in_q")[None], _mm_tn_cols(u, dkv_sb, "dw_in_kv"),
         _mm_tn_cols(u, dgp, "dw_in_gates")], axis=0)]
    (dh1, gs["g_mix"]), landed_d = _inproj_bwd(dconv, dq, dkv_sb, dgp, w_in, h1, gains["g_mix"], dh2, "inproj_bwd",
                                               comm=_pair_exchange(grads_d))
    sums_d = _reduce_group(grads_d, landed_d, core, ["w_in"])
    (dgu1, dh1b, dx, gs["g_ffn1"]), parts_d = _ffn_bwd(dh1, x, gains["g_ffn1"], gate1, up1, wg1, wd1, "ffn1_bwd",
                                                       comm=_chip_exchange(sums_d))
    grads_e = [_mm_tn_rows(dgu1, n1, FF_PAD, "dw_ffn1_gu"),
               _mm_tn_rows(act1, dh1b, FF_BLK, "dw_ffn1_down").reshape(N_DEV, DOWN_ROWS, D)]
    names_e = ["w_ffn1_gu", "w_ffn1_down"]
    landed_e = _run_exchange(_pair_exchange(grads_e), "grads_to_sibling_ffn1")
    sums_e = dict(zip(names_e, _reduce_group(grads_e, landed_e, core, names_e)))

    names = names_a + names_b + ["mix"] + ["w_in"]
    return loss, dx, dict(zip(names, parts_abc + parts_d)), sums_e, gs


def kernel(x, mem, g_ffn1, w_ffn1_gu, w_ffn1_down, g_mix, w_in, b_gate, conv_w, w_conv_out, w_attn_out, w_o, g_cross, g_mem, w_cq, w_ckv, w_co, g_ffn2, w_ffn2_gu, w_ffn2_down, g_final, loss_target, m_g_ffn1, m_w_ffn1_gu, m_w_ffn1_down, m_g_mix, m_w_in, m_b_gate, m_conv_w, m_w_conv_out, m_w_attn_out, m_w_o, m_g_cross, m_g_mem, m_w_cq, m_w_ckv, m_w_co, m_g_ffn2, m_w_ffn2_gu, m_w_ffn2_down, m_g_final, v_g_ffn1, v_w_ffn1_gu, v_w_ffn1_down, v_g_mix, v_w_in, v_b_gate, v_conv_w, v_w_conv_out, v_w_attn_out, v_w_o, v_g_cross, v_g_mem, v_w_cq, v_w_ckv, v_w_co, v_g_ffn2, v_w_ffn2_gu, v_w_ffn2_down, v_g_final):
    args = locals()
    wts = {n: args[n] for n in WEIGHT_ORDER}
    mom1 = {n: args["m_" + n] for n in WEIGHT_ORDER}
    mom2 = {n: args["v_" + n] for n in WEIGHT_ORDER}
    cx, cy, cc = _mesh_pos()
    dev = 4 * cx + 2 * cy + cc
    conv_cols = D // N_DEV

    conv_pad = jnp.concatenate([conv_w, jnp.zeros((SMALL_R - CONV_K, conv_cols), F32)], axis=0)
    gains = {n: wts[n].reshape(1, D) for n in GAINS}
    loss8, dx, parts, sums_e, gs = _step(x[0], mem[0], loss_target[0], _exchange_shards(wts), conv_pad, gains,
                                 b_gate.reshape(1, 2 * D), cc.reshape(1).astype(jnp.int32))
    loss = lax.psum(loss8[0, 0], MESH_AXES)

    grads, delta, new_m, new_v = {}, {}, {}, {}

    def operands(n, transposed):
        trio = (wts[n], mom1[n], mom2[n])
        return tuple(a.T for a in trio) if transposed else trio

    def record(n, res, transposed):
        grads[n], delta[n], new_m[n], new_v[n] = [r.T for r in res] if transposed else res

    early = [("w_ffn2_gu", "w_ffn2_gu", 0, True), ("w_ffn2_down", "w_ffn2_down", 0, False),
             ("w_ckv", "w_ckv", 0, False), ("w_in", "w_in", 0, False)]
    early += [(n, "mix", k, False) for k, n in enumerate(MIX_MATS)]
    early += [(n, "cross", k, False) for k, n in enumerate(CROSS_MATS)]
    last_names = ["w_ffn1_gu", "w_ffn1_down"]
    last_sums = [sums_e[n] for n in last_names]
    send_sems, recv_sems, sums_thru, lands_thru, token = _chip_exchange_begin(
        last_sums, _own_chip_slots(last_sums, "grads_own_slot_ffn1"), "grads_to_chips_ffn1_begin")
    for n, buf, row_block, transposed in early:
        w, m1, m2 = operands(n, transposed)
        record(n, _adamw(w, parts[buf], m1, m2, "adamw_" + n, row_block, token=token), transposed)
    after = jnp.concatenate([new_v[n][:1, :1] for n, _, _, _ in early], axis=0)
    last_parts = _chip_exchange_end(send_sems, recv_sems, sums_thru, lands_thru, after, "grads_to_chips_ffn1_end")
    for n, p, transposed in zip(last_names, last_parts, (True, False)):
        w, m1, m2 = operands(n, transposed)
        record(n, _adamw(w, p, m1, m2, "adamw_" + n), transposed)

    gs_rows = {n: gs[n] for n in GAINS + ("b_gate",)}
    small_all = _run_exchange(_gather_exchange([_pack_small(gs_rows, gs["conv_w"][:CONV_K])]), "gather_small_grads")[0]
    grad_small = _unpack_small(_sum_slots(small_all, "small_grads_sum"))
    grad_small["conv_w"] = lax.dynamic_slice_in_dim(grad_small["conv_w"], dev * conv_cols, conv_cols, axis=1)
    grads.update(grad_small)

    def small_buf(vals):
        return _pack_small(vals, jnp.concatenate([vals["conv_w"], jnp.zeros((CONV_K, D - conv_cols), F32)], axis=1))

    _, d_s, m_s, v_s = _adamw(small_buf(wts), small_buf(grads)[None], small_buf(mom1), small_buf(mom2), "adamw_small")
    for res, buf in ((delta, d_s), (new_m, m_s), (new_v, v_s)):
        un = _unpack_small(buf)
        for n in GAINS + ("b_gate",):
            res[n] = un[n]
        res["conv_w"] = un["conv_w"][:, :conv_cols]

    return (loss, dx[None], *[grads[n] for n in WEIGHT_ORDER], *[delta[n] for n in WEIGHT_ORDER],
            *[new_m[n] for n in WEIGHT_ORDER], *[new_v[n] for n in WEIGHT_ORDER])
```

```python
import types

import jax
import jax.numpy as jnp
from jax import lax
from jax.experimental import pallas as pl
from jax.experimental.pallas import tpu as pltpu

F32 = jnp.float32
BF16 = jnp.bfloat16

D = 1024
DFF = 2816
SB_H = 8
SB_DH = 128
X_H = 4
X_DH = 256
CONV_K = 3
RMS_EPS = 1e-6
N_DEV = 8
N_CHIP = 4
SQ_ROWS = D // N_DEV

ADAM_LR = 0.001
ADAM_B1 = 0.9
ADAM_B2 = 0.999
ADAM_EPS = 1e-08
ADAM_WD = 0.01
ADAM_STEP = 10

TM = 256
TQ = 512
TK = 256
SB_HPS = 2
VMEM_LIMIT = 56 << 20

FF_BLK = DFF // 4
FF_PAD = 768
FF_SUB = 256
DOWN_ROWS = DFF // N_DEV

MIX_MATS = ("w_conv_out", "w_attn_out", "w_o")
CROSS_MATS = ("w_cq", "w_co")

MESH_AXES = ("x", "y", "c")
_ANY = pl.BlockSpec(memory_space=pl.ANY)


def _cparams(n_axes=1):
    return pltpu.CompilerParams(
        dimension_semantics=("arbitrary",) * n_axes, vmem_limit_bytes=VMEM_LIMIT)


def _row_spec(tm, n):
    return pl.BlockSpec((tm, n), lambda i: (i, 0))


def _blk_row_spec(nb, tm, n):
    return pl.BlockSpec((nb, tm, n), lambda i: (0, i, 0))


def _const_spec(shape):
    zeros = (0,) * len(shape)
    return pl.BlockSpec(shape, lambda i: zeros)


def _dot(a, b):
    return jnp.dot(a, b, preferred_element_type=F32)


def _dot_nt(a, b):
    return lax.dot_general(a, b, (((1,), (1,)), ((), ())), preferred_element_type=F32)


def _dot_tn(a, b):
    return lax.dot_general(a, b, (((0,), (0,)), ((), ())), preferred_element_type=F32)


def _sigmoid(x):
    return 1.0 / (1.0 + jnp.exp(-x))


def _call(body, operands, *, grid, in_specs, out_specs, out_shape, scratch_shapes, name, comm=None):
    n_in, n_out, n_sc = len(in_specs), len(out_specs), len(scratch_shapes)
    if comm is None:
        outs = pl.pallas_call(
            body, grid=grid, name=name, in_specs=in_specs, out_specs=out_specs, out_shape=out_shape,
            scratch_shapes=scratch_shapes, compiler_params=_cparams(len(grid)))(*operands)
        return list(outs), []
    c_in, c_out, c_sem = len(comm.inputs), len(comm.out_shapes), len(comm.sem_shapes)

    def hosted(*refs):
        bounds = [0, n_in, c_in, n_out, c_out, n_sc, c_sem]
        parts, pos = [], 0
        for k in bounds[1:]:
            parts.append(refs[pos:pos + k])
            pos += k
        ins, cins, outs, couts, scr, sems = parts
        step, n_steps = pl.program_id(0), grid[0]
        for ax in range(1, len(grid)):
            step, n_steps = step * grid[ax] + pl.program_id(ax), n_steps * grid[ax]

        @pl.when(step == 0)
        def _():
            comm.start(cins, couts, sems)

        @pl.when(step == (2 * n_steps) // 3)
        def _():
            comm.middle(cins, couts, sems)

        body(*ins, *outs, *scr)

        @pl.when(step == n_steps - 1)
        def _():
            comm.finish(cins, couts, sems)

    res = pl.pallas_call(
        hosted, grid=grid, name=name, in_specs=list(in_specs) + [_ANY] * c_in,
        out_specs=list(out_specs) + [_ANY] * c_out, out_shape=list(out_shape) + list(comm.out_shapes),
        scratch_shapes=list(scratch_shapes) + list(comm.sem_shapes),
        compiler_params=_cparams(len(grid)))(*operands, *comm.inputs)
    return list(res[:n_out]), list(res[n_out:])


def _load_resident(step, pairs, sems):
    @pl.when(step == 0)
    def _():
        copies = [pltpu.make_async_copy(src, dst, sems.at[k]) for k, (src, dst) in enumerate(pairs)]
        for cp in copies:
            cp.start()
        for cp in copies:
            cp.wait()


def _square_pairs(buf_hbm, index, dst):
    off = index * SQ_ROWS
    return [(buf_hbm.at[d, off:off + SQ_ROWS, :], dst.at[d * SQ_ROWS:(d + 1) * SQ_ROWS, :]) for d in range(N_DEV)]


def _down_pairs(wd_hbm, dst):
    return [(wd_hbm.at[d], dst.at[d // 2, (d % 2) * DOWN_ROWS:(d % 2 + 1) * DOWN_ROWS, :]) for d in range(N_DEV)]


def _zero_down_pad(step, dst):
    @pl.when(step == 0)
    def _():
        dst[:, FF_BLK:, :] = jnp.zeros((4, FF_PAD - FF_BLK, D), BF16)


def _rms_fwd_tile(xt, g):
    r = lax.rsqrt(jnp.mean(xt * xt, axis=-1, keepdims=True) + RMS_EPS)
    return (xt * r) * g


def _rms_bwd_tile(xt, g, dn):
    r = lax.rsqrt(jnp.mean(xt * xt, axis=-1, keepdims=True) + RMS_EPS)
    xhat = xt * r
    dxhat = dn * g
    dx = r * (dxhat - xhat * jnp.mean(dxhat * xhat, axis=-1, keepdims=True))
    dg = jnp.sum(dn * xhat, axis=0, keepdims=True)
    return dx, dg


def _accumulate(ref, step, value):
    @pl.when(step == 0)
    def _():
        ref[...] = value

    @pl.when(step != 0)
    def _():
        ref[...] = ref[...] + value


def _ffn_fwd(x, g, wgu, wd, name, comm=None):
    t = x.shape[0]

    def body(x_ref, g_ref, wgu_hbm, wd_hbm, n_ref, gate_ref, up_ref, act_ref, h_ref, wgu_v, wd_v, sems):
        step = pl.program_id(0)
        _zero_down_pad(step, wd_v)
        _load_resident(step, [(wgu_hbm, wgu_v)] + _down_pairs(wd_hbm, wd_v), sems)
        xt = x_ref[...]
        n = _rms_fwd_tile(xt, g_ref[...]).astype(BF16)
        n_ref[...] = n
        acc = jnp.zeros((TM, D), F32)
        for j in range(4):
            for s in range(FF_PAD // FF_SUB):
                lo, hi = s * FF_SUB, (s + 1) * FF_SUB
                gt = _dot_nt(n, wgu_v[j, lo:hi, :])
                ut = _dot_nt(n, wgu_v[4 + j, lo:hi, :])
                gate_ref[j, :, lo:hi] = gt.astype(BF16)
                up_ref[j, :, lo:hi] = ut.astype(BF16)
                act_ref[j, :, lo:hi] = ((gt * _sigmoid(gt)) * ut).astype(BF16)
            acc = acc + _dot(act_ref[j], wd_v[j])
        h_ref[...] = xt + 0.5 * acc

    ff = jax.ShapeDtypeStruct((4, t, FF_PAD), BF16)
    return _call(
        body, (x, g, wgu, wd), grid=(t // TM,), name=name, comm=comm,
        in_specs=[_row_spec(TM, D), _const_spec((1, D)), _ANY, _ANY],
        out_specs=[_row_spec(TM, D)] + [_blk_row_spec(4, TM, FF_PAD)] * 3 + [_row_spec(TM, D)],
        out_shape=[jax.ShapeDtypeStruct((t, D), BF16), ff, ff, ff, jax.ShapeDtypeStruct((t, D), F32)],
        scratch_shapes=[pltpu.VMEM((N_DEV, FF_PAD, D), BF16), pltpu.VMEM((4, FF_PAD, D), BF16),
                        pltpu.SemaphoreType.DMA((1 + N_DEV,))])


def _ffn_bwd(dh, xin, g, gate, up, wgu, wd, name, comm=None):
    t = dh.shape[0]

    def body(dh_ref, x_ref, g_ref, gate_ref, up_ref, wgu_hbm, wd_hbm,
             dgu_ref, dhb_ref, dx_ref, dg_ref, wgu_v, wd_v, sems):
        step = pl.program_id(0)
        _zero_down_pad(step, wd_v)
        _load_resident(step, [(wgu_hbm, wgu_v)] + _down_pairs(wd_hbm, wd_v), sems)
        dht = dh_ref[...]
        dhb = (0.5 * dht).astype(BF16)
        dhb_ref[...] = dhb
        dn = jnp.zeros((TM, D), F32)
        for j in range(4):
            for s in range(FF_PAD // FF_SUB):
                lo, hi = s * FF_SUB, (s + 1) * FF_SUB
                da = _dot_nt(dhb, wd_v[j, lo:hi, :])
                gt = gate_ref[j, :, lo:hi].astype(F32)
                ut = up_ref[j, :, lo:hi].astype(F32)
                sg = _sigmoid(gt)
                dgt = (da * ut * (sg * (1.0 + gt * (1.0 - sg)))).astype(BF16)
                dut = (da * (gt * sg)).astype(BF16)
                dgu_ref[j, :, lo:hi] = dgt
                dgu_ref[4 + j, :, lo:hi] = dut
            dn = dn + _dot(dgu_ref[j], wgu_v[j]) + _dot(dgu_ref[4 + j], wgu_v[4 + j])
        dx, dg = _rms_bwd_tile(x_ref[...], g_ref[...], dn)
        dx_ref[...] = dht + dx
        _accumulate(dg_ref, step, dg)

    return _call(
        body, (dh, xin, g, gate, up, wgu, wd), grid=(t // TM,), name=name, comm=comm,
        in_specs=[_row_spec(TM, D), _row_spec(TM, D), _const_spec((1, D)), _blk_row_spec(4, TM, FF_PAD),
                  _blk_row_spec(4, TM, FF_PAD), _ANY, _ANY],
        out_specs=[_blk_row_spec(N_DEV, TM, FF_PAD), _row_spec(TM, D), _row_spec(TM, D), _const_spec((1, D))],
        out_shape=[jax.ShapeDtypeStruct((N_DEV, t, FF_PAD), BF16), jax.ShapeDtypeStruct((t, D), BF16),
                   jax.ShapeDtypeStruct((t, D), F32), jax.ShapeDtypeStruct((1, D), F32)],
        scratch_shapes=[pltpu.VMEM((N_DEV, FF_PAD, D), BF16), pltpu.VMEM((4, FF_PAD, D), BF16),
                        pltpu.SemaphoreType.DMA((1 + N_DEV,))])


WIDE_TILES = (1024, 512, 256, 128)


def _pick_tile(n, options=(512, 256, 128)):
    for o in options:
        if n % o == 0:
            return o
    return n


def _mm_tn(a, b, name):
    k, m = a.shape
    _, n = b.shape
    tm, tn = _pick_tile(m, WIDE_TILES), _pick_tile(n)

    def body(a_ref, b_ref, o_ref):
        o_ref[...] = _dot_tn(a_ref[...].astype(BF16), b_ref[...].astype(BF16)).astype(BF16)

    return pl.pallas_call(
        body, grid=(m // tm, n // tn), name=name,
        in_specs=[pl.BlockSpec((k, tm), lambda i, j: (0, i)), pl.BlockSpec((k, tn), lambda i, j: (0, j))],
        out_specs=pl.BlockSpec((tm, tn), lambda i, j: (i, j)),
        out_shape=jax.ShapeDtypeStruct((m, n), BF16),
        compiler_params=_cparams(2),
    )(a, b)


def _mm_tn_cols(a, b, name):
    k, m = a.shape
    nb, _, n = b.shape
    tm = _pick_tile(m, WIDE_TILES)

    def body(a_ref, b_ref, o_ref):
        o_ref[0] = _dot_tn(a_ref[...].astype(BF16), b_ref[0].astype(BF16)).astype(BF16)

    return pl.pallas_call(
        body, grid=(nb, m // tm), name=name,
        in_specs=[pl.BlockSpec((k, tm), lambda j, i: (0, i)), pl.BlockSpec((1, k, n), lambda j, i: (j, 0, 0))],
        out_specs=pl.BlockSpec((1, tm, n), lambda j, i: (j, i, 0)),
        out_shape=jax.ShapeDtypeStruct((nb, m, n), BF16),
        compiler_params=_cparams(2),
    )(a, b)


def _mm_tn_rows(a, b, keep, name):
    nb, k, m = a.shape
    _, n = b.shape
    tn = _pick_tile(n, WIDE_TILES)

    def body(a_ref, b_ref, o_ref):
        o_ref[0] = _dot_tn(a_ref[0], b_ref[...])[:keep].astype(BF16)

    return pl.pallas_call(
        body, grid=(nb, n // tn), name=name,
        in_specs=[pl.BlockSpec((1, k, m), lambda j, i: (j, 0, 0)), pl.BlockSpec((k, tn), lambda j, i: (0, i))],
        out_specs=pl.BlockSpec((1, keep, tn), lambda j, i: (j, 0, i)),
        out_shape=jax.ShapeDtypeStruct((nb, keep, n), BF16),
        compiler_params=_cparams(2),
    )(a, b)


PCG_W = 5 * D
QKV_W = 3 * D
PROJ_SUB = 512


def _inproj_fwd(h, g, w_in, name, comm=None):
    t = h.shape[0]

    def body(h_ref, g_ref, w_hbm, u_ref, pcg_ref, qkv_ref, w_v, sems):
        _load_resident(pl.program_id(0), [(w_hbm, w_v)], sems)
        u = _rms_fwd_tile(h_ref[...], g_ref[...]).astype(BF16)
        u_ref[...] = u
        for blk in range(N_DEV):
            for s in range(D // PROJ_SUB):
                lo, hi = s * PROJ_SUB, (s + 1) * PROJ_SUB
                p = _dot(u, w_v[blk, :, lo:hi])
                if blk < 3:
                    pcg_ref[:, blk * D + lo:blk * D + hi] = p
                elif blk < 6:
                    qkv_ref[:, (blk - 3) * D + lo:(blk - 3) * D + hi] = p.astype(BF16)
                else:
                    pcg_ref[:, (blk - 3) * D + lo:(blk - 3) * D + hi] = p

    return _call(
        body, (h, g, w_in), grid=(t // TM,), name=name, comm=comm,
        in_specs=[_row_spec(TM, D), _const_spec((1, D)), _ANY],
        out_specs=[_row_spec(TM, D), _row_spec(TM, PCG_W), _row_spec(TM, QKV_W)],
        out_shape=[jax.ShapeDtypeStruct((t, D), BF16), jax.ShapeDtypeStruct((t, PCG_W), F32),
                   jax.ShapeDtypeStruct((t, QKV_W), BF16)],
        scratch_shapes=[pltpu.VMEM((N_DEV, D, D), BF16), pltpu.SemaphoreType.DMA((1,))])


CONV_CW = 256


def _shift_down(v, k, rows):
    return jnp.where(rows >= k, pltpu.roll(v, k, 0), 0.0)


def _shift_up(v, k, rows, t):
    return jnp.where(rows < t - k, pltpu.roll(v, t - k, 0), 0.0)


def _col_spec(t, cw, off):
    return pl.BlockSpec((t, cw), lambda j: (0, j + off))


def _conv_fwd(pcg, conv_w, name):
    t = pcg.shape[0]
    nb = D // CONV_CW

    def body(cb_ref, cc_ref, cx_ref, w_ref, y_ref):
        rows = lax.broadcasted_iota(jnp.int32, (t, CONV_CW), 0)
        xc = cc_ref[...] * cx_ref[...]
        conv = (w_ref[0:1, :] * _shift_down(xc, 2, rows) + w_ref[1:2, :] * _shift_down(xc, 1, rows)
                + w_ref[2:3, :] * xc)
        y_ref[...] = (cb_ref[...] * conv).astype(BF16)

    return pl.pallas_call(
        body, grid=(nb,), name=name,
        in_specs=[_col_spec(t, CONV_CW, 0), _col_spec(t, CONV_CW, nb), _col_spec(t, CONV_CW, 2 * nb),
                  pl.BlockSpec((CONV_K, CONV_CW), lambda j: (0, j))],
        out_specs=_col_spec(t, CONV_CW, 0),
        out_shape=jax.ShapeDtypeStruct((t, D), BF16),
        compiler_params=_cparams(),
    )(pcg, pcg, pcg, conv_w)


def _conv_bwd(pcg, conv_w, dyc, name):
    t = pcg.shape[0]
    nb = D // CONV_CW

    def body(cb_ref, cc_ref, cx_ref, w_ref, dy_ref, dc_ref, dw_ref):
        rows = lax.broadcasted_iota(jnp.int32, (t, CONV_CW), 0)
        cc, cx = cc_ref[...], cx_ref[...]
        xc = cc * cx
        x1 = _shift_down(xc, 1, rows)
        x2 = _shift_down(xc, 2, rows)
        w0, w1, w2 = w_ref[0:1, :], w_ref[1:2, :], w_ref[2:3, :]
        conv = w0 * x2 + w1 * x1 + w2 * xc
        dy = dy_ref[...]
        dc_ref[0] = (dy * conv).astype(BF16)
        dconv = dy * cb_ref[...]
        dw_ref[...] = jnp.zeros((8, CONV_CW), F32)
        dw_ref[0:1, :] = jnp.sum(dconv * x2, axis=0, keepdims=True)
        dw_ref[1:2, :] = jnp.sum(dconv * x1, axis=0, keepdims=True)
        dw_ref[2:3, :] = jnp.sum(dconv * xc, axis=0, keepdims=True)
        dxc = w2 * dconv + w1 * _shift_up(dconv, 1, rows, t) + w0 * _shift_up(dconv, 2, rows, t)
        dc_ref[1] = (dxc * cx).astype(BF16)
        dc_ref[2] = (dxc * cc).astype(BF16)

    return pl.pallas_call(
        body, grid=(nb,), name=name,
        in_specs=[_col_spec(t, CONV_CW, 0), _col_spec(t, CONV_CW, nb), _col_spec(t, CONV_CW, 2 * nb),
                  pl.BlockSpec((CONV_K, CONV_CW), lambda j: (0, j)), _col_spec(t, CONV_CW, 0)],
        out_specs=[pl.BlockSpec((3, t, CONV_CW), lambda j: (0, 0, j)), pl.BlockSpec((8, CONV_CW), lambda j: (0, j))],
        out_shape=[jax.ShapeDtypeStruct((3, t, D), BF16), jax.ShapeDtypeStruct((8, D), F32)],
        compiler_params=_cparams(),
    )(pcg, pcg, pcg, conv_w, dyc)


def _tri2(cond):
    rr = lax.broadcasted_iota(jnp.int32, (2 * TK, TK), 0) & (TK - 1)
    cc = lax.broadcasted_iota(jnp.int32, (2 * TK, TK), 1)
    return cond(rr, cc).astype(BF16)


def _causal(shift, row0=0):
    rr = lax.broadcasted_iota(jnp.int32, (TQ - row0, TK), 0) + row0
    cc = lax.broadcasted_iota(jnp.int32, (TQ - row0, TK), 1)
    return cc + shift < rr


def _cumdot(v, tri2):
    hi = v.astype(BF16)
    lo = (v - hi.astype(F32)).astype(BF16)
    return _dot(jnp.concatenate([hi, lo], axis=1), tri2)


def _log_1m_beta(z):
    return -(jnp.maximum(z, 0.0) + jnp.log(1.0 + jnp.exp(-jnp.abs(z))))


def _sb_specs(t):
    g = SB_H // SB_HPS
    w = SB_HPS * SB_DH
    q_spec = pl.BlockSpec((TQ, w), lambda h, i: (i, h))
    k_spec = pl.BlockSpec((t, w), lambda h, i: (0, g + h))
    v_spec = pl.BlockSpec((t, w), lambda h, i: (0, 2 * g + h))
    ct_spec = pl.BlockSpec((SB_HPS, TQ, 1), lambda h, i: (h, i, 0))
    return g, w, q_spec, k_spec, v_spec, ct_spec


def _sb_fwd(qkv, name, comm=None):
    t = qkv.shape[0]
    scale = SB_DH ** -0.5
    g, w, q_spec, k_spec, v_spec, ct_spec = _sb_specs(t)

    def body(q_ref, k_ref, v_ref, y_ref, ct_ref):
        i = pl.program_id(1)
        later = _tri2(lambda j, s: j > s)
        n_diag = TQ // TK

        def block(j, carry, shift):
            off = pl.multiple_of(j * TK, TK)
            zs, ms = [], []
            for hd in range(SB_HPS):
                cols = slice(hd * SB_DH, (hd + 1) * SB_DH)
                z = _dot_nt(q_ref[:, cols], k_ref[pl.ds(off, TK), cols]) * scale
                m = _log_1m_beta(z)
                if shift is not None:
                    m = jnp.where(_causal(shift), m, 0.0)
                zs.append(z)
                ms.append(m)
            after = _cumdot(jnp.concatenate(ms, axis=0), later)
            out = []
            for hd in range(SB_HPS):
                acc, c_sum = carry[hd]
                cols = slice(hd * SB_DH, (hd + 1) * SB_DH)
                a = jnp.exp((ms[hd] + zs[hd]) + (c_sum + after[hd * TQ:(hd + 1) * TQ]))
                if shift is not None:
                    a = jnp.where(_causal(shift), a, 0.0)
                out.append((acc + _dot(a.astype(BF16), v_ref[pl.ds(off, TK), cols]),
                            c_sum + jnp.sum(ms[hd], axis=1, keepdims=True)))
            return tuple(out)

        carry = tuple((jnp.zeros((TQ, SB_DH), F32), jnp.zeros((TQ, 1), F32)) for _ in range(SB_HPS))
        for d in reversed(range(n_diag)):
            carry = block(i * n_diag + d, carry, d * TK)
        carry = lax.fori_loop(0, i * n_diag, lambda jj, c: block(i * n_diag - 1 - jj, c, None), carry)
        for hd in range(SB_HPS):
            y_ref[:, hd * SB_DH:(hd + 1) * SB_DH] = carry[hd][0].astype(BF16)
            ct_ref[hd] = carry[hd][1]

    return _call(
        body, (qkv, qkv, qkv), grid=(g, t // TQ), name=name, comm=comm,
        in_specs=[q_spec, k_spec, v_spec],
        out_specs=[q_spec, ct_spec],
        out_shape=[jax.ShapeDtypeStruct((t, D), BF16), jax.ShapeDtypeStruct((SB_H, t, 1), F32)],
        scratch_shapes=[])


def _sb_bwd(qkv, dy, ctot, name, comm=None):
    t = qkv.shape[0]
    scale = SB_DH ** -0.5
    g, w, q_spec, k_spec, v_spec, ct_spec = _sb_specs(t)
    acc_spec = pl.BlockSpec((2, t, w), lambda h, i: (0, 0, h))

    def body(q_ref, k_ref, v_ref, dy_ref, ct_ref, dq_ref, dkv_ref):
        i = pl.program_id(1)

        @pl.when(i == 0)
        def _():
            dkv_ref[...] = jnp.zeros_like(dkv_ref)

        upto = _tri2(lambda j, s: j <= s)
        n_diag = TQ // TK

        def block(j, carry, shift):
            off = pl.multiple_of(j * TK, TK)
            r0 = 0 if shift is None else shift
            nr = TQ - r0
            causal = None if shift is None else _causal(shift, r0)

            def grow(old, delta):
                return old + delta if r0 == 0 else jnp.concatenate([old[:r0], old[r0:] + delta], axis=0)

            zs, ms = [], []
            for hd in range(SB_HPS):
                cols = slice(hd * SB_DH, (hd + 1) * SB_DH)
                z = _dot_nt(q_ref[r0:, cols], k_ref[pl.ds(off, TK), cols]) * scale
                m = _log_1m_beta(z)
                if causal is not None:
                    m = jnp.where(causal, m, 0.0)
                zs.append(z)
                ms.append(m)
            m_upto = _cumdot(jnp.concatenate(ms, axis=0), upto)
            ls, a_s, es = [], [], []
            for hd in range(SB_HPS):
                cols = slice(hd * SB_DH, (hd + 1) * SB_DH)
                l = ms[hd] + zs[hd]
                a = jnp.exp(l + ((ct_ref[hd, r0:] - carry[hd][1][r0:]) - m_upto[hd * nr:(hd + 1) * nr]))
                if causal is not None:
                    a = jnp.where(causal, a, 0.0)
                ls.append(l)
                a_s.append(a)
                es.append(_dot_nt(dy_ref[r0:, cols], v_ref[pl.ds(off, TK), cols]) * a)
            e_upto = _dot(jnp.concatenate(es, axis=0).astype(BF16), upto[:TK])
            out = []
            for hd in range(SB_HPS):
                dq, p_sum, e_sum = carry[hd]
                cols = slice(hd * SB_DH, (hd + 1) * SB_DH)
                e = es[hd]
                dz = e - jnp.exp(ls[hd]) * (e_sum[r0:] + e_upto[hd * nr:(hd + 1) * nr])
                if causal is not None:
                    dz = jnp.where(causal, dz, 0.0)
                dzs = (dz * scale).astype(BF16)
                dkv_ref[0, pl.ds(off, TK), cols] += _dot_tn(dzs, q_ref[r0:, cols])
                dkv_ref[1, pl.ds(off, TK), cols] += _dot_tn(a_s[hd].astype(BF16), dy_ref[r0:, cols])
                out.append((grow(dq, _dot(dzs, k_ref[pl.ds(off, TK), cols])),
                            grow(p_sum, jnp.sum(ms[hd], axis=1, keepdims=True)),
                            grow(e_sum, jnp.sum(e, axis=1, keepdims=True))))
            return tuple(out)

        zero = jnp.zeros((TQ, 1), F32)
        init = tuple((jnp.zeros((TQ, SB_DH), F32), zero, zero) for _ in range(SB_HPS))
        carry = lax.fori_loop(0, i * n_diag, lambda j, c: block(j, c, None), init)
        for d in range(n_diag):
            carry = block(i * n_diag + d, carry, d * TK)
        for hd in range(SB_HPS):
            dq_ref[:, hd * SB_DH:(hd + 1) * SB_DH] = carry[hd][0].astype(BF16)

    return _call(
        body, (qkv, qkv, qkv, dy, ctot), grid=(g, t // TQ), name=name, comm=comm,
        in_specs=[q_spec, k_spec, v_spec, q_spec, ct_spec],
        out_specs=[q_spec, acc_spec],
        out_shape=[jax.ShapeDtypeStruct((t, D), BF16), jax.ShapeDtypeStruct((2, t, D), F32)],
        scratch_shapes=[])


def _gate_specs():
    return [pl.BlockSpec((TM, D), lambda i: (i, 3)), pl.BlockSpec((TM, D), lambda i: (i, 4))]


def _mix_pairs(mix_hbm, dsts):
    pairs = []
    for index, dst in enumerate(dsts):
        pairs += _square_pairs(mix_hbm, index, dst)
    return pairs


def _mix_out_fwd(yc, ysb, pcg, b_gate, h, w_mix, name):
    t = h.shape[0]

    def body(yc_ref, ysb_ref, gc_ref, gs_ref, b_ref, h_ref, mix_hbm,
             a_ref, b_out_ref, mg_ref, h2_ref, wc_v, wa_v, wo_v, sems):
        _load_resident(pl.program_id(0), _mix_pairs(mix_hbm, (wc_v, wa_v, wo_v)), sems)
        a = _dot(yc_ref[...], wc_v[...])
        b = _dot(ysb_ref[...], wa_v[...])
        merged = (_sigmoid(gc_ref[...] + b_ref[:, :D]) * a + _sigmoid(gs_ref[...] + b_ref[:, D:]) * b).astype(BF16)
        a_ref[...] = a
        b_out_ref[...] = b
        mg_ref[...] = merged
        h2_ref[...] = h_ref[...] + _dot(merged, wo_v[...])

    return pl.pallas_call(
        body, grid=(t // TM,), name=name,
        in_specs=[_row_spec(TM, D), _row_spec(TM, D)] + _gate_specs()
                 + [_const_spec((1, 2 * D)), _row_spec(TM, D), _ANY],
        out_specs=[_row_spec(TM, D)] * 4,
        out_shape=[jax.ShapeDtypeStruct((t, D), F32), jax.ShapeDtypeStruct((t, D), F32),
                   jax.ShapeDtypeStruct((t, D), BF16), jax.ShapeDtypeStruct((t, D), F32)],
        scratch_shapes=[pltpu.VMEM((D, D), BF16)] * 3 + [pltpu.SemaphoreType.DMA((3 * N_DEV,))],
        compiler_params=_cparams(),
    )(yc, ysb, pcg, pcg, b_gate, h, w_mix)


def _mix_out_bwd(dh2, a, b, pcg, b_gate, w_mix, name, comm=None):
    t = dh2.shape[0]

    def body(dh_ref, a_ref, b_ref, gc_ref, gs_ref, bias_ref, mix_hbm,
             dhb_ref, da_ref, db_ref, dgp_ref, dyc_ref, dysb_ref, dbias_ref, wc_v, wa_v, wo_v, sems):
        step = pl.program_id(0)
        _load_resident(step, _mix_pairs(mix_hbm, (wc_v, wa_v, wo_v)), sems)
        dhb = dh_ref[...].astype(BF16)
        dhb_ref[...] = dhb
        dm = _dot_nt(dhb, wo_v[...])
        gc = _sigmoid(gc_ref[...] + bias_ref[:, :D])
        gs = _sigmoid(gs_ref[...] + bias_ref[:, D:])
        da = (dm * gc).astype(BF16)
        db = (dm * gs).astype(BF16)
        da_ref[...] = da
        db_ref[...] = db
        dgc = dm * a_ref[...] * (gc * (1.0 - gc))
        dgs = dm * b_ref[...] * (gs * (1.0 - gs))
        dgp_ref[0] = dgc.astype(BF16)
        dgp_ref[1] = dgs.astype(BF16)
        _accumulate(dbias_ref.at[:, :D], step, jnp.sum(dgc, axis=0, keepdims=True))
        _accumulate(dbias_ref.at[:, D:], step, jnp.sum(dgs, axis=0, keepdims=True))
        dyc_ref[...] = _dot_nt(da, wc_v[...])
        dysb_ref[...] = _dot_nt(db, wa_v[...]).astype(BF16)

    return _call(
        body, (dh2, a, b, pcg, pcg, b_gate, w_mix), grid=(t // TM,), name=name, comm=comm,
        in_specs=[_row_spec(TM, D)] * 3 + _gate_specs() + [_const_spec((1, 2 * D)), _ANY],
        out_specs=[_row_spec(TM, D)] * 3 + [_blk_row_spec(2, TM, D), _row_spec(TM, D), _row_spec(TM, D),
                                            _const_spec((1, 2 * D))],
        out_shape=[jax.ShapeDtypeStruct((t, D), BF16)] * 3
                  + [jax.ShapeDtypeStruct((2, t, D), BF16), jax.ShapeDtypeStruct((t, D), F32),
                     jax.ShapeDtypeStruct((t, D), BF16), jax.ShapeDtypeStruct((1, 2 * D), F32)],
        scratch_shapes=[pltpu.VMEM((D, D), BF16)] * 3 + [pltpu.SemaphoreType.DMA((3 * N_DEV,))])


def _inproj_bwd(dconv, dq, dkv, dgp, w_in, h, g, dh_res, name, comm=None):
    t = h.shape[0]

    def body(dc_ref, dq_ref, dkv_ref, dgp_ref, w_hbm, h_ref, g_ref, dres_ref, dh_ref, dg_ref, w_v, sems):
        step = pl.program_id(0)
        _load_resident(step, [(w_hbm, w_v)], sems)
        du = _dot_nt(dq_ref[...], w_v[3])
        for k in range(3):
            du = du + _dot_nt(dc_ref[k], w_v[k])
        for k in range(2):
            du = du + _dot_nt(dkv_ref[k].astype(BF16), w_v[4 + k]) + _dot_nt(dgp_ref[k], w_v[6 + k])
        dx, dg = _rms_bwd_tile(h_ref[...], g_ref[...], du)
        dh_ref[...] = dres_ref[...] + dx
        _accumulate(dg_ref, step, dg)

    return _call(
        body, (dconv, dq, dkv, dgp, w_in, h, g, dh_res), grid=(t // TM,), name=name, comm=comm,
        in_specs=[_blk_row_spec(3, TM, D), _row_spec(TM, D), _blk_row_spec(2, TM, D), _blk_row_spec(2, TM, D), _ANY,
                  _row_spec(TM, D), _const_spec((1, D)), _row_spec(TM, D)],
        out_specs=[_row_spec(TM, D), _const_spec((1, D))],
        out_shape=[jax.ShapeDtypeStruct((t, D), F32), jax.ShapeDtypeStruct((1, D), F32)],
        scratch_shapes=[pltpu.VMEM((N_DEV, D, D), BF16), pltpu.SemaphoreType.DMA((1,))])


def _memkv_fwd(mem, g, w_ckv, name):
    m = mem.shape[0]

    def body(mem_ref, g_ref, w_ref, mn_ref, kv_ref):
        mn = _rms_fwd_tile(mem_ref[...], g_ref[...]).astype(BF16)
        mn_ref[...] = mn
        for j in range(N_DEV):
            kv_ref[j] = _dot(mn, w_ref[j]).astype(BF16)

    return pl.pallas_call(
        body, grid=(1,), name=name,
        in_specs=[_const_spec((m, D)), _const_spec((1, D)), _const_spec((N_DEV, D, X_DH))],
        out_specs=[_const_spec((m, D)), _const_spec((N_DEV, m, X_DH))],
        out_shape=[jax.ShapeDtypeStruct((m, D), BF16), jax.ShapeDtypeStruct((N_DEV, m, X_DH), BF16)],
        compiler_params=_cparams(),
    )(mem, g, w_ckv)


def _memkv_bwd(dkv, mem, g, w_ckv, name):
    m = mem.shape[0]

    def body(dkv_ref, mem_ref, g_ref, w_ref, dg_ref):
        dmn = jnp.zeros((m, D), F32)
        for j in range(N_DEV):
            dmn = dmn + _dot_nt(dkv_ref[j].astype(BF16), w_ref[j])
        _, dg = _rms_bwd_tile(mem_ref[...], g_ref[...], dmn)
        dg_ref[...] = dg

    return pl.pallas_call(
        body, grid=(1,), name=name,
        in_specs=[_const_spec((N_DEV, m, X_DH)), _const_spec((m, D)), _const_spec((1, D)),
                  _const_spec((N_DEV, D, X_DH))],
        out_specs=_const_spec((1, D)),
        out_shape=jax.ShapeDtypeStruct((1, D), F32),
        compiler_params=_cparams(),
    )(dkv, mem, g, w_ckv)


def _softmax_rows(s):
    e = jnp.exp(s - jnp.max(s, axis=-1, keepdims=True))
    return e / jnp.sum(e, axis=-1, keepdims=True)


def _cross_pairs(cross_hbm, wq_v, wo_v):
    return _square_pairs(cross_hbm, 0, wq_v) + _square_pairs(cross_hbm, 1, wo_v)


def _cross_fwd(h, g, kv, w_cross, name):
    t = h.shape[0]
    m = kv.shape[1]
    scale = X_DH ** -0.5

    def body(h_ref, g_ref, kv_ref, cross_hbm, hn_ref, qx_ref, o_ref, h3_ref, wq_v, wo_v, sems):
        _load_resident(pl.program_id(0), _cross_pairs(cross_hbm, wq_v, wo_v), sems)
        ht = h_ref[...]
        hn = _rms_fwd_tile(ht, g_ref[...]).astype(BF16)
        hn_ref[...] = hn
        qx = _dot(hn, wq_v[...]).astype(BF16)
        qx_ref[...] = qx
        for hd in range(X_H):
            lo, hi = hd * X_DH, (hd + 1) * X_DH
            p = _softmax_rows(_dot_nt(qx[:, lo:hi], kv_ref[hd]) * scale)
            o_ref[:, lo:hi] = _dot(p.astype(BF16), kv_ref[X_H + hd]).astype(BF16)
        h3_ref[...] = ht + _dot(o_ref[...], wo_v[...])

    return pl.pallas_call(
        body, grid=(t // TM,), name=name,
        in_specs=[_row_spec(TM, D), _const_spec((1, D)), _const_spec((N_DEV, m, X_DH)), _ANY],
        out_specs=[_row_spec(TM, D)] * 4,
        out_shape=[jax.ShapeDtypeStruct((t, D), BF16)] * 3 + [jax.ShapeDtypeStruct((t, D), F32)],
        scratch_shapes=[pltpu.VMEM((D, D), BF16)] * 2 + [pltpu.SemaphoreType.DMA((2 * N_DEV,))],
        compiler_params=_cparams(),
    )(h, g, kv, w_cross)


def _cross_bwd(dh3, h, g, qx, kv, w_cross, name, comm=None):
    t = h.shape[0]
    m = kv.shape[1]
    scale = X_DH ** -0.5

    def body(dh_ref, h_ref, g_ref, qx_ref, kv_ref, cross_hbm,
             dhb_ref, dqx_ref, dkv_ref, dh2_ref, dg_ref, wq_v, wo_v, sems):
        step = pl.program_id(0)
        _load_resident(step, _cross_pairs(cross_hbm, wq_v, wo_v), sems)

        @pl.when(step == 0)
        def _():
            dkv_ref[...] = jnp.zeros_like(dkv_ref)

        dht = dh_ref[...]
        dhb = dht.astype(BF16)
        dhb_ref[...] = dhb
        do = _dot_nt(dhb, wo_v[...]).astype(BF16)
        for hd in range(X_H):
            lo, hi = hd * X_DH, (hd + 1) * X_DH
            qh = qx_ref[:, lo:hi]
            kh = kv_ref[hd]
            p = _softmax_rows(_dot_nt(qh, kh) * scale)
            doh = do[:, lo:hi]
            dp = _dot_nt(doh, kv_ref[X_H + hd])
            ds = (p * (dp - jnp.sum(dp * p, axis=-1, keepdims=True)) * scale).astype(BF16)
            dqx_ref[:, lo:hi] = _dot(ds, kh).astype(BF16)
            dkv_ref[hd] += _dot_tn(ds, qh)
            dkv_ref[X_H + hd] += _dot_tn(p.astype(BF16), doh)
        dhn = _dot_nt(dqx_ref[...], wq_v[...])
        dx, dg = _rms_bwd_tile(h_ref[...], g_ref[...], dhn)
        dh2_ref[...] = dht + dx
        _accumulate(dg_ref, step, dg)

    return _call(
        body, (dh3, h, g, qx, kv, w_cross), grid=(t // TM,), name=name, comm=comm,
        in_specs=[_row_spec(TM, D), _row_spec(TM, D), _const_spec((1, D)), _row_spec(TM, D),
                  _const_spec((N_DEV, m, X_DH)), _ANY],
        out_specs=[_row_spec(TM, D), _row_spec(TM, D), _const_spec((N_DEV, m, X_DH)), _row_spec(TM, D),
                   _const_spec((1, D))],
        out_shape=[jax.ShapeDtypeStruct((t, D), BF16), jax.ShapeDtypeStruct((t, D), BF16),
                   jax.ShapeDtypeStruct((N_DEV, m, X_DH), F32), jax.ShapeDtypeStruct((t, D), F32),
                   jax.ShapeDtypeStruct((1, D), F32)],
        scratch_shapes=[pltpu.VMEM((D, D), BF16)] * 2 + [pltpu.SemaphoreType.DMA((2 * N_DEV,))])


def _loss_bwd(h, g, target, name):
    t = h.shape[0]

    def body(h_ref, g_ref, t_ref, loss_ref, dh_ref, dg_ref):
        step = pl.program_id(0)
        ht = h_ref[...]
        gain = g_ref[...]
        diff = _rms_fwd_tile(ht, gain) - t_ref[...]
        part = 0.5 * jnp.sum(jnp.sum(diff * diff, axis=-1, keepdims=True) / D, axis=0, keepdims=True)
        dx, dg = _rms_bwd_tile(ht, gain, diff / D)
        dh_ref[...] = dx
        _accumulate(loss_ref, step, jnp.broadcast_to(part, (8, 128)))
        _accumulate(dg_ref, step, dg)

    return pl.pallas_call(
        body, grid=(t // TM,), name=name,
        in_specs=[_row_spec(TM, D), _const_spec((1, D)), _row_spec(TM, D)],
        out_specs=[_const_spec((8, 128)), _row_spec(TM, D), _const_spec((1, D))],
        out_shape=[jax.ShapeDtypeStruct((8, 128), F32), jax.ShapeDtypeStruct((t, D), F32),
                   jax.ShapeDtypeStruct((1, D), F32)],
        compiler_params=_cparams(),
    )(h, g, target)


def _adamw(w, parts, m, v, name, row_block=0, token=None):
    r, c = w.shape
    n = parts.shape[0]
    tr = _pick_tile(r, (256, 352, 128))
    off = row_block * (r // tr)

    def body(*refs):
        if token is None:
            _adamw_update(None, *refs)
        else:
            _adamw_update(refs[4], *refs[:4], *refs[5:])

    spec = _row_spec(tr, c)
    in_specs = [spec, pl.BlockSpec((n, tr, c), lambda i: (0, i + off, 0)), spec, spec]
    operands = (w, parts, m, v)
    if token is not None:
        in_specs.append(_const_spec(token.shape))
        operands += (token,)
    return pl.pallas_call(
        body, grid=(r // tr,), name=name, in_specs=in_specs, out_specs=[spec] * 4,
        out_shape=[jax.ShapeDtypeStruct((r, c), F32)] * 4,
        compiler_params=_cparams(),
    )(*operands)


def _adamw_update(tok_ref, w_ref, p_ref, m_ref, v_ref, g_ref, d_ref, nm_ref, nv_ref):
    gt = p_ref[0].astype(F32)
    for k in range(1, p_ref.shape[0]):
        gt = gt + p_ref[k].astype(F32)
    if tok_ref is not None:
        gt = gt + tok_ref[0:1, 0:1]
    _adamw_apply(gt, w_ref, m_ref, v_ref, g_ref, d_ref, nm_ref, nv_ref)


def _adamw_own(w, land, own, chip, m, v, name):
    r, c = w.shape
    tr = _pick_tile(r, (256, 352, 128))

    def body(chip_ref, w_ref, land_ref, own_ref, m_ref, v_ref, g_ref, d_ref, nm_ref, nv_ref):
        mine = own_ref[0].astype(F32)
        gt = jnp.where(chip_ref[0] == 0, mine, land_ref[0].astype(F32))
        for k in range(1, N_CHIP):
            gt = gt + jnp.where(chip_ref[0] == k, mine, land_ref[k].astype(F32))
        _adamw_apply(gt, w_ref, m_ref, v_ref, g_ref, d_ref, nm_ref, nv_ref)

    spec = pl.BlockSpec((tr, c), lambda i, chip_ref: (i, 0))
    return pl.pallas_call(
        body, name=name,
        grid_spec=pltpu.PrefetchScalarGridSpec(
            num_scalar_prefetch=1, grid=(r // tr,),
            in_specs=[spec, pl.BlockSpec((N_CHIP, tr, c), lambda i, chip_ref: (0, i, 0)),
                      pl.BlockSpec((1, tr, c), lambda i, chip_ref: (chip_ref[0], i, 0)), spec, spec],
            out_specs=[spec] * 4),
        out_shape=[jax.ShapeDtypeStruct((r, c), F32)] * 4,
        compiler_params=_cparams(),
    )(chip, w, land, own, m, v)


def _adamw_apply(gt, w_ref, m_ref, v_ref, g_ref, d_ref, nm_ref, nv_ref):
    g_ref[...] = gt
    nm = ADAM_B1 * m_ref[...] + (1.0 - ADAM_B1) * gt
    nv = ADAM_B2 * v_ref[...] + (1.0 - ADAM_B2) * jnp.square(gt)
    m_hat = nm / (1.0 - ADAM_B1 ** ADAM_STEP)
    v_hat = nv / (1.0 - ADAM_B2 ** ADAM_STEP)
    d_ref[...] = -ADAM_LR * (m_hat / (jnp.sqrt(v_hat) + ADAM_EPS) + ADAM_WD * w_ref[...])
    nm_ref[...] = nm
    nv_ref[...] = nv


def _mesh_pos():
    return lax.axis_index("x"), lax.axis_index("y"), lax.axis_index("c")


def _no_round(in_refs, out_refs, sems):
    pass


def _run_exchange(comm, name):
    c_in, c_out = len(comm.inputs), len(comm.out_shapes)

    def body(*refs):
        cins, couts, sems = refs[:c_in], refs[c_in:c_in + c_out], refs[c_in + c_out:]
        comm.start(cins, couts, sems)
        comm.middle(cins, couts, sems)
        comm.finish(cins, couts, sems)

    return list(pl.pallas_call(
        body, name=name, out_shape=list(comm.out_shapes),
        in_specs=[_ANY] * c_in, out_specs=[_ANY] * c_out, scratch_shapes=list(comm.sem_shapes),
    )(*comm.inputs))


def _gather_exchange(shards):
    n_arr = len(shards)

    def plan(x_refs, out_refs, sems):
        send_sems, recv_sems, local_sems = sems
        x, y, c = _mesh_pos()
        me, sibling = (x, y, c), (x, y, 1 - c)
        xn, yn, diag = (1 - x, y), (x, 1 - y), (1 - x, 1 - y)

        def slot(a, px, py, pc, half=None):
            ref = out_refs[a].at[4 * px + 2 * py + pc]
            if half is None:
                return ref
            rows = shards[a].shape[0] // 2
            return ref.at[half * rows:(half + 1) * rows]

        def copy(a, k, block, to, half=None, src=None):
            dst = slot(a, *block, half)
            return pltpu.make_async_remote_copy(
                src_ref=dst if src is None else src, dst_ref=dst,
                send_sem=send_sems.at[a, k], recv_sem=recv_sems.at[a, k],
                device_id=to, device_id_type=pl.DeviceIdType.MESH)

        return types.SimpleNamespace(
            me=me, sibling=sibling, xn=xn, yn=yn, diag=diag, c=c, copy=copy,
            mine=[pltpu.make_async_copy(x_refs[a], slot(a, *me), local_sems.at[a]) for a in range(n_arr)],
            first=[cp for a in range(n_arr) for cp in (
                copy(a, 0, me, sibling, src=x_refs[a]), copy(a, 1, me, (*xn, c), src=x_refs[a]),
                copy(a, 2, me, (*yn, c), src=x_refs[a]))],
            second=lambda a: (copy(a, 3, (*xn, c), (*yn, c), half=0), copy(a, 5, (*xn, c), sibling),
                              copy(a, 4, (*yn, c), (*xn, c), half=1), copy(a, 6, (*yn, c), sibling)),
            third=lambda a: (copy(a, 7, (*diag, c), sibling, half=0), copy(a, 8, (*diag, c), sibling, half=1)))

    def start(x_refs, out_refs, sems):
        p = plan(x_refs, out_refs, sems)
        for cp in p.mine + p.first:
            cp.start()

    def middle(x_refs, out_refs, sems):
        p = plan(x_refs, out_refs, sems)
        for a in range(n_arr):
            to_yn, x_to_sib, to_xn, y_to_sib = p.second(a)
            p.copy(a, 1, (*p.xn, p.c), p.me).wait_recv()
            to_yn.start()
            x_to_sib.start()
            p.copy(a, 2, (*p.yn, p.c), p.me).wait_recv()
            to_xn.start()
            y_to_sib.start()

    def finish(x_refs, out_refs, sems):
        p = plan(x_refs, out_refs, sems)
        for a in range(n_arr):
            half0_to_sib, half1_to_sib = p.third(a)
            p.copy(a, 3, (*p.diag, p.c), p.me, half=0).wait_recv()
            half0_to_sib.start()
            p.copy(a, 4, (*p.diag, p.c), p.me, half=1).wait_recv()
            half1_to_sib.start()
        other = 1 - p.c
        for a in range(n_arr):
            p.copy(a, 0, p.sibling, p.me).wait_recv()
            p.copy(a, 5, (*p.xn, other), p.me).wait_recv()
            p.copy(a, 6, (*p.yn, other), p.me).wait_recv()
            p.copy(a, 7, (*p.diag, other), p.me, half=0).wait_recv()
            p.copy(a, 8, (*p.diag, other), p.me, half=1).wait_recv()
        for cp in p.first:
            cp.wait_send()
        for a in range(n_arr):
            for cp in p.second(a) + p.third(a):
                cp.wait_send()
        for cp in p.mine:
            cp.wait()

    return types.SimpleNamespace(
        inputs=list(shards), start=start, middle=middle, finish=finish,
        out_shapes=[jax.ShapeDtypeStruct((N_DEV,) + s.shape, s.dtype) for s in shards],
        sem_shapes=[pltpu.SemaphoreType.DMA((n_arr, 9)), pltpu.SemaphoreType.DMA((n_arr, 9)),
                    pltpu.SemaphoreType.DMA((n_arr,))])


def _pair_exchange(grads):
    n_arr = len(grads)

    def plan(g_refs, land_refs, sems):
        send_sems, recv_sems = sems
        x, y, c = _mesh_pos()
        return [pltpu.make_async_remote_copy(
            src_ref=g_refs[a].at[2 * k + 1 - c], dst_ref=land_refs[a].at[k],
            send_sem=send_sems.at[a, k], recv_sem=recv_sems.at[a, k],
            device_id=(x, y, 1 - c), device_id_type=pl.DeviceIdType.MESH)
            for a in range(n_arr) for k in range(N_CHIP)]

    def start(g_refs, land_refs, sems):
        for cp in plan(g_refs, land_refs, sems):
            cp.start()

    def finish(g_refs, land_refs, sems):
        for cp in plan(g_refs, land_refs, sems):
            cp.wait()

    return types.SimpleNamespace(
        inputs=list(grads), start=start, middle=_no_round, finish=finish,
        out_shapes=[jax.ShapeDtypeStruct((N_CHIP,) + g.shape[1:], g.dtype) for g in grads],
        sem_shapes=[pltpu.SemaphoreType.DMA((n_arr, N_CHIP)), pltpu.SemaphoreType.DMA((n_arr, N_CHIP))])


def _chip_exchange(parts):
    n_arr = len(parts)

    def plan(p_refs, land_refs, sems):
        send_sems, recv_sems, local_sems = sems
        x, y, c = _mesh_pos()
        my_chip = 2 * x + y
        chips = [(1 - x, y), (x, 1 - y), (1 - x, 1 - y)]
        local = [pltpu.make_async_copy(p_refs[a].at[my_chip], land_refs[a].at[my_chip], local_sems.at[a])
                 for a in range(n_arr)]

        def copy(a, k, src_slot, dst_slot, px, py):
            return pltpu.make_async_remote_copy(
                src_ref=p_refs[a].at[src_slot], dst_ref=land_refs[a].at[dst_slot],
                send_sem=send_sems.at[a, k], recv_sem=recv_sems.at[a, k],
                device_id=(px, py, c), device_id_type=pl.DeviceIdType.MESH)

        sends = [copy(a, k, 2 * px + py, my_chip, px, py) for a in range(n_arr) for k, (px, py) in enumerate(chips)]
        arrivals = [copy(a, k, my_chip, 2 * px + py, px, py) for a in range(n_arr)
                    for k, (px, py) in enumerate(chips)]
        return local, sends, arrivals

    def start(p_refs, land_refs, sems):
        local, sends, _ = plan(p_refs, land_refs, sems)
        for cp in local + sends:
            cp.start()

    def finish(p_refs, land_refs, sems):
        local, sends, arrivals = plan(p_refs, land_refs, sems)
        for cp in arrivals:
            cp.wait_recv()
        for cp in sends:
            cp.wait_send()
        for cp in local:
            cp.wait()

    return types.SimpleNamespace(
        inputs=list(parts), start=start, middle=_no_round, finish=finish,
        out_shapes=[jax.ShapeDtypeStruct(p.shape, p.dtype) for p in parts],
        sem_shapes=[pltpu.SemaphoreType.DMA((n_arr, 3)), pltpu.SemaphoreType.DMA((n_arr, 3)),
                    pltpu.SemaphoreType.DMA((n_arr,))])


_HBM = pl.BlockSpec(memory_space=pltpu.HBM)
_SEM = pl.BlockSpec(memory_space=pltpu.SEMAPHORE)
_DATAFLOW = pltpu.SideEffectType.DATAFLOW_SIDE_EFFECTING


def _chip_copies(p_refs, land_refs, send_sems, recv_sems):
    x, y, c = _mesh_pos()
    my_chip = 2 * x + y
    chips = [(1 - x, y), (x, 1 - y), (1 - x, 1 - y)]
    return [pltpu.make_async_remote_copy(
        src_ref=p_refs[a].at[2 * px + py], dst_ref=land_refs[a].at[my_chip],
        send_sem=send_sems[3 * a + k], recv_sem=recv_sems[3 * a + k],
        device_id=(px, py, c), device_id_type=pl.DeviceIdType.MESH)
        for a in range(len(p_refs)) for k, (px, py) in enumerate(chips)]


def _chip_exchange_begin(parts, lands, name):
    n_arr = len(parts)
    n_buf, n_copy = 2 * n_arr, 3 * n_arr

    def body(*refs):
        p_refs, land_refs = refs[:n_arr], refs[n_arr:n_buf]
        send_sems, recv_sems, token = refs[n_buf:n_buf + n_copy], refs[n_buf + n_copy:n_buf + 2 * n_copy], refs[-1]
        for cp in _chip_copies(p_refs, land_refs, send_sems, recv_sems):
            cp.start()
        token[...] = jnp.zeros_like(token)

    bufs = list(parts) + list(lands)
    outs = pl.pallas_call(
        body, name=name,
        out_shape=(*[pltpu.SemaphoreType.DMA(())] * (2 * n_copy), *[pltpu.HBM(b.shape, b.dtype) for b in bufs],
                   jax.ShapeDtypeStruct((8, 128), F32)),
        in_specs=[_HBM] * n_buf,
        out_specs=(*[_SEM] * (2 * n_copy), *[_HBM] * n_buf, pl.BlockSpec(memory_space=pltpu.VMEM)),
        input_output_aliases={i: 2 * n_copy + i for i in range(n_buf)},
        compiler_params=pltpu.CompilerParams(has_side_effects=_DATAFLOW),
    )(*[pltpu.with_memory_space_constraint(b, pltpu.HBM) for b in bufs])
    sems = list(outs[:2 * n_copy])
    thru = list(outs[2 * n_copy:2 * n_copy + n_buf])
    return sems[:n_copy], sems[n_copy:], thru[:n_arr], thru[n_arr:], outs[-1]


def _chip_exchange_end(send_sems, recv_sems, parts, lands, after, name):
    n_arr = len(parts)
    n_buf, n_copy = 2 * n_arr, 3 * n_arr

    def body(*refs):
        p_refs, land_refs = refs[:n_arr], refs[n_arr:n_buf]
        sems = refs[n_buf:n_buf + 2 * n_copy]
        for cp in _chip_copies(p_refs, land_refs, sems[:n_copy], sems[n_copy:]):
            cp.wait_send()
            cp.wait_recv()

    bufs = list(parts) + list(lands)
    outs = pl.pallas_call(
        body, name=name, out_shape=tuple(pltpu.HBM(b.shape, b.dtype) for b in bufs),
        in_specs=[_HBM] * n_buf + [_SEM] * (2 * n_copy) + [_ANY], out_specs=tuple([_HBM] * n_buf),
        input_output_aliases={i: i for i in range(n_buf)},
        compiler_params=pltpu.CompilerParams(has_side_effects=_DATAFLOW),
    )(*bufs, *send_sems, *recv_sems, after)
    return list(outs[:n_arr]), list(outs[n_arr:])


def _row_tile(r, cap=640):
    best = None
    for cand in range(16, min(r, cap) + 1, 16):
        if r % cand == 0:
            best = cand
    return best if best is not None else r


def _pair_sum(g, landed, core, name):
    _, r, c_dim = g.shape
    tr = _row_tile(r)

    def body(core_ref, mine_ref, theirs_ref, o_ref):
        o_ref[0] = (mine_ref[0].astype(F32) + theirs_ref[0].astype(F32)).astype(o_ref.dtype)

    return pl.pallas_call(
        body, name=name,
        grid_spec=pltpu.PrefetchScalarGridSpec(
            num_scalar_prefetch=1, grid=(N_CHIP, r // tr),
            in_specs=[pl.BlockSpec((1, tr, c_dim), lambda k, i, core_ref: (2 * k + core_ref[0], i, 0)),
                      pl.BlockSpec((1, tr, c_dim), lambda k, i, core_ref: (k, i, 0))],
            out_specs=pl.BlockSpec((1, tr, c_dim), lambda k, i, core_ref: (k, i, 0))),
        out_shape=jax.ShapeDtypeStruct((N_CHIP, r, c_dim), g.dtype),
        compiler_params=_cparams(2),
    )(core, g, landed)


def _sum_slots(parts, name):
    n, r, c_dim = parts.shape
    tr = _row_tile(r)

    def body(p_ref, o_ref):
        acc = p_ref[0].astype(F32)
        for k in range(1, n):
            acc = acc + p_ref[k].astype(F32)
        o_ref[...] = acc

    return pl.pallas_call(
        body, grid=(r // tr,), name=name,
        in_specs=[pl.BlockSpec((n, tr, c_dim), lambda i: (0, i, 0))],
        out_specs=_row_spec(tr, c_dim),
        out_shape=jax.ShapeDtypeStruct((r, c_dim), F32),
        compiler_params=_cparams(),
    )(parts)


GAINS = ("g_ffn1", "g_mix", "g_cross", "g_mem", "g_ffn2", "g_final")
SMALL = GAINS + ("b_gate", "conv_w")
SMALL_R = 16
WEIGHT_ORDER = ("g_ffn1", "w_ffn1_gu", "w_ffn1_down", "g_mix", "w_in", "b_gate", "conv_w", "w_conv_out",
                "w_attn_out", "w_o", "g_cross", "g_mem", "w_cq", "w_ckv", "w_co", "g_ffn2", "w_ffn2_gu",
                "w_ffn2_down", "g_final")
GU_NAMES = ("w_ffn1_gu", "w_ffn2_gu")


def _pack_small(vals, conv_rows):
    rows = [vals[n].reshape(1, D) for n in GAINS] + [vals["b_gate"].reshape(2, D), conv_rows.reshape(CONV_K, D)]
    used = len(GAINS) + 2 + CONV_K
    return jnp.concatenate(rows + [jnp.zeros((SMALL_R - used, D), F32)], axis=0)


def _unpack_small(buf):
    out = {n: buf[k] for k, n in enumerate(GAINS)}
    out["b_gate"] = buf[6:8].reshape(2 * D)
    out["conv_w"] = buf[8:8 + CONV_K]
    return out


def _exchange_shards(wts):
    out = {n: jnp.pad(wts[n].T.astype(BF16), ((0, FF_PAD - FF_BLK), (0, 0))) for n in GU_NAMES}
    for n in ("w_ckv", "w_in", "w_ffn1_down", "w_ffn2_down"):
        out[n] = wts[n].astype(BF16)
    out["mix"] = jnp.concatenate([wts[n].astype(BF16) for n in MIX_MATS], axis=0)
    out["cross"] = jnp.concatenate([wts[n].astype(BF16) for n in CROSS_MATS], axis=0)
    return out


def _by_device(dw):
    return dw.reshape(N_DEV, SQ_ROWS, D)


def _reduce_group(grads, landed, core, names):
    return [_pair_sum(g, l, core, "grads_pair_sum_" + n) for g, l, n in zip(grads, landed, names)]


def _step(x, mem, target, sh, conv_pad, gains, b_gate, core):
    wg1, wd1, conv_all = _run_exchange(_gather_exchange([sh["w_ffn1_gu"], sh["w_ffn1_down"], conv_pad]), "gather_ffn1")
    conv_w = conv_all[:, :CONV_K, :].transpose(1, 0, 2).reshape(CONV_K, D)
    (n1, gate1, up1, act1, h1), (w_in,) = _ffn_fwd(
        x, gains["g_ffn1"], wg1, wd1, "ffn1_fwd", comm=_gather_exchange([sh["w_in"]]))
    (u, pcg, qkv), (w_mix,) = _inproj_fwd(h1, gains["g_mix"], w_in, "inproj_fwd", comm=_gather_exchange([sh["mix"]]))
    yc = _conv_fwd(pcg, conv_w, "conv_fwd")
    (ysb, ctot), (w_cross, w_ckv, wg2, wd2) = _sb_fwd(
        qkv, "sb_fwd", comm=_gather_exchange([sh["cross"], sh["w_ckv"], sh["w_ffn2_gu"], sh["w_ffn2_down"]]))
    a_mix, b_mix, merged, h2 = _mix_out_fwd(yc, ysb, pcg, b_gate, h1, w_mix, "mix_out_fwd")
    mn, kv = _memkv_fwd(mem, gains["g_mem"], w_ckv, "memkv_fwd")
    hn, qx, o_x, h3 = _cross_fwd(h2, gains["g_cross"], kv, w_cross, "cross_fwd")
    (n4, gate2, up2, act2, h4), _ = _ffn_fwd(h3, gains["g_ffn2"], wg2, wd2, "ffn2_fwd")
    loss, dh4, dg_final = _loss_bwd(h4, gains["g_final"], target, "loss_bwd")

    gs = {"g_final": dg_final}
    (dgu2, dh4b, dh3, gs["g_ffn2"]), _ = _ffn_bwd(dh4, h3, gains["g_ffn2"], gate2, up2, wg2, wd2, "ffn2_bwd")
    grads_a = [_mm_tn_rows(dgu2, n4, FF_PAD, "dw_ffn2_gu"),
               _mm_tn_rows(act2, dh4b, FF_BLK, "dw_ffn2_down").reshape(N_DEV, DOWN_ROWS, D)]
    names_a = ["w_ffn2_gu", "w_ffn2_down"]
    (dh3b, dqx, dkv, dh2, gs["g_cross"]), landed_a = _cross_bwd(
        dh3, h2, gains["g_cross"], qx, kv, w_cross, "cross_bwd", comm=_pair_exchange(grads_a))
    sums_a = _reduce_group(grads_a, landed_a, core, names_a)
    grads_b = [_mm_tn_cols(mn, dkv, "dw_ckv"),
               jnp.concatenate([_by_device(_mm_tn(hn, dqx, "dw_cq")), _by_device(_mm_tn(o_x, dh3b, "dw_co"))], axis=1)]
    names_b = ["w_ckv", "cross"]
    gs["g_mem"] = _memkv_bwd(dkv, mem, gains["g_mem"], w_ckv, "memkv_bwd")
    (dh2b, da_mix, db_mix, dgp, dyc, dysb, gs["b_gate"]), landed_b = _mix_out_bwd(
        dh2, a_mix, b_mix, pcg, b_gate, w_mix, "mix_out_bwd", comm=_pair_exchange(grads_b))
    sums_b = _reduce_group(grads_b, landed_b, core, names_b)
    grads_c = [jnp.concatenate([_by_device(_mm_tn(yc, da_mix, "dw_conv_out")),
                                _by_device(_mm_tn(ysb, db_mix, "dw_attn_out")),
                                _by_device(_mm_tn(merged, dh2b, "dw_o"))], axis=1)]
    landed_c = _run_exchange(_pair_exchange(grads_c), "grads_to_sibling_mix")
    sums_c = _reduce_group(grads_c, landed_c, core, ["mix"])
    (dq, dkv_sb), parts_abc = _sb_bwd(qkv, dysb, ctot, "sb_bwd", comm=_chip_exchange(sums_a + sums_b + sums_c))
    dconv, gs["conv_w"] = _conv_bwd(pcg, conv_w, dyc, "conv_bwd")
    grads_d = [jnp.concatenate(
        [_mm_tn_cols(u, dconv, "dw_in_conv"), _mm_tn(u, dq, "dw_in_q")[None], _mm_tn_cols(u, dkv_sb, "dw_in_kv"),
         _mm_tn_cols(u, dgp, "dw_in_gates")], axis=0)]
    (dh1, gs["g_mix"]), landed_d = _inproj_bwd(dconv, dq, dkv_sb, dgp, w_in, h1, gains["g_mix"], dh2, "inproj_bwd",
                                               comm=_pair_exchange(grads_d))
    sums_d = _reduce_group(grads_d, landed_d, core, ["w_in"])
    (dgu1, dh1b, dx, gs["g_ffn1"]), parts_d = _ffn_bwd(dh1, x, gains["g_ffn1"], gate1, up1, wg1, wd1, "ffn1_bwd",
                                                       comm=_chip_exchange(sums_d))
    grads_e = [_mm_tn_rows(dgu1, n1, FF_PAD, "dw_ffn1_gu"),
               _mm_tn_rows(act1, dh1b, FF_BLK, "dw_ffn1_down").reshape(N_DEV, DOWN_ROWS, D)]
    names_e = ["w_ffn1_gu", "w_ffn1_down"]
    landed_e = _run_exchange(_pair_exchange(grads_e), "grads_to_sibling_ffn1")
    sums_e = dict(zip(names_e, _reduce_group(grads_e, landed_e, core, names_e)))

    names = names_a + names_b + ["mix"] + ["w_in"]
    return loss, dx, dict(zip(names, parts_abc + parts_d)), sums_e, gs


def kernel(x, mem, g_ffn1, w_ffn1_gu, w_ffn1_down, g_mix, w_in, b_gate, conv_w, w_conv_out, w_attn_out, w_o, g_cross, g_mem, w_cq, w_ckv, w_co, g_ffn2, w_ffn2_gu, w_ffn2_down, g_final, loss_target, m_g_ffn1, m_w_ffn1_gu, m_w_ffn1_down, m_g_mix, m_w_in, m_b_gate, m_conv_w, m_w_conv_out, m_w_attn_out, m_w_o, m_g_cross, m_g_mem, m_w_cq, m_w_ckv, m_w_co, m_g_ffn2, m_w_ffn2_gu, m_w_ffn2_down, m_g_final, v_g_ffn1, v_w_ffn1_gu, v_w_ffn1_down, v_g_mix, v_w_in, v_b_gate, v_conv_w, v_w_conv_out, v_w_attn_out, v_w_o, v_g_cross, v_g_mem, v_w_cq, v_w_ckv, v_w_co, v_g_ffn2, v_w_ffn2_gu, v_w_ffn2_down, v_g_final):
    args = locals()
    wts = {n: args[n] for n in WEIGHT_ORDER}
    mom1 = {n: args["m_" + n] for n in WEIGHT_ORDER}
    mom2 = {n: args["v_" + n] for n in WEIGHT_ORDER}
    cx, cy, cc = _mesh_pos()
    dev = 4 * cx + 2 * cy + cc
    conv_cols = D // N_DEV

    conv_pad = jnp.concatenate([conv_w, jnp.zeros((SMALL_R - CONV_K, conv_cols), F32)], axis=0)
    gains = {n: wts[n].reshape(1, D) for n in GAINS}
    loss8, dx, parts, sums_e, gs = _step(x[0], mem[0], loss_target[0], _exchange_shards(wts), conv_pad, gains,
                                 b_gate.reshape(1, 2 * D), cc.reshape(1).astype(jnp.int32))
    loss = lax.psum(loss8[0, 0], MESH_AXES)

    grads, delta, new_m, new_v = {}, {}, {}, {}

    def operands(n, transposed):
        trio = (wts[n], mom1[n], mom2[n])
        return tuple(a.T for a in trio) if transposed else trio

    def record(n, res, transposed):
        grads[n], delta[n], new_m[n], new_v[n] = [r.T for r in res] if transposed else res

    early = [("w_ffn2_gu", "w_ffn2_gu", 0, True), ("w_ffn2_down", "w_ffn2_down", 0, False),
             ("w_ckv", "w_ckv", 0, False), ("w_in", "w_in", 0, False)]
    early += [(n, "mix", k, False) for k, n in enumerate(MIX_MATS)]
    early += [(n, "cross", k, False) for k, n in enumerate(CROSS_MATS)]
    last_names = ["w_ffn1_gu", "w_ffn1_down"]
    last_sums = [sums_e[n] for n in last_names]
    send_sems, recv_sems, sums_thru, lands_thru, token = _chip_exchange_begin(
        last_sums, [lax.empty(p.shape, p.dtype) for p in last_sums], "grads_to_chips_ffn1_begin")
    for n, buf, row_block, transposed in early:
        w, m1, m2 = operands(n, transposed)
        record(n, _adamw(w, parts[buf], m1, m2, "adamw_" + n, row_block, token=token), transposed)
    after = jnp.concatenate([new_v[n][:1, :1] for n, _, _, _ in early], axis=0)
    own_parts, landed = _chip_exchange_end(send_sems, recv_sems, sums_thru, lands_thru, after,
                                           "grads_to_chips_ffn1_end")
    chip = (2 * cx + cy).reshape(1).astype(jnp.int32)
    for n, own, land, transposed in zip(last_names, own_parts, landed, (True, False)):
        w, m1, m2 = operands(n, transposed)
        record(n, _adamw_own(w, land, own, chip, m1, m2, "adamw_" + n), transposed)

    gs_rows = {n: gs[n] for n in GAINS + ("b_gate",)}
    small_all = _run_exchange(_gather_exchange([_pack_small(gs_rows, gs["conv_w"][:CONV_K])]), "gather_small_grads")[0]
    grad_small = _unpack_small(_sum_slots(small_all, "small_grads_sum"))
    grad_small["conv_w"] = lax.dynamic_slice_in_dim(grad_small["conv_w"], dev * conv_cols, conv_cols, axis=1)
    grads.update(grad_small)

    def small_buf(vals):
        return _pack_small(vals, jnp.concatenate([vals["conv_w"], jnp.zeros((CONV_K, D - conv_cols), F32)], axis=1))

    _, d_s, m_s, v_s = _adamw(small_buf(wts), small_buf(grads)[None], small_buf(mom1), small_buf(mom2), "adamw_small")
    for res, buf in ((delta, d_s), (new_m, m_s), (new_v, v_s)):
        un = _unpack_small(buf)
        for n in GAINS + ("b_gate",):
            res[n] = un[n]
        res["conv_w"] = un["conv_w"][:, :conv_cols]

    return (loss, dx[None], *[grads[n] for n in WEIGHT_ORDER], *[delta[n] for n in WEIGHT_ORDER],
            *[new_m[n] for n in WEIGHT_ORDER], *[new_v[n] for n in WEIGHT_ORDER])
```

```python
import types

import jax
import jax.numpy as jnp
from jax import lax
from jax.experimental import pallas as pl
from jax.experimental.pallas import tpu as pltpu

F32 = jnp.float32
BF16 = jnp.bfloat16

D = 1024
DFF = 2816
SB_H = 8
SB_DH = 128
X_H = 4
X_DH = 256
CONV_K = 3
RMS_EPS = 1e-6
N_DEV = 8
N_CHIP = 4
SQ_ROWS = D // N_DEV

ADAM_LR = 0.001
ADAM_B1 = 0.9
ADAM_B2 = 0.999
ADAM_EPS = 1e-08
ADAM_WD = 0.01
ADAM_STEP = 10

TM = 256
TQ = 512
TK = 256
SB_HPS = 2
VMEM_LIMIT = 56 << 20

FF_BLK = DFF // 4
FF_PAD = 768
FF_SUB = 256
DOWN_ROWS = DFF // N_DEV

MIX_MATS = ("w_conv_out", "w_attn_out", "w_o")
CROSS_MATS = ("w_cq", "w_co")

MESH_AXES = ("x", "y", "c")
_ANY = pl.BlockSpec(memory_space=pl.ANY)


def _cparams(n_axes=1):
    return pltpu.CompilerParams(
        dimension_semantics=("arbitrary",) * n_axes, vmem_limit_bytes=VMEM_LIMIT)


def _row_spec(tm, n):
    return pl.BlockSpec((tm, n), lambda i: (i, 0))


def _blk_row_spec(nb, tm, n):
    return pl.BlockSpec((nb, tm, n), lambda i: (0, i, 0))


def _const_spec(shape):
    zeros = (0,) * len(shape)
    return pl.BlockSpec(shape, lambda i: zeros)


def _dot(a, b):
    return jnp.dot(a, b, preferred_element_type=F32)


def _dot_nt(a, b):
    return lax.dot_general(a, b, (((1,), (1,)), ((), ())), preferred_element_type=F32)


def _dot_tn(a, b):
    return lax.dot_general(a, b, (((0,), (0,)), ((), ())), preferred_element_type=F32)


def _sigmoid(x):
    return 1.0 / (1.0 + jnp.exp(-x))


def _call(body, operands, *, grid, in_specs, out_specs, out_shape, scratch_shapes, name, comm=None):
    n_in, n_out, n_sc = len(in_specs), len(out_specs), len(scratch_shapes)
    if comm is None:
        outs = pl.pallas_call(
            body, grid=grid, name=name, in_specs=in_specs, out_specs=out_specs, out_shape=out_shape,
            scratch_shapes=scratch_shapes, compiler_params=_cparams(len(grid)))(*operands)
        return list(outs), []
    c_in, c_out, c_sem = len(comm.inputs), len(comm.out_shapes), len(comm.sem_shapes)

    def hosted(*refs):
        bounds = [0, n_in, c_in, n_out, c_out, n_sc, c_sem]
        parts, pos = [], 0
        for k in bounds[1:]:
            parts.append(refs[pos:pos + k])
            pos += k
        ins, cins, outs, couts, scr, sems = parts
        step, n_steps = pl.program_id(0), grid[0]
        for ax in range(1, len(grid)):
            step, n_steps = step * grid[ax] + pl.program_id(ax), n_steps * grid[ax]

        @pl.when(step == 0)
        def _():
            comm.start(cins, couts, sems)

        @pl.when(step == (2 * n_steps) // 3)
        def _():
            comm.middle(cins, couts, sems)

        body(*ins, *outs, *scr)

        @pl.when(step == n_steps - 1)
        def _():
            comm.finish(cins, couts, sems)

    res = pl.pallas_call(
        hosted, grid=grid, name=name, in_specs=list(in_specs) + [_ANY] * c_in,
        out_specs=list(out_specs) + [_ANY] * c_out, out_shape=list(out_shape) + list(comm.out_shapes),
        scratch_shapes=list(scratch_shapes) + list(comm.sem_shapes),
        compiler_params=_cparams(len(grid)))(*operands, *comm.inputs)
    return list(res[:n_out]), list(res[n_out:])


def _load_resident(step, pairs, sems):
    @pl.when(step == 0)
    def _():
        copies = [pltpu.make_async_copy(src, dst, sems.at[k]) for k, (src, dst) in enumerate(pairs)]
        for cp in copies:
            cp.start()
        for cp in copies:
            cp.wait()


def _square_pairs(buf_hbm, index, dst):
    off = index * SQ_ROWS
    return [(buf_hbm.at[d, off:off + SQ_ROWS, :], dst.at[d * SQ_ROWS:(d + 1) * SQ_ROWS, :]) for d in range(N_DEV)]


def _down_pairs(wd_hbm, dst):
    return [(wd_hbm.at[d], dst.at[d // 2, (d % 2) * DOWN_ROWS:(d % 2 + 1) * DOWN_ROWS, :]) for d in range(N_DEV)]


def _zero_down_pad(step, dst):
    @pl.when(step == 0)
    def _():
        dst[:, FF_BLK:, :] = jnp.zeros((4, FF_PAD - FF_BLK, D), BF16)


def _rms_fwd_tile(xt, g):
    r = lax.rsqrt(jnp.mean(xt * xt, axis=-1, keepdims=True) + RMS_EPS)
    return (xt * r) * g


def _rms_bwd_tile(xt, g, dn):
    r = lax.rsqrt(jnp.mean(xt * xt, axis=-1, keepdims=True) + RMS_EPS)
    xhat = xt * r
    dxhat = dn * g
    dx = r * (dxhat - xhat * jnp.mean(dxhat * xhat, axis=-1, keepdims=True))
    dg = jnp.sum(dn * xhat, axis=0, keepdims=True)
    return dx, dg


def _accumulate(ref, step, value):
    @pl.when(step == 0)
    def _():
        ref[...] = value

    @pl.when(step != 0)
    def _():
        ref[...] = ref[...] + value


def _ffn_fwd(x, g, wgu, wd, name, comm=None):
    t = x.shape[0]

    def body(x_ref, g_ref, wgu_hbm, wd_hbm, n_ref, gate_ref, up_ref, act_ref, h_ref, wgu_v, wd_v, sems):
        step = pl.program_id(0)
        _zero_down_pad(step, wd_v)
        _load_resident(step, [(wgu_hbm, wgu_v)] + _down_pairs(wd_hbm, wd_v), sems)
        xt = x_ref[...]
        n = _rms_fwd_tile(xt, g_ref[...]).astype(BF16)
        n_ref[...] = n
        acc = jnp.zeros((TM, D), F32)
        for j in range(4):
            for s in range(FF_PAD // FF_SUB):
                lo, hi = s * FF_SUB, (s + 1) * FF_SUB
                gt = _dot_nt(n, wgu_v[j, lo:hi, :])
                ut = _dot_nt(n, wgu_v[4 + j, lo:hi, :])
                gate_ref[j, :, lo:hi] = gt.astype(BF16)
                up_ref[j, :, lo:hi] = ut.astype(BF16)
                act_ref[j, :, lo:hi] = ((gt * _sigmoid(gt)) * ut).astype(BF16)
            acc = acc + _dot(act_ref[j], wd_v[j])
        h_ref[...] = xt + 0.5 * acc

    ff = jax.ShapeDtypeStruct((4, t, FF_PAD), BF16)
    return _call(
        body, (x, g, wgu, wd), grid=(t // TM,), name=name, comm=comm,
        in_specs=[_row_spec(TM, D), _const_spec((1, D)), _ANY, _ANY],
        out_specs=[_row_spec(TM, D)] + [_blk_row_spec(4, TM, FF_PAD)] * 3 + [_row_spec(TM, D)],
        out_shape=[jax.ShapeDtypeStruct((t, D), BF16), ff, ff, ff, jax.ShapeDtypeStruct((t, D), F32)],
        scratch_shapes=[pltpu.VMEM((N_DEV, FF_PAD, D), BF16), pltpu.VMEM((4, FF_PAD, D), BF16),
                        pltpu.SemaphoreType.DMA((1 + N_DEV,))])


def _ffn_bwd(dh, xin, g, gate, up, wgu, wd, name, comm=None):
    t = dh.shape[0]

    def body(dh_ref, x_ref, g_ref, gate_ref, up_ref, wgu_hbm, wd_hbm,
             dgu_ref, dhb_ref, dx_ref, dg_ref, wgu_v, wd_v, sems):
        step = pl.program_id(0)
        _zero_down_pad(step, wd_v)
        _load_resident(step, [(wgu_hbm, wgu_v)] + _down_pairs(wd_hbm, wd_v), sems)
        dht = dh_ref[...]
        dhb = (0.5 * dht).astype(BF16)
        dhb_ref[...] = dhb
        dn = jnp.zeros((TM, D), F32)
        for j in range(4):
            for s in range(FF_PAD // FF_SUB):
                lo, hi = s * FF_SUB, (s + 1) * FF_SUB
                da = _dot_nt(dhb, wd_v[j, lo:hi, :])
                gt = gate_ref[j, :, lo:hi].astype(F32)
                ut = up_ref[j, :, lo:hi].astype(F32)
                sg = _sigmoid(gt)
                dgt = (da * ut * (sg * (1.0 + gt * (1.0 - sg)))).astype(BF16)
                dut = (da * (gt * sg)).astype(BF16)
                dgu_ref[j, :, lo:hi] = dgt
                dgu_ref[4 + j, :, lo:hi] = dut
            dn = dn + _dot(dgu_ref[j], wgu_v[j]) + _dot(dgu_ref[4 + j], wgu_v[4 + j])
        dx, dg = _rms_bwd_tile(x_ref[...], g_ref[...], dn)
        dx_ref[...] = dht + dx
        _accumulate(dg_ref, step, dg)

    return _call(
        body, (dh, xin, g, gate, up, wgu, wd), grid=(t // TM,), name=name, comm=comm,
        in_specs=[_row_spec(TM, D), _row_spec(TM, D), _const_spec((1, D)), _blk_row_spec(4, TM, FF_PAD),
                  _blk_row_spec(4, TM, FF_PAD), _ANY, _ANY],
        out_specs=[_blk_row_spec(N_DEV, TM, FF_PAD), _row_spec(TM, D), _row_spec(TM, D), _const_spec((1, D))],
        out_shape=[jax.ShapeDtypeStruct((N_DEV, t, FF_PAD), BF16), jax.ShapeDtypeStruct((t, D), BF16),
                   jax.ShapeDtypeStruct((t, D), F32), jax.ShapeDtypeStruct((1, D), F32)],
        scratch_shapes=[pltpu.VMEM((N_DEV, FF_PAD, D), BF16), pltpu.VMEM((4, FF_PAD, D), BF16),
                        pltpu.SemaphoreType.DMA((1 + N_DEV,))])


WIDE_TILES = (1024, 512, 256, 128)


def _pick_tile(n, options=(512, 256, 128)):
    for o in options:
        if n % o == 0:
            return o
    return n


def _mm_tn(a, b, name):
    k, m = a.shape
    _, n = b.shape
    tm, tn = _pick_tile(m, WIDE_TILES), _pick_tile(n)

    def body(a_ref, b_ref, o_ref):
        o_ref[...] = _dot_tn(a_ref[...].astype(BF16), b_ref[...].astype(BF16)).astype(BF16)

    return pl.pallas_call(
        body, grid=(m // tm, n // tn), name=name,
        in_specs=[pl.BlockSpec((k, tm), lambda i, j: (0, i)), pl.BlockSpec((k, tn), lambda i, j: (0, j))],
        out_specs=pl.BlockSpec((tm, tn), lambda i, j: (i, j)),
        out_shape=jax.ShapeDtypeStruct((m, n), BF16),
        compiler_params=_cparams(2),
    )(a, b)


def _mm_tn_cols(a, b, name):
    k, m = a.shape
    nb, _, n = b.shape
    tm = _pick_tile(m, WIDE_TILES)

    def body(a_ref, b_ref, o_ref):
        o_ref[0] = _dot_tn(a_ref[...].astype(BF16), b_ref[0].astype(BF16)).astype(BF16)

    return pl.pallas_call(
        body, grid=(nb, m // tm), name=name,
        in_specs=[pl.BlockSpec((k, tm), lambda j, i: (0, i)), pl.BlockSpec((1, k, n), lambda j, i: (j, 0, 0))],
        out_specs=pl.BlockSpec((1, tm, n), lambda j, i: (j, i, 0)),
        out_shape=jax.ShapeDtypeStruct((nb, m, n), BF16),
        compiler_params=_cparams(2),
    )(a, b)


def _mm_tn_rows(a, b, keep, name):
    nb, k, m = a.shape
    _, n = b.shape
    tn = _pick_tile(n, WIDE_TILES)

    def body(a_ref, b_ref, o_ref):
        o_ref[0] = _dot_tn(a_ref[0], b_ref[...])[:keep].astype(BF16)

    return pl.pallas_call(
        body, grid=(nb, n // tn), name=name,
        in_specs=[pl.BlockSpec((1, k, m), lambda j, i: (j, 0, 0)), pl.BlockSpec((k, tn), lambda j, i: (0, i))],
        out_specs=pl.BlockSpec((1, keep, tn), lambda j, i: (j, 0, i)),
        out_shape=jax.ShapeDtypeStruct((nb, keep, n), BF16),
        compiler_params=_cparams(2),
    )(a, b)


PCG_W = 5 * D
QKV_W = 3 * D
PROJ_SUB = 512


def _inproj_fwd(h, g, w_in, name, comm=None):
    t = h.shape[0]

    def body(h_ref, g_ref, w_hbm, u_ref, pcg_ref, qkv_ref, w_v, sems):
        _load_resident(pl.program_id(0), [(w_hbm, w_v)], sems)
        u = _rms_fwd_tile(h_ref[...], g_ref[...]).astype(BF16)
        u_ref[...] = u
        for blk in range(N_DEV):
            for s in range(D // PROJ_SUB):
                lo, hi = s * PROJ_SUB, (s + 1) * PROJ_SUB
                p = _dot(u, w_v[blk, :, lo:hi])
                if blk < 3:
                    pcg_ref[:, blk * D + lo:blk * D + hi] = p
                elif blk < 6:
                    qkv_ref[:, (blk - 3) * D + lo:(blk - 3) * D + hi] = p.astype(BF16)
                else:
                    pcg_ref[:, (blk - 3) * D + lo:(blk - 3) * D + hi] = p

    return _call(
        body, (h, g, w_in), grid=(t // TM,), name=name, comm=comm,
        in_specs=[_row_spec(TM, D), _const_spec((1, D)), _ANY],
        out_specs=[_row_spec(TM, D), _row_spec(TM, PCG_W), _row_spec(TM, QKV_W)],
        out_shape=[jax.ShapeDtypeStruct((t, D), BF16), jax.ShapeDtypeStruct((t, PCG_W), F32),
                   jax.ShapeDtypeStruct((t, QKV_W), BF16)],
        scratch_shapes=[pltpu.VMEM((N_DEV, D, D), BF16), pltpu.SemaphoreType.DMA((1,))])


CONV_CW = 256


def _shift_down(v, k, rows):
    return jnp.where(rows >= k, pltpu.roll(v, k, 0), 0.0)


def _shift_up(v, k, rows, t):
    return jnp.where(rows < t - k, pltpu.roll(v, t - k, 0), 0.0)


def _col_spec(t, cw, off):
    return pl.BlockSpec((t, cw), lambda j: (0, j + off))


def _conv_fwd(pcg, conv_w, name):
    t = pcg.shape[0]
    nb = D // CONV_CW

    def body(cb_ref, cc_ref, cx_ref, w_ref, y_ref):
        rows = lax.broadcasted_iota(jnp.int32, (t, CONV_CW), 0)
        xc = cc_ref[...] * cx_ref[...]
        conv = (w_ref[0:1, :] * _shift_down(xc, 2, rows) + w_ref[1:2, :] * _shift_down(xc, 1, rows)
                + w_ref[2:3, :] * xc)
        y_ref[...] = (cb_ref[...] * conv).astype(BF16)

    return pl.pallas_call(
        body, grid=(nb,), name=name,
        in_specs=[_col_spec(t, CONV_CW, 0), _col_spec(t, CONV_CW, nb), _col_spec(t, CONV_CW, 2 * nb),
                  pl.BlockSpec((CONV_K, CONV_CW), lambda j: (0, j))],
        out_specs=_col_spec(t, CONV_CW, 0),
        out_shape=jax.ShapeDtypeStruct((t, D), BF16),
        compiler_params=_cparams(),
    )(pcg, pcg, pcg, conv_w)


def _conv_bwd(pcg, conv_w, dyc, name):
    t = pcg.shape[0]
    nb = D // CONV_CW

    def body(cb_ref, cc_ref, cx_ref, w_ref, dy_ref, dc_ref, dw_ref):
        rows = lax.broadcasted_iota(jnp.int32, (t, CONV_CW), 0)
        cc, cx = cc_ref[...], cx_ref[...]
        xc = cc * cx
        x1 = _shift_down(xc, 1, rows)
        x2 = _shift_down(xc, 2, rows)
        w0, w1, w2 = w_ref[0:1, :], w_ref[1:2, :], w_ref[2:3, :]
        conv = w0 * x2 + w1 * x1 + w2 * xc
        dy = dy_ref[...]
        dc_ref[0] = (dy * conv).astype(BF16)
        dconv = dy * cb_ref[...]
        dw_ref[...] = jnp.zeros((8, CONV_CW), F32)
        dw_ref[0:1, :] = jnp.sum(dconv * x2, axis=0, keepdims=True)
        dw_ref[1:2, :] = jnp.sum(dconv * x1, axis=0, keepdims=True)
        dw_ref[2:3, :] = jnp.sum(dconv * xc, axis=0, keepdims=True)
        dxc = w2 * dconv + w1 * _shift_up(dconv, 1, rows, t) + w0 * _shift_up(dconv, 2, rows, t)
        dc_ref[1] = (dxc * cx).astype(BF16)
        dc_ref[2] = (dxc * cc).astype(BF16)

    return pl.pallas_call(
        body, grid=(nb,), name=name,
        in_specs=[_col_spec(t, CONV_CW, 0), _col_spec(t, CONV_CW, nb), _col_spec(t, CONV_CW, 2 * nb),
                  pl.BlockSpec((CONV_K, CONV_CW), lambda j: (0, j)), _col_spec(t, CONV_CW, 0)],
        out_specs=[pl.BlockSpec((3, t, CONV_CW), lambda j: (0, 0, j)), pl.BlockSpec((8, CONV_CW), lambda j: (0, j))],
        out_shape=[jax.ShapeDtypeStruct((3, t, D), BF16), jax.ShapeDtypeStruct((8, D), F32)],
        compiler_params=_cparams(),
    )(pcg, pcg, pcg, conv_w, dyc)


def _tri2(cond):
    rr = lax.broadcasted_iota(jnp.int32, (2 * TK, TK), 0) & (TK - 1)
    cc = lax.broadcasted_iota(jnp.int32, (2 * TK, TK), 1)
    return cond(rr, cc).astype(BF16)


def _causal(shift, row0=0):
    rr = lax.broadcasted_iota(jnp.int32, (TQ - row0, TK), 0) + row0
    cc = lax.broadcasted_iota(jnp.int32, (TQ - row0, TK), 1)
    return cc + shift < rr


def _cumdot(v, tri2):
    hi = v.astype(BF16)
    lo = (v - hi.astype(F32)).astype(BF16)
    return _dot(jnp.concatenate([hi, lo], axis=1), tri2)


def _log_1m_beta(z):
    return -(jnp.maximum(z, 0.0) + jnp.log(1.0 + jnp.exp(-jnp.abs(z))))


def _sb_specs(t):
    g = SB_H // SB_HPS
    w = SB_HPS * SB_DH
    q_spec = pl.BlockSpec((TQ, w), lambda h, i: (i, h))
    k_spec = pl.BlockSpec((t, w), lambda h, i: (0, g + h))
    v_spec = pl.BlockSpec((t, w), lambda h, i: (0, 2 * g + h))
    ct_spec = pl.BlockSpec((SB_HPS, TQ, 1), lambda h, i: (h, i, 0))
    return g, w, q_spec, k_spec, v_spec, ct_spec


def _sb_fwd(qkv, name, comm=None):
    t = qkv.shape[0]
    scale = SB_DH ** -0.5
    g, w, q_spec, k_spec, v_spec, ct_spec = _sb_specs(t)

    def body(q_ref, k_ref, v_ref, y_ref, ct_ref):
        i = pl.program_id(1)
        later = _tri2(lambda j, s: j > s)
        n_diag = TQ // TK

        def block(j, carry, shift):
            off = pl.multiple_of(j * TK, TK)
            zs, ms = [], []
            for hd in range(SB_HPS):
                cols = slice(hd * SB_DH, (hd + 1) * SB_DH)
                z = _dot_nt(q_ref[:, cols], k_ref[pl.ds(off, TK), cols]) * scale
                m = _log_1m_beta(z)
                if shift is not None:
                    m = jnp.where(_causal(shift), m, 0.0)
                zs.append(z)
                ms.append(m)
            after = _cumdot(jnp.concatenate(ms, axis=0), later)
            out = []
            for hd in range(SB_HPS):
                acc, c_sum = carry[hd]
                cols = slice(hd * SB_DH, (hd + 1) * SB_DH)
                a = jnp.exp((ms[hd] + zs[hd]) + (c_sum + after[hd * TQ:(hd + 1) * TQ]))
                if shift is not None:
                    a = jnp.where(_causal(shift), a, 0.0)
                out.append((acc + _dot(a.astype(BF16), v_ref[pl.ds(off, TK), cols]),
                            c_sum + jnp.sum(ms[hd], axis=1, keepdims=True)))
            return tuple(out)

        carry = tuple((jnp.zeros((TQ, SB_DH), F32), jnp.zeros((TQ, 1), F32)) for _ in range(SB_HPS))
        for d in reversed(range(n_diag)):
            carry = block(i * n_diag + d, carry, d * TK)
        carry = lax.fori_loop(0, i * n_diag, lambda jj, c: block(i * n_diag - 1 - jj, c, None), carry)
        for hd in range(SB_HPS):
            y_ref[:, hd * SB_DH:(hd + 1) * SB_DH] = carry[hd][0].astype(BF16)
            ct_ref[hd] = carry[hd][1]

    return _call(
        body, (qkv, qkv, qkv), grid=(g, t // TQ), name=name, comm=comm,
        in_specs=[q_spec, k_spec, v_spec],
        out_specs=[q_spec, ct_spec],
        out_shape=[jax.ShapeDtypeStruct((t, D), BF16), jax.ShapeDtypeStruct((SB_H, t, 1), F32)],
        scratch_shapes=[])


def _sb_bwd(qkv, dy, ctot, name, comm=None):
    t = qkv.shape[0]
    scale = SB_DH ** -0.5
    g, w, q_spec, k_spec, v_spec, ct_spec = _sb_specs(t)
    acc_spec = pl.BlockSpec((2, t, w), lambda h, i: (0, 0, h))

    def body(q_ref, k_ref, v_ref, dy_ref, ct_ref, dq_ref, dkv_ref):
        i = pl.program_id(1)

        @pl.when(i == 0)
        def _():
            dkv_ref[...] = jnp.zeros_like(dkv_ref)

        upto = _tri2(lambda j, s: j <= s)
        n_diag = TQ // TK

        def block(j, carry, shift):
            off = pl.multiple_of(j * TK, TK)
            r0 = 0 if shift is None else shift
            nr = TQ - r0
            causal = None if shift is None else _causal(shift, r0)

            def grow(old, delta):
                return old + delta if r0 == 0 else jnp.concatenate([old[:r0], old[r0:] + delta], axis=0)

            zs, ms = [], []
            for hd in range(SB_HPS):
                cols = slice(hd * SB_DH, (hd + 1) * SB_DH)
                z = _dot_nt(q_ref[r0:, cols], k_ref[pl.ds(off, TK), cols]) * scale
                m = _log_1m_beta(z)
                if causal is not None:
                    m = jnp.where(causal, m, 0.0)
                zs.append(z)
                ms.append(m)
            m_upto = _cumdot(jnp.concatenate(ms, axis=0), upto)
            ls, a_s, es = [], [], []
            for hd in range(SB_HPS):
                cols = slice(hd * SB_DH, (hd + 1) * SB_DH)
                l = ms[hd] + zs[hd]
                a = jnp.exp(l + ((ct_ref[hd, r0:] - carry[hd][1][r0:]) - m_upto[hd * nr:(hd + 1) * nr]))
                if causal is not None:
                    a = jnp.where(causal, a, 0.0)
                ls.append(l)
                a_s.append(a)
                es.append(_dot_nt(dy_ref[r0:, cols], v_ref[pl.ds(off, TK), cols]) * a)
            e_upto = _dot(jnp.concatenate(es, axis=0).astype(BF16), upto[:TK])
            out = []
            for hd in range(SB_HPS):
                dq, p_sum, e_sum = carry[hd]
                cols = slice(hd * SB_DH, (hd + 1) * SB_DH)
                e = es[hd]
                dz = e - jnp.exp(ls[hd]) * (e_sum[r0:] + e_upto[hd * nr:(hd + 1) * nr])
                if causal is not None:
                    dz = jnp.where(causal, dz, 0.0)
                dzs = (dz * scale).astype(BF16)
                dkv_ref[0, pl.ds(off, TK), cols] += _dot_tn(dzs, q_ref[r0:, cols])
                dkv_ref[1, pl.ds(off, TK), cols] += _dot_tn(a_s[hd].astype(BF16), dy_ref[r0:, cols])
                out.append((grow(dq, _dot(dzs, k_ref[pl.ds(off, TK), cols])),
                            grow(p_sum, jnp.sum(ms[hd], axis=1, keepdims=True)),
                            grow(e_sum, jnp.sum(e, axis=1, keepdims=True))))
            return tuple(out)

        zero = jnp.zeros((TQ, 1), F32)
        init = tuple((jnp.zeros((TQ, SB_DH), F32), zero, zero) for _ in range(SB_HPS))
        carry = lax.fori_loop(0, i * n_diag, lambda j, c: block(j, c, None), init)
        for d in range(n_diag):
            carry = block(i * n_diag + d, carry, d * TK)
        for hd in range(SB_HPS):
            dq_ref[:, hd * SB_DH:(hd + 1) * SB_DH] = carry[hd][0].astype(BF16)

    return _call(
        body, (qkv, qkv, qkv, dy, ctot), grid=(g, t // TQ), name=name, comm=comm,
        in_specs=[q_spec, k_spec, v_spec, q_spec, ct_spec],
        out_specs=[q_spec, acc_spec],
        out_shape=[jax.ShapeDtypeStruct((t, D), BF16), jax.ShapeDtypeStruct((2, t, D), F32)],
        scratch_shapes=[])


def _gate_specs():
    return [pl.BlockSpec((TM, D), lambda i: (i, 3)), pl.BlockSpec((TM, D), lambda i: (i, 4))]


def _mix_pairs(mix_hbm, dsts):
    pairs = []
    for index, dst in enumerate(dsts):
        pairs += _square_pairs(mix_hbm, index, dst)
    return pairs


def _mix_out_fwd(yc, ysb, pcg, b_gate, h, w_mix, name):
    t = h.shape[0]

    def body(yc_ref, ysb_ref, gc_ref, gs_ref, b_ref, h_ref, mix_hbm,
             a_ref, b_out_ref, mg_ref, h2_ref, wc_v, wa_v, wo_v, sems):
        _load_resident(pl.program_id(0), _mix_pairs(mix_hbm, (wc_v, wa_v, wo_v)), sems)
        a = _dot(yc_ref[...], wc_v[...])
        b = _dot(ysb_ref[...], wa_v[...])
        merged = (_sigmoid(gc_ref[...] + b_ref[:, :D]) * a + _sigmoid(gs_ref[...] + b_ref[:, D:]) * b).astype(BF16)
        a_ref[...] = a
        b_out_ref[...] = b
        mg_ref[...] = merged
        h2_ref[...] = h_ref[...] + _dot(merged, wo_v[...])

    return pl.pallas_call(
        body, grid=(t // TM,), name=name,
        in_specs=[_row_spec(TM, D), _row_spec(TM, D)] + _gate_specs()
                 + [_const_spec((1, 2 * D)), _row_spec(TM, D), _ANY],
        out_specs=[_row_spec(TM, D)] * 4,
        out_shape=[jax.ShapeDtypeStruct((t, D), F32), jax.ShapeDtypeStruct((t, D), F32),
                   jax.ShapeDtypeStruct((t, D), BF16), jax.ShapeDtypeStruct((t, D), F32)],
        scratch_shapes=[pltpu.VMEM((D, D), BF16)] * 3 + [pltpu.SemaphoreType.DMA((3 * N_DEV,))],
        compiler_params=_cparams(),
    )(yc, ysb, pcg, pcg, b_gate, h, w_mix)


def _mix_out_bwd(dh2, a, b, pcg, b_gate, w_mix, name, comm=None):
    t = dh2.shape[0]

    def body(dh_ref, a_ref, b_ref, gc_ref, gs_ref, bias_ref, mix_hbm,
             dhb_ref, da_ref, db_ref, dgp_ref, dyc_ref, dysb_ref, dbias_ref, wc_v, wa_v, wo_v, sems):
        step = pl.program_id(0)
        _load_resident(step, _mix_pairs(mix_hbm, (wc_v, wa_v, wo_v)), sems)
        dhb = dh_ref[...].astype(BF16)
        dhb_ref[...] = dhb
        dm = _dot_nt(dhb, wo_v[...])
        gc = _sigmoid(gc_ref[...] + bias_ref[:, :D])
        gs = _sigmoid(gs_ref[...] + bias_ref[:, D:])
        da = (dm * gc).astype(BF16)
        db = (dm * gs).astype(BF16)
        da_ref[...] = da
        db_ref[...] = db
        dgc = dm * a_ref[...] * (gc * (1.0 - gc))
        dgs = dm * b_ref[...] * (gs * (1.0 - gs))
        dgp_ref[0] = dgc.astype(BF16)
        dgp_ref[1] = dgs.astype(BF16)
        _accumulate(dbias_ref.at[:, :D], step, jnp.sum(dgc, axis=0, keepdims=True))
        _accumulate(dbias_ref.at[:, D:], step, jnp.sum(dgs, axis=0, keepdims=True))
        dyc_ref[...] = _dot_nt(da, wc_v[...])
        dysb_ref[...] = _dot_nt(db, wa_v[...]).astype(BF16)

    return _call(
        body, (dh2, a, b, pcg, pcg, b_gate, w_mix), grid=(t // TM,), name=name, comm=comm,
        in_specs=[_row_spec(TM, D)] * 3 + _gate_specs() + [_const_spec((1, 2 * D)), _ANY],
        out_specs=[_row_spec(TM, D)] * 3 + [_blk_row_spec(2, TM, D), _row_spec(TM, D), _row_spec(TM, D),
                                            _const_spec((1, 2 * D))],
        out_shape=[jax.ShapeDtypeStruct((t, D), BF16)] * 3
                  + [jax.ShapeDtypeStruct((2, t, D), BF16), jax.ShapeDtypeStruct((t, D), F32),
                     jax.ShapeDtypeStruct((t, D), BF16), jax.ShapeDtypeStruct((1, 2 * D), F32)],
        scratch_shapes=[pltpu.VMEM((D, D), BF16)] * 3 + [pltpu.SemaphoreType.DMA((3 * N_DEV,))])


def _inproj_bwd(dconv, dq, dkv, dgp, w_in, h, g, dh_res, name, comm=None):
    t = h.shape[0]

    def body(dc_ref, dq_ref, dkv_ref, dgp_ref, w_hbm, h_ref, g_ref, dres_ref, dh_ref, dg_ref, w_v, sems):
        step = pl.program_id(0)
        _load_resident(step, [(w_hbm, w_v)], sems)
        du = _dot_nt(dq_ref[...], w_v[3])
        for k in range(3):
            du = du + _dot_nt(dc_ref[k], w_v[k])
        for k in range(2):
            du = du + _dot_nt(dkv_ref[k].astype(BF16), w_v[4 + k]) + _dot_nt(dgp_ref[k], w_v[6 + k])
        dx, dg = _rms_bwd_tile(h_ref[...], g_ref[...], du)
        dh_ref[...] = dres_ref[...] + dx
        _accumulate(dg_ref, step, dg)

    return _call(
        body, (dconv, dq, dkv, dgp, w_in, h, g, dh_res), grid=(t // TM,), name=name, comm=comm,
        in_specs=[_blk_row_spec(3, TM, D), _row_spec(TM, D), _blk_row_spec(2, TM, D), _blk_row_spec(2, TM, D), _ANY,
                  _row_spec(TM, D), _const_spec((1, D)), _row_spec(TM, D)],
        out_specs=[_row_spec(TM, D), _const_spec((1, D))],
        out_shape=[jax.ShapeDtypeStruct((t, D), F32), jax.ShapeDtypeStruct((1, D), F32)],
        scratch_shapes=[pltpu.VMEM((N_DEV, D, D), BF16), pltpu.SemaphoreType.DMA((1,))])


def _memkv_fwd(mem, g, w_ckv, name):
    m = mem.shape[0]

    def body(mem_ref, g_ref, w_ref, mn_ref, kv_ref):
        mn = _rms_fwd_tile(mem_ref[...], g_ref[...]).astype(BF16)
        mn_ref[...] = mn
        for j in range(N_DEV):
            kv_ref[j] = _dot(mn, w_ref[j]).astype(BF16)

    return pl.pallas_call(
        body, grid=(1,), name=name,
        in_specs=[_const_spec((m, D)), _const_spec((1, D)), _const_spec((N_DEV, D, X_DH))],
        out_specs=[_const_spec((m, D)), _const_spec((N_DEV, m, X_DH))],
        out_shape=[jax.ShapeDtypeStruct((m, D), BF16), jax.ShapeDtypeStruct((N_DEV, m, X_DH), BF16)],
        compiler_params=_cparams(),
    )(mem, g, w_ckv)


def _memkv_bwd(dkv, mem, g, w_ckv, name):
    m = mem.shape[0]

    def body(dkv_ref, mem_ref, g_ref, w_ref, dg_ref):
        dmn = jnp.zeros((m, D), F32)
        for j in range(N_DEV):
            dmn = dmn + _dot_nt(dkv_ref[j].astype(BF16), w_ref[j])
        _, dg = _rms_bwd_tile(mem_ref[...], g_ref[...], dmn)
        dg_ref[...] = dg

    return pl.pallas_call(
        body, grid=(1,), name=name,
        in_specs=[_const_spec((N_DEV, m, X_DH)), _const_spec((m, D)), _const_spec((1, D)),
                  _const_spec((N_DEV, D, X_DH))],
        out_specs=_const_spec((1, D)),
        out_shape=jax.ShapeDtypeStruct((1, D), F32),
        compiler_params=_cparams(),
    )(dkv, mem, g, w_ckv)


def _softmax_rows(s):
    e = jnp.exp(s - jnp.max(s, axis=-1, keepdims=True))
    return e / jnp.sum(e, axis=-1, keepdims=True)


def _cross_pairs(cross_hbm, wq_v, wo_v):
    return _square_pairs(cross_hbm, 0, wq_v) + _square_pairs(cross_hbm, 1, wo_v)


def _cross_fwd(h, g, kv, w_cross, name):
    t = h.shape[0]
    m = kv.shape[1]
    scale = X_DH ** -0.5

    def body(h_ref, g_ref, kv_ref, cross_hbm, hn_ref, qx_ref, o_ref, h3_ref, wq_v, wo_v, sems):
        _load_resident(pl.program_id(0), _cross_pairs(cross_hbm, wq_v, wo_v), sems)
        ht = h_ref[...]
        hn = _rms_fwd_tile(ht, g_ref[...]).astype(BF16)
        hn_ref[...] = hn
        qx = _dot(hn, wq_v[...]).astype(BF16)
        qx_ref[...] = qx
        for hd in range(X_H):
            lo, hi = hd * X_DH, (hd + 1) * X_DH
            p = _softmax_rows(_dot_nt(qx[:, lo:hi], kv_ref[hd]) * scale)
            o_ref[:, lo:hi] = _dot(p.astype(BF16), kv_ref[X_H + hd]).astype(BF16)
        h3_ref[...] = ht + _dot(o_ref[...], wo_v[...])

    return pl.pallas_call(
        body, grid=(t // TM,), name=name,
        in_specs=[_row_spec(TM, D), _const_spec((1, D)), _const_spec((N_DEV, m, X_DH)), _ANY],
        out_specs=[_row_spec(TM, D)] * 4,
        out_shape=[jax.ShapeDtypeStruct((t, D), BF16)] * 3 + [jax.ShapeDtypeStruct((t, D), F32)],
        scratch_shapes=[pltpu.VMEM((D, D), BF16)] * 2 + [pltpu.SemaphoreType.DMA((2 * N_DEV,))],
        compiler_params=_cparams(),
    )(h, g, kv, w_cross)


def _cross_bwd(dh3, h, g, qx, kv, w_cross, name, comm=None):
    t = h.shape[0]
    m = kv.shape[1]
    scale = X_DH ** -0.5

    def body(dh_ref, h_ref, g_ref, qx_ref, kv_ref, cross_hbm,
             dhb_ref, dqx_ref, dkv_ref, dh2_ref, dg_ref, wq_v, wo_v, sems):
        step = pl.program_id(0)
        _load_resident(step, _cross_pairs(cross_hbm, wq_v, wo_v), sems)

        @pl.when(step == 0)
        def _():
            dkv_ref[...] = jnp.zeros_like(dkv_ref)

        dht = dh_ref[...]
        dhb = dht.astype(BF16)
        dhb_ref[...] = dhb
        do = _dot_nt(dhb, wo_v[...]).astype(BF16)
        for hd in range(X_H):
            lo, hi = hd * X_DH, (hd + 1) * X_DH
            qh = qx_ref[:, lo:hi]
            kh = kv_ref[hd]
            p = _softmax_rows(_dot_nt(qh, kh) * scale)
            doh = do[:, lo:hi]
            dp = _dot_nt(doh, kv_ref[X_H + hd])
            ds = (p * (dp - jnp.sum(dp * p, axis=-1, keepdims=True)) * scale).astype(BF16)
            dqx_ref[:, lo:hi] = _dot(ds, kh).astype(BF16)
            dkv_ref[hd] += _dot_tn(ds, qh)
            dkv_ref[X_H + hd] += _dot_tn(p.astype(BF16), doh)
        dhn = _dot_nt(dqx_ref[...], wq_v[...])
        dx, dg = _rms_bwd_tile(h_ref[...], g_ref[...], dhn)
        dh2_ref[...] = dht + dx
        _accumulate(dg_ref, step, dg)

    return _call(
        body, (dh3, h, g, qx, kv, w_cross), grid=(t // TM,), name=name, comm=comm,
        in_specs=[_row_spec(TM, D), _row_spec(TM, D), _const_spec((1, D)), _row_spec(TM, D),
                  _const_spec((N_DEV, m, X_DH)), _ANY],
        out_specs=[_row_spec(TM, D), _row_spec(TM, D), _const_spec((N_DEV, m, X_DH)), _row_spec(TM, D),
                   _const_spec((1, D))],
        out_shape=[jax.ShapeDtypeStruct((t, D), BF16), jax.ShapeDtypeStruct((t, D), BF16),
                   jax.ShapeDtypeStruct((N_DEV, m, X_DH), F32), jax.ShapeDtypeStruct((t, D), F32),
                   jax.ShapeDtypeStruct((1, D), F32)],
        scratch_shapes=[pltpu.VMEM((D, D), BF16)] * 2 + [pltpu.SemaphoreType.DMA((2 * N_DEV,))])


def _loss_bwd(h, g, target, name):
    t = h.shape[0]

    def body(h_ref, g_ref, t_ref, loss_ref, dh_ref, dg_ref):
        step = pl.program_id(0)
        ht = h_ref[...]
        gain = g_ref[...]
        diff = _rms_fwd_tile(ht, gain) - t_ref[...]
        part = 0.5 * jnp.sum(jnp.sum(diff * diff, axis=-1, keepdims=True) / D, axis=0, keepdims=True)
        dx, dg = _rms_bwd_tile(ht, gain, diff / D)
        dh_ref[...] = dx
        _accumulate(loss_ref, step, jnp.broadcast_to(part, (8, 128)))
        _accumulate(dg_ref, step, dg)

    return pl.pallas_call(
        body, grid=(t // TM,), name=name,
        in_specs=[_row_spec(TM, D), _const_spec((1, D)), _row_spec(TM, D)],
        out_specs=[_const_spec((8, 128)), _row_spec(TM, D), _const_spec((1, D))],
        out_shape=[jax.ShapeDtypeStruct((8, 128), F32), jax.ShapeDtypeStruct((t, D), F32),
                   jax.ShapeDtypeStruct((1, D), F32)],
        compiler_params=_cparams(),
    )(h, g, target)


def _adamw(w, parts, m, v, name, row_block=0, token=None):
    r, c = w.shape
    n = parts.shape[0]
    tr = _pick_tile(r, (256, 352, 128))
    off = row_block * (r // tr)

    def body(*refs):
        if token is None:
            _adamw_update(None, *refs)
        else:
            _adamw_update(refs[4], *refs[:4], *refs[5:])

    spec = _row_spec(tr, c)
    in_specs = [spec, pl.BlockSpec((n, tr, c), lambda i: (0, i + off, 0)), spec, spec]
    operands = (w, parts, m, v)
    if token is not None:
        in_specs.append(_const_spec(token.shape))
        operands += (token,)
    return pl.pallas_call(
        body, grid=(r // tr,), name=name, in_specs=in_specs, out_specs=[spec] * 4,
        out_shape=[jax.ShapeDtypeStruct((r, c), F32)] * 4,
        compiler_params=_cparams(),
    )(*operands)


def _adamw_update(tok_ref, w_ref, p_ref, m_ref, v_ref, g_ref, d_ref, nm_ref, nv_ref):
    gt = p_ref[0].astype(F32)
    for k in range(1, p_ref.shape[0]):
        gt = gt + p_ref[k].astype(F32)
    if tok_ref is not None:
        gt = gt + tok_ref[0:1, 0:1]
    _adamw_apply(gt, w_ref, m_ref, v_ref, g_ref, d_ref, nm_ref, nv_ref)


def _adamw_own(w, land, own, chip, m, v, name, row_block=0, token=None):
    r, c = w.shape
    tr = _pick_tile(r, (256, 352, 128))
    off = row_block * (r // tr)

    def body(chip_ref, w_ref, land_ref, own_ref, m_ref, v_ref, *rest):
        mine = own_ref[0].astype(F32)
        gt = jnp.where(chip_ref[0] == 0, mine, land_ref[0].astype(F32))
        for k in range(1, N_CHIP):
            gt = gt + jnp.where(chip_ref[0] == k, mine, land_ref[k].astype(F32))
        if token is not None:
            gt = gt + rest[0][0:1, 0:1]
        _adamw_apply(gt, w_ref, m_ref, v_ref, *rest[-4:])

    spec = pl.BlockSpec((tr, c), lambda i, chip_ref: (i, 0))
    in_specs = [spec, pl.BlockSpec((N_CHIP, tr, c), lambda i, chip_ref: (0, i + off, 0)),
                pl.BlockSpec((1, tr, c), lambda i, chip_ref: (chip_ref[0], i + off, 0)), spec, spec]
    operands = (chip, w, land, own, m, v)
    if token is not None:
        in_specs.append(pl.BlockSpec(token.shape, lambda i, chip_ref: (0, 0)))
        operands += (token,)
    return pl.pallas_call(
        body, name=name,
        grid_spec=pltpu.PrefetchScalarGridSpec(
            num_scalar_prefetch=1, grid=(r // tr,), in_specs=in_specs, out_specs=[spec] * 4),
        out_shape=[jax.ShapeDtypeStruct((r, c), F32)] * 4,
        compiler_params=_cparams(),
    )(*operands)


def _adamw_apply(gt, w_ref, m_ref, v_ref, g_ref, d_ref, nm_ref, nv_ref):
    g_ref[...] = gt
    nm = ADAM_B1 * m_ref[...] + (1.0 - ADAM_B1) * gt
    nv = ADAM_B2 * v_ref[...] + (1.0 - ADAM_B2) * jnp.square(gt)
    m_hat = nm / (1.0 - ADAM_B1 ** ADAM_STEP)
    v_hat = nv / (1.0 - ADAM_B2 ** ADAM_STEP)
    d_ref[...] = -ADAM_LR * (m_hat / (jnp.sqrt(v_hat) + ADAM_EPS) + ADAM_WD * w_ref[...])
    nm_ref[...] = nm
    nv_ref[...] = nv


def _mesh_pos():
    return lax.axis_index("x"), lax.axis_index("y"), lax.axis_index("c")


def _no_round(in_refs, out_refs, sems):
    pass


def _run_exchange(comm, name):
    c_in, c_out = len(comm.inputs), len(comm.out_shapes)

    def body(*refs):
        cins, couts, sems = refs[:c_in], refs[c_in:c_in + c_out], refs[c_in + c_out:]
        comm.start(cins, couts, sems)
        comm.middle(cins, couts, sems)
        comm.finish(cins, couts, sems)

    return list(pl.pallas_call(
        body, name=name, out_shape=list(comm.out_shapes),
        in_specs=[_ANY] * c_in, out_specs=[_ANY] * c_out, scratch_shapes=list(comm.sem_shapes),
    )(*comm.inputs))


def _gather_exchange(shards):
    n_arr = len(shards)

    def plan(x_refs, out_refs, sems):
        send_sems, recv_sems, local_sems = sems
        x, y, c = _mesh_pos()
        me, sibling = (x, y, c), (x, y, 1 - c)
        xn, yn, diag = (1 - x, y), (x, 1 - y), (1 - x, 1 - y)

        def slot(a, px, py, pc, half=None):
            ref = out_refs[a].at[4 * px + 2 * py + pc]
            if half is None:
                return ref
            rows = shards[a].shape[0] // 2
            return ref.at[half * rows:(half + 1) * rows]

        def copy(a, k, block, to, half=None, src=None):
            dst = slot(a, *block, half)
            return pltpu.make_async_remote_copy(
                src_ref=dst if src is None else src, dst_ref=dst,
                send_sem=send_sems.at[a, k], recv_sem=recv_sems.at[a, k],
                device_id=to, device_id_type=pl.DeviceIdType.MESH)

        return types.SimpleNamespace(
            me=me, sibling=sibling, xn=xn, yn=yn, diag=diag, c=c, copy=copy,
            mine=[pltpu.make_async_copy(x_refs[a], slot(a, *me), local_sems.at[a]) for a in range(n_arr)],
            first=[cp for a in range(n_arr) for cp in (
                copy(a, 0, me, sibling, src=x_refs[a]), copy(a, 1, me, (*xn, c), src=x_refs[a]),
                copy(a, 2, me, (*yn, c), src=x_refs[a]))],
            second=lambda a: (copy(a, 3, (*xn, c), (*yn, c), half=0), copy(a, 5, (*xn, c), sibling),
                              copy(a, 4, (*yn, c), (*xn, c), half=1), copy(a, 6, (*yn, c), sibling)),
            third=lambda a: (copy(a, 7, (*diag, c), sibling, half=0), copy(a, 8, (*diag, c), sibling, half=1)))

    def start(x_refs, out_refs, sems):
        p = plan(x_refs, out_refs, sems)
        for cp in p.mine + p.first:
            cp.start()

    def middle(x_refs, out_refs, sems):
        p = plan(x_refs, out_refs, sems)
        for a in range(n_arr):
            to_yn, x_to_sib, to_xn, y_to_sib = p.second(a)
            p.copy(a, 1, (*p.xn, p.c), p.me).wait_recv()
            to_yn.start()
            x_to_sib.start()
            p.copy(a, 2, (*p.yn, p.c), p.me).wait_recv()
            to_xn.start()
            y_to_sib.start()

    def finish(x_refs, out_refs, sems):
        p = plan(x_refs, out_refs, sems)
        for a in range(n_arr):
            half0_to_sib, half1_to_sib = p.third(a)
            p.copy(a, 3, (*p.diag, p.c), p.me, half=0).wait_recv()
            half0_to_sib.start()
            p.copy(a, 4, (*p.diag, p.c), p.me, half=1).wait_recv()
            half1_to_sib.start()
        other = 1 - p.c
        for a in range(n_arr):
            p.copy(a, 0, p.sibling, p.me).wait_recv()
            p.copy(a, 5, (*p.xn, other), p.me).wait_recv()
            p.copy(a, 6, (*p.yn, other), p.me).wait_recv()
            p.copy(a, 7, (*p.diag, other), p.me, half=0).wait_recv()
            p.copy(a, 8, (*p.diag, other), p.me, half=1).wait_recv()
        for cp in p.first:
            cp.wait_send()
        for a in range(n_arr):
            for cp in p.second(a) + p.third(a):
                cp.wait_send()
        for cp in p.mine:
            cp.wait()

    return types.SimpleNamespace(
        inputs=list(shards), start=start, middle=middle, finish=finish,
        out_shapes=[jax.ShapeDtypeStruct((N_DEV,) + s.shape, s.dtype) for s in shards],
        sem_shapes=[pltpu.SemaphoreType.DMA((n_arr, 9)), pltpu.SemaphoreType.DMA((n_arr, 9)),
                    pltpu.SemaphoreType.DMA((n_arr,))])


def _pair_exchange(grads):
    n_arr = len(grads)

    def plan(g_refs, land_refs, sems):
        send_sems, recv_sems = sems
        x, y, c = _mesh_pos()
        return [pltpu.make_async_remote_copy(
            src_ref=g_refs[a].at[2 * k + 1 - c], dst_ref=land_refs[a].at[k],
            send_sem=send_sems.at[a, k], recv_sem=recv_sems.at[a, k],
            device_id=(x, y, 1 - c), device_id_type=pl.DeviceIdType.MESH)
            for a in range(n_arr) for k in range(N_CHIP)]

    def start(g_refs, land_refs, sems):
        for cp in plan(g_refs, land_refs, sems):
            cp.start()

    def finish(g_refs, land_refs, sems):
        for cp in plan(g_refs, land_refs, sems):
            cp.wait()

    return types.SimpleNamespace(
        inputs=list(grads), start=start, middle=_no_round, finish=finish,
        out_shapes=[jax.ShapeDtypeStruct((N_CHIP,) + g.shape[1:], g.dtype) for g in grads],
        sem_shapes=[pltpu.SemaphoreType.DMA((n_arr, N_CHIP)), pltpu.SemaphoreType.DMA((n_arr, N_CHIP))])


def _chip_exchange(parts):
    n_arr = len(parts)

    def plan(p_refs, land_refs, sems):
        send_sems, recv_sems, local_sems = sems
        x, y, c = _mesh_pos()
        my_chip = 2 * x + y
        chips = [(1 - x, y), (x, 1 - y), (1 - x, 1 - y)]
        local = [pltpu.make_async_copy(p_refs[a].at[my_chip], land_refs[a].at[my_chip], local_sems.at[a])
                 for a in range(n_arr)]

        def copy(a, k, src_slot, dst_slot, px, py):
            return pltpu.make_async_remote_copy(
                src_ref=p_refs[a].at[src_slot], dst_ref=land_refs[a].at[dst_slot],
                send_sem=send_sems.at[a, k], recv_sem=recv_sems.at[a, k],
                device_id=(px, py, c), device_id_type=pl.DeviceIdType.MESH)

        sends = [copy(a, k, 2 * px + py, my_chip, px, py) for a in range(n_arr) for k, (px, py) in enumerate(chips)]
        arrivals = [copy(a, k, my_chip, 2 * px + py, px, py) for a in range(n_arr)
                    for k, (px, py) in enumerate(chips)]
        return local, sends, arrivals

    def start(p_refs, land_refs, sems):
        local, sends, _ = plan(p_refs, land_refs, sems)
        for cp in local + sends:
            cp.start()

    def finish(p_refs, land_refs, sems):
        local, sends, arrivals = plan(p_refs, land_refs, sems)
        for cp in arrivals:
            cp.wait_recv()
        for cp in sends:
            cp.wait_send()
        for cp in local:
            cp.wait()

    return types.SimpleNamespace(
        inputs=list(parts), start=start, middle=_no_round, finish=finish,
        out_shapes=[jax.ShapeDtypeStruct(p.shape, p.dtype) for p in parts],
        sem_shapes=[pltpu.SemaphoreType.DMA((n_arr, 3)), pltpu.SemaphoreType.DMA((n_arr, 3)),
                    pltpu.SemaphoreType.DMA((n_arr,))])


_HBM = pl.BlockSpec(memory_space=pltpu.HBM)
_SEM = pl.BlockSpec(memory_space=pltpu.SEMAPHORE)
_DATAFLOW = pltpu.SideEffectType.DATAFLOW_SIDE_EFFECTING


def _chip_copies(p_refs, land_refs, send_sems, recv_sems):
    x, y, c = _mesh_pos()
    my_chip = 2 * x + y
    chips = [(1 - x, y), (x, 1 - y), (1 - x, 1 - y)]
    return [pltpu.make_async_remote_copy(
        src_ref=p_refs[a].at[2 * px + py], dst_ref=land_refs[a].at[my_chip],
        send_sem=send_sems[3 * a + k], recv_sem=recv_sems[3 * a + k],
        device_id=(px, py, c), device_id_type=pl.DeviceIdType.MESH)
        for a in range(len(p_refs)) for k, (px, py) in enumerate(chips)]


def _chip_exchange_begin(parts, name):
    n_arr = len(parts)
    n_buf, n_copy = 2 * n_arr, 3 * n_arr
    lands = [lax.empty(p.shape, p.dtype) for p in parts]

    def body(*refs):
        p_refs, land_refs = refs[:n_arr], refs[n_arr:n_buf]
        send_sems, recv_sems, token = refs[n_buf:n_buf + n_copy], refs[n_buf + n_copy:n_buf + 2 * n_copy], refs[-1]
        for cp in _chip_copies(p_refs, land_refs, send_sems, recv_sems):
            cp.start()
        token[...] = jnp.zeros_like(token)

    bufs = list(parts) + list(lands)
    outs = pl.pallas_call(
        body, name=name,
        out_shape=(*[pltpu.SemaphoreType.DMA(())] * (2 * n_copy), *[pltpu.HBM(b.shape, b.dtype) for b in bufs],
                   jax.ShapeDtypeStruct((8, 128), F32)),
        in_specs=[_HBM] * n_buf,
        out_specs=(*[_SEM] * (2 * n_copy), *[_HBM] * n_buf, pl.BlockSpec(memory_space=pltpu.VMEM)),
        input_output_aliases={i: 2 * n_copy + i for i in range(n_buf)},
        compiler_params=pltpu.CompilerParams(has_side_effects=_DATAFLOW),
    )(*[pltpu.with_memory_space_constraint(b, pltpu.HBM) for b in bufs])
    sems = list(outs[:2 * n_copy])
    thru = list(outs[2 * n_copy:2 * n_copy + n_buf])
    return types.SimpleNamespace(send_sems=sems[:n_copy], recv_sems=sems[n_copy:], parts=thru[:n_arr],
                                 lands=thru[n_arr:], token=outs[-1])


def _chip_exchange_end(flight, after, name):
    send_sems, recv_sems, parts, lands = flight.send_sems, flight.recv_sems, flight.parts, flight.lands
    n_arr = len(parts)
    n_buf, n_copy = 2 * n_arr, 3 * n_arr

    def body(*refs):
        p_refs, land_refs = refs[:n_arr], refs[n_arr:n_buf]
        sems = refs[n_buf:n_buf + 2 * n_copy]
        for cp in _chip_copies(p_refs, land_refs, sems[:n_copy], sems[n_copy:]):
            cp.wait_send()
            cp.wait_recv()

    bufs = list(parts) + list(lands)
    outs = pl.pallas_call(
        body, name=name, out_shape=tuple(pltpu.HBM(b.shape, b.dtype) for b in bufs),
        in_specs=[_HBM] * n_buf + [_SEM] * (2 * n_copy) + [_ANY], out_specs=tuple([_HBM] * n_buf),
        input_output_aliases={i: i for i in range(n_buf)},
        compiler_params=pltpu.CompilerParams(has_side_effects=_DATAFLOW),
    )(*bufs, *send_sems, *recv_sems, after)
    return list(outs[:n_arr]), list(outs[n_arr:])


def _row_tile(r, cap=640):
    best = None
    for cand in range(16, min(r, cap) + 1, 16):
        if r % cand == 0:
            best = cand
    return best if best is not None else r


def _pair_sum(g, landed, core, name):
    _, r, c_dim = g.shape
    tr = _row_tile(r)

    def body(core_ref, mine_ref, theirs_ref, o_ref):
        o_ref[0] = (mine_ref[0].astype(F32) + theirs_ref[0].astype(F32)).astype(o_ref.dtype)

    return pl.pallas_call(
        body, name=name,
        grid_spec=pltpu.PrefetchScalarGridSpec(
            num_scalar_prefetch=1, grid=(N_CHIP, r // tr),
            in_specs=[pl.BlockSpec((1, tr, c_dim), lambda k, i, core_ref: (2 * k + core_ref[0], i, 0)),
                      pl.BlockSpec((1, tr, c_dim), lambda k, i, core_ref: (k, i, 0))],
            out_specs=pl.BlockSpec((1, tr, c_dim), lambda k, i, core_ref: (k, i, 0))),
        out_shape=jax.ShapeDtypeStruct((N_CHIP, r, c_dim), g.dtype),
        compiler_params=_cparams(2),
    )(core, g, landed)


def _sum_slots(parts, name):
    n, r, c_dim = parts.shape
    tr = _row_tile(r)

    def body(p_ref, o_ref):
        acc = p_ref[0].astype(F32)
        for k in range(1, n):
            acc = acc + p_ref[k].astype(F32)
        o_ref[...] = acc

    return pl.pallas_call(
        body, grid=(r // tr,), name=name,
        in_specs=[pl.BlockSpec((n, tr, c_dim), lambda i: (0, i, 0))],
        out_specs=_row_spec(tr, c_dim),
        out_shape=jax.ShapeDtypeStruct((r, c_dim), F32),
        compiler_params=_cparams(),
    )(parts)


GAINS = ("g_ffn1", "g_mix", "g_cross", "g_mem", "g_ffn2", "g_final")
SMALL = GAINS + ("b_gate", "conv_w")
SMALL_R = 16
WEIGHT_ORDER = ("g_ffn1", "w_ffn1_gu", "w_ffn1_down", "g_mix", "w_in", "b_gate", "conv_w", "w_conv_out",
                "w_attn_out", "w_o", "g_cross", "g_mem", "w_cq", "w_ckv", "w_co", "g_ffn2", "w_ffn2_gu",
                "w_ffn2_down", "g_final")
GU_NAMES = ("w_ffn1_gu", "w_ffn2_gu")


def _pack_small(vals, conv_rows):
    rows = [vals[n].reshape(1, D) for n in GAINS] + [vals["b_gate"].reshape(2, D), conv_rows.reshape(CONV_K, D)]
    used = len(GAINS) + 2 + CONV_K
    return jnp.concatenate(rows + [jnp.zeros((SMALL_R - used, D), F32)], axis=0)


def _unpack_small(buf):
    out = {n: buf[k] for k, n in enumerate(GAINS)}
    out["b_gate"] = buf[6:8].reshape(2 * D)
    out["conv_w"] = buf[8:8 + CONV_K]
    return out


def _exchange_shards(wts):
    out = {n: jnp.pad(wts[n].T.astype(BF16), ((0, FF_PAD - FF_BLK), (0, 0))) for n in GU_NAMES}
    for n in ("w_ckv", "w_in", "w_ffn1_down", "w_ffn2_down"):
        out[n] = wts[n].astype(BF16)
    out["mix"] = jnp.concatenate([wts[n].astype(BF16) for n in MIX_MATS], axis=0)
    out["cross"] = jnp.concatenate([wts[n].astype(BF16) for n in CROSS_MATS], axis=0)
    return out


def _by_device(dw):
    return dw.reshape(N_DEV, SQ_ROWS, D)


def _reduce_group(grads, landed, core, names):
    return [_pair_sum(g, l, core, "grads_pair_sum_" + n) for g, l, n in zip(grads, landed, names)]


def _step(x, mem, target, sh, conv_pad, gains, b_gate, core):
    wg1, wd1, conv_all = _run_exchange(_gather_exchange([sh["w_ffn1_gu"], sh["w_ffn1_down"], conv_pad]), "gather_ffn1")
    conv_w = conv_all[:, :CONV_K, :].transpose(1, 0, 2).reshape(CONV_K, D)
    (n1, gate1, up1, act1, h1), (w_in,) = _ffn_fwd(
        x, gains["g_ffn1"], wg1, wd1, "ffn1_fwd", comm=_gather_exchange([sh["w_in"]]))
    (u, pcg, qkv), (w_mix,) = _inproj_fwd(h1, gains["g_mix"], w_in, "inproj_fwd", comm=_gather_exchange([sh["mix"]]))
    yc = _conv_fwd(pcg, conv_w, "conv_fwd")
    (ysb, ctot), (w_cross, w_ckv, wg2, wd2) = _sb_fwd(
        qkv, "sb_fwd", comm=_gather_exchange([sh["cross"], sh["w_ckv"], sh["w_ffn2_gu"], sh["w_ffn2_down"]]))
    a_mix, b_mix, merged, h2 = _mix_out_fwd(yc, ysb, pcg, b_gate, h1, w_mix, "mix_out_fwd")
    mn, kv = _memkv_fwd(mem, gains["g_mem"], w_ckv, "memkv_fwd")
    hn, qx, o_x, h3 = _cross_fwd(h2, gains["g_cross"], kv, w_cross, "cross_fwd")
    (n4, gate2, up2, act2, h4), _ = _ffn_fwd(h3, gains["g_ffn2"], wg2, wd2, "ffn2_fwd")
    loss, dh4, dg_final = _loss_bwd(h4, gains["g_final"], target, "loss_bwd")

    gs = {"g_final": dg_final}
    (dgu2, dh4b, dh3, gs["g_ffn2"]), _ = _ffn_bwd(dh4, h3, gains["g_ffn2"], gate2, up2, wg2, wd2, "ffn2_bwd")
    grads_a = [_mm_tn_rows(dgu2, n4, FF_PAD, "dw_ffn2_gu"),
               _mm_tn_rows(act2, dh4b, FF_BLK, "dw_ffn2_down").reshape(N_DEV, DOWN_ROWS, D)]
    names_a = ["w_ffn2_gu", "w_ffn2_down"]
    (dh3b, dqx, dkv, dh2, gs["g_cross"]), landed_a = _cross_bwd(
        dh3, h2, gains["g_cross"], qx, kv, w_cross, "cross_bwd", comm=_pair_exchange(grads_a))
    sums_a = _reduce_group(grads_a, landed_a, core, names_a)
    grads_b = [_mm_tn_cols(mn, dkv, "dw_ckv"),
               jnp.concatenate([_by_device(_mm_tn(hn, dqx, "dw_cq")), _by_device(_mm_tn(o_x, dh3b, "dw_co"))], axis=1)]
    names_b = ["w_ckv", "cross"]
    gs["g_mem"] = _memkv_bwd(dkv, mem, gains["g_mem"], w_ckv, "memkv_bwd")
    (dh2b, da_mix, db_mix, dgp, dyc, dysb, gs["b_gate"]), landed_b = _mix_out_bwd(
        dh2, a_mix, b_mix, pcg, b_gate, w_mix, "mix_out_bwd", comm=_pair_exchange(grads_b))
    sums_b = _reduce_group(grads_b, landed_b, core, names_b)
    grads_c = [jnp.concatenate([_by_device(_mm_tn(yc, da_mix, "dw_conv_out")),
                                _by_device(_mm_tn(ysb, db_mix, "dw_attn_out")),
                                _by_device(_mm_tn(merged, dh2b, "dw_o"))], axis=1)]
    landed_c = _run_exchange(_pair_exchange(grads_c), "grads_to_sibling_mix")
    sums_c = _reduce_group(grads_c, landed_c, core, ["mix"])
    flight_abc = _chip_exchange_begin(sums_a + sums_b + sums_c, "grads_to_chips_early_begin")
    (dq, dkv_sb), _ = _sb_bwd(qkv, dysb, ctot + flight_abc.token[0, 0], "sb_bwd")
    dconv, gs["conv_w"] = _conv_bwd(pcg, conv_w, dyc, "conv_bwd")
    grads_d = [jnp.concatenate(
        [_mm_tn_cols(u, dconv, "dw_in_conv"), _mm_tn(u, dq, "dw_in_q")[None], _mm_tn_cols(u, dkv_sb, "dw_in_kv"),
         _mm_tn_cols(u, dgp, "dw_in_gates")], axis=0)]
    (dh1, gs["g_mix"]), landed_d = _inproj_bwd(dconv, dq, dkv_sb, dgp, w_in, h1, gains["g_mix"], dh2, "inproj_bwd",
                                               comm=_pair_exchange(grads_d))
    sums_d = _reduce_group(grads_d, landed_d, core, ["w_in"])
    flight_d = _chip_exchange_begin(sums_d, "grads_to_chips_w_in_begin")
    (dgu1, dh1b, dx, gs["g_ffn1"]), _ = _ffn_bwd(dh1, x, gains["g_ffn1"] + flight_d.token[0, 0], gate1, up1, wg1, wd1,
                                                 "ffn1_bwd")
    grads_e = [_mm_tn_rows(dgu1, n1, FF_PAD, "dw_ffn1_gu"),
               _mm_tn_rows(act1, dh1b, FF_BLK, "dw_ffn1_down").reshape(N_DEV, DOWN_ROWS, D)]
    names_e = ["w_ffn1_gu", "w_ffn1_down"]
    landed_e = _run_exchange(_pair_exchange(grads_e), "grads_to_sibling_ffn1")
    flight_e = _chip_exchange_begin(_reduce_group(grads_e, landed_e, core, names_e), "grads_to_chips_ffn1_begin")
    flights = [(names_a + names_b + ["mix"], flight_abc), (["w_in"], flight_d), (names_e, flight_e)]
    return loss, dx, flights, gs


def kernel(x, mem, g_ffn1, w_ffn1_gu, w_ffn1_down, g_mix, w_in, b_gate, conv_w, w_conv_out, w_attn_out, w_o, g_cross, g_mem, w_cq, w_ckv, w_co, g_ffn2, w_ffn2_gu, w_ffn2_down, g_final, loss_target, m_g_ffn1, m_w_ffn1_gu, m_w_ffn1_down, m_g_mix, m_w_in, m_b_gate, m_conv_w, m_w_conv_out, m_w_attn_out, m_w_o, m_g_cross, m_g_mem, m_w_cq, m_w_ckv, m_w_co, m_g_ffn2, m_w_ffn2_gu, m_w_ffn2_down, m_g_final, v_g_ffn1, v_w_ffn1_gu, v_w_ffn1_down, v_g_mix, v_w_in, v_b_gate, v_conv_w, v_w_conv_out, v_w_attn_out, v_w_o, v_g_cross, v_g_mem, v_w_cq, v_w_ckv, v_w_co, v_g_ffn2, v_w_ffn2_gu, v_w_ffn2_down, v_g_final):
    args = locals()
    wts = {n: args[n] for n in WEIGHT_ORDER}
    mom1 = {n: args["m_" + n] for n in WEIGHT_ORDER}
    mom2 = {n: args["v_" + n] for n in WEIGHT_ORDER}
    cx, cy, cc = _mesh_pos()
    dev = 4 * cx + 2 * cy + cc
    conv_cols = D // N_DEV

    conv_pad = jnp.concatenate([conv_w, jnp.zeros((SMALL_R - CONV_K, conv_cols), F32)], axis=0)
    gains = {n: wts[n].reshape(1, D) for n in GAINS}
    loss8, dx, flights, gs = _step(x[0], mem[0], loss_target[0], _exchange_shards(wts), conv_pad, gains,
                                 b_gate.reshape(1, 2 * D), cc.reshape(1).astype(jnp.int32))
    loss = lax.psum(loss8[0, 0], MESH_AXES)

    grads, delta, new_m, new_v = {}, {}, {}, {}

    def operands(n, transposed):
        trio = (wts[n], mom1[n], mom2[n])
        return tuple(a.T for a in trio) if transposed else trio

    def record(n, res, transposed):
        grads[n], delta[n], new_m[n], new_v[n] = [r.T for r in res] if transposed else res

    early = [("w_ffn2_gu", "w_ffn2_gu", 0, True), ("w_ffn2_down", "w_ffn2_down", 0, False),
             ("w_ckv", "w_ckv", 0, False), ("w_in", "w_in", 0, False)]
    early += [(n, "mix", k, False) for k, n in enumerate(MIX_MATS)]
    early += [(n, "cross", k, False) for k, n in enumerate(CROSS_MATS)]
    chip = (2 * cx + cy).reshape(1).astype(jnp.int32)
    (names_early, flight_early), (names_w_in, flight_w_in), (last_names, flight_last) = flights
    token = flight_last.token
    own, land = {}, {}
    for names, flight, tag in ((names_early, flight_early, "early"), (names_w_in, flight_w_in, "w_in")):
        own_parts, landed = _chip_exchange_end(flight, token, "grads_to_chips_%s_end" % tag)
        own.update(zip(names, own_parts))
        land.update(zip(names, landed))
    for n, buf, row_block, transposed in early:
        w, m1, m2 = operands(n, transposed)
        record(n, _adamw_own(w, land[buf], own[buf], chip, m1, m2, "adamw_" + n, row_block, token), transposed)

    gs_rows = {n: gs[n] for n in GAINS + ("b_gate",)}
    small_mine = _pack_small(gs_rows, gs["conv_w"][:CONV_K]) + token[0, 0]
    small_all = _run_exchange(_gather_exchange([small_mine]), "gather_small_grads")[0]
    grad_small = _unpack_small(_sum_slots(small_all, "small_grads_sum"))
    grad_small["conv_w"] = lax.dynamic_slice_in_dim(grad_small["conv_w"], dev * conv_cols, conv_cols, axis=1)
    grads.update(grad_small)

    after = jnp.concatenate([new_v[n][:1, :1] for n, _, _, _ in early] + [grad_small["g_ffn1"][:1].reshape(1, 1)],
                            axis=0)
    own_parts, landed = _chip_exchange_end(flight_last, after, "grads_to_chips_ffn1_end")
    for n, own_n, land_n, transposed in zip(last_names, own_parts, landed, (True, False)):
        w, m1, m2 = operands(n, transposed)
        record(n, _adamw_own(w, land_n, own_n, chip, m1, m2, "adamw_" + n), transposed)

    def small_buf(vals):
        return _pack_small(vals, jnp.concatenate([vals["conv_w"], jnp.zeros((CONV_K, D - conv_cols), F32)], axis=1))

    _, d_s, m_s, v_s = _adamw(small_buf(wts), small_buf(grads)[None], small_buf(mom1), small_buf(mom2), "adamw_small")
    for res, buf in ((delta, d_s), (new_m, m_s), (new_v, v_s)):
        un = _unpack_small(buf)
        for n in GAINS + ("b_gate",):
            res[n] = un[n]
        res["conv_w"] = un["conv_w"][:, :conv_cols]

    return (loss, dx[None], *[grads[n] for n in WEIGHT_ORDER], *[delta[n] for n in WEIGHT_ORDER],
            *[new_m[n] for n in WEIGHT_ORDER], *[new_v[n] for n in WEIGHT_ORDER])
```

```python
import types

import jax
import jax.numpy as jnp
from jax import lax
from jax.experimental import pallas as pl
from jax.experimental.pallas import tpu as pltpu

F32 = jnp.float32
BF16 = jnp.bfloat16

D = 1024
DFF = 2816
SB_H = 8
SB_DH = 128
X_H = 4
X_DH = 256
CONV_K = 3
RMS_EPS = 1e-6
N_DEV = 8
N_CHIP = 4
SQ_ROWS = D // N_DEV

ADAM_LR = 0.001
ADAM_B1 = 0.9
ADAM_B2 = 0.999
ADAM_EPS = 1e-08
ADAM_WD = 0.01
ADAM_STEP = 10

TM = 256
TQ = 512
TK = 256
SB_HPS = 2
VMEM_LIMIT = 56 << 20

FF_BLK = DFF // 4
FF_PAD = 768
FF_SUB = 256
DOWN_ROWS = DFF // N_DEV

MIX_MATS = ("w_conv_out", "w_attn_out", "w_o")
CROSS_MATS = ("w_cq", "w_co")

MESH_AXES = ("x", "y", "c")
_ANY = pl.BlockSpec(memory_space=pl.ANY)


def _cparams(n_axes=1):
    return pltpu.CompilerParams(
        dimension_semantics=("arbitrary",) * n_axes, vmem_limit_bytes=VMEM_LIMIT)


def _row_spec(tm, n):
    return pl.BlockSpec((tm, n), lambda i: (i, 0))


def _blk_row_spec(nb, tm, n):
    return pl.BlockSpec((nb, tm, n), lambda i: (0, i, 0))


def _const_spec(shape):
    zeros = (0,) * len(shape)
    return pl.BlockSpec(shape, lambda i: zeros)


def _dot(a, b):
    return jnp.dot(a, b, preferred_element_type=F32)


def _dot_nt(a, b):
    return lax.dot_general(a, b, (((1,), (1,)), ((), ())), preferred_element_type=F32)


def _dot_tn(a, b):
    return lax.dot_general(a, b, (((0,), (0,)), ((), ())), preferred_element_type=F32)


def _sigmoid(x):
    return 1.0 / (1.0 + jnp.exp(-x))


def _call(body, operands, *, grid, in_specs, out_specs, out_shape, scratch_shapes, name, comm=None):
    n_in, n_out, n_sc = len(in_specs), len(out_specs), len(scratch_shapes)
    if comm is None:
        outs = pl.pallas_call(
            body, grid=grid, name=name, in_specs=in_specs, out_specs=out_specs, out_shape=out_shape,
            scratch_shapes=scratch_shapes, compiler_params=_cparams(len(grid)))(*operands)
        return list(outs), []
    c_in, c_out, c_sem = len(comm.inputs), len(comm.out_shapes), len(comm.sem_shapes)

    def hosted(*refs):
        bounds = [0, n_in, c_in, n_out, c_out, n_sc, c_sem]
        parts, pos = [], 0
        for k in bounds[1:]:
            parts.append(refs[pos:pos + k])
            pos += k
        ins, cins, outs, couts, scr, sems = parts
        step, n_steps = pl.program_id(0), grid[0]
        for ax in range(1, len(grid)):
            step, n_steps = step * grid[ax] + pl.program_id(ax), n_steps * grid[ax]

        @pl.when(step == 0)
        def _():
            comm.start(cins, couts, sems)

        @pl.when(step == (2 * n_steps) // 3)
        def _():
            comm.middle(cins, couts, sems)

        body(*ins, *outs, *scr)

        @pl.when(step == n_steps - 1)
        def _():
            comm.finish(cins, couts, sems)

    res = pl.pallas_call(
        hosted, grid=grid, name=name, in_specs=list(in_specs) + [_ANY] * c_in,
        out_specs=list(out_specs) + [_ANY] * c_out, out_shape=list(out_shape) + list(comm.out_shapes),
        scratch_shapes=list(scratch_shapes) + list(comm.sem_shapes),
        compiler_params=_cparams(len(grid)))(*operands, *comm.inputs)
    return list(res[:n_out]), list(res[n_out:])


def _load_resident(step, pairs, sems):
    @pl.when(step == 0)
    def _():
        copies = [pltpu.make_async_copy(src, dst, sems.at[k]) for k, (src, dst) in enumerate(pairs)]
        for cp in copies:
            cp.start()
        for cp in copies:
            cp.wait()


def _square_pairs(buf_hbm, index, dst):
    off = index * SQ_ROWS
    return [(buf_hbm.at[d, off:off + SQ_ROWS, :], dst.at[d * SQ_ROWS:(d + 1) * SQ_ROWS, :]) for d in range(N_DEV)]


def _down_pairs(wd_hbm, dst):
    return [(wd_hbm.at[d], dst.at[d // 2, (d % 2) * DOWN_ROWS:(d % 2 + 1) * DOWN_ROWS, :]) for d in range(N_DEV)]


def _zero_down_pad(step, dst):
    @pl.when(step == 0)
    def _():
        dst[:, FF_BLK:, :] = jnp.zeros((4, FF_PAD - FF_BLK, D), BF16)


def _rms_fwd_tile(xt, g):
    r = lax.rsqrt(jnp.mean(xt * xt, axis=-1, keepdims=True) + RMS_EPS)
    return (xt * r) * g


def _rms_bwd_tile(xt, g, dn):
    r = lax.rsqrt(jnp.mean(xt * xt, axis=-1, keepdims=True) + RMS_EPS)
    xhat = xt * r
    dxhat = dn * g
    dx = r * (dxhat - xhat * jnp.mean(dxhat * xhat, axis=-1, keepdims=True))
    dg = jnp.sum(dn * xhat, axis=0, keepdims=True)
    return dx, dg


def _accumulate(ref, step, value):
    @pl.when(step == 0)
    def _():
        ref[...] = value

    @pl.when(step != 0)
    def _():
        ref[...] = ref[...] + value


def _ffn_fwd(x, g, wgu, wd, name, comm=None):
    t = x.shape[0]

    def body(x_ref, g_ref, wgu_hbm, wd_hbm, n_ref, gate_ref, up_ref, act_ref, h_ref, wgu_v, wd_v, sems):
        step = pl.program_id(0)
        _zero_down_pad(step, wd_v)
        _load_resident(step, [(wgu_hbm, wgu_v)] + _down_pairs(wd_hbm, wd_v), sems)
        xt = x_ref[...]
        n = _rms_fwd_tile(xt, g_ref[...]).astype(BF16)
        n_ref[...] = n
        acc = jnp.zeros((TM, D), F32)
        for j in range(4):
            for s in range(FF_PAD // FF_SUB):
                lo, hi = s * FF_SUB, (s + 1) * FF_SUB
                gt = _dot_nt(n, wgu_v[j, lo:hi, :])
                ut = _dot_nt(n, wgu_v[4 + j, lo:hi, :])
                gate_ref[j, :, lo:hi] = gt.astype(BF16)
                up_ref[j, :, lo:hi] = ut.astype(BF16)
                act_ref[j, :, lo:hi] = ((gt * _sigmoid(gt)) * ut).astype(BF16)
            acc = acc + _dot(act_ref[j], wd_v[j])
        h_ref[...] = xt + 0.5 * acc

    ff = jax.ShapeDtypeStruct((4, t, FF_PAD), BF16)
    return _call(
        body, (x, g, wgu, wd), grid=(t // TM,), name=name, comm=comm,
        in_specs=[_row_spec(TM, D), _const_spec((1, D)), _ANY, _ANY],
        out_specs=[_row_spec(TM, D)] + [_blk_row_spec(4, TM, FF_PAD)] * 3 + [_row_spec(TM, D)],
        out_shape=[jax.ShapeDtypeStruct((t, D), BF16), ff, ff, ff, jax.ShapeDtypeStruct((t, D), F32)],
        scratch_shapes=[pltpu.VMEM((N_DEV, FF_PAD, D), BF16), pltpu.VMEM((4, FF_PAD, D), BF16),
                        pltpu.SemaphoreType.DMA((1 + N_DEV,))])


def _ffn_bwd(dh, xin, g, gate, up, wgu, wd, name, comm=None):
    t = dh.shape[0]

    def body(dh_ref, x_ref, g_ref, gate_ref, up_ref, wgu_hbm, wd_hbm,
             dgu_ref, dhb_ref, dx_ref, dg_ref, wgu_v, wd_v, sems):
        step = pl.program_id(0)
        _zero_down_pad(step, wd_v)
        _load_resident(step, [(wgu_hbm, wgu_v)] + _down_pairs(wd_hbm, wd_v), sems)
        dht = dh_ref[...]
        dhb = (0.5 * dht).astype(BF16)
        dhb_ref[...] = dhb
        dn = jnp.zeros((TM, D), F32)
        for j in range(4):
            for s in range(FF_PAD // FF_SUB):
                lo, hi = s * FF_SUB, (s + 1) * FF_SUB
                da = _dot_nt(dhb, wd_v[j, lo:hi, :])
                gt = gate_ref[j, :, lo:hi].astype(F32)
                ut = up_ref[j, :, lo:hi].astype(F32)
                sg = _sigmoid(gt)
                dgt = (da * ut * (sg * (1.0 + gt * (1.0 - sg)))).astype(BF16)
                dut = (da * (gt * sg)).astype(BF16)
                dgu_ref[j, :, lo:hi] = dgt
                dgu_ref[4 + j, :, lo:hi] = dut
            dn = dn + _dot(dgu_ref[j], wgu_v[j]) + _dot(dgu_ref[4 + j], wgu_v[4 + j])
        dx, dg = _rms_bwd_tile(x_ref[...], g_ref[...], dn)
        dx_ref[...] = dht + dx
        _accumulate(dg_ref, step, dg)

    return _call(
        body, (dh, xin, g, gate, up, wgu, wd), grid=(t // TM,), name=name, comm=comm,
        in_specs=[_row_spec(TM, D), _row_spec(TM, D), _const_spec((1, D)), _blk_row_spec(4, TM, FF_PAD),
                  _blk_row_spec(4, TM, FF_PAD), _ANY, _ANY],
        out_specs=[_blk_row_spec(N_DEV, TM, FF_PAD), _row_spec(TM, D), _row_spec(TM, D), _const_spec((1, D))],
        out_shape=[jax.ShapeDtypeStruct((N_DEV, t, FF_PAD), BF16), jax.ShapeDtypeStruct((t, D), BF16),
                   jax.ShapeDtypeStruct((t, D), F32), jax.ShapeDtypeStruct((1, D), F32)],
        scratch_shapes=[pltpu.VMEM((N_DEV, FF_PAD, D), BF16), pltpu.VMEM((4, FF_PAD, D), BF16),
                        pltpu.SemaphoreType.DMA((1 + N_DEV,))])


WIDE_TILES = (1024, 512, 256, 128)


def _pick_tile(n, options=(512, 256, 128)):
    for o in options:
        if n % o == 0:
            return o
    return n


def _mm_tn(a, b, name):
    k, m = a.shape
    _, n = b.shape
    tm, tn = _pick_tile(m, WIDE_TILES), _pick_tile(n)

    def body(a_ref, b_ref, o_ref):
        o_ref[...] = _dot_tn(a_ref[...].astype(BF16), b_ref[...].astype(BF16)).astype(BF16)

    return pl.pallas_call(
        body, grid=(m // tm, n // tn), name=name,
        in_specs=[pl.BlockSpec((k, tm), lambda i, j: (0, i)), pl.BlockSpec((k, tn), lambda i, j: (0, j))],
        out_specs=pl.BlockSpec((tm, tn), lambda i, j: (i, j)),
        out_shape=jax.ShapeDtypeStruct((m, n), BF16),
        compiler_params=_cparams(2),
    )(a, b)


def _mm_tn_cols(a, b, name):
    k, m = a.shape
    nb, _, n = b.shape
    tm = _pick_tile(m, WIDE_TILES)

    def body(a_ref, b_ref, o_ref):
        o_ref[0] = _dot_tn(a_ref[...].astype(BF16), b_ref[0].astype(BF16)).astype(BF16)

    return pl.pallas_call(
        body, grid=(nb, m // tm), name=name,
        in_specs=[pl.BlockSpec((k, tm), lambda j, i: (0, i)), pl.BlockSpec((1, k, n), lambda j, i: (j, 0, 0))],
        out_specs=pl.BlockSpec((1, tm, n), lambda j, i: (j, i, 0)),
        out_shape=jax.ShapeDtypeStruct((nb, m, n), BF16),
        compiler_params=_cparams(2),
    )(a, b)


def _mm_tn_rows(a, b, keep, name):
    nb, k, m = a.shape
    _, n = b.shape
    tn = _pick_tile(n, WIDE_TILES)

    def body(a_ref, b_ref, o_ref):
        o_ref[0] = _dot_tn(a_ref[0], b_ref[...])[:keep].astype(BF16)

    return pl.pallas_call(
        body, grid=(nb, n // tn), name=name,
        in_specs=[pl.BlockSpec((1, k, m), lambda j, i: (j, 0, 0)), pl.BlockSpec((k, tn), lambda j, i: (0, i))],
        out_specs=pl.BlockSpec((1, keep, tn), lambda j, i: (j, 0, i)),
        out_shape=jax.ShapeDtypeStruct((nb, keep, n), BF16),
        compiler_params=_cparams(2),
    )(a, b)


PCG_W = 5 * D
QKV_W = 3 * D
PROJ_SUB = 512


def _inproj_fwd(h, g, w_in, name, comm=None):
    t = h.shape[0]

    def body(h_ref, g_ref, w_hbm, u_ref, pcg_ref, qkv_ref, w_v, sems):
        _load_resident(pl.program_id(0), [(w_hbm, w_v)], sems)
        u = _rms_fwd_tile(h_ref[...], g_ref[...]).astype(BF16)
        u_ref[...] = u
        for blk in range(N_DEV):
            for s in range(D // PROJ_SUB):
                lo, hi = s * PROJ_SUB, (s + 1) * PROJ_SUB
                p = _dot(u, w_v[blk, :, lo:hi])
                if blk < 3:
                    pcg_ref[:, blk * D + lo:blk * D + hi] = p
                elif blk < 6:
                    qkv_ref[:, (blk - 3) * D + lo:(blk - 3) * D + hi] = p.astype(BF16)
                else:
                    pcg_ref[:, (blk - 3) * D + lo:(blk - 3) * D + hi] = p

    return _call(
        body, (h, g, w_in), grid=(t // TM,), name=name, comm=comm,
        in_specs=[_row_spec(TM, D), _const_spec((1, D)), _ANY],
        out_specs=[_row_spec(TM, D), _row_spec(TM, PCG_W), _row_spec(TM, QKV_W)],
        out_shape=[jax.ShapeDtypeStruct((t, D), BF16), jax.ShapeDtypeStruct((t, PCG_W), F32),
                   jax.ShapeDtypeStruct((t, QKV_W), BF16)],
        scratch_shapes=[pltpu.VMEM((N_DEV, D, D), BF16), pltpu.SemaphoreType.DMA((1,))])


CONV_CW = 256


def _shift_down(v, k, rows):
    return jnp.where(rows >= k, pltpu.roll(v, k, 0), 0.0)


def _shift_up(v, k, rows, t):
    return jnp.where(rows < t - k, pltpu.roll(v, t - k, 0), 0.0)


def _col_spec(t, cw, off):
    return pl.BlockSpec((t, cw), lambda j: (0, j + off))


def _conv_fwd(pcg, conv_w, name):
    t = pcg.shape[0]
    nb = D // CONV_CW

    def body(cb_ref, cc_ref, cx_ref, w_ref, y_ref):
        rows = lax.broadcasted_iota(jnp.int32, (t, CONV_CW), 0)
        xc = cc_ref[...] * cx_ref[...]
        conv = (w_ref[0:1, :] * _shift_down(xc, 2, rows) + w_ref[1:2, :] * _shift_down(xc, 1, rows)
                + w_ref[2:3, :] * xc)
        y_ref[...] = (cb_ref[...] * conv).astype(BF16)

    return pl.pallas_call(
        body, grid=(nb,), name=name,
        in_specs=[_col_spec(t, CONV_CW, 0), _col_spec(t, CONV_CW, nb), _col_spec(t, CONV_CW, 2 * nb),
                  pl.BlockSpec((CONV_K, CONV_CW), lambda j: (0, j))],
        out_specs=_col_spec(t, CONV_CW, 0),
        out_shape=jax.ShapeDtypeStruct((t, D), BF16),
        compiler_params=_cparams(),
    )(pcg, pcg, pcg, conv_w)


def _conv_bwd(pcg, conv_w, dyc, name):
    t = pcg.shape[0]
    nb = D // CONV_CW

    def body(cb_ref, cc_ref, cx_ref, w_ref, dy_ref, dc_ref, dw_ref):
        rows = lax.broadcasted_iota(jnp.int32, (t, CONV_CW), 0)
        cc, cx = cc_ref[...], cx_ref[...]
        xc = cc * cx
        x1 = _shift_down(xc, 1, rows)
        x2 = _shift_down(xc, 2, rows)
        w0, w1, w2 = w_ref[0:1, :], w_ref[1:2, :], w_ref[2:3, :]
        conv = w0 * x2 + w1 * x1 + w2 * xc
        dy = dy_ref[...]
        dc_ref[0] = (dy * conv).astype(BF16)
        dconv = dy * cb_ref[...]
        dw_ref[...] = jnp.zeros((8, CONV_CW), F32)
        dw_ref[0:1, :] = jnp.sum(dconv * x2, axis=0, keepdims=True)
        dw_ref[1:2, :] = jnp.sum(dconv * x1, axis=0, keepdims=True)
        dw_ref[2:3, :] = jnp.sum(dconv * xc, axis=0, keepdims=True)
        dxc = w2 * dconv + w1 * _shift_up(dconv, 1, rows, t) + w0 * _shift_up(dconv, 2, rows, t)
        dc_ref[1] = (dxc * cx).astype(BF16)
        dc_ref[2] = (dxc * cc).astype(BF16)

    return pl.pallas_call(
        body, grid=(nb,), name=name,
        in_specs=[_col_spec(t, CONV_CW, 0), _col_spec(t, CONV_CW, nb), _col_spec(t, CONV_CW, 2 * nb),
                  pl.BlockSpec((CONV_K, CONV_CW), lambda j: (0, j)), _col_spec(t, CONV_CW, 0)],
        out_specs=[pl.BlockSpec((3, t, CONV_CW), lambda j: (0, 0, j)), pl.BlockSpec((8, CONV_CW), lambda j: (0, j))],
        out_shape=[jax.ShapeDtypeStruct((3, t, D), BF16), jax.ShapeDtypeStruct((8, D), F32)],
        compiler_params=_cparams(),
    )(pcg, pcg, pcg, conv_w, dyc)


def _tri2(cond):
    rr = lax.broadcasted_iota(jnp.int32, (2 * TK, TK), 0) & (TK - 1)
    cc = lax.broadcasted_iota(jnp.int32, (2 * TK, TK), 1)
    return cond(rr, cc).astype(BF16)


def _causal(shift, row0=0):
    rr = lax.broadcasted_iota(jnp.int32, (TQ - row0, TK), 0) + row0
    cc = lax.broadcasted_iota(jnp.int32, (TQ - row0, TK), 1)
    return cc + shift < rr


def _cumdot(v, tri2):
    hi = v.astype(BF16)
    lo = (v - hi.astype(F32)).astype(BF16)
    return _dot(jnp.concatenate([hi, lo], axis=1), tri2)


def _log_1m_beta(z):
    return -(jnp.maximum(z, 0.0) + jnp.log(1.0 + jnp.exp(-jnp.abs(z))))


def _sb_specs(t):
    g = SB_H // SB_HPS
    w = SB_HPS * SB_DH
    q_spec = pl.BlockSpec((TQ, w), lambda h, i: (i, h))
    k_spec = pl.BlockSpec((t, w), lambda h, i: (0, g + h))
    v_spec = pl.BlockSpec((t, w), lambda h, i: (0, 2 * g + h))
    ct_spec = pl.BlockSpec((SB_HPS, TQ, 1), lambda h, i: (h, i, 0))
    return g, w, q_spec, k_spec, v_spec, ct_spec


def _sb_fwd(qkv, name, comm=None):
    t = qkv.shape[0]
    scale = SB_DH ** -0.5
    g, w, q_spec, k_spec, v_spec, ct_spec = _sb_specs(t)

    def body(q_ref, k_ref, v_ref, y_ref, ct_ref):
        i = pl.program_id(1)
        later = _tri2(lambda j, s: j > s)
        n_diag = TQ // TK

        def block(j, carry, shift):
            off = pl.multiple_of(j * TK, TK)
            zs, ms = [], []
            for hd in range(SB_HPS):
                cols = slice(hd * SB_DH, (hd + 1) * SB_DH)
                z = _dot_nt(q_ref[:, cols], k_ref[pl.ds(off, TK), cols]) * scale
                m = _log_1m_beta(z)
                if shift is not None:
                    m = jnp.where(_causal(shift), m, 0.0)
                zs.append(z)
                ms.append(m)
            after = _cumdot(jnp.concatenate(ms, axis=0), later)
            out = []
            for hd in range(SB_HPS):
                acc, c_sum = carry[hd]
                cols = slice(hd * SB_DH, (hd + 1) * SB_DH)
                a = jnp.exp((ms[hd] + zs[hd]) + (c_sum + after[hd * TQ:(hd + 1) * TQ]))
                if shift is not None:
                    a = jnp.where(_causal(shift), a, 0.0)
                out.append((acc + _dot(a.astype(BF16), v_ref[pl.ds(off, TK), cols]),
                            c_sum + jnp.sum(ms[hd], axis=1, keepdims=True)))
            return tuple(out)

        carry = tuple((jnp.zeros((TQ, SB_DH), F32), jnp.zeros((TQ, 1), F32)) for _ in range(SB_HPS))
        for d in reversed(range(n_diag)):
            carry = block(i * n_diag + d, carry, d * TK)
        carry = lax.fori_loop(0, i * n_diag, lambda jj, c: block(i * n_diag - 1 - jj, c, None), carry)
        for hd in range(SB_HPS):
            y_ref[:, hd * SB_DH:(hd + 1) * SB_DH] = carry[hd][0].astype(BF16)
            ct_ref[hd] = carry[hd][1]

    return _call(
        body, (qkv, qkv, qkv), grid=(g, t // TQ), name=name, comm=comm,
        in_specs=[q_spec, k_spec, v_spec],
        out_specs=[q_spec, ct_spec],
        out_shape=[jax.ShapeDtypeStruct((t, D), BF16), jax.ShapeDtypeStruct((SB_H, t, 1), F32)],
        scratch_shapes=[])


def _sb_bwd(qkv, dy, ctot, name, comm=None):
    t = qkv.shape[0]
    scale = SB_DH ** -0.5
    g, w, q_spec, k_spec, v_spec, ct_spec = _sb_specs(t)
    acc_spec = pl.BlockSpec((2, t, w), lambda h, i: (0, 0, h))

    def body(q_ref, k_ref, v_ref, dy_ref, ct_ref, dq_ref, dkv_ref):
        i = pl.program_id(1)

        @pl.when(i == 0)
        def _():
            dkv_ref[...] = jnp.zeros_like(dkv_ref)

        upto = _tri2(lambda j, s: j <= s)
        n_diag = TQ // TK

        def block(j, carry, shift):
            off = pl.multiple_of(j * TK, TK)
            r0 = 0 if shift is None else shift
            nr = TQ - r0
            causal = None if shift is None else _causal(shift, r0)

            def grow(old, delta):
                return old + delta if r0 == 0 else jnp.concatenate([old[:r0], old[r0:] + delta], axis=0)

            zs, ms = [], []
            for hd in range(SB_HPS):
                cols = slice(hd * SB_DH, (hd + 1) * SB_DH)
                z = _dot_nt(q_ref[r0:, cols], k_ref[pl.ds(off, TK), cols]) * scale
                m = _log_1m_beta(z)
                if causal is not None:
                    m = jnp.where(causal, m, 0.0)
                zs.append(z)
                ms.append(m)
            m_upto = _cumdot(jnp.concatenate(ms, axis=0), upto)
            ls, a_s, es = [], [], []
            for hd in range(SB_HPS):
                cols = slice(hd * SB_DH, (hd + 1) * SB_DH)
                l = ms[hd] + zs[hd]
                a = jnp.exp(l + ((ct_ref[hd, r0:] - carry[hd][1][r0:]) - m_upto[hd * nr:(hd + 1) * nr]))
                if causal is not None:
                    a = jnp.where(causal, a, 0.0)
                ls.append(l)
                a_s.append(a)
                es.append(_dot_nt(dy_ref[r0:, cols], v_ref[pl.ds(off, TK), cols]) * a)
            e_upto = _dot(jnp.concatenate(es, axis=0).astype(BF16), upto[:TK])
            out = []
            for hd in range(SB_HPS):
                dq, p_sum, e_sum = carry[hd]
                cols = slice(hd * SB_DH, (hd + 1) * SB_DH)
                e = es[hd]
                dz = e - jnp.exp(ls[hd]) * (e_sum[r0:] + e_upto[hd * nr:(hd + 1) * nr])
                if causal is not None:
                    dz = jnp.where(causal, dz, 0.0)
                dzs = (dz * scale).astype(BF16)
                dkv_ref[0, pl.ds(off, TK), cols] += _dot_tn(dzs, q_ref[r0:, cols])
                dkv_ref[1, pl.ds(off, TK), cols] += _dot_tn(a_s[hd].astype(BF16), dy_ref[r0:, cols])
                out.append((grow(dq, _dot(dzs, k_ref[pl.ds(off, TK), cols])),
                            grow(p_sum, jnp.sum(ms[hd], axis=1, keepdims=True)),
                            grow(e_sum, jnp.sum(e, axis=1, keepdims=True))))
            return tuple(out)

        zero = jnp.zeros((TQ, 1), F32)
        init = tuple((jnp.zeros((TQ, SB_DH), F32), zero, zero) for _ in range(SB_HPS))
        carry = lax.fori_loop(0, i * n_diag, lambda j, c: block(j, c, None), init)
        for d in range(n_diag):
            carry = block(i * n_diag + d, carry, d * TK)
        for hd in range(SB_HPS):
            dq_ref[:, hd * SB_DH:(hd + 1) * SB_DH] = carry[hd][0].astype(BF16)

    return _call(
        body, (qkv, qkv, qkv, dy, ctot), grid=(g, t // TQ), name=name, comm=comm,
        in_specs=[q_spec, k_spec, v_spec, q_spec, ct_spec],
        out_specs=[q_spec, acc_spec],
        out_shape=[jax.ShapeDtypeStruct((t, D), BF16), jax.ShapeDtypeStruct((2, t, D), F32)],
        scratch_shapes=[])


def _gate_specs():
    return [pl.BlockSpec((TM, D), lambda i: (i, 3)), pl.BlockSpec((TM, D), lambda i: (i, 4))]


def _mix_pairs(mix_hbm, dsts):
    pairs = []
    for index, dst in enumerate(dsts):
        pairs += _square_pairs(mix_hbm, index, dst)
    return pairs


def _mix_out_fwd(yc, ysb, pcg, b_gate, h, w_mix, name):
    t = h.shape[0]

    def body(yc_ref, ysb_ref, gc_ref, gs_ref, b_ref, h_ref, mix_hbm,
             a_ref, b_out_ref, mg_ref, h2_ref, wc_v, wa_v, wo_v, sems):
        _load_resident(pl.program_id(0), _mix_pairs(mix_hbm, (wc_v, wa_v, wo_v)), sems)
        a = _dot(yc_ref[...], wc_v[...])
        b = _dot(ysb_ref[...], wa_v[...])
        merged = (_sigmoid(gc_ref[...] + b_ref[:, :D]) * a + _sigmoid(gs_ref[...] + b_ref[:, D:]) * b).astype(BF16)
        a_ref[...] = a
        b_out_ref[...] = b
        mg_ref[...] = merged
        h2_ref[...] = h_ref[...] + _dot(merged, wo_v[...])

    return pl.pallas_call(
        body, grid=(t // TM,), name=name,
        in_specs=[_row_spec(TM, D), _row_spec(TM, D)] + _gate_specs()
                 + [_const_spec((1, 2 * D)), _row_spec(TM, D), _ANY],
        out_specs=[_row_spec(TM, D)] * 4,
        out_shape=[jax.ShapeDtypeStruct((t, D), F32), jax.ShapeDtypeStruct((t, D), F32),
                   jax.ShapeDtypeStruct((t, D), BF16), jax.ShapeDtypeStruct((t, D), F32)],
        scratch_shapes=[pltpu.VMEM((D, D), BF16)] * 3 + [pltpu.SemaphoreType.DMA((3 * N_DEV,))],
        compiler_params=_cparams(),
    )(yc, ysb, pcg, pcg, b_gate, h, w_mix)


def _mix_out_bwd(dh2, a, b, pcg, b_gate, w_mix, name, comm=None):
    t = dh2.shape[0]

    def body(dh_ref, a_ref, b_ref, gc_ref, gs_ref, bias_ref, mix_hbm,
             dhb_ref, da_ref, db_ref, dgp_ref, dyc_ref, dysb_ref, dbias_ref, wc_v, wa_v, wo_v, sems):
        step = pl.program_id(0)
        _load_resident(step, _mix_pairs(mix_hbm, (wc_v, wa_v, wo_v)), sems)
        dhb = dh_ref[...].astype(BF16)
        dhb_ref[...] = dhb
        dm = _dot_nt(dhb, wo_v[...])
        gc = _sigmoid(gc_ref[...] + bias_ref[:, :D])
        gs = _sigmoid(gs_ref[...] + bias_ref[:, D:])
        da = (dm * gc).astype(BF16)
        db = (dm * gs).astype(BF16)
        da_ref[...] = da
        db_ref[...] = db
        dgc = dm * a_ref[...] * (gc * (1.0 - gc))
        dgs = dm * b_ref[...] * (gs * (1.0 - gs))
        dgp_ref[0] = dgc.astype(BF16)
        dgp_ref[1] = dgs.astype(BF16)
        _accumulate(dbias_ref.at[:, :D], step, jnp.sum(dgc, axis=0, keepdims=True))
        _accumulate(dbias_ref.at[:, D:], step, jnp.sum(dgs, axis=0, keepdims=True))
        dyc_ref[...] = _dot_nt(da, wc_v[...])
        dysb_ref[...] = _dot_nt(db, wa_v[...]).astype(BF16)

    return _call(
        body, (dh2, a, b, pcg, pcg, b_gate, w_mix), grid=(t // TM,), name=name, comm=comm,
        in_specs=[_row_spec(TM, D)] * 3 + _gate_specs() + [_const_spec((1, 2 * D)), _ANY],
        out_specs=[_row_spec(TM, D)] * 3 + [_blk_row_spec(2, TM, D), _row_spec(TM, D), _row_spec(TM, D),
                                            _const_spec((1, 2 * D))],
        out_shape=[jax.ShapeDtypeStruct((t, D), BF16)] * 3
                  + [jax.ShapeDtypeStruct((2, t, D), BF16), jax.ShapeDtypeStruct((t, D), F32),
                     jax.ShapeDtypeStruct((t, D), BF16), jax.ShapeDtypeStruct((1, 2 * D), F32)],
        scratch_shapes=[pltpu.VMEM((D, D), BF16)] * 3 + [pltpu.SemaphoreType.DMA((3 * N_DEV,))])


def _inproj_bwd(dconv, dq, dkv, dgp, w_in, h, g, dh_res, name, comm=None):
    t = h.shape[0]

    def body(dc_ref, dq_ref, dkv_ref, dgp_ref, w_hbm, h_ref, g_ref, dres_ref, dh_ref, dg_ref, w_v, sems):
        step = pl.program_id(0)
        _load_resident(step, [(w_hbm, w_v)], sems)
        du = _dot_nt(dq_ref[...], w_v[3])
        for k in range(3):
            du = du + _dot_nt(dc_ref[k], w_v[k])
        for k in range(2):
            du = du + _dot_nt(dkv_ref[k].astype(BF16), w_v[4 + k]) + _dot_nt(dgp_ref[k], w_v[6 + k])
        dx, dg = _rms_bwd_tile(h_ref[...], g_ref[...], du)
        dh_ref[...] = dres_ref[...] + dx
        _accumulate(dg_ref, step, dg)

    return _call(
        body, (dconv, dq, dkv, dgp, w_in, h, g, dh_res), grid=(t // TM,), name=name, comm=comm,
        in_specs=[_blk_row_spec(3, TM, D), _row_spec(TM, D), _blk_row_spec(2, TM, D), _blk_row_spec(2, TM, D), _ANY,
                  _row_spec(TM, D), _const_spec((1, D)), _row_spec(TM, D)],
        out_specs=[_row_spec(TM, D), _const_spec((1, D))],
        out_shape=[jax.ShapeDtypeStruct((t, D), F32), jax.ShapeDtypeStruct((1, D), F32)],
        scratch_shapes=[pltpu.VMEM((N_DEV, D, D), BF16), pltpu.SemaphoreType.DMA((1,))])


def _memkv_fwd(mem, g, w_ckv, name):
    m = mem.shape[0]

    def body(mem_ref, g_ref, w_ref, mn_ref, kv_ref):
        mn = _rms_fwd_tile(mem_ref[...], g_ref[...]).astype(BF16)
        mn_ref[...] = mn
        for j in range(N_DEV):
            kv_ref[j] = _dot(mn, w_ref[j]).astype(BF16)

    return pl.pallas_call(
        body, grid=(1,), name=name,
        in_specs=[_const_spec((m, D)), _const_spec((1, D)), _const_spec((N_DEV, D, X_DH))],
        out_specs=[_const_spec((m, D)), _const_spec((N_DEV, m, X_DH))],
        out_shape=[jax.ShapeDtypeStruct((m, D), BF16), jax.ShapeDtypeStruct((N_DEV, m, X_DH), BF16)],
        compiler_params=_cparams(),
    )(mem, g, w_ckv)


def _memkv_bwd(dkv, mem, g, w_ckv, name):
    m = mem.shape[0]

    def body(dkv_ref, mem_ref, g_ref, w_ref, dg_ref):
        dmn = jnp.zeros((m, D), F32)
        for j in range(N_DEV):
            dmn = dmn + _dot_nt(dkv_ref[j].astype(BF16), w_ref[j])
        _, dg = _rms_bwd_tile(mem_ref[...], g_ref[...], dmn)
        dg_ref[...] = dg

    return pl.pallas_call(
        body, grid=(1,), name=name,
        in_specs=[_const_spec((N_DEV, m, X_DH)), _const_spec((m, D)), _const_spec((1, D)),
                  _const_spec((N_DEV, D, X_DH))],
        out_specs=_const_spec((1, D)),
        out_shape=jax.ShapeDtypeStruct((1, D), F32),
        compiler_params=_cparams(),
    )(dkv, mem, g, w_ckv)


def _softmax_rows(s):
    e = jnp.exp(s - jnp.max(s, axis=-1, keepdims=True))
    return e / jnp.sum(e, axis=-1, keepdims=True)


def _cross_pairs(cross_hbm, wq_v, wo_v):
    return _square_pairs(cross_hbm, 0, wq_v) + _square_pairs(cross_hbm, 1, wo_v)


def _cross_fwd(h, g, kv, w_cross, name):
    t = h.shape[0]
    m = kv.shape[1]
    scale = X_DH ** -0.5

    def body(h_ref, g_ref, kv_ref, cross_hbm, hn_ref, qx_ref, o_ref, h3_ref, wq_v, wo_v, sems):
        _load_resident(pl.program_id(0), _cross_pairs(cross_hbm, wq_v, wo_v), sems)
        ht = h_ref[...]
        hn = _rms_fwd_tile(ht, g_ref[...]).astype(BF16)
        hn_ref[...] = hn
        qx = _dot(hn, wq_v[...]).astype(BF16)
        qx_ref[...] = qx
        for hd in range(X_H):
            lo, hi = hd * X_DH, (hd + 1) * X_DH
            p = _softmax_rows(_dot_nt(qx[:, lo:hi], kv_ref[hd]) * scale)
            o_ref[:, lo:hi] = _dot(p.astype(BF16), kv_ref[X_H + hd]).astype(BF16)
        h3_ref[...] = ht + _dot(o_ref[...], wo_v[...])

    return pl.pallas_call(
        body, grid=(t // TM,), name=name,
        in_specs=[_row_spec(TM, D), _const_spec((1, D)), _const_spec((N_DEV, m, X_DH)), _ANY],
        out_specs=[_row_spec(TM, D)] * 4,
        out_shape=[jax.ShapeDtypeStruct((t, D), BF16)] * 3 + [jax.ShapeDtypeStruct((t, D), F32)],
        scratch_shapes=[pltpu.VMEM((D, D), BF16)] * 2 + [pltpu.SemaphoreType.DMA((2 * N_DEV,))],
        compiler_params=_cparams(),
    )(h, g, kv, w_cross)


def _cross_bwd(dh3, h, g, qx, kv, w_cross, name, comm=None):
    t = h.shape[0]
    m = kv.shape[1]
    scale = X_DH ** -0.5

    def body(dh_ref, h_ref, g_ref, qx_ref, kv_ref, cross_hbm,
             dhb_ref, dqx_ref, dkv_ref, dh2_ref, dg_ref, wq_v, wo_v, sems):
        step = pl.program_id(0)
        _load_resident(step, _cross_pairs(cross_hbm, wq_v, wo_v), sems)

        @pl.when(step == 0)
        def _():
            dkv_ref[...] = jnp.zeros_like(dkv_ref)

        dht = dh_ref[...]
        dhb = dht.astype(BF16)
        dhb_ref[...] = dhb
        do = _dot_nt(dhb, wo_v[...]).astype(BF16)
        for hd in range(X_H):
            lo, hi = hd * X_DH, (hd + 1) * X_DH
            qh = qx_ref[:, lo:hi]
            kh = kv_ref[hd]
            p = _softmax_rows(_dot_nt(qh, kh) * scale)
            doh = do[:, lo:hi]
            dp = _dot_nt(doh, kv_ref[X_H + hd])
            ds = (p * (dp - jnp.sum(dp * p, axis=-1, keepdims=True)) * scale).astype(BF16)
            dqx_ref[:, lo:hi] = _dot(ds, kh).astype(BF16)
            dkv_ref[hd] += _dot_tn(ds, qh)
            dkv_ref[X_H + hd] += _dot_tn(p.astype(BF16), doh)
        dhn = _dot_nt(dqx_ref[...], wq_v[...])
        dx, dg = _rms_bwd_tile(h_ref[...], g_ref[...], dhn)
        dh2_ref[...] = dht + dx
        _accumulate(dg_ref, step, dg)

    return _call(
        body, (dh3, h, g, qx, kv, w_cross), grid=(t // TM,), name=name, comm=comm,
        in_specs=[_row_spec(TM, D), _row_spec(TM, D), _const_spec((1, D)), _row_spec(TM, D),
                  _const_spec((N_DEV, m, X_DH)), _ANY],
        out_specs=[_row_spec(TM, D), _row_spec(TM, D), _const_spec((N_DEV, m, X_DH)), _row_spec(TM, D),
                   _const_spec((1, D))],
        out_shape=[jax.ShapeDtypeStruct((t, D), BF16), jax.ShapeDtypeStruct((t, D), BF16),
                   jax.ShapeDtypeStruct((N_DEV, m, X_DH), F32), jax.ShapeDtypeStruct((t, D), F32),
                   jax.ShapeDtypeStruct((1, D), F32)],
        scratch_shapes=[pltpu.VMEM((D, D), BF16)] * 2 + [pltpu.SemaphoreType.DMA((2 * N_DEV,))])


def _loss_bwd(h, g, target, name):
    t = h.shape[0]

    def body(h_ref, g_ref, t_ref, loss_ref, dh_ref, dg_ref):
        step = pl.program_id(0)
        ht = h_ref[...]
        gain = g_ref[...]
        diff = _rms_fwd_tile(ht, gain) - t_ref[...]
        part = 0.5 * jnp.sum(jnp.sum(diff * diff, axis=-1, keepdims=True) / D, axis=0, keepdims=True)
        dx, dg = _rms_bwd_tile(ht, gain, diff / D)
        dh_ref[...] = dx
        _accumulate(loss_ref, step, jnp.broadcast_to(part, (8, 128)))
        _accumulate(dg_ref, step, dg)

    return pl.pallas_call(
        body, grid=(t // TM,), name=name,
        in_specs=[_row_spec(TM, D), _const_spec((1, D)), _row_spec(TM, D)],
        out_specs=[_const_spec((8, 128)), _row_spec(TM, D), _const_spec((1, D))],
        out_shape=[jax.ShapeDtypeStruct((8, 128), F32), jax.ShapeDtypeStruct((t, D), F32),
                   jax.ShapeDtypeStruct((1, D), F32)],
        compiler_params=_cparams(),
    )(h, g, target)


def _adamw(w, parts, m, v, name, row_block=0, token=None):
    r, c = w.shape
    n = parts.shape[0]
    tr = _pick_tile(r, (256, 352, 128))
    off = row_block * (r // tr)

    def body(*refs):
        if token is None:
            _adamw_update(None, *refs)
        else:
            _adamw_update(refs[4], *refs[:4], *refs[5:])

    spec = _row_spec(tr, c)
    in_specs = [spec, pl.BlockSpec((n, tr, c), lambda i: (0, i + off, 0)), spec, spec]
    operands = (w, parts, m, v)
    if token is not None:
        in_specs.append(_const_spec(token.shape))
        operands += (token,)
    return pl.pallas_call(
        body, grid=(r // tr,), name=name, in_specs=in_specs, out_specs=[spec] * 4,
        out_shape=[jax.ShapeDtypeStruct((r, c), F32)] * 4,
        compiler_params=_cparams(),
    )(*operands)


def _adamw_update(tok_ref, w_ref, p_ref, m_ref, v_ref, g_ref, d_ref, nm_ref, nv_ref):
    gt = p_ref[0].astype(F32)
    for k in range(1, p_ref.shape[0]):
        gt = gt + p_ref[k].astype(F32)
    if tok_ref is not None:
        gt = gt + tok_ref[0:1, 0:1]
    _adamw_apply(gt, w_ref, m_ref, v_ref, g_ref, d_ref, nm_ref, nv_ref)


def _adamw_own(w, land, own, chip, m, v, name, row_block=0, token=None):
    r, c = w.shape
    tr = _pick_tile(r, (256, 352, 128))
    off = row_block * (r // tr)

    def body(chip_ref, w_ref, land_ref, own_ref, m_ref, v_ref, *rest):
        mine = own_ref[0].astype(F32)
        gt = jnp.where(chip_ref[0] == 0, mine, land_ref[0].astype(F32))
        for k in range(1, N_CHIP):
            gt = gt + jnp.where(chip_ref[0] == k, mine, land_ref[k].astype(F32))
        if token is not None:
            gt = gt + rest[0][0:1, 0:1]
        _adamw_apply(gt, w_ref, m_ref, v_ref, *rest[-4:])

    spec = pl.BlockSpec((tr, c), lambda i, chip_ref: (i, 0))
    in_specs = [spec, pl.BlockSpec((N_CHIP, tr, c), lambda i, chip_ref: (0, i + off, 0)),
                pl.BlockSpec((1, tr, c), lambda i, chip_ref: (chip_ref[0], i + off, 0)), spec, spec]
    operands = (chip, w, land, own, m, v)
    if token is not None:
        in_specs.append(pl.BlockSpec(token.shape, lambda i, chip_ref: (0, 0)))
        operands += (token,)
    return pl.pallas_call(
        body, name=name,
        grid_spec=pltpu.PrefetchScalarGridSpec(
            num_scalar_prefetch=1, grid=(r // tr,), in_specs=in_specs, out_specs=[spec] * 4),
        out_shape=[jax.ShapeDtypeStruct((r, c), F32)] * 4,
        compiler_params=_cparams(),
    )(*operands)


def _adamw_apply(gt, w_ref, m_ref, v_ref, g_ref, d_ref, nm_ref, nv_ref):
    g_ref[...] = gt
    nm = ADAM_B1 * m_ref[...] + (1.0 - ADAM_B1) * gt
    nv = ADAM_B2 * v_ref[...] + (1.0 - ADAM_B2) * jnp.square(gt)
    m_hat = nm / (1.0 - ADAM_B1 ** ADAM_STEP)
    v_hat = nv / (1.0 - ADAM_B2 ** ADAM_STEP)
    d_ref[...] = -ADAM_LR * (m_hat / (jnp.sqrt(v_hat) + ADAM_EPS) + ADAM_WD * w_ref[...])
    nm_ref[...] = nm
    nv_ref[...] = nv


def _mesh_pos():
    return lax.axis_index("x"), lax.axis_index("y"), lax.axis_index("c")


def _no_round(in_refs, out_refs, sems):
    pass


def _run_exchange(comm, name):
    c_in, c_out = len(comm.inputs), len(comm.out_shapes)

    def body(*refs):
        cins, couts, sems = refs[:c_in], refs[c_in:c_in + c_out], refs[c_in + c_out:]
        comm.start(cins, couts, sems)
        comm.middle(cins, couts, sems)
        comm.finish(cins, couts, sems)

    return list(pl.pallas_call(
        body, name=name, out_shape=list(comm.out_shapes),
        in_specs=[_ANY] * c_in, out_specs=[_ANY] * c_out, scratch_shapes=list(comm.sem_shapes),
    )(*comm.inputs))


def _gather_exchange(shards):
    n_arr = len(shards)

    def plan(x_refs, out_refs, sems):
        send_sems, recv_sems, local_sems = sems
        x, y, c = _mesh_pos()
        me, sibling = (x, y, c), (x, y, 1 - c)
        xn, yn, diag = (1 - x, y), (x, 1 - y), (1 - x, 1 - y)

        def slot(a, px, py, pc, half=None):
            ref = out_refs[a].at[4 * px + 2 * py + pc]
            if half is None:
                return ref
            rows = shards[a].shape[0] // 2
            return ref.at[half * rows:(half + 1) * rows]

        def copy(a, k, block, to, half=None, src=None):
            dst = slot(a, *block, half)
            return pltpu.make_async_remote_copy(
                src_ref=dst if src is None else src, dst_ref=dst,
                send_sem=send_sems.at[a, k], recv_sem=recv_sems.at[a, k],
                device_id=to, device_id_type=pl.DeviceIdType.MESH)

        return types.SimpleNamespace(
            me=me, sibling=sibling, xn=xn, yn=yn, diag=diag, c=c, copy=copy,
            mine=[pltpu.make_async_copy(x_refs[a], slot(a, *me), local_sems.at[a]) for a in range(n_arr)],
            first=[cp for a in range(n_arr) for cp in (
                copy(a, 0, me, sibling, src=x_refs[a]), copy(a, 1, me, (*xn, c), src=x_refs[a]),
                copy(a, 2, me, (*yn, c), src=x_refs[a]))],
            second=lambda a: (copy(a, 3, (*xn, c), (*yn, c), half=0), copy(a, 5, (*xn, c), sibling),
                              copy(a, 4, (*yn, c), (*xn, c), half=1), copy(a, 6, (*yn, c), sibling)),
            third=lambda a: (copy(a, 7, (*diag, c), sibling, half=0), copy(a, 8, (*diag, c), sibling, half=1)))

    def start(x_refs, out_refs, sems):
        p = plan(x_refs, out_refs, sems)
        for cp in p.mine + p.first:
            cp.start()

    def middle(x_refs, out_refs, sems):
        p = plan(x_refs, out_refs, sems)
        for a in range(n_arr):
            to_yn, x_to_sib, to_xn, y_to_sib = p.second(a)
            p.copy(a, 1, (*p.xn, p.c), p.me).wait_recv()
            to_yn.start()
            x_to_sib.start()
            p.copy(a, 2, (*p.yn, p.c), p.me).wait_recv()
            to_xn.start()
            y_to_sib.start()

    def finish(x_refs, out_refs, sems):
        p = plan(x_refs, out_refs, sems)
        for a in range(n_arr):
            half0_to_sib, half1_to_sib = p.third(a)
            p.copy(a, 3, (*p.diag, p.c), p.me, half=0).wait_recv()
            half0_to_sib.start()
            p.copy(a, 4, (*p.diag, p.c), p.me, half=1).wait_recv()
            half1_to_sib.start()
        other = 1 - p.c
        for a in range(n_arr):
            p.copy(a, 0, p.sibling, p.me).wait_recv()
            p.copy(a, 5, (*p.xn, other), p.me).wait_recv()
            p.copy(a, 6, (*p.yn, other), p.me).wait_recv()
            p.copy(a, 7, (*p.diag, other), p.me, half=0).wait_recv()
            p.copy(a, 8, (*p.diag, other), p.me, half=1).wait_recv()
        for cp in p.first:
            cp.wait_send()
        for a in range(n_arr):
            for cp in p.second(a) + p.third(a):
                cp.wait_send()
        for cp in p.mine:
            cp.wait()

    return types.SimpleNamespace(
        inputs=list(shards), start=start, middle=middle, finish=finish,
        out_shapes=[jax.ShapeDtypeStruct((N_DEV,) + s.shape, s.dtype) for s in shards],
        sem_shapes=[pltpu.SemaphoreType.DMA((n_arr, 9)), pltpu.SemaphoreType.DMA((n_arr, 9)),
                    pltpu.SemaphoreType.DMA((n_arr,))])


def _pair_exchange(grads):
    n_arr = len(grads)

    def plan(g_refs, land_refs, sems):
        send_sems, recv_sems = sems
        x, y, c = _mesh_pos()
        return [pltpu.make_async_remote_copy(
            src_ref=g_refs[a].at[2 * k + 1 - c], dst_ref=land_refs[a].at[k],
            send_sem=send_sems.at[a, k], recv_sem=recv_sems.at[a, k],
            device_id=(x, y, 1 - c), device_id_type=pl.DeviceIdType.MESH)
            for a in range(n_arr) for k in range(N_CHIP)]

    def start(g_refs, land_refs, sems):
        for cp in plan(g_refs, land_refs, sems):
            cp.start()

    def finish(g_refs, land_refs, sems):
        for cp in plan(g_refs, land_refs, sems):
            cp.wait()

    return types.SimpleNamespace(
        inputs=list(grads), start=start, middle=_no_round, finish=finish,
        out_shapes=[jax.ShapeDtypeStruct((N_CHIP,) + g.shape[1:], g.dtype) for g in grads],
        sem_shapes=[pltpu.SemaphoreType.DMA((n_arr, N_CHIP)), pltpu.SemaphoreType.DMA((n_arr, N_CHIP))])


def _chip_exchange(parts):
    n_arr = len(parts)

    def plan(p_refs, land_refs, sems):
        send_sems, recv_sems, local_sems = sems
        x, y, c = _mesh_pos()
        my_chip = 2 * x + y
        chips = [(1 - x, y), (x, 1 - y), (1 - x, 1 - y)]
        local = [pltpu.make_async_copy(p_refs[a].at[my_chip], land_refs[a].at[my_chip], local_sems.at[a])
                 for a in range(n_arr)]

        def copy(a, k, src_slot, dst_slot, px, py):
            return pltpu.make_async_remote_copy(
                src_ref=p_refs[a].at[src_slot], dst_ref=land_refs[a].at[dst_slot],
                send_sem=send_sems.at[a, k], recv_sem=recv_sems.at[a, k],
                device_id=(px, py, c), device_id_type=pl.DeviceIdType.MESH)

        sends = [copy(a, k, 2 * px + py, my_chip, px, py) for a in range(n_arr) for k, (px, py) in enumerate(chips)]
        arrivals = [copy(a, k, my_chip, 2 * px + py, px, py) for a in range(n_arr)
                    for k, (px, py) in enumerate(chips)]
        return local, sends, arrivals

    def start(p_refs, land_refs, sems):
        local, sends, _ = plan(p_refs, land_refs, sems)
        for cp in local + sends:
            cp.start()

    def finish(p_refs, land_refs, sems):
        local, sends, arrivals = plan(p_refs, land_refs, sems)
        for cp in arrivals:
            cp.wait_recv()
        for cp in sends:
            cp.wait_send()
        for cp in local:
            cp.wait()

    return types.SimpleNamespace(
        inputs=list(parts), start=start, middle=_no_round, finish=finish,
        out_shapes=[jax.ShapeDtypeStruct(p.shape, p.dtype) for p in parts],
        sem_shapes=[pltpu.SemaphoreType.DMA((n_arr, 3)), pltpu.SemaphoreType.DMA((n_arr, 3)),
                    pltpu.SemaphoreType.DMA((n_arr,))])


_HBM = pl.BlockSpec(memory_space=pltpu.HBM)
_SEM = pl.BlockSpec(memory_space=pltpu.SEMAPHORE)
_DATAFLOW = pltpu.SideEffectType.DATAFLOW_SIDE_EFFECTING


def _chip_copies(p_refs, land_refs, send_sems, recv_sems):
    x, y, c = _mesh_pos()
    my_chip = 2 * x + y
    chips = [(1 - x, y), (x, 1 - y), (1 - x, 1 - y)]
    return [pltpu.make_async_remote_copy(
        src_ref=p_refs[a].at[2 * px + py], dst_ref=land_refs[a].at[my_chip],
        send_sem=send_sems[3 * a + k], recv_sem=recv_sems[3 * a + k],
        device_id=(px, py, c), device_id_type=pl.DeviceIdType.MESH)
        for a in range(len(p_refs)) for k, (px, py) in enumerate(chips)]


def _chip_exchange_begin(parts, name):
    n_arr = len(parts)
    n_buf, n_copy = 2 * n_arr, 3 * n_arr
    lands = [lax.empty(p.shape, p.dtype) for p in parts]

    def body(*refs):
        p_refs, land_refs = refs[:n_arr], refs[n_arr:n_buf]
        send_sems, recv_sems, token = refs[n_buf:n_buf + n_copy], refs[n_buf + n_copy:n_buf + 2 * n_copy], refs[-1]
        for cp in _chip_copies(p_refs, land_refs, send_sems, recv_sems):
            cp.start()
        token[...] = jnp.zeros_like(token)

    bufs = list(parts) + list(lands)
    outs = pl.pallas_call(
        body, name=name,
        out_shape=(*[pltpu.SemaphoreType.DMA(())] * (2 * n_copy), *[pltpu.HBM(b.shape, b.dtype) for b in bufs],
                   jax.ShapeDtypeStruct((8, 128), F32)),
        in_specs=[_HBM] * n_buf,
        out_specs=(*[_SEM] * (2 * n_copy), *[_HBM] * n_buf, pl.BlockSpec(memory_space=pltpu.VMEM)),
        input_output_aliases={i: 2 * n_copy + i for i in range(n_buf)},
        compiler_params=pltpu.CompilerParams(has_side_effects=_DATAFLOW),
    )(*[pltpu.with_memory_space_constraint(b, pltpu.HBM) for b in bufs])
    sems = list(outs[:2 * n_copy])
    thru = list(outs[2 * n_copy:2 * n_copy + n_buf])
    return types.SimpleNamespace(send_sems=sems[:n_copy], recv_sems=sems[n_copy:], parts=thru[:n_arr],
                                 lands=thru[n_arr:], token=outs[-1])


def _chip_exchange_end(flight, after, name):
    send_sems, recv_sems, parts, lands = flight.send_sems, flight.recv_sems, flight.parts, flight.lands
    n_arr = len(parts)
    n_buf, n_copy = 2 * n_arr, 3 * n_arr

    def body(*refs):
        p_refs, land_refs = refs[:n_arr], refs[n_arr:n_buf]
        sems = refs[n_buf:n_buf + 2 * n_copy]
        for cp in _chip_copies(p_refs, land_refs, sems[:n_copy], sems[n_copy:]):
            cp.wait_send()
            cp.wait_recv()

    bufs = list(parts) + list(lands)
    outs = pl.pallas_call(
        body, name=name, out_shape=tuple(pltpu.HBM(b.shape, b.dtype) for b in bufs),
        in_specs=[_HBM] * n_buf + [_SEM] * (2 * n_copy) + [_ANY], out_specs=tuple([_HBM] * n_buf),
        input_output_aliases={i: i for i in range(n_buf)},
        compiler_params=pltpu.CompilerParams(has_side_effects=_DATAFLOW),
    )(*bufs, *send_sems, *recv_sems, after)
    return list(outs[:n_arr]), list(outs[n_arr:])


def _row_tile(r, cap=640):
    best = None
    for cand in range(16, min(r, cap) + 1, 16):
        if r % cand == 0:
            best = cand
    return best if best is not None else r


def _pair_sum(g, landed, core, name):
    _, r, c_dim = g.shape
    tr = _row_tile(r)

    def body(core_ref, mine_ref, theirs_ref, o_ref):
        o_ref[0] = (mine_ref[0].astype(F32) + theirs_ref[0].astype(F32)).astype(o_ref.dtype)

    return pl.pallas_call(
        body, name=name,
        grid_spec=pltpu.PrefetchScalarGridSpec(
            num_scalar_prefetch=1, grid=(N_CHIP, r // tr),
            in_specs=[pl.BlockSpec((1, tr, c_dim), lambda k, i, core_ref: (2 * k + core_ref[0], i, 0)),
                      pl.BlockSpec((1, tr, c_dim), lambda k, i, core_ref: (k, i, 0))],
            out_specs=pl.BlockSpec((1, tr, c_dim), lambda k, i, core_ref: (k, i, 0))),
        out_shape=jax.ShapeDtypeStruct((N_CHIP, r, c_dim), g.dtype),
        compiler_params=_cparams(2),
    )(core, g, landed)


def _sum_slots(parts, name):
    n, r, c_dim = parts.shape
    tr = _row_tile(r)

    def body(p_ref, o_ref):
        acc = p_ref[0].astype(F32)
        for k in range(1, n):
            acc = acc + p_ref[k].astype(F32)
        o_ref[...] = acc

    return pl.pallas_call(
        body, grid=(r // tr,), name=name,
        in_specs=[pl.BlockSpec((n, tr, c_dim), lambda i: (0, i, 0))],
        out_specs=_row_spec(tr, c_dim),
        out_shape=jax.ShapeDtypeStruct((r, c_dim), F32),
        compiler_params=_cparams(),
    )(parts)


GAINS = ("g_ffn1", "g_mix", "g_cross", "g_mem", "g_ffn2", "g_final")
SMALL = GAINS + ("b_gate", "conv_w")
SMALL_R = 16
WEIGHT_ORDER = ("g_ffn1", "w_ffn1_gu", "w_ffn1_down", "g_mix", "w_in", "b_gate", "conv_w", "w_conv_out",
                "w_attn_out", "w_o", "g_cross", "g_mem", "w_cq", "w_ckv", "w_co", "g_ffn2", "w_ffn2_gu",
                "w_ffn2_down", "g_final")
GU_NAMES = ("w_ffn1_gu", "w_ffn2_gu")


def _pack_small(vals, conv_rows):
    rows = [vals[n].reshape(1, D) for n in GAINS] + [vals["b_gate"].reshape(2, D), conv_rows.reshape(CONV_K, D)]
    used = len(GAINS) + 2 + CONV_K
    return jnp.concatenate(rows + [jnp.zeros((SMALL_R - used, D), F32)], axis=0)


def _unpack_small(buf):
    out = {n: buf[k] for k, n in enumerate(GAINS)}
    out["b_gate"] = buf[6:8].reshape(2 * D)
    out["conv_w"] = buf[8:8 + CONV_K]
    return out


def _exchange_shards(wts):
    out = {n: jnp.pad(wts[n].T.astype(BF16), ((0, FF_PAD - FF_BLK), (0, 0))) for n in GU_NAMES}
    for n in ("w_ckv", "w_in", "w_ffn1_down", "w_ffn2_down"):
        out[n] = wts[n].astype(BF16)
    out["mix"] = jnp.concatenate([wts[n].astype(BF16) for n in MIX_MATS], axis=0)
    out["cross"] = jnp.concatenate([wts[n].astype(BF16) for n in CROSS_MATS], axis=0)
    return out


def _by_device(dw):
    return dw.reshape(N_DEV, SQ_ROWS, D)


def _reduce_group(grads, landed, core, names):
    return [_pair_sum(g, l, core, "grads_pair_sum_" + n) for g, l, n in zip(grads, landed, names)]


def _step(x, mem, target, sh, conv_pad, gains, b_gate, core):
    wg1, wd1, conv_all = _run_exchange(_gather_exchange([sh["w_ffn1_gu"], sh["w_ffn1_down"], conv_pad]), "gather_ffn1")
    conv_w = conv_all[:, :CONV_K, :].transpose(1, 0, 2).reshape(CONV_K, D)
    (n1, gate1, up1, act1, h1), (w_in,) = _ffn_fwd(
        x, gains["g_ffn1"], wg1, wd1, "ffn1_fwd", comm=_gather_exchange([sh["w_in"]]))
    (u, pcg, qkv), (w_mix,) = _inproj_fwd(h1, gains["g_mix"], w_in, "inproj_fwd", comm=_gather_exchange([sh["mix"]]))
    yc = _conv_fwd(pcg, conv_w, "conv_fwd")
    (ysb, ctot), (w_cross, w_ckv, wg2, wd2) = _sb_fwd(
        qkv, "sb_fwd", comm=_gather_exchange([sh["cross"], sh["w_ckv"], sh["w_ffn2_gu"], sh["w_ffn2_down"]]))
    a_mix, b_mix, merged, h2 = _mix_out_fwd(yc, ysb, pcg, b_gate, h1, w_mix, "mix_out_fwd")
    mn, kv = _memkv_fwd(mem, gains["g_mem"], w_ckv, "memkv_fwd")
    hn, qx, o_x, h3 = _cross_fwd(h2, gains["g_cross"], kv, w_cross, "cross_fwd")
    (n4, gate2, up2, act2, h4), _ = _ffn_fwd(h3, gains["g_ffn2"], wg2, wd2, "ffn2_fwd")
    loss, dh4, dg_final = _loss_bwd(h4, gains["g_final"], target, "loss_bwd")

    gs = {"g_final": dg_final}
    (dgu2, dh4b, dh3, gs["g_ffn2"]), _ = _ffn_bwd(dh4, h3, gains["g_ffn2"], gate2, up2, wg2, wd2, "ffn2_bwd")
    grads_a = [_mm_tn_rows(dgu2, n4, FF_PAD, "dw_ffn2_gu"),
               _mm_tn_rows(act2, dh4b, FF_BLK, "dw_ffn2_down").reshape(N_DEV, DOWN_ROWS, D)]
    names_a = ["w_ffn2_gu", "w_ffn2_down"]
    (dh3b, dqx, dkv, dh2, gs["g_cross"]), landed_a = _cross_bwd(
        dh3, h2, gains["g_cross"], qx, kv, w_cross, "cross_bwd", comm=_pair_exchange(grads_a))
    sums_a = _reduce_group(grads_a, landed_a, core, names_a)
    grads_b = [_mm_tn_cols(mn, dkv, "dw_ckv"),
               jnp.concatenate([_by_device(_mm_tn(hn, dqx, "dw_cq")), _by_device(_mm_tn(o_x, dh3b, "dw_co"))], axis=1)]
    names_b = ["w_ckv", "cross"]
    gs["g_mem"] = _memkv_bwd(dkv, mem, gains["g_mem"], w_ckv, "memkv_bwd")
    (dh2b, da_mix, db_mix, dgp, dyc, dysb, gs["b_gate"]), landed_b = _mix_out_bwd(
        dh2, a_mix, b_mix, pcg, b_gate, w_mix, "mix_out_bwd", comm=_pair_exchange(grads_b))
    sums_b = _reduce_group(grads_b, landed_b, core, names_b)
    grads_c = [jnp.concatenate([_by_device(_mm_tn(yc, da_mix, "dw_conv_out")),
                                _by_device(_mm_tn(ysb, db_mix, "dw_attn_out")),
                                _by_device(_mm_tn(merged, dh2b, "dw_o"))], axis=1)]
    landed_c = _run_exchange(_pair_exchange(grads_c), "grads_to_sibling_mix")
    sums_c = _reduce_group(grads_c, landed_c, core, ["mix"])
    flight_abc = _chip_exchange_begin(sums_a + sums_b + sums_c, "grads_to_chips_early_begin")
    (dq, dkv_sb), _ = _sb_bwd(qkv, dysb, ctot + flight_abc.token[0, 0], "sb_bwd")
    dconv, gs["conv_w"] = _conv_bwd(pcg, conv_w, dyc, "conv_bwd")
    grads_d = [jnp.concatenate(
        [_mm_tn_cols(u, dconv, "dw_in_conv"), _mm_tn(u, dq, "dw_in_q")[None], _mm_tn_cols(u, dkv_sb, "dw_in_kv"),
         _mm_tn_cols(u, dgp, "dw_in_gates")], axis=0)]
    (dh1, gs["g_mix"]), landed_d = _inproj_bwd(dconv, dq, dkv_sb, dgp, w_in, h1, gains["g_mix"], dh2, "inproj_bwd",
                                               comm=_pair_exchange(grads_d))
    sums_d = _reduce_group(grads_d, landed_d, core, ["w_in"])
    flight_d = _chip_exchange_begin(sums_d, "grads_to_chips_w_in_begin")
    (dgu1, dh1b, dx, gs["g_ffn1"]), _ = _ffn_bwd(dh1, x, gains["g_ffn1"] + flight_d.token[0, 0], gate1, up1, wg1, wd1,
                                                 "ffn1_bwd")
    grads_e = [_mm_tn_rows(dgu1, n1, FF_PAD, "dw_ffn1_gu"),
               _mm_tn_rows(act1, dh1b, FF_BLK, "dw_ffn1_down").reshape(N_DEV, DOWN_ROWS, D)]
    names_e = ["w_ffn1_gu", "w_ffn1_down"]
    landed_e = _run_exchange(_pair_exchange(grads_e), "grads_to_sibling_ffn1")
    flight_e = _chip_exchange_begin(_reduce_group(grads_e, landed_e, core, names_e), "grads_to_chips_ffn1_begin")
    flights = [(names_a + names_b + ["mix"], flight_abc), (["w_in"], flight_d), (names_e, flight_e)]
    return loss, dx, flights, gs


def kernel(x, mem, g_ffn1, w_ffn1_gu, w_ffn1_down, g_mix, w_in, b_gate, conv_w, w_conv_out, w_attn_out, w_o, g_cross, g_mem, w_cq, w_ckv, w_co, g_ffn2, w_ffn2_gu, w_ffn2_down, g_final, loss_target, m_g_ffn1, m_w_ffn1_gu, m_w_ffn1_down, m_g_mix, m_w_in, m_b_gate, m_conv_w, m_w_conv_out, m_w_attn_out, m_w_o, m_g_cross, m_g_mem, m_w_cq, m_w_ckv, m_w_co, m_g_ffn2, m_w_ffn2_gu, m_w_ffn2_down, m_g_final, v_g_ffn1, v_w_ffn1_gu, v_w_ffn1_down, v_g_mix, v_w_in, v_b_gate, v_conv_w, v_w_conv_out, v_w_attn_out, v_w_o, v_g_cross, v_g_mem, v_w_cq, v_w_ckv, v_w_co, v_g_ffn2, v_w_ffn2_gu, v_w_ffn2_down, v_g_final):
    args = locals()
    wts = {n: args[n] for n in WEIGHT_ORDER}
    mom1 = {n: args["m_" + n] for n in WEIGHT_ORDER}
    mom2 = {n: args["v_" + n] for n in WEIGHT_ORDER}
    cx, cy, cc = _mesh_pos()
    dev = 4 * cx + 2 * cy + cc
    conv_cols = D // N_DEV

    conv_pad = jnp.concatenate([conv_w, jnp.zeros((SMALL_R - CONV_K, conv_cols), F32)], axis=0)
    gains = {n: wts[n].reshape(1, D) for n in GAINS}
    loss8, dx, flights, gs = _step(x[0], mem[0], loss_target[0], _exchange_shards(wts), conv_pad, gains,
                                 b_gate.reshape(1, 2 * D), cc.reshape(1).astype(jnp.int32))
    loss = lax.psum(loss8[0, 0], MESH_AXES)

    grads, delta, new_m, new_v = {}, {}, {}, {}

    def operands(n, transposed):
        trio = (wts[n], mom1[n], mom2[n])
        return tuple(a.T for a in trio) if transposed else trio

    def record(n, res, transposed):
        grads[n], delta[n], new_m[n], new_v[n] = [r.T for r in res] if transposed else res

    early = [("w_ffn2_gu", "w_ffn2_gu", 0, True), ("w_ffn2_down", "w_ffn2_down", 0, False),
             ("w_ckv", "w_ckv", 0, False), ("w_in", "w_in", 0, False)]
    early += [(n, "mix", k, False) for k, n in enumerate(MIX_MATS)]
    early += [(n, "cross", k, False) for k, n in enumerate(CROSS_MATS)]
    chip = (2 * cx + cy).reshape(1).astype(jnp.int32)
    (names_early, flight_early), (names_w_in, flight_w_in), (last_names, flight_last) = flights
    token = flight_last.token
    own, land = {}, {}
    for names, flight, tag in ((names_early, flight_early, "early"), (names_w_in, flight_w_in, "w_in")):
        own_parts, landed = _chip_exchange_end(flight, token, "grads_to_chips_%s_end" % tag)
        own.update(zip(names, own_parts))
        land.update(zip(names, landed))
    for n, buf, row_block, transposed in early:
        w, m1, m2 = operands(n, transposed)
        record(n, _adamw_own(w, land[buf], own[buf], chip, m1, m2, "adamw_" + n, row_block, token), transposed)

    after = jnp.concatenate([new_v[n][:1, :1] for n, _, _, _ in early], axis=0)
    own_parts, landed = _chip_exchange_end(flight_last, after, "grads_to_chips_ffn1_end")
    for n, own_n, land_n, transposed in zip(last_names, own_parts, landed, (True, False)):
        w, m1, m2 = operands(n, transposed)
        record(n, _adamw_own(w, land_n, own_n, chip, m1, m2, "adamw_" + n), transposed)

    gs_rows = {n: gs[n] for n in GAINS + ("b_gate",)}
    small_mine = _pack_small(gs_rows, gs["conv_w"][:CONV_K]) + new_v[last_names[-1]][0, 0] * 0.0
    small_all = _run_exchange(_gather_exchange([small_mine]), "gather_small_grads")[0]
    grad_small = _unpack_small(_sum_slots(small_all, "small_grads_sum"))
    grad_small["conv_w"] = lax.dynamic_slice_in_dim(grad_small["conv_w"], dev * conv_cols, conv_cols, axis=1)
    grads.update(grad_small)

    def small_buf(vals):
        return _pack_small(vals, jnp.concatenate([vals["conv_w"], jnp.zeros((CONV_K, D - conv_cols), F32)], axis=1))

    _, d_s, m_s, v_s = _adamw(small_buf(wts), small_buf(grads)[None], small_buf(mom1), small_buf(mom2), "adamw_small")
    for res, buf in ((delta, d_s), (new_m, m_s), (new_v, v_s)):
        un = _unpack_small(buf)
        for n in GAINS + ("b_gate",):
            res[n] = un[n]
        res["conv_w"] = un["conv_w"][:, :conv_cols]

    return (loss, dx[None], *[grads[n] for n in WEIGHT_ORDER], *[delta[n] for n in WEIGHT_ORDER],
            *[new_m[n] for n in WEIGHT_ORDER], *[new_v[n] for n in WEIGHT_ORDER])
```

```python
import types

import jax
import jax.numpy as jnp
from jax import lax
from jax.experimental import pallas as pl
from jax.experimental.pallas import tpu as pltpu

F32 = jnp.float32
BF16 = jnp.bfloat16

D = 1024
DFF = 2816
SB_H = 8
SB_DH = 128
X_H = 4
X_DH = 256
CONV_K = 3
RMS_EPS = 1e-6
N_DEV = 8
N_CHIP = 4
SQ_ROWS = D // N_DEV

ADAM_LR = 0.001
ADAM_B1 = 0.9
ADAM_B2 = 0.999
ADAM_EPS = 1e-08
ADAM_WD = 0.01
ADAM_STEP = 10

TM = 256
TQ = 512
TK = 256
SB_HPS = 2
VMEM_LIMIT = 56 << 20

FF_BLK = DFF // 4
FF_PAD = 768
FF_SUB = 256
DOWN_ROWS = DFF // N_DEV

MIX_MATS = ("w_conv_out", "w_attn_out", "w_o")
CROSS_MATS = ("w_cq", "w_co")

_ANY = pl.BlockSpec(memory_space=pl.ANY)


def _cparams(n_axes=1):
    return pltpu.CompilerParams(
        dimension_semantics=("arbitrary",) * n_axes, vmem_limit_bytes=VMEM_LIMIT)


def _row_spec(tm, n):
    return pl.BlockSpec((tm, n), lambda i: (i, 0))


def _blk_row_spec(nb, tm, n):
    return pl.BlockSpec((nb, tm, n), lambda i: (0, i, 0))


def _const_spec(shape):
    zeros = (0,) * len(shape)
    return pl.BlockSpec(shape, lambda i: zeros)


def _dot(a, b):
    return jnp.dot(a, b, preferred_element_type=F32)


def _dot_nt(a, b):
    return lax.dot_general(a, b, (((1,), (1,)), ((), ())), preferred_element_type=F32)


def _dot_tn(a, b):
    return lax.dot_general(a, b, (((0,), (0,)), ((), ())), preferred_element_type=F32)


def _sigmoid(x):
    return 1.0 / (1.0 + jnp.exp(-x))


def _call(body, operands, *, grid, in_specs, out_specs, out_shape, scratch_shapes, name, comm=None):
    n_in, n_out, n_sc = len(in_specs), len(out_specs), len(scratch_shapes)
    if comm is None:
        outs = pl.pallas_call(
            body, grid=grid, name=name, in_specs=in_specs, out_specs=out_specs, out_shape=out_shape,
            scratch_shapes=scratch_shapes, compiler_params=_cparams(len(grid)))(*operands)
        return list(outs), []
    c_in, c_out, c_sem = len(comm.inputs), len(comm.out_shapes), len(comm.sem_shapes)

    def hosted(*refs):
        bounds = [0, n_in, c_in, n_out, c_out, n_sc, c_sem]
        parts, pos = [], 0
        for k in bounds[1:]:
            parts.append(refs[pos:pos + k])
            pos += k
        ins, cins, outs, couts, scr, sems = parts
        step, n_steps = pl.program_id(0), grid[0]
        for ax in range(1, len(grid)):
            step, n_steps = step * grid[ax] + pl.program_id(ax), n_steps * grid[ax]

        @pl.when(step == 0)
        def _():
            comm.start(cins, couts, sems)

        @pl.when(step == (2 * n_steps) // 3)
        def _():
            comm.middle(cins, couts, sems)

        body(*ins, *outs, *scr)

        @pl.when(step == n_steps - 1)
        def _():
            comm.finish(cins, couts, sems)

    res = pl.pallas_call(
        hosted, grid=grid, name=name, in_specs=list(in_specs) + [_ANY] * c_in,
        out_specs=list(out_specs) + [_ANY] * c_out, out_shape=list(out_shape) + list(comm.out_shapes),
        scratch_shapes=list(scratch_shapes) + list(comm.sem_shapes),
        compiler_params=_cparams(len(grid)))(*operands, *comm.inputs)
    return list(res[:n_out]), list(res[n_out:])


def _load_resident(step, pairs, sems):
    @pl.when(step == 0)
    def _():
        copies = [pltpu.make_async_copy(src, dst, sems.at[k]) for k, (src, dst) in enumerate(pairs)]
        for cp in copies:
            cp.start()
        for cp in copies:
            cp.wait()


def _square_pairs(buf_hbm, index, dst):
    off = index * SQ_ROWS
    return [(buf_hbm.at[d, off:off + SQ_ROWS, :], dst.at[d * SQ_ROWS:(d + 1) * SQ_ROWS, :]) for d in range(N_DEV)]


def _down_pairs(wd_hbm, dst):
    return [(wd_hbm.at[d], dst.at[d // 2, (d % 2) * DOWN_ROWS:(d % 2 + 1) * DOWN_ROWS, :]) for d in range(N_DEV)]


def _zero_down_pad(step, dst):
    @pl.when(step == 0)
    def _():
        dst[:, FF_BLK:, :] = jnp.zeros((4, FF_PAD - FF_BLK, D), BF16)


def _rms_fwd_tile(xt, g):
    r = lax.rsqrt(jnp.mean(xt * xt, axis=-1, keepdims=True) + RMS_EPS)
    return (xt * r) * g


def _rms_bwd_tile(xt, g, dn):
    r = lax.rsqrt(jnp.mean(xt * xt, axis=-1, keepdims=True) + RMS_EPS)
    xhat = xt * r
    dxhat = dn * g
    dx = r * (dxhat - xhat * jnp.mean(dxhat * xhat, axis=-1, keepdims=True))
    dg = jnp.sum(dn * xhat, axis=0, keepdims=True)
    return dx, dg


def _accumulate(ref, step, value):
    @pl.when(step == 0)
    def _():
        ref[...] = value

    @pl.when(step != 0)
    def _():
        ref[...] = ref[...] + value


def _ffn_fwd(x, g, wgu, wd, name, comm=None, head=None):
    t = x.shape[0]

    def body(x_ref, g_ref, wgu_hbm, wd_hbm, *refs):
        if head is None:
            n_ref, gate_ref, up_ref, act_ref, h_ref, wgu_v, wd_v, sems = refs
        else:
            gf_ref, t_ref, n_ref, gate_ref, up_ref, act_ref, dh_ref, loss_ref, dgf_ref, wgu_v, wd_v, sems = refs
        step = pl.program_id(0)
        _zero_down_pad(step, wd_v)
        _load_resident(step, [(wgu_hbm, wgu_v)] + _down_pairs(wd_hbm, wd_v), sems)
        xt = x_ref[...]
        n = _rms_fwd_tile(xt, g_ref[...]).astype(BF16)
        n_ref[...] = n
        acc = jnp.zeros((TM, D), F32)
        for j in range(4):
            for s in range(FF_PAD // FF_SUB):
                lo, hi = s * FF_SUB, (s + 1) * FF_SUB
                gt = _dot_nt(n, wgu_v[j, lo:hi, :])
                ut = _dot_nt(n, wgu_v[4 + j, lo:hi, :])
                gate_ref[j, :, lo:hi] = gt.astype(BF16)
                up_ref[j, :, lo:hi] = ut.astype(BF16)
                act_ref[j, :, lo:hi] = ((gt * _sigmoid(gt)) * ut).astype(BF16)
            acc = acc + _dot(act_ref[j], wd_v[j])
        ht = xt + 0.5 * acc
        if head is None:
            h_ref[...] = ht
        else:
            gain = gf_ref[...]
            diff = _rms_fwd_tile(ht, gain) - t_ref[...]
            part = 0.5 * jnp.sum(jnp.sum(diff * diff, axis=-1, keepdims=True) / D, axis=0, keepdims=True)
            dx, dg = _rms_bwd_tile(ht, gain, diff / D)
            dh_ref[...] = dx
            _accumulate(loss_ref, step, jnp.broadcast_to(part, (8, 128)))
            _accumulate(dgf_ref, step, dg)

    ff = jax.ShapeDtypeStruct((4, t, FF_PAD), BF16)
    operands, in_specs = (x, g, wgu, wd), [_row_spec(TM, D), _const_spec((1, D)), _ANY, _ANY]
    out_specs = [_row_spec(TM, D)] + [_blk_row_spec(4, TM, FF_PAD)] * 3 + [_row_spec(TM, D)]
    out_shape = [jax.ShapeDtypeStruct((t, D), BF16), ff, ff, ff, jax.ShapeDtypeStruct((t, D), F32)]
    if head is not None:
        operands += tuple(head)
        in_specs += [_const_spec((1, D)), _row_spec(TM, D)]
        out_specs += [_const_spec((8, 128)), _const_spec((1, D))]
        out_shape += [jax.ShapeDtypeStruct((8, 128), F32), jax.ShapeDtypeStruct((1, D), F32)]
    return _call(
        body, operands, grid=(t // TM,), name=name, comm=comm, in_specs=in_specs, out_specs=out_specs,
        out_shape=out_shape,
        scratch_shapes=[pltpu.VMEM((N_DEV, FF_PAD, D), BF16), pltpu.VMEM((4, FF_PAD, D), BF16),
                        pltpu.SemaphoreType.DMA((1 + N_DEV,))])


def _ffn_bwd(dh, xin, g, gate, up, wgu, wd, name, comm=None):
    t = dh.shape[0]

    def body(dh_ref, x_ref, g_ref, gate_ref, up_ref, wgu_hbm, wd_hbm,
             dgu_ref, dhb_ref, dx_ref, dg_ref, wgu_v, wd_v, sems):
        step = pl.program_id(0)
        _zero_down_pad(step, wd_v)
        _load_resident(step, [(wgu_hbm, wgu_v)] + _down_pairs(wd_hbm, wd_v), sems)
        dht = dh_ref[...]
        dhb = (0.5 * dht).astype(BF16)
        dhb_ref[...] = dhb
        dn = jnp.zeros((TM, D), F32)
        for j in range(4):
            for s in range(FF_PAD // FF_SUB):
                lo, hi = s * FF_SUB, (s + 1) * FF_SUB
                da = _dot_nt(dhb, wd_v[j, lo:hi, :])
                gt = gate_ref[j, :, lo:hi].astype(F32)
                ut = up_ref[j, :, lo:hi].astype(F32)
                sg = _sigmoid(gt)
                dgt = (da * ut * (sg * (1.0 + gt * (1.0 - sg)))).astype(BF16)
                dut = (da * (gt * sg)).astype(BF16)
                dgu_ref[j, :, lo:hi] = dgt
                dgu_ref[4 + j, :, lo:hi] = dut
            dn = dn + _dot(dgu_ref[j], wgu_v[j]) + _dot(dgu_ref[4 + j], wgu_v[4 + j])
        dx, dg = _rms_bwd_tile(x_ref[...], g_ref[...], dn)
        dx_ref[...] = dht + dx
        _accumulate(dg_ref, step, dg)

    return _call(
        body, (dh, xin, g, gate, up, wgu, wd), grid=(t // TM,), name=name, comm=comm,
        in_specs=[_row_spec(TM, D), _row_spec(TM, D), _const_spec((1, D)), _blk_row_spec(4, TM, FF_PAD),
                  _blk_row_spec(4, TM, FF_PAD), _ANY, _ANY],
        out_specs=[_blk_row_spec(N_DEV, TM, FF_PAD), _row_spec(TM, D), _row_spec(TM, D), _const_spec((1, D))],
        out_shape=[jax.ShapeDtypeStruct((N_DEV, t, FF_PAD), BF16), jax.ShapeDtypeStruct((t, D), BF16),
                   jax.ShapeDtypeStruct((t, D), F32), jax.ShapeDtypeStruct((1, D), F32)],
        scratch_shapes=[pltpu.VMEM((N_DEV, FF_PAD, D), BF16), pltpu.VMEM((4, FF_PAD, D), BF16),
                        pltpu.SemaphoreType.DMA((1 + N_DEV,))])


WIDE_TILES = (1024, 512, 256, 128)


def _pick_tile(n, options=(512, 256, 128)):
    for o in options:
        if n % o == 0:
            return o
    return n


def _mm_tn(a, b, name):
    k, m = a.shape
    _, n = b.shape
    tm, tn = _pick_tile(m, WIDE_TILES), _pick_tile(n)

    def body(a_ref, b_ref, o_ref):
        o_ref[...] = _dot_tn(a_ref[...].astype(BF16), b_ref[...].astype(BF16)).astype(BF16)

    return pl.pallas_call(
        body, grid=(m // tm, n // tn), name=name,
        in_specs=[pl.BlockSpec((k, tm), lambda i, j: (0, i)), pl.BlockSpec((k, tn), lambda i, j: (0, j))],
        out_specs=pl.BlockSpec((tm, tn), lambda i, j: (i, j)),
        out_shape=jax.ShapeDtypeStruct((m, n), BF16),
        compiler_params=_cparams(2),
    )(a, b)


def _mm_tn_cols(a, b, name):
    k, m = a.shape
    nb, _, n = b.shape
    tm = _pick_tile(m, WIDE_TILES)

    def body(a_ref, b_ref, o_ref):
        o_ref[0] = _dot_tn(a_ref[...].astype(BF16), b_ref[0].astype(BF16)).astype(BF16)

    return pl.pallas_call(
        body, grid=(nb, m // tm), name=name,
        in_specs=[pl.BlockSpec((k, tm), lambda j, i: (0, i)), pl.BlockSpec((1, k, n), lambda j, i: (j, 0, 0))],
        out_specs=pl.BlockSpec((1, tm, n), lambda j, i: (j, i, 0)),
        out_shape=jax.ShapeDtypeStruct((nb, m, n), BF16),
        compiler_params=_cparams(2),
    )(a, b)


def _mm_tn_rows(a, b, keep, name):
    nb, k, m = a.shape
    _, n = b.shape
    tn = _pick_tile(n, WIDE_TILES)

    def body(a_ref, b_ref, o_ref):
        o_ref[0] = _dot_tn(a_ref[0], b_ref[...])[:keep].astype(BF16)

    return pl.pallas_call(
        body, grid=(nb, n // tn), name=name,
        in_specs=[pl.BlockSpec((1, k, m), lambda j, i: (j, 0, 0)), pl.BlockSpec((k, tn), lambda j, i: (0, i))],
        out_specs=pl.BlockSpec((1, keep, tn), lambda j, i: (j, 0, i)),
        out_shape=jax.ShapeDtypeStruct((nb, keep, n), BF16),
        compiler_params=_cparams(2),
    )(a, b)


PCG_W = 5 * D
QKV_W = 3 * D
PROJ_SUB = 512


def _inproj_fwd(h, g, w_in, name, comm=None):
    t = h.shape[0]

    def body(h_ref, g_ref, w_hbm, u_ref, pcg_ref, qkv_ref, w_v, sems):
        _load_resident(pl.program_id(0), [(w_hbm, w_v)], sems)
        u = _rms_fwd_tile(h_ref[...], g_ref[...]).astype(BF16)
        u_ref[...] = u
        for blk in range(N_DEV):
            for s in range(D // PROJ_SUB):
                lo, hi = s * PROJ_SUB, (s + 1) * PROJ_SUB
                p = _dot(u, w_v[blk, :, lo:hi])
                if blk < 3:
                    pcg_ref[:, blk * D + lo:blk * D + hi] = p
                elif blk < 6:
                    qkv_ref[:, (blk - 3) * D + lo:(blk - 3) * D + hi] = p.astype(BF16)
                else:
                    pcg_ref[:, (blk - 3) * D + lo:(blk - 3) * D + hi] = p

    return _call(
        body, (h, g, w_in), grid=(t // TM,), name=name, comm=comm,
        in_specs=[_row_spec(TM, D), _const_spec((1, D)), _ANY],
        out_specs=[_row_spec(TM, D), _row_spec(TM, PCG_W), _row_spec(TM, QKV_W)],
        out_shape=[jax.ShapeDtypeStruct((t, D), BF16), jax.ShapeDtypeStruct((t, PCG_W), F32),
                   jax.ShapeDtypeStruct((t, QKV_W), BF16)],
        scratch_shapes=[pltpu.VMEM((N_DEV, D, D), BF16), pltpu.SemaphoreType.DMA((1,))])


CONV_CW = 256


def _shift_down(v, k, rows):
    return jnp.where(rows >= k, pltpu.roll(v, k, 0), 0.0)


def _shift_up(v, k, rows, t):
    return jnp.where(rows < t - k, pltpu.roll(v, t - k, 0), 0.0)


def _col_spec(t, cw, off):
    return pl.BlockSpec((t, cw), lambda j: (0, j + off))


def _conv_fwd(pcg, conv_w, name):
    t = pcg.shape[0]
    nb = D // CONV_CW

    def body(cb_ref, cc_ref, cx_ref, w_ref, y_ref):
        rows = lax.broadcasted_iota(jnp.int32, (t, CONV_CW), 0)
        xc = cc_ref[...] * cx_ref[...]
        conv = (w_ref[0:1, :] * _shift_down(xc, 2, rows) + w_ref[1:2, :] * _shift_down(xc, 1, rows)
                + w_ref[2:3, :] * xc)
        y_ref[...] = (cb_ref[...] * conv).astype(BF16)

    return pl.pallas_call(
        body, grid=(nb,), name=name,
        in_specs=[_col_spec(t, CONV_CW, 0), _col_spec(t, CONV_CW, nb), _col_spec(t, CONV_CW, 2 * nb),
                  pl.BlockSpec((CONV_K, CONV_CW), lambda j: (0, j))],
        out_specs=_col_spec(t, CONV_CW, 0),
        out_shape=jax.ShapeDtypeStruct((t, D), BF16),
        compiler_params=_cparams(),
    )(pcg, pcg, pcg, conv_w)


def _conv_bwd(pcg, conv_w, dyc, name):
    t = pcg.shape[0]
    nb = D // CONV_CW

    def body(cb_ref, cc_ref, cx_ref, w_ref, dy_ref, dc_ref, dw_ref):
        rows = lax.broadcasted_iota(jnp.int32, (t, CONV_CW), 0)
        cc, cx = cc_ref[...], cx_ref[...]
        xc = cc * cx
        x1 = _shift_down(xc, 1, rows)
        x2 = _shift_down(xc, 2, rows)
        w0, w1, w2 = w_ref[0:1, :], w_ref[1:2, :], w_ref[2:3, :]
        conv = w0 * x2 + w1 * x1 + w2 * xc
        dy = dy_ref[...]
        dc_ref[0] = (dy * conv).astype(BF16)
        dconv = dy * cb_ref[...]
        dw_ref[...] = jnp.zeros((8, CONV_CW), F32)
        dw_ref[0:1, :] = jnp.sum(dconv * x2, axis=0, keepdims=True)
        dw_ref[1:2, :] = jnp.sum(dconv * x1, axis=0, keepdims=True)
        dw_ref[2:3, :] = jnp.sum(dconv * xc, axis=0, keepdims=True)
        dxc = w2 * dconv + w1 * _shift_up(dconv, 1, rows, t) + w0 * _shift_up(dconv, 2, rows, t)
        dc_ref[1] = (dxc * cx).astype(BF16)
        dc_ref[2] = (dxc * cc).astype(BF16)

    return pl.pallas_call(
        body, grid=(nb,), name=name,
        in_specs=[_col_spec(t, CONV_CW, 0), _col_spec(t, CONV_CW, nb), _col_spec(t, CONV_CW, 2 * nb),
                  pl.BlockSpec((CONV_K, CONV_CW), lambda j: (0, j)), _col_spec(t, CONV_CW, 0)],
        out_specs=[pl.BlockSpec((3, t, CONV_CW), lambda j: (0, 0, j)), pl.BlockSpec((8, CONV_CW), lambda j: (0, j))],
        out_shape=[jax.ShapeDtypeStruct((3, t, D), BF16), jax.ShapeDtypeStruct((8, D), F32)],
        compiler_params=_cparams(),
    )(pcg, pcg, pcg, conv_w, dyc)


def _tri2(cond):
    rr = lax.broadcasted_iota(jnp.int32, (2 * TK, TK), 0) & (TK - 1)
    cc = lax.broadcasted_iota(jnp.int32, (2 * TK, TK), 1)
    return cond(rr, cc).astype(BF16)


def _causal(shift, row0=0):
    rr = lax.broadcasted_iota(jnp.int32, (TQ - row0, TK), 0) + row0
    cc = lax.broadcasted_iota(jnp.int32, (TQ - row0, TK), 1)
    return cc + shift < rr


def _cumdot(v, tri2):
    hi = v.astype(BF16)
    lo = (v - hi.astype(F32)).astype(BF16)
    return _dot(jnp.concatenate([hi, lo], axis=1), tri2)


def _log_1m_beta(z):
    return -(jnp.maximum(z, 0.0) + jnp.log(1.0 + jnp.exp(-jnp.abs(z))))


def _sb_specs(t):
    g = SB_H // SB_HPS
    w = SB_HPS * SB_DH
    q_spec = pl.BlockSpec((TQ, w), lambda h, i: (i, h))
    k_spec = pl.BlockSpec((t, w), lambda h, i: (0, g + h))
    v_spec = pl.BlockSpec((t, w), lambda h, i: (0, 2 * g + h))
    ct_spec = pl.BlockSpec((SB_HPS, TQ, 1), lambda h, i: (h, i, 0))
    return g, w, q_spec, k_spec, v_spec, ct_spec


def _sb_fwd(qkv, name, comm=None):
    t = qkv.shape[0]
    scale = SB_DH ** -0.5
    g, w, q_spec, k_spec, v_spec, ct_spec = _sb_specs(t)

    def body(q_ref, k_ref, v_ref, y_ref, ct_ref):
        i = pl.program_id(1)
        later = _tri2(lambda j, s: j > s)
        n_diag = TQ // TK

        def block(j, carry, shift):
            off = pl.multiple_of(j * TK, TK)
            zs, ms = [], []
            for hd in range(SB_HPS):
                cols = slice(hd * SB_DH, (hd + 1) * SB_DH)
                z = _dot_nt(q_ref[:, cols], k_ref[pl.ds(off, TK), cols]) * scale
                m = _log_1m_beta(z)
                if shift is not None:
                    m = jnp.where(_causal(shift), m, 0.0)
                zs.append(z)
                ms.append(m)
            after = _cumdot(jnp.concatenate(ms, axis=0), later)
            out = []
            for hd in range(SB_HPS):
                acc, c_sum = carry[hd]
                cols = slice(hd * SB_DH, (hd + 1) * SB_DH)
                a = jnp.exp((ms[hd] + zs[hd]) + (c_sum + after[hd * TQ:(hd + 1) * TQ]))
                if shift is not None:
                    a = jnp.where(_causal(shift), a, 0.0)
                out.append((acc + _dot(a.astype(BF16), v_ref[pl.ds(off, TK), cols]),
                            c_sum + jnp.sum(ms[hd], axis=1, keepdims=True)))
            return tuple(out)

        carry = tuple((jnp.zeros((TQ, SB_DH), F32), jnp.zeros((TQ, 1), F32)) for _ in range(SB_HPS))
        for d in reversed(range(n_diag)):
            carry = block(i * n_diag + d, carry, d * TK)
        carry = lax.fori_loop(0, i * n_diag, lambda jj, c: block(i * n_diag - 1 - jj, c, None), carry)
        for hd in range(SB_HPS):
            y_ref[:, hd * SB_DH:(hd + 1) * SB_DH] = carry[hd][0].astype(BF16)
            ct_ref[hd] = carry[hd][1]

    return _call(
        body, (qkv, qkv, qkv), grid=(g, t // TQ), name=name, comm=comm,
        in_specs=[q_spec, k_spec, v_spec],
        out_specs=[q_spec, ct_spec],
        out_shape=[jax.ShapeDtypeStruct((t, D), BF16), jax.ShapeDtypeStruct((SB_H, t, 1), F32)],
        scratch_shapes=[])


def _sb_bwd(qkv, dy, ctot, name, comm=None):
    t = qkv.shape[0]
    scale = SB_DH ** -0.5
    g, w, q_spec, k_spec, v_spec, ct_spec = _sb_specs(t)
    acc_spec = pl.BlockSpec((2, t, w), lambda h, i: (0, 0, h))

    def body(q_ref, k_ref, v_ref, dy_ref, ct_ref, dq_ref, dkv_ref):
        i = pl.program_id(1)

        @pl.when(i == 0)
        def _():
            dkv_ref[...] = jnp.zeros_like(dkv_ref)

        upto = _tri2(lambda j, s: j <= s)
        n_diag = TQ // TK

        def block(j, carry, shift):
            off = pl.multiple_of(j * TK, TK)
            r0 = 0 if shift is None else shift
            nr = TQ - r0
            causal = None if shift is None else _causal(shift, r0)

            def grow(old, delta):
                return old + delta if r0 == 0 else jnp.concatenate([old[:r0], old[r0:] + delta], axis=0)

            zs, ms = [], []
            for hd in range(SB_HPS):
                cols = slice(hd * SB_DH, (hd + 1) * SB_DH)
                z = _dot_nt(q_ref[r0:, cols], k_ref[pl.ds(off, TK), cols]) * scale
                m = _log_1m_beta(z)
                if causal is not None:
                    m = jnp.where(causal, m, 0.0)
                zs.append(z)
                ms.append(m)
            m_upto = _cumdot(jnp.concatenate(ms, axis=0), upto)
            ls, a_s, es = [], [], []
            for hd in range(SB_HPS):
                cols = slice(hd * SB_DH, (hd + 1) * SB_DH)
                l = ms[hd] + zs[hd]
                a = jnp.exp(l + ((ct_ref[hd, r0:] - carry[hd][1][r0:]) - m_upto[hd * nr:(hd + 1) * nr]))
                if causal is not None:
                    a = jnp.where(causal, a, 0.0)
                ls.append(l)
                a_s.append(a)
                es.append(_dot_nt(dy_ref[r0:, cols], v_ref[pl.ds(off, TK), cols]) * a)
            e_upto = _dot(jnp.concatenate(es, axis=0).astype(BF16), upto[:TK])
            out = []
            for hd in range(SB_HPS):
                dq, p_sum, e_sum = carry[hd]
                cols = slice(hd * SB_DH, (hd + 1) * SB_DH)
                e = es[hd]
                dz = e - jnp.exp(ls[hd]) * (e_sum[r0:] + e_upto[hd * nr:(hd + 1) * nr])
                if causal is not None:
                    dz = jnp.where(causal, dz, 0.0)
                dzs = (dz * scale).astype(BF16)
                dkv_ref[0, pl.ds(off, TK), cols] += _dot_tn(dzs, q_ref[r0:, cols])
                dkv_ref[1, pl.ds(off, TK), cols] += _dot_tn(a_s[hd].astype(BF16), dy_ref[r0:, cols])
                out.append((grow(dq, _dot(dzs, k_ref[pl.ds(off, TK), cols])),
                            grow(p_sum, jnp.sum(ms[hd], axis=1, keepdims=True)),
                            grow(e_sum, jnp.sum(e, axis=1, keepdims=True))))
            return tuple(out)

        zero = jnp.zeros((TQ, 1), F32)
        init = tuple((jnp.zeros((TQ, SB_DH), F32), zero, zero) for _ in range(SB_HPS))
        carry = lax.fori_loop(0, i * n_diag, lambda j, c: block(j, c, None), init)
        for d in range(n_diag):
            carry = block(i * n_diag + d, carry, d * TK)
        for hd in range(SB_HPS):
            dq_ref[:, hd * SB_DH:(hd + 1) * SB_DH] = carry[hd][0].astype(BF16)

    return _call(
        body, (qkv, qkv, qkv, dy, ctot), grid=(g, t // TQ), name=name, comm=comm,
        in_specs=[q_spec, k_spec, v_spec, q_spec, ct_spec],
        out_specs=[q_spec, acc_spec],
        out_shape=[jax.ShapeDtypeStruct((t, D), BF16), jax.ShapeDtypeStruct((2, t, D), F32)],
        scratch_shapes=[])


def _gate_specs():
    return [pl.BlockSpec((TM, D), lambda i: (i, 3)), pl.BlockSpec((TM, D), lambda i: (i, 4))]


def _mix_pairs(mix_hbm, dsts):
    pairs = []
    for index, dst in enumerate(dsts):
        pairs += _square_pairs(mix_hbm, index, dst)
    return pairs


def _mix_out_fwd(yc, ysb, pcg, b_gate, h, w_mix, name):
    t = h.shape[0]

    def body(yc_ref, ysb_ref, gc_ref, gs_ref, b_ref, h_ref, mix_hbm,
             a_ref, b_out_ref, mg_ref, h2_ref, wc_v, wa_v, wo_v, sems):
        _load_resident(pl.program_id(0), _mix_pairs(mix_hbm, (wc_v, wa_v, wo_v)), sems)
        a = _dot(yc_ref[...], wc_v[...])
        b = _dot(ysb_ref[...], wa_v[...])
        merged = (_sigmoid(gc_ref[...] + b_ref[:, :D]) * a + _sigmoid(gs_ref[...] + b_ref[:, D:]) * b).astype(BF16)
        a_ref[...] = a
        b_out_ref[...] = b
        mg_ref[...] = merged
        h2_ref[...] = h_ref[...] + _dot(merged, wo_v[...])

    return pl.pallas_call(
        body, grid=(t // TM,), name=name,
        in_specs=[_row_spec(TM, D), _row_spec(TM, D)] + _gate_specs()
                 + [_const_spec((1, 2 * D)), _row_spec(TM, D), _ANY],
        out_specs=[_row_spec(TM, D)] * 4,
        out_shape=[jax.ShapeDtypeStruct((t, D), F32), jax.ShapeDtypeStruct((t, D), F32),
                   jax.ShapeDtypeStruct((t, D), BF16), jax.ShapeDtypeStruct((t, D), F32)],
        scratch_shapes=[pltpu.VMEM((D, D), BF16)] * 3 + [pltpu.SemaphoreType.DMA((3 * N_DEV,))],
        compiler_params=_cparams(),
    )(yc, ysb, pcg, pcg, b_gate, h, w_mix)


def _mix_out_bwd(dh2, a, b, pcg, b_gate, w_mix, name, comm=None):
    t = dh2.shape[0]

    def body(dh_ref, a_ref, b_ref, gc_ref, gs_ref, bias_ref, mix_hbm,
             dhb_ref, da_ref, db_ref, dgp_ref, dyc_ref, dysb_ref, dbias_ref, wc_v, wa_v, wo_v, sems):
        step = pl.program_id(0)
        _load_resident(step, _mix_pairs(mix_hbm, (wc_v, wa_v, wo_v)), sems)
        dhb = dh_ref[...].astype(BF16)
        dhb_ref[...] = dhb
        dm = _dot_nt(dhb, wo_v[...])
        gc = _sigmoid(gc_ref[...] + bias_ref[:, :D])
        gs = _sigmoid(gs_ref[...] + bias_ref[:, D:])
        da = (dm * gc).astype(BF16)
        db = (dm * gs).astype(BF16)
        da_ref[...] = da
        db_ref[...] = db
        dgc = dm * a_ref[...] * (gc * (1.0 - gc))
        dgs = dm * b_ref[...] * (gs * (1.0 - gs))
        dgp_ref[0] = dgc.astype(BF16)
        dgp_ref[1] = dgs.astype(BF16)
        _accumulate(dbias_ref.at[:, :D], step, jnp.sum(dgc, axis=0, keepdims=True))
        _accumulate(dbias_ref.at[:, D:], step, jnp.sum(dgs, axis=0, keepdims=True))
        dyc_ref[...] = _dot_nt(da, wc_v[...])
        dysb_ref[...] = _dot_nt(db, wa_v[...]).astype(BF16)

    return _call(
        body, (dh2, a, b, pcg, pcg, b_gate, w_mix), grid=(t // TM,), name=name, comm=comm,
        in_specs=[_row_spec(TM, D)] * 3 + _gate_specs() + [_const_spec((1, 2 * D)), _ANY],
        out_specs=[_row_spec(TM, D)] * 3 + [_blk_row_spec(2, TM, D), _row_spec(TM, D), _row_spec(TM, D),
                                            _const_spec((1, 2 * D))],
        out_shape=[jax.ShapeDtypeStruct((t, D), BF16)] * 3
                  + [jax.ShapeDtypeStruct((2, t, D), BF16), jax.ShapeDtypeStruct((t, D), F32),
                     jax.ShapeDtypeStruct((t, D), BF16), jax.ShapeDtypeStruct((1, 2 * D), F32)],
        scratch_shapes=[pltpu.VMEM((D, D), BF16)] * 3 + [pltpu.SemaphoreType.DMA((3 * N_DEV,))])


def _inproj_bwd(dconv, dq, dkv, dgp, w_in, h, g, dh_res, name, comm=None):
    t = h.shape[0]

    def body(dc_ref, dq_ref, dkv_ref, dgp_ref, w_hbm, h_ref, g_ref, dres_ref, dh_ref, dg_ref, w_v, sems):
        step = pl.program_id(0)
        _load_resident(step, [(w_hbm, w_v)], sems)
        du = _dot_nt(dq_ref[...], w_v[3])
        for k in range(3):
            du = du + _dot_nt(dc_ref[k], w_v[k])
        for k in range(2):
            du = du + _dot_nt(dkv_ref[k].astype(BF16), w_v[4 + k]) + _dot_nt(dgp_ref[k], w_v[6 + k])
        dx, dg = _rms_bwd_tile(h_ref[...], g_ref[...], du)
        dh_ref[...] = dres_ref[...] + dx
        _accumulate(dg_ref, step, dg)

    return _call(
        body, (dconv, dq, dkv, dgp, w_in, h, g, dh_res), grid=(t // TM,), name=name, comm=comm,
        in_specs=[_blk_row_spec(3, TM, D), _row_spec(TM, D), _blk_row_spec(2, TM, D), _blk_row_spec(2, TM, D), _ANY,
                  _row_spec(TM, D), _const_spec((1, D)), _row_spec(TM, D)],
        out_specs=[_row_spec(TM, D), _const_spec((1, D))],
        out_shape=[jax.ShapeDtypeStruct((t, D), F32), jax.ShapeDtypeStruct((1, D), F32)],
        scratch_shapes=[pltpu.VMEM((N_DEV, D, D), BF16), pltpu.SemaphoreType.DMA((1,))])


def _memkv_fwd(mem, g, w_ckv, name):
    m = mem.shape[0]

    def body(mem_ref, g_ref, w_ref, mn_ref, kv_ref):
        mn = _rms_fwd_tile(mem_ref[...], g_ref[...]).astype(BF16)
        mn_ref[...] = mn
        for j in range(N_DEV):
            kv_ref[j] = _dot(mn, w_ref[j]).astype(BF16)

    return pl.pallas_call(
        body, grid=(1,), name=name,
        in_specs=[_const_spec((m, D)), _const_spec((1, D)), _const_spec((N_DEV, D, X_DH))],
        out_specs=[_const_spec((m, D)), _const_spec((N_DEV, m, X_DH))],
        out_shape=[jax.ShapeDtypeStruct((m, D), BF16), jax.ShapeDtypeStruct((N_DEV, m, X_DH), BF16)],
        compiler_params=_cparams(),
    )(mem, g, w_ckv)


def _memkv_bwd(dkv, mem, g, w_ckv, name):
    m = mem.shape[0]

    def body(dkv_ref, mem_ref, g_ref, w_ref, dg_ref):
        dmn = jnp.zeros((m, D), F32)
        for j in range(N_DEV):
            dmn = dmn + _dot_nt(dkv_ref[j].astype(BF16), w_ref[j])
        _, dg = _rms_bwd_tile(mem_ref[...], g_ref[...], dmn)
        dg_ref[...] = dg

    return pl.pallas_call(
        body, grid=(1,), name=name,
        in_specs=[_const_spec((N_DEV, m, X_DH)), _const_spec((m, D)), _const_spec((1, D)),
                  _const_spec((N_DEV, D, X_DH))],
        out_specs=_const_spec((1, D)),
        out_shape=jax.ShapeDtypeStruct((1, D), F32),
        compiler_params=_cparams(),
    )(dkv, mem, g, w_ckv)


def _softmax_rows(s):
    e = jnp.exp(s - jnp.max(s, axis=-1, keepdims=True))
    return e / jnp.sum(e, axis=-1, keepdims=True)


def _cross_pairs(cross_hbm, wq_v, wo_v):
    return _square_pairs(cross_hbm, 0, wq_v) + _square_pairs(cross_hbm, 1, wo_v)


def _cross_fwd(h, g, kv, w_cross, name):
    t = h.shape[0]
    m = kv.shape[1]
    scale = X_DH ** -0.5

    def body(h_ref, g_ref, kv_ref, cross_hbm, hn_ref, qx_ref, o_ref, h3_ref, wq_v, wo_v, sems):
        _load_resident(pl.program_id(0), _cross_pairs(cross_hbm, wq_v, wo_v), sems)
        ht = h_ref[...]
        hn = _rms_fwd_tile(ht, g_ref[...]).astype(BF16)
        hn_ref[...] = hn
        qx = _dot(hn, wq_v[...]).astype(BF16)
        qx_ref[...] = qx
        for hd in range(X_H):
            lo, hi = hd * X_DH, (hd + 1) * X_DH
            p = _softmax_rows(_dot_nt(qx[:, lo:hi], kv_ref[hd]) * scale)
            o_ref[:, lo:hi] = _dot(p.astype(BF16), kv_ref[X_H + hd]).astype(BF16)
        h3_ref[...] = ht + _dot(o_ref[...], wo_v[...])

    return pl.pallas_call(
        body, grid=(t // TM,), name=name,
        in_specs=[_row_spec(TM, D), _const_spec((1, D)), _const_spec((N_DEV, m, X_DH)), _ANY],
        out_specs=[_row_spec(TM, D)] * 4,
        out_shape=[jax.ShapeDtypeStruct((t, D), BF16)] * 3 + [jax.ShapeDtypeStruct((t, D), F32)],
        scratch_shapes=[pltpu.VMEM((D, D), BF16)] * 2 + [pltpu.SemaphoreType.DMA((2 * N_DEV,))],
        compiler_params=_cparams(),
    )(h, g, kv, w_cross)


def _cross_bwd(dh3, h, g, qx, kv, w_cross, name, comm=None):
    t = h.shape[0]
    m = kv.shape[1]
    scale = X_DH ** -0.5

    def body(dh_ref, h_ref, g_ref, qx_ref, kv_ref, cross_hbm,
             dhb_ref, dqx_ref, dkv_ref, dh2_ref, dg_ref, wq_v, wo_v, sems):
        step = pl.program_id(0)
        _load_resident(step, _cross_pairs(cross_hbm, wq_v, wo_v), sems)

        @pl.when(step == 0)
        def _():
            dkv_ref[...] = jnp.zeros_like(dkv_ref)

        dht = dh_ref[...]
        dhb = dht.astype(BF16)
        dhb_ref[...] = dhb
        do = _dot_nt(dhb, wo_v[...]).astype(BF16)
        for hd in range(X_H):
            lo, hi = hd * X_DH, (hd + 1) * X_DH
            qh = qx_ref[:, lo:hi]
            kh = kv_ref[hd]
            p = _softmax_rows(_dot_nt(qh, kh) * scale)
            doh = do[:, lo:hi]
            dp = _dot_nt(doh, kv_ref[X_H + hd])
            ds = (p * (dp - jnp.sum(dp * p, axis=-1, keepdims=True)) * scale).astype(BF16)
            dqx_ref[:, lo:hi] = _dot(ds, kh).astype(BF16)
            dkv_ref[hd] += _dot_tn(ds, qh)
            dkv_ref[X_H + hd] += _dot_tn(p.astype(BF16), doh)
        dhn = _dot_nt(dqx_ref[...], wq_v[...])
        dx, dg = _rms_bwd_tile(h_ref[...], g_ref[...], dhn)
        dh2_ref[...] = dht + dx
        _accumulate(dg_ref, step, dg)

    return _call(
        body, (dh3, h, g, qx, kv, w_cross), grid=(t // TM,), name=name, comm=comm,
        in_specs=[_row_spec(TM, D), _row_spec(TM, D), _const_spec((1, D)), _row_spec(TM, D),
                  _const_spec((N_DEV, m, X_DH)), _ANY],
        out_specs=[_row_spec(TM, D), _row_spec(TM, D), _const_spec((N_DEV, m, X_DH)), _row_spec(TM, D),
                   _const_spec((1, D))],
        out_shape=[jax.ShapeDtypeStruct((t, D), BF16), jax.ShapeDtypeStruct((t, D), BF16),
                   jax.ShapeDtypeStruct((N_DEV, m, X_DH), F32), jax.ShapeDtypeStruct((t, D), F32),
                   jax.ShapeDtypeStruct((1, D), F32)],
        scratch_shapes=[pltpu.VMEM((D, D), BF16)] * 2 + [pltpu.SemaphoreType.DMA((2 * N_DEV,))])


def _adamw(w, parts, m, v, name, row_block=0, token=None):
    r, c = w.shape
    n = parts.shape[0]
    tr = _pick_tile(r, (256, 352, 128))
    off = row_block * (r // tr)

    def body(*refs):
        if token is None:
            _adamw_update(None, *refs)
        else:
            _adamw_update(refs[4], *refs[:4], *refs[5:])

    spec = _row_spec(tr, c)
    in_specs = [spec, pl.BlockSpec((n, tr, c), lambda i: (0, i + off, 0)), spec, spec]
    operands = (w, parts, m, v)
    if token is not None:
        in_specs.append(_const_spec(token.shape))
        operands += (token,)
    return pl.pallas_call(
        body, grid=(r // tr,), name=name, in_specs=in_specs, out_specs=[spec] * 4,
        out_shape=[jax.ShapeDtypeStruct((r, c), F32)] * 4,
        compiler_params=_cparams(),
    )(*operands)


def _adamw_update(tok_ref, w_ref, p_ref, m_ref, v_ref, g_ref, d_ref, nm_ref, nv_ref):
    gt = p_ref[0].astype(F32)
    for k in range(1, p_ref.shape[0]):
        gt = gt + p_ref[k].astype(F32)
    if tok_ref is not None:
        gt = gt + tok_ref[0:1, 0:1]
    _adamw_apply(gt, w_ref, m_ref, v_ref, g_ref, d_ref, nm_ref, nv_ref)


def _adamw_own(w, land, own, chip, m, v, name, row_block=0, token=None):
    r, c = w.shape
    tr = _pick_tile(r, (256, 352, 128))
    off = row_block * (r // tr)

    def body(chip_ref, w_ref, land_ref, own_ref, m_ref, v_ref, *rest):
        mine = own_ref[0].astype(F32)
        gt = jnp.where(chip_ref[0] == 0, mine, land_ref[0].astype(F32))
        for k in range(1, N_CHIP):
            gt = gt + jnp.where(chip_ref[0] == k, mine, land_ref[k].astype(F32))
        if token is not None:
            gt = gt + rest[0][0:1, 0:1]
        _adamw_apply(gt, w_ref, m_ref, v_ref, *rest[-4:])

    spec = pl.BlockSpec((tr, c), lambda i, chip_ref: (i, 0))
    in_specs = [spec, pl.BlockSpec((N_CHIP, tr, c), lambda i, chip_ref: (0, i + off, 0)),
                pl.BlockSpec((1, tr, c), lambda i, chip_ref: (chip_ref[0], i + off, 0)), spec, spec]
    operands = (chip, w, land, own, m, v)
    if token is not None:
        in_specs.append(pl.BlockSpec(token.shape, lambda i, chip_ref: (0, 0)))
        operands += (token,)
    return pl.pallas_call(
        body, name=name,
        grid_spec=pltpu.PrefetchScalarGridSpec(
            num_scalar_prefetch=1, grid=(r // tr,), in_specs=in_specs, out_specs=[spec] * 4),
        out_shape=[jax.ShapeDtypeStruct((r, c), F32)] * 4,
        compiler_params=_cparams(),
    )(*operands)


def _adamw_apply(gt, w_ref, m_ref, v_ref, g_ref, d_ref, nm_ref, nv_ref):
    g_ref[...] = gt
    nm = ADAM_B1 * m_ref[...] + (1.0 - ADAM_B1) * gt
    nv = ADAM_B2 * v_ref[...] + (1.0 - ADAM_B2) * jnp.square(gt)
    m_hat = nm / (1.0 - ADAM_B1 ** ADAM_STEP)
    v_hat = nv / (1.0 - ADAM_B2 ** ADAM_STEP)
    d_ref[...] = -ADAM_LR * (m_hat / (jnp.sqrt(v_hat) + ADAM_EPS) + ADAM_WD * w_ref[...])
    nm_ref[...] = nm
    nv_ref[...] = nv


def _mesh_pos():
    return lax.axis_index("x"), lax.axis_index("y"), lax.axis_index("c")


def _no_round(in_refs, out_refs, sems):
    pass


def _run_exchange(comm, name):
    c_in, c_out = len(comm.inputs), len(comm.out_shapes)

    def body(*refs):
        cins, couts, sems = refs[:c_in], refs[c_in:c_in + c_out], refs[c_in + c_out:]
        comm.start(cins, couts, sems)
        comm.middle(cins, couts, sems)
        comm.finish(cins, couts, sems)

    return list(pl.pallas_call(
        body, name=name, out_shape=list(comm.out_shapes),
        in_specs=[_ANY] * c_in, out_specs=[_ANY] * c_out, scratch_shapes=list(comm.sem_shapes),
    )(*comm.inputs))


def _gather_exchange(shards):
    n_arr = len(shards)

    def plan(x_refs, out_refs, sems):
        send_sems, recv_sems, local_sems = sems
        x, y, c = _mesh_pos()
        me, sibling = (x, y, c), (x, y, 1 - c)
        xn, yn, diag = (1 - x, y), (x, 1 - y), (1 - x, 1 - y)

        def slot(a, px, py, pc, half=None):
            ref = out_refs[a].at[4 * px + 2 * py + pc]
            if half is None:
                return ref
            rows = shards[a].shape[0] // 2
            return ref.at[half * rows:(half + 1) * rows]

        def copy(a, k, block, to, half=None, src=None):
            dst = slot(a, *block, half)
            return pltpu.make_async_remote_copy(
                src_ref=dst if src is None else src, dst_ref=dst,
                send_sem=send_sems.at[a, k], recv_sem=recv_sems.at[a, k],
                device_id=to, device_id_type=pl.DeviceIdType.MESH)

        return types.SimpleNamespace(
            me=me, sibling=sibling, xn=xn, yn=yn, diag=diag, c=c, copy=copy,
            mine=[pltpu.make_async_copy(x_refs[a], slot(a, *me), local_sems.at[a]) for a in range(n_arr)],
            first=[cp for a in range(n_arr) for cp in (
                copy(a, 0, me, sibling, src=x_refs[a]), copy(a, 1, me, (*xn, c), src=x_refs[a]),
                copy(a, 2, me, (*yn, c), src=x_refs[a]))],
            second=lambda a: (copy(a, 3, (*xn, c), (*yn, c), half=0), copy(a, 5, (*xn, c), sibling),
                              copy(a, 4, (*yn, c), (*xn, c), half=1), copy(a, 6, (*yn, c), sibling)),
            third=lambda a: (copy(a, 7, (*diag, c), sibling, half=0), copy(a, 8, (*diag, c), sibling, half=1)))

    def start(x_refs, out_refs, sems):
        p = plan(x_refs, out_refs, sems)
        for cp in p.mine + p.first:
            cp.start()

    def middle(x_refs, out_refs, sems):
        p = plan(x_refs, out_refs, sems)
        for a in range(n_arr):
            to_yn, x_to_sib, to_xn, y_to_sib = p.second(a)
            p.copy(a, 1, (*p.xn, p.c), p.me).wait_recv()
            to_yn.start()
            x_to_sib.start()
            p.copy(a, 2, (*p.yn, p.c), p.me).wait_recv()
            to_xn.start()
            y_to_sib.start()

    def finish(x_refs, out_refs, sems):
        p = plan(x_refs, out_refs, sems)
        for a in range(n_arr):
            half0_to_sib, half1_to_sib = p.third(a)
            p.copy(a, 3, (*p.diag, p.c), p.me, half=0).wait_recv()
            half0_to_sib.start()
            p.copy(a, 4, (*p.diag, p.c), p.me, half=1).wait_recv()
            half1_to_sib.start()
        other = 1 - p.c
        for a in range(n_arr):
            p.copy(a, 0, p.sibling, p.me).wait_recv()
            p.copy(a, 5, (*p.xn, other), p.me).wait_recv()
            p.copy(a, 6, (*p.yn, other), p.me).wait_recv()
            p.copy(a, 7, (*p.diag, other), p.me, half=0).wait_recv()
            p.copy(a, 8, (*p.diag, other), p.me, half=1).wait_recv()
        for cp in p.first:
            cp.wait_send()
        for a in range(n_arr):
            for cp in p.second(a) + p.third(a):
                cp.wait_send()
        for cp in p.mine:
            cp.wait()

    return types.SimpleNamespace(
        inputs=list(shards), start=start, middle=middle, finish=finish,
        out_shapes=[jax.ShapeDtypeStruct((N_DEV,) + s.shape, s.dtype) for s in shards],
        sem_shapes=[pltpu.SemaphoreType.DMA((n_arr, 9)), pltpu.SemaphoreType.DMA((n_arr, 9)),
                    pltpu.SemaphoreType.DMA((n_arr,))])


def _pair_exchange(grads):
    n_arr = len(grads)

    def plan(g_refs, land_refs, sems):
        send_sems, recv_sems = sems
        x, y, c = _mesh_pos()
        return [pltpu.make_async_remote_copy(
            src_ref=g_refs[a].at[2 * k + 1 - c], dst_ref=land_refs[a].at[k],
            send_sem=send_sems.at[a, k], recv_sem=recv_sems.at[a, k],
            device_id=(x, y, 1 - c), device_id_type=pl.DeviceIdType.MESH)
            for a in range(n_arr) for k in range(N_CHIP)]

    def start(g_refs, land_refs, sems):
        for cp in plan(g_refs, land_refs, sems):
            cp.start()

    def finish(g_refs, land_refs, sems):
        for cp in plan(g_refs, land_refs, sems):
            cp.wait()

    return types.SimpleNamespace(
        inputs=list(grads), start=start, middle=_no_round, finish=finish,
        out_shapes=[jax.ShapeDtypeStruct((N_CHIP,) + g.shape[1:], g.dtype) for g in grads],
        sem_shapes=[pltpu.SemaphoreType.DMA((n_arr, N_CHIP)), pltpu.SemaphoreType.DMA((n_arr, N_CHIP))])


def _chip_exchange(parts):
    n_arr = len(parts)

    def plan(p_refs, land_refs, sems):
        send_sems, recv_sems, local_sems = sems
        x, y, c = _mesh_pos()
        my_chip = 2 * x + y
        chips = [(1 - x, y), (x, 1 - y), (1 - x, 1 - y)]
        local = [pltpu.make_async_copy(p_refs[a].at[my_chip], land_refs[a].at[my_chip], local_sems.at[a])
                 for a in range(n_arr)]

        def copy(a, k, src_slot, dst_slot, px, py):
            return pltpu.make_async_remote_copy(
                src_ref=p_refs[a].at[src_slot], dst_ref=land_refs[a].at[dst_slot],
                send_sem=send_sems.at[a, k], recv_sem=recv_sems.at[a, k],
                device_id=(px, py, c), device_id_type=pl.DeviceIdType.MESH)

        sends = [copy(a, k, 2 * px + py, my_chip, px, py) for a in range(n_arr) for k, (px, py) in enumerate(chips)]
        arrivals = [copy(a, k, my_chip, 2 * px + py, px, py) for a in range(n_arr)
                    for k, (px, py) in enumerate(chips)]
        return local, sends, arrivals

    def start(p_refs, land_refs, sems):
        local, sends, _ = plan(p_refs, land_refs, sems)
        for cp in local + sends:
            cp.start()

    def finish(p_refs, land_refs, sems):
        local, sends, arrivals = plan(p_refs, land_refs, sems)
        for cp in arrivals:
            cp.wait_recv()
        for cp in sends:
            cp.wait_send()
        for cp in local:
            cp.wait()

    return types.SimpleNamespace(
        inputs=list(parts), start=start, middle=_no_round, finish=finish,
        out_shapes=[jax.ShapeDtypeStruct(p.shape, p.dtype) for p in parts],
        sem_shapes=[pltpu.SemaphoreType.DMA((n_arr, 3)), pltpu.SemaphoreType.DMA((n_arr, 3)),
                    pltpu.SemaphoreType.DMA((n_arr,))])


_HBM = pl.BlockSpec(memory_space=pltpu.HBM)
_SEM = pl.BlockSpec(memory_space=pltpu.SEMAPHORE)
_DATAFLOW = pltpu.SideEffectType.DATAFLOW_SIDE_EFFECTING


def _chip_copies(p_refs, land_refs, send_sems, recv_sems):
    x, y, c = _mesh_pos()
    my_chip = 2 * x + y
    chips = [(1 - x, y), (x, 1 - y), (1 - x, 1 - y)]
    return [pltpu.make_async_remote_copy(
        src_ref=p_refs[a].at[2 * px + py], dst_ref=land_refs[a].at[my_chip],
        send_sem=send_sems[3 * a + k], recv_sem=recv_sems[3 * a + k],
        device_id=(px, py, c), device_id_type=pl.DeviceIdType.MESH)
        for a in range(len(p_refs)) for k, (px, py) in enumerate(chips)]


def _chip_exchange_begin(parts, name):
    n_arr = len(parts)
    n_buf, n_copy = 2 * n_arr, 3 * n_arr
    lands = [lax.empty(p.shape, p.dtype) for p in parts]

    def body(*refs):
        p_refs, land_refs = refs[:n_arr], refs[n_arr:n_buf]
        send_sems, recv_sems, token = refs[n_buf:n_buf + n_copy], refs[n_buf + n_copy:n_buf + 2 * n_copy], refs[-1]
        for cp in _chip_copies(p_refs, land_refs, send_sems, recv_sems):
            cp.start()
        token[...] = jnp.zeros_like(token)

    bufs = list(parts) + list(lands)
    outs = pl.pallas_call(
        body, name=name,
        out_shape=(*[pltpu.SemaphoreType.DMA(())] * (2 * n_copy), *[pltpu.HBM(b.shape, b.dtype) for b in bufs],
                   jax.ShapeDtypeStruct((8, 128), F32)),
        in_specs=[_HBM] * n_buf,
        out_specs=(*[_SEM] * (2 * n_copy), *[_HBM] * n_buf, pl.BlockSpec(memory_space=pltpu.VMEM)),
        input_output_aliases={i: 2 * n_copy + i for i in range(n_buf)},
        compiler_params=pltpu.CompilerParams(has_side_effects=_DATAFLOW),
    )(*[pltpu.with_memory_space_constraint(b, pltpu.HBM) for b in bufs])
    sems = list(outs[:2 * n_copy])
    thru = list(outs[2 * n_copy:2 * n_copy + n_buf])
    return types.SimpleNamespace(send_sems=sems[:n_copy], recv_sems=sems[n_copy:], parts=thru[:n_arr],
                                 lands=thru[n_arr:], token=outs[-1])


def _chip_exchange_end(flight, after, name):
    send_sems, recv_sems, parts, lands = flight.send_sems, flight.recv_sems, flight.parts, flight.lands
    n_arr = len(parts)
    n_buf, n_copy = 2 * n_arr, 3 * n_arr

    def body(*refs):
        p_refs, land_refs = refs[:n_arr], refs[n_arr:n_buf]
        sems = refs[n_buf:n_buf + 2 * n_copy]
        for cp in _chip_copies(p_refs, land_refs, sems[:n_copy], sems[n_copy:]):
            cp.wait_send()
            cp.wait_recv()

    bufs = list(parts) + list(lands)
    outs = pl.pallas_call(
        body, name=name, out_shape=tuple(pltpu.HBM(b.shape, b.dtype) for b in bufs),
        in_specs=[_HBM] * n_buf + [_SEM] * (2 * n_copy) + [_ANY], out_specs=tuple([_HBM] * n_buf),
        input_output_aliases={i: i for i in range(n_buf)},
        compiler_params=pltpu.CompilerParams(has_side_effects=_DATAFLOW),
    )(*bufs, *send_sems, *recv_sems, after)
    return list(outs[:n_arr]), list(outs[n_arr:])


def _row_tile(r, cap=640):
    best = None
    for cand in range(16, min(r, cap) + 1, 16):
        if r % cand == 0:
            best = cand
    return best if best is not None else r


def _pair_sum(g, landed, core, name):
    _, r, c_dim = g.shape
    tr = _row_tile(r)

    def body(core_ref, mine_ref, theirs_ref, o_ref):
        o_ref[0] = (mine_ref[0].astype(F32) + theirs_ref[0].astype(F32)).astype(o_ref.dtype)

    return pl.pallas_call(
        body, name=name,
        grid_spec=pltpu.PrefetchScalarGridSpec(
            num_scalar_prefetch=1, grid=(N_CHIP, r // tr),
            in_specs=[pl.BlockSpec((1, tr, c_dim), lambda k, i, core_ref: (2 * k + core_ref[0], i, 0)),
                      pl.BlockSpec((1, tr, c_dim), lambda k, i, core_ref: (k, i, 0))],
            out_specs=pl.BlockSpec((1, tr, c_dim), lambda k, i, core_ref: (k, i, 0))),
        out_shape=jax.ShapeDtypeStruct((N_CHIP, r, c_dim), g.dtype),
        compiler_params=_cparams(2),
    )(core, g, landed)


def _sum_slots(parts, name):
    n, r, c_dim = parts.shape
    tr = _row_tile(r)

    def body(p_ref, o_ref):
        acc = p_ref[0].astype(F32)
        for k in range(1, n):
            acc = acc + p_ref[k].astype(F32)
        o_ref[...] = acc

    return pl.pallas_call(
        body, grid=(r // tr,), name=name,
        in_specs=[pl.BlockSpec((n, tr, c_dim), lambda i: (0, i, 0))],
        out_specs=_row_spec(tr, c_dim),
        out_shape=jax.ShapeDtypeStruct((r, c_dim), F32),
        compiler_params=_cparams(),
    )(parts)


GAINS = ("g_ffn1", "g_mix", "g_cross", "g_mem", "g_ffn2", "g_final")
SMALL = GAINS + ("b_gate", "conv_w")
SMALL_R = 16
LOSS_ROW = 11
WEIGHT_ORDER = ("g_ffn1", "w_ffn1_gu", "w_ffn1_down", "g_mix", "w_in", "b_gate", "conv_w", "w_conv_out",
                "w_attn_out", "w_o", "g_cross", "g_mem", "w_cq", "w_ckv", "w_co", "g_ffn2", "w_ffn2_gu",
                "w_ffn2_down", "g_final")
GU_NAMES = ("w_ffn1_gu", "w_ffn2_gu")


def _pack_small(vals, conv_rows):
    rows = [vals[n].reshape(1, D) for n in GAINS] + [vals["b_gate"].reshape(2, D), conv_rows.reshape(CONV_K, D)]
    used = len(GAINS) + 2 + CONV_K
    return jnp.concatenate(rows + [jnp.zeros((SMALL_R - used, D), F32)], axis=0)


def _unpack_small(buf):
    out = {n: buf[k] for k, n in enumerate(GAINS)}
    out["b_gate"] = buf[6:8].reshape(2 * D)
    out["conv_w"] = buf[8:8 + CONV_K]
    return out


def _exchange_shards(wts):
    out = {n: jnp.pad(wts[n].T.astype(BF16), ((0, FF_PAD - FF_BLK), (0, 0))) for n in GU_NAMES}
    for n in ("w_ckv", "w_in", "w_ffn1_down", "w_ffn2_down"):
        out[n] = wts[n].astype(BF16)
    out["mix"] = jnp.concatenate([wts[n].astype(BF16) for n in MIX_MATS], axis=0)
    out["cross"] = jnp.concatenate([wts[n].astype(BF16) for n in CROSS_MATS], axis=0)
    return out


def _by_device(dw):
    return dw.reshape(N_DEV, SQ_ROWS, D)


def _reduce_group(grads, landed, core, names):
    return [_pair_sum(g, l, core, "grads_pair_sum_" + n) for g, l, n in zip(grads, landed, names)]


def _step(x, mem, target, sh, conv_pad, gains, b_gate, core):
    wg1, wd1, conv_all = _run_exchange(_gather_exchange([sh["w_ffn1_gu"], sh["w_ffn1_down"], conv_pad]), "gather_ffn1")
    conv_w = conv_all[:, :CONV_K, :].transpose(1, 0, 2).reshape(CONV_K, D)
    (n1, gate1, up1, act1, h1), (w_in,) = _ffn_fwd(
        x, gains["g_ffn1"], wg1, wd1, "ffn1_fwd", comm=_gather_exchange([sh["w_in"]]))
    (u, pcg, qkv), (w_mix,) = _inproj_fwd(h1, gains["g_mix"], w_in, "inproj_fwd", comm=_gather_exchange([sh["mix"]]))
    yc = _conv_fwd(pcg, conv_w, "conv_fwd")
    (ysb, ctot), (w_cross, w_ckv, wg2, wd2) = _sb_fwd(
        qkv, "sb_fwd", comm=_gather_exchange([sh["cross"], sh["w_ckv"], sh["w_ffn2_gu"], sh["w_ffn2_down"]]))
    a_mix, b_mix, merged, h2 = _mix_out_fwd(yc, ysb, pcg, b_gate, h1, w_mix, "mix_out_fwd")
    mn, kv = _memkv_fwd(mem, gains["g_mem"], w_ckv, "memkv_fwd")
    hn, qx, o_x, h3 = _cross_fwd(h2, gains["g_cross"], kv, w_cross, "cross_fwd")
    (n4, gate2, up2, act2, dh4, loss, dg_final), _ = _ffn_fwd(h3, gains["g_ffn2"], wg2, wd2, "ffn2_fwd",
                                                              head=(gains["g_final"], target))

    gs = {"g_final": dg_final}
    (dgu2, dh4b, dh3, gs["g_ffn2"]), _ = _ffn_bwd(dh4, h3, gains["g_ffn2"], gate2, up2, wg2, wd2, "ffn2_bwd")
    grads_a = [_mm_tn_rows(dgu2, n4, FF_PAD, "dw_ffn2_gu"),
               _mm_tn_rows(act2, dh4b, FF_BLK, "dw_ffn2_down").reshape(N_DEV, DOWN_ROWS, D)]
    names_a = ["w_ffn2_gu", "w_ffn2_down"]
    (dh3b, dqx, dkv, dh2, gs["g_cross"]), landed_a = _cross_bwd(
        dh3, h2, gains["g_cross"], qx, kv, w_cross, "cross_bwd", comm=_pair_exchange(grads_a))
    sums_a = _reduce_group(grads_a, landed_a, core, names_a)
    grads_b = [_mm_tn_cols(mn, dkv, "dw_ckv"),
               jnp.concatenate([_by_device(_mm_tn(hn, dqx, "dw_cq")), _by_device(_mm_tn(o_x, dh3b, "dw_co"))], axis=1)]
    names_b = ["w_ckv", "cross"]
    gs["g_mem"] = _memkv_bwd(dkv, mem, gains["g_mem"], w_ckv, "memkv_bwd")
    (dh2b, da_mix, db_mix, dgp, dyc, dysb, gs["b_gate"]), landed_b = _mix_out_bwd(
        dh2, a_mix, b_mix, pcg, b_gate, w_mix, "mix_out_bwd", comm=_pair_exchange(grads_b))
    sums_b = _reduce_group(grads_b, landed_b, core, names_b)
    grads_c = [jnp.concatenate([_by_device(_mm_tn(yc, da_mix, "dw_conv_out")),
                                _by_device(_mm_tn(ysb, db_mix, "dw_attn_out")),
                                _by_device(_mm_tn(merged, dh2b, "dw_o"))], axis=1)]
    landed_c = _run_exchange(_pair_exchange(grads_c), "grads_to_sibling_mix")
    sums_c = _reduce_group(grads_c, landed_c, core, ["mix"])
    flight_abc = _chip_exchange_begin(sums_a + sums_b + sums_c, "grads_to_chips_early_begin")
    (dq, dkv_sb), _ = _sb_bwd(qkv, dysb, ctot + flight_abc.token[0, 0], "sb_bwd")
    dconv, gs["conv_w"] = _conv_bwd(pcg, conv_w, dyc, "conv_bwd")
    grads_d = [jnp.concatenate(
        [_mm_tn_cols(u, dconv, "dw_in_conv"), _mm_tn(u, dq, "dw_in_q")[None], _mm_tn_cols(u, dkv_sb, "dw_in_kv"),
         _mm_tn_cols(u, dgp, "dw_in_gates")], axis=0)]
    (dh1, gs["g_mix"]), landed_d = _inproj_bwd(dconv, dq, dkv_sb, dgp, w_in, h1, gains["g_mix"], dh2, "inproj_bwd",
                                               comm=_pair_exchange(grads_d))
    sums_d = _reduce_group(grads_d, landed_d, core, ["w_in"])
    flight_d = _chip_exchange_begin(sums_d, "grads_to_chips_w_in_begin")
    (dgu1, dh1b, dx, gs["g_ffn1"]), _ = _ffn_bwd(dh1, x, gains["g_ffn1"] + flight_d.token[0, 0], gate1, up1, wg1, wd1,
                                                 "ffn1_bwd")
    grads_e = [_mm_tn_rows(dgu1, n1, FF_PAD, "dw_ffn1_gu"),
               _mm_tn_rows(act1, dh1b, FF_BLK, "dw_ffn1_down").reshape(N_DEV, DOWN_ROWS, D)]
    names_e = ["w_ffn1_gu", "w_ffn1_down"]
    landed_e = _run_exchange(_pair_exchange(grads_e), "grads_to_sibling_ffn1")
    flight_e = _chip_exchange_begin(_reduce_group(grads_e, landed_e, core, names_e), "grads_to_chips_ffn1_begin")
    flights = [(names_a + names_b + ["mix"], flight_abc), (["w_in"], flight_d), (names_e, flight_e)]
    return loss, dx, flights, gs


def kernel(x, mem, g_ffn1, w_ffn1_gu, w_ffn1_down, g_mix, w_in, b_gate, conv_w, w_conv_out, w_attn_out, w_o, g_cross, g_mem, w_cq, w_ckv, w_co, g_ffn2, w_ffn2_gu, w_ffn2_down, g_final, loss_target, m_g_ffn1, m_w_ffn1_gu, m_w_ffn1_down, m_g_mix, m_w_in, m_b_gate, m_conv_w, m_w_conv_out, m_w_attn_out, m_w_o, m_g_cross, m_g_mem, m_w_cq, m_w_ckv, m_w_co, m_g_ffn2, m_w_ffn2_gu, m_w_ffn2_down, m_g_final, v_g_ffn1, v_w_ffn1_gu, v_w_ffn1_down, v_g_mix, v_w_in, v_b_gate, v_conv_w, v_w_conv_out, v_w_attn_out, v_w_o, v_g_cross, v_g_mem, v_w_cq, v_w_ckv, v_w_co, v_g_ffn2, v_w_ffn2_gu, v_w_ffn2_down, v_g_final):
    args = locals()
    wts = {n: args[n] for n in WEIGHT_ORDER}
    mom1 = {n: args["m_" + n] for n in WEIGHT_ORDER}
    mom2 = {n: args["v_" + n] for n in WEIGHT_ORDER}
    cx, cy, cc = _mesh_pos()
    dev = 4 * cx + 2 * cy + cc
    conv_cols = D // N_DEV

    conv_pad = jnp.concatenate([conv_w, jnp.zeros((SMALL_R - CONV_K, conv_cols), F32)], axis=0)
    gains = {n: wts[n].reshape(1, D) for n in GAINS}
    loss8, dx, flights, gs = _step(x[0], mem[0], loss_target[0], _exchange_shards(wts), conv_pad, gains,
                                 b_gate.reshape(1, 2 * D), cc.reshape(1).astype(jnp.int32))

    grads, delta, new_m, new_v = {}, {}, {}, {}

    def operands(n, transposed):
        trio = (wts[n], mom1[n], mom2[n])
        return tuple(a.T for a in trio) if transposed else trio

    def record(n, res, transposed):
        grads[n], delta[n], new_m[n], new_v[n] = [r.T for r in res] if transposed else res

    early = [("w_ffn2_gu", "w_ffn2_gu", 0, True), ("w_ffn2_down", "w_ffn2_down", 0, False),
             ("w_ckv", "w_ckv", 0, False), ("w_in", "w_in", 0, False)]
    early += [(n, "mix", k, False) for k, n in enumerate(MIX_MATS)]
    early += [(n, "cross", k, False) for k, n in enumerate(CROSS_MATS)]
    chip = (2 * cx + cy).reshape(1).astype(jnp.int32)
    (names_early, flight_early), (names_w_in, flight_w_in), (last_names, flight_last) = flights
    token = flight_last.token
    own, land = {}, {}
    for names, flight, tag in ((names_early, flight_early, "early"), (names_w_in, flight_w_in, "w_in")):
        own_parts, landed = _chip_exchange_end(flight, token, "grads_to_chips_%s_end" % tag)
        own.update(zip(names, own_parts))
        land.update(zip(names, landed))
    for n, buf, row_block, transposed in early:
        w, m1, m2 = operands(n, transposed)
        record(n, _adamw_own(w, land[buf], own[buf], chip, m1, m2, "adamw_" + n, row_block, token), transposed)

    after = jnp.concatenate([new_v[n][:1, :1] for n, _, _, _ in early], axis=0)
    own_parts, landed = _chip_exchange_end(flight_last, after, "grads_to_chips_ffn1_end")
    for n, own_n, land_n, transposed in zip(last_names, own_parts, landed, (True, False)):
        w, m1, m2 = operands(n, transposed)
        record(n, _adamw_own(w, land_n, own_n, chip, m1, m2, "adamw_" + n), transposed)

    gs_rows = {n: gs[n] for n in GAINS + ("b_gate",)}
    small_mine = _pack_small(gs_rows, gs["conv_w"][:CONV_K]) + new_v[last_names[-1]][0, 0] * 0.0
    small_mine = small_mine.at[LOSS_ROW, 0].set(loss8[0, 0])
    small_all = _run_exchange(_gather_exchange([small_mine]), "gather_small_grads")[0]
    small_sum = _sum_slots(small_all, "small_grads_sum")
    loss = small_sum[LOSS_ROW, 0]
    grad_small = _unpack_small(small_sum)
    grad_small["conv_w"] = lax.dynamic_slice_in_dim(grad_small["conv_w"], dev * conv_cols, conv_cols, axis=1)
    grads.update(grad_small)

    def small_buf(vals):
        return _pack_small(vals, jnp.concatenate([vals["conv_w"], jnp.zeros((CONV_K, D - conv_cols), F32)], axis=1))

    _, d_s, m_s, v_s = _adamw(small_buf(wts), small_buf(grads)[None], small_buf(mom1), small_buf(mom2), "adamw_small")
    for res, buf in ((delta, d_s), (new_m, m_s), (new_v, v_s)):
        un = _unpack_small(buf)
        for n in GAINS + ("b_gate",):
            res[n] = un[n]
        res["conv_w"] = un["conv_w"][:, :conv_cols]

    return (loss, dx[None], *[grads[n] for n in WEIGHT_ORDER], *[delta[n] for n in WEIGHT_ORDER],
            *[new_m[n] for n in WEIGHT_ORDER], *[new_v[n] for n in WEIGHT_ORDER])
```

```python
import types

import jax
import jax.numpy as jnp
from jax import lax
from jax.experimental import pallas as pl
from jax.experimental.pallas import tpu as pltpu

F32 = jnp.float32
BF16 = jnp.bfloat16

D = 1024
DFF = 2816
SB_H = 8
SB_DH = 128
X_H = 4
X_DH = 256
CONV_K = 3
RMS_EPS = 1e-6
N_DEV = 8
N_CHIP = 4
SQ_ROWS = D // N_DEV

ADAM_LR = 0.001
ADAM_B1 = 0.9
ADAM_B2 = 0.999
ADAM_EPS = 1e-08
ADAM_WD = 0.01
ADAM_STEP = 10

TM = 256
TQ = 512
TK = 256
SB_HPS = 2
VMEM_LIMIT = 56 << 20

FF_BLK = DFF // 4
FF_PAD = 768
FF_SUB = 256
DOWN_ROWS = DFF // N_DEV

MIX_MATS = ("w_conv_out", "w_attn_out", "w_o")
CROSS_MATS = ("w_cq", "w_co")

_ANY = pl.BlockSpec(memory_space=pl.ANY)


def _cparams(n_axes=1):
    return pltpu.CompilerParams(
        dimension_semantics=("arbitrary",) * n_axes, vmem_limit_bytes=VMEM_LIMIT)


def _row_spec(tm, n):
    return pl.BlockSpec((tm, n), lambda i: (i, 0))


def _blk_row_spec(nb, tm, n):
    return pl.BlockSpec((nb, tm, n), lambda i: (0, i, 0))


def _const_spec(shape):
    zeros = (0,) * len(shape)
    return pl.BlockSpec(shape, lambda i: zeros)


def _dot(a, b):
    return jnp.dot(a, b, preferred_element_type=F32)


def _dot_nt(a, b):
    return lax.dot_general(a, b, (((1,), (1,)), ((), ())), preferred_element_type=F32)


def _dot_tn(a, b):
    return lax.dot_general(a, b, (((0,), (0,)), ((), ())), preferred_element_type=F32)


def _sigmoid(x):
    return 1.0 / (1.0 + jnp.exp(-x))


def _call(body, operands, *, grid, in_specs, out_specs, out_shape, scratch_shapes, name, comm=None):
    n_in, n_out, n_sc = len(in_specs), len(out_specs), len(scratch_shapes)
    if comm is None:
        outs = pl.pallas_call(
            body, grid=grid, name=name, in_specs=in_specs, out_specs=out_specs, out_shape=out_shape,
            scratch_shapes=scratch_shapes, compiler_params=_cparams(len(grid)))(*operands)
        return list(outs), []
    c_in, c_out, c_sem = len(comm.inputs), len(comm.out_shapes), len(comm.sem_shapes)

    def hosted(*refs):
        bounds = [0, n_in, c_in, n_out, c_out, n_sc, c_sem]
        parts, pos = [], 0
        for k in bounds[1:]:
            parts.append(refs[pos:pos + k])
            pos += k
        ins, cins, outs, couts, scr, sems = parts
        step, n_steps = pl.program_id(0), grid[0]
        for ax in range(1, len(grid)):
            step, n_steps = step * grid[ax] + pl.program_id(ax), n_steps * grid[ax]

        @pl.when(step == 0)
        def _():
            comm.start(cins, couts, sems)

        @pl.when(step == (2 * n_steps) // 3)
        def _():
            comm.middle(cins, couts, sems)

        body(*ins, *outs, *scr)

        @pl.when(step == n_steps - 1)
        def _():
            comm.finish(cins, couts, sems)

    res = pl.pallas_call(
        hosted, grid=grid, name=name, in_specs=list(in_specs) + [_ANY] * c_in,
        out_specs=list(out_specs) + [_ANY] * c_out, out_shape=list(out_shape) + list(comm.out_shapes),
        scratch_shapes=list(scratch_shapes) + list(comm.sem_shapes),
        compiler_params=_cparams(len(grid)))(*operands, *comm.inputs)
    return list(res[:n_out]), list(res[n_out:])


def _load_resident(step, pairs, sems):
    @pl.when(step == 0)
    def _():
        copies = [pltpu.make_async_copy(src, dst, sems.at[k]) for k, (src, dst) in enumerate(pairs)]
        for cp in copies:
            cp.start()
        for cp in copies:
            cp.wait()


def _square_pairs(buf_hbm, index, dst):
    off = index * SQ_ROWS
    return [(buf_hbm.at[d, off:off + SQ_ROWS, :], dst.at[d * SQ_ROWS:(d + 1) * SQ_ROWS, :]) for d in range(N_DEV)]


def _down_pairs(wd_hbm, dst):
    return [(wd_hbm.at[d], dst.at[d // 2, (d % 2) * DOWN_ROWS:(d % 2 + 1) * DOWN_ROWS, :]) for d in range(N_DEV)]


def _zero_down_pad(step, dst):
    @pl.when(step == 0)
    def _():
        dst[:, FF_BLK:, :] = jnp.zeros((4, FF_PAD - FF_BLK, D), BF16)


def _rms_fwd_tile(xt, g):
    r = lax.rsqrt(jnp.mean(xt * xt, axis=-1, keepdims=True) + RMS_EPS)
    return (xt * r) * g


def _rms_bwd_tile(xt, g, dn):
    r = lax.rsqrt(jnp.mean(xt * xt, axis=-1, keepdims=True) + RMS_EPS)
    xhat = xt * r
    dxhat = dn * g
    dx = r * (dxhat - xhat * jnp.mean(dxhat * xhat, axis=-1, keepdims=True))
    dg = jnp.sum(dn * xhat, axis=0, keepdims=True)
    return dx, dg


def _accumulate(ref, step, value):
    @pl.when(step == 0)
    def _():
        ref[...] = value

    @pl.when(step != 0)
    def _():
        ref[...] = ref[...] + value


def _ffn_fwd(x, g, wgu, wd, name, comm=None, head=None):
    t = x.shape[0]

    def body(x_ref, g_ref, wgu_hbm, wd_hbm, *refs):
        if head is None:
            n_ref, gate_ref, up_ref, act_ref, h_ref, wgu_v, wd_v, sems = refs
        else:
            gf_ref, t_ref, n_ref, gate_ref, up_ref, act_ref, dh_ref, loss_ref, dgf_ref, wgu_v, wd_v, sems = refs
        step = pl.program_id(0)
        _zero_down_pad(step, wd_v)
        _load_resident(step, [(wgu_hbm, wgu_v)] + _down_pairs(wd_hbm, wd_v), sems)
        xt = x_ref[...]
        n = _rms_fwd_tile(xt, g_ref[...]).astype(BF16)
        n_ref[...] = n
        acc = jnp.zeros((TM, D), F32)
        for j in range(4):
            for s in range(FF_PAD // FF_SUB):
                lo, hi = s * FF_SUB, (s + 1) * FF_SUB
                gt = _dot_nt(n, wgu_v[j, lo:hi, :])
                ut = _dot_nt(n, wgu_v[4 + j, lo:hi, :])
                gate_ref[j, :, lo:hi] = gt.astype(BF16)
                up_ref[j, :, lo:hi] = ut.astype(BF16)
                act_ref[j, :, lo:hi] = ((gt * _sigmoid(gt)) * ut).astype(BF16)
            acc = acc + _dot(act_ref[j], wd_v[j])
        ht = xt + 0.5 * acc
        if head is None:
            h_ref[...] = ht
        else:
            gain = gf_ref[...]
            diff = _rms_fwd_tile(ht, gain) - t_ref[...]
            part = 0.5 * jnp.sum(jnp.sum(diff * diff, axis=-1, keepdims=True) / D, axis=0, keepdims=True)
            dx, dg = _rms_bwd_tile(ht, gain, diff / D)
            dh_ref[...] = dx
            _accumulate(loss_ref, step, jnp.broadcast_to(part, (8, 128)))
            _accumulate(dgf_ref, step, dg)

    ff = jax.ShapeDtypeStruct((4, t, FF_PAD), BF16)
    operands, in_specs = (x, g, wgu, wd), [_row_spec(TM, D), _const_spec((1, D)), _ANY, _ANY]
    out_specs = [_row_spec(TM, D)] + [_blk_row_spec(4, TM, FF_PAD)] * 3 + [_row_spec(TM, D)]
    out_shape = [jax.ShapeDtypeStruct((t, D), BF16), ff, ff, ff, jax.ShapeDtypeStruct((t, D), F32)]
    if head is not None:
        operands += tuple(head)
        in_specs += [_const_spec((1, D)), _row_spec(TM, D)]
        out_specs += [_const_spec((8, 128)), _const_spec((1, D))]
        out_shape += [jax.ShapeDtypeStruct((8, 128), F32), jax.ShapeDtypeStruct((1, D), F32)]
    return _call(
        body, operands, grid=(t // TM,), name=name, comm=comm, in_specs=in_specs, out_specs=out_specs,
        out_shape=out_shape,
        scratch_shapes=[pltpu.VMEM((N_DEV, FF_PAD, D), BF16), pltpu.VMEM((4, FF_PAD, D), BF16),
                        pltpu.SemaphoreType.DMA((1 + N_DEV,))])


def _ffn_bwd(dh, xin, g, gate, up, wgu, wd, name, comm=None):
    t = dh.shape[0]

    def body(dh_ref, x_ref, g_ref, gate_ref, up_ref, wgu_hbm, wd_hbm,
             dgu_ref, dhb_ref, dx_ref, dg_ref, wgu_v, wd_v, sems):
        step = pl.program_id(0)
        _zero_down_pad(step, wd_v)
        _load_resident(step, [(wgu_hbm, wgu_v)] + _down_pairs(wd_hbm, wd_v), sems)
        dht = dh_ref[...]
        dhb = (0.5 * dht).astype(BF16)
        dhb_ref[...] = dhb
        dn = jnp.zeros((TM, D), F32)
        for j in range(4):
            for s in range(FF_PAD // FF_SUB):
                lo, hi = s * FF_SUB, (s + 1) * FF_SUB
                da = _dot_nt(dhb, wd_v[j, lo:hi, :])
                gt = gate_ref[j, :, lo:hi].astype(F32)
                ut = up_ref[j, :, lo:hi].astype(F32)
                sg = _sigmoid(gt)
                dgt = (da * ut * (sg * (1.0 + gt * (1.0 - sg)))).astype(BF16)
                dut = (da * (gt * sg)).astype(BF16)
                dgu_ref[j, :, lo:hi] = dgt
                dgu_ref[4 + j, :, lo:hi] = dut
            dn = dn + _dot(dgu_ref[j], wgu_v[j]) + _dot(dgu_ref[4 + j], wgu_v[4 + j])
        dx, dg = _rms_bwd_tile(x_ref[...], g_ref[...], dn)
        dx_ref[...] = dht + dx
        _accumulate(dg_ref, step, dg)

    return _call(
        body, (dh, xin, g, gate, up, wgu, wd), grid=(t // TM,), name=name, comm=comm,
        in_specs=[_row_spec(TM, D), _row_spec(TM, D), _const_spec((1, D)), _blk_row_spec(4, TM, FF_PAD),
                  _blk_row_spec(4, TM, FF_PAD), _ANY, _ANY],
        out_specs=[_blk_row_spec(N_DEV, TM, FF_PAD), _row_spec(TM, D), _row_spec(TM, D), _const_spec((1, D))],
        out_shape=[jax.ShapeDtypeStruct((N_DEV, t, FF_PAD), BF16), jax.ShapeDtypeStruct((t, D), BF16),
                   jax.ShapeDtypeStruct((t, D), F32), jax.ShapeDtypeStruct((1, D), F32)],
        scratch_shapes=[pltpu.VMEM((N_DEV, FF_PAD, D), BF16), pltpu.VMEM((4, FF_PAD, D), BF16),
                        pltpu.SemaphoreType.DMA((1 + N_DEV,))])


WIDE_TILES = (1024, 512, 256, 128)


def _pick_tile(n, options=(512, 256, 128)):
    for o in options:
        if n % o == 0:
            return o
    return n


def _into(stack, n_operands):
    if stack is None:
        return (), [], {}
    return (stack,), [_ANY], {n_operands: 0}


def _mm_tn_square(a, b, name, index, count, stack=None):
    k, m = a.shape
    _, n = b.shape
    tn = _pick_tile(n)
    extra, extra_specs, aliases = _into(stack, 2)

    def body(a_ref, b_ref, *rest):
        rest[-1][...] = _dot_tn(a_ref[...], b_ref[...]).astype(BF16).reshape(N_DEV, m // N_DEV, tn)

    return pl.pallas_call(
        body, grid=(n // tn,), name=name,
        in_specs=[pl.BlockSpec((k, m), lambda j: (0, 0)), pl.BlockSpec((k, tn), lambda j: (0, j))] + extra_specs,
        out_specs=pl.BlockSpec((N_DEV, m // N_DEV, tn), lambda j: (0, index, j)),
        out_shape=jax.ShapeDtypeStruct((N_DEV, count * (m // N_DEV), n), BF16),
        input_output_aliases=aliases, compiler_params=_cparams(1),
    )(a, b, *extra)


def _mm_tn_cols(a, b, name, first=0, count=None, stack=None):
    k, m = a.shape
    nb, _, n = b.shape
    tm = _pick_tile(m, WIDE_TILES)
    extra, extra_specs, aliases = _into(stack, 2)

    def body(a_ref, b_ref, *rest):
        rest[-1][0] = _dot_tn(a_ref[...].astype(BF16), b_ref[0].astype(BF16)).astype(BF16)

    return pl.pallas_call(
        body, grid=(nb, m // tm), name=name,
        in_specs=[pl.BlockSpec((k, tm), lambda j, i: (0, i)), pl.BlockSpec((1, k, n), lambda j, i: (j, 0, 0))]
                 + extra_specs,
        out_specs=pl.BlockSpec((1, tm, n), lambda j, i: (j + first, i, 0)),
        out_shape=jax.ShapeDtypeStruct((nb if count is None else count, m, n), BF16),
        input_output_aliases=aliases, compiler_params=_cparams(2),
    )(a, b, *extra)


def _mm_tn_rows(a, b, keep, name):
    nb, k, m = a.shape
    _, n = b.shape
    tn = _pick_tile(n, WIDE_TILES)

    def body(a_ref, b_ref, o_ref):
        o_ref[0] = _dot_tn(a_ref[0], b_ref[...])[:keep].astype(BF16)

    return pl.pallas_call(
        body, grid=(nb, n // tn), name=name,
        in_specs=[pl.BlockSpec((1, k, m), lambda j, i: (j, 0, 0)), pl.BlockSpec((k, tn), lambda j, i: (0, i))],
        out_specs=pl.BlockSpec((1, keep, tn), lambda j, i: (j, 0, i)),
        out_shape=jax.ShapeDtypeStruct((nb, keep, n), BF16),
        compiler_params=_cparams(2),
    )(a, b)


PCG_W = 5 * D
QKV_W = 3 * D
PROJ_SUB = 512


def _inproj_fwd(h, g, w_in, name, comm=None):
    t = h.shape[0]

    def body(h_ref, g_ref, w_hbm, u_ref, pcg_ref, qkv_ref, w_v, sems):
        _load_resident(pl.program_id(0), [(w_hbm, w_v)], sems)
        u = _rms_fwd_tile(h_ref[...], g_ref[...]).astype(BF16)
        u_ref[...] = u
        for blk in range(N_DEV):
            for s in range(D // PROJ_SUB):
                lo, hi = s * PROJ_SUB, (s + 1) * PROJ_SUB
                p = _dot(u, w_v[blk, :, lo:hi])
                if blk < 3:
                    pcg_ref[:, blk * D + lo:blk * D + hi] = p
                elif blk < 6:
                    qkv_ref[:, (blk - 3) * D + lo:(blk - 3) * D + hi] = p.astype(BF16)
                else:
                    pcg_ref[:, (blk - 3) * D + lo:(blk - 3) * D + hi] = p

    return _call(
        body, (h, g, w_in), grid=(t // TM,), name=name, comm=comm,
        in_specs=[_row_spec(TM, D), _const_spec((1, D)), _ANY],
        out_specs=[_row_spec(TM, D), _row_spec(TM, PCG_W), _row_spec(TM, QKV_W)],
        out_shape=[jax.ShapeDtypeStruct((t, D), BF16), jax.ShapeDtypeStruct((t, PCG_W), F32),
                   jax.ShapeDtypeStruct((t, QKV_W), BF16)],
        scratch_shapes=[pltpu.VMEM((N_DEV, D, D), BF16), pltpu.SemaphoreType.DMA((1,))])


CONV_CW = 256


def _shift_down(v, k, rows):
    return jnp.where(rows >= k, pltpu.roll(v, k, 0), 0.0)


def _shift_up(v, k, rows, t):
    return jnp.where(rows < t - k, pltpu.roll(v, t - k, 0), 0.0)


def _col_spec(t, cw, off):
    return pl.BlockSpec((t, cw), lambda j: (0, j + off))


def _conv_fwd(pcg, conv_w, name):
    t = pcg.shape[0]
    nb = D // CONV_CW

    def body(cb_ref, cc_ref, cx_ref, w_ref, y_ref):
        rows = lax.broadcasted_iota(jnp.int32, (t, CONV_CW), 0)
        xc = cc_ref[...] * cx_ref[...]
        conv = (w_ref[0:1, :] * _shift_down(xc, 2, rows) + w_ref[1:2, :] * _shift_down(xc, 1, rows)
                + w_ref[2:3, :] * xc)
        y_ref[...] = (cb_ref[...] * conv).astype(BF16)

    return pl.pallas_call(
        body, grid=(nb,), name=name,
        in_specs=[_col_spec(t, CONV_CW, 0), _col_spec(t, CONV_CW, nb), _col_spec(t, CONV_CW, 2 * nb),
                  pl.BlockSpec((CONV_K, CONV_CW), lambda j: (0, j))],
        out_specs=_col_spec(t, CONV_CW, 0),
        out_shape=jax.ShapeDtypeStruct((t, D), BF16),
        compiler_params=_cparams(),
    )(pcg, pcg, pcg, conv_w)


def _conv_bwd(pcg, conv_w, dyc, name):
    t = pcg.shape[0]
    nb = D // CONV_CW

    def body(cb_ref, cc_ref, cx_ref, w_ref, dy_ref, dc_ref, dw_ref):
        rows = lax.broadcasted_iota(jnp.int32, (t, CONV_CW), 0)
        cc, cx = cc_ref[...], cx_ref[...]
        xc = cc * cx
        x1 = _shift_down(xc, 1, rows)
        x2 = _shift_down(xc, 2, rows)
        w0, w1, w2 = w_ref[0:1, :], w_ref[1:2, :], w_ref[2:3, :]
        conv = w0 * x2 + w1 * x1 + w2 * xc
        dy = dy_ref[...]
        dc_ref[0] = (dy * conv).astype(BF16)
        dconv = dy * cb_ref[...]
        dw_ref[...] = jnp.zeros((8, CONV_CW), F32)
        dw_ref[0:1, :] = jnp.sum(dconv * x2, axis=0, keepdims=True)
        dw_ref[1:2, :] = jnp.sum(dconv * x1, axis=0, keepdims=True)
        dw_ref[2:3, :] = jnp.sum(dconv * xc, axis=0, keepdims=True)
        dxc = w2 * dconv + w1 * _shift_up(dconv, 1, rows, t) + w0 * _shift_up(dconv, 2, rows, t)
        dc_ref[1] = (dxc * cx).astype(BF16)
        dc_ref[2] = (dxc * cc).astype(BF16)

    return pl.pallas_call(
        body, grid=(nb,), name=name,
        in_specs=[_col_spec(t, CONV_CW, 0), _col_spec(t, CONV_CW, nb), _col_spec(t, CONV_CW, 2 * nb),
                  pl.BlockSpec((CONV_K, CONV_CW), lambda j: (0, j)), _col_spec(t, CONV_CW, 0)],
        out_specs=[pl.BlockSpec((3, t, CONV_CW), lambda j: (0, 0, j)), pl.BlockSpec((8, CONV_CW), lambda j: (0, j))],
        out_shape=[jax.ShapeDtypeStruct((3, t, D), BF16), jax.ShapeDtypeStruct((8, D), F32)],
        compiler_params=_cparams(),
    )(pcg, pcg, pcg, conv_w, dyc)


def _tri2(cond):
    rr = lax.broadcasted_iota(jnp.int32, (2 * TK, TK), 0) & (TK - 1)
    cc = lax.broadcasted_iota(jnp.int32, (2 * TK, TK), 1)
    return cond(rr, cc).astype(BF16)


def _causal(shift, row0=0):
    rr = lax.broadcasted_iota(jnp.int32, (TQ - row0, TK), 0) + row0
    cc = lax.broadcasted_iota(jnp.int32, (TQ - row0, TK), 1)
    return cc + shift < rr


def _cumdot(v, tri2):
    hi = v.astype(BF16)
    lo = (v - hi.astype(F32)).astype(BF16)
    return _dot(jnp.concatenate([hi, lo], axis=1), tri2)


def _log_1m_beta(z):
    return -(jnp.maximum(z, 0.0) + jnp.log(1.0 + jnp.exp(-jnp.abs(z))))


def _sb_specs(t):
    g = SB_H // SB_HPS
    w = SB_HPS * SB_DH
    q_spec = pl.BlockSpec((TQ, w), lambda h, i: (i, h))
    k_spec = pl.BlockSpec((t, w), lambda h, i: (0, g + h))
    v_spec = pl.BlockSpec((t, w), lambda h, i: (0, 2 * g + h))
    ct_spec = pl.BlockSpec((SB_HPS, TQ, 1), lambda h, i: (h, i, 0))
    return g, w, q_spec, k_spec, v_spec, ct_spec


def _sb_fwd(qkv, name, comm=None):
    t = qkv.shape[0]
    scale = SB_DH ** -0.5
    g, w, q_spec, k_spec, v_spec, ct_spec = _sb_specs(t)

    def body(q_ref, k_ref, v_ref, y_ref, ct_ref):
        i = pl.program_id(1)
        later = _tri2(lambda j, s: j > s)
        n_diag = TQ // TK

        def block(j, carry, shift):
            off = pl.multiple_of(j * TK, TK)
            zs, ms = [], []
            for hd in range(SB_HPS):
                cols = slice(hd * SB_DH, (hd + 1) * SB_DH)
                z = _dot_nt(q_ref[:, cols], k_ref[pl.ds(off, TK), cols]) * scale
                m = _log_1m_beta(z)
                if shift is not None:
                    m = jnp.where(_causal(shift), m, 0.0)
                zs.append(z)
                ms.append(m)
            after = _cumdot(jnp.concatenate(ms, axis=0), later)
            out = []
            for hd in range(SB_HPS):
                acc, c_sum = carry[hd]
                cols = slice(hd * SB_DH, (hd + 1) * SB_DH)
                a = jnp.exp((ms[hd] + zs[hd]) + (c_sum + after[hd * TQ:(hd + 1) * TQ]))
                if shift is not None:
                    a = jnp.where(_causal(shift), a, 0.0)
                out.append((acc + _dot(a.astype(BF16), v_ref[pl.ds(off, TK), cols]),
                            c_sum + jnp.sum(ms[hd], axis=1, keepdims=True)))
            return tuple(out)

        carry = tuple((jnp.zeros((TQ, SB_DH), F32), jnp.zeros((TQ, 1), F32)) for _ in range(SB_HPS))
        for d in reversed(range(n_diag)):
            carry = block(i * n_diag + d, carry, d * TK)
        carry = lax.fori_loop(0, i * n_diag, lambda jj, c: block(i * n_diag - 1 - jj, c, None), carry)
        for hd in range(SB_HPS):
            y_ref[:, hd * SB_DH:(hd + 1) * SB_DH] = carry[hd][0].astype(BF16)
            ct_ref[hd] = carry[hd][1]

    return _call(
        body, (qkv, qkv, qkv), grid=(g, t // TQ), name=name, comm=comm,
        in_specs=[q_spec, k_spec, v_spec],
        out_specs=[q_spec, ct_spec],
        out_shape=[jax.ShapeDtypeStruct((t, D), BF16), jax.ShapeDtypeStruct((SB_H, t, 1), F32)],
        scratch_shapes=[])


def _sb_bwd(qkv, dy, ctot, after, name, comm=None):
    t = qkv.shape[0]
    scale = SB_DH ** -0.5
    g, w, q_spec, k_spec, v_spec, ct_spec = _sb_specs(t)
    acc_spec = pl.BlockSpec((2, t, w), lambda h, i: (0, 0, h))

    def body(q_ref, k_ref, v_ref, dy_ref, ct_ref, after_ref, dq_ref, dkv_ref):
        i = pl.program_id(1)

        @pl.when(i == 0)
        def _():
            dkv_ref[...] = jnp.zeros_like(dkv_ref)

        upto = _tri2(lambda j, s: j <= s)
        n_diag = TQ // TK

        def block(j, carry, shift):
            off = pl.multiple_of(j * TK, TK)
            r0 = 0 if shift is None else shift
            nr = TQ - r0
            causal = None if shift is None else _causal(shift, r0)

            def grow(old, delta):
                return old + delta if r0 == 0 else jnp.concatenate([old[:r0], old[r0:] + delta], axis=0)

            zs, ms = [], []
            for hd in range(SB_HPS):
                cols = slice(hd * SB_DH, (hd + 1) * SB_DH)
                z = _dot_nt(q_ref[r0:, cols], k_ref[pl.ds(off, TK), cols]) * scale
                m = _log_1m_beta(z)
                if causal is not None:
                    m = jnp.where(causal, m, 0.0)
                zs.append(z)
                ms.append(m)
            m_upto = _cumdot(jnp.concatenate(ms, axis=0), upto)
            ls, a_s, es = [], [], []
            for hd in range(SB_HPS):
                cols = slice(hd * SB_DH, (hd + 1) * SB_DH)
                l = ms[hd] + zs[hd]
                a = jnp.exp(l + ((ct_ref[hd, r0:] - carry[hd][1][r0:]) - m_upto[hd * nr:(hd + 1) * nr]))
                if causal is not None:
                    a = jnp.where(causal, a, 0.0)
                ls.append(l)
                a_s.append(a)
                es.append(_dot_nt(dy_ref[r0:, cols], v_ref[pl.ds(off, TK), cols]) * a)
            e_upto = _dot(jnp.concatenate(es, axis=0).astype(BF16), upto[:TK])
            out = []
            for hd in range(SB_HPS):
                dq, p_sum, e_sum = carry[hd]
                cols = slice(hd * SB_DH, (hd + 1) * SB_DH)
                e = es[hd]
                dz = e - jnp.exp(ls[hd]) * (e_sum[r0:] + e_upto[hd * nr:(hd + 1) * nr])
                if causal is not None:
                    dz = jnp.where(causal, dz, 0.0)
                dzs = (dz * scale).astype(BF16)
                dkv_ref[0, pl.ds(off, TK), cols] += _dot_tn(dzs, q_ref[r0:, cols])
                dkv_ref[1, pl.ds(off, TK), cols] += _dot_tn(a_s[hd].astype(BF16), dy_ref[r0:, cols])
                out.append((grow(dq, _dot(dzs, k_ref[pl.ds(off, TK), cols])),
                            grow(p_sum, jnp.sum(ms[hd], axis=1, keepdims=True)),
                            grow(e_sum, jnp.sum(e, axis=1, keepdims=True))))
            return tuple(out)

        zero = jnp.zeros((TQ, 1), F32)
        init = tuple((jnp.zeros((TQ, SB_DH), F32), zero, zero) for _ in range(SB_HPS))
        carry = lax.fori_loop(0, i * n_diag, lambda j, c: block(j, c, None), init)
        for d in range(n_diag):
            carry = block(i * n_diag + d, carry, d * TK)
        for hd in range(SB_HPS):
            dq_ref[:, hd * SB_DH:(hd + 1) * SB_DH] = carry[hd][0].astype(BF16)

    return _call(
        body, (qkv, qkv, qkv, dy, ctot, after), grid=(g, t // TQ), name=name, comm=comm,
        in_specs=[q_spec, k_spec, v_spec, q_spec, ct_spec, pl.BlockSpec(after.shape, lambda h, i: (0, 0))],
        out_specs=[q_spec, acc_spec],
        out_shape=[jax.ShapeDtypeStruct((t, D), BF16), jax.ShapeDtypeStruct((2, t, D), F32)],
        scratch_shapes=[])


def _gate_specs():
    return [pl.BlockSpec((TM, D), lambda i: (i, 3)), pl.BlockSpec((TM, D), lambda i: (i, 4))]


def _mix_pairs(mix_hbm, dsts):
    pairs = []
    for index, dst in enumerate(dsts):
        pairs += _square_pairs(mix_hbm, index, dst)
    return pairs


def _mix_out_fwd(yc, ysb, pcg, b_gate, h, w_mix, name):
    t = h.shape[0]

    def body(yc_ref, ysb_ref, gc_ref, gs_ref, b_ref, h_ref, mix_hbm,
             a_ref, b_out_ref, mg_ref, h2_ref, wc_v, wa_v, wo_v, sems):
        _load_resident(pl.program_id(0), _mix_pairs(mix_hbm, (wc_v, wa_v, wo_v)), sems)
        a = _dot(yc_ref[...], wc_v[...])
        b = _dot(ysb_ref[...], wa_v[...])
        merged = (_sigmoid(gc_ref[...] + b_ref[:, :D]) * a + _sigmoid(gs_ref[...] + b_ref[:, D:]) * b).astype(BF16)
        a_ref[...] = a
        b_out_ref[...] = b
        mg_ref[...] = merged
        h2_ref[...] = h_ref[...] + _dot(merged, wo_v[...])

    return pl.pallas_call(
        body, grid=(t // TM,), name=name,
        in_specs=[_row_spec(TM, D), _row_spec(TM, D)] + _gate_specs()
                 + [_const_spec((1, 2 * D)), _row_spec(TM, D), _ANY],
        out_specs=[_row_spec(TM, D)] * 4,
        out_shape=[jax.ShapeDtypeStruct((t, D), F32), jax.ShapeDtypeStruct((t, D), F32),
                   jax.ShapeDtypeStruct((t, D), BF16), jax.ShapeDtypeStruct((t, D), F32)],
        scratch_shapes=[pltpu.VMEM((D, D), BF16)] * 3 + [pltpu.SemaphoreType.DMA((3 * N_DEV,))],
        compiler_params=_cparams(),
    )(yc, ysb, pcg, pcg, b_gate, h, w_mix)


def _mix_out_bwd(dh2, a, b, pcg, b_gate, w_mix, name, comm=None):
    t = dh2.shape[0]

    def body(dh_ref, a_ref, b_ref, gc_ref, gs_ref, bias_ref, mix_hbm,
             dhb_ref, da_ref, db_ref, dgp_ref, dyc_ref, dysb_ref, dbias_ref, wc_v, wa_v, wo_v, sems):
        step = pl.program_id(0)
        _load_resident(step, _mix_pairs(mix_hbm, (wc_v, wa_v, wo_v)), sems)
        dhb = dh_ref[...].astype(BF16)
        dhb_ref[...] = dhb
        dm = _dot_nt(dhb, wo_v[...])
        gc = _sigmoid(gc_ref[...] + bias_ref[:, :D])
        gs = _sigmoid(gs_ref[...] + bias_ref[:, D:])
        da = (dm * gc).astype(BF16)
        db = (dm * gs).astype(BF16)
        da_ref[...] = da
        db_ref[...] = db
        dgc = dm * a_ref[...] * (gc * (1.0 - gc))
        dgs = dm * b_ref[...] * (gs * (1.0 - gs))
        dgp_ref[0] = dgc.astype(BF16)
        dgp_ref[1] = dgs.astype(BF16)
        _accumulate(dbias_ref.at[:, :D], step, jnp.sum(dgc, axis=0, keepdims=True))
        _accumulate(dbias_ref.at[:, D:], step, jnp.sum(dgs, axis=0, keepdims=True))
        dyc_ref[...] = _dot_nt(da, wc_v[...])
        dysb_ref[...] = _dot_nt(db, wa_v[...]).astype(BF16)

    return _call(
        body, (dh2, a, b, pcg, pcg, b_gate, w_mix), grid=(t // TM,), name=name, comm=comm,
        in_specs=[_row_spec(TM, D)] * 3 + _gate_specs() + [_const_spec((1, 2 * D)), _ANY],
        out_specs=[_row_spec(TM, D)] * 3 + [_blk_row_spec(2, TM, D), _row_spec(TM, D), _row_spec(TM, D),
                                            _const_spec((1, 2 * D))],
        out_shape=[jax.ShapeDtypeStruct((t, D), BF16)] * 3
                  + [jax.ShapeDtypeStruct((2, t, D), BF16), jax.ShapeDtypeStruct((t, D), F32),
                     jax.ShapeDtypeStruct((t, D), BF16), jax.ShapeDtypeStruct((1, 2 * D), F32)],
        scratch_shapes=[pltpu.VMEM((D, D), BF16)] * 3 + [pltpu.SemaphoreType.DMA((3 * N_DEV,))])


def _inproj_bwd(dconv, dq, dkv, dgp, w_in, h, g, dh_res, name, comm=None):
    t = h.shape[0]

    def body(dc_ref, dq_ref, dkv_ref, dgp_ref, w_hbm, h_ref, g_ref, dres_ref, dh_ref, dg_ref, w_v, sems):
        step = pl.program_id(0)
        _load_resident(step, [(w_hbm, w_v)], sems)
        du = _dot_nt(dq_ref[...], w_v[3])
        for k in range(3):
            du = du + _dot_nt(dc_ref[k], w_v[k])
        for k in range(2):
            du = du + _dot_nt(dkv_ref[k].astype(BF16), w_v[4 + k]) + _dot_nt(dgp_ref[k], w_v[6 + k])
        dx, dg = _rms_bwd_tile(h_ref[...], g_ref[...], du)
        dh_ref[...] = dres_ref[...] + dx
        _accumulate(dg_ref, step, dg)

    return _call(
        body, (dconv, dq, dkv, dgp, w_in, h, g, dh_res), grid=(t // TM,), name=name, comm=comm,
        in_specs=[_blk_row_spec(3, TM, D), _row_spec(TM, D), _blk_row_spec(2, TM, D), _blk_row_spec(2, TM, D), _ANY,
                  _row_spec(TM, D), _const_spec((1, D)), _row_spec(TM, D)],
        out_specs=[_row_spec(TM, D), _const_spec((1, D))],
        out_shape=[jax.ShapeDtypeStruct((t, D), F32), jax.ShapeDtypeStruct((1, D), F32)],
        scratch_shapes=[pltpu.VMEM((N_DEV, D, D), BF16), pltpu.SemaphoreType.DMA((1,))])


def _memkv_fwd(mem, g, w_ckv, name):
    m = mem.shape[0]

    def body(mem_ref, g_ref, w_ref, mn_ref, kv_ref):
        mn = _rms_fwd_tile(mem_ref[...], g_ref[...]).astype(BF16)
        mn_ref[...] = mn
        for j in range(N_DEV):
            kv_ref[j] = _dot(mn, w_ref[j]).astype(BF16)

    return pl.pallas_call(
        body, grid=(1,), name=name,
        in_specs=[_const_spec((m, D)), _const_spec((1, D)), _const_spec((N_DEV, D, X_DH))],
        out_specs=[_const_spec((m, D)), _const_spec((N_DEV, m, X_DH))],
        out_shape=[jax.ShapeDtypeStruct((m, D), BF16), jax.ShapeDtypeStruct((N_DEV, m, X_DH), BF16)],
        compiler_params=_cparams(),
    )(mem, g, w_ckv)


def _memkv_bwd(dkv, mem, g, w_ckv, name):
    m = mem.shape[0]

    def body(dkv_ref, mem_ref, g_ref, w_ref, dg_ref):
        dmn = jnp.zeros((m, D), F32)
        for j in range(N_DEV):
            dmn = dmn + _dot_nt(dkv_ref[j].astype(BF16), w_ref[j])
        _, dg = _rms_bwd_tile(mem_ref[...], g_ref[...], dmn)
        dg_ref[...] = dg

    return pl.pallas_call(
        body, grid=(1,), name=name,
        in_specs=[_const_spec((N_DEV, m, X_DH)), _const_spec((m, D)), _const_spec((1, D)),
                  _const_spec((N_DEV, D, X_DH))],
        out_specs=_const_spec((1, D)),
        out_shape=jax.ShapeDtypeStruct((1, D), F32),
        compiler_params=_cparams(),
    )(dkv, mem, g, w_ckv)


def _softmax_rows(s):
    e = jnp.exp(s - jnp.max(s, axis=-1, keepdims=True))
    return e / jnp.sum(e, axis=-1, keepdims=True)


def _cross_pairs(cross_hbm, wq_v, wo_v):
    return _square_pairs(cross_hbm, 0, wq_v) + _square_pairs(cross_hbm, 1, wo_v)


def _cross_fwd(h, g, kv, w_cross, name):
    t = h.shape[0]
    m = kv.shape[1]
    scale = X_DH ** -0.5

    def body(h_ref, g_ref, kv_ref, cross_hbm, hn_ref, qx_ref, o_ref, h3_ref, wq_v, wo_v, sems):
        _load_resident(pl.program_id(0), _cross_pairs(cross_hbm, wq_v, wo_v), sems)
        ht = h_ref[...]
        hn = _rms_fwd_tile(ht, g_ref[...]).astype(BF16)
        hn_ref[...] = hn
        qx = _dot(hn, wq_v[...]).astype(BF16)
        qx_ref[...] = qx
        for hd in range(X_H):
            lo, hi = hd * X_DH, (hd + 1) * X_DH
            p = _softmax_rows(_dot_nt(qx[:, lo:hi], kv_ref[hd]) * scale)
            o_ref[:, lo:hi] = _dot(p.astype(BF16), kv_ref[X_H + hd]).astype(BF16)
        h3_ref[...] = ht + _dot(o_ref[...], wo_v[...])

    return pl.pallas_call(
        body, grid=(t // TM,), name=name,
        in_specs=[_row_spec(TM, D), _const_spec((1, D)), _const_spec((N_DEV, m, X_DH)), _ANY],
        out_specs=[_row_spec(TM, D)] * 4,
        out_shape=[jax.ShapeDtypeStruct((t, D), BF16)] * 3 + [jax.ShapeDtypeStruct((t, D), F32)],
        scratch_shapes=[pltpu.VMEM((D, D), BF16)] * 2 + [pltpu.SemaphoreType.DMA((2 * N_DEV,))],
        compiler_params=_cparams(),
    )(h, g, kv, w_cross)


def _cross_bwd(dh3, h, g, qx, kv, w_cross, name, comm=None):
    t = h.shape[0]
    m = kv.shape[1]
    scale = X_DH ** -0.5

    def body(dh_ref, h_ref, g_ref, qx_ref, kv_ref, cross_hbm,
             dhb_ref, dqx_ref, dkv_ref, dh2_ref, dg_ref, wq_v, wo_v, sems):
        step = pl.program_id(0)
        _load_resident(step, _cross_pairs(cross_hbm, wq_v, wo_v), sems)

        @pl.when(step == 0)
        def _():
            dkv_ref[...] = jnp.zeros_like(dkv_ref)

        dht = dh_ref[...]
        dhb = dht.astype(BF16)
        dhb_ref[...] = dhb
        do = _dot_nt(dhb, wo_v[...]).astype(BF16)
        for hd in range(X_H):
            lo, hi = hd * X_DH, (hd + 1) * X_DH
            qh = qx_ref[:, lo:hi]
            kh = kv_ref[hd]
            p = _softmax_rows(_dot_nt(qh, kh) * scale)
            doh = do[:, lo:hi]
            dp = _dot_nt(doh, kv_ref[X_H + hd])
            ds = (p * (dp - jnp.sum(dp * p, axis=-1, keepdims=True)) * scale).astype(BF16)
            dqx_ref[:, lo:hi] = _dot(ds, kh).astype(BF16)
            dkv_ref[hd] += _dot_tn(ds, qh)
            dkv_ref[X_H + hd] += _dot_tn(p.astype(BF16), doh)
        dhn = _dot_nt(dqx_ref[...], wq_v[...])
        dx, dg = _rms_bwd_tile(h_ref[...], g_ref[...], dhn)
        dh2_ref[...] = dht + dx
        _accumulate(dg_ref, step, dg)

    return _call(
        body, (dh3, h, g, qx, kv, w_cross), grid=(t // TM,), name=name, comm=comm,
        in_specs=[_row_spec(TM, D), _row_spec(TM, D), _const_spec((1, D)), _row_spec(TM, D),
                  _const_spec((N_DEV, m, X_DH)), _ANY],
        out_specs=[_row_spec(TM, D), _row_spec(TM, D), _const_spec((N_DEV, m, X_DH)), _row_spec(TM, D),
                   _const_spec((1, D))],
        out_shape=[jax.ShapeDtypeStruct((t, D), BF16), jax.ShapeDtypeStruct((t, D), BF16),
                   jax.ShapeDtypeStruct((N_DEV, m, X_DH), F32), jax.ShapeDtypeStruct((t, D), F32),
                   jax.ShapeDtypeStruct((1, D), F32)],
        scratch_shapes=[pltpu.VMEM((D, D), BF16)] * 2 + [pltpu.SemaphoreType.DMA((2 * N_DEV,))])


def _adamw(w, parts, m, v, name, row_block=0, token=None):
    r, c = w.shape
    n = parts.shape[0]
    tr = _pick_tile(r, (256, 352, 128))
    off = row_block * (r // tr)

    def body(*refs):
        if token is None:
            _adamw_update(None, *refs)
        else:
            _adamw_update(refs[4], *refs[:4], *refs[5:])

    spec = _row_spec(tr, c)
    in_specs = [spec, pl.BlockSpec((n, tr, c), lambda i: (0, i + off, 0)), spec, spec]
    operands = (w, parts, m, v)
    if token is not None:
        in_specs.append(_const_spec(token.shape))
        operands += (token,)
    return pl.pallas_call(
        body, grid=(r // tr,), name=name, in_specs=in_specs, out_specs=[spec] * 4,
        out_shape=[jax.ShapeDtypeStruct((r, c), F32)] * 4,
        compiler_params=_cparams(),
    )(*operands)


def _adamw_update(tok_ref, w_ref, p_ref, m_ref, v_ref, g_ref, d_ref, nm_ref, nv_ref):
    gt = p_ref[0].astype(F32)
    for k in range(1, p_ref.shape[0]):
        gt = gt + p_ref[k].astype(F32)
    if tok_ref is not None:
        gt = gt + tok_ref[0:1, 0:1]
    _adamw_apply(gt, w_ref, m_ref, v_ref, g_ref, d_ref, nm_ref, nv_ref)


def _adamw_own(w, land, own, chip, m, v, name, row_block=0, token=None):
    r, c = w.shape
    tr = _pick_tile(r, (256, 352, 128))
    off = row_block * (r // tr)

    def body(chip_ref, w_ref, land_ref, own_ref, m_ref, v_ref, *rest):
        mine = own_ref[0].astype(F32)
        gt = jnp.where(chip_ref[0] == 0, mine, land_ref[0].astype(F32))
        for k in range(1, N_CHIP):
            gt = gt + jnp.where(chip_ref[0] == k, mine, land_ref[k].astype(F32))
        if token is not None:
            gt = gt + rest[0][0:1, 0:1]
        _adamw_apply(gt, w_ref, m_ref, v_ref, *rest[-4:])

    spec = pl.BlockSpec((tr, c), lambda i, chip_ref: (i, 0))
    in_specs = [spec, pl.BlockSpec((N_CHIP, tr, c), lambda i, chip_ref: (0, i + off, 0)),
                pl.BlockSpec((1, tr, c), lambda i, chip_ref: (chip_ref[0], i + off, 0)), spec, spec]
    operands = (chip, w, land, own, m, v)
    if token is not None:
        in_specs.append(pl.BlockSpec(token.shape, lambda i, chip_ref: (0, 0)))
        operands += (token,)
    return pl.pallas_call(
        body, name=name,
        grid_spec=pltpu.PrefetchScalarGridSpec(
            num_scalar_prefetch=1, grid=(r // tr,), in_specs=in_specs, out_specs=[spec] * 4),
        out_shape=[jax.ShapeDtypeStruct((r, c), F32)] * 4,
        compiler_params=_cparams(),
    )(*operands)


def _adamw_apply(gt, w_ref, m_ref, v_ref, g_ref, d_ref, nm_ref, nv_ref):
    g_ref[...] = gt
    nm = ADAM_B1 * m_ref[...] + (1.0 - ADAM_B1) * gt
    nv = ADAM_B2 * v_ref[...] + (1.0 - ADAM_B2) * jnp.square(gt)
    m_hat = nm / (1.0 - ADAM_B1 ** ADAM_STEP)
    v_hat = nv / (1.0 - ADAM_B2 ** ADAM_STEP)
    d_ref[...] = -ADAM_LR * (m_hat / (jnp.sqrt(v_hat) + ADAM_EPS) + ADAM_WD * w_ref[...])
    nm_ref[...] = nm
    nv_ref[...] = nv


def _mesh_pos():
    return lax.axis_index("x"), lax.axis_index("y"), lax.axis_index("c")


def _no_round(in_refs, out_refs, sems):
    pass


def _run_exchange(comm, name):
    c_in, c_out = len(comm.inputs), len(comm.out_shapes)

    def body(*refs):
        cins, couts, sems = refs[:c_in], refs[c_in:c_in + c_out], refs[c_in + c_out:]
        comm.start(cins, couts, sems)
        comm.middle(cins, couts, sems)
        comm.finish(cins, couts, sems)

    return list(pl.pallas_call(
        body, name=name, out_shape=list(comm.out_shapes),
        in_specs=[_ANY] * c_in, out_specs=[_ANY] * c_out, scratch_shapes=list(comm.sem_shapes),
    )(*comm.inputs))


def _gather_exchange(shards):
    n_arr = len(shards)

    def plan(x_refs, out_refs, sems):
        send_sems, recv_sems, local_sems = sems
        x, y, c = _mesh_pos()
        me, sibling = (x, y, c), (x, y, 1 - c)
        xn, yn, diag = (1 - x, y), (x, 1 - y), (1 - x, 1 - y)

        def slot(a, px, py, pc, half=None):
            ref = out_refs[a].at[4 * px + 2 * py + pc]
            if half is None:
                return ref
            rows = shards[a].shape[0] // 2
            return ref.at[half * rows:(half + 1) * rows]

        def copy(a, k, block, to, half=None, src=None):
            dst = slot(a, *block, half)
            return pltpu.make_async_remote_copy(
                src_ref=dst if src is None else src, dst_ref=dst,
                send_sem=send_sems.at[a, k], recv_sem=recv_sems.at[a, k],
                device_id=to, device_id_type=pl.DeviceIdType.MESH)

        return types.SimpleNamespace(
            me=me, sibling=sibling, xn=xn, yn=yn, diag=diag, c=c, copy=copy,
            mine=[pltpu.make_async_copy(x_refs[a], slot(a, *me), local_sems.at[a]) for a in range(n_arr)],
            first=[cp for a in range(n_arr) for cp in (
                copy(a, 0, me, sibling, src=x_refs[a]), copy(a, 1, me, (*xn, c), src=x_refs[a]),
                copy(a, 2, me, (*yn, c), src=x_refs[a]))],
            second=lambda a: (copy(a, 3, (*xn, c), (*yn, c), half=0), copy(a, 5, (*xn, c), sibling),
                              copy(a, 4, (*yn, c), (*xn, c), half=1), copy(a, 6, (*yn, c), sibling)),
            third=lambda a: (copy(a, 7, (*diag, c), sibling, half=0), copy(a, 8, (*diag, c), sibling, half=1)))

    def start(x_refs, out_refs, sems):
        p = plan(x_refs, out_refs, sems)
        for cp in p.mine + p.first:
            cp.start()

    def middle(x_refs, out_refs, sems):
        p = plan(x_refs, out_refs, sems)
        for a in range(n_arr):
            to_yn, x_to_sib, to_xn, y_to_sib = p.second(a)
            p.copy(a, 1, (*p.xn, p.c), p.me).wait_recv()
            to_yn.start()
            x_to_sib.start()
            p.copy(a, 2, (*p.yn, p.c), p.me).wait_recv()
            to_xn.start()
            y_to_sib.start()

    def finish(x_refs, out_refs, sems):
        p = plan(x_refs, out_refs, sems)
        for a in range(n_arr):
            half0_to_sib, half1_to_sib = p.third(a)
            p.copy(a, 3, (*p.diag, p.c), p.me, half=0).wait_recv()
            half0_to_sib.start()
            p.copy(a, 4, (*p.diag, p.c), p.me, half=1).wait_recv()
            half1_to_sib.start()
        other = 1 - p.c
        for a in range(n_arr):
            p.copy(a, 0, p.sibling, p.me).wait_recv()
            p.copy(a, 5, (*p.xn, other), p.me).wait_recv()
            p.copy(a, 6, (*p.yn, other), p.me).wait_recv()
            p.copy(a, 7, (*p.diag, other), p.me, half=0).wait_recv()
            p.copy(a, 8, (*p.diag, other), p.me, half=1).wait_recv()
        for cp in p.first:
            cp.wait_send()
        for a in range(n_arr):
            for cp in p.second(a) + p.third(a):
                cp.wait_send()
        for cp in p.mine:
            cp.wait()

    return types.SimpleNamespace(
        inputs=list(shards), start=start, middle=middle, finish=finish,
        out_shapes=[jax.ShapeDtypeStruct((N_DEV,) + s.shape, s.dtype) for s in shards],
        sem_shapes=[pltpu.SemaphoreType.DMA((n_arr, 9)), pltpu.SemaphoreType.DMA((n_arr, 9)),
                    pltpu.SemaphoreType.DMA((n_arr,))])


def _pair_exchange(grads):
    n_arr = len(grads)

    def plan(g_refs, land_refs, sems):
        send_sems, recv_sems = sems
        x, y, c = _mesh_pos()
        return [pltpu.make_async_remote_copy(
            src_ref=g_refs[a].at[2 * k + 1 - c], dst_ref=land_refs[a].at[k],
            send_sem=send_sems.at[a, k], recv_sem=recv_sems.at[a, k],
            device_id=(x, y, 1 - c), device_id_type=pl.DeviceIdType.MESH)
            for a in range(n_arr) for k in range(N_CHIP)]

    def start(g_refs, land_refs, sems):
        for cp in plan(g_refs, land_refs, sems):
            cp.start()

    def finish(g_refs, land_refs, sems):
        for cp in plan(g_refs, land_refs, sems):
            cp.wait()

    return types.SimpleNamespace(
        inputs=list(grads), start=start, middle=_no_round, finish=finish,
        out_shapes=[jax.ShapeDtypeStruct((N_CHIP,) + g.shape[1:], g.dtype) for g in grads],
        sem_shapes=[pltpu.SemaphoreType.DMA((n_arr, N_CHIP)), pltpu.SemaphoreType.DMA((n_arr, N_CHIP))])


def _chip_exchange(parts):
    n_arr = len(parts)

    def plan(p_refs, land_refs, sems):
        send_sems, recv_sems, local_sems = sems
        x, y, c = _mesh_pos()
        my_chip = 2 * x + y
        chips = [(1 - x, y), (x, 1 - y), (1 - x, 1 - y)]
        local = [pltpu.make_async_copy(p_refs[a].at[my_chip], land_refs[a].at[my_chip], local_sems.at[a])
                 for a in range(n_arr)]

        def copy(a, k, src_slot, dst_slot, px, py):
            return pltpu.make_async_remote_copy(
                src_ref=p_refs[a].at[src_slot], dst_ref=land_refs[a].at[dst_slot],
                send_sem=send_sems.at[a, k], recv_sem=recv_sems.at[a, k],
                device_id=(px, py, c), device_id_type=pl.DeviceIdType.MESH)

        sends = [copy(a, k, 2 * px + py, my_chip, px, py) for a in range(n_arr) for k, (px, py) in enumerate(chips)]
        arrivals = [copy(a, k, my_chip, 2 * px + py, px, py) for a in range(n_arr)
                    for k, (px, py) in enumerate(chips)]
        return local, sends, arrivals

    def start(p_refs, land_refs, sems):
        local, sends, _ = plan(p_refs, land_refs, sems)
        for cp in local + sends:
            cp.start()

    def finish(p_refs, land_refs, sems):
        local, sends, arrivals = plan(p_refs, land_refs, sems)
        for cp in arrivals:
            cp.wait_recv()
        for cp in sends:
            cp.wait_send()
        for cp in local:
            cp.wait()

    return types.SimpleNamespace(
        inputs=list(parts), start=start, middle=_no_round, finish=finish,
        out_shapes=[jax.ShapeDtypeStruct(p.shape, p.dtype) for p in parts],
        sem_shapes=[pltpu.SemaphoreType.DMA((n_arr, 3)), pltpu.SemaphoreType.DMA((n_arr, 3)),
                    pltpu.SemaphoreType.DMA((n_arr,))])


_HBM = pl.BlockSpec(memory_space=pltpu.HBM)
_SEM = pl.BlockSpec(memory_space=pltpu.SEMAPHORE)
_DATAFLOW = pltpu.SideEffectType.DATAFLOW_SIDE_EFFECTING


def _chip_copies(p_refs, land_refs, send_sems, recv_sems):
    x, y, c = _mesh_pos()
    my_chip = 2 * x + y
    chips = [(1 - x, y), (x, 1 - y), (1 - x, 1 - y)]
    return [pltpu.make_async_remote_copy(
        src_ref=p_refs[a].at[2 * px + py], dst_ref=land_refs[a].at[my_chip],
        send_sem=send_sems[3 * a + k], recv_sem=recv_sems[3 * a + k],
        device_id=(px, py, c), device_id_type=pl.DeviceIdType.MESH)
        for a in range(len(p_refs)) for k, (px, py) in enumerate(chips)]


def _chip_exchange_begin(parts, name):
    n_arr = len(parts)
    n_buf, n_copy = 2 * n_arr, 3 * n_arr
    lands = [lax.empty(p.shape, p.dtype) for p in parts]

    def body(*refs):
        p_refs, land_refs = refs[:n_arr], refs[n_arr:n_buf]
        send_sems, recv_sems, token = refs[n_buf:n_buf + n_copy], refs[n_buf + n_copy:n_buf + 2 * n_copy], refs[-1]
        for cp in _chip_copies(p_refs, land_refs, send_sems, recv_sems):
            cp.start()
        token[...] = jnp.zeros_like(token)

    bufs = list(parts) + list(lands)
    outs = pl.pallas_call(
        body, name=name,
        out_shape=(*[pltpu.SemaphoreType.DMA(())] * (2 * n_copy), *[pltpu.HBM(b.shape, b.dtype) for b in bufs],
                   jax.ShapeDtypeStruct((8, 128), F32)),
        in_specs=[_HBM] * n_buf,
        out_specs=(*[_SEM] * (2 * n_copy), *[_HBM] * n_buf, pl.BlockSpec(memory_space=pltpu.VMEM)),
        input_output_aliases={i: 2 * n_copy + i for i in range(n_buf)},
        compiler_params=pltpu.CompilerParams(has_side_effects=_DATAFLOW),
    )(*[pltpu.with_memory_space_constraint(b, pltpu.HBM) for b in bufs])
    sems = list(outs[:2 * n_copy])
    thru = list(outs[2 * n_copy:2 * n_copy + n_buf])
    return types.SimpleNamespace(send_sems=sems[:n_copy], recv_sems=sems[n_copy:], parts=thru[:n_arr],
                                 lands=thru[n_arr:], token=outs[-1])


def _chip_exchange_end(flight, after, name):
    send_sems, recv_sems, parts, lands = flight.send_sems, flight.recv_sems, flight.parts, flight.lands
    n_arr = len(parts)
    n_buf, n_copy = 2 * n_arr, 3 * n_arr

    def body(*refs):
        p_refs, land_refs = refs[:n_arr], refs[n_arr:n_buf]
        sems = refs[n_buf:n_buf + 2 * n_copy]
        for cp in _chip_copies(p_refs, land_refs, sems[:n_copy], sems[n_copy:]):
            cp.wait_send()
            cp.wait_recv()

    bufs = list(parts) + list(lands)
    outs = pl.pallas_call(
        body, name=name, out_shape=tuple(pltpu.HBM(b.shape, b.dtype) for b in bufs),
        in_specs=[_HBM] * n_buf + [_SEM] * (2 * n_copy) + [_ANY], out_specs=tuple([_HBM] * n_buf),
        input_output_aliases={i: i for i in range(n_buf)},
        compiler_params=pltpu.CompilerParams(has_side_effects=_DATAFLOW),
    )(*bufs, *send_sems, *recv_sems, after)
    return list(outs[:n_arr]), list(outs[n_arr:])


def _row_tile(r, cap=640):
    best = None
    for cand in range(16, min(r, cap) + 1, 16):
        if r % cand == 0:
            best = cand
    return best if best is not None else r


def _pair_sum(g, landed, core, name):
    _, r, c_dim = g.shape
    tr = _row_tile(r)

    def body(core_ref, mine_ref, theirs_ref, o_ref):
        o_ref[0] = (mine_ref[0].astype(F32) + theirs_ref[0].astype(F32)).astype(o_ref.dtype)

    return pl.pallas_call(
        body, name=name,
        grid_spec=pltpu.PrefetchScalarGridSpec(
            num_scalar_prefetch=1, grid=(N_CHIP, r // tr),
            in_specs=[pl.BlockSpec((1, tr, c_dim), lambda k, i, core_ref: (2 * k + core_ref[0], i, 0)),
                      pl.BlockSpec((1, tr, c_dim), lambda k, i, core_ref: (k, i, 0))],
            out_specs=pl.BlockSpec((1, tr, c_dim), lambda k, i, core_ref: (k, i, 0))),
        out_shape=jax.ShapeDtypeStruct((N_CHIP, r, c_dim), g.dtype),
        compiler_params=_cparams(2),
    )(core, g, landed)


def _sum_slots(parts, name):
    n, r, c_dim = parts.shape
    tr = _row_tile(r)

    def body(p_ref, o_ref):
        acc = p_ref[0].astype(F32)
        for k in range(1, n):
            acc = acc + p_ref[k].astype(F32)
        o_ref[...] = acc

    return pl.pallas_call(
        body, grid=(r // tr,), name=name,
        in_specs=[pl.BlockSpec((n, tr, c_dim), lambda i: (0, i, 0))],
        out_specs=_row_spec(tr, c_dim),
        out_shape=jax.ShapeDtypeStruct((r, c_dim), F32),
        compiler_params=_cparams(),
    )(parts)


GAINS = ("g_ffn1", "g_mix", "g_cross", "g_mem", "g_ffn2", "g_final")
SMALL = GAINS + ("b_gate", "conv_w")
SMALL_R = 16
LOSS_ROW = 11
WEIGHT_ORDER = ("g_ffn1", "w_ffn1_gu", "w_ffn1_down", "g_mix", "w_in", "b_gate", "conv_w", "w_conv_out",
                "w_attn_out", "w_o", "g_cross", "g_mem", "w_cq", "w_ckv", "w_co", "g_ffn2", "w_ffn2_gu",
                "w_ffn2_down", "g_final")
GU_NAMES = ("w_ffn1_gu", "w_ffn2_gu")


def _pack_small(vals, conv_rows):
    rows = [vals[n].reshape(1, D) for n in GAINS] + [vals["b_gate"].reshape(2, D), conv_rows.reshape(CONV_K, D)]
    used = len(GAINS) + 2 + CONV_K
    return jnp.concatenate(rows + [jnp.zeros((SMALL_R - used, D), F32)], axis=0)


def _unpack_small(buf):
    out = {n: buf[k] for k, n in enumerate(GAINS)}
    out["b_gate"] = buf[6:8].reshape(2 * D)
    out["conv_w"] = buf[8:8 + CONV_K]
    return out


def _exchange_shards(wts):
    out = {n: jnp.pad(wts[n].T.astype(BF16), ((0, FF_PAD - FF_BLK), (0, 0))) for n in GU_NAMES}
    for n in ("w_ckv", "w_in", "w_ffn1_down", "w_ffn2_down"):
        out[n] = wts[n].astype(BF16)
    out["mix"] = jnp.concatenate([wts[n].astype(BF16) for n in MIX_MATS], axis=0)
    out["cross"] = jnp.concatenate([wts[n].astype(BF16) for n in CROSS_MATS], axis=0)
    return out


def _reduce_group(grads, landed, core, names):
    return [_pair_sum(g, l, core, "grads_pair_sum_" + n) for g, l, n in zip(grads, landed, names)]


def _step(x, mem, target, sh, conv_pad, gains, b_gate, core):
    wg1, wd1, conv_all = _run_exchange(_gather_exchange([sh["w_ffn1_gu"], sh["w_ffn1_down"], conv_pad]), "gather_ffn1")
    conv_w = conv_all[:, :CONV_K, :].transpose(1, 0, 2).reshape(CONV_K, D)
    (n1, gate1, up1, act1, h1), (w_in,) = _ffn_fwd(
        x, gains["g_ffn1"], wg1, wd1, "ffn1_fwd", comm=_gather_exchange([sh["w_in"]]))
    (u, pcg, qkv), (w_mix,) = _inproj_fwd(h1, gains["g_mix"], w_in, "inproj_fwd", comm=_gather_exchange([sh["mix"]]))
    yc = _conv_fwd(pcg, conv_w, "conv_fwd")
    (ysb, ctot), (w_cross, w_ckv, wg2, wd2) = _sb_fwd(
        qkv, "sb_fwd", comm=_gather_exchange([sh["cross"], sh["w_ckv"], sh["w_ffn2_gu"], sh["w_ffn2_down"]]))
    a_mix, b_mix, merged, h2 = _mix_out_fwd(yc, ysb, pcg, b_gate, h1, w_mix, "mix_out_fwd")
    mn, kv = _memkv_fwd(mem, gains["g_mem"], w_ckv, "memkv_fwd")
    hn, qx, o_x, h3 = _cross_fwd(h2, gains["g_cross"], kv, w_cross, "cross_fwd")
    (n4, gate2, up2, act2, dh4, loss, dg_final), _ = _ffn_fwd(h3, gains["g_ffn2"], wg2, wd2, "ffn2_fwd",
                                                              head=(gains["g_final"], target))

    gs = {"g_final": dg_final}
    (dgu2, dh4b, dh3, gs["g_ffn2"]), _ = _ffn_bwd(dh4, h3, gains["g_ffn2"], gate2, up2, wg2, wd2, "ffn2_bwd")
    grads_a = [_mm_tn_rows(dgu2, n4, FF_PAD, "dw_ffn2_gu"),
               _mm_tn_rows(act2, dh4b, FF_BLK, "dw_ffn2_down").reshape(N_DEV, DOWN_ROWS, D)]
    names_a = ["w_ffn2_gu", "w_ffn2_down"]
    (dh3b, dqx, dkv, dh2, gs["g_cross"]), landed_a = _cross_bwd(
        dh3, h2, gains["g_cross"], qx, kv, w_cross, "cross_bwd", comm=_pair_exchange(grads_a))
    sums_a = _reduce_group(grads_a, landed_a, core, names_a)
    cross_stack = _mm_tn_square(hn, dqx, "dw_cq", 0, len(CROSS_MATS))
    grads_b = [_mm_tn_cols(mn, dkv, "dw_ckv"), _mm_tn_square(o_x, dh3b, "dw_co", 1, len(CROSS_MATS), cross_stack)]
    names_b = ["w_ckv", "cross"]
    gs["g_mem"] = _memkv_bwd(dkv, mem, gains["g_mem"], w_ckv, "memkv_bwd")
    (dh2b, da_mix, db_mix, dgp, dyc, dysb, gs["b_gate"]), landed_b = _mix_out_bwd(
        dh2, a_mix, b_mix, pcg, b_gate, w_mix, "mix_out_bwd", comm=_pair_exchange(grads_b))
    sums_b = _reduce_group(grads_b, landed_b, core, names_b)
    mix_stack = _mm_tn_square(yc, da_mix, "dw_conv_out", 0, len(MIX_MATS))
    mix_stack = _mm_tn_square(ysb, db_mix, "dw_attn_out", 1, len(MIX_MATS), mix_stack)
    grads_c = [_mm_tn_square(merged, dh2b, "dw_o", 2, len(MIX_MATS), mix_stack)]
    landed_c = _run_exchange(_pair_exchange(grads_c), "grads_to_sibling_mix")
    sums_c = _reduce_group(grads_c, landed_c, core, ["mix"])
    flight_abc = _chip_exchange_begin(sums_a + sums_b + sums_c, "grads_to_chips_early_begin")
    (dq, dkv_sb), _ = _sb_bwd(qkv, dysb, ctot, flight_abc.token, "sb_bwd")
    dconv, gs["conv_w"] = _conv_bwd(pcg, conv_w, dyc, "conv_bwd")
    w_in_stack = _mm_tn_cols(u, dconv, "dw_in_conv", 0, N_DEV)
    w_in_stack = _mm_tn_cols(u, dq[None], "dw_in_q", 3, N_DEV, w_in_stack)
    w_in_stack = _mm_tn_cols(u, dkv_sb, "dw_in_kv", 4, N_DEV, w_in_stack)
    grads_d = [_mm_tn_cols(u, dgp, "dw_in_gates", 6, N_DEV, w_in_stack)]
    (dh1, gs["g_mix"]), landed_d = _inproj_bwd(dconv, dq, dkv_sb, dgp, w_in, h1, gains["g_mix"], dh2, "inproj_bwd",
                                               comm=_pair_exchange(grads_d))
    sums_d = _reduce_group(grads_d, landed_d, core, ["w_in"])
    flight_d = _chip_exchange_begin(sums_d, "grads_to_chips_w_in_begin")
    (dgu1, dh1b, dx, gs["g_ffn1"]), _ = _ffn_bwd(dh1, x, gains["g_ffn1"] + flight_d.token[0, 0], gate1, up1, wg1, wd1,
                                                 "ffn1_bwd")
    grads_e = [_mm_tn_rows(dgu1, n1, FF_PAD, "dw_ffn1_gu"),
               _mm_tn_rows(act1, dh1b, FF_BLK, "dw_ffn1_down").reshape(N_DEV, DOWN_ROWS, D)]
    names_e = ["w_ffn1_gu", "w_ffn1_down"]
    landed_e = _run_exchange(_pair_exchange(grads_e), "grads_to_sibling_ffn1")
    flight_e = _chip_exchange_begin(_reduce_group(grads_e, landed_e, core, names_e), "grads_to_chips_ffn1_begin")
    flights = [(names_a + names_b + ["mix"], flight_abc), (["w_in"], flight_d), (names_e, flight_e)]
    return loss, dx, flights, gs


def kernel(x, mem, g_ffn1, w_ffn1_gu, w_ffn1_down, g_mix, w_in, b_gate, conv_w, w_conv_out, w_attn_out, w_o, g_cross, g_mem, w_cq, w_ckv, w_co, g_ffn2, w_ffn2_gu, w_ffn2_down, g_final, loss_target, m_g_ffn1, m_w_ffn1_gu, m_w_ffn1_down, m_g_mix, m_w_in, m_b_gate, m_conv_w, m_w_conv_out, m_w_attn_out, m_w_o, m_g_cross, m_g_mem, m_w_cq, m_w_ckv, m_w_co, m_g_ffn2, m_w_ffn2_gu, m_w_ffn2_down, m_g_final, v_g_ffn1, v_w_ffn1_gu, v_w_ffn1_down, v_g_mix, v_w_in, v_b_gate, v_conv_w, v_w_conv_out, v_w_attn_out, v_w_o, v_g_cross, v_g_mem, v_w_cq, v_w_ckv, v_w_co, v_g_ffn2, v_w_ffn2_gu, v_w_ffn2_down, v_g_final):
    args = locals()
    wts = {n: args[n] for n in WEIGHT_ORDER}
    mom1 = {n: args["m_" + n] for n in WEIGHT_ORDER}
    mom2 = {n: args["v_" + n] for n in WEIGHT_ORDER}
    cx, cy, cc = _mesh_pos()
    dev = 4 * cx + 2 * cy + cc
    conv_cols = D // N_DEV

    conv_pad = jnp.concatenate([conv_w, jnp.zeros((SMALL_R - CONV_K, conv_cols), F32)], axis=0)
    gains = {n: wts[n].reshape(1, D) for n in GAINS}
    loss8, dx, flights, gs = _step(x[0], mem[0], loss_target[0], _exchange_shards(wts), conv_pad, gains,
                                 b_gate.reshape(1, 2 * D), cc.reshape(1).astype(jnp.int32))

    grads, delta, new_m, new_v = {}, {}, {}, {}

    def operands(n, transposed):
        trio = (wts[n], mom1[n], mom2[n])
        return tuple(a.T for a in trio) if transposed else trio

    def record(n, res, transposed):
        grads[n], delta[n], new_m[n], new_v[n] = [r.T for r in res] if transposed else res

    early = [("w_ffn2_gu", "w_ffn2_gu", 0, True), ("w_ffn2_down", "w_ffn2_down", 0, False),
             ("w_ckv", "w_ckv", 0, False), ("w_in", "w_in", 0, False)]
    early += [(n, "mix", k, False) for k, n in enumerate(MIX_MATS)]
    early += [(n, "cross", k, False) for k, n in enumerate(CROSS_MATS)]
    chip = (2 * cx + cy).reshape(1).astype(jnp.int32)
    (names_early, flight_early), (names_w_in, flight_w_in), (last_names, flight_last) = flights
    token = flight_last.token
    own, land = {}, {}
    for names, flight, tag in ((names_early, flight_early, "early"), (names_w_in, flight_w_in, "w_in")):
        own_parts, landed = _chip_exchange_end(flight, token, "grads_to_chips_%s_end" % tag)
        own.update(zip(names, own_parts))
        land.update(zip(names, landed))
    for n, buf, row_block, transposed in early:
        w, m1, m2 = operands(n, transposed)
        record(n, _adamw_own(w, land[buf], own[buf], chip, m1, m2, "adamw_" + n, row_block, token), transposed)

    after = jnp.concatenate([new_v[n][:1, :1] for n, _, _, _ in early], axis=0)
    own_parts, landed = _chip_exchange_end(flight_last, after, "grads_to_chips_ffn1_end")
    for n, own_n, land_n, transposed in zip(last_names, own_parts, landed, (True, False)):
        w, m1, m2 = operands(n, transposed)
        record(n, _adamw_own(w, land_n, own_n, chip, m1, m2, "adamw_" + n), transposed)

    gs_rows = {n: gs[n] for n in GAINS + ("b_gate",)}
    small_mine = _pack_small(gs_rows, gs["conv_w"][:CONV_K]) + new_v[last_names[-1]][0, 0] * 0.0
    small_mine = small_mine.at[LOSS_ROW, 0].set(loss8[0, 0])
    small_all = _run_exchange(_gather_exchange([small_mine]), "gather_small_grads")[0]
    small_sum = _sum_slots(small_all, "small_grads_sum")
    loss = small_sum[LOSS_ROW, 0]
    grad_small = _unpack_small(small_sum)
    grad_small["conv_w"] = lax.dynamic_slice_in_dim(grad_small["conv_w"], dev * conv_cols, conv_cols, axis=1)
    grads.update(grad_small)

    def small_buf(vals):
        return _pack_small(vals, jnp.concatenate([vals["conv_w"], jnp.zeros((CONV_K, D - conv_cols), F32)], axis=1))

    _, d_s, m_s, v_s = _adamw(small_buf(wts), small_buf(grads)[None], small_buf(mom1), small_buf(mom2), "adamw_small")
    for res, buf in ((delta, d_s), (new_m, m_s), (new_v, v_s)):
        un = _unpack_small(buf)
        for n in GAINS + ("b_gate",):
            res[n] = un[n]
        res["conv_w"] = un["conv_w"][:, :conv_cols]

    return (loss, dx[None], *[grads[n] for n in WEIGHT_ORDER], *[delta[n] for n in WEIGHT_ORDER],
            *[new_m[n] for n in WEIGHT_ORDER], *[new_v[n] for n in WEIGHT_ORDER])
```

```python
import types

import jax
import jax.numpy as jnp
from jax import lax
from jax.experimental import pallas as pl
from jax.experimental.pallas import tpu as pltpu

F32 = jnp.float32
BF16 = jnp.bfloat16

D = 1024
DFF = 2816
SB_H = 8
SB_DH = 128
X_H = 4
X_DH = 256
CONV_K = 3
RMS_EPS = 1e-6
N_DEV = 8
N_CHIP = 4
SQ_ROWS = D // N_DEV

ADAM_LR = 0.001
ADAM_B1 = 0.9
ADAM_B2 = 0.999
ADAM_EPS = 1e-08
ADAM_WD = 0.01
ADAM_STEP = 10

TM = 256
TQ = 512
TK = 256
SB_HPS = 2
VMEM_LIMIT = 56 << 20

FF_BLK = DFF // 4
FF_PAD = 768
FF_SUB = 256
DOWN_ROWS = DFF // N_DEV

MIX_MATS = ("w_conv_out", "w_attn_out", "w_o")
CROSS_MATS = ("w_cq", "w_co")

_ANY = pl.BlockSpec(memory_space=pl.ANY)


def _cparams(n_axes=1):
    return pltpu.CompilerParams(
        dimension_semantics=("arbitrary",) * n_axes, vmem_limit_bytes=VMEM_LIMIT)


def _row_spec(tm, n):
    return pl.BlockSpec((tm, n), lambda i: (i, 0))


def _blk_row_spec(nb, tm, n):
    return pl.BlockSpec((nb, tm, n), lambda i: (0, i, 0))


def _const_spec(shape):
    zeros = (0,) * len(shape)
    return pl.BlockSpec(shape, lambda i: zeros)


def _dot(a, b):
    return jnp.dot(a, b, preferred_element_type=F32)


def _dot_nt(a, b):
    return lax.dot_general(a, b, (((1,), (1,)), ((), ())), preferred_element_type=F32)


def _dot_tn(a, b):
    return lax.dot_general(a, b, (((0,), (0,)), ((), ())), preferred_element_type=F32)


def _sigmoid(x):
    return 1.0 / (1.0 + jnp.exp(-x))


def _call(body, operands, *, grid, in_specs, out_specs, out_shape, scratch_shapes, name, comm=None):
    n_in, n_out, n_sc = len(in_specs), len(out_specs), len(scratch_shapes)
    if comm is None:
        outs = pl.pallas_call(
            body, grid=grid, name=name, in_specs=in_specs, out_specs=out_specs, out_shape=out_shape,
            scratch_shapes=scratch_shapes, compiler_params=_cparams(len(grid)))(*operands)
        return list(outs), []
    c_in, c_out, c_sem = len(comm.inputs), len(comm.out_shapes), len(comm.sem_shapes)

    def hosted(*refs):
        bounds = [0, n_in, c_in, n_out, c_out, n_sc, c_sem]
        parts, pos = [], 0
        for k in bounds[1:]:
            parts.append(refs[pos:pos + k])
            pos += k
        ins, cins, outs, couts, scr, sems = parts
        step, n_steps = pl.program_id(0), grid[0]
        for ax in range(1, len(grid)):
            step, n_steps = step * grid[ax] + pl.program_id(ax), n_steps * grid[ax]

        @pl.when(step == 0)
        def _():
            comm.start(cins, couts, sems)

        @pl.when(step == (2 * n_steps) // 3)
        def _():
            comm.middle(cins, couts, sems)

        body(*ins, *outs, *scr)

        @pl.when(step == n_steps - 1)
        def _():
            comm.finish(cins, couts, sems)

    res = pl.pallas_call(
        hosted, grid=grid, name=name, in_specs=list(in_specs) + [_ANY] * c_in,
        out_specs=list(out_specs) + [_ANY] * c_out, out_shape=list(out_shape) + list(comm.out_shapes),
        scratch_shapes=list(scratch_shapes) + list(comm.sem_shapes),
        compiler_params=_cparams(len(grid)))(*operands, *comm.inputs)
    return list(res[:n_out]), list(res[n_out:])


def _load_resident(step, pairs, sems):
    @pl.when(step == 0)
    def _():
        copies = [pltpu.make_async_copy(src, dst, sems.at[k]) for k, (src, dst) in enumerate(pairs)]
        for cp in copies:
            cp.start()
        for cp in copies:
            cp.wait()


def _square_pairs(buf_hbm, index, dst):
    off = index * SQ_ROWS
    return [(buf_hbm.at[d, off:off + SQ_ROWS, :], dst.at[d * SQ_ROWS:(d + 1) * SQ_ROWS, :]) for d in range(N_DEV)]


def _down_pairs(wd_hbm, dst):
    return [(wd_hbm.at[d], dst.at[d // 2, (d % 2) * DOWN_ROWS:(d % 2 + 1) * DOWN_ROWS, :]) for d in range(N_DEV)]


def _zero_down_pad(step, dst):
    @pl.when(step == 0)
    def _():
        dst[:, FF_BLK:, :] = jnp.zeros((4, FF_PAD - FF_BLK, D), BF16)


def _rms_fwd_tile(xt, g):
    r = lax.rsqrt(jnp.mean(xt * xt, axis=-1, keepdims=True) + RMS_EPS)
    return (xt * r) * g


def _rms_bwd_tile(xt, g, dn):
    r = lax.rsqrt(jnp.mean(xt * xt, axis=-1, keepdims=True) + RMS_EPS)
    xhat = xt * r
    dxhat = dn * g
    dx = r * (dxhat - xhat * jnp.mean(dxhat * xhat, axis=-1, keepdims=True))
    dg = jnp.sum(dn * xhat, axis=0, keepdims=True)
    return dx, dg


def _accumulate(ref, step, value):
    @pl.when(step == 0)
    def _():
        ref[...] = value

    @pl.when(step != 0)
    def _():
        ref[...] = ref[...] + value


def _ffn_fwd(x, g, wgu, wd, name, comm=None, head=None):
    t = x.shape[0]

    def body(x_ref, g_ref, wgu_hbm, wd_hbm, *refs):
        if head is None:
            n_ref, gate_ref, up_ref, act_ref, h_ref, wgu_v, wd_v, sems = refs
        else:
            gf_ref, t_ref, n_ref, gate_ref, up_ref, act_ref, dh_ref, loss_ref, dgf_ref, wgu_v, wd_v, sems = refs
        step = pl.program_id(0)
        _zero_down_pad(step, wd_v)
        _load_resident(step, [(wgu_hbm, wgu_v)] + _down_pairs(wd_hbm, wd_v), sems)
        xt = x_ref[...]
        n = _rms_fwd_tile(xt, g_ref[...]).astype(BF16)
        n_ref[...] = n
        acc = jnp.zeros((TM, D), F32)
        for j in range(4):
            for s in range(FF_PAD // FF_SUB):
                lo, hi = s * FF_SUB, (s + 1) * FF_SUB
                gt = _dot_nt(n, wgu_v[j, lo:hi, :])
                ut = _dot_nt(n, wgu_v[4 + j, lo:hi, :])
                gate_ref[j, :, lo:hi] = gt.astype(BF16)
                up_ref[j, :, lo:hi] = ut.astype(BF16)
                act_ref[j, :, lo:hi] = ((gt * _sigmoid(gt)) * ut).astype(BF16)
            acc = acc + _dot(act_ref[j], wd_v[j])
        ht = xt + 0.5 * acc
        if head is None:
            h_ref[...] = ht
        else:
            gain = gf_ref[...]
            diff = _rms_fwd_tile(ht, gain) - t_ref[...]
            part = 0.5 * jnp.sum(jnp.sum(diff * diff, axis=-1, keepdims=True) / D, axis=0, keepdims=True)
            dx, dg = _rms_bwd_tile(ht, gain, diff / D)
            dh_ref[...] = dx
            _accumulate(loss_ref, step, jnp.broadcast_to(part, (8, 128)))
            _accumulate(dgf_ref, step, dg)

    ff = jax.ShapeDtypeStruct((4, t, FF_PAD), BF16)
    operands, in_specs = (x, g, wgu, wd), [_row_spec(TM, D), _const_spec((1, D)), _ANY, _ANY]
    out_specs = [_row_spec(TM, D)] + [_blk_row_spec(4, TM, FF_PAD)] * 3 + [_row_spec(TM, D)]
    out_shape = [jax.ShapeDtypeStruct((t, D), BF16), ff, ff, ff, jax.ShapeDtypeStruct((t, D), F32)]
    if head is not None:
        operands += tuple(head)
        in_specs += [_const_spec((1, D)), _row_spec(TM, D)]
        out_specs += [_const_spec((8, 128)), _const_spec((1, D))]
        out_shape += [jax.ShapeDtypeStruct((8, 128), F32), jax.ShapeDtypeStruct((1, D), F32)]
    return _call(
        body, operands, grid=(t // TM,), name=name, comm=comm, in_specs=in_specs, out_specs=out_specs,
        out_shape=out_shape,
        scratch_shapes=[pltpu.VMEM((N_DEV, FF_PAD, D), BF16), pltpu.VMEM((4, FF_PAD, D), BF16),
                        pltpu.SemaphoreType.DMA((1 + N_DEV,))])


def _ffn_bwd(dh, xin, g, gate, up, wgu, wd, name, comm=None):
    t = dh.shape[0]

    def body(dh_ref, x_ref, g_ref, gate_ref, up_ref, wgu_hbm, wd_hbm,
             dgu_ref, dhb_ref, dx_ref, dg_ref, wgu_v, wd_v, sems):
        step = pl.program_id(0)
        _zero_down_pad(step, wd_v)
        _load_resident(step, [(wgu_hbm, wgu_v)] + _down_pairs(wd_hbm, wd_v), sems)
        dht = dh_ref[...]
        dhb = (0.5 * dht).astype(BF16)
        dhb_ref[...] = dhb
        dn = jnp.zeros((TM, D), F32)
        for j in range(4):
            for s in range(FF_PAD // FF_SUB):
                lo, hi = s * FF_SUB, (s + 1) * FF_SUB
                da = _dot_nt(dhb, wd_v[j, lo:hi, :])
                gt = gate_ref[j, :, lo:hi].astype(F32)
                ut = up_ref[j, :, lo:hi].astype(F32)
                sg = _sigmoid(gt)
                dgt = (da * ut * (sg * (1.0 + gt * (1.0 - sg)))).astype(BF16)
                dut = (da * (gt * sg)).astype(BF16)
                dgu_ref[j, :, lo:hi] = dgt
                dgu_ref[4 + j, :, lo:hi] = dut
            dn = dn + _dot(dgu_ref[j], wgu_v[j]) + _dot(dgu_ref[4 + j], wgu_v[4 + j])
        dx, dg = _rms_bwd_tile(x_ref[...], g_ref[...], dn)
        dx_ref[...] = dht + dx
        _accumulate(dg_ref, step, dg)

    return _call(
        body, (dh, xin, g, gate, up, wgu, wd), grid=(t // TM,), name=name, comm=comm,
        in_specs=[_row_spec(TM, D), _row_spec(TM, D), _const_spec((1, D)), _blk_row_spec(4, TM, FF_PAD),
                  _blk_row_spec(4, TM, FF_PAD), _ANY, _ANY],
        out_specs=[_blk_row_spec(N_DEV, TM, FF_PAD), _row_spec(TM, D), _row_spec(TM, D), _const_spec((1, D))],
        out_shape=[jax.ShapeDtypeStruct((N_DEV, t, FF_PAD), BF16), jax.ShapeDtypeStruct((t, D), BF16),
                   jax.ShapeDtypeStruct((t, D), F32), jax.ShapeDtypeStruct((1, D), F32)],
        scratch_shapes=[pltpu.VMEM((N_DEV, FF_PAD, D), BF16), pltpu.VMEM((4, FF_PAD, D), BF16),
                        pltpu.SemaphoreType.DMA((1 + N_DEV,))])


WIDE_TILES = (1024, 512, 256, 128)


def _pick_tile(n, options=(512, 256, 128)):
    for o in options:
        if n % o == 0:
            return o
    return n


def _into(stack, n_operands):
    if stack is None:
        return (), [], {}
    return (stack,), [_ANY], {n_operands: 0}


def _mm_tn_square(a, b, name, index, count, stack=None):
    k, m = a.shape
    _, n = b.shape
    tn = _pick_tile(n)
    extra, extra_specs, aliases = _into(stack, 2)

    def body(a_ref, b_ref, *rest):
        rest[-1][...] = _dot_tn(a_ref[...], b_ref[...]).astype(BF16).reshape(N_DEV, m // N_DEV, tn)

    return pl.pallas_call(
        body, grid=(n // tn,), name=name,
        in_specs=[pl.BlockSpec((k, m), lambda j: (0, 0)), pl.BlockSpec((k, tn), lambda j: (0, j))] + extra_specs,
        out_specs=pl.BlockSpec((N_DEV, m // N_DEV, tn), lambda j: (0, index, j)),
        out_shape=jax.ShapeDtypeStruct((N_DEV, count * (m // N_DEV), n), BF16),
        input_output_aliases=aliases, compiler_params=_cparams(1),
    )(a, b, *extra)


def _mm_tn_cols(a, b, name, first=0, count=None, stack=None):
    k, m = a.shape
    nb, _, n = b.shape
    tm = _pick_tile(m, WIDE_TILES)
    extra, extra_specs, aliases = _into(stack, 2)

    def body(a_ref, b_ref, *rest):
        rest[-1][0] = _dot_tn(a_ref[...].astype(BF16), b_ref[0].astype(BF16)).astype(BF16)

    return pl.pallas_call(
        body, grid=(nb, m // tm), name=name,
        in_specs=[pl.BlockSpec((k, tm), lambda j, i: (0, i)), pl.BlockSpec((1, k, n), lambda j, i: (j, 0, 0))]
                 + extra_specs,
        out_specs=pl.BlockSpec((1, tm, n), lambda j, i: (j + first, i, 0)),
        out_shape=jax.ShapeDtypeStruct((nb if count is None else count, m, n), BF16),
        input_output_aliases=aliases, compiler_params=_cparams(2),
    )(a, b, *extra)


def _mm_tn_rows(a, b, keep, name):
    nb, k, m = a.shape
    _, n = b.shape
    tn = _pick_tile(n, WIDE_TILES)

    def body(a_ref, b_ref, o_ref):
        o_ref[0] = _dot_tn(a_ref[0], b_ref[...])[:keep].astype(BF16)

    return pl.pallas_call(
        body, grid=(nb, n // tn), name=name,
        in_specs=[pl.BlockSpec((1, k, m), lambda j, i: (j, 0, 0)), pl.BlockSpec((k, tn), lambda j, i: (0, i))],
        out_specs=pl.BlockSpec((1, keep, tn), lambda j, i: (j, 0, i)),
        out_shape=jax.ShapeDtypeStruct((nb, keep, n), BF16),
        compiler_params=_cparams(2),
    )(a, b)


PCG_W = 5 * D
QKV_W = 3 * D
PROJ_SUB = 512


def _inproj_fwd(h, g, w_in, name, comm=None):
    t = h.shape[0]

    def body(h_ref, g_ref, w_hbm, u_ref, pcg_ref, qkv_ref, w_v, sems):
        _load_resident(pl.program_id(0), [(w_hbm, w_v)], sems)
        u = _rms_fwd_tile(h_ref[...], g_ref[...]).astype(BF16)
        u_ref[...] = u
        for blk in range(N_DEV):
            for s in range(D // PROJ_SUB):
                lo, hi = s * PROJ_SUB, (s + 1) * PROJ_SUB
                p = _dot(u, w_v[blk, :, lo:hi])
                if blk < 3:
                    pcg_ref[:, blk * D + lo:blk * D + hi] = p
                elif blk < 6:
                    qkv_ref[:, (blk - 3) * D + lo:(blk - 3) * D + hi] = p.astype(BF16)
                else:
                    pcg_ref[:, (blk - 3) * D + lo:(blk - 3) * D + hi] = p

    return _call(
        body, (h, g, w_in), grid=(t // TM,), name=name, comm=comm,
        in_specs=[_row_spec(TM, D), _const_spec((1, D)), _ANY],
        out_specs=[_row_spec(TM, D), _row_spec(TM, PCG_W), _row_spec(TM, QKV_W)],
        out_shape=[jax.ShapeDtypeStruct((t, D), BF16), jax.ShapeDtypeStruct((t, PCG_W), F32),
                   jax.ShapeDtypeStruct((t, QKV_W), BF16)],
        scratch_shapes=[pltpu.VMEM((N_DEV, D, D), BF16), pltpu.SemaphoreType.DMA((1,))])


CONV_CW = 256


def _shift_down(v, k, rows):
    return jnp.where(rows >= k, pltpu.roll(v, k, 0), 0.0)


def _shift_up(v, k, rows, t):
    return jnp.where(rows < t - k, pltpu.roll(v, t - k, 0), 0.0)


def _col_spec(t, cw, off):
    return pl.BlockSpec((t, cw), lambda j: (0, j + off))


def _conv_fwd(pcg, conv_w, name):
    t = pcg.shape[0]
    nb = D // CONV_CW

    def body(cb_ref, cc_ref, cx_ref, w_ref, y_ref):
        rows = lax.broadcasted_iota(jnp.int32, (t, CONV_CW), 0)
        xc = cc_ref[...] * cx_ref[...]
        conv = (w_ref[0:1, :] * _shift_down(xc, 2, rows) + w_ref[1:2, :] * _shift_down(xc, 1, rows)
                + w_ref[2:3, :] * xc)
        y_ref[...] = (cb_ref[...] * conv).astype(BF16)

    return pl.pallas_call(
        body, grid=(nb,), name=name,
        in_specs=[_col_spec(t, CONV_CW, 0), _col_spec(t, CONV_CW, nb), _col_spec(t, CONV_CW, 2 * nb),
                  pl.BlockSpec((CONV_K, CONV_CW), lambda j: (0, j))],
        out_specs=_col_spec(t, CONV_CW, 0),
        out_shape=jax.ShapeDtypeStruct((t, D), BF16),
        compiler_params=_cparams(),
    )(pcg, pcg, pcg, conv_w)


def _conv_bwd(pcg, conv_w, dyc, name):
    t = pcg.shape[0]
    nb = D // CONV_CW

    def body(cb_ref, cc_ref, cx_ref, w_ref, dy_ref, dc_ref, dw_ref):
        rows = lax.broadcasted_iota(jnp.int32, (t, CONV_CW), 0)
        cc, cx = cc_ref[...], cx_ref[...]
        xc = cc * cx
        x1 = _shift_down(xc, 1, rows)
        x2 = _shift_down(xc, 2, rows)
        w0, w1, w2 = w_ref[0:1, :], w_ref[1:2, :], w_ref[2:3, :]
        conv = w0 * x2 + w1 * x1 + w2 * xc
        dy = dy_ref[...]
        dc_ref[0] = (dy * conv).astype(BF16)
        dconv = dy * cb_ref[...]
        dw_ref[...] = jnp.zeros((8, CONV_CW), F32)
        dw_ref[0:1, :] = jnp.sum(dconv * x2, axis=0, keepdims=True)
        dw_ref[1:2, :] = jnp.sum(dconv * x1, axis=0, keepdims=True)
        dw_ref[2:3, :] = jnp.sum(dconv * xc, axis=0, keepdims=True)
        dxc = w2 * dconv + w1 * _shift_up(dconv, 1, rows, t) + w0 * _shift_up(dconv, 2, rows, t)
        dc_ref[1] = (dxc * cx).astype(BF16)
        dc_ref[2] = (dxc * cc).astype(BF16)

    return pl.pallas_call(
        body, grid=(nb,), name=name,
        in_specs=[_col_spec(t, CONV_CW, 0), _col_spec(t, CONV_CW, nb), _col_spec(t, CONV_CW, 2 * nb),
                  pl.BlockSpec((CONV_K, CONV_CW), lambda j: (0, j)), _col_spec(t, CONV_CW, 0)],
        out_specs=[pl.BlockSpec((3, t, CONV_CW), lambda j: (0, 0, j)), pl.BlockSpec((8, CONV_CW), lambda j: (0, j))],
        out_shape=[jax.ShapeDtypeStruct((3, t, D), BF16), jax.ShapeDtypeStruct((8, D), F32)],
        compiler_params=_cparams(),
    )(pcg, pcg, pcg, conv_w, dyc)


def _tri2(cond):
    rr = lax.broadcasted_iota(jnp.int32, (2 * TK, TK), 0) & (TK - 1)
    cc = lax.broadcasted_iota(jnp.int32, (2 * TK, TK), 1)
    return cond(rr, cc).astype(BF16)


def _causal(shift, row0=0):
    rr = lax.broadcasted_iota(jnp.int32, (TQ - row0, TK), 0) + row0
    cc = lax.broadcasted_iota(jnp.int32, (TQ - row0, TK), 1)
    return cc + shift < rr


def _cumdot(v, tri2):
    hi = v.astype(BF16)
    lo = (v - hi.astype(F32)).astype(BF16)
    return _dot(jnp.concatenate([hi, lo], axis=1), tri2)


def _log_1m_beta(z):
    return -(jnp.maximum(z, 0.0) + jnp.log(1.0 + jnp.exp(-jnp.abs(z))))


def _sb_specs(t):
    g = SB_H // SB_HPS
    w = SB_HPS * SB_DH
    q_spec = pl.BlockSpec((TQ, w), lambda h, i: (i, h))
    k_spec = pl.BlockSpec((t, w), lambda h, i: (0, g + h))
    v_spec = pl.BlockSpec((t, w), lambda h, i: (0, 2 * g + h))
    ct_spec = pl.BlockSpec((SB_HPS, TQ, 1), lambda h, i: (h, i, 0))
    return g, w, q_spec, k_spec, v_spec, ct_spec


def _sb_fwd(qkv, name, comm=None):
    t = qkv.shape[0]
    scale = SB_DH ** -0.5
    g, w, q_spec, k_spec, v_spec, ct_spec = _sb_specs(t)

    def body(q_ref, k_ref, v_ref, y_ref, ct_ref):
        i = pl.program_id(1)
        later = _tri2(lambda j, s: j > s)
        n_diag = TQ // TK

        def block(j, carry, shift):
            off = pl.multiple_of(j * TK, TK)
            zs, ms = [], []
            for hd in range(SB_HPS):
                cols = slice(hd * SB_DH, (hd + 1) * SB_DH)
                z = _dot_nt(q_ref[:, cols], k_ref[pl.ds(off, TK), cols]) * scale
                m = _log_1m_beta(z)
                if shift is not None:
                    m = jnp.where(_causal(shift), m, 0.0)
                zs.append(z)
                ms.append(m)
            after = _cumdot(jnp.concatenate(ms, axis=0), later)
            out = []
            for hd in range(SB_HPS):
                acc, c_sum = carry[hd]
                cols = slice(hd * SB_DH, (hd + 1) * SB_DH)
                a = jnp.exp((ms[hd] + zs[hd]) + (c_sum + after[hd * TQ:(hd + 1) * TQ]))
                if shift is not None:
                    a = jnp.where(_causal(shift), a, 0.0)
                out.append((acc + _dot(a.astype(BF16), v_ref[pl.ds(off, TK), cols]),
                            c_sum + jnp.sum(ms[hd], axis=1, keepdims=True)))
            return tuple(out)

        carry = tuple((jnp.zeros((TQ, SB_DH), F32), jnp.zeros((TQ, 1), F32)) for _ in range(SB_HPS))
        for d in reversed(range(n_diag)):
            carry = block(i * n_diag + d, carry, d * TK)
        carry = lax.fori_loop(0, i * n_diag, lambda jj, c: block(i * n_diag - 1 - jj, c, None), carry)
        for hd in range(SB_HPS):
            y_ref[:, hd * SB_DH:(hd + 1) * SB_DH] = carry[hd][0].astype(BF16)
            ct_ref[hd] = carry[hd][1]

    return _call(
        body, (qkv, qkv, qkv), grid=(g, t // TQ), name=name, comm=comm,
        in_specs=[q_spec, k_spec, v_spec],
        out_specs=[q_spec, ct_spec],
        out_shape=[jax.ShapeDtypeStruct((t, D), BF16), jax.ShapeDtypeStruct((SB_H, t, 1), F32)],
        scratch_shapes=[])


def _sb_bwd(qkv, dy, ctot, after, name, comm=None):
    t = qkv.shape[0]
    scale = SB_DH ** -0.5
    g, w, q_spec, k_spec, v_spec, ct_spec = _sb_specs(t)
    acc_spec = pl.BlockSpec((2, t, w), lambda h, i: (0, 0, h))

    def body(q_ref, k_ref, v_ref, dy_ref, ct_ref, after_ref, dq_ref, dkv_ref):
        i = pl.program_id(1)

        @pl.when(i == 0)
        def _():
            dkv_ref[...] = jnp.zeros_like(dkv_ref)

        upto = _tri2(lambda j, s: j <= s)
        n_diag = TQ // TK

        def block(j, carry, shift):
            off = pl.multiple_of(j * TK, TK)
            r0 = 0 if shift is None else shift
            nr = TQ - r0
            causal = None if shift is None else _causal(shift, r0)

            def grow(old, delta):
                return old + delta if r0 == 0 else jnp.concatenate([old[:r0], old[r0:] + delta], axis=0)

            zs, ms = [], []
            for hd in range(SB_HPS):
                cols = slice(hd * SB_DH, (hd + 1) * SB_DH)
                z = _dot_nt(q_ref[r0:, cols], k_ref[pl.ds(off, TK), cols]) * scale
                m = _log_1m_beta(z)
                if causal is not None:
                    m = jnp.where(causal, m, 0.0)
                zs.append(z)
                ms.append(m)
            m_upto = _cumdot(jnp.concatenate(ms, axis=0), upto)
            ls, a_s, es = [], [], []
            for hd in range(SB_HPS):
                cols = slice(hd * SB_DH, (hd + 1) * SB_DH)
                l = ms[hd] + zs[hd]
                a = jnp.exp(l + ((ct_ref[hd, r0:] - carry[hd][1][r0:]) - m_upto[hd * nr:(hd + 1) * nr]))
                if causal is not None:
                    a = jnp.where(causal, a, 0.0)
                ls.append(l)
                a_s.append(a)
                es.append(_dot_nt(dy_ref[r0:, cols], v_ref[pl.ds(off, TK), cols]) * a)
            e_upto = _dot(jnp.concatenate(es, axis=0).astype(BF16), upto[:TK])
            out = []
            for hd in range(SB_HPS):
                dq, p_sum, e_sum = carry[hd]
                cols = slice(hd * SB_DH, (hd + 1) * SB_DH)
                e = es[hd]
                dz = e - jnp.exp(ls[hd]) * (e_sum[r0:] + e_upto[hd * nr:(hd + 1) * nr])
                if causal is not None:
                    dz = jnp.where(causal, dz, 0.0)
                dzs = (dz * scale).astype(BF16)
                dkv_ref[0, pl.ds(off, TK), cols] += _dot_tn(dzs, q_ref[r0:, cols])
                dkv_ref[1, pl.ds(off, TK), cols] += _dot_tn(a_s[hd].astype(BF16), dy_ref[r0:, cols])
                out.append((grow(dq, _dot(dzs, k_ref[pl.ds(off, TK), cols])),
                            grow(p_sum, jnp.sum(ms[hd], axis=1, keepdims=True)),
                            grow(e_sum, jnp.sum(e, axis=1, keepdims=True))))
            return tuple(out)

        zero = jnp.zeros((TQ, 1), F32)
        init = tuple((jnp.zeros((TQ, SB_DH), F32), zero, zero) for _ in range(SB_HPS))
        carry = lax.fori_loop(0, i * n_diag, lambda j, c: block(j, c, None), init)
        for d in range(n_diag):
            carry = block(i * n_diag + d, carry, d * TK)
        for hd in range(SB_HPS):
            dq_ref[:, hd * SB_DH:(hd + 1) * SB_DH] = carry[hd][0].astype(BF16)

    return _call(
        body, (qkv, qkv, qkv, dy, ctot, after), grid=(g, t // TQ), name=name, comm=comm,
        in_specs=[q_spec, k_spec, v_spec, q_spec, ct_spec, pl.BlockSpec(after.shape, lambda h, i: (0, 0))],
        out_specs=[q_spec, acc_spec],
        out_shape=[jax.ShapeDtypeStruct((t, D), BF16), jax.ShapeDtypeStruct((2, t, D), F32)],
        scratch_shapes=[])


def _gate_specs():
    return [pl.BlockSpec((TM, D), lambda i: (i, 3)), pl.BlockSpec((TM, D), lambda i: (i, 4))]


def _mix_pairs(mix_hbm, dsts):
    pairs = []
    for index, dst in enumerate(dsts):
        pairs += _square_pairs(mix_hbm, index, dst)
    return pairs


def _mix_out_fwd(yc, ysb, pcg, b_gate, h, w_mix, name, comm=None):
    t = h.shape[0]

    def body(yc_ref, ysb_ref, gc_ref, gs_ref, b_ref, h_ref, mix_hbm,
             a_ref, b_out_ref, mg_ref, h2_ref, wc_v, wa_v, wo_v, sems):
        _load_resident(pl.program_id(0), _mix_pairs(mix_hbm, (wc_v, wa_v, wo_v)), sems)
        a = _dot(yc_ref[...], wc_v[...])
        b = _dot(ysb_ref[...], wa_v[...])
        merged = (_sigmoid(gc_ref[...] + b_ref[:, :D]) * a + _sigmoid(gs_ref[...] + b_ref[:, D:]) * b).astype(BF16)
        a_ref[...] = a
        b_out_ref[...] = b
        mg_ref[...] = merged
        h2_ref[...] = h_ref[...] + _dot(merged, wo_v[...])

    return _call(
        body, (yc, ysb, pcg, pcg, b_gate, h, w_mix), grid=(t // TM,), name=name, comm=comm,
        in_specs=[_row_spec(TM, D), _row_spec(TM, D)] + _gate_specs()
                 + [_const_spec((1, 2 * D)), _row_spec(TM, D), _ANY],
        out_specs=[_row_spec(TM, D)] * 4,
        out_shape=[jax.ShapeDtypeStruct((t, D), F32), jax.ShapeDtypeStruct((t, D), F32),
                   jax.ShapeDtypeStruct((t, D), BF16), jax.ShapeDtypeStruct((t, D), F32)],
        scratch_shapes=[pltpu.VMEM((D, D), BF16)] * 3 + [pltpu.SemaphoreType.DMA((3 * N_DEV,))])


def _mix_out_bwd(dh2, a, b, pcg, b_gate, w_mix, name, comm=None):
    t = dh2.shape[0]

    def body(dh_ref, a_ref, b_ref, gc_ref, gs_ref, bias_ref, mix_hbm,
             dhb_ref, da_ref, db_ref, dgp_ref, dyc_ref, dysb_ref, dbias_ref, wc_v, wa_v, wo_v, sems):
        step = pl.program_id(0)
        _load_resident(step, _mix_pairs(mix_hbm, (wc_v, wa_v, wo_v)), sems)
        dhb = dh_ref[...].astype(BF16)
        dhb_ref[...] = dhb
        dm = _dot_nt(dhb, wo_v[...])
        gc = _sigmoid(gc_ref[...] + bias_ref[:, :D])
        gs = _sigmoid(gs_ref[...] + bias_ref[:, D:])
        da = (dm * gc).astype(BF16)
        db = (dm * gs).astype(BF16)
        da_ref[...] = da
        db_ref[...] = db
        dgc = dm * a_ref[...] * (gc * (1.0 - gc))
        dgs = dm * b_ref[...] * (gs * (1.0 - gs))
        dgp_ref[0] = dgc.astype(BF16)
        dgp_ref[1] = dgs.astype(BF16)
        _accumulate(dbias_ref.at[:, :D], step, jnp.sum(dgc, axis=0, keepdims=True))
        _accumulate(dbias_ref.at[:, D:], step, jnp.sum(dgs, axis=0, keepdims=True))
        dyc_ref[...] = _dot_nt(da, wc_v[...])
        dysb_ref[...] = _dot_nt(db, wa_v[...]).astype(BF16)

    return _call(
        body, (dh2, a, b, pcg, pcg, b_gate, w_mix), grid=(t // TM,), name=name, comm=comm,
        in_specs=[_row_spec(TM, D)] * 3 + _gate_specs() + [_const_spec((1, 2 * D)), _ANY],
        out_specs=[_row_spec(TM, D)] * 3 + [_blk_row_spec(2, TM, D), _row_spec(TM, D), _row_spec(TM, D),
                                            _const_spec((1, 2 * D))],
        out_shape=[jax.ShapeDtypeStruct((t, D), BF16)] * 3
                  + [jax.ShapeDtypeStruct((2, t, D), BF16), jax.ShapeDtypeStruct((t, D), F32),
                     jax.ShapeDtypeStruct((t, D), BF16), jax.ShapeDtypeStruct((1, 2 * D), F32)],
        scratch_shapes=[pltpu.VMEM((D, D), BF16)] * 3 + [pltpu.SemaphoreType.DMA((3 * N_DEV,))])


def _inproj_bwd(dconv, dq, dkv, dgp, w_in, h, g, dh_res, name, comm=None):
    t = h.shape[0]

    def body(dc_ref, dq_ref, dkv_ref, dgp_ref, w_hbm, h_ref, g_ref, dres_ref, dh_ref, dg_ref, w_v, sems):
        step = pl.program_id(0)
        _load_resident(step, [(w_hbm, w_v)], sems)
        du = _dot_nt(dq_ref[...], w_v[3])
        for k in range(3):
            du = du + _dot_nt(dc_ref[k], w_v[k])
        for k in range(2):
            du = du + _dot_nt(dkv_ref[k].astype(BF16), w_v[4 + k]) + _dot_nt(dgp_ref[k], w_v[6 + k])
        dx, dg = _rms_bwd_tile(h_ref[...], g_ref[...], du)
        dh_ref[...] = dres_ref[...] + dx
        _accumulate(dg_ref, step, dg)

    return _call(
        body, (dconv, dq, dkv, dgp, w_in, h, g, dh_res), grid=(t // TM,), name=name, comm=comm,
        in_specs=[_blk_row_spec(3, TM, D), _row_spec(TM, D), _blk_row_spec(2, TM, D), _blk_row_spec(2, TM, D), _ANY,
                  _row_spec(TM, D), _const_spec((1, D)), _row_spec(TM, D)],
        out_specs=[_row_spec(TM, D), _const_spec((1, D))],
        out_shape=[jax.ShapeDtypeStruct((t, D), F32), jax.ShapeDtypeStruct((1, D), F32)],
        scratch_shapes=[pltpu.VMEM((N_DEV, D, D), BF16), pltpu.SemaphoreType.DMA((1,))])


def _memkv_fwd(mem, g, w_ckv, name):
    m = mem.shape[0]

    def body(mem_ref, g_ref, w_ref, mn_ref, kv_ref):
        mn = _rms_fwd_tile(mem_ref[...], g_ref[...]).astype(BF16)
        mn_ref[...] = mn
        for j in range(N_DEV):
            kv_ref[j] = _dot(mn, w_ref[j]).astype(BF16)

    return pl.pallas_call(
        body, grid=(1,), name=name,
        in_specs=[_const_spec((m, D)), _const_spec((1, D)), _const_spec((N_DEV, D, X_DH))],
        out_specs=[_const_spec((m, D)), _const_spec((N_DEV, m, X_DH))],
        out_shape=[jax.ShapeDtypeStruct((m, D), BF16), jax.ShapeDtypeStruct((N_DEV, m, X_DH), BF16)],
        compiler_params=_cparams(),
    )(mem, g, w_ckv)


def _memkv_bwd(dkv, mem, g, w_ckv, name):
    m = mem.shape[0]

    def body(dkv_ref, mem_ref, g_ref, w_ref, dg_ref):
        dmn = jnp.zeros((m, D), F32)
        for j in range(N_DEV):
            dmn = dmn + _dot_nt(dkv_ref[j].astype(BF16), w_ref[j])
        _, dg = _rms_bwd_tile(mem_ref[...], g_ref[...], dmn)
        dg_ref[...] = dg

    return pl.pallas_call(
        body, grid=(1,), name=name,
        in_specs=[_const_spec((N_DEV, m, X_DH)), _const_spec((m, D)), _const_spec((1, D)),
                  _const_spec((N_DEV, D, X_DH))],
        out_specs=_const_spec((1, D)),
        out_shape=jax.ShapeDtypeStruct((1, D), F32),
        compiler_params=_cparams(),
    )(dkv, mem, g, w_ckv)


def _softmax_rows(s):
    e = jnp.exp(s - jnp.max(s, axis=-1, keepdims=True))
    return e / jnp.sum(e, axis=-1, keepdims=True)


def _cross_pairs(cross_hbm, wq_v, wo_v):
    return _square_pairs(cross_hbm, 0, wq_v) + _square_pairs(cross_hbm, 1, wo_v)


def _cross_fwd(h, g, kv, w_cross, name):
    t = h.shape[0]
    m = kv.shape[1]
    scale = X_DH ** -0.5

    def body(h_ref, g_ref, kv_ref, cross_hbm, hn_ref, qx_ref, o_ref, h3_ref, wq_v, wo_v, sems):
        _load_resident(pl.program_id(0), _cross_pairs(cross_hbm, wq_v, wo_v), sems)
        ht = h_ref[...]
        hn = _rms_fwd_tile(ht, g_ref[...]).astype(BF16)
        hn_ref[...] = hn
        qx = _dot(hn, wq_v[...]).astype(BF16)
        qx_ref[...] = qx
        for hd in range(X_H):
            lo, hi = hd * X_DH, (hd + 1) * X_DH
            p = _softmax_rows(_dot_nt(qx[:, lo:hi], kv_ref[hd]) * scale)
            o_ref[:, lo:hi] = _dot(p.astype(BF16), kv_ref[X_H + hd]).astype(BF16)
        h3_ref[...] = ht + _dot(o_ref[...], wo_v[...])

    return pl.pallas_call(
        body, grid=(t // TM,), name=name,
        in_specs=[_row_spec(TM, D), _const_spec((1, D)), _const_spec((N_DEV, m, X_DH)), _ANY],
        out_specs=[_row_spec(TM, D)] * 4,
        out_shape=[jax.ShapeDtypeStruct((t, D), BF16)] * 3 + [jax.ShapeDtypeStruct((t, D), F32)],
        scratch_shapes=[pltpu.VMEM((D, D), BF16)] * 2 + [pltpu.SemaphoreType.DMA((2 * N_DEV,))],
        compiler_params=_cparams(),
    )(h, g, kv, w_cross)


def _cross_bwd(dh3, h, g, qx, kv, w_cross, name, comm=None):
    t = h.shape[0]
    m = kv.shape[1]
    scale = X_DH ** -0.5

    def body(dh_ref, h_ref, g_ref, qx_ref, kv_ref, cross_hbm,
             dhb_ref, dqx_ref, dkv_ref, dh2_ref, dg_ref, wq_v, wo_v, sems):
        step = pl.program_id(0)
        _load_resident(step, _cross_pairs(cross_hbm, wq_v, wo_v), sems)

        @pl.when(step == 0)
        def _():
            dkv_ref[...] = jnp.zeros_like(dkv_ref)

        dht = dh_ref[...]
        dhb = dht.astype(BF16)
        dhb_ref[...] = dhb
        do = _dot_nt(dhb, wo_v[...]).astype(BF16)
        for hd in range(X_H):
            lo, hi = hd * X_DH, (hd + 1) * X_DH
            qh = qx_ref[:, lo:hi]
            kh = kv_ref[hd]
            p = _softmax_rows(_dot_nt(qh, kh) * scale)
            doh = do[:, lo:hi]
            dp = _dot_nt(doh, kv_ref[X_H + hd])
            ds = (p * (dp - jnp.sum(dp * p, axis=-1, keepdims=True)) * scale).astype(BF16)
            dqx_ref[:, lo:hi] = _dot(ds, kh).astype(BF16)
            dkv_ref[hd] += _dot_tn(ds, qh)
            dkv_ref[X_H + hd] += _dot_tn(p.astype(BF16), doh)
        dhn = _dot_nt(dqx_ref[...], wq_v[...])
        dx, dg = _rms_bwd_tile(h_ref[...], g_ref[...], dhn)
        dh2_ref[...] = dht + dx
        _accumulate(dg_ref, step, dg)

    return _call(
        body, (dh3, h, g, qx, kv, w_cross), grid=(t // TM,), name=name, comm=comm,
        in_specs=[_row_spec(TM, D), _row_spec(TM, D), _const_spec((1, D)), _row_spec(TM, D),
                  _const_spec((N_DEV, m, X_DH)), _ANY],
        out_specs=[_row_spec(TM, D), _row_spec(TM, D), _const_spec((N_DEV, m, X_DH)), _row_spec(TM, D),
                   _const_spec((1, D))],
        out_shape=[jax.ShapeDtypeStruct((t, D), BF16), jax.ShapeDtypeStruct((t, D), BF16),
                   jax.ShapeDtypeStruct((N_DEV, m, X_DH), F32), jax.ShapeDtypeStruct((t, D), F32),
                   jax.ShapeDtypeStruct((1, D), F32)],
        scratch_shapes=[pltpu.VMEM((D, D), BF16)] * 2 + [pltpu.SemaphoreType.DMA((2 * N_DEV,))])


def _adamw(w, parts, m, v, name, row_block=0, token=None):
    r, c = w.shape
    n = parts.shape[0]
    tr = _pick_tile(r, (256, 352, 128))
    off = row_block * (r // tr)

    def body(*refs):
        if token is None:
            _adamw_update(None, *refs)
        else:
            _adamw_update(refs[4], *refs[:4], *refs[5:])

    spec = _row_spec(tr, c)
    in_specs = [spec, pl.BlockSpec((n, tr, c), lambda i: (0, i + off, 0)), spec, spec]
    operands = (w, parts, m, v)
    if token is not None:
        in_specs.append(_const_spec(token.shape))
        operands += (token,)
    return pl.pallas_call(
        body, grid=(r // tr,), name=name, in_specs=in_specs, out_specs=[spec] * 4,
        out_shape=[jax.ShapeDtypeStruct((r, c), F32)] * 4,
        compiler_params=_cparams(),
    )(*operands)


def _adamw_update(tok_ref, w_ref, p_ref, m_ref, v_ref, g_ref, d_ref, nm_ref, nv_ref):
    gt = p_ref[0].astype(F32)
    for k in range(1, p_ref.shape[0]):
        gt = gt + p_ref[k].astype(F32)
    if tok_ref is not None:
        gt = gt + tok_ref[0:1, 0:1]
    _adamw_apply(gt, w_ref, m_ref, v_ref, g_ref, d_ref, nm_ref, nv_ref)


def _adamw_own(w, land, own, chip, m, v, name, row_block=0, token=None):
    r, c = w.shape
    tr = _pick_tile(r, (256, 352, 128))
    off = row_block * (r // tr)

    def body(chip_ref, w_ref, land_ref, own_ref, m_ref, v_ref, *rest):
        mine = own_ref[0].astype(F32)
        gt = jnp.where(chip_ref[0] == 0, mine, land_ref[0].astype(F32))
        for k in range(1, N_CHIP):
            gt = gt + jnp.where(chip_ref[0] == k, mine, land_ref[k].astype(F32))
        if token is not None:
            gt = gt + rest[0][0:1, 0:1]
        _adamw_apply(gt, w_ref, m_ref, v_ref, *rest[-4:])

    spec = pl.BlockSpec((tr, c), lambda i, chip_ref: (i, 0))
    in_specs = [spec, pl.BlockSpec((N_CHIP, tr, c), lambda i, chip_ref: (0, i + off, 0)),
                pl.BlockSpec((1, tr, c), lambda i, chip_ref: (chip_ref[0], i + off, 0)), spec, spec]
    operands = (chip, w, land, own, m, v)
    if token is not None:
        in_specs.append(pl.BlockSpec(token.shape, lambda i, chip_ref: (0, 0)))
        operands += (token,)
    return pl.pallas_call(
        body, name=name,
        grid_spec=pltpu.PrefetchScalarGridSpec(
            num_scalar_prefetch=1, grid=(r // tr,), in_specs=in_specs, out_specs=[spec] * 4),
        out_shape=[jax.ShapeDtypeStruct((r, c), F32)] * 4,
        compiler_params=_cparams(),
    )(*operands)


def _adamw_apply(gt, w_ref, m_ref, v_ref, g_ref, d_ref, nm_ref, nv_ref):
    g_ref[...] = gt
    nm = ADAM_B1 * m_ref[...] + (1.0 - ADAM_B1) * gt
    nv = ADAM_B2 * v_ref[...] + (1.0 - ADAM_B2) * jnp.square(gt)
    m_hat = nm / (1.0 - ADAM_B1 ** ADAM_STEP)
    v_hat = nv / (1.0 - ADAM_B2 ** ADAM_STEP)
    d_ref[...] = -ADAM_LR * (m_hat / (jnp.sqrt(v_hat) + ADAM_EPS) + ADAM_WD * w_ref[...])
    nm_ref[...] = nm
    nv_ref[...] = nv


def _mesh_pos():
    return lax.axis_index("x"), lax.axis_index("y"), lax.axis_index("c")


def _no_round(in_refs, out_refs, sems):
    pass


def _run_exchange(comm, name):
    c_in, c_out = len(comm.inputs), len(comm.out_shapes)

    def body(*refs):
        cins, couts, sems = refs[:c_in], refs[c_in:c_in + c_out], refs[c_in + c_out:]
        comm.start(cins, couts, sems)
        comm.middle(cins, couts, sems)
        comm.finish(cins, couts, sems)

    return list(pl.pallas_call(
        body, name=name, out_shape=list(comm.out_shapes),
        in_specs=[_ANY] * c_in, out_specs=[_ANY] * c_out, scratch_shapes=list(comm.sem_shapes),
    )(*comm.inputs))


def _gather_exchange(shards):
    n_arr = len(shards)

    def plan(x_refs, out_refs, sems):
        send_sems, recv_sems, local_sems = sems
        x, y, c = _mesh_pos()
        me, sibling = (x, y, c), (x, y, 1 - c)
        xn, yn, diag = (1 - x, y), (x, 1 - y), (1 - x, 1 - y)

        def slot(a, px, py, pc, half=None):
            ref = out_refs[a].at[4 * px + 2 * py + pc]
            if half is None:
                return ref
            rows = shards[a].shape[0] // 2
            return ref.at[half * rows:(half + 1) * rows]

        def copy(a, k, block, to, half=None, src=None):
            dst = slot(a, *block, half)
            return pltpu.make_async_remote_copy(
                src_ref=dst if src is None else src, dst_ref=dst,
                send_sem=send_sems.at[a, k], recv_sem=recv_sems.at[a, k],
                device_id=to, device_id_type=pl.DeviceIdType.MESH)

        return types.SimpleNamespace(
            me=me, sibling=sibling, xn=xn, yn=yn, diag=diag, c=c, copy=copy,
            mine=[pltpu.make_async_copy(x_refs[a], slot(a, *me), local_sems.at[a]) for a in range(n_arr)],
            first=[cp for a in range(n_arr) for cp in (
                copy(a, 0, me, sibling, src=x_refs[a]), copy(a, 1, me, (*xn, c), src=x_refs[a]),
                copy(a, 2, me, (*yn, c), src=x_refs[a]))],
            second=lambda a: (copy(a, 3, (*xn, c), (*yn, c), half=0), copy(a, 5, (*xn, c), sibling),
                              copy(a, 4, (*yn, c), (*xn, c), half=1), copy(a, 6, (*yn, c), sibling)),
            third=lambda a: (copy(a, 7, (*diag, c), sibling, half=0), copy(a, 8, (*diag, c), sibling, half=1)))

    def start(x_refs, out_refs, sems):
        p = plan(x_refs, out_refs, sems)
        for cp in p.mine + p.first:
            cp.start()

    def middle(x_refs, out_refs, sems):
        p = plan(x_refs, out_refs, sems)
        for a in range(n_arr):
            to_yn, x_to_sib, to_xn, y_to_sib = p.second(a)
            p.copy(a, 1, (*p.xn, p.c), p.me).wait_recv()
            to_yn.start()
            x_to_sib.start()
            p.copy(a, 2, (*p.yn, p.c), p.me).wait_recv()
            to_xn.start()
            y_to_sib.start()

    def finish(x_refs, out_refs, sems):
        p = plan(x_refs, out_refs, sems)
        for a in range(n_arr):
            half0_to_sib, half1_to_sib = p.third(a)
            p.copy(a, 3, (*p.diag, p.c), p.me, half=0).wait_recv()
            half0_to_sib.start()
            p.copy(a, 4, (*p.diag, p.c), p.me, half=1).wait_recv()
            half1_to_sib.start()
        other = 1 - p.c
        for a in range(n_arr):
            p.copy(a, 0, p.sibling, p.me).wait_recv()
            p.copy(a, 5, (*p.xn, other), p.me).wait_recv()
            p.copy(a, 6, (*p.yn, other), p.me).wait_recv()
            p.copy(a, 7, (*p.diag, other), p.me, half=0).wait_recv()
            p.copy(a, 8, (*p.diag, other), p.me, half=1).wait_recv()
        for cp in p.first:
            cp.wait_send()
        for a in range(n_arr):
            for cp in p.second(a) + p.third(a):
                cp.wait_send()
        for cp in p.mine:
            cp.wait()

    return types.SimpleNamespace(
        inputs=list(shards), start=start, middle=middle, finish=finish,
        out_shapes=[jax.ShapeDtypeStruct((N_DEV,) + s.shape, s.dtype) for s in shards],
        sem_shapes=[pltpu.SemaphoreType.DMA((n_arr, 9)), pltpu.SemaphoreType.DMA((n_arr, 9)),
                    pltpu.SemaphoreType.DMA((n_arr,))])


def _pair_exchange(grads):
    n_arr = len(grads)

    def plan(g_refs, land_refs, sems):
        send_sems, recv_sems = sems
        x, y, c = _mesh_pos()
        return [pltpu.make_async_remote_copy(
            src_ref=g_refs[a].at[2 * k + 1 - c], dst_ref=land_refs[a].at[k],
            send_sem=send_sems.at[a, k], recv_sem=recv_sems.at[a, k],
            device_id=(x, y, 1 - c), device_id_type=pl.DeviceIdType.MESH)
            for a in range(n_arr) for k in range(N_CHIP)]

    def start(g_refs, land_refs, sems):
        for cp in plan(g_refs, land_refs, sems):
            cp.start()

    def finish(g_refs, land_refs, sems):
        for cp in plan(g_refs, land_refs, sems):
            cp.wait()

    return types.SimpleNamespace(
        inputs=list(grads), start=start, middle=_no_round, finish=finish,
        out_shapes=[jax.ShapeDtypeStruct((N_CHIP,) + g.shape[1:], g.dtype) for g in grads],
        sem_shapes=[pltpu.SemaphoreType.DMA((n_arr, N_CHIP)), pltpu.SemaphoreType.DMA((n_arr, N_CHIP))])


def _chip_exchange(parts):
    n_arr = len(parts)

    def plan(p_refs, land_refs, sems):
        send_sems, recv_sems, local_sems = sems
        x, y, c = _mesh_pos()
        my_chip = 2 * x + y
        chips = [(1 - x, y), (x, 1 - y), (1 - x, 1 - y)]
        local = [pltpu.make_async_copy(p_refs[a].at[my_chip], land_refs[a].at[my_chip], local_sems.at[a])
                 for a in range(n_arr)]

        def copy(a, k, src_slot, dst_slot, px, py):
            return pltpu.make_async_remote_copy(
                src_ref=p_refs[a].at[src_slot], dst_ref=land_refs[a].at[dst_slot],
                send_sem=send_sems.at[a, k], recv_sem=recv_sems.at[a, k],
                device_id=(px, py, c), device_id_type=pl.DeviceIdType.MESH)

        sends = [copy(a, k, 2 * px + py, my_chip, px, py) for a in range(n_arr) for k, (px, py) in enumerate(chips)]
        arrivals = [copy(a, k, my_chip, 2 * px + py, px, py) for a in range(n_arr)
                    for k, (px, py) in enumerate(chips)]
        return local, sends, arrivals

    def start(p_refs, land_refs, sems):
        local, sends, _ = plan(p_refs, land_refs, sems)
        for cp in local + sends:
            cp.start()

    def finish(p_refs, land_refs, sems):
        local, sends, arrivals = plan(p_refs, land_refs, sems)
        for cp in arrivals:
            cp.wait_recv()
        for cp in sends:
            cp.wait_send()
        for cp in local:
            cp.wait()

    return types.SimpleNamespace(
        inputs=list(parts), start=start, middle=_no_round, finish=finish,
        out_shapes=[jax.ShapeDtypeStruct(p.shape, p.dtype) for p in parts],
        sem_shapes=[pltpu.SemaphoreType.DMA((n_arr, 3)), pltpu.SemaphoreType.DMA((n_arr, 3)),
                    pltpu.SemaphoreType.DMA((n_arr,))])


_HBM = pl.BlockSpec(memory_space=pltpu.HBM)
_SEM = pl.BlockSpec(memory_space=pltpu.SEMAPHORE)
_DATAFLOW = pltpu.SideEffectType.DATAFLOW_SIDE_EFFECTING


def _chip_copies(p_refs, land_refs, send_sems, recv_sems):
    x, y, c = _mesh_pos()
    my_chip = 2 * x + y
    chips = [(1 - x, y), (x, 1 - y), (1 - x, 1 - y)]
    return [pltpu.make_async_remote_copy(
        src_ref=p_refs[a].at[2 * px + py], dst_ref=land_refs[a].at[my_chip],
        send_sem=send_sems[3 * a + k], recv_sem=recv_sems[3 * a + k],
        device_id=(px, py, c), device_id_type=pl.DeviceIdType.MESH)
        for a in range(len(p_refs)) for k, (px, py) in enumerate(chips)]


def _chip_exchange_begin(parts, name):
    n_arr = len(parts)
    n_buf, n_copy = 2 * n_arr, 3 * n_arr
    lands = [lax.empty(p.shape, p.dtype) for p in parts]

    def body(*refs):
        p_refs, land_refs = refs[:n_arr], refs[n_arr:n_buf]
        send_sems, recv_sems, token = refs[n_buf:n_buf + n_copy], refs[n_buf + n_copy:n_buf + 2 * n_copy], refs[-1]
        for cp in _chip_copies(p_refs, land_refs, send_sems, recv_sems):
            cp.start()
        token[...] = jnp.zeros_like(token)

    bufs = list(parts) + list(lands)
    outs = pl.pallas_call(
        body, name=name,
        out_shape=(*[pltpu.SemaphoreType.DMA(())] * (2 * n_copy), *[pltpu.HBM(b.shape, b.dtype) for b in bufs],
                   jax.ShapeDtypeStruct((8, 128), F32)),
        in_specs=[_HBM] * n_buf,
        out_specs=(*[_SEM] * (2 * n_copy), *[_HBM] * n_buf, pl.BlockSpec(memory_space=pltpu.VMEM)),
        input_output_aliases={i: 2 * n_copy + i for i in range(n_buf)},
        compiler_params=pltpu.CompilerParams(has_side_effects=_DATAFLOW),
    )(*[pltpu.with_memory_space_constraint(b, pltpu.HBM) for b in bufs])
    sems = list(outs[:2 * n_copy])
    thru = list(outs[2 * n_copy:2 * n_copy + n_buf])
    return types.SimpleNamespace(send_sems=sems[:n_copy], recv_sems=sems[n_copy:], parts=thru[:n_arr],
                                 lands=thru[n_arr:], token=outs[-1])


def _chip_exchange_end(flight, after, name):
    send_sems, recv_sems, parts, lands = flight.send_sems, flight.recv_sems, flight.parts, flight.lands
    n_arr = len(parts)
    n_buf, n_copy = 2 * n_arr, 3 * n_arr

    def body(*refs):
        p_refs, land_refs = refs[:n_arr], refs[n_arr:n_buf]
        sems = refs[n_buf:n_buf + 2 * n_copy]
        for cp in _chip_copies(p_refs, land_refs, sems[:n_copy], sems[n_copy:]):
            cp.wait_send()
            cp.wait_recv()

    bufs = list(parts) + list(lands)
    outs = pl.pallas_call(
        body, name=name, out_shape=tuple(pltpu.HBM(b.shape, b.dtype) for b in bufs),
        in_specs=[_HBM] * n_buf + [_SEM] * (2 * n_copy) + [_ANY], out_specs=tuple([_HBM] * n_buf),
        input_output_aliases={i: i for i in range(n_buf)},
        compiler_params=pltpu.CompilerParams(has_side_effects=_DATAFLOW),
    )(*bufs, *send_sems, *recv_sems, after)
    return list(outs[:n_arr]), list(outs[n_arr:])


def _row_tile(r, cap=640):
    best = None
    for cand in range(16, min(r, cap) + 1, 16):
        if r % cand == 0:
            best = cand
    return best if best is not None else r


def _pair_sum(g, landed, core, name):
    _, r, c_dim = g.shape
    tr = _row_tile(r)

    def body(core_ref, mine_ref, theirs_ref, o_ref):
        o_ref[0] = (mine_ref[0].astype(F32) + theirs_ref[0].astype(F32)).astype(o_ref.dtype)

    return pl.pallas_call(
        body, name=name,
        grid_spec=pltpu.PrefetchScalarGridSpec(
            num_scalar_prefetch=1, grid=(N_CHIP, r // tr),
            in_specs=[pl.BlockSpec((1, tr, c_dim), lambda k, i, core_ref: (2 * k + core_ref[0], i, 0)),
                      pl.BlockSpec((1, tr, c_dim), lambda k, i, core_ref: (k, i, 0))],
            out_specs=pl.BlockSpec((1, tr, c_dim), lambda k, i, core_ref: (k, i, 0))),
        out_shape=jax.ShapeDtypeStruct((N_CHIP, r, c_dim), g.dtype),
        compiler_params=_cparams(2),
    )(core, g, landed)


def _sum_slots(parts, name):
    n, r, c_dim = parts.shape
    tr = _row_tile(r)

    def body(p_ref, o_ref):
        acc = p_ref[0].astype(F32)
        for k in range(1, n):
            acc = acc + p_ref[k].astype(F32)
        o_ref[...] = acc

    return pl.pallas_call(
        body, grid=(r // tr,), name=name,
        in_specs=[pl.BlockSpec((n, tr, c_dim), lambda i: (0, i, 0))],
        out_specs=_row_spec(tr, c_dim),
        out_shape=jax.ShapeDtypeStruct((r, c_dim), F32),
        compiler_params=_cparams(),
    )(parts)


GAINS = ("g_ffn1", "g_mix", "g_cross", "g_mem", "g_ffn2", "g_final")
SMALL = GAINS + ("b_gate", "conv_w")
SMALL_R = 16
LOSS_ROW = 11
WEIGHT_ORDER = ("g_ffn1", "w_ffn1_gu", "w_ffn1_down", "g_mix", "w_in", "b_gate", "conv_w", "w_conv_out",
                "w_attn_out", "w_o", "g_cross", "g_mem", "w_cq", "w_ckv", "w_co", "g_ffn2", "w_ffn2_gu",
                "w_ffn2_down", "g_final")
GU_NAMES = ("w_ffn1_gu", "w_ffn2_gu")


def _pack_small(vals, conv_rows):
    rows = [vals[n].reshape(1, D) for n in GAINS] + [vals["b_gate"].reshape(2, D), conv_rows.reshape(CONV_K, D)]
    used = len(GAINS) + 2 + CONV_K
    return jnp.concatenate(rows + [jnp.zeros((SMALL_R - used, D), F32)], axis=0)


def _unpack_small(buf):
    out = {n: buf[k] for k, n in enumerate(GAINS)}
    out["b_gate"] = buf[6:8].reshape(2 * D)
    out["conv_w"] = buf[8:8 + CONV_K]
    return out


def _exchange_shards(wts):
    out = {n: jnp.pad(wts[n].T.astype(BF16), ((0, FF_PAD - FF_BLK), (0, 0))) for n in GU_NAMES}
    for n in ("w_ckv", "w_in", "w_ffn1_down", "w_ffn2_down"):
        out[n] = wts[n].astype(BF16)
    out["mix"] = jnp.concatenate([wts[n].astype(BF16) for n in MIX_MATS], axis=0)
    out["cross"] = jnp.concatenate([wts[n].astype(BF16) for n in CROSS_MATS], axis=0)
    return out


def _reduce_group(grads, landed, core, names):
    return [_pair_sum(g, l, core, "grads_pair_sum_" + n) for g, l, n in zip(grads, landed, names)]


def _step(x, mem, target, sh, conv_pad, gains, b_gate, core):
    wg1, wd1, conv_all = _run_exchange(_gather_exchange([sh["w_ffn1_gu"], sh["w_ffn1_down"], conv_pad]), "gather_ffn1")
    conv_w = conv_all[:, :CONV_K, :].transpose(1, 0, 2).reshape(CONV_K, D)
    (n1, gate1, up1, act1, h1), (w_in,) = _ffn_fwd(
        x, gains["g_ffn1"], wg1, wd1, "ffn1_fwd", comm=_gather_exchange([sh["w_in"]]))
    (u, pcg, qkv), (w_mix,) = _inproj_fwd(h1, gains["g_mix"], w_in, "inproj_fwd", comm=_gather_exchange([sh["mix"]]))
    yc = _conv_fwd(pcg, conv_w, "conv_fwd")
    (ysb, ctot), (w_cross, w_ckv, wg2) = _sb_fwd(
        qkv, "sb_fwd", comm=_gather_exchange([sh["cross"], sh["w_ckv"], sh["w_ffn2_gu"]]))
    (a_mix, b_mix, merged, h2), (wd2,) = _mix_out_fwd(yc, ysb, pcg, b_gate, h1, w_mix, "mix_out_fwd",
                                                      comm=_gather_exchange([sh["w_ffn2_down"]]))
    mn, kv = _memkv_fwd(mem, gains["g_mem"], w_ckv, "memkv_fwd")
    hn, qx, o_x, h3 = _cross_fwd(h2, gains["g_cross"], kv, w_cross, "cross_fwd")
    (n4, gate2, up2, act2, dh4, loss, dg_final), _ = _ffn_fwd(h3, gains["g_ffn2"], wg2, wd2, "ffn2_fwd",
                                                              head=(gains["g_final"], target))

    gs = {"g_final": dg_final}
    (dgu2, dh4b, dh3, gs["g_ffn2"]), _ = _ffn_bwd(dh4, h3, gains["g_ffn2"], gate2, up2, wg2, wd2, "ffn2_bwd")
    grads_a = [_mm_tn_rows(dgu2, n4, FF_PAD, "dw_ffn2_gu"),
               _mm_tn_rows(act2, dh4b, FF_BLK, "dw_ffn2_down").reshape(N_DEV, DOWN_ROWS, D)]
    names_a = ["w_ffn2_gu", "w_ffn2_down"]
    (dh3b, dqx, dkv, dh2, gs["g_cross"]), landed_a = _cross_bwd(
        dh3, h2, gains["g_cross"], qx, kv, w_cross, "cross_bwd", comm=_pair_exchange(grads_a))
    sums_a = _reduce_group(grads_a, landed_a, core, names_a)
    cross_stack = _mm_tn_square(hn, dqx, "dw_cq", 0, len(CROSS_MATS))
    grads_b = [_mm_tn_cols(mn, dkv, "dw_ckv"), _mm_tn_square(o_x, dh3b, "dw_co", 1, len(CROSS_MATS), cross_stack)]
    names_b = ["w_ckv", "cross"]
    gs["g_mem"] = _memkv_bwd(dkv, mem, gains["g_mem"], w_ckv, "memkv_bwd")
    (dh2b, da_mix, db_mix, dgp, dyc, dysb, gs["b_gate"]), landed_b = _mix_out_bwd(
        dh2, a_mix, b_mix, pcg, b_gate, w_mix, "mix_out_bwd", comm=_pair_exchange(grads_b))
    sums_b = _reduce_group(grads_b, landed_b, core, names_b)
    mix_stack = _mm_tn_square(yc, da_mix, "dw_conv_out", 0, len(MIX_MATS))
    mix_stack = _mm_tn_square(ysb, db_mix, "dw_attn_out", 1, len(MIX_MATS), mix_stack)
    grads_c = [_mm_tn_square(merged, dh2b, "dw_o", 2, len(MIX_MATS), mix_stack)]
    landed_c = _run_exchange(_pair_exchange(grads_c), "grads_to_sibling_mix")
    sums_c = _reduce_group(grads_c, landed_c, core, ["mix"])
    flight_abc = _chip_exchange_begin(sums_a + sums_b + sums_c, "grads_to_chips_early_begin")
    (dq, dkv_sb), _ = _sb_bwd(qkv, dysb, ctot, flight_abc.token, "sb_bwd")
    dconv, gs["conv_w"] = _conv_bwd(pcg, conv_w, dyc, "conv_bwd")
    w_in_stack = _mm_tn_cols(u, dconv, "dw_in_conv", 0, N_DEV)
    w_in_stack = _mm_tn_cols(u, dq[None], "dw_in_q", 3, N_DEV, w_in_stack)
    w_in_stack = _mm_tn_cols(u, dkv_sb, "dw_in_kv", 4, N_DEV, w_in_stack)
    grads_d = [_mm_tn_cols(u, dgp, "dw_in_gates", 6, N_DEV, w_in_stack)]
    (dh1, gs["g_mix"]), landed_d = _inproj_bwd(dconv, dq, dkv_sb, dgp, w_in, h1, gains["g_mix"], dh2, "inproj_bwd",
                                               comm=_pair_exchange(grads_d))
    sums_d = _reduce_group(grads_d, landed_d, core, ["w_in"])
    flight_d = _chip_exchange_begin(sums_d, "grads_to_chips_w_in_begin")
    (dgu1, dh1b, dx, gs["g_ffn1"]), _ = _ffn_bwd(dh1, x, gains["g_ffn1"] + flight_d.token[0, 0], gate1, up1, wg1, wd1,
                                                 "ffn1_bwd")
    grads_e = [_mm_tn_rows(dgu1, n1, FF_PAD, "dw_ffn1_gu"),
               _mm_tn_rows(act1, dh1b, FF_BLK, "dw_ffn1_down").reshape(N_DEV, DOWN_ROWS, D)]
    names_e = ["w_ffn1_gu", "w_ffn1_down"]
    landed_e = _run_exchange(_pair_exchange(grads_e), "grads_to_sibling_ffn1")
    flight_e = _chip_exchange_begin(_reduce_group(grads_e, landed_e, core, names_e), "grads_to_chips_ffn1_begin")
    flights = [(names_a + names_b + ["mix"], flight_abc), (["w_in"], flight_d), (names_e, flight_e)]
    return loss, dx, flights, gs


def kernel(x, mem, g_ffn1, w_ffn1_gu, w_ffn1_down, g_mix, w_in, b_gate, conv_w, w_conv_out, w_attn_out, w_o, g_cross, g_mem, w_cq, w_ckv, w_co, g_ffn2, w_ffn2_gu, w_ffn2_down, g_final, loss_target, m_g_ffn1, m_w_ffn1_gu, m_w_ffn1_down, m_g_mix, m_w_in, m_b_gate, m_conv_w, m_w_conv_out, m_w_attn_out, m_w_o, m_g_cross, m_g_mem, m_w_cq, m_w_ckv, m_w_co, m_g_ffn2, m_w_ffn2_gu, m_w_ffn2_down, m_g_final, v_g_ffn1, v_w_ffn1_gu, v_w_ffn1_down, v_g_mix, v_w_in, v_b_gate, v_conv_w, v_w_conv_out, v_w_attn_out, v_w_o, v_g_cross, v_g_mem, v_w_cq, v_w_ckv, v_w_co, v_g_ffn2, v_w_ffn2_gu, v_w_ffn2_down, v_g_final):
    args = locals()
    wts = {n: args[n] for n in WEIGHT_ORDER}
    mom1 = {n: args["m_" + n] for n in WEIGHT_ORDER}
    mom2 = {n: args["v_" + n] for n in WEIGHT_ORDER}
    cx, cy, cc = _mesh_pos()
    dev = 4 * cx + 2 * cy + cc
    conv_cols = D // N_DEV

    conv_pad = jnp.concatenate([conv_w, jnp.zeros((SMALL_R - CONV_K, conv_cols), F32)], axis=0)
    gains = {n: wts[n].reshape(1, D) for n in GAINS}
    loss8, dx, flights, gs = _step(x[0], mem[0], loss_target[0], _exchange_shards(wts), conv_pad, gains,
                                 b_gate.reshape(1, 2 * D), cc.reshape(1).astype(jnp.int32))

    grads, delta, new_m, new_v = {}, {}, {}, {}

    def operands(n, transposed):
        trio = (wts[n], mom1[n], mom2[n])
        return tuple(a.T for a in trio) if transposed else trio

    def record(n, res, transposed):
        grads[n], delta[n], new_m[n], new_v[n] = [r.T for r in res] if transposed else res

    early = [("w_ffn2_gu", "w_ffn2_gu", 0, True), ("w_ffn2_down", "w_ffn2_down", 0, False),
             ("w_ckv", "w_ckv", 0, False), ("w_in", "w_in", 0, False)]
    early += [(n, "mix", k, False) for k, n in enumerate(MIX_MATS)]
    early += [(n, "cross", k, False) for k, n in enumerate(CROSS_MATS)]
    chip = (2 * cx + cy).reshape(1).astype(jnp.int32)
    (names_early, flight_early), (names_w_in, flight_w_in), (last_names, flight_last) = flights
    token = flight_last.token
    own, land = {}, {}
    for names, flight, tag in ((names_early, flight_early, "early"), (names_w_in, flight_w_in, "w_in")):
        own_parts, landed = _chip_exchange_end(flight, token, "grads_to_chips_%s_end" % tag)
        own.update(zip(names, own_parts))
        land.update(zip(names, landed))
    for n, buf, row_block, transposed in early:
        w, m1, m2 = operands(n, transposed)
        record(n, _adamw_own(w, land[buf], own[buf], chip, m1, m2, "adamw_" + n, row_block, token), transposed)

    after = jnp.concatenate([new_v[n][:1, :1] for n, _, _, _ in early], axis=0)
    own_parts, landed = _chip_exchange_end(flight_last, after, "grads_to_chips_ffn1_end")
    for n, own_n, land_n, transposed in zip(last_names, own_parts, landed, (True, False)):
        w, m1, m2 = operands(n, transposed)
        record(n, _adamw_own(w, land_n, own_n, chip, m1, m2, "adamw_" + n), transposed)

    gs_rows = {n: gs[n] for n in GAINS + ("b_gate",)}
    small_mine = _pack_small(gs_rows, gs["conv_w"][:CONV_K]) + new_v[last_names[-1]][0, 0] * 0.0
    small_mine = small_mine.at[LOSS_ROW, 0].set(loss8[0, 0])
    small_all = _run_exchange(_gather_exchange([small_mine]), "gather_small_grads")[0]
    small_sum = _sum_slots(small_all, "small_grads_sum")
    loss = small_sum[LOSS_ROW, 0]
    grad_small = _unpack_small(small_sum)
    grad_small["conv_w"] = lax.dynamic_slice_in_dim(grad_small["conv_w"], dev * conv_cols, conv_cols, axis=1)
    grads.update(grad_small)

    def small_buf(vals):
        return _pack_small(vals, jnp.concatenate([vals["conv_w"], jnp.zeros((CONV_K, D - conv_cols), F32)], axis=1))

    _, d_s, m_s, v_s = _adamw(small_buf(wts), small_buf(grads)[None], small_buf(mom1), small_buf(mom2), "adamw_small")
    for res, buf in ((delta, d_s), (new_m, m_s), (new_v, v_s)):
        un = _unpack_small(buf)
        for n in GAINS + ("b_gate",):
            res[n] = un[n]
        res["conv_w"] = un["conv_w"][:, :conv_cols]

    return (loss, dx[None], *[grads[n] for n in WEIGHT_ORDER], *[delta[n] for n in WEIGHT_ORDER],
            *[new_m[n] for n in WEIGHT_ORDER], *[new_v[n] for n in WEIGHT_ORDER])
```

```python
import types

import jax
import jax.numpy as jnp
from jax import lax
from jax.experimental import pallas as pl
from jax.experimental.pallas import tpu as pltpu

F32 = jnp.float32
BF16 = jnp.bfloat16

D = 1024
DFF = 2816
SB_H = 8
SB_DH = 128
X_H = 4
X_DH = 256
CONV_K = 3
RMS_EPS = 1e-6
N_DEV = 8
N_CHIP = 4
SQ_ROWS = D // N_DEV

ADAM_LR = 0.001
ADAM_B1 = 0.9
ADAM_B2 = 0.999
ADAM_EPS = 1e-08
ADAM_WD = 0.01
ADAM_STEP = 10

TM = 256
TQ = 512
TK = 256
SB_HPS = 2
VMEM_LIMIT = 56 << 20

FF_BLK = DFF // 4
FF_PAD = 768
FF_SUB = 256
DOWN_ROWS = DFF // N_DEV

MIX_MATS = ("w_conv_out", "w_attn_out", "w_o")
CROSS_MATS = ("w_cq", "w_co")

_ANY = pl.BlockSpec(memory_space=pl.ANY)


def _cparams(n_axes=1):
    return pltpu.CompilerParams(
        dimension_semantics=("arbitrary",) * n_axes, vmem_limit_bytes=VMEM_LIMIT)


def _row_spec(tm, n):
    return pl.BlockSpec((tm, n), lambda i: (i, 0))


def _blk_row_spec(nb, tm, n):
    return pl.BlockSpec((nb, tm, n), lambda i: (0, i, 0))


def _const_spec(shape):
    zeros = (0,) * len(shape)
    return pl.BlockSpec(shape, lambda i: zeros)


def _dot(a, b):
    return jnp.dot(a, b, preferred_element_type=F32)


def _dot_nt(a, b):
    return lax.dot_general(a, b, (((1,), (1,)), ((), ())), preferred_element_type=F32)


def _dot_tn(a, b):
    return lax.dot_general(a, b, (((0,), (0,)), ((), ())), preferred_element_type=F32)


def _sigmoid(x):
    return 1.0 / (1.0 + jnp.exp(-x))


def _call(body, operands, *, grid, in_specs, out_specs, out_shape, scratch_shapes, name, comm=None):
    n_in, n_out, n_sc = len(in_specs), len(out_specs), len(scratch_shapes)
    if comm is None:
        outs = pl.pallas_call(
            body, grid=grid, name=name, in_specs=in_specs, out_specs=out_specs, out_shape=out_shape,
            scratch_shapes=scratch_shapes, compiler_params=_cparams(len(grid)))(*operands)
        return list(outs), []
    c_in, c_out, c_sem = len(comm.inputs), len(comm.out_shapes), len(comm.sem_shapes)

    def hosted(*refs):
        bounds = [0, n_in, c_in, n_out, c_out, n_sc, c_sem]
        parts, pos = [], 0
        for k in bounds[1:]:
            parts.append(refs[pos:pos + k])
            pos += k
        ins, cins, outs, couts, scr, sems = parts
        step, n_steps = pl.program_id(0), grid[0]
        for ax in range(1, len(grid)):
            step, n_steps = step * grid[ax] + pl.program_id(ax), n_steps * grid[ax]

        @pl.when(step == 0)
        def _():
            comm.start(cins, couts, sems)

        @pl.when(step == (2 * n_steps) // 3)
        def _():
            comm.middle(cins, couts, sems)

        body(*ins, *outs, *scr)

        @pl.when(step == n_steps - 1)
        def _():
            comm.finish(cins, couts, sems)

    res = pl.pallas_call(
        hosted, grid=grid, name=name, in_specs=list(in_specs) + [_ANY] * c_in,
        out_specs=list(out_specs) + [_ANY] * c_out, out_shape=list(out_shape) + list(comm.out_shapes),
        scratch_shapes=list(scratch_shapes) + list(comm.sem_shapes),
        compiler_params=_cparams(len(grid)))(*operands, *comm.inputs)
    return list(res[:n_out]), list(res[n_out:])


def _load_resident(step, pairs, sems):
    @pl.when(step == 0)
    def _():
        copies = [pltpu.make_async_copy(src, dst, sems.at[k]) for k, (src, dst) in enumerate(pairs)]
        for cp in copies:
            cp.start()
        for cp in copies:
            cp.wait()


def _square_pairs(buf_hbm, index, dst):
    off = index * SQ_ROWS
    return [(buf_hbm.at[d, off:off + SQ_ROWS, :], dst.at[d * SQ_ROWS:(d + 1) * SQ_ROWS, :]) for d in range(N_DEV)]


def _down_pairs(wd_hbm, dst):
    return [(wd_hbm.at[d], dst.at[d // 2, (d % 2) * DOWN_ROWS:(d % 2 + 1) * DOWN_ROWS, :]) for d in range(N_DEV)]


def _zero_down_pad(step, dst):
    @pl.when(step == 0)
    def _():
        dst[:, FF_BLK:, :] = jnp.zeros((4, FF_PAD - FF_BLK, D), BF16)


def _rms_fwd_tile(xt, g):
    r = lax.rsqrt(jnp.mean(xt * xt, axis=-1, keepdims=True) + RMS_EPS)
    return (xt * r) * g


def _rms_bwd_tile(xt, g, dn):
    r = lax.rsqrt(jnp.mean(xt * xt, axis=-1, keepdims=True) + RMS_EPS)
    xhat = xt * r
    dxhat = dn * g
    dx = r * (dxhat - xhat * jnp.mean(dxhat * xhat, axis=-1, keepdims=True))
    dg = jnp.sum(dn * xhat, axis=0, keepdims=True)
    return dx, dg


def _accumulate(ref, step, value):
    @pl.when(step == 0)
    def _():
        ref[...] = value

    @pl.when(step != 0)
    def _():
        ref[...] = ref[...] + value


def _ffn_fwd(x, g, wgu, wd, name, comm=None, head=None):
    t = x.shape[0]

    def body(x_ref, g_ref, wgu_hbm, wd_hbm, *refs):
        if head is None:
            n_ref, gate_ref, up_ref, act_ref, h_ref, wgu_v, wd_v, sems = refs
        else:
            gf_ref, t_ref, n_ref, gate_ref, up_ref, act_ref, dh_ref, loss_ref, dgf_ref, wgu_v, wd_v, sems = refs
        step = pl.program_id(0)
        _zero_down_pad(step, wd_v)
        _load_resident(step, [(wgu_hbm, wgu_v)] + _down_pairs(wd_hbm, wd_v), sems)
        xt = x_ref[...]
        n = _rms_fwd_tile(xt, g_ref[...]).astype(BF16)
        n_ref[...] = n
        acc = jnp.zeros((TM, D), F32)
        for j in range(4):
            for s in range(FF_PAD // FF_SUB):
                lo, hi = s * FF_SUB, (s + 1) * FF_SUB
                gt = _dot_nt(n, wgu_v[j, lo:hi, :])
                ut = _dot_nt(n, wgu_v[4 + j, lo:hi, :])
                gate_ref[j, :, lo:hi] = gt.astype(BF16)
                up_ref[j, :, lo:hi] = ut.astype(BF16)
                act_ref[j, :, lo:hi] = ((gt * _sigmoid(gt)) * ut).astype(BF16)
            acc = acc + _dot(act_ref[j], wd_v[j])
        ht = xt + 0.5 * acc
        if head is None:
            h_ref[...] = ht
        else:
            gain = gf_ref[...]
            diff = _rms_fwd_tile(ht, gain) - t_ref[...]
            part = 0.5 * jnp.sum(jnp.sum(diff * diff, axis=-1, keepdims=True) / D, axis=0, keepdims=True)
            dx, dg = _rms_bwd_tile(ht, gain, diff / D)
            dh_ref[...] = dx
            _accumulate(loss_ref, step, jnp.broadcast_to(part, (8, 128)))
            _accumulate(dgf_ref, step, dg)

    ff = jax.ShapeDtypeStruct((4, t, FF_PAD), BF16)
    operands, in_specs = (x, g, wgu, wd), [_row_spec(TM, D), _const_spec((1, D)), _ANY, _ANY]
    out_specs = [_row_spec(TM, D)] + [_blk_row_spec(4, TM, FF_PAD)] * 3 + [_row_spec(TM, D)]
    out_shape = [jax.ShapeDtypeStruct((t, D), BF16), ff, ff, ff, jax.ShapeDtypeStruct((t, D), F32)]
    if head is not None:
        operands += tuple(head)
        in_specs += [_const_spec((1, D)), _row_spec(TM, D)]
        out_specs += [_const_spec((8, 128)), _const_spec((1, D))]
        out_shape += [jax.ShapeDtypeStruct((8, 128), F32), jax.ShapeDtypeStruct((1, D), F32)]
    return _call(
        body, operands, grid=(t // TM,), name=name, comm=comm, in_specs=in_specs, out_specs=out_specs,
        out_shape=out_shape,
        scratch_shapes=[pltpu.VMEM((N_DEV, FF_PAD, D), BF16), pltpu.VMEM((4, FF_PAD, D), BF16),
                        pltpu.SemaphoreType.DMA((1 + N_DEV,))])


def _ffn_bwd(dh, xin, g, gate, up, wgu, wd, name, comm=None):
    t = dh.shape[0]

    def body(dh_ref, x_ref, g_ref, gate_ref, up_ref, wgu_hbm, wd_hbm,
             dgu_ref, dhb_ref, dx_ref, dg_ref, wgu_v, wd_v, sems):
        step = pl.program_id(0)
        _zero_down_pad(step, wd_v)
        _load_resident(step, [(wgu_hbm, wgu_v)] + _down_pairs(wd_hbm, wd_v), sems)
        dht = dh_ref[...]
        dhb = (0.5 * dht).astype(BF16)
        dhb_ref[...] = dhb
        dn = jnp.zeros((TM, D), F32)
        for j in range(4):
            for s in range(FF_PAD // FF_SUB):
                lo, hi = s * FF_SUB, (s + 1) * FF_SUB
                da = _dot_nt(dhb, wd_v[j, lo:hi, :])
                gt = gate_ref[j, :, lo:hi].astype(F32)
                ut = up_ref[j, :, lo:hi].astype(F32)
                sg = _sigmoid(gt)
                dgt = (da * ut * (sg * (1.0 + gt * (1.0 - sg)))).astype(BF16)
                dut = (da * (gt * sg)).astype(BF16)
                dgu_ref[j, :, lo:hi] = dgt
                dgu_ref[4 + j, :, lo:hi] = dut
            dn = dn + _dot(dgu_ref[j], wgu_v[j]) + _dot(dgu_ref[4 + j], wgu_v[4 + j])
        dx, dg = _rms_bwd_tile(x_ref[...], g_ref[...], dn)
        dx_ref[...] = dht + dx
        _accumulate(dg_ref, step, dg)

    return _call(
        body, (dh, xin, g, gate, up, wgu, wd), grid=(t // TM,), name=name, comm=comm,
        in_specs=[_row_spec(TM, D), _row_spec(TM, D), _const_spec((1, D)), _blk_row_spec(4, TM, FF_PAD),
                  _blk_row_spec(4, TM, FF_PAD), _ANY, _ANY],
        out_specs=[_blk_row_spec(N_DEV, TM, FF_PAD), _row_spec(TM, D), _row_spec(TM, D), _const_spec((1, D))],
        out_shape=[jax.ShapeDtypeStruct((N_DEV, t, FF_PAD), BF16), jax.ShapeDtypeStruct((t, D), BF16),
                   jax.ShapeDtypeStruct((t, D), F32), jax.ShapeDtypeStruct((1, D), F32)],
        scratch_shapes=[pltpu.VMEM((N_DEV, FF_PAD, D), BF16), pltpu.VMEM((4, FF_PAD, D), BF16),
                        pltpu.SemaphoreType.DMA((1 + N_DEV,))])


WIDE_TILES = (1024, 512, 256, 128)


def _pick_tile(n, options=(512, 256, 128)):
    for o in options:
        if n % o == 0:
            return o
    return n


def _into(stack, n_operands):
    if stack is None:
        return (), [], {}
    return (stack,), [_ANY], {n_operands: 0}


def _mm_tn_square(a, b, name, index, count, stack=None):
    k, m = a.shape
    _, n = b.shape
    tn = _pick_tile(n)
    extra, extra_specs, aliases = _into(stack, 2)

    def body(a_ref, b_ref, *rest):
        rest[-1][...] = _dot_tn(a_ref[...], b_ref[...]).astype(BF16).reshape(N_DEV, m // N_DEV, tn)

    return pl.pallas_call(
        body, grid=(n // tn,), name=name,
        in_specs=[pl.BlockSpec((k, m), lambda j: (0, 0)), pl.BlockSpec((k, tn), lambda j: (0, j))] + extra_specs,
        out_specs=pl.BlockSpec((N_DEV, m // N_DEV, tn), lambda j: (0, index, j)),
        out_shape=jax.ShapeDtypeStruct((N_DEV, count * (m // N_DEV), n), BF16),
        input_output_aliases=aliases, compiler_params=_cparams(1),
    )(a, b, *extra)


def _mm_tn_cols(a, b, name, first=0, count=None, stack=None):
    k, m = a.shape
    nb, _, n = b.shape
    tm = _pick_tile(m, WIDE_TILES)
    extra, extra_specs, aliases = _into(stack, 2)

    def body(a_ref, b_ref, *rest):
        rest[-1][0] = _dot_tn(a_ref[...].astype(BF16), b_ref[0].astype(BF16)).astype(BF16)

    return pl.pallas_call(
        body, grid=(nb, m // tm), name=name,
        in_specs=[pl.BlockSpec((k, tm), lambda j, i: (0, i)), pl.BlockSpec((1, k, n), lambda j, i: (j, 0, 0))]
                 + extra_specs,
        out_specs=pl.BlockSpec((1, tm, n), lambda j, i: (j + first, i, 0)),
        out_shape=jax.ShapeDtypeStruct((nb if count is None else count, m, n), BF16),
        input_output_aliases=aliases, compiler_params=_cparams(2),
    )(a, b, *extra)


def _mm_tn_rows(a, b, keep, name):
    nb, k, m = a.shape
    _, n = b.shape
    tn = _pick_tile(n, WIDE_TILES)

    def body(a_ref, b_ref, o_ref):
        o_ref[0] = _dot_tn(a_ref[0], b_ref[...])[:keep].astype(BF16)

    return pl.pallas_call(
        body, grid=(nb, n // tn), name=name,
        in_specs=[pl.BlockSpec((1, k, m), lambda j, i: (j, 0, 0)), pl.BlockSpec((k, tn), lambda j, i: (0, i))],
        out_specs=pl.BlockSpec((1, keep, tn), lambda j, i: (j, 0, i)),
        out_shape=jax.ShapeDtypeStruct((nb, keep, n), BF16),
        compiler_params=_cparams(2),
    )(a, b)


PCG_W = 5 * D
QKV_W = 3 * D
PROJ_SUB = 512


def _inproj_fwd(h, g, w_in, conv_w, name, comm=None):
    t = h.shape[0]

    def body(h_ref, g_ref, w_hbm, cw_ref, u_ref, pcg_ref, qkv_ref, yc_ref, w_v, tail_v, sems):
        step = pl.program_id(0)
        _load_resident(step, [(w_hbm, w_v)], sems)

        @pl.when(step == 0)
        def _():
            tail_v[...] = jnp.zeros_like(tail_v)

        u = _rms_fwd_tile(h_ref[...], g_ref[...]).astype(BF16)
        u_ref[...] = u
        for blk in range(N_DEV):
            for s in range(D // PROJ_SUB):
                lo, hi = s * PROJ_SUB, (s + 1) * PROJ_SUB
                p = _dot(u, w_v[blk, :, lo:hi])
                if blk < 3:
                    pcg_ref[:, blk * D + lo:blk * D + hi] = p
                elif blk < 6:
                    qkv_ref[:, (blk - 3) * D + lo:(blk - 3) * D + hi] = p.astype(BF16)
                else:
                    pcg_ref[:, (blk - 3) * D + lo:(blk - 3) * D + hi] = p
        xc = pcg_ref[:, D:2 * D] * pcg_ref[:, 2 * D:3 * D]
        ext = jnp.concatenate([tail_v[...], xc], axis=0)
        conv = (cw_ref[0:1, :] * pltpu.roll(ext, 2, 0)[8:] + cw_ref[1:2, :] * pltpu.roll(ext, 1, 0)[8:]
                + cw_ref[2:3, :] * xc)
        yc_ref[...] = (pcg_ref[:, 0:D] * conv).astype(BF16)
        tail_v[...] = xc[TM - 8:]

    return _call(
        body, (h, g, w_in, conv_w), grid=(t // TM,), name=name, comm=comm,
        in_specs=[_row_spec(TM, D), _const_spec((1, D)), _ANY, _const_spec((CONV_K, D))],
        out_specs=[_row_spec(TM, D), _row_spec(TM, PCG_W), _row_spec(TM, QKV_W), _row_spec(TM, D)],
        out_shape=[jax.ShapeDtypeStruct((t, D), BF16), jax.ShapeDtypeStruct((t, PCG_W), F32),
                   jax.ShapeDtypeStruct((t, QKV_W), BF16), jax.ShapeDtypeStruct((t, D), BF16)],
        scratch_shapes=[pltpu.VMEM((N_DEV, D, D), BF16), pltpu.VMEM((8, D), F32), pltpu.SemaphoreType.DMA((1,))])


CONV_CW = 256


def _shift_down(v, k, rows):
    return jnp.where(rows >= k, pltpu.roll(v, k, 0), 0.0)


def _shift_up(v, k, rows, t):
    return jnp.where(rows < t - k, pltpu.roll(v, t - k, 0), 0.0)


def _col_spec(t, cw, off):
    return pl.BlockSpec((t, cw), lambda j: (0, j + off))


def _conv_bwd(pcg, conv_w, dyc, name):
    t = pcg.shape[0]
    nb = D // CONV_CW

    def body(cb_ref, cc_ref, cx_ref, w_ref, dy_ref, dc_ref, dw_ref):
        rows = lax.broadcasted_iota(jnp.int32, (t, CONV_CW), 0)
        cc, cx = cc_ref[...], cx_ref[...]
        xc = cc * cx
        x1 = _shift_down(xc, 1, rows)
        x2 = _shift_down(xc, 2, rows)
        w0, w1, w2 = w_ref[0:1, :], w_ref[1:2, :], w_ref[2:3, :]
        conv = w0 * x2 + w1 * x1 + w2 * xc
        dy = dy_ref[...]
        dc_ref[0] = (dy * conv).astype(BF16)
        dconv = dy * cb_ref[...]
        dw_ref[...] = jnp.zeros((8, CONV_CW), F32)
        dw_ref[0:1, :] = jnp.sum(dconv * x2, axis=0, keepdims=True)
        dw_ref[1:2, :] = jnp.sum(dconv * x1, axis=0, keepdims=True)
        dw_ref[2:3, :] = jnp.sum(dconv * xc, axis=0, keepdims=True)
        dxc = w2 * dconv + w1 * _shift_up(dconv, 1, rows, t) + w0 * _shift_up(dconv, 2, rows, t)
        dc_ref[1] = (dxc * cx).astype(BF16)
        dc_ref[2] = (dxc * cc).astype(BF16)

    return pl.pallas_call(
        body, grid=(nb,), name=name,
        in_specs=[_col_spec(t, CONV_CW, 0), _col_spec(t, CONV_CW, nb), _col_spec(t, CONV_CW, 2 * nb),
                  pl.BlockSpec((CONV_K, CONV_CW), lambda j: (0, j)), _col_spec(t, CONV_CW, 0)],
        out_specs=[pl.BlockSpec((3, t, CONV_CW), lambda j: (0, 0, j)), pl.BlockSpec((8, CONV_CW), lambda j: (0, j))],
        out_shape=[jax.ShapeDtypeStruct((3, t, D), BF16), jax.ShapeDtypeStruct((8, D), F32)],
        compiler_params=_cparams(),
    )(pcg, pcg, pcg, conv_w, dyc)


def _tri2(cond):
    rr = lax.broadcasted_iota(jnp.int32, (2 * TK, TK), 0) & (TK - 1)
    cc = lax.broadcasted_iota(jnp.int32, (2 * TK, TK), 1)
    return cond(rr, cc).astype(BF16)


def _causal(shift, row0=0):
    rr = lax.broadcasted_iota(jnp.int32, (TQ - row0, TK), 0) + row0
    cc = lax.broadcasted_iota(jnp.int32, (TQ - row0, TK), 1)
    return cc + shift < rr


def _cumdot(v, tri2):
    hi = v.astype(BF16)
    lo = (v - hi.astype(F32)).astype(BF16)
    return _dot(jnp.concatenate([hi, lo], axis=1), tri2)


def _log_1m_beta(z):
    return -(jnp.maximum(z, 0.0) + jnp.log(1.0 + jnp.exp(-jnp.abs(z))))


def _sb_specs(t):
    g = SB_H // SB_HPS
    w = SB_HPS * SB_DH
    q_spec = pl.BlockSpec((TQ, w), lambda h, i: (i, h))
    k_spec = pl.BlockSpec((t, w), lambda h, i: (0, g + h))
    v_spec = pl.BlockSpec((t, w), lambda h, i: (0, 2 * g + h))
    ct_spec = pl.BlockSpec((SB_HPS, TQ, 1), lambda h, i: (h, i, 0))
    return g, w, q_spec, k_spec, v_spec, ct_spec


def _sb_fwd(qkv, name, comm=None):
    t = qkv.shape[0]
    scale = SB_DH ** -0.5
    g, w, q_spec, k_spec, v_spec, ct_spec = _sb_specs(t)

    def body(q_ref, k_ref, v_ref, y_ref, ct_ref):
        i = pl.program_id(1)
        later = _tri2(lambda j, s: j > s)
        n_diag = TQ // TK

        def block(j, carry, shift):
            off = pl.multiple_of(j * TK, TK)
            zs, ms = [], []
            for hd in range(SB_HPS):
                cols = slice(hd * SB_DH, (hd + 1) * SB_DH)
                z = _dot_nt(q_ref[:, cols], k_ref[pl.ds(off, TK), cols]) * scale
                m = _log_1m_beta(z)
                if shift is not None:
                    m = jnp.where(_causal(shift), m, 0.0)
                zs.append(z)
                ms.append(m)
            after = _cumdot(jnp.concatenate(ms, axis=0), later)
            out = []
            for hd in range(SB_HPS):
                acc, c_sum = carry[hd]
                cols = slice(hd * SB_DH, (hd + 1) * SB_DH)
                a = jnp.exp((ms[hd] + zs[hd]) + (c_sum + after[hd * TQ:(hd + 1) * TQ]))
                if shift is not None:
                    a = jnp.where(_causal(shift), a, 0.0)
                out.append((acc + _dot(a.astype(BF16), v_ref[pl.ds(off, TK), cols]),
                            c_sum + jnp.sum(ms[hd], axis=1, keepdims=True)))
            return tuple(out)

        carry = tuple((jnp.zeros((TQ, SB_DH), F32), jnp.zeros((TQ, 1), F32)) for _ in range(SB_HPS))
        for d in reversed(range(n_diag)):
            carry = block(i * n_diag + d, carry, d * TK)
        carry = lax.fori_loop(0, i * n_diag, lambda jj, c: block(i * n_diag - 1 - jj, c, None), carry)
        for hd in range(SB_HPS):
            y_ref[:, hd * SB_DH:(hd + 1) * SB_DH] = carry[hd][0].astype(BF16)
            ct_ref[hd] = carry[hd][1]

    return _call(
        body, (qkv, qkv, qkv), grid=(g, t // TQ), name=name, comm=comm,
        in_specs=[q_spec, k_spec, v_spec],
        out_specs=[q_spec, ct_spec],
        out_shape=[jax.ShapeDtypeStruct((t, D), BF16), jax.ShapeDtypeStruct((SB_H, t, 1), F32)],
        scratch_shapes=[])


def _sb_bwd(qkv, dy, ctot, after, name, comm=None):
    t = qkv.shape[0]
    scale = SB_DH ** -0.5
    g, w, q_spec, k_spec, v_spec, ct_spec = _sb_specs(t)
    acc_spec = pl.BlockSpec((2, t, w), lambda h, i: (0, 0, h))

    def body(q_ref, k_ref, v_ref, dy_ref, ct_ref, after_ref, dq_ref, dkv_ref):
        i = pl.program_id(1)

        @pl.when(i == 0)
        def _():
            dkv_ref[...] = jnp.zeros_like(dkv_ref)

        upto = _tri2(lambda j, s: j <= s)
        n_diag = TQ // TK

        def block(j, carry, shift):
            off = pl.multiple_of(j * TK, TK)
            r0 = 0 if shift is None else shift
            nr = TQ - r0
            causal = None if shift is None else _causal(shift, r0)

            def grow(old, delta):
                return old + delta if r0 == 0 else jnp.concatenate([old[:r0], old[r0:] + delta], axis=0)

            zs, ms = [], []
            for hd in range(SB_HPS):
                cols = slice(hd * SB_DH, (hd + 1) * SB_DH)
                z = _dot_nt(q_ref[r0:, cols], k_ref[pl.ds(off, TK), cols]) * scale
                m = _log_1m_beta(z)
                if causal is not None:
                    m = jnp.where(causal, m, 0.0)
                zs.append(z)
                ms.append(m)
            m_upto = _cumdot(jnp.concatenate(ms, axis=0), upto)
            ls, a_s, es = [], [], []
            for hd in range(SB_HPS):
                cols = slice(hd * SB_DH, (hd + 1) * SB_DH)
                l = ms[hd] + zs[hd]
                a = jnp.exp(l + ((ct_ref[hd, r0:] - carry[hd][1][r0:]) - m_upto[hd * nr:(hd + 1) * nr]))
                if causal is not None:
                    a = jnp.where(causal, a, 0.0)
                ls.append(l)
                a_s.append(a)
                es.append(_dot_nt(dy_ref[r0:, cols], v_ref[pl.ds(off, TK), cols]) * a)
            e_upto = _dot(jnp.concatenate(es, axis=0).astype(BF16), upto[:TK])
            out = []
            for hd in range(SB_HPS):
                dq, p_sum, e_sum = carry[hd]
                cols = slice(hd * SB_DH, (hd + 1) * SB_DH)
                e = es[hd]
                dz = e - jnp.exp(ls[hd]) * (e_sum[r0:] + e_upto[hd * nr:(hd + 1) * nr])
                if causal is not None:
                    dz = jnp.where(causal, dz, 0.0)
                dzs = (dz * scale).astype(BF16)
                dkv_ref[0, pl.ds(off, TK), cols] += _dot_tn(dzs, q_ref[r0:, cols])
                dkv_ref[1, pl.ds(off, TK), cols] += _dot_tn(a_s[hd].astype(BF16), dy_ref[r0:, cols])
                out.append((grow(dq, _dot(dzs, k_ref[pl.ds(off, TK), cols])),
                            grow(p_sum, jnp.sum(ms[hd], axis=1, keepdims=True)),
                            grow(e_sum, jnp.sum(e, axis=1, keepdims=True))))
            return tuple(out)

        zero = jnp.zeros((TQ, 1), F32)
        init = tuple((jnp.zeros((TQ, SB_DH), F32), zero, zero) for _ in range(SB_HPS))
        carry = lax.fori_loop(0, i * n_diag, lambda j, c: block(j, c, None), init)
        for d in range(n_diag):
            carry = block(i * n_diag + d, carry, d * TK)
        for hd in range(SB_HPS):
            dq_ref[:, hd * SB_DH:(hd + 1) * SB_DH] = carry[hd][0].astype(BF16)

    return _call(
        body, (qkv, qkv, qkv, dy, ctot, after), grid=(g, t // TQ), name=name, comm=comm,
        in_specs=[q_spec, k_spec, v_spec, q_spec, ct_spec, pl.BlockSpec(after.shape, lambda h, i: (0, 0))],
        out_specs=[q_spec, acc_spec],
        out_shape=[jax.ShapeDtypeStruct((t, D), BF16), jax.ShapeDtypeStruct((2, t, D), F32)],
        scratch_shapes=[])


def _gate_specs():
    return [pl.BlockSpec((TM, D), lambda i: (i, 3)), pl.BlockSpec((TM, D), lambda i: (i, 4))]


def _mix_pairs(mix_hbm, dsts):
    pairs = []
    for index, dst in enumerate(dsts):
        pairs += _square_pairs(mix_hbm, index, dst)
    return pairs


def _mix_out_fwd(yc, ysb, pcg, b_gate, h, w_mix, name, comm=None):
    t = h.shape[0]

    def body(yc_ref, ysb_ref, gc_ref, gs_ref, b_ref, h_ref, mix_hbm,
             a_ref, b_out_ref, mg_ref, h2_ref, wc_v, wa_v, wo_v, sems):
        _load_resident(pl.program_id(0), _mix_pairs(mix_hbm, (wc_v, wa_v, wo_v)), sems)
        a = _dot(yc_ref[...], wc_v[...])
        b = _dot(ysb_ref[...], wa_v[...])
        merged = (_sigmoid(gc_ref[...] + b_ref[:, :D]) * a + _sigmoid(gs_ref[...] + b_ref[:, D:]) * b).astype(BF16)
        a_ref[...] = a
        b_out_ref[...] = b
        mg_ref[...] = merged
        h2_ref[...] = h_ref[...] + _dot(merged, wo_v[...])

    return _call(
        body, (yc, ysb, pcg, pcg, b_gate, h, w_mix), grid=(t // TM,), name=name, comm=comm,
        in_specs=[_row_spec(TM, D), _row_spec(TM, D)] + _gate_specs()
                 + [_const_spec((1, 2 * D)), _row_spec(TM, D), _ANY],
        out_specs=[_row_spec(TM, D)] * 4,
        out_shape=[jax.ShapeDtypeStruct((t, D), F32), jax.ShapeDtypeStruct((t, D), F32),
                   jax.ShapeDtypeStruct((t, D), BF16), jax.ShapeDtypeStruct((t, D), F32)],
        scratch_shapes=[pltpu.VMEM((D, D), BF16)] * 3 + [pltpu.SemaphoreType.DMA((3 * N_DEV,))])


def _mix_out_bwd(dh2, a, b, pcg, b_gate, w_mix, name, comm=None):
    t = dh2.shape[0]

    def body(dh_ref, a_ref, b_ref, gc_ref, gs_ref, bias_ref, mix_hbm,
             dhb_ref, da_ref, db_ref, dgp_ref, dyc_ref, dysb_ref, dbias_ref, wc_v, wa_v, wo_v, sems):
        step = pl.program_id(0)
        _load_resident(step, _mix_pairs(mix_hbm, (wc_v, wa_v, wo_v)), sems)
        dhb = dh_ref[...].astype(BF16)
        dhb_ref[...] = dhb
        dm = _dot_nt(dhb, wo_v[...])
        gc = _sigmoid(gc_ref[...] + bias_ref[:, :D])
        gs = _sigmoid(gs_ref[...] + bias_ref[:, D:])
        da = (dm * gc).astype(BF16)
        db = (dm * gs).astype(BF16)
        da_ref[...] = da
        db_ref[...] = db
        dgc = dm * a_ref[...] * (gc * (1.0 - gc))
        dgs = dm * b_ref[...] * (gs * (1.0 - gs))
        dgp_ref[0] = dgc.astype(BF16)
        dgp_ref[1] = dgs.astype(BF16)
        _accumulate(dbias_ref.at[:, :D], step, jnp.sum(dgc, axis=0, keepdims=True))
        _accumulate(dbias_ref.at[:, D:], step, jnp.sum(dgs, axis=0, keepdims=True))
        dyc_ref[...] = _dot_nt(da, wc_v[...])
        dysb_ref[...] = _dot_nt(db, wa_v[...]).astype(BF16)

    return _call(
        body, (dh2, a, b, pcg, pcg, b_gate, w_mix), grid=(t // TM,), name=name, comm=comm,
        in_specs=[_row_spec(TM, D)] * 3 + _gate_specs() + [_const_spec((1, 2 * D)), _ANY],
        out_specs=[_row_spec(TM, D)] * 3 + [_blk_row_spec(2, TM, D), _row_spec(TM, D), _row_spec(TM, D),
                                            _const_spec((1, 2 * D))],
        out_shape=[jax.ShapeDtypeStruct((t, D), BF16)] * 3
                  + [jax.ShapeDtypeStruct((2, t, D), BF16), jax.ShapeDtypeStruct((t, D), F32),
                     jax.ShapeDtypeStruct((t, D), BF16), jax.ShapeDtypeStruct((1, 2 * D), F32)],
        scratch_shapes=[pltpu.VMEM((D, D), BF16)] * 3 + [pltpu.SemaphoreType.DMA((3 * N_DEV,))])


def _inproj_bwd(dconv, dq, dkv, dgp, w_in, h, g, dh_res, name, comm=None):
    t = h.shape[0]

    def body(dc_ref, dq_ref, dkv_ref, dgp_ref, w_hbm, h_ref, g_ref, dres_ref, dh_ref, dg_ref, w_v, sems):
        step = pl.program_id(0)
        _load_resident(step, [(w_hbm, w_v)], sems)
        du = _dot_nt(dq_ref[...], w_v[3])
        for k in range(3):
            du = du + _dot_nt(dc_ref[k], w_v[k])
        for k in range(2):
            du = du + _dot_nt(dkv_ref[k].astype(BF16), w_v[4 + k]) + _dot_nt(dgp_ref[k], w_v[6 + k])
        dx, dg = _rms_bwd_tile(h_ref[...], g_ref[...], du)
        dh_ref[...] = dres_ref[...] + dx
        _accumulate(dg_ref, step, dg)

    return _call(
        body, (dconv, dq, dkv, dgp, w_in, h, g, dh_res), grid=(t // TM,), name=name, comm=comm,
        in_specs=[_blk_row_spec(3, TM, D), _row_spec(TM, D), _blk_row_spec(2, TM, D), _blk_row_spec(2, TM, D), _ANY,
                  _row_spec(TM, D), _const_spec((1, D)), _row_spec(TM, D)],
        out_specs=[_row_spec(TM, D), _const_spec((1, D))],
        out_shape=[jax.ShapeDtypeStruct((t, D), F32), jax.ShapeDtypeStruct((1, D), F32)],
        scratch_shapes=[pltpu.VMEM((N_DEV, D, D), BF16), pltpu.SemaphoreType.DMA((1,))])


def _memkv_fwd(mem, g, w_ckv, name):
    m = mem.shape[0]

    def body(mem_ref, g_ref, w_ref, mn_ref, kv_ref):
        mn = _rms_fwd_tile(mem_ref[...], g_ref[...]).astype(BF16)
        mn_ref[...] = mn
        for j in range(N_DEV):
            kv_ref[j] = _dot(mn, w_ref[j]).astype(BF16)

    return pl.pallas_call(
        body, grid=(1,), name=name,
        in_specs=[_const_spec((m, D)), _const_spec((1, D)), _const_spec((N_DEV, D, X_DH))],
        out_specs=[_const_spec((m, D)), _const_spec((N_DEV, m, X_DH))],
        out_shape=[jax.ShapeDtypeStruct((m, D), BF16), jax.ShapeDtypeStruct((N_DEV, m, X_DH), BF16)],
        compiler_params=_cparams(),
    )(mem, g, w_ckv)


def _memkv_bwd(dkv, mem, g, w_ckv, name):
    m = mem.shape[0]

    def body(dkv_ref, mem_ref, g_ref, w_ref, dg_ref):
        dmn = jnp.zeros((m, D), F32)
        for j in range(N_DEV):
            dmn = dmn + _dot_nt(dkv_ref[j].astype(BF16), w_ref[j])
        _, dg = _rms_bwd_tile(mem_ref[...], g_ref[...], dmn)
        dg_ref[...] = dg

    return pl.pallas_call(
        body, grid=(1,), name=name,
        in_specs=[_const_spec((N_DEV, m, X_DH)), _const_spec((m, D)), _const_spec((1, D)),
                  _const_spec((N_DEV, D, X_DH))],
        out_specs=_const_spec((1, D)),
        out_shape=jax.ShapeDtypeStruct((1, D), F32),
        compiler_params=_cparams(),
    )(dkv, mem, g, w_ckv)


def _softmax_rows(s):
    e = jnp.exp(s - jnp.max(s, axis=-1, keepdims=True))
    return e / jnp.sum(e, axis=-1, keepdims=True)


def _cross_pairs(cross_hbm, wq_v, wo_v):
    return _square_pairs(cross_hbm, 0, wq_v) + _square_pairs(cross_hbm, 1, wo_v)


def _cross_fwd(h, g, kv, w_cross, name):
    t = h.shape[0]
    m = kv.shape[1]
    scale = X_DH ** -0.5

    def body(h_ref, g_ref, kv_ref, cross_hbm, hn_ref, qx_ref, o_ref, h3_ref, wq_v, wo_v, sems):
        _load_resident(pl.program_id(0), _cross_pairs(cross_hbm, wq_v, wo_v), sems)
        ht = h_ref[...]
        hn = _rms_fwd_tile(ht, g_ref[...]).astype(BF16)
        hn_ref[...] = hn
        qx = _dot(hn, wq_v[...]).astype(BF16)
        qx_ref[...] = qx
        for hd in range(X_H):
            lo, hi = hd * X_DH, (hd + 1) * X_DH
            p = _softmax_rows(_dot_nt(qx[:, lo:hi], kv_ref[hd]) * scale)
            o_ref[:, lo:hi] = _dot(p.astype(BF16), kv_ref[X_H + hd]).astype(BF16)
        h3_ref[...] = ht + _dot(o_ref[...], wo_v[...])

    return pl.pallas_call(
        body, grid=(t // TM,), name=name,
        in_specs=[_row_spec(TM, D), _const_spec((1, D)), _const_spec((N_DEV, m, X_DH)), _ANY],
        out_specs=[_row_spec(TM, D)] * 4,
        out_shape=[jax.ShapeDtypeStruct((t, D), BF16)] * 3 + [jax.ShapeDtypeStruct((t, D), F32)],
        scratch_shapes=[pltpu.VMEM((D, D), BF16)] * 2 + [pltpu.SemaphoreType.DMA((2 * N_DEV,))],
        compiler_params=_cparams(),
    )(h, g, kv, w_cross)


def _cross_bwd(dh3, h, g, qx, kv, w_cross, name, comm=None):
    t = h.shape[0]
    m = kv.shape[1]
    scale = X_DH ** -0.5

    def body(dh_ref, h_ref, g_ref, qx_ref, kv_ref, cross_hbm,
             dhb_ref, dqx_ref, dkv_ref, dh2_ref, dg_ref, wq_v, wo_v, sems):
        step = pl.program_id(0)
        _load_resident(step, _cross_pairs(cross_hbm, wq_v, wo_v), sems)

        @pl.when(step == 0)
        def _():
            dkv_ref[...] = jnp.zeros_like(dkv_ref)

        dht = dh_ref[...]
        dhb = dht.astype(BF16)
        dhb_ref[...] = dhb
        do = _dot_nt(dhb, wo_v[...]).astype(BF16)
        for hd in range(X_H):
            lo, hi = hd * X_DH, (hd + 1) * X_DH
            qh = qx_ref[:, lo:hi]
            kh = kv_ref[hd]
            p = _softmax_rows(_dot_nt(qh, kh) * scale)
            doh = do[:, lo:hi]
            dp = _dot_nt(doh, kv_ref[X_H + hd])
            ds = (p * (dp - jnp.sum(dp * p, axis=-1, keepdims=True)) * scale).astype(BF16)
            dqx_ref[:, lo:hi] = _dot(ds, kh).astype(BF16)
            dkv_ref[hd] += _dot_tn(ds, qh)
            dkv_ref[X_H + hd] += _dot_tn(p.astype(BF16), doh)
        dhn = _dot_nt(dqx_ref[...], wq_v[...])
        dx, dg = _rms_bwd_tile(h_ref[...], g_ref[...], dhn)
        dh2_ref[...] = dht + dx
        _accumulate(dg_ref, step, dg)

    return _call(
        body, (dh3, h, g, qx, kv, w_cross), grid=(t // TM,), name=name, comm=comm,
        in_specs=[_row_spec(TM, D), _row_spec(TM, D), _const_spec((1, D)), _row_spec(TM, D),
                  _const_spec((N_DEV, m, X_DH)), _ANY],
        out_specs=[_row_spec(TM, D), _row_spec(TM, D), _const_spec((N_DEV, m, X_DH)), _row_spec(TM, D),
                   _const_spec((1, D))],
        out_shape=[jax.ShapeDtypeStruct((t, D), BF16), jax.ShapeDtypeStruct((t, D), BF16),
                   jax.ShapeDtypeStruct((N_DEV, m, X_DH), F32), jax.ShapeDtypeStruct((t, D), F32),
                   jax.ShapeDtypeStruct((1, D), F32)],
        scratch_shapes=[pltpu.VMEM((D, D), BF16)] * 2 + [pltpu.SemaphoreType.DMA((2 * N_DEV,))])


def _adamw(w, parts, m, v, name, row_block=0, token=None):
    r, c = w.shape
    n = parts.shape[0]
    tr = _pick_tile(r, (256, 352, 128))
    off = row_block * (r // tr)

    def body(*refs):
        if token is None:
            _adamw_update(None, *refs)
        else:
            _adamw_update(refs[4], *refs[:4], *refs[5:])

    spec = _row_spec(tr, c)
    in_specs = [spec, pl.BlockSpec((n, tr, c), lambda i: (0, i + off, 0)), spec, spec]
    operands = (w, parts, m, v)
    if token is not None:
        in_specs.append(_const_spec(token.shape))
        operands += (token,)
    return pl.pallas_call(
        body, grid=(r // tr,), name=name, in_specs=in_specs, out_specs=[spec] * 4,
        out_shape=[jax.ShapeDtypeStruct((r, c), F32)] * 4,
        compiler_params=_cparams(),
    )(*operands)


def _adamw_update(tok_ref, w_ref, p_ref, m_ref, v_ref, g_ref, d_ref, nm_ref, nv_ref):
    gt = p_ref[0].astype(F32)
    for k in range(1, p_ref.shape[0]):
        gt = gt + p_ref[k].astype(F32)
    if tok_ref is not None:
        gt = gt + tok_ref[0:1, 0:1]
    _adamw_apply(gt, w_ref, m_ref, v_ref, g_ref, d_ref, nm_ref, nv_ref)


def _adamw_own(w, land, own, chip, m, v, name, row_block=0, token=None):
    r, c = w.shape
    tr = _pick_tile(r, (256, 352, 128))
    off = row_block * (r // tr)

    def body(chip_ref, w_ref, land_ref, own_ref, m_ref, v_ref, *rest):
        mine = own_ref[0].astype(F32)
        gt = jnp.where(chip_ref[0] == 0, mine, land_ref[0].astype(F32))
        for k in range(1, N_CHIP):
            gt = gt + jnp.where(chip_ref[0] == k, mine, land_ref[k].astype(F32))
        if token is not None:
            gt = gt + rest[0][0:1, 0:1]
        _adamw_apply(gt, w_ref, m_ref, v_ref, *rest[-4:])

    spec = pl.BlockSpec((tr, c), lambda i, chip_ref: (i, 0))
    in_specs = [spec, pl.BlockSpec((N_CHIP, tr, c), lambda i, chip_ref: (0, i + off, 0)),
                pl.BlockSpec((1, tr, c), lambda i, chip_ref: (chip_ref[0], i + off, 0)), spec, spec]
    operands = (chip, w, land, own, m, v)
    if token is not None:
        in_specs.append(pl.BlockSpec(token.shape, lambda i, chip_ref: (0, 0)))
        operands += (token,)
    return pl.pallas_call(
        body, name=name,
        grid_spec=pltpu.PrefetchScalarGridSpec(
            num_scalar_prefetch=1, grid=(r // tr,), in_specs=in_specs, out_specs=[spec] * 4),
        out_shape=[jax.ShapeDtypeStruct((r, c), F32)] * 4,
        compiler_params=_cparams(),
    )(*operands)


def _adamw_apply(gt, w_ref, m_ref, v_ref, g_ref, d_ref, nm_ref, nv_ref):
    g_ref[...] = gt
    nm = ADAM_B1 * m_ref[...] + (1.0 - ADAM_B1) * gt
    nv = ADAM_B2 * v_ref[...] + (1.0 - ADAM_B2) * jnp.square(gt)
    m_hat = nm / (1.0 - ADAM_B1 ** ADAM_STEP)
    v_hat = nv / (1.0 - ADAM_B2 ** ADAM_STEP)
    d_ref[...] = -ADAM_LR * (m_hat / (jnp.sqrt(v_hat) + ADAM_EPS) + ADAM_WD * w_ref[...])
    nm_ref[...] = nm
    nv_ref[...] = nv


def _mesh_pos():
    return lax.axis_index("x"), lax.axis_index("y"), lax.axis_index("c")


def _no_round(in_refs, out_refs, sems):
    pass


def _run_exchange(comm, name):
    c_in, c_out = len(comm.inputs), len(comm.out_shapes)

    def body(*refs):
        cins, couts, sems = refs[:c_in], refs[c_in:c_in + c_out], refs[c_in + c_out:]
        comm.start(cins, couts, sems)
        comm.middle(cins, couts, sems)
        comm.finish(cins, couts, sems)

    return list(pl.pallas_call(
        body, name=name, out_shape=list(comm.out_shapes),
        in_specs=[_ANY] * c_in, out_specs=[_ANY] * c_out, scratch_shapes=list(comm.sem_shapes),
    )(*comm.inputs))


def _gather_exchange(shards):
    n_arr = len(shards)

    def plan(x_refs, out_refs, sems):
        send_sems, recv_sems, local_sems = sems[:3]
        stage = sems[3:]
        x, y, c = _mesh_pos()
        me, sibling = (x, y, c), (x, y, 1 - c)
        xn, yn, diag = (1 - x, y), (x, 1 - y), (1 - x, 1 - y)

        def slot(a, px, py, pc, half=None):
            ref = out_refs[a].at[4 * px + 2 * py + pc]
            if half is None:
                return ref
            rows = shards[a].shape[0] // 2
            return ref.at[half * rows:(half + 1) * rows]

        def copy(a, k, block, to, half=None, src=None):
            dst = slot(a, *block, half)
            return pltpu.make_async_remote_copy(
                src_ref=dst if src is None else src, dst_ref=dst,
                send_sem=send_sems.at[a, k], recv_sem=recv_sems.at[a, k],
                device_id=to, device_id_type=pl.DeviceIdType.MESH)

        return types.SimpleNamespace(
            me=me, sibling=sibling, xn=xn, yn=yn, diag=diag, c=c, copy=copy,
            mine_in=[pltpu.make_async_copy(x_refs[a], stage[a], local_sems.at[a, 0]) for a in range(n_arr)],
            mine_out=[pltpu.make_async_copy(stage[a], slot(a, *me), local_sems.at[a, 1]) for a in range(n_arr)],
            first=[cp for a in range(n_arr) for cp in (
                copy(a, 0, me, sibling, src=x_refs[a]), copy(a, 1, me, (*xn, c), src=x_refs[a]),
                copy(a, 2, me, (*yn, c), src=x_refs[a]))],
            second=lambda a: (copy(a, 3, (*xn, c), (*yn, c), half=0), copy(a, 5, (*xn, c), sibling),
                              copy(a, 4, (*yn, c), (*xn, c), half=1), copy(a, 6, (*yn, c), sibling)),
            third=lambda a: (copy(a, 7, (*diag, c), sibling, half=0), copy(a, 8, (*diag, c), sibling, half=1)))

    def start(x_refs, out_refs, sems):
        p = plan(x_refs, out_refs, sems)
        for cp in p.first + p.mine_in:
            cp.start()
        for cp_in, cp_out in zip(p.mine_in, p.mine_out):
            cp_in.wait()
            cp_out.start()

    def middle(x_refs, out_refs, sems):
        p = plan(x_refs, out_refs, sems)
        for a in range(n_arr):
            to_yn, x_to_sib, to_xn, y_to_sib = p.second(a)
            p.copy(a, 1, (*p.xn, p.c), p.me).wait_recv()
            to_yn.start()
            x_to_sib.start()
            p.copy(a, 2, (*p.yn, p.c), p.me).wait_recv()
            to_xn.start()
            y_to_sib.start()

    def finish(x_refs, out_refs, sems):
        p = plan(x_refs, out_refs, sems)
        for a in range(n_arr):
            half0_to_sib, half1_to_sib = p.third(a)
            p.copy(a, 3, (*p.diag, p.c), p.me, half=0).wait_recv()
            half0_to_sib.start()
            p.copy(a, 4, (*p.diag, p.c), p.me, half=1).wait_recv()
            half1_to_sib.start()
        other = 1 - p.c
        for a in range(n_arr):
            p.copy(a, 0, p.sibling, p.me).wait_recv()
            p.copy(a, 5, (*p.xn, other), p.me).wait_recv()
            p.copy(a, 6, (*p.yn, other), p.me).wait_recv()
            p.copy(a, 7, (*p.diag, other), p.me, half=0).wait_recv()
            p.copy(a, 8, (*p.diag, other), p.me, half=1).wait_recv()
        for cp in p.first:
            cp.wait_send()
        for a in range(n_arr):
            for cp in p.second(a) + p.third(a):
                cp.wait_send()
        for cp in p.mine_out:
            cp.wait()

    return types.SimpleNamespace(
        inputs=list(shards), start=start, middle=middle, finish=finish,
        out_shapes=[jax.ShapeDtypeStruct((N_DEV,) + s.shape, s.dtype) for s in shards],
        sem_shapes=[pltpu.SemaphoreType.DMA((n_arr, 9)), pltpu.SemaphoreType.DMA((n_arr, 9)),
                    pltpu.SemaphoreType.DMA((n_arr, 2))] + [pltpu.VMEM(s.shape, s.dtype) for s in shards])


def _pair_exchange(grads):
    n_arr = len(grads)

    def plan(g_refs, land_refs, sems):
        send_sems, recv_sems = sems
        x, y, c = _mesh_pos()
        return [pltpu.make_async_remote_copy(
            src_ref=g_refs[a].at[2 * k + 1 - c], dst_ref=land_refs[a].at[k],
            send_sem=send_sems.at[a, k], recv_sem=recv_sems.at[a, k],
            device_id=(x, y, 1 - c), device_id_type=pl.DeviceIdType.MESH)
            for a in range(n_arr) for k in range(N_CHIP)]

    def start(g_refs, land_refs, sems):
        for cp in plan(g_refs, land_refs, sems):
            cp.start()

    def finish(g_refs, land_refs, sems):
        for cp in plan(g_refs, land_refs, sems):
            cp.wait()

    return types.SimpleNamespace(
        inputs=list(grads), start=start, middle=_no_round, finish=finish,
        out_shapes=[jax.ShapeDtypeStruct((N_CHIP,) + g.shape[1:], g.dtype) for g in grads],
        sem_shapes=[pltpu.SemaphoreType.DMA((n_arr, N_CHIP)), pltpu.SemaphoreType.DMA((n_arr, N_CHIP))])


def _chip_exchange(parts):
    n_arr = len(parts)

    def plan(p_refs, land_refs, sems):
        send_sems, recv_sems, local_sems = sems
        x, y, c = _mesh_pos()
        my_chip = 2 * x + y
        chips = [(1 - x, y), (x, 1 - y), (1 - x, 1 - y)]
        local = [pltpu.make_async_copy(p_refs[a].at[my_chip], land_refs[a].at[my_chip], local_sems.at[a])
                 for a in range(n_arr)]

        def copy(a, k, src_slot, dst_slot, px, py):
            return pltpu.make_async_remote_copy(
                src_ref=p_refs[a].at[src_slot], dst_ref=land_refs[a].at[dst_slot],
                send_sem=send_sems.at[a, k], recv_sem=recv_sems.at[a, k],
                device_id=(px, py, c), device_id_type=pl.DeviceIdType.MESH)

        sends = [copy(a, k, 2 * px + py, my_chip, px, py) for a in range(n_arr) for k, (px, py) in enumerate(chips)]
        arrivals = [copy(a, k, my_chip, 2 * px + py, px, py) for a in range(n_arr)
                    for k, (px, py) in enumerate(chips)]
        return local, sends, arrivals

    def start(p_refs, land_refs, sems):
        local, sends, _ = plan(p_refs, land_refs, sems)
        for cp in local + sends:
            cp.start()

    def finish(p_refs, land_refs, sems):
        local, sends, arrivals = plan(p_refs, land_refs, sems)
        for cp in arrivals:
            cp.wait_recv()
        for cp in sends:
            cp.wait_send()
        for cp in local:
            cp.wait()

    return types.SimpleNamespace(
        inputs=list(parts), start=start, middle=_no_round, finish=finish,
        out_shapes=[jax.ShapeDtypeStruct(p.shape, p.dtype) for p in parts],
        sem_shapes=[pltpu.SemaphoreType.DMA((n_arr, 3)), pltpu.SemaphoreType.DMA((n_arr, 3)),
                    pltpu.SemaphoreType.DMA((n_arr,))])


_HBM = pl.BlockSpec(memory_space=pltpu.HBM)
_SEM = pl.BlockSpec(memory_space=pltpu.SEMAPHORE)
_DATAFLOW = pltpu.SideEffectType.DATAFLOW_SIDE_EFFECTING


def _chip_copies(p_refs, land_refs, send_sems, recv_sems):
    x, y, c = _mesh_pos()
    my_chip = 2 * x + y
    chips = [(1 - x, y), (x, 1 - y), (1 - x, 1 - y)]
    return [pltpu.make_async_remote_copy(
        src_ref=p_refs[a].at[2 * px + py], dst_ref=land_refs[a].at[my_chip],
        send_sem=send_sems[3 * a + k], recv_sem=recv_sems[3 * a + k],
        device_id=(px, py, c), device_id_type=pl.DeviceIdType.MESH)
        for a in range(len(p_refs)) for k, (px, py) in enumerate(chips)]


def _chip_exchange_begin(parts, name):
    n_arr = len(parts)
    n_buf, n_copy = 2 * n_arr, 3 * n_arr
    lands = [lax.empty(p.shape, p.dtype) for p in parts]

    def body(*refs):
        p_refs, land_refs = refs[:n_arr], refs[n_arr:n_buf]
        send_sems, recv_sems, token = refs[n_buf:n_buf + n_copy], refs[n_buf + n_copy:n_buf + 2 * n_copy], refs[-1]
        for cp in _chip_copies(p_refs, land_refs, send_sems, recv_sems):
            cp.start()
        token[...] = jnp.zeros_like(token)

    bufs = list(parts) + list(lands)
    outs = pl.pallas_call(
        body, name=name,
        out_shape=(*[pltpu.SemaphoreType.DMA(())] * (2 * n_copy), *[pltpu.HBM(b.shape, b.dtype) for b in bufs],
                   jax.ShapeDtypeStruct((8, 128), F32)),
        in_specs=[_HBM] * n_buf,
        out_specs=(*[_SEM] * (2 * n_copy), *[_HBM] * n_buf, pl.BlockSpec(memory_space=pltpu.VMEM)),
        input_output_aliases={i: 2 * n_copy + i for i in range(n_buf)},
        compiler_params=pltpu.CompilerParams(has_side_effects=_DATAFLOW),
    )(*[pltpu.with_memory_space_constraint(b, pltpu.HBM) for b in bufs])
    sems = list(outs[:2 * n_copy])
    thru = list(outs[2 * n_copy:2 * n_copy + n_buf])
    return types.SimpleNamespace(send_sems=sems[:n_copy], recv_sems=sems[n_copy:], parts=thru[:n_arr],
                                 lands=thru[n_arr:], token=outs[-1])


def _chip_exchange_end(flight, after, name):
    send_sems, recv_sems, parts, lands = flight.send_sems, flight.recv_sems, flight.parts, flight.lands
    n_arr = len(parts)
    n_buf, n_copy = 2 * n_arr, 3 * n_arr

    def body(*refs):
        p_refs, land_refs = refs[:n_arr], refs[n_arr:n_buf]
        sems = refs[n_buf:n_buf + 2 * n_copy]
        for cp in _chip_copies(p_refs, land_refs, sems[:n_copy], sems[n_copy:]):
            cp.wait_send()
            cp.wait_recv()

    bufs = list(parts) + list(lands)
    outs = pl.pallas_call(
        body, name=name, out_shape=tuple(pltpu.HBM(b.shape, b.dtype) for b in bufs),
        in_specs=[_HBM] * n_buf + [_SEM] * (2 * n_copy) + [_ANY], out_specs=tuple([_HBM] * n_buf),
        input_output_aliases={i: i for i in range(n_buf)},
        compiler_params=pltpu.CompilerParams(has_side_effects=_DATAFLOW),
    )(*bufs, *send_sems, *recv_sems, after)
    return list(outs[:n_arr]), list(outs[n_arr:])


def _row_tile(r, cap=640):
    best = None
    for cand in range(16, min(r, cap) + 1, 16):
        if r % cand == 0:
            best = cand
    return best if best is not None else r


def _pair_sum(g, landed, core, name):
    _, r, c_dim = g.shape
    tr = _row_tile(r)

    def body(core_ref, mine_ref, theirs_ref, o_ref):
        o_ref[0] = (mine_ref[0].astype(F32) + theirs_ref[0].astype(F32)).astype(o_ref.dtype)

    return pl.pallas_call(
        body, name=name,
        grid_spec=pltpu.PrefetchScalarGridSpec(
            num_scalar_prefetch=1, grid=(N_CHIP, r // tr),
            in_specs=[pl.BlockSpec((1, tr, c_dim), lambda k, i, core_ref: (2 * k + core_ref[0], i, 0)),
                      pl.BlockSpec((1, tr, c_dim), lambda k, i, core_ref: (k, i, 0))],
            out_specs=pl.BlockSpec((1, tr, c_dim), lambda k, i, core_ref: (k, i, 0))),
        out_shape=jax.ShapeDtypeStruct((N_CHIP, r, c_dim), g.dtype),
        compiler_params=_cparams(2),
    )(core, g, landed)


def _sum_slots(parts, name):
    n, r, c_dim = parts.shape
    tr = _row_tile(r)

    def body(p_ref, o_ref):
        acc = p_ref[0].astype(F32)
        for k in range(1, n):
            acc = acc + p_ref[k].astype(F32)
        o_ref[...] = acc

    return pl.pallas_call(
        body, grid=(r // tr,), name=name,
        in_specs=[pl.BlockSpec((n, tr, c_dim), lambda i: (0, i, 0))],
        out_specs=_row_spec(tr, c_dim),
        out_shape=jax.ShapeDtypeStruct((r, c_dim), F32),
        compiler_params=_cparams(),
    )(parts)


GAINS = ("g_ffn1", "g_mix", "g_cross", "g_mem", "g_ffn2", "g_final")
SMALL = GAINS + ("b_gate", "conv_w")
SMALL_R = 16
LOSS_ROW = 11
WEIGHT_ORDER = ("g_ffn1", "w_ffn1_gu", "w_ffn1_down", "g_mix", "w_in", "b_gate", "conv_w", "w_conv_out",
                "w_attn_out", "w_o", "g_cross", "g_mem", "w_cq", "w_ckv", "w_co", "g_ffn2", "w_ffn2_gu",
                "w_ffn2_down", "g_final")
GU_NAMES = ("w_ffn1_gu", "w_ffn2_gu")


def _pack_small(vals, conv_rows):
    rows = [vals[n].reshape(1, D) for n in GAINS] + [vals["b_gate"].reshape(2, D), conv_rows.reshape(CONV_K, D)]
    used = len(GAINS) + 2 + CONV_K
    return jnp.concatenate(rows + [jnp.zeros((SMALL_R - used, D), F32)], axis=0)


def _unpack_small(buf):
    out = {n: buf[k] for k, n in enumerate(GAINS)}
    out["b_gate"] = buf[6:8].reshape(2 * D)
    out["conv_w"] = buf[8:8 + CONV_K]
    return out


def _exchange_shards(wts):
    out = {n: jnp.pad(wts[n].T.astype(BF16), ((0, FF_PAD - FF_BLK), (0, 0))) for n in GU_NAMES}
    for n in ("w_ckv", "w_in", "w_ffn1_down", "w_ffn2_down"):
        out[n] = wts[n].astype(BF16)
    out["mix"] = jnp.concatenate([wts[n].astype(BF16) for n in MIX_MATS], axis=0)
    out["cross"] = jnp.concatenate([wts[n].astype(BF16) for n in CROSS_MATS], axis=0)
    return out


def _reduce_group(grads, landed, core, names):
    return [_pair_sum(g, l, core, "grads_pair_sum_" + n) for g, l, n in zip(grads, landed, names)]


def _step(x, mem, target, sh, conv_pad, gains, b_gate, core):
    wg1, wd1, conv_all = _run_exchange(_gather_exchange([sh["w_ffn1_gu"], sh["w_ffn1_down"], conv_pad]), "gather_ffn1")
    conv_w = conv_all[:, :CONV_K, :].transpose(1, 0, 2).reshape(CONV_K, D)
    (n1, gate1, up1, act1, h1), (w_in,) = _ffn_fwd(
        x, gains["g_ffn1"], wg1, wd1, "ffn1_fwd", comm=_gather_exchange([sh["w_in"]]))
    (u, pcg, qkv, yc), (w_mix,) = _inproj_fwd(h1, gains["g_mix"], w_in, conv_w, "inproj_fwd",
                                              comm=_gather_exchange([sh["mix"]]))
    (ysb, ctot), (w_cross, w_ckv, wg2) = _sb_fwd(
        qkv, "sb_fwd", comm=_gather_exchange([sh["cross"], sh["w_ckv"], sh["w_ffn2_gu"]]))
    (a_mix, b_mix, merged, h2), (wd2,) = _mix_out_fwd(yc, ysb, pcg, b_gate, h1, w_mix, "mix_out_fwd",
                                                      comm=_gather_exchange([sh["w_ffn2_down"]]))
    mn, kv = _memkv_fwd(mem, gains["g_mem"], w_ckv, "memkv_fwd")
    hn, qx, o_x, h3 = _cross_fwd(h2, gains["g_cross"], kv, w_cross, "cross_fwd")
    (n4, gate2, up2, act2, dh4, loss, dg_final), _ = _ffn_fwd(h3, gains["g_ffn2"], wg2, wd2, "ffn2_fwd",
                                                              head=(gains["g_final"], target))

    gs = {"g_final": dg_final}
    (dgu2, dh4b, dh3, gs["g_ffn2"]), _ = _ffn_bwd(dh4, h3, gains["g_ffn2"], gate2, up2, wg2, wd2, "ffn2_bwd")
    grads_a = [_mm_tn_rows(dgu2, n4, FF_PAD, "dw_ffn2_gu"),
               _mm_tn_rows(act2, dh4b, FF_BLK, "dw_ffn2_down").reshape(N_DEV, DOWN_ROWS, D)]
    names_a = ["w_ffn2_gu", "w_ffn2_down"]
    (dh3b, dqx, dkv, dh2, gs["g_cross"]), landed_a = _cross_bwd(
        dh3, h2, gains["g_cross"], qx, kv, w_cross, "cross_bwd", comm=_pair_exchange(grads_a))
    sums_a = _reduce_group(grads_a, landed_a, core, names_a)
    cross_stack = _mm_tn_square(hn, dqx, "dw_cq", 0, len(CROSS_MATS))
    grads_b = [_mm_tn_cols(mn, dkv, "dw_ckv"), _mm_tn_square(o_x, dh3b, "dw_co", 1, len(CROSS_MATS), cross_stack)]
    names_b = ["w_ckv", "cross"]
    gs["g_mem"] = _memkv_bwd(dkv, mem, gains["g_mem"], w_ckv, "memkv_bwd")
    (dh2b, da_mix, db_mix, dgp, dyc, dysb, gs["b_gate"]), landed_b = _mix_out_bwd(
        dh2, a_mix, b_mix, pcg, b_gate, w_mix, "mix_out_bwd", comm=_pair_exchange(grads_b))
    sums_b = _reduce_group(grads_b, landed_b, core, names_b)
    mix_stack = _mm_tn_square(yc, da_mix, "dw_conv_out", 0, len(MIX_MATS))
    mix_stack = _mm_tn_square(ysb, db_mix, "dw_attn_out", 1, len(MIX_MATS), mix_stack)
    grads_c = [_mm_tn_square(merged, dh2b, "dw_o", 2, len(MIX_MATS), mix_stack)]
    landed_c = _run_exchange(_pair_exchange(grads_c), "grads_to_sibling_mix")
    sums_c = _reduce_group(grads_c, landed_c, core, ["mix"])
    flight_abc = _chip_exchange_begin(sums_a + sums_b + sums_c, "grads_to_chips_early_begin")
    (dq, dkv_sb), _ = _sb_bwd(qkv, dysb, ctot, flight_abc.token, "sb_bwd")
    dconv, gs["conv_w"] = _conv_bwd(pcg, conv_w, dyc, "conv_bwd")
    w_in_stack = _mm_tn_cols(u, dconv, "dw_in_conv", 0, N_DEV)
    w_in_stack = _mm_tn_cols(u, dq[None], "dw_in_q", 3, N_DEV, w_in_stack)
    w_in_stack = _mm_tn_cols(u, dkv_sb, "dw_in_kv", 4, N_DEV, w_in_stack)
    grads_d = [_mm_tn_cols(u, dgp, "dw_in_gates", 6, N_DEV, w_in_stack)]
    (dh1, gs["g_mix"]), landed_d = _inproj_bwd(dconv, dq, dkv_sb, dgp, w_in, h1, gains["g_mix"], dh2, "inproj_bwd",
                                               comm=_pair_exchange(grads_d))
    sums_d = _reduce_group(grads_d, landed_d, core, ["w_in"])
    flight_d = _chip_exchange_begin(sums_d, "grads_to_chips_w_in_begin")
    (dgu1, dh1b, dx, gs["g_ffn1"]), _ = _ffn_bwd(dh1, x, gains["g_ffn1"] + flight_d.token[0, 0], gate1, up1, wg1, wd1,
                                                 "ffn1_bwd")
    grads_e = [_mm_tn_rows(dgu1, n1, FF_PAD, "dw_ffn1_gu"),
               _mm_tn_rows(act1, dh1b, FF_BLK, "dw_ffn1_down").reshape(N_DEV, DOWN_ROWS, D)]
    names_e = ["w_ffn1_gu", "w_ffn1_down"]
    landed_e = _run_exchange(_pair_exchange(grads_e), "grads_to_sibling_ffn1")
    flight_e = _chip_exchange_begin(_reduce_group(grads_e, landed_e, core, names_e), "grads_to_chips_ffn1_begin")
    flights = [(names_a + names_b + ["mix"], flight_abc), (["w_in"], flight_d), (names_e, flight_e)]
    return loss, dx, flights, gs


def kernel(x, mem, g_ffn1, w_ffn1_gu, w_ffn1_down, g_mix, w_in, b_gate, conv_w, w_conv_out, w_attn_out, w_o, g_cross, g_mem, w_cq, w_ckv, w_co, g_ffn2, w_ffn2_gu, w_ffn2_down, g_final, loss_target, m_g_ffn1, m_w_ffn1_gu, m_w_ffn1_down, m_g_mix, m_w_in, m_b_gate, m_conv_w, m_w_conv_out, m_w_attn_out, m_w_o, m_g_cross, m_g_mem, m_w_cq, m_w_ckv, m_w_co, m_g_ffn2, m_w_ffn2_gu, m_w_ffn2_down, m_g_final, v_g_ffn1, v_w_ffn1_gu, v_w_ffn1_down, v_g_mix, v_w_in, v_b_gate, v_conv_w, v_w_conv_out, v_w_attn_out, v_w_o, v_g_cross, v_g_mem, v_w_cq, v_w_ckv, v_w_co, v_g_ffn2, v_w_ffn2_gu, v_w_ffn2_down, v_g_final):
    args = locals()
    wts = {n: args[n] for n in WEIGHT_ORDER}
    mom1 = {n: args["m_" + n] for n in WEIGHT_ORDER}
    mom2 = {n: args["v_" + n] for n in WEIGHT_ORDER}
    cx, cy, cc = _mesh_pos()
    dev = 4 * cx + 2 * cy + cc
    conv_cols = D // N_DEV

    conv_pad = jnp.concatenate([conv_w, jnp.zeros((SMALL_R - CONV_K, conv_cols), F32)], axis=0)
    gains = {n: wts[n].reshape(1, D) for n in GAINS}
    loss8, dx, flights, gs = _step(x[0], mem[0], loss_target[0], _exchange_shards(wts), conv_pad, gains,
                                 b_gate.reshape(1, 2 * D), cc.reshape(1).astype(jnp.int32))

    grads, delta, new_m, new_v = {}, {}, {}, {}

    def operands(n, transposed):
        trio = (wts[n], mom1[n], mom2[n])
        return tuple(a.T for a in trio) if transposed else trio

    def record(n, res, transposed):
        grads[n], delta[n], new_m[n], new_v[n] = [r.T for r in res] if transposed else res

    early = [("w_ffn2_gu", "w_ffn2_gu", 0, True), ("w_ffn2_down", "w_ffn2_down", 0, False),
             ("w_ckv", "w_ckv", 0, False), ("w_in", "w_in", 0, False)]
    early += [(n, "mix", k, False) for k, n in enumerate(MIX_MATS)]
    early += [(n, "cross", k, False) for k, n in enumerate(CROSS_MATS)]
    chip = (2 * cx + cy).reshape(1).astype(jnp.int32)
    (names_early, flight_early), (names_w_in, flight_w_in), (last_names, flight_last) = flights
    token = flight_last.token
    own, land = {}, {}
    for names, flight, tag in ((names_early, flight_early, "early"), (names_w_in, flight_w_in, "w_in")):
        own_parts, landed = _chip_exchange_end(flight, token, "grads_to_chips_%s_end" % tag)
        own.update(zip(names, own_parts))
        land.update(zip(names, landed))
    for n, buf, row_block, transposed in early:
        w, m1, m2 = operands(n, transposed)
        record(n, _adamw_own(w, land[buf], own[buf], chip, m1, m2, "adamw_" + n, row_block, token), transposed)

    after = jnp.concatenate([new_v[n][:1, :1] for n, _, _, _ in early], axis=0)
    own_parts, landed = _chip_exchange_end(flight_last, after, "grads_to_chips_ffn1_end")
    for n, own_n, land_n, transposed in zip(last_names, own_parts, landed, (True, False)):
        w, m1, m2 = operands(n, transposed)
        record(n, _adamw_own(w, land_n, own_n, chip, m1, m2, "adamw_" + n), transposed)

    gs_rows = {n: gs[n] for n in GAINS + ("b_gate",)}
    small_mine = _pack_small(gs_rows, gs["conv_w"][:CONV_K]) + new_v[last_names[-1]][0, 0] * 0.0
    small_mine = small_mine.at[LOSS_ROW, 0].set(loss8[0, 0])
    small_all = _run_exchange(_gather_exchange([small_mine]), "gather_small_grads")[0]
    small_sum = _sum_slots(small_all, "small_grads_sum")
    loss = small_sum[LOSS_ROW, 0]
    grad_small = _unpack_small(small_sum)
    grad_small["conv_w"] = lax.dynamic_slice_in_dim(grad_small["conv_w"], dev * conv_cols, conv_cols, axis=1)
    grads.update(grad_small)

    def small_buf(vals):
        return _pack_small(vals, jnp.concatenate([vals["conv_w"], jnp.zeros((CONV_K, D - conv_cols), F32)], axis=1))

    _, d_s, m_s, v_s = _adamw(small_buf(wts), small_buf(grads)[None], small_buf(mom1), small_buf(mom2), "adamw_small")
    for res, buf in ((delta, d_s), (new_m, m_s), (new_v, v_s)):
        un = _unpack_small(buf)
        for n in GAINS + ("b_gate",):
            res[n] = un[n]
        res["conv_w"] = un["conv_w"][:, :conv_cols]

    return (loss, dx[None], *[grads[n] for n in WEIGHT_ORDER], *[delta[n] for n in WEIGHT_ORDER],
            *[new_m[n] for n in WEIGHT_ORDER], *[new_v[n] for n in WEIGHT_ORDER])
```

```python
import types

import jax
import jax.numpy as jnp
from jax import lax
from jax.experimental import pallas as pl
from jax.experimental.pallas import tpu as pltpu

F32 = jnp.float32
BF16 = jnp.bfloat16

D = 1024
DFF = 2816
SB_H = 8
SB_DH = 128
X_H = 4
X_DH = 256
CONV_K = 3
RMS_EPS = 1e-6
N_DEV = 8
N_CHIP = 4
SQ_ROWS = D // N_DEV

ADAM_LR = 0.001
ADAM_B1 = 0.9
ADAM_B2 = 0.999
ADAM_EPS = 1e-08
ADAM_WD = 0.01
ADAM_STEP = 10

TM = 256
TQ = 512
TK = 256
SB_HPS = 2
VMEM_LIMIT = 56 << 20

FF_BLK = DFF // 4
FF_PAD = 768
FF_SUB = 256
DOWN_ROWS = DFF // N_DEV

MIX_MATS = ("w_conv_out", "w_attn_out", "w_o")
CROSS_MATS = ("w_cq", "w_co")

_ANY = pl.BlockSpec(memory_space=pl.ANY)


def _cparams(n_axes=1):
    return pltpu.CompilerParams(
        dimension_semantics=("arbitrary",) * n_axes, vmem_limit_bytes=VMEM_LIMIT)


def _row_spec(tm, n):
    return pl.BlockSpec((tm, n), lambda i: (i, 0))


def _blk_row_spec(nb, tm, n):
    return pl.BlockSpec((nb, tm, n), lambda i: (0, i, 0))


def _const_spec(shape):
    zeros = (0,) * len(shape)
    return pl.BlockSpec(shape, lambda i: zeros)


def _dot(a, b):
    return jnp.dot(a, b, preferred_element_type=F32)


def _dot_nt(a, b):
    return lax.dot_general(a, b, (((1,), (1,)), ((), ())), preferred_element_type=F32)


def _dot_tn(a, b):
    return lax.dot_general(a, b, (((0,), (0,)), ((), ())), preferred_element_type=F32)


def _sigmoid(x):
    return 1.0 / (1.0 + jnp.exp(-x))


def _call(body, operands, *, grid, in_specs, out_specs, out_shape, scratch_shapes, name, comm=None):
    n_in, n_out, n_sc = len(in_specs), len(out_specs), len(scratch_shapes)
    if comm is None:
        outs = pl.pallas_call(
            body, grid=grid, name=name, in_specs=in_specs, out_specs=out_specs, out_shape=out_shape,
            scratch_shapes=scratch_shapes, compiler_params=_cparams(len(grid)))(*operands)
        return list(outs), []
    c_in, c_out, c_sem = len(comm.inputs), len(comm.out_shapes), len(comm.sem_shapes)

    def hosted(*refs):
        bounds = [0, n_in, c_in, n_out, c_out, n_sc, c_sem]
        parts, pos = [], 0
        for k in bounds[1:]:
            parts.append(refs[pos:pos + k])
            pos += k
        ins, cins, outs, couts, scr, sems = parts
        step, n_steps = pl.program_id(0), grid[0]
        for ax in range(1, len(grid)):
            step, n_steps = step * grid[ax] + pl.program_id(ax), n_steps * grid[ax]

        @pl.when(step == 0)
        def _():
            comm.start(cins, couts, sems)

        @pl.when(step == (2 * n_steps) // 3)
        def _():
            comm.middle(cins, couts, sems)

        body(*ins, *outs, *scr)

        @pl.when(step == n_steps - 1)
        def _():
            comm.finish(cins, couts, sems)

    res = pl.pallas_call(
        hosted, grid=grid, name=name, in_specs=list(in_specs) + [_ANY] * c_in,
        out_specs=list(out_specs) + [_ANY] * c_out, out_shape=list(out_shape) + list(comm.out_shapes),
        scratch_shapes=list(scratch_shapes) + list(comm.sem_shapes),
        compiler_params=_cparams(len(grid)))(*operands, *comm.inputs)
    return list(res[:n_out]), list(res[n_out:])


def _load_resident(step, pairs, sems):
    @pl.when(step == 0)
    def _():
        copies = [pltpu.make_async_copy(src, dst, sems.at[k]) for k, (src, dst) in enumerate(pairs)]
        for cp in copies:
            cp.start()
        for cp in copies:
            cp.wait()


def _square_pairs(buf_hbm, index, dst):
    off = index * SQ_ROWS
    return [(buf_hbm.at[d, off:off + SQ_ROWS, :], dst.at[d * SQ_ROWS:(d + 1) * SQ_ROWS, :]) for d in range(N_DEV)]


def _down_pairs(wd_hbm, dst):
    return [(wd_hbm.at[d], dst.at[d // 2, (d % 2) * DOWN_ROWS:(d % 2 + 1) * DOWN_ROWS, :]) for d in range(N_DEV)]


def _zero_down_pad(step, dst):
    @pl.when(step == 0)
    def _():
        dst[:, FF_BLK:, :] = jnp.zeros((4, FF_PAD - FF_BLK, D), BF16)


def _rms_fwd_tile(xt, g):
    r = lax.rsqrt(jnp.mean(xt * xt, axis=-1, keepdims=True) + RMS_EPS)
    return (xt * r) * g


def _rms_bwd_tile(xt, g, dn):
    r = lax.rsqrt(jnp.mean(xt * xt, axis=-1, keepdims=True) + RMS_EPS)
    xhat = xt * r
    dxhat = dn * g
    dx = r * (dxhat - xhat * jnp.mean(dxhat * xhat, axis=-1, keepdims=True))
    dg = jnp.sum(dn * xhat, axis=0, keepdims=True)
    return dx, dg


def _accumulate(ref, step, value):
    @pl.when(step == 0)
    def _():
        ref[...] = value

    @pl.when(step != 0)
    def _():
        ref[...] = ref[...] + value


def _ffn_fwd(x, g, wgu, wd, name, comm=None, head=None):
    t = x.shape[0]

    def body(x_ref, g_ref, wgu_hbm, wd_hbm, *refs):
        if head is None:
            n_ref, gate_ref, up_ref, act_ref, h_ref, wgu_v, wd_v, sems = refs
        else:
            gf_ref, t_ref, n_ref, gate_ref, up_ref, act_ref, dh_ref, loss_ref, dgf_ref, wgu_v, wd_v, sems = refs
        step = pl.program_id(0)
        _zero_down_pad(step, wd_v)
        _load_resident(step, [(wgu_hbm, wgu_v)] + _down_pairs(wd_hbm, wd_v), sems)
        xt = x_ref[...]
        n = _rms_fwd_tile(xt, g_ref[...]).astype(BF16)
        n_ref[...] = n
        acc = jnp.zeros((TM, D), F32)
        for j in range(4):
            for s in range(FF_PAD // FF_SUB):
                lo, hi = s * FF_SUB, (s + 1) * FF_SUB
                gt = _dot_nt(n, wgu_v[j, lo:hi, :])
                ut = _dot_nt(n, wgu_v[4 + j, lo:hi, :])
                gate_ref[j, :, lo:hi] = gt.astype(BF16)
                up_ref[j, :, lo:hi] = ut.astype(BF16)
                act_ref[j, :, lo:hi] = ((gt * _sigmoid(gt)) * ut).astype(BF16)
            acc = acc + _dot(act_ref[j], wd_v[j])
        ht = xt + 0.5 * acc
        if head is None:
            h_ref[...] = ht
        else:
            gain = gf_ref[...]
            diff = _rms_fwd_tile(ht, gain) - t_ref[...]
            part = 0.5 * jnp.sum(jnp.sum(diff * diff, axis=-1, keepdims=True) / D, axis=0, keepdims=True)
            dx, dg = _rms_bwd_tile(ht, gain, diff / D)
            dh_ref[...] = dx
            _accumulate(loss_ref, step, jnp.broadcast_to(part, (8, 128)))
            _accumulate(dgf_ref, step, dg)

    ff = jax.ShapeDtypeStruct((4, t, FF_PAD), BF16)
    operands, in_specs = (x, g, wgu, wd), [_row_spec(TM, D), _const_spec((1, D)), _ANY, _ANY]
    out_specs = [_row_spec(TM, D)] + [_blk_row_spec(4, TM, FF_PAD)] * 3 + [_row_spec(TM, D)]
    out_shape = [jax.ShapeDtypeStruct((t, D), BF16), ff, ff, ff, jax.ShapeDtypeStruct((t, D), F32)]
    if head is not None:
        operands += tuple(head)
        in_specs += [_const_spec((1, D)), _row_spec(TM, D)]
        out_specs += [_const_spec((8, 128)), _const_spec((1, D))]
        out_shape += [jax.ShapeDtypeStruct((8, 128), F32), jax.ShapeDtypeStruct((1, D), F32)]
    return _call(
        body, operands, grid=(t // TM,), name=name, comm=comm, in_specs=in_specs, out_specs=out_specs,
        out_shape=out_shape,
        scratch_shapes=[pltpu.VMEM((N_DEV, FF_PAD, D), BF16), pltpu.VMEM((4, FF_PAD, D), BF16),
                        pltpu.SemaphoreType.DMA((1 + N_DEV,))])


def _ffn_bwd(dh, xin, g, gate, up, wgu, wd, name, comm=None):
    t = dh.shape[0]

    def body(dh_ref, x_ref, g_ref, gate_ref, up_ref, wgu_hbm, wd_hbm,
             dgu_ref, dhb_ref, dx_ref, dg_ref, wgu_v, wd_v, sems):
        step = pl.program_id(0)
        _zero_down_pad(step, wd_v)
        _load_resident(step, [(wgu_hbm, wgu_v)] + _down_pairs(wd_hbm, wd_v), sems)
        dht = dh_ref[...]
        dhb = (0.5 * dht).astype(BF16)
        dhb_ref[...] = dhb
        dn = jnp.zeros((TM, D), F32)
        for j in range(4):
            for s in range(FF_PAD // FF_SUB):
                lo, hi = s * FF_SUB, (s + 1) * FF_SUB
                da = _dot_nt(dhb, wd_v[j, lo:hi, :])
                gt = gate_ref[j, :, lo:hi].astype(F32)
                ut = up_ref[j, :, lo:hi].astype(F32)
                sg = _sigmoid(gt)
                dgt = (da * ut * (sg * (1.0 + gt * (1.0 - sg)))).astype(BF16)
                dut = (da * (gt * sg)).astype(BF16)
                dgu_ref[j, :, lo:hi] = dgt
                dgu_ref[4 + j, :, lo:hi] = dut
            dn = dn + _dot(dgu_ref[j], wgu_v[j]) + _dot(dgu_ref[4 + j], wgu_v[4 + j])
        dx, dg = _rms_bwd_tile(x_ref[...], g_ref[...], dn)
        dx_ref[...] = dht + dx
        _accumulate(dg_ref, step, dg)

    return _call(
        body, (dh, xin, g, gate, up, wgu, wd), grid=(t // TM,), name=name, comm=comm,
        in_specs=[_row_spec(TM, D), _row_spec(TM, D), _const_spec((1, D)), _blk_row_spec(4, TM, FF_PAD),
                  _blk_row_spec(4, TM, FF_PAD), _ANY, _ANY],
        out_specs=[_blk_row_spec(N_DEV, TM, FF_PAD), _row_spec(TM, D), _row_spec(TM, D), _const_spec((1, D))],
        out_shape=[jax.ShapeDtypeStruct((N_DEV, t, FF_PAD), BF16), jax.ShapeDtypeStruct((t, D), BF16),
                   jax.ShapeDtypeStruct((t, D), F32), jax.ShapeDtypeStruct((1, D), F32)],
        scratch_shapes=[pltpu.VMEM((N_DEV, FF_PAD, D), BF16), pltpu.VMEM((4, FF_PAD, D), BF16),
                        pltpu.SemaphoreType.DMA((1 + N_DEV,))])


WIDE_TILES = (1024, 512, 256, 128)


def _pick_tile(n, options=(512, 256, 128)):
    for o in options:
        if n % o == 0:
            return o
    return n


def _into(stack, n_operands):
    if stack is None:
        return (), [], {}
    return (stack,), [_ANY], {n_operands: 0}


def _mm_tn_square(a, b, name, index, count, stack=None):
    k, m = a.shape
    _, n = b.shape
    tn = _pick_tile(n)
    extra, extra_specs, aliases = _into(stack, 2)

    def body(a_ref, b_ref, *rest):
        rest[-1][...] = _dot_tn(a_ref[...], b_ref[...]).astype(BF16).reshape(N_DEV, m // N_DEV, tn)

    return pl.pallas_call(
        body, grid=(n // tn,), name=name,
        in_specs=[pl.BlockSpec((k, m), lambda j: (0, 0)), pl.BlockSpec((k, tn), lambda j: (0, j))] + extra_specs,
        out_specs=pl.BlockSpec((N_DEV, m // N_DEV, tn), lambda j: (0, index, j)),
        out_shape=jax.ShapeDtypeStruct((N_DEV, count * (m // N_DEV), n), BF16),
        input_output_aliases=aliases, compiler_params=_cparams(1),
    )(a, b, *extra)


def _mm_tn_cols(a, b, name, first=0, count=None, stack=None):
    k, m = a.shape
    nb, _, n = b.shape
    tm = _pick_tile(m, WIDE_TILES)
    extra, extra_specs, aliases = _into(stack, 2)

    def body(a_ref, b_ref, *rest):
        rest[-1][0] = _dot_tn(a_ref[...].astype(BF16), b_ref[0].astype(BF16)).astype(BF16)

    return pl.pallas_call(
        body, grid=(nb, m // tm), name=name,
        in_specs=[pl.BlockSpec((k, tm), lambda j, i: (0, i)), pl.BlockSpec((1, k, n), lambda j, i: (j, 0, 0))]
                 + extra_specs,
        out_specs=pl.BlockSpec((1, tm, n), lambda j, i: (j + first, i, 0)),
        out_shape=jax.ShapeDtypeStruct((nb if count is None else count, m, n), BF16),
        input_output_aliases=aliases, compiler_params=_cparams(2),
    )(a, b, *extra)


def _mm_tn_rows(a, b, keep, name, comm=None):
    nb, k, m = a.shape
    _, n = b.shape
    tn = _pick_tile(n, WIDE_TILES)

    def body(a_ref, b_ref, o_ref):
        o_ref[0] = _dot_tn(a_ref[0], b_ref[...])[:keep].astype(BF16)

    (out,), couts = _call(
        body, (a, b), grid=(nb, n // tn), name=name, comm=comm,
        in_specs=[pl.BlockSpec((1, k, m), lambda j, i: (j, 0, 0)), pl.BlockSpec((k, tn), lambda j, i: (0, i))],
        out_specs=[pl.BlockSpec((1, keep, tn), lambda j, i: (j, 0, i))],
        out_shape=[jax.ShapeDtypeStruct((nb, keep, n), BF16)], scratch_shapes=[])
    return out if comm is None else (out, couts)


PCG_W = 5 * D
QKV_W = 3 * D
PROJ_SUB = 512


def _inproj_fwd(h, g, w_in, conv_w, name, comm=None):
    t = h.shape[0]

    def body(h_ref, g_ref, w_hbm, cw_ref, u_ref, pcg_ref, qkv_ref, yc_ref, w_v, tail_v, sems):
        step = pl.program_id(0)
        _load_resident(step, [(w_hbm, w_v)], sems)

        @pl.when(step == 0)
        def _():
            tail_v[...] = jnp.zeros_like(tail_v)

        u = _rms_fwd_tile(h_ref[...], g_ref[...]).astype(BF16)
        u_ref[...] = u
        for blk in range(N_DEV):
            for s in range(D // PROJ_SUB):
                lo, hi = s * PROJ_SUB, (s + 1) * PROJ_SUB
                p = _dot(u, w_v[blk, :, lo:hi])
                if blk < 3:
                    pcg_ref[:, blk * D + lo:blk * D + hi] = p
                elif blk < 6:
                    qkv_ref[:, (blk - 3) * D + lo:(blk - 3) * D + hi] = p.astype(BF16)
                else:
                    pcg_ref[:, (blk - 3) * D + lo:(blk - 3) * D + hi] = p
        xc = pcg_ref[:, D:2 * D] * pcg_ref[:, 2 * D:3 * D]
        ext = jnp.concatenate([tail_v[...], xc], axis=0)
        conv = (cw_ref[0:1, :] * pltpu.roll(ext, 2, 0)[8:] + cw_ref[1:2, :] * pltpu.roll(ext, 1, 0)[8:]
                + cw_ref[2:3, :] * xc)
        yc_ref[...] = (pcg_ref[:, 0:D] * conv).astype(BF16)
        tail_v[...] = xc[TM - 8:]

    return _call(
        body, (h, g, w_in, conv_w), grid=(t // TM,), name=name, comm=comm,
        in_specs=[_row_spec(TM, D), _const_spec((1, D)), _ANY, _const_spec((CONV_K, D))],
        out_specs=[_row_spec(TM, D), _row_spec(TM, PCG_W), _row_spec(TM, QKV_W), _row_spec(TM, D)],
        out_shape=[jax.ShapeDtypeStruct((t, D), BF16), jax.ShapeDtypeStruct((t, PCG_W), F32),
                   jax.ShapeDtypeStruct((t, QKV_W), BF16), jax.ShapeDtypeStruct((t, D), BF16)],
        scratch_shapes=[pltpu.VMEM((N_DEV, D, D), BF16), pltpu.VMEM((8, D), F32), pltpu.SemaphoreType.DMA((1,))])


CONV_CW = 256


def _shift_down(v, k, rows):
    return jnp.where(rows >= k, pltpu.roll(v, k, 0), 0.0)


def _shift_up(v, k, rows, t):
    return jnp.where(rows < t - k, pltpu.roll(v, t - k, 0), 0.0)


def _col_spec(t, cw, off):
    return pl.BlockSpec((t, cw), lambda j: (0, j + off))


def _conv_bwd(pcg, conv_w, dyc, name):
    t = pcg.shape[0]
    nb = D // CONV_CW

    def body(cb_ref, cc_ref, cx_ref, w_ref, dy_ref, dc_ref, dw_ref):
        rows = lax.broadcasted_iota(jnp.int32, (t, CONV_CW), 0)
        cc, cx = cc_ref[...], cx_ref[...]
        xc = cc * cx
        x1 = _shift_down(xc, 1, rows)
        x2 = _shift_down(xc, 2, rows)
        w0, w1, w2 = w_ref[0:1, :], w_ref[1:2, :], w_ref[2:3, :]
        conv = w0 * x2 + w1 * x1 + w2 * xc
        dy = dy_ref[...]
        dc_ref[0] = (dy * conv).astype(BF16)
        dconv = dy * cb_ref[...]
        dw_ref[...] = jnp.zeros((8, CONV_CW), F32)
        dw_ref[0:1, :] = jnp.sum(dconv * x2, axis=0, keepdims=True)
        dw_ref[1:2, :] = jnp.sum(dconv * x1, axis=0, keepdims=True)
        dw_ref[2:3, :] = jnp.sum(dconv * xc, axis=0, keepdims=True)
        dxc = w2 * dconv + w1 * _shift_up(dconv, 1, rows, t) + w0 * _shift_up(dconv, 2, rows, t)
        dc_ref[1] = (dxc * cx).astype(BF16)
        dc_ref[2] = (dxc * cc).astype(BF16)

    return pl.pallas_call(
        body, grid=(nb,), name=name,
        in_specs=[_col_spec(t, CONV_CW, 0), _col_spec(t, CONV_CW, nb), _col_spec(t, CONV_CW, 2 * nb),
                  pl.BlockSpec((CONV_K, CONV_CW), lambda j: (0, j)), _col_spec(t, CONV_CW, 0)],
        out_specs=[pl.BlockSpec((3, t, CONV_CW), lambda j: (0, 0, j)), pl.BlockSpec((8, CONV_CW), lambda j: (0, j))],
        out_shape=[jax.ShapeDtypeStruct((3, t, D), BF16), jax.ShapeDtypeStruct((8, D), F32)],
        compiler_params=_cparams(),
    )(pcg, pcg, pcg, conv_w, dyc)


def _tri2(cond):
    rr = lax.broadcasted_iota(jnp.int32, (2 * TK, TK), 0) & (TK - 1)
    cc = lax.broadcasted_iota(jnp.int32, (2 * TK, TK), 1)
    return cond(rr, cc).astype(BF16)


def _causal(shift, row0=0):
    rr = lax.broadcasted_iota(jnp.int32, (TQ - row0, TK), 0) + row0
    cc = lax.broadcasted_iota(jnp.int32, (TQ - row0, TK), 1)
    return cc + shift < rr


def _cumdot(v, tri2):
    hi = v.astype(BF16)
    lo = (v - hi.astype(F32)).astype(BF16)
    return _dot(jnp.concatenate([hi, lo], axis=1), tri2)


def _log_1m_beta(z):
    return -(jnp.maximum(z, 0.0) + jnp.log(1.0 + jnp.exp(-jnp.abs(z))))


def _sb_specs(t):
    g = SB_H // SB_HPS
    w = SB_HPS * SB_DH
    q_spec = pl.BlockSpec((TQ, w), lambda h, i: (i, h))
    k_spec = pl.BlockSpec((t, w), lambda h, i: (0, g + h))
    v_spec = pl.BlockSpec((t, w), lambda h, i: (0, 2 * g + h))
    ct_spec = pl.BlockSpec((SB_HPS, TQ, 1), lambda h, i: (h, i, 0))
    return g, w, q_spec, k_spec, v_spec, ct_spec


def _sb_fwd(qkv, name, comm=None):
    t = qkv.shape[0]
    scale = SB_DH ** -0.5
    g, w, q_spec, k_spec, v_spec, ct_spec = _sb_specs(t)

    def body(q_ref, k_ref, v_ref, y_ref, ct_ref):
        i = pl.program_id(1)
        later = _tri2(lambda j, s: j > s)
        n_diag = TQ // TK

        def block(j, carry, shift):
            off = pl.multiple_of(j * TK, TK)
            zs, ms = [], []
            for hd in range(SB_HPS):
                cols = slice(hd * SB_DH, (hd + 1) * SB_DH)
                z = _dot_nt(q_ref[:, cols], k_ref[pl.ds(off, TK), cols]) * scale
                m = _log_1m_beta(z)
                if shift is not None:
                    m = jnp.where(_causal(shift), m, 0.0)
                zs.append(z)
                ms.append(m)
            after = _cumdot(jnp.concatenate(ms, axis=0), later)
            out = []
            for hd in range(SB_HPS):
                acc, c_sum = carry[hd]
                cols = slice(hd * SB_DH, (hd + 1) * SB_DH)
                a = jnp.exp((ms[hd] + zs[hd]) + (c_sum + after[hd * TQ:(hd + 1) * TQ]))
                if shift is not None:
                    a = jnp.where(_causal(shift), a, 0.0)
                out.append((acc + _dot(a.astype(BF16), v_ref[pl.ds(off, TK), cols]),
                            c_sum + jnp.sum(ms[hd], axis=1, keepdims=True)))
            return tuple(out)

        carry = tuple((jnp.zeros((TQ, SB_DH), F32), jnp.zeros((TQ, 1), F32)) for _ in range(SB_HPS))
        for d in reversed(range(n_diag)):
            carry = block(i * n_diag + d, carry, d * TK)
        carry = lax.fori_loop(0, i * n_diag, lambda jj, c: block(i * n_diag - 1 - jj, c, None), carry)
        for hd in range(SB_HPS):
            y_ref[:, hd * SB_DH:(hd + 1) * SB_DH] = carry[hd][0].astype(BF16)
            ct_ref[hd] = carry[hd][1]

    return _call(
        body, (qkv, qkv, qkv), grid=(g, t // TQ), name=name, comm=comm,
        in_specs=[q_spec, k_spec, v_spec],
        out_specs=[q_spec, ct_spec],
        out_shape=[jax.ShapeDtypeStruct((t, D), BF16), jax.ShapeDtypeStruct((SB_H, t, 1), F32)],
        scratch_shapes=[])


def _sb_bwd(qkv, dy, ctot, after, name, comm=None):
    t = qkv.shape[0]
    scale = SB_DH ** -0.5
    g, w, q_spec, k_spec, v_spec, ct_spec = _sb_specs(t)
    acc_spec = pl.BlockSpec((2, t, w), lambda h, i: (0, 0, h))

    def body(q_ref, k_ref, v_ref, dy_ref, ct_ref, after_ref, dq_ref, dkv_ref):
        i = pl.program_id(1)

        @pl.when(i == 0)
        def _():
            dkv_ref[...] = jnp.zeros_like(dkv_ref)

        upto = _tri2(lambda j, s: j <= s)
        n_diag = TQ // TK

        def block(j, carry, shift):
            off = pl.multiple_of(j * TK, TK)
            r0 = 0 if shift is None else shift
            nr = TQ - r0
            causal = None if shift is None else _causal(shift, r0)

            def grow(old, delta):
                return old + delta if r0 == 0 else jnp.concatenate([old[:r0], old[r0:] + delta], axis=0)

            zs, ms = [], []
            for hd in range(SB_HPS):
                cols = slice(hd * SB_DH, (hd + 1) * SB_DH)
                z = _dot_nt(q_ref[r0:, cols], k_ref[pl.ds(off, TK), cols]) * scale
                m = _log_1m_beta(z)
                if causal is not None:
                    m = jnp.where(causal, m, 0.0)
                zs.append(z)
                ms.append(m)
            m_upto = _cumdot(jnp.concatenate(ms, axis=0), upto)
            ls, a_s, es = [], [], []
            for hd in range(SB_HPS):
                cols = slice(hd * SB_DH, (hd + 1) * SB_DH)
                l = ms[hd] + zs[hd]
                a = jnp.exp(l + ((ct_ref[hd, r0:] - carry[hd][1][r0:]) - m_upto[hd * nr:(hd + 1) * nr]))
                if causal is not None:
                    a = jnp.where(causal, a, 0.0)
                ls.append(l)
                a_s.append(a)
                es.append(_dot_nt(dy_ref[r0:, cols], v_ref[pl.ds(off, TK), cols]) * a)
            e_upto = _dot(jnp.concatenate(es, axis=0).astype(BF16), upto[:TK])
            out = []
            for hd in range(SB_HPS):
                dq, p_sum, e_sum = carry[hd]
                cols = slice(hd * SB_DH, (hd + 1) * SB_DH)
                e = es[hd]
                dz = e - jnp.exp(ls[hd]) * (e_sum[r0:] + e_upto[hd * nr:(hd + 1) * nr])
                if causal is not None:
                    dz = jnp.where(causal, dz, 0.0)
                dzs = (dz * scale).astype(BF16)
                dkv_ref[0, pl.ds(off, TK), cols] += _dot_tn(dzs, q_ref[r0:, cols])
                dkv_ref[1, pl.ds(off, TK), cols] += _dot_tn(a_s[hd].astype(BF16), dy_ref[r0:, cols])
                out.append((grow(dq, _dot(dzs, k_ref[pl.ds(off, TK), cols])),
                            grow(p_sum, jnp.sum(ms[hd], axis=1, keepdims=True)),
                            grow(e_sum, jnp.sum(e, axis=1, keepdims=True))))
            return tuple(out)

        zero = jnp.zeros((TQ, 1), F32)
        init = tuple((jnp.zeros((TQ, SB_DH), F32), zero, zero) for _ in range(SB_HPS))
        carry = lax.fori_loop(0, i * n_diag, lambda j, c: block(j, c, None), init)
        for d in range(n_diag):
            carry = block(i * n_diag + d, carry, d * TK)
        for hd in range(SB_HPS):
            dq_ref[:, hd * SB_DH:(hd + 1) * SB_DH] = carry[hd][0].astype(BF16)

    return _call(
        body, (qkv, qkv, qkv, dy, ctot, after), grid=(g, t // TQ), name=name, comm=comm,
        in_specs=[q_spec, k_spec, v_spec, q_spec, ct_spec, pl.BlockSpec(after.shape, lambda h, i: (0, 0))],
        out_specs=[q_spec, acc_spec],
        out_shape=[jax.ShapeDtypeStruct((t, D), BF16), jax.ShapeDtypeStruct((2, t, D), F32)],
        scratch_shapes=[])


def _gate_specs():
    return [pl.BlockSpec((TM, D), lambda i: (i, 3)), pl.BlockSpec((TM, D), lambda i: (i, 4))]


def _mix_pairs(mix_hbm, dsts):
    pairs = []
    for index, dst in enumerate(dsts):
        pairs += _square_pairs(mix_hbm, index, dst)
    return pairs


def _mix_out_fwd(yc, ysb, pcg, b_gate, h, w_mix, name, comm=None):
    t = h.shape[0]

    def body(yc_ref, ysb_ref, gc_ref, gs_ref, b_ref, h_ref, mix_hbm,
             a_ref, b_out_ref, mg_ref, h2_ref, wc_v, wa_v, wo_v, sems):
        _load_resident(pl.program_id(0), _mix_pairs(mix_hbm, (wc_v, wa_v, wo_v)), sems)
        a = _dot(yc_ref[...], wc_v[...])
        b = _dot(ysb_ref[...], wa_v[...])
        merged = (_sigmoid(gc_ref[...] + b_ref[:, :D]) * a + _sigmoid(gs_ref[...] + b_ref[:, D:]) * b).astype(BF16)
        a_ref[...] = a
        b_out_ref[...] = b
        mg_ref[...] = merged
        h2_ref[...] = h_ref[...] + _dot(merged, wo_v[...])

    return _call(
        body, (yc, ysb, pcg, pcg, b_gate, h, w_mix), grid=(t // TM,), name=name, comm=comm,
        in_specs=[_row_spec(TM, D), _row_spec(TM, D)] + _gate_specs()
                 + [_const_spec((1, 2 * D)), _row_spec(TM, D), _ANY],
        out_specs=[_row_spec(TM, D)] * 4,
        out_shape=[jax.ShapeDtypeStruct((t, D), F32), jax.ShapeDtypeStruct((t, D), F32),
                   jax.ShapeDtypeStruct((t, D), BF16), jax.ShapeDtypeStruct((t, D), F32)],
        scratch_shapes=[pltpu.VMEM((D, D), BF16)] * 3 + [pltpu.SemaphoreType.DMA((3 * N_DEV,))])


def _mix_out_bwd(dh2, a, b, pcg, b_gate, w_mix, name, comm=None):
    t = dh2.shape[0]

    def body(dh_ref, a_ref, b_ref, gc_ref, gs_ref, bias_ref, mix_hbm,
             dhb_ref, da_ref, db_ref, dgp_ref, dyc_ref, dysb_ref, dbias_ref, wc_v, wa_v, wo_v, sems):
        step = pl.program_id(0)
        _load_resident(step, _mix_pairs(mix_hbm, (wc_v, wa_v, wo_v)), sems)
        dhb = dh_ref[...].astype(BF16)
        dhb_ref[...] = dhb
        dm = _dot_nt(dhb, wo_v[...])
        gc = _sigmoid(gc_ref[...] + bias_ref[:, :D])
        gs = _sigmoid(gs_ref[...] + bias_ref[:, D:])
        da = (dm * gc).astype(BF16)
        db = (dm * gs).astype(BF16)
        da_ref[...] = da
        db_ref[...] = db
        dgc = dm * a_ref[...] * (gc * (1.0 - gc))
        dgs = dm * b_ref[...] * (gs * (1.0 - gs))
        dgp_ref[0] = dgc.astype(BF16)
        dgp_ref[1] = dgs.astype(BF16)
        _accumulate(dbias_ref.at[:, :D], step, jnp.sum(dgc, axis=0, keepdims=True))
        _accumulate(dbias_ref.at[:, D:], step, jnp.sum(dgs, axis=0, keepdims=True))
        dyc_ref[...] = _dot_nt(da, wc_v[...])
        dysb_ref[...] = _dot_nt(db, wa_v[...]).astype(BF16)

    return _call(
        body, (dh2, a, b, pcg, pcg, b_gate, w_mix), grid=(t // TM,), name=name, comm=comm,
        in_specs=[_row_spec(TM, D)] * 3 + _gate_specs() + [_const_spec((1, 2 * D)), _ANY],
        out_specs=[_row_spec(TM, D)] * 3 + [_blk_row_spec(2, TM, D), _row_spec(TM, D), _row_spec(TM, D),
                                            _const_spec((1, 2 * D))],
        out_shape=[jax.ShapeDtypeStruct((t, D), BF16)] * 3
                  + [jax.ShapeDtypeStruct((2, t, D), BF16), jax.ShapeDtypeStruct((t, D), F32),
                     jax.ShapeDtypeStruct((t, D), BF16), jax.ShapeDtypeStruct((1, 2 * D), F32)],
        scratch_shapes=[pltpu.VMEM((D, D), BF16)] * 3 + [pltpu.SemaphoreType.DMA((3 * N_DEV,))])


def _inproj_bwd(dconv, dq, dkv, dgp, w_in, h, g, dh_res, name, comm=None):
    t = h.shape[0]

    def body(dc_ref, dq_ref, dkv_ref, dgp_ref, w_hbm, h_ref, g_ref, dres_ref, dh_ref, dg_ref, w_v, sems):
        step = pl.program_id(0)
        _load_resident(step, [(w_hbm, w_v)], sems)
        du = _dot_nt(dq_ref[...], w_v[3])
        for k in range(3):
            du = du + _dot_nt(dc_ref[k], w_v[k])
        for k in range(2):
            du = du + _dot_nt(dkv_ref[k].astype(BF16), w_v[4 + k]) + _dot_nt(dgp_ref[k], w_v[6 + k])
        dx, dg = _rms_bwd_tile(h_ref[...], g_ref[...], du)
        dh_ref[...] = dres_ref[...] + dx
        _accumulate(dg_ref, step, dg)

    return _call(
        body, (dconv, dq, dkv, dgp, w_in, h, g, dh_res), grid=(t // TM,), name=name, comm=comm,
        in_specs=[_blk_row_spec(3, TM, D), _row_spec(TM, D), _blk_row_spec(2, TM, D), _blk_row_spec(2, TM, D), _ANY,
                  _row_spec(TM, D), _const_spec((1, D)), _row_spec(TM, D)],
        out_specs=[_row_spec(TM, D), _const_spec((1, D))],
        out_shape=[jax.ShapeDtypeStruct((t, D), F32), jax.ShapeDtypeStruct((1, D), F32)],
        scratch_shapes=[pltpu.VMEM((N_DEV, D, D), BF16), pltpu.SemaphoreType.DMA((1,))])


def _memkv_fwd(mem, g, w_ckv, name):
    m = mem.shape[0]

    def body(mem_ref, g_ref, w_ref, mn_ref, kv_ref):
        mn = _rms_fwd_tile(mem_ref[...], g_ref[...]).astype(BF16)
        mn_ref[...] = mn
        for j in range(N_DEV):
            kv_ref[j] = _dot(mn, w_ref[j]).astype(BF16)

    return pl.pallas_call(
        body, grid=(1,), name=name,
        in_specs=[_const_spec((m, D)), _const_spec((1, D)), _const_spec((N_DEV, D, X_DH))],
        out_specs=[_const_spec((m, D)), _const_spec((N_DEV, m, X_DH))],
        out_shape=[jax.ShapeDtypeStruct((m, D), BF16), jax.ShapeDtypeStruct((N_DEV, m, X_DH), BF16)],
        compiler_params=_cparams(),
    )(mem, g, w_ckv)


def _memkv_bwd(dkv, mem, g, w_ckv, name):
    m = mem.shape[0]

    def body(dkv_ref, mem_ref, g_ref, w_ref, dg_ref):
        dmn = jnp.zeros((m, D), F32)
        for j in range(N_DEV):
            dmn = dmn + _dot_nt(dkv_ref[j].astype(BF16), w_ref[j])
        _, dg = _rms_bwd_tile(mem_ref[...], g_ref[...], dmn)
        dg_ref[...] = dg

    return pl.pallas_call(
        body, grid=(1,), name=name,
        in_specs=[_const_spec((N_DEV, m, X_DH)), _const_spec((m, D)), _const_spec((1, D)),
                  _const_spec((N_DEV, D, X_DH))],
        out_specs=_const_spec((1, D)),
        out_shape=jax.ShapeDtypeStruct((1, D), F32),
        compiler_params=_cparams(),
    )(dkv, mem, g, w_ckv)


def _softmax_rows(s):
    e = jnp.exp(s - jnp.max(s, axis=-1, keepdims=True))
    return e / jnp.sum(e, axis=-1, keepdims=True)


def _cross_pairs(cross_hbm, wq_v, wo_v):
    return _square_pairs(cross_hbm, 0, wq_v) + _square_pairs(cross_hbm, 1, wo_v)


def _cross_fwd(h, g, kv, w_cross, name):
    t = h.shape[0]
    m = kv.shape[1]
    scale = X_DH ** -0.5

    def body(h_ref, g_ref, kv_ref, cross_hbm, hn_ref, qx_ref, o_ref, h3_ref, wq_v, wo_v, sems):
        _load_resident(pl.program_id(0), _cross_pairs(cross_hbm, wq_v, wo_v), sems)
        ht = h_ref[...]
        hn = _rms_fwd_tile(ht, g_ref[...]).astype(BF16)
        hn_ref[...] = hn
        qx = _dot(hn, wq_v[...]).astype(BF16)
        qx_ref[...] = qx
        for hd in range(X_H):
            lo, hi = hd * X_DH, (hd + 1) * X_DH
            p = _softmax_rows(_dot_nt(qx[:, lo:hi], kv_ref[hd]) * scale)
            o_ref[:, lo:hi] = _dot(p.astype(BF16), kv_ref[X_H + hd]).astype(BF16)
        h3_ref[...] = ht + _dot(o_ref[...], wo_v[...])

    return pl.pallas_call(
        body, grid=(t // TM,), name=name,
        in_specs=[_row_spec(TM, D), _const_spec((1, D)), _const_spec((N_DEV, m, X_DH)), _ANY],
        out_specs=[_row_spec(TM, D)] * 4,
        out_shape=[jax.ShapeDtypeStruct((t, D), BF16)] * 3 + [jax.ShapeDtypeStruct((t, D), F32)],
        scratch_shapes=[pltpu.VMEM((D, D), BF16)] * 2 + [pltpu.SemaphoreType.DMA((2 * N_DEV,))],
        compiler_params=_cparams(),
    )(h, g, kv, w_cross)


def _cross_bwd(dh3, h, g, qx, kv, w_cross, name, comm=None):
    t = h.shape[0]
    m = kv.shape[1]
    scale = X_DH ** -0.5

    def body(dh_ref, h_ref, g_ref, qx_ref, kv_ref, cross_hbm,
             dhb_ref, dqx_ref, dkv_ref, dh2_ref, dg_ref, wq_v, wo_v, sems):
        step = pl.program_id(0)
        _load_resident(step, _cross_pairs(cross_hbm, wq_v, wo_v), sems)

        @pl.when(step == 0)
        def _():
            dkv_ref[...] = jnp.zeros_like(dkv_ref)

        dht = dh_ref[...]
        dhb = dht.astype(BF16)
        dhb_ref[...] = dhb
        do = _dot_nt(dhb, wo_v[...]).astype(BF16)
        for hd in range(X_H):
            lo, hi = hd * X_DH, (hd + 1) * X_DH
            qh = qx_ref[:, lo:hi]
            kh = kv_ref[hd]
            p = _softmax_rows(_dot_nt(qh, kh) * scale)
            doh = do[:, lo:hi]
            dp = _dot_nt(doh, kv_ref[X_H + hd])
            ds = (p * (dp - jnp.sum(dp * p, axis=-1, keepdims=True)) * scale).astype(BF16)
            dqx_ref[:, lo:hi] = _dot(ds, kh).astype(BF16)
            dkv_ref[hd] += _dot_tn(ds, qh)
            dkv_ref[X_H + hd] += _dot_tn(p.astype(BF16), doh)
        dhn = _dot_nt(dqx_ref[...], wq_v[...])
        dx, dg = _rms_bwd_tile(h_ref[...], g_ref[...], dhn)
        dh2_ref[...] = dht + dx
        _accumulate(dg_ref, step, dg)

    return _call(
        body, (dh3, h, g, qx, kv, w_cross), grid=(t // TM,), name=name, comm=comm,
        in_specs=[_row_spec(TM, D), _row_spec(TM, D), _const_spec((1, D)), _row_spec(TM, D),
                  _const_spec((N_DEV, m, X_DH)), _ANY],
        out_specs=[_row_spec(TM, D), _row_spec(TM, D), _const_spec((N_DEV, m, X_DH)), _row_spec(TM, D),
                   _const_spec((1, D))],
        out_shape=[jax.ShapeDtypeStruct((t, D), BF16), jax.ShapeDtypeStruct((t, D), BF16),
                   jax.ShapeDtypeStruct((N_DEV, m, X_DH), F32), jax.ShapeDtypeStruct((t, D), F32),
                   jax.ShapeDtypeStruct((1, D), F32)],
        scratch_shapes=[pltpu.VMEM((D, D), BF16)] * 2 + [pltpu.SemaphoreType.DMA((2 * N_DEV,))])


def _adamw(w, parts, m, v, name, row_block=0, token=None):
    r, c = w.shape
    n = parts.shape[0]
    tr = _pick_tile(r, (256, 352, 128))
    off = row_block * (r // tr)

    def body(*refs):
        if token is None:
            _adamw_update(None, *refs)
        else:
            _adamw_update(refs[4], *refs[:4], *refs[5:])

    spec = _row_spec(tr, c)
    in_specs = [spec, pl.BlockSpec((n, tr, c), lambda i: (0, i + off, 0)), spec, spec]
    operands = (w, parts, m, v)
    if token is not None:
        in_specs.append(_const_spec(token.shape))
        operands += (token,)
    return pl.pallas_call(
        body, grid=(r // tr,), name=name, in_specs=in_specs, out_specs=[spec] * 4,
        out_shape=[jax.ShapeDtypeStruct((r, c), F32)] * 4,
        compiler_params=_cparams(),
    )(*operands)


def _adamw_update(tok_ref, w_ref, p_ref, m_ref, v_ref, g_ref, d_ref, nm_ref, nv_ref):
    gt = p_ref[0].astype(F32)
    for k in range(1, p_ref.shape[0]):
        gt = gt + p_ref[k].astype(F32)
    if tok_ref is not None:
        gt = gt + tok_ref[0:1, 0:1]
    _adamw_apply(gt, w_ref, m_ref, v_ref, g_ref, d_ref, nm_ref, nv_ref)


def _adamw_own(w, land, own, chip, m, v, name, row_block=0, token=None):
    r, c = w.shape
    tr = _pick_tile(r, (256, 352, 128))
    off = row_block * (r // tr)

    def body(chip_ref, w_ref, land_ref, own_ref, m_ref, v_ref, *rest):
        mine = own_ref[0].astype(F32)
        gt = jnp.where(chip_ref[0] == 0, mine, land_ref[0].astype(F32))
        for k in range(1, N_CHIP):
            gt = gt + jnp.where(chip_ref[0] == k, mine, land_ref[k].astype(F32))
        if token is not None:
            gt = gt + rest[0][0:1, 0:1]
        _adamw_apply(gt, w_ref, m_ref, v_ref, *rest[-4:])

    spec = pl.BlockSpec((tr, c), lambda i, chip_ref: (i, 0))
    in_specs = [spec, pl.BlockSpec((N_CHIP, tr, c), lambda i, chip_ref: (0, i + off, 0)),
                pl.BlockSpec((1, tr, c), lambda i, chip_ref: (chip_ref[0], i + off, 0)), spec, spec]
    operands = (chip, w, land, own, m, v)
    if token is not None:
        in_specs.append(pl.BlockSpec(token.shape, lambda i, chip_ref: (0, 0)))
        operands += (token,)
    return pl.pallas_call(
        body, name=name,
        grid_spec=pltpu.PrefetchScalarGridSpec(
            num_scalar_prefetch=1, grid=(r // tr,), in_specs=in_specs, out_specs=[spec] * 4),
        out_shape=[jax.ShapeDtypeStruct((r, c), F32)] * 4,
        compiler_params=_cparams(),
    )(*operands)


def _adamw_apply(gt, w_ref, m_ref, v_ref, g_ref, d_ref, nm_ref, nv_ref):
    g_ref[...] = gt
    nm = ADAM_B1 * m_ref[...] + (1.0 - ADAM_B1) * gt
    nv = ADAM_B2 * v_ref[...] + (1.0 - ADAM_B2) * jnp.square(gt)
    m_hat = nm / (1.0 - ADAM_B1 ** ADAM_STEP)
    v_hat = nv / (1.0 - ADAM_B2 ** ADAM_STEP)
    d_ref[...] = -ADAM_LR * (m_hat / (jnp.sqrt(v_hat) + ADAM_EPS) + ADAM_WD * w_ref[...])
    nm_ref[...] = nm
    nv_ref[...] = nv


def _mesh_pos():
    return lax.axis_index("x"), lax.axis_index("y"), lax.axis_index("c")


def _no_round(in_refs, out_refs, sems):
    pass


def _run_exchange(comm, name):
    c_in, c_out = len(comm.inputs), len(comm.out_shapes)

    def body(*refs):
        cins, couts, sems = refs[:c_in], refs[c_in:c_in + c_out], refs[c_in + c_out:]
        comm.start(cins, couts, sems)
        comm.middle(cins, couts, sems)
        comm.finish(cins, couts, sems)

    return list(pl.pallas_call(
        body, name=name, out_shape=list(comm.out_shapes),
        in_specs=[_ANY] * c_in, out_specs=[_ANY] * c_out, scratch_shapes=list(comm.sem_shapes),
    )(*comm.inputs))


def _gather_exchange(shards):
    n_arr = len(shards)

    def plan(x_refs, out_refs, sems):
        send_sems, recv_sems, local_sems = sems[:3]
        stage = sems[3:]
        x, y, c = _mesh_pos()
        me, sibling = (x, y, c), (x, y, 1 - c)
        xn, yn, diag = (1 - x, y), (x, 1 - y), (1 - x, 1 - y)

        def slot(a, px, py, pc, half=None):
            ref = out_refs[a].at[4 * px + 2 * py + pc]
            if half is None:
                return ref
            rows = shards[a].shape[0] // 2
            return ref.at[half * rows:(half + 1) * rows]

        def copy(a, k, block, to, half=None, src=None):
            dst = slot(a, *block, half)
            return pltpu.make_async_remote_copy(
                src_ref=dst if src is None else src, dst_ref=dst,
                send_sem=send_sems.at[a, k], recv_sem=recv_sems.at[a, k],
                device_id=to, device_id_type=pl.DeviceIdType.MESH)

        return types.SimpleNamespace(
            me=me, sibling=sibling, xn=xn, yn=yn, diag=diag, c=c, copy=copy,
            mine_in=[pltpu.make_async_copy(x_refs[a], stage[a], local_sems.at[a, 0]) for a in range(n_arr)],
            mine_out=[pltpu.make_async_copy(stage[a], slot(a, *me), local_sems.at[a, 1]) for a in range(n_arr)],
            first=[cp for a in range(n_arr) for cp in (
                copy(a, 0, me, sibling, src=x_refs[a]), copy(a, 1, me, (*xn, c), src=x_refs[a]),
                copy(a, 2, me, (*yn, c), src=x_refs[a]))],
            second=lambda a: (copy(a, 3, (*xn, c), (*yn, c), half=0), copy(a, 5, (*xn, c), sibling),
                              copy(a, 4, (*yn, c), (*xn, c), half=1), copy(a, 6, (*yn, c), sibling)),
            third=lambda a: (copy(a, 7, (*diag, c), sibling, half=0), copy(a, 8, (*diag, c), sibling, half=1)))

    def start(x_refs, out_refs, sems):
        p = plan(x_refs, out_refs, sems)
        for cp in p.first + p.mine_in:
            cp.start()
        for cp_in, cp_out in zip(p.mine_in, p.mine_out):
            cp_in.wait()
            cp_out.start()

    def middle(x_refs, out_refs, sems):
        p = plan(x_refs, out_refs, sems)
        for a in range(n_arr):
            to_yn, x_to_sib, to_xn, y_to_sib = p.second(a)
            p.copy(a, 1, (*p.xn, p.c), p.me).wait_recv()
            to_yn.start()
            x_to_sib.start()
            p.copy(a, 2, (*p.yn, p.c), p.me).wait_recv()
            to_xn.start()
            y_to_sib.start()

    def finish(x_refs, out_refs, sems):
        p = plan(x_refs, out_refs, sems)
        for a in range(n_arr):
            half0_to_sib, half1_to_sib = p.third(a)
            p.copy(a, 3, (*p.diag, p.c), p.me, half=0).wait_recv()
            half0_to_sib.start()
            p.copy(a, 4, (*p.diag, p.c), p.me, half=1).wait_recv()
            half1_to_sib.start()
        other = 1 - p.c
        for a in range(n_arr):
            p.copy(a, 0, p.sibling, p.me).wait_recv()
            p.copy(a, 5, (*p.xn, other), p.me).wait_recv()
            p.copy(a, 6, (*p.yn, other), p.me).wait_recv()
            p.copy(a, 7, (*p.diag, other), p.me, half=0).wait_recv()
            p.copy(a, 8, (*p.diag, other), p.me, half=1).wait_recv()
        for cp in p.first:
            cp.wait_send()
        for a in range(n_arr):
            for cp in p.second(a) + p.third(a):
                cp.wait_send()
        for cp in p.mine_out:
            cp.wait()

    return types.SimpleNamespace(
        inputs=list(shards), start=start, middle=middle, finish=finish,
        out_shapes=[jax.ShapeDtypeStruct((N_DEV,) + s.shape, s.dtype) for s in shards],
        sem_shapes=[pltpu.SemaphoreType.DMA((n_arr, 9)), pltpu.SemaphoreType.DMA((n_arr, 9)),
                    pltpu.SemaphoreType.DMA((n_arr, 2))] + [pltpu.VMEM(s.shape, s.dtype) for s in shards])


def _pair_exchange(grads):
    n_arr = len(grads)

    def plan(g_refs, land_refs, sems):
        send_sems, recv_sems = sems
        x, y, c = _mesh_pos()
        return [pltpu.make_async_remote_copy(
            src_ref=g_refs[a].at[2 * k + 1 - c], dst_ref=land_refs[a].at[k],
            send_sem=send_sems.at[a, k], recv_sem=recv_sems.at[a, k],
            device_id=(x, y, 1 - c), device_id_type=pl.DeviceIdType.MESH)
            for a in range(n_arr) for k in range(N_CHIP)]

    def start(g_refs, land_refs, sems):
        for cp in plan(g_refs, land_refs, sems):
            cp.start()

    def finish(g_refs, land_refs, sems):
        for cp in plan(g_refs, land_refs, sems):
            cp.wait()

    return types.SimpleNamespace(
        inputs=list(grads), start=start, middle=_no_round, finish=finish,
        out_shapes=[jax.ShapeDtypeStruct((N_CHIP,) + g.shape[1:], g.dtype) for g in grads],
        sem_shapes=[pltpu.SemaphoreType.DMA((n_arr, N_CHIP)), pltpu.SemaphoreType.DMA((n_arr, N_CHIP))])


def _chip_exchange(parts):
    n_arr = len(parts)

    def plan(p_refs, land_refs, sems):
        send_sems, recv_sems, local_sems = sems
        x, y, c = _mesh_pos()
        my_chip = 2 * x + y
        chips = [(1 - x, y), (x, 1 - y), (1 - x, 1 - y)]
        local = [pltpu.make_async_copy(p_refs[a].at[my_chip], land_refs[a].at[my_chip], local_sems.at[a])
                 for a in range(n_arr)]

        def copy(a, k, src_slot, dst_slot, px, py):
            return pltpu.make_async_remote_copy(
                src_ref=p_refs[a].at[src_slot], dst_ref=land_refs[a].at[dst_slot],
                send_sem=send_sems.at[a, k], recv_sem=recv_sems.at[a, k],
                device_id=(px, py, c), device_id_type=pl.DeviceIdType.MESH)

        sends = [copy(a, k, 2 * px + py, my_chip, px, py) for a in range(n_arr) for k, (px, py) in enumerate(chips)]
        arrivals = [copy(a, k, my_chip, 2 * px + py, px, py) for a in range(n_arr)
                    for k, (px, py) in enumerate(chips)]
        return local, sends, arrivals

    def start(p_refs, land_refs, sems):
        local, sends, _ = plan(p_refs, land_refs, sems)
        for cp in local + sends:
            cp.start()

    def finish(p_refs, land_refs, sems):
        local, sends, arrivals = plan(p_refs, land_refs, sems)
        for cp in arrivals:
            cp.wait_recv()
        for cp in sends:
            cp.wait_send()
        for cp in local:
            cp.wait()

    return types.SimpleNamespace(
        inputs=list(parts), start=start, middle=_no_round, finish=finish,
        out_shapes=[jax.ShapeDtypeStruct(p.shape, p.dtype) for p in parts],
        sem_shapes=[pltpu.SemaphoreType.DMA((n_arr, 3)), pltpu.SemaphoreType.DMA((n_arr, 3)),
                    pltpu.SemaphoreType.DMA((n_arr,))])


_HBM = pl.BlockSpec(memory_space=pltpu.HBM)
_SEM = pl.BlockSpec(memory_space=pltpu.SEMAPHORE)
_DATAFLOW = pltpu.SideEffectType.DATAFLOW_SIDE_EFFECTING


def _chip_copies(p_refs, land_refs, send_sems, recv_sems):
    x, y, c = _mesh_pos()
    my_chip = 2 * x + y
    chips = [(1 - x, y), (x, 1 - y), (1 - x, 1 - y)]
    return [pltpu.make_async_remote_copy(
        src_ref=p_refs[a].at[2 * px + py], dst_ref=land_refs[a].at[my_chip],
        send_sem=send_sems[3 * a + k], recv_sem=recv_sems[3 * a + k],
        device_id=(px, py, c), device_id_type=pl.DeviceIdType.MESH)
        for a in range(len(p_refs)) for k, (px, py) in enumerate(chips)]


def _chip_exchange_begin(parts, name):
    n_arr = len(parts)
    n_buf, n_copy = 2 * n_arr, 3 * n_arr
    lands = [lax.empty(p.shape, p.dtype) for p in parts]

    def body(*refs):
        p_refs, land_refs = refs[:n_arr], refs[n_arr:n_buf]
        send_sems, recv_sems, token = refs[n_buf:n_buf + n_copy], refs[n_buf + n_copy:n_buf + 2 * n_copy], refs[-1]
        for cp in _chip_copies(p_refs, land_refs, send_sems, recv_sems):
            cp.start()
        token[...] = jnp.zeros_like(token)

    bufs = list(parts) + list(lands)
    outs = pl.pallas_call(
        body, name=name,
        out_shape=(*[pltpu.SemaphoreType.DMA(())] * (2 * n_copy), *[pltpu.HBM(b.shape, b.dtype) for b in bufs],
                   jax.ShapeDtypeStruct((8, 128), F32)),
        in_specs=[_HBM] * n_buf,
        out_specs=(*[_SEM] * (2 * n_copy), *[_HBM] * n_buf, pl.BlockSpec(memory_space=pltpu.VMEM)),
        input_output_aliases={i: 2 * n_copy + i for i in range(n_buf)},
        compiler_params=pltpu.CompilerParams(has_side_effects=_DATAFLOW),
    )(*[pltpu.with_memory_space_constraint(b, pltpu.HBM) for b in bufs])
    sems = list(outs[:2 * n_copy])
    thru = list(outs[2 * n_copy:2 * n_copy + n_buf])
    return types.SimpleNamespace(send_sems=sems[:n_copy], recv_sems=sems[n_copy:], parts=thru[:n_arr],
                                 lands=thru[n_arr:], token=outs[-1])


def _chip_exchange_end(flight, after, name):
    send_sems, recv_sems, parts, lands = flight.send_sems, flight.recv_sems, flight.parts, flight.lands
    n_arr = len(parts)
    n_buf, n_copy = 2 * n_arr, 3 * n_arr

    def body(*refs):
        p_refs, land_refs = refs[:n_arr], refs[n_arr:n_buf]
        sems = refs[n_buf:n_buf + 2 * n_copy]
        for cp in _chip_copies(p_refs, land_refs, sems[:n_copy], sems[n_copy:]):
            cp.wait_send()
            cp.wait_recv()

    bufs = list(parts) + list(lands)
    outs = pl.pallas_call(
        body, name=name, out_shape=tuple(pltpu.HBM(b.shape, b.dtype) for b in bufs),
        in_specs=[_HBM] * n_buf + [_SEM] * (2 * n_copy) + [_ANY], out_specs=tuple([_HBM] * n_buf),
        input_output_aliases={i: i for i in range(n_buf)},
        compiler_params=pltpu.CompilerParams(has_side_effects=_DATAFLOW),
    )(*bufs, *send_sems, *recv_sems, after)
    return list(outs[:n_arr]), list(outs[n_arr:])


def _row_tile(r, cap=640):
    best = None
    for cand in range(16, min(r, cap) + 1, 16):
        if r % cand == 0:
            best = cand
    return best if best is not None else r


def _pair_sum(g, landed, core, name):
    _, r, c_dim = g.shape
    tr = _row_tile(r)

    def body(core_ref, mine_ref, theirs_ref, o_ref):
        o_ref[0] = (mine_ref[0].astype(F32) + theirs_ref[0].astype(F32)).astype(o_ref.dtype)

    return pl.pallas_call(
        body, name=name,
        grid_spec=pltpu.PrefetchScalarGridSpec(
            num_scalar_prefetch=1, grid=(N_CHIP, r // tr),
            in_specs=[pl.BlockSpec((1, tr, c_dim), lambda k, i, core_ref: (2 * k + core_ref[0], i, 0)),
                      pl.BlockSpec((1, tr, c_dim), lambda k, i, core_ref: (k, i, 0))],
            out_specs=pl.BlockSpec((1, tr, c_dim), lambda k, i, core_ref: (k, i, 0))),
        out_shape=jax.ShapeDtypeStruct((N_CHIP, r, c_dim), g.dtype),
        compiler_params=_cparams(2),
    )(core, g, landed)


def _sum_slots(parts, name):
    n, r, c_dim = parts.shape
    tr = _row_tile(r)

    def body(p_ref, o_ref):
        acc = p_ref[0].astype(F32)
        for k in range(1, n):
            acc = acc + p_ref[k].astype(F32)
        o_ref[...] = acc

    return pl.pallas_call(
        body, grid=(r // tr,), name=name,
        in_specs=[pl.BlockSpec((n, tr, c_dim), lambda i: (0, i, 0))],
        out_specs=_row_spec(tr, c_dim),
        out_shape=jax.ShapeDtypeStruct((r, c_dim), F32),
        compiler_params=_cparams(),
    )(parts)


GAINS = ("g_ffn1", "g_mix", "g_cross", "g_mem", "g_ffn2", "g_final")
SMALL = GAINS + ("b_gate", "conv_w")
SMALL_R = 16
LOSS_ROW = 11
WEIGHT_ORDER = ("g_ffn1", "w_ffn1_gu", "w_ffn1_down", "g_mix", "w_in", "b_gate", "conv_w", "w_conv_out",
                "w_attn_out", "w_o", "g_cross", "g_mem", "w_cq", "w_ckv", "w_co", "g_ffn2", "w_ffn2_gu",
                "w_ffn2_down", "g_final")
GU_NAMES = ("w_ffn1_gu", "w_ffn2_gu")


def _pack_small(vals, conv_rows):
    rows = [vals[n].reshape(1, D) for n in GAINS] + [vals["b_gate"].reshape(2, D), conv_rows.reshape(CONV_K, D)]
    used = len(GAINS) + 2 + CONV_K
    return jnp.concatenate(rows + [jnp.zeros((SMALL_R - used, D), F32)], axis=0)


def _unpack_small(buf):
    out = {n: buf[k] for k, n in enumerate(GAINS)}
    out["b_gate"] = buf[6:8].reshape(2 * D)
    out["conv_w"] = buf[8:8 + CONV_K]
    return out


def _exchange_shards(wts):
    out = {n: jnp.pad(wts[n].T.astype(BF16), ((0, FF_PAD - FF_BLK), (0, 0))) for n in GU_NAMES}
    for n in ("w_ckv", "w_in", "w_ffn1_down", "w_ffn2_down"):
        out[n] = wts[n].astype(BF16)
    out["mix"] = jnp.concatenate([wts[n].astype(BF16) for n in MIX_MATS], axis=0)
    out["cross"] = jnp.concatenate([wts[n].astype(BF16) for n in CROSS_MATS], axis=0)
    return out


def _reduce_group(grads, landed, core, names):
    return [_pair_sum(g, l, core, "grads_pair_sum_" + n) for g, l, n in zip(grads, landed, names)]


def _step(x, mem, target, sh, conv_pad, gains, b_gate, core):
    wg1, wd1, conv_all = _run_exchange(_gather_exchange([sh["w_ffn1_gu"], sh["w_ffn1_down"], conv_pad]), "gather_ffn1")
    conv_w = conv_all[:, :CONV_K, :].transpose(1, 0, 2).reshape(CONV_K, D)
    (n1, gate1, up1, act1, h1), (w_in,) = _ffn_fwd(
        x, gains["g_ffn1"], wg1, wd1, "ffn1_fwd", comm=_gather_exchange([sh["w_in"]]))
    (u, pcg, qkv, yc), (w_mix,) = _inproj_fwd(h1, gains["g_mix"], w_in, conv_w, "inproj_fwd",
                                              comm=_gather_exchange([sh["mix"]]))
    (ysb, ctot), (w_cross, w_ckv, wg2) = _sb_fwd(
        qkv, "sb_fwd", comm=_gather_exchange([sh["cross"], sh["w_ckv"], sh["w_ffn2_gu"]]))
    (a_mix, b_mix, merged, h2), (wd2,) = _mix_out_fwd(yc, ysb, pcg, b_gate, h1, w_mix, "mix_out_fwd",
                                                      comm=_gather_exchange([sh["w_ffn2_down"]]))
    mn, kv = _memkv_fwd(mem, gains["g_mem"], w_ckv, "memkv_fwd")
    hn, qx, o_x, h3 = _cross_fwd(h2, gains["g_cross"], kv, w_cross, "cross_fwd")
    (n4, gate2, up2, act2, dh4, loss, dg_final), _ = _ffn_fwd(h3, gains["g_ffn2"], wg2, wd2, "ffn2_fwd",
                                                              head=(gains["g_final"], target))

    gs = {"g_final": dg_final}
    (dgu2, dh4b, dh3, gs["g_ffn2"]), _ = _ffn_bwd(dh4, h3, gains["g_ffn2"], gate2, up2, wg2, wd2, "ffn2_bwd")
    grads_a = [_mm_tn_rows(dgu2, n4, FF_PAD, "dw_ffn2_gu"),
               _mm_tn_rows(act2, dh4b, FF_BLK, "dw_ffn2_down").reshape(N_DEV, DOWN_ROWS, D)]
    names_a = ["w_ffn2_gu", "w_ffn2_down"]
    (dh3b, dqx, dkv, dh2, gs["g_cross"]), landed_a = _cross_bwd(
        dh3, h2, gains["g_cross"], qx, kv, w_cross, "cross_bwd", comm=_pair_exchange(grads_a))
    sums_a = _reduce_group(grads_a, landed_a, core, names_a)
    cross_stack = _mm_tn_square(hn, dqx, "dw_cq", 0, len(CROSS_MATS))
    grads_b = [_mm_tn_cols(mn, dkv, "dw_ckv"), _mm_tn_square(o_x, dh3b, "dw_co", 1, len(CROSS_MATS), cross_stack)]
    names_b = ["w_ckv", "cross"]
    gs["g_mem"] = _memkv_bwd(dkv, mem, gains["g_mem"], w_ckv, "memkv_bwd")
    (dh2b, da_mix, db_mix, dgp, dyc, dysb, gs["b_gate"]), landed_b = _mix_out_bwd(
        dh2, a_mix, b_mix, pcg, b_gate, w_mix, "mix_out_bwd", comm=_pair_exchange(grads_b))
    sums_b = _reduce_group(grads_b, landed_b, core, names_b)
    mix_stack = _mm_tn_square(yc, da_mix, "dw_conv_out", 0, len(MIX_MATS))
    mix_stack = _mm_tn_square(ysb, db_mix, "dw_attn_out", 1, len(MIX_MATS), mix_stack)
    grads_c = [_mm_tn_square(merged, dh2b, "dw_o", 2, len(MIX_MATS), mix_stack)]
    flight_ab = _chip_exchange_begin(sums_a + sums_b, "grads_to_chips_early_begin")
    (dq, dkv_sb), landed_c = _sb_bwd(qkv, dysb, ctot, flight_ab.token, "sb_bwd", comm=_pair_exchange(grads_c))
    sums_c = _reduce_group(grads_c, landed_c, core, ["mix"])
    dconv, gs["conv_w"] = _conv_bwd(pcg, conv_w, dyc, "conv_bwd")
    w_in_stack = _mm_tn_cols(u, dconv, "dw_in_conv", 0, N_DEV)
    w_in_stack = _mm_tn_cols(u, dq[None], "dw_in_q", 3, N_DEV, w_in_stack)
    w_in_stack = _mm_tn_cols(u, dkv_sb, "dw_in_kv", 4, N_DEV, w_in_stack)
    grads_d = [_mm_tn_cols(u, dgp, "dw_in_gates", 6, N_DEV, w_in_stack)]
    (dh1, gs["g_mix"]), landed_d = _inproj_bwd(dconv, dq, dkv_sb, dgp, w_in, h1, gains["g_mix"], dh2, "inproj_bwd",
                                               comm=_pair_exchange(grads_d))
    sums_d = _reduce_group(grads_d, landed_d, core, ["w_in"])
    flight_d = _chip_exchange_begin(sums_c + sums_d, "grads_to_chips_w_in_begin")
    (dgu1, dh1b, dx, gs["g_ffn1"]), _ = _ffn_bwd(dh1, x, gains["g_ffn1"] + flight_d.token[0, 0], gate1, up1, wg1, wd1,
                                                 "ffn1_bwd")
    dw_gu1 = _mm_tn_rows(dgu1, n1, FF_PAD, "dw_ffn1_gu")
    dw_down1, landed_gu1 = _mm_tn_rows(act1, dh1b, FF_BLK, "dw_ffn1_down", comm=_pair_exchange([dw_gu1]))
    grads_e = [dw_gu1, dw_down1.reshape(N_DEV, DOWN_ROWS, D)]
    names_e = ["w_ffn1_gu", "w_ffn1_down"]
    landed_e = landed_gu1 + _run_exchange(_pair_exchange(grads_e[1:]), "grads_to_sibling_ffn1_down")
    flight_e = _chip_exchange_begin(_reduce_group(grads_e, landed_e, core, names_e), "grads_to_chips_ffn1_begin")
    flights = [(names_a + names_b, flight_ab), (["mix", "w_in"], flight_d), (names_e, flight_e)]
    return loss, dx, flights, gs


def kernel(x, mem, g_ffn1, w_ffn1_gu, w_ffn1_down, g_mix, w_in, b_gate, conv_w, w_conv_out, w_attn_out, w_o, g_cross, g_mem, w_cq, w_ckv, w_co, g_ffn2, w_ffn2_gu, w_ffn2_down, g_final, loss_target, m_g_ffn1, m_w_ffn1_gu, m_w_ffn1_down, m_g_mix, m_w_in, m_b_gate, m_conv_w, m_w_conv_out, m_w_attn_out, m_w_o, m_g_cross, m_g_mem, m_w_cq, m_w_ckv, m_w_co, m_g_ffn2, m_w_ffn2_gu, m_w_ffn2_down, m_g_final, v_g_ffn1, v_w_ffn1_gu, v_w_ffn1_down, v_g_mix, v_w_in, v_b_gate, v_conv_w, v_w_conv_out, v_w_attn_out, v_w_o, v_g_cross, v_g_mem, v_w_cq, v_w_ckv, v_w_co, v_g_ffn2, v_w_ffn2_gu, v_w_ffn2_down, v_g_final):
    args = locals()
    wts = {n: args[n] for n in WEIGHT_ORDER}
    mom1 = {n: args["m_" + n] for n in WEIGHT_ORDER}
    mom2 = {n: args["v_" + n] for n in WEIGHT_ORDER}
    cx, cy, cc = _mesh_pos()
    dev = 4 * cx + 2 * cy + cc
    conv_cols = D // N_DEV

    conv_pad = jnp.concatenate([conv_w, jnp.zeros((SMALL_R - CONV_K, conv_cols), F32)], axis=0)
    gains = {n: wts[n].reshape(1, D) for n in GAINS}
    loss8, dx, flights, gs = _step(x[0], mem[0], loss_target[0], _exchange_shards(wts), conv_pad, gains,
                                 b_gate.reshape(1, 2 * D), cc.reshape(1).astype(jnp.int32))

    grads, delta, new_m, new_v = {}, {}, {}, {}

    def operands(n, transposed):
        trio = (wts[n], mom1[n], mom2[n])
        return tuple(a.T for a in trio) if transposed else trio

    def record(n, res, transposed):
        grads[n], delta[n], new_m[n], new_v[n] = [r.T for r in res] if transposed else res

    early = [("w_ffn2_gu", "w_ffn2_gu", 0, True), ("w_ffn2_down", "w_ffn2_down", 0, False),
             ("w_ckv", "w_ckv", 0, False), ("w_in", "w_in", 0, False)]
    early += [(n, "mix", k, False) for k, n in enumerate(MIX_MATS)]
    early += [(n, "cross", k, False) for k, n in enumerate(CROSS_MATS)]
    chip = (2 * cx + cy).reshape(1).astype(jnp.int32)
    (names_early, flight_early), (names_w_in, flight_w_in), (last_names, flight_last) = flights
    token = flight_last.token
    own, land = {}, {}
    for names, flight, tag in ((names_early, flight_early, "early"), (names_w_in, flight_w_in, "w_in")):
        own_parts, landed = _chip_exchange_end(flight, token, "grads_to_chips_%s_end" % tag)
        own.update(zip(names, own_parts))
        land.update(zip(names, landed))
    for n, buf, row_block, transposed in early:
        w, m1, m2 = operands(n, transposed)
        record(n, _adamw_own(w, land[buf], own[buf], chip, m1, m2, "adamw_" + n, row_block, token), transposed)

    after = jnp.concatenate([new_v[n][:1, :1] for n, _, _, _ in early], axis=0)
    own_parts, landed = _chip_exchange_end(flight_last, after, "grads_to_chips_ffn1_end")
    for n, own_n, land_n, transposed in zip(last_names, own_parts, landed, (True, False)):
        w, m1, m2 = operands(n, transposed)
        record(n, _adamw_own(w, land_n, own_n, chip, m1, m2, "adamw_" + n), transposed)

    gs_rows = {n: gs[n] for n in GAINS + ("b_gate",)}
    small_mine = _pack_small(gs_rows, gs["conv_w"][:CONV_K]) + new_v[last_names[-1]][0, 0] * 0.0
    small_mine = small_mine.at[LOSS_ROW, 0].set(loss8[0, 0])
    small_all = _run_exchange(_gather_exchange([small_mine]), "gather_small_grads")[0]
    small_sum = _sum_slots(small_all, "small_grads_sum")
    loss = small_sum[LOSS_ROW, 0]
    grad_small = _unpack_small(small_sum)
    grad_small["conv_w"] = lax.dynamic_slice_in_dim(grad_small["conv_w"], dev * conv_cols, conv_cols, axis=1)
    grads.update(grad_small)

    def small_buf(vals):
        return _pack_small(vals, jnp.concatenate([vals["conv_w"], jnp.zeros((CONV_K, D - conv_cols), F32)], axis=1))

    _, d_s, m_s, v_s = _adamw(small_buf(wts), small_buf(grads)[None], small_buf(mom1), small_buf(mom2), "adamw_small")
    for res, buf in ((delta, d_s), (new_m, m_s), (new_v, v_s)):
        un = _unpack_small(buf)
        for n in GAINS + ("b_gate",):
            res[n] = un[n]
        res["conv_w"] = un["conv_w"][:, :conv_cols]

    return (loss, dx[None], *[grads[n] for n in WEIGHT_ORDER], *[delta[n] for n in WEIGHT_ORDER],
            *[new_m[n] for n in WEIGHT_ORDER], *[new_v[n] for n in WEIGHT_ORDER])
```

```python
import types

import jax
import jax.numpy as jnp
from jax import lax
from jax.experimental import pallas as pl
from jax.experimental.pallas import tpu as pltpu

F32 = jnp.float32
BF16 = jnp.bfloat16

D = 1024
DFF = 2816
SB_H = 8
SB_DH = 128
X_H = 4
X_DH = 256
CONV_K = 3
RMS_EPS = 1e-6
N_DEV = 8
N_CHIP = 4
SQ_ROWS = D // N_DEV

ADAM_LR = 0.001
ADAM_B1 = 0.9
ADAM_B2 = 0.999
ADAM_EPS = 1e-08
ADAM_WD = 0.01
ADAM_STEP = 10

TM = 256
TQ = 512
TK = 256
SB_HPS = 2
VMEM_LIMIT = 56 << 20

FF_BLK = DFF // 4
FF_PAD = 768
FF_SUB = 256
DOWN_ROWS = DFF // N_DEV

MIX_MATS = ("w_conv_out", "w_attn_out", "w_o")
CROSS_MATS = ("w_cq", "w_co")

_ANY = pl.BlockSpec(memory_space=pl.ANY)


def _cparams(n_axes=1):
    return pltpu.CompilerParams(
        dimension_semantics=("arbitrary",) * n_axes, vmem_limit_bytes=VMEM_LIMIT)


def _row_spec(tm, n):
    return pl.BlockSpec((tm, n), lambda i: (i, 0))


def _blk_row_spec(nb, tm, n):
    return pl.BlockSpec((nb, tm, n), lambda i: (0, i, 0))


def _const_spec(shape):
    zeros = (0,) * len(shape)
    return pl.BlockSpec(shape, lambda i: zeros)


def _dot(a, b):
    return jnp.dot(a, b, preferred_element_type=F32)


def _dot_nt(a, b):
    return lax.dot_general(a, b, (((1,), (1,)), ((), ())), preferred_element_type=F32)


def _dot_tn(a, b):
    return lax.dot_general(a, b, (((0,), (0,)), ((), ())), preferred_element_type=F32)


def _sigmoid(x):
    return 1.0 / (1.0 + jnp.exp(-x))


def _call(body, operands, *, grid, in_specs, out_specs, out_shape, scratch_shapes, name, comm=None):
    n_in, n_out, n_sc = len(in_specs), len(out_specs), len(scratch_shapes)
    if comm is None:
        outs = pl.pallas_call(
            body, grid=grid, name=name, in_specs=in_specs, out_specs=out_specs, out_shape=out_shape,
            scratch_shapes=scratch_shapes, compiler_params=_cparams(len(grid)))(*operands)
        return list(outs), []
    c_in, c_out, c_sem = len(comm.inputs), len(comm.out_shapes), len(comm.sem_shapes)

    def hosted(*refs):
        bounds = [0, n_in, c_in, n_out, c_out, n_sc, c_sem]
        parts, pos = [], 0
        for k in bounds[1:]:
            parts.append(refs[pos:pos + k])
            pos += k
        ins, cins, outs, couts, scr, sems = parts
        step, n_steps = pl.program_id(0), grid[0]
        for ax in range(1, len(grid)):
            step, n_steps = step * grid[ax] + pl.program_id(ax), n_steps * grid[ax]

        @pl.when(step == 0)
        def _():
            comm.start(cins, couts, sems)

        @pl.when(step == (2 * n_steps) // 3)
        def _():
            comm.middle(cins, couts, sems)

        body(*ins, *outs, *scr)

        @pl.when(step == n_steps - 1)
        def _():
            comm.finish(cins, couts, sems)

    res = pl.pallas_call(
        hosted, grid=grid, name=name, in_specs=list(in_specs) + [_ANY] * c_in,
        out_specs=list(out_specs) + [_ANY] * c_out, out_shape=list(out_shape) + list(comm.out_shapes),
        scratch_shapes=list(scratch_shapes) + list(comm.sem_shapes),
        compiler_params=_cparams(len(grid)))(*operands, *comm.inputs)
    return list(res[:n_out]), list(res[n_out:])


def _load_resident(step, pairs, sems):
    @pl.when(step == 0)
    def _():
        copies = [pltpu.make_async_copy(src, dst, sems.at[k]) for k, (src, dst) in enumerate(pairs)]
        for cp in copies:
            cp.start()
        for cp in copies:
            cp.wait()


def _square_pairs(buf_hbm, index, dst):
    off = index * SQ_ROWS
    return [(buf_hbm.at[d, off:off + SQ_ROWS, :], dst.at[d * SQ_ROWS:(d + 1) * SQ_ROWS, :]) for d in range(N_DEV)]


def _down_pairs(wd_hbm, dst):
    return [(wd_hbm.at[d], dst.at[d // 2, (d % 2) * DOWN_ROWS:(d % 2 + 1) * DOWN_ROWS, :]) for d in range(N_DEV)]


def _zero_down_pad(step, dst):
    @pl.when(step == 0)
    def _():
        dst[:, FF_BLK:, :] = jnp.zeros((4, FF_PAD - FF_BLK, D), BF16)


def _rms_fwd_tile(xt, g):
    r = lax.rsqrt(jnp.mean(xt * xt, axis=-1, keepdims=True) + RMS_EPS)
    return (xt * r) * g


def _rms_bwd_tile(xt, g, dn):
    r = lax.rsqrt(jnp.mean(xt * xt, axis=-1, keepdims=True) + RMS_EPS)
    xhat = xt * r
    dxhat = dn * g
    dx = r * (dxhat - xhat * jnp.mean(dxhat * xhat, axis=-1, keepdims=True))
    dg = jnp.sum(dn * xhat, axis=0, keepdims=True)
    return dx, dg


def _accumulate(ref, step, value):
    @pl.when(step == 0)
    def _():
        ref[...] = value

    @pl.when(step != 0)
    def _():
        ref[...] = ref[...] + value


def _ffn_fwd(x, g, wgu, wd, name, comm=None, head=None):
    t = x.shape[0]

    def body(x_ref, g_ref, wgu_hbm, wd_hbm, *refs):
        if head is None:
            n_ref, gate_ref, up_ref, act_ref, h_ref, wgu_v, wd_v, sems = refs
        else:
            gf_ref, t_ref, n_ref, gate_ref, up_ref, act_ref, dh_ref, loss_ref, dgf_ref, wgu_v, wd_v, sems = refs
        step = pl.program_id(0)
        _zero_down_pad(step, wd_v)
        _load_resident(step, [(wgu_hbm, wgu_v)] + _down_pairs(wd_hbm, wd_v), sems)
        xt = x_ref[...]
        n = _rms_fwd_tile(xt, g_ref[...]).astype(BF16)
        n_ref[...] = n
        acc = jnp.zeros((TM, D), F32)
        for j in range(4):
            for s in range(FF_PAD // FF_SUB):
                lo, hi = s * FF_SUB, (s + 1) * FF_SUB
                gt = _dot_nt(n, wgu_v[j, lo:hi, :])
                ut = _dot_nt(n, wgu_v[4 + j, lo:hi, :])
                gate_ref[j, :, lo:hi] = gt.astype(BF16)
                up_ref[j, :, lo:hi] = ut.astype(BF16)
                act_ref[j, :, lo:hi] = ((gt * _sigmoid(gt)) * ut).astype(BF16)
            acc = acc + _dot(act_ref[j], wd_v[j])
        ht = xt + 0.5 * acc
        if head is None:
            h_ref[...] = ht
        else:
            gain = gf_ref[...]
            diff = _rms_fwd_tile(ht, gain) - t_ref[...]
            part = 0.5 * jnp.sum(jnp.sum(diff * diff, axis=-1, keepdims=True) / D, axis=0, keepdims=True)
            dx, dg = _rms_bwd_tile(ht, gain, diff / D)
            dh_ref[...] = dx
            _accumulate(loss_ref, step, jnp.broadcast_to(part, (8, 128)))
            _accumulate(dgf_ref, step, dg)

    ff = jax.ShapeDtypeStruct((4, t, FF_PAD), BF16)
    operands, in_specs = (x, g, wgu, wd), [_row_spec(TM, D), _const_spec((1, D)), _ANY, _ANY]
    out_specs = [_row_spec(TM, D)] + [_blk_row_spec(4, TM, FF_PAD)] * 3 + [_row_spec(TM, D)]
    out_shape = [jax.ShapeDtypeStruct((t, D), BF16), ff, ff, ff, jax.ShapeDtypeStruct((t, D), F32)]
    if head is not None:
        operands += tuple(head)
        in_specs += [_const_spec((1, D)), _row_spec(TM, D)]
        out_specs += [_const_spec((8, 128)), _const_spec((1, D))]
        out_shape += [jax.ShapeDtypeStruct((8, 128), F32), jax.ShapeDtypeStruct((1, D), F32)]
    return _call(
        body, operands, grid=(t // TM,), name=name, comm=comm, in_specs=in_specs, out_specs=out_specs,
        out_shape=out_shape,
        scratch_shapes=[pltpu.VMEM((N_DEV, FF_PAD, D), BF16), pltpu.VMEM((4, FF_PAD, D), BF16),
                        pltpu.SemaphoreType.DMA((1 + N_DEV,))])


def _ffn_bwd(dh, xin, g, gate, up, wgu, wd, name, comm=None):
    t = dh.shape[0]

    def body(dh_ref, x_ref, g_ref, gate_ref, up_ref, wgu_hbm, wd_hbm,
             dgu_ref, dhb_ref, dx_ref, dg_ref, wgu_v, wd_v, sems):
        step = pl.program_id(0)
        _zero_down_pad(step, wd_v)
        _load_resident(step, [(wgu_hbm, wgu_v)] + _down_pairs(wd_hbm, wd_v), sems)
        dht = dh_ref[...]
        dhb = (0.5 * dht).astype(BF16)
        dhb_ref[...] = dhb
        dn = jnp.zeros((TM, D), F32)
        for j in range(4):
            for s in range(FF_PAD // FF_SUB):
                lo, hi = s * FF_SUB, (s + 1) * FF_SUB
                da = _dot_nt(dhb, wd_v[j, lo:hi, :])
                gt = gate_ref[j, :, lo:hi].astype(F32)
                ut = up_ref[j, :, lo:hi].astype(F32)
                sg = _sigmoid(gt)
                dgt = (da * ut * (sg * (1.0 + gt * (1.0 - sg)))).astype(BF16)
                dut = (da * (gt * sg)).astype(BF16)
                dgu_ref[j, :, lo:hi] = dgt
                dgu_ref[4 + j, :, lo:hi] = dut
            dn = dn + _dot(dgu_ref[j], wgu_v[j]) + _dot(dgu_ref[4 + j], wgu_v[4 + j])
        dx, dg = _rms_bwd_tile(x_ref[...], g_ref[...], dn)
        dx_ref[...] = dht + dx
        _accumulate(dg_ref, step, dg)

    return _call(
        body, (dh, xin, g, gate, up, wgu, wd), grid=(t // TM,), name=name, comm=comm,
        in_specs=[_row_spec(TM, D), _row_spec(TM, D), _const_spec((1, D)), _blk_row_spec(4, TM, FF_PAD),
                  _blk_row_spec(4, TM, FF_PAD), _ANY, _ANY],
        out_specs=[_blk_row_spec(N_DEV, TM, FF_PAD), _row_spec(TM, D), _row_spec(TM, D), _const_spec((1, D))],
        out_shape=[jax.ShapeDtypeStruct((N_DEV, t, FF_PAD), BF16), jax.ShapeDtypeStruct((t, D), BF16),
                   jax.ShapeDtypeStruct((t, D), F32), jax.ShapeDtypeStruct((1, D), F32)],
        scratch_shapes=[pltpu.VMEM((N_DEV, FF_PAD, D), BF16), pltpu.VMEM((4, FF_PAD, D), BF16),
                        pltpu.SemaphoreType.DMA((1 + N_DEV,))])


WIDE_TILES = (1024, 512, 256, 128)


def _pick_tile(n, options=(512, 256, 128)):
    for o in options:
        if n % o == 0:
            return o
    return n


def _into(stack, n_operands):
    if stack is None:
        return (), [], {}
    return (stack,), [_ANY], {n_operands: 0}


def _mm_tn_square(a, b, name, index, count, stack=None):
    k, m = a.shape
    _, n = b.shape
    tn = _pick_tile(n)
    extra, extra_specs, aliases = _into(stack, 2)

    def body(a_ref, b_ref, *rest):
        rest[-1][...] = _dot_tn(a_ref[...], b_ref[...]).astype(BF16).reshape(N_DEV, m // N_DEV, tn)

    return pl.pallas_call(
        body, grid=(n // tn,), name=name,
        in_specs=[pl.BlockSpec((k, m), lambda j: (0, 0)), pl.BlockSpec((k, tn), lambda j: (0, j))] + extra_specs,
        out_specs=pl.BlockSpec((N_DEV, m // N_DEV, tn), lambda j: (0, index, j)),
        out_shape=jax.ShapeDtypeStruct((N_DEV, count * (m // N_DEV), n), BF16),
        input_output_aliases=aliases, compiler_params=_cparams(1),
    )(a, b, *extra)


def _mm_tn_cols(a, b, name, first=0, count=None, stack=None):
    k, m = a.shape
    nb, _, n = b.shape
    tm = _pick_tile(m, WIDE_TILES)
    extra, extra_specs, aliases = _into(stack, 2)

    def body(a_ref, b_ref, *rest):
        rest[-1][0] = _dot_tn(a_ref[...].astype(BF16), b_ref[0].astype(BF16)).astype(BF16)

    return pl.pallas_call(
        body, grid=(nb, m // tm), name=name,
        in_specs=[pl.BlockSpec((k, tm), lambda j, i: (0, i)), pl.BlockSpec((1, k, n), lambda j, i: (j, 0, 0))]
                 + extra_specs,
        out_specs=pl.BlockSpec((1, tm, n), lambda j, i: (j + first, i, 0)),
        out_shape=jax.ShapeDtypeStruct((nb if count is None else count, m, n), BF16),
        input_output_aliases=aliases, compiler_params=_cparams(2),
    )(a, b, *extra)


def _mm_tn_rows(a, b, keep, name, comm=None):
    nb, k, m = a.shape
    _, n = b.shape
    tn = _pick_tile(n, WIDE_TILES)

    def body(a_ref, b_ref, o_ref):
        o_ref[0] = _dot_tn(a_ref[0], b_ref[...])[:keep].astype(BF16)

    (out,), couts = _call(
        body, (a, b), grid=(nb, n // tn), name=name, comm=comm,
        in_specs=[pl.BlockSpec((1, k, m), lambda j, i: (j, 0, 0)), pl.BlockSpec((k, tn), lambda j, i: (0, i))],
        out_specs=[pl.BlockSpec((1, keep, tn), lambda j, i: (j, 0, i))],
        out_shape=[jax.ShapeDtypeStruct((nb, keep, n), BF16)], scratch_shapes=[])
    return out if comm is None else (out, couts)


PCG_W = 5 * D
QKV_W = 3 * D
PROJ_SUB = 512


def _inproj_fwd(h, g, w_in, conv_w, name, comm=None):
    t = h.shape[0]

    def body(h_ref, g_ref, w_hbm, cw_ref, u_ref, pcg_ref, qkv_ref, yc_ref, w_v, tail_v, sems):
        step = pl.program_id(0)
        _load_resident(step, [(w_hbm, w_v)], sems)

        @pl.when(step == 0)
        def _():
            tail_v[...] = jnp.zeros_like(tail_v)

        u = _rms_fwd_tile(h_ref[...], g_ref[...]).astype(BF16)
        u_ref[...] = u
        for blk in range(N_DEV):
            for s in range(D // PROJ_SUB):
                lo, hi = s * PROJ_SUB, (s + 1) * PROJ_SUB
                p = _dot(u, w_v[blk, :, lo:hi])
                if blk < 3:
                    pcg_ref[:, blk * D + lo:blk * D + hi] = p
                elif blk < 6:
                    qkv_ref[:, (blk - 3) * D + lo:(blk - 3) * D + hi] = p.astype(BF16)
                else:
                    pcg_ref[:, (blk - 3) * D + lo:(blk - 3) * D + hi] = p
        xc = pcg_ref[:, D:2 * D] * pcg_ref[:, 2 * D:3 * D]
        ext = jnp.concatenate([tail_v[...], xc], axis=0)
        conv = (cw_ref[0:1, :] * pltpu.roll(ext, 2, 0)[8:] + cw_ref[1:2, :] * pltpu.roll(ext, 1, 0)[8:]
                + cw_ref[2:3, :] * xc)
        yc_ref[...] = (pcg_ref[:, 0:D] * conv).astype(BF16)
        tail_v[...] = xc[TM - 8:]

    return _call(
        body, (h, g, w_in, conv_w), grid=(t // TM,), name=name, comm=comm,
        in_specs=[_row_spec(TM, D), _const_spec((1, D)), _ANY, _const_spec((CONV_K, D))],
        out_specs=[_row_spec(TM, D), _row_spec(TM, PCG_W), _row_spec(TM, QKV_W), _row_spec(TM, D)],
        out_shape=[jax.ShapeDtypeStruct((t, D), BF16), jax.ShapeDtypeStruct((t, PCG_W), F32),
                   jax.ShapeDtypeStruct((t, QKV_W), BF16), jax.ShapeDtypeStruct((t, D), BF16)],
        scratch_shapes=[pltpu.VMEM((N_DEV, D, D), BF16), pltpu.VMEM((8, D), F32), pltpu.SemaphoreType.DMA((1,))])


def _tri2(cond):
    rr = lax.broadcasted_iota(jnp.int32, (2 * TK, TK), 0) & (TK - 1)
    cc = lax.broadcasted_iota(jnp.int32, (2 * TK, TK), 1)
    return cond(rr, cc).astype(BF16)


def _causal(shift, row0=0):
    rr = lax.broadcasted_iota(jnp.int32, (TQ - row0, TK), 0) + row0
    cc = lax.broadcasted_iota(jnp.int32, (TQ - row0, TK), 1)
    return cc + shift < rr


def _cumdot(v, tri2):
    hi = v.astype(BF16)
    lo = (v - hi.astype(F32)).astype(BF16)
    return _dot(jnp.concatenate([hi, lo], axis=1), tri2)


def _log_1m_beta(z):
    return -(jnp.maximum(z, 0.0) + jnp.log(1.0 + jnp.exp(-jnp.abs(z))))


def _sb_specs(t):
    g = SB_H // SB_HPS
    w = SB_HPS * SB_DH
    q_spec = pl.BlockSpec((TQ, w), lambda h, i: (i, h))
    k_spec = pl.BlockSpec((t, w), lambda h, i: (0, g + h))
    v_spec = pl.BlockSpec((t, w), lambda h, i: (0, 2 * g + h))
    ct_spec = pl.BlockSpec((SB_HPS, TQ, 1), lambda h, i: (h, i, 0))
    return g, w, q_spec, k_spec, v_spec, ct_spec


def _sb_fwd(qkv, name, comm=None):
    t = qkv.shape[0]
    scale = SB_DH ** -0.5
    g, w, q_spec, k_spec, v_spec, ct_spec = _sb_specs(t)

    def body(q_ref, k_ref, v_ref, y_ref, ct_ref):
        i = pl.program_id(1)
        later = _tri2(lambda j, s: j > s)
        n_diag = TQ // TK

        def block(j, carry, shift):
            off = pl.multiple_of(j * TK, TK)
            zs, ms = [], []
            for hd in range(SB_HPS):
                cols = slice(hd * SB_DH, (hd + 1) * SB_DH)
                z = _dot_nt(q_ref[:, cols], k_ref[pl.ds(off, TK), cols]) * scale
                m = _log_1m_beta(z)
                if shift is not None:
                    m = jnp.where(_causal(shift), m, 0.0)
                zs.append(z)
                ms.append(m)
            after = _cumdot(jnp.concatenate(ms, axis=0), later)
            out = []
            for hd in range(SB_HPS):
                acc, c_sum = carry[hd]
                cols = slice(hd * SB_DH, (hd + 1) * SB_DH)
                a = jnp.exp((ms[hd] + zs[hd]) + (c_sum + after[hd * TQ:(hd + 1) * TQ]))
                if shift is not None:
                    a = jnp.where(_causal(shift), a, 0.0)
                out.append((acc + _dot(a.astype(BF16), v_ref[pl.ds(off, TK), cols]),
                            c_sum + jnp.sum(ms[hd], axis=1, keepdims=True)))
            return tuple(out)

        carry = tuple((jnp.zeros((TQ, SB_DH), F32), jnp.zeros((TQ, 1), F32)) for _ in range(SB_HPS))
        for d in reversed(range(n_diag)):
            carry = block(i * n_diag + d, carry, d * TK)
        carry = lax.fori_loop(0, i * n_diag, lambda jj, c: block(i * n_diag - 1 - jj, c, None), carry)
        for hd in range(SB_HPS):
            y_ref[:, hd * SB_DH:(hd + 1) * SB_DH] = carry[hd][0].astype(BF16)
            ct_ref[hd] = carry[hd][1]

    return _call(
        body, (qkv, qkv, qkv), grid=(g, t // TQ), name=name, comm=comm,
        in_specs=[q_spec, k_spec, v_spec],
        out_specs=[q_spec, ct_spec],
        out_shape=[jax.ShapeDtypeStruct((t, D), BF16), jax.ShapeDtypeStruct((SB_H, t, 1), F32)],
        scratch_shapes=[])


def _sb_bwd(qkv, dy, ctot, after, name, comm=None):
    t = qkv.shape[0]
    scale = SB_DH ** -0.5
    g, w, q_spec, k_spec, v_spec, ct_spec = _sb_specs(t)
    acc_spec = pl.BlockSpec((2, t, w), lambda h, i: (0, 0, h))

    def body(q_ref, k_ref, v_ref, dy_ref, ct_ref, after_ref, dq_ref, dkv_ref):
        i = pl.program_id(1)

        @pl.when(i == 0)
        def _():
            dkv_ref[...] = jnp.zeros_like(dkv_ref)

        upto = _tri2(lambda j, s: j <= s)
        n_diag = TQ // TK

        def block(j, carry, shift):
            off = pl.multiple_of(j * TK, TK)
            r0 = 0 if shift is None else shift
            nr = TQ - r0
            causal = None if shift is None else _causal(shift, r0)

            def grow(old, delta):
                return old + delta if r0 == 0 else jnp.concatenate([old[:r0], old[r0:] + delta], axis=0)

            zs, ms = [], []
            for hd in range(SB_HPS):
                cols = slice(hd * SB_DH, (hd + 1) * SB_DH)
                z = _dot_nt(q_ref[r0:, cols], k_ref[pl.ds(off, TK), cols]) * scale
                m = _log_1m_beta(z)
                if causal is not None:
                    m = jnp.where(causal, m, 0.0)
                zs.append(z)
                ms.append(m)
            m_upto = _cumdot(jnp.concatenate(ms, axis=0), upto)
            ls, a_s, es = [], [], []
            for hd in range(SB_HPS):
                cols = slice(hd * SB_DH, (hd + 1) * SB_DH)
                l = ms[hd] + zs[hd]
                a = jnp.exp(l + ((ct_ref[hd, r0:] - carry[hd][1][r0:]) - m_upto[hd * nr:(hd + 1) * nr]))
                if causal is not None:
                    a = jnp.where(causal, a, 0.0)
                ls.append(l)
                a_s.append(a)
                es.append(_dot_nt(dy_ref[r0:, cols], v_ref[pl.ds(off, TK), cols]) * a)
            e_upto = _dot(jnp.concatenate(es, axis=0).astype(BF16), upto[:TK])
            out = []
            for hd in range(SB_HPS):
                dq, p_sum, e_sum = carry[hd]
                cols = slice(hd * SB_DH, (hd + 1) * SB_DH)
                e = es[hd]
                dz = e - jnp.exp(ls[hd]) * (e_sum[r0:] + e_upto[hd * nr:(hd + 1) * nr])
                if causal is not None:
                    dz = jnp.where(causal, dz, 0.0)
                dzs = (dz * scale).astype(BF16)
                dkv_ref[0, pl.ds(off, TK), cols] += _dot_tn(dzs, q_ref[r0:, cols])
                dkv_ref[1, pl.ds(off, TK), cols] += _dot_tn(a_s[hd].astype(BF16), dy_ref[r0:, cols])
                out.append((grow(dq, _dot(dzs, k_ref[pl.ds(off, TK), cols])),
                            grow(p_sum, jnp.sum(ms[hd], axis=1, keepdims=True)),
                            grow(e_sum, jnp.sum(e, axis=1, keepdims=True))))
            return tuple(out)

        zero = jnp.zeros((TQ, 1), F32)
        init = tuple((jnp.zeros((TQ, SB_DH), F32), zero, zero) for _ in range(SB_HPS))
        carry = lax.fori_loop(0, i * n_diag, lambda j, c: block(j, c, None), init)
        for d in range(n_diag):
            carry = block(i * n_diag + d, carry, d * TK)
        for hd in range(SB_HPS):
            dq_ref[:, hd * SB_DH:(hd + 1) * SB_DH] = carry[hd][0].astype(BF16)

    return _call(
        body, (qkv, qkv, qkv, dy, ctot, after), grid=(g, t // TQ), name=name, comm=comm,
        in_specs=[q_spec, k_spec, v_spec, q_spec, ct_spec, pl.BlockSpec(after.shape, lambda h, i: (0, 0))],
        out_specs=[q_spec, acc_spec],
        out_shape=[jax.ShapeDtypeStruct((t, D), BF16), jax.ShapeDtypeStruct((2, t, D), F32)],
        scratch_shapes=[])


def _gate_specs():
    return [pl.BlockSpec((TM, D), lambda i: (i, 3)), pl.BlockSpec((TM, D), lambda i: (i, 4))]


def _mix_pairs(mix_hbm, dsts):
    pairs = []
    for index, dst in enumerate(dsts):
        pairs += _square_pairs(mix_hbm, index, dst)
    return pairs


def _mix_out_fwd(yc, ysb, pcg, b_gate, h, w_mix, name, comm=None):
    t = h.shape[0]

    def body(yc_ref, ysb_ref, gc_ref, gs_ref, b_ref, h_ref, mix_hbm,
             a_ref, b_out_ref, mg_ref, h2_ref, wc_v, wa_v, wo_v, sems):
        _load_resident(pl.program_id(0), _mix_pairs(mix_hbm, (wc_v, wa_v, wo_v)), sems)
        a = _dot(yc_ref[...], wc_v[...])
        b = _dot(ysb_ref[...], wa_v[...])
        merged = (_sigmoid(gc_ref[...] + b_ref[:, :D]) * a + _sigmoid(gs_ref[...] + b_ref[:, D:]) * b).astype(BF16)
        a_ref[...] = a
        b_out_ref[...] = b
        mg_ref[...] = merged
        h2_ref[...] = h_ref[...] + _dot(merged, wo_v[...])

    return _call(
        body, (yc, ysb, pcg, pcg, b_gate, h, w_mix), grid=(t // TM,), name=name, comm=comm,
        in_specs=[_row_spec(TM, D), _row_spec(TM, D)] + _gate_specs()
                 + [_const_spec((1, 2 * D)), _row_spec(TM, D), _ANY],
        out_specs=[_row_spec(TM, D)] * 4,
        out_shape=[jax.ShapeDtypeStruct((t, D), F32), jax.ShapeDtypeStruct((t, D), F32),
                   jax.ShapeDtypeStruct((t, D), BF16), jax.ShapeDtypeStruct((t, D), F32)],
        scratch_shapes=[pltpu.VMEM((D, D), BF16)] * 3 + [pltpu.SemaphoreType.DMA((3 * N_DEV,))])


def _mix_out_bwd(dh2, a, b, pcg, b_gate, conv_w, w_mix, name, comm=None):
    t = dh2.shape[0]
    n_tile = t // TM
    per8 = TM // 8

    def rows(n):
        return pl.BlockSpec((TM, n), lambda i: (n_tile - 1 - i, 0))

    def cols(block):
        return pl.BlockSpec((TM, D), lambda i: (n_tile - 1 - i, block))

    def before(block):
        return pl.BlockSpec((8, D), lambda i: (jnp.maximum((n_tile - 1 - i) * per8 - 1, 0), block))

    def body(dh_ref, a_ref, b_ref, gc_ref, gs_ref, cb_ref, cc_ref, cx_ref, ccp_ref, cxp_ref, bias_ref, cw_ref, mix_hbm,
             dhb_ref, da_ref, db_ref, dgp_ref, dc_ref, dysb_ref, dbias_ref, dcw_ref, wc_v, wa_v, wo_v, head_v, sems):
        step = pl.program_id(0)
        _load_resident(step, _mix_pairs(mix_hbm, (wc_v, wa_v, wo_v)), sems)

        @pl.when(step == 0)
        def _():
            head_v[...] = jnp.zeros_like(head_v)
            dcw_ref[...] = jnp.zeros_like(dcw_ref)

        dhb = dh_ref[...].astype(BF16)
        dhb_ref[...] = dhb
        dm = _dot_nt(dhb, wo_v[...])
        gc = _sigmoid(gc_ref[...] + bias_ref[:, :D])
        gs = _sigmoid(gs_ref[...] + bias_ref[:, D:])
        da = (dm * gc).astype(BF16)
        db = (dm * gs).astype(BF16)
        da_ref[...] = da
        db_ref[...] = db
        dgc = dm * a_ref[...] * (gc * (1.0 - gc))
        dgs = dm * b_ref[...] * (gs * (1.0 - gs))
        dgp_ref[0] = dgc.astype(BF16)
        dgp_ref[1] = dgs.astype(BF16)
        _accumulate(dbias_ref.at[:, :D], step, jnp.sum(dgc, axis=0, keepdims=True))
        _accumulate(dbias_ref.at[:, D:], step, jnp.sum(dgs, axis=0, keepdims=True))
        dysb_ref[...] = _dot_nt(db, wa_v[...]).astype(BF16)
        dyc = _dot_nt(da, wc_v[...])
        cc, cx = cc_ref[...], cx_ref[...]
        xc = cc * cx
        xc_before = jnp.where(step == n_tile - 1, 0.0, ccp_ref[...] * cxp_ref[...])
        ext = jnp.concatenate([xc_before, xc], axis=0)
        x1 = pltpu.roll(ext, 1, 0)[8:]
        x2 = pltpu.roll(ext, 2, 0)[8:]
        w0, w1, w2 = cw_ref[0:1, :], cw_ref[1:2, :], cw_ref[2:3, :]
        dc_ref[0] = (dyc * (w0 * x2 + w1 * x1 + w2 * xc)).astype(BF16)
        dconv = dyc * cb_ref[...]
        dcw_ref[0:1, :] += jnp.sum(dconv * x2, axis=0, keepdims=True)
        dcw_ref[1:2, :] += jnp.sum(dconv * x1, axis=0, keepdims=True)
        dcw_ref[2:3, :] += jnp.sum(dconv * xc, axis=0, keepdims=True)
        after = jnp.concatenate([dconv, head_v[...]], axis=0)
        dxc = w2 * dconv + w1 * pltpu.roll(after, TM + 7, 0)[:TM] + w0 * pltpu.roll(after, TM + 6, 0)[:TM]
        dc_ref[1] = (dxc * cx).astype(BF16)
        dc_ref[2] = (dxc * cc).astype(BF16)
        head_v[...] = dconv[:8]

    return _call(
        body, (dh2, a, b, pcg, pcg, pcg, pcg, pcg, pcg, pcg, b_gate, conv_w, w_mix), grid=(n_tile,), name=name,
        comm=comm,
        in_specs=[rows(D)] * 3 + [cols(3), cols(4), cols(0), cols(1), cols(2), before(1), before(2),
                                  _const_spec((1, 2 * D)), _const_spec((CONV_K, D)), _ANY],
        out_specs=[rows(D)] * 3 + [pl.BlockSpec((2, TM, D), lambda i: (0, n_tile - 1 - i, 0)),
                                   pl.BlockSpec((3, TM, D), lambda i: (0, n_tile - 1 - i, 0)), rows(D),
                                   _const_spec((1, 2 * D)), _const_spec((8, D))],
        out_shape=[jax.ShapeDtypeStruct((t, D), BF16)] * 3
                  + [jax.ShapeDtypeStruct((2, t, D), BF16), jax.ShapeDtypeStruct((3, t, D), BF16),
                     jax.ShapeDtypeStruct((t, D), BF16), jax.ShapeDtypeStruct((1, 2 * D), F32),
                     jax.ShapeDtypeStruct((8, D), F32)],
        scratch_shapes=[pltpu.VMEM((D, D), BF16)] * 3 + [pltpu.VMEM((8, D), F32),
                                                         pltpu.SemaphoreType.DMA((3 * N_DEV,))])


def _inproj_bwd(dconv, dq, dkv, dgp, w_in, h, g, dh_res, name, comm=None):
    t = h.shape[0]

    def body(dc_ref, dq_ref, dkv_ref, dgp_ref, w_hbm, h_ref, g_ref, dres_ref, dh_ref, dg_ref, w_v, sems):
        step = pl.program_id(0)
        _load_resident(step, [(w_hbm, w_v)], sems)
        du = _dot_nt(dq_ref[...], w_v[3])
        for k in range(3):
            du = du + _dot_nt(dc_ref[k], w_v[k])
        for k in range(2):
            du = du + _dot_nt(dkv_ref[k].astype(BF16), w_v[4 + k]) + _dot_nt(dgp_ref[k], w_v[6 + k])
        dx, dg = _rms_bwd_tile(h_ref[...], g_ref[...], du)
        dh_ref[...] = dres_ref[...] + dx
        _accumulate(dg_ref, step, dg)

    return _call(
        body, (dconv, dq, dkv, dgp, w_in, h, g, dh_res), grid=(t // TM,), name=name, comm=comm,
        in_specs=[_blk_row_spec(3, TM, D), _row_spec(TM, D), _blk_row_spec(2, TM, D), _blk_row_spec(2, TM, D), _ANY,
                  _row_spec(TM, D), _const_spec((1, D)), _row_spec(TM, D)],
        out_specs=[_row_spec(TM, D), _const_spec((1, D))],
        out_shape=[jax.ShapeDtypeStruct((t, D), F32), jax.ShapeDtypeStruct((1, D), F32)],
        scratch_shapes=[pltpu.VMEM((N_DEV, D, D), BF16), pltpu.SemaphoreType.DMA((1,))])


def _memkv_fwd(mem, g, w_ckv, name):
    m = mem.shape[0]

    def body(mem_ref, g_ref, w_ref, mn_ref, kv_ref):
        mn = _rms_fwd_tile(mem_ref[...], g_ref[...]).astype(BF16)
        mn_ref[...] = mn
        for j in range(N_DEV):
            kv_ref[j] = _dot(mn, w_ref[j]).astype(BF16)

    return pl.pallas_call(
        body, grid=(1,), name=name,
        in_specs=[_const_spec((m, D)), _const_spec((1, D)), _const_spec((N_DEV, D, X_DH))],
        out_specs=[_const_spec((m, D)), _const_spec((N_DEV, m, X_DH))],
        out_shape=[jax.ShapeDtypeStruct((m, D), BF16), jax.ShapeDtypeStruct((N_DEV, m, X_DH), BF16)],
        compiler_params=_cparams(),
    )(mem, g, w_ckv)


def _memkv_bwd(dkv, mem, g, w_ckv, name):
    m = mem.shape[0]

    def body(dkv_ref, mem_ref, g_ref, w_ref, dg_ref):
        dmn = jnp.zeros((m, D), F32)
        for j in range(N_DEV):
            dmn = dmn + _dot_nt(dkv_ref[j].astype(BF16), w_ref[j])
        _, dg = _rms_bwd_tile(mem_ref[...], g_ref[...], dmn)
        dg_ref[...] = dg

    return pl.pallas_call(
        body, grid=(1,), name=name,
        in_specs=[_const_spec((N_DEV, m, X_DH)), _const_spec((m, D)), _const_spec((1, D)),
                  _const_spec((N_DEV, D, X_DH))],
        out_specs=_const_spec((1, D)),
        out_shape=jax.ShapeDtypeStruct((1, D), F32),
        compiler_params=_cparams(),
    )(dkv, mem, g, w_ckv)


def _softmax_rows(s):
    e = jnp.exp(s - jnp.max(s, axis=-1, keepdims=True))
    return e / jnp.sum(e, axis=-1, keepdims=True)


def _cross_pairs(cross_hbm, wq_v, wo_v):
    return _square_pairs(cross_hbm, 0, wq_v) + _square_pairs(cross_hbm, 1, wo_v)


def _cross_fwd(h, g, kv, w_cross, name):
    t = h.shape[0]
    m = kv.shape[1]
    scale = X_DH ** -0.5

    def body(h_ref, g_ref, kv_ref, cross_hbm, hn_ref, qx_ref, o_ref, h3_ref, wq_v, wo_v, sems):
        _load_resident(pl.program_id(0), _cross_pairs(cross_hbm, wq_v, wo_v), sems)
        ht = h_ref[...]
        hn = _rms_fwd_tile(ht, g_ref[...]).astype(BF16)
        hn_ref[...] = hn
        qx = _dot(hn, wq_v[...]).astype(BF16)
        qx_ref[...] = qx
        for hd in range(X_H):
            lo, hi = hd * X_DH, (hd + 1) * X_DH
            p = _softmax_rows(_dot_nt(qx[:, lo:hi], kv_ref[hd]) * scale)
            o_ref[:, lo:hi] = _dot(p.astype(BF16), kv_ref[X_H + hd]).astype(BF16)
        h3_ref[...] = ht + _dot(o_ref[...], wo_v[...])

    return pl.pallas_call(
        body, grid=(t // TM,), name=name,
        in_specs=[_row_spec(TM, D), _const_spec((1, D)), _const_spec((N_DEV, m, X_DH)), _ANY],
        out_specs=[_row_spec(TM, D)] * 4,
        out_shape=[jax.ShapeDtypeStruct((t, D), BF16)] * 3 + [jax.ShapeDtypeStruct((t, D), F32)],
        scratch_shapes=[pltpu.VMEM((D, D), BF16)] * 2 + [pltpu.SemaphoreType.DMA((2 * N_DEV,))],
        compiler_params=_cparams(),
    )(h, g, kv, w_cross)


def _cross_bwd(dh3, h, g, qx, kv, w_cross, name, comm=None):
    t = h.shape[0]
    m = kv.shape[1]
    scale = X_DH ** -0.5

    def body(dh_ref, h_ref, g_ref, qx_ref, kv_ref, cross_hbm,
             dhb_ref, dqx_ref, dkv_ref, dh2_ref, dg_ref, wq_v, wo_v, sems):
        step = pl.program_id(0)
        _load_resident(step, _cross_pairs(cross_hbm, wq_v, wo_v), sems)

        @pl.when(step == 0)
        def _():
            dkv_ref[...] = jnp.zeros_like(dkv_ref)

        dht = dh_ref[...]
        dhb = dht.astype(BF16)
        dhb_ref[...] = dhb
        do = _dot_nt(dhb, wo_v[...]).astype(BF16)
        for hd in range(X_H):
            lo, hi = hd * X_DH, (hd + 1) * X_DH
            qh = qx_ref[:, lo:hi]
            kh = kv_ref[hd]
            p = _softmax_rows(_dot_nt(qh, kh) * scale)
            doh = do[:, lo:hi]
            dp = _dot_nt(doh, kv_ref[X_H + hd])
            ds = (p * (dp - jnp.sum(dp * p, axis=-1, keepdims=True)) * scale).astype(BF16)
            dqx_ref[:, lo:hi] = _dot(ds, kh).astype(BF16)
            dkv_ref[hd] += _dot_tn(ds, qh)
            dkv_ref[X_H + hd] += _dot_tn(p.astype(BF16), doh)
        dhn = _dot_nt(dqx_ref[...], wq_v[...])
        dx, dg = _rms_bwd_tile(h_ref[...], g_ref[...], dhn)
        dh2_ref[...] = dht + dx
        _accumulate(dg_ref, step, dg)

    return _call(
        body, (dh3, h, g, qx, kv, w_cross), grid=(t // TM,), name=name, comm=comm,
        in_specs=[_row_spec(TM, D), _row_spec(TM, D), _const_spec((1, D)), _row_spec(TM, D),
                  _const_spec((N_DEV, m, X_DH)), _ANY],
        out_specs=[_row_spec(TM, D), _row_spec(TM, D), _const_spec((N_DEV, m, X_DH)), _row_spec(TM, D),
                   _const_spec((1, D))],
        out_shape=[jax.ShapeDtypeStruct((t, D), BF16), jax.ShapeDtypeStruct((t, D), BF16),
                   jax.ShapeDtypeStruct((N_DEV, m, X_DH), F32), jax.ShapeDtypeStruct((t, D), F32),
                   jax.ShapeDtypeStruct((1, D), F32)],
        scratch_shapes=[pltpu.VMEM((D, D), BF16)] * 2 + [pltpu.SemaphoreType.DMA((2 * N_DEV,))])


def _adamw(w, parts, m, v, name, row_block=0, token=None):
    r, c = w.shape
    n = parts.shape[0]
    tr = _pick_tile(r, (256, 352, 128))
    off = row_block * (r // tr)

    def body(*refs):
        if token is None:
            _adamw_update(None, *refs)
        else:
            _adamw_update(refs[4], *refs[:4], *refs[5:])

    spec = _row_spec(tr, c)
    in_specs = [spec, pl.BlockSpec((n, tr, c), lambda i: (0, i + off, 0)), spec, spec]
    operands = (w, parts, m, v)
    if token is not None:
        in_specs.append(_const_spec(token.shape))
        operands += (token,)
    return pl.pallas_call(
        body, grid=(r // tr,), name=name, in_specs=in_specs, out_specs=[spec] * 4,
        out_shape=[jax.ShapeDtypeStruct((r, c), F32)] * 4,
        compiler_params=_cparams(),
    )(*operands)


def _adamw_update(tok_ref, w_ref, p_ref, m_ref, v_ref, g_ref, d_ref, nm_ref, nv_ref):
    gt = p_ref[0].astype(F32)
    for k in range(1, p_ref.shape[0]):
        gt = gt + p_ref[k].astype(F32)
    if tok_ref is not None:
        gt = gt + tok_ref[0:1, 0:1]
    _adamw_apply(gt, w_ref, m_ref, v_ref, g_ref, d_ref, nm_ref, nv_ref)


def _adamw_own(w, land, own, chip, m, v, name, row_block=0, token=None):
    r, c = w.shape
    tr = _pick_tile(r, (256, 352, 128))
    off = row_block * (r // tr)

    def body(chip_ref, w_ref, land_ref, own_ref, m_ref, v_ref, *rest):
        mine = own_ref[0].astype(F32)
        gt = jnp.where(chip_ref[0] == 0, mine, land_ref[0].astype(F32))
        for k in range(1, N_CHIP):
            gt = gt + jnp.where(chip_ref[0] == k, mine, land_ref[k].astype(F32))
        if token is not None:
            gt = gt + rest[0][0:1, 0:1]
        _adamw_apply(gt, w_ref, m_ref, v_ref, *rest[-4:])

    spec = pl.BlockSpec((tr, c), lambda i, chip_ref: (i, 0))
    in_specs = [spec, pl.BlockSpec((N_CHIP, tr, c), lambda i, chip_ref: (0, i + off, 0)),
                pl.BlockSpec((1, tr, c), lambda i, chip_ref: (chip_ref[0], i + off, 0)), spec, spec]
    operands = (chip, w, land, own, m, v)
    if token is not None:
        in_specs.append(pl.BlockSpec(token.shape, lambda i, chip_ref: (0, 0)))
        operands += (token,)
    return pl.pallas_call(
        body, name=name,
        grid_spec=pltpu.PrefetchScalarGridSpec(
            num_scalar_prefetch=1, grid=(r // tr,), in_specs=in_specs, out_specs=[spec] * 4),
        out_shape=[jax.ShapeDtypeStruct((r, c), F32)] * 4,
        compiler_params=_cparams(),
    )(*operands)


def _adamw_apply(gt, w_ref, m_ref, v_ref, g_ref, d_ref, nm_ref, nv_ref):
    g_ref[...] = gt
    nm = ADAM_B1 * m_ref[...] + (1.0 - ADAM_B1) * gt
    nv = ADAM_B2 * v_ref[...] + (1.0 - ADAM_B2) * jnp.square(gt)
    m_hat = nm / (1.0 - ADAM_B1 ** ADAM_STEP)
    v_hat = nv / (1.0 - ADAM_B2 ** ADAM_STEP)
    d_ref[...] = -ADAM_LR * (m_hat / (jnp.sqrt(v_hat) + ADAM_EPS) + ADAM_WD * w_ref[...])
    nm_ref[...] = nm
    nv_ref[...] = nv


def _mesh_pos():
    return lax.axis_index("x"), lax.axis_index("y"), lax.axis_index("c")


def _no_round(in_refs, out_refs, sems):
    pass


def _run_exchange(comm, name):
    c_in, c_out = len(comm.inputs), len(comm.out_shapes)

    def body(*refs):
        cins, couts, sems = refs[:c_in], refs[c_in:c_in + c_out], refs[c_in + c_out:]
        comm.start(cins, couts, sems)
        comm.middle(cins, couts, sems)
        comm.finish(cins, couts, sems)

    return list(pl.pallas_call(
        body, name=name, out_shape=list(comm.out_shapes),
        in_specs=[_ANY] * c_in, out_specs=[_ANY] * c_out, scratch_shapes=list(comm.sem_shapes),
    )(*comm.inputs))


def _gather_exchange(shards):
    n_arr = len(shards)

    def plan(x_refs, out_refs, sems):
        send_sems, recv_sems, local_sems = sems[:3]
        stage = sems[3:]
        x, y, c = _mesh_pos()
        me, sibling = (x, y, c), (x, y, 1 - c)
        xn, yn, diag = (1 - x, y), (x, 1 - y), (1 - x, 1 - y)

        def slot(a, px, py, pc, half=None):
            ref = out_refs[a].at[4 * px + 2 * py + pc]
            if half is None:
                return ref
            rows = shards[a].shape[0] // 2
            return ref.at[half * rows:(half + 1) * rows]

        def copy(a, k, block, to, half=None, src=None):
            dst = slot(a, *block, half)
            return pltpu.make_async_remote_copy(
                src_ref=dst if src is None else src, dst_ref=dst,
                send_sem=send_sems.at[a, k], recv_sem=recv_sems.at[a, k],
                device_id=to, device_id_type=pl.DeviceIdType.MESH)

        return types.SimpleNamespace(
            me=me, sibling=sibling, xn=xn, yn=yn, diag=diag, c=c, copy=copy,
            mine_in=[pltpu.make_async_copy(x_refs[a], stage[a], local_sems.at[a, 0]) for a in range(n_arr)],
            mine_out=[pltpu.make_async_copy(stage[a], slot(a, *me), local_sems.at[a, 1]) for a in range(n_arr)],
            first=[cp for a in range(n_arr) for cp in (
                copy(a, 0, me, sibling, src=x_refs[a]), copy(a, 1, me, (*xn, c), src=x_refs[a]),
                copy(a, 2, me, (*yn, c), src=x_refs[a]))],
            second=lambda a: (copy(a, 3, (*xn, c), (*yn, c), half=0), copy(a, 5, (*xn, c), sibling),
                              copy(a, 4, (*yn, c), (*xn, c), half=1), copy(a, 6, (*yn, c), sibling)),
            third=lambda a: (copy(a, 7, (*diag, c), sibling, half=0), copy(a, 8, (*diag, c), sibling, half=1)))

    def start(x_refs, out_refs, sems):
        p = plan(x_refs, out_refs, sems)
        for cp in p.first + p.mine_in:
            cp.start()
        for cp_in, cp_out in zip(p.mine_in, p.mine_out):
            cp_in.wait()
            cp_out.start()

    def middle(x_refs, out_refs, sems):
        p = plan(x_refs, out_refs, sems)
        for a in range(n_arr):
            to_yn, x_to_sib, to_xn, y_to_sib = p.second(a)
            p.copy(a, 1, (*p.xn, p.c), p.me).wait_recv()
            to_yn.start()
            x_to_sib.start()
            p.copy(a, 2, (*p.yn, p.c), p.me).wait_recv()
            to_xn.start()
            y_to_sib.start()

    def finish(x_refs, out_refs, sems):
        p = plan(x_refs, out_refs, sems)
        for a in range(n_arr):
            half0_to_sib, half1_to_sib = p.third(a)
            p.copy(a, 3, (*p.diag, p.c), p.me, half=0).wait_recv()
            half0_to_sib.start()
            p.copy(a, 4, (*p.diag, p.c), p.me, half=1).wait_recv()
            half1_to_sib.start()
        other = 1 - p.c
        for a in range(n_arr):
            p.copy(a, 0, p.sibling, p.me).wait_recv()
            p.copy(a, 5, (*p.xn, other), p.me).wait_recv()
            p.copy(a, 6, (*p.yn, other), p.me).wait_recv()
            p.copy(a, 7, (*p.diag, other), p.me, half=0).wait_recv()
            p.copy(a, 8, (*p.diag, other), p.me, half=1).wait_recv()
        for cp in p.first:
            cp.wait_send()
        for a in range(n_arr):
            for cp in p.second(a) + p.third(a):
                cp.wait_send()
        for cp in p.mine_out:
            cp.wait()

    return types.SimpleNamespace(
        inputs=list(shards), start=start, middle=middle, finish=finish,
        out_shapes=[jax.ShapeDtypeStruct((N_DEV,) + s.shape, s.dtype) for s in shards],
        sem_shapes=[pltpu.SemaphoreType.DMA((n_arr, 9)), pltpu.SemaphoreType.DMA((n_arr, 9)),
                    pltpu.SemaphoreType.DMA((n_arr, 2))] + [pltpu.VMEM(s.shape, s.dtype) for s in shards])


def _pair_exchange(grads):
    n_arr = len(grads)

    def plan(g_refs, land_refs, sems):
        send_sems, recv_sems = sems
        x, y, c = _mesh_pos()
        return [pltpu.make_async_remote_copy(
            src_ref=g_refs[a].at[2 * k + 1 - c], dst_ref=land_refs[a].at[k],
            send_sem=send_sems.at[a, k], recv_sem=recv_sems.at[a, k],
            device_id=(x, y, 1 - c), device_id_type=pl.DeviceIdType.MESH)
            for a in range(n_arr) for k in range(N_CHIP)]

    def start(g_refs, land_refs, sems):
        for cp in plan(g_refs, land_refs, sems):
            cp.start()

    def finish(g_refs, land_refs, sems):
        for cp in plan(g_refs, land_refs, sems):
            cp.wait()

    return types.SimpleNamespace(
        inputs=list(grads), start=start, middle=_no_round, finish=finish,
        out_shapes=[jax.ShapeDtypeStruct((N_CHIP,) + g.shape[1:], g.dtype) for g in grads],
        sem_shapes=[pltpu.SemaphoreType.DMA((n_arr, N_CHIP)), pltpu.SemaphoreType.DMA((n_arr, N_CHIP))])


def _chip_exchange(parts):
    n_arr = len(parts)

    def plan(p_refs, land_refs, sems):
        send_sems, recv_sems, local_sems = sems
        x, y, c = _mesh_pos()
        my_chip = 2 * x + y
        chips = [(1 - x, y), (x, 1 - y), (1 - x, 1 - y)]
        local = [pltpu.make_async_copy(p_refs[a].at[my_chip], land_refs[a].at[my_chip], local_sems.at[a])
                 for a in range(n_arr)]

        def copy(a, k, src_slot, dst_slot, px, py):
            return pltpu.make_async_remote_copy(
                src_ref=p_refs[a].at[src_slot], dst_ref=land_refs[a].at[dst_slot],
                send_sem=send_sems.at[a, k], recv_sem=recv_sems.at[a, k],
                device_id=(px, py, c), device_id_type=pl.DeviceIdType.MESH)

        sends = [copy(a, k, 2 * px + py, my_chip, px, py) for a in range(n_arr) for k, (px, py) in enumerate(chips)]
        arrivals = [copy(a, k, my_chip, 2 * px + py, px, py) for a in range(n_arr)
                    for k, (px, py) in enumerate(chips)]
        return local, sends, arrivals

    def start(p_refs, land_refs, sems):
        local, sends, _ = plan(p_refs, land_refs, sems)
        for cp in local + sends:
            cp.start()

    def finish(p_refs, land_refs, sems):
        local, sends, arrivals = plan(p_refs, land_refs, sems)
        for cp in arrivals:
            cp.wait_recv()
        for cp in sends:
            cp.wait_send()
        for cp in local:
            cp.wait()

    return types.SimpleNamespace(
        inputs=list(parts), start=start, middle=_no_round, finish=finish,
        out_shapes=[jax.ShapeDtypeStruct(p.shape, p.dtype) for p in parts],
        sem_shapes=[pltpu.SemaphoreType.DMA((n_arr, 3)), pltpu.SemaphoreType.DMA((n_arr, 3)),
                    pltpu.SemaphoreType.DMA((n_arr,))])


_HBM = pl.BlockSpec(memory_space=pltpu.HBM)
_SEM = pl.BlockSpec(memory_space=pltpu.SEMAPHORE)
_DATAFLOW = pltpu.SideEffectType.DATAFLOW_SIDE_EFFECTING


def _chip_copies(p_refs, land_refs, send_sems, recv_sems):
    x, y, c = _mesh_pos()
    my_chip = 2 * x + y
    chips = [(1 - x, y), (x, 1 - y), (1 - x, 1 - y)]
    return [pltpu.make_async_remote_copy(
        src_ref=p_refs[a].at[2 * px + py], dst_ref=land_refs[a].at[my_chip],
        send_sem=send_sems[3 * a + k], recv_sem=recv_sems[3 * a + k],
        device_id=(px, py, c), device_id_type=pl.DeviceIdType.MESH)
        for a in range(len(p_refs)) for k, (px, py) in enumerate(chips)]


def _chip_exchange_begin(parts, name):
    n_arr = len(parts)
    n_buf, n_copy = 2 * n_arr, 3 * n_arr
    lands = [lax.empty(p.shape, p.dtype) for p in parts]

    def body(*refs):
        p_refs, land_refs = refs[:n_arr], refs[n_arr:n_buf]
        send_sems, recv_sems, token = refs[n_buf:n_buf + n_copy], refs[n_buf + n_copy:n_buf + 2 * n_copy], refs[-1]
        for cp in _chip_copies(p_refs, land_refs, send_sems, recv_sems):
            cp.start()
        token[...] = jnp.zeros_like(token)

    bufs = list(parts) + list(lands)
    outs = pl.pallas_call(
        body, name=name,
        out_shape=(*[pltpu.SemaphoreType.DMA(())] * (2 * n_copy), *[pltpu.HBM(b.shape, b.dtype) for b in bufs],
                   jax.ShapeDtypeStruct((8, 128), F32)),
        in_specs=[_HBM] * n_buf,
        out_specs=(*[_SEM] * (2 * n_copy), *[_HBM] * n_buf, pl.BlockSpec(memory_space=pltpu.VMEM)),
        input_output_aliases={i: 2 * n_copy + i for i in range(n_buf)},
        compiler_params=pltpu.CompilerParams(has_side_effects=_DATAFLOW),
    )(*[pltpu.with_memory_space_constraint(b, pltpu.HBM) for b in bufs])
    sems = list(outs[:2 * n_copy])
    thru = list(outs[2 * n_copy:2 * n_copy + n_buf])
    return types.SimpleNamespace(send_sems=sems[:n_copy], recv_sems=sems[n_copy:], parts=thru[:n_arr],
                                 lands=thru[n_arr:], token=outs[-1])


def _chip_exchange_end(flight, after, name):
    send_sems, recv_sems, parts, lands = flight.send_sems, flight.recv_sems, flight.parts, flight.lands
    n_arr = len(parts)
    n_buf, n_copy = 2 * n_arr, 3 * n_arr

    def body(*refs):
        p_refs, land_refs = refs[:n_arr], refs[n_arr:n_buf]
        sems = refs[n_buf:n_buf + 2 * n_copy]
        for cp in _chip_copies(p_refs, land_refs, sems[:n_copy], sems[n_copy:]):
            cp.wait_send()
            cp.wait_recv()

    bufs = list(parts) + list(lands)
    outs = pl.pallas_call(
        body, name=name, out_shape=tuple(pltpu.HBM(b.shape, b.dtype) for b in bufs),
        in_specs=[_HBM] * n_buf + [_SEM] * (2 * n_copy) + [_ANY], out_specs=tuple([_HBM] * n_buf),
        input_output_aliases={i: i for i in range(n_buf)},
        compiler_params=pltpu.CompilerParams(has_side_effects=_DATAFLOW),
    )(*bufs, *send_sems, *recv_sems, after)
    return list(outs[:n_arr]), list(outs[n_arr:])


def _row_tile(r, cap=640):
    best = None
    for cand in range(16, min(r, cap) + 1, 16):
        if r % cand == 0:
            best = cand
    return best if best is not None else r


def _pair_sum(g, landed, core, name):
    _, r, c_dim = g.shape
    tr = _row_tile(r)

    def body(core_ref, mine_ref, theirs_ref, o_ref):
        o_ref[0] = (mine_ref[0].astype(F32) + theirs_ref[0].astype(F32)).astype(o_ref.dtype)

    return pl.pallas_call(
        body, name=name,
        grid_spec=pltpu.PrefetchScalarGridSpec(
            num_scalar_prefetch=1, grid=(N_CHIP, r // tr),
            in_specs=[pl.BlockSpec((1, tr, c_dim), lambda k, i, core_ref: (2 * k + core_ref[0], i, 0)),
                      pl.BlockSpec((1, tr, c_dim), lambda k, i, core_ref: (k, i, 0))],
            out_specs=pl.BlockSpec((1, tr, c_dim), lambda k, i, core_ref: (k, i, 0))),
        out_shape=jax.ShapeDtypeStruct((N_CHIP, r, c_dim), g.dtype),
        compiler_params=_cparams(2),
    )(core, g, landed)


def _sum_slots(parts, name):
    n, r, c_dim = parts.shape
    tr = _row_tile(r)

    def body(p_ref, o_ref):
        acc = p_ref[0].astype(F32)
        for k in range(1, n):
            acc = acc + p_ref[k].astype(F32)
        o_ref[...] = acc

    return pl.pallas_call(
        body, grid=(r // tr,), name=name,
        in_specs=[pl.BlockSpec((n, tr, c_dim), lambda i: (0, i, 0))],
        out_specs=_row_spec(tr, c_dim),
        out_shape=jax.ShapeDtypeStruct((r, c_dim), F32),
        compiler_params=_cparams(),
    )(parts)


GAINS = ("g_ffn1", "g_mix", "g_cross", "g_mem", "g_ffn2", "g_final")
SMALL = GAINS + ("b_gate", "conv_w")
SMALL_R = 16
LOSS_ROW = 11
WEIGHT_ORDER = ("g_ffn1", "w_ffn1_gu", "w_ffn1_down", "g_mix", "w_in", "b_gate", "conv_w", "w_conv_out",
                "w_attn_out", "w_o", "g_cross", "g_mem", "w_cq", "w_ckv", "w_co", "g_ffn2", "w_ffn2_gu",
                "w_ffn2_down", "g_final")
GU_NAMES = ("w_ffn1_gu", "w_ffn2_gu")


def _pack_small(vals, conv_rows):
    rows = [vals[n].reshape(1, D) for n in GAINS] + [vals["b_gate"].reshape(2, D), conv_rows.reshape(CONV_K, D)]
    used = len(GAINS) + 2 + CONV_K
    return jnp.concatenate(rows + [jnp.zeros((SMALL_R - used, D), F32)], axis=0)


def _unpack_small(buf):
    out = {n: buf[k] for k, n in enumerate(GAINS)}
    out["b_gate"] = buf[6:8].reshape(2 * D)
    out["conv_w"] = buf[8:8 + CONV_K]
    return out


def _exchange_shards(wts):
    out = {n: jnp.pad(wts[n].T.astype(BF16), ((0, FF_PAD - FF_BLK), (0, 0))) for n in GU_NAMES}
    for n in ("w_ckv", "w_in", "w_ffn1_down", "w_ffn2_down"):
        out[n] = wts[n].astype(BF16)
    out["mix"] = jnp.concatenate([wts[n].astype(BF16) for n in MIX_MATS], axis=0)
    out["cross"] = jnp.concatenate([wts[n].astype(BF16) for n in CROSS_MATS], axis=0)
    return out


def _reduce_group(grads, landed, core, names):
    return [_pair_sum(g, l, core, "grads_pair_sum_" + n) for g, l, n in zip(grads, landed, names)]


def _step(x, mem, target, sh, conv_pad, gains, b_gate, core):
    wg1, wd1, conv_all = _run_exchange(_gather_exchange([sh["w_ffn1_gu"], sh["w_ffn1_down"], conv_pad]), "gather_ffn1")
    conv_w = conv_all[:, :CONV_K, :].transpose(1, 0, 2).reshape(CONV_K, D)
    (n1, gate1, up1, act1, h1), (w_in,) = _ffn_fwd(
        x, gains["g_ffn1"], wg1, wd1, "ffn1_fwd", comm=_gather_exchange([sh["w_in"]]))
    (u, pcg, qkv, yc), (w_mix,) = _inproj_fwd(h1, gains["g_mix"], w_in, conv_w, "inproj_fwd",
                                              comm=_gather_exchange([sh["mix"]]))
    (ysb, ctot), (w_cross, w_ckv, wg2) = _sb_fwd(
        qkv, "sb_fwd", comm=_gather_exchange([sh["cross"], sh["w_ckv"], sh["w_ffn2_gu"]]))
    (a_mix, b_mix, merged, h2), (wd2,) = _mix_out_fwd(yc, ysb, pcg, b_gate, h1, w_mix, "mix_out_fwd",
                                                      comm=_gather_exchange([sh["w_ffn2_down"]]))
    mn, kv = _memkv_fwd(mem, gains["g_mem"], w_ckv, "memkv_fwd")
    hn, qx, o_x, h3 = _cross_fwd(h2, gains["g_cross"], kv, w_cross, "cross_fwd")
    (n4, gate2, up2, act2, dh4, loss, dg_final), _ = _ffn_fwd(h3, gains["g_ffn2"], wg2, wd2, "ffn2_fwd",
                                                              head=(gains["g_final"], target))

    gs = {"g_final": dg_final}
    (dgu2, dh4b, dh3, gs["g_ffn2"]), _ = _ffn_bwd(dh4, h3, gains["g_ffn2"], gate2, up2, wg2, wd2, "ffn2_bwd")
    grads_a = [_mm_tn_rows(dgu2, n4, FF_PAD, "dw_ffn2_gu"),
               _mm_tn_rows(act2, dh4b, FF_BLK, "dw_ffn2_down").reshape(N_DEV, DOWN_ROWS, D)]
    names_a = ["w_ffn2_gu", "w_ffn2_down"]
    (dh3b, dqx, dkv, dh2, gs["g_cross"]), landed_a = _cross_bwd(
        dh3, h2, gains["g_cross"], qx, kv, w_cross, "cross_bwd", comm=_pair_exchange(grads_a))
    sums_a = _reduce_group(grads_a, landed_a, core, names_a)
    cross_stack = _mm_tn_square(hn, dqx, "dw_cq", 0, len(CROSS_MATS))
    grads_b = [_mm_tn_cols(mn, dkv, "dw_ckv"), _mm_tn_square(o_x, dh3b, "dw_co", 1, len(CROSS_MATS), cross_stack)]
    names_b = ["w_ckv", "cross"]
    gs["g_mem"] = _memkv_bwd(dkv, mem, gains["g_mem"], w_ckv, "memkv_bwd")
    (dh2b, da_mix, db_mix, dgp, dconv, dysb, gs["b_gate"], gs["conv_w"]), landed_b = _mix_out_bwd(
        dh2, a_mix, b_mix, pcg, b_gate, conv_w, w_mix, "mix_out_bwd", comm=_pair_exchange(grads_b))
    sums_b = _reduce_group(grads_b, landed_b, core, names_b)
    mix_stack = _mm_tn_square(yc, da_mix, "dw_conv_out", 0, len(MIX_MATS))
    mix_stack = _mm_tn_square(ysb, db_mix, "dw_attn_out", 1, len(MIX_MATS), mix_stack)
    grads_c = [_mm_tn_square(merged, dh2b, "dw_o", 2, len(MIX_MATS), mix_stack)]
    flight_ab = _chip_exchange_begin(sums_a + sums_b, "grads_to_chips_early_begin")
    (dq, dkv_sb), landed_c = _sb_bwd(qkv, dysb, ctot, flight_ab.token, "sb_bwd", comm=_pair_exchange(grads_c))
    sums_c = _reduce_group(grads_c, landed_c, core, ["mix"])
    w_in_stack = _mm_tn_cols(u, dconv, "dw_in_conv", 0, N_DEV)
    w_in_stack = _mm_tn_cols(u, dq[None], "dw_in_q", 3, N_DEV, w_in_stack)
    w_in_stack = _mm_tn_cols(u, dkv_sb, "dw_in_kv", 4, N_DEV, w_in_stack)
    grads_d = [_mm_tn_cols(u, dgp, "dw_in_gates", 6, N_DEV, w_in_stack)]
    (dh1, gs["g_mix"]), landed_d = _inproj_bwd(dconv, dq, dkv_sb, dgp, w_in, h1, gains["g_mix"], dh2, "inproj_bwd",
                                               comm=_pair_exchange(grads_d))
    sums_d = _reduce_group(grads_d, landed_d, core, ["w_in"])
    flight_d = _chip_exchange_begin(sums_c + sums_d, "grads_to_chips_w_in_begin")
    (dgu1, dh1b, dx, gs["g_ffn1"]), _ = _ffn_bwd(dh1, x, gains["g_ffn1"] + flight_d.token[0, 0], gate1, up1, wg1, wd1,
                                                 "ffn1_bwd")
    dw_gu1 = _mm_tn_rows(dgu1, n1, FF_PAD, "dw_ffn1_gu")
    dw_down1, landed_gu1 = _mm_tn_rows(act1, dh1b, FF_BLK, "dw_ffn1_down", comm=_pair_exchange([dw_gu1]))
    grads_e = [dw_gu1, dw_down1.reshape(N_DEV, DOWN_ROWS, D)]
    names_e = ["w_ffn1_gu", "w_ffn1_down"]
    landed_e = landed_gu1 + _run_exchange(_pair_exchange(grads_e[1:]), "grads_to_sibling_ffn1_down")
    flight_e = _chip_exchange_begin(_reduce_group(grads_e, landed_e, core, names_e), "grads_to_chips_ffn1_begin")
    flights = [(names_a + names_b, flight_ab), (["mix", "w_in"], flight_d), (names_e, flight_e)]
    return loss, dx, flights, gs


def kernel(x, mem, g_ffn1, w_ffn1_gu, w_ffn1_down, g_mix, w_in, b_gate, conv_w, w_conv_out, w_attn_out, w_o, g_cross, g_mem, w_cq, w_ckv, w_co, g_ffn2, w_ffn2_gu, w_ffn2_down, g_final, loss_target, m_g_ffn1, m_w_ffn1_gu, m_w_ffn1_down, m_g_mix, m_w_in, m_b_gate, m_conv_w, m_w_conv_out, m_w_attn_out, m_w_o, m_g_cross, m_g_mem, m_w_cq, m_w_ckv, m_w_co, m_g_ffn2, m_w_ffn2_gu, m_w_ffn2_down, m_g_final, v_g_ffn1, v_w_ffn1_gu, v_w_ffn1_down, v_g_mix, v_w_in, v_b_gate, v_conv_w, v_w_conv_out, v_w_attn_out, v_w_o, v_g_cross, v_g_mem, v_w_cq, v_w_ckv, v_w_co, v_g_ffn2, v_w_ffn2_gu, v_w_ffn2_down, v_g_final):
    args = locals()
    wts = {n: args[n] for n in WEIGHT_ORDER}
    mom1 = {n: args["m_" + n] for n in WEIGHT_ORDER}
    mom2 = {n: args["v_" + n] for n in WEIGHT_ORDER}
    cx, cy, cc = _mesh_pos()
    dev = 4 * cx + 2 * cy + cc
    conv_cols = D // N_DEV

    conv_pad = jnp.concatenate([conv_w, jnp.zeros((SMALL_R - CONV_K, conv_cols), F32)], axis=0)
    gains = {n: wts[n].reshape(1, D) for n in GAINS}
    loss8, dx, flights, gs = _step(x[0], mem[0], loss_target[0], _exchange_shards(wts), conv_pad, gains,
                                 b_gate.reshape(1, 2 * D), cc.reshape(1).astype(jnp.int32))

    grads, delta, new_m, new_v = {}, {}, {}, {}

    def operands(n, transposed):
        trio = (wts[n], mom1[n], mom2[n])
        return tuple(a.T for a in trio) if transposed else trio

    def record(n, res, transposed):
        grads[n], delta[n], new_m[n], new_v[n] = [r.T for r in res] if transposed else res

    early = [("w_ffn2_gu", "w_ffn2_gu", 0, True), ("w_ffn2_down", "w_ffn2_down", 0, False),
             ("w_ckv", "w_ckv", 0, False), ("w_in", "w_in", 0, False)]
    early += [(n, "mix", k, False) for k, n in enumerate(MIX_MATS)]
    early += [(n, "cross", k, False) for k, n in enumerate(CROSS_MATS)]
    chip = (2 * cx + cy).reshape(1).astype(jnp.int32)
    (names_early, flight_early), (names_w_in, flight_w_in), (last_names, flight_last) = flights
    token = flight_last.token
    own, land = {}, {}
    for names, flight, tag in ((names_early, flight_early, "early"), (names_w_in, flight_w_in, "w_in")):
        own_parts, landed = _chip_exchange_end(flight, token, "grads_to_chips_%s_end" % tag)
        own.update(zip(names, own_parts))
        land.update(zip(names, landed))
    for n, buf, row_block, transposed in early:
        w, m1, m2 = operands(n, transposed)
        record(n, _adamw_own(w, land[buf], own[buf], chip, m1, m2, "adamw_" + n, row_block, token), transposed)

    after = jnp.concatenate([new_v[n][:1, :1] for n, _, _, _ in early], axis=0)
    own_parts, landed = _chip_exchange_end(flight_last, after, "grads_to_chips_ffn1_end")
    for n, own_n, land_n, transposed in zip(last_names, own_parts, landed, (True, False)):
        w, m1, m2 = operands(n, transposed)
        record(n, _adamw_own(w, land_n, own_n, chip, m1, m2, "adamw_" + n), transposed)

    gs_rows = {n: gs[n] for n in GAINS + ("b_gate",)}
    small_mine = _pack_small(gs_rows, gs["conv_w"][:CONV_K]) + new_v[last_names[-1]][0, 0] * 0.0
    small_mine = small_mine.at[LOSS_ROW, 0].set(loss8[0, 0])
    small_all = _run_exchange(_gather_exchange([small_mine]), "gather_small_grads")[0]
    small_sum = _sum_slots(small_all, "small_grads_sum")
    loss = small_sum[LOSS_ROW, 0]
    grad_small = _unpack_small(small_sum)
    grad_small["conv_w"] = lax.dynamic_slice_in_dim(grad_small["conv_w"], dev * conv_cols, conv_cols, axis=1)
    grads.update(grad_small)

    def small_buf(vals):
        return _pack_small(vals, jnp.concatenate([vals["conv_w"], jnp.zeros((CONV_K, D - conv_cols), F32)], axis=1))

    _, d_s, m_s, v_s = _adamw(small_buf(wts), small_buf(grads)[None], small_buf(mom1), small_buf(mom2), "adamw_small")
    for res, buf in ((delta, d_s), (new_m, m_s), (new_v, v_s)):
        un = _unpack_small(buf)
        for n in GAINS + ("b_gate",):
            res[n] = un[n]
        res["conv_w"] = un["conv_w"][:, :conv_cols]

    return (loss, dx[None], *[grads[n] for n in WEIGHT_ORDER], *[delta[n] for n in WEIGHT_ORDER],
            *[new_m[n] for n in WEIGHT_ORDER], *[new_v[n] for n in WEIGHT_ORDER])
```

```python
import types

import jax
import jax.numpy as jnp
from jax import lax
from jax.experimental import pallas as pl
from jax.experimental.pallas import tpu as pltpu

F32 = jnp.float32
BF16 = jnp.bfloat16

D = 1024
DFF = 2816
SB_H = 8
SB_DH = 128
X_H = 4
X_DH = 256
CONV_K = 3
RMS_EPS = 1e-6
N_DEV = 8
N_CHIP = 4
SQ_ROWS = D // N_DEV

ADAM_LR = 0.001
ADAM_B1 = 0.9
ADAM_B2 = 0.999
ADAM_EPS = 1e-08
ADAM_WD = 0.01
ADAM_STEP = 10

TM = 256
TQ = 512
TK = 256
SB_HPS = 2
VMEM_LIMIT = 56 << 20

FF_BLK = DFF // 4
FF_PAD = 768
FF_SUB = 256
DOWN_ROWS = DFF // N_DEV

MIX_MATS = ("w_conv_out", "w_attn_out", "w_o")
CROSS_MATS = ("w_cq", "w_co")

_ANY = pl.BlockSpec(memory_space=pl.ANY)


def _cparams(n_axes=1):
    return pltpu.CompilerParams(
        dimension_semantics=("arbitrary",) * n_axes, vmem_limit_bytes=VMEM_LIMIT)


def _row_spec(tm, n):
    return pl.BlockSpec((tm, n), lambda i: (i, 0))


def _blk_row_spec(nb, tm, n):
    return pl.BlockSpec((nb, tm, n), lambda i: (0, i, 0))


def _const_spec(shape):
    zeros = (0,) * len(shape)
    return pl.BlockSpec(shape, lambda i: zeros)


def _dot(a, b):
    return jnp.dot(a, b, preferred_element_type=F32)


def _dot_nt(a, b):
    return lax.dot_general(a, b, (((1,), (1,)), ((), ())), preferred_element_type=F32)


def _dot_tn(a, b):
    return lax.dot_general(a, b, (((0,), (0,)), ((), ())), preferred_element_type=F32)


def _sigmoid(x):
    return 1.0 / (1.0 + jnp.exp(-x))


def _call(body, operands, *, grid, in_specs, out_specs, out_shape, scratch_shapes, name, comm=None):
    n_in, n_out, n_sc = len(in_specs), len(out_specs), len(scratch_shapes)
    if comm is None:
        outs = pl.pallas_call(
            body, grid=grid, name=name, in_specs=in_specs, out_specs=out_specs, out_shape=out_shape,
            scratch_shapes=scratch_shapes, compiler_params=_cparams(len(grid)))(*operands)
        return list(outs), []
    c_in, c_out, c_sem = len(comm.inputs), len(comm.out_shapes), len(comm.sem_shapes)

    def hosted(*refs):
        bounds = [0, n_in, c_in, n_out, c_out, n_sc, c_sem]
        parts, pos = [], 0
        for k in bounds[1:]:
            parts.append(refs[pos:pos + k])
            pos += k
        ins, cins, outs, couts, scr, sems = parts
        step, n_steps = pl.program_id(0), grid[0]
        for ax in range(1, len(grid)):
            step, n_steps = step * grid[ax] + pl.program_id(ax), n_steps * grid[ax]

        @pl.when(step == 0)
        def _():
            comm.start(cins, couts, sems)

        @pl.when(step == (2 * n_steps) // 3)
        def _():
            comm.middle(cins, couts, sems)

        body(*ins, *outs, *scr)

        @pl.when(step == n_steps - 1)
        def _():
            comm.finish(cins, couts, sems)

    res = pl.pallas_call(
        hosted, grid=grid, name=name, in_specs=list(in_specs) + [_ANY] * c_in,
        out_specs=list(out_specs) + [_ANY] * c_out, out_shape=list(out_shape) + list(comm.out_shapes),
        scratch_shapes=list(scratch_shapes) + list(comm.sem_shapes),
        compiler_params=_cparams(len(grid)))(*operands, *comm.inputs)
    return list(res[:n_out]), list(res[n_out:])


def _load_resident(step, pairs, sems):
    @pl.when(step == 0)
    def _():
        copies = [pltpu.make_async_copy(src, dst, sems.at[k]) for k, (src, dst) in enumerate(pairs)]
        for cp in copies:
            cp.start()
        for cp in copies:
            cp.wait()


def _load_staged(step, pairs, sems):
    copies = [pltpu.make_async_copy(src, dst, sems.at[k]) for k, (src, dst) in enumerate(pairs)]

    @pl.when(step == 0)
    def _():
        for cp in copies:
            cp.start()

    def ready(ks):
        @pl.when(step == 0)
        def _():
            for k in ks:
                copies[k].wait()

    return ready


def _square_pairs(buf_hbm, index, dst):
    off = index * SQ_ROWS
    return [(buf_hbm.at[d, off:off + SQ_ROWS, :], dst.at[d * SQ_ROWS:(d + 1) * SQ_ROWS, :]) for d in range(N_DEV)]


def _down_pairs(wd_hbm, dst):
    return [(wd_hbm.at[d], dst.at[d // 2, (d % 2) * DOWN_ROWS:(d % 2 + 1) * DOWN_ROWS, :]) for d in range(N_DEV)]


def _ffn_pairs(wgu_hbm, wgu_v, wd_hbm, wd_v):
    down = _down_pairs(wd_hbm, wd_v)
    pairs = []
    for j in range(4):
        pairs += [(wgu_hbm.at[j], wgu_v.at[j]), (wgu_hbm.at[4 + j], wgu_v.at[4 + j]), down[2 * j], down[2 * j + 1]]
    return pairs


def _zero_down_pad(step, dst):
    @pl.when(step == 0)
    def _():
        dst[:, FF_BLK:, :] = jnp.zeros((4, FF_PAD - FF_BLK, D), BF16)


def _rms_fwd_tile(xt, g):
    r = lax.rsqrt(jnp.mean(xt * xt, axis=-1, keepdims=True) + RMS_EPS)
    return (xt * r) * g


def _rms_bwd_tile(xt, g, dn):
    r = lax.rsqrt(jnp.mean(xt * xt, axis=-1, keepdims=True) + RMS_EPS)
    xhat = xt * r
    dxhat = dn * g
    dx = r * (dxhat - xhat * jnp.mean(dxhat * xhat, axis=-1, keepdims=True))
    dg = jnp.sum(dn * xhat, axis=0, keepdims=True)
    return dx, dg


def _accumulate(ref, step, value):
    @pl.when(step == 0)
    def _():
        ref[...] = value

    @pl.when(step != 0)
    def _():
        ref[...] = ref[...] + value


def _ffn_fwd(x, g, wgu, wd, name, comm=None, head=None):
    t = x.shape[0]

    def body(x_ref, g_ref, wgu_hbm, wd_hbm, *refs):
        if head is None:
            n_ref, gate_ref, up_ref, act_ref, h_ref, wgu_v, wd_v, sems = refs
        else:
            gf_ref, t_ref, n_ref, gate_ref, up_ref, act_ref, dh_ref, loss_ref, dgf_ref, wgu_v, wd_v, sems = refs
        step = pl.program_id(0)
        _zero_down_pad(step, wd_v)
        ready = _load_staged(step, _ffn_pairs(wgu_hbm, wgu_v, wd_hbm, wd_v), sems)
        xt = x_ref[...]
        n = _rms_fwd_tile(xt, g_ref[...]).astype(BF16)
        n_ref[...] = n
        acc = jnp.zeros((TM, D), F32)
        for j in range(4):
            ready(range(4 * j, 4 * j + 4))
            for s in range(FF_PAD // FF_SUB):
                lo, hi = s * FF_SUB, (s + 1) * FF_SUB
                gt = _dot_nt(n, wgu_v[j, lo:hi, :])
                ut = _dot_nt(n, wgu_v[4 + j, lo:hi, :])
                gate_ref[j, :, lo:hi] = gt.astype(BF16)
                up_ref[j, :, lo:hi] = ut.astype(BF16)
                act_ref[j, :, lo:hi] = ((gt * _sigmoid(gt)) * ut).astype(BF16)
            acc = acc + _dot(act_ref[j], wd_v[j])
        ht = xt + 0.5 * acc
        if head is None:
            h_ref[...] = ht
        else:
            gain = gf_ref[...]
            diff = _rms_fwd_tile(ht, gain) - t_ref[...]
            part = 0.5 * jnp.sum(jnp.sum(diff * diff, axis=-1, keepdims=True) / D, axis=0, keepdims=True)
            dx, dg = _rms_bwd_tile(ht, gain, diff / D)
            dh_ref[...] = dx
            _accumulate(loss_ref, step, jnp.broadcast_to(part, (8, 128)))
            _accumulate(dgf_ref, step, dg)

    ff = jax.ShapeDtypeStruct((4, t, FF_PAD), BF16)
    operands, in_specs = (x, g, wgu, wd), [_row_spec(TM, D), _const_spec((1, D)), _ANY, _ANY]
    out_specs = [_row_spec(TM, D)] + [_blk_row_spec(4, TM, FF_PAD)] * 3 + [_row_spec(TM, D)]
    out_shape = [jax.ShapeDtypeStruct((t, D), BF16), ff, ff, ff, jax.ShapeDtypeStruct((t, D), F32)]
    if head is not None:
        operands += tuple(head)
        in_specs += [_const_spec((1, D)), _row_spec(TM, D)]
        out_specs += [_const_spec((8, 128)), _const_spec((1, D))]
        out_shape += [jax.ShapeDtypeStruct((8, 128), F32), jax.ShapeDtypeStruct((1, D), F32)]
    return _call(
        body, operands, grid=(t // TM,), name=name, comm=comm, in_specs=in_specs, out_specs=out_specs,
        out_shape=out_shape,
        scratch_shapes=[pltpu.VMEM((N_DEV, FF_PAD, D), BF16), pltpu.VMEM((4, FF_PAD, D), BF16),
                        pltpu.SemaphoreType.DMA((2 * N_DEV,))])


def _ffn_bwd(dh, xin, g, gate, up, wgu, wd, name, comm=None):
    t = dh.shape[0]

    def body(dh_ref, x_ref, g_ref, gate_ref, up_ref, wgu_hbm, wd_hbm,
             dgu_ref, dhb_ref, dx_ref, dg_ref, wgu_v, wd_v, sems):
        step = pl.program_id(0)
        _zero_down_pad(step, wd_v)
        ready = _load_staged(step, _ffn_pairs(wgu_hbm, wgu_v, wd_hbm, wd_v), sems)
        dht = dh_ref[...]
        dhb = (0.5 * dht).astype(BF16)
        dhb_ref[...] = dhb
        dn = jnp.zeros((TM, D), F32)
        for j in range(4):
            ready(range(4 * j, 4 * j + 4))
            for s in range(FF_PAD // FF_SUB):
                lo, hi = s * FF_SUB, (s + 1) * FF_SUB
                da = _dot_nt(dhb, wd_v[j, lo:hi, :])
                gt = gate_ref[j, :, lo:hi].astype(F32)
                ut = up_ref[j, :, lo:hi].astype(F32)
                sg = _sigmoid(gt)
                dgt = (da * ut * (sg * (1.0 + gt * (1.0 - sg)))).astype(BF16)
                dut = (da * (gt * sg)).astype(BF16)
                dgu_ref[j, :, lo:hi] = dgt
                dgu_ref[4 + j, :, lo:hi] = dut
            dn = dn + _dot(dgu_ref[j], wgu_v[j]) + _dot(dgu_ref[4 + j], wgu_v[4 + j])
        dx, dg = _rms_bwd_tile(x_ref[...], g_ref[...], dn)
        dx_ref[...] = dht + dx
        _accumulate(dg_ref, step, dg)

    return _call(
        body, (dh, xin, g, gate, up, wgu, wd), grid=(t // TM,), name=name, comm=comm,
        in_specs=[_row_spec(TM, D), _row_spec(TM, D), _const_spec((1, D)), _blk_row_spec(4, TM, FF_PAD),
                  _blk_row_spec(4, TM, FF_PAD), _ANY, _ANY],
        out_specs=[_blk_row_spec(N_DEV, TM, FF_PAD), _row_spec(TM, D), _row_spec(TM, D), _const_spec((1, D))],
        out_shape=[jax.ShapeDtypeStruct((N_DEV, t, FF_PAD), BF16), jax.ShapeDtypeStruct((t, D), BF16),
                   jax.ShapeDtypeStruct((t, D), F32), jax.ShapeDtypeStruct((1, D), F32)],
        scratch_shapes=[pltpu.VMEM((N_DEV, FF_PAD, D), BF16), pltpu.VMEM((4, FF_PAD, D), BF16),
                        pltpu.SemaphoreType.DMA((2 * N_DEV,))])


WIDE_TILES = (1024, 512, 256, 128)


def _pick_tile(n, options=(512, 256, 128)):
    for o in options:
        if n % o == 0:
            return o
    return n


def _into(stack, n_operands):
    if stack is None:
        return (), [], {}
    return (stack,), [_ANY], {n_operands: 0}


def _mm_tn_square(a, b, name, index, count, stack=None):
    k, m = a.shape
    _, n = b.shape
    tn = _pick_tile(n)
    extra, extra_specs, aliases = _into(stack, 2)

    def body(a_ref, b_ref, *rest):
        rest[-1][...] = _dot_tn(a_ref[...], b_ref[...]).astype(BF16).reshape(N_DEV, m // N_DEV, tn)

    return pl.pallas_call(
        body, grid=(n // tn,), name=name,
        in_specs=[pl.BlockSpec((k, m), lambda j: (0, 0)), pl.BlockSpec((k, tn), lambda j: (0, j))] + extra_specs,
        out_specs=pl.BlockSpec((N_DEV, m // N_DEV, tn), lambda j: (0, index, j)),
        out_shape=jax.ShapeDtypeStruct((N_DEV, count * (m // N_DEV), n), BF16),
        input_output_aliases=aliases, compiler_params=_cparams(1),
    )(a, b, *extra)


def _mm_tn_cols(a, b, name, first=0, count=None, stack=None):
    k, m = a.shape
    nb, _, n = b.shape
    tm = _pick_tile(m, WIDE_TILES)
    extra, extra_specs, aliases = _into(stack, 2)

    def body(a_ref, b_ref, *rest):
        rest[-1][0] = _dot_tn(a_ref[...].astype(BF16), b_ref[0].astype(BF16)).astype(BF16)

    return pl.pallas_call(
        body, grid=(nb, m // tm), name=name,
        in_specs=[pl.BlockSpec((k, tm), lambda j, i: (0, i)), pl.BlockSpec((1, k, n), lambda j, i: (j, 0, 0))]
                 + extra_specs,
        out_specs=pl.BlockSpec((1, tm, n), lambda j, i: (j + first, i, 0)),
        out_shape=jax.ShapeDtypeStruct((nb if count is None else count, m, n), BF16),
        input_output_aliases=aliases, compiler_params=_cparams(2),
    )(a, b, *extra)


def _mm_tn_rows(a, b, keep, name, comm=None):
    nb, k, m = a.shape
    _, n = b.shape
    tn = _pick_tile(n, WIDE_TILES)

    def body(a_ref, b_ref, o_ref):
        o_ref[0] = _dot_tn(a_ref[0], b_ref[...])[:keep].astype(BF16)

    (out,), couts = _call(
        body, (a, b), grid=(nb, n // tn), name=name, comm=comm,
        in_specs=[pl.BlockSpec((1, k, m), lambda j, i: (j, 0, 0)), pl.BlockSpec((k, tn), lambda j, i: (0, i))],
        out_specs=[pl.BlockSpec((1, keep, tn), lambda j, i: (j, 0, i))],
        out_shape=[jax.ShapeDtypeStruct((nb, keep, n), BF16)], scratch_shapes=[])
    return out if comm is None else (out, couts)


PCG_W = 5 * D
QKV_W = 3 * D
PROJ_SUB = 512


def _inproj_fwd(h, g, w_in, conv_w, name, comm=None):
    t = h.shape[0]

    def body(h_ref, g_ref, w_hbm, cw_ref, u_ref, pcg_ref, qkv_ref, yc_ref, w_v, tail_v, sems):
        step = pl.program_id(0)
        ready = _load_staged(step, [(w_hbm.at[blk], w_v.at[blk]) for blk in range(N_DEV)], sems)

        @pl.when(step == 0)
        def _():
            tail_v[...] = jnp.zeros_like(tail_v)

        u = _rms_fwd_tile(h_ref[...], g_ref[...]).astype(BF16)
        u_ref[...] = u
        for blk in range(N_DEV):
            ready([blk])
            for s in range(D // PROJ_SUB):
                lo, hi = s * PROJ_SUB, (s + 1) * PROJ_SUB
                p = _dot(u, w_v[blk, :, lo:hi])
                if blk < 3:
                    pcg_ref[:, blk * D + lo:blk * D + hi] = p
                elif blk < 6:
                    qkv_ref[:, (blk - 3) * D + lo:(blk - 3) * D + hi] = p.astype(BF16)
                else:
                    pcg_ref[:, (blk - 3) * D + lo:(blk - 3) * D + hi] = p
        xc = pcg_ref[:, D:2 * D] * pcg_ref[:, 2 * D:3 * D]
        ext = jnp.concatenate([tail_v[...], xc], axis=0)
        conv = (cw_ref[0:1, :] * pltpu.roll(ext, 2, 0)[8:] + cw_ref[1:2, :] * pltpu.roll(ext, 1, 0)[8:]
                + cw_ref[2:3, :] * xc)
        yc_ref[...] = (pcg_ref[:, 0:D] * conv).astype(BF16)
        tail_v[...] = xc[TM - 8:]

    return _call(
        body, (h, g, w_in, conv_w), grid=(t // TM,), name=name, comm=comm,
        in_specs=[_row_spec(TM, D), _const_spec((1, D)), _ANY, _const_spec((CONV_K, D))],
        out_specs=[_row_spec(TM, D), _row_spec(TM, PCG_W), _row_spec(TM, QKV_W), _row_spec(TM, D)],
        out_shape=[jax.ShapeDtypeStruct((t, D), BF16), jax.ShapeDtypeStruct((t, PCG_W), F32),
                   jax.ShapeDtypeStruct((t, QKV_W), BF16), jax.ShapeDtypeStruct((t, D), BF16)],
        scratch_shapes=[pltpu.VMEM((N_DEV, D, D), BF16), pltpu.VMEM((8, D), F32),
                        pltpu.SemaphoreType.DMA((N_DEV,))])


def _tri2(cond):
    rr = lax.broadcasted_iota(jnp.int32, (2 * TK, TK), 0) & (TK - 1)
    cc = lax.broadcasted_iota(jnp.int32, (2 * TK, TK), 1)
    return cond(rr, cc).astype(BF16)


def _causal(shift, row0=0):
    rr = lax.broadcasted_iota(jnp.int32, (TQ - row0, TK), 0) + row0
    cc = lax.broadcasted_iota(jnp.int32, (TQ - row0, TK), 1)
    return cc + shift < rr


def _cumdot(v, tri2):
    hi = v.astype(BF16)
    lo = (v - hi.astype(F32)).astype(BF16)
    return _dot(jnp.concatenate([hi, lo], axis=1), tri2)


def _log_1m_beta(z):
    return -(jnp.maximum(z, 0.0) + jnp.log(1.0 + jnp.exp(-jnp.abs(z))))


def _sb_specs(t):
    g = SB_H // SB_HPS
    w = SB_HPS * SB_DH
    q_spec = pl.BlockSpec((TQ, w), lambda h, i: (i, h))
    k_spec = pl.BlockSpec((t, w), lambda h, i: (0, g + h))
    v_spec = pl.BlockSpec((t, w), lambda h, i: (0, 2 * g + h))
    ct_spec = pl.BlockSpec((SB_HPS, TQ, 1), lambda h, i: (h, i, 0))
    return g, w, q_spec, k_spec, v_spec, ct_spec


def _sb_fwd(qkv, name, comm=None):
    t = qkv.shape[0]
    scale = SB_DH ** -0.5
    g, w, q_spec, k_spec, v_spec, ct_spec = _sb_specs(t)

    def body(q_ref, k_ref, v_ref, y_ref, ct_ref):
        i = pl.program_id(1)
        later = _tri2(lambda j, s: j > s)
        n_diag = TQ // TK

        def block(j, carry, shift):
            off = pl.multiple_of(j * TK, TK)
            zs, ms = [], []
            for hd in range(SB_HPS):
                cols = slice(hd * SB_DH, (hd + 1) * SB_DH)
                z = _dot_nt(q_ref[:, cols], k_ref[pl.ds(off, TK), cols]) * scale
                m = _log_1m_beta(z)
                if shift is not None:
                    m = jnp.where(_causal(shift), m, 0.0)
                zs.append(z)
                ms.append(m)
            after = _cumdot(jnp.concatenate(ms, axis=0), later)
            out = []
            for hd in range(SB_HPS):
                acc, c_sum = carry[hd]
                cols = slice(hd * SB_DH, (hd + 1) * SB_DH)
                a = jnp.exp((ms[hd] + zs[hd]) + (c_sum + after[hd * TQ:(hd + 1) * TQ]))
                if shift is not None:
                    a = jnp.where(_causal(shift), a, 0.0)
                out.append((acc + _dot(a.astype(BF16), v_ref[pl.ds(off, TK), cols]),
                            c_sum + jnp.sum(ms[hd], axis=1, keepdims=True)))
            return tuple(out)

        carry = tuple((jnp.zeros((TQ, SB_DH), F32), jnp.zeros((TQ, 1), F32)) for _ in range(SB_HPS))
        for d in reversed(range(n_diag)):
            carry = block(i * n_diag + d, carry, d * TK)
        carry = lax.fori_loop(0, i * n_diag, lambda jj, c: block(i * n_diag - 1 - jj, c, None), carry)
        for hd in range(SB_HPS):
            y_ref[:, hd * SB_DH:(hd + 1) * SB_DH] = carry[hd][0].astype(BF16)
            ct_ref[hd] = carry[hd][1]

    return _call(
        body, (qkv, qkv, qkv), grid=(g, t // TQ), name=name, comm=comm,
        in_specs=[q_spec, k_spec, v_spec],
        out_specs=[q_spec, ct_spec],
        out_shape=[jax.ShapeDtypeStruct((t, D), BF16), jax.ShapeDtypeStruct((SB_H, t, 1), F32)],
        scratch_shapes=[])


def _sb_bwd(qkv, dy, ctot, after, name, comm=None):
    t = qkv.shape[0]
    scale = SB_DH ** -0.5
    g, w, q_spec, k_spec, v_spec, ct_spec = _sb_specs(t)
    acc_spec = pl.BlockSpec((2, t, w), lambda h, i: (0, 0, h))

    def body(q_ref, k_ref, v_ref, dy_ref, ct_ref, after_ref, dq_ref, dkv_ref):
        i = pl.program_id(1)

        @pl.when(i == 0)
        def _():
            dkv_ref[...] = jnp.zeros_like(dkv_ref)

        upto = _tri2(lambda j, s: j <= s)
        n_diag = TQ // TK

        def block(j, carry, shift):
            off = pl.multiple_of(j * TK, TK)
            r0 = 0 if shift is None else shift
            nr = TQ - r0
            causal = None if shift is None else _causal(shift, r0)

            def grow(old, delta):
                return old + delta if r0 == 0 else jnp.concatenate([old[:r0], old[r0:] + delta], axis=0)

            zs, ms = [], []
            for hd in range(SB_HPS):
                cols = slice(hd * SB_DH, (hd + 1) * SB_DH)
                z = _dot_nt(q_ref[r0:, cols], k_ref[pl.ds(off, TK), cols]) * scale
                m = _log_1m_beta(z)
                if causal is not None:
                    m = jnp.where(causal, m, 0.0)
                zs.append(z)
                ms.append(m)
            m_upto = _cumdot(jnp.concatenate(ms, axis=0), upto)
            ls, a_s, es = [], [], []
            for hd in range(SB_HPS):
                cols = slice(hd * SB_DH, (hd + 1) * SB_DH)
                l = ms[hd] + zs[hd]
                a = jnp.exp(l + ((ct_ref[hd, r0:] - carry[hd][1][r0:]) - m_upto[hd * nr:(hd + 1) * nr]))
                if causal is not None:
                    a = jnp.where(causal, a, 0.0)
                ls.append(l)
                a_s.append(a)
                es.append(_dot_nt(dy_ref[r0:, cols], v_ref[pl.ds(off, TK), cols]) * a)
            e_upto = _dot(jnp.concatenate(es, axis=0).astype(BF16), upto[:TK])
            out = []
            for hd in range(SB_HPS):
                dq, p_sum, e_sum = carry[hd]
                cols = slice(hd * SB_DH, (hd + 1) * SB_DH)
                e = es[hd]
                dz = e - jnp.exp(ls[hd]) * (e_sum[r0:] + e_upto[hd * nr:(hd + 1) * nr])
                if causal is not None:
                    dz = jnp.where(causal, dz, 0.0)
                dzs = (dz * scale).astype(BF16)
                dkv_ref[0, pl.ds(off, TK), cols] += _dot_tn(dzs, q_ref[r0:, cols])
                dkv_ref[1, pl.ds(off, TK), cols] += _dot_tn(a_s[hd].astype(BF16), dy_ref[r0:, cols])
                out.append((grow(dq, _dot(dzs, k_ref[pl.ds(off, TK), cols])),
                            grow(p_sum, jnp.sum(ms[hd], axis=1, keepdims=True)),
                            grow(e_sum, jnp.sum(e, axis=1, keepdims=True))))
            return tuple(out)

        zero = jnp.zeros((TQ, 1), F32)
        init = tuple((jnp.zeros((TQ, SB_DH), F32), zero, zero) for _ in range(SB_HPS))
        carry = lax.fori_loop(0, i * n_diag, lambda j, c: block(j, c, None), init)
        for d in range(n_diag):
            carry = block(i * n_diag + d, carry, d * TK)
        for hd in range(SB_HPS):
            dq_ref[:, hd * SB_DH:(hd + 1) * SB_DH] = carry[hd][0].astype(BF16)

    return _call(
        body, (qkv, qkv, qkv, dy, ctot, after), grid=(g, t // TQ), name=name, comm=comm,
        in_specs=[q_spec, k_spec, v_spec, q_spec, ct_spec, pl.BlockSpec(after.shape, lambda h, i: (0, 0))],
        out_specs=[q_spec, acc_spec],
        out_shape=[jax.ShapeDtypeStruct((t, D), BF16), jax.ShapeDtypeStruct((2, t, D), F32)],
        scratch_shapes=[])


def _gate_specs():
    return [pl.BlockSpec((TM, D), lambda i: (i, 3)), pl.BlockSpec((TM, D), lambda i: (i, 4))]


def _mix_pairs(mix_hbm, dsts):
    pairs = []
    for index, dst in enumerate(dsts):
        pairs += _square_pairs(mix_hbm, index, dst)
    return pairs


def _mix_out_fwd(yc, ysb, pcg, b_gate, h, w_mix, name, comm=None):
    t = h.shape[0]

    def body(yc_ref, ysb_ref, gc_ref, gs_ref, b_ref, h_ref, mix_hbm,
             a_ref, b_out_ref, mg_ref, h2_ref, wc_v, wa_v, wo_v, sems):
        _load_resident(pl.program_id(0), _mix_pairs(mix_hbm, (wc_v, wa_v, wo_v)), sems)
        a = _dot(yc_ref[...], wc_v[...])
        b = _dot(ysb_ref[...], wa_v[...])
        merged = (_sigmoid(gc_ref[...] + b_ref[:, :D]) * a + _sigmoid(gs_ref[...] + b_ref[:, D:]) * b).astype(BF16)
        a_ref[...] = a
        b_out_ref[...] = b
        mg_ref[...] = merged
        h2_ref[...] = h_ref[...] + _dot(merged, wo_v[...])

    return _call(
        body, (yc, ysb, pcg, pcg, b_gate, h, w_mix), grid=(t // TM,), name=name, comm=comm,
        in_specs=[_row_spec(TM, D), _row_spec(TM, D)] + _gate_specs()
                 + [_const_spec((1, 2 * D)), _row_spec(TM, D), _ANY],
        out_specs=[_row_spec(TM, D)] * 4,
        out_shape=[jax.ShapeDtypeStruct((t, D), F32), jax.ShapeDtypeStruct((t, D), F32),
                   jax.ShapeDtypeStruct((t, D), BF16), jax.ShapeDtypeStruct((t, D), F32)],
        scratch_shapes=[pltpu.VMEM((D, D), BF16)] * 3 + [pltpu.SemaphoreType.DMA((3 * N_DEV,))])


def _mix_out_bwd(dh2, a, b, pcg, b_gate, conv_w, w_mix, name, comm=None):
    t = dh2.shape[0]
    n_tile = t // TM
    per8 = TM // 8

    def rows(n):
        return pl.BlockSpec((TM, n), lambda i: (n_tile - 1 - i, 0))

    def cols(block):
        return pl.BlockSpec((TM, D), lambda i: (n_tile - 1 - i, block))

    def before(block):
        return pl.BlockSpec((8, D), lambda i: (jnp.maximum((n_tile - 1 - i) * per8 - 1, 0), block))

    def body(dh_ref, a_ref, b_ref, gc_ref, gs_ref, cb_ref, cc_ref, cx_ref, ccp_ref, cxp_ref, bias_ref, cw_ref, mix_hbm,
             dhb_ref, da_ref, db_ref, dgp_ref, dc_ref, dysb_ref, dbias_ref, dcw_ref, wc_v, wa_v, wo_v, head_v, sems):
        step = pl.program_id(0)
        _load_resident(step, _mix_pairs(mix_hbm, (wc_v, wa_v, wo_v)), sems)

        @pl.when(step == 0)
        def _():
            head_v[...] = jnp.zeros_like(head_v)
            dcw_ref[...] = jnp.zeros_like(dcw_ref)

        dhb = dh_ref[...].astype(BF16)
        dhb_ref[...] = dhb
        dm = _dot_nt(dhb, wo_v[...])
        gc = _sigmoid(gc_ref[...] + bias_ref[:, :D])
        gs = _sigmoid(gs_ref[...] + bias_ref[:, D:])
        da = (dm * gc).astype(BF16)
        db = (dm * gs).astype(BF16)
        da_ref[...] = da
        db_ref[...] = db
        dgc = dm * a_ref[...] * (gc * (1.0 - gc))
        dgs = dm * b_ref[...] * (gs * (1.0 - gs))
        dgp_ref[0] = dgc.astype(BF16)
        dgp_ref[1] = dgs.astype(BF16)
        _accumulate(dbias_ref.at[:, :D], step, jnp.sum(dgc, axis=0, keepdims=True))
        _accumulate(dbias_ref.at[:, D:], step, jnp.sum(dgs, axis=0, keepdims=True))
        dysb_ref[...] = _dot_nt(db, wa_v[...]).astype(BF16)
        dyc = _dot_nt(da, wc_v[...])
        cc, cx = cc_ref[...], cx_ref[...]
        xc = cc * cx
        xc_before = jnp.where(step == n_tile - 1, 0.0, ccp_ref[...] * cxp_ref[...])
        ext = jnp.concatenate([xc_before, xc], axis=0)
        x1 = pltpu.roll(ext, 1, 0)[8:]
        x2 = pltpu.roll(ext, 2, 0)[8:]
        w0, w1, w2 = cw_ref[0:1, :], cw_ref[1:2, :], cw_ref[2:3, :]
        dc_ref[0] = (dyc * (w0 * x2 + w1 * x1 + w2 * xc)).astype(BF16)
        dconv = dyc * cb_ref[...]
        dcw_ref[0:1, :] += jnp.sum(dconv * x2, axis=0, keepdims=True)
        dcw_ref[1:2, :] += jnp.sum(dconv * x1, axis=0, keepdims=True)
        dcw_ref[2:3, :] += jnp.sum(dconv * xc, axis=0, keepdims=True)
        after = jnp.concatenate([dconv, head_v[...]], axis=0)
        dxc = w2 * dconv + w1 * pltpu.roll(after, TM + 7, 0)[:TM] + w0 * pltpu.roll(after, TM + 6, 0)[:TM]
        dc_ref[1] = (dxc * cx).astype(BF16)
        dc_ref[2] = (dxc * cc).astype(BF16)
        head_v[...] = dconv[:8]

    return _call(
        body, (dh2, a, b, pcg, pcg, pcg, pcg, pcg, pcg, pcg, b_gate, conv_w, w_mix), grid=(n_tile,), name=name,
        comm=comm,
        in_specs=[rows(D)] * 3 + [cols(3), cols(4), cols(0), cols(1), cols(2), before(1), before(2),
                                  _const_spec((1, 2 * D)), _const_spec((CONV_K, D)), _ANY],
        out_specs=[rows(D)] * 3 + [pl.BlockSpec((2, TM, D), lambda i: (0, n_tile - 1 - i, 0)),
                                   pl.BlockSpec((3, TM, D), lambda i: (0, n_tile - 1 - i, 0)), rows(D),
                                   _const_spec((1, 2 * D)), _const_spec((8, D))],
        out_shape=[jax.ShapeDtypeStruct((t, D), BF16)] * 3
                  + [jax.ShapeDtypeStruct((2, t, D), BF16), jax.ShapeDtypeStruct((3, t, D), BF16),
                     jax.ShapeDtypeStruct((t, D), BF16), jax.ShapeDtypeStruct((1, 2 * D), F32),
                     jax.ShapeDtypeStruct((8, D), F32)],
        scratch_shapes=[pltpu.VMEM((D, D), BF16)] * 3 + [pltpu.VMEM((8, D), F32),
                                                         pltpu.SemaphoreType.DMA((3 * N_DEV,))])


def _inproj_bwd(dconv, dq, dkv, dgp, w_in, h, g, dh_res, name, comm=None):
    t = h.shape[0]

    def body(dc_ref, dq_ref, dkv_ref, dgp_ref, w_hbm, h_ref, g_ref, dres_ref, dh_ref, dg_ref, w_v, sems):
        step = pl.program_id(0)
        ready = _load_staged(step, [(w_hbm.at[blk], w_v.at[blk]) for blk in range(N_DEV)], sems)
        ready([0])
        du = _dot_nt(dc_ref[0], w_v[0])
        for k in range(1, 3):
            ready([k])
            du = du + _dot_nt(dc_ref[k], w_v[k])
        ready([3])
        du = du + _dot_nt(dq_ref[...], w_v[3])
        for k in range(2):
            ready([4 + k])
            du = du + _dot_nt(dkv_ref[k].astype(BF16), w_v[4 + k])
        for k in range(2):
            ready([6 + k])
            du = du + _dot_nt(dgp_ref[k], w_v[6 + k])
        dx, dg = _rms_bwd_tile(h_ref[...], g_ref[...], du)
        dh_ref[...] = dres_ref[...] + dx
        _accumulate(dg_ref, step, dg)

    return _call(
        body, (dconv, dq, dkv, dgp, w_in, h, g, dh_res), grid=(t // TM,), name=name, comm=comm,
        in_specs=[_blk_row_spec(3, TM, D), _row_spec(TM, D), _blk_row_spec(2, TM, D), _blk_row_spec(2, TM, D), _ANY,
                  _row_spec(TM, D), _const_spec((1, D)), _row_spec(TM, D)],
        out_specs=[_row_spec(TM, D), _const_spec((1, D))],
        out_shape=[jax.ShapeDtypeStruct((t, D), F32), jax.ShapeDtypeStruct((1, D), F32)],
        scratch_shapes=[pltpu.VMEM((N_DEV, D, D), BF16), pltpu.SemaphoreType.DMA((N_DEV,))])


def _memkv_fwd(mem, g, w_ckv, name):
    m = mem.shape[0]

    def body(mem_ref, g_ref, w_ref, mn_ref, kv_ref):
        mn = _rms_fwd_tile(mem_ref[...], g_ref[...]).astype(BF16)
        mn_ref[...] = mn
        for j in range(N_DEV):
            kv_ref[j] = _dot(mn, w_ref[j]).astype(BF16)

    return pl.pallas_call(
        body, grid=(1,), name=name,
        in_specs=[_const_spec((m, D)), _const_spec((1, D)), _const_spec((N_DEV, D, X_DH))],
        out_specs=[_const_spec((m, D)), _const_spec((N_DEV, m, X_DH))],
        out_shape=[jax.ShapeDtypeStruct((m, D), BF16), jax.ShapeDtypeStruct((N_DEV, m, X_DH), BF16)],
        compiler_params=_cparams(),
    )(mem, g, w_ckv)


def _memkv_bwd(dkv, mem, g, w_ckv, name):
    m = mem.shape[0]

    def body(dkv_ref, mem_ref, g_ref, w_ref, dg_ref):
        dmn = jnp.zeros((m, D), F32)
        for j in range(N_DEV):
            dmn = dmn + _dot_nt(dkv_ref[j].astype(BF16), w_ref[j])
        _, dg = _rms_bwd_tile(mem_ref[...], g_ref[...], dmn)
        dg_ref[...] = dg

    return pl.pallas_call(
        body, grid=(1,), name=name,
        in_specs=[_const_spec((N_DEV, m, X_DH)), _const_spec((m, D)), _const_spec((1, D)),
                  _const_spec((N_DEV, D, X_DH))],
        out_specs=_const_spec((1, D)),
        out_shape=jax.ShapeDtypeStruct((1, D), F32),
        compiler_params=_cparams(),
    )(dkv, mem, g, w_ckv)


def _softmax_rows(s):
    e = jnp.exp(s - jnp.max(s, axis=-1, keepdims=True))
    return e / jnp.sum(e, axis=-1, keepdims=True)


def _cross_pairs(cross_hbm, wq_v, wo_v):
    return _square_pairs(cross_hbm, 0, wq_v) + _square_pairs(cross_hbm, 1, wo_v)


def _cross_fwd(h, g, kv, w_cross, name):
    t = h.shape[0]
    m = kv.shape[1]
    scale = X_DH ** -0.5

    def body(h_ref, g_ref, kv_ref, cross_hbm, hn_ref, qx_ref, o_ref, h3_ref, wq_v, wo_v, sems):
        _load_resident(pl.program_id(0), _cross_pairs(cross_hbm, wq_v, wo_v), sems)
        ht = h_ref[...]
        hn = _rms_fwd_tile(ht, g_ref[...]).astype(BF16)
        hn_ref[...] = hn
        qx = _dot(hn, wq_v[...]).astype(BF16)
        qx_ref[...] = qx
        for hd in range(X_H):
            lo, hi = hd * X_DH, (hd + 1) * X_DH
            p = _softmax_rows(_dot_nt(qx[:, lo:hi], kv_ref[hd]) * scale)
            o_ref[:, lo:hi] = _dot(p.astype(BF16), kv_ref[X_H + hd]).astype(BF16)
        h3_ref[...] = ht + _dot(o_ref[...], wo_v[...])

    return pl.pallas_call(
        body, grid=(t // TM,), name=name,
        in_specs=[_row_spec(TM, D), _const_spec((1, D)), _const_spec((N_DEV, m, X_DH)), _ANY],
        out_specs=[_row_spec(TM, D)] * 4,
        out_shape=[jax.ShapeDtypeStruct((t, D), BF16)] * 3 + [jax.ShapeDtypeStruct((t, D), F32)],
        scratch_shapes=[pltpu.VMEM((D, D), BF16)] * 2 + [pltpu.SemaphoreType.DMA((2 * N_DEV,))],
        compiler_params=_cparams(),
    )(h, g, kv, w_cross)


def _cross_bwd(dh3, h, g, qx, kv, w_cross, name, comm=None):
    t = h.shape[0]
    m = kv.shape[1]
    scale = X_DH ** -0.5

    def body(dh_ref, h_ref, g_ref, qx_ref, kv_ref, cross_hbm,
             dhb_ref, dqx_ref, dkv_ref, dh2_ref, dg_ref, wq_v, wo_v, sems):
        step = pl.program_id(0)
        _load_resident(step, _cross_pairs(cross_hbm, wq_v, wo_v), sems)

        @pl.when(step == 0)
        def _():
            dkv_ref[...] = jnp.zeros_like(dkv_ref)

        dht = dh_ref[...]
        dhb = dht.astype(BF16)
        dhb_ref[...] = dhb
        do = _dot_nt(dhb, wo_v[...]).astype(BF16)
        for hd in range(X_H):
            lo, hi = hd * X_DH, (hd + 1) * X_DH
            qh = qx_ref[:, lo:hi]
            kh = kv_ref[hd]
            p = _softmax_rows(_dot_nt(qh, kh) * scale)
            doh = do[:, lo:hi]
            dp = _dot_nt(doh, kv_ref[X_H + hd])
            ds = (p * (dp - jnp.sum(dp * p, axis=-1, keepdims=True)) * scale).astype(BF16)
            dqx_ref[:, lo:hi] = _dot(ds, kh).astype(BF16)
            dkv_ref[hd] += _dot_tn(ds, qh)
            dkv_ref[X_H + hd] += _dot_tn(p.astype(BF16), doh)
        dhn = _dot_nt(dqx_ref[...], wq_v[...])
        dx, dg = _rms_bwd_tile(h_ref[...], g_ref[...], dhn)
        dh2_ref[...] = dht + dx
        _accumulate(dg_ref, step, dg)

    return _call(
        body, (dh3, h, g, qx, kv, w_cross), grid=(t // TM,), name=name, comm=comm,
        in_specs=[_row_spec(TM, D), _row_spec(TM, D), _const_spec((1, D)), _row_spec(TM, D),
                  _const_spec((N_DEV, m, X_DH)), _ANY],
        out_specs=[_row_spec(TM, D), _row_spec(TM, D), _const_spec((N_DEV, m, X_DH)), _row_spec(TM, D),
                   _const_spec((1, D))],
        out_shape=[jax.ShapeDtypeStruct((t, D), BF16), jax.ShapeDtypeStruct((t, D), BF16),
                   jax.ShapeDtypeStruct((N_DEV, m, X_DH), F32), jax.ShapeDtypeStruct((t, D), F32),
                   jax.ShapeDtypeStruct((1, D), F32)],
        scratch_shapes=[pltpu.VMEM((D, D), BF16)] * 2 + [pltpu.SemaphoreType.DMA((2 * N_DEV,))])


def _adamw(w, parts, m, v, name, row_block=0, token=None):
    r, c = w.shape
    n = parts.shape[0]
    tr = _pick_tile(r, (256, 352, 128))
    off = row_block * (r // tr)

    def body(*refs):
        if token is None:
            _adamw_update(None, *refs)
        else:
            _adamw_update(refs[4], *refs[:4], *refs[5:])

    spec = _row_spec(tr, c)
    in_specs = [spec, pl.BlockSpec((n, tr, c), lambda i: (0, i + off, 0)), spec, spec]
    operands = (w, parts, m, v)
    if token is not None:
        in_specs.append(_const_spec(token.shape))
        operands += (token,)
    return pl.pallas_call(
        body, grid=(r // tr,), name=name, in_specs=in_specs, out_specs=[spec] * 4,
        out_shape=[jax.ShapeDtypeStruct((r, c), F32)] * 4,
        compiler_params=_cparams(),
    )(*operands)


def _adamw_update(tok_ref, w_ref, p_ref, m_ref, v_ref, g_ref, d_ref, nm_ref, nv_ref):
    gt = p_ref[0].astype(F32)
    for k in range(1, p_ref.shape[0]):
        gt = gt + p_ref[k].astype(F32)
    if tok_ref is not None:
        gt = gt + tok_ref[0:1, 0:1]
    _adamw_apply(gt, w_ref, m_ref, v_ref, g_ref, d_ref, nm_ref, nv_ref)


def _adamw_own(w, land, own, chip, m, v, name, row_block=0, token=None):
    r, c = w.shape
    tr = _pick_tile(r, (256, 352, 128))
    off = row_block * (r // tr)

    def body(chip_ref, w_ref, land_ref, own_ref, m_ref, v_ref, *rest):
        mine = own_ref[0].astype(F32)
        gt = jnp.where(chip_ref[0] == 0, mine, land_ref[0].astype(F32))
        for k in range(1, N_CHIP):
            gt = gt + jnp.where(chip_ref[0] == k, mine, land_ref[k].astype(F32))
        if token is not None:
            gt = gt + rest[0][0:1, 0:1]
        _adamw_apply(gt, w_ref, m_ref, v_ref, *rest[-4:])

    spec = pl.BlockSpec((tr, c), lambda i, chip_ref: (i, 0))
    in_specs = [spec, pl.BlockSpec((N_CHIP, tr, c), lambda i, chip_ref: (0, i + off, 0)),
                pl.BlockSpec((1, tr, c), lambda i, chip_ref: (chip_ref[0], i + off, 0)), spec, spec]
    operands = (chip, w, land, own, m, v)
    if token is not None:
        in_specs.append(pl.BlockSpec(token.shape, lambda i, chip_ref: (0, 0)))
        operands += (token,)
    return pl.pallas_call(
        body, name=name,
        grid_spec=pltpu.PrefetchScalarGridSpec(
            num_scalar_prefetch=1, grid=(r // tr,), in_specs=in_specs, out_specs=[spec] * 4),
        out_shape=[jax.ShapeDtypeStruct((r, c), F32)] * 4,
        compiler_params=_cparams(),
    )(*operands)


def _adamw_apply(gt, w_ref, m_ref, v_ref, g_ref, d_ref, nm_ref, nv_ref):
    g_ref[...] = gt
    nm = ADAM_B1 * m_ref[...] + (1.0 - ADAM_B1) * gt
    nv = ADAM_B2 * v_ref[...] + (1.0 - ADAM_B2) * jnp.square(gt)
    m_hat = nm / (1.0 - ADAM_B1 ** ADAM_STEP)
    v_hat = nv / (1.0 - ADAM_B2 ** ADAM_STEP)
    d_ref[...] = -ADAM_LR * (m_hat / (jnp.sqrt(v_hat) + ADAM_EPS) + ADAM_WD * w_ref[...])
    nm_ref[...] = nm
    nv_ref[...] = nv


def _mesh_pos():
    return lax.axis_index("x"), lax.axis_index("y"), lax.axis_index("c")


def _no_round(in_refs, out_refs, sems):
    pass


def _run_exchange(comm, name):
    c_in, c_out = len(comm.inputs), len(comm.out_shapes)

    def body(*refs):
        cins, couts, sems = refs[:c_in], refs[c_in:c_in + c_out], refs[c_in + c_out:]
        comm.start(cins, couts, sems)
        comm.middle(cins, couts, sems)
        comm.finish(cins, couts, sems)

    return list(pl.pallas_call(
        body, name=name, out_shape=list(comm.out_shapes),
        in_specs=[_ANY] * c_in, out_specs=[_ANY] * c_out, scratch_shapes=list(comm.sem_shapes),
    )(*comm.inputs))


def _gather_exchange(shards):
    n_arr = len(shards)

    def plan(x_refs, out_refs, sems):
        send_sems, recv_sems, local_sems = sems[:3]
        stage = sems[3:]
        x, y, c = _mesh_pos()
        me, sibling = (x, y, c), (x, y, 1 - c)
        xn, yn, diag = (1 - x, y), (x, 1 - y), (1 - x, 1 - y)

        def slot(a, px, py, pc, half=None):
            ref = out_refs[a].at[4 * px + 2 * py + pc]
            if half is None:
                return ref
            rows = shards[a].shape[0] // 2
            return ref.at[half * rows:(half + 1) * rows]

        def copy(a, k, block, to, half=None, src=None):
            dst = slot(a, *block, half)
            return pltpu.make_async_remote_copy(
                src_ref=dst if src is None else src, dst_ref=dst,
                send_sem=send_sems.at[a, k], recv_sem=recv_sems.at[a, k],
                device_id=to, device_id_type=pl.DeviceIdType.MESH)

        return types.SimpleNamespace(
            me=me, sibling=sibling, xn=xn, yn=yn, diag=diag, c=c, copy=copy,
            mine_in=[pltpu.make_async_copy(x_refs[a], stage[a], local_sems.at[a, 0]) for a in range(n_arr)],
            mine_out=[pltpu.make_async_copy(stage[a], slot(a, *me), local_sems.at[a, 1]) for a in range(n_arr)],
            first=[cp for a in range(n_arr) for cp in (
                copy(a, 0, me, sibling, src=x_refs[a]), copy(a, 1, me, (*xn, c), src=x_refs[a]),
                copy(a, 2, me, (*yn, c), src=x_refs[a]))],
            second=lambda a: (copy(a, 3, (*xn, c), (*yn, c), half=0), copy(a, 5, (*xn, c), sibling),
                              copy(a, 4, (*yn, c), (*xn, c), half=1), copy(a, 6, (*yn, c), sibling)),
            third=lambda a: (copy(a, 7, (*diag, c), sibling, half=0), copy(a, 8, (*diag, c), sibling, half=1)))

    def start(x_refs, out_refs, sems):
        p = plan(x_refs, out_refs, sems)
        for cp in p.first + p.mine_in:
            cp.start()
        for cp_in, cp_out in zip(p.mine_in, p.mine_out):
            cp_in.wait()
            cp_out.start()

    def middle(x_refs, out_refs, sems):
        p = plan(x_refs, out_refs, sems)
        for a in range(n_arr):
            to_yn, x_to_sib, to_xn, y_to_sib = p.second(a)
            p.copy(a, 1, (*p.xn, p.c), p.me).wait_recv()
            to_yn.start()
            x_to_sib.start()
            p.copy(a, 2, (*p.yn, p.c), p.me).wait_recv()
            to_xn.start()
            y_to_sib.start()

    def finish(x_refs, out_refs, sems):
        p = plan(x_refs, out_refs, sems)
        for a in range(n_arr):
            half0_to_sib, half1_to_sib = p.third(a)
            p.copy(a, 3, (*p.diag, p.c), p.me, half=0).wait_recv()
            half0_to_sib.start()
            p.copy(a, 4, (*p.diag, p.c), p.me, half=1).wait_recv()
            half1_to_sib.start()
        other = 1 - p.c
        for a in range(n_arr):
            p.copy(a, 0, p.sibling, p.me).wait_recv()
            p.copy(a, 5, (*p.xn, other), p.me).wait_recv()
            p.copy(a, 6, (*p.yn, other), p.me).wait_recv()
            p.copy(a, 7, (*p.diag, other), p.me, half=0).wait_recv()
            p.copy(a, 8, (*p.diag, other), p.me, half=1).wait_recv()
        for cp in p.first:
            cp.wait_send()
        for a in range(n_arr):
            for cp in p.second(a) + p.third(a):
                cp.wait_send()
        for cp in p.mine_out:
            cp.wait()

    return types.SimpleNamespace(
        inputs=list(shards), start=start, middle=middle, finish=finish,
        out_shapes=[jax.ShapeDtypeStruct((N_DEV,) + s.shape, s.dtype) for s in shards],
        sem_shapes=[pltpu.SemaphoreType.DMA((n_arr, 9)), pltpu.SemaphoreType.DMA((n_arr, 9)),
                    pltpu.SemaphoreType.DMA((n_arr, 2))] + [pltpu.VMEM(s.shape, s.dtype) for s in shards])


def _pair_exchange(grads):
    n_arr = len(grads)

    def plan(g_refs, land_refs, sems):
        send_sems, recv_sems = sems
        x, y, c = _mesh_pos()
        return [pltpu.make_async_remote_copy(
            src_ref=g_refs[a].at[2 * k + 1 - c], dst_ref=land_refs[a].at[k],
            send_sem=send_sems.at[a, k], recv_sem=recv_sems.at[a, k],
            device_id=(x, y, 1 - c), device_id_type=pl.DeviceIdType.MESH)
            for a in range(n_arr) for k in range(N_CHIP)]

    def start(g_refs, land_refs, sems):
        for cp in plan(g_refs, land_refs, sems):
            cp.start()

    def finish(g_refs, land_refs, sems):
        for cp in plan(g_refs, land_refs, sems):
            cp.wait()

    return types.SimpleNamespace(
        inputs=list(grads), start=start, middle=_no_round, finish=finish,
        out_shapes=[jax.ShapeDtypeStruct((N_CHIP,) + g.shape[1:], g.dtype) for g in grads],
        sem_shapes=[pltpu.SemaphoreType.DMA((n_arr, N_CHIP)), pltpu.SemaphoreType.DMA((n_arr, N_CHIP))])


def _chip_exchange(parts):
    n_arr = len(parts)

    def plan(p_refs, land_refs, sems):
        send_sems, recv_sems, local_sems = sems
        x, y, c = _mesh_pos()
        my_chip = 2 * x + y
        chips = [(1 - x, y), (x, 1 - y), (1 - x, 1 - y)]
        local = [pltpu.make_async_copy(p_refs[a].at[my_chip], land_refs[a].at[my_chip], local_sems.at[a])
                 for a in range(n_arr)]

        def copy(a, k, src_slot, dst_slot, px, py):
            return pltpu.make_async_remote_copy(
                src_ref=p_refs[a].at[src_slot], dst_ref=land_refs[a].at[dst_slot],
                send_sem=send_sems.at[a, k], recv_sem=recv_sems.at[a, k],
                device_id=(px, py, c), device_id_type=pl.DeviceIdType.MESH)

        sends = [copy(a, k, 2 * px + py, my_chip, px, py) for a in range(n_arr) for k, (px, py) in enumerate(chips)]
        arrivals = [copy(a, k, my_chip, 2 * px + py, px, py) for a in range(n_arr)
                    for k, (px, py) in enumerate(chips)]
        return local, sends, arrivals

    def start(p_refs, land_refs, sems):
        local, sends, _ = plan(p_refs, land_refs, sems)
        for cp in local + sends:
            cp.start()

    def finish(p_refs, land_refs, sems):
        local, sends, arrivals = plan(p_refs, land_refs, sems)
        for cp in arrivals:
            cp.wait_recv()
        for cp in sends:
            cp.wait_send()
        for cp in local:
            cp.wait()

    return types.SimpleNamespace(
        inputs=list(parts), start=start, middle=_no_round, finish=finish,
        out_shapes=[jax.ShapeDtypeStruct(p.shape, p.dtype) for p in parts],
        sem_shapes=[pltpu.SemaphoreType.DMA((n_arr, 3)), pltpu.SemaphoreType.DMA((n_arr, 3)),
                    pltpu.SemaphoreType.DMA((n_arr,))])


_HBM = pl.BlockSpec(memory_space=pltpu.HBM)
_SEM = pl.BlockSpec(memory_space=pltpu.SEMAPHORE)
_DATAFLOW = pltpu.SideEffectType.DATAFLOW_SIDE_EFFECTING


def _chip_copies(p_refs, land_refs, send_sems, recv_sems):
    x, y, c = _mesh_pos()
    my_chip = 2 * x + y
    chips = [(1 - x, y), (x, 1 - y), (1 - x, 1 - y)]
    return [pltpu.make_async_remote_copy(
        src_ref=p_refs[a].at[2 * px + py], dst_ref=land_refs[a].at[my_chip],
        send_sem=send_sems[3 * a + k], recv_sem=recv_sems[3 * a + k],
        device_id=(px, py, c), device_id_type=pl.DeviceIdType.MESH)
        for a in range(len(p_refs)) for k, (px, py) in enumerate(chips)]


def _chip_exchange_begin(parts, name):
    n_arr = len(parts)
    n_buf, n_copy = 2 * n_arr, 3 * n_arr
    lands = [lax.empty(p.shape, p.dtype) for p in parts]

    def body(*refs):
        p_refs, land_refs = refs[:n_arr], refs[n_arr:n_buf]
        send_sems, recv_sems, token = refs[n_buf:n_buf + n_copy], refs[n_buf + n_copy:n_buf + 2 * n_copy], refs[-1]
        for cp in _chip_copies(p_refs, land_refs, send_sems, recv_sems):
            cp.start()
        token[...] = jnp.zeros_like(token)

    bufs = list(parts) + list(lands)
    outs = pl.pallas_call(
        body, name=name,
        out_shape=(*[pltpu.SemaphoreType.DMA(())] * (2 * n_copy), *[pltpu.HBM(b.shape, b.dtype) for b in bufs],
                   jax.ShapeDtypeStruct((8, 128), F32)),
        in_specs=[_HBM] * n_buf,
        out_specs=(*[_SEM] * (2 * n_copy), *[_HBM] * n_buf, pl.BlockSpec(memory_space=pltpu.VMEM)),
        input_output_aliases={i: 2 * n_copy + i for i in range(n_buf)},
        compiler_params=pltpu.CompilerParams(has_side_effects=_DATAFLOW),
    )(*[pltpu.with_memory_space_constraint(b, pltpu.HBM) for b in bufs])
    sems = list(outs[:2 * n_copy])
    thru = list(outs[2 * n_copy:2 * n_copy + n_buf])
    return types.SimpleNamespace(send_sems=sems[:n_copy], recv_sems=sems[n_copy:], parts=thru[:n_arr],
                                 lands=thru[n_arr:], token=outs[-1])


def _chip_exchange_end(flight, after, name):
    send_sems, recv_sems, parts, lands = flight.send_sems, flight.recv_sems, flight.parts, flight.lands
    n_arr = len(parts)
    n_buf, n_copy = 2 * n_arr, 3 * n_arr

    def body(*refs):
        p_refs, land_refs = refs[:n_arr], refs[n_arr:n_buf]
        sems = refs[n_buf:n_buf + 2 * n_copy]
        for cp in _chip_copies(p_refs, land_refs, sems[:n_copy], sems[n_copy:]):
            cp.wait_send()
            cp.wait_recv()

    bufs = list(parts) + list(lands)
    outs = pl.pallas_call(
        body, name=name, out_shape=tuple(pltpu.HBM(b.shape, b.dtype) for b in bufs),
        in_specs=[_HBM] * n_buf + [_SEM] * (2 * n_copy) + [_ANY], out_specs=tuple([_HBM] * n_buf),
        input_output_aliases={i: i for i in range(n_buf)},
        compiler_params=pltpu.CompilerParams(has_side_effects=_DATAFLOW),
    )(*bufs, *send_sems, *recv_sems, after)
    return list(outs[:n_arr]), list(outs[n_arr:])


def _row_tile(r, cap=640):
    best = None
    for cand in range(16, min(r, cap) + 1, 16):
        if r % cand == 0:
            best = cand
    return best if best is not None else r


def _pair_sum(g, landed, core, name):
    _, r, c_dim = g.shape
    tr = _row_tile(r)

    def body(core_ref, mine_ref, theirs_ref, o_ref):
        o_ref[0] = (mine_ref[0].astype(F32) + theirs_ref[0].astype(F32)).astype(o_ref.dtype)

    return pl.pallas_call(
        body, name=name,
        grid_spec=pltpu.PrefetchScalarGridSpec(
            num_scalar_prefetch=1, grid=(N_CHIP, r // tr),
            in_specs=[pl.BlockSpec((1, tr, c_dim), lambda k, i, core_ref: (2 * k + core_ref[0], i, 0)),
                      pl.BlockSpec((1, tr, c_dim), lambda k, i, core_ref: (k, i, 0))],
            out_specs=pl.BlockSpec((1, tr, c_dim), lambda k, i, core_ref: (k, i, 0))),
        out_shape=jax.ShapeDtypeStruct((N_CHIP, r, c_dim), g.dtype),
        compiler_params=_cparams(2),
    )(core, g, landed)


def _sum_slots(parts, name):
    n, r, c_dim = parts.shape
    tr = _row_tile(r)

    def body(p_ref, o_ref):
        acc = p_ref[0].astype(F32)
        for k in range(1, n):
            acc = acc + p_ref[k].astype(F32)
        o_ref[...] = acc

    return pl.pallas_call(
        body, grid=(r // tr,), name=name,
        in_specs=[pl.BlockSpec((n, tr, c_dim), lambda i: (0, i, 0))],
        out_specs=_row_spec(tr, c_dim),
        out_shape=jax.ShapeDtypeStruct((r, c_dim), F32),
        compiler_params=_cparams(),
    )(parts)


GAINS = ("g_ffn1", "g_mix", "g_cross", "g_mem", "g_ffn2", "g_final")
SMALL = GAINS + ("b_gate", "conv_w")
SMALL_R = 16
LOSS_ROW = 11
WEIGHT_ORDER = ("g_ffn1", "w_ffn1_gu", "w_ffn1_down", "g_mix", "w_in", "b_gate", "conv_w", "w_conv_out",
                "w_attn_out", "w_o", "g_cross", "g_mem", "w_cq", "w_ckv", "w_co", "g_ffn2", "w_ffn2_gu",
                "w_ffn2_down", "g_final")
GU_NAMES = ("w_ffn1_gu", "w_ffn2_gu")


def _pack_small(vals, conv_rows):
    rows = [vals[n].reshape(1, D) for n in GAINS] + [vals["b_gate"].reshape(2, D), conv_rows.reshape(CONV_K, D)]
    used = len(GAINS) + 2 + CONV_K
    return jnp.concatenate(rows + [jnp.zeros((SMALL_R - used, D), F32)], axis=0)


def _unpack_small(buf):
    out = {n: buf[k] for k, n in enumerate(GAINS)}
    out["b_gate"] = buf[6:8].reshape(2 * D)
    out["conv_w"] = buf[8:8 + CONV_K]
    return out


def _exchange_shards(wts):
    out = {n: jnp.pad(wts[n].T.astype(BF16), ((0, FF_PAD - FF_BLK), (0, 0))) for n in GU_NAMES}
    for n in ("w_ckv", "w_in", "w_ffn1_down", "w_ffn2_down"):
        out[n] = wts[n].astype(BF16)
    out["mix"] = jnp.concatenate([wts[n].astype(BF16) for n in MIX_MATS], axis=0)
    out["cross"] = jnp.concatenate([wts[n].astype(BF16) for n in CROSS_MATS], axis=0)
    return out


def _reduce_group(grads, landed, core, names):
    return [_pair_sum(g, l, core, "grads_pair_sum_" + n) for g, l, n in zip(grads, landed, names)]


def _step(x, mem, target, sh, conv_pad, gains, b_gate, core):
    wg1, wd1, conv_all = _run_exchange(_gather_exchange([sh["w_ffn1_gu"], sh["w_ffn1_down"], conv_pad]), "gather_ffn1")
    conv_w = conv_all[:, :CONV_K, :].transpose(1, 0, 2).reshape(CONV_K, D)
    (n1, gate1, up1, act1, h1), (w_in,) = _ffn_fwd(
        x, gains["g_ffn1"], wg1, wd1, "ffn1_fwd", comm=_gather_exchange([sh["w_in"]]))
    (u, pcg, qkv, yc), (w_mix,) = _inproj_fwd(h1, gains["g_mix"], w_in, conv_w, "inproj_fwd",
                                              comm=_gather_exchange([sh["mix"]]))
    (ysb, ctot), (w_cross, w_ckv, wg2) = _sb_fwd(
        qkv, "sb_fwd", comm=_gather_exchange([sh["cross"], sh["w_ckv"], sh["w_ffn2_gu"]]))
    (a_mix, b_mix, merged, h2), (wd2,) = _mix_out_fwd(yc, ysb, pcg, b_gate, h1, w_mix, "mix_out_fwd",
                                                      comm=_gather_exchange([sh["w_ffn2_down"]]))
    mn, kv = _memkv_fwd(mem, gains["g_mem"], w_ckv, "memkv_fwd")
    hn, qx, o_x, h3 = _cross_fwd(h2, gains["g_cross"], kv, w_cross, "cross_fwd")
    (n4, gate2, up2, act2, dh4, loss, dg_final), _ = _ffn_fwd(h3, gains["g_ffn2"], wg2, wd2, "ffn2_fwd",
                                                              head=(gains["g_final"], target))

    gs = {"g_final": dg_final}
    (dgu2, dh4b, dh3, gs["g_ffn2"]), _ = _ffn_bwd(dh4, h3, gains["g_ffn2"], gate2, up2, wg2, wd2, "ffn2_bwd")
    grads_a = [_mm_tn_rows(dgu2, n4, FF_PAD, "dw_ffn2_gu"),
               _mm_tn_rows(act2, dh4b, FF_BLK, "dw_ffn2_down").reshape(N_DEV, DOWN_ROWS, D)]
    names_a = ["w_ffn2_gu", "w_ffn2_down"]
    (dh3b, dqx, dkv, dh2, gs["g_cross"]), landed_a = _cross_bwd(
        dh3, h2, gains["g_cross"], qx, kv, w_cross, "cross_bwd", comm=_pair_exchange(grads_a))
    sums_a = _reduce_group(grads_a, landed_a, core, names_a)
    cross_stack = _mm_tn_square(hn, dqx, "dw_cq", 0, len(CROSS_MATS))
    grads_b = [_mm_tn_cols(mn, dkv, "dw_ckv"), _mm_tn_square(o_x, dh3b, "dw_co", 1, len(CROSS_MATS), cross_stack)]
    names_b = ["w_ckv", "cross"]
    gs["g_mem"] = _memkv_bwd(dkv, mem, gains["g_mem"], w_ckv, "memkv_bwd")
    (dh2b, da_mix, db_mix, dgp, dconv, dysb, gs["b_gate"], gs["conv_w"]), landed_b = _mix_out_bwd(
        dh2, a_mix, b_mix, pcg, b_gate, conv_w, w_mix, "mix_out_bwd", comm=_pair_exchange(grads_b))
    sums_b = _reduce_group(grads_b, landed_b, core, names_b)
    mix_stack = _mm_tn_square(yc, da_mix, "dw_conv_out", 0, len(MIX_MATS))
    mix_stack = _mm_tn_square(ysb, db_mix, "dw_attn_out", 1, len(MIX_MATS), mix_stack)
    grads_c = [_mm_tn_square(merged, dh2b, "dw_o", 2, len(MIX_MATS), mix_stack)]
    flight_ab = _chip_exchange_begin(sums_a + sums_b, "grads_to_chips_early_begin")
    (dq, dkv_sb), landed_c = _sb_bwd(qkv, dysb, ctot, flight_ab.token, "sb_bwd", comm=_pair_exchange(grads_c))
    sums_c = _reduce_group(grads_c, landed_c, core, ["mix"])
    w_in_stack = _mm_tn_cols(u, dconv, "dw_in_conv", 0, N_DEV)
    w_in_stack = _mm_tn_cols(u, dq[None], "dw_in_q", 3, N_DEV, w_in_stack)
    w_in_stack = _mm_tn_cols(u, dkv_sb, "dw_in_kv", 4, N_DEV, w_in_stack)
    grads_d = [_mm_tn_cols(u, dgp, "dw_in_gates", 6, N_DEV, w_in_stack)]
    (dh1, gs["g_mix"]), landed_d = _inproj_bwd(dconv, dq, dkv_sb, dgp, w_in, h1, gains["g_mix"], dh2, "inproj_bwd",
                                               comm=_pair_exchange(grads_d))
    sums_d = _reduce_group(grads_d, landed_d, core, ["w_in"])
    flight_d = _chip_exchange_begin(sums_c + sums_d, "grads_to_chips_w_in_begin")
    (dgu1, dh1b, dx, gs["g_ffn1"]), _ = _ffn_bwd(dh1, x, gains["g_ffn1"] + flight_d.token[0, 0], gate1, up1, wg1, wd1,
                                                 "ffn1_bwd")
    dw_gu1 = _mm_tn_rows(dgu1, n1, FF_PAD, "dw_ffn1_gu")
    dw_down1, landed_gu1 = _mm_tn_rows(act1, dh1b, FF_BLK, "dw_ffn1_down", comm=_pair_exchange([dw_gu1]))
    grads_e = [dw_gu1, dw_down1.reshape(N_DEV, DOWN_ROWS, D)]
    names_e = ["w_ffn1_gu", "w_ffn1_down"]
    landed_e = landed_gu1 + _run_exchange(_pair_exchange(grads_e[1:]), "grads_to_sibling_ffn1_down")
    flight_e = _chip_exchange_begin(_reduce_group(grads_e, landed_e, core, names_e), "grads_to_chips_ffn1_begin")
    flights = [(names_a + names_b, flight_ab), (["mix", "w_in"], flight_d), (names_e, flight_e)]
    return loss, dx, flights, gs


def kernel(x, mem, g_ffn1, w_ffn1_gu, w_ffn1_down, g_mix, w_in, b_gate, conv_w, w_conv_out, w_attn_out, w_o, g_cross, g_mem, w_cq, w_ckv, w_co, g_ffn2, w_ffn2_gu, w_ffn2_down, g_final, loss_target, m_g_ffn1, m_w_ffn1_gu, m_w_ffn1_down, m_g_mix, m_w_in, m_b_gate, m_conv_w, m_w_conv_out, m_w_attn_out, m_w_o, m_g_cross, m_g_mem, m_w_cq, m_w_ckv, m_w_co, m_g_ffn2, m_w_ffn2_gu, m_w_ffn2_down, m_g_final, v_g_ffn1, v_w_ffn1_gu, v_w_ffn1_down, v_g_mix, v_w_in, v_b_gate, v_conv_w, v_w_conv_out, v_w_attn_out, v_w_o, v_g_cross, v_g_mem, v_w_cq, v_w_ckv, v_w_co, v_g_ffn2, v_w_ffn2_gu, v_w_ffn2_down, v_g_final):
    args = locals()
    wts = {n: args[n] for n in WEIGHT_ORDER}
    mom1 = {n: args["m_" + n] for n in WEIGHT_ORDER}
    mom2 = {n: args["v_" + n] for n in WEIGHT_ORDER}
    cx, cy, cc = _mesh_pos()
    dev = 4 * cx + 2 * cy + cc
    conv_cols = D // N_DEV

    conv_pad = jnp.concatenate([conv_w, jnp.zeros((SMALL_R - CONV_K, conv_cols), F32)], axis=0)
    gains = {n: wts[n].reshape(1, D) for n in GAINS}
    loss8, dx, flights, gs = _step(x[0], mem[0], loss_target[0], _exchange_shards(wts), conv_pad, gains,
                                 b_gate.reshape(1, 2 * D), cc.reshape(1).astype(jnp.int32))

    grads, delta, new_m, new_v = {}, {}, {}, {}

    def operands(n, transposed):
        trio = (wts[n], mom1[n], mom2[n])
        return tuple(a.T for a in trio) if transposed else trio

    def record(n, res, transposed):
        grads[n], delta[n], new_m[n], new_v[n] = [r.T for r in res] if transposed else res

    early = [("w_ffn2_gu", "w_ffn2_gu", 0, True), ("w_ffn2_down", "w_ffn2_down", 0, False),
             ("w_ckv", "w_ckv", 0, False), ("w_in", "w_in", 0, False)]
    early += [(n, "mix", k, False) for k, n in enumerate(MIX_MATS)]
    early += [(n, "cross", k, False) for k, n in enumerate(CROSS_MATS)]
    chip = (2 * cx + cy).reshape(1).astype(jnp.int32)
    (names_early, flight_early), (names_w_in, flight_w_in), (last_names, flight_last) = flights
    token = flight_last.token
    own, land = {}, {}
    for names, flight, tag in ((names_early, flight_early, "early"), (names_w_in, flight_w_in, "w_in")):
        own_parts, landed = _chip_exchange_end(flight, token, "grads_to_chips_%s_end" % tag)
        own.update(zip(names, own_parts))
        land.update(zip(names, landed))
    for n, buf, row_block, transposed in early:
        w, m1, m2 = operands(n, transposed)
        record(n, _adamw_own(w, land[buf], own[buf], chip, m1, m2, "adamw_" + n, row_block, token), transposed)

    after = jnp.concatenate([new_v[n][:1, :1] for n, _, _, _ in early], axis=0)
    own_parts, landed = _chip_exchange_end(flight_last, after, "grads_to_chips_ffn1_end")
    for n, own_n, land_n, transposed in zip(last_names, own_parts, landed, (True, False)):
        w, m1, m2 = operands(n, transposed)
        record(n, _adamw_own(w, land_n, own_n, chip, m1, m2, "adamw_" + n), transposed)

    gs_rows = {n: gs[n] for n in GAINS + ("b_gate",)}
    small_mine = _pack_small(gs_rows, gs["conv_w"][:CONV_K]) + new_v[last_names[-1]][0, 0] * 0.0
    small_mine = small_mine.at[LOSS_ROW, 0].set(loss8[0, 0])
    small_all = _run_exchange(_gather_exchange([small_mine]), "gather_small_grads")[0]
    small_sum = _sum_slots(small_all, "small_grads_sum")
    loss = small_sum[LOSS_ROW, 0]
    grad_small = _unpack_small(small_sum)
    grad_small["conv_w"] = lax.dynamic_slice_in_dim(grad_small["conv_w"], dev * conv_cols, conv_cols, axis=1)
    grads.update(grad_small)

    def small_buf(vals):
        return _pack_small(vals, jnp.concatenate([vals["conv_w"], jnp.zeros((CONV_K, D - conv_cols), F32)], axis=1))

    _, d_s, m_s, v_s = _adamw(small_buf(wts), small_buf(grads)[None], small_buf(mom1), small_buf(mom2), "adamw_small")
    for res, buf in ((delta, d_s), (new_m, m_s), (new_v, v_s)):
        un = _unpack_small(buf)
        for n in GAINS + ("b_gate",):
            res[n] = un[n]
        res["conv_w"] = un["conv_w"][:, :conv_cols]

    return (loss, dx[None], *[grads[n] for n in WEIGHT_ORDER], *[delta[n] for n in WEIGHT_ORDER],
            *[new_m[n] for n in WEIGHT_ORDER], *[new_v[n] for n in WEIGHT_ORDER])
```

```python
import types

import jax
import jax.numpy as jnp
from jax import lax
from jax.experimental import pallas as pl
from jax.experimental.pallas import tpu as pltpu

F32 = jnp.float32
BF16 = jnp.bfloat16

D = 1024
DFF = 2816
SB_H = 8
SB_DH = 128
X_H = 4
X_DH = 256
CONV_K = 3
RMS_EPS = 1e-6
N_DEV = 8
N_CHIP = 4
SQ_ROWS = D // N_DEV

ADAM_LR = 0.001
ADAM_B1 = 0.9
ADAM_B2 = 0.999
ADAM_EPS = 1e-08
ADAM_WD = 0.01
ADAM_STEP = 10

TM = 256
TQ = 512
TK = 256
SB_HPS = 2
VMEM_LIMIT = 56 << 20

FF_BLK = DFF // 4
FF_PAD = 768
FF_SUB = 256
DOWN_ROWS = DFF // N_DEV

MIX_MATS = ("w_conv_out", "w_attn_out", "w_o")
CROSS_MATS = ("w_cq", "w_co")

_ANY = pl.BlockSpec(memory_space=pl.ANY)


def _cparams(n_axes=1):
    return pltpu.CompilerParams(
        dimension_semantics=("arbitrary",) * n_axes, vmem_limit_bytes=VMEM_LIMIT)


def _row_spec(tm, n):
    return pl.BlockSpec((tm, n), lambda i: (i, 0))


def _blk_row_spec(nb, tm, n):
    return pl.BlockSpec((nb, tm, n), lambda i: (0, i, 0))


def _const_spec(shape):
    zeros = (0,) * len(shape)
    return pl.BlockSpec(shape, lambda i: zeros)


def _dot(a, b):
    return jnp.dot(a, b, preferred_element_type=F32)


def _dot_nt(a, b):
    return lax.dot_general(a, b, (((1,), (1,)), ((), ())), preferred_element_type=F32)


def _dot_tn(a, b):
    return lax.dot_general(a, b, (((0,), (0,)), ((), ())), preferred_element_type=F32)


def _sigmoid(x):
    return 1.0 / (1.0 + jnp.exp(-x))


def _call(body, operands, *, grid, in_specs, out_specs, out_shape, scratch_shapes, name, comm=None):
    n_in, n_out, n_sc = len(in_specs), len(out_specs), len(scratch_shapes)
    if comm is None:
        outs = pl.pallas_call(
            body, grid=grid, name=name, in_specs=in_specs, out_specs=out_specs, out_shape=out_shape,
            scratch_shapes=scratch_shapes, compiler_params=_cparams(len(grid)))(*operands)
        return list(outs), []
    c_in, c_out, c_sem = len(comm.inputs), len(comm.out_shapes), len(comm.sem_shapes)

    def hosted(*refs):
        bounds = [0, n_in, c_in, n_out, c_out, n_sc, c_sem]
        parts, pos = [], 0
        for k in bounds[1:]:
            parts.append(refs[pos:pos + k])
            pos += k
        ins, cins, outs, couts, scr, sems = parts
        step, n_steps = pl.program_id(0), grid[0]
        for ax in range(1, len(grid)):
            step, n_steps = step * grid[ax] + pl.program_id(ax), n_steps * grid[ax]

        @pl.when(step == 0)
        def _():
            comm.start(cins, couts, sems)

        @pl.when(step == (2 * n_steps) // 3)
        def _():
            comm.middle(cins, couts, sems)

        body(*ins, *outs, *scr)

        @pl.when(step == n_steps - 1)
        def _():
            comm.finish(cins, couts, sems)

    res = pl.pallas_call(
        hosted, grid=grid, name=name, in_specs=list(in_specs) + [_ANY] * c_in,
        out_specs=list(out_specs) + [_ANY] * c_out, out_shape=list(out_shape) + list(comm.out_shapes),
        scratch_shapes=list(scratch_shapes) + list(comm.sem_shapes),
        compiler_params=_cparams(len(grid)))(*operands, *comm.inputs)
    return list(res[:n_out]), list(res[n_out:])


def _load_resident(step, pairs, sems):
    @pl.when(step == 0)
    def _():
        copies = [pltpu.make_async_copy(src, dst, sems.at[k]) for k, (src, dst) in enumerate(pairs)]
        for cp in copies:
            cp.start()
        for cp in copies:
            cp.wait()


def _square_pairs(buf_hbm, index, dst):
    off = index * SQ_ROWS
    return [(buf_hbm.at[d, off:off + SQ_ROWS, :], dst.at[d * SQ_ROWS:(d + 1) * SQ_ROWS, :]) for d in range(N_DEV)]


def _down_pairs(wd_hbm, dst):
    return [(wd_hbm.at[d], dst.at[d // 2, (d % 2) * DOWN_ROWS:(d % 2 + 1) * DOWN_ROWS, :]) for d in range(N_DEV)]


def _zero_down_pad(step, dst):
    @pl.when(step == 0)
    def _():
        dst[:, FF_BLK:, :] = jnp.zeros((4, FF_PAD - FF_BLK, D), BF16)


def _rms_fwd_tile(xt, g):
    r = lax.rsqrt(jnp.mean(xt * xt, axis=-1, keepdims=True) + RMS_EPS)
    return (xt * r) * g


def _rms_bwd_tile(xt, g, dn):
    r = lax.rsqrt(jnp.mean(xt * xt, axis=-1, keepdims=True) + RMS_EPS)
    xhat = xt * r
    dxhat = dn * g
    dx = r * (dxhat - xhat * jnp.mean(dxhat * xhat, axis=-1, keepdims=True))
    dg = jnp.sum(dn * xhat, axis=0, keepdims=True)
    return dx, dg


def _accumulate(ref, step, value):
    @pl.when(step == 0)
    def _():
        ref[...] = value

    @pl.when(step != 0)
    def _():
        ref[...] = ref[...] + value


def _ffn_fwd(x, g, wgu, wd, name, comm=None, head=None):
    t = x.shape[0]

    def body(x_ref, g_ref, wgu_hbm, wd_hbm, *refs):
        if head is None:
            n_ref, gate_ref, up_ref, act_ref, h_ref, wgu_v, wd_v, sems = refs
        else:
            gf_ref, t_ref, n_ref, gate_ref, up_ref, act_ref, dh_ref, loss_ref, dgf_ref, wgu_v, wd_v, sems = refs
        step = pl.program_id(0)
        _zero_down_pad(step, wd_v)
        _load_resident(step, [(wgu_hbm, wgu_v)] + _down_pairs(wd_hbm, wd_v), sems)
        xt = x_ref[...]
        n = _rms_fwd_tile(xt, g_ref[...]).astype(BF16)
        n_ref[...] = n
        acc = jnp.zeros((TM, D), F32)
        for j in range(4):
            for s in range(FF_PAD // FF_SUB):
                lo, hi = s * FF_SUB, (s + 1) * FF_SUB
                gt = _dot_nt(n, wgu_v[j, lo:hi, :])
                ut = _dot_nt(n, wgu_v[4 + j, lo:hi, :])
                gate_ref[j, :, lo:hi] = gt.astype(BF16)
                up_ref[j, :, lo:hi] = ut.astype(BF16)
                act_ref[j, :, lo:hi] = ((gt * _sigmoid(gt)) * ut).astype(BF16)
            acc = acc + _dot(act_ref[j], wd_v[j])
        ht = xt + 0.5 * acc
        if head is None:
            h_ref[...] = ht
        else:
            gain = gf_ref[...]
            diff = _rms_fwd_tile(ht, gain) - t_ref[...]
            part = 0.5 * jnp.sum(jnp.sum(diff * diff, axis=-1, keepdims=True) / D, axis=0, keepdims=True)
            dx, dg = _rms_bwd_tile(ht, gain, diff / D)
            dh_ref[...] = dx
            _accumulate(loss_ref, step, jnp.broadcast_to(part, (8, 128)))
            _accumulate(dgf_ref, step, dg)

    ff = jax.ShapeDtypeStruct((4, t, FF_PAD), BF16)
    operands, in_specs = (x, g, wgu, wd), [_row_spec(TM, D), _const_spec((1, D)), _ANY, _ANY]
    out_specs = [_row_spec(TM, D)] + [_blk_row_spec(4, TM, FF_PAD)] * 3 + [_row_spec(TM, D)]
    out_shape = [jax.ShapeDtypeStruct((t, D), BF16), ff, ff, ff, jax.ShapeDtypeStruct((t, D), F32)]
    if head is not None:
        operands += tuple(head)
        in_specs += [_const_spec((1, D)), _row_spec(TM, D)]
        out_specs += [_const_spec((8, 128)), _const_spec((1, D))]
        out_shape += [jax.ShapeDtypeStruct((8, 128), F32), jax.ShapeDtypeStruct((1, D), F32)]
    return _call(
        body, operands, grid=(t // TM,), name=name, comm=comm, in_specs=in_specs, out_specs=out_specs,
        out_shape=out_shape,
        scratch_shapes=[pltpu.VMEM((N_DEV, FF_PAD, D), BF16), pltpu.VMEM((4, FF_PAD, D), BF16),
                        pltpu.SemaphoreType.DMA((1 + N_DEV,))])


def _ffn_bwd(dh, xin, g, gate, up, wgu, wd, name, comm=None):
    t = dh.shape[0]

    def body(dh_ref, x_ref, g_ref, gate_ref, up_ref, wgu_hbm, wd_hbm,
             dgu_ref, dhb_ref, dx_ref, dg_ref, wgu_v, wd_v, sems):
        step = pl.program_id(0)
        _zero_down_pad(step, wd_v)
        _load_resident(step, [(wgu_hbm, wgu_v)] + _down_pairs(wd_hbm, wd_v), sems)
        dht = dh_ref[...]
        dhb = (0.5 * dht).astype(BF16)
        dhb_ref[...] = dhb
        dn = jnp.zeros((TM, D), F32)
        for j in range(4):
            for s in range(FF_PAD // FF_SUB):
                lo, hi = s * FF_SUB, (s + 1) * FF_SUB
                da = _dot_nt(dhb, wd_v[j, lo:hi, :])
                gt = gate_ref[j, :, lo:hi].astype(F32)
                ut = up_ref[j, :, lo:hi].astype(F32)
                sg = _sigmoid(gt)
                dgt = (da * ut * (sg * (1.0 + gt * (1.0 - sg)))).astype(BF16)
                dut = (da * (gt * sg)).astype(BF16)
                dgu_ref[j, :, lo:hi] = dgt
                dgu_ref[4 + j, :, lo:hi] = dut
            dn = dn + _dot(dgu_ref[j], wgu_v[j]) + _dot(dgu_ref[4 + j], wgu_v[4 + j])
        dx, dg = _rms_bwd_tile(x_ref[...], g_ref[...], dn)
        dx_ref[...] = dht + dx
        _accumulate(dg_ref, step, dg)

    return _call(
        body, (dh, xin, g, gate, up, wgu, wd), grid=(t // TM,), name=name, comm=comm,
        in_specs=[_row_spec(TM, D), _row_spec(TM, D), _const_spec((1, D)), _blk_row_spec(4, TM, FF_PAD),
                  _blk_row_spec(4, TM, FF_PAD), _ANY, _ANY],
        out_specs=[_blk_row_spec(N_DEV, TM, FF_PAD), _row_spec(TM, D), _row_spec(TM, D), _const_spec((1, D))],
        out_shape=[jax.ShapeDtypeStruct((N_DEV, t, FF_PAD), BF16), jax.ShapeDtypeStruct((t, D), BF16),
                   jax.ShapeDtypeStruct((t, D), F32), jax.ShapeDtypeStruct((1, D), F32)],
        scratch_shapes=[pltpu.VMEM((N_DEV, FF_PAD, D), BF16), pltpu.VMEM((4, FF_PAD, D), BF16),
                        pltpu.SemaphoreType.DMA((1 + N_DEV,))])


WIDE_TILES = (1024, 512, 256, 128)


def _pick_tile(n, options=(512, 256, 128)):
    for o in options:
        if n % o == 0:
            return o
    return n


def _into(stack, n_operands):
    if stack is None:
        return (), [], {}
    return (stack,), [_ANY], {n_operands: 0}


def _mm_tn_square(a, b, name, index, count, stack=None):
    k, m = a.shape
    _, n = b.shape
    tn = _pick_tile(n)
    extra, extra_specs, aliases = _into(stack, 2)

    def body(a_ref, b_ref, *rest):
        rest[-1][...] = _dot_tn(a_ref[...], b_ref[...]).astype(BF16).reshape(N_DEV, m // N_DEV, tn)

    return pl.pallas_call(
        body, grid=(n // tn,), name=name,
        in_specs=[pl.BlockSpec((k, m), lambda j: (0, 0)), pl.BlockSpec((k, tn), lambda j: (0, j))] + extra_specs,
        out_specs=pl.BlockSpec((N_DEV, m // N_DEV, tn), lambda j: (0, index, j)),
        out_shape=jax.ShapeDtypeStruct((N_DEV, count * (m // N_DEV), n), BF16),
        input_output_aliases=aliases, compiler_params=_cparams(1),
    )(a, b, *extra)


def _mm_tn_cols(a, b, name, first=0, count=None, stack=None):
    k, m = a.shape
    nb, _, n = b.shape
    tm = _pick_tile(m, WIDE_TILES)
    extra, extra_specs, aliases = _into(stack, 2)

    def body(a_ref, b_ref, *rest):
        rest[-1][0] = _dot_tn(a_ref[...].astype(BF16), b_ref[0].astype(BF16)).astype(BF16)

    return pl.pallas_call(
        body, grid=(nb, m // tm), name=name,
        in_specs=[pl.BlockSpec((k, tm), lambda j, i: (0, i)), pl.BlockSpec((1, k, n), lambda j, i: (j, 0, 0))]
                 + extra_specs,
        out_specs=pl.BlockSpec((1, tm, n), lambda j, i: (j + first, i, 0)),
        out_shape=jax.ShapeDtypeStruct((nb if count is None else count, m, n), BF16),
        input_output_aliases=aliases, compiler_params=_cparams(2),
    )(a, b, *extra)


def _mm_tn_rows(a, b, keep, name, comm=None):
    nb, k, m = a.shape
    _, n = b.shape
    tn = _pick_tile(n, WIDE_TILES)

    def body(a_ref, b_ref, o_ref):
        o_ref[0] = _dot_tn(a_ref[0], b_ref[...])[:keep].astype(BF16)

    (out,), couts = _call(
        body, (a, b), grid=(nb, n // tn), name=name, comm=comm,
        in_specs=[pl.BlockSpec((1, k, m), lambda j, i: (j, 0, 0)), pl.BlockSpec((k, tn), lambda j, i: (0, i))],
        out_specs=[pl.BlockSpec((1, keep, tn), lambda j, i: (j, 0, i))],
        out_shape=[jax.ShapeDtypeStruct((nb, keep, n), BF16)], scratch_shapes=[])
    return out if comm is None else (out, couts)


PCG_W = 5 * D
QKV_W = 3 * D
PROJ_SUB = 512


def _inproj_fwd(h, g, w_in, conv_w, name, comm=None):
    t = h.shape[0]

    def body(h_ref, g_ref, w_hbm, cw_ref, u_ref, pcg_ref, qkv_ref, yc_ref, w_v, tail_v, sems):
        step = pl.program_id(0)
        _load_resident(step, [(w_hbm, w_v)], sems)

        @pl.when(step == 0)
        def _():
            tail_v[...] = jnp.zeros_like(tail_v)

        u = _rms_fwd_tile(h_ref[...], g_ref[...]).astype(BF16)
        u_ref[...] = u
        for blk in range(N_DEV):
            for s in range(D // PROJ_SUB):
                lo, hi = s * PROJ_SUB, (s + 1) * PROJ_SUB
                p = _dot(u, w_v[blk, :, lo:hi])
                if blk < 3:
                    pcg_ref[:, blk * D + lo:blk * D + hi] = p
                elif blk < 6:
                    qkv_ref[:, (blk - 3) * D + lo:(blk - 3) * D + hi] = p.astype(BF16)
                else:
                    pcg_ref[:, (blk - 3) * D + lo:(blk - 3) * D + hi] = p
        xc = pcg_ref[:, D:2 * D] * pcg_ref[:, 2 * D:3 * D]
        ext = jnp.concatenate([tail_v[...], xc], axis=0)
        conv = (cw_ref[0:1, :] * pltpu.roll(ext, 2, 0)[8:] + cw_ref[1:2, :] * pltpu.roll(ext, 1, 0)[8:]
                + cw_ref[2:3, :] * xc)
        yc_ref[...] = (pcg_ref[:, 0:D] * conv).astype(BF16)
        tail_v[...] = xc[TM - 8:]

    return _call(
        body, (h, g, w_in, conv_w), grid=(t // TM,), name=name, comm=comm,
        in_specs=[_row_spec(TM, D), _const_spec((1, D)), _ANY, _const_spec((CONV_K, D))],
        out_specs=[_row_spec(TM, D), _row_spec(TM, PCG_W), _row_spec(TM, QKV_W), _row_spec(TM, D)],
        out_shape=[jax.ShapeDtypeStruct((t, D), BF16), jax.ShapeDtypeStruct((t, PCG_W), F32),
                   jax.ShapeDtypeStruct((t, QKV_W), BF16), jax.ShapeDtypeStruct((t, D), BF16)],
        scratch_shapes=[pltpu.VMEM((N_DEV, D, D), BF16), pltpu.VMEM((8, D), F32), pltpu.SemaphoreType.DMA((1,))])


def _tri2(cond):
    rr = lax.broadcasted_iota(jnp.int32, (2 * TK, TK), 0) & (TK - 1)
    cc = lax.broadcasted_iota(jnp.int32, (2 * TK, TK), 1)
    return cond(rr, cc).astype(BF16)


def _causal(shift, row0=0):
    rr = lax.broadcasted_iota(jnp.int32, (TQ - row0, TK), 0) + row0
    cc = lax.broadcasted_iota(jnp.int32, (TQ - row0, TK), 1)
    return cc + shift < rr


def _cumdot(v, tri2):
    hi = v.astype(BF16)
    lo = (v - hi.astype(F32)).astype(BF16)
    return _dot(jnp.concatenate([hi, lo], axis=1), tri2)


def _log_1m_beta(z):
    return -(jnp.maximum(z, 0.0) + jnp.log(1.0 + jnp.exp(-jnp.abs(z))))


def _sb_specs(t):
    g = SB_H // SB_HPS
    w = SB_HPS * SB_DH
    q_spec = pl.BlockSpec((TQ, w), lambda h, i: (i, h))
    k_spec = pl.BlockSpec((t, w), lambda h, i: (0, g + h))
    v_spec = pl.BlockSpec((t, w), lambda h, i: (0, 2 * g + h))
    ct_spec = pl.BlockSpec((SB_HPS, TQ, 1), lambda h, i: (h, i, 0))
    return g, w, q_spec, k_spec, v_spec, ct_spec


def _sb_fwd(qkv, name, comm=None):
    t = qkv.shape[0]
    scale = SB_DH ** -0.5
    g, w, q_spec, k_spec, v_spec, ct_spec = _sb_specs(t)

    def body(q_ref, k_ref, v_ref, y_ref, ct_ref):
        i = pl.program_id(1)
        later = _tri2(lambda j, s: j > s)
        n_diag = TQ // TK

        def block(j, carry, shift):
            off = pl.multiple_of(j * TK, TK)
            zs, ms = [], []
            for hd in range(SB_HPS):
                cols = slice(hd * SB_DH, (hd + 1) * SB_DH)
                z = _dot_nt(q_ref[:, cols], k_ref[pl.ds(off, TK), cols]) * scale
                m = _log_1m_beta(z)
                if shift is not None:
                    m = jnp.where(_causal(shift), m, 0.0)
                zs.append(z)
                ms.append(m)
            after = _cumdot(jnp.concatenate(ms, axis=0), later)
            out = []
            for hd in range(SB_HPS):
                acc, c_sum = carry[hd]
                cols = slice(hd * SB_DH, (hd + 1) * SB_DH)
                a = jnp.exp((ms[hd] + zs[hd]) + (c_sum + after[hd * TQ:(hd + 1) * TQ]))
                if shift is not None:
                    a = jnp.where(_causal(shift), a, 0.0)
                out.append((acc + _dot(a.astype(BF16), v_ref[pl.ds(off, TK), cols]),
                            c_sum + jnp.sum(ms[hd], axis=1, keepdims=True)))
            return tuple(out)

        carry = tuple((jnp.zeros((TQ, SB_DH), F32), jnp.zeros((TQ, 1), F32)) for _ in range(SB_HPS))
        for d in reversed(range(n_diag)):
            carry = block(i * n_diag + d, carry, d * TK)
        carry = lax.fori_loop(0, i * n_diag, lambda jj, c: block(i * n_diag - 1 - jj, c, None), carry)
        for hd in range(SB_HPS):
            y_ref[:, hd * SB_DH:(hd + 1) * SB_DH] = carry[hd][0].astype(BF16)
            ct_ref[hd] = carry[hd][1]

    return _call(
        body, (qkv, qkv, qkv), grid=(g, t // TQ), name=name, comm=comm,
        in_specs=[q_spec, k_spec, v_spec],
        out_specs=[q_spec, ct_spec],
        out_shape=[jax.ShapeDtypeStruct((t, D), BF16), jax.ShapeDtypeStruct((SB_H, t, 1), F32)],
        scratch_shapes=[])


def _sb_bwd(qkv, dy, ctot, after, name, comm=None):
    t = qkv.shape[0]
    scale = SB_DH ** -0.5
    g, w, q_spec, k_spec, v_spec, ct_spec = _sb_specs(t)
    acc_spec = pl.BlockSpec((2, t, w), lambda h, i: (0, 0, h))

    def body(q_ref, k_ref, v_ref, dy_ref, ct_ref, after_ref, dq_ref, dkv_ref):
        i = pl.program_id(1)

        @pl.when(i == 0)
        def _():
            dkv_ref[...] = jnp.zeros_like(dkv_ref)

        upto = _tri2(lambda j, s: j <= s)
        n_diag = TQ // TK

        def block(j, carry, shift):
            off = pl.multiple_of(j * TK, TK)
            r0 = 0 if shift is None else shift
            nr = TQ - r0
            causal = None if shift is None else _causal(shift, r0)

            def grow(old, delta):
                return old + delta if r0 == 0 else jnp.concatenate([old[:r0], old[r0:] + delta], axis=0)

            zs, ms = [], []
            for hd in range(SB_HPS):
                cols = slice(hd * SB_DH, (hd + 1) * SB_DH)
                z = _dot_nt(q_ref[r0:, cols], k_ref[pl.ds(off, TK), cols]) * scale
                m = _log_1m_beta(z)
                if causal is not None:
                    m = jnp.where(causal, m, 0.0)
                zs.append(z)
                ms.append(m)
            m_upto = _cumdot(jnp.concatenate(ms, axis=0), upto)
            ls, a_s, es = [], [], []
            for hd in range(SB_HPS):
                cols = slice(hd * SB_DH, (hd + 1) * SB_DH)
                l = ms[hd] + zs[hd]
                a = jnp.exp(l + ((ct_ref[hd, r0:] - carry[hd][1][r0:]) - m_upto[hd * nr:(hd + 1) * nr]))
                if causal is not None:
                    a = jnp.where(causal, a, 0.0)
                ls.append(l)
                a_s.append(a)
                es.append(_dot_nt(dy_ref[r0:, cols], v_ref[pl.ds(off, TK), cols]) * a)
            e_upto = _dot(jnp.concatenate(es, axis=0).astype(BF16), upto[:TK])
            out = []
            for hd in range(SB_HPS):
                dq, p_sum, e_sum = carry[hd]
                cols = slice(hd * SB_DH, (hd + 1) * SB_DH)
                e = es[hd]
                dz = e - jnp.exp(ls[hd]) * (e_sum[r0:] + e_upto[hd * nr:(hd + 1) * nr])
                if causal is not None:
                    dz = jnp.where(causal, dz, 0.0)
                dzs = (dz * scale).astype(BF16)
                dkv_ref[0, pl.ds(off, TK), cols] += _dot_tn(dzs, q_ref[r0:, cols])
                dkv_ref[1, pl.ds(off, TK), cols] += _dot_tn(a_s[hd].astype(BF16), dy_ref[r0:, cols])
                out.append((grow(dq, _dot(dzs, k_ref[pl.ds(off, TK), cols])),
                            grow(p_sum, jnp.sum(ms[hd], axis=1, keepdims=True)),
                            grow(e_sum, jnp.sum(e, axis=1, keepdims=True))))
            return tuple(out)

        zero = jnp.zeros((TQ, 1), F32)
        init = tuple((jnp.zeros((TQ, SB_DH), F32), zero, zero) for _ in range(SB_HPS))
        carry = lax.fori_loop(0, i * n_diag, lambda j, c: block(j, c, None), init)
        for d in range(n_diag):
            carry = block(i * n_diag + d, carry, d * TK)
        for hd in range(SB_HPS):
            dq_ref[:, hd * SB_DH:(hd + 1) * SB_DH] = carry[hd][0].astype(BF16)

    return _call(
        body, (qkv, qkv, qkv, dy, ctot, after), grid=(g, t // TQ), name=name, comm=comm,
        in_specs=[q_spec, k_spec, v_spec, q_spec, ct_spec, pl.BlockSpec(after.shape, lambda h, i: (0, 0))],
        out_specs=[q_spec, acc_spec],
        out_shape=[jax.ShapeDtypeStruct((t, D), BF16), jax.ShapeDtypeStruct((2, t, D), F32)],
        scratch_shapes=[])


def _gate_specs():
    return [pl.BlockSpec((TM, D), lambda i: (i, 3)), pl.BlockSpec((TM, D), lambda i: (i, 4))]


def _mix_pairs(mix_hbm, dsts):
    pairs = []
    for index, dst in enumerate(dsts):
        pairs += _square_pairs(mix_hbm, index, dst)
    return pairs


def _mix_out_fwd(yc, ysb, pcg, b_gate, h, w_mix, name, comm=None):
    t = h.shape[0]

    def body(yc_ref, ysb_ref, gc_ref, gs_ref, b_ref, h_ref, mix_hbm,
             a_ref, b_out_ref, mg_ref, h2_ref, wc_v, wa_v, wo_v, sems):
        _load_resident(pl.program_id(0), _mix_pairs(mix_hbm, (wc_v, wa_v, wo_v)), sems)
        a = _dot(yc_ref[...], wc_v[...])
        b = _dot(ysb_ref[...], wa_v[...])
        merged = (_sigmoid(gc_ref[...] + b_ref[:, :D]) * a + _sigmoid(gs_ref[...] + b_ref[:, D:]) * b).astype(BF16)
        a_ref[...] = a
        b_out_ref[...] = b
        mg_ref[...] = merged
        h2_ref[...] = h_ref[...] + _dot(merged, wo_v[...])

    return _call(
        body, (yc, ysb, pcg, pcg, b_gate, h, w_mix), grid=(t // TM,), name=name, comm=comm,
        in_specs=[_row_spec(TM, D), _row_spec(TM, D)] + _gate_specs()
                 + [_const_spec((1, 2 * D)), _row_spec(TM, D), _ANY],
        out_specs=[_row_spec(TM, D)] * 4,
        out_shape=[jax.ShapeDtypeStruct((t, D), F32), jax.ShapeDtypeStruct((t, D), F32),
                   jax.ShapeDtypeStruct((t, D), BF16), jax.ShapeDtypeStruct((t, D), F32)],
        scratch_shapes=[pltpu.VMEM((D, D), BF16)] * 3 + [pltpu.SemaphoreType.DMA((3 * N_DEV,))])


def _mix_out_bwd(dh2, a, b, pcg, b_gate, conv_w, w_mix, name, comm=None):
    t = dh2.shape[0]
    n_tile = t // TM
    per8 = TM // 8

    def rows(n):
        return pl.BlockSpec((TM, n), lambda i: (n_tile - 1 - i, 0))

    def cols(block):
        return pl.BlockSpec((TM, D), lambda i: (n_tile - 1 - i, block))

    def before(block):
        return pl.BlockSpec((8, D), lambda i: (jnp.maximum((n_tile - 1 - i) * per8 - 1, 0), block))

    def body(dh_ref, a_ref, b_ref, gc_ref, gs_ref, cb_ref, cc_ref, cx_ref, ccp_ref, cxp_ref, bias_ref, cw_ref, mix_hbm,
             dhb_ref, da_ref, db_ref, dgp_ref, dc_ref, dysb_ref, dbias_ref, dcw_ref, wc_v, wa_v, wo_v, head_v, sems):
        step = pl.program_id(0)
        _load_resident(step, _mix_pairs(mix_hbm, (wc_v, wa_v, wo_v)), sems)

        @pl.when(step == 0)
        def _():
            head_v[...] = jnp.zeros_like(head_v)
            dcw_ref[...] = jnp.zeros_like(dcw_ref)

        dhb = dh_ref[...].astype(BF16)
        dhb_ref[...] = dhb
        dm = _dot_nt(dhb, wo_v[...])
        gc = _sigmoid(gc_ref[...] + bias_ref[:, :D])
        gs = _sigmoid(gs_ref[...] + bias_ref[:, D:])
        da = (dm * gc).astype(BF16)
        db = (dm * gs).astype(BF16)
        da_ref[...] = da
        db_ref[...] = db
        dgc = dm * a_ref[...] * (gc * (1.0 - gc))
        dgs = dm * b_ref[...] * (gs * (1.0 - gs))
        dgp_ref[0] = dgc.astype(BF16)
        dgp_ref[1] = dgs.astype(BF16)
        _accumulate(dbias_ref.at[:, :D], step, jnp.sum(dgc, axis=0, keepdims=True))
        _accumulate(dbias_ref.at[:, D:], step, jnp.sum(dgs, axis=0, keepdims=True))
        dysb_ref[...] = _dot_nt(db, wa_v[...]).astype(BF16)
        dyc = _dot_nt(da, wc_v[...])
        cc, cx = cc_ref[...], cx_ref[...]
        xc = cc * cx
        xc_before = jnp.where(step == n_tile - 1, 0.0, ccp_ref[...] * cxp_ref[...])
        ext = jnp.concatenate([xc_before, xc], axis=0)
        x1 = pltpu.roll(ext, 1, 0)[8:]
        x2 = pltpu.roll(ext, 2, 0)[8:]
        w0, w1, w2 = cw_ref[0:1, :], cw_ref[1:2, :], cw_ref[2:3, :]
        dc_ref[0] = (dyc * (w0 * x2 + w1 * x1 + w2 * xc)).astype(BF16)
        dconv = dyc * cb_ref[...]
        dcw_ref[0:1, :] += jnp.sum(dconv * x2, axis=0, keepdims=True)
        dcw_ref[1:2, :] += jnp.sum(dconv * x1, axis=0, keepdims=True)
        dcw_ref[2:3, :] += jnp.sum(dconv * xc, axis=0, keepdims=True)
        after = jnp.concatenate([dconv, head_v[...]], axis=0)
        dxc = w2 * dconv + w1 * pltpu.roll(after, TM + 7, 0)[:TM] + w0 * pltpu.roll(after, TM + 6, 0)[:TM]
        dc_ref[1] = (dxc * cx).astype(BF16)
        dc_ref[2] = (dxc * cc).astype(BF16)
        head_v[...] = dconv[:8]

    return _call(
        body, (dh2, a, b, pcg, pcg, pcg, pcg, pcg, pcg, pcg, b_gate, conv_w, w_mix), grid=(n_tile,), name=name,
        comm=comm,
        in_specs=[rows(D)] * 3 + [cols(3), cols(4), cols(0), cols(1), cols(2), before(1), before(2),
                                  _const_spec((1, 2 * D)), _const_spec((CONV_K, D)), _ANY],
        out_specs=[rows(D)] * 3 + [pl.BlockSpec((2, TM, D), lambda i: (0, n_tile - 1 - i, 0)),
                                   pl.BlockSpec((3, TM, D), lambda i: (0, n_tile - 1 - i, 0)), rows(D),
                                   _const_spec((1, 2 * D)), _const_spec((8, D))],
        out_shape=[jax.ShapeDtypeStruct((t, D), BF16)] * 3
                  + [jax.ShapeDtypeStruct((2, t, D), BF16), jax.ShapeDtypeStruct((3, t, D), BF16),
                     jax.ShapeDtypeStruct((t, D), BF16), jax.ShapeDtypeStruct((1, 2 * D), F32),
                     jax.ShapeDtypeStruct((8, D), F32)],
        scratch_shapes=[pltpu.VMEM((D, D), BF16)] * 3 + [pltpu.VMEM((8, D), F32),
                                                         pltpu.SemaphoreType.DMA((3 * N_DEV,))])


def _inproj_bwd(dconv, dq, dkv, dgp, w_in, h, g, dh_res, name, comm=None):
    t = h.shape[0]

    def body(dc_ref, dq_ref, dkv_ref, dgp_ref, w_hbm, h_ref, g_ref, dres_ref, dh_ref, dg_ref, w_v, sems):
        step = pl.program_id(0)
        _load_resident(step, [(w_hbm, w_v)], sems)
        du = _dot_nt(dq_ref[...], w_v[3])
        for k in range(3):
            du = du + _dot_nt(dc_ref[k], w_v[k])
        for k in range(2):
            du = du + _dot_nt(dkv_ref[k].astype(BF16), w_v[4 + k]) + _dot_nt(dgp_ref[k], w_v[6 + k])
        dx, dg = _rms_bwd_tile(h_ref[...], g_ref[...], du)
        dh_ref[...] = dres_ref[...] + dx
        _accumulate(dg_ref, step, dg)

    return _call(
        body, (dconv, dq, dkv, dgp, w_in, h, g, dh_res), grid=(t // TM,), name=name, comm=comm,
        in_specs=[_blk_row_spec(3, TM, D), _row_spec(TM, D), _blk_row_spec(2, TM, D), _blk_row_spec(2, TM, D), _ANY,
                  _row_spec(TM, D), _const_spec((1, D)), _row_spec(TM, D)],
        out_specs=[_row_spec(TM, D), _const_spec((1, D))],
        out_shape=[jax.ShapeDtypeStruct((t, D), F32), jax.ShapeDtypeStruct((1, D), F32)],
        scratch_shapes=[pltpu.VMEM((N_DEV, D, D), BF16), pltpu.SemaphoreType.DMA((1,))])


def _memkv_fwd(mem, g, w_ckv, name):
    m = mem.shape[0]

    def body(mem_ref, g_ref, w_ref, mn_ref, kv_ref):
        mn = _rms_fwd_tile(mem_ref[...], g_ref[...]).astype(BF16)
        mn_ref[...] = mn
        for j in range(N_DEV):
            kv_ref[j] = _dot(mn, w_ref[j]).astype(BF16)

    return pl.pallas_call(
        body, grid=(1,), name=name,
        in_specs=[_const_spec((m, D)), _const_spec((1, D)), _const_spec((N_DEV, D, X_DH))],
        out_specs=[_const_spec((m, D)), _const_spec((N_DEV, m, X_DH))],
        out_shape=[jax.ShapeDtypeStruct((m, D), BF16), jax.ShapeDtypeStruct((N_DEV, m, X_DH), BF16)],
        compiler_params=_cparams(),
    )(mem, g, w_ckv)


def _memkv_bwd(dkv, mem, g, w_ckv, name):
    m = mem.shape[0]

    def body(dkv_ref, mem_ref, g_ref, w_ref, dg_ref):
        dmn = jnp.zeros((m, D), F32)
        for j in range(N_DEV):
            dmn = dmn + _dot_nt(dkv_ref[j].astype(BF16), w_ref[j])
        _, dg = _rms_bwd_tile(mem_ref[...], g_ref[...], dmn)
        dg_ref[...] = dg

    return pl.pallas_call(
        body, grid=(1,), name=name,
        in_specs=[_const_spec((N_DEV, m, X_DH)), _const_spec((m, D)), _const_spec((1, D)),
                  _const_spec((N_DEV, D, X_DH))],
        out_specs=_const_spec((1, D)),
        out_shape=jax.ShapeDtypeStruct((1, D), F32),
        compiler_params=_cparams(),
    )(dkv, mem, g, w_ckv)


def _softmax_rows(s):
    e = jnp.exp(s - jnp.max(s, axis=-1, keepdims=True))
    return e / jnp.sum(e, axis=-1, keepdims=True)


def _cross_pairs(cross_hbm, wq_v, wo_v):
    return _square_pairs(cross_hbm, 0, wq_v) + _square_pairs(cross_hbm, 1, wo_v)


def _cross_fwd(h, g, kv, w_cross, name):
    t = h.shape[0]
    m = kv.shape[1]
    scale = X_DH ** -0.5

    def body(h_ref, g_ref, kv_ref, cross_hbm, hn_ref, qx_ref, o_ref, h3_ref, wq_v, wo_v, sems):
        _load_resident(pl.program_id(0), _cross_pairs(cross_hbm, wq_v, wo_v), sems)
        ht = h_ref[...]
        hn = _rms_fwd_tile(ht, g_ref[...]).astype(BF16)
        hn_ref[...] = hn
        qx = _dot(hn, wq_v[...]).astype(BF16)
        qx_ref[...] = qx
        for hd in range(X_H):
            lo, hi = hd * X_DH, (hd + 1) * X_DH
            p = _softmax_rows(_dot_nt(qx[:, lo:hi], kv_ref[hd]) * scale)
            o_ref[:, lo:hi] = _dot(p.astype(BF16), kv_ref[X_H + hd]).astype(BF16)
        h3_ref[...] = ht + _dot(o_ref[...], wo_v[...])

    return pl.pallas_call(
        body, grid=(t // TM,), name=name,
        in_specs=[_row_spec(TM, D), _const_spec((1, D)), _const_spec((N_DEV, m, X_DH)), _ANY],
        out_specs=[_row_spec(TM, D)] * 4,
        out_shape=[jax.ShapeDtypeStruct((t, D), BF16)] * 3 + [jax.ShapeDtypeStruct((t, D), F32)],
        scratch_shapes=[pltpu.VMEM((D, D), BF16)] * 2 + [pltpu.SemaphoreType.DMA((2 * N_DEV,))],
        compiler_params=_cparams(),
    )(h, g, kv, w_cross)


def _cross_bwd(dh3, h, g, qx, kv, w_cross, name, comm=None):
    t = h.shape[0]
    m = kv.shape[1]
    scale = X_DH ** -0.5

    def body(dh_ref, h_ref, g_ref, qx_ref, kv_ref, cross_hbm,
             dhb_ref, dqx_ref, dkv_ref, dh2_ref, dg_ref, wq_v, wo_v, sems):
        step = pl.program_id(0)
        _load_resident(step, _cross_pairs(cross_hbm, wq_v, wo_v), sems)

        @pl.when(step == 0)
        def _():
            dkv_ref[...] = jnp.zeros_like(dkv_ref)

        dht = dh_ref[...]
        dhb = dht.astype(BF16)
        dhb_ref[...] = dhb
        do = _dot_nt(dhb, wo_v[...]).astype(BF16)
        for hd in range(X_H):
            lo, hi = hd * X_DH, (hd + 1) * X_DH
            qh = qx_ref[:, lo:hi]
            kh = kv_ref[hd]
            p = _softmax_rows(_dot_nt(qh, kh) * scale)
            doh = do[:, lo:hi]
            dp = _dot_nt(doh, kv_ref[X_H + hd])
            ds = (p * (dp - jnp.sum(dp * p, axis=-1, keepdims=True)) * scale).astype(BF16)
            dqx_ref[:, lo:hi] = _dot(ds, kh).astype(BF16)
            dkv_ref[hd] += _dot_tn(ds, qh)
            dkv_ref[X_H + hd] += _dot_tn(p.astype(BF16), doh)
        dhn = _dot_nt(dqx_ref[...], wq_v[...])
        dx, dg = _rms_bwd_tile(h_ref[...], g_ref[...], dhn)
        dh2_ref[...] = dht + dx
        _accumulate(dg_ref, step, dg)

    return _call(
        body, (dh3, h, g, qx, kv, w_cross), grid=(t // TM,), name=name, comm=comm,
        in_specs=[_row_spec(TM, D), _row_spec(TM, D), _const_spec((1, D)), _row_spec(TM, D),
                  _const_spec((N_DEV, m, X_DH)), _ANY],
        out_specs=[_row_spec(TM, D), _row_spec(TM, D), _const_spec((N_DEV, m, X_DH)), _row_spec(TM, D),
                   _const_spec((1, D))],
        out_shape=[jax.ShapeDtypeStruct((t, D), BF16), jax.ShapeDtypeStruct((t, D), BF16),
                   jax.ShapeDtypeStruct((N_DEV, m, X_DH), F32), jax.ShapeDtypeStruct((t, D), F32),
                   jax.ShapeDtypeStruct((1, D), F32)],
        scratch_shapes=[pltpu.VMEM((D, D), BF16)] * 2 + [pltpu.SemaphoreType.DMA((2 * N_DEV,))])


def _adamw(w, parts, m, v, name, row_block=0, token=None):
    r, c = w.shape
    n = parts.shape[0]
    tr = _pick_tile(r, (256, 352, 128))
    off = row_block * (r // tr)

    def body(*refs):
        if token is None:
            _adamw_update(None, *refs)
        else:
            _adamw_update(refs[4], *refs[:4], *refs[5:])

    spec = _row_spec(tr, c)
    in_specs = [spec, pl.BlockSpec((n, tr, c), lambda i: (0, i + off, 0)), spec, spec]
    operands = (w, parts, m, v)
    if token is not None:
        in_specs.append(_const_spec(token.shape))
        operands += (token,)
    return pl.pallas_call(
        body, grid=(r // tr,), name=name, in_specs=in_specs, out_specs=[spec] * 4,
        out_shape=[jax.ShapeDtypeStruct((r, c), F32)] * 4,
        compiler_params=_cparams(),
    )(*operands)


def _adamw_update(tok_ref, w_ref, p_ref, m_ref, v_ref, g_ref, d_ref, nm_ref, nv_ref):
    gt = p_ref[0].astype(F32)
    for k in range(1, p_ref.shape[0]):
        gt = gt + p_ref[k].astype(F32)
    if tok_ref is not None:
        gt = gt + tok_ref[0:1, 0:1]
    _adamw_apply(gt, w_ref, m_ref, v_ref, g_ref, d_ref, nm_ref, nv_ref)


def _adamw_own(w, land, own, chip, m, v, name, row_block=0, token=None):
    r, c = w.shape
    tr = _pick_tile(r, (256, 352, 128))
    off = row_block * (r // tr)

    def body(chip_ref, w_ref, land_ref, own_ref, m_ref, v_ref, *rest):
        mine = own_ref[0].astype(F32)
        gt = jnp.where(chip_ref[0] == 0, mine, land_ref[0].astype(F32))
        for k in range(1, N_CHIP):
            gt = gt + jnp.where(chip_ref[0] == k, mine, land_ref[k].astype(F32))
        if token is not None:
            gt = gt + rest[0][0:1, 0:1]
        _adamw_apply(gt, w_ref, m_ref, v_ref, *rest[-4:])

    spec = pl.BlockSpec((tr, c), lambda i, chip_ref: (i, 0))
    in_specs = [spec, pl.BlockSpec((N_CHIP, tr, c), lambda i, chip_ref: (0, i + off, 0)),
                pl.BlockSpec((1, tr, c), lambda i, chip_ref: (chip_ref[0], i + off, 0)), spec, spec]
    operands = (chip, w, land, own, m, v)
    if token is not None:
        in_specs.append(pl.BlockSpec(token.shape, lambda i, chip_ref: (0, 0)))
        operands += (token,)
    return pl.pallas_call(
        body, name=name,
        grid_spec=pltpu.PrefetchScalarGridSpec(
            num_scalar_prefetch=1, grid=(r // tr,), in_specs=in_specs, out_specs=[spec] * 4),
        out_shape=[jax.ShapeDtypeStruct((r, c), F32)] * 4,
        compiler_params=_cparams(),
    )(*operands)


def _adamw_apply(gt, w_ref, m_ref, v_ref, g_ref, d_ref, nm_ref, nv_ref):
    g_ref[...] = gt
    nm = ADAM_B1 * m_ref[...] + (1.0 - ADAM_B1) * gt
    nv = ADAM_B2 * v_ref[...] + (1.0 - ADAM_B2) * jnp.square(gt)
    m_hat = nm / (1.0 - ADAM_B1 ** ADAM_STEP)
    v_hat = nv / (1.0 - ADAM_B2 ** ADAM_STEP)
    d_ref[...] = -ADAM_LR * (m_hat / (jnp.sqrt(v_hat) + ADAM_EPS) + ADAM_WD * w_ref[...])
    nm_ref[...] = nm
    nv_ref[...] = nv


def _mesh_pos():
    return lax.axis_index("x"), lax.axis_index("y"), lax.axis_index("c")


def _no_round(in_refs, out_refs, sems):
    pass


def _run_exchange(comm, name):
    c_in, c_out = len(comm.inputs), len(comm.out_shapes)

    def body(*refs):
        cins, couts, sems = refs[:c_in], refs[c_in:c_in + c_out], refs[c_in + c_out:]
        comm.start(cins, couts, sems)
        comm.middle(cins, couts, sems)
        comm.finish(cins, couts, sems)

    return list(pl.pallas_call(
        body, name=name, out_shape=list(comm.out_shapes),
        in_specs=[_ANY] * c_in, out_specs=[_ANY] * c_out, scratch_shapes=list(comm.sem_shapes),
    )(*comm.inputs))


def _gather_exchange(shards):
    n_arr = len(shards)

    def plan(x_refs, out_refs, sems):
        send_sems, recv_sems, local_sems = sems[:3]
        stage = sems[3:]
        x, y, c = _mesh_pos()
        me, sibling = (x, y, c), (x, y, 1 - c)
        xn, yn, diag = (1 - x, y), (x, 1 - y), (1 - x, 1 - y)

        def slot(a, px, py, pc, half=None):
            ref = out_refs[a].at[4 * px + 2 * py + pc]
            if half is None:
                return ref
            rows = shards[a].shape[0] // 2
            return ref.at[half * rows:(half + 1) * rows]

        def copy(a, k, block, to, half=None, src=None):
            dst = slot(a, *block, half)
            return pltpu.make_async_remote_copy(
                src_ref=dst if src is None else src, dst_ref=dst,
                send_sem=send_sems.at[a, k], recv_sem=recv_sems.at[a, k],
                device_id=to, device_id_type=pl.DeviceIdType.MESH)

        return types.SimpleNamespace(
            me=me, sibling=sibling, xn=xn, yn=yn, diag=diag, c=c, copy=copy,
            mine_in=[pltpu.make_async_copy(x_refs[a], stage[a], local_sems.at[a, 0]) for a in range(n_arr)],
            mine_out=[pltpu.make_async_copy(stage[a], slot(a, *me), local_sems.at[a, 1]) for a in range(n_arr)],
            first=[cp for a in range(n_arr) for cp in (
                copy(a, 0, me, sibling, src=x_refs[a]), copy(a, 1, me, (*xn, c), src=x_refs[a]),
                copy(a, 2, me, (*yn, c), src=x_refs[a]))],
            second=lambda a: (copy(a, 3, (*xn, c), (*yn, c), half=0), copy(a, 5, (*xn, c), sibling),
                              copy(a, 4, (*yn, c), (*xn, c), half=1), copy(a, 6, (*yn, c), sibling)),
            third=lambda a: (copy(a, 7, (*diag, c), sibling, half=0), copy(a, 8, (*diag, c), sibling, half=1)))

    def start(x_refs, out_refs, sems):
        p = plan(x_refs, out_refs, sems)
        for cp in p.first + p.mine_in:
            cp.start()
        for cp_in, cp_out in zip(p.mine_in, p.mine_out):
            cp_in.wait()
            cp_out.start()

    def middle(x_refs, out_refs, sems):
        p = plan(x_refs, out_refs, sems)
        for a in range(n_arr):
            to_yn, x_to_sib, to_xn, y_to_sib = p.second(a)
            p.copy(a, 1, (*p.xn, p.c), p.me).wait_recv()
            to_yn.start()
            x_to_sib.start()
            p.copy(a, 2, (*p.yn, p.c), p.me).wait_recv()
            to_xn.start()
            y_to_sib.start()

    def finish(x_refs, out_refs, sems):
        p = plan(x_refs, out_refs, sems)
        for a in range(n_arr):
            half0_to_sib, half1_to_sib = p.third(a)
            p.copy(a, 3, (*p.diag, p.c), p.me, half=0).wait_recv()
            half0_to_sib.start()
            p.copy(a, 4, (*p.diag, p.c), p.me, half=1).wait_recv()
            half1_to_sib.start()
        other = 1 - p.c
        for a in range(n_arr):
            p.copy(a, 0, p.sibling, p.me).wait_recv()
            p.copy(a, 5, (*p.xn, other), p.me).wait_recv()
            p.copy(a, 6, (*p.yn, other), p.me).wait_recv()
            p.copy(a, 7, (*p.diag, other), p.me, half=0).wait_recv()
            p.copy(a, 8, (*p.diag, other), p.me, half=1).wait_recv()
        for cp in p.first:
            cp.wait_send()
        for a in range(n_arr):
            for cp in p.second(a) + p.third(a):
                cp.wait_send()
        for cp in p.mine_out:
            cp.wait()

    return types.SimpleNamespace(
        inputs=list(shards), start=start, middle=middle, finish=finish,
        out_shapes=[jax.ShapeDtypeStruct((N_DEV,) + s.shape, s.dtype) for s in shards],
        sem_shapes=[pltpu.SemaphoreType.DMA((n_arr, 9)), pltpu.SemaphoreType.DMA((n_arr, 9)),
                    pltpu.SemaphoreType.DMA((n_arr, 2))] + [pltpu.VMEM(s.shape, s.dtype) for s in shards])


def _pair_exchange(grads):
    n_arr = len(grads)

    def plan(g_refs, land_refs, sems):
        send_sems, recv_sems = sems
        x, y, c = _mesh_pos()
        return [pltpu.make_async_remote_copy(
            src_ref=g_refs[a].at[2 * k + 1 - c], dst_ref=land_refs[a].at[k],
            send_sem=send_sems.at[a, k], recv_sem=recv_sems.at[a, k],
            device_id=(x, y, 1 - c), device_id_type=pl.DeviceIdType.MESH)
            for a in range(n_arr) for k in range(N_CHIP)]

    def start(g_refs, land_refs, sems):
        for cp in plan(g_refs, land_refs, sems):
            cp.start()

    def finish(g_refs, land_refs, sems):
        for cp in plan(g_refs, land_refs, sems):
            cp.wait()

    return types.SimpleNamespace(
        inputs=list(grads), start=start, middle=_no_round, finish=finish,
        out_shapes=[jax.ShapeDtypeStruct((N_CHIP,) + g.shape[1:], g.dtype) for g in grads],
        sem_shapes=[pltpu.SemaphoreType.DMA((n_arr, N_CHIP)), pltpu.SemaphoreType.DMA((n_arr, N_CHIP))])


def _chip_exchange(parts):
    n_arr = len(parts)

    def plan(p_refs, land_refs, sems):
        send_sems, recv_sems, local_sems = sems
        x, y, c = _mesh_pos()
        my_chip = 2 * x + y
        chips = [(1 - x, y), (x, 1 - y), (1 - x, 1 - y)]
        local = [pltpu.make_async_copy(p_refs[a].at[my_chip], land_refs[a].at[my_chip], local_sems.at[a])
                 for a in range(n_arr)]

        def copy(a, k, src_slot, dst_slot, px, py):
            return pltpu.make_async_remote_copy(
                src_ref=p_refs[a].at[src_slot], dst_ref=land_refs[a].at[dst_slot],
                send_sem=send_sems.at[a, k], recv_sem=recv_sems.at[a, k],
                device_id=(px, py, c), device_id_type=pl.DeviceIdType.MESH)

        sends = [copy(a, k, 2 * px + py, my_chip, px, py) for a in range(n_arr) for k, (px, py) in enumerate(chips)]
        arrivals = [copy(a, k, my_chip, 2 * px + py, px, py) for a in range(n_arr)
                    for k, (px, py) in enumerate(chips)]
        return local, sends, arrivals

    def start(p_refs, land_refs, sems):
        local, sends, _ = plan(p_refs, land_refs, sems)
        for cp in local + sends:
            cp.start()

    def finish(p_refs, land_refs, sems):
        local, sends, arrivals = plan(p_refs, land_refs, sems)
        for cp in arrivals:
            cp.wait_recv()
        for cp in sends:
            cp.wait_send()
        for cp in local:
            cp.wait()

    return types.SimpleNamespace(
        inputs=list(parts), start=start, middle=_no_round, finish=finish,
        out_shapes=[jax.ShapeDtypeStruct(p.shape, p.dtype) for p in parts],
        sem_shapes=[pltpu.SemaphoreType.DMA((n_arr, 3)), pltpu.SemaphoreType.DMA((n_arr, 3)),
                    pltpu.SemaphoreType.DMA((n_arr,))])


_HBM = pl.BlockSpec(memory_space=pltpu.HBM)
_SEM = pl.BlockSpec(memory_space=pltpu.SEMAPHORE)
_DATAFLOW = pltpu.SideEffectType.DATAFLOW_SIDE_EFFECTING


def _chip_copies(p_refs, land_refs, send_sems, recv_sems):
    x, y, c = _mesh_pos()
    my_chip = 2 * x + y
    chips = [(1 - x, y), (x, 1 - y), (1 - x, 1 - y)]
    return [pltpu.make_async_remote_copy(
        src_ref=p_refs[a].at[2 * px + py], dst_ref=land_refs[a].at[my_chip],
        send_sem=send_sems[3 * a + k], recv_sem=recv_sems[3 * a + k],
        device_id=(px, py, c), device_id_type=pl.DeviceIdType.MESH)
        for a in range(len(p_refs)) for k, (px, py) in enumerate(chips)]


def _chip_exchange_begin(parts, name):
    n_arr = len(parts)
    n_buf, n_copy = 2 * n_arr, 3 * n_arr
    lands = [lax.empty(p.shape, p.dtype) for p in parts]

    def body(*refs):
        p_refs, land_refs = refs[:n_arr], refs[n_arr:n_buf]
        send_sems, recv_sems, token = refs[n_buf:n_buf + n_copy], refs[n_buf + n_copy:n_buf + 2 * n_copy], refs[-1]
        for cp in _chip_copies(p_refs, land_refs, send_sems, recv_sems):
            cp.start()
        token[...] = jnp.zeros_like(token)

    bufs = list(parts) + list(lands)
    outs = pl.pallas_call(
        body, name=name,
        out_shape=(*[pltpu.SemaphoreType.DMA(())] * (2 * n_copy), *[pltpu.HBM(b.shape, b.dtype) for b in bufs],
                   jax.ShapeDtypeStruct((8, 128), F32)),
        in_specs=[_HBM] * n_buf,
        out_specs=(*[_SEM] * (2 * n_copy), *[_HBM] * n_buf, pl.BlockSpec(memory_space=pltpu.VMEM)),
        input_output_aliases={i: 2 * n_copy + i for i in range(n_buf)},
        compiler_params=pltpu.CompilerParams(has_side_effects=_DATAFLOW),
    )(*[pltpu.with_memory_space_constraint(b, pltpu.HBM) for b in bufs])
    sems = list(outs[:2 * n_copy])
    thru = list(outs[2 * n_copy:2 * n_copy + n_buf])
    return types.SimpleNamespace(send_sems=sems[:n_copy], recv_sems=sems[n_copy:], parts=thru[:n_arr],
                                 lands=thru[n_arr:], token=outs[-1])


def _chip_exchange_end(flight, after, name):
    send_sems, recv_sems, parts, lands = flight.send_sems, flight.recv_sems, flight.parts, flight.lands
    n_arr = len(parts)
    n_buf, n_copy = 2 * n_arr, 3 * n_arr

    def body(*refs):
        p_refs, land_refs = refs[:n_arr], refs[n_arr:n_buf]
        sems = refs[n_buf:n_buf + 2 * n_copy]
        for cp in _chip_copies(p_refs, land_refs, sems[:n_copy], sems[n_copy:]):
            cp.wait_send()
            cp.wait_recv()

    bufs = list(parts) + list(lands)
    outs = pl.pallas_call(
        body, name=name, out_shape=tuple(pltpu.HBM(b.shape, b.dtype) for b in bufs),
        in_specs=[_HBM] * n_buf + [_SEM] * (2 * n_copy) + [_ANY], out_specs=tuple([_HBM] * n_buf),
        input_output_aliases={i: i for i in range(n_buf)},
        compiler_params=pltpu.CompilerParams(has_side_effects=_DATAFLOW),
    )(*bufs, *send_sems, *recv_sems, after)
    return list(outs[:n_arr]), list(outs[n_arr:])


def _row_tile(r, cap=640):
    best = None
    for cand in range(16, min(r, cap) + 1, 16):
        if r % cand == 0:
            best = cand
    return best if best is not None else r


def _pair_sum(gs, landeds, core, name):
    tiles = [_row_tile(g.shape[1]) for g in gs]
    counts = [g.shape[1] // tr for g, tr in zip(gs, tiles)]
    n_arr = len(gs)

    def body(core_ref, *refs):
        for a in range(n_arr):
            mine, theirs, out = refs[2 * a], refs[2 * a + 1], refs[2 * n_arr + a]
            out[0] = (mine[0].astype(F32) + theirs[0].astype(F32)).astype(out.dtype)

    in_specs, out_specs, operands = [], [], []
    for g, landed, tr, count in zip(gs, landeds, tiles, counts):
        c_dim = g.shape[2]
        last = count - 1
        in_specs += [pl.BlockSpec((1, tr, c_dim),
                                  lambda k, i, core_ref, last=last: (2 * k + core_ref[0], jnp.minimum(i, last), 0)),
                     pl.BlockSpec((1, tr, c_dim), lambda k, i, core_ref, last=last: (k, jnp.minimum(i, last), 0))]
        out_specs.append(pl.BlockSpec((1, tr, c_dim), lambda k, i, core_ref, last=last: (k, jnp.minimum(i, last), 0)))
        operands += [g, landed]
    return list(pl.pallas_call(
        body, name=name,
        grid_spec=pltpu.PrefetchScalarGridSpec(
            num_scalar_prefetch=1, grid=(N_CHIP, max(counts)), in_specs=in_specs, out_specs=out_specs),
        out_shape=[jax.ShapeDtypeStruct((N_CHIP,) + g.shape[1:], g.dtype) for g in gs],
        compiler_params=_cparams(2),
    )(core, *operands))


def _sum_slots(parts, name):
    n, r, c_dim = parts.shape
    tr = _row_tile(r)

    def body(p_ref, o_ref):
        acc = p_ref[0].astype(F32)
        for k in range(1, n):
            acc = acc + p_ref[k].astype(F32)
        o_ref[...] = acc

    return pl.pallas_call(
        body, grid=(r // tr,), name=name,
        in_specs=[pl.BlockSpec((n, tr, c_dim), lambda i: (0, i, 0))],
        out_specs=_row_spec(tr, c_dim),
        out_shape=jax.ShapeDtypeStruct((r, c_dim), F32),
        compiler_params=_cparams(),
    )(parts)


GAINS = ("g_ffn1", "g_mix", "g_cross", "g_mem", "g_ffn2", "g_final")
SMALL = GAINS + ("b_gate", "conv_w")
SMALL_R = 16
LOSS_ROW = 11
WEIGHT_ORDER = ("g_ffn1", "w_ffn1_gu", "w_ffn1_down", "g_mix", "w_in", "b_gate", "conv_w", "w_conv_out",
                "w_attn_out", "w_o", "g_cross", "g_mem", "w_cq", "w_ckv", "w_co", "g_ffn2", "w_ffn2_gu",
                "w_ffn2_down", "g_final")
GU_NAMES = ("w_ffn1_gu", "w_ffn2_gu")


def _pack_small(vals, conv_rows):
    rows = [vals[n].reshape(1, D) for n in GAINS] + [vals["b_gate"].reshape(2, D), conv_rows.reshape(CONV_K, D)]
    used = len(GAINS) + 2 + CONV_K
    return jnp.concatenate(rows + [jnp.zeros((SMALL_R - used, D), F32)], axis=0)


def _unpack_small(buf):
    out = {n: buf[k] for k, n in enumerate(GAINS)}
    out["b_gate"] = buf[6:8].reshape(2 * D)
    out["conv_w"] = buf[8:8 + CONV_K]
    return out


def _exchange_shards(wts):
    out = {n: jnp.pad(wts[n].T.astype(BF16), ((0, FF_PAD - FF_BLK), (0, 0))) for n in GU_NAMES}
    for n in ("w_ckv", "w_in", "w_ffn1_down", "w_ffn2_down"):
        out[n] = wts[n].astype(BF16)
    out["mix"] = jnp.concatenate([wts[n].astype(BF16) for n in MIX_MATS], axis=0)
    out["cross"] = jnp.concatenate([wts[n].astype(BF16) for n in CROSS_MATS], axis=0)
    return out


def _reduce_group(grads, landed, core, names):
    return _pair_sum(grads, landed, core, "grads_pair_sum_" + "_".join(names))


def _step(x, mem, target, sh, conv_pad, gains, b_gate, core):
    wg1, wd1, conv_all = _run_exchange(_gather_exchange([sh["w_ffn1_gu"], sh["w_ffn1_down"], conv_pad]), "gather_ffn1")
    conv_w = conv_all[:, :CONV_K, :].transpose(1, 0, 2).reshape(CONV_K, D)
    (n1, gate1, up1, act1, h1), (w_in,) = _ffn_fwd(
        x, gains["g_ffn1"], wg1, wd1, "ffn1_fwd", comm=_gather_exchange([sh["w_in"]]))
    (u, pcg, qkv, yc), (w_mix,) = _inproj_fwd(h1, gains["g_mix"], w_in, conv_w, "inproj_fwd",
                                              comm=_gather_exchange([sh["mix"]]))
    (ysb, ctot), (w_cross, w_ckv, wg2) = _sb_fwd(
        qkv, "sb_fwd", comm=_gather_exchange([sh["cross"], sh["w_ckv"], sh["w_ffn2_gu"]]))
    (a_mix, b_mix, merged, h2), (wd2,) = _mix_out_fwd(yc, ysb, pcg, b_gate, h1, w_mix, "mix_out_fwd",
                                                      comm=_gather_exchange([sh["w_ffn2_down"]]))
    mn, kv = _memkv_fwd(mem, gains["g_mem"], w_ckv, "memkv_fwd")
    hn, qx, o_x, h3 = _cross_fwd(h2, gains["g_cross"], kv, w_cross, "cross_fwd")
    (n4, gate2, up2, act2, dh4, loss, dg_final), _ = _ffn_fwd(h3, gains["g_ffn2"], wg2, wd2, "ffn2_fwd",
                                                              head=(gains["g_final"], target))

    gs = {"g_final": dg_final}
    (dgu2, dh4b, dh3, gs["g_ffn2"]), _ = _ffn_bwd(dh4, h3, gains["g_ffn2"], gate2, up2, wg2, wd2, "ffn2_bwd")
    grads_a = [_mm_tn_rows(dgu2, n4, FF_PAD, "dw_ffn2_gu"),
               _mm_tn_rows(act2, dh4b, FF_BLK, "dw_ffn2_down").reshape(N_DEV, DOWN_ROWS, D)]
    names_a = ["w_ffn2_gu", "w_ffn2_down"]
    (dh3b, dqx, dkv, dh2, gs["g_cross"]), landed_a = _cross_bwd(
        dh3, h2, gains["g_cross"], qx, kv, w_cross, "cross_bwd", comm=_pair_exchange(grads_a))
    sums_a = _reduce_group(grads_a, landed_a, core, names_a)
    cross_stack = _mm_tn_square(hn, dqx, "dw_cq", 0, len(CROSS_MATS))
    grads_b = [_mm_tn_cols(mn, dkv, "dw_ckv"), _mm_tn_square(o_x, dh3b, "dw_co", 1, len(CROSS_MATS), cross_stack)]
    names_b = ["w_ckv", "cross"]
    gs["g_mem"] = _memkv_bwd(dkv, mem, gains["g_mem"], w_ckv, "memkv_bwd")
    (dh2b, da_mix, db_mix, dgp, dconv, dysb, gs["b_gate"], gs["conv_w"]), landed_b = _mix_out_bwd(
        dh2, a_mix, b_mix, pcg, b_gate, conv_w, w_mix, "mix_out_bwd", comm=_pair_exchange(grads_b))
    sums_b = _reduce_group(grads_b, landed_b, core, names_b)
    mix_stack = _mm_tn_square(yc, da_mix, "dw_conv_out", 0, len(MIX_MATS))
    mix_stack = _mm_tn_square(ysb, db_mix, "dw_attn_out", 1, len(MIX_MATS), mix_stack)
    grads_c = [_mm_tn_square(merged, dh2b, "dw_o", 2, len(MIX_MATS), mix_stack)]
    flight_ab = _chip_exchange_begin(sums_a + sums_b, "grads_to_chips_early_begin")
    (dq, dkv_sb), landed_c = _sb_bwd(qkv, dysb, ctot, flight_ab.token, "sb_bwd", comm=_pair_exchange(grads_c))
    sums_c = _reduce_group(grads_c, landed_c, core, ["mix"])
    w_in_stack = _mm_tn_cols(u, dconv, "dw_in_conv", 0, N_DEV)
    w_in_stack = _mm_tn_cols(u, dq[None], "dw_in_q", 3, N_DEV, w_in_stack)
    w_in_stack = _mm_tn_cols(u, dkv_sb, "dw_in_kv", 4, N_DEV, w_in_stack)
    grads_d = [_mm_tn_cols(u, dgp, "dw_in_gates", 6, N_DEV, w_in_stack)]
    (dh1, gs["g_mix"]), landed_d = _inproj_bwd(dconv, dq, dkv_sb, dgp, w_in, h1, gains["g_mix"], dh2, "inproj_bwd",
                                               comm=_pair_exchange(grads_d))
    sums_d = _reduce_group(grads_d, landed_d, core, ["w_in"])
    flight_d = _chip_exchange_begin(sums_c + sums_d, "grads_to_chips_w_in_begin")
    (dgu1, dh1b, dx, gs["g_ffn1"]), _ = _ffn_bwd(dh1, x, gains["g_ffn1"] + flight_d.token[0, 0], gate1, up1, wg1, wd1,
                                                 "ffn1_bwd")
    dw_gu1 = _mm_tn_rows(dgu1, n1, FF_PAD, "dw_ffn1_gu")
    dw_down1, landed_gu1 = _mm_tn_rows(act1, dh1b, FF_BLK, "dw_ffn1_down", comm=_pair_exchange([dw_gu1]))
    grads_e = [dw_gu1, dw_down1.reshape(N_DEV, DOWN_ROWS, D)]
    names_e = ["w_ffn1_gu", "w_ffn1_down"]
    landed_e = landed_gu1 + _run_exchange(_pair_exchange(grads_e[1:]), "grads_to_sibling_ffn1_down")
    flight_e = _chip_exchange_begin(_reduce_group(grads_e, landed_e, core, names_e), "grads_to_chips_ffn1_begin")
    flights = [(names_a + names_b, flight_ab), (["mix", "w_in"], flight_d), (names_e, flight_e)]
    return loss, dx, flights, gs


def kernel(x, mem, g_ffn1, w_ffn1_gu, w_ffn1_down, g_mix, w_in, b_gate, conv_w, w_conv_out, w_attn_out, w_o, g_cross, g_mem, w_cq, w_ckv, w_co, g_ffn2, w_ffn2_gu, w_ffn2_down, g_final, loss_target, m_g_ffn1, m_w_ffn1_gu, m_w_ffn1_down, m_g_mix, m_w_in, m_b_gate, m_conv_w, m_w_conv_out, m_w_attn_out, m_w_o, m_g_cross, m_g_mem, m_w_cq, m_w_ckv, m_w_co, m_g_ffn2, m_w_ffn2_gu, m_w_ffn2_down, m_g_final, v_g_ffn1, v_w_ffn1_gu, v_w_ffn1_down, v_g_mix, v_w_in, v_b_gate, v_conv_w, v_w_conv_out, v_w_attn_out, v_w_o, v_g_cross, v_g_mem, v_w_cq, v_w_ckv, v_w_co, v_g_ffn2, v_w_ffn2_gu, v_w_ffn2_down, v_g_final):
    args = locals()
    wts = {n: args[n] for n in WEIGHT_ORDER}
    mom1 = {n: args["m_" + n] for n in WEIGHT_ORDER}
    mom2 = {n: args["v_" + n] for n in WEIGHT_ORDER}
    cx, cy, cc = _mesh_pos()
    dev = 4 * cx + 2 * cy + cc
    conv_cols = D // N_DEV

    conv_pad = jnp.concatenate([conv_w, jnp.zeros((SMALL_R - CONV_K, conv_cols), F32)], axis=0)
    gains = {n: wts[n].reshape(1, D) for n in GAINS}
    loss8, dx, flights, gs = _step(x[0], mem[0], loss_target[0], _exchange_shards(wts), conv_pad, gains,
                                 b_gate.reshape(1, 2 * D), cc.reshape(1).astype(jnp.int32))

    grads, delta, new_m, new_v = {}, {}, {}, {}

    def operands(n, transposed):
        trio = (wts[n], mom1[n], mom2[n])
        return tuple(a.T for a in trio) if transposed else trio

    def record(n, res, transposed):
        grads[n], delta[n], new_m[n], new_v[n] = [r.T for r in res] if transposed else res

    early = [("w_ffn2_gu", "w_ffn2_gu", 0, True), ("w_ffn2_down", "w_ffn2_down", 0, False),
             ("w_ckv", "w_ckv", 0, False), ("w_in", "w_in", 0, False)]
    early += [(n, "mix", k, False) for k, n in enumerate(MIX_MATS)]
    early += [(n, "cross", k, False) for k, n in enumerate(CROSS_MATS)]
    chip = (2 * cx + cy).reshape(1).astype(jnp.int32)
    (names_early, flight_early), (names_w_in, flight_w_in), (last_names, flight_last) = flights
    token = flight_last.token
    own, land = {}, {}
    for names, flight, tag in ((names_early, flight_early, "early"), (names_w_in, flight_w_in, "w_in")):
        own_parts, landed = _chip_exchange_end(flight, token, "grads_to_chips_%s_end" % tag)
        own.update(zip(names, own_parts))
        land.update(zip(names, landed))
    for n, buf, row_block, transposed in early:
        w, m1, m2 = operands(n, transposed)
        record(n, _adamw_own(w, land[buf], own[buf], chip, m1, m2, "adamw_" + n, row_block, token), transposed)

    after = jnp.concatenate([new_v[n][:1, :1] for n, _, _, _ in early], axis=0)
    own_parts, landed = _chip_exchange_end(flight_last, after, "grads_to_chips_ffn1_end")
    for n, own_n, land_n, transposed in zip(last_names, own_parts, landed, (True, False)):
        w, m1, m2 = operands(n, transposed)
        record(n, _adamw_own(w, land_n, own_n, chip, m1, m2, "adamw_" + n), transposed)

    gs_rows = {n: gs[n] for n in GAINS + ("b_gate",)}
    small_mine = _pack_small(gs_rows, gs["conv_w"][:CONV_K]) + new_v[last_names[-1]][0, 0] * 0.0
    small_mine = small_mine.at[LOSS_ROW, 0].set(loss8[0, 0])
    small_all = _run_exchange(_gather_exchange([small_mine]), "gather_small_grads")[0]
    small_sum = _sum_slots(small_all, "small_grads_sum")
    loss = small_sum[LOSS_ROW, 0]
    grad_small = _unpack_small(small_sum)
    grad_small["conv_w"] = lax.dynamic_slice_in_dim(grad_small["conv_w"], dev * conv_cols, conv_cols, axis=1)
    grads.update(grad_small)

    def small_buf(vals):
        return _pack_small(vals, jnp.concatenate([vals["conv_w"], jnp.zeros((CONV_K, D - conv_cols), F32)], axis=1))

    _, d_s, m_s, v_s = _adamw(small_buf(wts), small_buf(grads)[None], small_buf(mom1), small_buf(mom2), "adamw_small")
    for res, buf in ((delta, d_s), (new_m, m_s), (new_v, v_s)):
        un = _unpack_small(buf)
        for n in GAINS + ("b_gate",):
            res[n] = un[n]
        res["conv_w"] = un["conv_w"][:, :conv_cols]

    return (loss, dx[None], *[grads[n] for n in WEIGHT_ORDER], *[delta[n] for n in WEIGHT_ORDER],
            *[new_m[n] for n in WEIGHT_ORDER], *[new_v[n] for n in WEIGHT_ORDER])
```

```python
import types

import jax
import jax.numpy as jnp
from jax import lax
from jax.experimental import pallas as pl
from jax.experimental.pallas import tpu as pltpu

F32 = jnp.float32
BF16 = jnp.bfloat16

D = 1024
DFF = 2816
SB_H = 8
SB_DH = 128
X_H = 4
X_DH = 256
CONV_K = 3
RMS_EPS = 1e-6
N_DEV = 8
N_CHIP = 4
SQ_ROWS = D // N_DEV

ADAM_LR = 0.001
ADAM_B1 = 0.9
ADAM_B2 = 0.999
ADAM_EPS = 1e-08
ADAM_WD = 0.01
ADAM_STEP = 10

TM = 256
TQ = 512
TK = 256
SB_HPS = 2
VMEM_LIMIT = 56 << 20

FF_BLK = DFF // 4
FF_PAD = 768
FF_SUB = 256
DOWN_ROWS = DFF // N_DEV

MIX_MATS = ("w_conv_out", "w_attn_out", "w_o")
CROSS_MATS = ("w_cq", "w_co")

_ANY = pl.BlockSpec(memory_space=pl.ANY)


def _cparams(n_axes=1):
    return pltpu.CompilerParams(
        dimension_semantics=("arbitrary",) * n_axes, vmem_limit_bytes=VMEM_LIMIT)


def _row_spec(tm, n):
    return pl.BlockSpec((tm, n), lambda i: (i, 0))


def _blk_row_spec(nb, tm, n):
    return pl.BlockSpec((nb, tm, n), lambda i: (0, i, 0))


def _const_spec(shape):
    zeros = (0,) * len(shape)
    return pl.BlockSpec(shape, lambda i: zeros)


def _dot(a, b):
    return jnp.dot(a, b, preferred_element_type=F32)


def _dot_nt(a, b):
    return lax.dot_general(a, b, (((1,), (1,)), ((), ())), preferred_element_type=F32)


def _dot_tn(a, b):
    return lax.dot_general(a, b, (((0,), (0,)), ((), ())), preferred_element_type=F32)


def _sigmoid(x):
    return 1.0 / (1.0 + jnp.exp(-x))


def _call(body, operands, *, grid, in_specs, out_specs, out_shape, scratch_shapes, name, comm=None):
    n_in, n_out, n_sc = len(in_specs), len(out_specs), len(scratch_shapes)
    if comm is None:
        outs = pl.pallas_call(
            body, grid=grid, name=name, in_specs=in_specs, out_specs=out_specs, out_shape=out_shape,
            scratch_shapes=scratch_shapes, compiler_params=_cparams(len(grid)))(*operands)
        return list(outs), []
    c_in, c_out, c_sem = len(comm.inputs), len(comm.out_shapes), len(comm.sem_shapes)

    def hosted(*refs):
        bounds = [0, n_in, c_in, n_out, c_out, n_sc, c_sem]
        parts, pos = [], 0
        for k in bounds[1:]:
            parts.append(refs[pos:pos + k])
            pos += k
        ins, cins, outs, couts, scr, sems = parts
        step, n_steps = pl.program_id(0), grid[0]
        for ax in range(1, len(grid)):
            step, n_steps = step * grid[ax] + pl.program_id(ax), n_steps * grid[ax]

        @pl.when(step == 0)
        def _():
            comm.start(cins, couts, sems)

        @pl.when(step == (2 * n_steps) // 3)
        def _():
            comm.middle(cins, couts, sems)

        body(*ins, *outs, *scr)

        @pl.when(step == n_steps - 1)
        def _():
            comm.finish(cins, couts, sems)

    res = pl.pallas_call(
        hosted, grid=grid, name=name, in_specs=list(in_specs) + [_ANY] * c_in,
        out_specs=list(out_specs) + [_ANY] * c_out, out_shape=list(out_shape) + list(comm.out_shapes),
        scratch_shapes=list(scratch_shapes) + list(comm.sem_shapes),
        compiler_params=_cparams(len(grid)))(*operands, *comm.inputs)
    return list(res[:n_out]), list(res[n_out:])


def _load_resident(step, pairs, sems):
    @pl.when(step == 0)
    def _():
        copies = [pltpu.make_async_copy(src, dst, sems.at[k]) for k, (src, dst) in enumerate(pairs)]
        for cp in copies:
            cp.start()
        for cp in copies:
            cp.wait()


def _square_pairs(buf_hbm, index, dst):
    off = index * SQ_ROWS
    return [(buf_hbm.at[d, off:off + SQ_ROWS, :], dst.at[d * SQ_ROWS:(d + 1) * SQ_ROWS, :]) for d in range(N_DEV)]


def _down_pairs(wd_hbm, dst):
    return [(wd_hbm.at[d], dst.at[d // 2, (d % 2) * DOWN_ROWS:(d % 2 + 1) * DOWN_ROWS, :]) for d in range(N_DEV)]


def _zero_down_pad(step, dst):
    @pl.when(step == 0)
    def _():
        dst[:, FF_BLK:, :] = jnp.zeros((4, FF_PAD - FF_BLK, D), BF16)


def _rms_fwd_tile(xt, g):
    r = lax.rsqrt(jnp.mean(xt * xt, axis=-1, keepdims=True) + RMS_EPS)
    return (xt * r) * g


def _rms_bwd_tile(xt, g, dn):
    r = lax.rsqrt(jnp.mean(xt * xt, axis=-1, keepdims=True) + RMS_EPS)
    xhat = xt * r
    dxhat = dn * g
    dx = r * (dxhat - xhat * jnp.mean(dxhat * xhat, axis=-1, keepdims=True))
    dg = jnp.sum(dn * xhat, axis=0, keepdims=True)
    return dx, dg


def _accumulate(ref, step, value):
    @pl.when(step == 0)
    def _():
        ref[...] = value

    @pl.when(step != 0)
    def _():
        ref[...] = ref[...] + value


def _ffn_fwd(x, g, wgu, wd, name, comm=None, head=None):
    t = x.shape[0]

    def body(x_ref, g_ref, wgu_hbm, wd_hbm, *refs):
        if head is None:
            n_ref, gate_ref, up_ref, act_ref, h_ref, wgu_v, wd_v, sems = refs
        else:
            gf_ref, t_ref, n_ref, gate_ref, up_ref, act_ref, dh_ref, loss_ref, dgf_ref, wgu_v, wd_v, sems = refs
        step = pl.program_id(0)
        _zero_down_pad(step, wd_v)
        _load_resident(step, [(wgu_hbm, wgu_v)] + _down_pairs(wd_hbm, wd_v), sems)
        xt = x_ref[...]
        n = _rms_fwd_tile(xt, g_ref[...]).astype(BF16)
        n_ref[...] = n
        acc = jnp.zeros((TM, D), F32)
        for j in range(4):
            for s in range(FF_PAD // FF_SUB):
                lo, hi = s * FF_SUB, (s + 1) * FF_SUB
                gt = _dot_nt(n, wgu_v[j, lo:hi, :])
                ut = _dot_nt(n, wgu_v[4 + j, lo:hi, :])
                gate_ref[j, :, lo:hi] = gt.astype(BF16)
                up_ref[j, :, lo:hi] = ut.astype(BF16)
                act_ref[j, :, lo:hi] = ((gt * _sigmoid(gt)) * ut).astype(BF16)
            acc = acc + _dot(act_ref[j], wd_v[j])
        ht = xt + 0.5 * acc
        if head is None:
            h_ref[...] = ht
        else:
            gain = gf_ref[...]
            diff = _rms_fwd_tile(ht, gain) - t_ref[...]
            part = 0.5 * jnp.sum(jnp.sum(diff * diff, axis=-1, keepdims=True) / D, axis=0, keepdims=True)
            dx, dg = _rms_bwd_tile(ht, gain, diff / D)
            dh_ref[...] = dx
            _accumulate(loss_ref, step, jnp.broadcast_to(part, (8, 128)))
            _accumulate(dgf_ref, step, dg)

    ff = jax.ShapeDtypeStruct((4, t, FF_PAD), BF16)
    operands, in_specs = (x, g, wgu, wd), [_row_spec(TM, D), _const_spec((1, D)), _ANY, _ANY]
    out_specs = [_row_spec(TM, D)] + [_blk_row_spec(4, TM, FF_PAD)] * 3 + [_row_spec(TM, D)]
    out_shape = [jax.ShapeDtypeStruct((t, D), BF16), ff, ff, ff, jax.ShapeDtypeStruct((t, D), F32)]
    if head is not None:
        operands += tuple(head)
        in_specs += [_const_spec((1, D)), _row_spec(TM, D)]
        out_specs += [_const_spec((8, 128)), _const_spec((1, D))]
        out_shape += [jax.ShapeDtypeStruct((8, 128), F32), jax.ShapeDtypeStruct((1, D), F32)]
    return _call(
        body, operands, grid=(t // TM,), name=name, comm=comm, in_specs=in_specs, out_specs=out_specs,
        out_shape=out_shape,
        scratch_shapes=[pltpu.VMEM((N_DEV, FF_PAD, D), BF16), pltpu.VMEM((4, FF_PAD, D), BF16),
                        pltpu.SemaphoreType.DMA((1 + N_DEV,))])


def _ffn_bwd(dh, xin, g, gate, up, wgu, wd, name, comm=None):
    t = dh.shape[0]

    def body(dh_ref, x_ref, g_ref, gate_ref, up_ref, wgu_hbm, wd_hbm,
             dgu_ref, dhb_ref, dx_ref, dg_ref, wgu_v, wd_v, sems):
        step = pl.program_id(0)
        _zero_down_pad(step, wd_v)
        _load_resident(step, [(wgu_hbm, wgu_v)] + _down_pairs(wd_hbm, wd_v), sems)
        dht = dh_ref[...]
        dhb = (0.5 * dht).astype(BF16)
        dhb_ref[...] = dhb
        dn = jnp.zeros((TM, D), F32)
        for j in range(4):
            for s in range(FF_PAD // FF_SUB):
                lo, hi = s * FF_SUB, (s + 1) * FF_SUB
                da = _dot_nt(dhb, wd_v[j, lo:hi, :])
                gt = gate_ref[j, :, lo:hi].astype(F32)
                ut = up_ref[j, :, lo:hi].astype(F32)
                sg = _sigmoid(gt)
                dgt = (da * ut * (sg * (1.0 + gt * (1.0 - sg)))).astype(BF16)
                dut = (da * (gt * sg)).astype(BF16)
                dgu_ref[j, :, lo:hi] = dgt
                dgu_ref[4 + j, :, lo:hi] = dut
            dn = dn + _dot(dgu_ref[j], wgu_v[j]) + _dot(dgu_ref[4 + j], wgu_v[4 + j])
        dx, dg = _rms_bwd_tile(x_ref[...], g_ref[...], dn)
        dx_ref[...] = dht + dx
        _accumulate(dg_ref, step, dg)

    return _call(
        body, (dh, xin, g, gate, up, wgu, wd), grid=(t // TM,), name=name, comm=comm,
        in_specs=[_row_spec(TM, D), _row_spec(TM, D), _const_spec((1, D)), _blk_row_spec(4, TM, FF_PAD),
                  _blk_row_spec(4, TM, FF_PAD), _ANY, _ANY],
        out_specs=[_blk_row_spec(N_DEV, TM, FF_PAD), _row_spec(TM, D), _row_spec(TM, D), _const_spec((1, D))],
        out_shape=[jax.ShapeDtypeStruct((N_DEV, t, FF_PAD), BF16), jax.ShapeDtypeStruct((t, D), BF16),
                   jax.ShapeDtypeStruct((t, D), F32), jax.ShapeDtypeStruct((1, D), F32)],
        scratch_shapes=[pltpu.VMEM((N_DEV, FF_PAD, D), BF16), pltpu.VMEM((4, FF_PAD, D), BF16),
                        pltpu.SemaphoreType.DMA((1 + N_DEV,))])


WIDE_TILES = (1024, 512, 256, 128)


def _pick_tile(n, options=(512, 256, 128)):
    for o in options:
        if n % o == 0:
            return o
    return n


def _into(stack, n_operands):
    if stack is None:
        return (), [], {}
    return (stack,), [_ANY], {n_operands: 0}


def _mm_tn_square(a, b, name, index, count, stack=None):
    k, m = a.shape
    _, n = b.shape
    tn = _pick_tile(n)
    extra, extra_specs, aliases = _into(stack, 2)

    def body(a_ref, b_ref, *rest):
        rest[-1][...] = _dot_tn(a_ref[...], b_ref[...]).astype(BF16).reshape(N_DEV, m // N_DEV, tn)

    return pl.pallas_call(
        body, grid=(n // tn,), name=name,
        in_specs=[pl.BlockSpec((k, m), lambda j: (0, 0)), pl.BlockSpec((k, tn), lambda j: (0, j))] + extra_specs,
        out_specs=pl.BlockSpec((N_DEV, m // N_DEV, tn), lambda j: (0, index, j)),
        out_shape=jax.ShapeDtypeStruct((N_DEV, count * (m // N_DEV), n), BF16),
        input_output_aliases=aliases, compiler_params=_cparams(1),
    )(a, b, *extra)


def _mm_tn_cols(a, b, name, first=0, count=None, stack=None):
    k, m = a.shape
    nb, _, n = b.shape
    tm = _pick_tile(m, WIDE_TILES)
    extra, extra_specs, aliases = _into(stack, 2)

    def body(a_ref, b_ref, *rest):
        rest[-1][0] = _dot_tn(a_ref[...].astype(BF16), b_ref[0].astype(BF16)).astype(BF16)

    return pl.pallas_call(
        body, grid=(nb, m // tm), name=name,
        in_specs=[pl.BlockSpec((k, tm), lambda j, i: (0, i)), pl.BlockSpec((1, k, n), lambda j, i: (j, 0, 0))]
                 + extra_specs,
        out_specs=pl.BlockSpec((1, tm, n), lambda j, i: (j + first, i, 0)),
        out_shape=jax.ShapeDtypeStruct((nb if count is None else count, m, n), BF16),
        input_output_aliases=aliases, compiler_params=_cparams(2),
    )(a, b, *extra)


def _mm_tn_rows(a, b, keep, name, comm=None):
    nb, k, m = a.shape
    _, n = b.shape
    tn = _pick_tile(n, WIDE_TILES)

    def body(a_ref, b_ref, o_ref):
        o_ref[0] = _dot_tn(a_ref[0], b_ref[...])[:keep].astype(BF16)

    (out,), couts = _call(
        body, (a, b), grid=(nb, n // tn), name=name, comm=comm,
        in_specs=[pl.BlockSpec((1, k, m), lambda j, i: (j, 0, 0)), pl.BlockSpec((k, tn), lambda j, i: (0, i))],
        out_specs=[pl.BlockSpec((1, keep, tn), lambda j, i: (j, 0, i))],
        out_shape=[jax.ShapeDtypeStruct((nb, keep, n), BF16)], scratch_shapes=[])
    return out if comm is None else (out, couts)


PCG_W = 5 * D
QKV_W = 3 * D
PROJ_SUB = 512


def _inproj_fwd(h, g, w_in, conv_w, name, comm=None):
    t = h.shape[0]

    def body(h_ref, g_ref, w_hbm, cw_ref, u_ref, pcg_ref, qkv_ref, yc_ref, w_v, tail_v, sems):
        step = pl.program_id(0)
        _load_resident(step, [(w_hbm, w_v)], sems)

        @pl.when(step == 0)
        def _():
            tail_v[...] = jnp.zeros_like(tail_v)

        u = _rms_fwd_tile(h_ref[...], g_ref[...]).astype(BF16)
        u_ref[...] = u
        for blk in range(N_DEV):
            for s in range(D // PROJ_SUB):
                lo, hi = s * PROJ_SUB, (s + 1) * PROJ_SUB
                p = _dot(u, w_v[blk, :, lo:hi])
                if blk < 3:
                    pcg_ref[:, blk * D + lo:blk * D + hi] = p
                elif blk < 6:
                    qkv_ref[:, (blk - 3) * D + lo:(blk - 3) * D + hi] = p.astype(BF16)
                else:
                    pcg_ref[:, (blk - 3) * D + lo:(blk - 3) * D + hi] = p
        xc = pcg_ref[:, D:2 * D] * pcg_ref[:, 2 * D:3 * D]
        ext = jnp.concatenate([tail_v[...], xc], axis=0)
        conv = (cw_ref[0:1, :] * pltpu.roll(ext, 2, 0)[8:] + cw_ref[1:2, :] * pltpu.roll(ext, 1, 0)[8:]
                + cw_ref[2:3, :] * xc)
        yc_ref[...] = (pcg_ref[:, 0:D] * conv).astype(BF16)
        tail_v[...] = xc[TM - 8:]

    return _call(
        body, (h, g, w_in, conv_w), grid=(t // TM,), name=name, comm=comm,
        in_specs=[_row_spec(TM, D), _const_spec((1, D)), _ANY, _const_spec((CONV_K, D))],
        out_specs=[_row_spec(TM, D), _row_spec(TM, PCG_W), _row_spec(TM, QKV_W), _row_spec(TM, D)],
        out_shape=[jax.ShapeDtypeStruct((t, D), BF16), jax.ShapeDtypeStruct((t, PCG_W), F32),
                   jax.ShapeDtypeStruct((t, QKV_W), BF16), jax.ShapeDtypeStruct((t, D), BF16)],
        scratch_shapes=[pltpu.VMEM((N_DEV, D, D), BF16), pltpu.VMEM((8, D), F32), pltpu.SemaphoreType.DMA((1,))])


def _tri2(cond):
    rr = lax.broadcasted_iota(jnp.int32, (2 * TK, TK), 0) & (TK - 1)
    cc = lax.broadcasted_iota(jnp.int32, (2 * TK, TK), 1)
    return cond(rr, cc).astype(BF16)


def _causal(shift, row0=0):
    rr = lax.broadcasted_iota(jnp.int32, (TQ - row0, TK), 0) + row0
    cc = lax.broadcasted_iota(jnp.int32, (TQ - row0, TK), 1)
    return cc + shift < rr


def _cumdot(v, tri2):
    hi = v.astype(BF16)
    lo = (v - hi.astype(F32)).astype(BF16)
    return _dot(jnp.concatenate([hi, lo], axis=1), tri2)


def _log_1m_beta(z):
    return -(jnp.maximum(z, 0.0) + jnp.log(1.0 + jnp.exp(-jnp.abs(z))))


def _sb_specs(t):
    g = SB_H // SB_HPS
    w = SB_HPS * SB_DH
    q_spec = pl.BlockSpec((TQ, w), lambda h, i: (i, h))
    k_spec = pl.BlockSpec((t, w), lambda h, i: (0, g + h))
    v_spec = pl.BlockSpec((t, w), lambda h, i: (0, 2 * g + h))
    ct_spec = pl.BlockSpec((SB_HPS, TQ, 1), lambda h, i: (h, i, 0))
    return g, w, q_spec, k_spec, v_spec, ct_spec


def _sb_fwd(qkv, name, comm=None):
    t = qkv.shape[0]
    scale = SB_DH ** -0.5
    g, w, q_spec, k_spec, v_spec, ct_spec = _sb_specs(t)

    def body(q_ref, k_ref, v_ref, y_ref, ct_ref):
        i = pl.program_id(1)
        later = _tri2(lambda j, s: j > s)
        n_diag = TQ // TK

        def block(j, carry, shift):
            off = pl.multiple_of(j * TK, TK)
            zs, ms = [], []
            for hd in range(SB_HPS):
                cols = slice(hd * SB_DH, (hd + 1) * SB_DH)
                z = _dot_nt(q_ref[:, cols], k_ref[pl.ds(off, TK), cols]) * scale
                m = _log_1m_beta(z)
                if shift is not None:
                    m = jnp.where(_causal(shift), m, 0.0)
                zs.append(z)
                ms.append(m)
            after = _cumdot(jnp.concatenate(ms, axis=0), later)
            out = []
            for hd in range(SB_HPS):
                acc, c_sum = carry[hd]
                cols = slice(hd * SB_DH, (hd + 1) * SB_DH)
                a = jnp.exp((ms[hd] + zs[hd]) + (c_sum + after[hd * TQ:(hd + 1) * TQ]))
                if shift is not None:
                    a = jnp.where(_causal(shift), a, 0.0)
                out.append((acc + _dot(a.astype(BF16), v_ref[pl.ds(off, TK), cols]),
                            c_sum + jnp.sum(ms[hd], axis=1, keepdims=True)))
            return tuple(out)

        carry = tuple((jnp.zeros((TQ, SB_DH), F32), jnp.zeros((TQ, 1), F32)) for _ in range(SB_HPS))
        for d in reversed(range(n_diag)):
            carry = block(i * n_diag + d, carry, d * TK)
        carry = lax.fori_loop(0, i * n_diag, lambda jj, c: block(i * n_diag - 1 - jj, c, None), carry)
        for hd in range(SB_HPS):
            y_ref[:, hd * SB_DH:(hd + 1) * SB_DH] = carry[hd][0].astype(BF16)
            ct_ref[hd] = carry[hd][1]

    return _call(
        body, (qkv, qkv, qkv), grid=(g, t // TQ), name=name, comm=comm,
        in_specs=[q_spec, k_spec, v_spec],
        out_specs=[q_spec, ct_spec],
        out_shape=[jax.ShapeDtypeStruct((t, D), BF16), jax.ShapeDtypeStruct((SB_H, t, 1), F32)],
        scratch_shapes=[])


def _sb_bwd(qkv, dy, ctot, after, name, comm=None):
    t = qkv.shape[0]
    scale = SB_DH ** -0.5
    g, w, q_spec, k_spec, v_spec, ct_spec = _sb_specs(t)
    acc_spec = pl.BlockSpec((2, t, w), lambda h, i: (0, 0, h))

    def body(q_ref, k_ref, v_ref, dy_ref, ct_ref, after_ref, dq_ref, dkv_ref):
        i = pl.program_id(1)

        @pl.when(i == 0)
        def _():
            dkv_ref[...] = jnp.zeros_like(dkv_ref)

        upto = _tri2(lambda j, s: j <= s)
        n_diag = TQ // TK

        def block(j, carry, shift):
            off = pl.multiple_of(j * TK, TK)
            r0 = 0 if shift is None else shift
            nr = TQ - r0
            causal = None if shift is None else _causal(shift, r0)

            def grow(old, delta):
                return old + delta if r0 == 0 else jnp.concatenate([old[:r0], old[r0:] + delta], axis=0)

            zs, ms = [], []
            for hd in range(SB_HPS):
                cols = slice(hd * SB_DH, (hd + 1) * SB_DH)
                z = _dot_nt(q_ref[r0:, cols], k_ref[pl.ds(off, TK), cols]) * scale
                m = _log_1m_beta(z)
                if causal is not None:
                    m = jnp.where(causal, m, 0.0)
                zs.append(z)
                ms.append(m)
            m_upto = _cumdot(jnp.concatenate(ms, axis=0), upto)
            ls, a_s, es = [], [], []
            for hd in range(SB_HPS):
                cols = slice(hd * SB_DH, (hd + 1) * SB_DH)
                l = ms[hd] + zs[hd]
                a = jnp.exp(l + ((ct_ref[hd, r0:] - carry[hd][1][r0:]) - m_upto[hd * nr:(hd + 1) * nr]))
                if causal is not None:
                    a = jnp.where(causal, a, 0.0)
                ls.append(l)
                a_s.append(a)
                es.append(_dot_nt(dy_ref[r0:, cols], v_ref[pl.ds(off, TK), cols]) * a)
            e_upto = _dot(jnp.concatenate(es, axis=0).astype(BF16), upto[:TK])
            out = []
            for hd in range(SB_HPS):
                dq, p_sum, e_sum = carry[hd]
                cols = slice(hd * SB_DH, (hd + 1) * SB_DH)
                e = es[hd]
                dz = e - jnp.exp(ls[hd]) * (e_sum[r0:] + e_upto[hd * nr:(hd + 1) * nr])
                if causal is not None:
                    dz = jnp.where(causal, dz, 0.0)
                dzs = (dz * scale).astype(BF16)
                dkv_ref[0, pl.ds(off, TK), cols] += _dot_tn(dzs, q_ref[r0:, cols])
                dkv_ref[1, pl.ds(off, TK), cols] += _dot_tn(a_s[hd].astype(BF16), dy_ref[r0:, cols])
                out.append((grow(dq, _dot(dzs, k_ref[pl.ds(off, TK), cols])),
                            grow(p_sum, jnp.sum(ms[hd], axis=1, keepdims=True)),
                            grow(e_sum, jnp.sum(e, axis=1, keepdims=True))))
            return tuple(out)

        zero = jnp.zeros((TQ, 1), F32)
        init = tuple((jnp.zeros((TQ, SB_DH), F32), zero, zero) for _ in range(SB_HPS))
        carry = lax.fori_loop(0, i * n_diag, lambda j, c: block(j, c, None), init)
        for d in range(n_diag):
            carry = block(i * n_diag + d, carry, d * TK)
        for hd in range(SB_HPS):
            dq_ref[:, hd * SB_DH:(hd + 1) * SB_DH] = carry[hd][0].astype(BF16)

    return _call(
        body, (qkv, qkv, qkv, dy, ctot, after), grid=(g, t // TQ), name=name, comm=comm,
        in_specs=[q_spec, k_spec, v_spec, q_spec, ct_spec, pl.BlockSpec(after.shape, lambda h, i: (0, 0))],
        out_specs=[q_spec, acc_spec],
        out_shape=[jax.ShapeDtypeStruct((t, D), BF16), jax.ShapeDtypeStruct((2, t, D), F32)],
        scratch_shapes=[])


def _gate_specs():
    return [pl.BlockSpec((TM, D), lambda i: (i, 3)), pl.BlockSpec((TM, D), lambda i: (i, 4))]


def _mix_pairs(mix_hbm, dsts):
    pairs = []
    for index, dst in enumerate(dsts):
        pairs += _square_pairs(mix_hbm, index, dst)
    return pairs


def _mix_out_fwd(yc, ysb, pcg, b_gate, h, w_mix, name, comm=None):
    t = h.shape[0]

    def body(yc_ref, ysb_ref, gc_ref, gs_ref, b_ref, h_ref, mix_hbm,
             a_ref, b_out_ref, mg_ref, h2_ref, wc_v, wa_v, wo_v, sems):
        _load_resident(pl.program_id(0), _mix_pairs(mix_hbm, (wc_v, wa_v, wo_v)), sems)
        a = _dot(yc_ref[...], wc_v[...])
        b = _dot(ysb_ref[...], wa_v[...])
        merged = (_sigmoid(gc_ref[...] + b_ref[:, :D]) * a + _sigmoid(gs_ref[...] + b_ref[:, D:]) * b).astype(BF16)
        a_ref[...] = a
        b_out_ref[...] = b
        mg_ref[...] = merged
        h2_ref[...] = h_ref[...] + _dot(merged, wo_v[...])

    return _call(
        body, (yc, ysb, pcg, pcg, b_gate, h, w_mix), grid=(t // TM,), name=name, comm=comm,
        in_specs=[_row_spec(TM, D), _row_spec(TM, D)] + _gate_specs()
                 + [_const_spec((1, 2 * D)), _row_spec(TM, D), _ANY],
        out_specs=[_row_spec(TM, D)] * 4,
        out_shape=[jax.ShapeDtypeStruct((t, D), F32), jax.ShapeDtypeStruct((t, D), F32),
                   jax.ShapeDtypeStruct((t, D), BF16), jax.ShapeDtypeStruct((t, D), F32)],
        scratch_shapes=[pltpu.VMEM((D, D), BF16)] * 3 + [pltpu.SemaphoreType.DMA((3 * N_DEV,))])


def _mix_out_bwd(dh2, a, b, pcg, b_gate, conv_w, w_mix, name, comm=None):
    t = dh2.shape[0]
    n_tile = t // TM
    per8 = TM // 8

    def rows(n):
        return pl.BlockSpec((TM, n), lambda i: (n_tile - 1 - i, 0))

    def cols(block):
        return pl.BlockSpec((TM, D), lambda i: (n_tile - 1 - i, block))

    def before(block):
        return pl.BlockSpec((8, D), lambda i: (jnp.maximum((n_tile - 1 - i) * per8 - 1, 0), block))

    def body(dh_ref, a_ref, b_ref, gc_ref, gs_ref, cb_ref, cc_ref, cx_ref, ccp_ref, cxp_ref, bias_ref, cw_ref, mix_hbm,
             dhb_ref, da_ref, db_ref, dgp_ref, dc_ref, dysb_ref, dbias_ref, dcw_ref, wc_v, wa_v, wo_v, head_v, sems):
        step = pl.program_id(0)
        _load_resident(step, _mix_pairs(mix_hbm, (wc_v, wa_v, wo_v)), sems)

        @pl.when(step == 0)
        def _():
            head_v[...] = jnp.zeros_like(head_v)
            dcw_ref[...] = jnp.zeros_like(dcw_ref)

        dhb = dh_ref[...].astype(BF16)
        dhb_ref[...] = dhb
        dm = _dot_nt(dhb, wo_v[...])
        gc = _sigmoid(gc_ref[...] + bias_ref[:, :D])
        gs = _sigmoid(gs_ref[...] + bias_ref[:, D:])
        da = (dm * gc).astype(BF16)
        db = (dm * gs).astype(BF16)
        da_ref[...] = da
        db_ref[...] = db
        dgc = dm * a_ref[...] * (gc * (1.0 - gc))
        dgs = dm * b_ref[...] * (gs * (1.0 - gs))
        dgp_ref[0] = dgc.astype(BF16)
        dgp_ref[1] = dgs.astype(BF16)
        _accumulate(dbias_ref.at[:, :D], step, jnp.sum(dgc, axis=0, keepdims=True))
        _accumulate(dbias_ref.at[:, D:], step, jnp.sum(dgs, axis=0, keepdims=True))
        dysb_ref[...] = _dot_nt(db, wa_v[...]).astype(BF16)
        dyc = _dot_nt(da, wc_v[...])
        cc, cx = cc_ref[...], cx_ref[...]
        xc = cc * cx
        xc_before = jnp.where(step == n_tile - 1, 0.0, ccp_ref[...] * cxp_ref[...])
        ext = jnp.concatenate([xc_before, xc], axis=0)
        x1 = pltpu.roll(ext, 1, 0)[8:]
        x2 = pltpu.roll(ext, 2, 0)[8:]
        w0, w1, w2 = cw_ref[0:1, :], cw_ref[1:2, :], cw_ref[2:3, :]
        dc_ref[0] = (dyc * (w0 * x2 + w1 * x1 + w2 * xc)).astype(BF16)
        dconv = dyc * cb_ref[...]
        dcw_ref[0:1, :] += jnp.sum(dconv * x2, axis=0, keepdims=True)
        dcw_ref[1:2, :] += jnp.sum(dconv * x1, axis=0, keepdims=True)
        dcw_ref[2:3, :] += jnp.sum(dconv * xc, axis=0, keepdims=True)
        after = jnp.concatenate([dconv, head_v[...]], axis=0)
        dxc = w2 * dconv + w1 * pltpu.roll(after, TM + 7, 0)[:TM] + w0 * pltpu.roll(after, TM + 6, 0)[:TM]
        dc_ref[1] = (dxc * cx).astype(BF16)
        dc_ref[2] = (dxc * cc).astype(BF16)
        head_v[...] = dconv[:8]

    return _call(
        body, (dh2, a, b, pcg, pcg, pcg, pcg, pcg, pcg, pcg, b_gate, conv_w, w_mix), grid=(n_tile,), name=name,
        comm=comm,
        in_specs=[rows(D)] * 3 + [cols(3), cols(4), cols(0), cols(1), cols(2), before(1), before(2),
                                  _const_spec((1, 2 * D)), _const_spec((CONV_K, D)), _ANY],
        out_specs=[rows(D)] * 3 + [pl.BlockSpec((2, TM, D), lambda i: (0, n_tile - 1 - i, 0)),
                                   pl.BlockSpec((3, TM, D), lambda i: (0, n_tile - 1 - i, 0)), rows(D),
                                   _const_spec((1, 2 * D)), _const_spec((8, D))],
        out_shape=[jax.ShapeDtypeStruct((t, D), BF16)] * 3
                  + [jax.ShapeDtypeStruct((2, t, D), BF16), jax.ShapeDtypeStruct((3, t, D), BF16),
                     jax.ShapeDtypeStruct((t, D), BF16), jax.ShapeDtypeStruct((1, 2 * D), F32),
                     jax.ShapeDtypeStruct((8, D), F32)],
        scratch_shapes=[pltpu.VMEM((D, D), BF16)] * 3 + [pltpu.VMEM((8, D), F32),
                                                         pltpu.SemaphoreType.DMA((3 * N_DEV,))])


def _inproj_bwd(dconv, dq, dkv, dgp, w_in, h, g, dh_res, name, comm=None):
    t = h.shape[0]

    def body(dc_ref, dq_ref, dkv_ref, dgp_ref, w_hbm, h_ref, g_ref, dres_ref, dh_ref, dg_ref, w_v, sems):
        step = pl.program_id(0)
        _load_resident(step, [(w_hbm, w_v)], sems)
        du = _dot_nt(dq_ref[...], w_v[3])
        for k in range(3):
            du = du + _dot_nt(dc_ref[k], w_v[k])
        for k in range(2):
            du = du + _dot_nt(dkv_ref[k].astype(BF16), w_v[4 + k]) + _dot_nt(dgp_ref[k], w_v[6 + k])
        dx, dg = _rms_bwd_tile(h_ref[...], g_ref[...], du)
        dh_ref[...] = dres_ref[...] + dx
        _accumulate(dg_ref, step, dg)

    return _call(
        body, (dconv, dq, dkv, dgp, w_in, h, g, dh_res), grid=(t // TM,), name=name, comm=comm,
        in_specs=[_blk_row_spec(3, TM, D), _row_spec(TM, D), _blk_row_spec(2, TM, D), _blk_row_spec(2, TM, D), _ANY,
                  _row_spec(TM, D), _const_spec((1, D)), _row_spec(TM, D)],
        out_specs=[_row_spec(TM, D), _const_spec((1, D))],
        out_shape=[jax.ShapeDtypeStruct((t, D), F32), jax.ShapeDtypeStruct((1, D), F32)],
        scratch_shapes=[pltpu.VMEM((N_DEV, D, D), BF16), pltpu.SemaphoreType.DMA((1,))])


def _softmax_rows(s):
    e = jnp.exp(s - jnp.max(s, axis=-1, keepdims=True))
    return e / jnp.sum(e, axis=-1, keepdims=True)


def _cross_pairs(cross_hbm, wq_v, wo_v):
    return _square_pairs(cross_hbm, 0, wq_v) + _square_pairs(cross_hbm, 1, wo_v)


def _cross_fwd(h, g, mem, g_mem, w_ckv, w_cross, name):
    t = h.shape[0]
    m = mem.shape[0]
    scale = X_DH ** -0.5

    def body(h_ref, g_ref, mem_ref, gm_ref, wkv_ref, cross_hbm, hn_ref, qx_ref, o_ref, h3_ref, mn_ref, kv_ref,
             wq_v, wo_v, sems):
        _load_resident(pl.program_id(0), _cross_pairs(cross_hbm, wq_v, wo_v), sems)

        @pl.when(pl.program_id(0) == 0)
        def _():
            mn = _rms_fwd_tile(mem_ref[...], gm_ref[...]).astype(BF16)
            mn_ref[...] = mn
            for j in range(N_DEV):
                kv_ref[j] = _dot(mn, wkv_ref[j]).astype(BF16)

        ht = h_ref[...]
        hn = _rms_fwd_tile(ht, g_ref[...]).astype(BF16)
        hn_ref[...] = hn
        qx = _dot(hn, wq_v[...]).astype(BF16)
        qx_ref[...] = qx
        for hd in range(X_H):
            lo, hi = hd * X_DH, (hd + 1) * X_DH
            p = _softmax_rows(_dot_nt(qx[:, lo:hi], kv_ref[hd]) * scale)
            o_ref[:, lo:hi] = _dot(p.astype(BF16), kv_ref[X_H + hd]).astype(BF16)
        h3_ref[...] = ht + _dot(o_ref[...], wo_v[...])

    return pl.pallas_call(
        body, grid=(t // TM,), name=name,
        in_specs=[_row_spec(TM, D), _const_spec((1, D)), _const_spec((m, D)), _const_spec((1, D)),
                  _const_spec((N_DEV, D, X_DH)), _ANY],
        out_specs=[_row_spec(TM, D)] * 4 + [_const_spec((m, D)), _const_spec((N_DEV, m, X_DH))],
        out_shape=[jax.ShapeDtypeStruct((t, D), BF16)] * 3 + [jax.ShapeDtypeStruct((t, D), F32),
                                                              jax.ShapeDtypeStruct((m, D), BF16),
                                                              jax.ShapeDtypeStruct((N_DEV, m, X_DH), BF16)],
        scratch_shapes=[pltpu.VMEM((D, D), BF16)] * 2 + [pltpu.SemaphoreType.DMA((2 * N_DEV,))],
        compiler_params=_cparams(),
    )(h, g, mem, g_mem, w_ckv, w_cross)


def _cross_bwd(dh3, h, g, qx, kv, mem, g_mem, w_ckv, w_cross, name, comm=None):
    t = h.shape[0]
    m = kv.shape[1]
    scale = X_DH ** -0.5

    def body(dh_ref, h_ref, g_ref, qx_ref, kv_ref, mem_ref, gm_ref, wkv_ref, cross_hbm,
             dhb_ref, dqx_ref, dkv_ref, dh2_ref, dg_ref, dgm_ref, wq_v, wo_v, sems):
        step = pl.program_id(0)
        _load_resident(step, _cross_pairs(cross_hbm, wq_v, wo_v), sems)

        @pl.when(step == 0)
        def _():
            dkv_ref[...] = jnp.zeros_like(dkv_ref)

        dht = dh_ref[...]
        dhb = dht.astype(BF16)
        dhb_ref[...] = dhb
        do = _dot_nt(dhb, wo_v[...]).astype(BF16)
        for hd in range(X_H):
            lo, hi = hd * X_DH, (hd + 1) * X_DH
            qh = qx_ref[:, lo:hi]
            kh = kv_ref[hd]
            p = _softmax_rows(_dot_nt(qh, kh) * scale)
            doh = do[:, lo:hi]
            dp = _dot_nt(doh, kv_ref[X_H + hd])
            ds = (p * (dp - jnp.sum(dp * p, axis=-1, keepdims=True)) * scale).astype(BF16)
            dqx_ref[:, lo:hi] = _dot(ds, kh).astype(BF16)
            dkv_ref[hd] += _dot_tn(ds, qh)
            dkv_ref[X_H + hd] += _dot_tn(p.astype(BF16), doh)
        dhn = _dot_nt(dqx_ref[...], wq_v[...])
        dx, dg = _rms_bwd_tile(h_ref[...], g_ref[...], dhn)
        dh2_ref[...] = dht + dx
        _accumulate(dg_ref, step, dg)

        @pl.when(step == t // TM - 1)
        def _():
            dmn = jnp.zeros((m, D), F32)
            for j in range(N_DEV):
                dmn = dmn + _dot_nt(dkv_ref[j].astype(BF16), wkv_ref[j])
            dgm_ref[...] = _rms_bwd_tile(mem_ref[...], gm_ref[...], dmn)[1]

    return _call(
        body, (dh3, h, g, qx, kv, mem, g_mem, w_ckv, w_cross), grid=(t // TM,), name=name, comm=comm,
        in_specs=[_row_spec(TM, D), _row_spec(TM, D), _const_spec((1, D)), _row_spec(TM, D),
                  _const_spec((N_DEV, m, X_DH)), _const_spec((m, D)), _const_spec((1, D)),
                  _const_spec((N_DEV, D, X_DH)), _ANY],
        out_specs=[_row_spec(TM, D), _row_spec(TM, D), _const_spec((N_DEV, m, X_DH)), _row_spec(TM, D),
                   _const_spec((1, D)), _const_spec((1, D))],
        out_shape=[jax.ShapeDtypeStruct((t, D), BF16), jax.ShapeDtypeStruct((t, D), BF16),
                   jax.ShapeDtypeStruct((N_DEV, m, X_DH), F32), jax.ShapeDtypeStruct((t, D), F32),
                   jax.ShapeDtypeStruct((1, D), F32), jax.ShapeDtypeStruct((1, D), F32)],
        scratch_shapes=[pltpu.VMEM((D, D), BF16)] * 2 + [pltpu.SemaphoreType.DMA((2 * N_DEV,))])


def _adamw(w, parts, m, v, name, row_block=0, token=None):
    r, c = w.shape
    n = parts.shape[0]
    tr = _pick_tile(r, (256, 352, 128))
    off = row_block * (r // tr)

    def body(*refs):
        if token is None:
            _adamw_update(None, *refs)
        else:
            _adamw_update(refs[4], *refs[:4], *refs[5:])

    spec = _row_spec(tr, c)
    in_specs = [spec, pl.BlockSpec((n, tr, c), lambda i: (0, i + off, 0)), spec, spec]
    operands = (w, parts, m, v)
    if token is not None:
        in_specs.append(_const_spec(token.shape))
        operands += (token,)
    return pl.pallas_call(
        body, grid=(r // tr,), name=name, in_specs=in_specs, out_specs=[spec] * 4,
        out_shape=[jax.ShapeDtypeStruct((r, c), F32)] * 4,
        compiler_params=_cparams(),
    )(*operands)


def _adamw_update(tok_ref, w_ref, p_ref, m_ref, v_ref, g_ref, d_ref, nm_ref, nv_ref):
    gt = p_ref[0].astype(F32)
    for k in range(1, p_ref.shape[0]):
        gt = gt + p_ref[k].astype(F32)
    if tok_ref is not None:
        gt = gt + tok_ref[0:1, 0:1]
    _adamw_apply(gt, w_ref, m_ref, v_ref, g_ref, d_ref, nm_ref, nv_ref)


def _adamw_own(w, land, own, chip, m, v, name, row_block=0, token=None):
    r, c = w.shape
    tr = _pick_tile(r, (256, 352, 128))
    off = row_block * (r // tr)

    def body(chip_ref, w_ref, land_ref, own_ref, m_ref, v_ref, *rest):
        mine = own_ref[0].astype(F32)
        gt = jnp.where(chip_ref[0] == 0, mine, land_ref[0].astype(F32))
        for k in range(1, N_CHIP):
            gt = gt + jnp.where(chip_ref[0] == k, mine, land_ref[k].astype(F32))
        if token is not None:
            gt = gt + rest[0][0:1, 0:1]
        _adamw_apply(gt, w_ref, m_ref, v_ref, *rest[-4:])

    spec = pl.BlockSpec((tr, c), lambda i, chip_ref: (i, 0))
    in_specs = [spec, pl.BlockSpec((N_CHIP, tr, c), lambda i, chip_ref: (0, i + off, 0)),
                pl.BlockSpec((1, tr, c), lambda i, chip_ref: (chip_ref[0], i + off, 0)), spec, spec]
    operands = (chip, w, land, own, m, v)
    if token is not None:
        in_specs.append(pl.BlockSpec(token.shape, lambda i, chip_ref: (0, 0)))
        operands += (token,)
    return pl.pallas_call(
        body, name=name,
        grid_spec=pltpu.PrefetchScalarGridSpec(
            num_scalar_prefetch=1, grid=(r // tr,), in_specs=in_specs, out_specs=[spec] * 4),
        out_shape=[jax.ShapeDtypeStruct((r, c), F32)] * 4,
        compiler_params=_cparams(),
    )(*operands)


def _adamw_apply(gt, w_ref, m_ref, v_ref, g_ref, d_ref, nm_ref, nv_ref):
    g_ref[...] = gt
    nm = ADAM_B1 * m_ref[...] + (1.0 - ADAM_B1) * gt
    nv = ADAM_B2 * v_ref[...] + (1.0 - ADAM_B2) * jnp.square(gt)
    m_hat = nm / (1.0 - ADAM_B1 ** ADAM_STEP)
    v_hat = nv / (1.0 - ADAM_B2 ** ADAM_STEP)
    d_ref[...] = -ADAM_LR * (m_hat / (jnp.sqrt(v_hat) + ADAM_EPS) + ADAM_WD * w_ref[...])
    nm_ref[...] = nm
    nv_ref[...] = nv


def _mesh_pos():
    return lax.axis_index("x"), lax.axis_index("y"), lax.axis_index("c")


def _no_round(in_refs, out_refs, sems):
    pass


def _run_exchange(comm, name):
    c_in, c_out = len(comm.inputs), len(comm.out_shapes)

    def body(*refs):
        cins, couts, sems = refs[:c_in], refs[c_in:c_in + c_out], refs[c_in + c_out:]
        comm.start(cins, couts, sems)
        comm.middle(cins, couts, sems)
        comm.finish(cins, couts, sems)

    return list(pl.pallas_call(
        body, name=name, out_shape=list(comm.out_shapes),
        in_specs=[_ANY] * c_in, out_specs=[_ANY] * c_out, scratch_shapes=list(comm.sem_shapes),
    )(*comm.inputs))


def _gather_exchange(shards):
    n_arr = len(shards)

    def plan(x_refs, out_refs, sems):
        send_sems, recv_sems, local_sems = sems[:3]
        stage = sems[3:]
        x, y, c = _mesh_pos()
        me, sibling = (x, y, c), (x, y, 1 - c)
        xn, yn, diag = (1 - x, y), (x, 1 - y), (1 - x, 1 - y)

        def slot(a, px, py, pc, half=None):
            ref = out_refs[a].at[4 * px + 2 * py + pc]
            if half is None:
                return ref
            rows = shards[a].shape[0] // 2
            return ref.at[half * rows:(half + 1) * rows]

        def copy(a, k, block, to, half=None, src=None):
            dst = slot(a, *block, half)
            return pltpu.make_async_remote_copy(
                src_ref=dst if src is None else src, dst_ref=dst,
                send_sem=send_sems.at[a, k], recv_sem=recv_sems.at[a, k],
                device_id=to, device_id_type=pl.DeviceIdType.MESH)

        return types.SimpleNamespace(
            me=me, sibling=sibling, xn=xn, yn=yn, diag=diag, c=c, copy=copy,
            mine_in=[pltpu.make_async_copy(x_refs[a], stage[a], local_sems.at[a, 0]) for a in range(n_arr)],
            mine_out=[pltpu.make_async_copy(stage[a], slot(a, *me), local_sems.at[a, 1]) for a in range(n_arr)],
            first=[cp for a in range(n_arr) for cp in (
                copy(a, 0, me, sibling, src=x_refs[a]), copy(a, 1, me, (*xn, c), src=x_refs[a]),
                copy(a, 2, me, (*yn, c), src=x_refs[a]))],
            second=lambda a: (copy(a, 3, (*xn, c), (*yn, c), half=0), copy(a, 5, (*xn, c), sibling),
                              copy(a, 4, (*yn, c), (*xn, c), half=1), copy(a, 6, (*yn, c), sibling)),
            third=lambda a: (copy(a, 7, (*diag, c), sibling, half=0), copy(a, 8, (*diag, c), sibling, half=1)))

    def start(x_refs, out_refs, sems):
        p = plan(x_refs, out_refs, sems)
        for cp in p.first + p.mine_in:
            cp.start()
        for cp_in, cp_out in zip(p.mine_in, p.mine_out):
            cp_in.wait()
            cp_out.start()

    def middle(x_refs, out_refs, sems):
        p = plan(x_refs, out_refs, sems)
        for a in range(n_arr):
            to_yn, x_to_sib, to_xn, y_to_sib = p.second(a)
            p.copy(a, 1, (*p.xn, p.c), p.me).wait_recv()
            to_yn.start()
            x_to_sib.start()
            p.copy(a, 2, (*p.yn, p.c), p.me).wait_recv()
            to_xn.start()
            y_to_sib.start()

    def finish(x_refs, out_refs, sems):
        p = plan(x_refs, out_refs, sems)
        for a in range(n_arr):
            half0_to_sib, half1_to_sib = p.third(a)
            p.copy(a, 3, (*p.diag, p.c), p.me, half=0).wait_recv()
            half0_to_sib.start()
            p.copy(a, 4, (*p.diag, p.c), p.me, half=1).wait_recv()
            half1_to_sib.start()
        other = 1 - p.c
        for a in range(n_arr):
            p.copy(a, 0, p.sibling, p.me).wait_recv()
            p.copy(a, 5, (*p.xn, other), p.me).wait_recv()
            p.copy(a, 6, (*p.yn, other), p.me).wait_recv()
            p.copy(a, 7, (*p.diag, other), p.me, half=0).wait_recv()
            p.copy(a, 8, (*p.diag, other), p.me, half=1).wait_recv()
        for cp in p.first:
            cp.wait_send()
        for a in range(n_arr):
            for cp in p.second(a) + p.third(a):
                cp.wait_send()
        for cp in p.mine_out:
            cp.wait()

    return types.SimpleNamespace(
        inputs=list(shards), start=start, middle=middle, finish=finish,
        out_shapes=[jax.ShapeDtypeStruct((N_DEV,) + s.shape, s.dtype) for s in shards],
        sem_shapes=[pltpu.SemaphoreType.DMA((n_arr, 9)), pltpu.SemaphoreType.DMA((n_arr, 9)),
                    pltpu.SemaphoreType.DMA((n_arr, 2))] + [pltpu.VMEM(s.shape, s.dtype) for s in shards])


def _pair_exchange(grads):
    n_arr = len(grads)

    def plan(g_refs, land_refs, sems):
        send_sems, recv_sems = sems
        x, y, c = _mesh_pos()
        return [pltpu.make_async_remote_copy(
            src_ref=g_refs[a].at[2 * k + 1 - c], dst_ref=land_refs[a].at[k],
            send_sem=send_sems.at[a, k], recv_sem=recv_sems.at[a, k],
            device_id=(x, y, 1 - c), device_id_type=pl.DeviceIdType.MESH)
            for a in range(n_arr) for k in range(N_CHIP)]

    def start(g_refs, land_refs, sems):
        for cp in plan(g_refs, land_refs, sems):
            cp.start()

    def finish(g_refs, land_refs, sems):
        for cp in plan(g_refs, land_refs, sems):
            cp.wait()

    return types.SimpleNamespace(
        inputs=list(grads), start=start, middle=_no_round, finish=finish,
        out_shapes=[jax.ShapeDtypeStruct((N_CHIP,) + g.shape[1:], g.dtype) for g in grads],
        sem_shapes=[pltpu.SemaphoreType.DMA((n_arr, N_CHIP)), pltpu.SemaphoreType.DMA((n_arr, N_CHIP))])


def _chip_exchange(parts):
    n_arr = len(parts)

    def plan(p_refs, land_refs, sems):
        send_sems, recv_sems, local_sems = sems
        x, y, c = _mesh_pos()
        my_chip = 2 * x + y
        chips = [(1 - x, y), (x, 1 - y), (1 - x, 1 - y)]
        local = [pltpu.make_async_copy(p_refs[a].at[my_chip], land_refs[a].at[my_chip], local_sems.at[a])
                 for a in range(n_arr)]

        def copy(a, k, src_slot, dst_slot, px, py):
            return pltpu.make_async_remote_copy(
                src_ref=p_refs[a].at[src_slot], dst_ref=land_refs[a].at[dst_slot],
                send_sem=send_sems.at[a, k], recv_sem=recv_sems.at[a, k],
                device_id=(px, py, c), device_id_type=pl.DeviceIdType.MESH)

        sends = [copy(a, k, 2 * px + py, my_chip, px, py) for a in range(n_arr) for k, (px, py) in enumerate(chips)]
        arrivals = [copy(a, k, my_chip, 2 * px + py, px, py) for a in range(n_arr)
                    for k, (px, py) in enumerate(chips)]
        return local, sends, arrivals

    def start(p_refs, land_refs, sems):
        local, sends, _ = plan(p_refs, land_refs, sems)
        for cp in local + sends:
            cp.start()

    def finish(p_refs, land_refs, sems):
        local, sends, arrivals = plan(p_refs, land_refs, sems)
        for cp in arrivals:
            cp.wait_recv()
        for cp in sends:
            cp.wait_send()
        for cp in local:
            cp.wait()

    return types.SimpleNamespace(
        inputs=list(parts), start=start, middle=_no_round, finish=finish,
        out_shapes=[jax.ShapeDtypeStruct(p.shape, p.dtype) for p in parts],
        sem_shapes=[pltpu.SemaphoreType.DMA((n_arr, 3)), pltpu.SemaphoreType.DMA((n_arr, 3)),
                    pltpu.SemaphoreType.DMA((n_arr,))])


_HBM = pl.BlockSpec(memory_space=pltpu.HBM)
_SEM = pl.BlockSpec(memory_space=pltpu.SEMAPHORE)
_DATAFLOW = pltpu.SideEffectType.DATAFLOW_SIDE_EFFECTING


def _chip_copies(p_refs, land_refs, send_sems, recv_sems):
    x, y, c = _mesh_pos()
    my_chip = 2 * x + y
    chips = [(1 - x, y), (x, 1 - y), (1 - x, 1 - y)]
    return [pltpu.make_async_remote_copy(
        src_ref=p_refs[a].at[2 * px + py], dst_ref=land_refs[a].at[my_chip],
        send_sem=send_sems[3 * a + k], recv_sem=recv_sems[3 * a + k],
        device_id=(px, py, c), device_id_type=pl.DeviceIdType.MESH)
        for a in range(len(p_refs)) for k, (px, py) in enumerate(chips)]


def _chip_exchange_begin(parts, name):
    n_arr = len(parts)
    n_buf, n_copy = 2 * n_arr, 3 * n_arr
    lands = [lax.empty(p.shape, p.dtype) for p in parts]

    def body(*refs):
        p_refs, land_refs = refs[:n_arr], refs[n_arr:n_buf]
        send_sems, recv_sems, token = refs[n_buf:n_buf + n_copy], refs[n_buf + n_copy:n_buf + 2 * n_copy], refs[-1]
        for cp in _chip_copies(p_refs, land_refs, send_sems, recv_sems):
            cp.start()
        token[...] = jnp.zeros_like(token)

    bufs = list(parts) + list(lands)
    outs = pl.pallas_call(
        body, name=name,
        out_shape=(*[pltpu.SemaphoreType.DMA(())] * (2 * n_copy), *[pltpu.HBM(b.shape, b.dtype) for b in bufs],
                   jax.ShapeDtypeStruct((8, 128), F32)),
        in_specs=[_HBM] * n_buf,
        out_specs=(*[_SEM] * (2 * n_copy), *[_HBM] * n_buf, pl.BlockSpec(memory_space=pltpu.VMEM)),
        input_output_aliases={i: 2 * n_copy + i for i in range(n_buf)},
        compiler_params=pltpu.CompilerParams(has_side_effects=_DATAFLOW),
    )(*[pltpu.with_memory_space_constraint(b, pltpu.HBM) for b in bufs])
    sems = list(outs[:2 * n_copy])
    thru = list(outs[2 * n_copy:2 * n_copy + n_buf])
    return types.SimpleNamespace(send_sems=sems[:n_copy], recv_sems=sems[n_copy:], parts=thru[:n_arr],
                                 lands=thru[n_arr:], token=outs[-1])


def _chip_exchange_end(flight, after, name):
    send_sems, recv_sems, parts, lands = flight.send_sems, flight.recv_sems, flight.parts, flight.lands
    n_arr = len(parts)
    n_buf, n_copy = 2 * n_arr, 3 * n_arr

    def body(*refs):
        p_refs, land_refs = refs[:n_arr], refs[n_arr:n_buf]
        sems = refs[n_buf:n_buf + 2 * n_copy]
        for cp in _chip_copies(p_refs, land_refs, sems[:n_copy], sems[n_copy:]):
            cp.wait_send()
            cp.wait_recv()

    bufs = list(parts) + list(lands)
    outs = pl.pallas_call(
        body, name=name, out_shape=tuple(pltpu.HBM(b.shape, b.dtype) for b in bufs),
        in_specs=[_HBM] * n_buf + [_SEM] * (2 * n_copy) + [_ANY], out_specs=tuple([_HBM] * n_buf),
        input_output_aliases={i: i for i in range(n_buf)},
        compiler_params=pltpu.CompilerParams(has_side_effects=_DATAFLOW),
    )(*bufs, *send_sems, *recv_sems, after)
    return list(outs[:n_arr]), list(outs[n_arr:])


def _row_tile(r, cap=640):
    best = None
    for cand in range(16, min(r, cap) + 1, 16):
        if r % cand == 0:
            best = cand
    return best if best is not None else r


def _pair_sum(gs, landeds, core, name):
    tiles = [_row_tile(g.shape[1]) for g in gs]
    counts = [g.shape[1] // tr for g, tr in zip(gs, tiles)]
    n_arr = len(gs)

    def body(core_ref, *refs):
        for a in range(n_arr):
            mine, theirs, out = refs[2 * a], refs[2 * a + 1], refs[2 * n_arr + a]
            out[0] = (mine[0].astype(F32) + theirs[0].astype(F32)).astype(out.dtype)

    in_specs, out_specs, operands = [], [], []
    for g, landed, tr, count in zip(gs, landeds, tiles, counts):
        c_dim = g.shape[2]
        last = count - 1
        in_specs += [pl.BlockSpec((1, tr, c_dim),
                                  lambda k, i, core_ref, last=last: (2 * k + core_ref[0], jnp.minimum(i, last), 0)),
                     pl.BlockSpec((1, tr, c_dim), lambda k, i, core_ref, last=last: (k, jnp.minimum(i, last), 0))]
        out_specs.append(pl.BlockSpec((1, tr, c_dim), lambda k, i, core_ref, last=last: (k, jnp.minimum(i, last), 0)))
        operands += [g, landed]
    return list(pl.pallas_call(
        body, name=name,
        grid_spec=pltpu.PrefetchScalarGridSpec(
            num_scalar_prefetch=1, grid=(N_CHIP, max(counts)), in_specs=in_specs, out_specs=out_specs),
        out_shape=[jax.ShapeDtypeStruct((N_CHIP,) + g.shape[1:], g.dtype) for g in gs],
        compiler_params=_cparams(2),
    )(core, *operands))


def _sum_slots(parts, name):
    n, r, c_dim = parts.shape
    tr = _row_tile(r)

    def body(p_ref, o_ref):
        acc = p_ref[0].astype(F32)
        for k in range(1, n):
            acc = acc + p_ref[k].astype(F32)
        o_ref[...] = acc

    return pl.pallas_call(
        body, grid=(r // tr,), name=name,
        in_specs=[pl.BlockSpec((n, tr, c_dim), lambda i: (0, i, 0))],
        out_specs=_row_spec(tr, c_dim),
        out_shape=jax.ShapeDtypeStruct((r, c_dim), F32),
        compiler_params=_cparams(),
    )(parts)


GAINS = ("g_ffn1", "g_mix", "g_cross", "g_mem", "g_ffn2", "g_final")
SMALL = GAINS + ("b_gate", "conv_w")
SMALL_R = 16
LOSS_ROW = 11
WEIGHT_ORDER = ("g_ffn1", "w_ffn1_gu", "w_ffn1_down", "g_mix", "w_in", "b_gate", "conv_w", "w_conv_out",
                "w_attn_out", "w_o", "g_cross", "g_mem", "w_cq", "w_ckv", "w_co", "g_ffn2", "w_ffn2_gu",
                "w_ffn2_down", "g_final")
GU_NAMES = ("w_ffn1_gu", "w_ffn2_gu")


def _pack_small(vals, conv_rows):
    rows = [vals[n].reshape(1, D) for n in GAINS] + [vals["b_gate"].reshape(2, D), conv_rows.reshape(CONV_K, D)]
    used = len(GAINS) + 2 + CONV_K
    return jnp.concatenate(rows + [jnp.zeros((SMALL_R - used, D), F32)], axis=0)


def _unpack_small(buf):
    out = {n: buf[k] for k, n in enumerate(GAINS)}
    out["b_gate"] = buf[6:8].reshape(2 * D)
    out["conv_w"] = buf[8:8 + CONV_K]
    return out


def _exchange_shards(wts):
    out = {n: jnp.pad(wts[n].T.astype(BF16), ((0, FF_PAD - FF_BLK), (0, 0))) for n in GU_NAMES}
    for n in ("w_ckv", "w_in", "w_ffn1_down", "w_ffn2_down"):
        out[n] = wts[n].astype(BF16)
    out["mix"] = jnp.concatenate([wts[n].astype(BF16) for n in MIX_MATS], axis=0)
    out["cross"] = jnp.concatenate([wts[n].astype(BF16) for n in CROSS_MATS], axis=0)
    return out


def _reduce_group(grads, landed, core, names):
    return _pair_sum(grads, landed, core, "grads_pair_sum_" + "_".join(names))


def _step(x, mem, target, sh, conv_pad, gains, b_gate, core):
    wg1, wd1, conv_all = _run_exchange(_gather_exchange([sh["w_ffn1_gu"], sh["w_ffn1_down"], conv_pad]), "gather_ffn1")
    conv_w = conv_all[:, :CONV_K, :].transpose(1, 0, 2).reshape(CONV_K, D)
    (n1, gate1, up1, act1, h1), (w_in,) = _ffn_fwd(
        x, gains["g_ffn1"], wg1, wd1, "ffn1_fwd", comm=_gather_exchange([sh["w_in"]]))
    (u, pcg, qkv, yc), (w_mix,) = _inproj_fwd(h1, gains["g_mix"], w_in, conv_w, "inproj_fwd",
                                              comm=_gather_exchange([sh["mix"]]))
    (ysb, ctot), (w_cross, w_ckv, wg2) = _sb_fwd(
        qkv, "sb_fwd", comm=_gather_exchange([sh["cross"], sh["w_ckv"], sh["w_ffn2_gu"]]))
    (a_mix, b_mix, merged, h2), (wd2,) = _mix_out_fwd(yc, ysb, pcg, b_gate, h1, w_mix, "mix_out_fwd",
                                                      comm=_gather_exchange([sh["w_ffn2_down"]]))
    hn, qx, o_x, h3, mn, kv = _cross_fwd(h2, gains["g_cross"], mem, gains["g_mem"], w_ckv, w_cross, "cross_fwd")
    (n4, gate2, up2, act2, dh4, loss, dg_final), _ = _ffn_fwd(h3, gains["g_ffn2"], wg2, wd2, "ffn2_fwd",
                                                              head=(gains["g_final"], target))

    gs = {"g_final": dg_final}
    (dgu2, dh4b, dh3, gs["g_ffn2"]), _ = _ffn_bwd(dh4, h3, gains["g_ffn2"], gate2, up2, wg2, wd2, "ffn2_bwd")
    grads_a = [_mm_tn_rows(dgu2, n4, FF_PAD, "dw_ffn2_gu"),
               _mm_tn_rows(act2, dh4b, FF_BLK, "dw_ffn2_down").reshape(N_DEV, DOWN_ROWS, D)]
    names_a = ["w_ffn2_gu", "w_ffn2_down"]
    (dh3b, dqx, dkv, dh2, gs["g_cross"], gs["g_mem"]), landed_a = _cross_bwd(
        dh3, h2, gains["g_cross"], qx, kv, mem, gains["g_mem"], w_ckv, w_cross, "cross_bwd",
        comm=_pair_exchange(grads_a))
    sums_a = _reduce_group(grads_a, landed_a, core, names_a)
    cross_stack = _mm_tn_square(hn, dqx, "dw_cq", 0, len(CROSS_MATS))
    grads_b = [_mm_tn_cols(mn, dkv, "dw_ckv"), _mm_tn_square(o_x, dh3b, "dw_co", 1, len(CROSS_MATS), cross_stack)]
    names_b = ["w_ckv", "cross"]
    (dh2b, da_mix, db_mix, dgp, dconv, dysb, gs["b_gate"], gs["conv_w"]), landed_b = _mix_out_bwd(
        dh2, a_mix, b_mix, pcg, b_gate, conv_w, w_mix, "mix_out_bwd", comm=_pair_exchange(grads_b))
    sums_b = _reduce_group(grads_b, landed_b, core, names_b)
    mix_stack = _mm_tn_square(yc, da_mix, "dw_conv_out", 0, len(MIX_MATS))
    mix_stack = _mm_tn_square(ysb, db_mix, "dw_attn_out", 1, len(MIX_MATS), mix_stack)
    grads_c = [_mm_tn_square(merged, dh2b, "dw_o", 2, len(MIX_MATS), mix_stack)]
    flight_ab = _chip_exchange_begin(sums_a + sums_b, "grads_to_chips_early_begin")
    (dq, dkv_sb), landed_c = _sb_bwd(qkv, dysb, ctot, flight_ab.token, "sb_bwd", comm=_pair_exchange(grads_c))
    sums_c = _reduce_group(grads_c, landed_c, core, ["mix"])
    w_in_stack = _mm_tn_cols(u, dconv, "dw_in_conv", 0, N_DEV)
    w_in_stack = _mm_tn_cols(u, dq[None], "dw_in_q", 3, N_DEV, w_in_stack)
    w_in_stack = _mm_tn_cols(u, dkv_sb, "dw_in_kv", 4, N_DEV, w_in_stack)
    grads_d = [_mm_tn_cols(u, dgp, "dw_in_gates", 6, N_DEV, w_in_stack)]
    (dh1, gs["g_mix"]), landed_d = _inproj_bwd(dconv, dq, dkv_sb, dgp, w_in, h1, gains["g_mix"], dh2, "inproj_bwd",
                                               comm=_pair_exchange(grads_d))
    sums_d = _reduce_group(grads_d, landed_d, core, ["w_in"])
    flight_d = _chip_exchange_begin(sums_c + sums_d, "grads_to_chips_w_in_begin")
    (dgu1, dh1b, dx, gs["g_ffn1"]), _ = _ffn_bwd(dh1, x, gains["g_ffn1"] + flight_d.token[0, 0], gate1, up1, wg1, wd1,
                                                 "ffn1_bwd")
    dw_gu1 = _mm_tn_rows(dgu1, n1, FF_PAD, "dw_ffn1_gu")
    dw_down1, landed_gu1 = _mm_tn_rows(act1, dh1b, FF_BLK, "dw_ffn1_down", comm=_pair_exchange([dw_gu1]))
    grads_e = [dw_gu1, dw_down1.reshape(N_DEV, DOWN_ROWS, D)]
    names_e = ["w_ffn1_gu", "w_ffn1_down"]
    landed_e = landed_gu1 + _run_exchange(_pair_exchange(grads_e[1:]), "grads_to_sibling_ffn1_down")
    flight_e = _chip_exchange_begin(_reduce_group(grads_e, landed_e, core, names_e), "grads_to_chips_ffn1_begin")
    flights = [(names_a + names_b, flight_ab), (["mix", "w_in"], flight_d), (names_e, flight_e)]
    return loss, dx, flights, gs


def kernel(x, mem, g_ffn1, w_ffn1_gu, w_ffn1_down, g_mix, w_in, b_gate, conv_w, w_conv_out, w_attn_out, w_o, g_cross, g_mem, w_cq, w_ckv, w_co, g_ffn2, w_ffn2_gu, w_ffn2_down, g_final, loss_target, m_g_ffn1, m_w_ffn1_gu, m_w_ffn1_down, m_g_mix, m_w_in, m_b_gate, m_conv_w, m_w_conv_out, m_w_attn_out, m_w_o, m_g_cross, m_g_mem, m_w_cq, m_w_ckv, m_w_co, m_g_ffn2, m_w_ffn2_gu, m_w_ffn2_down, m_g_final, v_g_ffn1, v_w_ffn1_gu, v_w_ffn1_down, v_g_mix, v_w_in, v_b_gate, v_conv_w, v_w_conv_out, v_w_attn_out, v_w_o, v_g_cross, v_g_mem, v_w_cq, v_w_ckv, v_w_co, v_g_ffn2, v_w_ffn2_gu, v_w_ffn2_down, v_g_final):
    args = locals()
    wts = {n: args[n] for n in WEIGHT_ORDER}
    mom1 = {n: args["m_" + n] for n in WEIGHT_ORDER}
    mom2 = {n: args["v_" + n] for n in WEIGHT_ORDER}
    cx, cy, cc = _mesh_pos()
    dev = 4 * cx + 2 * cy + cc
    conv_cols = D // N_DEV

    conv_pad = jnp.concatenate([conv_w, jnp.zeros((SMALL_R - CONV_K, conv_cols), F32)], axis=0)
    gains = {n: wts[n].reshape(1, D) for n in GAINS}
    loss8, dx, flights, gs = _step(x[0], mem[0], loss_target[0], _exchange_shards(wts), conv_pad, gains,
                                 b_gate.reshape(1, 2 * D), cc.reshape(1).astype(jnp.int32))

    grads, delta, new_m, new_v = {}, {}, {}, {}

    def operands(n, transposed):
        trio = (wts[n], mom1[n], mom2[n])
        return tuple(a.T for a in trio) if transposed else trio

    def record(n, res, transposed):
        grads[n], delta[n], new_m[n], new_v[n] = [r.T for r in res] if transposed else res

    early = [("w_ffn2_gu", "w_ffn2_gu", 0, True), ("w_ffn2_down", "w_ffn2_down", 0, False),
             ("w_ckv", "w_ckv", 0, False), ("w_in", "w_in", 0, False)]
    early += [(n, "mix", k, False) for k, n in enumerate(MIX_MATS)]
    early += [(n, "cross", k, False) for k, n in enumerate(CROSS_MATS)]
    chip = (2 * cx + cy).reshape(1).astype(jnp.int32)
    (names_early, flight_early), (names_w_in, flight_w_in), (last_names, flight_last) = flights
    token = flight_last.token
    own, land = {}, {}
    for names, flight, tag in ((names_early, flight_early, "early"), (names_w_in, flight_w_in, "w_in")):
        own_parts, landed = _chip_exchange_end(flight, token, "grads_to_chips_%s_end" % tag)
        own.update(zip(names, own_parts))
        land.update(zip(names, landed))
    for n, buf, row_block, transposed in early:
        w, m1, m2 = operands(n, transposed)
        record(n, _adamw_own(w, land[buf], own[buf], chip, m1, m2, "adamw_" + n, row_block, token), transposed)

    after = jnp.concatenate([new_v[n][:1, :1] for n, _, _, _ in early], axis=0)
    own_parts, landed = _chip_exchange_end(flight_last, after, "grads_to_chips_ffn1_end")
    for n, own_n, land_n, transposed in zip(last_names, own_parts, landed, (True, False)):
        w, m1, m2 = operands(n, transposed)
        record(n, _adamw_own(w, land_n, own_n, chip, m1, m2, "adamw_" + n), transposed)

    gs_rows = {n: gs[n] for n in GAINS + ("b_gate",)}
    small_mine = _pack_small(gs_rows, gs["conv_w"][:CONV_K]) + new_v[last_names[-1]][0, 0] * 0.0
    small_mine = small_mine.at[LOSS_ROW, 0].set(loss8[0, 0])
    small_all = _run_exchange(_gather_exchange([small_mine]), "gather_small_grads")[0]
    small_sum = _sum_slots(small_all, "small_grads_sum")
    loss = small_sum[LOSS_ROW, 0]
    grad_small = _unpack_small(small_sum)
    grad_small["conv_w"] = lax.dynamic_slice_in_dim(grad_small["conv_w"], dev * conv_cols, conv_cols, axis=1)
    grads.update(grad_small)

    def small_buf(vals):
        return _pack_small(vals, jnp.concatenate([vals["conv_w"], jnp.zeros((CONV_K, D - conv_cols), F32)], axis=1))

    _, d_s, m_s, v_s = _adamw(small_buf(wts), small_buf(grads)[None], small_buf(mom1), small_buf(mom2), "adamw_small")
    for res, buf in ((delta, d_s), (new_m, m_s), (new_v, v_s)):
        un = _unpack_small(buf)
        for n in GAINS + ("b_gate",):
            res[n] = un[n]
        res["conv_w"] = un["conv_w"][:, :conv_cols]

    return (loss, dx[None], *[grads[n] for n in WEIGHT_ORDER], *[delta[n] for n in WEIGHT_ORDER],
            *[new_m[n] for n in WEIGHT_ORDER], *[new_v[n] for n in WEIGHT_ORDER])
```

```python
import types

import jax
import jax.numpy as jnp
from jax import lax
from jax.experimental import pallas as pl
from jax.experimental.pallas import tpu as pltpu

F32 = jnp.float32
BF16 = jnp.bfloat16

D = 1024
DFF = 2816
SB_H = 8
SB_DH = 128
X_H = 4
X_DH = 256
CONV_K = 3
RMS_EPS = 1e-6
N_DEV = 8
N_CHIP = 4
SQ_ROWS = D // N_DEV

ADAM_LR = 0.001
ADAM_B1 = 0.9
ADAM_B2 = 0.999
ADAM_EPS = 1e-08
ADAM_WD = 0.01
ADAM_STEP = 10

TM = 256
TQ = 512
TK = 256
SB_HPS = 2
VMEM_LIMIT = 56 << 20

FF_BLK = DFF // 4
FF_PAD = 768
FF_SUB = 256
DOWN_ROWS = DFF // N_DEV

MIX_MATS = ("w_conv_out", "w_attn_out", "w_o")
CROSS_MATS = ("w_cq", "w_co")

_ANY = pl.BlockSpec(memory_space=pl.ANY)


def _cparams(n_axes=1):
    return pltpu.CompilerParams(
        dimension_semantics=("arbitrary",) * n_axes, vmem_limit_bytes=VMEM_LIMIT)


def _row_spec(tm, n):
    return pl.BlockSpec((tm, n), lambda i: (i, 0))


def _blk_row_spec(nb, tm, n):
    return pl.BlockSpec((nb, tm, n), lambda i: (0, i, 0))


def _const_spec(shape):
    zeros = (0,) * len(shape)
    return pl.BlockSpec(shape, lambda i: zeros)


def _dot(a, b):
    return jnp.dot(a, b, preferred_element_type=F32)


def _dot_nt(a, b):
    return lax.dot_general(a, b, (((1,), (1,)), ((), ())), preferred_element_type=F32)


def _dot_tn(a, b):
    return lax.dot_general(a, b, (((0,), (0,)), ((), ())), preferred_element_type=F32)


def _sigmoid(x):
    return 1.0 / (1.0 + jnp.exp(-x))


def _call(body, operands, *, grid, in_specs, out_specs, out_shape, scratch_shapes, name, comm=None):
    n_in, n_out, n_sc = len(in_specs), len(out_specs), len(scratch_shapes)
    if comm is None:
        outs = pl.pallas_call(
            body, grid=grid, name=name, in_specs=in_specs, out_specs=out_specs, out_shape=out_shape,
            scratch_shapes=scratch_shapes, compiler_params=_cparams(len(grid)))(*operands)
        return list(outs), []
    c_in, c_out, c_sem = len(comm.inputs), len(comm.out_shapes), len(comm.sem_shapes)

    def hosted(*refs):
        bounds = [0, n_in, c_in, n_out, c_out, n_sc, c_sem]
        parts, pos = [], 0
        for k in bounds[1:]:
            parts.append(refs[pos:pos + k])
            pos += k
        ins, cins, outs, couts, scr, sems = parts
        step, n_steps = pl.program_id(0), grid[0]
        for ax in range(1, len(grid)):
            step, n_steps = step * grid[ax] + pl.program_id(ax), n_steps * grid[ax]

        @pl.when(step == 0)
        def _():
            comm.start(cins, couts, sems)

        @pl.when(step == (2 * n_steps) // 3)
        def _():
            comm.middle(cins, couts, sems)

        body(*ins, *outs, *scr)

        @pl.when(step == n_steps - 1)
        def _():
            comm.finish(cins, couts, sems)

    res = pl.pallas_call(
        hosted, grid=grid, name=name, in_specs=list(in_specs) + [_ANY] * c_in,
        out_specs=list(out_specs) + [_ANY] * c_out, out_shape=list(out_shape) + list(comm.out_shapes),
        scratch_shapes=list(scratch_shapes) + list(comm.sem_shapes),
        compiler_params=_cparams(len(grid)))(*operands, *comm.inputs)
    return list(res[:n_out]), list(res[n_out:])


def _load_resident(step, pairs, sems):
    @pl.when(step == 0)
    def _():
        copies = [pltpu.make_async_copy(src, dst, sems.at[k]) for k, (src, dst) in enumerate(pairs)]
        for cp in copies:
            cp.start()
        for cp in copies:
            cp.wait()


def _square_pairs(buf_hbm, index, dst):
    off = index * SQ_ROWS
    return [(buf_hbm.at[d, off:off + SQ_ROWS, :], dst.at[d * SQ_ROWS:(d + 1) * SQ_ROWS, :]) for d in range(N_DEV)]


def _down_pairs(wd_hbm, dst):
    return [(wd_hbm.at[d], dst.at[d // 2, (d % 2) * DOWN_ROWS:(d % 2 + 1) * DOWN_ROWS, :]) for d in range(N_DEV)]


def _zero_down_pad(step, dst):
    @pl.when(step == 0)
    def _():
        dst[:, FF_BLK:, :] = jnp.zeros((4, FF_PAD - FF_BLK, D), BF16)


def _rms_fwd_tile(xt, g):
    r = lax.rsqrt(jnp.mean(xt * xt, axis=-1, keepdims=True) + RMS_EPS)
    return (xt * r) * g


def _rms_bwd_tile(xt, g, dn):
    r = lax.rsqrt(jnp.mean(xt * xt, axis=-1, keepdims=True) + RMS_EPS)
    xhat = xt * r
    dxhat = dn * g
    dx = r * (dxhat - xhat * jnp.mean(dxhat * xhat, axis=-1, keepdims=True))
    dg = jnp.sum(dn * xhat, axis=0, keepdims=True)
    return dx, dg


def _accumulate(ref, step, value):
    @pl.when(step == 0)
    def _():
        ref[...] = value

    @pl.when(step != 0)
    def _():
        ref[...] = ref[...] + value


def _ffn_fwd(x, g, wgu, wd, name, comm=None, head=None):
    t = x.shape[0]

    def body(x_ref, g_ref, wgu_hbm, wd_hbm, *refs):
        if head is None:
            n_ref, gate_ref, up_ref, act_ref, h_ref, wgu_v, wd_v, sems = refs
        else:
            gf_ref, t_ref, n_ref, gate_ref, up_ref, act_ref, dh_ref, loss_ref, dgf_ref, wgu_v, wd_v, sems = refs
        step = pl.program_id(0)
        _zero_down_pad(step, wd_v)
        _load_resident(step, [(wgu_hbm, wgu_v)] + _down_pairs(wd_hbm, wd_v), sems)
        xt = x_ref[...]
        n = _rms_fwd_tile(xt, g_ref[...]).astype(BF16)
        n_ref[...] = n
        acc = jnp.zeros((TM, D), F32)
        for j in range(4):
            for s in range(FF_PAD // FF_SUB):
                lo, hi = s * FF_SUB, (s + 1) * FF_SUB
                gt = _dot_nt(n, wgu_v[j, lo:hi, :])
                ut = _dot_nt(n, wgu_v[4 + j, lo:hi, :])
                gate_ref[j, :, lo:hi] = gt.astype(BF16)
                up_ref[j, :, lo:hi] = ut.astype(BF16)
                act_ref[j, :, lo:hi] = ((gt * _sigmoid(gt)) * ut).astype(BF16)
            acc = acc + _dot(act_ref[j], wd_v[j])
        ht = xt + 0.5 * acc
        if head is None:
            h_ref[...] = ht
        else:
            gain = gf_ref[...]
            diff = _rms_fwd_tile(ht, gain) - t_ref[...]
            part = 0.5 * jnp.sum(jnp.sum(diff * diff, axis=-1, keepdims=True) / D, axis=0, keepdims=True)
            dx, dg = _rms_bwd_tile(ht, gain, diff / D)
            dh_ref[...] = dx
            _accumulate(loss_ref, step, jnp.broadcast_to(part, (8, 128)))
            _accumulate(dgf_ref, step, dg)

    ff = jax.ShapeDtypeStruct((4, t, FF_PAD), BF16)
    operands, in_specs = (x, g, wgu, wd), [_row_spec(TM, D), _const_spec((1, D)), _ANY, _ANY]
    out_specs = [_row_spec(TM, D)] + [_blk_row_spec(4, TM, FF_PAD)] * 3 + [_row_spec(TM, D)]
    out_shape = [jax.ShapeDtypeStruct((t, D), BF16), ff, ff, ff, jax.ShapeDtypeStruct((t, D), F32)]
    if head is not None:
        operands += tuple(head)
        in_specs += [_const_spec((1, D)), _row_spec(TM, D)]
        out_specs += [_const_spec((8, 128)), _const_spec((1, D))]
        out_shape += [jax.ShapeDtypeStruct((8, 128), F32), jax.ShapeDtypeStruct((1, D), F32)]
    return _call(
        body, operands, grid=(t // TM,), name=name, comm=comm, in_specs=in_specs, out_specs=out_specs,
        out_shape=out_shape,
        scratch_shapes=[pltpu.VMEM((N_DEV, FF_PAD, D), BF16), pltpu.VMEM((4, FF_PAD, D), BF16),
                        pltpu.SemaphoreType.DMA((1 + N_DEV,))])


def _ffn_bwd(dh, xin, g, gate, up, wgu, wd, name, comm=None):
    t = dh.shape[0]

    def body(dh_ref, x_ref, g_ref, gate_ref, up_ref, wgu_hbm, wd_hbm,
             dgu_ref, dhb_ref, dx_ref, dg_ref, wgu_v, wd_v, sems):
        step = pl.program_id(0)
        _zero_down_pad(step, wd_v)
        _load_resident(step, [(wgu_hbm, wgu_v)] + _down_pairs(wd_hbm, wd_v), sems)
        dht = dh_ref[...]
        dhb = (0.5 * dht).astype(BF16)
        dhb_ref[...] = dhb
        dn = jnp.zeros((TM, D), F32)
        for j in range(4):
            for s in range(FF_PAD // FF_SUB):
                lo, hi = s * FF_SUB, (s + 1) * FF_SUB
                da = _dot_nt(dhb, wd_v[j, lo:hi, :])
                gt = gate_ref[j, :, lo:hi].astype(F32)
                ut = up_ref[j, :, lo:hi].astype(F32)
                sg = _sigmoid(gt)
                dgt = (da * ut * (sg * (1.0 + gt * (1.0 - sg)))).astype(BF16)
                dut = (da * (gt * sg)).astype(BF16)
                dgu_ref[j, :, lo:hi] = dgt
                dgu_ref[4 + j, :, lo:hi] = dut
            dn = dn + _dot(dgu_ref[j], wgu_v[j]) + _dot(dgu_ref[4 + j], wgu_v[4 + j])
        dx, dg = _rms_bwd_tile(x_ref[...], g_ref[...], dn)
        dx_ref[...] = dht + dx
        _accumulate(dg_ref, step, dg)

    return _call(
        body, (dh, xin, g, gate, up, wgu, wd), grid=(t // TM,), name=name, comm=comm,
        in_specs=[_row_spec(TM, D), _row_spec(TM, D), _const_spec((1, D)), _blk_row_spec(4, TM, FF_PAD),
                  _blk_row_spec(4, TM, FF_PAD), _ANY, _ANY],
        out_specs=[_blk_row_spec(N_DEV, TM, FF_PAD), _row_spec(TM, D), _row_spec(TM, D), _const_spec((1, D))],
        out_shape=[jax.ShapeDtypeStruct((N_DEV, t, FF_PAD), BF16), jax.ShapeDtypeStruct((t, D), BF16),
                   jax.ShapeDtypeStruct((t, D), F32), jax.ShapeDtypeStruct((1, D), F32)],
        scratch_shapes=[pltpu.VMEM((N_DEV, FF_PAD, D), BF16), pltpu.VMEM((4, FF_PAD, D), BF16),
                        pltpu.SemaphoreType.DMA((1 + N_DEV,))])


WIDE_TILES = (1024, 512, 256, 128)


def _pick_tile(n, options=(512, 256, 128)):
    for o in options:
        if n % o == 0:
            return o
    return n


def _into(stack, n_operands):
    if stack is None:
        return (), [], {}
    return (stack,), [_ANY], {n_operands: 0}


def _mm_tn_square(a, b, name, index, count, stack=None):
    k, m = a.shape
    _, n = b.shape
    tn = _pick_tile(n)
    extra, extra_specs, aliases = _into(stack, 2)

    def body(a_ref, b_ref, *rest):
        rest[-1][...] = _dot_tn(a_ref[...], b_ref[...]).astype(BF16).reshape(N_DEV, m // N_DEV, tn)

    return pl.pallas_call(
        body, grid=(n // tn,), name=name,
        in_specs=[pl.BlockSpec((k, m), lambda j: (0, 0)), pl.BlockSpec((k, tn), lambda j: (0, j))] + extra_specs,
        out_specs=pl.BlockSpec((N_DEV, m // N_DEV, tn), lambda j: (0, index, j)),
        out_shape=jax.ShapeDtypeStruct((N_DEV, count * (m // N_DEV), n), BF16),
        input_output_aliases=aliases, compiler_params=_cparams(1),
    )(a, b, *extra)


def _mm_tn_cols(a, b, name, first=0, count=None, stack=None):
    k, m = a.shape
    nb, _, n = b.shape
    tm = _pick_tile(m, WIDE_TILES)
    extra, extra_specs, aliases = _into(stack, 2)

    def body(a_ref, b_ref, *rest):
        rest[-1][0] = _dot_tn(a_ref[...].astype(BF16), b_ref[0].astype(BF16)).astype(BF16)

    return pl.pallas_call(
        body, grid=(nb, m // tm), name=name,
        in_specs=[pl.BlockSpec((k, tm), lambda j, i: (0, i)), pl.BlockSpec((1, k, n), lambda j, i: (j, 0, 0))]
                 + extra_specs,
        out_specs=pl.BlockSpec((1, tm, n), lambda j, i: (j + first, i, 0)),
        out_shape=jax.ShapeDtypeStruct((nb if count is None else count, m, n), BF16),
        input_output_aliases=aliases, compiler_params=_cparams(2),
    )(a, b, *extra)


def _mm_tn_rows(a, b, keep, name, comm=None):
    nb, k, m = a.shape
    _, n = b.shape
    tn = _pick_tile(n, WIDE_TILES)

    def body(a_ref, b_ref, o_ref):
        o_ref[0] = _dot_tn(a_ref[0], b_ref[...])[:keep].astype(BF16)

    (out,), couts = _call(
        body, (a, b), grid=(nb, n // tn), name=name, comm=comm,
        in_specs=[pl.BlockSpec((1, k, m), lambda j, i: (j, 0, 0)), pl.BlockSpec((k, tn), lambda j, i: (0, i))],
        out_specs=[pl.BlockSpec((1, keep, tn), lambda j, i: (j, 0, i))],
        out_shape=[jax.ShapeDtypeStruct((nb, keep, n), BF16)], scratch_shapes=[])
    return out if comm is None else (out, couts)


PCG_W = 5 * D
QKV_W = 3 * D
PROJ_SUB = 512


def _inproj_fwd(h, g, w_in, conv_w, name, comm=None):
    t = h.shape[0]

    def body(h_ref, g_ref, w_hbm, cw_ref, u_ref, pcg_ref, qkv_ref, yc_ref, w_v, tail_v, sems):
        step = pl.program_id(0)
        _load_resident(step, [(w_hbm, w_v)], sems)

        @pl.when(step == 0)
        def _():
            tail_v[...] = jnp.zeros_like(tail_v)

        u = _rms_fwd_tile(h_ref[...], g_ref[...]).astype(BF16)
        u_ref[...] = u
        for blk in range(N_DEV):
            for s in range(D // PROJ_SUB):
                lo, hi = s * PROJ_SUB, (s + 1) * PROJ_SUB
                p = _dot(u, w_v[blk, :, lo:hi])
                if blk < 3:
                    pcg_ref[:, blk * D + lo:blk * D + hi] = p
                elif blk < 6:
                    qkv_ref[:, (blk - 3) * D + lo:(blk - 3) * D + hi] = p.astype(BF16)
                else:
                    pcg_ref[:, (blk - 3) * D + lo:(blk - 3) * D + hi] = p
        xc = pcg_ref[:, D:2 * D] * pcg_ref[:, 2 * D:3 * D]
        ext = jnp.concatenate([tail_v[...], xc], axis=0)
        conv = (cw_ref[0:1, :] * pltpu.roll(ext, 2, 0)[8:] + cw_ref[1:2, :] * pltpu.roll(ext, 1, 0)[8:]
                + cw_ref[2:3, :] * xc)
        yc_ref[...] = (pcg_ref[:, 0:D] * conv).astype(BF16)
        tail_v[...] = xc[TM - 8:]

    return _call(
        body, (h, g, w_in, conv_w), grid=(t // TM,), name=name, comm=comm,
        in_specs=[_row_spec(TM, D), _const_spec((1, D)), _ANY, _const_spec((CONV_K, D))],
        out_specs=[_row_spec(TM, D), _row_spec(TM, PCG_W), _row_spec(TM, QKV_W), _row_spec(TM, D)],
        out_shape=[jax.ShapeDtypeStruct((t, D), BF16), jax.ShapeDtypeStruct((t, PCG_W), F32),
                   jax.ShapeDtypeStruct((t, QKV_W), BF16), jax.ShapeDtypeStruct((t, D), BF16)],
        scratch_shapes=[pltpu.VMEM((N_DEV, D, D), BF16), pltpu.VMEM((8, D), F32), pltpu.SemaphoreType.DMA((1,))])


def _tri2(cond):
    rr = lax.broadcasted_iota(jnp.int32, (2 * TK, TK), 0) & (TK - 1)
    cc = lax.broadcasted_iota(jnp.int32, (2 * TK, TK), 1)
    return cond(rr, cc).astype(BF16)


def _causal(shift, row0=0):
    rr = lax.broadcasted_iota(jnp.int32, (TQ - row0, TK), 0) + row0
    cc = lax.broadcasted_iota(jnp.int32, (TQ - row0, TK), 1)
    return cc + shift < rr


def _cumdot(v, tri2):
    hi = v.astype(BF16)
    lo = (v - hi.astype(F32)).astype(BF16)
    return _dot(jnp.concatenate([hi, lo], axis=1), tri2)


def _log_1m_beta(z):
    return -(jnp.maximum(z, 0.0) + jnp.log(1.0 + jnp.exp(-jnp.abs(z))))


def _sb_specs(t):
    g = SB_H // SB_HPS
    w = SB_HPS * SB_DH
    q_spec = pl.BlockSpec((TQ, w), lambda h, i: (i, h))
    k_spec = pl.BlockSpec((t, w), lambda h, i: (0, g + h))
    v_spec = pl.BlockSpec((t, w), lambda h, i: (0, 2 * g + h))
    ct_spec = pl.BlockSpec((SB_HPS, TQ, 1), lambda h, i: (h, i, 0))
    return g, w, q_spec, k_spec, v_spec, ct_spec


def _sb_fwd(qkv, name, comm=None):
    t = qkv.shape[0]
    scale = SB_DH ** -0.5
    g, w, q_spec, k_spec, v_spec, ct_spec = _sb_specs(t)

    def body(q_ref, k_ref, v_ref, y_ref, ct_ref):
        i = pl.program_id(1)
        later = _tri2(lambda j, s: j > s)
        n_diag = TQ // TK

        def block(j, carry, shift):
            off = pl.multiple_of(j * TK, TK)
            zs, ms = [], []
            for hd in range(SB_HPS):
                cols = slice(hd * SB_DH, (hd + 1) * SB_DH)
                z = _dot_nt(q_ref[:, cols], k_ref[pl.ds(off, TK), cols]) * scale
                m = _log_1m_beta(z)
                if shift is not None:
                    m = jnp.where(_causal(shift), m, 0.0)
                zs.append(z)
                ms.append(m)
            after = _cumdot(jnp.concatenate(ms, axis=0), later)
            out = []
            for hd in range(SB_HPS):
                acc, c_sum = carry[hd]
                cols = slice(hd * SB_DH, (hd + 1) * SB_DH)
                a = jnp.exp((ms[hd] + zs[hd]) + (c_sum + after[hd * TQ:(hd + 1) * TQ]))
                if shift is not None:
                    a = jnp.where(_causal(shift), a, 0.0)
                out.append((acc + _dot(a.astype(BF16), v_ref[pl.ds(off, TK), cols]),
                            c_sum + jnp.sum(ms[hd], axis=1, keepdims=True)))
            return tuple(out)

        carry = tuple((jnp.zeros((TQ, SB_DH), F32), jnp.zeros((TQ, 1), F32)) for _ in range(SB_HPS))
        for d in reversed(range(n_diag)):
            carry = block(i * n_diag + d, carry, d * TK)
        carry = lax.fori_loop(0, i * n_diag, lambda jj, c: block(i * n_diag - 1 - jj, c, None), carry)
        for hd in range(SB_HPS):
            y_ref[:, hd * SB_DH:(hd + 1) * SB_DH] = carry[hd][0].astype(BF16)
            ct_ref[hd] = carry[hd][1]

    return _call(
        body, (qkv, qkv, qkv), grid=(g, t // TQ), name=name, comm=comm,
        in_specs=[q_spec, k_spec, v_spec],
        out_specs=[q_spec, ct_spec],
        out_shape=[jax.ShapeDtypeStruct((t, D), BF16), jax.ShapeDtypeStruct((SB_H, t, 1), F32)],
        scratch_shapes=[])


def _sb_bwd(qkv, dy, ctot, after, name, comm=None):
    t = qkv.shape[0]
    scale = SB_DH ** -0.5
    g, w, q_spec, k_spec, v_spec, ct_spec = _sb_specs(t)
    acc_spec = pl.BlockSpec((2, t, w), lambda h, i: (0, 0, h))

    def body(q_ref, k_ref, v_ref, dy_ref, ct_ref, after_ref, dq_ref, dkv_ref):
        i = pl.program_id(1)

        @pl.when(i == 0)
        def _():
            dkv_ref[...] = jnp.zeros_like(dkv_ref)

        upto = _tri2(lambda j, s: j <= s)
        n_diag = TQ // TK

        def block(j, carry, shift):
            off = pl.multiple_of(j * TK, TK)
            r0 = 0 if shift is None else shift
            nr = TQ - r0
            causal = None if shift is None else _causal(shift, r0)

            def grow(old, delta):
                return old + delta if r0 == 0 else jnp.concatenate([old[:r0], old[r0:] + delta], axis=0)

            zs, ms = [], []
            for hd in range(SB_HPS):
                cols = slice(hd * SB_DH, (hd + 1) * SB_DH)
                z = _dot_nt(q_ref[r0:, cols], k_ref[pl.ds(off, TK), cols]) * scale
                m = _log_1m_beta(z)
                if causal is not None:
                    m = jnp.where(causal, m, 0.0)
                zs.append(z)
                ms.append(m)
            m_upto = _cumdot(jnp.concatenate(ms, axis=0), upto)
            ls, a_s, es = [], [], []
            for hd in range(SB_HPS):
                cols = slice(hd * SB_DH, (hd + 1) * SB_DH)
                l = ms[hd] + zs[hd]
                a = jnp.exp(l + ((ct_ref[hd, r0:] - carry[hd][1][r0:]) - m_upto[hd * nr:(hd + 1) * nr]))
                if causal is not None:
                    a = jnp.where(causal, a, 0.0)
                ls.append(l)
                a_s.append(a)
                es.append(_dot_nt(dy_ref[r0:, cols], v_ref[pl.ds(off, TK), cols]) * a)
            e_upto = _dot(jnp.concatenate(es, axis=0).astype(BF16), upto[:TK])
            out = []
            for hd in range(SB_HPS):
                dq, p_sum, e_sum = carry[hd]
                cols = slice(hd * SB_DH, (hd + 1) * SB_DH)
                e = es[hd]
                dz = e - jnp.exp(ls[hd]) * (e_sum[r0:] + e_upto[hd * nr:(hd + 1) * nr])
                if causal is not None:
                    dz = jnp.where(causal, dz, 0.0)
                dzs = (dz * scale).astype(BF16)
                dkv_ref[0, pl.ds(off, TK), cols] += _dot_tn(dzs, q_ref[r0:, cols])
                dkv_ref[1, pl.ds(off, TK), cols] += _dot_tn(a_s[hd].astype(BF16), dy_ref[r0:, cols])
                out.append((grow(dq, _dot(dzs, k_ref[pl.ds(off, TK), cols])),
                            grow(p_sum, jnp.sum(ms[hd], axis=1, keepdims=True)),
                            grow(e_sum, jnp.sum(e, axis=1, keepdims=True))))
            return tuple(out)

        zero = jnp.zeros((TQ, 1), F32)
        init = tuple((jnp.zeros((TQ, SB_DH), F32), zero, zero) for _ in range(SB_HPS))
        carry = lax.fori_loop(0, i * n_diag, lambda j, c: block(j, c, None), init)
        for d in range(n_diag):
            carry = block(i * n_diag + d, carry, d * TK)
        for hd in range(SB_HPS):
            dq_ref[:, hd * SB_DH:(hd + 1) * SB_DH] = carry[hd][0].astype(BF16)

    return _call(
        body, (qkv, qkv, qkv, dy, ctot, after), grid=(g, t // TQ), name=name, comm=comm,
        in_specs=[q_spec, k_spec, v_spec, q_spec, ct_spec, pl.BlockSpec(after.shape, lambda h, i: (0, 0))],
        out_specs=[q_spec, acc_spec],
        out_shape=[jax.ShapeDtypeStruct((t, D), BF16), jax.ShapeDtypeStruct((2, t, D), F32)],
        scratch_shapes=[])


def _gate_specs():
    return [pl.BlockSpec((TM, D), lambda i: (i, 3)), pl.BlockSpec((TM, D), lambda i: (i, 4))]


def _mix_pairs(mix_hbm, dsts):
    pairs = []
    for index, dst in enumerate(dsts):
        pairs += _square_pairs(mix_hbm, index, dst)
    return pairs


def _mix_out_fwd(yc, ysb, pcg, b_gate, h, w_mix, name, comm=None):
    t = h.shape[0]

    def body(yc_ref, ysb_ref, gc_ref, gs_ref, b_ref, h_ref, mix_hbm,
             a_ref, b_out_ref, mg_ref, h2_ref, wc_v, wa_v, wo_v, sems):
        _load_resident(pl.program_id(0), _mix_pairs(mix_hbm, (wc_v, wa_v, wo_v)), sems)
        a = _dot(yc_ref[...], wc_v[...])
        b = _dot(ysb_ref[...], wa_v[...])
        merged = (_sigmoid(gc_ref[...] + b_ref[:, :D]) * a + _sigmoid(gs_ref[...] + b_ref[:, D:]) * b).astype(BF16)
        a_ref[...] = a
        b_out_ref[...] = b
        mg_ref[...] = merged
        h2_ref[...] = h_ref[...] + _dot(merged, wo_v[...])

    return _call(
        body, (yc, ysb, pcg, pcg, b_gate, h, w_mix), grid=(t // TM,), name=name, comm=comm,
        in_specs=[_row_spec(TM, D), _row_spec(TM, D)] + _gate_specs()
                 + [_const_spec((1, 2 * D)), _row_spec(TM, D), _ANY],
        out_specs=[_row_spec(TM, D)] * 4,
        out_shape=[jax.ShapeDtypeStruct((t, D), F32), jax.ShapeDtypeStruct((t, D), F32),
                   jax.ShapeDtypeStruct((t, D), BF16), jax.ShapeDtypeStruct((t, D), F32)],
        scratch_shapes=[pltpu.VMEM((D, D), BF16)] * 3 + [pltpu.SemaphoreType.DMA((3 * N_DEV,))])


def _mix_out_bwd(dh2, a, b, pcg, b_gate, conv_w, w_mix, name, comm=None):
    t = dh2.shape[0]
    n_tile = t // TM
    per8 = TM // 8

    def rows(n):
        return pl.BlockSpec((TM, n), lambda i: (n_tile - 1 - i, 0))

    def cols(block):
        return pl.BlockSpec((TM, D), lambda i: (n_tile - 1 - i, block))

    def before(block):
        return pl.BlockSpec((8, D), lambda i: (jnp.maximum((n_tile - 1 - i) * per8 - 1, 0), block))

    def body(dh_ref, a_ref, b_ref, gc_ref, gs_ref, cb_ref, cc_ref, cx_ref, ccp_ref, cxp_ref, bias_ref, cw_ref, mix_hbm,
             dhb_ref, da_ref, db_ref, dgp_ref, dc_ref, dysb_ref, dbias_ref, dcw_ref, wc_v, wa_v, wo_v, head_v, sems):
        step = pl.program_id(0)
        _load_resident(step, _mix_pairs(mix_hbm, (wc_v, wa_v, wo_v)), sems)

        @pl.when(step == 0)
        def _():
            head_v[...] = jnp.zeros_like(head_v)
            dcw_ref[...] = jnp.zeros_like(dcw_ref)

        dhb = dh_ref[...].astype(BF16)
        dhb_ref[...] = dhb
        dm = _dot_nt(dhb, wo_v[...])
        gc = _sigmoid(gc_ref[...] + bias_ref[:, :D])
        gs = _sigmoid(gs_ref[...] + bias_ref[:, D:])
        da = (dm * gc).astype(BF16)
        db = (dm * gs).astype(BF16)
        da_ref[...] = da
        db_ref[...] = db
        dgc = dm * a_ref[...] * (gc * (1.0 - gc))
        dgs = dm * b_ref[...] * (gs * (1.0 - gs))
        dgp_ref[0] = dgc.astype(BF16)
        dgp_ref[1] = dgs.astype(BF16)
        _accumulate(dbias_ref.at[:, :D], step, jnp.sum(dgc, axis=0, keepdims=True))
        _accumulate(dbias_ref.at[:, D:], step, jnp.sum(dgs, axis=0, keepdims=True))
        dysb_ref[...] = _dot_nt(db, wa_v[...]).astype(BF16)
        dyc = _dot_nt(da, wc_v[...])
        cc, cx = cc_ref[...], cx_ref[...]
        xc = cc * cx
        xc_before = jnp.where(step == n_tile - 1, 0.0, ccp_ref[...] * cxp_ref[...])
        ext = jnp.concatenate([xc_before, xc], axis=0)
        x1 = pltpu.roll(ext, 1, 0)[8:]
        x2 = pltpu.roll(ext, 2, 0)[8:]
        w0, w1, w2 = cw_ref[0:1, :], cw_ref[1:2, :], cw_ref[2:3, :]
        dc_ref[0] = (dyc * (w0 * x2 + w1 * x1 + w2 * xc)).astype(BF16)
        dconv = dyc * cb_ref[...]
        dcw_ref[0:1, :] += jnp.sum(dconv * x2, axis=0, keepdims=True)
        dcw_ref[1:2, :] += jnp.sum(dconv * x1, axis=0, keepdims=True)
        dcw_ref[2:3, :] += jnp.sum(dconv * xc, axis=0, keepdims=True)
        after = jnp.concatenate([dconv, head_v[...]], axis=0)
        dxc = w2 * dconv + w1 * pltpu.roll(after, TM + 7, 0)[:TM] + w0 * pltpu.roll(after, TM + 6, 0)[:TM]
        dc_ref[1] = (dxc * cx).astype(BF16)
        dc_ref[2] = (dxc * cc).astype(BF16)
        head_v[...] = dconv[:8]

    return _call(
        body, (dh2, a, b, pcg, pcg, pcg, pcg, pcg, pcg, pcg, b_gate, conv_w, w_mix), grid=(n_tile,), name=name,
        comm=comm,
        in_specs=[rows(D)] * 3 + [cols(3), cols(4), cols(0), cols(1), cols(2), before(1), before(2),
                                  _const_spec((1, 2 * D)), _const_spec((CONV_K, D)), _ANY],
        out_specs=[rows(D)] * 3 + [pl.BlockSpec((2, TM, D), lambda i: (0, n_tile - 1 - i, 0)),
                                   pl.BlockSpec((3, TM, D), lambda i: (0, n_tile - 1 - i, 0)), rows(D),
                                   _const_spec((1, 2 * D)), _const_spec((8, D))],
        out_shape=[jax.ShapeDtypeStruct((t, D), BF16)] * 3
                  + [jax.ShapeDtypeStruct((2, t, D), BF16), jax.ShapeDtypeStruct((3, t, D), BF16),
                     jax.ShapeDtypeStruct((t, D), BF16), jax.ShapeDtypeStruct((1, 2 * D), F32),
                     jax.ShapeDtypeStruct((8, D), F32)],
        scratch_shapes=[pltpu.VMEM((D, D), BF16)] * 3 + [pltpu.VMEM((8, D), F32),
                                                         pltpu.SemaphoreType.DMA((3 * N_DEV,))])


def _inproj_bwd(dconv, dq, dkv, dgp, w_in, h, g, dh_res, name, comm=None):
    t = h.shape[0]

    def body(dc_ref, dq_ref, dkv_ref, dgp_ref, w_hbm, h_ref, g_ref, dres_ref, dh_ref, dg_ref, w_v, sems):
        step = pl.program_id(0)
        _load_resident(step, [(w_hbm, w_v)], sems)
        du = _dot_nt(dq_ref[...], w_v[3])
        for k in range(3):
            du = du + _dot_nt(dc_ref[k], w_v[k])
        for k in range(2):
            du = du + _dot_nt(dkv_ref[k].astype(BF16), w_v[4 + k]) + _dot_nt(dgp_ref[k], w_v[6 + k])
        dx, dg = _rms_bwd_tile(h_ref[...], g_ref[...], du)
        dh_ref[...] = dres_ref[...] + dx
        _accumulate(dg_ref, step, dg)

    return _call(
        body, (dconv, dq, dkv, dgp, w_in, h, g, dh_res), grid=(t // TM,), name=name, comm=comm,
        in_specs=[_blk_row_spec(3, TM, D), _row_spec(TM, D), _blk_row_spec(2, TM, D), _blk_row_spec(2, TM, D), _ANY,
                  _row_spec(TM, D), _const_spec((1, D)), _row_spec(TM, D)],
        out_specs=[_row_spec(TM, D), _const_spec((1, D))],
        out_shape=[jax.ShapeDtypeStruct((t, D), F32), jax.ShapeDtypeStruct((1, D), F32)],
        scratch_shapes=[pltpu.VMEM((N_DEV, D, D), BF16), pltpu.SemaphoreType.DMA((1,))])


def _softmax_rows(s):
    e = jnp.exp(s - jnp.max(s, axis=-1, keepdims=True))
    return e / jnp.sum(e, axis=-1, keepdims=True)


def _cross_pairs(cross_hbm, wq_v, wo_v):
    return _square_pairs(cross_hbm, 0, wq_v) + _square_pairs(cross_hbm, 1, wo_v)


def _cross_fwd(h, g, mem, g_mem, w_ckv, w_cross, name):
    t = h.shape[0]
    m = mem.shape[0]
    scale = X_DH ** -0.5

    def body(h_ref, g_ref, mem_ref, gm_ref, wkv_ref, cross_hbm, hn_ref, qx_ref, o_ref, h3_ref, mn_ref, kv_ref,
             wq_v, wo_v, sems):
        _load_resident(pl.program_id(0), _cross_pairs(cross_hbm, wq_v, wo_v), sems)

        @pl.when(pl.program_id(0) == 0)
        def _():
            mn = _rms_fwd_tile(mem_ref[...], gm_ref[...]).astype(BF16)
            mn_ref[...] = mn
            for j in range(N_DEV):
                kv_ref[j] = _dot(mn, wkv_ref[j]).astype(BF16)

        ht = h_ref[...]
        hn = _rms_fwd_tile(ht, g_ref[...]).astype(BF16)
        hn_ref[...] = hn
        qx = _dot(hn, wq_v[...]).astype(BF16)
        qx_ref[...] = qx
        for hd in range(X_H):
            lo, hi = hd * X_DH, (hd + 1) * X_DH
            p = _softmax_rows(_dot_nt(qx[:, lo:hi], kv_ref[hd]) * scale)
            o_ref[:, lo:hi] = _dot(p.astype(BF16), kv_ref[X_H + hd]).astype(BF16)
        h3_ref[...] = ht + _dot(o_ref[...], wo_v[...])

    return pl.pallas_call(
        body, grid=(t // TM,), name=name,
        in_specs=[_row_spec(TM, D), _const_spec((1, D)), _const_spec((m, D)), _const_spec((1, D)),
                  _const_spec((N_DEV, D, X_DH)), _ANY],
        out_specs=[_row_spec(TM, D)] * 4 + [_const_spec((m, D)), _const_spec((N_DEV, m, X_DH))],
        out_shape=[jax.ShapeDtypeStruct((t, D), BF16)] * 3 + [jax.ShapeDtypeStruct((t, D), F32),
                                                              jax.ShapeDtypeStruct((m, D), BF16),
                                                              jax.ShapeDtypeStruct((N_DEV, m, X_DH), BF16)],
        scratch_shapes=[pltpu.VMEM((D, D), BF16)] * 2 + [pltpu.SemaphoreType.DMA((2 * N_DEV,))],
        compiler_params=_cparams(),
    )(h, g, mem, g_mem, w_ckv, w_cross)


def _cross_bwd(dh3, h, g, qx, kv, mem, g_mem, w_ckv, w_cross, name, comm=None):
    t = h.shape[0]
    m = kv.shape[1]
    scale = X_DH ** -0.5

    def body(dh_ref, h_ref, g_ref, qx_ref, kv_ref, mem_ref, gm_ref, wkv_ref, cross_hbm,
             dhb_ref, dqx_ref, dkv_ref, dh2_ref, dg_ref, dgm_ref, wq_v, wo_v, sems):
        step = pl.program_id(0)
        _load_resident(step, _cross_pairs(cross_hbm, wq_v, wo_v), sems)

        @pl.when(step == 0)
        def _():
            dkv_ref[...] = jnp.zeros_like(dkv_ref)

        dht = dh_ref[...]
        dhb = dht.astype(BF16)
        dhb_ref[...] = dhb
        do = _dot_nt(dhb, wo_v[...]).astype(BF16)
        for hd in range(X_H):
            lo, hi = hd * X_DH, (hd + 1) * X_DH
            qh = qx_ref[:, lo:hi]
            kh = kv_ref[hd]
            p = _softmax_rows(_dot_nt(qh, kh) * scale)
            doh = do[:, lo:hi]
            dp = _dot_nt(doh, kv_ref[X_H + hd])
            ds = (p * (dp - jnp.sum(dp * p, axis=-1, keepdims=True)) * scale).astype(BF16)
            dqx_ref[:, lo:hi] = _dot(ds, kh).astype(BF16)
            dkv_ref[hd] += _dot_tn(ds, qh)
            dkv_ref[X_H + hd] += _dot_tn(p.astype(BF16), doh)
        dhn = _dot_nt(dqx_ref[...], wq_v[...])
        dx, dg = _rms_bwd_tile(h_ref[...], g_ref[...], dhn)
        dh2_ref[...] = dht + dx
        _accumulate(dg_ref, step, dg)

        @pl.when(step == t // TM - 1)
        def _():
            dmn = jnp.zeros((m, D), F32)
            for j in range(N_DEV):
                dmn = dmn + _dot_nt(dkv_ref[j].astype(BF16), wkv_ref[j])
            dgm_ref[...] = _rms_bwd_tile(mem_ref[...], gm_ref[...], dmn)[1]

    return _call(
        body, (dh3, h, g, qx, kv, mem, g_mem, w_ckv, w_cross), grid=(t // TM,), name=name, comm=comm,
        in_specs=[_row_spec(TM, D), _row_spec(TM, D), _const_spec((1, D)), _row_spec(TM, D),
                  _const_spec((N_DEV, m, X_DH)), _const_spec((m, D)), _const_spec((1, D)),
                  _const_spec((N_DEV, D, X_DH)), _ANY],
        out_specs=[_row_spec(TM, D), _row_spec(TM, D), _const_spec((N_DEV, m, X_DH)), _row_spec(TM, D),
                   _const_spec((1, D)), _const_spec((1, D))],
        out_shape=[jax.ShapeDtypeStruct((t, D), BF16), jax.ShapeDtypeStruct((t, D), BF16),
                   jax.ShapeDtypeStruct((N_DEV, m, X_DH), F32), jax.ShapeDtypeStruct((t, D), F32),
                   jax.ShapeDtypeStruct((1, D), F32), jax.ShapeDtypeStruct((1, D), F32)],
        scratch_shapes=[pltpu.VMEM((D, D), BF16)] * 2 + [pltpu.SemaphoreType.DMA((2 * N_DEV,))])


def _adamw(w, parts, m, v, name, row_block=0, token=None):
    r, c = w.shape
    n = parts.shape[0]
    tr = _pick_tile(r, (256, 352, 128))
    off = row_block * (r // tr)

    def body(*refs):
        if token is None:
            _adamw_update(None, *refs)
        else:
            _adamw_update(refs[4], *refs[:4], *refs[5:])

    spec = _row_spec(tr, c)
    in_specs = [spec, pl.BlockSpec((n, tr, c), lambda i: (0, i + off, 0)), spec, spec]
    operands = (w, parts, m, v)
    if token is not None:
        in_specs.append(_const_spec(token.shape))
        operands += (token,)
    return pl.pallas_call(
        body, grid=(r // tr,), name=name, in_specs=in_specs, out_specs=[spec] * 4,
        out_shape=[jax.ShapeDtypeStruct((r, c), F32)] * 4,
        compiler_params=_cparams(),
    )(*operands)


def _adamw_update(tok_ref, w_ref, p_ref, m_ref, v_ref, g_ref, d_ref, nm_ref, nv_ref):
    gt = p_ref[0].astype(F32)
    for k in range(1, p_ref.shape[0]):
        gt = gt + p_ref[k].astype(F32)
    if tok_ref is not None:
        gt = gt + tok_ref[0:1, 0:1]
    _adamw_apply(gt, w_ref, m_ref, v_ref, g_ref, d_ref, nm_ref, nv_ref)


def _adamw_own(w, land, own, chip, m, v, name, row_block=0, token=None):
    r, c = w.shape
    tr = _pick_tile(r, (256, 352, 128))
    off = row_block * (r // tr)

    def body(chip_ref, w_ref, land_ref, own_ref, m_ref, v_ref, *rest):
        mine = own_ref[0].astype(F32)
        gt = jnp.where(chip_ref[0] == 0, mine, land_ref[0].astype(F32))
        for k in range(1, N_CHIP):
            gt = gt + jnp.where(chip_ref[0] == k, mine, land_ref[k].astype(F32))
        if token is not None:
            gt = gt + rest[0][0:1, 0:1]
        _adamw_apply(gt, w_ref, m_ref, v_ref, *rest[-4:])

    spec = pl.BlockSpec((tr, c), lambda i, chip_ref: (i, 0))
    in_specs = [spec, pl.BlockSpec((N_CHIP, tr, c), lambda i, chip_ref: (0, i + off, 0)),
                pl.BlockSpec((1, tr, c), lambda i, chip_ref: (chip_ref[0], i + off, 0)), spec, spec]
    operands = (chip, w, land, own, m, v)
    if token is not None:
        in_specs.append(pl.BlockSpec(token.shape, lambda i, chip_ref: (0, 0)))
        operands += (token,)
    return pl.pallas_call(
        body, name=name,
        grid_spec=pltpu.PrefetchScalarGridSpec(
            num_scalar_prefetch=1, grid=(r // tr,), in_specs=in_specs, out_specs=[spec] * 4),
        out_shape=[jax.ShapeDtypeStruct((r, c), F32)] * 4,
        compiler_params=_cparams(),
    )(*operands)


def _adamw_apply(gt, w_ref, m_ref, v_ref, g_ref, d_ref, nm_ref, nv_ref):
    g_ref[...] = gt
    nm = ADAM_B1 * m_ref[...] + (1.0 - ADAM_B1) * gt
    nv = ADAM_B2 * v_ref[...] + (1.0 - ADAM_B2) * jnp.square(gt)
    m_hat = nm / (1.0 - ADAM_B1 ** ADAM_STEP)
    v_hat = nv / (1.0 - ADAM_B2 ** ADAM_STEP)
    d_ref[...] = -ADAM_LR * (m_hat / (jnp.sqrt(v_hat) + ADAM_EPS) + ADAM_WD * w_ref[...])
    nm_ref[...] = nm
    nv_ref[...] = nv


def _mesh_pos():
    return lax.axis_index("x"), lax.axis_index("y"), lax.axis_index("c")


def _no_round(in_refs, out_refs, sems):
    pass


def _run_exchange(comm, name):
    c_in, c_out = len(comm.inputs), len(comm.out_shapes)

    def body(*refs):
        cins, couts, sems = refs[:c_in], refs[c_in:c_in + c_out], refs[c_in + c_out:]
        comm.start(cins, couts, sems)
        comm.middle(cins, couts, sems)
        comm.finish(cins, couts, sems)

    return list(pl.pallas_call(
        body, name=name, out_shape=list(comm.out_shapes),
        in_specs=[_ANY] * c_in, out_specs=[_ANY] * c_out, scratch_shapes=list(comm.sem_shapes),
    )(*comm.inputs))


def _gather_exchange(shards):
    n_arr = len(shards)

    def plan(x_refs, out_refs, sems):
        send_sems, recv_sems, local_sems = sems[:3]
        stage = sems[3:]
        x, y, c = _mesh_pos()
        me, sibling = (x, y, c), (x, y, 1 - c)
        xn, yn, diag = (1 - x, y), (x, 1 - y), (1 - x, 1 - y)

        def slot(a, px, py, pc, half=None):
            ref = out_refs[a].at[4 * px + 2 * py + pc]
            if half is None:
                return ref
            rows = shards[a].shape[0] // 2
            return ref.at[half * rows:(half + 1) * rows]

        def copy(a, k, block, to, half=None, src=None):
            dst = slot(a, *block, half)
            return pltpu.make_async_remote_copy(
                src_ref=dst if src is None else src, dst_ref=dst,
                send_sem=send_sems.at[a, k], recv_sem=recv_sems.at[a, k],
                device_id=to, device_id_type=pl.DeviceIdType.MESH)

        return types.SimpleNamespace(
            me=me, sibling=sibling, xn=xn, yn=yn, diag=diag, c=c, copy=copy,
            mine_in=[pltpu.make_async_copy(x_refs[a], stage[a], local_sems.at[a, 0]) for a in range(n_arr)],
            mine_out=[pltpu.make_async_copy(stage[a], slot(a, *me), local_sems.at[a, 1]) for a in range(n_arr)],
            first=[cp for a in range(n_arr) for cp in (
                copy(a, 0, me, sibling, src=x_refs[a]), copy(a, 1, me, (*xn, c), src=x_refs[a]),
                copy(a, 2, me, (*yn, c), src=x_refs[a]))],
            second=lambda a: (copy(a, 3, (*xn, c), (*yn, c), half=0), copy(a, 5, (*xn, c), sibling),
                              copy(a, 4, (*yn, c), (*xn, c), half=1), copy(a, 6, (*yn, c), sibling)),
            third=lambda a: (copy(a, 7, (*diag, c), sibling, half=0), copy(a, 8, (*diag, c), sibling, half=1)))

    def start(x_refs, out_refs, sems):
        p = plan(x_refs, out_refs, sems)
        for cp in p.first + p.mine_in:
            cp.start()
        for cp_in, cp_out in zip(p.mine_in, p.mine_out):
            cp_in.wait()
            cp_out.start()

    def middle(x_refs, out_refs, sems):
        p = plan(x_refs, out_refs, sems)
        for a in range(n_arr):
            to_yn, x_to_sib, to_xn, y_to_sib = p.second(a)
            p.copy(a, 1, (*p.xn, p.c), p.me).wait_recv()
            to_yn.start()
            x_to_sib.start()
            p.copy(a, 2, (*p.yn, p.c), p.me).wait_recv()
            to_xn.start()
            y_to_sib.start()

    def finish(x_refs, out_refs, sems):
        p = plan(x_refs, out_refs, sems)
        for a in range(n_arr):
            half0_to_sib, half1_to_sib = p.third(a)
            p.copy(a, 3, (*p.diag, p.c), p.me, half=0).wait_recv()
            half0_to_sib.start()
            p.copy(a, 4, (*p.diag, p.c), p.me, half=1).wait_recv()
            half1_to_sib.start()
        other = 1 - p.c
        for a in range(n_arr):
            p.copy(a, 0, p.sibling, p.me).wait_recv()
            p.copy(a, 5, (*p.xn, other), p.me).wait_recv()
            p.copy(a, 6, (*p.yn, other), p.me).wait_recv()
            p.copy(a, 7, (*p.diag, other), p.me, half=0).wait_recv()
            p.copy(a, 8, (*p.diag, other), p.me, half=1).wait_recv()
        for cp in p.first:
            cp.wait_send()
        for a in range(n_arr):
            for cp in p.second(a) + p.third(a):
                cp.wait_send()
        for cp in p.mine_out:
            cp.wait()

    return types.SimpleNamespace(
        inputs=list(shards), start=start, middle=middle, finish=finish,
        out_shapes=[jax.ShapeDtypeStruct((N_DEV,) + s.shape, s.dtype) for s in shards],
        sem_shapes=[pltpu.SemaphoreType.DMA((n_arr, 9)), pltpu.SemaphoreType.DMA((n_arr, 9)),
                    pltpu.SemaphoreType.DMA((n_arr, 2))] + [pltpu.VMEM(s.shape, s.dtype) for s in shards])


def _pair_exchange(grads):
    n_arr = len(grads)

    def plan(g_refs, land_refs, sems):
        send_sems, recv_sems = sems
        x, y, c = _mesh_pos()
        return [pltpu.make_async_remote_copy(
            src_ref=g_refs[a].at[2 * k + 1 - c], dst_ref=land_refs[a].at[k],
            send_sem=send_sems.at[a, k], recv_sem=recv_sems.at[a, k],
            device_id=(x, y, 1 - c), device_id_type=pl.DeviceIdType.MESH)
            for a in range(n_arr) for k in range(N_CHIP)]

    def start(g_refs, land_refs, sems):
        for cp in plan(g_refs, land_refs, sems):
            cp.start()

    def finish(g_refs, land_refs, sems):
        for cp in plan(g_refs, land_refs, sems):
            cp.wait()

    return types.SimpleNamespace(
        inputs=list(grads), start=start, middle=_no_round, finish=finish,
        out_shapes=[jax.ShapeDtypeStruct((N_CHIP,) + g.shape[1:], g.dtype) for g in grads],
        sem_shapes=[pltpu.SemaphoreType.DMA((n_arr, N_CHIP)), pltpu.SemaphoreType.DMA((n_arr, N_CHIP))])


def _chip_exchange(parts):
    n_arr = len(parts)

    def plan(p_refs, land_refs, sems):
        send_sems, recv_sems, local_sems = sems
        x, y, c = _mesh_pos()
        my_chip = 2 * x + y
        chips = [(1 - x, y), (x, 1 - y), (1 - x, 1 - y)]
        local = [pltpu.make_async_copy(p_refs[a].at[my_chip], land_refs[a].at[my_chip], local_sems.at[a])
                 for a in range(n_arr)]

        def copy(a, k, src_slot, dst_slot, px, py):
            return pltpu.make_async_remote_copy(
                src_ref=p_refs[a].at[src_slot], dst_ref=land_refs[a].at[dst_slot],
                send_sem=send_sems.at[a, k], recv_sem=recv_sems.at[a, k],
                device_id=(px, py, c), device_id_type=pl.DeviceIdType.MESH)

        sends = [copy(a, k, 2 * px + py, my_chip, px, py) for a in range(n_arr) for k, (px, py) in enumerate(chips)]
        arrivals = [copy(a, k, my_chip, 2 * px + py, px, py) for a in range(n_arr)
                    for k, (px, py) in enumerate(chips)]
        return local, sends, arrivals

    def start(p_refs, land_refs, sems):
        local, sends, _ = plan(p_refs, land_refs, sems)
        for cp in local + sends:
            cp.start()

    def finish(p_refs, land_refs, sems):
        local, sends, arrivals = plan(p_refs, land_refs, sems)
        for cp in arrivals:
            cp.wait_recv()
        for cp in sends:
            cp.wait_send()
        for cp in local:
            cp.wait()

    return types.SimpleNamespace(
        inputs=list(parts), start=start, middle=_no_round, finish=finish,
        out_shapes=[jax.ShapeDtypeStruct(p.shape, p.dtype) for p in parts],
        sem_shapes=[pltpu.SemaphoreType.DMA((n_arr, 3)), pltpu.SemaphoreType.DMA((n_arr, 3)),
                    pltpu.SemaphoreType.DMA((n_arr,))])


_HBM = pl.BlockSpec(memory_space=pltpu.HBM)
_SEM = pl.BlockSpec(memory_space=pltpu.SEMAPHORE)
_DATAFLOW = pltpu.SideEffectType.DATAFLOW_SIDE_EFFECTING


def _chip_copies(p_refs, land_refs, send_sems, recv_sems):
    x, y, c = _mesh_pos()
    my_chip = 2 * x + y
    chips = [(1 - x, y), (x, 1 - y), (1 - x, 1 - y)]
    return [pltpu.make_async_remote_copy(
        src_ref=p_refs[a].at[2 * px + py], dst_ref=land_refs[a].at[my_chip],
        send_sem=send_sems[3 * a + k], recv_sem=recv_sems[3 * a + k],
        device_id=(px, py, c), device_id_type=pl.DeviceIdType.MESH)
        for a in range(len(p_refs)) for k, (px, py) in enumerate(chips)]


def _chip_exchange_begin(parts, name):
    n_arr = len(parts)
    n_buf, n_copy = 2 * n_arr, 3 * n_arr
    lands = [lax.empty(p.shape, p.dtype) for p in parts]

    def body(*refs):
        p_refs, land_refs = refs[:n_arr], refs[n_arr:n_buf]
        send_sems, recv_sems, token = refs[n_buf:n_buf + n_copy], refs[n_buf + n_copy:n_buf + 2 * n_copy], refs[-1]
        for cp in _chip_copies(p_refs, land_refs, send_sems, recv_sems):
            cp.start()
        token[...] = jnp.zeros_like(token)

    bufs = list(parts) + list(lands)
    outs = pl.pallas_call(
        body, name=name,
        out_shape=(*[pltpu.SemaphoreType.DMA(())] * (2 * n_copy), *[pltpu.HBM(b.shape, b.dtype) for b in bufs],
                   jax.ShapeDtypeStruct((8, 128), F32)),
        in_specs=[_HBM] * n_buf,
        out_specs=(*[_SEM] * (2 * n_copy), *[_HBM] * n_buf, pl.BlockSpec(memory_space=pltpu.VMEM)),
        input_output_aliases={i: 2 * n_copy + i for i in range(n_buf)},
        compiler_params=pltpu.CompilerParams(has_side_effects=_DATAFLOW),
    )(*[pltpu.with_memory_space_constraint(b, pltpu.HBM) for b in bufs])
    sems = list(outs[:2 * n_copy])
    thru = list(outs[2 * n_copy:2 * n_copy + n_buf])
    return types.SimpleNamespace(send_sems=sems[:n_copy], recv_sems=sems[n_copy:], parts=thru[:n_arr],
                                 lands=thru[n_arr:], token=outs[-1])


def _chip_exchange_end(flight, after, name):
    send_sems, recv_sems, parts, lands = flight.send_sems, flight.recv_sems, flight.parts, flight.lands
    n_arr = len(parts)
    n_buf, n_copy = 2 * n_arr, 3 * n_arr

    def body(*refs):
        p_refs, land_refs = refs[:n_arr], refs[n_arr:n_buf]
        sems = refs[n_buf:n_buf + 2 * n_copy]
        for cp in _chip_copies(p_refs, land_refs, sems[:n_copy], sems[n_copy:]):
            cp.wait_send()
            cp.wait_recv()

    bufs = list(parts) + list(lands)
    outs = pl.pallas_call(
        body, name=name, out_shape=tuple(pltpu.HBM(b.shape, b.dtype) for b in bufs),
        in_specs=[_HBM] * n_buf + [_SEM] * (2 * n_copy) + [_ANY] * len(after), out_specs=tuple([_HBM] * n_buf),
        input_output_aliases={i: i for i in range(n_buf)},
        compiler_params=pltpu.CompilerParams(has_side_effects=_DATAFLOW),
    )(*bufs, *send_sems, *recv_sems, *after)
    return list(outs[:n_arr]), list(outs[n_arr:])


def _row_tile(r, cap=640):
    best = None
    for cand in range(16, min(r, cap) + 1, 16):
        if r % cand == 0:
            best = cand
    return best if best is not None else r


def _pair_sum(gs, landeds, core, name):
    tiles = [_row_tile(g.shape[1]) for g in gs]
    counts = [g.shape[1] // tr for g, tr in zip(gs, tiles)]
    n_arr = len(gs)

    def body(core_ref, *refs):
        for a in range(n_arr):
            mine, theirs, out = refs[2 * a], refs[2 * a + 1], refs[2 * n_arr + a]
            out[0] = (mine[0].astype(F32) + theirs[0].astype(F32)).astype(out.dtype)

    in_specs, out_specs, operands = [], [], []
    for g, landed, tr, count in zip(gs, landeds, tiles, counts):
        c_dim = g.shape[2]
        last = count - 1
        in_specs += [pl.BlockSpec((1, tr, c_dim),
                                  lambda k, i, core_ref, last=last: (2 * k + core_ref[0], jnp.minimum(i, last), 0)),
                     pl.BlockSpec((1, tr, c_dim), lambda k, i, core_ref, last=last: (k, jnp.minimum(i, last), 0))]
        out_specs.append(pl.BlockSpec((1, tr, c_dim), lambda k, i, core_ref, last=last: (k, jnp.minimum(i, last), 0)))
        operands += [g, landed]
    return list(pl.pallas_call(
        body, name=name,
        grid_spec=pltpu.PrefetchScalarGridSpec(
            num_scalar_prefetch=1, grid=(N_CHIP, max(counts)), in_specs=in_specs, out_specs=out_specs),
        out_shape=[jax.ShapeDtypeStruct((N_CHIP,) + g.shape[1:], g.dtype) for g in gs],
        compiler_params=_cparams(2),
    )(core, *operands))


def _sum_slots(parts, name):
    n, r, c_dim = parts.shape
    tr = _row_tile(r)

    def body(p_ref, o_ref):
        acc = p_ref[0].astype(F32)
        for k in range(1, n):
            acc = acc + p_ref[k].astype(F32)
        o_ref[...] = acc

    return pl.pallas_call(
        body, grid=(r // tr,), name=name,
        in_specs=[pl.BlockSpec((n, tr, c_dim), lambda i: (0, i, 0))],
        out_specs=_row_spec(tr, c_dim),
        out_shape=jax.ShapeDtypeStruct((r, c_dim), F32),
        compiler_params=_cparams(),
    )(parts)


GAINS = ("g_ffn1", "g_mix", "g_cross", "g_mem", "g_ffn2", "g_final")
SMALL = GAINS + ("b_gate", "conv_w")
SMALL_R = 16
LOSS_ROW = 11
WEIGHT_ORDER = ("g_ffn1", "w_ffn1_gu", "w_ffn1_down", "g_mix", "w_in", "b_gate", "conv_w", "w_conv_out",
                "w_attn_out", "w_o", "g_cross", "g_mem", "w_cq", "w_ckv", "w_co", "g_ffn2", "w_ffn2_gu",
                "w_ffn2_down", "g_final")
GU_NAMES = ("w_ffn1_gu", "w_ffn2_gu")


def _pack_small(vals, conv_rows):
    rows = [vals[n].reshape(1, D) for n in GAINS] + [vals["b_gate"].reshape(2, D), conv_rows.reshape(CONV_K, D)]
    used = len(GAINS) + 2 + CONV_K
    return jnp.concatenate(rows + [jnp.zeros((SMALL_R - used, D), F32)], axis=0)


def _unpack_small(buf):
    out = {n: buf[k] for k, n in enumerate(GAINS)}
    out["b_gate"] = buf[6:8].reshape(2 * D)
    out["conv_w"] = buf[8:8 + CONV_K]
    return out


def _exchange_shards(wts):
    out = {n: jnp.pad(wts[n].T.astype(BF16), ((0, FF_PAD - FF_BLK), (0, 0))) for n in GU_NAMES}
    for n in ("w_ckv", "w_in", "w_ffn1_down", "w_ffn2_down"):
        out[n] = wts[n].astype(BF16)
    out["mix"] = jnp.concatenate([wts[n].astype(BF16) for n in MIX_MATS], axis=0)
    out["cross"] = jnp.concatenate([wts[n].astype(BF16) for n in CROSS_MATS], axis=0)
    return out


def _reduce_group(grads, landed, core, names):
    return _pair_sum(grads, landed, core, "grads_pair_sum_" + "_".join(names))


def _step(x, mem, target, sh, conv_pad, gains, b_gate, core):
    wg1, wd1, conv_all = _run_exchange(_gather_exchange([sh["w_ffn1_gu"], sh["w_ffn1_down"], conv_pad]), "gather_ffn1")
    conv_w = conv_all[:, :CONV_K, :].transpose(1, 0, 2).reshape(CONV_K, D)
    (n1, gate1, up1, act1, h1), (w_in,) = _ffn_fwd(
        x, gains["g_ffn1"], wg1, wd1, "ffn1_fwd", comm=_gather_exchange([sh["w_in"]]))
    (u, pcg, qkv, yc), (w_mix,) = _inproj_fwd(h1, gains["g_mix"], w_in, conv_w, "inproj_fwd",
                                              comm=_gather_exchange([sh["mix"]]))
    (ysb, ctot), (w_cross, w_ckv, wg2) = _sb_fwd(
        qkv, "sb_fwd", comm=_gather_exchange([sh["cross"], sh["w_ckv"], sh["w_ffn2_gu"]]))
    (a_mix, b_mix, merged, h2), (wd2,) = _mix_out_fwd(yc, ysb, pcg, b_gate, h1, w_mix, "mix_out_fwd",
                                                      comm=_gather_exchange([sh["w_ffn2_down"]]))
    hn, qx, o_x, h3, mn, kv = _cross_fwd(h2, gains["g_cross"], mem, gains["g_mem"], w_ckv, w_cross, "cross_fwd")
    (n4, gate2, up2, act2, dh4, loss, dg_final), _ = _ffn_fwd(h3, gains["g_ffn2"], wg2, wd2, "ffn2_fwd",
                                                              head=(gains["g_final"], target))

    gs = {"g_final": dg_final}
    (dgu2, dh4b, dh3, gs["g_ffn2"]), _ = _ffn_bwd(dh4, h3, gains["g_ffn2"], gate2, up2, wg2, wd2, "ffn2_bwd")
    grads_a = [_mm_tn_rows(dgu2, n4, FF_PAD, "dw_ffn2_gu"),
               _mm_tn_rows(act2, dh4b, FF_BLK, "dw_ffn2_down").reshape(N_DEV, DOWN_ROWS, D)]
    names_a = ["w_ffn2_gu", "w_ffn2_down"]
    (dh3b, dqx, dkv, dh2, gs["g_cross"], gs["g_mem"]), landed_a = _cross_bwd(
        dh3, h2, gains["g_cross"], qx, kv, mem, gains["g_mem"], w_ckv, w_cross, "cross_bwd",
        comm=_pair_exchange(grads_a))
    sums_a = _reduce_group(grads_a, landed_a, core, names_a)
    cross_stack = _mm_tn_square(hn, dqx, "dw_cq", 0, len(CROSS_MATS))
    grads_b = [_mm_tn_cols(mn, dkv, "dw_ckv"), _mm_tn_square(o_x, dh3b, "dw_co", 1, len(CROSS_MATS), cross_stack)]
    names_b = ["w_ckv", "cross"]
    (dh2b, da_mix, db_mix, dgp, dconv, dysb, gs["b_gate"], gs["conv_w"]), landed_b = _mix_out_bwd(
        dh2, a_mix, b_mix, pcg, b_gate, conv_w, w_mix, "mix_out_bwd", comm=_pair_exchange(grads_b))
    sums_b = _reduce_group(grads_b, landed_b, core, names_b)
    mix_stack = _mm_tn_square(yc, da_mix, "dw_conv_out", 0, len(MIX_MATS))
    mix_stack = _mm_tn_square(ysb, db_mix, "dw_attn_out", 1, len(MIX_MATS), mix_stack)
    grads_c = [_mm_tn_square(merged, dh2b, "dw_o", 2, len(MIX_MATS), mix_stack)]
    flight_ab = _chip_exchange_begin(sums_a + sums_b, "grads_to_chips_early_begin")
    (dq, dkv_sb), landed_c = _sb_bwd(qkv, dysb, ctot, flight_ab.token, "sb_bwd", comm=_pair_exchange(grads_c))
    sums_c = _reduce_group(grads_c, landed_c, core, ["mix"])
    w_in_stack = _mm_tn_cols(u, dconv, "dw_in_conv", 0, N_DEV)
    w_in_stack = _mm_tn_cols(u, dq[None], "dw_in_q", 3, N_DEV, w_in_stack)
    w_in_stack = _mm_tn_cols(u, dkv_sb, "dw_in_kv", 4, N_DEV, w_in_stack)
    grads_d = [_mm_tn_cols(u, dgp, "dw_in_gates", 6, N_DEV, w_in_stack)]
    (dh1, gs["g_mix"]), landed_d = _inproj_bwd(dconv, dq, dkv_sb, dgp, w_in, h1, gains["g_mix"], dh2, "inproj_bwd",
                                               comm=_pair_exchange(grads_d))
    sums_d = _reduce_group(grads_d, landed_d, core, ["w_in"])
    flight_d = _chip_exchange_begin(sums_c + sums_d, "grads_to_chips_w_in_begin")
    (dgu1, dh1b, dx, gs["g_ffn1"]), _ = _ffn_bwd(dh1, x, gains["g_ffn1"] + flight_d.token[0, 0], gate1, up1, wg1, wd1,
                                                 "ffn1_bwd")
    dw_gu1 = _mm_tn_rows(dgu1, n1, FF_PAD, "dw_ffn1_gu")
    dw_down1, landed_gu1 = _mm_tn_rows(act1, dh1b, FF_BLK, "dw_ffn1_down", comm=_pair_exchange([dw_gu1]))
    grads_e = [dw_gu1, dw_down1.reshape(N_DEV, DOWN_ROWS, D)]
    names_e = ["w_ffn1_gu", "w_ffn1_down"]
    landed_e = landed_gu1 + _run_exchange(_pair_exchange(grads_e[1:]), "grads_to_sibling_ffn1_down")
    flight_e = _chip_exchange_begin(_reduce_group(grads_e, landed_e, core, names_e), "grads_to_chips_ffn1_begin")
    flights = [(names_a + names_b, flight_ab), (["mix", "w_in"], flight_d), (names_e, flight_e)]
    return loss, dx, flights, gs


def kernel(x, mem, g_ffn1, w_ffn1_gu, w_ffn1_down, g_mix, w_in, b_gate, conv_w, w_conv_out, w_attn_out, w_o, g_cross, g_mem, w_cq, w_ckv, w_co, g_ffn2, w_ffn2_gu, w_ffn2_down, g_final, loss_target, m_g_ffn1, m_w_ffn1_gu, m_w_ffn1_down, m_g_mix, m_w_in, m_b_gate, m_conv_w, m_w_conv_out, m_w_attn_out, m_w_o, m_g_cross, m_g_mem, m_w_cq, m_w_ckv, m_w_co, m_g_ffn2, m_w_ffn2_gu, m_w_ffn2_down, m_g_final, v_g_ffn1, v_w_ffn1_gu, v_w_ffn1_down, v_g_mix, v_w_in, v_b_gate, v_conv_w, v_w_conv_out, v_w_attn_out, v_w_o, v_g_cross, v_g_mem, v_w_cq, v_w_ckv, v_w_co, v_g_ffn2, v_w_ffn2_gu, v_w_ffn2_down, v_g_final):
    args = locals()
    wts = {n: args[n] for n in WEIGHT_ORDER}
    mom1 = {n: args["m_" + n] for n in WEIGHT_ORDER}
    mom2 = {n: args["v_" + n] for n in WEIGHT_ORDER}
    cx, cy, cc = _mesh_pos()
    dev = 4 * cx + 2 * cy + cc
    conv_cols = D // N_DEV

    conv_pad = jnp.concatenate([conv_w, jnp.zeros((SMALL_R - CONV_K, conv_cols), F32)], axis=0)
    gains = {n: wts[n].reshape(1, D) for n in GAINS}
    loss8, dx, flights, gs = _step(x[0], mem[0], loss_target[0], _exchange_shards(wts), conv_pad, gains,
                                 b_gate.reshape(1, 2 * D), cc.reshape(1).astype(jnp.int32))

    grads, delta, new_m, new_v = {}, {}, {}, {}

    def operands(n, transposed):
        trio = (wts[n], mom1[n], mom2[n])
        return tuple(a.T for a in trio) if transposed else trio

    def record(n, res, transposed):
        grads[n], delta[n], new_m[n], new_v[n] = [r.T for r in res] if transposed else res

    early = [("w_ffn2_gu", "w_ffn2_gu", 0, True), ("w_ffn2_down", "w_ffn2_down", 0, False),
             ("w_ckv", "w_ckv", 0, False), ("w_in", "w_in", 0, False)]
    early += [(n, "mix", k, False) for k, n in enumerate(MIX_MATS)]
    early += [(n, "cross", k, False) for k, n in enumerate(CROSS_MATS)]
    chip = (2 * cx + cy).reshape(1).astype(jnp.int32)
    (names_early, flight_early), (names_w_in, flight_w_in), (last_names, flight_last) = flights
    token = flight_last.token
    own, land = {}, {}
    for names, flight, tag in ((names_early, flight_early, "early"), (names_w_in, flight_w_in, "w_in")):
        own_parts, landed = _chip_exchange_end(flight, [token], "grads_to_chips_%s_end" % tag)
        own.update(zip(names, own_parts))
        land.update(zip(names, landed))
    for n, buf, row_block, transposed in early:
        w, m1, m2 = operands(n, transposed)
        record(n, _adamw_own(w, land[buf], own[buf], chip, m1, m2, "adamw_" + n, row_block, token), transposed)

    after = [delta[n] for n, _, _, transposed in early if not transposed]
    own_parts, landed = _chip_exchange_end(flight_last, after, "grads_to_chips_ffn1_end")
    for n, own_n, land_n, transposed in zip(last_names, own_parts, landed, (True, False)):
        w, m1, m2 = operands(n, transposed)
        record(n, _adamw_own(w, land_n, own_n, chip, m1, m2, "adamw_" + n), transposed)

    gs_rows = {n: gs[n] for n in GAINS + ("b_gate",)}
    small_mine = _pack_small(gs_rows, gs["conv_w"][:CONV_K]) + new_v[last_names[-1]][0, 0] * 0.0
    small_mine = small_mine.at[LOSS_ROW, 0].set(loss8[0, 0])
    small_all = _run_exchange(_gather_exchange([small_mine]), "gather_small_grads")[0]
    small_sum = _sum_slots(small_all, "small_grads_sum")
    loss = small_sum[LOSS_ROW, 0]
    grad_small = _unpack_small(small_sum)
    grad_small["conv_w"] = lax.dynamic_slice_in_dim(grad_small["conv_w"], dev * conv_cols, conv_cols, axis=1)
    grads.update(grad_small)

    def small_buf(vals):
        return _pack_small(vals, jnp.concatenate([vals["conv_w"], jnp.zeros((CONV_K, D - conv_cols), F32)], axis=1))

    _, d_s, m_s, v_s = _adamw(small_buf(wts), small_buf(grads)[None], small_buf(mom1), small_buf(mom2), "adamw_small")
    for res, buf in ((delta, d_s), (new_m, m_s), (new_v, v_s)):
        un = _unpack_small(buf)
        for n in GAINS + ("b_gate",):
            res[n] = un[n]
        res["conv_w"] = un["conv_w"][:, :conv_cols]

    return (loss, dx[None], *[grads[n] for n in WEIGHT_ORDER], *[delta[n] for n in WEIGHT_ORDER],
            *[new_m[n] for n in WEIGHT_ORDER], *[new_v[n] for n in WEIGHT_ORDER])
```

```python
import types

import jax
import jax.numpy as jnp
from jax import lax
from jax.experimental import pallas as pl
from jax.experimental.pallas import tpu as pltpu

F32 = jnp.float32
BF16 = jnp.bfloat16

D = 1024
DFF = 2816
SB_H = 8
SB_DH = 128
X_H = 4
X_DH = 256
CONV_K = 3
RMS_EPS = 1e-6
N_DEV = 8
N_CHIP = 4
SQ_ROWS = D // N_DEV

ADAM_LR = 0.001
ADAM_B1 = 0.9
ADAM_B2 = 0.999
ADAM_EPS = 1e-08
ADAM_WD = 0.01
ADAM_STEP = 10

TM = 256
TQ = 512
TK = 256
SB_HPS = 2
VMEM_LIMIT = 56 << 20

FF_BLK = DFF // 4
FF_PAD = 768
FF_SUB = 256
DOWN_ROWS = DFF // N_DEV

MIX_MATS = ("w_conv_out", "w_attn_out", "w_o")
CROSS_MATS = ("w_cq", "w_co")

_ANY = pl.BlockSpec(memory_space=pl.ANY)


def _cparams(n_axes=1):
    return pltpu.CompilerParams(
        dimension_semantics=("arbitrary",) * n_axes, vmem_limit_bytes=VMEM_LIMIT)


def _row_spec(tm, n):
    return pl.BlockSpec((tm, n), lambda i: (i, 0))


def _blk_row_spec(nb, tm, n):
    return pl.BlockSpec((nb, tm, n), lambda i: (0, i, 0))


def _const_spec(shape):
    zeros = (0,) * len(shape)
    return pl.BlockSpec(shape, lambda i: zeros)


def _dot(a, b):
    return jnp.dot(a, b, preferred_element_type=F32)


def _dot_nt(a, b):
    return lax.dot_general(a, b, (((1,), (1,)), ((), ())), preferred_element_type=F32)


def _dot_tn(a, b):
    return lax.dot_general(a, b, (((0,), (0,)), ((), ())), preferred_element_type=F32)


def _sigmoid(x):
    return 1.0 / (1.0 + jnp.exp(-x))


def _call(body, operands, *, grid, in_specs, out_specs, out_shape, scratch_shapes, name, comm=None):
    n_in, n_out, n_sc = len(in_specs), len(out_specs), len(scratch_shapes)
    if comm is None:
        outs = pl.pallas_call(
            body, grid=grid, name=name, in_specs=in_specs, out_specs=out_specs, out_shape=out_shape,
            scratch_shapes=scratch_shapes, compiler_params=_cparams(len(grid)))(*operands)
        return list(outs), []
    c_in, c_out, c_sem = len(comm.inputs), len(comm.out_shapes), len(comm.sem_shapes)

    def hosted(*refs):
        bounds = [0, n_in, c_in, n_out, c_out, n_sc, c_sem]
        parts, pos = [], 0
        for k in bounds[1:]:
            parts.append(refs[pos:pos + k])
            pos += k
        ins, cins, outs, couts, scr, sems = parts
        step, n_steps = pl.program_id(0), grid[0]
        for ax in range(1, len(grid)):
            step, n_steps = step * grid[ax] + pl.program_id(ax), n_steps * grid[ax]

        @pl.when(step == 0)
        def _():
            comm.start(cins, couts, sems)

        @pl.when(step == (2 * n_steps) // 3)
        def _():
            comm.middle(cins, couts, sems)

        body(*ins, *outs, *scr)

        @pl.when(step == n_steps - 1)
        def _():
            comm.finish(cins, couts, sems)

    res = pl.pallas_call(
        hosted, grid=grid, name=name, in_specs=list(in_specs) + [_ANY] * c_in,
        out_specs=list(out_specs) + [_ANY] * c_out, out_shape=list(out_shape) + list(comm.out_shapes),
        scratch_shapes=list(scratch_shapes) + list(comm.sem_shapes),
        compiler_params=_cparams(len(grid)))(*operands, *comm.inputs)
    return list(res[:n_out]), list(res[n_out:])


def _load_resident(step, pairs, sems):
    @pl.when(step == 0)
    def _():
        copies = [pltpu.make_async_copy(src, dst, sems.at[k]) for k, (src, dst) in enumerate(pairs)]
        for cp in copies:
            cp.start()
        for cp in copies:
            cp.wait()


def _square_pairs(buf_hbm, index, dst):
    off = index * SQ_ROWS
    return [(buf_hbm.at[d, off:off + SQ_ROWS, :], dst.at[d * SQ_ROWS:(d + 1) * SQ_ROWS, :]) for d in range(N_DEV)]


def _down_pairs(wd_hbm, dst):
    return [(wd_hbm.at[d], dst.at[d // 2, (d % 2) * DOWN_ROWS:(d % 2 + 1) * DOWN_ROWS, :]) for d in range(N_DEV)]


def _zero_down_pad(step, dst):
    @pl.when(step == 0)
    def _():
        dst[:, FF_BLK:, :] = jnp.zeros((4, FF_PAD - FF_BLK, D), BF16)


def _rms_fwd_tile(xt, g):
    r = lax.rsqrt(jnp.mean(xt * xt, axis=-1, keepdims=True) + RMS_EPS)
    return (xt * r) * g


def _rms_bwd_tile(xt, g, dn):
    r = lax.rsqrt(jnp.mean(xt * xt, axis=-1, keepdims=True) + RMS_EPS)
    xhat = xt * r
    dxhat = dn * g
    dx = r * (dxhat - xhat * jnp.mean(dxhat * xhat, axis=-1, keepdims=True))
    dg = jnp.sum(dn * xhat, axis=0, keepdims=True)
    return dx, dg


def _accumulate(ref, step, value):
    @pl.when(step == 0)
    def _():
        ref[...] = value

    @pl.when(step != 0)
    def _():
        ref[...] = ref[...] + value


def _ffn_fwd(x, g, wgu, wd, name, comm=None, head=None):
    t = x.shape[0]

    def body(x_ref, g_ref, wgu_hbm, wd_hbm, *refs):
        if head is None:
            n_ref, gate_ref, up_ref, act_ref, h_ref, wgu_v, wd_v, sems = refs
        else:
            gf_ref, t_ref, n_ref, gate_ref, up_ref, act_ref, dh_ref, loss_ref, dgf_ref, wgu_v, wd_v, sems = refs
        step = pl.program_id(0)
        _zero_down_pad(step, wd_v)
        _load_resident(step, [(wgu_hbm, wgu_v)] + _down_pairs(wd_hbm, wd_v), sems)
        xt = x_ref[...]
        n = _rms_fwd_tile(xt, g_ref[...]).astype(BF16)
        n_ref[...] = n
        acc = jnp.zeros((TM, D), F32)
        for j in range(4):
            for s in range(FF_PAD // FF_SUB):
                lo, hi = s * FF_SUB, (s + 1) * FF_SUB
                gt = _dot_nt(n, wgu_v[j, lo:hi, :])
                ut = _dot_nt(n, wgu_v[4 + j, lo:hi, :])
                gate_ref[j, :, lo:hi] = gt.astype(BF16)
                up_ref[j, :, lo:hi] = ut.astype(BF16)
                act_ref[j, :, lo:hi] = ((gt * _sigmoid(gt)) * ut).astype(BF16)
            acc = acc + _dot(act_ref[j], wd_v[j])
        ht = xt + 0.5 * acc
        if head is None:
            h_ref[...] = ht
        else:
            gain = gf_ref[...]
            diff = _rms_fwd_tile(ht, gain) - t_ref[...]
            part = 0.5 * jnp.sum(jnp.sum(diff * diff, axis=-1, keepdims=True) / D, axis=0, keepdims=True)
            dx, dg = _rms_bwd_tile(ht, gain, diff / D)
            dh_ref[...] = dx
            _accumulate(loss_ref, step, jnp.broadcast_to(part, (8, 128)))
            _accumulate(dgf_ref, step, dg)

    ff = jax.ShapeDtypeStruct((4, t, FF_PAD), BF16)
    operands, in_specs = (x, g, wgu, wd), [_row_spec(TM, D), _const_spec((1, D)), _ANY, _ANY]
    out_specs = [_row_spec(TM, D)] + [_blk_row_spec(4, TM, FF_PAD)] * 3 + [_row_spec(TM, D)]
    out_shape = [jax.ShapeDtypeStruct((t, D), BF16), ff, ff, ff, jax.ShapeDtypeStruct((t, D), F32)]
    if head is not None:
        operands += tuple(head)
        in_specs += [_const_spec((1, D)), _row_spec(TM, D)]
        out_specs += [_const_spec((8, 128)), _const_spec((1, D))]
        out_shape += [jax.ShapeDtypeStruct((8, 128), F32), jax.ShapeDtypeStruct((1, D), F32)]
    return _call(
        body, operands, grid=(t // TM,), name=name, comm=comm, in_specs=in_specs, out_specs=out_specs,
        out_shape=out_shape,
        scratch_shapes=[pltpu.VMEM((N_DEV, FF_PAD, D), BF16), pltpu.VMEM((4, FF_PAD, D), BF16),
                        pltpu.SemaphoreType.DMA((1 + N_DEV,))])


def _ffn_bwd(dh, xin, g, gate, up, wgu, wd, name, comm=None):
    t = dh.shape[0]

    def body(dh_ref, x_ref, g_ref, gate_ref, up_ref, wgu_hbm, wd_hbm,
             dgu_ref, dhb_ref, dx_ref, dg_ref, wgu_v, wd_v, sems):
        step = pl.program_id(0)
        _zero_down_pad(step, wd_v)
        _load_resident(step, [(wgu_hbm, wgu_v)] + _down_pairs(wd_hbm, wd_v), sems)
        dht = dh_ref[...]
        dhb = (0.5 * dht).astype(BF16)
        dhb_ref[...] = dhb
        dn = jnp.zeros((TM, D), F32)
        for j in range(4):
            for s in range(FF_PAD // FF_SUB):
                lo, hi = s * FF_SUB, (s + 1) * FF_SUB
                da = _dot_nt(dhb, wd_v[j, lo:hi, :])
                gt = gate_ref[j, :, lo:hi].astype(F32)
                ut = up_ref[j, :, lo:hi].astype(F32)
                sg = _sigmoid(gt)
                dgt = (da * ut * (sg * (1.0 + gt * (1.0 - sg)))).astype(BF16)
                dut = (da * (gt * sg)).astype(BF16)
                dgu_ref[j, :, lo:hi] = dgt
                dgu_ref[4 + j, :, lo:hi] = dut
            dn = dn + _dot(dgu_ref[j], wgu_v[j]) + _dot(dgu_ref[4 + j], wgu_v[4 + j])
        dx, dg = _rms_bwd_tile(x_ref[...], g_ref[...], dn)
        dx_ref[...] = dht + dx
        _accumulate(dg_ref, step, dg)

    return _call(
        body, (dh, xin, g, gate, up, wgu, wd), grid=(t // TM,), name=name, comm=comm,
        in_specs=[_row_spec(TM, D), _row_spec(TM, D), _const_spec((1, D)), _blk_row_spec(4, TM, FF_PAD),
                  _blk_row_spec(4, TM, FF_PAD), _ANY, _ANY],
        out_specs=[_blk_row_spec(N_DEV, TM, FF_PAD), _row_spec(TM, D), _row_spec(TM, D), _const_spec((1, D))],
        out_shape=[jax.ShapeDtypeStruct((N_DEV, t, FF_PAD), BF16), jax.ShapeDtypeStruct((t, D), BF16),
                   jax.ShapeDtypeStruct((t, D), F32), jax.ShapeDtypeStruct((1, D), F32)],
        scratch_shapes=[pltpu.VMEM((N_DEV, FF_PAD, D), BF16), pltpu.VMEM((4, FF_PAD, D), BF16),
                        pltpu.SemaphoreType.DMA((1 + N_DEV,))])


WIDE_TILES = (1024, 512, 256, 128)


def _pick_tile(n, options=(512, 256, 128)):
    for o in options:
        if n % o == 0:
            return o
    return n


def _into(stack, n_operands):
    if stack is None:
        return (), [], {}
    return (stack,), [_ANY], {n_operands: 0}


def _mm_tn_square(a, b, name, index, count, stack=None):
    k, m = a.shape
    _, n = b.shape
    tn = _pick_tile(n)
    extra, extra_specs, aliases = _into(stack, 2)

    def body(a_ref, b_ref, *rest):
        rest[-1][...] = _dot_tn(a_ref[...], b_ref[...]).astype(BF16).reshape(N_DEV, m // N_DEV, tn)

    return pl.pallas_call(
        body, grid=(n // tn,), name=name,
        in_specs=[pl.BlockSpec((k, m), lambda j: (0, 0)), pl.BlockSpec((k, tn), lambda j: (0, j))] + extra_specs,
        out_specs=pl.BlockSpec((N_DEV, m // N_DEV, tn), lambda j: (0, index, j)),
        out_shape=jax.ShapeDtypeStruct((N_DEV, count * (m // N_DEV), n), BF16),
        input_output_aliases=aliases, compiler_params=_cparams(1),
    )(a, b, *extra)


def _mm_tn_cols(a, b, name, first=0, count=None, stack=None):
    k, m = a.shape
    nb, _, n = b.shape
    tm = _pick_tile(m, WIDE_TILES)
    extra, extra_specs, aliases = _into(stack, 2)

    def body(a_ref, b_ref, *rest):
        rest[-1][0] = _dot_tn(a_ref[...].astype(BF16), b_ref[0].astype(BF16)).astype(BF16)

    return pl.pallas_call(
        body, grid=(nb, m // tm), name=name,
        in_specs=[pl.BlockSpec((k, tm), lambda j, i: (0, i)), pl.BlockSpec((1, k, n), lambda j, i: (j, 0, 0))]
                 + extra_specs,
        out_specs=pl.BlockSpec((1, tm, n), lambda j, i: (j + first, i, 0)),
        out_shape=jax.ShapeDtypeStruct((nb if count is None else count, m, n), BF16),
        input_output_aliases=aliases, compiler_params=_cparams(2),
    )(a, b, *extra)


def _mm_tn_rows(a, b, keep, name, comm=None):
    nb, k, m = a.shape
    _, n = b.shape
    tn = _pick_tile(n, WIDE_TILES)

    def body(a_ref, b_ref, o_ref):
        o_ref[0] = _dot_tn(a_ref[0], b_ref[...])[:keep].astype(BF16)

    (out,), couts = _call(
        body, (a, b), grid=(nb, n // tn), name=name, comm=comm,
        in_specs=[pl.BlockSpec((1, k, m), lambda j, i: (j, 0, 0)), pl.BlockSpec((k, tn), lambda j, i: (0, i))],
        out_specs=[pl.BlockSpec((1, keep, tn), lambda j, i: (j, 0, i))],
        out_shape=[jax.ShapeDtypeStruct((nb, keep, n), BF16)], scratch_shapes=[])
    return out if comm is None else (out, couts)


PCG_W = 5 * D
QKV_W = 3 * D
PROJ_SUB = 512


def _inproj_fwd(h, g, w_in, conv_w, name, comm=None):
    t = h.shape[0]

    def body(h_ref, g_ref, w_hbm, cw_ref, u_ref, pcg_ref, qkv_ref, yc_ref, w_v, tail_v, sems):
        step = pl.program_id(0)
        _load_resident(step, [(w_hbm, w_v)], sems)

        @pl.when(step == 0)
        def _():
            tail_v[...] = jnp.zeros_like(tail_v)

        u = _rms_fwd_tile(h_ref[...], g_ref[...]).astype(BF16)
        u_ref[...] = u
        for blk in range(N_DEV):
            for s in range(D // PROJ_SUB):
                lo, hi = s * PROJ_SUB, (s + 1) * PROJ_SUB
                p = _dot(u, w_v[blk, :, lo:hi])
                if blk < 3:
                    pcg_ref[:, blk * D + lo:blk * D + hi] = p
                elif blk < 6:
                    qkv_ref[:, (blk - 3) * D + lo:(blk - 3) * D + hi] = p.astype(BF16)
                else:
                    pcg_ref[:, (blk - 3) * D + lo:(blk - 3) * D + hi] = p
        xc = pcg_ref[:, D:2 * D] * pcg_ref[:, 2 * D:3 * D]
        ext = jnp.concatenate([tail_v[...], xc], axis=0)
        conv = (cw_ref[0:1, :] * pltpu.roll(ext, 2, 0)[8:] + cw_ref[1:2, :] * pltpu.roll(ext, 1, 0)[8:]
                + cw_ref[2:3, :] * xc)
        yc_ref[...] = (pcg_ref[:, 0:D] * conv).astype(BF16)
        tail_v[...] = xc[TM - 8:]

    return _call(
        body, (h, g, w_in, conv_w), grid=(t // TM,), name=name, comm=comm,
        in_specs=[_row_spec(TM, D), _const_spec((1, D)), _ANY, _const_spec((CONV_K, D))],
        out_specs=[_row_spec(TM, D), _row_spec(TM, PCG_W), _row_spec(TM, QKV_W), _row_spec(TM, D)],
        out_shape=[jax.ShapeDtypeStruct((t, D), BF16), jax.ShapeDtypeStruct((t, PCG_W), F32),
                   jax.ShapeDtypeStruct((t, QKV_W), BF16), jax.ShapeDtypeStruct((t, D), BF16)],
        scratch_shapes=[pltpu.VMEM((N_DEV, D, D), BF16), pltpu.VMEM((8, D), F32), pltpu.SemaphoreType.DMA((1,))])


def _tri2(cond):
    rr = lax.broadcasted_iota(jnp.int32, (2 * TK, TK), 0) & (TK - 1)
    cc = lax.broadcasted_iota(jnp.int32, (2 * TK, TK), 1)
    return cond(rr, cc).astype(BF16)


def _causal(shift, row0=0):
    rr = lax.broadcasted_iota(jnp.int32, (TQ - row0, TK), 0) + row0
    cc = lax.broadcasted_iota(jnp.int32, (TQ - row0, TK), 1)
    return cc + shift < rr


def _cumdot(v, tri2):
    hi = v.astype(BF16)
    lo = (v - hi.astype(F32)).astype(BF16)
    return _dot(jnp.concatenate([hi, lo], axis=1), tri2)


def _log_1m_beta(z):
    return -(jnp.maximum(z, 0.0) + jnp.log(1.0 + jnp.exp(-jnp.abs(z))))


def _sb_specs(t):
    g = SB_H // SB_HPS
    w = SB_HPS * SB_DH
    q_spec = pl.BlockSpec((TQ, w), lambda h, i: (i, h))
    k_spec = pl.BlockSpec((t, w), lambda h, i: (0, g + h))
    v_spec = pl.BlockSpec((t, w), lambda h, i: (0, 2 * g + h))
    ct_spec = pl.BlockSpec((SB_HPS, TQ, 1), lambda h, i: (h, i, 0))
    return g, w, q_spec, k_spec, v_spec, ct_spec


def _sb_fwd(qkv, name, comm=None):
    t = qkv.shape[0]
    scale = SB_DH ** -0.5
    g, w, q_spec, k_spec, v_spec, ct_spec = _sb_specs(t)

    def body(q_ref, k_ref, v_ref, y_ref, ct_ref):
        i = pl.program_id(1)
        later = _tri2(lambda j, s: j > s)
        n_diag = TQ // TK

        def block(j, carry, shift):
            off = pl.multiple_of(j * TK, TK)
            zs, ms = [], []
            for hd in range(SB_HPS):
                cols = slice(hd * SB_DH, (hd + 1) * SB_DH)
                z = _dot_nt(q_ref[:, cols], k_ref[pl.ds(off, TK), cols]) * scale
                m = _log_1m_beta(z)
                if shift is not None:
                    m = jnp.where(_causal(shift), m, 0.0)
                zs.append(z)
                ms.append(m)
            after = _cumdot(jnp.concatenate(ms, axis=0), later)
            out = []
            for hd in range(SB_HPS):
                acc, c_sum = carry[hd]
                cols = slice(hd * SB_DH, (hd + 1) * SB_DH)
                a = jnp.exp((ms[hd] + zs[hd]) + (c_sum + after[hd * TQ:(hd + 1) * TQ]))
                if shift is not None:
                    a = jnp.where(_causal(shift), a, 0.0)
                out.append((acc + _dot(a.astype(BF16), v_ref[pl.ds(off, TK), cols]),
                            c_sum + jnp.sum(ms[hd], axis=1, keepdims=True)))
            return tuple(out)

        carry = tuple((jnp.zeros((TQ, SB_DH), F32), jnp.zeros((TQ, 1), F32)) for _ in range(SB_HPS))
        for d in reversed(range(n_diag)):
            carry = block(i * n_diag + d, carry, d * TK)
        carry = lax.fori_loop(0, i * n_diag, lambda jj, c: block(i * n_diag - 1 - jj, c, None), carry)
        for hd in range(SB_HPS):
            y_ref[:, hd * SB_DH:(hd + 1) * SB_DH] = carry[hd][0].astype(BF16)
            ct_ref[hd] = carry[hd][1]

    return _call(
        body, (qkv, qkv, qkv), grid=(g, t // TQ), name=name, comm=comm,
        in_specs=[q_spec, k_spec, v_spec],
        out_specs=[q_spec, ct_spec],
        out_shape=[jax.ShapeDtypeStruct((t, D), BF16), jax.ShapeDtypeStruct((SB_H, t, 1), F32)],
        scratch_shapes=[])


def _sb_bwd(qkv, dy, ctot, after, name, comm=None):
    t = qkv.shape[0]
    scale = SB_DH ** -0.5
    g, w, q_spec, k_spec, v_spec, ct_spec = _sb_specs(t)
    acc_spec = pl.BlockSpec((2, t, w), lambda h, i: (0, 0, h))

    def body(q_ref, k_ref, v_ref, dy_ref, ct_ref, after_ref, dq_ref, dkv_ref):
        i = pl.program_id(1)

        @pl.when(i == 0)
        def _():
            dkv_ref[...] = jnp.zeros_like(dkv_ref)

        upto = _tri2(lambda j, s: j <= s)
        n_diag = TQ // TK

        def block(j, carry, shift):
            off = pl.multiple_of(j * TK, TK)
            r0 = 0 if shift is None else shift
            nr = TQ - r0
            causal = None if shift is None else _causal(shift, r0)

            def grow(old, delta):
                return old + delta if r0 == 0 else jnp.concatenate([old[:r0], old[r0:] + delta], axis=0)

            zs, ms = [], []
            for hd in range(SB_HPS):
                cols = slice(hd * SB_DH, (hd + 1) * SB_DH)
                z = _dot_nt(q_ref[r0:, cols], k_ref[pl.ds(off, TK), cols]) * scale
                m = _log_1m_beta(z)
                if causal is not None:
                    m = jnp.where(causal, m, 0.0)
                zs.append(z)
                ms.append(m)
            m_upto = _cumdot(jnp.concatenate(ms, axis=0), upto)
            ls, a_s, es = [], [], []
            for hd in range(SB_HPS):
                cols = slice(hd * SB_DH, (hd + 1) * SB_DH)
                l = ms[hd] + zs[hd]
                a = jnp.exp(l + ((ct_ref[hd, r0:] - carry[hd][1][r0:]) - m_upto[hd * nr:(hd + 1) * nr]))
                if causal is not None:
                    a = jnp.where(causal, a, 0.0)
                ls.append(l)
                a_s.append(a)
                es.append(_dot_nt(dy_ref[r0:, cols], v_ref[pl.ds(off, TK), cols]) * a)
            e_upto = _dot(jnp.concatenate(es, axis=0).astype(BF16), upto[:TK])
            out = []
            for hd in range(SB_HPS):
                dq, p_sum, e_sum = carry[hd]
                cols = slice(hd * SB_DH, (hd + 1) * SB_DH)
                e = es[hd]
                dz = e - jnp.exp(ls[hd]) * (e_sum[r0:] + e_upto[hd * nr:(hd + 1) * nr])
                if causal is not None:
                    dz = jnp.where(causal, dz, 0.0)
                dzs = (dz * scale).astype(BF16)
                dkv_ref[0, pl.ds(off, TK), cols] += _dot_tn(dzs, q_ref[r0:, cols])
                dkv_ref[1, pl.ds(off, TK), cols] += _dot_tn(a_s[hd].astype(BF16), dy_ref[r0:, cols])
                out.append((grow(dq, _dot(dzs, k_ref[pl.ds(off, TK), cols])),
                            grow(p_sum, jnp.sum(ms[hd], axis=1, keepdims=True)),
                            grow(e_sum, jnp.sum(e, axis=1, keepdims=True))))
            return tuple(out)

        zero = jnp.zeros((TQ, 1), F32)
        init = tuple((jnp.zeros((TQ, SB_DH), F32), zero, zero) for _ in range(SB_HPS))
        carry = lax.fori_loop(0, i * n_diag, lambda j, c: block(j, c, None), init)
        for d in range(n_diag):
            carry = block(i * n_diag + d, carry, d * TK)
        for hd in range(SB_HPS):
            dq_ref[:, hd * SB_DH:(hd + 1) * SB_DH] = carry[hd][0].astype(BF16)

    return _call(
        body, (qkv, qkv, qkv, dy, ctot, after), grid=(g, t // TQ), name=name, comm=comm,
        in_specs=[q_spec, k_spec, v_spec, q_spec, ct_spec, pl.BlockSpec(after.shape, lambda h, i: (0, 0))],
        out_specs=[q_spec, acc_spec],
        out_shape=[jax.ShapeDtypeStruct((t, D), BF16), jax.ShapeDtypeStruct((2, t, D), F32)],
        scratch_shapes=[])


def _gate_specs():
    return [pl.BlockSpec((TM, D), lambda i: (i, 3)), pl.BlockSpec((TM, D), lambda i: (i, 4))]


def _mix_pairs(mix_hbm, dsts):
    pairs = []
    for index, dst in enumerate(dsts):
        pairs += _square_pairs(mix_hbm, index, dst)
    return pairs


def _mix_out_fwd(yc, ysb, pcg, b_gate, h, w_mix, name, comm=None):
    t = h.shape[0]

    def body(yc_ref, ysb_ref, gc_ref, gs_ref, b_ref, h_ref, mix_hbm,
             a_ref, b_out_ref, mg_ref, h2_ref, wc_v, wa_v, wo_v, sems):
        _load_resident(pl.program_id(0), _mix_pairs(mix_hbm, (wc_v, wa_v, wo_v)), sems)
        a = _dot(yc_ref[...], wc_v[...])
        b = _dot(ysb_ref[...], wa_v[...])
        merged = (_sigmoid(gc_ref[...] + b_ref[:, :D]) * a + _sigmoid(gs_ref[...] + b_ref[:, D:]) * b).astype(BF16)
        a_ref[...] = a
        b_out_ref[...] = b
        mg_ref[...] = merged
        h2_ref[...] = h_ref[...] + _dot(merged, wo_v[...])

    return _call(
        body, (yc, ysb, pcg, pcg, b_gate, h, w_mix), grid=(t // TM,), name=name, comm=comm,
        in_specs=[_row_spec(TM, D), _row_spec(TM, D)] + _gate_specs()
                 + [_const_spec((1, 2 * D)), _row_spec(TM, D), _ANY],
        out_specs=[_row_spec(TM, D)] * 4,
        out_shape=[jax.ShapeDtypeStruct((t, D), F32), jax.ShapeDtypeStruct((t, D), F32),
                   jax.ShapeDtypeStruct((t, D), BF16), jax.ShapeDtypeStruct((t, D), F32)],
        scratch_shapes=[pltpu.VMEM((D, D), BF16)] * 3 + [pltpu.SemaphoreType.DMA((3 * N_DEV,))])


def _mix_out_bwd(dh2, a, b, pcg, b_gate, conv_w, w_mix, name, comm=None):
    t = dh2.shape[0]
    n_tile = t // TM
    per8 = TM // 8

    def rows(n):
        return pl.BlockSpec((TM, n), lambda i: (n_tile - 1 - i, 0))

    def cols(block):
        return pl.BlockSpec((TM, D), lambda i: (n_tile - 1 - i, block))

    def before(block):
        return pl.BlockSpec((8, D), lambda i: (jnp.maximum((n_tile - 1 - i) * per8 - 1, 0), block))

    def body(dh_ref, a_ref, b_ref, gc_ref, gs_ref, cb_ref, cc_ref, cx_ref, ccp_ref, cxp_ref, bias_ref, cw_ref, mix_hbm,
             dhb_ref, da_ref, db_ref, dgp_ref, dc_ref, dysb_ref, dbias_ref, dcw_ref, wc_v, wa_v, wo_v, head_v, sems):
        step = pl.program_id(0)
        _load_resident(step, _mix_pairs(mix_hbm, (wc_v, wa_v, wo_v)), sems)

        @pl.when(step == 0)
        def _():
            head_v[...] = jnp.zeros_like(head_v)
            dcw_ref[...] = jnp.zeros_like(dcw_ref)

        dhb = dh_ref[...].astype(BF16)
        dhb_ref[...] = dhb
        dm = _dot_nt(dhb, wo_v[...])
        gc = _sigmoid(gc_ref[...] + bias_ref[:, :D])
        gs = _sigmoid(gs_ref[...] + bias_ref[:, D:])
        da = (dm * gc).astype(BF16)
        db = (dm * gs).astype(BF16)
        da_ref[...] = da
        db_ref[...] = db
        dgc = dm * a_ref[...] * (gc * (1.0 - gc))
        dgs = dm * b_ref[...] * (gs * (1.0 - gs))
        dgp_ref[0] = dgc.astype(BF16)
        dgp_ref[1] = dgs.astype(BF16)
        _accumulate(dbias_ref.at[:, :D], step, jnp.sum(dgc, axis=0, keepdims=True))
        _accumulate(dbias_ref.at[:, D:], step, jnp.sum(dgs, axis=0, keepdims=True))
        dysb_ref[...] = _dot_nt(db, wa_v[...]).astype(BF16)
        dyc = _dot_nt(da, wc_v[...])
        cc, cx = cc_ref[...], cx_ref[...]
        xc = cc * cx
        xc_before = jnp.where(step == n_tile - 1, 0.0, ccp_ref[...] * cxp_ref[...])
        ext = jnp.concatenate([xc_before, xc], axis=0)
        x1 = pltpu.roll(ext, 1, 0)[8:]
        x2 = pltpu.roll(ext, 2, 0)[8:]
        w0, w1, w2 = cw_ref[0:1, :], cw_ref[1:2, :], cw_ref[2:3, :]
        dc_ref[0] = (dyc * (w0 * x2 + w1 * x1 + w2 * xc)).astype(BF16)
        dconv = dyc * cb_ref[...]
        dcw_ref[0:1, :] += jnp.sum(dconv * x2, axis=0, keepdims=True)
        dcw_ref[1:2, :] += jnp.sum(dconv * x1, axis=0, keepdims=True)
        dcw_ref[2:3, :] += jnp.sum(dconv * xc, axis=0, keepdims=True)
        after = jnp.concatenate([dconv, head_v[...]], axis=0)
        dxc = w2 * dconv + w1 * pltpu.roll(after, TM + 7, 0)[:TM] + w0 * pltpu.roll(after, TM + 6, 0)[:TM]
        dc_ref[1] = (dxc * cx).astype(BF16)
        dc_ref[2] = (dxc * cc).astype(BF16)
        head_v[...] = dconv[:8]

    return _call(
        body, (dh2, a, b, pcg, pcg, pcg, pcg, pcg, pcg, pcg, b_gate, conv_w, w_mix), grid=(n_tile,), name=name,
        comm=comm,
        in_specs=[rows(D)] * 3 + [cols(3), cols(4), cols(0), cols(1), cols(2), before(1), before(2),
                                  _const_spec((1, 2 * D)), _const_spec((CONV_K, D)), _ANY],
        out_specs=[rows(D)] * 3 + [pl.BlockSpec((2, TM, D), lambda i: (0, n_tile - 1 - i, 0)),
                                   pl.BlockSpec((3, TM, D), lambda i: (0, n_tile - 1 - i, 0)), rows(D),
                                   _const_spec((1, 2 * D)), _const_spec((8, D))],
        out_shape=[jax.ShapeDtypeStruct((t, D), BF16)] * 3
                  + [jax.ShapeDtypeStruct((2, t, D), BF16), jax.ShapeDtypeStruct((3, t, D), BF16),
                     jax.ShapeDtypeStruct((t, D), BF16), jax.ShapeDtypeStruct((1, 2 * D), F32),
                     jax.ShapeDtypeStruct((8, D), F32)],
        scratch_shapes=[pltpu.VMEM((D, D), BF16)] * 3 + [pltpu.VMEM((8, D), F32),
                                                         pltpu.SemaphoreType.DMA((3 * N_DEV,))])


def _inproj_bwd(dconv, dq, dkv, dgp, w_in, h, g, dh_res, name, comm=None):
    t = h.shape[0]

    def body(dc_ref, dq_ref, dkv_ref, dgp_ref, w_hbm, h_ref, g_ref, dres_ref, dh_ref, dg_ref, w_v, sems):
        step = pl.program_id(0)
        _load_resident(step, [(w_hbm, w_v)], sems)
        du = _dot_nt(dq_ref[...], w_v[3])
        for k in range(3):
            du = du + _dot_nt(dc_ref[k], w_v[k])
        for k in range(2):
            du = du + _dot_nt(dkv_ref[k].astype(BF16), w_v[4 + k]) + _dot_nt(dgp_ref[k], w_v[6 + k])
        dx, dg = _rms_bwd_tile(h_ref[...], g_ref[...], du)
        dh_ref[...] = dres_ref[...] + dx
        _accumulate(dg_ref, step, dg)

    return _call(
        body, (dconv, dq, dkv, dgp, w_in, h, g, dh_res), grid=(t // TM,), name=name, comm=comm,
        in_specs=[_blk_row_spec(3, TM, D), _row_spec(TM, D), _blk_row_spec(2, TM, D), _blk_row_spec(2, TM, D), _ANY,
                  _row_spec(TM, D), _const_spec((1, D)), _row_spec(TM, D)],
        out_specs=[_row_spec(TM, D), _const_spec((1, D))],
        out_shape=[jax.ShapeDtypeStruct((t, D), F32), jax.ShapeDtypeStruct((1, D), F32)],
        scratch_shapes=[pltpu.VMEM((N_DEV, D, D), BF16), pltpu.SemaphoreType.DMA((1,))])


def _softmax_rows(s):
    e = jnp.exp(s - jnp.max(s, axis=-1, keepdims=True))
    return e / jnp.sum(e, axis=-1, keepdims=True)


def _cross_pairs(cross_hbm, wq_v, wo_v):
    return _square_pairs(cross_hbm, 0, wq_v) + _square_pairs(cross_hbm, 1, wo_v)


def _cross_fwd(h, g, mem, g_mem, w_ckv, w_cross, name):
    t = h.shape[0]
    m = mem.shape[0]
    scale = X_DH ** -0.5

    def body(h_ref, g_ref, mem_ref, gm_ref, wkv_ref, cross_hbm, hn_ref, qx_ref, o_ref, h3_ref, mn_ref, kv_ref,
             wq_v, wo_v, sems):
        _load_resident(pl.program_id(0), _cross_pairs(cross_hbm, wq_v, wo_v), sems)

        @pl.when(pl.program_id(0) == 0)
        def _():
            mn = _rms_fwd_tile(mem_ref[...], gm_ref[...]).astype(BF16)
            mn_ref[...] = mn
            for j in range(N_DEV):
                kv_ref[j] = _dot(mn, wkv_ref[j]).astype(BF16)

        ht = h_ref[...]
        hn = _rms_fwd_tile(ht, g_ref[...]).astype(BF16)
        hn_ref[...] = hn
        qx = _dot(hn, wq_v[...]).astype(BF16)
        qx_ref[...] = qx
        for hd in range(X_H):
            lo, hi = hd * X_DH, (hd + 1) * X_DH
            p = _softmax_rows(_dot_nt(qx[:, lo:hi], kv_ref[hd]) * scale)
            o_ref[:, lo:hi] = _dot(p.astype(BF16), kv_ref[X_H + hd]).astype(BF16)
        h3_ref[...] = ht + _dot(o_ref[...], wo_v[...])

    return pl.pallas_call(
        body, grid=(t // TM,), name=name,
        in_specs=[_row_spec(TM, D), _const_spec((1, D)), _const_spec((m, D)), _const_spec((1, D)),
                  _const_spec((N_DEV, D, X_DH)), _ANY],
        out_specs=[_row_spec(TM, D)] * 4 + [_const_spec((m, D)), _const_spec((N_DEV, m, X_DH))],
        out_shape=[jax.ShapeDtypeStruct((t, D), BF16)] * 3 + [jax.ShapeDtypeStruct((t, D), F32),
                                                              jax.ShapeDtypeStruct((m, D), BF16),
                                                              jax.ShapeDtypeStruct((N_DEV, m, X_DH), BF16)],
        scratch_shapes=[pltpu.VMEM((D, D), BF16)] * 2 + [pltpu.SemaphoreType.DMA((2 * N_DEV,))],
        compiler_params=_cparams(),
    )(h, g, mem, g_mem, w_ckv, w_cross)


def _cross_bwd(dh3, h, g, qx, kv, mem, g_mem, w_ckv, w_cross, name, comm=None):
    t = h.shape[0]
    m = kv.shape[1]
    scale = X_DH ** -0.5

    def body(dh_ref, h_ref, g_ref, qx_ref, kv_ref, mem_ref, gm_ref, wkv_ref, cross_hbm,
             dhb_ref, dqx_ref, dkv_ref, dh2_ref, dg_ref, dgm_ref, wq_v, wo_v, sems):
        step = pl.program_id(0)
        _load_resident(step, _cross_pairs(cross_hbm, wq_v, wo_v), sems)

        @pl.when(step == 0)
        def _():
            dkv_ref[...] = jnp.zeros_like(dkv_ref)

        dht = dh_ref[...]
        dhb = dht.astype(BF16)
        dhb_ref[...] = dhb
        do = _dot_nt(dhb, wo_v[...]).astype(BF16)
        for hd in range(X_H):
            lo, hi = hd * X_DH, (hd + 1) * X_DH
            qh = qx_ref[:, lo:hi]
            kh = kv_ref[hd]
            p = _softmax_rows(_dot_nt(qh, kh) * scale)
            doh = do[:, lo:hi]
            dp = _dot_nt(doh, kv_ref[X_H + hd])
            ds = (p * (dp - jnp.sum(dp * p, axis=-1, keepdims=True)) * scale).astype(BF16)
            dqx_ref[:, lo:hi] = _dot(ds, kh).astype(BF16)
            dkv_ref[hd] += _dot_tn(ds, qh)
            dkv_ref[X_H + hd] += _dot_tn(p.astype(BF16), doh)
        dhn = _dot_nt(dqx_ref[...], wq_v[...])
        dx, dg = _rms_bwd_tile(h_ref[...], g_ref[...], dhn)
        dh2_ref[...] = dht + dx
        _accumulate(dg_ref, step, dg)

        @pl.when(step == t // TM - 1)
        def _():
            dmn = jnp.zeros((m, D), F32)
            for j in range(N_DEV):
                dmn = dmn + _dot_nt(dkv_ref[j].astype(BF16), wkv_ref[j])
            dgm_ref[...] = _rms_bwd_tile(mem_ref[...], gm_ref[...], dmn)[1]

    return _call(
        body, (dh3, h, g, qx, kv, mem, g_mem, w_ckv, w_cross), grid=(t // TM,), name=name, comm=comm,
        in_specs=[_row_spec(TM, D), _row_spec(TM, D), _const_spec((1, D)), _row_spec(TM, D),
                  _const_spec((N_DEV, m, X_DH)), _const_spec((m, D)), _const_spec((1, D)),
                  _const_spec((N_DEV, D, X_DH)), _ANY],
        out_specs=[_row_spec(TM, D), _row_spec(TM, D), _const_spec((N_DEV, m, X_DH)), _row_spec(TM, D),
                   _const_spec((1, D)), _const_spec((1, D))],
        out_shape=[jax.ShapeDtypeStruct((t, D), BF16), jax.ShapeDtypeStruct((t, D), BF16),
                   jax.ShapeDtypeStruct((N_DEV, m, X_DH), F32), jax.ShapeDtypeStruct((t, D), F32),
                   jax.ShapeDtypeStruct((1, D), F32), jax.ShapeDtypeStruct((1, D), F32)],
        scratch_shapes=[pltpu.VMEM((D, D), BF16)] * 2 + [pltpu.SemaphoreType.DMA((2 * N_DEV,))])


def _adamw(w, parts, m, v, name, row_block=0, token=None):
    r, c = w.shape
    n = parts.shape[0]
    tr = _pick_tile(r, (256, 352, 128))
    off = row_block * (r // tr)

    def body(*refs):
        if token is None:
            _adamw_update(None, *refs)
        else:
            _adamw_update(refs[4], *refs[:4], *refs[5:])

    spec = _row_spec(tr, c)
    in_specs = [spec, pl.BlockSpec((n, tr, c), lambda i: (0, i + off, 0)), spec, spec]
    operands = (w, parts, m, v)
    if token is not None:
        in_specs.append(_const_spec(token.shape))
        operands += (token,)
    return pl.pallas_call(
        body, grid=(r // tr,), name=name, in_specs=in_specs, out_specs=[spec] * 4,
        out_shape=[jax.ShapeDtypeStruct((r, c), F32)] * 4,
        compiler_params=_cparams(),
    )(*operands)


def _adamw_update(tok_ref, w_ref, p_ref, m_ref, v_ref, g_ref, d_ref, nm_ref, nv_ref):
    gt = p_ref[0].astype(F32)
    for k in range(1, p_ref.shape[0]):
        gt = gt + p_ref[k].astype(F32)
    if tok_ref is not None:
        gt = gt + tok_ref[0:1, 0:1]
    _adamw_apply(gt, w_ref, m_ref, v_ref, g_ref, d_ref, nm_ref, nv_ref)


def _adamw_own(w, land, own, chip, m, v, name, row_block=0, token=None):
    r, c = w.shape
    tr = _pick_tile(r, (256, 352, 128))
    off = row_block * (r // tr)

    def body(chip_ref, w_ref, land_ref, own_ref, m_ref, v_ref, *rest):
        mine = own_ref[0].astype(F32)
        gt = jnp.where(chip_ref[0] == 0, mine, land_ref[0].astype(F32))
        for k in range(1, N_CHIP):
            gt = gt + jnp.where(chip_ref[0] == k, mine, land_ref[k].astype(F32))
        if token is not None:
            gt = gt + rest[0][0:1, 0:1]
        _adamw_apply(gt, w_ref, m_ref, v_ref, *rest[-4:])

    spec = pl.BlockSpec((tr, c), lambda i, chip_ref: (i, 0))
    in_specs = [spec, pl.BlockSpec((N_CHIP, tr, c), lambda i, chip_ref: (0, i + off, 0)),
                pl.BlockSpec((1, tr, c), lambda i, chip_ref: (chip_ref[0], i + off, 0)), spec, spec]
    operands = (chip, w, land, own, m, v)
    if token is not None:
        in_specs.append(pl.BlockSpec(token.shape, lambda i, chip_ref: (0, 0)))
        operands += (token,)
    return pl.pallas_call(
        body, name=name,
        grid_spec=pltpu.PrefetchScalarGridSpec(
            num_scalar_prefetch=1, grid=(r // tr,), in_specs=in_specs, out_specs=[spec] * 4),
        out_shape=[jax.ShapeDtypeStruct((r, c), F32)] * 4,
        compiler_params=_cparams(),
    )(*operands)


def _adamw_apply(gt, w_ref, m_ref, v_ref, g_ref, d_ref, nm_ref, nv_ref):
    g_ref[...] = gt
    nm = ADAM_B1 * m_ref[...] + (1.0 - ADAM_B1) * gt
    nv = ADAM_B2 * v_ref[...] + (1.0 - ADAM_B2) * jnp.square(gt)
    m_hat = nm / (1.0 - ADAM_B1 ** ADAM_STEP)
    v_hat = nv / (1.0 - ADAM_B2 ** ADAM_STEP)
    d_ref[...] = -ADAM_LR * (m_hat / (jnp.sqrt(v_hat) + ADAM_EPS) + ADAM_WD * w_ref[...])
    nm_ref[...] = nm
    nv_ref[...] = nv


def _mesh_pos():
    return lax.axis_index("x"), lax.axis_index("y"), lax.axis_index("c")


def _both(first, second):
    n_in, n_out, n_sem = len(first.inputs), len(first.out_shapes), len(first.sem_shapes)

    def run(round_name):
        def both(in_refs, out_refs, sems):
            getattr(first, round_name)(in_refs[:n_in], out_refs[:n_out], sems[:n_sem])
            getattr(second, round_name)(in_refs[n_in:], out_refs[n_out:], sems[n_sem:])
        return both

    return types.SimpleNamespace(
        inputs=first.inputs + second.inputs, out_shapes=first.out_shapes + second.out_shapes,
        sem_shapes=first.sem_shapes + second.sem_shapes, start=run("start"), middle=run("middle"),
        finish=run("finish"))


def _no_round(in_refs, out_refs, sems):
    pass


def _run_exchange(comm, name):
    c_in, c_out = len(comm.inputs), len(comm.out_shapes)

    def body(*refs):
        cins, couts, sems = refs[:c_in], refs[c_in:c_in + c_out], refs[c_in + c_out:]
        comm.start(cins, couts, sems)
        comm.middle(cins, couts, sems)
        comm.finish(cins, couts, sems)

    return list(pl.pallas_call(
        body, name=name, out_shape=list(comm.out_shapes),
        in_specs=[_ANY] * c_in, out_specs=[_ANY] * c_out, scratch_shapes=list(comm.sem_shapes),
    )(*comm.inputs))


def _gather_exchange(shards):
    n_arr = len(shards)

    def plan(x_refs, out_refs, sems):
        send_sems, recv_sems, local_sems = sems[:3]
        stage = sems[3:]
        x, y, c = _mesh_pos()
        me, sibling = (x, y, c), (x, y, 1 - c)
        xn, yn, diag = (1 - x, y), (x, 1 - y), (1 - x, 1 - y)

        def slot(a, px, py, pc, half=None):
            ref = out_refs[a].at[4 * px + 2 * py + pc]
            if half is None:
                return ref
            rows = shards[a].shape[0] // 2
            return ref.at[half * rows:(half + 1) * rows]

        def copy(a, k, block, to, half=None, src=None):
            dst = slot(a, *block, half)
            return pltpu.make_async_remote_copy(
                src_ref=dst if src is None else src, dst_ref=dst,
                send_sem=send_sems.at[a, k], recv_sem=recv_sems.at[a, k],
                device_id=to, device_id_type=pl.DeviceIdType.MESH)

        return types.SimpleNamespace(
            me=me, sibling=sibling, xn=xn, yn=yn, diag=diag, c=c, copy=copy,
            mine_in=[pltpu.make_async_copy(x_refs[a], stage[a], local_sems.at[a, 0]) for a in range(n_arr)],
            mine_out=[pltpu.make_async_copy(stage[a], slot(a, *me), local_sems.at[a, 1]) for a in range(n_arr)],
            first=[cp for a in range(n_arr) for cp in (
                copy(a, 0, me, sibling, src=x_refs[a]), copy(a, 1, me, (*xn, c), src=x_refs[a]),
                copy(a, 2, me, (*yn, c), src=x_refs[a]))],
            second=lambda a: (copy(a, 3, (*xn, c), (*yn, c), half=0), copy(a, 5, (*xn, c), sibling),
                              copy(a, 4, (*yn, c), (*xn, c), half=1), copy(a, 6, (*yn, c), sibling)),
            third=lambda a: (copy(a, 7, (*diag, c), sibling, half=0), copy(a, 8, (*diag, c), sibling, half=1)))

    def start(x_refs, out_refs, sems):
        p = plan(x_refs, out_refs, sems)
        for cp in p.first + p.mine_in:
            cp.start()
        for cp_in, cp_out in zip(p.mine_in, p.mine_out):
            cp_in.wait()
            cp_out.start()

    def middle(x_refs, out_refs, sems):
        p = plan(x_refs, out_refs, sems)
        for a in range(n_arr):
            to_yn, x_to_sib, to_xn, y_to_sib = p.second(a)
            p.copy(a, 1, (*p.xn, p.c), p.me).wait_recv()
            to_yn.start()
            x_to_sib.start()
            p.copy(a, 2, (*p.yn, p.c), p.me).wait_recv()
            to_xn.start()
            y_to_sib.start()

    def finish(x_refs, out_refs, sems):
        p = plan(x_refs, out_refs, sems)
        for a in range(n_arr):
            half0_to_sib, half1_to_sib = p.third(a)
            p.copy(a, 3, (*p.diag, p.c), p.me, half=0).wait_recv()
            half0_to_sib.start()
            p.copy(a, 4, (*p.diag, p.c), p.me, half=1).wait_recv()
            half1_to_sib.start()
        other = 1 - p.c
        for a in range(n_arr):
            p.copy(a, 0, p.sibling, p.me).wait_recv()
            p.copy(a, 5, (*p.xn, other), p.me).wait_recv()
            p.copy(a, 6, (*p.yn, other), p.me).wait_recv()
            p.copy(a, 7, (*p.diag, other), p.me, half=0).wait_recv()
            p.copy(a, 8, (*p.diag, other), p.me, half=1).wait_recv()
        for cp in p.first:
            cp.wait_send()
        for a in range(n_arr):
            for cp in p.second(a) + p.third(a):
                cp.wait_send()
        for cp in p.mine_out:
            cp.wait()

    return types.SimpleNamespace(
        inputs=list(shards), start=start, middle=middle, finish=finish,
        out_shapes=[jax.ShapeDtypeStruct((N_DEV,) + s.shape, s.dtype) for s in shards],
        sem_shapes=[pltpu.SemaphoreType.DMA((n_arr, 9)), pltpu.SemaphoreType.DMA((n_arr, 9)),
                    pltpu.SemaphoreType.DMA((n_arr, 2))] + [pltpu.VMEM(s.shape, s.dtype) for s in shards])


def _pair_exchange(grads):
    n_arr = len(grads)

    def plan(g_refs, land_refs, sems):
        send_sems, recv_sems = sems
        x, y, c = _mesh_pos()
        return [pltpu.make_async_remote_copy(
            src_ref=g_refs[a].at[2 * k + 1 - c], dst_ref=land_refs[a].at[k],
            send_sem=send_sems.at[a, k], recv_sem=recv_sems.at[a, k],
            device_id=(x, y, 1 - c), device_id_type=pl.DeviceIdType.MESH)
            for a in range(n_arr) for k in range(N_CHIP)]

    def start(g_refs, land_refs, sems):
        for cp in plan(g_refs, land_refs, sems):
            cp.start()

    def finish(g_refs, land_refs, sems):
        for cp in plan(g_refs, land_refs, sems):
            cp.wait()

    return types.SimpleNamespace(
        inputs=list(grads), start=start, middle=_no_round, finish=finish,
        out_shapes=[jax.ShapeDtypeStruct((N_CHIP,) + g.shape[1:], g.dtype) for g in grads],
        sem_shapes=[pltpu.SemaphoreType.DMA((n_arr, N_CHIP)), pltpu.SemaphoreType.DMA((n_arr, N_CHIP))])


def _chip_exchange(parts):
    n_arr = len(parts)

    def plan(p_refs, land_refs, sems):
        send_sems, recv_sems, local_sems = sems
        x, y, c = _mesh_pos()
        my_chip = 2 * x + y
        chips = [(1 - x, y), (x, 1 - y), (1 - x, 1 - y)]
        local = [pltpu.make_async_copy(p_refs[a].at[my_chip], land_refs[a].at[my_chip], local_sems.at[a])
                 for a in range(n_arr)]

        def copy(a, k, src_slot, dst_slot, px, py):
            return pltpu.make_async_remote_copy(
                src_ref=p_refs[a].at[src_slot], dst_ref=land_refs[a].at[dst_slot],
                send_sem=send_sems.at[a, k], recv_sem=recv_sems.at[a, k],
                device_id=(px, py, c), device_id_type=pl.DeviceIdType.MESH)

        sends = [copy(a, k, 2 * px + py, my_chip, px, py) for a in range(n_arr) for k, (px, py) in enumerate(chips)]
        arrivals = [copy(a, k, my_chip, 2 * px + py, px, py) for a in range(n_arr)
                    for k, (px, py) in enumerate(chips)]
        return local, sends, arrivals

    def start(p_refs, land_refs, sems):
        local, sends, _ = plan(p_refs, land_refs, sems)
        for cp in local + sends:
            cp.start()

    def finish(p_refs, land_refs, sems):
        local, sends, arrivals = plan(p_refs, land_refs, sems)
        for cp in arrivals:
            cp.wait_recv()
        for cp in sends:
            cp.wait_send()
        for cp in local:
            cp.wait()

    return types.SimpleNamespace(
        inputs=list(parts), start=start, middle=_no_round, finish=finish,
        out_shapes=[jax.ShapeDtypeStruct(p.shape, p.dtype) for p in parts],
        sem_shapes=[pltpu.SemaphoreType.DMA((n_arr, 3)), pltpu.SemaphoreType.DMA((n_arr, 3)),
                    pltpu.SemaphoreType.DMA((n_arr,))])


_HBM = pl.BlockSpec(memory_space=pltpu.HBM)
_SEM = pl.BlockSpec(memory_space=pltpu.SEMAPHORE)
_DATAFLOW = pltpu.SideEffectType.DATAFLOW_SIDE_EFFECTING


def _chip_copies(p_refs, land_refs, send_sems, recv_sems):
    x, y, c = _mesh_pos()
    my_chip = 2 * x + y
    chips = [(1 - x, y), (x, 1 - y), (1 - x, 1 - y)]
    return [pltpu.make_async_remote_copy(
        src_ref=p_refs[a].at[2 * px + py], dst_ref=land_refs[a].at[my_chip],
        send_sem=send_sems[3 * a + k], recv_sem=recv_sems[3 * a + k],
        device_id=(px, py, c), device_id_type=pl.DeviceIdType.MESH)
        for a in range(len(p_refs)) for k, (px, py) in enumerate(chips)]


def _chip_exchange_begin(parts, name):
    n_arr = len(parts)
    n_buf, n_copy = 2 * n_arr, 3 * n_arr
    lands = [lax.empty(p.shape, p.dtype) for p in parts]

    def body(*refs):
        p_refs, land_refs = refs[:n_arr], refs[n_arr:n_buf]
        send_sems, recv_sems, token = refs[n_buf:n_buf + n_copy], refs[n_buf + n_copy:n_buf + 2 * n_copy], refs[-1]
        for cp in _chip_copies(p_refs, land_refs, send_sems, recv_sems):
            cp.start()
        token[...] = jnp.zeros_like(token)

    bufs = list(parts) + list(lands)
    outs = pl.pallas_call(
        body, name=name,
        out_shape=(*[pltpu.SemaphoreType.DMA(())] * (2 * n_copy), *[pltpu.HBM(b.shape, b.dtype) for b in bufs],
                   jax.ShapeDtypeStruct((8, 128), F32)),
        in_specs=[_HBM] * n_buf,
        out_specs=(*[_SEM] * (2 * n_copy), *[_HBM] * n_buf, pl.BlockSpec(memory_space=pltpu.VMEM)),
        input_output_aliases={i: 2 * n_copy + i for i in range(n_buf)},
        compiler_params=pltpu.CompilerParams(has_side_effects=_DATAFLOW),
    )(*[pltpu.with_memory_space_constraint(b, pltpu.HBM) for b in bufs])
    sems = list(outs[:2 * n_copy])
    thru = list(outs[2 * n_copy:2 * n_copy + n_buf])
    return types.SimpleNamespace(send_sems=sems[:n_copy], recv_sems=sems[n_copy:], parts=thru[:n_arr],
                                 lands=thru[n_arr:], token=outs[-1])


def _chip_exchange_end(flight, after, name):
    send_sems, recv_sems, parts, lands = flight.send_sems, flight.recv_sems, flight.parts, flight.lands
    n_arr = len(parts)
    n_buf, n_copy = 2 * n_arr, 3 * n_arr

    def body(*refs):
        p_refs, land_refs = refs[:n_arr], refs[n_arr:n_buf]
        sems = refs[n_buf:n_buf + 2 * n_copy]
        for cp in _chip_copies(p_refs, land_refs, sems[:n_copy], sems[n_copy:]):
            cp.wait_send()
            cp.wait_recv()

    bufs = list(parts) + list(lands)
    outs = pl.pallas_call(
        body, name=name, out_shape=tuple(pltpu.HBM(b.shape, b.dtype) for b in bufs),
        in_specs=[_HBM] * n_buf + [_SEM] * (2 * n_copy) + [_ANY], out_specs=tuple([_HBM] * n_buf),
        input_output_aliases={i: i for i in range(n_buf)},
        compiler_params=pltpu.CompilerParams(has_side_effects=_DATAFLOW),
    )(*bufs, *send_sems, *recv_sems, after)
    return list(outs[:n_arr]), list(outs[n_arr:])


def _row_tile(r, cap=640):
    best = None
    for cand in range(16, min(r, cap) + 1, 16):
        if r % cand == 0:
            best = cand
    return best if best is not None else r


def _pair_sum(gs, landeds, core, name):
    tiles = [_row_tile(g.shape[1]) for g in gs]
    counts = [g.shape[1] // tr for g, tr in zip(gs, tiles)]
    n_arr = len(gs)

    def body(core_ref, *refs):
        for a in range(n_arr):
            mine, theirs, out = refs[2 * a], refs[2 * a + 1], refs[2 * n_arr + a]
            out[0] = (mine[0].astype(F32) + theirs[0].astype(F32)).astype(out.dtype)

    in_specs, out_specs, operands = [], [], []
    for g, landed, tr, count in zip(gs, landeds, tiles, counts):
        c_dim = g.shape[2]
        last = count - 1
        in_specs += [pl.BlockSpec((1, tr, c_dim),
                                  lambda k, i, core_ref, last=last: (2 * k + core_ref[0], jnp.minimum(i, last), 0)),
                     pl.BlockSpec((1, tr, c_dim), lambda k, i, core_ref, last=last: (k, jnp.minimum(i, last), 0))]
        out_specs.append(pl.BlockSpec((1, tr, c_dim), lambda k, i, core_ref, last=last: (k, jnp.minimum(i, last), 0)))
        operands += [g, landed]
    return list(pl.pallas_call(
        body, name=name,
        grid_spec=pltpu.PrefetchScalarGridSpec(
            num_scalar_prefetch=1, grid=(N_CHIP, max(counts)), in_specs=in_specs, out_specs=out_specs),
        out_shape=[jax.ShapeDtypeStruct((N_CHIP,) + g.shape[1:], g.dtype) for g in gs],
        compiler_params=_cparams(2),
    )(core, *operands))


def _sum_slots(parts, name):
    n, r, c_dim = parts.shape
    tr = _row_tile(r)

    def body(p_ref, o_ref):
        acc = p_ref[0].astype(F32)
        for k in range(1, n):
            acc = acc + p_ref[k].astype(F32)
        o_ref[...] = acc

    return pl.pallas_call(
        body, grid=(r // tr,), name=name,
        in_specs=[pl.BlockSpec((n, tr, c_dim), lambda i: (0, i, 0))],
        out_specs=_row_spec(tr, c_dim),
        out_shape=jax.ShapeDtypeStruct((r, c_dim), F32),
        compiler_params=_cparams(),
    )(parts)


GAINS = ("g_ffn1", "g_mix", "g_cross", "g_mem", "g_ffn2", "g_final")
SMALL = GAINS + ("b_gate", "conv_w")
SMALL_R = 16
LOSS_ROW = 11
WEIGHT_ORDER = ("g_ffn1", "w_ffn1_gu", "w_ffn1_down", "g_mix", "w_in", "b_gate", "conv_w", "w_conv_out",
                "w_attn_out", "w_o", "g_cross", "g_mem", "w_cq", "w_ckv", "w_co", "g_ffn2", "w_ffn2_gu",
                "w_ffn2_down", "g_final")
GU_NAMES = ("w_ffn1_gu", "w_ffn2_gu")


def _pack_small(vals, conv_rows):
    rows = [vals[n].reshape(1, D) for n in GAINS] + [vals["b_gate"].reshape(2, D), conv_rows.reshape(CONV_K, D)]
    used = len(GAINS) + 2 + CONV_K
    return jnp.concatenate(rows + [jnp.zeros((SMALL_R - used, D), F32)], axis=0)


def _unpack_small(buf):
    out = {n: buf[k] for k, n in enumerate(GAINS)}
    out["b_gate"] = buf[6:8].reshape(2 * D)
    out["conv_w"] = buf[8:8 + CONV_K]
    return out


def _exchange_shards(wts):
    out = {n: jnp.pad(wts[n].T.astype(BF16), ((0, FF_PAD - FF_BLK), (0, 0))) for n in GU_NAMES}
    for n in ("w_ckv", "w_in", "w_ffn1_down", "w_ffn2_down"):
        out[n] = wts[n].astype(BF16)
    out["mix"] = jnp.concatenate([wts[n].astype(BF16) for n in MIX_MATS], axis=0)
    out["cross"] = jnp.concatenate([wts[n].astype(BF16) for n in CROSS_MATS], axis=0)
    return out


def _reduce_group(grads, landed, core, names):
    return _pair_sum(grads, landed, core, "grads_pair_sum_" + "_".join(names))


def _step(x, mem, target, sh, conv_pad, gains, b_gate, core):
    wg1, wd1, conv_all = _run_exchange(_gather_exchange([sh["w_ffn1_gu"], sh["w_ffn1_down"], conv_pad]), "gather_ffn1")
    conv_w = conv_all[:, :CONV_K, :].transpose(1, 0, 2).reshape(CONV_K, D)
    (n1, gate1, up1, act1, h1), (w_in,) = _ffn_fwd(
        x, gains["g_ffn1"], wg1, wd1, "ffn1_fwd", comm=_gather_exchange([sh["w_in"]]))
    (u, pcg, qkv, yc), (w_mix,) = _inproj_fwd(h1, gains["g_mix"], w_in, conv_w, "inproj_fwd",
                                              comm=_gather_exchange([sh["mix"]]))
    (ysb, ctot), (w_cross, w_ckv, wg2) = _sb_fwd(
        qkv, "sb_fwd", comm=_gather_exchange([sh["cross"], sh["w_ckv"], sh["w_ffn2_gu"]]))
    (a_mix, b_mix, merged, h2), (wd2,) = _mix_out_fwd(yc, ysb, pcg, b_gate, h1, w_mix, "mix_out_fwd",
                                                      comm=_gather_exchange([sh["w_ffn2_down"]]))
    hn, qx, o_x, h3, mn, kv = _cross_fwd(h2, gains["g_cross"], mem, gains["g_mem"], w_ckv, w_cross, "cross_fwd")
    (n4, gate2, up2, act2, dh4, loss, dg_final), _ = _ffn_fwd(h3, gains["g_ffn2"], wg2, wd2, "ffn2_fwd",
                                                              head=(gains["g_final"], target))

    gs = {"g_final": dg_final}
    (dgu2, dh4b, dh3, gs["g_ffn2"]), _ = _ffn_bwd(dh4, h3, gains["g_ffn2"], gate2, up2, wg2, wd2, "ffn2_bwd")
    grads_a = [_mm_tn_rows(dgu2, n4, FF_PAD, "dw_ffn2_gu"),
               _mm_tn_rows(act2, dh4b, FF_BLK, "dw_ffn2_down").reshape(N_DEV, DOWN_ROWS, D)]
    names_a = ["w_ffn2_gu", "w_ffn2_down"]
    (dh3b, dqx, dkv, dh2, gs["g_cross"], gs["g_mem"]), landed_a = _cross_bwd(
        dh3, h2, gains["g_cross"], qx, kv, mem, gains["g_mem"], w_ckv, w_cross, "cross_bwd",
        comm=_pair_exchange(grads_a))
    sums_a = _reduce_group(grads_a, landed_a, core, names_a)
    cross_stack = _mm_tn_square(hn, dqx, "dw_cq", 0, len(CROSS_MATS))
    grads_b = [_mm_tn_cols(mn, dkv, "dw_ckv"), _mm_tn_square(o_x, dh3b, "dw_co", 1, len(CROSS_MATS), cross_stack)]
    names_b = ["w_ckv", "cross"]
    (dh2b, da_mix, db_mix, dgp, dconv, dysb, gs["b_gate"], gs["conv_w"]), landed_b = _mix_out_bwd(
        dh2, a_mix, b_mix, pcg, b_gate, conv_w, w_mix, "mix_out_bwd", comm=_pair_exchange(grads_b))
    sums_b = _reduce_group(grads_b, landed_b, core, names_b)
    mix_stack = _mm_tn_square(yc, da_mix, "dw_conv_out", 0, len(MIX_MATS))
    mix_stack = _mm_tn_square(ysb, db_mix, "dw_attn_out", 1, len(MIX_MATS), mix_stack)
    grads_c = [_mm_tn_square(merged, dh2b, "dw_o", 2, len(MIX_MATS), mix_stack)]
    flight_ab = _chip_exchange_begin(sums_a + sums_b, "grads_to_chips_early_begin")
    (dq, dkv_sb), landed_c = _sb_bwd(qkv, dysb, ctot, flight_ab.token, "sb_bwd", comm=_pair_exchange(grads_c))
    sums_c = _reduce_group(grads_c, landed_c, core, ["mix"])
    w_in_stack = _mm_tn_cols(u, dconv, "dw_in_conv", 0, N_DEV)
    w_in_stack = _mm_tn_cols(u, dq[None], "dw_in_q", 3, N_DEV, w_in_stack)
    w_in_stack = _mm_tn_cols(u, dkv_sb, "dw_in_kv", 4, N_DEV, w_in_stack)
    grads_d = [_mm_tn_cols(u, dgp, "dw_in_gates", 6, N_DEV, w_in_stack)]
    (dh1, gs["g_mix"]), landed_d = _inproj_bwd(dconv, dq, dkv_sb, dgp, w_in, h1, gains["g_mix"], dh2, "inproj_bwd",
                                               comm=_pair_exchange(grads_d))
    sums_d = _reduce_group(grads_d, landed_d, core, ["w_in"])
    flight_d = _chip_exchange_begin(sums_c + sums_d, "grads_to_chips_w_in_begin")
    (dgu1, dh1b, dx, gs["g_ffn1"]), _ = _ffn_bwd(dh1, x, gains["g_ffn1"] + flight_d.token[0, 0], gate1, up1, wg1, wd1,
                                                 "ffn1_bwd")
    dw_gu1 = _mm_tn_rows(dgu1, n1, FF_PAD, "dw_ffn1_gu")
    small_mine = _pack_small({n: gs[n] for n in GAINS + ("b_gate",)}, gs["conv_w"][:CONV_K])
    small_mine = small_mine.at[LOSS_ROW, 0].set(loss[0, 0])
    dw_down1, (landed_gu1, small_all) = _mm_tn_rows(
        act1, dh1b, FF_BLK, "dw_ffn1_down", comm=_both(_pair_exchange([dw_gu1]), _gather_exchange([small_mine])))
    grads_e = [dw_gu1, dw_down1.reshape(N_DEV, DOWN_ROWS, D)]
    names_e = ["w_ffn1_gu", "w_ffn1_down"]
    landed_e = [landed_gu1] + _run_exchange(_pair_exchange(grads_e[1:]), "grads_to_sibling_ffn1_down")
    flight_e = _chip_exchange_begin(_reduce_group(grads_e, landed_e, core, names_e), "grads_to_chips_ffn1_begin")
    flights = [(names_a + names_b, flight_ab), (["mix", "w_in"], flight_d), (names_e, flight_e)]
    return dx, flights, small_all


def kernel(x, mem, g_ffn1, w_ffn1_gu, w_ffn1_down, g_mix, w_in, b_gate, conv_w, w_conv_out, w_attn_out, w_o, g_cross, g_mem, w_cq, w_ckv, w_co, g_ffn2, w_ffn2_gu, w_ffn2_down, g_final, loss_target, m_g_ffn1, m_w_ffn1_gu, m_w_ffn1_down, m_g_mix, m_w_in, m_b_gate, m_conv_w, m_w_conv_out, m_w_attn_out, m_w_o, m_g_cross, m_g_mem, m_w_cq, m_w_ckv, m_w_co, m_g_ffn2, m_w_ffn2_gu, m_w_ffn2_down, m_g_final, v_g_ffn1, v_w_ffn1_gu, v_w_ffn1_down, v_g_mix, v_w_in, v_b_gate, v_conv_w, v_w_conv_out, v_w_attn_out, v_w_o, v_g_cross, v_g_mem, v_w_cq, v_w_ckv, v_w_co, v_g_ffn2, v_w_ffn2_gu, v_w_ffn2_down, v_g_final):
    args = locals()
    wts = {n: args[n] for n in WEIGHT_ORDER}
    mom1 = {n: args["m_" + n] for n in WEIGHT_ORDER}
    mom2 = {n: args["v_" + n] for n in WEIGHT_ORDER}
    cx, cy, cc = _mesh_pos()
    dev = 4 * cx + 2 * cy + cc
    conv_cols = D // N_DEV

    conv_pad = jnp.concatenate([conv_w, jnp.zeros((SMALL_R - CONV_K, conv_cols), F32)], axis=0)
    gains = {n: wts[n].reshape(1, D) for n in GAINS}
    dx, flights, small_all = _step(x[0], mem[0], loss_target[0], _exchange_shards(wts), conv_pad, gains,
                                 b_gate.reshape(1, 2 * D), cc.reshape(1).astype(jnp.int32))

    grads, delta, new_m, new_v = {}, {}, {}, {}

    def operands(n, transposed):
        trio = (wts[n], mom1[n], mom2[n])
        return tuple(a.T for a in trio) if transposed else trio

    def record(n, res, transposed):
        grads[n], delta[n], new_m[n], new_v[n] = [r.T for r in res] if transposed else res

    early = [("w_ffn2_gu", "w_ffn2_gu", 0, True), ("w_ffn2_down", "w_ffn2_down", 0, False),
             ("w_ckv", "w_ckv", 0, False), ("w_in", "w_in", 0, False)]
    early += [(n, "mix", k, False) for k, n in enumerate(MIX_MATS)]
    early += [(n, "cross", k, False) for k, n in enumerate(CROSS_MATS)]
    chip = (2 * cx + cy).reshape(1).astype(jnp.int32)
    (names_early, flight_early), (names_w_in, flight_w_in), (last_names, flight_last) = flights
    token = flight_last.token
    own, land = {}, {}
    for names, flight, tag in ((names_early, flight_early, "early"), (names_w_in, flight_w_in, "w_in")):
        own_parts, landed = _chip_exchange_end(flight, token, "grads_to_chips_%s_end" % tag)
        own.update(zip(names, own_parts))
        land.update(zip(names, landed))
    for n, buf, row_block, transposed in early:
        w, m1, m2 = operands(n, transposed)
        record(n, _adamw_own(w, land[buf], own[buf], chip, m1, m2, "adamw_" + n, row_block, token), transposed)

    after = jnp.concatenate([new_v[n][:1, :1] for n, _, _, _ in early], axis=0)
    own_parts, landed = _chip_exchange_end(flight_last, after, "grads_to_chips_ffn1_end")
    for n, own_n, land_n, transposed in zip(last_names, own_parts, landed, (True, False)):
        w, m1, m2 = operands(n, transposed)
        record(n, _adamw_own(w, land_n, own_n, chip, m1, m2, "adamw_" + n), transposed)

    small_sum = _sum_slots(small_all, "small_grads_sum")
    loss = small_sum[LOSS_ROW, 0]
    grad_small = _unpack_small(small_sum)
    grad_small["conv_w"] = lax.dynamic_slice_in_dim(grad_small["conv_w"], dev * conv_cols, conv_cols, axis=1)
    grads.update(grad_small)

    def small_buf(vals):
        return _pack_small(vals, jnp.concatenate([vals["conv_w"], jnp.zeros((CONV_K, D - conv_cols), F32)], axis=1))

    _, d_s, m_s, v_s = _adamw(small_buf(wts), small_buf(grads)[None], small_buf(mom1), small_buf(mom2), "adamw_small")
    for res, buf in ((delta, d_s), (new_m, m_s), (new_v, v_s)):
        un = _unpack_small(buf)
        for n in GAINS + ("b_gate",):
            res[n] = un[n]
        res["conv_w"] = un["conv_w"][:, :conv_cols]

    return (loss, dx[None], *[grads[n] for n in WEIGHT_ORDER], *[delta[n] for n in WEIGHT_ORDER],
            *[new_m[n] for n in WEIGHT_ORDER], *[new_v[n] for n in WEIGHT_ORDER])
```

```python
import types

import jax
import jax.numpy as jnp
from jax import lax
from jax.experimental import pallas as pl
from jax.experimental.pallas import tpu as pltpu

F32 = jnp.float32
BF16 = jnp.bfloat16

D = 1024
DFF = 2816
SB_H = 8
SB_DH = 128
X_H = 4
X_DH = 256
CONV_K = 3
RMS_EPS = 1e-6
N_DEV = 8
N_CHIP = 4
SQ_ROWS = D // N_DEV

ADAM_LR = 0.001
ADAM_B1 = 0.9
ADAM_B2 = 0.999
ADAM_EPS = 1e-08
ADAM_WD = 0.01
ADAM_STEP = 10

TM = 256
TQ = 512
TK = 256
SB_HPS = 2
VMEM_LIMIT = 56 << 20

FF_BLK = DFF // 4
FF_PAD = 768
FF_SUB = 256
DOWN_ROWS = DFF // N_DEV

MIX_MATS = ("w_conv_out", "w_attn_out", "w_o")
CROSS_MATS = ("w_cq", "w_co")

_ANY = pl.BlockSpec(memory_space=pl.ANY)


def _cparams(n_axes=1):
    return pltpu.CompilerParams(
        dimension_semantics=("arbitrary",) * n_axes, vmem_limit_bytes=VMEM_LIMIT)


def _row_spec(tm, n):
    return pl.BlockSpec((tm, n), lambda i: (i, 0))


def _blk_row_spec(nb, tm, n):
    return pl.BlockSpec((nb, tm, n), lambda i: (0, i, 0))


def _const_spec(shape):
    zeros = (0,) * len(shape)
    return pl.BlockSpec(shape, lambda i: zeros)


def _dot(a, b):
    return jnp.dot(a, b, preferred_element_type=F32)


def _dot_nt(a, b):
    return lax.dot_general(a, b, (((1,), (1,)), ((), ())), preferred_element_type=F32)


def _dot_tn(a, b):
    return lax.dot_general(a, b, (((0,), (0,)), ((), ())), preferred_element_type=F32)


def _sigmoid(x):
    return 1.0 / (1.0 + jnp.exp(-x))


def _call(body, operands, *, grid, in_specs, out_specs, out_shape, scratch_shapes, name, comm=None):
    n_in, n_out, n_sc = len(in_specs), len(out_specs), len(scratch_shapes)
    if comm is None:
        outs = pl.pallas_call(
            body, grid=grid, name=name, in_specs=in_specs, out_specs=out_specs, out_shape=out_shape,
            scratch_shapes=scratch_shapes, compiler_params=_cparams(len(grid)))(*operands)
        return list(outs), []
    c_in, c_out, c_sem = len(comm.inputs), len(comm.out_shapes), len(comm.sem_shapes)

    def hosted(*refs):
        bounds = [0, n_in, c_in, n_out, c_out, n_sc, c_sem]
        parts, pos = [], 0
        for k in bounds[1:]:
            parts.append(refs[pos:pos + k])
            pos += k
        ins, cins, outs, couts, scr, sems = parts
        step, n_steps = pl.program_id(0), grid[0]
        for ax in range(1, len(grid)):
            step, n_steps = step * grid[ax] + pl.program_id(ax), n_steps * grid[ax]

        @pl.when(step == 0)
        def _():
            comm.start(cins, couts, sems)

        @pl.when(step == (2 * n_steps) // 3)
        def _():
            comm.middle(cins, couts, sems)

        body(*ins, *outs, *scr)

        @pl.when(step == n_steps - 1)
        def _():
            comm.finish(cins, couts, sems)

    res = pl.pallas_call(
        hosted, grid=grid, name=name, in_specs=list(in_specs) + [_ANY] * c_in,
        out_specs=list(out_specs) + [_ANY] * c_out, out_shape=list(out_shape) + list(comm.out_shapes),
        scratch_shapes=list(scratch_shapes) + list(comm.sem_shapes),
        compiler_params=_cparams(len(grid)))(*operands, *comm.inputs)
    return list(res[:n_out]), list(res[n_out:])


def _load_resident(step, pairs, sems):
    @pl.when(step == 0)
    def _():
        copies = [pltpu.make_async_copy(src, dst, sems.at[k]) for k, (src, dst) in enumerate(pairs)]
        for cp in copies:
            cp.start()
        for cp in copies:
            cp.wait()


def _square_pairs(buf_hbm, index, dst):
    off = index * SQ_ROWS
    return [(buf_hbm.at[d, off:off + SQ_ROWS, :], dst.at[d * SQ_ROWS:(d + 1) * SQ_ROWS, :]) for d in range(N_DEV)]


def _down_pairs(wd_hbm, dst):
    return [(wd_hbm.at[d], dst.at[d // 2, (d % 2) * DOWN_ROWS:(d % 2 + 1) * DOWN_ROWS, :]) for d in range(N_DEV)]


def _zero_down_pad(step, dst):
    @pl.when(step == 0)
    def _():
        dst[:, FF_BLK:, :] = jnp.zeros((4, FF_PAD - FF_BLK, D), BF16)


def _rms_fwd_tile(xt, g):
    r = lax.rsqrt(jnp.mean(xt * xt, axis=-1, keepdims=True) + RMS_EPS)
    return (xt * r) * g


def _rms_bwd_tile(xt, g, dn):
    r = lax.rsqrt(jnp.mean(xt * xt, axis=-1, keepdims=True) + RMS_EPS)
    xhat = xt * r
    dxhat = dn * g
    dx = r * (dxhat - xhat * jnp.mean(dxhat * xhat, axis=-1, keepdims=True))
    dg = jnp.sum(dn * xhat, axis=0, keepdims=True)
    return dx, dg


def _accumulate(ref, step, value):
    @pl.when(step == 0)
    def _():
        ref[...] = value

    @pl.when(step != 0)
    def _():
        ref[...] = ref[...] + value


def _ffn_fwd(x, g, wgu, wd, name, comm=None, head=None):
    t = x.shape[0]

    def body(x_ref, g_ref, wgu_hbm, wd_hbm, *refs):
        if head is None:
            n_ref, gate_ref, up_ref, act_ref, h_ref, wgu_v, wd_v, sems = refs
        else:
            gf_ref, t_ref, n_ref, gate_ref, up_ref, act_ref, dh_ref, loss_ref, dgf_ref, wgu_v, wd_v, sems = refs
        step = pl.program_id(0)
        _zero_down_pad(step, wd_v)
        _load_resident(step, [(wgu_hbm, wgu_v)] + _down_pairs(wd_hbm, wd_v), sems)
        xt = x_ref[...]
        n = _rms_fwd_tile(xt, g_ref[...]).astype(BF16)
        n_ref[...] = n
        acc = jnp.zeros((TM, D), F32)
        for j in range(4):
            for s in range(FF_PAD // FF_SUB):
                lo, hi = s * FF_SUB, (s + 1) * FF_SUB
                gt = _dot_nt(n, wgu_v[j, lo:hi, :])
                ut = _dot_nt(n, wgu_v[4 + j, lo:hi, :])
                gate_ref[j, :, lo:hi] = gt.astype(BF16)
                up_ref[j, :, lo:hi] = ut.astype(BF16)
                act_ref[j, :, lo:hi] = ((gt * _sigmoid(gt)) * ut).astype(BF16)
            acc = acc + _dot(act_ref[j], wd_v[j])
        ht = xt + 0.5 * acc
        if head is None:
            h_ref[...] = ht
        else:
            gain = gf_ref[...]
            diff = _rms_fwd_tile(ht, gain) - t_ref[...]
            part = 0.5 * jnp.sum(jnp.sum(diff * diff, axis=-1, keepdims=True) / D, axis=0, keepdims=True)
            dx, dg = _rms_bwd_tile(ht, gain, diff / D)
            dh_ref[...] = dx
            _accumulate(loss_ref, step, jnp.broadcast_to(part, (8, 128)))
            _accumulate(dgf_ref, step, dg)

    ff = jax.ShapeDtypeStruct((4, t, FF_PAD), BF16)
    operands, in_specs = (x, g, wgu, wd), [_row_spec(TM, D), _const_spec((1, D)), _ANY, _ANY]
    out_specs = [_row_spec(TM, D)] + [_blk_row_spec(4, TM, FF_PAD)] * 3 + [_row_spec(TM, D)]
    out_shape = [jax.ShapeDtypeStruct((t, D), BF16), ff, ff, ff, jax.ShapeDtypeStruct((t, D), F32)]
    if head is not None:
        operands += tuple(head)
        in_specs += [_const_spec((1, D)), _row_spec(TM, D)]
        out_specs += [_const_spec((8, 128)), _const_spec((1, D))]
        out_shape += [jax.ShapeDtypeStruct((8, 128), F32), jax.ShapeDtypeStruct((1, D), F32)]
    return _call(
        body, operands, grid=(t // TM,), name=name, comm=comm, in_specs=in_specs, out_specs=out_specs,
        out_shape=out_shape,
        scratch_shapes=[pltpu.VMEM((N_DEV, FF_PAD, D), BF16), pltpu.VMEM((4, FF_PAD, D), BF16),
                        pltpu.SemaphoreType.DMA((1 + N_DEV,))])


def _ffn_bwd(dh, xin, g, gate, up, wgu, wd, name, comm=None):
    t = dh.shape[0]

    def body(dh_ref, x_ref, g_ref, gate_ref, up_ref, wgu_hbm, wd_hbm,
             dgu_ref, dhb_ref, dx_ref, dg_ref, wgu_v, wd_v, sems):
        step = pl.program_id(0)
        _zero_down_pad(step, wd_v)
        _load_resident(step, [(wgu_hbm, wgu_v)] + _down_pairs(wd_hbm, wd_v), sems)
        dht = dh_ref[...]
        dhb = (0.5 * dht).astype(BF16)
        dhb_ref[...] = dhb
        dn = jnp.zeros((TM, D), F32)
        for j in range(4):
            for s in range(FF_PAD // FF_SUB):
                lo, hi = s * FF_SUB, (s + 1) * FF_SUB
                da = _dot_nt(dhb, wd_v[j, lo:hi, :])
                gt = gate_ref[j, :, lo:hi].astype(F32)
                ut = up_ref[j, :, lo:hi].astype(F32)
                sg = _sigmoid(gt)
                dgt = (da * ut * (sg * (1.0 + gt * (1.0 - sg)))).astype(BF16)
                dut = (da * (gt * sg)).astype(BF16)
                dgu_ref[j, :, lo:hi] = dgt
                dgu_ref[4 + j, :, lo:hi] = dut
            dn = dn + _dot(dgu_ref[j], wgu_v[j]) + _dot(dgu_ref[4 + j], wgu_v[4 + j])
        dx, dg = _rms_bwd_tile(x_ref[...], g_ref[...], dn)
        dx_ref[...] = dht + dx
        _accumulate(dg_ref, step, dg)

    return _call(
        body, (dh, xin, g, gate, up, wgu, wd), grid=(t // TM,), name=name, comm=comm,
        in_specs=[_row_spec(TM, D), _row_spec(TM, D), _const_spec((1, D)), _blk_row_spec(4, TM, FF_PAD),
                  _blk_row_spec(4, TM, FF_PAD), _ANY, _ANY],
        out_specs=[_blk_row_spec(N_DEV, TM, FF_PAD), _row_spec(TM, D), _row_spec(TM, D), _const_spec((1, D))],
        out_shape=[jax.ShapeDtypeStruct((N_DEV, t, FF_PAD), BF16), jax.ShapeDtypeStruct((t, D), BF16),
                   jax.ShapeDtypeStruct((t, D), F32), jax.ShapeDtypeStruct((1, D), F32)],
        scratch_shapes=[pltpu.VMEM((N_DEV, FF_PAD, D), BF16), pltpu.VMEM((4, FF_PAD, D), BF16),
                        pltpu.SemaphoreType.DMA((1 + N_DEV,))])


WIDE_TILES = (1024, 512, 256, 128)


def _pick_tile(n, options=(512, 256, 128)):
    for o in options:
        if n % o == 0:
            return o
    return n


def _into(stack, n_operands):
    if stack is None:
        return (), [], {}
    return (stack,), [_ANY], {n_operands: 0}


def _mm_tn_square(a, b, name, index, count, stack=None):
    k, m = a.shape
    _, n = b.shape
    tn = _pick_tile(n)
    extra, extra_specs, aliases = _into(stack, 2)

    def body(a_ref, b_ref, *rest):
        rest[-1][...] = _dot_tn(a_ref[...], b_ref[...]).astype(BF16).reshape(N_DEV, m // N_DEV, tn)

    return pl.pallas_call(
        body, grid=(n // tn,), name=name,
        in_specs=[pl.BlockSpec((k, m), lambda j: (0, 0)), pl.BlockSpec((k, tn), lambda j: (0, j))] + extra_specs,
        out_specs=pl.BlockSpec((N_DEV, m // N_DEV, tn), lambda j: (0, index, j)),
        out_shape=jax.ShapeDtypeStruct((N_DEV, count * (m // N_DEV), n), BF16),
        input_output_aliases=aliases, compiler_params=_cparams(1),
    )(a, b, *extra)


def _mm_tn_cols(a, b, name, first=0, count=None, stack=None):
    k, m = a.shape
    nb, _, n = b.shape
    tm = _pick_tile(m, WIDE_TILES)
    extra, extra_specs, aliases = _into(stack, 2)

    def body(a_ref, b_ref, *rest):
        rest[-1][0] = _dot_tn(a_ref[...].astype(BF16), b_ref[0].astype(BF16)).astype(BF16)

    return pl.pallas_call(
        body, grid=(nb, m // tm), name=name,
        in_specs=[pl.BlockSpec((k, tm), lambda j, i: (0, i)), pl.BlockSpec((1, k, n), lambda j, i: (j, 0, 0))]
                 + extra_specs,
        out_specs=pl.BlockSpec((1, tm, n), lambda j, i: (j + first, i, 0)),
        out_shape=jax.ShapeDtypeStruct((nb if count is None else count, m, n), BF16),
        input_output_aliases=aliases, compiler_params=_cparams(2),
    )(a, b, *extra)


def _mm_tn_rows(a, b, keep, name, comm=None):
    nb, k, m = a.shape
    _, n = b.shape
    tn = _pick_tile(n, WIDE_TILES)

    def body(a_ref, b_ref, o_ref):
        o_ref[0] = _dot_tn(a_ref[0], b_ref[...])[:keep].astype(BF16)

    (out,), couts = _call(
        body, (a, b), grid=(nb, n // tn), name=name, comm=comm,
        in_specs=[pl.BlockSpec((1, k, m), lambda j, i: (j, 0, 0)), pl.BlockSpec((k, tn), lambda j, i: (0, i))],
        out_specs=[pl.BlockSpec((1, keep, tn), lambda j, i: (j, 0, i))],
        out_shape=[jax.ShapeDtypeStruct((nb, keep, n), BF16)], scratch_shapes=[])
    return out if comm is None else (out, couts)


PCG_W = 5 * D
QKV_W = 3 * D
PROJ_SUB = 512


def _inproj_fwd(h, g, w_in, conv_w, name, comm=None):
    t = h.shape[0]

    def body(h_ref, g_ref, w_hbm, cw_ref, u_ref, pcg_ref, qkv_ref, yc_ref, w_v, tail_v, sems):
        step = pl.program_id(0)
        _load_resident(step, [(w_hbm, w_v)], sems)

        @pl.when(step == 0)
        def _():
            tail_v[...] = jnp.zeros_like(tail_v)

        u = _rms_fwd_tile(h_ref[...], g_ref[...]).astype(BF16)
        u_ref[...] = u
        for blk in range(N_DEV):
            for s in range(D // PROJ_SUB):
                lo, hi = s * PROJ_SUB, (s + 1) * PROJ_SUB
                p = _dot(u, w_v[blk, :, lo:hi])
                if blk < 3:
                    pcg_ref[:, blk * D + lo:blk * D + hi] = p
                elif blk < 6:
                    qkv_ref[:, (blk - 3) * D + lo:(blk - 3) * D + hi] = p.astype(BF16)
                else:
                    pcg_ref[:, (blk - 3) * D + lo:(blk - 3) * D + hi] = p
        xc = pcg_ref[:, D:2 * D] * pcg_ref[:, 2 * D:3 * D]
        ext = jnp.concatenate([tail_v[...], xc], axis=0)
        conv = (cw_ref[0:1, :] * pltpu.roll(ext, 2, 0)[8:] + cw_ref[1:2, :] * pltpu.roll(ext, 1, 0)[8:]
                + cw_ref[2:3, :] * xc)
        yc_ref[...] = (pcg_ref[:, 0:D] * conv).astype(BF16)
        tail_v[...] = xc[TM - 8:]

    return _call(
        body, (h, g, w_in, conv_w), grid=(t // TM,), name=name, comm=comm,
        in_specs=[_row_spec(TM, D), _const_spec((1, D)), _ANY, _const_spec((CONV_K, D))],
        out_specs=[_row_spec(TM, D), _row_spec(TM, PCG_W), _row_spec(TM, QKV_W), _row_spec(TM, D)],
        out_shape=[jax.ShapeDtypeStruct((t, D), BF16), jax.ShapeDtypeStruct((t, PCG_W), F32),
                   jax.ShapeDtypeStruct((t, QKV_W), BF16), jax.ShapeDtypeStruct((t, D), BF16)],
        scratch_shapes=[pltpu.VMEM((N_DEV, D, D), BF16), pltpu.VMEM((8, D), F32), pltpu.SemaphoreType.DMA((1,))])


def _tri2(cond):
    rr = lax.broadcasted_iota(jnp.int32, (2 * TK, TK), 0) & (TK - 1)
    cc = lax.broadcasted_iota(jnp.int32, (2 * TK, TK), 1)
    return cond(rr, cc).astype(BF16)


def _causal(shift, row0=0):
    rr = lax.broadcasted_iota(jnp.int32, (TQ - row0, TK), 0) + row0
    cc = lax.broadcasted_iota(jnp.int32, (TQ - row0, TK), 1)
    return cc + shift < rr


def _cumdot(v, tri2):
    hi = v.astype(BF16)
    lo = (v - hi.astype(F32)).astype(BF16)
    return _dot(jnp.concatenate([hi, lo], axis=1), tri2)


def _log_1m_beta(z):
    return -(jnp.maximum(z, 0.0) + jnp.log(1.0 + jnp.exp(-jnp.abs(z))))


def _sb_specs(t):
    g = SB_H // SB_HPS
    w = SB_HPS * SB_DH
    q_spec = pl.BlockSpec((TQ, w), lambda h, i: (i, h))
    k_spec = pl.BlockSpec((t, w), lambda h, i: (0, g + h))
    v_spec = pl.BlockSpec((t, w), lambda h, i: (0, 2 * g + h))
    ct_spec = pl.BlockSpec((SB_HPS, TQ, 1), lambda h, i: (h, i, 0))
    return g, w, q_spec, k_spec, v_spec, ct_spec


def _sb_fwd(qkv, name, comm=None):
    t = qkv.shape[0]
    scale = SB_DH ** -0.5
    g, w, q_spec, k_spec, v_spec, ct_spec = _sb_specs(t)

    def body(q_ref, k_ref, v_ref, y_ref, ct_ref):
        i = pl.program_id(1)
        later = _tri2(lambda j, s: j > s)
        n_diag = TQ // TK

        def block(j, carry, shift):
            off = pl.multiple_of(j * TK, TK)
            zs, ms = [], []
            for hd in range(SB_HPS):
                cols = slice(hd * SB_DH, (hd + 1) * SB_DH)
                z = _dot_nt(q_ref[:, cols], k_ref[pl.ds(off, TK), cols]) * scale
                m = _log_1m_beta(z)
                if shift is not None:
                    m = jnp.where(_causal(shift), m, 0.0)
                zs.append(z)
                ms.append(m)
            after = _cumdot(jnp.concatenate(ms, axis=0), later)
            out = []
            for hd in range(SB_HPS):
                acc, c_sum = carry[hd]
                cols = slice(hd * SB_DH, (hd + 1) * SB_DH)
                a = jnp.exp((ms[hd] + zs[hd]) + (c_sum + after[hd * TQ:(hd + 1) * TQ]))
                if shift is not None:
                    a = jnp.where(_causal(shift), a, 0.0)
                out.append((acc + _dot(a.astype(BF16), v_ref[pl.ds(off, TK), cols]),
                            c_sum + jnp.sum(ms[hd], axis=1, keepdims=True)))
            return tuple(out)

        carry = tuple((jnp.zeros((TQ, SB_DH), F32), jnp.zeros((TQ, 1), F32)) for _ in range(SB_HPS))
        for d in reversed(range(n_diag)):
            carry = block(i * n_diag + d, carry, d * TK)
        carry = lax.fori_loop(0, i * n_diag, lambda jj, c: block(i * n_diag - 1 - jj, c, None), carry)
        for hd in range(SB_HPS):
            y_ref[:, hd * SB_DH:(hd + 1) * SB_DH] = carry[hd][0].astype(BF16)
            ct_ref[hd] = carry[hd][1]

    return _call(
        body, (qkv, qkv, qkv), grid=(g, t // TQ), name=name, comm=comm,
        in_specs=[q_spec, k_spec, v_spec],
        out_specs=[q_spec, ct_spec],
        out_shape=[jax.ShapeDtypeStruct((t, D), BF16), jax.ShapeDtypeStruct((SB_H, t, 1), F32)],
        scratch_shapes=[])


def _sb_bwd(qkv, dy, ctot, after, name, comm=None):
    t = qkv.shape[0]
    scale = SB_DH ** -0.5
    g, w, q_spec, k_spec, v_spec, ct_spec = _sb_specs(t)
    acc_spec = pl.BlockSpec((2, t, w), lambda h, i: (0, 0, h))

    def body(q_ref, k_ref, v_ref, dy_ref, ct_ref, after_ref, dq_ref, dkv_ref):
        i = pl.program_id(1)

        @pl.when(i == 0)
        def _():
            dkv_ref[...] = jnp.zeros_like(dkv_ref)

        upto = _tri2(lambda j, s: j <= s)
        n_diag = TQ // TK

        def block(j, carry, shift):
            off = pl.multiple_of(j * TK, TK)
            r0 = 0 if shift is None else shift
            nr = TQ - r0
            causal = None if shift is None else _causal(shift, r0)

            def grow(old, delta):
                return old + delta if r0 == 0 else jnp.concatenate([old[:r0], old[r0:] + delta], axis=0)

            zs, ms = [], []
            for hd in range(SB_HPS):
                cols = slice(hd * SB_DH, (hd + 1) * SB_DH)
                z = _dot_nt(q_ref[r0:, cols], k_ref[pl.ds(off, TK), cols]) * scale
                m = _log_1m_beta(z)
                if causal is not None:
                    m = jnp.where(causal, m, 0.0)
                zs.append(z)
                ms.append(m)
            m_upto = _cumdot(jnp.concatenate(ms, axis=0), upto)
            ls, a_s, es = [], [], []
            for hd in range(SB_HPS):
                cols = slice(hd * SB_DH, (hd + 1) * SB_DH)
                l = ms[hd] + zs[hd]
                a = jnp.exp(l + ((ct_ref[hd, r0:] - carry[hd][1][r0:]) - m_upto[hd * nr:(hd + 1) * nr]))
                if causal is not None:
                    a = jnp.where(causal, a, 0.0)
                ls.append(l)
                a_s.append(a)
                es.append(_dot_nt(dy_ref[r0:, cols], v_ref[pl.ds(off, TK), cols]) * a)
            e_upto = _dot(jnp.concatenate(es, axis=0).astype(BF16), upto[:TK])
            out = []
            for hd in range(SB_HPS):
                dq, p_sum, e_sum = carry[hd]
                cols = slice(hd * SB_DH, (hd + 1) * SB_DH)
                e = es[hd]
                dz = e - jnp.exp(ls[hd]) * (e_sum[r0:] + e_upto[hd * nr:(hd + 1) * nr])
                if causal is not None:
                    dz = jnp.where(causal, dz, 0.0)
                dzs = (dz * scale).astype(BF16)
                dkv_ref[0, pl.ds(off, TK), cols] += _dot_tn(dzs, q_ref[r0:, cols])
                dkv_ref[1, pl.ds(off, TK), cols] += _dot_tn(a_s[hd].astype(BF16), dy_ref[r0:, cols])
                out.append((grow(dq, _dot(dzs, k_ref[pl.ds(off, TK), cols])),
                            grow(p_sum, jnp.sum(ms[hd], axis=1, keepdims=True)),
                            grow(e_sum, jnp.sum(e, axis=1, keepdims=True))))
            return tuple(out)

        zero = jnp.zeros((TQ, 1), F32)
        init = tuple((jnp.zeros((TQ, SB_DH), F32), zero, zero) for _ in range(SB_HPS))
        carry = lax.fori_loop(0, i * n_diag, lambda j, c: block(j, c, None), init)
        for d in range(n_diag):
            carry = block(i * n_diag + d, carry, d * TK)
        for hd in range(SB_HPS):
            dq_ref[:, hd * SB_DH:(hd + 1) * SB_DH] = carry[hd][0].astype(BF16)

    return _call(
        body, (qkv, qkv, qkv, dy, ctot, after), grid=(g, t // TQ), name=name, comm=comm,
        in_specs=[q_spec, k_spec, v_spec, q_spec, ct_spec, pl.BlockSpec(after.shape, lambda h, i: (0, 0))],
        out_specs=[q_spec, acc_spec],
        out_shape=[jax.ShapeDtypeStruct((t, D), BF16), jax.ShapeDtypeStruct((2, t, D), F32)],
        scratch_shapes=[])


def _gate_specs():
    return [pl.BlockSpec((TM, D), lambda i: (i, 3)), pl.BlockSpec((TM, D), lambda i: (i, 4))]


def _mix_pairs(mix_hbm, dsts):
    pairs = []
    for index, dst in enumerate(dsts):
        pairs += _square_pairs(mix_hbm, index, dst)
    return pairs


def _mix_out_fwd(yc, ysb, pcg, b_gate, h, w_mix, name, comm=None):
    t = h.shape[0]

    def body(yc_ref, ysb_ref, gc_ref, gs_ref, b_ref, h_ref, mix_hbm,
             a_ref, b_out_ref, mg_ref, h2_ref, wc_v, wa_v, wo_v, sems):
        _load_resident(pl.program_id(0), _mix_pairs(mix_hbm, (wc_v, wa_v, wo_v)), sems)
        a = _dot(yc_ref[...], wc_v[...])
        b = _dot(ysb_ref[...], wa_v[...])
        merged = (_sigmoid(gc_ref[...] + b_ref[:, :D]) * a + _sigmoid(gs_ref[...] + b_ref[:, D:]) * b).astype(BF16)
        a_ref[...] = a
        b_out_ref[...] = b
        mg_ref[...] = merged
        h2_ref[...] = h_ref[...] + _dot(merged, wo_v[...])

    return _call(
        body, (yc, ysb, pcg, pcg, b_gate, h, w_mix), grid=(t // TM,), name=name, comm=comm,
        in_specs=[_row_spec(TM, D), _row_spec(TM, D)] + _gate_specs()
                 + [_const_spec((1, 2 * D)), _row_spec(TM, D), _ANY],
        out_specs=[_row_spec(TM, D)] * 4,
        out_shape=[jax.ShapeDtypeStruct((t, D), F32), jax.ShapeDtypeStruct((t, D), F32),
                   jax.ShapeDtypeStruct((t, D), BF16), jax.ShapeDtypeStruct((t, D), F32)],
        scratch_shapes=[pltpu.VMEM((D, D), BF16)] * 3 + [pltpu.SemaphoreType.DMA((3 * N_DEV,))])


def _mix_out_bwd(dh2, a, b, pcg, b_gate, conv_w, w_mix, name, comm=None):
    t = dh2.shape[0]
    n_tile = t // TM
    per8 = TM // 8

    def rows(n):
        return pl.BlockSpec((TM, n), lambda i: (n_tile - 1 - i, 0))

    def cols(block):
        return pl.BlockSpec((TM, D), lambda i: (n_tile - 1 - i, block))

    def before(block):
        return pl.BlockSpec((8, D), lambda i: (jnp.maximum((n_tile - 1 - i) * per8 - 1, 0), block))

    def body(dh_ref, a_ref, b_ref, gc_ref, gs_ref, cb_ref, cc_ref, cx_ref, ccp_ref, cxp_ref, bias_ref, cw_ref, mix_hbm,
             dhb_ref, da_ref, db_ref, dgp_ref, dc_ref, dysb_ref, dbias_ref, dcw_ref, wc_v, wa_v, wo_v, head_v, sems):
        step = pl.program_id(0)
        _load_resident(step, _mix_pairs(mix_hbm, (wc_v, wa_v, wo_v)), sems)

        @pl.when(step == 0)
        def _():
            head_v[...] = jnp.zeros_like(head_v)
            dcw_ref[...] = jnp.zeros_like(dcw_ref)

        dhb = dh_ref[...].astype(BF16)
        dhb_ref[...] = dhb
        dm = _dot_nt(dhb, wo_v[...])
        gc = _sigmoid(gc_ref[...] + bias_ref[:, :D])
        gs = _sigmoid(gs_ref[...] + bias_ref[:, D:])
        da = (dm * gc).astype(BF16)
        db = (dm * gs).astype(BF16)
        da_ref[...] = da
        db_ref[...] = db
        dgc = dm * a_ref[...] * (gc * (1.0 - gc))
        dgs = dm * b_ref[...] * (gs * (1.0 - gs))
        dgp_ref[0] = dgc.astype(BF16)
        dgp_ref[1] = dgs.astype(BF16)
        _accumulate(dbias_ref.at[:, :D], step, jnp.sum(dgc, axis=0, keepdims=True))
        _accumulate(dbias_ref.at[:, D:], step, jnp.sum(dgs, axis=0, keepdims=True))
        dysb_ref[...] = _dot_nt(db, wa_v[...]).astype(BF16)
        dyc = _dot_nt(da, wc_v[...])
        cc, cx = cc_ref[...], cx_ref[...]
        xc = cc * cx
        xc_before = jnp.where(step == n_tile - 1, 0.0, ccp_ref[...] * cxp_ref[...])
        ext = jnp.concatenate([xc_before, xc], axis=0)
        x1 = pltpu.roll(ext, 1, 0)[8:]
        x2 = pltpu.roll(ext, 2, 0)[8:]
        w0, w1, w2 = cw_ref[0:1, :], cw_ref[1:2, :], cw_ref[2:3, :]
        dc_ref[0] = (dyc * (w0 * x2 + w1 * x1 + w2 * xc)).astype(BF16)
        dconv = dyc * cb_ref[...]
        dcw_ref[0:1, :] += jnp.sum(dconv * x2, axis=0, keepdims=True)
        dcw_ref[1:2, :] += jnp.sum(dconv * x1, axis=0, keepdims=True)
        dcw_ref[2:3, :] += jnp.sum(dconv * xc, axis=0, keepdims=True)
        after = jnp.concatenate([dconv, head_v[...]], axis=0)
        dxc = w2 * dconv + w1 * pltpu.roll(after, TM + 7, 0)[:TM] + w0 * pltpu.roll(after, TM + 6, 0)[:TM]
        dc_ref[1] = (dxc * cx).astype(BF16)
        dc_ref[2] = (dxc * cc).astype(BF16)
        head_v[...] = dconv[:8]

    return _call(
        body, (dh2, a, b, pcg, pcg, pcg, pcg, pcg, pcg, pcg, b_gate, conv_w, w_mix), grid=(n_tile,), name=name,
        comm=comm,
        in_specs=[rows(D)] * 3 + [cols(3), cols(4), cols(0), cols(1), cols(2), before(1), before(2),
                                  _const_spec((1, 2 * D)), _const_spec((CONV_K, D)), _ANY],
        out_specs=[rows(D)] * 3 + [pl.BlockSpec((2, TM, D), lambda i: (0, n_tile - 1 - i, 0)),
                                   pl.BlockSpec((3, TM, D), lambda i: (0, n_tile - 1 - i, 0)), rows(D),
                                   _const_spec((1, 2 * D)), _const_spec((8, D))],
        out_shape=[jax.ShapeDtypeStruct((t, D), BF16)] * 3
                  + [jax.ShapeDtypeStruct((2, t, D), BF16), jax.ShapeDtypeStruct((3, t, D), BF16),
                     jax.ShapeDtypeStruct((t, D), BF16), jax.ShapeDtypeStruct((1, 2 * D), F32),
                     jax.ShapeDtypeStruct((8, D), F32)],
        scratch_shapes=[pltpu.VMEM((D, D), BF16)] * 3 + [pltpu.VMEM((8, D), F32),
                                                         pltpu.SemaphoreType.DMA((3 * N_DEV,))])


def _inproj_bwd(dconv, dq, dkv, dgp, w_in, h, g, dh_res, name, comm=None):
    t = h.shape[0]

    def body(dc_ref, dq_ref, dkv_ref, dgp_ref, w_hbm, h_ref, g_ref, dres_ref, dh_ref, dg_ref, w_v, sems):
        step = pl.program_id(0)
        _load_resident(step, [(w_hbm, w_v)], sems)
        du = _dot_nt(dq_ref[...], w_v[3])
        for k in range(3):
            du = du + _dot_nt(dc_ref[k], w_v[k])
        for k in range(2):
            du = du + _dot_nt(dkv_ref[k].astype(BF16), w_v[4 + k]) + _dot_nt(dgp_ref[k], w_v[6 + k])
        dx, dg = _rms_bwd_tile(h_ref[...], g_ref[...], du)
        dh_ref[...] = dres_ref[...] + dx
        _accumulate(dg_ref, step, dg)

    return _call(
        body, (dconv, dq, dkv, dgp, w_in, h, g, dh_res), grid=(t // TM,), name=name, comm=comm,
        in_specs=[_blk_row_spec(3, TM, D), _row_spec(TM, D), _blk_row_spec(2, TM, D), _blk_row_spec(2, TM, D), _ANY,
                  _row_spec(TM, D), _const_spec((1, D)), _row_spec(TM, D)],
        out_specs=[_row_spec(TM, D), _const_spec((1, D))],
        out_shape=[jax.ShapeDtypeStruct((t, D), F32), jax.ShapeDtypeStruct((1, D), F32)],
        scratch_shapes=[pltpu.VMEM((N_DEV, D, D), BF16), pltpu.SemaphoreType.DMA((1,))])


def _softmax_rows(s):
    e = jnp.exp(s - jnp.max(s, axis=-1, keepdims=True))
    return e / jnp.sum(e, axis=-1, keepdims=True)


def _cross_pairs(cross_hbm, wq_v, wo_v):
    return _square_pairs(cross_hbm, 0, wq_v) + _square_pairs(cross_hbm, 1, wo_v)


def _cross_fwd(h, g, mem, g_mem, w_ckv, w_cross, name):
    t = h.shape[0]
    m = mem.shape[0]
    scale = X_DH ** -0.5

    def body(h_ref, g_ref, mem_ref, gm_ref, wkv_ref, cross_hbm, hn_ref, qx_ref, o_ref, h3_ref, mn_ref, kv_ref,
             wq_v, wo_v, sems):
        _load_resident(pl.program_id(0), _cross_pairs(cross_hbm, wq_v, wo_v), sems)

        @pl.when(pl.program_id(0) == 0)
        def _():
            mn = _rms_fwd_tile(mem_ref[...], gm_ref[...]).astype(BF16)
            mn_ref[...] = mn
            for j in range(N_DEV):
                kv_ref[j] = _dot(mn, wkv_ref[j]).astype(BF16)

        ht = h_ref[...]
        hn = _rms_fwd_tile(ht, g_ref[...]).astype(BF16)
        hn_ref[...] = hn
        qx = _dot(hn, wq_v[...]).astype(BF16)
        qx_ref[...] = qx
        for hd in range(X_H):
            lo, hi = hd * X_DH, (hd + 1) * X_DH
            p = _softmax_rows(_dot_nt(qx[:, lo:hi], kv_ref[hd]) * scale)
            o_ref[:, lo:hi] = _dot(p.astype(BF16), kv_ref[X_H + hd]).astype(BF16)
        h3_ref[...] = ht + _dot(o_ref[...], wo_v[...])

    return pl.pallas_call(
        body, grid=(t // TM,), name=name,
        in_specs=[_row_spec(TM, D), _const_spec((1, D)), _const_spec((m, D)), _const_spec((1, D)),
                  _const_spec((N_DEV, D, X_DH)), _ANY],
        out_specs=[_row_spec(TM, D)] * 4 + [_const_spec((m, D)), _const_spec((N_DEV, m, X_DH))],
        out_shape=[jax.ShapeDtypeStruct((t, D), BF16)] * 3 + [jax.ShapeDtypeStruct((t, D), F32),
                                                              jax.ShapeDtypeStruct((m, D), BF16),
                                                              jax.ShapeDtypeStruct((N_DEV, m, X_DH), BF16)],
        scratch_shapes=[pltpu.VMEM((D, D), BF16)] * 2 + [pltpu.SemaphoreType.DMA((2 * N_DEV,))],
        compiler_params=_cparams(),
    )(h, g, mem, g_mem, w_ckv, w_cross)


def _cross_bwd(dh3, h, g, qx, kv, mem, g_mem, w_ckv, w_cross, name, comm=None):
    t = h.shape[0]
    m = kv.shape[1]
    scale = X_DH ** -0.5

    def body(dh_ref, h_ref, g_ref, qx_ref, kv_ref, mem_ref, gm_ref, wkv_ref, cross_hbm,
             dhb_ref, dqx_ref, dkv_ref, dh2_ref, dg_ref, dgm_ref, wq_v, wo_v, sems):
        step = pl.program_id(0)
        _load_resident(step, _cross_pairs(cross_hbm, wq_v, wo_v), sems)

        @pl.when(step == 0)
        def _():
            dkv_ref[...] = jnp.zeros_like(dkv_ref)

        dht = dh_ref[...]
        dhb = dht.astype(BF16)
        dhb_ref[...] = dhb
        do = _dot_nt(dhb, wo_v[...]).astype(BF16)
        for hd in range(X_H):
            lo, hi = hd * X_DH, (hd + 1) * X_DH
            qh = qx_ref[:, lo:hi]
            kh = kv_ref[hd]
            p = _softmax_rows(_dot_nt(qh, kh) * scale)
            doh = do[:, lo:hi]
            dp = _dot_nt(doh, kv_ref[X_H + hd])
            ds = (p * (dp - jnp.sum(dp * p, axis=-1, keepdims=True)) * scale).astype(BF16)
            dqx_ref[:, lo:hi] = _dot(ds, kh).astype(BF16)
            dkv_ref[hd] += _dot_tn(ds, qh)
            dkv_ref[X_H + hd] += _dot_tn(p.astype(BF16), doh)
        dhn = _dot_nt(dqx_ref[...], wq_v[...])
        dx, dg = _rms_bwd_tile(h_ref[...], g_ref[...], dhn)
        dh2_ref[...] = dht + dx
        _accumulate(dg_ref, step, dg)

        @pl.when(step == t // TM - 1)
        def _():
            dmn = jnp.zeros((m, D), F32)
            for j in range(N_DEV):
                dmn = dmn + _dot_nt(dkv_ref[j].astype(BF16), wkv_ref[j])
            dgm_ref[...] = _rms_bwd_tile(mem_ref[...], gm_ref[...], dmn)[1]

    return _call(
        body, (dh3, h, g, qx, kv, mem, g_mem, w_ckv, w_cross), grid=(t // TM,), name=name, comm=comm,
        in_specs=[_row_spec(TM, D), _row_spec(TM, D), _const_spec((1, D)), _row_spec(TM, D),
                  _const_spec((N_DEV, m, X_DH)), _const_spec((m, D)), _const_spec((1, D)),
                  _const_spec((N_DEV, D, X_DH)), _ANY],
        out_specs=[_row_spec(TM, D), _row_spec(TM, D), _const_spec((N_DEV, m, X_DH)), _row_spec(TM, D),
                   _const_spec((1, D)), _const_spec((1, D))],
        out_shape=[jax.ShapeDtypeStruct((t, D), BF16), jax.ShapeDtypeStruct((t, D), BF16),
                   jax.ShapeDtypeStruct((N_DEV, m, X_DH), F32), jax.ShapeDtypeStruct((t, D), F32),
                   jax.ShapeDtypeStruct((1, D), F32), jax.ShapeDtypeStruct((1, D), F32)],
        scratch_shapes=[pltpu.VMEM((D, D), BF16)] * 2 + [pltpu.SemaphoreType.DMA((2 * N_DEV,))])


def _adamw(w, parts, m, v, name, row_block=0, token=None):
    r, c = w.shape
    n = parts.shape[0]
    tr = _pick_tile(r, (256, 352, 128))
    off = row_block * (r // tr)

    def body(*refs):
        if token is None:
            _adamw_update(None, *refs)
        else:
            _adamw_update(refs[4], *refs[:4], *refs[5:])

    spec = _row_spec(tr, c)
    in_specs = [spec, pl.BlockSpec((n, tr, c), lambda i: (0, i + off, 0)), spec, spec]
    operands = (w, parts, m, v)
    if token is not None:
        in_specs.append(_const_spec(token.shape))
        operands += (token,)
    return pl.pallas_call(
        body, grid=(r // tr,), name=name, in_specs=in_specs, out_specs=[spec] * 4,
        out_shape=[jax.ShapeDtypeStruct((r, c), F32)] * 4,
        compiler_params=_cparams(),
    )(*operands)


def _adamw_update(tok_ref, w_ref, p_ref, m_ref, v_ref, g_ref, d_ref, nm_ref, nv_ref):
    gt = p_ref[0].astype(F32)
    for k in range(1, p_ref.shape[0]):
        gt = gt + p_ref[k].astype(F32)
    if tok_ref is not None:
        gt = gt + tok_ref[0:1, 0:1]
    _adamw_apply(gt, w_ref, m_ref, v_ref, g_ref, d_ref, nm_ref, nv_ref)


def _adamw_own(w, land, own, chip, m, v, name, row_block=0, token=None):
    r, c = w.shape
    tr = _pick_tile(r, (256, 352, 128))
    off = row_block * (r // tr)

    def body(chip_ref, w_ref, land_ref, own_ref, m_ref, v_ref, *rest):
        mine = own_ref[0].astype(F32)
        gt = jnp.where(chip_ref[0] == 0, mine, land_ref[0].astype(F32))
        for k in range(1, N_CHIP):
            gt = gt + jnp.where(chip_ref[0] == k, mine, land_ref[k].astype(F32))
        if token is not None:
            gt = gt + rest[0][0:1, 0:1]
        _adamw_apply(gt, w_ref, m_ref, v_ref, *rest[-4:])

    spec = pl.BlockSpec((tr, c), lambda i, chip_ref: (i, 0))
    in_specs = [spec, pl.BlockSpec((N_CHIP, tr, c), lambda i, chip_ref: (0, i + off, 0)),
                pl.BlockSpec((1, tr, c), lambda i, chip_ref: (chip_ref[0], i + off, 0)), spec, spec]
    operands = (chip, w, land, own, m, v)
    if token is not None:
        in_specs.append(pl.BlockSpec(token.shape, lambda i, chip_ref: (0, 0)))
        operands += (token,)
    return pl.pallas_call(
        body, name=name,
        grid_spec=pltpu.PrefetchScalarGridSpec(
            num_scalar_prefetch=1, grid=(r // tr,), in_specs=in_specs, out_specs=[spec] * 4),
        out_shape=[jax.ShapeDtypeStruct((r, c), F32)] * 4,
        compiler_params=_cparams(),
    )(*operands)


def _adamw_apply(gt, w_ref, m_ref, v_ref, g_ref, d_ref, nm_ref, nv_ref):
    g_ref[...] = gt
    nm = ADAM_B1 * m_ref[...] + (1.0 - ADAM_B1) * gt
    nv = ADAM_B2 * v_ref[...] + (1.0 - ADAM_B2) * jnp.square(gt)
    m_hat = nm / (1.0 - ADAM_B1 ** ADAM_STEP)
    v_hat = nv / (1.0 - ADAM_B2 ** ADAM_STEP)
    d_ref[...] = -ADAM_LR * (m_hat / (jnp.sqrt(v_hat) + ADAM_EPS) + ADAM_WD * w_ref[...])
    nm_ref[...] = nm
    nv_ref[...] = nv


def _mesh_pos():
    return lax.axis_index("x"), lax.axis_index("y"), lax.axis_index("c")


def _both(first, second):
    n_in, n_out, n_sem = len(first.inputs), len(first.out_shapes), len(first.sem_shapes)

    def run(round_name):
        def both(in_refs, out_refs, sems):
            getattr(first, round_name)(in_refs[:n_in], out_refs[:n_out], sems[:n_sem])
            getattr(second, round_name)(in_refs[n_in:], out_refs[n_out:], sems[n_sem:])
        return both

    return types.SimpleNamespace(
        inputs=first.inputs + second.inputs, out_shapes=first.out_shapes + second.out_shapes,
        sem_shapes=first.sem_shapes + second.sem_shapes, start=run("start"), middle=run("middle"),
        finish=run("finish"))


def _no_round(in_refs, out_refs, sems):
    pass


def _run_exchange(comm, name):
    c_in, c_out = len(comm.inputs), len(comm.out_shapes)

    def body(*refs):
        cins, couts, sems = refs[:c_in], refs[c_in:c_in + c_out], refs[c_in + c_out:]
        comm.start(cins, couts, sems)
        comm.middle(cins, couts, sems)
        comm.finish(cins, couts, sems)

    return list(pl.pallas_call(
        body, name=name, out_shape=list(comm.out_shapes),
        in_specs=[_ANY] * c_in, out_specs=[_ANY] * c_out, scratch_shapes=list(comm.sem_shapes),
    )(*comm.inputs))


def _gather_exchange(shards):
    n_arr = len(shards)

    def plan(x_refs, out_refs, sems):
        send_sems, recv_sems, local_sems = sems[:3]
        stage = sems[3:]
        x, y, c = _mesh_pos()
        me, sibling = (x, y, c), (x, y, 1 - c)
        xn, yn, diag = (1 - x, y), (x, 1 - y), (1 - x, 1 - y)

        def slot(a, px, py, pc, half=None):
            ref = out_refs[a].at[4 * px + 2 * py + pc]
            if half is None:
                return ref
            rows = shards[a].shape[0] // 2
            return ref.at[half * rows:(half + 1) * rows]

        def copy(a, k, block, to, half=None, src=None):
            dst = slot(a, *block, half)
            return pltpu.make_async_remote_copy(
                src_ref=dst if src is None else src, dst_ref=dst,
                send_sem=send_sems.at[a, k], recv_sem=recv_sems.at[a, k],
                device_id=to, device_id_type=pl.DeviceIdType.MESH)

        return types.SimpleNamespace(
            me=me, sibling=sibling, xn=xn, yn=yn, diag=diag, c=c, copy=copy,
            mine_in=[pltpu.make_async_copy(x_refs[a], stage[a], local_sems.at[a, 0]) for a in range(n_arr)],
            mine_out=[pltpu.make_async_copy(stage[a], slot(a, *me), local_sems.at[a, 1]) for a in range(n_arr)],
            first=[cp for a in range(n_arr) for cp in (
                copy(a, 0, me, sibling, src=x_refs[a]), copy(a, 1, me, (*xn, c), src=x_refs[a]),
                copy(a, 2, me, (*yn, c), src=x_refs[a]))],
            second=lambda a: (copy(a, 3, (*xn, c), (*yn, c), half=0), copy(a, 5, (*xn, c), sibling),
                              copy(a, 4, (*yn, c), (*xn, c), half=1), copy(a, 6, (*yn, c), sibling)),
            third=lambda a: (copy(a, 7, (*diag, c), sibling, half=0), copy(a, 8, (*diag, c), sibling, half=1)))

    def start(x_refs, out_refs, sems):
        p = plan(x_refs, out_refs, sems)
        for cp in p.first + p.mine_in:
            cp.start()
        for cp_in, cp_out in zip(p.mine_in, p.mine_out):
            cp_in.wait()
            cp_out.start()

    def middle(x_refs, out_refs, sems):
        p = plan(x_refs, out_refs, sems)
        for a in range(n_arr):
            to_yn, x_to_sib, to_xn, y_to_sib = p.second(a)
            p.copy(a, 1, (*p.xn, p.c), p.me).wait_recv()
            to_yn.start()
            x_to_sib.start()
            p.copy(a, 2, (*p.yn, p.c), p.me).wait_recv()
            to_xn.start()
            y_to_sib.start()

    def finish(x_refs, out_refs, sems):
        p = plan(x_refs, out_refs, sems)
        for a in range(n_arr):
            half0_to_sib, half1_to_sib = p.third(a)
            p.copy(a, 3, (*p.diag, p.c), p.me, half=0).wait_recv()
            half0_to_sib.start()
            p.copy(a, 4, (*p.diag, p.c), p.me, half=1).wait_recv()
            half1_to_sib.start()
        other = 1 - p.c
        for a in range(n_arr):
            p.copy(a, 0, p.sibling, p.me).wait_recv()
            p.copy(a, 5, (*p.xn, other), p.me).wait_recv()
            p.copy(a, 6, (*p.yn, other), p.me).wait_recv()
            p.copy(a, 7, (*p.diag, other), p.me, half=0).wait_recv()
            p.copy(a, 8, (*p.diag, other), p.me, half=1).wait_recv()
        for cp in p.first:
            cp.wait_send()
        for a in range(n_arr):
            for cp in p.second(a) + p.third(a):
                cp.wait_send()
        for cp in p.mine_out:
            cp.wait()

    return types.SimpleNamespace(
        inputs=list(shards), start=start, middle=middle, finish=finish,
        out_shapes=[jax.ShapeDtypeStruct((N_DEV,) + s.shape, s.dtype) for s in shards],
        sem_shapes=[pltpu.SemaphoreType.DMA((n_arr, 9)), pltpu.SemaphoreType.DMA((n_arr, 9)),
                    pltpu.SemaphoreType.DMA((n_arr, 2))] + [pltpu.VMEM(s.shape, s.dtype) for s in shards])


def _pair_exchange(grads):
    n_arr = len(grads)

    def plan(g_refs, land_refs, sems):
        send_sems, recv_sems = sems
        x, y, c = _mesh_pos()
        return [pltpu.make_async_remote_copy(
            src_ref=g_refs[a].at[2 * k + 1 - c], dst_ref=land_refs[a].at[k],
            send_sem=send_sems.at[a, k], recv_sem=recv_sems.at[a, k],
            device_id=(x, y, 1 - c), device_id_type=pl.DeviceIdType.MESH)
            for a in range(n_arr) for k in range(N_CHIP)]

    def start(g_refs, land_refs, sems):
        for cp in plan(g_refs, land_refs, sems):
            cp.start()

    def finish(g_refs, land_refs, sems):
        for cp in plan(g_refs, land_refs, sems):
            cp.wait()

    return types.SimpleNamespace(
        inputs=list(grads), start=start, middle=_no_round, finish=finish,
        out_shapes=[jax.ShapeDtypeStruct((N_CHIP,) + g.shape[1:], g.dtype) for g in grads],
        sem_shapes=[pltpu.SemaphoreType.DMA((n_arr, N_CHIP)), pltpu.SemaphoreType.DMA((n_arr, N_CHIP))])


def _chip_exchange(parts):
    n_arr = len(parts)

    def plan(p_refs, land_refs, sems):
        send_sems, recv_sems, local_sems = sems
        x, y, c = _mesh_pos()
        my_chip = 2 * x + y
        chips = [(1 - x, y), (x, 1 - y), (1 - x, 1 - y)]
        local = [pltpu.make_async_copy(p_refs[a].at[my_chip], land_refs[a].at[my_chip], local_sems.at[a])
                 for a in range(n_arr)]

        def copy(a, k, src_slot, dst_slot, px, py):
            return pltpu.make_async_remote_copy(
                src_ref=p_refs[a].at[src_slot], dst_ref=land_refs[a].at[dst_slot],
                send_sem=send_sems.at[a, k], recv_sem=recv_sems.at[a, k],
                device_id=(px, py, c), device_id_type=pl.DeviceIdType.MESH)

        sends = [copy(a, k, 2 * px + py, my_chip, px, py) for a in range(n_arr) for k, (px, py) in enumerate(chips)]
        arrivals = [copy(a, k, my_chip, 2 * px + py, px, py) for a in range(n_arr)
                    for k, (px, py) in enumerate(chips)]
        return local, sends, arrivals

    def start(p_refs, land_refs, sems):
        local, sends, _ = plan(p_refs, land_refs, sems)
        for cp in local + sends:
            cp.start()

    def finish(p_refs, land_refs, sems):
        local, sends, arrivals = plan(p_refs, land_refs, sems)
        for cp in arrivals:
            cp.wait_recv()
        for cp in sends:
            cp.wait_send()
        for cp in local:
            cp.wait()

    return types.SimpleNamespace(
        inputs=list(parts), start=start, middle=_no_round, finish=finish,
        out_shapes=[jax.ShapeDtypeStruct(p.shape, p.dtype) for p in parts],
        sem_shapes=[pltpu.SemaphoreType.DMA((n_arr, 3)), pltpu.SemaphoreType.DMA((n_arr, 3)),
                    pltpu.SemaphoreType.DMA((n_arr,))])


_HBM = pl.BlockSpec(memory_space=pltpu.HBM)
_SEM = pl.BlockSpec(memory_space=pltpu.SEMAPHORE)
_DATAFLOW = pltpu.SideEffectType.DATAFLOW_SIDE_EFFECTING


def _chip_copies(p_refs, land_refs, send_sems, recv_sems):
    x, y, c = _mesh_pos()
    my_chip = 2 * x + y
    chips = [(1 - x, y), (x, 1 - y), (1 - x, 1 - y)]
    return [pltpu.make_async_remote_copy(
        src_ref=p_refs[a].at[2 * px + py], dst_ref=land_refs[a].at[my_chip],
        send_sem=send_sems[3 * a + k], recv_sem=recv_sems[3 * a + k],
        device_id=(px, py, c), device_id_type=pl.DeviceIdType.MESH)
        for a in range(len(p_refs)) for k, (px, py) in enumerate(chips)]


def _chip_exchange_begin(parts, name):
    n_arr = len(parts)
    n_buf, n_copy = 2 * n_arr, 3 * n_arr
    lands = [lax.empty(p.shape, p.dtype) for p in parts]

    def body(*refs):
        p_refs, land_refs = refs[:n_arr], refs[n_arr:n_buf]
        send_sems, recv_sems, token = refs[n_buf:n_buf + n_copy], refs[n_buf + n_copy:n_buf + 2 * n_copy], refs[-1]
        for cp in _chip_copies(p_refs, land_refs, send_sems, recv_sems):
            cp.start()
        token[...] = jnp.zeros_like(token)

    bufs = list(parts) + list(lands)
    outs = pl.pallas_call(
        body, name=name,
        out_shape=(*[pltpu.SemaphoreType.DMA(())] * (2 * n_copy), *[pltpu.HBM(b.shape, b.dtype) for b in bufs],
                   jax.ShapeDtypeStruct((8, 128), F32)),
        in_specs=[_HBM] * n_buf,
        out_specs=(*[_SEM] * (2 * n_copy), *[_HBM] * n_buf, pl.BlockSpec(memory_space=pltpu.VMEM)),
        input_output_aliases={i: 2 * n_copy + i for i in range(n_buf)},
        compiler_params=pltpu.CompilerParams(has_side_effects=_DATAFLOW),
    )(*[pltpu.with_memory_space_constraint(b, pltpu.HBM) for b in bufs])
    sems = list(outs[:2 * n_copy])
    thru = list(outs[2 * n_copy:2 * n_copy + n_buf])
    return types.SimpleNamespace(send_sems=sems[:n_copy], recv_sems=sems[n_copy:], parts=thru[:n_arr],
                                 lands=thru[n_arr:], token=outs[-1])


def _chip_exchange_end(flight, after, name):
    send_sems, recv_sems, parts, lands = flight.send_sems, flight.recv_sems, flight.parts, flight.lands
    n_arr = len(parts)
    n_buf, n_copy = 2 * n_arr, 3 * n_arr

    def body(*refs):
        p_refs, land_refs = refs[:n_arr], refs[n_arr:n_buf]
        sems = refs[n_buf:n_buf + 2 * n_copy]
        for cp in _chip_copies(p_refs, land_refs, sems[:n_copy], sems[n_copy:]):
            cp.wait_send()
            cp.wait_recv()

    bufs = list(parts) + list(lands)
    outs = pl.pallas_call(
        body, name=name, out_shape=tuple(pltpu.HBM(b.shape, b.dtype) for b in bufs),
        in_specs=[_HBM] * n_buf + [_SEM] * (2 * n_copy) + [_ANY], out_specs=tuple([_HBM] * n_buf),
        input_output_aliases={i: i for i in range(n_buf)},
        compiler_params=pltpu.CompilerParams(has_side_effects=_DATAFLOW),
    )(*bufs, *send_sems, *recv_sems, after)
    return list(outs[:n_arr]), list(outs[n_arr:])


def _row_tile(r, cap=640):
    best = None
    for cand in range(16, min(r, cap) + 1, 16):
        if r % cand == 0:
            best = cand
    return best if best is not None else r


def _pair_sum(gs, landeds, core, name):
    tiles = [_row_tile(g.shape[1]) for g in gs]
    counts = [g.shape[1] // tr for g, tr in zip(gs, tiles)]
    n_arr = len(gs)

    def body(core_ref, *refs):
        for a in range(n_arr):
            mine, theirs, out = refs[2 * a], refs[2 * a + 1], refs[2 * n_arr + a]
            out[0] = (mine[0].astype(F32) + theirs[0].astype(F32)).astype(out.dtype)

    in_specs, out_specs, operands = [], [], []
    for g, landed, tr, count in zip(gs, landeds, tiles, counts):
        c_dim = g.shape[2]
        last = count - 1
        in_specs += [pl.BlockSpec((1, tr, c_dim),
                                  lambda k, i, core_ref, last=last: (2 * k + core_ref[0], jnp.minimum(i, last), 0)),
                     pl.BlockSpec((1, tr, c_dim), lambda k, i, core_ref, last=last: (k, jnp.minimum(i, last), 0))]
        out_specs.append(pl.BlockSpec((1, tr, c_dim), lambda k, i, core_ref, last=last: (k, jnp.minimum(i, last), 0)))
        operands += [g, landed]
    return list(pl.pallas_call(
        body, name=name,
        grid_spec=pltpu.PrefetchScalarGridSpec(
            num_scalar_prefetch=1, grid=(N_CHIP, max(counts)), in_specs=in_specs, out_specs=out_specs),
        out_shape=[jax.ShapeDtypeStruct((N_CHIP,) + g.shape[1:], g.dtype) for g in gs],
        compiler_params=_cparams(2),
    )(core, *operands))


def _sum_slots(parts, name):
    n, r, c_dim = parts.shape
    tr = _row_tile(r)

    def body(p_ref, o_ref):
        acc = p_ref[0].astype(F32)
        for k in range(1, n):
            acc = acc + p_ref[k].astype(F32)
        o_ref[...] = acc

    return pl.pallas_call(
        body, grid=(r // tr,), name=name,
        in_specs=[pl.BlockSpec((n, tr, c_dim), lambda i: (0, i, 0))],
        out_specs=_row_spec(tr, c_dim),
        out_shape=jax.ShapeDtypeStruct((r, c_dim), F32),
        compiler_params=_cparams(),
    )(parts)


GAINS = ("g_ffn1", "g_mix", "g_cross", "g_mem", "g_ffn2", "g_final")
SMALL = GAINS + ("b_gate", "conv_w")
SMALL_R = 16
LOSS_ROW = 11
WEIGHT_ORDER = ("g_ffn1", "w_ffn1_gu", "w_ffn1_down", "g_mix", "w_in", "b_gate", "conv_w", "w_conv_out",
                "w_attn_out", "w_o", "g_cross", "g_mem", "w_cq", "w_ckv", "w_co", "g_ffn2", "w_ffn2_gu",
                "w_ffn2_down", "g_final")
GU_NAMES = ("w_ffn1_gu", "w_ffn2_gu")


def _pack_small(vals, conv_rows):
    rows = [vals[n].reshape(1, D) for n in GAINS] + [vals["b_gate"].reshape(2, D), conv_rows.reshape(CONV_K, D)]
    used = len(GAINS) + 2 + CONV_K
    return jnp.concatenate(rows + [jnp.zeros((SMALL_R - used, D), F32)], axis=0)


def _unpack_small(buf):
    out = {n: buf[k] for k, n in enumerate(GAINS)}
    out["b_gate"] = buf[6:8].reshape(2 * D)
    out["conv_w"] = buf[8:8 + CONV_K]
    return out


def _exchange_shards(wts):
    out = {n: jnp.pad(wts[n].T.astype(BF16), ((0, FF_PAD - FF_BLK), (0, 0))) for n in GU_NAMES}
    for n in ("w_ckv", "w_in", "w_ffn1_down", "w_ffn2_down"):
        out[n] = wts[n].astype(BF16)
    out["mix"] = jnp.concatenate([wts[n].astype(BF16) for n in MIX_MATS], axis=0)
    out["cross"] = jnp.concatenate([wts[n].astype(BF16) for n in CROSS_MATS], axis=0)
    return out


def _reduce_group(grads, landed, core, names):
    return _pair_sum(grads, landed, core, "grads_pair_sum_" + "_".join(names))


def _step(x, mem, target, sh, conv_pad, gains, b_gate, core):
    wg1, wd1, conv_all = _run_exchange(_gather_exchange([sh["w_ffn1_gu"], sh["w_ffn1_down"], conv_pad]), "gather_ffn1")
    conv_w = conv_all[:, :CONV_K, :].transpose(1, 0, 2).reshape(CONV_K, D)
    (n1, gate1, up1, act1, h1), (w_in,) = _ffn_fwd(
        x, gains["g_ffn1"], wg1, wd1, "ffn1_fwd", comm=_gather_exchange([sh["w_in"]]))
    (u, pcg, qkv, yc), (w_mix,) = _inproj_fwd(h1, gains["g_mix"], w_in, conv_w, "inproj_fwd",
                                              comm=_gather_exchange([sh["mix"]]))
    (ysb, ctot), (w_cross, w_ckv, wg2) = _sb_fwd(
        qkv, "sb_fwd", comm=_gather_exchange([sh["cross"], sh["w_ckv"], sh["w_ffn2_gu"]]))
    (a_mix, b_mix, merged, h2), (wd2,) = _mix_out_fwd(yc, ysb, pcg, b_gate, h1, w_mix, "mix_out_fwd",
                                                      comm=_gather_exchange([sh["w_ffn2_down"]]))
    hn, qx, o_x, h3, mn, kv = _cross_fwd(h2, gains["g_cross"], mem, gains["g_mem"], w_ckv, w_cross, "cross_fwd")
    (n4, gate2, up2, act2, dh4, loss, dg_final), _ = _ffn_fwd(h3, gains["g_ffn2"], wg2, wd2, "ffn2_fwd",
                                                              head=(gains["g_final"], target))

    gs = {"g_final": dg_final}
    (dgu2, dh4b, dh3, gs["g_ffn2"]), _ = _ffn_bwd(dh4, h3, gains["g_ffn2"], gate2, up2, wg2, wd2, "ffn2_bwd")
    grads_a = [_mm_tn_rows(dgu2, n4, FF_PAD, "dw_ffn2_gu"),
               _mm_tn_rows(act2, dh4b, FF_BLK, "dw_ffn2_down").reshape(N_DEV, DOWN_ROWS, D)]
    names_a = ["w_ffn2_gu", "w_ffn2_down"]
    (dh3b, dqx, dkv, dh2, gs["g_cross"], gs["g_mem"]), landed_a = _cross_bwd(
        dh3, h2, gains["g_cross"], qx, kv, mem, gains["g_mem"], w_ckv, w_cross, "cross_bwd",
        comm=_pair_exchange(grads_a))
    sums_a = _reduce_group(grads_a, landed_a, core, names_a)
    cross_stack = _mm_tn_square(hn, dqx, "dw_cq", 0, len(CROSS_MATS))
    grads_b = [_mm_tn_cols(mn, dkv, "dw_ckv"), _mm_tn_square(o_x, dh3b, "dw_co", 1, len(CROSS_MATS), cross_stack)]
    names_b = ["w_ckv", "cross"]
    (dh2b, da_mix, db_mix, dgp, dconv, dysb, gs["b_gate"], gs["conv_w"]), landed_b = _mix_out_bwd(
        dh2, a_mix, b_mix, pcg, b_gate, conv_w, w_mix, "mix_out_bwd", comm=_pair_exchange(grads_b))
    sums_b = _reduce_group(grads_b, landed_b, core, names_b)
    mix_stack = _mm_tn_square(yc, da_mix, "dw_conv_out", 0, len(MIX_MATS))
    mix_stack = _mm_tn_square(ysb, db_mix, "dw_attn_out", 1, len(MIX_MATS), mix_stack)
    grads_c = [_mm_tn_square(merged, dh2b, "dw_o", 2, len(MIX_MATS), mix_stack)]
    flight_ab = _chip_exchange_begin(sums_a + sums_b, "grads_to_chips_early_begin")
    (dq, dkv_sb), landed_c = _sb_bwd(qkv, dysb, ctot, flight_ab.token, "sb_bwd", comm=_pair_exchange(grads_c))
    sums_c = _reduce_group(grads_c, landed_c, core, ["mix"])
    w_in_stack = _mm_tn_cols(u, dconv, "dw_in_conv", 0, N_DEV)
    w_in_stack = _mm_tn_cols(u, dq[None], "dw_in_q", 3, N_DEV, w_in_stack)
    w_in_stack = _mm_tn_cols(u, dkv_sb, "dw_in_kv", 4, N_DEV, w_in_stack)
    grads_d = [_mm_tn_cols(u, dgp, "dw_in_gates", 6, N_DEV, w_in_stack)]
    (dh1, gs["g_mix"]), landed_d = _inproj_bwd(dconv, dq, dkv_sb, dgp, w_in, h1, gains["g_mix"], dh2, "inproj_bwd",
                                               comm=_pair_exchange(grads_d))
    sums_d = _reduce_group(grads_d, landed_d, core, ["w_in"])
    flight_d = _chip_exchange_begin(sums_c + sums_d, "grads_to_chips_w_in_begin")
    (dgu1, dh1b, dx, gs["g_ffn1"]), _ = _ffn_bwd(dh1, x, gains["g_ffn1"] + flight_d.token[0, 0], gate1, up1, wg1, wd1,
                                                 "ffn1_bwd")
    dw_gu1 = _mm_tn_rows(dgu1, n1, FF_PAD, "dw_ffn1_gu")
    dw_down1, landed_gu1 = _mm_tn_rows(act1, dh1b, FF_BLK, "dw_ffn1_down", comm=_pair_exchange([dw_gu1]))
    grads_e = [dw_gu1, dw_down1.reshape(N_DEV, DOWN_ROWS, D)]
    names_e = ["w_ffn1_gu", "w_ffn1_down"]
    small_mine = _pack_small({n: gs[n] for n in GAINS + ("b_gate",)}, gs["conv_w"][:CONV_K])
    small_mine = small_mine.at[LOSS_ROW, 0].set(loss[0, 0])
    landed_down1, small_all = _run_exchange(
        _both(_pair_exchange(grads_e[1:]), _gather_exchange([small_mine])), "grads_to_sibling_ffn1_down")
    landed_e = landed_gu1 + [landed_down1]
    flight_e = _chip_exchange_begin(_reduce_group(grads_e, landed_e, core, names_e), "grads_to_chips_ffn1_begin")
    flights = [(names_a + names_b, flight_ab), (["mix", "w_in"], flight_d), (names_e, flight_e)]
    return dx, flights, small_all


def kernel(x, mem, g_ffn1, w_ffn1_gu, w_ffn1_down, g_mix, w_in, b_gate, conv_w, w_conv_out, w_attn_out, w_o, g_cross, g_mem, w_cq, w_ckv, w_co, g_ffn2, w_ffn2_gu, w_ffn2_down, g_final, loss_target, m_g_ffn1, m_w_ffn1_gu, m_w_ffn1_down, m_g_mix, m_w_in, m_b_gate, m_conv_w, m_w_conv_out, m_w_attn_out, m_w_o, m_g_cross, m_g_mem, m_w_cq, m_w_ckv, m_w_co, m_g_ffn2, m_w_ffn2_gu, m_w_ffn2_down, m_g_final, v_g_ffn1, v_w_ffn1_gu, v_w_ffn1_down, v_g_mix, v_w_in, v_b_gate, v_conv_w, v_w_conv_out, v_w_attn_out, v_w_o, v_g_cross, v_g_mem, v_w_cq, v_w_ckv, v_w_co, v_g_ffn2, v_w_ffn2_gu, v_w_ffn2_down, v_g_final):
    args = locals()
    wts = {n: args[n] for n in WEIGHT_ORDER}
    mom1 = {n: args["m_" + n] for n in WEIGHT_ORDER}
    mom2 = {n: args["v_" + n] for n in WEIGHT_ORDER}
    cx, cy, cc = _mesh_pos()
    dev = 4 * cx + 2 * cy + cc
    conv_cols = D // N_DEV

    conv_pad = jnp.concatenate([conv_w, jnp.zeros((SMALL_R - CONV_K, conv_cols), F32)], axis=0)
    gains = {n: wts[n].reshape(1, D) for n in GAINS}
    dx, flights, small_all = _step(x[0], mem[0], loss_target[0], _exchange_shards(wts), conv_pad, gains,
                                 b_gate.reshape(1, 2 * D), cc.reshape(1).astype(jnp.int32))

    grads, delta, new_m, new_v = {}, {}, {}, {}

    def operands(n, transposed):
        trio = (wts[n], mom1[n], mom2[n])
        return tuple(a.T for a in trio) if transposed else trio

    def record(n, res, transposed):
        grads[n], delta[n], new_m[n], new_v[n] = [r.T for r in res] if transposed else res

    early = [("w_ffn2_gu", "w_ffn2_gu", 0, True), ("w_ffn2_down", "w_ffn2_down", 0, False),
             ("w_ckv", "w_ckv", 0, False), ("w_in", "w_in", 0, False)]
    early += [(n, "mix", k, False) for k, n in enumerate(MIX_MATS)]
    early += [(n, "cross", k, False) for k, n in enumerate(CROSS_MATS)]
    chip = (2 * cx + cy).reshape(1).astype(jnp.int32)
    (names_early, flight_early), (names_w_in, flight_w_in), (last_names, flight_last) = flights
    token = flight_last.token
    own, land = {}, {}
    for names, flight, tag in ((names_early, flight_early, "early"), (names_w_in, flight_w_in, "w_in")):
        own_parts, landed = _chip_exchange_end(flight, token, "grads_to_chips_%s_end" % tag)
        own.update(zip(names, own_parts))
        land.update(zip(names, landed))
    for n, buf, row_block, transposed in early:
        w, m1, m2 = operands(n, transposed)
        record(n, _adamw_own(w, land[buf], own[buf], chip, m1, m2, "adamw_" + n, row_block, token), transposed)

    after = jnp.concatenate([new_v[n][:1, :1] for n, _, _, _ in early], axis=0)
    own_parts, landed = _chip_exchange_end(flight_last, after, "grads_to_chips_ffn1_end")
    for n, own_n, land_n, transposed in zip(last_names, own_parts, landed, (True, False)):
        w, m1, m2 = operands(n, transposed)
        record(n, _adamw_own(w, land_n, own_n, chip, m1, m2, "adamw_" + n), transposed)

    small_sum = _sum_slots(small_all, "small_grads_sum")
    loss = small_sum[LOSS_ROW, 0]
    grad_small = _unpack_small(small_sum)
    grad_small["conv_w"] = lax.dynamic_slice_in_dim(grad_small["conv_w"], dev * conv_cols, conv_cols, axis=1)
    grads.update(grad_small)

    def small_buf(vals):
        return _pack_small(vals, jnp.concatenate([vals["conv_w"], jnp.zeros((CONV_K, D - conv_cols), F32)], axis=1))

    _, d_s, m_s, v_s = _adamw(small_buf(wts), small_buf(grads)[None], small_buf(mom1), small_buf(mom2), "adamw_small")
    for res, buf in ((delta, d_s), (new_m, m_s), (new_v, v_s)):
        un = _unpack_small(buf)
        for n in GAINS + ("b_gate",):
            res[n] = un[n]
        res["conv_w"] = un["conv_w"][:, :conv_cols]

    return (loss, dx[None], *[grads[n] for n in WEIGHT_ORDER], *[delta[n] for n in WEIGHT_ORDER],
            *[new_m[n] for n in WEIGHT_ORDER], *[new_v[n] for n in WEIGHT_ORDER])
```

```python
import types

import jax
import jax.numpy as jnp
from jax import lax
from jax.experimental import pallas as pl
from jax.experimental.pallas import tpu as pltpu

F32 = jnp.float32
BF16 = jnp.bfloat16

D = 1024
DFF = 2816
SB_H = 8
SB_DH = 128
X_H = 4
X_DH = 256
CONV_K = 3
RMS_EPS = 1e-6
N_DEV = 8
N_CHIP = 4
SQ_ROWS = D // N_DEV

ADAM_LR = 0.001
ADAM_B1 = 0.9
ADAM_B2 = 0.999
ADAM_EPS = 1e-08
ADAM_WD = 0.01
ADAM_STEP = 10

TM = 256
TQ = 512
TK = 256
SB_HPS = 2
VMEM_LIMIT = 56 << 20

FF_BLK = DFF // 4
FF_PAD = 768
FF_SUB = 256
DOWN_ROWS = DFF // N_DEV

MIX_MATS = ("w_conv_out", "w_attn_out", "w_o")
CROSS_MATS = ("w_cq", "w_co")

_ANY = pl.BlockSpec(memory_space=pl.ANY)


def _cparams(n_axes=1):
    return pltpu.CompilerParams(
        dimension_semantics=("arbitrary",) * n_axes, vmem_limit_bytes=VMEM_LIMIT)


def _row_spec(tm, n):
    return pl.BlockSpec((tm, n), lambda i: (i, 0))


def _blk_row_spec(nb, tm, n):
    return pl.BlockSpec((nb, tm, n), lambda i: (0, i, 0))


def _const_spec(shape):
    zeros = (0,) * len(shape)
    return pl.BlockSpec(shape, lambda i: zeros)


def _dot(a, b):
    return jnp.dot(a, b, preferred_element_type=F32)


def _dot_nt(a, b):
    return lax.dot_general(a, b, (((1,), (1,)), ((), ())), preferred_element_type=F32)


def _dot_tn(a, b):
    return lax.dot_general(a, b, (((0,), (0,)), ((), ())), preferred_element_type=F32)


def _sigmoid(x):
    return 1.0 / (1.0 + jnp.exp(-x))


def _call(body, operands, *, grid, in_specs, out_specs, out_shape, scratch_shapes, name, comm=None):
    n_in, n_out, n_sc = len(in_specs), len(out_specs), len(scratch_shapes)
    if comm is None:
        outs = pl.pallas_call(
            body, grid=grid, name=name, in_specs=in_specs, out_specs=out_specs, out_shape=out_shape,
            scratch_shapes=scratch_shapes, compiler_params=_cparams(len(grid)))(*operands)
        return list(outs), []
    c_in, c_out, c_sem = len(comm.inputs), len(comm.out_shapes), len(comm.sem_shapes)

    def hosted(*refs):
        bounds = [0, n_in, c_in, n_out, c_out, n_sc, c_sem]
        parts, pos = [], 0
        for k in bounds[1:]:
            parts.append(refs[pos:pos + k])
            pos += k
        ins, cins, outs, couts, scr, sems = parts
        step, n_steps = pl.program_id(0), grid[0]
        for ax in range(1, len(grid)):
            step, n_steps = step * grid[ax] + pl.program_id(ax), n_steps * grid[ax]

        @pl.when(step == 0)
        def _():
            comm.start(cins, couts, sems)

        @pl.when(step == (2 * n_steps) // 3)
        def _():
            comm.middle(cins, couts, sems)

        body(*ins, *outs, *scr)

        @pl.when(step == n_steps - 1)
        def _():
            comm.finish(cins, couts, sems)

    res = pl.pallas_call(
        hosted, grid=grid, name=name, in_specs=list(in_specs) + [_ANY] * c_in,
        out_specs=list(out_specs) + [_ANY] * c_out, out_shape=list(out_shape) + list(comm.out_shapes),
        scratch_shapes=list(scratch_shapes) + list(comm.sem_shapes),
        compiler_params=_cparams(len(grid)))(*operands, *comm.inputs)
    return list(res[:n_out]), list(res[n_out:])


def _load_resident(step, pairs, sems):
    @pl.when(step == 0)
    def _():
        copies = [pltpu.make_async_copy(src, dst, sems.at[k]) for k, (src, dst) in enumerate(pairs)]
        for cp in copies:
            cp.start()
        for cp in copies:
            cp.wait()


def _square_pairs(buf_hbm, index, dst):
    off = index * SQ_ROWS
    return [(buf_hbm.at[d, off:off + SQ_ROWS, :], dst.at[d * SQ_ROWS:(d + 1) * SQ_ROWS, :]) for d in range(N_DEV)]


def _down_pairs(wd_hbm, dst):
    return [(wd_hbm.at[d], dst.at[d // 2, (d % 2) * DOWN_ROWS:(d % 2 + 1) * DOWN_ROWS, :]) for d in range(N_DEV)]


def _zero_down_pad(step, dst):
    @pl.when(step == 0)
    def _():
        dst[:, FF_BLK:, :] = jnp.zeros((4, FF_PAD - FF_BLK, D), BF16)


def _rms_fwd_tile(xt, g):
    r = lax.rsqrt(jnp.mean(xt * xt, axis=-1, keepdims=True) + RMS_EPS)
    return (xt * r) * g


def _rms_bwd_tile(xt, g, dn):
    r = lax.rsqrt(jnp.mean(xt * xt, axis=-1, keepdims=True) + RMS_EPS)
    xhat = xt * r
    dxhat = dn * g
    dx = r * (dxhat - xhat * jnp.mean(dxhat * xhat, axis=-1, keepdims=True))
    dg = jnp.sum(dn * xhat, axis=0, keepdims=True)
    return dx, dg


def _accumulate(ref, step, value):
    @pl.when(step == 0)
    def _():
        ref[...] = value

    @pl.when(step != 0)
    def _():
        ref[...] = ref[...] + value


def _ffn_fwd(x, g, wgu, wd, name, comm=None, head=None):
    t = x.shape[0]

    def body(x_ref, g_ref, wgu_hbm, wd_hbm, *refs):
        if head is None:
            n_ref, gate_ref, up_ref, act_ref, h_ref, wgu_v, wd_v, sems = refs
        else:
            gf_ref, t_ref, n_ref, gate_ref, up_ref, act_ref, dh_ref, loss_ref, dgf_ref, wgu_v, wd_v, sems = refs
        step = pl.program_id(0)
        _zero_down_pad(step, wd_v)
        _load_resident(step, [(wgu_hbm, wgu_v)] + _down_pairs(wd_hbm, wd_v), sems)
        xt = x_ref[...]
        n = _rms_fwd_tile(xt, g_ref[...]).astype(BF16)
        n_ref[...] = n
        acc = jnp.zeros((TM, D), F32)
        for j in range(4):
            for s in range(FF_PAD // FF_SUB):
                lo, hi = s * FF_SUB, (s + 1) * FF_SUB
                gt = _dot_nt(n, wgu_v[j, lo:hi, :])
                ut = _dot_nt(n, wgu_v[4 + j, lo:hi, :])
                gate_ref[j, :, lo:hi] = gt.astype(BF16)
                up_ref[j, :, lo:hi] = ut.astype(BF16)
                act_ref[j, :, lo:hi] = ((gt * _sigmoid(gt)) * ut).astype(BF16)
            acc = acc + _dot(act_ref[j], wd_v[j])
        ht = xt + 0.5 * acc
        if head is None:
            h_ref[...] = ht
        else:
            gain = gf_ref[...]
            diff = _rms_fwd_tile(ht, gain) - t_ref[...]
            part = 0.5 * jnp.sum(jnp.sum(diff * diff, axis=-1, keepdims=True) / D, axis=0, keepdims=True)
            dx, dg = _rms_bwd_tile(ht, gain, diff / D)
            dh_ref[...] = dx
            _accumulate(loss_ref, step, jnp.broadcast_to(part, (8, 128)))
            _accumulate(dgf_ref, step, dg)

    ff = jax.ShapeDtypeStruct((4, t, FF_PAD), BF16)
    operands, in_specs = (x, g, wgu, wd), [_row_spec(TM, D), _const_spec((1, D)), _ANY, _ANY]
    out_specs = [_row_spec(TM, D)] + [_blk_row_spec(4, TM, FF_PAD)] * 3 + [_row_spec(TM, D)]
    out_shape = [jax.ShapeDtypeStruct((t, D), BF16), ff, ff, ff, jax.ShapeDtypeStruct((t, D), F32)]
    if head is not None:
        operands += tuple(head)
        in_specs += [_const_spec((1, D)), _row_spec(TM, D)]
        out_specs += [_const_spec((8, 128)), _const_spec((1, D))]
        out_shape += [jax.ShapeDtypeStruct((8, 128), F32), jax.ShapeDtypeStruct((1, D), F32)]
    return _call(
        body, operands, grid=(t // TM,), name=name, comm=comm, in_specs=in_specs, out_specs=out_specs,
        out_shape=out_shape,
        scratch_shapes=[pltpu.VMEM((N_DEV, FF_PAD, D), BF16), pltpu.VMEM((4, FF_PAD, D), BF16),
                        pltpu.SemaphoreType.DMA((1 + N_DEV,))])


def _ffn_bwd(dh, xin, g, gate, up, wgu, wd, name, comm=None):
    t = dh.shape[0]

    def body(dh_ref, x_ref, g_ref, gate_ref, up_ref, wgu_hbm, wd_hbm,
             dgu_ref, dhb_ref, dx_ref, dg_ref, wgu_v, wd_v, sems):
        step = pl.program_id(0)
        _zero_down_pad(step, wd_v)
        _load_resident(step, [(wgu_hbm, wgu_v)] + _down_pairs(wd_hbm, wd_v), sems)
        dht = dh_ref[...]
        dhb = (0.5 * dht).astype(BF16)
        dhb_ref[...] = dhb
        dn = jnp.zeros((TM, D), F32)
        for j in range(4):
            for s in range(FF_PAD // FF_SUB):
                lo, hi = s * FF_SUB, (s + 1) * FF_SUB
                da = _dot_nt(dhb, wd_v[j, lo:hi, :])
                gt = gate_ref[j, :, lo:hi].astype(F32)
                ut = up_ref[j, :, lo:hi].astype(F32)
                sg = _sigmoid(gt)
                dgt = (da * ut * (sg * (1.0 + gt * (1.0 - sg)))).astype(BF16)
                dut = (da * (gt * sg)).astype(BF16)
                dgu_ref[j, :, lo:hi] = dgt
                dgu_ref[4 + j, :, lo:hi] = dut
            dn = dn + _dot(dgu_ref[j], wgu_v[j]) + _dot(dgu_ref[4 + j], wgu_v[4 + j])
        dx, dg = _rms_bwd_tile(x_ref[...], g_ref[...], dn)
        dx_ref[...] = dht + dx
        _accumulate(dg_ref, step, dg)

    return _call(
        body, (dh, xin, g, gate, up, wgu, wd), grid=(t // TM,), name=name, comm=comm,
        in_specs=[_row_spec(TM, D), _row_spec(TM, D), _const_spec((1, D)), _blk_row_spec(4, TM, FF_PAD),
                  _blk_row_spec(4, TM, FF_PAD), _ANY, _ANY],
        out_specs=[_blk_row_spec(N_DEV, TM, FF_PAD), _row_spec(TM, D), _row_spec(TM, D), _const_spec((1, D))],
        out_shape=[jax.ShapeDtypeStruct((N_DEV, t, FF_PAD), BF16), jax.ShapeDtypeStruct((t, D), BF16),
                   jax.ShapeDtypeStruct((t, D), F32), jax.ShapeDtypeStruct((1, D), F32)],
        scratch_shapes=[pltpu.VMEM((N_DEV, FF_PAD, D), BF16), pltpu.VMEM((4, FF_PAD, D), BF16),
                        pltpu.SemaphoreType.DMA((1 + N_DEV,))])


WIDE_TILES = (1024, 512, 256, 128)


def _pick_tile(n, options=(512, 256, 128)):
    for o in options:
        if n % o == 0:
            return o
    return n


def _into(stack, n_operands):
    if stack is None:
        return (), [], {}
    return (stack,), [_ANY], {n_operands: 0}


def _mm_tn_square(a, b, name, index, count, stack=None):
    k, m = a.shape
    _, n = b.shape
    tn = _pick_tile(n)
    extra, extra_specs, aliases = _into(stack, 2)

    def body(a_ref, b_ref, *rest):
        rest[-1][...] = _dot_tn(a_ref[...], b_ref[...]).astype(BF16).reshape(N_DEV, m // N_DEV, tn)

    return pl.pallas_call(
        body, grid=(n // tn,), name=name,
        in_specs=[pl.BlockSpec((k, m), lambda j: (0, 0)), pl.BlockSpec((k, tn), lambda j: (0, j))] + extra_specs,
        out_specs=pl.BlockSpec((N_DEV, m // N_DEV, tn), lambda j: (0, index, j)),
        out_shape=jax.ShapeDtypeStruct((N_DEV, count * (m // N_DEV), n), BF16),
        input_output_aliases=aliases, compiler_params=_cparams(1),
    )(a, b, *extra)


def _mm_tn_cols(a, b, name, first=0, count=None, stack=None):
    k, m = a.shape
    nb, _, n = b.shape
    tm = _pick_tile(m, WIDE_TILES)
    extra, extra_specs, aliases = _into(stack, 2)

    def body(a_ref, b_ref, *rest):
        rest[-1][0] = _dot_tn(a_ref[...].astype(BF16), b_ref[0].astype(BF16)).astype(BF16)

    return pl.pallas_call(
        body, grid=(nb, m // tm), name=name,
        in_specs=[pl.BlockSpec((k, tm), lambda j, i: (0, i)), pl.BlockSpec((1, k, n), lambda j, i: (j, 0, 0))]
                 + extra_specs,
        out_specs=pl.BlockSpec((1, tm, n), lambda j, i: (j + first, i, 0)),
        out_shape=jax.ShapeDtypeStruct((nb if count is None else count, m, n), BF16),
        input_output_aliases=aliases, compiler_params=_cparams(2),
    )(a, b, *extra)


def _mm_tn_rows(a, b, keep, name, comm=None):
    nb, k, m = a.shape
    _, n = b.shape
    tn = _pick_tile(n, WIDE_TILES)

    def body(a_ref, b_ref, o_ref):
        o_ref[0] = _dot_tn(a_ref[0], b_ref[...])[:keep].astype(BF16)

    (out,), couts = _call(
        body, (a, b), grid=(nb, n // tn), name=name, comm=comm,
        in_specs=[pl.BlockSpec((1, k, m), lambda j, i: (j, 0, 0)), pl.BlockSpec((k, tn), lambda j, i: (0, i))],
        out_specs=[pl.BlockSpec((1, keep, tn), lambda j, i: (j, 0, i))],
        out_shape=[jax.ShapeDtypeStruct((nb, keep, n), BF16)], scratch_shapes=[])
    return out if comm is None else (out, couts)


PCG_W = 5 * D
QKV_W = 3 * D
PROJ_SUB = 512


def _inproj_fwd(h, g, w_in, conv_w, name, comm=None):
    t = h.shape[0]

    def body(h_ref, g_ref, w_hbm, cw_ref, u_ref, pcg_ref, qkv_ref, yc_ref, w_v, tail_v, sems):
        step = pl.program_id(0)
        _load_resident(step, [(w_hbm, w_v)], sems)

        @pl.when(step == 0)
        def _():
            tail_v[...] = jnp.zeros_like(tail_v)

        u = _rms_fwd_tile(h_ref[...], g_ref[...]).astype(BF16)
        u_ref[...] = u
        for blk in range(N_DEV):
            for s in range(D // PROJ_SUB):
                lo, hi = s * PROJ_SUB, (s + 1) * PROJ_SUB
                p = _dot(u, w_v[blk, :, lo:hi])
                if blk < 3:
                    pcg_ref[:, blk * D + lo:blk * D + hi] = p
                elif blk < 6:
                    qkv_ref[:, (blk - 3) * D + lo:(blk - 3) * D + hi] = p.astype(BF16)
                else:
                    pcg_ref[:, (blk - 3) * D + lo:(blk - 3) * D + hi] = p
        xc = pcg_ref[:, D:2 * D] * pcg_ref[:, 2 * D:3 * D]
        ext = jnp.concatenate([tail_v[...], xc], axis=0)
        conv = (cw_ref[0:1, :] * pltpu.roll(ext, 2, 0)[8:] + cw_ref[1:2, :] * pltpu.roll(ext, 1, 0)[8:]
                + cw_ref[2:3, :] * xc)
        yc_ref[...] = (pcg_ref[:, 0:D] * conv).astype(BF16)
        tail_v[...] = xc[TM - 8:]

    return _call(
        body, (h, g, w_in, conv_w), grid=(t // TM,), name=name, comm=comm,
        in_specs=[_row_spec(TM, D), _const_spec((1, D)), _ANY, _const_spec((CONV_K, D))],
        out_specs=[_row_spec(TM, D), _row_spec(TM, PCG_W), _row_spec(TM, QKV_W), _row_spec(TM, D)],
        out_shape=[jax.ShapeDtypeStruct((t, D), BF16), jax.ShapeDtypeStruct((t, PCG_W), F32),
                   jax.ShapeDtypeStruct((t, QKV_W), BF16), jax.ShapeDtypeStruct((t, D), BF16)],
        scratch_shapes=[pltpu.VMEM((N_DEV, D, D), BF16), pltpu.VMEM((8, D), F32), pltpu.SemaphoreType.DMA((1,))])


def _tri2(cond):
    rr = lax.broadcasted_iota(jnp.int32, (2 * TK, TK), 0) & (TK - 1)
    cc = lax.broadcasted_iota(jnp.int32, (2 * TK, TK), 1)
    return cond(rr, cc).astype(BF16)


def _causal(shift, row0=0):
    rr = lax.broadcasted_iota(jnp.int32, (TQ - row0, TK), 0) + row0
    cc = lax.broadcasted_iota(jnp.int32, (TQ - row0, TK), 1)
    return cc + shift < rr


def _cumdot(v, tri2):
    hi = v.astype(BF16)
    lo = (v - hi.astype(F32)).astype(BF16)
    return _dot(jnp.concatenate([hi, lo], axis=1), tri2)


def _log_1m_beta(z):
    return -(jnp.maximum(z, 0.0) + jnp.log(1.0 + jnp.exp(-jnp.abs(z))))


def _sb_specs(t):
    g = SB_H // SB_HPS
    w = SB_HPS * SB_DH
    q_spec = pl.BlockSpec((TQ, w), lambda h, i: (i, h))
    k_spec = pl.BlockSpec((t, w), lambda h, i: (0, g + h))
    v_spec = pl.BlockSpec((t, w), lambda h, i: (0, 2 * g + h))
    ct_spec = pl.BlockSpec((SB_HPS, TQ, 1), lambda h, i: (h, i, 0))
    return g, w, q_spec, k_spec, v_spec, ct_spec


def _sb_fwd(qkv, name, comm=None):
    t = qkv.shape[0]
    scale = SB_DH ** -0.5
    g, w, q_spec, k_spec, v_spec, ct_spec = _sb_specs(t)

    def body(q_ref, k_ref, v_ref, y_ref, ct_ref):
        i = pl.program_id(1)
        later = _tri2(lambda j, s: j > s)
        n_diag = TQ // TK

        def block(j, carry, shift):
            off = pl.multiple_of(j * TK, TK)
            zs, ms = [], []
            for hd in range(SB_HPS):
                cols = slice(hd * SB_DH, (hd + 1) * SB_DH)
                z = _dot_nt(q_ref[:, cols], k_ref[pl.ds(off, TK), cols]) * scale
                m = _log_1m_beta(z)
                if shift is not None:
                    m = jnp.where(_causal(shift), m, 0.0)
                zs.append(z)
                ms.append(m)
            after = _cumdot(jnp.concatenate(ms, axis=0), later)
            out = []
            for hd in range(SB_HPS):
                acc, c_sum = carry[hd]
                cols = slice(hd * SB_DH, (hd + 1) * SB_DH)
                a = jnp.exp((ms[hd] + zs[hd]) + (c_sum + after[hd * TQ:(hd + 1) * TQ]))
                if shift is not None:
                    a = jnp.where(_causal(shift), a, 0.0)
                out.append((acc + _dot(a.astype(BF16), v_ref[pl.ds(off, TK), cols]),
                            c_sum + jnp.sum(ms[hd], axis=1, keepdims=True)))
            return tuple(out)

        carry = tuple((jnp.zeros((TQ, SB_DH), F32), jnp.zeros((TQ, 1), F32)) for _ in range(SB_HPS))
        for d in reversed(range(n_diag)):
            carry = block(i * n_diag + d, carry, d * TK)
        carry = lax.fori_loop(0, i * n_diag, lambda jj, c: block(i * n_diag - 1 - jj, c, None), carry)
        for hd in range(SB_HPS):
            y_ref[:, hd * SB_DH:(hd + 1) * SB_DH] = carry[hd][0].astype(BF16)
            ct_ref[hd] = carry[hd][1]

    return _call(
        body, (qkv, qkv, qkv), grid=(g, t // TQ), name=name, comm=comm,
        in_specs=[q_spec, k_spec, v_spec],
        out_specs=[q_spec, ct_spec],
        out_shape=[jax.ShapeDtypeStruct((t, D), BF16), jax.ShapeDtypeStruct((SB_H, t, 1), F32)],
        scratch_shapes=[])


def _sb_bwd(qkv, dy, ctot, after, name, comm=None):
    t = qkv.shape[0]
    scale = SB_DH ** -0.5
    g, w, q_spec, k_spec, v_spec, ct_spec = _sb_specs(t)
    acc_spec = pl.BlockSpec((2, t, w), lambda h, i: (0, 0, h))

    def body(q_ref, k_ref, v_ref, dy_ref, ct_ref, after_ref, dq_ref, dkv_ref):
        i = pl.program_id(1)

        @pl.when(i == 0)
        def _():
            dkv_ref[...] = jnp.zeros_like(dkv_ref)

        upto = _tri2(lambda j, s: j <= s)
        n_diag = TQ // TK

        def block(j, carry, shift):
            off = pl.multiple_of(j * TK, TK)
            r0 = 0 if shift is None else shift
            nr = TQ - r0
            causal = None if shift is None else _causal(shift, r0)

            def grow(old, delta):
                return old + delta if r0 == 0 else jnp.concatenate([old[:r0], old[r0:] + delta], axis=0)

            zs, ms = [], []
            for hd in range(SB_HPS):
                cols = slice(hd * SB_DH, (hd + 1) * SB_DH)
                z = _dot_nt(q_ref[r0:, cols], k_ref[pl.ds(off, TK), cols]) * scale
                m = _log_1m_beta(z)
                if causal is not None:
                    m = jnp.where(causal, m, 0.0)
                zs.append(z)
                ms.append(m)
            m_upto = _cumdot(jnp.concatenate(ms, axis=0), upto)
            ls, a_s, es = [], [], []
            for hd in range(SB_HPS):
                cols = slice(hd * SB_DH, (hd + 1) * SB_DH)
                l = ms[hd] + zs[hd]
                a = jnp.exp(l + ((ct_ref[hd, r0:] - carry[hd][1][r0:]) - m_upto[hd * nr:(hd + 1) * nr]))
                if causal is not None:
                    a = jnp.where(causal, a, 0.0)
                ls.append(l)
                a_s.append(a)
                es.append(_dot_nt(dy_ref[r0:, cols], v_ref[pl.ds(off, TK), cols]) * a)
            e_upto = _dot(jnp.concatenate(es, axis=0).astype(BF16), upto[:TK])
            out = []
            for hd in range(SB_HPS):
                dq, p_sum, e_sum = carry[hd]
                cols = slice(hd * SB_DH, (hd + 1) * SB_DH)
                e = es[hd]
                dz = e - jnp.exp(ls[hd]) * (e_sum[r0:] + e_upto[hd * nr:(hd + 1) * nr])
                if causal is not None:
                    dz = jnp.where(causal, dz, 0.0)
                dzs = (dz * scale).astype(BF16)
                dkv_ref[0, pl.ds(off, TK), cols] += _dot_tn(dzs, q_ref[r0:, cols])
                dkv_ref[1, pl.ds(off, TK), cols] += _dot_tn(a_s[hd].astype(BF16), dy_ref[r0:, cols])
                out.append((grow(dq, _dot(dzs, k_ref[pl.ds(off, TK), cols])),
                            grow(p_sum, jnp.sum(ms[hd], axis=1, keepdims=True)),
                            grow(e_sum, jnp.sum(e, axis=1, keepdims=True))))
            return tuple(out)

        zero = jnp.zeros((TQ, 1), F32)
        init = tuple((jnp.zeros((TQ, SB_DH), F32), zero, zero) for _ in range(SB_HPS))
        carry = lax.fori_loop(0, i * n_diag, lambda j, c: block(j, c, None), init)
        for d in range(n_diag):
            carry = block(i * n_diag + d, carry, d * TK)
        for hd in range(SB_HPS):
            dq_ref[:, hd * SB_DH:(hd + 1) * SB_DH] = carry[hd][0].astype(BF16)

    return _call(
        body, (qkv, qkv, qkv, dy, ctot, after), grid=(g, t // TQ), name=name, comm=comm,
        in_specs=[q_spec, k_spec, v_spec, q_spec, ct_spec, pl.BlockSpec(after.shape, lambda h, i: (0, 0))],
        out_specs=[q_spec, acc_spec],
        out_shape=[jax.ShapeDtypeStruct((t, D), BF16), jax.ShapeDtypeStruct((2, t, D), F32)],
        scratch_shapes=[])


def _gate_specs():
    return [pl.BlockSpec((TM, D), lambda i: (i, 3)), pl.BlockSpec((TM, D), lambda i: (i, 4))]


def _mix_pairs(mix_hbm, dsts):
    pairs = []
    for index, dst in enumerate(dsts):
        pairs += _square_pairs(mix_hbm, index, dst)
    return pairs


def _mix_out_fwd(yc, ysb, pcg, b_gate, h, w_mix, name, comm=None):
    t = h.shape[0]

    def body(yc_ref, ysb_ref, gc_ref, gs_ref, b_ref, h_ref, mix_hbm,
             a_ref, b_out_ref, mg_ref, h2_ref, wc_v, wa_v, wo_v, sems):
        _load_resident(pl.program_id(0), _mix_pairs(mix_hbm, (wc_v, wa_v, wo_v)), sems)
        a = _dot(yc_ref[...], wc_v[...])
        b = _dot(ysb_ref[...], wa_v[...])
        merged = (_sigmoid(gc_ref[...] + b_ref[:, :D]) * a + _sigmoid(gs_ref[...] + b_ref[:, D:]) * b).astype(BF16)
        a_ref[...] = a
        b_out_ref[...] = b
        mg_ref[...] = merged
        h2_ref[...] = h_ref[...] + _dot(merged, wo_v[...])

    return _call(
        body, (yc, ysb, pcg, pcg, b_gate, h, w_mix), grid=(t // TM,), name=name, comm=comm,
        in_specs=[_row_spec(TM, D), _row_spec(TM, D)] + _gate_specs()
                 + [_const_spec((1, 2 * D)), _row_spec(TM, D), _ANY],
        out_specs=[_row_spec(TM, D)] * 4,
        out_shape=[jax.ShapeDtypeStruct((t, D), F32), jax.ShapeDtypeStruct((t, D), F32),
                   jax.ShapeDtypeStruct((t, D), BF16), jax.ShapeDtypeStruct((t, D), F32)],
        scratch_shapes=[pltpu.VMEM((D, D), BF16)] * 3 + [pltpu.SemaphoreType.DMA((3 * N_DEV,))])


def _mix_out_bwd(dh2, a, b, pcg, b_gate, conv_w, w_mix, name, comm=None):
    t = dh2.shape[0]
    n_tile = t // TM
    per8 = TM // 8

    def rows(n):
        return pl.BlockSpec((TM, n), lambda i: (n_tile - 1 - i, 0))

    def cols(block):
        return pl.BlockSpec((TM, D), lambda i: (n_tile - 1 - i, block))

    def before(block):
        return pl.BlockSpec((8, D), lambda i: (jnp.maximum((n_tile - 1 - i) * per8 - 1, 0), block))

    def body(dh_ref, a_ref, b_ref, gc_ref, gs_ref, cb_ref, cc_ref, cx_ref, ccp_ref, cxp_ref, bias_ref, cw_ref, mix_hbm,
             dhb_ref, da_ref, db_ref, dgp_ref, dc_ref, dysb_ref, dbias_ref, dcw_ref, wc_v, wa_v, wo_v, head_v, sems):
        step = pl.program_id(0)
        _load_resident(step, _mix_pairs(mix_hbm, (wc_v, wa_v, wo_v)), sems)

        @pl.when(step == 0)
        def _():
            head_v[...] = jnp.zeros_like(head_v)
            dcw_ref[...] = jnp.zeros_like(dcw_ref)

        dhb = dh_ref[...].astype(BF16)
        dhb_ref[...] = dhb
        dm = _dot_nt(dhb, wo_v[...])
        gc = _sigmoid(gc_ref[...] + bias_ref[:, :D])
        gs = _sigmoid(gs_ref[...] + bias_ref[:, D:])
        da = (dm * gc).astype(BF16)
        db = (dm * gs).astype(BF16)
        da_ref[...] = da
        db_ref[...] = db
        dgc = dm * a_ref[...] * (gc * (1.0 - gc))
        dgs = dm * b_ref[...] * (gs * (1.0 - gs))
        dgp_ref[0] = dgc.astype(BF16)
        dgp_ref[1] = dgs.astype(BF16)
        _accumulate(dbias_ref.at[:, :D], step, jnp.sum(dgc, axis=0, keepdims=True))
        _accumulate(dbias_ref.at[:, D:], step, jnp.sum(dgs, axis=0, keepdims=True))
        dysb_ref[...] = _dot_nt(db, wa_v[...]).astype(BF16)
        dyc = _dot_nt(da, wc_v[...])
        cc, cx = cc_ref[...], cx_ref[...]
        xc = cc * cx
        xc_before = jnp.where(step == n_tile - 1, 0.0, ccp_ref[...] * cxp_ref[...])
        ext = jnp.concatenate([xc_before, xc], axis=0)
        x1 = pltpu.roll(ext, 1, 0)[8:]
        x2 = pltpu.roll(ext, 2, 0)[8:]
        w0, w1, w2 = cw_ref[0:1, :], cw_ref[1:2, :], cw_ref[2:3, :]
        dc_ref[0] = (dyc * (w0 * x2 + w1 * x1 + w2 * xc)).astype(BF16)
        dconv = dyc * cb_ref[...]
        dcw_ref[0:1, :] += jnp.sum(dconv * x2, axis=0, keepdims=True)
        dcw_ref[1:2, :] += jnp.sum(dconv * x1, axis=0, keepdims=True)
        dcw_ref[2:3, :] += jnp.sum(dconv * xc, axis=0, keepdims=True)
        after = jnp.concatenate([dconv, head_v[...]], axis=0)
        dxc = w2 * dconv + w1 * pltpu.roll(after, TM + 7, 0)[:TM] + w0 * pltpu.roll(after, TM + 6, 0)[:TM]
        dc_ref[1] = (dxc * cx).astype(BF16)
        dc_ref[2] = (dxc * cc).astype(BF16)
        head_v[...] = dconv[:8]

    return _call(
        body, (dh2, a, b, pcg, pcg, pcg, pcg, pcg, pcg, pcg, b_gate, conv_w, w_mix), grid=(n_tile,), name=name,
        comm=comm,
        in_specs=[rows(D)] * 3 + [cols(3), cols(4), cols(0), cols(1), cols(2), before(1), before(2),
                                  _const_spec((1, 2 * D)), _const_spec((CONV_K, D)), _ANY],
        out_specs=[rows(D)] * 3 + [pl.BlockSpec((2, TM, D), lambda i: (0, n_tile - 1 - i, 0)),
                                   pl.BlockSpec((3, TM, D), lambda i: (0, n_tile - 1 - i, 0)), rows(D),
                                   _const_spec((1, 2 * D)), _const_spec((8, D))],
        out_shape=[jax.ShapeDtypeStruct((t, D), BF16)] * 3
                  + [jax.ShapeDtypeStruct((2, t, D), BF16), jax.ShapeDtypeStruct((3, t, D), BF16),
                     jax.ShapeDtypeStruct((t, D), BF16), jax.ShapeDtypeStruct((1, 2 * D), F32),
                     jax.ShapeDtypeStruct((8, D), F32)],
        scratch_shapes=[pltpu.VMEM((D, D), BF16)] * 3 + [pltpu.VMEM((8, D), F32),
                                                         pltpu.SemaphoreType.DMA((3 * N_DEV,))])


def _inproj_bwd(dconv, dq, dkv, dgp, w_in, h, g, dh_res, name, comm=None):
    t = h.shape[0]

    def body(dc_ref, dq_ref, dkv_ref, dgp_ref, w_hbm, h_ref, g_ref, dres_ref, dh_ref, dg_ref, w_v, sems):
        step = pl.program_id(0)
        _load_resident(step, [(w_hbm, w_v)], sems)
        du = _dot_nt(dq_ref[...], w_v[3])
        for k in range(3):
            du = du + _dot_nt(dc_ref[k], w_v[k])
        for k in range(2):
            du = du + _dot_nt(dkv_ref[k].astype(BF16), w_v[4 + k]) + _dot_nt(dgp_ref[k], w_v[6 + k])
        dx, dg = _rms_bwd_tile(h_ref[...], g_ref[...], du)
        dh_ref[...] = dres_ref[...] + dx
        _accumulate(dg_ref, step, dg)

    return _call(
        body, (dconv, dq, dkv, dgp, w_in, h, g, dh_res), grid=(t // TM,), name=name, comm=comm,
        in_specs=[_blk_row_spec(3, TM, D), _row_spec(TM, D), _blk_row_spec(2, TM, D), _blk_row_spec(2, TM, D), _ANY,
                  _row_spec(TM, D), _const_spec((1, D)), _row_spec(TM, D)],
        out_specs=[_row_spec(TM, D), _const_spec((1, D))],
        out_shape=[jax.ShapeDtypeStruct((t, D), F32), jax.ShapeDtypeStruct((1, D), F32)],
        scratch_shapes=[pltpu.VMEM((N_DEV, D, D), BF16), pltpu.SemaphoreType.DMA((1,))])


def _softmax_rows(s):
    e = jnp.exp(s - jnp.max(s, axis=-1, keepdims=True))
    return e / jnp.sum(e, axis=-1, keepdims=True)


def _cross_pairs(cross_hbm, wq_v, wo_v):
    return _square_pairs(cross_hbm, 0, wq_v) + _square_pairs(cross_hbm, 1, wo_v)


def _cross_fwd(h, g, mem, g_mem, w_ckv, w_cross, name):
    t = h.shape[0]
    m = mem.shape[0]
    scale = X_DH ** -0.5

    def body(h_ref, g_ref, mem_ref, gm_ref, wkv_ref, cross_hbm, hn_ref, qx_ref, o_ref, h3_ref, mn_ref, kv_ref,
             wq_v, wo_v, sems):
        _load_resident(pl.program_id(0), _cross_pairs(cross_hbm, wq_v, wo_v), sems)

        @pl.when(pl.program_id(0) == 0)
        def _():
            mn = _rms_fwd_tile(mem_ref[...], gm_ref[...]).astype(BF16)
            mn_ref[...] = mn
            for j in range(N_DEV):
                kv_ref[j] = _dot(mn, wkv_ref[j]).astype(BF16)

        ht = h_ref[...]
        hn = _rms_fwd_tile(ht, g_ref[...]).astype(BF16)
        hn_ref[...] = hn
        qx = _dot(hn, wq_v[...]).astype(BF16)
        qx_ref[...] = qx
        for hd in range(X_H):
            lo, hi = hd * X_DH, (hd + 1) * X_DH
            p = _softmax_rows(_dot_nt(qx[:, lo:hi], kv_ref[hd]) * scale)
            o_ref[:, lo:hi] = _dot(p.astype(BF16), kv_ref[X_H + hd]).astype(BF16)
        h3_ref[...] = ht + _dot(o_ref[...], wo_v[...])

    return pl.pallas_call(
        body, grid=(t // TM,), name=name,
        in_specs=[_row_spec(TM, D), _const_spec((1, D)), _const_spec((m, D)), _const_spec((1, D)),
                  _const_spec((N_DEV, D, X_DH)), _ANY],
        out_specs=[_row_spec(TM, D)] * 4 + [_const_spec((m, D)), _const_spec((N_DEV, m, X_DH))],
        out_shape=[jax.ShapeDtypeStruct((t, D), BF16)] * 3 + [jax.ShapeDtypeStruct((t, D), F32),
                                                              jax.ShapeDtypeStruct((m, D), BF16),
                                                              jax.ShapeDtypeStruct((N_DEV, m, X_DH), BF16)],
        scratch_shapes=[pltpu.VMEM((D, D), BF16)] * 2 + [pltpu.SemaphoreType.DMA((2 * N_DEV,))],
        compiler_params=_cparams(),
    )(h, g, mem, g_mem, w_ckv, w_cross)


def _cross_bwd(dh3, h, g, qx, kv, mem, g_mem, w_ckv, w_cross, name, comm=None):
    t = h.shape[0]
    m = kv.shape[1]
    scale = X_DH ** -0.5

    def body(dh_ref, h_ref, g_ref, qx_ref, kv_ref, mem_ref, gm_ref, wkv_ref, cross_hbm,
             dhb_ref, dqx_ref, dkv_ref, dh2_ref, dg_ref, dgm_ref, wq_v, wo_v, sems):
        step = pl.program_id(0)
        _load_resident(step, _cross_pairs(cross_hbm, wq_v, wo_v), sems)

        @pl.when(step == 0)
        def _():
            dkv_ref[...] = jnp.zeros_like(dkv_ref)

        dht = dh_ref[...]
        dhb = dht.astype(BF16)
        dhb_ref[...] = dhb
        do = _dot_nt(dhb, wo_v[...]).astype(BF16)
        for hd in range(X_H):
            lo, hi = hd * X_DH, (hd + 1) * X_DH
            qh = qx_ref[:, lo:hi]
            kh = kv_ref[hd]
            p = _softmax_rows(_dot_nt(qh, kh) * scale)
            doh = do[:, lo:hi]
            dp = _dot_nt(doh, kv_ref[X_H + hd])
            ds = (p * (dp - jnp.sum(dp * p, axis=-1, keepdims=True)) * scale).astype(BF16)
            dqx_ref[:, lo:hi] = _dot(ds, kh).astype(BF16)
            dkv_ref[hd] += _dot_tn(ds, qh)
            dkv_ref[X_H + hd] += _dot_tn(p.astype(BF16), doh)
        dhn = _dot_nt(dqx_ref[...], wq_v[...])
        dx, dg = _rms_bwd_tile(h_ref[...], g_ref[...], dhn)
        dh2_ref[...] = dht + dx
        _accumulate(dg_ref, step, dg)

        @pl.when(step == t // TM - 1)
        def _():
            dmn = jnp.zeros((m, D), F32)
            for j in range(N_DEV):
                dmn = dmn + _dot_nt(dkv_ref[j].astype(BF16), wkv_ref[j])
            dgm_ref[...] = _rms_bwd_tile(mem_ref[...], gm_ref[...], dmn)[1]

    return _call(
        body, (dh3, h, g, qx, kv, mem, g_mem, w_ckv, w_cross), grid=(t // TM,), name=name, comm=comm,
        in_specs=[_row_spec(TM, D), _row_spec(TM, D), _const_spec((1, D)), _row_spec(TM, D),
                  _const_spec((N_DEV, m, X_DH)), _const_spec((m, D)), _const_spec((1, D)),
                  _const_spec((N_DEV, D, X_DH)), _ANY],
        out_specs=[_row_spec(TM, D), _row_spec(TM, D), _const_spec((N_DEV, m, X_DH)), _row_spec(TM, D),
                   _const_spec((1, D)), _const_spec((1, D))],
        out_shape=[jax.ShapeDtypeStruct((t, D), BF16), jax.ShapeDtypeStruct((t, D), BF16),
                   jax.ShapeDtypeStruct((N_DEV, m, X_DH), F32), jax.ShapeDtypeStruct((t, D), F32),
                   jax.ShapeDtypeStruct((1, D), F32), jax.ShapeDtypeStruct((1, D), F32)],
        scratch_shapes=[pltpu.VMEM((D, D), BF16)] * 2 + [pltpu.SemaphoreType.DMA((2 * N_DEV,))])


def _adamw(w, parts, m, v, name, row_block=0, token=None):
    r, c = w.shape
    n = parts.shape[0]
    tr = _pick_tile(r, (256, 352, 128))
    off = row_block * (r // tr)

    def body(*refs):
        if token is None:
            _adamw_update(None, *refs)
        else:
            _adamw_update(refs[4], *refs[:4], *refs[5:])

    spec = _row_spec(tr, c)
    in_specs = [spec, pl.BlockSpec((n, tr, c), lambda i: (0, i + off, 0)), spec, spec]
    operands = (w, parts, m, v)
    if token is not None:
        in_specs.append(_const_spec(token.shape))
        operands += (token,)
    return pl.pallas_call(
        body, grid=(r // tr,), name=name, in_specs=in_specs, out_specs=[spec] * 4,
        out_shape=[jax.ShapeDtypeStruct((r, c), F32)] * 4,
        compiler_params=_cparams(),
    )(*operands)


def _adamw_update(tok_ref, w_ref, p_ref, m_ref, v_ref, g_ref, d_ref, nm_ref, nv_ref):
    gt = p_ref[0].astype(F32)
    for k in range(1, p_ref.shape[0]):
        gt = gt + p_ref[k].astype(F32)
    if tok_ref is not None:
        gt = gt + tok_ref[0:1, 0:1]
    _adamw_apply(gt, w_ref, m_ref, v_ref, g_ref, d_ref, nm_ref, nv_ref)


def _adamw_own(w, land, own, chip, m, v, name, row_block=0, token=None):
    r, c = w.shape
    tr = _pick_tile(r, (256, 352, 128))
    off = row_block * (r // tr)

    def body(chip_ref, w_ref, land_ref, own_ref, m_ref, v_ref, *rest):
        mine = own_ref[0].astype(F32)
        gt = jnp.where(chip_ref[0] == 0, mine, land_ref[0].astype(F32))
        for k in range(1, N_CHIP):
            gt = gt + jnp.where(chip_ref[0] == k, mine, land_ref[k].astype(F32))
        if token is not None:
            gt = gt + rest[0][0:1, 0:1]
        _adamw_apply(gt, w_ref, m_ref, v_ref, *rest[-4:])

    spec = pl.BlockSpec((tr, c), lambda i, chip_ref: (i, 0))
    in_specs = [spec, pl.BlockSpec((N_CHIP, tr, c), lambda i, chip_ref: (0, i + off, 0)),
                pl.BlockSpec((1, tr, c), lambda i, chip_ref: (chip_ref[0], i + off, 0)), spec, spec]
    operands = (chip, w, land, own, m, v)
    if token is not None:
        in_specs.append(pl.BlockSpec(token.shape, lambda i, chip_ref: (0, 0)))
        operands += (token,)
    return pl.pallas_call(
        body, name=name,
        grid_spec=pltpu.PrefetchScalarGridSpec(
            num_scalar_prefetch=1, grid=(r // tr,), in_specs=in_specs, out_specs=[spec] * 4),
        out_shape=[jax.ShapeDtypeStruct((r, c), F32)] * 4,
        compiler_params=_cparams(),
    )(*operands)


def _adamw_apply(gt, w_ref, m_ref, v_ref, g_ref, d_ref, nm_ref, nv_ref):
    g_ref[...] = gt
    nm = ADAM_B1 * m_ref[...] + (1.0 - ADAM_B1) * gt
    nv = ADAM_B2 * v_ref[...] + (1.0 - ADAM_B2) * jnp.square(gt)
    m_hat = nm / (1.0 - ADAM_B1 ** ADAM_STEP)
    v_hat = nv / (1.0 - ADAM_B2 ** ADAM_STEP)
    d_ref[...] = -ADAM_LR * (m_hat / (jnp.sqrt(v_hat) + ADAM_EPS) + ADAM_WD * w_ref[...])
    nm_ref[...] = nm
    nv_ref[...] = nv


def _mesh_pos():
    return lax.axis_index("x"), lax.axis_index("y"), lax.axis_index("c")


def _both(first, second):
    n_in, n_out, n_sem = len(first.inputs), len(first.out_shapes), len(first.sem_shapes)

    def run(round_name):
        def both(in_refs, out_refs, sems):
            getattr(first, round_name)(in_refs[:n_in], out_refs[:n_out], sems[:n_sem])
            getattr(second, round_name)(in_refs[n_in:], out_refs[n_out:], sems[n_sem:])
        return both

    return types.SimpleNamespace(
        inputs=first.inputs + second.inputs, out_shapes=first.out_shapes + second.out_shapes,
        sem_shapes=first.sem_shapes + second.sem_shapes, start=run("start"), middle=run("middle"),
        finish=run("finish"))


def _no_round(in_refs, out_refs, sems):
    pass


def _run_exchange(comm, name):
    c_in, c_out = len(comm.inputs), len(comm.out_shapes)

    def body(*refs):
        cins, couts, sems = refs[:c_in], refs[c_in:c_in + c_out], refs[c_in + c_out:]
        comm.start(cins, couts, sems)
        comm.middle(cins, couts, sems)
        comm.finish(cins, couts, sems)

    return list(pl.pallas_call(
        body, name=name, out_shape=list(comm.out_shapes),
        in_specs=[_ANY] * c_in, out_specs=[_ANY] * c_out, scratch_shapes=list(comm.sem_shapes),
    )(*comm.inputs))


def _gather_exchange(shards):
    n_arr = len(shards)

    def plan(x_refs, out_refs, sems):
        send_sems, recv_sems, local_sems = sems[:3]
        stage = sems[3:]
        x, y, c = _mesh_pos()
        me, sibling = (x, y, c), (x, y, 1 - c)
        xn, yn, diag = (1 - x, y), (x, 1 - y), (1 - x, 1 - y)

        def slot(a, px, py, pc, half=None):
            ref = out_refs[a].at[4 * px + 2 * py + pc]
            if half is None:
                return ref
            rows = shards[a].shape[0] // 2
            return ref.at[half * rows:(half + 1) * rows]

        def copy(a, k, block, to, half=None, src=None):
            dst = slot(a, *block, half)
            return pltpu.make_async_remote_copy(
                src_ref=dst if src is None else src, dst_ref=dst,
                send_sem=send_sems.at[a, k], recv_sem=recv_sems.at[a, k],
                device_id=to, device_id_type=pl.DeviceIdType.MESH)

        return types.SimpleNamespace(
            me=me, sibling=sibling, xn=xn, yn=yn, diag=diag, c=c, copy=copy,
            mine_in=[pltpu.make_async_copy(x_refs[a], stage[a], local_sems.at[a, 0]) for a in range(n_arr)],
            mine_out=[pltpu.make_async_copy(stage[a], slot(a, *me), local_sems.at[a, 1]) for a in range(n_arr)],
            first=[cp for a in range(n_arr) for cp in (
                copy(a, 0, me, sibling, src=x_refs[a]), copy(a, 1, me, (*xn, c), src=x_refs[a]),
                copy(a, 2, me, (*yn, c), src=x_refs[a]))],
            second=lambda a: (copy(a, 3, (*xn, c), (*yn, c), half=0), copy(a, 5, (*xn, c), sibling),
                              copy(a, 4, (*yn, c), (*xn, c), half=1), copy(a, 6, (*yn, c), sibling)),
            third=lambda a: (copy(a, 7, (*diag, c), sibling, half=0), copy(a, 8, (*diag, c), sibling, half=1)))

    def start(x_refs, out_refs, sems):
        p = plan(x_refs, out_refs, sems)
        for cp in p.first + p.mine_in:
            cp.start()
        for cp_in, cp_out in zip(p.mine_in, p.mine_out):
            cp_in.wait()
            cp_out.start()

    def middle(x_refs, out_refs, sems):
        p = plan(x_refs, out_refs, sems)
        for a in range(n_arr):
            to_yn, x_to_sib, to_xn, y_to_sib = p.second(a)
            p.copy(a, 1, (*p.xn, p.c), p.me).wait_recv()
            to_yn.start()
            x_to_sib.start()
            p.copy(a, 2, (*p.yn, p.c), p.me).wait_recv()
            to_xn.start()
            y_to_sib.start()

    def finish(x_refs, out_refs, sems):
        p = plan(x_refs, out_refs, sems)
        for a in range(n_arr):
            half0_to_sib, half1_to_sib = p.third(a)
            p.copy(a, 3, (*p.diag, p.c), p.me, half=0).wait_recv()
            half0_to_sib.start()
            p.copy(a, 4, (*p.diag, p.c), p.me, half=1).wait_recv()
            half1_to_sib.start()
        other = 1 - p.c
        for a in range(n_arr):
            p.copy(a, 0, p.sibling, p.me).wait_recv()
            p.copy(a, 5, (*p.xn, other), p.me).wait_recv()
            p.copy(a, 6, (*p.yn, other), p.me).wait_recv()
            p.copy(a, 7, (*p.diag, other), p.me, half=0).wait_recv()
            p.copy(a, 8, (*p.diag, other), p.me, half=1).wait_recv()
        for cp in p.first:
            cp.wait_send()
        for a in range(n_arr):
            for cp in p.second(a) + p.third(a):
                cp.wait_send()
        for cp in p.mine_out:
            cp.wait()

    return types.SimpleNamespace(
        inputs=list(shards), start=start, middle=middle, finish=finish,
        out_shapes=[jax.ShapeDtypeStruct((N_DEV,) + s.shape, s.dtype) for s in shards],
        sem_shapes=[pltpu.SemaphoreType.DMA((n_arr, 9)), pltpu.SemaphoreType.DMA((n_arr, 9)),
                    pltpu.SemaphoreType.DMA((n_arr, 2))] + [pltpu.VMEM(s.shape, s.dtype) for s in shards])


def _pair_exchange(grads):
    n_arr = len(grads)

    def plan(g_refs, land_refs, sems):
        send_sems, recv_sems = sems
        x, y, c = _mesh_pos()
        return [pltpu.make_async_remote_copy(
            src_ref=g_refs[a].at[2 * k + 1 - c], dst_ref=land_refs[a].at[k],
            send_sem=send_sems.at[a, k], recv_sem=recv_sems.at[a, k],
            device_id=(x, y, 1 - c), device_id_type=pl.DeviceIdType.MESH)
            for a in range(n_arr) for k in range(N_CHIP)]

    def start(g_refs, land_refs, sems):
        for cp in plan(g_refs, land_refs, sems):
            cp.start()

    def finish(g_refs, land_refs, sems):
        for cp in plan(g_refs, land_refs, sems):
            cp.wait()

    return types.SimpleNamespace(
        inputs=list(grads), start=start, middle=_no_round, finish=finish,
        out_shapes=[jax.ShapeDtypeStruct((N_CHIP,) + g.shape[1:], g.dtype) for g in grads],
        sem_shapes=[pltpu.SemaphoreType.DMA((n_arr, N_CHIP)), pltpu.SemaphoreType.DMA((n_arr, N_CHIP))])


def _chip_exchange(parts):
    n_arr = len(parts)

    def plan(p_refs, land_refs, sems):
        send_sems, recv_sems, local_sems = sems
        x, y, c = _mesh_pos()
        my_chip = 2 * x + y
        chips = [(1 - x, y), (x, 1 - y), (1 - x, 1 - y)]
        local = [pltpu.make_async_copy(p_refs[a].at[my_chip], land_refs[a].at[my_chip], local_sems.at[a])
                 for a in range(n_arr)]

        def copy(a, k, src_slot, dst_slot, px, py):
            return pltpu.make_async_remote_copy(
                src_ref=p_refs[a].at[src_slot], dst_ref=land_refs[a].at[dst_slot],
                send_sem=send_sems.at[a, k], recv_sem=recv_sems.at[a, k],
                device_id=(px, py, c), device_id_type=pl.DeviceIdType.MESH)

        sends = [copy(a, k, 2 * px + py, my_chip, px, py) for a in range(n_arr) for k, (px, py) in enumerate(chips)]
        arrivals = [copy(a, k, my_chip, 2 * px + py, px, py) for a in range(n_arr)
                    for k, (px, py) in enumerate(chips)]
        return local, sends, arrivals

    def start(p_refs, land_refs, sems):
        local, sends, _ = plan(p_refs, land_refs, sems)
        for cp in local + sends:
            cp.start()

    def finish(p_refs, land_refs, sems):
        local, sends, arrivals = plan(p_refs, land_refs, sems)
        for cp in arrivals:
            cp.wait_recv()
        for cp in sends:
            cp.wait_send()
        for cp in local:
            cp.wait()

    return types.SimpleNamespace(
        inputs=list(parts), start=start, middle=_no_round, finish=finish,
        out_shapes=[jax.ShapeDtypeStruct(p.shape, p.dtype) for p in parts],
        sem_shapes=[pltpu.SemaphoreType.DMA((n_arr, 3)), pltpu.SemaphoreType.DMA((n_arr, 3)),
                    pltpu.SemaphoreType.DMA((n_arr,))])


_HBM = pl.BlockSpec(memory_space=pltpu.HBM)
_SEM = pl.BlockSpec(memory_space=pltpu.SEMAPHORE)
_DATAFLOW = pltpu.SideEffectType.DATAFLOW_SIDE_EFFECTING


def _chip_copies(p_refs, land_refs, send_sems, recv_sems):
    x, y, c = _mesh_pos()
    my_chip = 2 * x + y
    chips = [(1 - x, y), (x, 1 - y), (1 - x, 1 - y)]
    return [pltpu.make_async_remote_copy(
        src_ref=p_refs[a].at[2 * px + py], dst_ref=land_refs[a].at[my_chip],
        send_sem=send_sems[3 * a + k], recv_sem=recv_sems[3 * a + k],
        device_id=(px, py, c), device_id_type=pl.DeviceIdType.MESH)
        for a in range(len(p_refs)) for k, (px, py) in enumerate(chips)]


def _chip_exchange_begin(parts, name):
    n_arr = len(parts)
    n_buf, n_copy = 2 * n_arr, 3 * n_arr
    lands = [lax.empty(p.shape, p.dtype) for p in parts]

    def body(*refs):
        p_refs, land_refs = refs[:n_arr], refs[n_arr:n_buf]
        send_sems, recv_sems, token = refs[n_buf:n_buf + n_copy], refs[n_buf + n_copy:n_buf + 2 * n_copy], refs[-1]
        for cp in _chip_copies(p_refs, land_refs, send_sems, recv_sems):
            cp.start()
        token[...] = jnp.zeros_like(token)

    bufs = list(parts) + list(lands)
    outs = pl.pallas_call(
        body, name=name,
        out_shape=(*[pltpu.SemaphoreType.DMA(())] * (2 * n_copy), *[pltpu.HBM(b.shape, b.dtype) for b in bufs],
                   jax.ShapeDtypeStruct((8, 128), F32)),
        in_specs=[_HBM] * n_buf,
        out_specs=(*[_SEM] * (2 * n_copy), *[_HBM] * n_buf, pl.BlockSpec(memory_space=pltpu.VMEM)),
        input_output_aliases={i: 2 * n_copy + i for i in range(n_buf)},
        compiler_params=pltpu.CompilerParams(has_side_effects=_DATAFLOW),
    )(*[pltpu.with_memory_space_constraint(b, pltpu.HBM) for b in bufs])
    sems = list(outs[:2 * n_copy])
    thru = list(outs[2 * n_copy:2 * n_copy + n_buf])
    return types.SimpleNamespace(send_sems=sems[:n_copy], recv_sems=sems[n_copy:], parts=thru[:n_arr],
                                 lands=thru[n_arr:], token=outs[-1])


def _chip_exchange_end(flight, after, name):
    send_sems, recv_sems, parts, lands = flight.send_sems, flight.recv_sems, flight.parts, flight.lands
    n_arr = len(parts)
    n_buf, n_copy = 2 * n_arr, 3 * n_arr

    def body(*refs):
        p_refs, land_refs = refs[:n_arr], refs[n_arr:n_buf]
        sems = refs[n_buf:n_buf + 2 * n_copy]
        for cp in _chip_copies(p_refs, land_refs, sems[:n_copy], sems[n_copy:]):
            cp.wait_send()
            cp.wait_recv()

    bufs = list(parts) + list(lands)
    outs = pl.pallas_call(
        body, name=name, out_shape=tuple(pltpu.HBM(b.shape, b.dtype) for b in bufs),
        in_specs=[_HBM] * n_buf + [_SEM] * (2 * n_copy) + [_ANY], out_specs=tuple([_HBM] * n_buf),
        input_output_aliases={i: i for i in range(n_buf)},
        compiler_params=pltpu.CompilerParams(has_side_effects=_DATAFLOW),
    )(*bufs, *send_sems, *recv_sems, after)
    return list(outs[:n_arr]), list(outs[n_arr:])


def _row_tile(r, cap=640):
    best = None
    for cand in range(16, min(r, cap) + 1, 16):
        if r % cand == 0:
            best = cand
    return best if best is not None else r


def _pair_sum(gs, landeds, core, name):
    tiles = [_row_tile(g.shape[1]) for g in gs]
    counts = [g.shape[1] // tr for g, tr in zip(gs, tiles)]
    n_arr = len(gs)

    def body(core_ref, *refs):
        for a in range(n_arr):
            mine, theirs, out = refs[2 * a], refs[2 * a + 1], refs[2 * n_arr + a]
            out[0] = (mine[0].astype(F32) + theirs[0].astype(F32)).astype(out.dtype)

    in_specs, out_specs, operands = [], [], []
    for g, landed, tr, count in zip(gs, landeds, tiles, counts):
        c_dim = g.shape[2]
        last = count - 1
        in_specs += [pl.BlockSpec((1, tr, c_dim),
                                  lambda k, i, core_ref, last=last: (2 * k + core_ref[0], jnp.minimum(i, last), 0)),
                     pl.BlockSpec((1, tr, c_dim), lambda k, i, core_ref, last=last: (k, jnp.minimum(i, last), 0))]
        out_specs.append(pl.BlockSpec((1, tr, c_dim), lambda k, i, core_ref, last=last: (k, jnp.minimum(i, last), 0)))
        operands += [g, landed]
    return list(pl.pallas_call(
        body, name=name,
        grid_spec=pltpu.PrefetchScalarGridSpec(
            num_scalar_prefetch=1, grid=(N_CHIP, max(counts)), in_specs=in_specs, out_specs=out_specs),
        out_shape=[jax.ShapeDtypeStruct((N_CHIP,) + g.shape[1:], g.dtype) for g in gs],
        compiler_params=_cparams(2),
    )(core, *operands))


def _sum_slots(parts, name):
    n, r, c_dim = parts.shape
    tr = _row_tile(r)

    def body(p_ref, o_ref):
        acc = p_ref[0].astype(F32)
        for k in range(1, n):
            acc = acc + p_ref[k].astype(F32)
        o_ref[...] = acc

    return pl.pallas_call(
        body, grid=(r // tr,), name=name,
        in_specs=[pl.BlockSpec((n, tr, c_dim), lambda i: (0, i, 0))],
        out_specs=_row_spec(tr, c_dim),
        out_shape=jax.ShapeDtypeStruct((r, c_dim), F32),
        compiler_params=_cparams(),
    )(parts)


GAINS = ("g_ffn1", "g_mix", "g_cross", "g_mem", "g_ffn2", "g_final")
SMALL = GAINS + ("b_gate", "conv_w")
SMALL_R = 16
LOSS_ROW = 11
WEIGHT_ORDER = ("g_ffn1", "w_ffn1_gu", "w_ffn1_down", "g_mix", "w_in", "b_gate", "conv_w", "w_conv_out",
                "w_attn_out", "w_o", "g_cross", "g_mem", "w_cq", "w_ckv", "w_co", "g_ffn2", "w_ffn2_gu",
                "w_ffn2_down", "g_final")
GU_NAMES = ("w_ffn1_gu", "w_ffn2_gu")


def _pack_small(vals, conv_rows):
    rows = [vals[n].reshape(1, D) for n in GAINS] + [vals["b_gate"].reshape(2, D), conv_rows.reshape(CONV_K, D)]
    used = len(GAINS) + 2 + CONV_K
    return jnp.concatenate(rows + [jnp.zeros((SMALL_R - used, D), F32)], axis=0)


def _unpack_small(buf):
    out = {n: buf[k] for k, n in enumerate(GAINS)}
    out["b_gate"] = buf[6:8].reshape(2 * D)
    out["conv_w"] = buf[8:8 + CONV_K]
    return out


def _exchange_shards(wts):
    out = {n: jnp.pad(wts[n].T.astype(BF16), ((0, FF_PAD - FF_BLK), (0, 0))) for n in GU_NAMES}
    for n in ("w_ckv", "w_in", "w_ffn1_down", "w_ffn2_down"):
        out[n] = wts[n].astype(BF16)
    out["mix"] = jnp.concatenate([wts[n].astype(BF16) for n in MIX_MATS], axis=0)
    out["cross"] = jnp.concatenate([wts[n].astype(BF16) for n in CROSS_MATS], axis=0)
    return out


def _reduce_group(grads, landed, core, names):
    return _pair_sum(grads, landed, core, "grads_pair_sum_" + "_".join(names))


def _step(x, mem, target, sh, conv_pad, gains, b_gate, core):
    wg1, wd1, conv_all = _run_exchange(_gather_exchange([sh["w_ffn1_gu"], sh["w_ffn1_down"], conv_pad]), "gather_ffn1")
    conv_w = conv_all[:, :CONV_K, :].transpose(1, 0, 2).reshape(CONV_K, D)
    (n1, gate1, up1, act1, h1), (w_in,) = _ffn_fwd(
        x, gains["g_ffn1"], wg1, wd1, "ffn1_fwd", comm=_gather_exchange([sh["w_in"]]))
    (u, pcg, qkv, yc), (w_mix, wd2) = _inproj_fwd(h1, gains["g_mix"], w_in, conv_w, "inproj_fwd",
                                                  comm=_gather_exchange([sh["mix"], sh["w_ffn2_down"]]))
    (ysb, ctot), (w_cross, w_ckv, wg2) = _sb_fwd(
        qkv, "sb_fwd", comm=_gather_exchange([sh["cross"], sh["w_ckv"], sh["w_ffn2_gu"]]))
    (a_mix, b_mix, merged, h2), _ = _mix_out_fwd(yc, ysb, pcg, b_gate, h1, w_mix, "mix_out_fwd")
    hn, qx, o_x, h3, mn, kv = _cross_fwd(h2, gains["g_cross"], mem, gains["g_mem"], w_ckv, w_cross, "cross_fwd")
    (n4, gate2, up2, act2, dh4, loss, dg_final), _ = _ffn_fwd(h3, gains["g_ffn2"], wg2, wd2, "ffn2_fwd",
                                                              head=(gains["g_final"], target))

    gs = {"g_final": dg_final}
    (dgu2, dh4b, dh3, gs["g_ffn2"]), _ = _ffn_bwd(dh4, h3, gains["g_ffn2"], gate2, up2, wg2, wd2, "ffn2_bwd")
    grads_a = [_mm_tn_rows(dgu2, n4, FF_PAD, "dw_ffn2_gu"),
               _mm_tn_rows(act2, dh4b, FF_BLK, "dw_ffn2_down").reshape(N_DEV, DOWN_ROWS, D)]
    names_a = ["w_ffn2_gu", "w_ffn2_down"]
    (dh3b, dqx, dkv, dh2, gs["g_cross"], gs["g_mem"]), _ = _cross_bwd(
        dh3, h2, gains["g_cross"], qx, kv, mem, gains["g_mem"], w_ckv, w_cross, "cross_bwd")
    cross_stack = _mm_tn_square(hn, dqx, "dw_cq", 0, len(CROSS_MATS))
    grads_b = [_mm_tn_cols(mn, dkv, "dw_ckv"), _mm_tn_square(o_x, dh3b, "dw_co", 1, len(CROSS_MATS), cross_stack)]
    names_b = ["w_ckv", "cross"]
    (dh2b, da_mix, db_mix, dgp, dconv, dysb, gs["b_gate"], gs["conv_w"]), landed_ab = _mix_out_bwd(
        dh2, a_mix, b_mix, pcg, b_gate, conv_w, w_mix, "mix_out_bwd", comm=_pair_exchange(grads_a + grads_b))
    sums_ab = _reduce_group(grads_a + grads_b, landed_ab, core, names_a + names_b)
    mix_stack = _mm_tn_square(yc, da_mix, "dw_conv_out", 0, len(MIX_MATS))
    mix_stack = _mm_tn_square(ysb, db_mix, "dw_attn_out", 1, len(MIX_MATS), mix_stack)
    grads_c = [_mm_tn_square(merged, dh2b, "dw_o", 2, len(MIX_MATS), mix_stack)]
    flight_ab = _chip_exchange_begin(sums_ab, "grads_to_chips_early_begin")
    (dq, dkv_sb), _ = _sb_bwd(qkv, dysb, ctot, flight_ab.token, "sb_bwd")
    w_in_stack = _mm_tn_cols(u, dconv, "dw_in_conv", 0, N_DEV)
    w_in_stack = _mm_tn_cols(u, dq[None], "dw_in_q", 3, N_DEV, w_in_stack)
    w_in_stack = _mm_tn_cols(u, dkv_sb, "dw_in_kv", 4, N_DEV, w_in_stack)
    grads_d = [_mm_tn_cols(u, dgp, "dw_in_gates", 6, N_DEV, w_in_stack)]
    (dh1, gs["g_mix"]), landed_cd = _inproj_bwd(dconv, dq, dkv_sb, dgp, w_in, h1, gains["g_mix"], dh2, "inproj_bwd",
                                                comm=_pair_exchange(grads_c + grads_d))
    sums_cd = _reduce_group(grads_c + grads_d, landed_cd, core, ["mix", "w_in"])
    flight_d = _chip_exchange_begin(sums_cd, "grads_to_chips_w_in_begin")
    (dgu1, dh1b, dx, gs["g_ffn1"]), _ = _ffn_bwd(dh1, x, gains["g_ffn1"] + flight_d.token[0, 0], gate1, up1, wg1, wd1,
                                                 "ffn1_bwd")
    dw_gu1 = _mm_tn_rows(dgu1, n1, FF_PAD, "dw_ffn1_gu")
    dw_down1, landed_gu1 = _mm_tn_rows(act1, dh1b, FF_BLK, "dw_ffn1_down", comm=_pair_exchange([dw_gu1]))
    grads_e = [dw_gu1, dw_down1.reshape(N_DEV, DOWN_ROWS, D)]
    names_e = ["w_ffn1_gu", "w_ffn1_down"]
    small_mine = _pack_small({n: gs[n] for n in GAINS + ("b_gate",)}, gs["conv_w"][:CONV_K])
    small_mine = small_mine.at[LOSS_ROW, 0].set(loss[0, 0])
    landed_down1, small_all = _run_exchange(
        _both(_pair_exchange(grads_e[1:]), _gather_exchange([small_mine])), "grads_to_sibling_ffn1_down")
    landed_e = landed_gu1 + [landed_down1]
    flight_e = _chip_exchange_begin(_reduce_group(grads_e, landed_e, core, names_e), "grads_to_chips_ffn1_begin")
    flights = [(names_a + names_b, flight_ab), (["mix", "w_in"], flight_d), (names_e, flight_e)]
    return dx, flights, small_all


def kernel(x, mem, g_ffn1, w_ffn1_gu, w_ffn1_down, g_mix, w_in, b_gate, conv_w, w_conv_out, w_attn_out, w_o, g_cross, g_mem, w_cq, w_ckv, w_co, g_ffn2, w_ffn2_gu, w_ffn2_down, g_final, loss_target, m_g_ffn1, m_w_ffn1_gu, m_w_ffn1_down, m_g_mix, m_w_in, m_b_gate, m_conv_w, m_w_conv_out, m_w_attn_out, m_w_o, m_g_cross, m_g_mem, m_w_cq, m_w_ckv, m_w_co, m_g_ffn2, m_w_ffn2_gu, m_w_ffn2_down, m_g_final, v_g_ffn1, v_w_ffn1_gu, v_w_ffn1_down, v_g_mix, v_w_in, v_b_gate, v_conv_w, v_w_conv_out, v_w_attn_out, v_w_o, v_g_cross, v_g_mem, v_w_cq, v_w_ckv, v_w_co, v_g_ffn2, v_w_ffn2_gu, v_w_ffn2_down, v_g_final):
    args = locals()
    wts = {n: args[n] for n in WEIGHT_ORDER}
    mom1 = {n: args["m_" + n] for n in WEIGHT_ORDER}
    mom2 = {n: args["v_" + n] for n in WEIGHT_ORDER}
    cx, cy, cc = _mesh_pos()
    dev = 4 * cx + 2 * cy + cc
    conv_cols = D // N_DEV

    conv_pad = jnp.concatenate([conv_w, jnp.zeros((SMALL_R - CONV_K, conv_cols), F32)], axis=0)
    gains = {n: wts[n].reshape(1, D) for n in GAINS}
    dx, flights, small_all = _step(x[0], mem[0], loss_target[0], _exchange_shards(wts), conv_pad, gains,
                                 b_gate.reshape(1, 2 * D), cc.reshape(1).astype(jnp.int32))

    grads, delta, new_m, new_v = {}, {}, {}, {}

    def operands(n, transposed):
        trio = (wts[n], mom1[n], mom2[n])
        return tuple(a.T for a in trio) if transposed else trio

    def record(n, res, transposed):
        grads[n], delta[n], new_m[n], new_v[n] = [r.T for r in res] if transposed else res

    early = [("w_ffn2_gu", "w_ffn2_gu", 0, True), ("w_ffn2_down", "w_ffn2_down", 0, False),
             ("w_ckv", "w_ckv", 0, False), ("w_in", "w_in", 0, False)]
    early += [(n, "mix", k, False) for k, n in enumerate(MIX_MATS)]
    early += [(n, "cross", k, False) for k, n in enumerate(CROSS_MATS)]
    chip = (2 * cx + cy).reshape(1).astype(jnp.int32)
    (names_early, flight_early), (names_w_in, flight_w_in), (last_names, flight_last) = flights
    token = flight_last.token
    own, land = {}, {}
    for names, flight, tag in ((names_early, flight_early, "early"), (names_w_in, flight_w_in, "w_in")):
        own_parts, landed = _chip_exchange_end(flight, token, "grads_to_chips_%s_end" % tag)
        own.update(zip(names, own_parts))
        land.update(zip(names, landed))
    for n, buf, row_block, transposed in early:
        w, m1, m2 = operands(n, transposed)
        record(n, _adamw_own(w, land[buf], own[buf], chip, m1, m2, "adamw_" + n, row_block, token), transposed)

    after = jnp.concatenate([new_v[n][:1, :1] for n, _, _, _ in early], axis=0)
    own_parts, landed = _chip_exchange_end(flight_last, after, "grads_to_chips_ffn1_end")
    for n, own_n, land_n, transposed in zip(last_names, own_parts, landed, (True, False)):
        w, m1, m2 = operands(n, transposed)
        record(n, _adamw_own(w, land_n, own_n, chip, m1, m2, "adamw_" + n), transposed)

    small_sum = _sum_slots(small_all, "small_grads_sum")
    loss = small_sum[LOSS_ROW, 0]
    grad_small = _unpack_small(small_sum)
    grad_small["conv_w"] = lax.dynamic_slice_in_dim(grad_small["conv_w"], dev * conv_cols, conv_cols, axis=1)
    grads.update(grad_small)

    def small_buf(vals):
        return _pack_small(vals, jnp.concatenate([vals["conv_w"], jnp.zeros((CONV_K, D - conv_cols), F32)], axis=1))

    _, d_s, m_s, v_s = _adamw(small_buf(wts), small_buf(grads)[None], small_buf(mom1), small_buf(mom2), "adamw_small")
    for res, buf in ((delta, d_s), (new_m, m_s), (new_v, v_s)):
        un = _unpack_small(buf)
        for n in GAINS + ("b_gate",):
            res[n] = un[n]
        res["conv_w"] = un["conv_w"][:, :conv_cols]

    return (loss, dx[None], *[grads[n] for n in WEIGHT_ORDER], *[delta[n] for n in WEIGHT_ORDER],
            *[new_m[n] for n in WEIGHT_ORDER], *[new_v[n] for n in WEIGHT_ORDER])
```

```python
import types

import jax
import jax.numpy as jnp
from jax import lax
from jax.experimental import pallas as pl
from jax.experimental.pallas import tpu as pltpu

F32 = jnp.float32
BF16 = jnp.bfloat16

D = 1024
DFF = 2816
SB_H = 8
SB_DH = 128
X_H = 4
X_DH = 256
CONV_K = 3
RMS_EPS = 1e-6
N_DEV = 8
N_CHIP = 4
SQ_ROWS = D // N_DEV

ADAM_LR = 0.001
ADAM_B1 = 0.9
ADAM_B2 = 0.999
ADAM_EPS = 1e-08
ADAM_WD = 0.01
ADAM_STEP = 10

TM = 256
TQ = 512
TK = 256
SB_HPS = 2
VMEM_LIMIT = 56 << 20

FF_BLK = DFF // 4
FF_PAD = 768
FF_SUB = 256
DOWN_ROWS = DFF // N_DEV

MIX_MATS = ("w_conv_out", "w_attn_out", "w_o")
CROSS_MATS = ("w_cq", "w_co")

_ANY = pl.BlockSpec(memory_space=pl.ANY)


def _cparams(n_axes=1):
    return pltpu.CompilerParams(
        dimension_semantics=("arbitrary",) * n_axes, vmem_limit_bytes=VMEM_LIMIT)


def _row_spec(tm, n):
    return pl.BlockSpec((tm, n), lambda i: (i, 0))


def _blk_row_spec(nb, tm, n):
    return pl.BlockSpec((nb, tm, n), lambda i: (0, i, 0))


def _const_spec(shape):
    zeros = (0,) * len(shape)
    return pl.BlockSpec(shape, lambda i: zeros)


def _dot(a, b):
    return jnp.dot(a, b, preferred_element_type=F32)


def _dot_nt(a, b):
    return lax.dot_general(a, b, (((1,), (1,)), ((), ())), preferred_element_type=F32)


def _dot_tn(a, b):
    return lax.dot_general(a, b, (((0,), (0,)), ((), ())), preferred_element_type=F32)


def _sigmoid(x):
    return 1.0 / (1.0 + jnp.exp(-x))


def _call(body, operands, *, grid, in_specs, out_specs, out_shape, scratch_shapes, name, comm=None):
    n_in, n_out, n_sc = len(in_specs), len(out_specs), len(scratch_shapes)
    if comm is None:
        outs = pl.pallas_call(
            body, grid=grid, name=name, in_specs=in_specs, out_specs=out_specs, out_shape=out_shape,
            scratch_shapes=scratch_shapes, compiler_params=_cparams(len(grid)))(*operands)
        return list(outs), []
    c_in, c_out, c_sem = len(comm.inputs), len(comm.out_shapes), len(comm.sem_shapes)

    def hosted(*refs):
        bounds = [0, n_in, c_in, n_out, c_out, n_sc, c_sem]
        parts, pos = [], 0
        for k in bounds[1:]:
            parts.append(refs[pos:pos + k])
            pos += k
        ins, cins, outs, couts, scr, sems = parts
        step, n_steps = pl.program_id(0), grid[0]
        for ax in range(1, len(grid)):
            step, n_steps = step * grid[ax] + pl.program_id(ax), n_steps * grid[ax]

        @pl.when(step == 0)
        def _():
            comm.start(cins, couts, sems)

        @pl.when(step == (2 * n_steps) // 3)
        def _():
            comm.middle(cins, couts, sems)

        body(*ins, *outs, *scr)

        @pl.when(step == n_steps - 1)
        def _():
            comm.finish(cins, couts, sems)

    res = pl.pallas_call(
        hosted, grid=grid, name=name, in_specs=list(in_specs) + [_ANY] * c_in,
        out_specs=list(out_specs) + [_ANY] * c_out, out_shape=list(out_shape) + list(comm.out_shapes),
        scratch_shapes=list(scratch_shapes) + list(comm.sem_shapes),
        compiler_params=_cparams(len(grid)))(*operands, *comm.inputs)
    return list(res[:n_out]), list(res[n_out:])


def _load_resident(step, pairs, sems):
    @pl.when(step == 0)
    def _():
        copies = [pltpu.make_async_copy(src, dst, sems.at[k]) for k, (src, dst) in enumerate(pairs)]
        for cp in copies:
            cp.start()
        for cp in copies:
            cp.wait()


def _square_pairs(buf_hbm, index, dst):
    off = index * SQ_ROWS
    return [(buf_hbm.at[d, off:off + SQ_ROWS, :], dst.at[d * SQ_ROWS:(d + 1) * SQ_ROWS, :]) for d in range(N_DEV)]


def _down_pairs(wd_hbm, dst):
    return [(wd_hbm.at[d], dst.at[d // 2, (d % 2) * DOWN_ROWS:(d % 2 + 1) * DOWN_ROWS, :]) for d in range(N_DEV)]


def _zero_down_pad(step, dst):
    @pl.when(step == 0)
    def _():
        dst[:, FF_BLK:, :] = jnp.zeros((4, FF_PAD - FF_BLK, D), BF16)


def _rms_fwd_tile(xt, g):
    r = lax.rsqrt(jnp.mean(xt * xt, axis=-1, keepdims=True) + RMS_EPS)
    return (xt * r) * g


def _rms_bwd_tile(xt, g, dn):
    r = lax.rsqrt(jnp.mean(xt * xt, axis=-1, keepdims=True) + RMS_EPS)
    xhat = xt * r
    dxhat = dn * g
    dx = r * (dxhat - xhat * jnp.mean(dxhat * xhat, axis=-1, keepdims=True))
    dg = jnp.sum(dn * xhat, axis=0, keepdims=True)
    return dx, dg


def _accumulate(ref, step, value):
    @pl.when(step == 0)
    def _():
        ref[...] = value

    @pl.when(step != 0)
    def _():
        ref[...] = ref[...] + value


def _ffn_fwd(x, g, wgu, wd, name, comm=None, head=None):
    t = x.shape[0]

    def body(x_ref, g_ref, wgu_hbm, wd_hbm, *refs):
        if head is None:
            n_ref, gate_ref, up_ref, act_ref, h_ref, wgu_v, wd_v, sems = refs
        else:
            gf_ref, t_ref, n_ref, gate_ref, up_ref, act_ref, dh_ref, loss_ref, dgf_ref, wgu_v, wd_v, sems = refs
        step = pl.program_id(0)
        _zero_down_pad(step, wd_v)
        _load_resident(step, [(wgu_hbm, wgu_v)] + _down_pairs(wd_hbm, wd_v), sems)
        xt = x_ref[...]
        n = _rms_fwd_tile(xt, g_ref[...]).astype(BF16)
        n_ref[...] = n
        acc = jnp.zeros((TM, D), F32)
        for j in range(4):
            for s in range(FF_PAD // FF_SUB):
                lo, hi = s * FF_SUB, (s + 1) * FF_SUB
                gt = _dot_nt(n, wgu_v[j, lo:hi, :])
                ut = _dot_nt(n, wgu_v[4 + j, lo:hi, :])
                gate_ref[j, :, lo:hi] = gt.astype(BF16)
                up_ref[j, :, lo:hi] = ut.astype(BF16)
                act_ref[j, :, lo:hi] = ((gt * _sigmoid(gt)) * ut).astype(BF16)
            acc = acc + _dot(act_ref[j], wd_v[j])
        ht = xt + 0.5 * acc
        if head is None:
            h_ref[...] = ht
        else:
            gain = gf_ref[...]
            diff = _rms_fwd_tile(ht, gain) - t_ref[...]
            part = 0.5 * jnp.sum(jnp.sum(diff * diff, axis=-1, keepdims=True) / D, axis=0, keepdims=True)
            dx, dg = _rms_bwd_tile(ht, gain, diff / D)
            dh_ref[...] = dx
            _accumulate(loss_ref, step, jnp.broadcast_to(part, (8, 128)))
            _accumulate(dgf_ref, step, dg)

    ff = jax.ShapeDtypeStruct((4, t, FF_PAD), BF16)
    operands, in_specs = (x, g, wgu, wd), [_row_spec(TM, D), _const_spec((1, D)), _ANY, _ANY]
    out_specs = [_row_spec(TM, D)] + [_blk_row_spec(4, TM, FF_PAD)] * 3 + [_row_spec(TM, D)]
    out_shape = [jax.ShapeDtypeStruct((t, D), BF16), ff, ff, ff, jax.ShapeDtypeStruct((t, D), F32)]
    if head is not None:
        operands += tuple(head)
        in_specs += [_const_spec((1, D)), _row_spec(TM, D)]
        out_specs += [_const_spec((8, 128)), _const_spec((1, D))]
        out_shape += [jax.ShapeDtypeStruct((8, 128), F32), jax.ShapeDtypeStruct((1, D), F32)]
    return _call(
        body, operands, grid=(t // TM,), name=name, comm=comm, in_specs=in_specs, out_specs=out_specs,
        out_shape=out_shape,
        scratch_shapes=[pltpu.VMEM((N_DEV, FF_PAD, D), BF16), pltpu.VMEM((4, FF_PAD, D), BF16),
                        pltpu.SemaphoreType.DMA((1 + N_DEV,))])


def _ffn_bwd(dh, xin, g, gate, up, wgu, wd, name, comm=None):
    t = dh.shape[0]

    def body(dh_ref, x_ref, g_ref, gate_ref, up_ref, wgu_hbm, wd_hbm,
             dgu_ref, dhb_ref, dx_ref, dg_ref, wgu_v, wd_v, sems):
        step = pl.program_id(0)
        _zero_down_pad(step, wd_v)
        _load_resident(step, [(wgu_hbm, wgu_v)] + _down_pairs(wd_hbm, wd_v), sems)
        dht = dh_ref[...]
        dhb = (0.5 * dht).astype(BF16)
        dhb_ref[...] = dhb
        dn = jnp.zeros((TM, D), F32)
        for j in range(4):
            for s in range(FF_PAD // FF_SUB):
                lo, hi = s * FF_SUB, (s + 1) * FF_SUB
                da = _dot_nt(dhb, wd_v[j, lo:hi, :])
                gt = gate_ref[j, :, lo:hi].astype(F32)
                ut = up_ref[j, :, lo:hi].astype(F32)
                sg = _sigmoid(gt)
                dgt = (da * ut * (sg * (1.0 + gt * (1.0 - sg)))).astype(BF16)
                dut = (da * (gt * sg)).astype(BF16)
                dgu_ref[j, :, lo:hi] = dgt
                dgu_ref[4 + j, :, lo:hi] = dut
            dn = dn + _dot(dgu_ref[j], wgu_v[j]) + _dot(dgu_ref[4 + j], wgu_v[4 + j])
        dx, dg = _rms_bwd_tile(x_ref[...], g_ref[...], dn)
        dx_ref[...] = dht + dx
        _accumulate(dg_ref, step, dg)

    return _call(
        body, (dh, xin, g, gate, up, wgu, wd), grid=(t // TM,), name=name, comm=comm,
        in_specs=[_row_spec(TM, D), _row_spec(TM, D), _const_spec((1, D)), _blk_row_spec(4, TM, FF_PAD),
                  _blk_row_spec(4, TM, FF_PAD), _ANY, _ANY],
        out_specs=[_blk_row_spec(N_DEV, TM, FF_PAD), _row_spec(TM, D), _row_spec(TM, D), _const_spec((1, D))],
        out_shape=[jax.ShapeDtypeStruct((N_DEV, t, FF_PAD), BF16), jax.ShapeDtypeStruct((t, D), BF16),
                   jax.ShapeDtypeStruct((t, D), F32), jax.ShapeDtypeStruct((1, D), F32)],
        scratch_shapes=[pltpu.VMEM((N_DEV, FF_PAD, D), BF16), pltpu.VMEM((4, FF_PAD, D), BF16),
                        pltpu.SemaphoreType.DMA((1 + N_DEV,))])


WIDE_TILES = (1024, 512, 256, 128)


def _pick_tile(n, options=(512, 256, 128)):
    for o in options:
        if n % o == 0:
            return o
    return n


def _mm_tn_squares(pairs, name):
    k, d = pairs[0][0].shape
    n = pairs[0][1].shape[1]
    tn = _pick_tile(n)
    count = len(pairs)

    def body(*refs):
        m = pl.program_id(0)
        for idx in range(count):
            @pl.when(m == idx)
            def _(idx=idx):
                refs[-1][...] = _dot_tn(refs[idx][...], refs[count + idx][...]).astype(BF16).reshape(
                    N_DEV, d // N_DEV, tn)

    a_specs = [pl.BlockSpec((k, d), lambda m, j: (0, 0)) for _ in pairs]
    b_specs = [pl.BlockSpec((k, tn), lambda m, j, idx=idx: (0, jnp.where(m == idx, j, 0))) for idx in range(count)]
    return pl.pallas_call(
        body, grid=(count, n // tn), name=name, in_specs=a_specs + b_specs,
        out_specs=pl.BlockSpec((N_DEV, d // N_DEV, tn), lambda m, j: (0, m, j)),
        out_shape=jax.ShapeDtypeStruct((N_DEV, count * (d // N_DEV), n), BF16),
        compiler_params=_cparams(2),
    )(*[a for a, _ in pairs], *[b for _, b in pairs])


def _mm_tn_cols_many(a, parts, name):
    k, m = a.shape
    n = parts[0].shape[2]
    tn = _pick_tile(n)
    firsts, total = [], 0
    for p in parts:
        firsts.append(total)
        total += p.shape[0]

    def body(a_ref, *refs):
        j = pl.program_id(0)
        for p, first, ref in zip(parts, firsts, refs):
            @pl.when(jnp.logical_and(j >= first, j < first + p.shape[0]))
            def _(ref=ref):
                refs[-1][0] = _dot_tn(a_ref[...].astype(BF16), ref[0].astype(BF16)).astype(BF16)

    specs = [pl.BlockSpec((1, k, tn), lambda j, i, first=first, last=p.shape[0] - 1:
                          (jnp.clip(j - first, 0, last), 0, jnp.where(jnp.logical_and(j >= first, j <= first + last), i, 0)))
             for p, first in zip(parts, firsts)]
    return pl.pallas_call(
        body, grid=(total, n // tn), name=name,
        in_specs=[pl.BlockSpec((k, m), lambda j, i: (0, 0))] + specs,
        out_specs=pl.BlockSpec((1, m, tn), lambda j, i: (j, 0, i)),
        out_shape=jax.ShapeDtypeStruct((total, m, n), BF16),
        compiler_params=_cparams(2),
    )(a, *parts)


def _mm_tn_cols(a, b, name):
    k, m = a.shape
    nb, _, n = b.shape
    tm = _pick_tile(m, WIDE_TILES)

    def body(a_ref, b_ref, o_ref):
        o_ref[0] = _dot_tn(a_ref[...].astype(BF16), b_ref[0].astype(BF16)).astype(BF16)

    return pl.pallas_call(
        body, grid=(nb, m // tm), name=name,
        in_specs=[pl.BlockSpec((k, tm), lambda j, i: (0, i)), pl.BlockSpec((1, k, n), lambda j, i: (j, 0, 0))],
        out_specs=pl.BlockSpec((1, tm, n), lambda j, i: (j, i, 0)),
        out_shape=jax.ShapeDtypeStruct((nb, m, n), BF16),
        compiler_params=_cparams(2),
    )(a, b)


def _mm_tn_rows(a, b, keep, name, comm=None):
    nb, k, m = a.shape
    _, n = b.shape
    tn = _pick_tile(n, WIDE_TILES)

    def body(a_ref, b_ref, o_ref):
        o_ref[0] = _dot_tn(a_ref[0], b_ref[...])[:keep].astype(BF16)

    (out,), couts = _call(
        body, (a, b), grid=(nb, n // tn), name=name, comm=comm,
        in_specs=[pl.BlockSpec((1, k, m), lambda j, i: (j, 0, 0)), pl.BlockSpec((k, tn), lambda j, i: (0, i))],
        out_specs=[pl.BlockSpec((1, keep, tn), lambda j, i: (j, 0, i))],
        out_shape=[jax.ShapeDtypeStruct((nb, keep, n), BF16)], scratch_shapes=[])
    return out if comm is None else (out, couts)


PCG_W = 5 * D
QKV_W = 3 * D
PROJ_SUB = 512


def _inproj_fwd(h, g, w_in, conv_w, name, comm=None):
    t = h.shape[0]

    def body(h_ref, g_ref, w_hbm, cw_ref, u_ref, pcg_ref, qkv_ref, yc_ref, w_v, tail_v, sems):
        step = pl.program_id(0)
        _load_resident(step, [(w_hbm, w_v)], sems)

        @pl.when(step == 0)
        def _():
            tail_v[...] = jnp.zeros_like(tail_v)

        u = _rms_fwd_tile(h_ref[...], g_ref[...]).astype(BF16)
        u_ref[...] = u
        for blk in range(N_DEV):
            for s in range(D // PROJ_SUB):
                lo, hi = s * PROJ_SUB, (s + 1) * PROJ_SUB
                p = _dot(u, w_v[blk, :, lo:hi])
                if blk < 3:
                    pcg_ref[:, blk * D + lo:blk * D + hi] = p
                elif blk < 6:
                    qkv_ref[:, (blk - 3) * D + lo:(blk - 3) * D + hi] = p.astype(BF16)
                else:
                    pcg_ref[:, (blk - 3) * D + lo:(blk - 3) * D + hi] = p
        xc = pcg_ref[:, D:2 * D] * pcg_ref[:, 2 * D:3 * D]
        ext = jnp.concatenate([tail_v[...], xc], axis=0)
        conv = (cw_ref[0:1, :] * pltpu.roll(ext, 2, 0)[8:] + cw_ref[1:2, :] * pltpu.roll(ext, 1, 0)[8:]
                + cw_ref[2:3, :] * xc)
        yc_ref[...] = (pcg_ref[:, 0:D] * conv).astype(BF16)
        tail_v[...] = xc[TM - 8:]

    return _call(
        body, (h, g, w_in, conv_w), grid=(t // TM,), name=name, comm=comm,
        in_specs=[_row_spec(TM, D), _const_spec((1, D)), _ANY, _const_spec((CONV_K, D))],
        out_specs=[_row_spec(TM, D), _row_spec(TM, PCG_W), _row_spec(TM, QKV_W), _row_spec(TM, D)],
        out_shape=[jax.ShapeDtypeStruct((t, D), BF16), jax.ShapeDtypeStruct((t, PCG_W), F32),
                   jax.ShapeDtypeStruct((t, QKV_W), BF16), jax.ShapeDtypeStruct((t, D), BF16)],
        scratch_shapes=[pltpu.VMEM((N_DEV, D, D), BF16), pltpu.VMEM((8, D), F32), pltpu.SemaphoreType.DMA((1,))])


def _tri2(cond):
    rr = lax.broadcasted_iota(jnp.int32, (2 * TK, TK), 0) & (TK - 1)
    cc = lax.broadcasted_iota(jnp.int32, (2 * TK, TK), 1)
    return cond(rr, cc).astype(BF16)


def _causal(shift, row0=0):
    rr = lax.broadcasted_iota(jnp.int32, (TQ - row0, TK), 0) + row0
    cc = lax.broadcasted_iota(jnp.int32, (TQ - row0, TK), 1)
    return cc + shift < rr


def _cumdot(v, tri2):
    hi = v.astype(BF16)
    lo = (v - hi.astype(F32)).astype(BF16)
    return _dot(jnp.concatenate([hi, lo], axis=1), tri2)


def _log_1m_beta(z):
    return -(jnp.maximum(z, 0.0) + jnp.log(1.0 + jnp.exp(-jnp.abs(z))))


def _sb_specs(t):
    g = SB_H // SB_HPS
    w = SB_HPS * SB_DH
    q_spec = pl.BlockSpec((TQ, w), lambda h, i: (i, h))
    k_spec = pl.BlockSpec((t, w), lambda h, i: (0, g + h))
    v_spec = pl.BlockSpec((t, w), lambda h, i: (0, 2 * g + h))
    ct_spec = pl.BlockSpec((SB_HPS, TQ, 1), lambda h, i: (h, i, 0))
    return g, w, q_spec, k_spec, v_spec, ct_spec


def _sb_fwd(qkv, name, comm=None):
    t = qkv.shape[0]
    scale = SB_DH ** -0.5
    g, w, q_spec, k_spec, v_spec, ct_spec = _sb_specs(t)

    def body(q_ref, k_ref, v_ref, y_ref, ct_ref):
        i = pl.program_id(1)
        later = _tri2(lambda j, s: j > s)
        n_diag = TQ // TK

        def block(j, carry, shift):
            off = pl.multiple_of(j * TK, TK)
            zs, ms = [], []
            for hd in range(SB_HPS):
                cols = slice(hd * SB_DH, (hd + 1) * SB_DH)
                z = _dot_nt(q_ref[:, cols], k_ref[pl.ds(off, TK), cols]) * scale
                m = _log_1m_beta(z)
                if shift is not None:
                    m = jnp.where(_causal(shift), m, 0.0)
                zs.append(z)
                ms.append(m)
            after = _cumdot(jnp.concatenate(ms, axis=0), later)
            out = []
            for hd in range(SB_HPS):
                acc, c_sum = carry[hd]
                cols = slice(hd * SB_DH, (hd + 1) * SB_DH)
                a = jnp.exp((ms[hd] + zs[hd]) + (c_sum + after[hd * TQ:(hd + 1) * TQ]))
                if shift is not None:
                    a = jnp.where(_causal(shift), a, 0.0)
                out.append((acc + _dot(a.astype(BF16), v_ref[pl.ds(off, TK), cols]),
                            c_sum + jnp.sum(ms[hd], axis=1, keepdims=True)))
            return tuple(out)

        carry = tuple((jnp.zeros((TQ, SB_DH), F32), jnp.zeros((TQ, 1), F32)) for _ in range(SB_HPS))
        for d in reversed(range(n_diag)):
            carry = block(i * n_diag + d, carry, d * TK)
        carry = lax.fori_loop(0, i * n_diag, lambda jj, c: block(i * n_diag - 1 - jj, c, None), carry)
        for hd in range(SB_HPS):
            y_ref[:, hd * SB_DH:(hd + 1) * SB_DH] = carry[hd][0].astype(BF16)
            ct_ref[hd] = carry[hd][1]

    return _call(
        body, (qkv, qkv, qkv), grid=(g, t // TQ), name=name, comm=comm,
        in_specs=[q_spec, k_spec, v_spec],
        out_specs=[q_spec, ct_spec],
        out_shape=[jax.ShapeDtypeStruct((t, D), BF16), jax.ShapeDtypeStruct((SB_H, t, 1), F32)],
        scratch_shapes=[])


def _sb_bwd(qkv, dy, ctot, after, name, comm=None):
    t = qkv.shape[0]
    scale = SB_DH ** -0.5
    g, w, q_spec, k_spec, v_spec, ct_spec = _sb_specs(t)
    acc_spec = pl.BlockSpec((2, t, w), lambda h, i: (0, 0, h))

    def body(q_ref, k_ref, v_ref, dy_ref, ct_ref, after_ref, dq_ref, dkv_ref):
        i = pl.program_id(1)

        @pl.when(i == 0)
        def _():
            dkv_ref[...] = jnp.zeros_like(dkv_ref)

        upto = _tri2(lambda j, s: j <= s)
        n_diag = TQ // TK

        def block(j, carry, shift):
            off = pl.multiple_of(j * TK, TK)
            r0 = 0 if shift is None else shift
            nr = TQ - r0
            causal = None if shift is None else _causal(shift, r0)

            def grow(old, delta):
                return old + delta if r0 == 0 else jnp.concatenate([old[:r0], old[r0:] + delta], axis=0)

            zs, ms = [], []
            for hd in range(SB_HPS):
                cols = slice(hd * SB_DH, (hd + 1) * SB_DH)
                z = _dot_nt(q_ref[r0:, cols], k_ref[pl.ds(off, TK), cols]) * scale
                m = _log_1m_beta(z)
                if causal is not None:
                    m = jnp.where(causal, m, 0.0)
                zs.append(z)
                ms.append(m)
            m_upto = _cumdot(jnp.concatenate(ms, axis=0), upto)
            ls, a_s, es = [], [], []
            for hd in range(SB_HPS):
                cols = slice(hd * SB_DH, (hd + 1) * SB_DH)
                l = ms[hd] + zs[hd]
                a = jnp.exp(l + ((ct_ref[hd, r0:] - carry[hd][1][r0:]) - m_upto[hd * nr:(hd + 1) * nr]))
                if causal is not None:
                    a = jnp.where(causal, a, 0.0)
                ls.append(l)
                a_s.append(a)
                es.append(_dot_nt(dy_ref[r0:, cols], v_ref[pl.ds(off, TK), cols]) * a)
            e_upto = _dot(jnp.concatenate(es, axis=0).astype(BF16), upto[:TK])
            out = []
            for hd in range(SB_HPS):
                dq, p_sum, e_sum = carry[hd]
                cols = slice(hd * SB_DH, (hd + 1) * SB_DH)
                e = es[hd]
                dz = e - jnp.exp(ls[hd]) * (e_sum[r0:] + e_upto[hd * nr:(hd + 1) * nr])
                if causal is not None:
                    dz = jnp.where(causal, dz, 0.0)
                dzs = (dz * scale).astype(BF16)
                dkv_ref[0, pl.ds(off, TK), cols] += _dot_tn(dzs, q_ref[r0:, cols])
                dkv_ref[1, pl.ds(off, TK), cols] += _dot_tn(a_s[hd].astype(BF16), dy_ref[r0:, cols])
                out.append((grow(dq, _dot(dzs, k_ref[pl.ds(off, TK), cols])),
                            grow(p_sum, jnp.sum(ms[hd], axis=1, keepdims=True)),
                            grow(e_sum, jnp.sum(e, axis=1, keepdims=True))))
            return tuple(out)

        zero = jnp.zeros((TQ, 1), F32)
        init = tuple((jnp.zeros((TQ, SB_DH), F32), zero, zero) for _ in range(SB_HPS))
        carry = lax.fori_loop(0, i * n_diag, lambda j, c: block(j, c, None), init)
        for d in range(n_diag):
            carry = block(i * n_diag + d, carry, d * TK)
        for hd in range(SB_HPS):
            dq_ref[:, hd * SB_DH:(hd + 1) * SB_DH] = carry[hd][0].astype(BF16)

    return _call(
        body, (qkv, qkv, qkv, dy, ctot, after), grid=(g, t // TQ), name=name, comm=comm,
        in_specs=[q_spec, k_spec, v_spec, q_spec, ct_spec, pl.BlockSpec(after.shape, lambda h, i: (0, 0))],
        out_specs=[q_spec, acc_spec],
        out_shape=[jax.ShapeDtypeStruct((t, D), BF16), jax.ShapeDtypeStruct((2, t, D), F32)],
        scratch_shapes=[])


def _gate_specs():
    return [pl.BlockSpec((TM, D), lambda i: (i, 3)), pl.BlockSpec((TM, D), lambda i: (i, 4))]


def _mix_pairs(mix_hbm, dsts):
    pairs = []
    for index, dst in enumerate(dsts):
        pairs += _square_pairs(mix_hbm, index, dst)
    return pairs


def _mix_out_fwd(yc, ysb, pcg, b_gate, h, w_mix, name, comm=None):
    t = h.shape[0]

    def body(yc_ref, ysb_ref, gc_ref, gs_ref, b_ref, h_ref, mix_hbm,
             a_ref, b_out_ref, mg_ref, h2_ref, wc_v, wa_v, wo_v, sems):
        _load_resident(pl.program_id(0), _mix_pairs(mix_hbm, (wc_v, wa_v, wo_v)), sems)
        a = _dot(yc_ref[...], wc_v[...])
        b = _dot(ysb_ref[...], wa_v[...])
        merged = (_sigmoid(gc_ref[...] + b_ref[:, :D]) * a + _sigmoid(gs_ref[...] + b_ref[:, D:]) * b).astype(BF16)
        a_ref[...] = a
        b_out_ref[...] = b
        mg_ref[...] = merged
        h2_ref[...] = h_ref[...] + _dot(merged, wo_v[...])

    return _call(
        body, (yc, ysb, pcg, pcg, b_gate, h, w_mix), grid=(t // TM,), name=name, comm=comm,
        in_specs=[_row_spec(TM, D), _row_spec(TM, D)] + _gate_specs()
                 + [_const_spec((1, 2 * D)), _row_spec(TM, D), _ANY],
        out_specs=[_row_spec(TM, D)] * 4,
        out_shape=[jax.ShapeDtypeStruct((t, D), F32), jax.ShapeDtypeStruct((t, D), F32),
                   jax.ShapeDtypeStruct((t, D), BF16), jax.ShapeDtypeStruct((t, D), F32)],
        scratch_shapes=[pltpu.VMEM((D, D), BF16)] * 3 + [pltpu.SemaphoreType.DMA((3 * N_DEV,))])


def _mix_out_bwd(dh2, a, b, pcg, b_gate, conv_w, w_mix, name, comm=None):
    t = dh2.shape[0]
    n_tile = t // TM
    per8 = TM // 8

    def rows(n):
        return pl.BlockSpec((TM, n), lambda i: (n_tile - 1 - i, 0))

    def cols(block):
        return pl.BlockSpec((TM, D), lambda i: (n_tile - 1 - i, block))

    def before(block):
        return pl.BlockSpec((8, D), lambda i: (jnp.maximum((n_tile - 1 - i) * per8 - 1, 0), block))

    def body(dh_ref, a_ref, b_ref, gc_ref, gs_ref, cb_ref, cc_ref, cx_ref, ccp_ref, cxp_ref, bias_ref, cw_ref, mix_hbm,
             dhb_ref, da_ref, db_ref, dgp_ref, dc_ref, dysb_ref, dbias_ref, dcw_ref, wc_v, wa_v, wo_v, head_v, sems):
        step = pl.program_id(0)
        _load_resident(step, _mix_pairs(mix_hbm, (wc_v, wa_v, wo_v)), sems)

        @pl.when(step == 0)
        def _():
            head_v[...] = jnp.zeros_like(head_v)
            dcw_ref[...] = jnp.zeros_like(dcw_ref)

        dhb = dh_ref[...].astype(BF16)
        dhb_ref[...] = dhb
        dm = _dot_nt(dhb, wo_v[...])
        gc = _sigmoid(gc_ref[...] + bias_ref[:, :D])
        gs = _sigmoid(gs_ref[...] + bias_ref[:, D:])
        da = (dm * gc).astype(BF16)
        db = (dm * gs).astype(BF16)
        da_ref[...] = da
        db_ref[...] = db
        dgc = dm * a_ref[...] * (gc * (1.0 - gc))
        dgs = dm * b_ref[...] * (gs * (1.0 - gs))
        dgp_ref[0] = dgc.astype(BF16)
        dgp_ref[1] = dgs.astype(BF16)
        _accumulate(dbias_ref.at[:, :D], step, jnp.sum(dgc, axis=0, keepdims=True))
        _accumulate(dbias_ref.at[:, D:], step, jnp.sum(dgs, axis=0, keepdims=True))
        dysb_ref[...] = _dot_nt(db, wa_v[...]).astype(BF16)
        dyc = _dot_nt(da, wc_v[...])
        cc, cx = cc_ref[...], cx_ref[...]
        xc = cc * cx
        xc_before = jnp.where(step == n_tile - 1, 0.0, ccp_ref[...] * cxp_ref[...])
        ext = jnp.concatenate([xc_before, xc], axis=0)
        x1 = pltpu.roll(ext, 1, 0)[8:]
        x2 = pltpu.roll(ext, 2, 0)[8:]
        w0, w1, w2 = cw_ref[0:1, :], cw_ref[1:2, :], cw_ref[2:3, :]
        dc_ref[0] = (dyc * (w0 * x2 + w1 * x1 + w2 * xc)).astype(BF16)
        dconv = dyc * cb_ref[...]
        dcw_ref[0:1, :] += jnp.sum(dconv * x2, axis=0, keepdims=True)
        dcw_ref[1:2, :] += jnp.sum(dconv * x1, axis=0, keepdims=True)
        dcw_ref[2:3, :] += jnp.sum(dconv * xc, axis=0, keepdims=True)
        after = jnp.concatenate([dconv, head_v[...]], axis=0)
        dxc = w2 * dconv + w1 * pltpu.roll(after, TM + 7, 0)[:TM] + w0 * pltpu.roll(after, TM + 6, 0)[:TM]
        dc_ref[1] = (dxc * cx).astype(BF16)
        dc_ref[2] = (dxc * cc).astype(BF16)
        head_v[...] = dconv[:8]

    return _call(
        body, (dh2, a, b, pcg, pcg, pcg, pcg, pcg, pcg, pcg, b_gate, conv_w, w_mix), grid=(n_tile,), name=name,
        comm=comm,
        in_specs=[rows(D)] * 3 + [cols(3), cols(4), cols(0), cols(1), cols(2), before(1), before(2),
                                  _const_spec((1, 2 * D)), _const_spec((CONV_K, D)), _ANY],
        out_specs=[rows(D)] * 3 + [pl.BlockSpec((2, TM, D), lambda i: (0, n_tile - 1 - i, 0)),
                                   pl.BlockSpec((3, TM, D), lambda i: (0, n_tile - 1 - i, 0)), rows(D),
                                   _const_spec((1, 2 * D)), _const_spec((8, D))],
        out_shape=[jax.ShapeDtypeStruct((t, D), BF16)] * 3
                  + [jax.ShapeDtypeStruct((2, t, D), BF16), jax.ShapeDtypeStruct((3, t, D), BF16),
                     jax.ShapeDtypeStruct((t, D), BF16), jax.ShapeDtypeStruct((1, 2 * D), F32),
                     jax.ShapeDtypeStruct((8, D), F32)],
        scratch_shapes=[pltpu.VMEM((D, D), BF16)] * 3 + [pltpu.VMEM((8, D), F32),
                                                         pltpu.SemaphoreType.DMA((3 * N_DEV,))])


def _inproj_bwd(dconv, dq, dkv, dgp, w_in, h, g, dh_res, name, comm=None):
    t = h.shape[0]

    def body(dc_ref, dq_ref, dkv_ref, dgp_ref, w_hbm, h_ref, g_ref, dres_ref, dh_ref, dg_ref, w_v, sems):
        step = pl.program_id(0)
        _load_resident(step, [(w_hbm, w_v)], sems)
        du = _dot_nt(dq_ref[...], w_v[3])
        for k in range(3):
            du = du + _dot_nt(dc_ref[k], w_v[k])
        for k in range(2):
            du = du + _dot_nt(dkv_ref[k].astype(BF16), w_v[4 + k]) + _dot_nt(dgp_ref[k], w_v[6 + k])
        dx, dg = _rms_bwd_tile(h_ref[...], g_ref[...], du)
        dh_ref[...] = dres_ref[...] + dx
        _accumulate(dg_ref, step, dg)

    return _call(
        body, (dconv, dq, dkv, dgp, w_in, h, g, dh_res), grid=(t // TM,), name=name, comm=comm,
        in_specs=[_blk_row_spec(3, TM, D), _row_spec(TM, D), _blk_row_spec(2, TM, D), _blk_row_spec(2, TM, D), _ANY,
                  _row_spec(TM, D), _const_spec((1, D)), _row_spec(TM, D)],
        out_specs=[_row_spec(TM, D), _const_spec((1, D))],
        out_shape=[jax.ShapeDtypeStruct((t, D), F32), jax.ShapeDtypeStruct((1, D), F32)],
        scratch_shapes=[pltpu.VMEM((N_DEV, D, D), BF16), pltpu.SemaphoreType.DMA((1,))])


def _softmax_rows(s):
    e = jnp.exp(s - jnp.max(s, axis=-1, keepdims=True))
    return e / jnp.sum(e, axis=-1, keepdims=True)


def _cross_pairs(cross_hbm, wq_v, wo_v):
    return _square_pairs(cross_hbm, 0, wq_v) + _square_pairs(cross_hbm, 1, wo_v)


def _cross_fwd(h, g, mem, g_mem, w_ckv, w_cross, name):
    t = h.shape[0]
    m = mem.shape[0]
    scale = X_DH ** -0.5

    def body(h_ref, g_ref, mem_ref, gm_ref, wkv_ref, cross_hbm, hn_ref, qx_ref, o_ref, h3_ref, mn_ref, kv_ref,
             wq_v, wo_v, sems):
        _load_resident(pl.program_id(0), _cross_pairs(cross_hbm, wq_v, wo_v), sems)

        @pl.when(pl.program_id(0) == 0)
        def _():
            mn = _rms_fwd_tile(mem_ref[...], gm_ref[...]).astype(BF16)
            mn_ref[...] = mn
            for j in range(N_DEV):
                kv_ref[j] = _dot(mn, wkv_ref[j]).astype(BF16)

        ht = h_ref[...]
        hn = _rms_fwd_tile(ht, g_ref[...]).astype(BF16)
        hn_ref[...] = hn
        qx = _dot(hn, wq_v[...]).astype(BF16)
        qx_ref[...] = qx
        for hd in range(X_H):
            lo, hi = hd * X_DH, (hd + 1) * X_DH
            p = _softmax_rows(_dot_nt(qx[:, lo:hi], kv_ref[hd]) * scale)
            o_ref[:, lo:hi] = _dot(p.astype(BF16), kv_ref[X_H + hd]).astype(BF16)
        h3_ref[...] = ht + _dot(o_ref[...], wo_v[...])

    return pl.pallas_call(
        body, grid=(t // TM,), name=name,
        in_specs=[_row_spec(TM, D), _const_spec((1, D)), _const_spec((m, D)), _const_spec((1, D)),
                  _const_spec((N_DEV, D, X_DH)), _ANY],
        out_specs=[_row_spec(TM, D)] * 4 + [_const_spec((m, D)), _const_spec((N_DEV, m, X_DH))],
        out_shape=[jax.ShapeDtypeStruct((t, D), BF16)] * 3 + [jax.ShapeDtypeStruct((t, D), F32),
                                                              jax.ShapeDtypeStruct((m, D), BF16),
                                                              jax.ShapeDtypeStruct((N_DEV, m, X_DH), BF16)],
        scratch_shapes=[pltpu.VMEM((D, D), BF16)] * 2 + [pltpu.SemaphoreType.DMA((2 * N_DEV,))],
        compiler_params=_cparams(),
    )(h, g, mem, g_mem, w_ckv, w_cross)


def _cross_bwd(dh3, h, g, qx, kv, mem, g_mem, w_ckv, w_cross, name, comm=None):
    t = h.shape[0]
    m = kv.shape[1]
    scale = X_DH ** -0.5

    def body(dh_ref, h_ref, g_ref, qx_ref, kv_ref, mem_ref, gm_ref, wkv_ref, cross_hbm,
             dhb_ref, dqx_ref, dkv_ref, dh2_ref, dg_ref, dgm_ref, wq_v, wo_v, sems):
        step = pl.program_id(0)
        _load_resident(step, _cross_pairs(cross_hbm, wq_v, wo_v), sems)

        @pl.when(step == 0)
        def _():
            dkv_ref[...] = jnp.zeros_like(dkv_ref)

        dht = dh_ref[...]
        dhb = dht.astype(BF16)
        dhb_ref[...] = dhb
        do = _dot_nt(dhb, wo_v[...]).astype(BF16)
        for hd in range(X_H):
            lo, hi = hd * X_DH, (hd + 1) * X_DH
            qh = qx_ref[:, lo:hi]
            kh = kv_ref[hd]
            p = _softmax_rows(_dot_nt(qh, kh) * scale)
            doh = do[:, lo:hi]
            dp = _dot_nt(doh, kv_ref[X_H + hd])
            ds = (p * (dp - jnp.sum(dp * p, axis=-1, keepdims=True)) * scale).astype(BF16)
            dqx_ref[:, lo:hi] = _dot(ds, kh).astype(BF16)
            dkv_ref[hd] += _dot_tn(ds, qh)
            dkv_ref[X_H + hd] += _dot_tn(p.astype(BF16), doh)
        dhn = _dot_nt(dqx_ref[...], wq_v[...])
        dx, dg = _rms_bwd_tile(h_ref[...], g_ref[...], dhn)
        dh2_ref[...] = dht + dx
        _accumulate(dg_ref, step, dg)

        @pl.when(step == t // TM - 1)
        def _():
            dmn = jnp.zeros((m, D), F32)
            for j in range(N_DEV):
                dmn = dmn + _dot_nt(dkv_ref[j].astype(BF16), wkv_ref[j])
            dgm_ref[...] = _rms_bwd_tile(mem_ref[...], gm_ref[...], dmn)[1]

    return _call(
        body, (dh3, h, g, qx, kv, mem, g_mem, w_ckv, w_cross), grid=(t // TM,), name=name, comm=comm,
        in_specs=[_row_spec(TM, D), _row_spec(TM, D), _const_spec((1, D)), _row_spec(TM, D),
                  _const_spec((N_DEV, m, X_DH)), _const_spec((m, D)), _const_spec((1, D)),
                  _const_spec((N_DEV, D, X_DH)), _ANY],
        out_specs=[_row_spec(TM, D), _row_spec(TM, D), _const_spec((N_DEV, m, X_DH)), _row_spec(TM, D),
                   _const_spec((1, D)), _const_spec((1, D))],
        out_shape=[jax.ShapeDtypeStruct((t, D), BF16), jax.ShapeDtypeStruct((t, D), BF16),
                   jax.ShapeDtypeStruct((N_DEV, m, X_DH), F32), jax.ShapeDtypeStruct((t, D), F32),
                   jax.ShapeDtypeStruct((1, D), F32), jax.ShapeDtypeStruct((1, D), F32)],
        scratch_shapes=[pltpu.VMEM((D, D), BF16)] * 2 + [pltpu.SemaphoreType.DMA((2 * N_DEV,))])


def _adamw(w, parts, m, v, name, row_block=0, token=None):
    r, c = w.shape
    n = parts.shape[0]
    tr = _pick_tile(r, (256, 352, 128))
    off = row_block * (r // tr)

    def body(*refs):
        if token is None:
            _adamw_update(None, *refs)
        else:
            _adamw_update(refs[4], *refs[:4], *refs[5:])

    spec = _row_spec(tr, c)
    in_specs = [spec, pl.BlockSpec((n, tr, c), lambda i: (0, i + off, 0)), spec, spec]
    operands = (w, parts, m, v)
    if token is not None:
        in_specs.append(_const_spec(token.shape))
        operands += (token,)
    return pl.pallas_call(
        body, grid=(r // tr,), name=name, in_specs=in_specs, out_specs=[spec] * 4,
        out_shape=[jax.ShapeDtypeStruct((r, c), F32)] * 4,
        compiler_params=_cparams(),
    )(*operands)


def _adamw_update(tok_ref, w_ref, p_ref, m_ref, v_ref, g_ref, d_ref, nm_ref, nv_ref):
    gt = p_ref[0].astype(F32)
    for k in range(1, p_ref.shape[0]):
        gt = gt + p_ref[k].astype(F32)
    if tok_ref is not None:
        gt = gt + tok_ref[0:1, 0:1]
    _adamw_apply(gt, w_ref, m_ref, v_ref, g_ref, d_ref, nm_ref, nv_ref)


def _adamw_own(w, land, own, chip, m, v, name, row_block=0, token=None):
    r, c = w.shape
    tr = _pick_tile(r, (256, 352, 128))
    off = row_block * (r // tr)

    def body(chip_ref, w_ref, land_ref, own_ref, m_ref, v_ref, *rest):
        mine = own_ref[0].astype(F32)
        gt = jnp.where(chip_ref[0] == 0, mine, land_ref[0].astype(F32))
        for k in range(1, N_CHIP):
            gt = gt + jnp.where(chip_ref[0] == k, mine, land_ref[k].astype(F32))
        if token is not None:
            gt = gt + rest[0][0:1, 0:1]
        _adamw_apply(gt, w_ref, m_ref, v_ref, *rest[-4:])

    spec = pl.BlockSpec((tr, c), lambda i, chip_ref: (i, 0))
    in_specs = [spec, pl.BlockSpec((N_CHIP, tr, c), lambda i, chip_ref: (0, i + off, 0)),
                pl.BlockSpec((1, tr, c), lambda i, chip_ref: (chip_ref[0], i + off, 0)), spec, spec]
    operands = (chip, w, land, own, m, v)
    if token is not None:
        in_specs.append(pl.BlockSpec(token.shape, lambda i, chip_ref: (0, 0)))
        operands += (token,)
    return pl.pallas_call(
        body, name=name,
        grid_spec=pltpu.PrefetchScalarGridSpec(
            num_scalar_prefetch=1, grid=(r // tr,), in_specs=in_specs, out_specs=[spec] * 4),
        out_shape=[jax.ShapeDtypeStruct((r, c), F32)] * 4,
        compiler_params=_cparams(),
    )(*operands)


def _adamw_apply(gt, w_ref, m_ref, v_ref, g_ref, d_ref, nm_ref, nv_ref):
    g_ref[...] = gt
    nm = ADAM_B1 * m_ref[...] + (1.0 - ADAM_B1) * gt
    nv = ADAM_B2 * v_ref[...] + (1.0 - ADAM_B2) * jnp.square(gt)
    m_hat = nm / (1.0 - ADAM_B1 ** ADAM_STEP)
    v_hat = nv / (1.0 - ADAM_B2 ** ADAM_STEP)
    d_ref[...] = -ADAM_LR * (m_hat / (jnp.sqrt(v_hat) + ADAM_EPS) + ADAM_WD * w_ref[...])
    nm_ref[...] = nm
    nv_ref[...] = nv


def _mesh_pos():
    return lax.axis_index("x"), lax.axis_index("y"), lax.axis_index("c")


def _both(first, second):
    n_in, n_out, n_sem = len(first.inputs), len(first.out_shapes), len(first.sem_shapes)

    def run(round_name):
        def both(in_refs, out_refs, sems):
            getattr(first, round_name)(in_refs[:n_in], out_refs[:n_out], sems[:n_sem])
            getattr(second, round_name)(in_refs[n_in:], out_refs[n_out:], sems[n_sem:])
        return both

    return types.SimpleNamespace(
        inputs=first.inputs + second.inputs, out_shapes=first.out_shapes + second.out_shapes,
        sem_shapes=first.sem_shapes + second.sem_shapes, start=run("start"), middle=run("middle"),
        finish=run("finish"))


def _no_round(in_refs, out_refs, sems):
    pass


def _run_exchange(comm, name):
    c_in, c_out = len(comm.inputs), len(comm.out_shapes)

    def body(*refs):
        cins, couts, sems = refs[:c_in], refs[c_in:c_in + c_out], refs[c_in + c_out:]
        comm.start(cins, couts, sems)
        comm.middle(cins, couts, sems)
        comm.finish(cins, couts, sems)

    return list(pl.pallas_call(
        body, name=name, out_shape=list(comm.out_shapes),
        in_specs=[_ANY] * c_in, out_specs=[_ANY] * c_out, scratch_shapes=list(comm.sem_shapes),
    )(*comm.inputs))


def _gather_exchange(shards):
    n_arr = len(shards)

    def plan(x_refs, out_refs, sems):
        send_sems, recv_sems, local_sems = sems[:3]
        stage = sems[3:]
        x, y, c = _mesh_pos()
        me, sibling = (x, y, c), (x, y, 1 - c)
        xn, yn, diag = (1 - x, y), (x, 1 - y), (1 - x, 1 - y)

        def slot(a, px, py, pc, half=None):
            ref = out_refs[a].at[4 * px + 2 * py + pc]
            if half is None:
                return ref
            rows = shards[a].shape[0] // 2
            return ref.at[half * rows:(half + 1) * rows]

        def copy(a, k, block, to, half=None, src=None):
            dst = slot(a, *block, half)
            return pltpu.make_async_remote_copy(
                src_ref=dst if src is None else src, dst_ref=dst,
                send_sem=send_sems.at[a, k], recv_sem=recv_sems.at[a, k],
                device_id=to, device_id_type=pl.DeviceIdType.MESH)

        return types.SimpleNamespace(
            me=me, sibling=sibling, xn=xn, yn=yn, diag=diag, c=c, copy=copy,
            mine_in=[pltpu.make_async_copy(x_refs[a], stage[a], local_sems.at[a, 0]) for a in range(n_arr)],
            mine_out=[pltpu.make_async_copy(stage[a], slot(a, *me), local_sems.at[a, 1]) for a in range(n_arr)],
            first=[cp for a in range(n_arr) for cp in (
                copy(a, 0, me, sibling, src=x_refs[a]), copy(a, 1, me, (*xn, c), src=x_refs[a]),
                copy(a, 2, me, (*yn, c), src=x_refs[a]))],
            second=lambda a: (copy(a, 3, (*xn, c), (*yn, c), half=0), copy(a, 5, (*xn, c), sibling),
                              copy(a, 4, (*yn, c), (*xn, c), half=1), copy(a, 6, (*yn, c), sibling)),
            third=lambda a: (copy(a, 7, (*diag, c), sibling, half=0), copy(a, 8, (*diag, c), sibling, half=1)))

    def start(x_refs, out_refs, sems):
        p = plan(x_refs, out_refs, sems)
        for cp in p.first + p.mine_in:
            cp.start()
        for cp_in, cp_out in zip(p.mine_in, p.mine_out):
            cp_in.wait()
            cp_out.start()

    def middle(x_refs, out_refs, sems):
        p = plan(x_refs, out_refs, sems)
        for a in range(n_arr):
            to_yn, x_to_sib, to_xn, y_to_sib = p.second(a)
            p.copy(a, 1, (*p.xn, p.c), p.me).wait_recv()
            to_yn.start()
            x_to_sib.start()
            p.copy(a, 2, (*p.yn, p.c), p.me).wait_recv()
            to_xn.start()
            y_to_sib.start()

    def finish(x_refs, out_refs, sems):
        p = plan(x_refs, out_refs, sems)
        for a in range(n_arr):
            half0_to_sib, half1_to_sib = p.third(a)
            p.copy(a, 3, (*p.diag, p.c), p.me, half=0).wait_recv()
            half0_to_sib.start()
            p.copy(a, 4, (*p.diag, p.c), p.me, half=1).wait_recv()
            half1_to_sib.start()
        other = 1 - p.c
        for a in range(n_arr):
            p.copy(a, 0, p.sibling, p.me).wait_recv()
            p.copy(a, 5, (*p.xn, other), p.me).wait_recv()
            p.copy(a, 6, (*p.yn, other), p.me).wait_recv()
            p.copy(a, 7, (*p.diag, other), p.me, half=0).wait_recv()
            p.copy(a, 8, (*p.diag, other), p.me, half=1).wait_recv()
        for cp in p.first:
            cp.wait_send()
        for a in range(n_arr):
            for cp in p.second(a) + p.third(a):
                cp.wait_send()
        for cp in p.mine_out:
            cp.wait()

    return types.SimpleNamespace(
        inputs=list(shards), start=start, middle=middle, finish=finish,
        out_shapes=[jax.ShapeDtypeStruct((N_DEV,) + s.shape, s.dtype) for s in shards],
        sem_shapes=[pltpu.SemaphoreType.DMA((n_arr, 9)), pltpu.SemaphoreType.DMA((n_arr, 9)),
                    pltpu.SemaphoreType.DMA((n_arr, 2))] + [pltpu.VMEM(s.shape, s.dtype) for s in shards])


def _pair_exchange(grads):
    n_arr = len(grads)

    def plan(g_refs, land_refs, sems):
        send_sems, recv_sems = sems
        x, y, c = _mesh_pos()
        return [pltpu.make_async_remote_copy(
            src_ref=g_refs[a].at[2 * k + 1 - c], dst_ref=land_refs[a].at[k],
            send_sem=send_sems.at[a, k], recv_sem=recv_sems.at[a, k],
            device_id=(x, y, 1 - c), device_id_type=pl.DeviceIdType.MESH)
            for a in range(n_arr) for k in range(N_CHIP)]

    def start(g_refs, land_refs, sems):
        for cp in plan(g_refs, land_refs, sems):
            cp.start()

    def finish(g_refs, land_refs, sems):
        for cp in plan(g_refs, land_refs, sems):
            cp.wait()

    return types.SimpleNamespace(
        inputs=list(grads), start=start, middle=_no_round, finish=finish,
        out_shapes=[jax.ShapeDtypeStruct((N_CHIP,) + g.shape[1:], g.dtype) for g in grads],
        sem_shapes=[pltpu.SemaphoreType.DMA((n_arr, N_CHIP)), pltpu.SemaphoreType.DMA((n_arr, N_CHIP))])


def _chip_exchange(parts):
    n_arr = len(parts)

    def plan(p_refs, land_refs, sems):
        send_sems, recv_sems, local_sems = sems
        x, y, c = _mesh_pos()
        my_chip = 2 * x + y
        chips = [(1 - x, y), (x, 1 - y), (1 - x, 1 - y)]
        local = [pltpu.make_async_copy(p_refs[a].at[my_chip], land_refs[a].at[my_chip], local_sems.at[a])
                 for a in range(n_arr)]

        def copy(a, k, src_slot, dst_slot, px, py):
            return pltpu.make_async_remote_copy(
                src_ref=p_refs[a].at[src_slot], dst_ref=land_refs[a].at[dst_slot],
                send_sem=send_sems.at[a, k], recv_sem=recv_sems.at[a, k],
                device_id=(px, py, c), device_id_type=pl.DeviceIdType.MESH)

        sends = [copy(a, k, 2 * px + py, my_chip, px, py) for a in range(n_arr) for k, (px, py) in enumerate(chips)]
        arrivals = [copy(a, k, my_chip, 2 * px + py, px, py) for a in range(n_arr)
                    for k, (px, py) in enumerate(chips)]
        return local, sends, arrivals

    def start(p_refs, land_refs, sems):
        local, sends, _ = plan(p_refs, land_refs, sems)
        for cp in local + sends:
            cp.start()

    def finish(p_refs, land_refs, sems):
        local, sends, arrivals = plan(p_refs, land_refs, sems)
        for cp in arrivals:
            cp.wait_recv()
        for cp in sends:
            cp.wait_send()
        for cp in local:
            cp.wait()

    return types.SimpleNamespace(
        inputs=list(parts), start=start, middle=_no_round, finish=finish,
        out_shapes=[jax.ShapeDtypeStruct(p.shape, p.dtype) for p in parts],
        sem_shapes=[pltpu.SemaphoreType.DMA((n_arr, 3)), pltpu.SemaphoreType.DMA((n_arr, 3)),
                    pltpu.SemaphoreType.DMA((n_arr,))])


_HBM = pl.BlockSpec(memory_space=pltpu.HBM)
_SEM = pl.BlockSpec(memory_space=pltpu.SEMAPHORE)
_DATAFLOW = pltpu.SideEffectType.DATAFLOW_SIDE_EFFECTING


def _chip_copies(p_refs, land_refs, send_sems, recv_sems):
    x, y, c = _mesh_pos()
    my_chip = 2 * x + y
    chips = [(1 - x, y), (x, 1 - y), (1 - x, 1 - y)]
    return [pltpu.make_async_remote_copy(
        src_ref=p_refs[a].at[2 * px + py], dst_ref=land_refs[a].at[my_chip],
        send_sem=send_sems[3 * a + k], recv_sem=recv_sems[3 * a + k],
        device_id=(px, py, c), device_id_type=pl.DeviceIdType.MESH)
        for a in range(len(p_refs)) for k, (px, py) in enumerate(chips)]


def _chip_exchange_begin(parts, name):
    n_arr = len(parts)
    n_buf, n_copy = 2 * n_arr, 3 * n_arr
    lands = [lax.empty(p.shape, p.dtype) for p in parts]

    def body(*refs):
        p_refs, land_refs = refs[:n_arr], refs[n_arr:n_buf]
        send_sems, recv_sems, token = refs[n_buf:n_buf + n_copy], refs[n_buf + n_copy:n_buf + 2 * n_copy], refs[-1]
        for cp in _chip_copies(p_refs, land_refs, send_sems, recv_sems):
            cp.start()
        token[...] = jnp.zeros_like(token)

    bufs = list(parts) + list(lands)
    outs = pl.pallas_call(
        body, name=name,
        out_shape=(*[pltpu.SemaphoreType.DMA(())] * (2 * n_copy), *[pltpu.HBM(b.shape, b.dtype) for b in bufs],
                   jax.ShapeDtypeStruct((8, 128), F32)),
        in_specs=[_HBM] * n_buf,
        out_specs=(*[_SEM] * (2 * n_copy), *[_HBM] * n_buf, pl.BlockSpec(memory_space=pltpu.VMEM)),
        input_output_aliases={i: 2 * n_copy + i for i in range(n_buf)},
        compiler_params=pltpu.CompilerParams(has_side_effects=_DATAFLOW),
    )(*[pltpu.with_memory_space_constraint(b, pltpu.HBM) for b in bufs])
    sems = list(outs[:2 * n_copy])
    thru = list(outs[2 * n_copy:2 * n_copy + n_buf])
    return types.SimpleNamespace(send_sems=sems[:n_copy], recv_sems=sems[n_copy:], parts=thru[:n_arr],
                                 lands=thru[n_arr:], token=outs[-1])


def _chip_exchange_end(flight, after, name):
    send_sems, recv_sems, parts, lands = flight.send_sems, flight.recv_sems, flight.parts, flight.lands
    n_arr = len(parts)
    n_buf, n_copy = 2 * n_arr, 3 * n_arr

    def body(*refs):
        p_refs, land_refs = refs[:n_arr], refs[n_arr:n_buf]
        sems = refs[n_buf:n_buf + 2 * n_copy]
        for cp in _chip_copies(p_refs, land_refs, sems[:n_copy], sems[n_copy:]):
            cp.wait_send()
            cp.wait_recv()

    bufs = list(parts) + list(lands)
    outs = pl.pallas_call(
        body, name=name, out_shape=tuple(pltpu.HBM(b.shape, b.dtype) for b in bufs),
        in_specs=[_HBM] * n_buf + [_SEM] * (2 * n_copy) + [_ANY], out_specs=tuple([_HBM] * n_buf),
        input_output_aliases={i: i for i in range(n_buf)},
        compiler_params=pltpu.CompilerParams(has_side_effects=_DATAFLOW),
    )(*bufs, *send_sems, *recv_sems, after)
    return list(outs[:n_arr]), list(outs[n_arr:])


def _row_tile(r, cap=640):
    best = None
    for cand in range(16, min(r, cap) + 1, 16):
        if r % cand == 0:
            best = cand
    return best if best is not None else r


def _pair_sum(gs, landeds, core, name):
    tiles = [_row_tile(g.shape[1]) for g in gs]
    counts = [g.shape[1] // tr for g, tr in zip(gs, tiles)]
    n_arr = len(gs)

    def body(core_ref, *refs):
        for a in range(n_arr):
            mine, theirs, out = refs[2 * a], refs[2 * a + 1], refs[2 * n_arr + a]
            out[0] = (mine[0].astype(F32) + theirs[0].astype(F32)).astype(out.dtype)

    in_specs, out_specs, operands = [], [], []
    for g, landed, tr, count in zip(gs, landeds, tiles, counts):
        c_dim = g.shape[2]
        last = count - 1
        in_specs += [pl.BlockSpec((1, tr, c_dim),
                                  lambda k, i, core_ref, last=last: (2 * k + core_ref[0], jnp.minimum(i, last), 0)),
                     pl.BlockSpec((1, tr, c_dim), lambda k, i, core_ref, last=last: (k, jnp.minimum(i, last), 0))]
        out_specs.append(pl.BlockSpec((1, tr, c_dim), lambda k, i, core_ref, last=last: (k, jnp.minimum(i, last), 0)))
        operands += [g, landed]
    return list(pl.pallas_call(
        body, name=name,
        grid_spec=pltpu.PrefetchScalarGridSpec(
            num_scalar_prefetch=1, grid=(N_CHIP, max(counts)), in_specs=in_specs, out_specs=out_specs),
        out_shape=[jax.ShapeDtypeStruct((N_CHIP,) + g.shape[1:], g.dtype) for g in gs],
        compiler_params=_cparams(2),
    )(core, *operands))


def _sum_slots(parts, name):
    n, r, c_dim = parts.shape
    tr = _row_tile(r)

    def body(p_ref, o_ref):
        acc = p_ref[0].astype(F32)
        for k in range(1, n):
            acc = acc + p_ref[k].astype(F32)
        o_ref[...] = acc

    return pl.pallas_call(
        body, grid=(r // tr,), name=name,
        in_specs=[pl.BlockSpec((n, tr, c_dim), lambda i: (0, i, 0))],
        out_specs=_row_spec(tr, c_dim),
        out_shape=jax.ShapeDtypeStruct((r, c_dim), F32),
        compiler_params=_cparams(),
    )(parts)


GAINS = ("g_ffn1", "g_mix", "g_cross", "g_mem", "g_ffn2", "g_final")
SMALL = GAINS + ("b_gate", "conv_w")
SMALL_R = 16
LOSS_ROW = 11
WEIGHT_ORDER = ("g_ffn1", "w_ffn1_gu", "w_ffn1_down", "g_mix", "w_in", "b_gate", "conv_w", "w_conv_out",
                "w_attn_out", "w_o", "g_cross", "g_mem", "w_cq", "w_ckv", "w_co", "g_ffn2", "w_ffn2_gu",
                "w_ffn2_down", "g_final")
GU_NAMES = ("w_ffn1_gu", "w_ffn2_gu")


def _pack_small(vals, conv_rows):
    rows = [vals[n].reshape(1, D) for n in GAINS] + [vals["b_gate"].reshape(2, D), conv_rows.reshape(CONV_K, D)]
    used = len(GAINS) + 2 + CONV_K
    return jnp.concatenate(rows + [jnp.zeros((SMALL_R - used, D), F32)], axis=0)


def _unpack_small(buf):
    out = {n: buf[k] for k, n in enumerate(GAINS)}
    out["b_gate"] = buf[6:8].reshape(2 * D)
    out["conv_w"] = buf[8:8 + CONV_K]
    return out


def _exchange_shards(wts):
    out = {n: jnp.pad(wts[n].T.astype(BF16), ((0, FF_PAD - FF_BLK), (0, 0))) for n in GU_NAMES}
    for n in ("w_ckv", "w_in", "w_ffn1_down", "w_ffn2_down"):
        out[n] = wts[n].astype(BF16)
    out["mix"] = jnp.concatenate([wts[n].astype(BF16) for n in MIX_MATS], axis=0)
    out["cross"] = jnp.concatenate([wts[n].astype(BF16) for n in CROSS_MATS], axis=0)
    return out


def _reduce_group(grads, landed, core, names):
    return _pair_sum(grads, landed, core, "grads_pair_sum_" + "_".join(names))


def _step(x, mem, target, sh, conv_pad, gains, b_gate, core):
    wg1, wd1, conv_all = _run_exchange(_gather_exchange([sh["w_ffn1_gu"], sh["w_ffn1_down"], conv_pad]), "gather_ffn1")
    conv_w = conv_all[:, :CONV_K, :].transpose(1, 0, 2).reshape(CONV_K, D)
    (n1, gate1, up1, act1, h1), (w_in,) = _ffn_fwd(
        x, gains["g_ffn1"], wg1, wd1, "ffn1_fwd", comm=_gather_exchange([sh["w_in"]]))
    (u, pcg, qkv, yc), (w_mix, wd2) = _inproj_fwd(h1, gains["g_mix"], w_in, conv_w, "inproj_fwd",
                                                  comm=_gather_exchange([sh["mix"], sh["w_ffn2_down"]]))
    (ysb, ctot), (w_cross, w_ckv, wg2) = _sb_fwd(
        qkv, "sb_fwd", comm=_gather_exchange([sh["cross"], sh["w_ckv"], sh["w_ffn2_gu"]]))
    (a_mix, b_mix, merged, h2), _ = _mix_out_fwd(yc, ysb, pcg, b_gate, h1, w_mix, "mix_out_fwd")
    hn, qx, o_x, h3, mn, kv = _cross_fwd(h2, gains["g_cross"], mem, gains["g_mem"], w_ckv, w_cross, "cross_fwd")
    (n4, gate2, up2, act2, dh4, loss, dg_final), _ = _ffn_fwd(h3, gains["g_ffn2"], wg2, wd2, "ffn2_fwd",
                                                              head=(gains["g_final"], target))

    gs = {"g_final": dg_final}
    (dgu2, dh4b, dh3, gs["g_ffn2"]), _ = _ffn_bwd(dh4, h3, gains["g_ffn2"], gate2, up2, wg2, wd2, "ffn2_bwd")
    grads_a = [_mm_tn_rows(dgu2, n4, FF_PAD, "dw_ffn2_gu"),
               _mm_tn_rows(act2, dh4b, FF_BLK, "dw_ffn2_down").reshape(N_DEV, DOWN_ROWS, D)]
    names_a = ["w_ffn2_gu", "w_ffn2_down"]
    (dh3b, dqx, dkv, dh2, gs["g_cross"], gs["g_mem"]), _ = _cross_bwd(
        dh3, h2, gains["g_cross"], qx, kv, mem, gains["g_mem"], w_ckv, w_cross, "cross_bwd")
    grads_b = [_mm_tn_cols(mn, dkv, "dw_ckv"), _mm_tn_squares([(hn, dqx), (o_x, dh3b)], "dw_cross")]
    names_b = ["w_ckv", "cross"]
    (dh2b, da_mix, db_mix, dgp, dconv, dysb, gs["b_gate"], gs["conv_w"]), landed_ab = _mix_out_bwd(
        dh2, a_mix, b_mix, pcg, b_gate, conv_w, w_mix, "mix_out_bwd", comm=_pair_exchange(grads_a + grads_b))
    sums_ab = _reduce_group(grads_a + grads_b, landed_ab, core, names_a + names_b)
    grads_c = [_mm_tn_squares([(yc, da_mix), (ysb, db_mix), (merged, dh2b)], "dw_mix")]
    flight_ab = _chip_exchange_begin(sums_ab, "grads_to_chips_early_begin")
    (dq, dkv_sb), _ = _sb_bwd(qkv, dysb, ctot, flight_ab.token, "sb_bwd")
    grads_d = [_mm_tn_cols_many(u, [dconv, dq[None], dkv_sb, dgp], "dw_in")]
    (dh1, gs["g_mix"]), landed_cd = _inproj_bwd(dconv, dq, dkv_sb, dgp, w_in, h1, gains["g_mix"], dh2, "inproj_bwd",
                                                comm=_pair_exchange(grads_c + grads_d))
    sums_cd = _reduce_group(grads_c + grads_d, landed_cd, core, ["mix", "w_in"])
    flight_d = _chip_exchange_begin(sums_cd, "grads_to_chips_w_in_begin")
    (dgu1, dh1b, dx, gs["g_ffn1"]), _ = _ffn_bwd(dh1, x, gains["g_ffn1"] + flight_d.token[0, 0], gate1, up1, wg1, wd1,
                                                 "ffn1_bwd")
    dw_gu1 = _mm_tn_rows(dgu1, n1, FF_PAD, "dw_ffn1_gu")
    dw_down1, landed_gu1 = _mm_tn_rows(act1, dh1b, FF_BLK, "dw_ffn1_down", comm=_pair_exchange([dw_gu1]))
    grads_e = [dw_gu1, dw_down1.reshape(N_DEV, DOWN_ROWS, D)]
    names_e = ["w_ffn1_gu", "w_ffn1_down"]
    small_mine = _pack_small({n: gs[n] for n in GAINS + ("b_gate",)}, gs["conv_w"][:CONV_K])
    small_mine = small_mine.at[LOSS_ROW, 0].set(loss[0, 0])
    landed_down1, small_all = _run_exchange(
        _both(_pair_exchange(grads_e[1:]), _gather_exchange([small_mine])), "grads_to_sibling_ffn1_down")
    landed_e = landed_gu1 + [landed_down1]
    flight_e = _chip_exchange_begin(_reduce_group(grads_e, landed_e, core, names_e), "grads_to_chips_ffn1_begin")
    flights = [(names_a + names_b, flight_ab), (["mix", "w_in"], flight_d), (names_e, flight_e)]
    return dx, flights, small_all


def kernel(x, mem, g_ffn1, w_ffn1_gu, w_ffn1_down, g_mix, w_in, b_gate, conv_w, w_conv_out, w_attn_out, w_o, g_cross, g_mem, w_cq, w_ckv, w_co, g_ffn2, w_ffn2_gu, w_ffn2_down, g_final, loss_target, m_g_ffn1, m_w_ffn1_gu, m_w_ffn1_down, m_g_mix, m_w_in, m_b_gate, m_conv_w, m_w_conv_out, m_w_attn_out, m_w_o, m_g_cross, m_g_mem, m_w_cq, m_w_ckv, m_w_co, m_g_ffn2, m_w_ffn2_gu, m_w_ffn2_down, m_g_final, v_g_ffn1, v_w_ffn1_gu, v_w_ffn1_down, v_g_mix, v_w_in, v_b_gate, v_conv_w, v_w_conv_out, v_w_attn_out, v_w_o, v_g_cross, v_g_mem, v_w_cq, v_w_ckv, v_w_co, v_g_ffn2, v_w_ffn2_gu, v_w_ffn2_down, v_g_final):
    args = locals()
    wts = {n: args[n] for n in WEIGHT_ORDER}
    mom1 = {n: args["m_" + n] for n in WEIGHT_ORDER}
    mom2 = {n: args["v_" + n] for n in WEIGHT_ORDER}
    cx, cy, cc = _mesh_pos()
    dev = 4 * cx + 2 * cy + cc
    conv_cols = D // N_DEV

    conv_pad = jnp.concatenate([conv_w, jnp.zeros((SMALL_R - CONV_K, conv_cols), F32)], axis=0)
    gains = {n: wts[n].reshape(1, D) for n in GAINS}
    dx, flights, small_all = _step(x[0], mem[0], loss_target[0], _exchange_shards(wts), conv_pad, gains,
                                 b_gate.reshape(1, 2 * D), cc.reshape(1).astype(jnp.int32))

    grads, delta, new_m, new_v = {}, {}, {}, {}

    def operands(n, transposed):
        trio = (wts[n], mom1[n], mom2[n])
        return tuple(a.T for a in trio) if transposed else trio

    def record(n, res, transposed):
        grads[n], delta[n], new_m[n], new_v[n] = [r.T for r in res] if transposed else res

    early = [("w_ffn2_gu", "w_ffn2_gu", 0, True), ("w_ffn2_down", "w_ffn2_down", 0, False),
             ("w_ckv", "w_ckv", 0, False), ("w_in", "w_in", 0, False)]
    early += [(n, "mix", k, False) for k, n in enumerate(MIX_MATS)]
    early += [(n, "cross", k, False) for k, n in enumerate(CROSS_MATS)]
    chip = (2 * cx + cy).reshape(1).astype(jnp.int32)
    (names_early, flight_early), (names_w_in, flight_w_in), (last_names, flight_last) = flights
    token = flight_last.token
    own, land = {}, {}
    for names, flight, tag in ((names_early, flight_early, "early"), (names_w_in, flight_w_in, "w_in")):
        own_parts, landed = _chip_exchange_end(flight, token, "grads_to_chips_%s_end" % tag)
        own.update(zip(names, own_parts))
        land.update(zip(names, landed))
    for n, buf, row_block, transposed in early:
        w, m1, m2 = operands(n, transposed)
        record(n, _adamw_own(w, land[buf], own[buf], chip, m1, m2, "adamw_" + n, row_block, token), transposed)

    after = jnp.concatenate([new_v[n][:1, :1] for n, _, _, _ in early], axis=0)
    own_parts, landed = _chip_exchange_end(flight_last, after, "grads_to_chips_ffn1_end")
    for n, own_n, land_n, transposed in zip(last_names, own_parts, landed, (True, False)):
        w, m1, m2 = operands(n, transposed)
        record(n, _adamw_own(w, land_n, own_n, chip, m1, m2, "adamw_" + n), transposed)

    small_sum = _sum_slots(small_all, "small_grads_sum")
    loss = small_sum[LOSS_ROW, 0]
    grad_small = _unpack_small(small_sum)
    grad_small["conv_w"] = lax.dynamic_slice_in_dim(grad_small["conv_w"], dev * conv_cols, conv_cols, axis=1)
    grads.update(grad_small)

    def small_buf(vals):
        return _pack_small(vals, jnp.concatenate([vals["conv_w"], jnp.zeros((CONV_K, D - conv_cols), F32)], axis=1))

    _, d_s, m_s, v_s = _adamw(small_buf(wts), small_buf(grads)[None], small_buf(mom1), small_buf(mom2), "adamw_small")
    for res, buf in ((delta, d_s), (new_m, m_s), (new_v, v_s)):
        un = _unpack_small(buf)
        for n in GAINS + ("b_gate",):
            res[n] = un[n]
        res["conv_w"] = un["conv_w"][:, :conv_cols]

    return (loss, dx[None], *[grads[n] for n in WEIGHT_ORDER], *[delta[n] for n in WEIGHT_ORDER],
            *[new_m[n] for n in WEIGHT_ORDER], *[new_v[n] for n in WEIGHT_ORDER])
```

```python
import types

import jax
import jax.numpy as jnp
from jax import lax
from jax.experimental import pallas as pl
from jax.experimental.pallas import tpu as pltpu

F32 = jnp.float32
BF16 = jnp.bfloat16

D = 1024
DFF = 2816
SB_H = 8
SB_DH = 128
X_H = 4
X_DH = 256
CONV_K = 3
RMS_EPS = 1e-6
N_DEV = 8
N_CHIP = 4
SQ_ROWS = D // N_DEV

ADAM_LR = 0.001
ADAM_B1 = 0.9
ADAM_B2 = 0.999
ADAM_EPS = 1e-08
ADAM_WD = 0.01
ADAM_STEP = 10

TM = 256
TQ = 512
TK = 256
SB_HPS = 2
VMEM_LIMIT = 56 << 20

FF_BLK = DFF // 4
FF_PAD = 768
FF_SUB = 256
DOWN_ROWS = DFF // N_DEV

MIX_MATS = ("w_conv_out", "w_attn_out", "w_o")
CROSS_MATS = ("w_cq", "w_co")

_ANY = pl.BlockSpec(memory_space=pl.ANY)


def _cparams(n_axes=1):
    return pltpu.CompilerParams(
        dimension_semantics=("arbitrary",) * n_axes, vmem_limit_bytes=VMEM_LIMIT)


def _row_spec(tm, n):
    return pl.BlockSpec((tm, n), lambda i: (i, 0))


def _blk_row_spec(nb, tm, n):
    return pl.BlockSpec((nb, tm, n), lambda i: (0, i, 0))


def _const_spec(shape):
    zeros = (0,) * len(shape)
    return pl.BlockSpec(shape, lambda i: zeros)


def _dot(a, b):
    return jnp.dot(a, b, preferred_element_type=F32)


def _dot_nt(a, b):
    return lax.dot_general(a, b, (((1,), (1,)), ((), ())), preferred_element_type=F32)


def _dot_tn(a, b):
    return lax.dot_general(a, b, (((0,), (0,)), ((), ())), preferred_element_type=F32)


def _sigmoid(x):
    return 1.0 / (1.0 + jnp.exp(-x))


def _call(body, operands, *, grid, in_specs, out_specs, out_shape, scratch_shapes, name, comm=None):
    n_in, n_out, n_sc = len(in_specs), len(out_specs), len(scratch_shapes)
    if comm is None:
        outs = pl.pallas_call(
            body, grid=grid, name=name, in_specs=in_specs, out_specs=out_specs, out_shape=out_shape,
            scratch_shapes=scratch_shapes, compiler_params=_cparams(len(grid)))(*operands)
        return list(outs), []
    c_in, c_out, c_sem = len(comm.inputs), len(comm.out_shapes), len(comm.sem_shapes)

    def hosted(*refs):
        bounds = [0, n_in, c_in, n_out, c_out, n_sc, c_sem]
        parts, pos = [], 0
        for k in bounds[1:]:
            parts.append(refs[pos:pos + k])
            pos += k
        ins, cins, outs, couts, scr, sems = parts
        step, n_steps = pl.program_id(0), grid[0]
        for ax in range(1, len(grid)):
            step, n_steps = step * grid[ax] + pl.program_id(ax), n_steps * grid[ax]

        @pl.when(step == 0)
        def _():
            comm.start(cins, couts, sems)

        @pl.when(step == (2 * n_steps) // 3)
        def _():
            comm.middle(cins, couts, sems)

        body(*ins, *outs, *scr)

        @pl.when(step == n_steps - 1)
        def _():
            comm.finish(cins, couts, sems)

    res = pl.pallas_call(
        hosted, grid=grid, name=name, in_specs=list(in_specs) + [_ANY] * c_in,
        out_specs=list(out_specs) + [_ANY] * c_out, out_shape=list(out_shape) + list(comm.out_shapes),
        scratch_shapes=list(scratch_shapes) + list(comm.sem_shapes),
        compiler_params=_cparams(len(grid)))(*operands, *comm.inputs)
    return list(res[:n_out]), list(res[n_out:])


def _load_resident(step, pairs, sems):
    @pl.when(step == 0)
    def _():
        copies = [pltpu.make_async_copy(src, dst, sems.at[k]) for k, (src, dst) in enumerate(pairs)]
        for cp in copies:
            cp.start()
        for cp in copies:
            cp.wait()


def _square_pairs(buf_hbm, index, dst):
    off = index * SQ_ROWS
    return [(buf_hbm.at[d, off:off + SQ_ROWS, :], dst.at[d * SQ_ROWS:(d + 1) * SQ_ROWS, :]) for d in range(N_DEV)]


def _down_pairs(wd_hbm, dst):
    return [(wd_hbm.at[d], dst.at[d // 2, (d % 2) * DOWN_ROWS:(d % 2 + 1) * DOWN_ROWS, :]) for d in range(N_DEV)]


def _zero_down_pad(step, dst):
    @pl.when(step == 0)
    def _():
        dst[:, FF_BLK:, :] = jnp.zeros((4, FF_PAD - FF_BLK, D), BF16)


def _rms_fwd_tile(xt, g):
    r = lax.rsqrt(jnp.mean(xt * xt, axis=-1, keepdims=True) + RMS_EPS)
    return (xt * r) * g


def _rms_bwd_tile(xt, g, dn):
    r = lax.rsqrt(jnp.mean(xt * xt, axis=-1, keepdims=True) + RMS_EPS)
    xhat = xt * r
    dxhat = dn * g
    dx = r * (dxhat - xhat * jnp.mean(dxhat * xhat, axis=-1, keepdims=True))
    dg = jnp.sum(dn * xhat, axis=0, keepdims=True)
    return dx, dg


def _accumulate(ref, step, value):
    @pl.when(step == 0)
    def _():
        ref[...] = value

    @pl.when(step != 0)
    def _():
        ref[...] = ref[...] + value


def _ffn_fwd(x, g, wgu, wd, name, comm=None, head=None):
    t = x.shape[0]

    def body(x_ref, g_ref, wgu_hbm, wd_hbm, *refs):
        if head is None:
            n_ref, gate_ref, up_ref, act_ref, h_ref, wgu_v, wd_v, sems = refs
        else:
            gf_ref, t_ref, n_ref, gate_ref, up_ref, act_ref, dh_ref, loss_ref, dgf_ref, wgu_v, wd_v, sems = refs
        step = pl.program_id(0)
        _zero_down_pad(step, wd_v)
        _load_resident(step, [(wgu_hbm, wgu_v)] + _down_pairs(wd_hbm, wd_v), sems)
        xt = x_ref[...]
        n = _rms_fwd_tile(xt, g_ref[...]).astype(BF16)
        n_ref[...] = n
        acc = jnp.zeros((TM, D), F32)
        for j in range(4):
            for s in range(FF_PAD // FF_SUB):
                lo, hi = s * FF_SUB, (s + 1) * FF_SUB
                gt = _dot_nt(n, wgu_v[j, lo:hi, :])
                ut = _dot_nt(n, wgu_v[4 + j, lo:hi, :])
                gate_ref[j, :, lo:hi] = gt.astype(BF16)
                up_ref[j, :, lo:hi] = ut.astype(BF16)
                act_ref[j, :, lo:hi] = ((gt * _sigmoid(gt)) * ut).astype(BF16)
            acc = acc + _dot(act_ref[j], wd_v[j])
        ht = xt + 0.5 * acc
        if head is None:
            h_ref[...] = ht
        else:
            gain = gf_ref[...]
            diff = _rms_fwd_tile(ht, gain) - t_ref[...]
            part = 0.5 * jnp.sum(jnp.sum(diff * diff, axis=-1, keepdims=True) / D, axis=0, keepdims=True)
            dx, dg = _rms_bwd_tile(ht, gain, diff / D)
            dh_ref[...] = dx
            _accumulate(loss_ref, step, jnp.broadcast_to(part, (8, 128)))
            _accumulate(dgf_ref, step, dg)

    ff = jax.ShapeDtypeStruct((4, t, FF_PAD), BF16)
    operands, in_specs = (x, g, wgu, wd), [_row_spec(TM, D), _const_spec((1, D)), _ANY, _ANY]
    out_specs = [_row_spec(TM, D)] + [_blk_row_spec(4, TM, FF_PAD)] * 3 + [_row_spec(TM, D)]
    out_shape = [jax.ShapeDtypeStruct((t, D), BF16), ff, ff, ff, jax.ShapeDtypeStruct((t, D), F32)]
    if head is not None:
        operands += tuple(head)
        in_specs += [_const_spec((1, D)), _row_spec(TM, D)]
        out_specs += [_const_spec((8, 128)), _const_spec((1, D))]
        out_shape += [jax.ShapeDtypeStruct((8, 128), F32), jax.ShapeDtypeStruct((1, D), F32)]
    return _call(
        body, operands, grid=(t // TM,), name=name, comm=comm, in_specs=in_specs, out_specs=out_specs,
        out_shape=out_shape,
        scratch_shapes=[pltpu.VMEM((N_DEV, FF_PAD, D), BF16), pltpu.VMEM((4, FF_PAD, D), BF16),
                        pltpu.SemaphoreType.DMA((1 + N_DEV,))])


def _ffn_bwd(dh, xin, g, gate, up, wgu, wd, name, comm=None):
    t = dh.shape[0]

    def body(dh_ref, x_ref, g_ref, gate_ref, up_ref, wgu_hbm, wd_hbm,
             dgu_ref, dhb_ref, dx_ref, dg_ref, wgu_v, wd_v, sems):
        step = pl.program_id(0)
        _zero_down_pad(step, wd_v)
        _load_resident(step, [(wgu_hbm, wgu_v)] + _down_pairs(wd_hbm, wd_v), sems)
        dht = dh_ref[...]
        dhb = (0.5 * dht).astype(BF16)
        dhb_ref[...] = dhb
        dn = jnp.zeros((TM, D), F32)
        for j in range(4):
            for s in range(FF_PAD // FF_SUB):
                lo, hi = s * FF_SUB, (s + 1) * FF_SUB
                da = _dot_nt(dhb, wd_v[j, lo:hi, :])
                gt = gate_ref[j, :, lo:hi].astype(F32)
                ut = up_ref[j, :, lo:hi].astype(F32)
                sg = _sigmoid(gt)
                dgt = (da * ut * (sg * (1.0 + gt * (1.0 - sg)))).astype(BF16)
                dut = (da * (gt * sg)).astype(BF16)
                dgu_ref[j, :, lo:hi] = dgt
                dgu_ref[4 + j, :, lo:hi] = dut
            dn = dn + _dot(dgu_ref[j], wgu_v[j]) + _dot(dgu_ref[4 + j], wgu_v[4 + j])
        dx, dg = _rms_bwd_tile(x_ref[...], g_ref[...], dn)
        dx_ref[...] = dht + dx
        _accumulate(dg_ref, step, dg)

    return _call(
        body, (dh, xin, g, gate, up, wgu, wd), grid=(t // TM,), name=name, comm=comm,
        in_specs=[_row_spec(TM, D), _row_spec(TM, D), _const_spec((1, D)), _blk_row_spec(4, TM, FF_PAD),
                  _blk_row_spec(4, TM, FF_PAD), _ANY, _ANY],
        out_specs=[_blk_row_spec(N_DEV, TM, FF_PAD), _row_spec(TM, D), _row_spec(TM, D), _const_spec((1, D))],
        out_shape=[jax.ShapeDtypeStruct((N_DEV, t, FF_PAD), BF16), jax.ShapeDtypeStruct((t, D), BF16),
                   jax.ShapeDtypeStruct((t, D), F32), jax.ShapeDtypeStruct((1, D), F32)],
        scratch_shapes=[pltpu.VMEM((N_DEV, FF_PAD, D), BF16), pltpu.VMEM((4, FF_PAD, D), BF16),
                        pltpu.SemaphoreType.DMA((1 + N_DEV,))])


WIDE_TILES = (1024, 512, 256, 128)


def _pick_tile(n, options=(512, 256, 128)):
    for o in options:
        if n % o == 0:
            return o
    return n


def _mm_tn_squares(pairs, name):
    k, d = pairs[0][0].shape
    n = pairs[0][1].shape[1]
    tn = _pick_tile(n)
    count = len(pairs)

    def body(*refs):
        m = pl.program_id(0)
        for idx in range(count):
            @pl.when(m == idx)
            def _(idx=idx):
                refs[-1][...] = _dot_tn(refs[idx][...], refs[count + idx][...]).astype(BF16).reshape(
                    N_DEV, d // N_DEV, tn)

    a_specs = [pl.BlockSpec((k, d), lambda m, j: (0, 0)) for _ in pairs]
    b_specs = [pl.BlockSpec((k, tn), lambda m, j, idx=idx: (0, jnp.where(m == idx, j, 0))) for idx in range(count)]
    return pl.pallas_call(
        body, grid=(count, n // tn), name=name, in_specs=a_specs + b_specs,
        out_specs=pl.BlockSpec((N_DEV, d // N_DEV, tn), lambda m, j: (0, m, j)),
        out_shape=jax.ShapeDtypeStruct((N_DEV, count * (d // N_DEV), n), BF16),
        compiler_params=_cparams(2),
    )(*[a for a, _ in pairs], *[b for _, b in pairs])


def _mm_tn_cols_many(a, parts, name):
    k, m = a.shape
    n = parts[0].shape[2]
    tn = _pick_tile(n)
    firsts, total = [], 0
    for p in parts:
        firsts.append(total)
        total += p.shape[0]

    def body(a_ref, *refs):
        j = pl.program_id(0)
        for p, first, ref in zip(parts, firsts, refs):
            @pl.when(jnp.logical_and(j >= first, j < first + p.shape[0]))
            def _(ref=ref):
                refs[-1][0] = _dot_tn(a_ref[...].astype(BF16), ref[0].astype(BF16)).astype(BF16)

    specs = [pl.BlockSpec((1, k, tn), lambda j, i, first=first, last=p.shape[0] - 1:
                          (jnp.clip(j - first, 0, last), 0, jnp.where(jnp.logical_and(j >= first, j <= first + last), i, 0)))
             for p, first in zip(parts, firsts)]
    return pl.pallas_call(
        body, grid=(total, n // tn), name=name,
        in_specs=[pl.BlockSpec((k, m), lambda j, i: (0, 0))] + specs,
        out_specs=pl.BlockSpec((1, m, tn), lambda j, i: (j, 0, i)),
        out_shape=jax.ShapeDtypeStruct((total, m, n), BF16),
        compiler_params=_cparams(2),
    )(a, *parts)


def _mm_tn_cols(a, b, name):
    k, m = a.shape
    nb, _, n = b.shape
    tm = _pick_tile(m, WIDE_TILES)

    def body(a_ref, b_ref, o_ref):
        o_ref[0] = _dot_tn(a_ref[...].astype(BF16), b_ref[0].astype(BF16)).astype(BF16)

    return pl.pallas_call(
        body, grid=(nb, m // tm), name=name,
        in_specs=[pl.BlockSpec((k, tm), lambda j, i: (0, i)), pl.BlockSpec((1, k, n), lambda j, i: (j, 0, 0))],
        out_specs=pl.BlockSpec((1, tm, n), lambda j, i: (j, i, 0)),
        out_shape=jax.ShapeDtypeStruct((nb, m, n), BF16),
        compiler_params=_cparams(2),
    )(a, b)


def _mm_tn_rows(a, b, keep, name, comm=None):
    nb, k, m = a.shape
    _, n = b.shape
    tn = _pick_tile(n, WIDE_TILES)

    def body(a_ref, b_ref, o_ref):
        o_ref[0] = _dot_tn(a_ref[0], b_ref[...])[:keep].astype(BF16)

    (out,), couts = _call(
        body, (a, b), grid=(nb, n // tn), name=name, comm=comm,
        in_specs=[pl.BlockSpec((1, k, m), lambda j, i: (j, 0, 0)), pl.BlockSpec((k, tn), lambda j, i: (0, i))],
        out_specs=[pl.BlockSpec((1, keep, tn), lambda j, i: (j, 0, i))],
        out_shape=[jax.ShapeDtypeStruct((nb, keep, n), BF16)], scratch_shapes=[])
    return out if comm is None else (out, couts)


PCG_W = 5 * D
QKV_W = 3 * D
PROJ_SUB = 512


def _inproj_fwd(h, g, w_in, conv_w, name, comm=None):
    t = h.shape[0]

    def body(h_ref, g_ref, w_hbm, cw_ref, u_ref, pcg_ref, qkv_ref, yc_ref, w_v, tail_v, sems):
        step = pl.program_id(0)
        _load_resident(step, [(w_hbm, w_v)], sems)

        @pl.when(step == 0)
        def _():
            tail_v[...] = jnp.zeros_like(tail_v)

        u = _rms_fwd_tile(h_ref[...], g_ref[...]).astype(BF16)
        u_ref[...] = u
        for blk in range(N_DEV):
            for s in range(D // PROJ_SUB):
                lo, hi = s * PROJ_SUB, (s + 1) * PROJ_SUB
                p = _dot(u, w_v[blk, :, lo:hi])
                if blk < 3:
                    pcg_ref[:, blk * D + lo:blk * D + hi] = p
                elif blk < 6:
                    qkv_ref[:, (blk - 3) * D + lo:(blk - 3) * D + hi] = p.astype(BF16)
                else:
                    pcg_ref[:, (blk - 3) * D + lo:(blk - 3) * D + hi] = p
        xc = pcg_ref[:, D:2 * D] * pcg_ref[:, 2 * D:3 * D]
        ext = jnp.concatenate([tail_v[...], xc], axis=0)
        conv = (cw_ref[0:1, :] * pltpu.roll(ext, 2, 0)[8:] + cw_ref[1:2, :] * pltpu.roll(ext, 1, 0)[8:]
                + cw_ref[2:3, :] * xc)
        yc_ref[...] = (pcg_ref[:, 0:D] * conv).astype(BF16)
        tail_v[...] = xc[TM - 8:]

    return _call(
        body, (h, g, w_in, conv_w), grid=(t // TM,), name=name, comm=comm,
        in_specs=[_row_spec(TM, D), _const_spec((1, D)), _ANY, _const_spec((CONV_K, D))],
        out_specs=[_row_spec(TM, D), _row_spec(TM, PCG_W), _row_spec(TM, QKV_W), _row_spec(TM, D)],
        out_shape=[jax.ShapeDtypeStruct((t, D), BF16), jax.ShapeDtypeStruct((t, PCG_W), F32),
                   jax.ShapeDtypeStruct((t, QKV_W), BF16), jax.ShapeDtypeStruct((t, D), BF16)],
        scratch_shapes=[pltpu.VMEM((N_DEV, D, D), BF16), pltpu.VMEM((8, D), F32), pltpu.SemaphoreType.DMA((1,))])


def _tri2(cond):
    rr = lax.broadcasted_iota(jnp.int32, (2 * TK, TK), 0) & (TK - 1)
    cc = lax.broadcasted_iota(jnp.int32, (2 * TK, TK), 1)
    return cond(rr, cc).astype(BF16)


def _causal(shift, row0=0):
    rr = lax.broadcasted_iota(jnp.int32, (TQ - row0, TK), 0) + row0
    cc = lax.broadcasted_iota(jnp.int32, (TQ - row0, TK), 1)
    return cc + shift < rr


def _cumdot(v, tri2):
    hi = v.astype(BF16)
    lo = (v - hi.astype(F32)).astype(BF16)
    return _dot(jnp.concatenate([hi, lo], axis=1), tri2)


LOG2_E = 1.4426950408889634


def _log_1m_beta(z):
    return -(jnp.maximum(z, 0.0) + jnp.log2(1.0 + jnp.exp2(-jnp.abs(z))))


def _sb_specs(t):
    g = SB_H // SB_HPS
    w = SB_HPS * SB_DH
    q_spec = pl.BlockSpec((TQ, w), lambda h, i: (i, h))
    k_spec = pl.BlockSpec((t, w), lambda h, i: (0, g + h))
    v_spec = pl.BlockSpec((t, w), lambda h, i: (0, 2 * g + h))
    ct_spec = pl.BlockSpec((SB_HPS, TQ, 1), lambda h, i: (h, i, 0))
    return g, w, q_spec, k_spec, v_spec, ct_spec


def _sb_fwd(qkv, name, comm=None):
    t = qkv.shape[0]
    scale = SB_DH ** -0.5
    g, w, q_spec, k_spec, v_spec, ct_spec = _sb_specs(t)

    def body(q_ref, k_ref, v_ref, y_ref, ct_ref):
        i = pl.program_id(1)
        later = _tri2(lambda j, s: j > s)
        n_diag = TQ // TK

        def block(j, carry, shift):
            off = pl.multiple_of(j * TK, TK)
            zs, ms = [], []
            for hd in range(SB_HPS):
                cols = slice(hd * SB_DH, (hd + 1) * SB_DH)
                z = _dot_nt(q_ref[:, cols], k_ref[pl.ds(off, TK), cols]) * (scale * LOG2_E)
                m = _log_1m_beta(z)
                if shift is not None:
                    m = jnp.where(_causal(shift), m, 0.0)
                zs.append(z)
                ms.append(m)
            after = _cumdot(jnp.concatenate(ms, axis=0), later)
            out = []
            for hd in range(SB_HPS):
                acc, c_sum = carry[hd]
                cols = slice(hd * SB_DH, (hd + 1) * SB_DH)
                a = jnp.exp2((ms[hd] + zs[hd]) + (c_sum + after[hd * TQ:(hd + 1) * TQ]))
                if shift is not None:
                    a = jnp.where(_causal(shift), a, 0.0)
                out.append((acc + _dot(a.astype(BF16), v_ref[pl.ds(off, TK), cols]),
                            c_sum + jnp.sum(ms[hd], axis=1, keepdims=True)))
            return tuple(out)

        carry = tuple((jnp.zeros((TQ, SB_DH), F32), jnp.zeros((TQ, 1), F32)) for _ in range(SB_HPS))
        for d in reversed(range(n_diag)):
            carry = block(i * n_diag + d, carry, d * TK)
        carry = lax.fori_loop(0, i * n_diag, lambda jj, c: block(i * n_diag - 1 - jj, c, None), carry)
        for hd in range(SB_HPS):
            y_ref[:, hd * SB_DH:(hd + 1) * SB_DH] = carry[hd][0].astype(BF16)
            ct_ref[hd] = carry[hd][1]

    return _call(
        body, (qkv, qkv, qkv), grid=(g, t // TQ), name=name, comm=comm,
        in_specs=[q_spec, k_spec, v_spec],
        out_specs=[q_spec, ct_spec],
        out_shape=[jax.ShapeDtypeStruct((t, D), BF16), jax.ShapeDtypeStruct((SB_H, t, 1), F32)],
        scratch_shapes=[])


def _sb_bwd(qkv, dy, ctot, after, name, comm=None):
    t = qkv.shape[0]
    scale = SB_DH ** -0.5
    g, w, q_spec, k_spec, v_spec, ct_spec = _sb_specs(t)
    acc_spec = pl.BlockSpec((2, t, w), lambda h, i: (0, 0, h))

    def body(q_ref, k_ref, v_ref, dy_ref, ct_ref, after_ref, dq_ref, dkv_ref):
        i = pl.program_id(1)

        @pl.when(i == 0)
        def _():
            dkv_ref[...] = jnp.zeros_like(dkv_ref)

        upto = _tri2(lambda j, s: j <= s)
        n_diag = TQ // TK

        def block(j, carry, shift):
            off = pl.multiple_of(j * TK, TK)
            r0 = 0 if shift is None else shift
            nr = TQ - r0
            causal = None if shift is None else _causal(shift, r0)

            def grow(old, delta):
                return old + delta if r0 == 0 else jnp.concatenate([old[:r0], old[r0:] + delta], axis=0)

            zs, ms = [], []
            for hd in range(SB_HPS):
                cols = slice(hd * SB_DH, (hd + 1) * SB_DH)
                z = _dot_nt(q_ref[r0:, cols], k_ref[pl.ds(off, TK), cols]) * (scale * LOG2_E)
                m = _log_1m_beta(z)
                if causal is not None:
                    m = jnp.where(causal, m, 0.0)
                zs.append(z)
                ms.append(m)
            m_upto = _cumdot(jnp.concatenate(ms, axis=0), upto)
            ls, a_s, es = [], [], []
            for hd in range(SB_HPS):
                cols = slice(hd * SB_DH, (hd + 1) * SB_DH)
                l = ms[hd] + zs[hd]
                a = jnp.exp2(l + ((ct_ref[hd, r0:] - carry[hd][1][r0:]) - m_upto[hd * nr:(hd + 1) * nr]))
                if causal is not None:
                    a = jnp.where(causal, a, 0.0)
                ls.append(l)
                a_s.append(a)
                es.append(_dot_nt(dy_ref[r0:, cols], v_ref[pl.ds(off, TK), cols]) * a)
            e_upto = _dot(jnp.concatenate(es, axis=0).astype(BF16), upto[:TK])
            out = []
            for hd in range(SB_HPS):
                dq, p_sum, e_sum = carry[hd]
                cols = slice(hd * SB_DH, (hd + 1) * SB_DH)
                e = es[hd]
                dz = e - jnp.exp2(ls[hd]) * (e_sum[r0:] + e_upto[hd * nr:(hd + 1) * nr])
                if causal is not None:
                    dz = jnp.where(causal, dz, 0.0)
                dzs = (dz * scale).astype(BF16)
                dkv_ref[0, pl.ds(off, TK), cols] += _dot_tn(dzs, q_ref[r0:, cols])
                dkv_ref[1, pl.ds(off, TK), cols] += _dot_tn(a_s[hd].astype(BF16), dy_ref[r0:, cols])
                out.append((grow(dq, _dot(dzs, k_ref[pl.ds(off, TK), cols])),
                            grow(p_sum, jnp.sum(ms[hd], axis=1, keepdims=True)),
                            grow(e_sum, jnp.sum(e, axis=1, keepdims=True))))
            return tuple(out)

        zero = jnp.zeros((TQ, 1), F32)
        init = tuple((jnp.zeros((TQ, SB_DH), F32), zero, zero) for _ in range(SB_HPS))
        carry = lax.fori_loop(0, i * n_diag, lambda j, c: block(j, c, None), init)
        for d in range(n_diag):
            carry = block(i * n_diag + d, carry, d * TK)
        for hd in range(SB_HPS):
            dq_ref[:, hd * SB_DH:(hd + 1) * SB_DH] = carry[hd][0].astype(BF16)

    return _call(
        body, (qkv, qkv, qkv, dy, ctot, after), grid=(g, t // TQ), name=name, comm=comm,
        in_specs=[q_spec, k_spec, v_spec, q_spec, ct_spec, pl.BlockSpec(after.shape, lambda h, i: (0, 0))],
        out_specs=[q_spec, acc_spec],
        out_shape=[jax.ShapeDtypeStruct((t, D), BF16), jax.ShapeDtypeStruct((2, t, D), F32)],
        scratch_shapes=[])


def _gate_specs():
    return [pl.BlockSpec((TM, D), lambda i: (i, 3)), pl.BlockSpec((TM, D), lambda i: (i, 4))]


def _mix_pairs(mix_hbm, dsts):
    pairs = []
    for index, dst in enumerate(dsts):
        pairs += _square_pairs(mix_hbm, index, dst)
    return pairs


def _mix_out_fwd(yc, ysb, pcg, b_gate, h, w_mix, name, comm=None):
    t = h.shape[0]

    def body(yc_ref, ysb_ref, gc_ref, gs_ref, b_ref, h_ref, mix_hbm,
             a_ref, b_out_ref, mg_ref, h2_ref, wc_v, wa_v, wo_v, sems):
        _load_resident(pl.program_id(0), _mix_pairs(mix_hbm, (wc_v, wa_v, wo_v)), sems)
        a = _dot(yc_ref[...], wc_v[...])
        b = _dot(ysb_ref[...], wa_v[...])
        merged = (_sigmoid(gc_ref[...] + b_ref[:, :D]) * a + _sigmoid(gs_ref[...] + b_ref[:, D:]) * b).astype(BF16)
        a_ref[...] = a
        b_out_ref[...] = b
        mg_ref[...] = merged
        h2_ref[...] = h_ref[...] + _dot(merged, wo_v[...])

    return _call(
        body, (yc, ysb, pcg, pcg, b_gate, h, w_mix), grid=(t // TM,), name=name, comm=comm,
        in_specs=[_row_spec(TM, D), _row_spec(TM, D)] + _gate_specs()
                 + [_const_spec((1, 2 * D)), _row_spec(TM, D), _ANY],
        out_specs=[_row_spec(TM, D)] * 4,
        out_shape=[jax.ShapeDtypeStruct((t, D), F32), jax.ShapeDtypeStruct((t, D), F32),
                   jax.ShapeDtypeStruct((t, D), BF16), jax.ShapeDtypeStruct((t, D), F32)],
        scratch_shapes=[pltpu.VMEM((D, D), BF16)] * 3 + [pltpu.SemaphoreType.DMA((3 * N_DEV,))])


def _mix_out_bwd(dh2, a, b, pcg, b_gate, conv_w, w_mix, name, comm=None):
    t = dh2.shape[0]
    n_tile = t // TM
    per8 = TM // 8

    def rows(n):
        return pl.BlockSpec((TM, n), lambda i: (n_tile - 1 - i, 0))

    def cols(block):
        return pl.BlockSpec((TM, D), lambda i: (n_tile - 1 - i, block))

    def before(block):
        return pl.BlockSpec((8, D), lambda i: (jnp.maximum((n_tile - 1 - i) * per8 - 1, 0), block))

    def body(dh_ref, a_ref, b_ref, gc_ref, gs_ref, cb_ref, cc_ref, cx_ref, ccp_ref, cxp_ref, bias_ref, cw_ref, mix_hbm,
             dhb_ref, da_ref, db_ref, dgp_ref, dc_ref, dysb_ref, dbias_ref, dcw_ref, wc_v, wa_v, wo_v, head_v, sems):
        step = pl.program_id(0)
        _load_resident(step, _mix_pairs(mix_hbm, (wc_v, wa_v, wo_v)), sems)

        @pl.when(step == 0)
        def _():
            head_v[...] = jnp.zeros_like(head_v)
            dcw_ref[...] = jnp.zeros_like(dcw_ref)

        dhb = dh_ref[...].astype(BF16)
        dhb_ref[...] = dhb
        dm = _dot_nt(dhb, wo_v[...])
        gc = _sigmoid(gc_ref[...] + bias_ref[:, :D])
        gs = _sigmoid(gs_ref[...] + bias_ref[:, D:])
        da = (dm * gc).astype(BF16)
        db = (dm * gs).astype(BF16)
        da_ref[...] = da
        db_ref[...] = db
        dgc = dm * a_ref[...] * (gc * (1.0 - gc))
        dgs = dm * b_ref[...] * (gs * (1.0 - gs))
        dgp_ref[0] = dgc.astype(BF16)
        dgp_ref[1] = dgs.astype(BF16)
        _accumulate(dbias_ref.at[:, :D], step, jnp.sum(dgc, axis=0, keepdims=True))
        _accumulate(dbias_ref.at[:, D:], step, jnp.sum(dgs, axis=0, keepdims=True))
        dysb_ref[...] = _dot_nt(db, wa_v[...]).astype(BF16)
        dyc = _dot_nt(da, wc_v[...])
        cc, cx = cc_ref[...], cx_ref[...]
        xc = cc * cx
        xc_before = jnp.where(step == n_tile - 1, 0.0, ccp_ref[...] * cxp_ref[...])
        ext = jnp.concatenate([xc_before, xc], axis=0)
        x1 = pltpu.roll(ext, 1, 0)[8:]
        x2 = pltpu.roll(ext, 2, 0)[8:]
        w0, w1, w2 = cw_ref[0:1, :], cw_ref[1:2, :], cw_ref[2:3, :]
        dc_ref[0] = (dyc * (w0 * x2 + w1 * x1 + w2 * xc)).astype(BF16)
        dconv = dyc * cb_ref[...]
        dcw_ref[0:1, :] += jnp.sum(dconv * x2, axis=0, keepdims=True)
        dcw_ref[1:2, :] += jnp.sum(dconv * x1, axis=0, keepdims=True)
        dcw_ref[2:3, :] += jnp.sum(dconv * xc, axis=0, keepdims=True)
        after = jnp.concatenate([dconv, head_v[...]], axis=0)
        dxc = w2 * dconv + w1 * pltpu.roll(after, TM + 7, 0)[:TM] + w0 * pltpu.roll(after, TM + 6, 0)[:TM]
        dc_ref[1] = (dxc * cx).astype(BF16)
        dc_ref[2] = (dxc * cc).astype(BF16)
        head_v[...] = dconv[:8]

    return _call(
        body, (dh2, a, b, pcg, pcg, pcg, pcg, pcg, pcg, pcg, b_gate, conv_w, w_mix), grid=(n_tile,), name=name,
        comm=comm,
        in_specs=[rows(D)] * 3 + [cols(3), cols(4), cols(0), cols(1), cols(2), before(1), before(2),
                                  _const_spec((1, 2 * D)), _const_spec((CONV_K, D)), _ANY],
        out_specs=[rows(D)] * 3 + [pl.BlockSpec((2, TM, D), lambda i: (0, n_tile - 1 - i, 0)),
                                   pl.BlockSpec((3, TM, D), lambda i: (0, n_tile - 1 - i, 0)), rows(D),
                                   _const_spec((1, 2 * D)), _const_spec((8, D))],
        out_shape=[jax.ShapeDtypeStruct((t, D), BF16)] * 3
                  + [jax.ShapeDtypeStruct((2, t, D), BF16), jax.ShapeDtypeStruct((3, t, D), BF16),
                     jax.ShapeDtypeStruct((t, D), BF16), jax.ShapeDtypeStruct((1, 2 * D), F32),
                     jax.ShapeDtypeStruct((8, D), F32)],
        scratch_shapes=[pltpu.VMEM((D, D), BF16)] * 3 + [pltpu.VMEM((8, D), F32),
                                                         pltpu.SemaphoreType.DMA((3 * N_DEV,))])


def _inproj_bwd(dconv, dq, dkv, dgp, w_in, h, g, dh_res, name, comm=None):
    t = h.shape[0]

    def body(dc_ref, dq_ref, dkv_ref, dgp_ref, w_hbm, h_ref, g_ref, dres_ref, dh_ref, dg_ref, w_v, sems):
        step = pl.program_id(0)
        _load_resident(step, [(w_hbm, w_v)], sems)
        du = _dot_nt(dq_ref[...], w_v[3])
        for k in range(3):
            du = du + _dot_nt(dc_ref[k], w_v[k])
        for k in range(2):
            du = du + _dot_nt(dkv_ref[k].astype(BF16), w_v[4 + k]) + _dot_nt(dgp_ref[k], w_v[6 + k])
        dx, dg = _rms_bwd_tile(h_ref[...], g_ref[...], du)
        dh_ref[...] = dres_ref[...] + dx
        _accumulate(dg_ref, step, dg)

    return _call(
        body, (dconv, dq, dkv, dgp, w_in, h, g, dh_res), grid=(t // TM,), name=name, comm=comm,
        in_specs=[_blk_row_spec(3, TM, D), _row_spec(TM, D), _blk_row_spec(2, TM, D), _blk_row_spec(2, TM, D), _ANY,
                  _row_spec(TM, D), _const_spec((1, D)), _row_spec(TM, D)],
        out_specs=[_row_spec(TM, D), _const_spec((1, D))],
        out_shape=[jax.ShapeDtypeStruct((t, D), F32), jax.ShapeDtypeStruct((1, D), F32)],
        scratch_shapes=[pltpu.VMEM((N_DEV, D, D), BF16), pltpu.SemaphoreType.DMA((1,))])


def _softmax_rows(s):
    e = jnp.exp(s - jnp.max(s, axis=-1, keepdims=True))
    return e / jnp.sum(e, axis=-1, keepdims=True)


def _cross_pairs(cross_hbm, wq_v, wo_v):
    return _square_pairs(cross_hbm, 0, wq_v) + _square_pairs(cross_hbm, 1, wo_v)


def _cross_fwd(h, g, mem, g_mem, w_ckv, w_cross, name):
    t = h.shape[0]
    m = mem.shape[0]
    scale = X_DH ** -0.5

    def body(h_ref, g_ref, mem_ref, gm_ref, wkv_ref, cross_hbm, hn_ref, qx_ref, o_ref, h3_ref, mn_ref, kv_ref,
             wq_v, wo_v, sems):
        _load_resident(pl.program_id(0), _cross_pairs(cross_hbm, wq_v, wo_v), sems)

        @pl.when(pl.program_id(0) == 0)
        def _():
            mn = _rms_fwd_tile(mem_ref[...], gm_ref[...]).astype(BF16)
            mn_ref[...] = mn
            for j in range(N_DEV):
                kv_ref[j] = _dot(mn, wkv_ref[j]).astype(BF16)

        ht = h_ref[...]
        hn = _rms_fwd_tile(ht, g_ref[...]).astype(BF16)
        hn_ref[...] = hn
        qx = _dot(hn, wq_v[...]).astype(BF16)
        qx_ref[...] = qx
        for hd in range(X_H):
            lo, hi = hd * X_DH, (hd + 1) * X_DH
            p = _softmax_rows(_dot_nt(qx[:, lo:hi], kv_ref[hd]) * scale)
            o_ref[:, lo:hi] = _dot(p.astype(BF16), kv_ref[X_H + hd]).astype(BF16)
        h3_ref[...] = ht + _dot(o_ref[...], wo_v[...])

    return pl.pallas_call(
        body, grid=(t // TM,), name=name,
        in_specs=[_row_spec(TM, D), _const_spec((1, D)), _const_spec((m, D)), _const_spec((1, D)),
                  _const_spec((N_DEV, D, X_DH)), _ANY],
        out_specs=[_row_spec(TM, D)] * 4 + [_const_spec((m, D)), _const_spec((N_DEV, m, X_DH))],
        out_shape=[jax.ShapeDtypeStruct((t, D), BF16)] * 3 + [jax.ShapeDtypeStruct((t, D), F32),
                                                              jax.ShapeDtypeStruct((m, D), BF16),
                                                              jax.ShapeDtypeStruct((N_DEV, m, X_DH), BF16)],
        scratch_shapes=[pltpu.VMEM((D, D), BF16)] * 2 + [pltpu.SemaphoreType.DMA((2 * N_DEV,))],
        compiler_params=_cparams(),
    )(h, g, mem, g_mem, w_ckv, w_cross)


def _cross_bwd(dh3, h, g, qx, kv, mem, g_mem, w_ckv, w_cross, name, comm=None):
    t = h.shape[0]
    m = kv.shape[1]
    scale = X_DH ** -0.5

    def body(dh_ref, h_ref, g_ref, qx_ref, kv_ref, mem_ref, gm_ref, wkv_ref, cross_hbm,
             dhb_ref, dqx_ref, dkv_ref, dh2_ref, dg_ref, dgm_ref, wq_v, wo_v, sems):
        step = pl.program_id(0)
        _load_resident(step, _cross_pairs(cross_hbm, wq_v, wo_v), sems)

        @pl.when(step == 0)
        def _():
            dkv_ref[...] = jnp.zeros_like(dkv_ref)

        dht = dh_ref[...]
        dhb = dht.astype(BF16)
        dhb_ref[...] = dhb
        do = _dot_nt(dhb, wo_v[...]).astype(BF16)
        for hd in range(X_H):
            lo, hi = hd * X_DH, (hd + 1) * X_DH
            qh = qx_ref[:, lo:hi]
            kh = kv_ref[hd]
            p = _softmax_rows(_dot_nt(qh, kh) * scale)
            doh = do[:, lo:hi]
            dp = _dot_nt(doh, kv_ref[X_H + hd])
            ds = (p * (dp - jnp.sum(dp * p, axis=-1, keepdims=True)) * scale).astype(BF16)
            dqx_ref[:, lo:hi] = _dot(ds, kh).astype(BF16)
            dkv_ref[hd] += _dot_tn(ds, qh)
            dkv_ref[X_H + hd] += _dot_tn(p.astype(BF16), doh)
        dhn = _dot_nt(dqx_ref[...], wq_v[...])
        dx, dg = _rms_bwd_tile(h_ref[...], g_ref[...], dhn)
        dh2_ref[...] = dht + dx
        _accumulate(dg_ref, step, dg)

        @pl.when(step == t // TM - 1)
        def _():
            dmn = jnp.zeros((m, D), F32)
            for j in range(N_DEV):
                dmn = dmn + _dot_nt(dkv_ref[j].astype(BF16), wkv_ref[j])
            dgm_ref[...] = _rms_bwd_tile(mem_ref[...], gm_ref[...], dmn)[1]

    return _call(
        body, (dh3, h, g, qx, kv, mem, g_mem, w_ckv, w_cross), grid=(t // TM,), name=name, comm=comm,
        in_specs=[_row_spec(TM, D), _row_spec(TM, D), _const_spec((1, D)), _row_spec(TM, D),
                  _const_spec((N_DEV, m, X_DH)), _const_spec((m, D)), _const_spec((1, D)),
                  _const_spec((N_DEV, D, X_DH)), _ANY],
        out_specs=[_row_spec(TM, D), _row_spec(TM, D), _const_spec((N_DEV, m, X_DH)), _row_spec(TM, D),
                   _const_spec((1, D)), _const_spec((1, D))],
        out_shape=[jax.ShapeDtypeStruct((t, D), BF16), jax.ShapeDtypeStruct((t, D), BF16),
                   jax.ShapeDtypeStruct((N_DEV, m, X_DH), F32), jax.ShapeDtypeStruct((t, D), F32),
                   jax.ShapeDtypeStruct((1, D), F32), jax.ShapeDtypeStruct((1, D), F32)],
        scratch_shapes=[pltpu.VMEM((D, D), BF16)] * 2 + [pltpu.SemaphoreType.DMA((2 * N_DEV,))])


def _adamw(w, parts, m, v, name, row_block=0, token=None):
    r, c = w.shape
    n = parts.shape[0]
    tr = _pick_tile(r, (256, 352, 128))
    off = row_block * (r // tr)

    def body(*refs):
        if token is None:
            _adamw_update(None, *refs)
        else:
            _adamw_update(refs[4], *refs[:4], *refs[5:])

    spec = _row_spec(tr, c)
    in_specs = [spec, pl.BlockSpec((n, tr, c), lambda i: (0, i + off, 0)), spec, spec]
    operands = (w, parts, m, v)
    if token is not None:
        in_specs.append(_const_spec(token.shape))
        operands += (token,)
    return pl.pallas_call(
        body, grid=(r // tr,), name=name, in_specs=in_specs, out_specs=[spec] * 4,
        out_shape=[jax.ShapeDtypeStruct((r, c), F32)] * 4,
        compiler_params=_cparams(),
    )(*operands)


def _adamw_update(tok_ref, w_ref, p_ref, m_ref, v_ref, g_ref, d_ref, nm_ref, nv_ref):
    gt = p_ref[0].astype(F32)
    for k in range(1, p_ref.shape[0]):
        gt = gt + p_ref[k].astype(F32)
    if tok_ref is not None:
        gt = gt + tok_ref[0:1, 0:1]
    _adamw_apply(gt, w_ref, m_ref, v_ref, g_ref, d_ref, nm_ref, nv_ref)


def _adamw_own(w, land, own, chip, m, v, name, row_block=0, token=None):
    r, c = w.shape
    tr = _pick_tile(r, (256, 352, 128))
    off = row_block * (r // tr)

    def body(chip_ref, w_ref, land_ref, own_ref, m_ref, v_ref, *rest):
        mine = own_ref[0].astype(F32)
        gt = jnp.where(chip_ref[0] == 0, mine, land_ref[0].astype(F32))
        for k in range(1, N_CHIP):
            gt = gt + jnp.where(chip_ref[0] == k, mine, land_ref[k].astype(F32))
        if token is not None:
            gt = gt + rest[0][0:1, 0:1]
        _adamw_apply(gt, w_ref, m_ref, v_ref, *rest[-4:])

    spec = pl.BlockSpec((tr, c), lambda i, chip_ref: (i, 0))
    in_specs = [spec, pl.BlockSpec((N_CHIP, tr, c), lambda i, chip_ref: (0, i + off, 0)),
                pl.BlockSpec((1, tr, c), lambda i, chip_ref: (chip_ref[0], i + off, 0)), spec, spec]
    operands = (chip, w, land, own, m, v)
    if token is not None:
        in_specs.append(pl.BlockSpec(token.shape, lambda i, chip_ref: (0, 0)))
        operands += (token,)
    return pl.pallas_call(
        body, name=name,
        grid_spec=pltpu.PrefetchScalarGridSpec(
            num_scalar_prefetch=1, grid=(r // tr,), in_specs=in_specs, out_specs=[spec] * 4),
        out_shape=[jax.ShapeDtypeStruct((r, c), F32)] * 4,
        compiler_params=_cparams(),
    )(*operands)


def _adamw_apply(gt, w_ref, m_ref, v_ref, g_ref, d_ref, nm_ref, nv_ref):
    g_ref[...] = gt
    nm = ADAM_B1 * m_ref[...] + (1.0 - ADAM_B1) * gt
    nv = ADAM_B2 * v_ref[...] + (1.0 - ADAM_B2) * jnp.square(gt)
    m_hat = nm / (1.0 - ADAM_B1 ** ADAM_STEP)
    v_hat = nv / (1.0 - ADAM_B2 ** ADAM_STEP)
    d_ref[...] = -ADAM_LR * (m_hat / (jnp.sqrt(v_hat) + ADAM_EPS) + ADAM_WD * w_ref[...])
    nm_ref[...] = nm
    nv_ref[...] = nv


def _mesh_pos():
    return lax.axis_index("x"), lax.axis_index("y"), lax.axis_index("c")


def _both(first, second):
    n_in, n_out, n_sem = len(first.inputs), len(first.out_shapes), len(first.sem_shapes)

    def run(round_name):
        def both(in_refs, out_refs, sems):
            getattr(first, round_name)(in_refs[:n_in], out_refs[:n_out], sems[:n_sem])
            getattr(second, round_name)(in_refs[n_in:], out_refs[n_out:], sems[n_sem:])
        return both

    return types.SimpleNamespace(
        inputs=first.inputs + second.inputs, out_shapes=first.out_shapes + second.out_shapes,
        sem_shapes=first.sem_shapes + second.sem_shapes, start=run("start"), middle=run("middle"),
        finish=run("finish"))


def _no_round(in_refs, out_refs, sems):
    pass


def _run_exchange(comm, name):
    c_in, c_out = len(comm.inputs), len(comm.out_shapes)

    def body(*refs):
        cins, couts, sems = refs[:c_in], refs[c_in:c_in + c_out], refs[c_in + c_out:]
        comm.start(cins, couts, sems)
        comm.middle(cins, couts, sems)
        comm.finish(cins, couts, sems)

    return list(pl.pallas_call(
        body, name=name, out_shape=list(comm.out_shapes),
        in_specs=[_ANY] * c_in, out_specs=[_ANY] * c_out, scratch_shapes=list(comm.sem_shapes),
    )(*comm.inputs))


def _gather_exchange(shards):
    n_arr = len(shards)

    def plan(x_refs, out_refs, sems):
        send_sems, recv_sems, local_sems = sems[:3]
        stage = sems[3:]
        x, y, c = _mesh_pos()
        me, sibling = (x, y, c), (x, y, 1 - c)
        xn, yn, diag = (1 - x, y), (x, 1 - y), (1 - x, 1 - y)

        def slot(a, px, py, pc, half=None):
            ref = out_refs[a].at[4 * px + 2 * py + pc]
            if half is None:
                return ref
            rows = shards[a].shape[0] // 2
            return ref.at[half * rows:(half + 1) * rows]

        def copy(a, k, block, to, half=None, src=None):
            dst = slot(a, *block, half)
            return pltpu.make_async_remote_copy(
                src_ref=dst if src is None else src, dst_ref=dst,
                send_sem=send_sems.at[a, k], recv_sem=recv_sems.at[a, k],
                device_id=to, device_id_type=pl.DeviceIdType.MESH)

        return types.SimpleNamespace(
            me=me, sibling=sibling, xn=xn, yn=yn, diag=diag, c=c, copy=copy,
            mine_in=[pltpu.make_async_copy(x_refs[a], stage[a], local_sems.at[a, 0]) for a in range(n_arr)],
            mine_out=[pltpu.make_async_copy(stage[a], slot(a, *me), local_sems.at[a, 1]) for a in range(n_arr)],
            first=[cp for a in range(n_arr) for cp in (
                copy(a, 0, me, sibling, src=x_refs[a]), copy(a, 1, me, (*xn, c), src=x_refs[a]),
                copy(a, 2, me, (*yn, c), src=x_refs[a]))],
            second=lambda a: (copy(a, 3, (*xn, c), (*yn, c), half=0), copy(a, 5, (*xn, c), sibling),
                              copy(a, 4, (*yn, c), (*xn, c), half=1), copy(a, 6, (*yn, c), sibling)),
            third=lambda a: (copy(a, 7, (*diag, c), sibling, half=0), copy(a, 8, (*diag, c), sibling, half=1)))

    def start(x_refs, out_refs, sems):
        p = plan(x_refs, out_refs, sems)
        for cp in p.first + p.mine_in:
            cp.start()
        for cp_in, cp_out in zip(p.mine_in, p.mine_out):
            cp_in.wait()
            cp_out.start()

    def middle(x_refs, out_refs, sems):
        p = plan(x_refs, out_refs, sems)
        for a in range(n_arr):
            to_yn, x_to_sib, to_xn, y_to_sib = p.second(a)
            p.copy(a, 1, (*p.xn, p.c), p.me).wait_recv()
            to_yn.start()
            x_to_sib.start()
            p.copy(a, 2, (*p.yn, p.c), p.me).wait_recv()
            to_xn.start()
            y_to_sib.start()

    def finish(x_refs, out_refs, sems):
        p = plan(x_refs, out_refs, sems)
        for a in range(n_arr):
            half0_to_sib, half1_to_sib = p.third(a)
            p.copy(a, 3, (*p.diag, p.c), p.me, half=0).wait_recv()
            half0_to_sib.start()
            p.copy(a, 4, (*p.diag, p.c), p.me, half=1).wait_recv()
            half1_to_sib.start()
        other = 1 - p.c
        for a in range(n_arr):
            p.copy(a, 0, p.sibling, p.me).wait_recv()
            p.copy(a, 5, (*p.xn, other), p.me).wait_recv()
            p.copy(a, 6, (*p.yn, other), p.me).wait_recv()
            p.copy(a, 7, (*p.diag, other), p.me, half=0).wait_recv()
            p.copy(a, 8, (*p.diag, other), p.me, half=1).wait_recv()
        for cp in p.first:
            cp.wait_send()
        for a in range(n_arr):
            for cp in p.second(a) + p.third(a):
                cp.wait_send()
        for cp in p.mine_out:
            cp.wait()

    return types.SimpleNamespace(
        inputs=list(shards), start=start, middle=middle, finish=finish,
        out_shapes=[jax.ShapeDtypeStruct((N_DEV,) + s.shape, s.dtype) for s in shards],
        sem_shapes=[pltpu.SemaphoreType.DMA((n_arr, 9)), pltpu.SemaphoreType.DMA((n_arr, 9)),
                    pltpu.SemaphoreType.DMA((n_arr, 2))] + [pltpu.VMEM(s.shape, s.dtype) for s in shards])


def _pair_exchange(grads):
    n_arr = len(grads)

    def plan(g_refs, land_refs, sems):
        send_sems, recv_sems = sems
        x, y, c = _mesh_pos()
        return [pltpu.make_async_remote_copy(
            src_ref=g_refs[a].at[2 * k + 1 - c], dst_ref=land_refs[a].at[k],
            send_sem=send_sems.at[a, k], recv_sem=recv_sems.at[a, k],
            device_id=(x, y, 1 - c), device_id_type=pl.DeviceIdType.MESH)
            for a in range(n_arr) for k in range(N_CHIP)]

    def start(g_refs, land_refs, sems):
        for cp in plan(g_refs, land_refs, sems):
            cp.start()

    def finish(g_refs, land_refs, sems):
        for cp in plan(g_refs, land_refs, sems):
            cp.wait()

    return types.SimpleNamespace(
        inputs=list(grads), start=start, middle=_no_round, finish=finish,
        out_shapes=[jax.ShapeDtypeStruct((N_CHIP,) + g.shape[1:], g.dtype) for g in grads],
        sem_shapes=[pltpu.SemaphoreType.DMA((n_arr, N_CHIP)), pltpu.SemaphoreType.DMA((n_arr, N_CHIP))])


def _chip_exchange(parts):
    n_arr = len(parts)

    def plan(p_refs, land_refs, sems):
        send_sems, recv_sems, local_sems = sems
        x, y, c = _mesh_pos()
        my_chip = 2 * x + y
        chips = [(1 - x, y), (x, 1 - y), (1 - x, 1 - y)]
        local = [pltpu.make_async_copy(p_refs[a].at[my_chip], land_refs[a].at[my_chip], local_sems.at[a])
                 for a in range(n_arr)]

        def copy(a, k, src_slot, dst_slot, px, py):
            return pltpu.make_async_remote_copy(
                src_ref=p_refs[a].at[src_slot], dst_ref=land_refs[a].at[dst_slot],
                send_sem=send_sems.at[a, k], recv_sem=recv_sems.at[a, k],
                device_id=(px, py, c), device_id_type=pl.DeviceIdType.MESH)

        sends = [copy(a, k, 2 * px + py, my_chip, px, py) for a in range(n_arr) for k, (px, py) in enumerate(chips)]
        arrivals = [copy(a, k, my_chip, 2 * px + py, px, py) for a in range(n_arr)
                    for k, (px, py) in enumerate(chips)]
        return local, sends, arrivals

    def start(p_refs, land_refs, sems):
        local, sends, _ = plan(p_refs, land_refs, sems)
        for cp in local + sends:
            cp.start()

    def finish(p_refs, land_refs, sems):
        local, sends, arrivals = plan(p_refs, land_refs, sems)
        for cp in arrivals:
            cp.wait_recv()
        for cp in sends:
            cp.wait_send()
        for cp in local:
            cp.wait()

    return types.SimpleNamespace(
        inputs=list(parts), start=start, middle=_no_round, finish=finish,
        out_shapes=[jax.ShapeDtypeStruct(p.shape, p.dtype) for p in parts],
        sem_shapes=[pltpu.SemaphoreType.DMA((n_arr, 3)), pltpu.SemaphoreType.DMA((n_arr, 3)),
                    pltpu.SemaphoreType.DMA((n_arr,))])


_HBM = pl.BlockSpec(memory_space=pltpu.HBM)
_SEM = pl.BlockSpec(memory_space=pltpu.SEMAPHORE)
_DATAFLOW = pltpu.SideEffectType.DATAFLOW_SIDE_EFFECTING


def _chip_copies(p_refs, land_refs, send_sems, recv_sems):
    x, y, c = _mesh_pos()
    my_chip = 2 * x + y
    chips = [(1 - x, y), (x, 1 - y), (1 - x, 1 - y)]
    return [pltpu.make_async_remote_copy(
        src_ref=p_refs[a].at[2 * px + py], dst_ref=land_refs[a].at[my_chip],
        send_sem=send_sems[3 * a + k], recv_sem=recv_sems[3 * a + k],
        device_id=(px, py, c), device_id_type=pl.DeviceIdType.MESH)
        for a in range(len(p_refs)) for k, (px, py) in enumerate(chips)]


def _chip_exchange_begin(parts, name):
    n_arr = len(parts)
    n_buf, n_copy = 2 * n_arr, 3 * n_arr
    lands = [lax.empty(p.shape, p.dtype) for p in parts]

    def body(*refs):
        p_refs, land_refs = refs[:n_arr], refs[n_arr:n_buf]
        send_sems, recv_sems, token = refs[n_buf:n_buf + n_copy], refs[n_buf + n_copy:n_buf + 2 * n_copy], refs[-1]
        for cp in _chip_copies(p_refs, land_refs, send_sems, recv_sems):
            cp.start()
        token[...] = jnp.zeros_like(token)

    bufs = list(parts) + list(lands)
    outs = pl.pallas_call(
        body, name=name,
        out_shape=(*[pltpu.SemaphoreType.DMA(())] * (2 * n_copy), *[pltpu.HBM(b.shape, b.dtype) for b in bufs],
                   jax.ShapeDtypeStruct((8, 128), F32)),
        in_specs=[_HBM] * n_buf,
        out_specs=(*[_SEM] * (2 * n_copy), *[_HBM] * n_buf, pl.BlockSpec(memory_space=pltpu.VMEM)),
        input_output_aliases={i: 2 * n_copy + i for i in range(n_buf)},
        compiler_params=pltpu.CompilerParams(has_side_effects=_DATAFLOW),
    )(*[pltpu.with_memory_space_constraint(b, pltpu.HBM) for b in bufs])
    sems = list(outs[:2 * n_copy])
    thru = list(outs[2 * n_copy:2 * n_copy + n_buf])
    return types.SimpleNamespace(send_sems=sems[:n_copy], recv_sems=sems[n_copy:], parts=thru[:n_arr],
                                 lands=thru[n_arr:], token=outs[-1])


def _chip_exchange_end(flight, after, name):
    send_sems, recv_sems, parts, lands = flight.send_sems, flight.recv_sems, flight.parts, flight.lands
    n_arr = len(parts)
    n_buf, n_copy = 2 * n_arr, 3 * n_arr

    def body(*refs):
        p_refs, land_refs = refs[:n_arr], refs[n_arr:n_buf]
        sems = refs[n_buf:n_buf + 2 * n_copy]
        for cp in _chip_copies(p_refs, land_refs, sems[:n_copy], sems[n_copy:]):
            cp.wait_send()
            cp.wait_recv()

    bufs = list(parts) + list(lands)
    outs = pl.pallas_call(
        body, name=name, out_shape=tuple(pltpu.HBM(b.shape, b.dtype) for b in bufs),
        in_specs=[_HBM] * n_buf + [_SEM] * (2 * n_copy) + [_ANY], out_specs=tuple([_HBM] * n_buf),
        input_output_aliases={i: i for i in range(n_buf)},
        compiler_params=pltpu.CompilerParams(has_side_effects=_DATAFLOW),
    )(*bufs, *send_sems, *recv_sems, after)
    return list(outs[:n_arr]), list(outs[n_arr:])


def _row_tile(r, cap=640):
    best = None
    for cand in range(16, min(r, cap) + 1, 16):
        if r % cand == 0:
            best = cand
    return best if best is not None else r


def _pair_sum(gs, landeds, core, name):
    tiles = [_row_tile(g.shape[1]) for g in gs]
    counts = [g.shape[1] // tr for g, tr in zip(gs, tiles)]
    n_arr = len(gs)

    def body(core_ref, *refs):
        for a in range(n_arr):
            mine, theirs, out = refs[2 * a], refs[2 * a + 1], refs[2 * n_arr + a]
            out[0] = (mine[0].astype(F32) + theirs[0].astype(F32)).astype(out.dtype)

    in_specs, out_specs, operands = [], [], []
    for g, landed, tr, count in zip(gs, landeds, tiles, counts):
        c_dim = g.shape[2]
        last = count - 1
        in_specs += [pl.BlockSpec((1, tr, c_dim),
                                  lambda k, i, core_ref, last=last: (2 * k + core_ref[0], jnp.minimum(i, last), 0)),
                     pl.BlockSpec((1, tr, c_dim), lambda k, i, core_ref, last=last: (k, jnp.minimum(i, last), 0))]
        out_specs.append(pl.BlockSpec((1, tr, c_dim), lambda k, i, core_ref, last=last: (k, jnp.minimum(i, last), 0)))
        operands += [g, landed]
    return list(pl.pallas_call(
        body, name=name,
        grid_spec=pltpu.PrefetchScalarGridSpec(
            num_scalar_prefetch=1, grid=(N_CHIP, max(counts)), in_specs=in_specs, out_specs=out_specs),
        out_shape=[jax.ShapeDtypeStruct((N_CHIP,) + g.shape[1:], g.dtype) for g in gs],
        compiler_params=_cparams(2),
    )(core, *operands))


def _sum_slots(parts, name):
    n, r, c_dim = parts.shape
    tr = _row_tile(r)

    def body(p_ref, o_ref):
        acc = p_ref[0].astype(F32)
        for k in range(1, n):
            acc = acc + p_ref[k].astype(F32)
        o_ref[...] = acc

    return pl.pallas_call(
        body, grid=(r // tr,), name=name,
        in_specs=[pl.BlockSpec((n, tr, c_dim), lambda i: (0, i, 0))],
        out_specs=_row_spec(tr, c_dim),
        out_shape=jax.ShapeDtypeStruct((r, c_dim), F32),
        compiler_params=_cparams(),
    )(parts)


GAINS = ("g_ffn1", "g_mix", "g_cross", "g_mem", "g_ffn2", "g_final")
SMALL = GAINS + ("b_gate", "conv_w")
SMALL_R = 16
LOSS_ROW = 11
WEIGHT_ORDER = ("g_ffn1", "w_ffn1_gu", "w_ffn1_down", "g_mix", "w_in", "b_gate", "conv_w", "w_conv_out",
                "w_attn_out", "w_o", "g_cross", "g_mem", "w_cq", "w_ckv", "w_co", "g_ffn2", "w_ffn2_gu",
                "w_ffn2_down", "g_final")
GU_NAMES = ("w_ffn1_gu", "w_ffn2_gu")


def _pack_small(vals, conv_rows):
    rows = [vals[n].reshape(1, D) for n in GAINS] + [vals["b_gate"].reshape(2, D), conv_rows.reshape(CONV_K, D)]
    used = len(GAINS) + 2 + CONV_K
    return jnp.concatenate(rows + [jnp.zeros((SMALL_R - used, D), F32)], axis=0)


def _unpack_small(buf):
    out = {n: buf[k] for k, n in enumerate(GAINS)}
    out["b_gate"] = buf[6:8].reshape(2 * D)
    out["conv_w"] = buf[8:8 + CONV_K]
    return out


def _exchange_shards(wts):
    out = {n: jnp.pad(wts[n].T.astype(BF16), ((0, FF_PAD - FF_BLK), (0, 0))) for n in GU_NAMES}
    for n in ("w_ckv", "w_in", "w_ffn1_down", "w_ffn2_down"):
        out[n] = wts[n].astype(BF16)
    out["mix"] = jnp.concatenate([wts[n].astype(BF16) for n in MIX_MATS], axis=0)
    out["cross"] = jnp.concatenate([wts[n].astype(BF16) for n in CROSS_MATS], axis=0)
    return out


def _reduce_group(grads, landed, core, names):
    return _pair_sum(grads, landed, core, "grads_pair_sum_" + "_".join(names))


def _step(x, mem, target, sh, conv_pad, gains, b_gate, core):
    wg1, wd1, conv_all = _run_exchange(_gather_exchange([sh["w_ffn1_gu"], sh["w_ffn1_down"], conv_pad]), "gather_ffn1")
    conv_w = conv_all[:, :CONV_K, :].transpose(1, 0, 2).reshape(CONV_K, D)
    (n1, gate1, up1, act1, h1), (w_in,) = _ffn_fwd(
        x, gains["g_ffn1"], wg1, wd1, "ffn1_fwd", comm=_gather_exchange([sh["w_in"]]))
    (u, pcg, qkv, yc), (w_mix, wd2) = _inproj_fwd(h1, gains["g_mix"], w_in, conv_w, "inproj_fwd",
                                                  comm=_gather_exchange([sh["mix"], sh["w_ffn2_down"]]))
    (ysb, ctot), (w_cross, w_ckv, wg2) = _sb_fwd(
        qkv, "sb_fwd", comm=_gather_exchange([sh["cross"], sh["w_ckv"], sh["w_ffn2_gu"]]))
    (a_mix, b_mix, merged, h2), _ = _mix_out_fwd(yc, ysb, pcg, b_gate, h1, w_mix, "mix_out_fwd")
    hn, qx, o_x, h3, mn, kv = _cross_fwd(h2, gains["g_cross"], mem, gains["g_mem"], w_ckv, w_cross, "cross_fwd")
    (n4, gate2, up2, act2, dh4, loss, dg_final), _ = _ffn_fwd(h3, gains["g_ffn2"], wg2, wd2, "ffn2_fwd",
                                                              head=(gains["g_final"], target))

    gs = {"g_final": dg_final}
    (dgu2, dh4b, dh3, gs["g_ffn2"]), _ = _ffn_bwd(dh4, h3, gains["g_ffn2"], gate2, up2, wg2, wd2, "ffn2_bwd")
    grads_a = [_mm_tn_rows(dgu2, n4, FF_PAD, "dw_ffn2_gu"),
               _mm_tn_rows(act2, dh4b, FF_BLK, "dw_ffn2_down").reshape(N_DEV, DOWN_ROWS, D)]
    names_a = ["w_ffn2_gu", "w_ffn2_down"]
    (dh3b, dqx, dkv, dh2, gs["g_cross"], gs["g_mem"]), _ = _cross_bwd(
        dh3, h2, gains["g_cross"], qx, kv, mem, gains["g_mem"], w_ckv, w_cross, "cross_bwd")
    grads_b = [_mm_tn_cols(mn, dkv, "dw_ckv"), _mm_tn_squares([(hn, dqx), (o_x, dh3b)], "dw_cross")]
    names_b = ["w_ckv", "cross"]
    (dh2b, da_mix, db_mix, dgp, dconv, dysb, gs["b_gate"], gs["conv_w"]), landed_ab = _mix_out_bwd(
        dh2, a_mix, b_mix, pcg, b_gate, conv_w, w_mix, "mix_out_bwd", comm=_pair_exchange(grads_a + grads_b))
    sums_ab = _reduce_group(grads_a + grads_b, landed_ab, core, names_a + names_b)
    grads_c = [_mm_tn_squares([(yc, da_mix), (ysb, db_mix), (merged, dh2b)], "dw_mix")]
    flight_ab = _chip_exchange_begin(sums_ab, "grads_to_chips_early_begin")
    (dq, dkv_sb), _ = _sb_bwd(qkv, dysb, ctot, flight_ab.token, "sb_bwd")
    grads_d = [_mm_tn_cols_many(u, [dconv, dq[None], dkv_sb, dgp], "dw_in")]
    (dh1, gs["g_mix"]), landed_cd = _inproj_bwd(dconv, dq, dkv_sb, dgp, w_in, h1, gains["g_mix"], dh2, "inproj_bwd",
                                                comm=_pair_exchange(grads_c + grads_d))
    sums_cd = _reduce_group(grads_c + grads_d, landed_cd, core, ["mix", "w_in"])
    flight_d = _chip_exchange_begin(sums_cd, "grads_to_chips_w_in_begin")
    (dgu1, dh1b, dx, gs["g_ffn1"]), _ = _ffn_bwd(dh1, x, gains["g_ffn1"] + flight_d.token[0, 0], gate1, up1, wg1, wd1,
                                                 "ffn1_bwd")
    dw_gu1 = _mm_tn_rows(dgu1, n1, FF_PAD, "dw_ffn1_gu")
    dw_down1, landed_gu1 = _mm_tn_rows(act1, dh1b, FF_BLK, "dw_ffn1_down", comm=_pair_exchange([dw_gu1]))
    grads_e = [dw_gu1, dw_down1.reshape(N_DEV, DOWN_ROWS, D)]
    names_e = ["w_ffn1_gu", "w_ffn1_down"]
    small_mine = _pack_small({n: gs[n] for n in GAINS + ("b_gate",)}, gs["conv_w"][:CONV_K])
    small_mine = small_mine.at[LOSS_ROW, 0].set(loss[0, 0])
    landed_down1, small_all = _run_exchange(
        _both(_pair_exchange(grads_e[1:]), _gather_exchange([small_mine])), "grads_to_sibling_ffn1_down")
    landed_e = landed_gu1 + [landed_down1]
    flight_e = _chip_exchange_begin(_reduce_group(grads_e, landed_e, core, names_e), "grads_to_chips_ffn1_begin")
    flights = [(names_a + names_b, flight_ab), (["mix", "w_in"], flight_d), (names_e, flight_e)]
    return dx, flights, small_all


def kernel(x, mem, g_ffn1, w_ffn1_gu, w_ffn1_down, g_mix, w_in, b_gate, conv_w, w_conv_out, w_attn_out, w_o, g_cross, g_mem, w_cq, w_ckv, w_co, g_ffn2, w_ffn2_gu, w_ffn2_down, g_final, loss_target, m_g_ffn1, m_w_ffn1_gu, m_w_ffn1_down, m_g_mix, m_w_in, m_b_gate, m_conv_w, m_w_conv_out, m_w_attn_out, m_w_o, m_g_cross, m_g_mem, m_w_cq, m_w_ckv, m_w_co, m_g_ffn2, m_w_ffn2_gu, m_w_ffn2_down, m_g_final, v_g_ffn1, v_w_ffn1_gu, v_w_ffn1_down, v_g_mix, v_w_in, v_b_gate, v_conv_w, v_w_conv_out, v_w_attn_out, v_w_o, v_g_cross, v_g_mem, v_w_cq, v_w_ckv, v_w_co, v_g_ffn2, v_w_ffn2_gu, v_w_ffn2_down, v_g_final):
    args = locals()
    wts = {n: args[n] for n in WEIGHT_ORDER}
    mom1 = {n: args["m_" + n] for n in WEIGHT_ORDER}
    mom2 = {n: args["v_" + n] for n in WEIGHT_ORDER}
    cx, cy, cc = _mesh_pos()
    dev = 4 * cx + 2 * cy + cc
    conv_cols = D // N_DEV

    conv_pad = jnp.concatenate([conv_w, jnp.zeros((SMALL_R - CONV_K, conv_cols), F32)], axis=0)
    gains = {n: wts[n].reshape(1, D) for n in GAINS}
    dx, flights, small_all = _step(x[0], mem[0], loss_target[0], _exchange_shards(wts), conv_pad, gains,
                                 b_gate.reshape(1, 2 * D), cc.reshape(1).astype(jnp.int32))

    grads, delta, new_m, new_v = {}, {}, {}, {}

    def operands(n, transposed):
        trio = (wts[n], mom1[n], mom2[n])
        return tuple(a.T for a in trio) if transposed else trio

    def record(n, res, transposed):
        grads[n], delta[n], new_m[n], new_v[n] = [r.T for r in res] if transposed else res

    early = [("w_ffn2_gu", "w_ffn2_gu", 0, True), ("w_ffn2_down", "w_ffn2_down", 0, False),
             ("w_ckv", "w_ckv", 0, False), ("w_in", "w_in", 0, False)]
    early += [(n, "mix", k, False) for k, n in enumerate(MIX_MATS)]
    early += [(n, "cross", k, False) for k, n in enumerate(CROSS_MATS)]
    chip = (2 * cx + cy).reshape(1).astype(jnp.int32)
    (names_early, flight_early), (names_w_in, flight_w_in), (last_names, flight_last) = flights
    token = flight_last.token
    own, land = {}, {}
    for names, flight, tag in ((names_early, flight_early, "early"), (names_w_in, flight_w_in, "w_in")):
        own_parts, landed = _chip_exchange_end(flight, token, "grads_to_chips_%s_end" % tag)
        own.update(zip(names, own_parts))
        land.update(zip(names, landed))
    for n, buf, row_block, transposed in early:
        w, m1, m2 = operands(n, transposed)
        record(n, _adamw_own(w, land[buf], own[buf], chip, m1, m2, "adamw_" + n, row_block, token), transposed)

    after = jnp.concatenate([new_v[n][:1, :1] for n, _, _, _ in early], axis=0)
    own_parts, landed = _chip_exchange_end(flight_last, after, "grads_to_chips_ffn1_end")
    for n, own_n, land_n, transposed in zip(last_names, own_parts, landed, (True, False)):
        w, m1, m2 = operands(n, transposed)
        record(n, _adamw_own(w, land_n, own_n, chip, m1, m2, "adamw_" + n), transposed)

    small_sum = _sum_slots(small_all, "small_grads_sum")
    loss = small_sum[LOSS_ROW, 0]
    grad_small = _unpack_small(small_sum)
    grad_small["conv_w"] = lax.dynamic_slice_in_dim(grad_small["conv_w"], dev * conv_cols, conv_cols, axis=1)
    grads.update(grad_small)

    def small_buf(vals):
        return _pack_small(vals, jnp.concatenate([vals["conv_w"], jnp.zeros((CONV_K, D - conv_cols), F32)], axis=1))

    _, d_s, m_s, v_s = _adamw(small_buf(wts), small_buf(grads)[None], small_buf(mom1), small_buf(mom2), "adamw_small")
    for res, buf in ((delta, d_s), (new_m, m_s), (new_v, v_s)):
        un = _unpack_small(buf)
        for n in GAINS + ("b_gate",):
            res[n] = un[n]
        res["conv_w"] = un["conv_w"][:, :conv_cols]

    return (loss, dx[None], *[grads[n] for n in WEIGHT_ORDER], *[delta[n] for n in WEIGHT_ORDER],
            *[new_m[n] for n in WEIGHT_ORDER], *[new_v[n] for n in WEIGHT_ORDER])
```

```python
import types

import jax
import jax.numpy as jnp
from jax import lax
from jax.experimental import pallas as pl
from jax.experimental.pallas import tpu as pltpu

F32 = jnp.float32
BF16 = jnp.bfloat16

D = 1024
DFF = 2816
SB_H = 8
SB_DH = 128
X_H = 4
X_DH = 256
CONV_K = 3
RMS_EPS = 1e-6
N_DEV = 8
N_CHIP = 4
SQ_ROWS = D // N_DEV

ADAM_LR = 0.001
ADAM_B1 = 0.9
ADAM_B2 = 0.999
ADAM_EPS = 1e-08
ADAM_WD = 0.01
ADAM_STEP = 10

TM = 256
TQ = 512
TK = 256
SB_HPS = 2
VMEM_LIMIT = 56 << 20

FF_BLK = DFF // 4
FF_PAD = 768
FF_SUB = 256
DOWN_ROWS = DFF // N_DEV

MIX_MATS = ("w_conv_out", "w_attn_out", "w_o")
CROSS_MATS = ("w_cq", "w_co")

_ANY = pl.BlockSpec(memory_space=pl.ANY)


def _cparams(n_axes=1):
    return pltpu.CompilerParams(
        dimension_semantics=("arbitrary",) * n_axes, vmem_limit_bytes=VMEM_LIMIT)


def _row_spec(tm, n):
    return pl.BlockSpec((tm, n), lambda i: (i, 0))


def _blk_row_spec(nb, tm, n):
    return pl.BlockSpec((nb, tm, n), lambda i: (0, i, 0))


def _const_spec(shape):
    zeros = (0,) * len(shape)
    return pl.BlockSpec(shape, lambda i: zeros)


def _dot(a, b):
    return jnp.dot(a, b, preferred_element_type=F32)


def _dot_nt(a, b):
    return lax.dot_general(a, b, (((1,), (1,)), ((), ())), preferred_element_type=F32)


def _dot_tn(a, b):
    return lax.dot_general(a, b, (((0,), (0,)), ((), ())), preferred_element_type=F32)


def _sigmoid(x):
    return 1.0 / (1.0 + jnp.exp(-x))


def _call(body, operands, *, grid, in_specs, out_specs, out_shape, scratch_shapes, name, comm=None):
    n_in, n_out, n_sc = len(in_specs), len(out_specs), len(scratch_shapes)
    if comm is None:
        outs = pl.pallas_call(
            body, grid=grid, name=name, in_specs=in_specs, out_specs=out_specs, out_shape=out_shape,
            scratch_shapes=scratch_shapes, compiler_params=_cparams(len(grid)))(*operands)
        return list(outs), []
    c_in, c_out, c_sem = len(comm.inputs), len(comm.out_shapes), len(comm.sem_shapes)

    def hosted(*refs):
        bounds = [0, n_in, c_in, n_out, c_out, n_sc, c_sem]
        parts, pos = [], 0
        for k in bounds[1:]:
            parts.append(refs[pos:pos + k])
            pos += k
        ins, cins, outs, couts, scr, sems = parts
        step, n_steps = pl.program_id(0), grid[0]
        for ax in range(1, len(grid)):
            step, n_steps = step * grid[ax] + pl.program_id(ax), n_steps * grid[ax]

        @pl.when(step == 0)
        def _():
            comm.start(cins, couts, sems)

        @pl.when(step == (2 * n_steps) // 3)
        def _():
            comm.middle(cins, couts, sems)

        body(*ins, *outs, *scr)

        @pl.when(step == n_steps - 1)
        def _():
            comm.finish(cins, couts, sems)

    res = pl.pallas_call(
        hosted, grid=grid, name=name, in_specs=list(in_specs) + [_ANY] * c_in,
        out_specs=list(out_specs) + [_ANY] * c_out, out_shape=list(out_shape) + list(comm.out_shapes),
        scratch_shapes=list(scratch_shapes) + list(comm.sem_shapes),
        compiler_params=_cparams(len(grid)))(*operands, *comm.inputs)
    return list(res[:n_out]), list(res[n_out:])


def _load_resident(step, pairs, sems):
    @pl.when(step == 0)
    def _():
        copies = [pltpu.make_async_copy(src, dst, sems.at[k]) for k, (src, dst) in enumerate(pairs)]
        for cp in copies:
            cp.start()
        for cp in copies:
            cp.wait()


def _square_pairs(buf_hbm, index, dst):
    off = index * SQ_ROWS
    return [(buf_hbm.at[d, off:off + SQ_ROWS, :], dst.at[d * SQ_ROWS:(d + 1) * SQ_ROWS, :]) for d in range(N_DEV)]


def _down_pairs(wd_hbm, dst):
    return [(wd_hbm.at[d], dst.at[d // 2, (d % 2) * DOWN_ROWS:(d % 2 + 1) * DOWN_ROWS, :]) for d in range(N_DEV)]


def _zero_down_pad(step, dst):
    @pl.when(step == 0)
    def _():
        dst[:, FF_BLK:, :] = jnp.zeros((4, FF_PAD - FF_BLK, D), BF16)


def _rms_fwd_tile(xt, g):
    r = lax.rsqrt(jnp.mean(xt * xt, axis=-1, keepdims=True) + RMS_EPS)
    return (xt * r) * g


def _rms_bwd_tile(xt, g, dn):
    r = lax.rsqrt(jnp.mean(xt * xt, axis=-1, keepdims=True) + RMS_EPS)
    xhat = xt * r
    dxhat = dn * g
    dx = r * (dxhat - xhat * jnp.mean(dxhat * xhat, axis=-1, keepdims=True))
    dg = jnp.sum(dn * xhat, axis=0, keepdims=True)
    return dx, dg


def _accumulate(ref, step, value):
    @pl.when(step == 0)
    def _():
        ref[...] = value

    @pl.when(step != 0)
    def _():
        ref[...] = ref[...] + value


def _ffn_fwd(x, g, wgu, wd, name, comm=None, head=None):
    t = x.shape[0]

    def body(x_ref, g_ref, wgu_hbm, wd_hbm, *refs):
        if head is None:
            n_ref, gate_ref, up_ref, act_ref, h_ref, wgu_v, wd_v, sems = refs
        else:
            gf_ref, t_ref, n_ref, gate_ref, up_ref, act_ref, dh_ref, loss_ref, dgf_ref, wgu_v, wd_v, sems = refs
        step = pl.program_id(0)
        _zero_down_pad(step, wd_v)
        _load_resident(step, [(wgu_hbm, wgu_v)] + _down_pairs(wd_hbm, wd_v), sems)
        xt = x_ref[...]
        n = _rms_fwd_tile(xt, g_ref[...]).astype(BF16)
        n_ref[...] = n
        acc = jnp.zeros((TM, D), F32)
        for j in range(4):
            for s in range(FF_PAD // FF_SUB):
                lo, hi = s * FF_SUB, (s + 1) * FF_SUB
                gt = _dot_nt(n, wgu_v[j, lo:hi, :])
                ut = _dot_nt(n, wgu_v[4 + j, lo:hi, :])
                gate_ref[j, :, lo:hi] = gt.astype(BF16)
                up_ref[j, :, lo:hi] = ut.astype(BF16)
                act_ref[j, :, lo:hi] = ((gt * _sigmoid(gt)) * ut).astype(BF16)
            acc = acc + _dot(act_ref[j], wd_v[j])
        ht = xt + 0.5 * acc
        if head is None:
            h_ref[...] = ht
        else:
            gain = gf_ref[...]
            diff = _rms_fwd_tile(ht, gain) - t_ref[...]
            part = 0.5 * jnp.sum(jnp.sum(diff * diff, axis=-1, keepdims=True) / D, axis=0, keepdims=True)
            dx, dg = _rms_bwd_tile(ht, gain, diff / D)
            dh_ref[...] = dx
            _accumulate(loss_ref, step, jnp.broadcast_to(part, (8, 128)))
            _accumulate(dgf_ref, step, dg)

    ff = jax.ShapeDtypeStruct((4, t, FF_PAD), BF16)
    operands, in_specs = (x, g, wgu, wd), [_row_spec(TM, D), _const_spec((1, D)), _ANY, _ANY]
    out_specs = [_row_spec(TM, D)] + [_blk_row_spec(4, TM, FF_PAD)] * 3 + [_row_spec(TM, D)]
    out_shape = [jax.ShapeDtypeStruct((t, D), BF16), ff, ff, ff, jax.ShapeDtypeStruct((t, D), F32)]
    if head is not None:
        operands += tuple(head)
        in_specs += [_const_spec((1, D)), _row_spec(TM, D)]
        out_specs += [_const_spec((8, 128)), _const_spec((1, D))]
        out_shape += [jax.ShapeDtypeStruct((8, 128), F32), jax.ShapeDtypeStruct((1, D), F32)]
    return _call(
        body, operands, grid=(t // TM,), name=name, comm=comm, in_specs=in_specs, out_specs=out_specs,
        out_shape=out_shape,
        scratch_shapes=[pltpu.VMEM((N_DEV, FF_PAD, D), BF16), pltpu.VMEM((4, FF_PAD, D), BF16),
                        pltpu.SemaphoreType.DMA((1 + N_DEV,))])


def _ffn_bwd(dh, xin, g, gate, up, wgu, wd, name, comm=None):
    t = dh.shape[0]

    def body(dh_ref, x_ref, g_ref, gate_ref, up_ref, wgu_hbm, wd_hbm,
             dgu_ref, dhb_ref, dx_ref, dg_ref, wgu_v, wd_v, sems):
        step = pl.program_id(0)
        _zero_down_pad(step, wd_v)
        _load_resident(step, [(wgu_hbm, wgu_v)] + _down_pairs(wd_hbm, wd_v), sems)
        dht = dh_ref[...]
        dhb = (0.5 * dht).astype(BF16)
        dhb_ref[...] = dhb
        dn = jnp.zeros((TM, D), F32)
        for j in range(4):
            for s in range(FF_PAD // FF_SUB):
                lo, hi = s * FF_SUB, (s + 1) * FF_SUB
                da = _dot_nt(dhb, wd_v[j, lo:hi, :])
                gt = gate_ref[j, :, lo:hi].astype(F32)
                ut = up_ref[j, :, lo:hi].astype(F32)
                sg = _sigmoid(gt)
                dgt = (da * ut * (sg * (1.0 + gt * (1.0 - sg)))).astype(BF16)
                dut = (da * (gt * sg)).astype(BF16)
                dgu_ref[j, :, lo:hi] = dgt
                dgu_ref[4 + j, :, lo:hi] = dut
            dn = dn + _dot(dgu_ref[j], wgu_v[j]) + _dot(dgu_ref[4 + j], wgu_v[4 + j])
        dx, dg = _rms_bwd_tile(x_ref[...], g_ref[...], dn)
        dx_ref[...] = dht + dx
        _accumulate(dg_ref, step, dg)

    return _call(
        body, (dh, xin, g, gate, up, wgu, wd), grid=(t // TM,), name=name, comm=comm,
        in_specs=[_row_spec(TM, D), _row_spec(TM, D), _const_spec((1, D)), _blk_row_spec(4, TM, FF_PAD),
                  _blk_row_spec(4, TM, FF_PAD), _ANY, _ANY],
        out_specs=[_blk_row_spec(N_DEV, TM, FF_PAD), _row_spec(TM, D), _row_spec(TM, D), _const_spec((1, D))],
        out_shape=[jax.ShapeDtypeStruct((N_DEV, t, FF_PAD), BF16), jax.ShapeDtypeStruct((t, D), BF16),
                   jax.ShapeDtypeStruct((t, D), F32), jax.ShapeDtypeStruct((1, D), F32)],
        scratch_shapes=[pltpu.VMEM((N_DEV, FF_PAD, D), BF16), pltpu.VMEM((4, FF_PAD, D), BF16),
                        pltpu.SemaphoreType.DMA((1 + N_DEV,))])


WIDE_TILES = (1024, 512, 256, 128)


def _pick_tile(n, options=(512, 256, 128)):
    for o in options:
        if n % o == 0:
            return o
    return n


def _mm_tn_squares(pairs, name):
    k, d = pairs[0][0].shape
    n = pairs[0][1].shape[1]
    tn = _pick_tile(n)
    count = len(pairs)

    def body(*refs):
        m = pl.program_id(0)
        for idx in range(count):
            @pl.when(m == idx)
            def _(idx=idx):
                refs[-1][...] = _dot_tn(refs[idx][...], refs[count + idx][...]).astype(BF16).reshape(
                    N_DEV, d // N_DEV, tn)

    a_specs = [pl.BlockSpec((k, d), lambda m, j: (0, 0)) for _ in pairs]
    b_specs = [pl.BlockSpec((k, tn), lambda m, j, idx=idx: (0, jnp.where(m == idx, j, 0))) for idx in range(count)]
    return pl.pallas_call(
        body, grid=(count, n // tn), name=name, in_specs=a_specs + b_specs,
        out_specs=pl.BlockSpec((N_DEV, d // N_DEV, tn), lambda m, j: (0, m, j)),
        out_shape=jax.ShapeDtypeStruct((N_DEV, count * (d // N_DEV), n), BF16),
        compiler_params=_cparams(2),
    )(*[a for a, _ in pairs], *[b for _, b in pairs])


def _mm_tn_cols_many(a, parts, name):
    k, m = a.shape
    n = parts[0].shape[2]
    tn = _pick_tile(n)
    firsts, total = [], 0
    for p in parts:
        firsts.append(total)
        total += p.shape[0]

    def body(a_ref, *refs):
        j = pl.program_id(0)
        for p, first, ref in zip(parts, firsts, refs):
            @pl.when(jnp.logical_and(j >= first, j < first + p.shape[0]))
            def _(ref=ref):
                refs[-1][0] = _dot_tn(a_ref[...].astype(BF16), ref[0].astype(BF16)).astype(BF16)

    specs = [pl.BlockSpec((1, k, tn), lambda j, i, first=first, last=p.shape[0] - 1:
                          (jnp.clip(j - first, 0, last), 0, jnp.where(jnp.logical_and(j >= first, j <= first + last), i, 0)))
             for p, first in zip(parts, firsts)]
    return pl.pallas_call(
        body, grid=(total, n // tn), name=name,
        in_specs=[pl.BlockSpec((k, m), lambda j, i: (0, 0))] + specs,
        out_specs=pl.BlockSpec((1, m, tn), lambda j, i: (j, 0, i)),
        out_shape=jax.ShapeDtypeStruct((total, m, n), BF16),
        compiler_params=_cparams(2),
    )(a, *parts)


def _mm_tn_cols(a, b, name):
    k, m = a.shape
    nb, _, n = b.shape
    tm = _pick_tile(m, WIDE_TILES)

    def body(a_ref, b_ref, o_ref):
        o_ref[0] = _dot_tn(a_ref[...].astype(BF16), b_ref[0].astype(BF16)).astype(BF16)

    return pl.pallas_call(
        body, grid=(nb, m // tm), name=name,
        in_specs=[pl.BlockSpec((k, tm), lambda j, i: (0, i)), pl.BlockSpec((1, k, n), lambda j, i: (j, 0, 0))],
        out_specs=pl.BlockSpec((1, tm, n), lambda j, i: (j, i, 0)),
        out_shape=jax.ShapeDtypeStruct((nb, m, n), BF16),
        compiler_params=_cparams(2),
    )(a, b)


def _mm_tn_rows(a, b, keep, name, comm=None):
    nb, k, m = a.shape
    _, n = b.shape
    tn = _pick_tile(n, WIDE_TILES)

    def body(a_ref, b_ref, o_ref):
        o_ref[0] = _dot_tn(a_ref[0], b_ref[...])[:keep].astype(BF16)

    (out,), couts = _call(
        body, (a, b), grid=(nb, n // tn), name=name, comm=comm,
        in_specs=[pl.BlockSpec((1, k, m), lambda j, i: (j, 0, 0)), pl.BlockSpec((k, tn), lambda j, i: (0, i))],
        out_specs=[pl.BlockSpec((1, keep, tn), lambda j, i: (j, 0, i))],
        out_shape=[jax.ShapeDtypeStruct((nb, keep, n), BF16)], scratch_shapes=[])
    return out if comm is None else (out, couts)


PCG_W = 5 * D
QKV_W = 3 * D
PROJ_SUB = 512


def _inproj_fwd(h, g, w_in, conv_w, name, comm=None):
    t = h.shape[0]

    def body(h_ref, g_ref, w_hbm, cw_ref, u_ref, pcg_ref, qkv_ref, yc_ref, w_v, tail_v, sems):
        step = pl.program_id(0)
        _load_resident(step, [(w_hbm, w_v)], sems)

        @pl.when(step == 0)
        def _():
            tail_v[...] = jnp.zeros_like(tail_v)

        u = _rms_fwd_tile(h_ref[...], g_ref[...]).astype(BF16)
        u_ref[...] = u
        for blk in range(N_DEV):
            for s in range(D // PROJ_SUB):
                lo, hi = s * PROJ_SUB, (s + 1) * PROJ_SUB
                p = _dot(u, w_v[blk, :, lo:hi])
                if blk < 3:
                    pcg_ref[:, blk * D + lo:blk * D + hi] = p
                elif blk < 6:
                    qkv_ref[:, (blk - 3) * D + lo:(blk - 3) * D + hi] = p.astype(BF16)
                else:
                    pcg_ref[:, (blk - 3) * D + lo:(blk - 3) * D + hi] = p
        xc = pcg_ref[:, D:2 * D] * pcg_ref[:, 2 * D:3 * D]
        ext = jnp.concatenate([tail_v[...], xc], axis=0)
        conv = (cw_ref[0:1, :] * pltpu.roll(ext, 2, 0)[8:] + cw_ref[1:2, :] * pltpu.roll(ext, 1, 0)[8:]
                + cw_ref[2:3, :] * xc)
        yc_ref[...] = (pcg_ref[:, 0:D] * conv).astype(BF16)
        tail_v[...] = xc[TM - 8:]

    return _call(
        body, (h, g, w_in, conv_w), grid=(t // TM,), name=name, comm=comm,
        in_specs=[_row_spec(TM, D), _const_spec((1, D)), _ANY, _const_spec((CONV_K, D))],
        out_specs=[_row_spec(TM, D), _row_spec(TM, PCG_W), _row_spec(TM, QKV_W), _row_spec(TM, D)],
        out_shape=[jax.ShapeDtypeStruct((t, D), BF16), jax.ShapeDtypeStruct((t, PCG_W), F32),
                   jax.ShapeDtypeStruct((t, QKV_W), BF16), jax.ShapeDtypeStruct((t, D), BF16)],
        scratch_shapes=[pltpu.VMEM((N_DEV, D, D), BF16), pltpu.VMEM((8, D), F32), pltpu.SemaphoreType.DMA((1,))])


def _tri2(cond):
    rr = lax.broadcasted_iota(jnp.int32, (2 * TK, TK), 0) & (TK - 1)
    cc = lax.broadcasted_iota(jnp.int32, (2 * TK, TK), 1)
    return cond(rr, cc).astype(BF16)


def _causal(shift, row0=0):
    rr = lax.broadcasted_iota(jnp.int32, (TQ - row0, TK), 0) + row0
    cc = lax.broadcasted_iota(jnp.int32, (TQ - row0, TK), 1)
    return cc + shift < rr


def _cumdot(v, tri2):
    hi = v.astype(BF16)
    lo = (v - hi.astype(F32)).astype(BF16)
    return _dot(jnp.concatenate([hi, lo], axis=1), tri2)


LOG2_E = 1.4426950408889634


def _log_1m_beta(z):
    return -(jnp.maximum(z, 0.0) + jnp.log2(1.0 + jnp.exp2(-jnp.abs(z))))


def _sb_specs(t):
    g = SB_H // SB_HPS
    w = SB_HPS * SB_DH
    q_spec = pl.BlockSpec((TQ, w), lambda h, i: (i, h))
    k_spec = pl.BlockSpec((t, w), lambda h, i: (0, g + h))
    v_spec = pl.BlockSpec((t, w), lambda h, i: (0, 2 * g + h))
    ct_spec = pl.BlockSpec((SB_HPS, TQ, 1), lambda h, i: (h, i, 0))
    return g, w, q_spec, k_spec, v_spec, ct_spec


def _sb_fwd(qkv, name, comm=None):
    t = qkv.shape[0]
    scale = SB_DH ** -0.5
    g, w, q_spec, k_spec, v_spec, ct_spec = _sb_specs(t)

    def body(q_ref, k_ref, v_ref, y_ref, ct_ref):
        i = pl.program_id(1)
        later = _tri2(lambda j, s: j > s)
        n_diag = TQ // TK

        def block(j, carry, shift):
            off = pl.multiple_of(j * TK, TK)
            zs, ms = [], []
            for hd in range(SB_HPS):
                cols = slice(hd * SB_DH, (hd + 1) * SB_DH)
                z = _dot_nt(q_ref[:, cols], k_ref[pl.ds(off, TK), cols]) * (scale * LOG2_E)
                m = _log_1m_beta(z)
                if shift is not None:
                    m = jnp.where(_causal(shift), m, 0.0)
                zs.append(z)
                ms.append(m)
            after = _cumdot(jnp.concatenate(ms, axis=0), later)
            out = []
            for hd in range(SB_HPS):
                acc, c_sum = carry[hd]
                cols = slice(hd * SB_DH, (hd + 1) * SB_DH)
                a = jnp.exp2((ms[hd] + zs[hd]) + (c_sum + after[hd * TQ:(hd + 1) * TQ]))
                if shift is not None:
                    a = jnp.where(_causal(shift), a, 0.0)
                out.append((acc + _dot(a.astype(BF16), v_ref[pl.ds(off, TK), cols]),
                            c_sum + jnp.sum(ms[hd], axis=1, keepdims=True)))
            return tuple(out)

        carry = tuple((jnp.zeros((TQ, SB_DH), F32), jnp.zeros((TQ, 1), F32)) for _ in range(SB_HPS))
        for d in reversed(range(n_diag)):
            carry = block(i * n_diag + d, carry, d * TK)
        carry = lax.fori_loop(0, i * n_diag, lambda jj, c: block(i * n_diag - 1 - jj, c, None), carry)
        for hd in range(SB_HPS):
            y_ref[:, hd * SB_DH:(hd + 1) * SB_DH] = carry[hd][0].astype(BF16)
            ct_ref[hd] = carry[hd][1]

    return _call(
        body, (qkv, qkv, qkv), grid=(g, t // TQ), name=name, comm=comm,
        in_specs=[q_spec, k_spec, v_spec],
        out_specs=[q_spec, ct_spec],
        out_shape=[jax.ShapeDtypeStruct((t, D), BF16), jax.ShapeDtypeStruct((SB_H, t, 1), F32)],
        scratch_shapes=[])


def _sb_bwd(qkv, dy, ctot, after, name, comm=None):
    t = qkv.shape[0]
    scale = SB_DH ** -0.5
    g, w, q_spec, k_spec, v_spec, ct_spec = _sb_specs(t)
    acc_spec = pl.BlockSpec((2, t, w), lambda h, i: (0, 0, h))

    def body(q_ref, k_ref, v_ref, dy_ref, ct_ref, after_ref, dq_ref, dkv_ref):
        i = pl.program_id(1)

        @pl.when(i == 0)
        def _():
            dkv_ref[...] = jnp.zeros_like(dkv_ref)

        upto = _tri2(lambda j, s: j <= s)
        n_diag = TQ // TK

        def block(j, carry, shift):
            off = pl.multiple_of(j * TK, TK)
            r0 = 0 if shift is None else shift
            nr = TQ - r0
            causal = None if shift is None else _causal(shift, r0)

            def grow(old, delta):
                return old + delta if r0 == 0 else jnp.concatenate([old[:r0], old[r0:] + delta], axis=0)

            zs, ms = [], []
            for hd in range(SB_HPS):
                cols = slice(hd * SB_DH, (hd + 1) * SB_DH)
                z = _dot_nt(q_ref[r0:, cols], k_ref[pl.ds(off, TK), cols]) * (scale * LOG2_E)
                m = _log_1m_beta(z)
                if causal is not None:
                    m = jnp.where(causal, m, 0.0)
                zs.append(z)
                ms.append(m)
            m_upto = _cumdot(jnp.concatenate(ms, axis=0), upto)
            ls, a_s, es = [], [], []
            for hd in range(SB_HPS):
                cols = slice(hd * SB_DH, (hd + 1) * SB_DH)
                l = ms[hd] + zs[hd]
                a = jnp.exp2(l + ((ct_ref[hd, r0:] - carry[hd][1][r0:]) - m_upto[hd * nr:(hd + 1) * nr]))
                if causal is not None:
                    a = jnp.where(causal, a, 0.0)
                ls.append(l)
                a_s.append(a)
                es.append(_dot_nt(dy_ref[r0:, cols], v_ref[pl.ds(off, TK), cols]) * a)
            e_upto = _dot(jnp.concatenate(es, axis=0).astype(BF16), upto[:TK])
            out = []
            for hd in range(SB_HPS):
                dq, p_sum, e_sum = carry[hd]
                cols = slice(hd * SB_DH, (hd + 1) * SB_DH)
                e = es[hd]
                dz = e - jnp.exp2(ls[hd]) * (e_sum[r0:] + e_upto[hd * nr:(hd + 1) * nr])
                if causal is not None:
                    dz = jnp.where(causal, dz, 0.0)
                dzs = (dz * scale).astype(BF16)
                dkv_ref[0, pl.ds(off, TK), cols] += _dot_tn(dzs, q_ref[r0:, cols])
                dkv_ref[1, pl.ds(off, TK), cols] += _dot_tn(a_s[hd].astype(BF16), dy_ref[r0:, cols])
                out.append((grow(dq, _dot(dzs, k_ref[pl.ds(off, TK), cols])),
                            grow(p_sum, jnp.sum(ms[hd], axis=1, keepdims=True)),
                            grow(e_sum, jnp.sum(e, axis=1, keepdims=True))))
            return tuple(out)

        zero = jnp.zeros((TQ, 1), F32)
        init = tuple((jnp.zeros((TQ, SB_DH), F32), zero, zero) for _ in range(SB_HPS))
        carry = lax.fori_loop(0, i * n_diag, lambda j, c: block(j, c, None), init)
        for d in range(n_diag):
            carry = block(i * n_diag + d, carry, d * TK)
        for hd in range(SB_HPS):
            dq_ref[:, hd * SB_DH:(hd + 1) * SB_DH] = carry[hd][0].astype(BF16)

    return _call(
        body, (qkv, qkv, qkv, dy, ctot, after), grid=(g, t // TQ), name=name, comm=comm,
        in_specs=[q_spec, k_spec, v_spec, q_spec, ct_spec, pl.BlockSpec(after.shape, lambda h, i: (0, 0))],
        out_specs=[q_spec, acc_spec],
        out_shape=[jax.ShapeDtypeStruct((t, D), BF16), jax.ShapeDtypeStruct((2, t, D), F32)],
        scratch_shapes=[])


def _gate_specs():
    return [pl.BlockSpec((TM, D), lambda i: (i, 3)), pl.BlockSpec((TM, D), lambda i: (i, 4))]


def _mix_pairs(mix_hbm, dsts):
    pairs = []
    for index, dst in enumerate(dsts):
        pairs += _square_pairs(mix_hbm, index, dst)
    return pairs


def _mix_out_fwd(yc, ysb, pcg, b_gate, h, w_mix, name, comm=None):
    t = h.shape[0]

    def body(yc_ref, ysb_ref, gc_ref, gs_ref, b_ref, h_ref, mix_hbm,
             a_ref, b_out_ref, mg_ref, h2_ref, wc_v, wa_v, wo_v, sems):
        _load_resident(pl.program_id(0), _mix_pairs(mix_hbm, (wc_v, wa_v, wo_v)), sems)
        a = _dot(yc_ref[...], wc_v[...])
        b = _dot(ysb_ref[...], wa_v[...])
        merged = (_sigmoid(gc_ref[...] + b_ref[:, :D]) * a + _sigmoid(gs_ref[...] + b_ref[:, D:]) * b).astype(BF16)
        a_ref[...] = a
        b_out_ref[...] = b
        mg_ref[...] = merged
        h2_ref[...] = h_ref[...] + _dot(merged, wo_v[...])

    return _call(
        body, (yc, ysb, pcg, pcg, b_gate, h, w_mix), grid=(t // TM,), name=name, comm=comm,
        in_specs=[_row_spec(TM, D), _row_spec(TM, D)] + _gate_specs()
                 + [_const_spec((1, 2 * D)), _row_spec(TM, D), _ANY],
        out_specs=[_row_spec(TM, D)] * 4,
        out_shape=[jax.ShapeDtypeStruct((t, D), F32), jax.ShapeDtypeStruct((t, D), F32),
                   jax.ShapeDtypeStruct((t, D), BF16), jax.ShapeDtypeStruct((t, D), F32)],
        scratch_shapes=[pltpu.VMEM((D, D), BF16)] * 3 + [pltpu.SemaphoreType.DMA((3 * N_DEV,))])


def _mix_out_bwd(dh2, a, b, pcg, b_gate, conv_w, w_mix, name, comm=None):
    t = dh2.shape[0]
    n_tile = t // TM
    per8 = TM // 8

    def rows(n):
        return pl.BlockSpec((TM, n), lambda i: (n_tile - 1 - i, 0))

    def cols(block):
        return pl.BlockSpec((TM, D), lambda i: (n_tile - 1 - i, block))

    def before(block):
        return pl.BlockSpec((8, D), lambda i: (jnp.maximum((n_tile - 1 - i) * per8 - 1, 0), block))

    def body(dh_ref, a_ref, b_ref, gc_ref, gs_ref, cb_ref, cc_ref, cx_ref, ccp_ref, cxp_ref, bias_ref, cw_ref, mix_hbm,
             dhb_ref, da_ref, db_ref, dgp_ref, dc_ref, dysb_ref, dbias_ref, dcw_ref, wc_v, wa_v, wo_v, head_v, sems):
        step = pl.program_id(0)
        _load_resident(step, _mix_pairs(mix_hbm, (wc_v, wa_v, wo_v)), sems)

        @pl.when(step == 0)
        def _():
            head_v[...] = jnp.zeros_like(head_v)
            dcw_ref[...] = jnp.zeros_like(dcw_ref)

        dhb = dh_ref[...].astype(BF16)
        dhb_ref[...] = dhb
        dm = _dot_nt(dhb, wo_v[...])
        gc = _sigmoid(gc_ref[...] + bias_ref[:, :D])
        gs = _sigmoid(gs_ref[...] + bias_ref[:, D:])
        da = (dm * gc).astype(BF16)
        db = (dm * gs).astype(BF16)
        da_ref[...] = da
        db_ref[...] = db
        dgc = dm * a_ref[...] * (gc * (1.0 - gc))
        dgs = dm * b_ref[...] * (gs * (1.0 - gs))
        dgp_ref[0] = dgc.astype(BF16)
        dgp_ref[1] = dgs.astype(BF16)
        _accumulate(dbias_ref.at[:, :D], step, jnp.sum(dgc, axis=0, keepdims=True))
        _accumulate(dbias_ref.at[:, D:], step, jnp.sum(dgs, axis=0, keepdims=True))
        dysb_ref[...] = _dot_nt(db, wa_v[...]).astype(BF16)
        dyc = _dot_nt(da, wc_v[...])
        cc, cx = cc_ref[...], cx_ref[...]
        xc = cc * cx
        xc_before = jnp.where(step == n_tile - 1, 0.0, ccp_ref[...] * cxp_ref[...])
        ext = jnp.concatenate([xc_before, xc], axis=0)
        x1 = pltpu.roll(ext, 1, 0)[8:]
        x2 = pltpu.roll(ext, 2, 0)[8:]
        w0, w1, w2 = cw_ref[0:1, :], cw_ref[1:2, :], cw_ref[2:3, :]
        dc_ref[0] = (dyc * (w0 * x2 + w1 * x1 + w2 * xc)).astype(BF16)
        dconv = dyc * cb_ref[...]
        dcw_ref[0:1, :] += jnp.sum(dconv * x2, axis=0, keepdims=True)
        dcw_ref[1:2, :] += jnp.sum(dconv * x1, axis=0, keepdims=True)
        dcw_ref[2:3, :] += jnp.sum(dconv * xc, axis=0, keepdims=True)
        after = jnp.concatenate([dconv, head_v[...]], axis=0)
        dxc = w2 * dconv + w1 * pltpu.roll(after, TM + 7, 0)[:TM] + w0 * pltpu.roll(after, TM + 6, 0)[:TM]
        dc_ref[1] = (dxc * cx).astype(BF16)
        dc_ref[2] = (dxc * cc).astype(BF16)
        head_v[...] = dconv[:8]

    return _call(
        body, (dh2, a, b, pcg, pcg, pcg, pcg, pcg, pcg, pcg, b_gate, conv_w, w_mix), grid=(n_tile,), name=name,
        comm=comm,
        in_specs=[rows(D)] * 3 + [cols(3), cols(4), cols(0), cols(1), cols(2), before(1), before(2),
                                  _const_spec((1, 2 * D)), _const_spec((CONV_K, D)), _ANY],
        out_specs=[rows(D)] * 3 + [pl.BlockSpec((2, TM, D), lambda i: (0, n_tile - 1 - i, 0)),
                                   pl.BlockSpec((3, TM, D), lambda i: (0, n_tile - 1 - i, 0)), rows(D),
                                   _const_spec((1, 2 * D)), _const_spec((8, D))],
        out_shape=[jax.ShapeDtypeStruct((t, D), BF16)] * 3
                  + [jax.ShapeDtypeStruct((2, t, D), BF16), jax.ShapeDtypeStruct((3, t, D), BF16),
                     jax.ShapeDtypeStruct((t, D), BF16), jax.ShapeDtypeStruct((1, 2 * D), F32),
                     jax.ShapeDtypeStruct((8, D), F32)],
        scratch_shapes=[pltpu.VMEM((D, D), BF16)] * 3 + [pltpu.VMEM((8, D), F32),
                                                         pltpu.SemaphoreType.DMA((3 * N_DEV,))])


def _inproj_bwd(dconv, dq, dkv, dgp, w_in, h, g, dh_res, name, comm=None):
    t = h.shape[0]

    def body(dc_ref, dq_ref, dkv_ref, dgp_ref, w_hbm, h_ref, g_ref, dres_ref, dh_ref, dg_ref, w_v, sems):
        step = pl.program_id(0)
        _load_resident(step, [(w_hbm, w_v)], sems)
        du = _dot_nt(dq_ref[...], w_v[3])
        for k in range(3):
            du = du + _dot_nt(dc_ref[k], w_v[k])
        for k in range(2):
            du = du + _dot_nt(dkv_ref[k].astype(BF16), w_v[4 + k]) + _dot_nt(dgp_ref[k], w_v[6 + k])
        dx, dg = _rms_bwd_tile(h_ref[...], g_ref[...], du)
        dh_ref[...] = dres_ref[...] + dx
        _accumulate(dg_ref, step, dg)

    return _call(
        body, (dconv, dq, dkv, dgp, w_in, h, g, dh_res), grid=(t // TM,), name=name, comm=comm,
        in_specs=[_blk_row_spec(3, TM, D), _row_spec(TM, D), _blk_row_spec(2, TM, D), _blk_row_spec(2, TM, D), _ANY,
                  _row_spec(TM, D), _const_spec((1, D)), _row_spec(TM, D)],
        out_specs=[_row_spec(TM, D), _const_spec((1, D))],
        out_shape=[jax.ShapeDtypeStruct((t, D), F32), jax.ShapeDtypeStruct((1, D), F32)],
        scratch_shapes=[pltpu.VMEM((N_DEV, D, D), BF16), pltpu.SemaphoreType.DMA((1,))])


def _softmax_rows(s):
    e = jnp.exp(s - jnp.max(s, axis=-1, keepdims=True))
    return e / jnp.sum(e, axis=-1, keepdims=True)


def _cross_pairs(cross_hbm, wq_v, wo_v):
    return _square_pairs(cross_hbm, 0, wq_v) + _square_pairs(cross_hbm, 1, wo_v)


def _cross_fwd(h, g, mem, g_mem, w_ckv, w_cross, name):
    t = h.shape[0]
    m = mem.shape[0]
    scale = X_DH ** -0.5

    def body(h_ref, g_ref, mem_ref, gm_ref, wkv_ref, cross_hbm, hn_ref, qx_ref, o_ref, h3_ref, mn_ref, kv_ref,
             wq_v, wo_v, sems):
        _load_resident(pl.program_id(0), _cross_pairs(cross_hbm, wq_v, wo_v), sems)

        @pl.when(pl.program_id(0) == 0)
        def _():
            mn = _rms_fwd_tile(mem_ref[...], gm_ref[...]).astype(BF16)
            mn_ref[...] = mn
            for j in range(N_DEV):
                kv_ref[j] = _dot(mn, wkv_ref[j]).astype(BF16)

        ht = h_ref[...]
        hn = _rms_fwd_tile(ht, g_ref[...]).astype(BF16)
        hn_ref[...] = hn
        qx = _dot(hn, wq_v[...]).astype(BF16)
        qx_ref[...] = qx
        for hd in range(X_H):
            lo, hi = hd * X_DH, (hd + 1) * X_DH
            p = _softmax_rows(_dot_nt(qx[:, lo:hi], kv_ref[hd]) * scale)
            o_ref[:, lo:hi] = _dot(p.astype(BF16), kv_ref[X_H + hd]).astype(BF16)
        h3_ref[...] = ht + _dot(o_ref[...], wo_v[...])

    return pl.pallas_call(
        body, grid=(t // TM,), name=name,
        in_specs=[_row_spec(TM, D), _const_spec((1, D)), _const_spec((m, D)), _const_spec((1, D)),
                  _const_spec((N_DEV, D, X_DH)), _ANY],
        out_specs=[_row_spec(TM, D)] * 4 + [_const_spec((m, D)), _const_spec((N_DEV, m, X_DH))],
        out_shape=[jax.ShapeDtypeStruct((t, D), BF16)] * 3 + [jax.ShapeDtypeStruct((t, D), F32),
                                                              jax.ShapeDtypeStruct((m, D), BF16),
                                                              jax.ShapeDtypeStruct((N_DEV, m, X_DH), BF16)],
        scratch_shapes=[pltpu.VMEM((D, D), BF16)] * 2 + [pltpu.SemaphoreType.DMA((2 * N_DEV,))],
        compiler_params=_cparams(),
    )(h, g, mem, g_mem, w_ckv, w_cross)


def _cross_bwd(dh3, h, g, qx, kv, mem, g_mem, w_ckv, w_cross, name, comm=None):
    t = h.shape[0]
    m = kv.shape[1]
    scale = X_DH ** -0.5

    def body(dh_ref, h_ref, g_ref, qx_ref, kv_ref, mem_ref, gm_ref, wkv_ref, cross_hbm,
             dhb_ref, dqx_ref, dkv_ref, dh2_ref, dg_ref, dgm_ref, wq_v, wo_v, sems):
        step = pl.program_id(0)
        _load_resident(step, _cross_pairs(cross_hbm, wq_v, wo_v), sems)

        @pl.when(step == 0)
        def _():
            dkv_ref[...] = jnp.zeros_like(dkv_ref)

        dht = dh_ref[...]
        dhb = dht.astype(BF16)
        dhb_ref[...] = dhb
        do = _dot_nt(dhb, wo_v[...]).astype(BF16)
        for hd in range(X_H):
            lo, hi = hd * X_DH, (hd + 1) * X_DH
            qh = qx_ref[:, lo:hi]
            kh = kv_ref[hd]
            p = _softmax_rows(_dot_nt(qh, kh) * scale)
            doh = do[:, lo:hi]
            dp = _dot_nt(doh, kv_ref[X_H + hd])
            ds = (p * (dp - jnp.sum(dp * p, axis=-1, keepdims=True)) * scale).astype(BF16)
            dqx_ref[:, lo:hi] = _dot(ds, kh).astype(BF16)
            dkv_ref[hd] += _dot_tn(ds, qh)
            dkv_ref[X_H + hd] += _dot_tn(p.astype(BF16), doh)
        dhn = _dot_nt(dqx_ref[...], wq_v[...])
        dx, dg = _rms_bwd_tile(h_ref[...], g_ref[...], dhn)
        dh2_ref[...] = dht + dx
        _accumulate(dg_ref, step, dg)

        @pl.when(step == t // TM - 1)
        def _():
            dmn = jnp.zeros((m, D), F32)
            for j in range(N_DEV):
                dmn = dmn + _dot_nt(dkv_ref[j].astype(BF16), wkv_ref[j])
            dgm_ref[...] = _rms_bwd_tile(mem_ref[...], gm_ref[...], dmn)[1]

    return _call(
        body, (dh3, h, g, qx, kv, mem, g_mem, w_ckv, w_cross), grid=(t // TM,), name=name, comm=comm,
        in_specs=[_row_spec(TM, D), _row_spec(TM, D), _const_spec((1, D)), _row_spec(TM, D),
                  _const_spec((N_DEV, m, X_DH)), _const_spec((m, D)), _const_spec((1, D)),
                  _const_spec((N_DEV, D, X_DH)), _ANY],
        out_specs=[_row_spec(TM, D), _row_spec(TM, D), _const_spec((N_DEV, m, X_DH)), _row_spec(TM, D),
                   _const_spec((1, D)), _const_spec((1, D))],
        out_shape=[jax.ShapeDtypeStruct((t, D), BF16), jax.ShapeDtypeStruct((t, D), BF16),
                   jax.ShapeDtypeStruct((N_DEV, m, X_DH), F32), jax.ShapeDtypeStruct((t, D), F32),
                   jax.ShapeDtypeStruct((1, D), F32), jax.ShapeDtypeStruct((1, D), F32)],
        scratch_shapes=[pltpu.VMEM((D, D), BF16)] * 2 + [pltpu.SemaphoreType.DMA((2 * N_DEV,))])


def _adamw(w, parts, m, v, name, row_block=0, token=None):
    r, c = w.shape
    n = parts.shape[0]
    tr = _pick_tile(r, (256, 352, 128))
    off = row_block * (r // tr)

    def body(*refs):
        if token is None:
            _adamw_update(None, *refs)
        else:
            _adamw_update(refs[4], *refs[:4], *refs[5:])

    spec = _row_spec(tr, c)
    in_specs = [spec, pl.BlockSpec((n, tr, c), lambda i: (0, i + off, 0)), spec, spec]
    operands = (w, parts, m, v)
    if token is not None:
        in_specs.append(_const_spec(token.shape))
        operands += (token,)
    return pl.pallas_call(
        body, grid=(r // tr,), name=name, in_specs=in_specs, out_specs=[spec] * 4,
        out_shape=[jax.ShapeDtypeStruct((r, c), F32)] * 4,
        compiler_params=_cparams(),
    )(*operands)


def _adamw_update(tok_ref, w_ref, p_ref, m_ref, v_ref, g_ref, d_ref, nm_ref, nv_ref):
    gt = p_ref[0].astype(F32)
    for k in range(1, p_ref.shape[0]):
        gt = gt + p_ref[k].astype(F32)
    if tok_ref is not None:
        gt = gt + tok_ref[0:1, 0:1]
    _adamw_apply(gt, w_ref, m_ref, v_ref, g_ref, d_ref, nm_ref, nv_ref)


def _adamw_own(w, land, own, chip, m, v, name, row_block=0, token=None):
    r, c = w.shape
    tr = _pick_tile(r, (256, 352, 128))
    off = row_block * (r // tr)

    def body(chip_ref, w_ref, land_ref, own_ref, m_ref, v_ref, *rest):
        mine = own_ref[0].astype(F32)
        gt = jnp.where(chip_ref[0] == 0, mine, land_ref[0].astype(F32))
        for k in range(1, N_CHIP):
            gt = gt + jnp.where(chip_ref[0] == k, mine, land_ref[k].astype(F32))
        if token is None:
            _adamw_apply(gt, w_ref, m_ref, v_ref, *rest)
        else:
            before, *outs, later = rest
            _adamw_apply(gt + before[0:1, 0:1], w_ref, m_ref, v_ref, *outs)
            later[...] = before[...]

    spec = pl.BlockSpec((tr, c), lambda i, chip_ref: (i, 0))
    in_specs = [spec, pl.BlockSpec((N_CHIP, tr, c), lambda i, chip_ref: (0, i + off, 0)),
                pl.BlockSpec((1, tr, c), lambda i, chip_ref: (chip_ref[0], i + off, 0)), spec, spec]
    operands = (chip, w, land, own, m, v)
    out_specs, out_shape = [spec] * 4, [jax.ShapeDtypeStruct((r, c), F32)] * 4
    if token is not None:
        token_spec = pl.BlockSpec(token.shape, lambda i, chip_ref: (0, 0))
        in_specs.append(token_spec)
        operands += (token,)
        out_specs, out_shape = out_specs + [token_spec], out_shape + [jax.ShapeDtypeStruct(token.shape, token.dtype)]
    return pl.pallas_call(
        body, name=name,
        grid_spec=pltpu.PrefetchScalarGridSpec(
            num_scalar_prefetch=1, grid=(r // tr,), in_specs=in_specs, out_specs=out_specs),
        out_shape=out_shape,
        compiler_params=_cparams(),
    )(*operands)


def _adamw_apply(gt, w_ref, m_ref, v_ref, g_ref, d_ref, nm_ref, nv_ref):
    g_ref[...] = gt
    nm = ADAM_B1 * m_ref[...] + (1.0 - ADAM_B1) * gt
    nv = ADAM_B2 * v_ref[...] + (1.0 - ADAM_B2) * jnp.square(gt)
    m_hat = nm / (1.0 - ADAM_B1 ** ADAM_STEP)
    v_hat = nv / (1.0 - ADAM_B2 ** ADAM_STEP)
    d_ref[...] = -ADAM_LR * (m_hat / (jnp.sqrt(v_hat) + ADAM_EPS) + ADAM_WD * w_ref[...])
    nm_ref[...] = nm
    nv_ref[...] = nv


def _mesh_pos():
    return lax.axis_index("x"), lax.axis_index("y"), lax.axis_index("c")


def _both(first, second):
    n_in, n_out, n_sem = len(first.inputs), len(first.out_shapes), len(first.sem_shapes)

    def run(round_name):
        def both(in_refs, out_refs, sems):
            getattr(first, round_name)(in_refs[:n_in], out_refs[:n_out], sems[:n_sem])
            getattr(second, round_name)(in_refs[n_in:], out_refs[n_out:], sems[n_sem:])
        return both

    return types.SimpleNamespace(
        inputs=first.inputs + second.inputs, out_shapes=first.out_shapes + second.out_shapes,
        sem_shapes=first.sem_shapes + second.sem_shapes, start=run("start"), middle=run("middle"),
        finish=run("finish"))


def _no_round(in_refs, out_refs, sems):
    pass


def _after(token):
    return types.SimpleNamespace(inputs=[token], out_shapes=[], sem_shapes=[], start=_no_round, middle=_no_round,
                                 finish=_no_round)


def _run_exchange(comm, name):
    c_in, c_out = len(comm.inputs), len(comm.out_shapes)

    def body(*refs):
        cins, couts, sems = refs[:c_in], refs[c_in:c_in + c_out], refs[c_in + c_out:]
        comm.start(cins, couts, sems)
        comm.middle(cins, couts, sems)
        comm.finish(cins, couts, sems)

    return list(pl.pallas_call(
        body, name=name, out_shape=list(comm.out_shapes),
        in_specs=[_ANY] * c_in, out_specs=[_ANY] * c_out, scratch_shapes=list(comm.sem_shapes),
    )(*comm.inputs))


def _gather_exchange(shards):
    n_arr = len(shards)

    def plan(x_refs, out_refs, sems):
        send_sems, recv_sems, local_sems = sems[:3]
        stage = sems[3:]
        x, y, c = _mesh_pos()
        me, sibling = (x, y, c), (x, y, 1 - c)
        xn, yn, diag = (1 - x, y), (x, 1 - y), (1 - x, 1 - y)

        def slot(a, px, py, pc, half=None):
            ref = out_refs[a].at[4 * px + 2 * py + pc]
            if half is None:
                return ref
            rows = shards[a].shape[0] // 2
            return ref.at[half * rows:(half + 1) * rows]

        def copy(a, k, block, to, half=None, src=None):
            dst = slot(a, *block, half)
            return pltpu.make_async_remote_copy(
                src_ref=dst if src is None else src, dst_ref=dst,
                send_sem=send_sems.at[a, k], recv_sem=recv_sems.at[a, k],
                device_id=to, device_id_type=pl.DeviceIdType.MESH)

        return types.SimpleNamespace(
            me=me, sibling=sibling, xn=xn, yn=yn, diag=diag, c=c, copy=copy,
            mine_in=[pltpu.make_async_copy(x_refs[a], stage[a], local_sems.at[a, 0]) for a in range(n_arr)],
            mine_out=[pltpu.make_async_copy(stage[a], slot(a, *me), local_sems.at[a, 1]) for a in range(n_arr)],
            first=[cp for a in range(n_arr) for cp in (
                copy(a, 0, me, sibling, src=x_refs[a]), copy(a, 1, me, (*xn, c), src=x_refs[a]),
                copy(a, 2, me, (*yn, c), src=x_refs[a]))],
            second=lambda a: (copy(a, 3, (*xn, c), (*yn, c), half=0), copy(a, 5, (*xn, c), sibling),
                              copy(a, 4, (*yn, c), (*xn, c), half=1), copy(a, 6, (*yn, c), sibling)),
            third=lambda a: (copy(a, 7, (*diag, c), sibling, half=0), copy(a, 8, (*diag, c), sibling, half=1)))

    def start(x_refs, out_refs, sems):
        p = plan(x_refs, out_refs, sems)
        for cp in p.first + p.mine_in:
            cp.start()
        for cp_in, cp_out in zip(p.mine_in, p.mine_out):
            cp_in.wait()
            cp_out.start()

    def middle(x_refs, out_refs, sems):
        p = plan(x_refs, out_refs, sems)
        for a in range(n_arr):
            to_yn, x_to_sib, to_xn, y_to_sib = p.second(a)
            p.copy(a, 1, (*p.xn, p.c), p.me).wait_recv()
            to_yn.start()
            x_to_sib.start()
            p.copy(a, 2, (*p.yn, p.c), p.me).wait_recv()
            to_xn.start()
            y_to_sib.start()

    def finish(x_refs, out_refs, sems):
        p = plan(x_refs, out_refs, sems)
        for a in range(n_arr):
            half0_to_sib, half1_to_sib = p.third(a)
            p.copy(a, 3, (*p.diag, p.c), p.me, half=0).wait_recv()
            half0_to_sib.start()
            p.copy(a, 4, (*p.diag, p.c), p.me, half=1).wait_recv()
            half1_to_sib.start()
        other = 1 - p.c
        for a in range(n_arr):
            p.copy(a, 0, p.sibling, p.me).wait_recv()
            p.copy(a, 5, (*p.xn, other), p.me).wait_recv()
            p.copy(a, 6, (*p.yn, other), p.me).wait_recv()
            p.copy(a, 7, (*p.diag, other), p.me, half=0).wait_recv()
            p.copy(a, 8, (*p.diag, other), p.me, half=1).wait_recv()
        for cp in p.first:
            cp.wait_send()
        for a in range(n_arr):
            for cp in p.second(a) + p.third(a):
                cp.wait_send()
        for cp in p.mine_out:
            cp.wait()

    return types.SimpleNamespace(
        inputs=list(shards), start=start, middle=middle, finish=finish,
        out_shapes=[jax.ShapeDtypeStruct((N_DEV,) + s.shape, s.dtype) for s in shards],
        sem_shapes=[pltpu.SemaphoreType.DMA((n_arr, 9)), pltpu.SemaphoreType.DMA((n_arr, 9)),
                    pltpu.SemaphoreType.DMA((n_arr, 2))] + [pltpu.VMEM(s.shape, s.dtype) for s in shards])


def _pair_exchange(grads):
    n_arr = len(grads)

    def plan(g_refs, land_refs, sems):
        send_sems, recv_sems = sems
        x, y, c = _mesh_pos()
        return [pltpu.make_async_remote_copy(
            src_ref=g_refs[a].at[2 * k + 1 - c], dst_ref=land_refs[a].at[k],
            send_sem=send_sems.at[a, k], recv_sem=recv_sems.at[a, k],
            device_id=(x, y, 1 - c), device_id_type=pl.DeviceIdType.MESH)
            for a in range(n_arr) for k in range(N_CHIP)]

    def start(g_refs, land_refs, sems):
        for cp in plan(g_refs, land_refs, sems):
            cp.start()

    def finish(g_refs, land_refs, sems):
        for cp in plan(g_refs, land_refs, sems):
            cp.wait()

    return types.SimpleNamespace(
        inputs=list(grads), start=start, middle=_no_round, finish=finish,
        out_shapes=[jax.ShapeDtypeStruct((N_CHIP,) + g.shape[1:], g.dtype) for g in grads],
        sem_shapes=[pltpu.SemaphoreType.DMA((n_arr, N_CHIP)), pltpu.SemaphoreType.DMA((n_arr, N_CHIP))])


def _chip_exchange(parts):
    n_arr = len(parts)

    def plan(p_refs, land_refs, sems):
        send_sems, recv_sems, local_sems = sems
        x, y, c = _mesh_pos()
        my_chip = 2 * x + y
        chips = [(1 - x, y), (x, 1 - y), (1 - x, 1 - y)]
        local = [pltpu.make_async_copy(p_refs[a].at[my_chip], land_refs[a].at[my_chip], local_sems.at[a])
                 for a in range(n_arr)]

        def copy(a, k, src_slot, dst_slot, px, py):
            return pltpu.make_async_remote_copy(
                src_ref=p_refs[a].at[src_slot], dst_ref=land_refs[a].at[dst_slot],
                send_sem=send_sems.at[a, k], recv_sem=recv_sems.at[a, k],
                device_id=(px, py, c), device_id_type=pl.DeviceIdType.MESH)

        sends = [copy(a, k, 2 * px + py, my_chip, px, py) for a in range(n_arr) for k, (px, py) in enumerate(chips)]
        arrivals = [copy(a, k, my_chip, 2 * px + py, px, py) for a in range(n_arr)
                    for k, (px, py) in enumerate(chips)]
        return local, sends, arrivals

    def start(p_refs, land_refs, sems):
        local, sends, _ = plan(p_refs, land_refs, sems)
        for cp in local + sends:
            cp.start()

    def finish(p_refs, land_refs, sems):
        local, sends, arrivals = plan(p_refs, land_refs, sems)
        for cp in arrivals:
            cp.wait_recv()
        for cp in sends:
            cp.wait_send()
        for cp in local:
            cp.wait()

    return types.SimpleNamespace(
        inputs=list(parts), start=start, middle=_no_round, finish=finish,
        out_shapes=[jax.ShapeDtypeStruct(p.shape, p.dtype) for p in parts],
        sem_shapes=[pltpu.SemaphoreType.DMA((n_arr, 3)), pltpu.SemaphoreType.DMA((n_arr, 3)),
                    pltpu.SemaphoreType.DMA((n_arr,))])


_HBM = pl.BlockSpec(memory_space=pltpu.HBM)
_SEM = pl.BlockSpec(memory_space=pltpu.SEMAPHORE)
_DATAFLOW = pltpu.SideEffectType.DATAFLOW_SIDE_EFFECTING


def _chip_copies(p_refs, land_refs, send_sems, recv_sems):
    x, y, c = _mesh_pos()
    my_chip = 2 * x + y
    chips = [(1 - x, y), (x, 1 - y), (1 - x, 1 - y)]
    return [pltpu.make_async_remote_copy(
        src_ref=p_refs[a].at[2 * px + py], dst_ref=land_refs[a].at[my_chip],
        send_sem=send_sems[3 * a + k], recv_sem=recv_sems[3 * a + k],
        device_id=(px, py, c), device_id_type=pl.DeviceIdType.MESH)
        for a in range(len(p_refs)) for k, (px, py) in enumerate(chips)]


def _chip_exchange_begin(parts, name):
    n_arr = len(parts)
    n_buf, n_copy = 2 * n_arr, 3 * n_arr
    lands = [lax.empty(p.shape, p.dtype) for p in parts]

    def body(*refs):
        p_refs, land_refs = refs[:n_arr], refs[n_arr:n_buf]
        send_sems, recv_sems, token = refs[n_buf:n_buf + n_copy], refs[n_buf + n_copy:n_buf + 2 * n_copy], refs[-1]
        for cp in _chip_copies(p_refs, land_refs, send_sems, recv_sems):
            cp.start()
        token[...] = jnp.zeros_like(token)

    bufs = list(parts) + list(lands)
    outs = pl.pallas_call(
        body, name=name,
        out_shape=(*[pltpu.SemaphoreType.DMA(())] * (2 * n_copy), *[pltpu.HBM(b.shape, b.dtype) for b in bufs],
                   jax.ShapeDtypeStruct((8, 128), F32)),
        in_specs=[_HBM] * n_buf,
        out_specs=(*[_SEM] * (2 * n_copy), *[_HBM] * n_buf, pl.BlockSpec(memory_space=pltpu.VMEM)),
        input_output_aliases={i: 2 * n_copy + i for i in range(n_buf)},
        compiler_params=pltpu.CompilerParams(has_side_effects=_DATAFLOW),
    )(*[pltpu.with_memory_space_constraint(b, pltpu.HBM) for b in bufs])
    sems = list(outs[:2 * n_copy])
    thru = list(outs[2 * n_copy:2 * n_copy + n_buf])
    return types.SimpleNamespace(send_sems=sems[:n_copy], recv_sems=sems[n_copy:], parts=thru[:n_arr],
                                 lands=thru[n_arr:], token=outs[-1])


def _chip_exchange_end(flight, after, name):
    send_sems, recv_sems, parts, lands = flight.send_sems, flight.recv_sems, flight.parts, flight.lands
    n_arr = len(parts)
    n_buf, n_copy = 2 * n_arr, 3 * n_arr

    def body(*refs):
        p_refs, land_refs = refs[:n_arr], refs[n_arr:n_buf]
        sems = refs[n_buf:n_buf + 2 * n_copy]
        for cp in _chip_copies(p_refs, land_refs, sems[:n_copy], sems[n_copy:]):
            cp.wait_send()
            cp.wait_recv()

    bufs = list(parts) + list(lands)
    outs = pl.pallas_call(
        body, name=name, out_shape=tuple(pltpu.HBM(b.shape, b.dtype) for b in bufs),
        in_specs=[_HBM] * n_buf + [_SEM] * (2 * n_copy) + [_ANY], out_specs=tuple([_HBM] * n_buf),
        input_output_aliases={i: i for i in range(n_buf)},
        compiler_params=pltpu.CompilerParams(has_side_effects=_DATAFLOW),
    )(*bufs, *send_sems, *recv_sems, after)
    return list(outs[:n_arr]), list(outs[n_arr:])


def _row_tile(r, cap=640):
    best = None
    for cand in range(16, min(r, cap) + 1, 16):
        if r % cand == 0:
            best = cand
    return best if best is not None else r


def _pair_sum(gs, landeds, core, name):
    tiles = [_row_tile(g.shape[1]) for g in gs]
    counts = [g.shape[1] // tr for g, tr in zip(gs, tiles)]
    n_arr = len(gs)

    def body(core_ref, *refs):
        for a in range(n_arr):
            mine, theirs, out = refs[2 * a], refs[2 * a + 1], refs[2 * n_arr + a]
            out[0] = (mine[0].astype(F32) + theirs[0].astype(F32)).astype(out.dtype)

    in_specs, out_specs, operands = [], [], []
    for g, landed, tr, count in zip(gs, landeds, tiles, counts):
        c_dim = g.shape[2]
        last = count - 1
        in_specs += [pl.BlockSpec((1, tr, c_dim),
                                  lambda k, i, core_ref, last=last: (2 * k + core_ref[0], jnp.minimum(i, last), 0)),
                     pl.BlockSpec((1, tr, c_dim), lambda k, i, core_ref, last=last: (k, jnp.minimum(i, last), 0))]
        out_specs.append(pl.BlockSpec((1, tr, c_dim), lambda k, i, core_ref, last=last: (k, jnp.minimum(i, last), 0)))
        operands += [g, landed]
    return list(pl.pallas_call(
        body, name=name,
        grid_spec=pltpu.PrefetchScalarGridSpec(
            num_scalar_prefetch=1, grid=(N_CHIP, max(counts)), in_specs=in_specs, out_specs=out_specs),
        out_shape=[jax.ShapeDtypeStruct((N_CHIP,) + g.shape[1:], g.dtype) for g in gs],
        compiler_params=_cparams(2),
    )(core, *operands))


def _sum_slots(parts, name):
    n, r, c_dim = parts.shape
    tr = _row_tile(r)

    def body(p_ref, o_ref):
        acc = p_ref[0].astype(F32)
        for k in range(1, n):
            acc = acc + p_ref[k].astype(F32)
        o_ref[...] = acc

    return pl.pallas_call(
        body, grid=(r // tr,), name=name,
        in_specs=[pl.BlockSpec((n, tr, c_dim), lambda i: (0, i, 0))],
        out_specs=_row_spec(tr, c_dim),
        out_shape=jax.ShapeDtypeStruct((r, c_dim), F32),
        compiler_params=_cparams(),
    )(parts)


GAINS = ("g_ffn1", "g_mix", "g_cross", "g_mem", "g_ffn2", "g_final")
SMALL = GAINS + ("b_gate", "conv_w")
SMALL_R = 16
LOSS_ROW = 11
WEIGHT_ORDER = ("g_ffn1", "w_ffn1_gu", "w_ffn1_down", "g_mix", "w_in", "b_gate", "conv_w", "w_conv_out",
                "w_attn_out", "w_o", "g_cross", "g_mem", "w_cq", "w_ckv", "w_co", "g_ffn2", "w_ffn2_gu",
                "w_ffn2_down", "g_final")
GU_NAMES = ("w_ffn1_gu", "w_ffn2_gu")


def _pack_small(vals, conv_rows):
    rows = [vals[n].reshape(1, D) for n in GAINS] + [vals["b_gate"].reshape(2, D), conv_rows.reshape(CONV_K, D)]
    used = len(GAINS) + 2 + CONV_K
    return jnp.concatenate(rows + [jnp.zeros((SMALL_R - used, D), F32)], axis=0)


def _unpack_small(buf):
    out = {n: buf[k] for k, n in enumerate(GAINS)}
    out["b_gate"] = buf[6:8].reshape(2 * D)
    out["conv_w"] = buf[8:8 + CONV_K]
    return out


def _exchange_shards(wts):
    out = {n: jnp.pad(wts[n].T.astype(BF16), ((0, FF_PAD - FF_BLK), (0, 0))) for n in GU_NAMES}
    for n in ("w_ckv", "w_in", "w_ffn1_down", "w_ffn2_down"):
        out[n] = wts[n].astype(BF16)
    out["mix"] = jnp.concatenate([wts[n].astype(BF16) for n in MIX_MATS], axis=0)
    out["cross"] = jnp.concatenate([wts[n].astype(BF16) for n in CROSS_MATS], axis=0)
    return out


def _reduce_group(grads, landed, core, names):
    return _pair_sum(grads, landed, core, "grads_pair_sum_" + "_".join(names))


def _step(x, mem, target, sh, conv_pad, gains, b_gate, core):
    wg1, wd1, conv_all = _run_exchange(_gather_exchange([sh["w_ffn1_gu"], sh["w_ffn1_down"], conv_pad]), "gather_ffn1")
    conv_w = conv_all[:, :CONV_K, :].transpose(1, 0, 2).reshape(CONV_K, D)
    (n1, gate1, up1, act1, h1), (w_in,) = _ffn_fwd(
        x, gains["g_ffn1"], wg1, wd1, "ffn1_fwd", comm=_gather_exchange([sh["w_in"]]))
    (u, pcg, qkv, yc), (w_mix, wd2) = _inproj_fwd(h1, gains["g_mix"], w_in, conv_w, "inproj_fwd",
                                                  comm=_gather_exchange([sh["mix"], sh["w_ffn2_down"]]))
    (ysb, ctot), (w_cross, w_ckv, wg2) = _sb_fwd(
        qkv, "sb_fwd", comm=_gather_exchange([sh["cross"], sh["w_ckv"], sh["w_ffn2_gu"]]))
    (a_mix, b_mix, merged, h2), _ = _mix_out_fwd(yc, ysb, pcg, b_gate, h1, w_mix, "mix_out_fwd")
    hn, qx, o_x, h3, mn, kv = _cross_fwd(h2, gains["g_cross"], mem, gains["g_mem"], w_ckv, w_cross, "cross_fwd")
    (n4, gate2, up2, act2, dh4, loss, dg_final), _ = _ffn_fwd(h3, gains["g_ffn2"], wg2, wd2, "ffn2_fwd",
                                                              head=(gains["g_final"], target))

    gs = {"g_final": dg_final}
    (dgu2, dh4b, dh3, gs["g_ffn2"]), _ = _ffn_bwd(dh4, h3, gains["g_ffn2"], gate2, up2, wg2, wd2, "ffn2_bwd")
    grads_a = [_mm_tn_rows(dgu2, n4, FF_PAD, "dw_ffn2_gu"),
               _mm_tn_rows(act2, dh4b, FF_BLK, "dw_ffn2_down").reshape(N_DEV, DOWN_ROWS, D)]
    names_a = ["w_ffn2_gu", "w_ffn2_down"]
    (dh3b, dqx, dkv, dh2, gs["g_cross"], gs["g_mem"]), _ = _cross_bwd(
        dh3, h2, gains["g_cross"], qx, kv, mem, gains["g_mem"], w_ckv, w_cross, "cross_bwd")
    grads_b = [_mm_tn_cols(mn, dkv, "dw_ckv"), _mm_tn_squares([(hn, dqx), (o_x, dh3b)], "dw_cross")]
    names_b = ["w_ckv", "cross"]
    (dh2b, da_mix, db_mix, dgp, dconv, dysb, gs["b_gate"], gs["conv_w"]), landed_ab = _mix_out_bwd(
        dh2, a_mix, b_mix, pcg, b_gate, conv_w, w_mix, "mix_out_bwd", comm=_pair_exchange(grads_a + grads_b))
    sums_ab = _reduce_group(grads_a + grads_b, landed_ab, core, names_a + names_b)
    grads_c = [_mm_tn_squares([(yc, da_mix), (ysb, db_mix), (merged, dh2b)], "dw_mix")]
    flight_ab = _chip_exchange_begin(sums_ab, "grads_to_chips_early_begin")
    (dq, dkv_sb), _ = _sb_bwd(qkv, dysb, ctot, flight_ab.token, "sb_bwd")
    grads_d = [_mm_tn_cols_many(u, [dconv, dq[None], dkv_sb, dgp], "dw_in")]
    (dh1, gs["g_mix"]), landed_cd = _inproj_bwd(dconv, dq, dkv_sb, dgp, w_in, h1, gains["g_mix"], dh2, "inproj_bwd",
                                                comm=_pair_exchange(grads_c + grads_d))
    sums_cd = _reduce_group(grads_c + grads_d, landed_cd, core, ["mix", "w_in"])
    flight_d = _chip_exchange_begin(sums_cd, "grads_to_chips_w_in_begin")
    (dgu1, dh1b, dx, gs["g_ffn1"]), _ = _ffn_bwd(dh1, x, gains["g_ffn1"] + flight_d.token[0, 0], gate1, up1, wg1, wd1,
                                                 "ffn1_bwd")
    dw_gu1 = _mm_tn_rows(dgu1, n1, FF_PAD, "dw_ffn1_gu")
    small_mine = _pack_small({n: gs[n] for n in GAINS + ("b_gate",)}, gs["conv_w"][:CONV_K])
    small_mine = small_mine.at[LOSS_ROW, 0].set(loss[0, 0])
    dw_down1, (landed_gu1, small_all) = _mm_tn_rows(
        act1, dh1b, FF_BLK, "dw_ffn1_down", comm=_both(_pair_exchange([dw_gu1]), _gather_exchange([small_mine])))
    flight_gu1 = _chip_exchange_begin(_reduce_group([dw_gu1], [landed_gu1], core, ["w_ffn1_gu"]),
                                      "grads_to_chips_ffn1_gu_begin")
    grads_down1 = [dw_down1.reshape(N_DEV, DOWN_ROWS, D)]
    landed_down1 = _run_exchange(_both(_pair_exchange(grads_down1), _after(flight_gu1.token)),
                                 "grads_to_sibling_ffn1_down")
    flight_down1 = _chip_exchange_begin(_reduce_group(grads_down1, landed_down1, core, ["w_ffn1_down"]),
                                        "grads_to_chips_ffn1_down_begin")
    flights = [(names_a + names_b, flight_ab), (["mix", "w_in"], flight_d), (["w_ffn1_gu"], flight_gu1),
               (["w_ffn1_down"], flight_down1)]
    return dx, flights, small_all


def kernel(x, mem, g_ffn1, w_ffn1_gu, w_ffn1_down, g_mix, w_in, b_gate, conv_w, w_conv_out, w_attn_out, w_o, g_cross, g_mem, w_cq, w_ckv, w_co, g_ffn2, w_ffn2_gu, w_ffn2_down, g_final, loss_target, m_g_ffn1, m_w_ffn1_gu, m_w_ffn1_down, m_g_mix, m_w_in, m_b_gate, m_conv_w, m_w_conv_out, m_w_attn_out, m_w_o, m_g_cross, m_g_mem, m_w_cq, m_w_ckv, m_w_co, m_g_ffn2, m_w_ffn2_gu, m_w_ffn2_down, m_g_final, v_g_ffn1, v_w_ffn1_gu, v_w_ffn1_down, v_g_mix, v_w_in, v_b_gate, v_conv_w, v_w_conv_out, v_w_attn_out, v_w_o, v_g_cross, v_g_mem, v_w_cq, v_w_ckv, v_w_co, v_g_ffn2, v_w_ffn2_gu, v_w_ffn2_down, v_g_final):
    args = locals()
    wts = {n: args[n] for n in WEIGHT_ORDER}
    mom1 = {n: args["m_" + n] for n in WEIGHT_ORDER}
    mom2 = {n: args["v_" + n] for n in WEIGHT_ORDER}
    cx, cy, cc = _mesh_pos()
    dev = 4 * cx + 2 * cy + cc
    conv_cols = D // N_DEV

    conv_pad = jnp.concatenate([conv_w, jnp.zeros((SMALL_R - CONV_K, conv_cols), F32)], axis=0)
    gains = {n: wts[n].reshape(1, D) for n in GAINS}
    dx, flights, small_all = _step(x[0], mem[0], loss_target[0], _exchange_shards(wts), conv_pad, gains,
                                 b_gate.reshape(1, 2 * D), cc.reshape(1).astype(jnp.int32))

    grads, delta, new_m, new_v = {}, {}, {}, {}

    def operands(n, transposed):
        trio = (wts[n], mom1[n], mom2[n])
        return tuple(a.T for a in trio) if transposed else trio

    def record(n, res, transposed):
        grads[n], delta[n], new_m[n], new_v[n] = [r.T for r in res] if transposed else res

    early = [("w_ffn2_gu", "w_ffn2_gu", 0, True), ("w_ffn2_down", "w_ffn2_down", 0, False),
             ("w_ckv", "w_ckv", 0, False), ("w_in", "w_in", 0, False)]
    early += [(n, "mix", k, False) for k, n in enumerate(MIX_MATS)]
    early += [(n, "cross", k, False) for k, n in enumerate(CROSS_MATS)]
    chip = (2 * cx + cy).reshape(1).astype(jnp.int32)
    (names_early, flight_early), (names_w_in, flight_w_in), *last_flights = flights
    token = last_flights[-1][1].token
    own, land = {}, {}
    for names, flight, tag in ((names_early, flight_early, "early"), (names_w_in, flight_w_in, "w_in")):
        own_parts, landed = _chip_exchange_end(flight, token, "grads_to_chips_%s_end" % tag)
        own.update(zip(names, own_parts))
        land.update(zip(names, landed))
    for n, buf, row_block, transposed in early:
        w, m1, m2 = operands(n, transposed)
        *res, token = _adamw_own(w, land[buf], own[buf], chip, m1, m2, "adamw_" + n, row_block, token)
        record(n, res, transposed)

    for ((n,), flight), transposed in zip(last_flights, (True, False)):
        (own_n,), (land_n,) = _chip_exchange_end(flight, token, "grads_to_chips_%s_end" % n)
        w, m1, m2 = operands(n, transposed)
        *res, token = _adamw_own(w, land_n, own_n, chip, m1, m2, "adamw_" + n, token=token)
        record(n, res, transposed)

    small_sum = _sum_slots(small_all, "small_grads_sum")
    loss = small_sum[LOSS_ROW, 0]
    grad_small = _unpack_small(small_sum)
    grad_small["conv_w"] = lax.dynamic_slice_in_dim(grad_small["conv_w"], dev * conv_cols, conv_cols, axis=1)
    grads.update(grad_small)

    def small_buf(vals):
        return _pack_small(vals, jnp.concatenate([vals["conv_w"], jnp.zeros((CONV_K, D - conv_cols), F32)], axis=1))

    _, d_s, m_s, v_s = _adamw(small_buf(wts), small_buf(grads)[None], small_buf(mom1), small_buf(mom2), "adamw_small")
    for res, buf in ((delta, d_s), (new_m, m_s), (new_v, v_s)):
        un = _unpack_small(buf)
        for n in GAINS + ("b_gate",):
            res[n] = un[n]
        res["conv_w"] = un["conv_w"][:, :conv_cols]

    return (loss, dx[None], *[grads[n] for n in WEIGHT_ORDER], *[delta[n] for n in WEIGHT_ORDER],
            *[new_m[n] for n in WEIGHT_ORDER], *[new_v[n] for n in WEIGHT_ORDER])
```

```python
import types

import jax
import jax.numpy as jnp
from jax import lax
from jax.experimental import pallas as pl
from jax.experimental.pallas import tpu as pltpu

F32 = jnp.float32
BF16 = jnp.bfloat16

D = 1024
DFF = 2816
SB_H = 8
SB_DH = 128
X_H = 4
X_DH = 256
CONV_K = 3
RMS_EPS = 1e-6
N_DEV = 8
N_CHIP = 4
SQ_ROWS = D // N_DEV

ADAM_LR = 0.001
ADAM_B1 = 0.9
ADAM_B2 = 0.999
ADAM_EPS = 1e-08
ADAM_WD = 0.01
ADAM_STEP = 10

TM = 256
TQ = 512
TK = 256
SB_HPS = 2
VMEM_LIMIT = 56 << 20

FF_BLK = DFF // 4
FF_PAD = 768
FF_SUB = 256
DOWN_ROWS = DFF // N_DEV

MIX_MATS = ("w_conv_out", "w_attn_out", "w_o")
CROSS_MATS = ("w_cq", "w_co")

_ANY = pl.BlockSpec(memory_space=pl.ANY)


def _cparams(n_axes=1):
    return pltpu.CompilerParams(
        dimension_semantics=("arbitrary",) * n_axes, vmem_limit_bytes=VMEM_LIMIT)


def _row_spec(tm, n):
    return pl.BlockSpec((tm, n), lambda i: (i, 0))


def _blk_row_spec(nb, tm, n):
    return pl.BlockSpec((nb, tm, n), lambda i: (0, i, 0))


def _const_spec(shape):
    zeros = (0,) * len(shape)
    return pl.BlockSpec(shape, lambda i: zeros)


def _dot(a, b):
    return jnp.dot(a, b, preferred_element_type=F32)


def _dot_nt(a, b):
    return lax.dot_general(a, b, (((1,), (1,)), ((), ())), preferred_element_type=F32)


def _dot_tn(a, b):
    return lax.dot_general(a, b, (((0,), (0,)), ((), ())), preferred_element_type=F32)


def _sigmoid(x):
    return 1.0 / (1.0 + jnp.exp(-x))


def _call(body, operands, *, grid, in_specs, out_specs, out_shape, scratch_shapes, name, comm=None):
    n_in, n_out, n_sc = len(in_specs), len(out_specs), len(scratch_shapes)
    if comm is None:
        outs = pl.pallas_call(
            body, grid=grid, name=name, in_specs=in_specs, out_specs=out_specs, out_shape=out_shape,
            scratch_shapes=scratch_shapes, compiler_params=_cparams(len(grid)))(*operands)
        return list(outs), []
    c_in, c_out, c_sem = len(comm.inputs), len(comm.out_shapes), len(comm.sem_shapes)

    def hosted(*refs):
        bounds = [0, n_in, c_in, n_out, c_out, n_sc, c_sem]
        parts, pos = [], 0
        for k in bounds[1:]:
            parts.append(refs[pos:pos + k])
            pos += k
        ins, cins, outs, couts, scr, sems = parts
        step, n_steps = pl.program_id(0), grid[0]
        for ax in range(1, len(grid)):
            step, n_steps = step * grid[ax] + pl.program_id(ax), n_steps * grid[ax]

        @pl.when(step == 0)
        def _():
            comm.start(cins, couts, sems)

        @pl.when(step == (2 * n_steps) // 3)
        def _():
            comm.middle(cins, couts, sems)

        body(*ins, *outs, *scr)

        @pl.when(step == n_steps - 1)
        def _():
            comm.finish(cins, couts, sems)

    res = pl.pallas_call(
        hosted, grid=grid, name=name, in_specs=list(in_specs) + [_ANY] * c_in,
        out_specs=list(out_specs) + [_ANY] * c_out, out_shape=list(out_shape) + list(comm.out_shapes),
        scratch_shapes=list(scratch_shapes) + list(comm.sem_shapes),
        compiler_params=_cparams(len(grid)))(*operands, *comm.inputs)
    return list(res[:n_out]), list(res[n_out:])


def _load_resident(step, pairs, sems):
    @pl.when(step == 0)
    def _():
        copies = [pltpu.make_async_copy(src, dst, sems.at[k]) for k, (src, dst) in enumerate(pairs)]
        for cp in copies:
            cp.start()
        for cp in copies:
            cp.wait()


def _square_pairs(buf_hbm, index, dst):
    off = index * SQ_ROWS
    return [(buf_hbm.at[d, off:off + SQ_ROWS, :], dst.at[d * SQ_ROWS:(d + 1) * SQ_ROWS, :]) for d in range(N_DEV)]


def _down_pairs(wd_hbm, dst):
    return [(wd_hbm.at[d], dst.at[d // 2, (d % 2) * DOWN_ROWS:(d % 2 + 1) * DOWN_ROWS, :]) for d in range(N_DEV)]


def _zero_down_pad(step, dst):
    @pl.when(step == 0)
    def _():
        dst[:, FF_BLK:, :] = jnp.zeros((4, FF_PAD - FF_BLK, D), BF16)


def _rms_fwd_tile(xt, g):
    r = lax.rsqrt(jnp.mean(xt * xt, axis=-1, keepdims=True) + RMS_EPS)
    return (xt * r) * g


def _rms_bwd_tile(xt, g, dn):
    r = lax.rsqrt(jnp.mean(xt * xt, axis=-1, keepdims=True) + RMS_EPS)
    xhat = xt * r
    dxhat = dn * g
    dx = r * (dxhat - xhat * jnp.mean(dxhat * xhat, axis=-1, keepdims=True))
    dg = jnp.sum(dn * xhat, axis=0, keepdims=True)
    return dx, dg


def _accumulate(ref, step, value):
    @pl.when(step == 0)
    def _():
        ref[...] = value

    @pl.when(step != 0)
    def _():
        ref[...] = ref[...] + value


def _ffn_fwd(x, g, wgu, wd, name, comm=None, head=None):
    t = x.shape[0]

    def body(x_ref, g_ref, wgu_hbm, wd_hbm, *refs):
        if head is None:
            n_ref, gate_ref, up_ref, act_ref, h_ref, wgu_v, wd_v, sems = refs
        else:
            gf_ref, t_ref, n_ref, gate_ref, up_ref, act_ref, dh_ref, loss_ref, dgf_ref, wgu_v, wd_v, sems = refs
        step = pl.program_id(0)
        _zero_down_pad(step, wd_v)
        _load_resident(step, [(wgu_hbm, wgu_v)] + _down_pairs(wd_hbm, wd_v), sems)
        xt = x_ref[...]
        n = _rms_fwd_tile(xt, g_ref[...]).astype(BF16)
        n_ref[...] = n
        acc = jnp.zeros((TM, D), F32)
        for j in range(4):
            for s in range(FF_PAD // FF_SUB):
                lo, hi = s * FF_SUB, (s + 1) * FF_SUB
                gt = _dot_nt(n, wgu_v[j, lo:hi, :])
                ut = _dot_nt(n, wgu_v[4 + j, lo:hi, :])
                gate_ref[j, :, lo:hi] = gt.astype(BF16)
                up_ref[j, :, lo:hi] = ut.astype(BF16)
                act_ref[j, :, lo:hi] = ((gt * _sigmoid(gt)) * ut).astype(BF16)
            acc = acc + _dot(act_ref[j], wd_v[j])
        ht = xt + 0.5 * acc
        if head is None:
            h_ref[...] = ht
        else:
            gain = gf_ref[...]
            diff = _rms_fwd_tile(ht, gain) - t_ref[...]
            part = 0.5 * jnp.sum(jnp.sum(diff * diff, axis=-1, keepdims=True) / D, axis=0, keepdims=True)
            dx, dg = _rms_bwd_tile(ht, gain, diff / D)
            dh_ref[...] = dx
            _accumulate(loss_ref, step, jnp.broadcast_to(part, (8, 128)))
            _accumulate(dgf_ref, step, dg)

    ff = jax.ShapeDtypeStruct((4, t, FF_PAD), BF16)
    operands, in_specs = (x, g, wgu, wd), [_row_spec(TM, D), _const_spec((1, D)), _ANY, _ANY]
    out_specs = [_row_spec(TM, D)] + [_blk_row_spec(4, TM, FF_PAD)] * 3 + [_row_spec(TM, D)]
    out_shape = [jax.ShapeDtypeStruct((t, D), BF16), ff, ff, ff, jax.ShapeDtypeStruct((t, D), F32)]
    if head is not None:
        operands += tuple(head)
        in_specs += [_const_spec((1, D)), _row_spec(TM, D)]
        out_specs += [_const_spec((8, 128)), _const_spec((1, D))]
        out_shape += [jax.ShapeDtypeStruct((8, 128), F32), jax.ShapeDtypeStruct((1, D), F32)]
    return _call(
        body, operands, grid=(t // TM,), name=name, comm=comm, in_specs=in_specs, out_specs=out_specs,
        out_shape=out_shape,
        scratch_shapes=[pltpu.VMEM((N_DEV, FF_PAD, D), BF16), pltpu.VMEM((4, FF_PAD, D), BF16),
                        pltpu.SemaphoreType.DMA((1 + N_DEV,))])


def _ffn_bwd(dh, xin, g, gate, up, wgu, wd, name, comm=None):
    t = dh.shape[0]

    def body(dh_ref, x_ref, g_ref, gate_ref, up_ref, wgu_hbm, wd_hbm,
             dgu_ref, dhb_ref, dx_ref, dg_ref, wgu_v, wd_v, sems):
        step = pl.program_id(0)
        _zero_down_pad(step, wd_v)
        _load_resident(step, [(wgu_hbm, wgu_v)] + _down_pairs(wd_hbm, wd_v), sems)
        dht = dh_ref[...]
        dhb = (0.5 * dht).astype(BF16)
        dhb_ref[...] = dhb
        dn = jnp.zeros((TM, D), F32)
        for j in range(4):
            for s in range(FF_PAD // FF_SUB):
                lo, hi = s * FF_SUB, (s + 1) * FF_SUB
                da = _dot_nt(dhb, wd_v[j, lo:hi, :])
                gt = gate_ref[j, :, lo:hi].astype(F32)
                ut = up_ref[j, :, lo:hi].astype(F32)
                sg = _sigmoid(gt)
                dgt = (da * ut * (sg * (1.0 + gt * (1.0 - sg)))).astype(BF16)
                dut = (da * (gt * sg)).astype(BF16)
                dgu_ref[j, :, lo:hi] = dgt
                dgu_ref[4 + j, :, lo:hi] = dut
            dn = dn + _dot(dgu_ref[j], wgu_v[j]) + _dot(dgu_ref[4 + j], wgu_v[4 + j])
        dx, dg = _rms_bwd_tile(x_ref[...], g_ref[...], dn)
        dx_ref[...] = dht + dx
        _accumulate(dg_ref, step, dg)

    return _call(
        body, (dh, xin, g, gate, up, wgu, wd), grid=(t // TM,), name=name, comm=comm,
        in_specs=[_row_spec(TM, D), _row_spec(TM, D), _const_spec((1, D)), _blk_row_spec(4, TM, FF_PAD),
                  _blk_row_spec(4, TM, FF_PAD), _ANY, _ANY],
        out_specs=[_blk_row_spec(N_DEV, TM, FF_PAD), _row_spec(TM, D), _row_spec(TM, D), _const_spec((1, D))],
        out_shape=[jax.ShapeDtypeStruct((N_DEV, t, FF_PAD), BF16), jax.ShapeDtypeStruct((t, D), BF16),
                   jax.ShapeDtypeStruct((t, D), F32), jax.ShapeDtypeStruct((1, D), F32)],
        scratch_shapes=[pltpu.VMEM((N_DEV, FF_PAD, D), BF16), pltpu.VMEM((4, FF_PAD, D), BF16),
                        pltpu.SemaphoreType.DMA((1 + N_DEV,))])


WIDE_TILES = (1024, 512, 256, 128)


def _pick_tile(n, options=(512, 256, 128)):
    for o in options:
        if n % o == 0:
            return o
    return n


def _mm_tn_squares(pairs, name):
    k, d = pairs[0][0].shape
    n = pairs[0][1].shape[1]
    tn = _pick_tile(n)
    count = len(pairs)

    def body(*refs):
        m = pl.program_id(0)
        for idx in range(count):
            @pl.when(m == idx)
            def _(idx=idx):
                refs[-1][...] = _dot_tn(refs[idx][...], refs[count + idx][...]).astype(BF16).reshape(
                    N_DEV, d // N_DEV, tn)

    a_specs = [pl.BlockSpec((k, d), lambda m, j: (0, 0)) for _ in pairs]
    b_specs = [pl.BlockSpec((k, tn), lambda m, j, idx=idx: (0, jnp.where(m == idx, j, 0))) for idx in range(count)]
    return pl.pallas_call(
        body, grid=(count, n // tn), name=name, in_specs=a_specs + b_specs,
        out_specs=pl.BlockSpec((N_DEV, d // N_DEV, tn), lambda m, j: (0, m, j)),
        out_shape=jax.ShapeDtypeStruct((N_DEV, count * (d // N_DEV), n), BF16),
        compiler_params=_cparams(2),
    )(*[a for a, _ in pairs], *[b for _, b in pairs])


def _mm_tn_cols_many(a, parts, name):
    k, m = a.shape
    n = parts[0].shape[2]
    tn = _pick_tile(n)
    firsts, total = [], 0
    for p in parts:
        firsts.append(total)
        total += p.shape[0]

    def body(a_ref, *refs):
        j = pl.program_id(0)
        for p, first, ref in zip(parts, firsts, refs):
            @pl.when(jnp.logical_and(j >= first, j < first + p.shape[0]))
            def _(ref=ref):
                refs[-1][0] = _dot_tn(a_ref[...].astype(BF16), ref[0].astype(BF16)).astype(BF16)

    specs = [pl.BlockSpec((1, k, tn), lambda j, i, first=first, last=p.shape[0] - 1:
                          (jnp.clip(j - first, 0, last), 0, jnp.where(jnp.logical_and(j >= first, j <= first + last), i, 0)))
             for p, first in zip(parts, firsts)]
    return pl.pallas_call(
        body, grid=(total, n // tn), name=name,
        in_specs=[pl.BlockSpec((k, m), lambda j, i: (0, 0))] + specs,
        out_specs=pl.BlockSpec((1, m, tn), lambda j, i: (j, 0, i)),
        out_shape=jax.ShapeDtypeStruct((total, m, n), BF16),
        compiler_params=_cparams(2),
    )(a, *parts)


def _mm_tn_cols(a, b, name):
    k, m = a.shape
    nb, _, n = b.shape
    tm = _pick_tile(m, WIDE_TILES)

    def body(a_ref, b_ref, o_ref):
        o_ref[0] = _dot_tn(a_ref[...].astype(BF16), b_ref[0].astype(BF16)).astype(BF16)

    return pl.pallas_call(
        body, grid=(nb, m // tm), name=name,
        in_specs=[pl.BlockSpec((k, tm), lambda j, i: (0, i)), pl.BlockSpec((1, k, n), lambda j, i: (j, 0, 0))],
        out_specs=pl.BlockSpec((1, tm, n), lambda j, i: (j, i, 0)),
        out_shape=jax.ShapeDtypeStruct((nb, m, n), BF16),
        compiler_params=_cparams(2),
    )(a, b)


def _mm_tn_rows(a, b, keep, name, comm=None):
    nb, k, m = a.shape
    _, n = b.shape
    tn = _pick_tile(n, WIDE_TILES)

    def body(a_ref, b_ref, o_ref):
        o_ref[0] = _dot_tn(a_ref[0], b_ref[...])[:keep].astype(BF16)

    (out,), couts = _call(
        body, (a, b), grid=(nb, n // tn), name=name, comm=comm,
        in_specs=[pl.BlockSpec((1, k, m), lambda j, i: (j, 0, 0)), pl.BlockSpec((k, tn), lambda j, i: (0, i))],
        out_specs=[pl.BlockSpec((1, keep, tn), lambda j, i: (j, 0, i))],
        out_shape=[jax.ShapeDtypeStruct((nb, keep, n), BF16)], scratch_shapes=[])
    return out if comm is None else (out, couts)


PCG_W = 5 * D
QKV_W = 3 * D
PROJ_SUB = 512


def _inproj_fwd(h, g, w_in, conv_w, name, comm=None):
    t = h.shape[0]

    def body(h_ref, g_ref, w_hbm, cw_ref, u_ref, pcg_ref, qkv_ref, yc_ref, w_v, tail_v, sems):
        step = pl.program_id(0)
        _load_resident(step, [(w_hbm, w_v)], sems)

        @pl.when(step == 0)
        def _():
            tail_v[...] = jnp.zeros_like(tail_v)

        u = _rms_fwd_tile(h_ref[...], g_ref[...]).astype(BF16)
        u_ref[...] = u
        for blk in range(N_DEV):
            for s in range(D // PROJ_SUB):
                lo, hi = s * PROJ_SUB, (s + 1) * PROJ_SUB
                p = _dot(u, w_v[blk, :, lo:hi])
                if blk < 3:
                    pcg_ref[:, blk * D + lo:blk * D + hi] = p
                elif blk < 6:
                    qkv_ref[:, (blk - 3) * D + lo:(blk - 3) * D + hi] = p.astype(BF16)
                else:
                    pcg_ref[:, (blk - 3) * D + lo:(blk - 3) * D + hi] = p
        xc = pcg_ref[:, D:2 * D] * pcg_ref[:, 2 * D:3 * D]
        ext = jnp.concatenate([tail_v[...], xc], axis=0)
        conv = (cw_ref[0:1, :] * pltpu.roll(ext, 2, 0)[8:] + cw_ref[1:2, :] * pltpu.roll(ext, 1, 0)[8:]
                + cw_ref[2:3, :] * xc)
        yc_ref[...] = (pcg_ref[:, 0:D] * conv).astype(BF16)
        tail_v[...] = xc[TM - 8:]

    return _call(
        body, (h, g, w_in, conv_w), grid=(t // TM,), name=name, comm=comm,
        in_specs=[_row_spec(TM, D), _const_spec((1, D)), _ANY, _const_spec((CONV_K, D))],
        out_specs=[_row_spec(TM, D), _row_spec(TM, PCG_W), _row_spec(TM, QKV_W), _row_spec(TM, D)],
        out_shape=[jax.ShapeDtypeStruct((t, D), BF16), jax.ShapeDtypeStruct((t, PCG_W), F32),
                   jax.ShapeDtypeStruct((t, QKV_W), BF16), jax.ShapeDtypeStruct((t, D), BF16)],
        scratch_shapes=[pltpu.VMEM((N_DEV, D, D), BF16), pltpu.VMEM((8, D), F32), pltpu.SemaphoreType.DMA((1,))])


def _tri2(cond):
    rr = lax.broadcasted_iota(jnp.int32, (2 * TK, TK), 0) & (TK - 1)
    cc = lax.broadcasted_iota(jnp.int32, (2 * TK, TK), 1)
    return cond(rr, cc).astype(BF16)


def _causal(shift, row0=0):
    rr = lax.broadcasted_iota(jnp.int32, (TQ - row0, TK), 0) + row0
    cc = lax.broadcasted_iota(jnp.int32, (TQ - row0, TK), 1)
    return cc + shift < rr


def _cumdot(v, tri2):
    hi = v.astype(BF16)
    lo = (v - hi.astype(F32)).astype(BF16)
    return _dot(jnp.concatenate([hi, lo], axis=1), tri2)


LOG2_E = 1.4426950408889634


def _log_1m_beta(z):
    return -(jnp.maximum(z, 0.0) + jnp.log2(1.0 + jnp.exp2(-jnp.abs(z))))


def _sb_specs(t):
    g = SB_H // SB_HPS
    w = SB_HPS * SB_DH
    q_spec = pl.BlockSpec((TQ, w), lambda h, i: (i, h))
    k_spec = pl.BlockSpec((t, w), lambda h, i: (0, g + h))
    v_spec = pl.BlockSpec((t, w), lambda h, i: (0, 2 * g + h))
    ct_spec = pl.BlockSpec((SB_HPS, TQ, 1), lambda h, i: (h, i, 0))
    return g, w, q_spec, k_spec, v_spec, ct_spec


def _sb_fwd(qkv, name, comm=None):
    t = qkv.shape[0]
    scale = SB_DH ** -0.5
    g, w, q_spec, k_spec, v_spec, ct_spec = _sb_specs(t)

    def body(q_ref, k_ref, v_ref, y_ref, ct_ref):
        i = pl.program_id(1)
        later = _tri2(lambda j, s: j > s)
        n_diag = TQ // TK

        def block(j, carry, shift):
            off = pl.multiple_of(j * TK, TK)
            zs, ms = [], []
            for hd in range(SB_HPS):
                cols = slice(hd * SB_DH, (hd + 1) * SB_DH)
                z = _dot_nt(q_ref[:, cols], k_ref[pl.ds(off, TK), cols]) * (scale * LOG2_E)
                m = _log_1m_beta(z)
                if shift is not None:
                    m = jnp.where(_causal(shift), m, 0.0)
                zs.append(z)
                ms.append(m)
            after = _cumdot(jnp.concatenate(ms, axis=0), later)
            out = []
            for hd in range(SB_HPS):
                acc, c_sum = carry[hd]
                cols = slice(hd * SB_DH, (hd + 1) * SB_DH)
                a = jnp.exp2((ms[hd] + zs[hd]) + (c_sum + after[hd * TQ:(hd + 1) * TQ]))
                if shift is not None:
                    a = jnp.where(_causal(shift), a, 0.0)
                out.append((acc + _dot(a.astype(BF16), v_ref[pl.ds(off, TK), cols]),
                            c_sum + jnp.sum(ms[hd], axis=1, keepdims=True)))
            return tuple(out)

        carry = tuple((jnp.zeros((TQ, SB_DH), F32), jnp.zeros((TQ, 1), F32)) for _ in range(SB_HPS))
        for d in reversed(range(n_diag)):
            carry = block(i * n_diag + d, carry, d * TK)
        carry = lax.fori_loop(0, i * n_diag, lambda jj, c: block(i * n_diag - 1 - jj, c, None), carry)
        for hd in range(SB_HPS):
            y_ref[:, hd * SB_DH:(hd + 1) * SB_DH] = carry[hd][0].astype(BF16)
            ct_ref[hd] = carry[hd][1]

    return _call(
        body, (qkv, qkv, qkv), grid=(g, t // TQ), name=name, comm=comm,
        in_specs=[q_spec, k_spec, v_spec],
        out_specs=[q_spec, ct_spec],
        out_shape=[jax.ShapeDtypeStruct((t, D), BF16), jax.ShapeDtypeStruct((SB_H, t, 1), F32)],
        scratch_shapes=[])


def _sb_bwd(qkv, dy, ctot, after, name, comm=None):
    t = qkv.shape[0]
    scale = SB_DH ** -0.5
    g, w, q_spec, k_spec, v_spec, ct_spec = _sb_specs(t)
    acc_spec = pl.BlockSpec((2, t, w), lambda h, i: (0, 0, h))

    def body(q_ref, k_ref, v_ref, dy_ref, ct_ref, after_ref, dq_ref, dkv_ref):
        i = pl.program_id(1)

        @pl.when(i == 0)
        def _():
            dkv_ref[...] = jnp.zeros_like(dkv_ref)

        upto = _tri2(lambda j, s: j <= s)
        n_diag = TQ // TK

        def block(j, carry, shift):
            off = pl.multiple_of(j * TK, TK)
            r0 = 0 if shift is None else shift
            nr = TQ - r0
            causal = None if shift is None else _causal(shift, r0)

            def grow(old, delta):
                return old + delta if r0 == 0 else jnp.concatenate([old[:r0], old[r0:] + delta], axis=0)

            zs, ms = [], []
            for hd in range(SB_HPS):
                cols = slice(hd * SB_DH, (hd + 1) * SB_DH)
                z = _dot_nt(q_ref[r0:, cols], k_ref[pl.ds(off, TK), cols]) * (scale * LOG2_E)
                m = _log_1m_beta(z)
                if causal is not None:
                    m = jnp.where(causal, m, 0.0)
                zs.append(z)
                ms.append(m)
            m_upto = _cumdot(jnp.concatenate(ms, axis=0), upto)
            ls, a_s, es = [], [], []
            for hd in range(SB_HPS):
                cols = slice(hd * SB_DH, (hd + 1) * SB_DH)
                l = ms[hd] + zs[hd]
                a = jnp.exp2(l + ((ct_ref[hd, r0:] - carry[hd][1][r0:]) - m_upto[hd * nr:(hd + 1) * nr]))
                if causal is not None:
                    a = jnp.where(causal, a, 0.0)
                ls.append(l)
                a_s.append(a)
                es.append(_dot_nt(dy_ref[r0:, cols], v_ref[pl.ds(off, TK), cols]) * a)
            e_upto = _dot(jnp.concatenate(es, axis=0).astype(BF16), upto[:TK])
            out = []
            for hd in range(SB_HPS):
                dq, p_sum, e_sum = carry[hd]
                cols = slice(hd * SB_DH, (hd + 1) * SB_DH)
                e = es[hd]
                dz = e - jnp.exp2(ls[hd]) * (e_sum[r0:] + e_upto[hd * nr:(hd + 1) * nr])
                if causal is not None:
                    dz = jnp.where(causal, dz, 0.0)
                dzs = (dz * scale).astype(BF16)
                dkv_ref[0, pl.ds(off, TK), cols] += _dot_tn(dzs, q_ref[r0:, cols])
                dkv_ref[1, pl.ds(off, TK), cols] += _dot_tn(a_s[hd].astype(BF16), dy_ref[r0:, cols])
                out.append((grow(dq, _dot(dzs, k_ref[pl.ds(off, TK), cols])),
                            grow(p_sum, jnp.sum(ms[hd], axis=1, keepdims=True)),
                            grow(e_sum, jnp.sum(e, axis=1, keepdims=True))))
            return tuple(out)

        zero = jnp.zeros((TQ, 1), F32)
        init = tuple((jnp.zeros((TQ, SB_DH), F32), zero, zero) for _ in range(SB_HPS))
        carry = lax.fori_loop(0, i * n_diag, lambda j, c: block(j, c, None), init)
        for d in range(n_diag):
            carry = block(i * n_diag + d, carry, d * TK)
        for hd in range(SB_HPS):
            dq_ref[:, hd * SB_DH:(hd + 1) * SB_DH] = carry[hd][0].astype(BF16)

    return _call(
        body, (qkv, qkv, qkv, dy, ctot, after), grid=(g, t // TQ), name=name, comm=comm,
        in_specs=[q_spec, k_spec, v_spec, q_spec, ct_spec, pl.BlockSpec(after.shape, lambda h, i: (0, 0))],
        out_specs=[q_spec, acc_spec],
        out_shape=[jax.ShapeDtypeStruct((t, D), BF16), jax.ShapeDtypeStruct((2, t, D), F32)],
        scratch_shapes=[])


def _gate_specs():
    return [pl.BlockSpec((TM, D), lambda i: (i, 3)), pl.BlockSpec((TM, D), lambda i: (i, 4))]


def _mix_pairs(mix_hbm, dsts):
    pairs = []
    for index, dst in enumerate(dsts):
        pairs += _square_pairs(mix_hbm, index, dst)
    return pairs


def _mix_out_fwd(yc, ysb, pcg, b_gate, h, w_mix, name, comm=None):
    t = h.shape[0]

    def body(yc_ref, ysb_ref, gc_ref, gs_ref, b_ref, h_ref, mix_hbm,
             a_ref, b_out_ref, mg_ref, h2_ref, wc_v, wa_v, wo_v, sems):
        _load_resident(pl.program_id(0), _mix_pairs(mix_hbm, (wc_v, wa_v, wo_v)), sems)
        a = _dot(yc_ref[...], wc_v[...])
        b = _dot(ysb_ref[...], wa_v[...])
        merged = (_sigmoid(gc_ref[...] + b_ref[:, :D]) * a + _sigmoid(gs_ref[...] + b_ref[:, D:]) * b).astype(BF16)
        a_ref[...] = a
        b_out_ref[...] = b
        mg_ref[...] = merged
        h2_ref[...] = h_ref[...] + _dot(merged, wo_v[...])

    return _call(
        body, (yc, ysb, pcg, pcg, b_gate, h, w_mix), grid=(t // TM,), name=name, comm=comm,
        in_specs=[_row_spec(TM, D), _row_spec(TM, D)] + _gate_specs()
                 + [_const_spec((1, 2 * D)), _row_spec(TM, D), _ANY],
        out_specs=[_row_spec(TM, D)] * 4,
        out_shape=[jax.ShapeDtypeStruct((t, D), F32), jax.ShapeDtypeStruct((t, D), F32),
                   jax.ShapeDtypeStruct((t, D), BF16), jax.ShapeDtypeStruct((t, D), F32)],
        scratch_shapes=[pltpu.VMEM((D, D), BF16)] * 3 + [pltpu.SemaphoreType.DMA((3 * N_DEV,))])


def _mix_out_bwd(dh2, a, b, pcg, b_gate, conv_w, w_mix, name, comm=None):
    t = dh2.shape[0]
    n_tile = t // TM
    per8 = TM // 8

    def rows(n):
        return pl.BlockSpec((TM, n), lambda i: (n_tile - 1 - i, 0))

    def cols(block):
        return pl.BlockSpec((TM, D), lambda i: (n_tile - 1 - i, block))

    def before(block):
        return pl.BlockSpec((8, D), lambda i: (jnp.maximum((n_tile - 1 - i) * per8 - 1, 0), block))

    def body(dh_ref, a_ref, b_ref, gc_ref, gs_ref, cb_ref, cc_ref, cx_ref, ccp_ref, cxp_ref, bias_ref, cw_ref, mix_hbm,
             dhb_ref, da_ref, db_ref, dgp_ref, dc_ref, dysb_ref, dbias_ref, dcw_ref, wc_v, wa_v, wo_v, head_v, sems):
        step = pl.program_id(0)
        _load_resident(step, _mix_pairs(mix_hbm, (wc_v, wa_v, wo_v)), sems)

        @pl.when(step == 0)
        def _():
            head_v[...] = jnp.zeros_like(head_v)
            dcw_ref[...] = jnp.zeros_like(dcw_ref)

        dhb = dh_ref[...].astype(BF16)
        dhb_ref[...] = dhb
        dm = _dot_nt(dhb, wo_v[...])
        gc = _sigmoid(gc_ref[...] + bias_ref[:, :D])
        gs = _sigmoid(gs_ref[...] + bias_ref[:, D:])
        da = (dm * gc).astype(BF16)
        db = (dm * gs).astype(BF16)
        da_ref[...] = da
        db_ref[...] = db
        dgc = dm * a_ref[...] * (gc * (1.0 - gc))
        dgs = dm * b_ref[...] * (gs * (1.0 - gs))
        dgp_ref[0] = dgc.astype(BF16)
        dgp_ref[1] = dgs.astype(BF16)
        _accumulate(dbias_ref.at[:, :D], step, jnp.sum(dgc, axis=0, keepdims=True))
        _accumulate(dbias_ref.at[:, D:], step, jnp.sum(dgs, axis=0, keepdims=True))
        dysb_ref[...] = _dot_nt(db, wa_v[...]).astype(BF16)
        dyc = _dot_nt(da, wc_v[...])
        cc, cx = cc_ref[...], cx_ref[...]
        xc = cc * cx
        xc_before = jnp.where(step == n_tile - 1, 0.0, ccp_ref[...] * cxp_ref[...])
        ext = jnp.concatenate([xc_before, xc], axis=0)
        x1 = pltpu.roll(ext, 1, 0)[8:]
        x2 = pltpu.roll(ext, 2, 0)[8:]
        w0, w1, w2 = cw_ref[0:1, :], cw_ref[1:2, :], cw_ref[2:3, :]
        dc_ref[0] = (dyc * (w0 * x2 + w1 * x1 + w2 * xc)).astype(BF16)
        dconv = dyc * cb_ref[...]
        dcw_ref[0:1, :] += jnp.sum(dconv * x2, axis=0, keepdims=True)
        dcw_ref[1:2, :] += jnp.sum(dconv * x1, axis=0, keepdims=True)
        dcw_ref[2:3, :] += jnp.sum(dconv * xc, axis=0, keepdims=True)
        after = jnp.concatenate([dconv, head_v[...]], axis=0)
        dxc = w2 * dconv + w1 * pltpu.roll(after, TM + 7, 0)[:TM] + w0 * pltpu.roll(after, TM + 6, 0)[:TM]
        dc_ref[1] = (dxc * cx).astype(BF16)
        dc_ref[2] = (dxc * cc).astype(BF16)
        head_v[...] = dconv[:8]

    return _call(
        body, (dh2, a, b, pcg, pcg, pcg, pcg, pcg, pcg, pcg, b_gate, conv_w, w_mix), grid=(n_tile,), name=name,
        comm=comm,
        in_specs=[rows(D)] * 3 + [cols(3), cols(4), cols(0), cols(1), cols(2), before(1), before(2),
                                  _const_spec((1, 2 * D)), _const_spec((CONV_K, D)), _ANY],
        out_specs=[rows(D)] * 3 + [pl.BlockSpec((2, TM, D), lambda i: (0, n_tile - 1 - i, 0)),
                                   pl.BlockSpec((3, TM, D), lambda i: (0, n_tile - 1 - i, 0)), rows(D),
                                   _const_spec((1, 2 * D)), _const_spec((8, D))],
        out_shape=[jax.ShapeDtypeStruct((t, D), BF16)] * 3
                  + [jax.ShapeDtypeStruct((2, t, D), BF16), jax.ShapeDtypeStruct((3, t, D), BF16),
                     jax.ShapeDtypeStruct((t, D), BF16), jax.ShapeDtypeStruct((1, 2 * D), F32),
                     jax.ShapeDtypeStruct((8, D), F32)],
        scratch_shapes=[pltpu.VMEM((D, D), BF16)] * 3 + [pltpu.VMEM((8, D), F32),
                                                         pltpu.SemaphoreType.DMA((3 * N_DEV,))])


def _inproj_bwd(dconv, dq, dkv, dgp, w_in, h, g, dh_res, name, comm=None):
    t = h.shape[0]

    def body(dc_ref, dq_ref, dkv_ref, dgp_ref, w_hbm, h_ref, g_ref, dres_ref, dh_ref, dg_ref, w_v, sems):
        step = pl.program_id(0)
        _load_resident(step, [(w_hbm, w_v)], sems)
        du = _dot_nt(dq_ref[...], w_v[3])
        for k in range(3):
            du = du + _dot_nt(dc_ref[k], w_v[k])
        for k in range(2):
            du = du + _dot_nt(dkv_ref[k].astype(BF16), w_v[4 + k]) + _dot_nt(dgp_ref[k], w_v[6 + k])
        dx, dg = _rms_bwd_tile(h_ref[...], g_ref[...], du)
        dh_ref[...] = dres_ref[...] + dx
        _accumulate(dg_ref, step, dg)

    return _call(
        body, (dconv, dq, dkv, dgp, w_in, h, g, dh_res), grid=(t // TM,), name=name, comm=comm,
        in_specs=[_blk_row_spec(3, TM, D), _row_spec(TM, D), _blk_row_spec(2, TM, D), _blk_row_spec(2, TM, D), _ANY,
                  _row_spec(TM, D), _const_spec((1, D)), _row_spec(TM, D)],
        out_specs=[_row_spec(TM, D), _const_spec((1, D))],
        out_shape=[jax.ShapeDtypeStruct((t, D), F32), jax.ShapeDtypeStruct((1, D), F32)],
        scratch_shapes=[pltpu.VMEM((N_DEV, D, D), BF16), pltpu.SemaphoreType.DMA((1,))])


def _softmax_rows(s):
    e = jnp.exp(s - jnp.max(s, axis=-1, keepdims=True))
    return e / jnp.sum(e, axis=-1, keepdims=True)


def _cross_pairs(cross_hbm, wq_v, wo_v):
    return _square_pairs(cross_hbm, 0, wq_v) + _square_pairs(cross_hbm, 1, wo_v)


def _cross_fwd(h, g, mem, g_mem, w_ckv, w_cross, name):
    t = h.shape[0]
    m = mem.shape[0]
    scale = X_DH ** -0.5

    def body(h_ref, g_ref, mem_ref, gm_ref, wkv_ref, cross_hbm, hn_ref, qx_ref, o_ref, h3_ref, mn_ref, kv_ref,
             wq_v, wo_v, sems):
        _load_resident(pl.program_id(0), _cross_pairs(cross_hbm, wq_v, wo_v), sems)

        @pl.when(pl.program_id(0) == 0)
        def _():
            mn = _rms_fwd_tile(mem_ref[...], gm_ref[...]).astype(BF16)
            mn_ref[...] = mn
            for j in range(N_DEV):
                kv_ref[j] = _dot(mn, wkv_ref[j]).astype(BF16)

        ht = h_ref[...]
        hn = _rms_fwd_tile(ht, g_ref[...]).astype(BF16)
        hn_ref[...] = hn
        qx = _dot(hn, wq_v[...]).astype(BF16)
        qx_ref[...] = qx
        for hd in range(X_H):
            lo, hi = hd * X_DH, (hd + 1) * X_DH
            p = _softmax_rows(_dot_nt(qx[:, lo:hi], kv_ref[hd]) * scale)
            o_ref[:, lo:hi] = _dot(p.astype(BF16), kv_ref[X_H + hd]).astype(BF16)
        h3_ref[...] = ht + _dot(o_ref[...], wo_v[...])

    return pl.pallas_call(
        body, grid=(t // TM,), name=name,
        in_specs=[_row_spec(TM, D), _const_spec((1, D)), _const_spec((m, D)), _const_spec((1, D)),
                  _const_spec((N_DEV, D, X_DH)), _ANY],
        out_specs=[_row_spec(TM, D)] * 4 + [_const_spec((m, D)), _const_spec((N_DEV, m, X_DH))],
        out_shape=[jax.ShapeDtypeStruct((t, D), BF16)] * 3 + [jax.ShapeDtypeStruct((t, D), F32),
                                                              jax.ShapeDtypeStruct((m, D), BF16),
                                                              jax.ShapeDtypeStruct((N_DEV, m, X_DH), BF16)],
        scratch_shapes=[pltpu.VMEM((D, D), BF16)] * 2 + [pltpu.SemaphoreType.DMA((2 * N_DEV,))],
        compiler_params=_cparams(),
    )(h, g, mem, g_mem, w_ckv, w_cross)


def _cross_bwd(dh3, h, g, qx, kv, mem, g_mem, w_ckv, w_cross, name, comm=None):
    t = h.shape[0]
    m = kv.shape[1]
    scale = X_DH ** -0.5

    def body(dh_ref, h_ref, g_ref, qx_ref, kv_ref, mem_ref, gm_ref, wkv_ref, cross_hbm,
             dhb_ref, dqx_ref, dkv_ref, dh2_ref, dg_ref, dgm_ref, wq_v, wo_v, sems):
        step = pl.program_id(0)
        _load_resident(step, _cross_pairs(cross_hbm, wq_v, wo_v), sems)

        @pl.when(step == 0)
        def _():
            dkv_ref[...] = jnp.zeros_like(dkv_ref)

        dht = dh_ref[...]
        dhb = dht.astype(BF16)
        dhb_ref[...] = dhb
        do = _dot_nt(dhb, wo_v[...]).astype(BF16)
        for hd in range(X_H):
            lo, hi = hd * X_DH, (hd + 1) * X_DH
            qh = qx_ref[:, lo:hi]
            kh = kv_ref[hd]
            p = _softmax_rows(_dot_nt(qh, kh) * scale)
            doh = do[:, lo:hi]
            dp = _dot_nt(doh, kv_ref[X_H + hd])
            ds = (p * (dp - jnp.sum(dp * p, axis=-1, keepdims=True)) * scale).astype(BF16)
            dqx_ref[:, lo:hi] = _dot(ds, kh).astype(BF16)
            dkv_ref[hd] += _dot_tn(ds, qh)
            dkv_ref[X_H + hd] += _dot_tn(p.astype(BF16), doh)
        dhn = _dot_nt(dqx_ref[...], wq_v[...])
        dx, dg = _rms_bwd_tile(h_ref[...], g_ref[...], dhn)
        dh2_ref[...] = dht + dx
        _accumulate(dg_ref, step, dg)

        @pl.when(step == t // TM - 1)
        def _():
            dmn = jnp.zeros((m, D), F32)
            for j in range(N_DEV):
                dmn = dmn + _dot_nt(dkv_ref[j].astype(BF16), wkv_ref[j])
            dgm_ref[...] = _rms_bwd_tile(mem_ref[...], gm_ref[...], dmn)[1]

    return _call(
        body, (dh3, h, g, qx, kv, mem, g_mem, w_ckv, w_cross), grid=(t // TM,), name=name, comm=comm,
        in_specs=[_row_spec(TM, D), _row_spec(TM, D), _const_spec((1, D)), _row_spec(TM, D),
                  _const_spec((N_DEV, m, X_DH)), _const_spec((m, D)), _const_spec((1, D)),
                  _const_spec((N_DEV, D, X_DH)), _ANY],
        out_specs=[_row_spec(TM, D), _row_spec(TM, D), _const_spec((N_DEV, m, X_DH)), _row_spec(TM, D),
                   _const_spec((1, D)), _const_spec((1, D))],
        out_shape=[jax.ShapeDtypeStruct((t, D), BF16), jax.ShapeDtypeStruct((t, D), BF16),
                   jax.ShapeDtypeStruct((N_DEV, m, X_DH), F32), jax.ShapeDtypeStruct((t, D), F32),
                   jax.ShapeDtypeStruct((1, D), F32), jax.ShapeDtypeStruct((1, D), F32)],
        scratch_shapes=[pltpu.VMEM((D, D), BF16)] * 2 + [pltpu.SemaphoreType.DMA((2 * N_DEV,))])


def _adamw(w, parts, m, v, name, row_block=0, token=None):
    r, c = w.shape
    n = parts.shape[0]
    tr = _pick_tile(r, (256, 352, 128))
    off = row_block * (r // tr)

    def body(*refs):
        if token is None:
            _adamw_update(None, *refs)
        else:
            _adamw_update(refs[4], *refs[:4], *refs[5:])

    spec = _row_spec(tr, c)
    in_specs = [spec, pl.BlockSpec((n, tr, c), lambda i: (0, i + off, 0)), spec, spec]
    operands = (w, parts, m, v)
    if token is not None:
        in_specs.append(_const_spec(token.shape))
        operands += (token,)
    return pl.pallas_call(
        body, grid=(r // tr,), name=name, in_specs=in_specs, out_specs=[spec] * 4,
        out_shape=[jax.ShapeDtypeStruct((r, c), F32)] * 4,
        compiler_params=_cparams(),
    )(*operands)


def _adamw_update(tok_ref, w_ref, p_ref, m_ref, v_ref, g_ref, d_ref, nm_ref, nv_ref):
    gt = p_ref[0].astype(F32)
    for k in range(1, p_ref.shape[0]):
        gt = gt + p_ref[k].astype(F32)
    if tok_ref is not None:
        gt = gt + tok_ref[0:1, 0:1]
    _adamw_apply(gt, w_ref, m_ref, v_ref, g_ref, d_ref, nm_ref, nv_ref)


def _adamw_own(w, land, own, chip, m, v, name, row_block=0, token=None):
    r, c = w.shape
    tr = _pick_tile(r, (256, 352, 128))
    off = row_block * (r // tr)

    def body(chip_ref, w_ref, land_ref, own_ref, m_ref, v_ref, *rest):
        mine = own_ref[0].astype(F32)
        gt = jnp.where(chip_ref[0] == 0, mine, land_ref[0].astype(F32))
        for k in range(1, N_CHIP):
            gt = gt + jnp.where(chip_ref[0] == k, mine, land_ref[k].astype(F32))
        if token is None:
            _adamw_apply(gt, w_ref, m_ref, v_ref, *rest)
        else:
            before, *outs, later = rest
            _adamw_apply(gt + before[0:1, 0:1], w_ref, m_ref, v_ref, *outs)
            later[...] = before[...]

    spec = pl.BlockSpec((tr, c), lambda i, chip_ref: (i, 0))
    in_specs = [spec, pl.BlockSpec((N_CHIP, tr, c), lambda i, chip_ref: (0, i + off, 0)),
                pl.BlockSpec((1, tr, c), lambda i, chip_ref: (chip_ref[0], i + off, 0)), spec, spec]
    operands = (chip, w, land, own, m, v)
    out_specs, out_shape = [spec] * 4, [jax.ShapeDtypeStruct((r, c), F32)] * 4
    if token is not None:
        token_spec = pl.BlockSpec(token.shape, lambda i, chip_ref: (0, 0))
        in_specs.append(token_spec)
        operands += (token,)
        out_specs, out_shape = out_specs + [token_spec], out_shape + [jax.ShapeDtypeStruct(token.shape, token.dtype)]
    return pl.pallas_call(
        body, name=name,
        grid_spec=pltpu.PrefetchScalarGridSpec(
            num_scalar_prefetch=1, grid=(r // tr,), in_specs=in_specs, out_specs=out_specs),
        out_shape=out_shape,
        compiler_params=_cparams(),
    )(*operands)


def _adamw_apply(gt, w_ref, m_ref, v_ref, g_ref, d_ref, nm_ref, nv_ref):
    g_ref[...] = gt
    nm = ADAM_B1 * m_ref[...] + (1.0 - ADAM_B1) * gt
    nv = ADAM_B2 * v_ref[...] + (1.0 - ADAM_B2) * jnp.square(gt)
    m_hat = nm / (1.0 - ADAM_B1 ** ADAM_STEP)
    v_hat = nv / (1.0 - ADAM_B2 ** ADAM_STEP)
    d_ref[...] = -ADAM_LR * (m_hat / (jnp.sqrt(v_hat) + ADAM_EPS) + ADAM_WD * w_ref[...])
    nm_ref[...] = nm
    nv_ref[...] = nv


def _mesh_pos():
    return lax.axis_index("x"), lax.axis_index("y"), lax.axis_index("c")


def _both(first, second):
    n_in, n_out, n_sem = len(first.inputs), len(first.out_shapes), len(first.sem_shapes)

    def run(round_name):
        def both(in_refs, out_refs, sems):
            getattr(first, round_name)(in_refs[:n_in], out_refs[:n_out], sems[:n_sem])
            getattr(second, round_name)(in_refs[n_in:], out_refs[n_out:], sems[n_sem:])
        return both

    return types.SimpleNamespace(
        inputs=first.inputs + second.inputs, out_shapes=first.out_shapes + second.out_shapes,
        sem_shapes=first.sem_shapes + second.sem_shapes, start=run("start"), middle=run("middle"),
        finish=run("finish"))


def _no_round(in_refs, out_refs, sems):
    pass


def _after(token):
    return types.SimpleNamespace(inputs=[token], out_shapes=[], sem_shapes=[], start=_no_round, middle=_no_round,
                                 finish=_no_round)


def _run_exchange(comm, name):
    c_in, c_out = len(comm.inputs), len(comm.out_shapes)

    def body(*refs):
        cins, couts, sems = refs[:c_in], refs[c_in:c_in + c_out], refs[c_in + c_out:]
        comm.start(cins, couts, sems)
        comm.middle(cins, couts, sems)
        comm.finish(cins, couts, sems)

    return list(pl.pallas_call(
        body, name=name, out_shape=list(comm.out_shapes),
        in_specs=[_ANY] * c_in, out_specs=[_ANY] * c_out, scratch_shapes=list(comm.sem_shapes),
    )(*comm.inputs))


def _gather_exchange(shards):
    n_arr = len(shards)

    def plan(x_refs, out_refs, sems):
        send_sems, recv_sems, local_sems = sems[:3]
        stage = sems[3:]
        x, y, c = _mesh_pos()
        me, sibling = (x, y, c), (x, y, 1 - c)
        xn, yn, diag = (1 - x, y), (x, 1 - y), (1 - x, 1 - y)

        def slot(a, px, py, pc, half=None):
            ref = out_refs[a].at[4 * px + 2 * py + pc]
            if half is None:
                return ref
            rows = shards[a].shape[0] // 2
            return ref.at[half * rows:(half + 1) * rows]

        def copy(a, k, block, to, half=None, src=None):
            dst = slot(a, *block, half)
            return pltpu.make_async_remote_copy(
                src_ref=dst if src is None else src, dst_ref=dst,
                send_sem=send_sems.at[a, k], recv_sem=recv_sems.at[a, k],
                device_id=to, device_id_type=pl.DeviceIdType.MESH)

        return types.SimpleNamespace(
            me=me, sibling=sibling, xn=xn, yn=yn, diag=diag, c=c, copy=copy,
            mine_in=[pltpu.make_async_copy(x_refs[a], stage[a], local_sems.at[a, 0]) for a in range(n_arr)],
            mine_out=[pltpu.make_async_copy(stage[a], slot(a, *me), local_sems.at[a, 1]) for a in range(n_arr)],
            first=[cp for a in range(n_arr) for cp in (
                copy(a, 0, me, sibling, src=x_refs[a]), copy(a, 1, me, (*xn, c), src=x_refs[a]),
                copy(a, 2, me, (*yn, c), src=x_refs[a]))],
            second=lambda a: (copy(a, 3, (*xn, c), (*yn, c), half=0), copy(a, 5, (*xn, c), sibling),
                              copy(a, 4, (*yn, c), (*xn, c), half=1), copy(a, 6, (*yn, c), sibling)),
            third=lambda a: (copy(a, 7, (*diag, c), sibling, half=0), copy(a, 8, (*diag, c), sibling, half=1)))

    def start(x_refs, out_refs, sems):
        p = plan(x_refs, out_refs, sems)
        for cp in p.first + p.mine_in:
            cp.start()
        for cp_in, cp_out in zip(p.mine_in, p.mine_out):
            cp_in.wait()
            cp_out.start()

    def middle(x_refs, out_refs, sems):
        p = plan(x_refs, out_refs, sems)
        for a in range(n_arr):
            to_yn, x_to_sib, to_xn, y_to_sib = p.second(a)
            p.copy(a, 1, (*p.xn, p.c), p.me).wait_recv()
            to_yn.start()
            x_to_sib.start()
            p.copy(a, 2, (*p.yn, p.c), p.me).wait_recv()
            to_xn.start()
            y_to_sib.start()

    def finish(x_refs, out_refs, sems):
        p = plan(x_refs, out_refs, sems)
        for a in range(n_arr):
            half0_to_sib, half1_to_sib = p.third(a)
            p.copy(a, 3, (*p.diag, p.c), p.me, half=0).wait_recv()
            half0_to_sib.start()
            p.copy(a, 4, (*p.diag, p.c), p.me, half=1).wait_recv()
            half1_to_sib.start()
        other = 1 - p.c
        for a in range(n_arr):
            p.copy(a, 0, p.sibling, p.me).wait_recv()
            p.copy(a, 5, (*p.xn, other), p.me).wait_recv()
            p.copy(a, 6, (*p.yn, other), p.me).wait_recv()
            p.copy(a, 7, (*p.diag, other), p.me, half=0).wait_recv()
            p.copy(a, 8, (*p.diag, other), p.me, half=1).wait_recv()
        for cp in p.first:
            cp.wait_send()
        for a in range(n_arr):
            for cp in p.second(a) + p.third(a):
                cp.wait_send()
        for cp in p.mine_out:
            cp.wait()

    return types.SimpleNamespace(
        inputs=list(shards), start=start, middle=middle, finish=finish,
        out_shapes=[jax.ShapeDtypeStruct((N_DEV,) + s.shape, s.dtype) for s in shards],
        sem_shapes=[pltpu.SemaphoreType.DMA((n_arr, 9)), pltpu.SemaphoreType.DMA((n_arr, 9)),
                    pltpu.SemaphoreType.DMA((n_arr, 2))] + [pltpu.VMEM(s.shape, s.dtype) for s in shards])


def _pair_exchange(grads):
    n_arr = len(grads)

    def plan(g_refs, land_refs, sems):
        send_sems, recv_sems = sems
        x, y, c = _mesh_pos()
        return [pltpu.make_async_remote_copy(
            src_ref=g_refs[a].at[2 * k + 1 - c], dst_ref=land_refs[a].at[k],
            send_sem=send_sems.at[a, k], recv_sem=recv_sems.at[a, k],
            device_id=(x, y, 1 - c), device_id_type=pl.DeviceIdType.MESH)
            for a in range(n_arr) for k in range(N_CHIP)]

    def start(g_refs, land_refs, sems):
        for cp in plan(g_refs, land_refs, sems):
            cp.start()

    def finish(g_refs, land_refs, sems):
        for cp in plan(g_refs, land_refs, sems):
            cp.wait()

    return types.SimpleNamespace(
        inputs=list(grads), start=start, middle=_no_round, finish=finish,
        out_shapes=[jax.ShapeDtypeStruct((N_CHIP,) + g.shape[1:], g.dtype) for g in grads],
        sem_shapes=[pltpu.SemaphoreType.DMA((n_arr, N_CHIP)), pltpu.SemaphoreType.DMA((n_arr, N_CHIP))])


def _direct_gather(shard):
    def plan(x_ref, out_ref, sems):
        send_sems, recv_sems, local_sem = sems
        x, y, c = _mesh_pos()
        mine = out_ref.at[4 * x + 2 * y + c]
        local = pltpu.make_async_copy(x_ref, mine, local_sem.at[0])
        remote = []
        for k in range(N_DEV - 1):
            peer = tuple(1 - pos if (k + 1) >> bit & 1 else pos for pos, bit in ((x, 2), (y, 1), (c, 0)))
            remote.append(pltpu.make_async_remote_copy(
                src_ref=x_ref, dst_ref=mine, send_sem=send_sems.at[k], recv_sem=recv_sems.at[k],
                device_id=peer, device_id_type=pl.DeviceIdType.MESH))
        return local, remote

    def start(x_refs, out_refs, sems):
        local, remote = plan(x_refs[0], out_refs[0], sems)
        for cp in [local] + remote:
            cp.start()

    def finish(x_refs, out_refs, sems):
        local, remote = plan(x_refs[0], out_refs[0], sems)
        for cp in remote:
            cp.wait_recv()
        for cp in remote:
            cp.wait_send()
        local.wait()

    return types.SimpleNamespace(
        inputs=[shard], start=start, middle=_no_round, finish=finish,
        out_shapes=[jax.ShapeDtypeStruct((N_DEV,) + shard.shape, shard.dtype)],
        sem_shapes=[pltpu.SemaphoreType.DMA((N_DEV - 1,)), pltpu.SemaphoreType.DMA((N_DEV - 1,)),
                    pltpu.SemaphoreType.DMA((1,))])


def _chip_exchange(parts):
    n_arr = len(parts)

    def plan(p_refs, land_refs, sems):
        send_sems, recv_sems, local_sems = sems
        x, y, c = _mesh_pos()
        my_chip = 2 * x + y
        chips = [(1 - x, y), (x, 1 - y), (1 - x, 1 - y)]
        local = [pltpu.make_async_copy(p_refs[a].at[my_chip], land_refs[a].at[my_chip], local_sems.at[a])
                 for a in range(n_arr)]

        def copy(a, k, src_slot, dst_slot, px, py):
            return pltpu.make_async_remote_copy(
                src_ref=p_refs[a].at[src_slot], dst_ref=land_refs[a].at[dst_slot],
                send_sem=send_sems.at[a, k], recv_sem=recv_sems.at[a, k],
                device_id=(px, py, c), device_id_type=pl.DeviceIdType.MESH)

        sends = [copy(a, k, 2 * px + py, my_chip, px, py) for a in range(n_arr) for k, (px, py) in enumerate(chips)]
        arrivals = [copy(a, k, my_chip, 2 * px + py, px, py) for a in range(n_arr)
                    for k, (px, py) in enumerate(chips)]
        return local, sends, arrivals

    def start(p_refs, land_refs, sems):
        local, sends, _ = plan(p_refs, land_refs, sems)
        for cp in local + sends:
            cp.start()

    def finish(p_refs, land_refs, sems):
        local, sends, arrivals = plan(p_refs, land_refs, sems)
        for cp in arrivals:
            cp.wait_recv()
        for cp in sends:
            cp.wait_send()
        for cp in local:
            cp.wait()

    return types.SimpleNamespace(
        inputs=list(parts), start=start, middle=_no_round, finish=finish,
        out_shapes=[jax.ShapeDtypeStruct(p.shape, p.dtype) for p in parts],
        sem_shapes=[pltpu.SemaphoreType.DMA((n_arr, 3)), pltpu.SemaphoreType.DMA((n_arr, 3)),
                    pltpu.SemaphoreType.DMA((n_arr,))])


_HBM = pl.BlockSpec(memory_space=pltpu.HBM)
_SEM = pl.BlockSpec(memory_space=pltpu.SEMAPHORE)
_DATAFLOW = pltpu.SideEffectType.DATAFLOW_SIDE_EFFECTING


def _chip_copies(p_refs, land_refs, send_sems, recv_sems):
    x, y, c = _mesh_pos()
    my_chip = 2 * x + y
    chips = [(1 - x, y), (x, 1 - y), (1 - x, 1 - y)]
    return [pltpu.make_async_remote_copy(
        src_ref=p_refs[a].at[2 * px + py], dst_ref=land_refs[a].at[my_chip],
        send_sem=send_sems[3 * a + k], recv_sem=recv_sems[3 * a + k],
        device_id=(px, py, c), device_id_type=pl.DeviceIdType.MESH)
        for a in range(len(p_refs)) for k, (px, py) in enumerate(chips)]


def _chip_exchange_begin(parts, name):
    n_arr = len(parts)
    n_buf, n_copy = 2 * n_arr, 3 * n_arr
    lands = [lax.empty(p.shape, p.dtype) for p in parts]

    def body(*refs):
        p_refs, land_refs = refs[:n_arr], refs[n_arr:n_buf]
        send_sems, recv_sems, token = refs[n_buf:n_buf + n_copy], refs[n_buf + n_copy:n_buf + 2 * n_copy], refs[-1]
        for cp in _chip_copies(p_refs, land_refs, send_sems, recv_sems):
            cp.start()
        token[...] = jnp.zeros_like(token)

    bufs = list(parts) + list(lands)
    outs = pl.pallas_call(
        body, name=name,
        out_shape=(*[pltpu.SemaphoreType.DMA(())] * (2 * n_copy), *[pltpu.HBM(b.shape, b.dtype) for b in bufs],
                   jax.ShapeDtypeStruct((8, 128), F32)),
        in_specs=[_HBM] * n_buf,
        out_specs=(*[_SEM] * (2 * n_copy), *[_HBM] * n_buf, pl.BlockSpec(memory_space=pltpu.VMEM)),
        input_output_aliases={i: 2 * n_copy + i for i in range(n_buf)},
        compiler_params=pltpu.CompilerParams(has_side_effects=_DATAFLOW),
    )(*[pltpu.with_memory_space_constraint(b, pltpu.HBM) for b in bufs])
    sems = list(outs[:2 * n_copy])
    thru = list(outs[2 * n_copy:2 * n_copy + n_buf])
    return types.SimpleNamespace(send_sems=sems[:n_copy], recv_sems=sems[n_copy:], parts=thru[:n_arr],
                                 lands=thru[n_arr:], token=outs[-1])


def _chip_exchange_end(flight, after, name):
    send_sems, recv_sems, parts, lands = flight.send_sems, flight.recv_sems, flight.parts, flight.lands
    n_arr = len(parts)
    n_buf, n_copy = 2 * n_arr, 3 * n_arr

    def body(*refs):
        p_refs, land_refs = refs[:n_arr], refs[n_arr:n_buf]
        sems = refs[n_buf:n_buf + 2 * n_copy]
        for cp in _chip_copies(p_refs, land_refs, sems[:n_copy], sems[n_copy:]):
            cp.wait_send()
            cp.wait_recv()

    bufs = list(parts) + list(lands)
    outs = pl.pallas_call(
        body, name=name, out_shape=tuple(pltpu.HBM(b.shape, b.dtype) for b in bufs),
        in_specs=[_HBM] * n_buf + [_SEM] * (2 * n_copy) + [_ANY], out_specs=tuple([_HBM] * n_buf),
        input_output_aliases={i: i for i in range(n_buf)},
        compiler_params=pltpu.CompilerParams(has_side_effects=_DATAFLOW),
    )(*bufs, *send_sems, *recv_sems, after)
    return list(outs[:n_arr]), list(outs[n_arr:])


def _row_tile(r, cap=640):
    best = None
    for cand in range(16, min(r, cap) + 1, 16):
        if r % cand == 0:
            best = cand
    return best if best is not None else r


def _pair_sum(gs, landeds, core, name):
    tiles = [_row_tile(g.shape[1]) for g in gs]
    counts = [g.shape[1] // tr for g, tr in zip(gs, tiles)]
    n_arr = len(gs)

    def body(core_ref, *refs):
        for a in range(n_arr):
            mine, theirs, out = refs[2 * a], refs[2 * a + 1], refs[2 * n_arr + a]
            out[0] = (mine[0].astype(F32) + theirs[0].astype(F32)).astype(out.dtype)

    in_specs, out_specs, operands = [], [], []
    for g, landed, tr, count in zip(gs, landeds, tiles, counts):
        c_dim = g.shape[2]
        last = count - 1
        in_specs += [pl.BlockSpec((1, tr, c_dim),
                                  lambda k, i, core_ref, last=last: (2 * k + core_ref[0], jnp.minimum(i, last), 0)),
                     pl.BlockSpec((1, tr, c_dim), lambda k, i, core_ref, last=last: (k, jnp.minimum(i, last), 0))]
        out_specs.append(pl.BlockSpec((1, tr, c_dim), lambda k, i, core_ref, last=last: (k, jnp.minimum(i, last), 0)))
        operands += [g, landed]
    return list(pl.pallas_call(
        body, name=name,
        grid_spec=pltpu.PrefetchScalarGridSpec(
            num_scalar_prefetch=1, grid=(N_CHIP, max(counts)), in_specs=in_specs, out_specs=out_specs),
        out_shape=[jax.ShapeDtypeStruct((N_CHIP,) + g.shape[1:], g.dtype) for g in gs],
        compiler_params=_cparams(2),
    )(core, *operands))


def _sum_slots(parts, name):
    n, r, c_dim = parts.shape
    tr = _row_tile(r)

    def body(p_ref, o_ref):
        acc = p_ref[0].astype(F32)
        for k in range(1, n):
            acc = acc + p_ref[k].astype(F32)
        o_ref[...] = acc

    return pl.pallas_call(
        body, grid=(r // tr,), name=name,
        in_specs=[pl.BlockSpec((n, tr, c_dim), lambda i: (0, i, 0))],
        out_specs=_row_spec(tr, c_dim),
        out_shape=jax.ShapeDtypeStruct((r, c_dim), F32),
        compiler_params=_cparams(),
    )(parts)


GAINS = ("g_ffn1", "g_mix", "g_cross", "g_mem", "g_ffn2", "g_final")
SMALL = GAINS + ("b_gate", "conv_w")
SMALL_R = 16
LOSS_ROW = 11
WEIGHT_ORDER = ("g_ffn1", "w_ffn1_gu", "w_ffn1_down", "g_mix", "w_in", "b_gate", "conv_w", "w_conv_out",
                "w_attn_out", "w_o", "g_cross", "g_mem", "w_cq", "w_ckv", "w_co", "g_ffn2", "w_ffn2_gu",
                "w_ffn2_down", "g_final")
GU_NAMES = ("w_ffn1_gu", "w_ffn2_gu")


def _pack_small(vals, conv_rows):
    rows = [vals[n].reshape(1, D) for n in GAINS] + [vals["b_gate"].reshape(2, D), conv_rows.reshape(CONV_K, D)]
    used = len(GAINS) + 2 + CONV_K
    return jnp.concatenate(rows + [jnp.zeros((SMALL_R - used, D), F32)], axis=0)


def _unpack_small(buf):
    out = {n: buf[k] for k, n in enumerate(GAINS)}
    out["b_gate"] = buf[6:8].reshape(2 * D)
    out["conv_w"] = buf[8:8 + CONV_K]
    return out


def _exchange_shards(wts):
    out = {n: jnp.pad(wts[n].T.astype(BF16), ((0, FF_PAD - FF_BLK), (0, 0))) for n in GU_NAMES}
    for n in ("w_ckv", "w_in", "w_ffn1_down", "w_ffn2_down"):
        out[n] = wts[n].astype(BF16)
    out["mix"] = jnp.concatenate([wts[n].astype(BF16) for n in MIX_MATS], axis=0)
    out["cross"] = jnp.concatenate([wts[n].astype(BF16) for n in CROSS_MATS], axis=0)
    return out


def _reduce_group(grads, landed, core, names):
    return _pair_sum(grads, landed, core, "grads_pair_sum_" + "_".join(names))


def _step(x, mem, target, sh, conv_pad, gains, b_gate, core):
    wg1, wd1, conv_all = _run_exchange(_gather_exchange([sh["w_ffn1_gu"], sh["w_ffn1_down"], conv_pad]), "gather_ffn1")
    conv_w = conv_all[:, :CONV_K, :].transpose(1, 0, 2).reshape(CONV_K, D)
    (n1, gate1, up1, act1, h1), (w_in,) = _ffn_fwd(
        x, gains["g_ffn1"], wg1, wd1, "ffn1_fwd", comm=_gather_exchange([sh["w_in"]]))
    (u, pcg, qkv, yc), (w_mix, wd2) = _inproj_fwd(h1, gains["g_mix"], w_in, conv_w, "inproj_fwd",
                                                  comm=_gather_exchange([sh["mix"], sh["w_ffn2_down"]]))
    (ysb, ctot), (w_cross, w_ckv, wg2) = _sb_fwd(
        qkv, "sb_fwd", comm=_gather_exchange([sh["cross"], sh["w_ckv"], sh["w_ffn2_gu"]]))
    (a_mix, b_mix, merged, h2), _ = _mix_out_fwd(yc, ysb, pcg, b_gate, h1, w_mix, "mix_out_fwd")
    hn, qx, o_x, h3, mn, kv = _cross_fwd(h2, gains["g_cross"], mem, gains["g_mem"], w_ckv, w_cross, "cross_fwd")
    (n4, gate2, up2, act2, dh4, loss, dg_final), _ = _ffn_fwd(h3, gains["g_ffn2"], wg2, wd2, "ffn2_fwd",
                                                              head=(gains["g_final"], target))

    gs = {"g_final": dg_final}
    (dgu2, dh4b, dh3, gs["g_ffn2"]), _ = _ffn_bwd(dh4, h3, gains["g_ffn2"], gate2, up2, wg2, wd2, "ffn2_bwd")
    grads_a = [_mm_tn_rows(dgu2, n4, FF_PAD, "dw_ffn2_gu"),
               _mm_tn_rows(act2, dh4b, FF_BLK, "dw_ffn2_down").reshape(N_DEV, DOWN_ROWS, D)]
    names_a = ["w_ffn2_gu", "w_ffn2_down"]
    (dh3b, dqx, dkv, dh2, gs["g_cross"], gs["g_mem"]), _ = _cross_bwd(
        dh3, h2, gains["g_cross"], qx, kv, mem, gains["g_mem"], w_ckv, w_cross, "cross_bwd")
    grads_b = [_mm_tn_cols(mn, dkv, "dw_ckv"), _mm_tn_squares([(hn, dqx), (o_x, dh3b)], "dw_cross")]
    names_b = ["w_ckv", "cross"]
    (dh2b, da_mix, db_mix, dgp, dconv, dysb, gs["b_gate"], gs["conv_w"]), landed_ab = _mix_out_bwd(
        dh2, a_mix, b_mix, pcg, b_gate, conv_w, w_mix, "mix_out_bwd", comm=_pair_exchange(grads_a + grads_b))
    sums_ab = _reduce_group(grads_a + grads_b, landed_ab, core, names_a + names_b)
    grads_c = [_mm_tn_squares([(yc, da_mix), (ysb, db_mix), (merged, dh2b)], "dw_mix")]
    flight_ab = _chip_exchange_begin(sums_ab, "grads_to_chips_early_begin")
    (dq, dkv_sb), _ = _sb_bwd(qkv, dysb, ctot, flight_ab.token, "sb_bwd")
    grads_d = [_mm_tn_cols_many(u, [dconv, dq[None], dkv_sb, dgp], "dw_in")]
    (dh1, gs["g_mix"]), landed_cd = _inproj_bwd(dconv, dq, dkv_sb, dgp, w_in, h1, gains["g_mix"], dh2, "inproj_bwd",
                                                comm=_pair_exchange(grads_c + grads_d))
    sums_cd = _reduce_group(grads_c + grads_d, landed_cd, core, ["mix", "w_in"])
    flight_d = _chip_exchange_begin(sums_cd, "grads_to_chips_w_in_begin")
    (dgu1, dh1b, dx, gs["g_ffn1"]), _ = _ffn_bwd(dh1, x, gains["g_ffn1"] + flight_d.token[0, 0], gate1, up1, wg1, wd1,
                                                 "ffn1_bwd")
    dw_gu1 = _mm_tn_rows(dgu1, n1, FF_PAD, "dw_ffn1_gu")
    small_mine = _pack_small({n: gs[n] for n in GAINS + ("b_gate",)}, gs["conv_w"][:CONV_K])
    small_mine = small_mine.at[LOSS_ROW, 0].set(loss[0, 0])
    dw_down1, (landed_gu1, small_all) = _mm_tn_rows(
        act1, dh1b, FF_BLK, "dw_ffn1_down", comm=_both(_pair_exchange([dw_gu1]), _direct_gather(small_mine)))
    flight_gu1 = _chip_exchange_begin(_reduce_group([dw_gu1], [landed_gu1], core, ["w_ffn1_gu"]),
                                      "grads_to_chips_ffn1_gu_begin")
    grads_down1 = [dw_down1.reshape(N_DEV, DOWN_ROWS, D)]
    landed_down1 = _run_exchange(_both(_pair_exchange(grads_down1), _after(flight_gu1.token)),
                                 "grads_to_sibling_ffn1_down")
    flight_down1 = _chip_exchange_begin(_reduce_group(grads_down1, landed_down1, core, ["w_ffn1_down"]),
                                        "grads_to_chips_ffn1_down_begin")
    flights = [(names_a + names_b, flight_ab), (["mix", "w_in"], flight_d), (["w_ffn1_gu"], flight_gu1),
               (["w_ffn1_down"], flight_down1)]
    return dx, flights, small_all


def kernel(x, mem, g_ffn1, w_ffn1_gu, w_ffn1_down, g_mix, w_in, b_gate, conv_w, w_conv_out, w_attn_out, w_o, g_cross, g_mem, w_cq, w_ckv, w_co, g_ffn2, w_ffn2_gu, w_ffn2_down, g_final, loss_target, m_g_ffn1, m_w_ffn1_gu, m_w_ffn1_down, m_g_mix, m_w_in, m_b_gate, m_conv_w, m_w_conv_out, m_w_attn_out, m_w_o, m_g_cross, m_g_mem, m_w_cq, m_w_ckv, m_w_co, m_g_ffn2, m_w_ffn2_gu, m_w_ffn2_down, m_g_final, v_g_ffn1, v_w_ffn1_gu, v_w_ffn1_down, v_g_mix, v_w_in, v_b_gate, v_conv_w, v_w_conv_out, v_w_attn_out, v_w_o, v_g_cross, v_g_mem, v_w_cq, v_w_ckv, v_w_co, v_g_ffn2, v_w_ffn2_gu, v_w_ffn2_down, v_g_final):
    args = locals()
    wts = {n: args[n] for n in WEIGHT_ORDER}
    mom1 = {n: args["m_" + n] for n in WEIGHT_ORDER}
    mom2 = {n: args["v_" + n] for n in WEIGHT_ORDER}
    cx, cy, cc = _mesh_pos()
    dev = 4 * cx + 2 * cy + cc
    conv_cols = D // N_DEV

    conv_pad = jnp.concatenate([conv_w, jnp.zeros((SMALL_R - CONV_K, conv_cols), F32)], axis=0)
    gains = {n: wts[n].reshape(1, D) for n in GAINS}
    dx, flights, small_all = _step(x[0], mem[0], loss_target[0], _exchange_shards(wts), conv_pad, gains,
                                 b_gate.reshape(1, 2 * D), cc.reshape(1).astype(jnp.int32))

    grads, delta, new_m, new_v = {}, {}, {}, {}

    def operands(n, transposed):
        trio = (wts[n], mom1[n], mom2[n])
        return tuple(a.T for a in trio) if transposed else trio

    def record(n, res, transposed):
        grads[n], delta[n], new_m[n], new_v[n] = [r.T for r in res] if transposed else res

    early = [("w_ffn2_gu", "w_ffn2_gu", 0, True), ("w_ffn2_down", "w_ffn2_down", 0, False),
             ("w_ckv", "w_ckv", 0, False), ("w_in", "w_in", 0, False)]
    early += [(n, "mix", k, False) for k, n in enumerate(MIX_MATS)]
    early += [(n, "cross", k, False) for k, n in enumerate(CROSS_MATS)]
    chip = (2 * cx + cy).reshape(1).astype(jnp.int32)
    (names_early, flight_early), (names_w_in, flight_w_in), *last_flights = flights
    token = last_flights[-1][1].token
    own, land = {}, {}
    for names, flight, tag in ((names_early, flight_early, "early"), (names_w_in, flight_w_in, "w_in")):
        own_parts, landed = _chip_exchange_end(flight, token, "grads_to_chips_%s_end" % tag)
        own.update(zip(names, own_parts))
        land.update(zip(names, landed))
    for n, buf, row_block, transposed in early:
        w, m1, m2 = operands(n, transposed)
        *res, token = _adamw_own(w, land[buf], own[buf], chip, m1, m2, "adamw_" + n, row_block, token)
        record(n, res, transposed)

    for ((n,), flight), transposed in zip(last_flights, (True, False)):
        (own_n,), (land_n,) = _chip_exchange_end(flight, token, "grads_to_chips_%s_end" % n)
        w, m1, m2 = operands(n, transposed)
        *res, token = _adamw_own(w, land_n, own_n, chip, m1, m2, "adamw_" + n, token=token)
        record(n, res, transposed)

    small_sum = _sum_slots(small_all, "small_grads_sum")
    loss = small_sum[LOSS_ROW, 0]
    grad_small = _unpack_small(small_sum)
    grad_small["conv_w"] = lax.dynamic_slice_in_dim(grad_small["conv_w"], dev * conv_cols, conv_cols, axis=1)
    grads.update(grad_small)

    def small_buf(vals):
        return _pack_small(vals, jnp.concatenate([vals["conv_w"], jnp.zeros((CONV_K, D - conv_cols), F32)], axis=1))

    _, d_s, m_s, v_s = _adamw(small_buf(wts), small_buf(grads)[None], small_buf(mom1), small_buf(mom2), "adamw_small")
    for res, buf in ((delta, d_s), (new_m, m_s), (new_v, v_s)):
        un = _unpack_small(buf)
        for n in GAINS + ("b_gate",):
            res[n] = un[n]
        res["conv_w"] = un["conv_w"][:, :conv_cols]

    return (loss, dx[None], *[grads[n] for n in WEIGHT_ORDER], *[delta[n] for n in WEIGHT_ORDER],
            *[new_m[n] for n in WEIGHT_ORDER], *[new_v[n] for n in WEIGHT_ORDER])
```

```python
import types

import jax
import jax.numpy as jnp
from jax import lax
from jax.experimental import pallas as pl
from jax.experimental.pallas import tpu as pltpu

F32 = jnp.float32
BF16 = jnp.bfloat16

D = 1024
DFF = 2816
SB_H = 8
SB_DH = 128
X_H = 4
X_DH = 256
CONV_K = 3
RMS_EPS = 1e-6
N_DEV = 8
N_CHIP = 4
SQ_ROWS = D // N_DEV

ADAM_LR = 0.001
ADAM_B1 = 0.9
ADAM_B2 = 0.999
ADAM_EPS = 1e-08
ADAM_WD = 0.01
ADAM_STEP = 10

TM = 256
TQ = 512
TK = 256
SB_HPS = 2
VMEM_LIMIT = 56 << 20

FF_BLK = DFF // 4
FF_PAD = 768
FF_SUB = 256
DOWN_ROWS = DFF // N_DEV

MIX_MATS = ("w_conv_out", "w_attn_out", "w_o")
CROSS_MATS = ("w_cq", "w_co")

_ANY = pl.BlockSpec(memory_space=pl.ANY)


def _cparams(n_axes=1):
    return pltpu.CompilerParams(
        dimension_semantics=("arbitrary",) * n_axes, vmem_limit_bytes=VMEM_LIMIT)


def _row_spec(tm, n):
    return pl.BlockSpec((tm, n), lambda i: (i, 0))


def _blk_row_spec(nb, tm, n):
    return pl.BlockSpec((nb, tm, n), lambda i: (0, i, 0))


def _const_spec(shape):
    zeros = (0,) * len(shape)
    return pl.BlockSpec(shape, lambda i: zeros)


def _dot(a, b):
    return jnp.dot(a, b, preferred_element_type=F32)


def _dot_nt(a, b):
    return lax.dot_general(a, b, (((1,), (1,)), ((), ())), preferred_element_type=F32)


def _dot_tn(a, b):
    return lax.dot_general(a, b, (((0,), (0,)), ((), ())), preferred_element_type=F32)


def _sigmoid(x):
    return 1.0 / (1.0 + jnp.exp(-x))


def _call(body, operands, *, grid, in_specs, out_specs, out_shape, scratch_shapes, name, comm=None):
    n_in, n_out, n_sc = len(in_specs), len(out_specs), len(scratch_shapes)
    if comm is None:
        outs = pl.pallas_call(
            body, grid=grid, name=name, in_specs=in_specs, out_specs=out_specs, out_shape=out_shape,
            scratch_shapes=scratch_shapes, compiler_params=_cparams(len(grid)))(*operands)
        return list(outs), []
    c_in, c_out, c_sem = len(comm.inputs), len(comm.out_shapes), len(comm.sem_shapes)

    def hosted(*refs):
        bounds = [0, n_in, c_in, n_out, c_out, n_sc, c_sem]
        parts, pos = [], 0
        for k in bounds[1:]:
            parts.append(refs[pos:pos + k])
            pos += k
        ins, cins, outs, couts, scr, sems = parts
        step, n_steps = pl.program_id(0), grid[0]
        for ax in range(1, len(grid)):
            step, n_steps = step * grid[ax] + pl.program_id(ax), n_steps * grid[ax]

        @pl.when(step == 0)
        def _():
            comm.start(cins, couts, sems)

        @pl.when(step == (2 * n_steps) // 3)
        def _():
            comm.middle(cins, couts, sems)

        body(*ins, *outs, *scr)

        @pl.when(step == n_steps - 1)
        def _():
            comm.finish(cins, couts, sems)

    res = pl.pallas_call(
        hosted, grid=grid, name=name, in_specs=list(in_specs) + [_ANY] * c_in,
        out_specs=list(out_specs) + [_ANY] * c_out, out_shape=list(out_shape) + list(comm.out_shapes),
        scratch_shapes=list(scratch_shapes) + list(comm.sem_shapes),
        compiler_params=_cparams(len(grid)))(*operands, *comm.inputs)
    return list(res[:n_out]), list(res[n_out:])


def _load_resident(step, pairs, sems):
    @pl.when(step == 0)
    def _():
        copies = [pltpu.make_async_copy(src, dst, sems.at[k]) for k, (src, dst) in enumerate(pairs)]
        for cp in copies:
            cp.start()
        for cp in copies:
            cp.wait()


def _square_pairs(buf_hbm, index, dst):
    off = index * SQ_ROWS
    return [(buf_hbm.at[d, off:off + SQ_ROWS, :], dst.at[d * SQ_ROWS:(d + 1) * SQ_ROWS, :]) for d in range(N_DEV)]


def _down_pairs(wd_hbm, dst):
    return [(wd_hbm.at[d], dst.at[d // 2, (d % 2) * DOWN_ROWS:(d % 2 + 1) * DOWN_ROWS, :]) for d in range(N_DEV)]


def _zero_down_pad(step, dst):
    @pl.when(step == 0)
    def _():
        dst[:, FF_BLK:, :] = jnp.zeros((4, FF_PAD - FF_BLK, D), BF16)


def _rms_fwd_tile(xt, g):
    r = lax.rsqrt(jnp.mean(xt * xt, axis=-1, keepdims=True) + RMS_EPS)
    return (xt * r) * g


def _rms_bwd_tile(xt, g, dn):
    r = lax.rsqrt(jnp.mean(xt * xt, axis=-1, keepdims=True) + RMS_EPS)
    xhat = xt * r
    dxhat = dn * g
    dx = r * (dxhat - xhat * jnp.mean(dxhat * xhat, axis=-1, keepdims=True))
    dg = jnp.sum(dn * xhat, axis=0, keepdims=True)
    return dx, dg


def _accumulate(ref, step, value):
    @pl.when(step == 0)
    def _():
        ref[...] = value

    @pl.when(step != 0)
    def _():
        ref[...] = ref[...] + value


def _ffn_fwd(x, g, wgu, wd, name, comm=None, head=None):
    t = x.shape[0]

    def body(x_ref, g_ref, wgu_hbm, wd_hbm, *refs):
        if head is None:
            n_ref, gate_ref, up_ref, act_ref, h_ref, wgu_v, wd_v, sems = refs
        else:
            gf_ref, t_ref, n_ref, gate_ref, up_ref, act_ref, dh_ref, loss_ref, dgf_ref, wgu_v, wd_v, sems = refs
        step = pl.program_id(0)
        _zero_down_pad(step, wd_v)
        _load_resident(step, [(wgu_hbm, wgu_v)] + _down_pairs(wd_hbm, wd_v), sems)
        xt = x_ref[...]
        n = _rms_fwd_tile(xt, g_ref[...]).astype(BF16)
        n_ref[...] = n
        acc = jnp.zeros((TM, D), F32)
        for j in range(4):
            for s in range(FF_PAD // FF_SUB):
                lo, hi = s * FF_SUB, (s + 1) * FF_SUB
                gt = _dot_nt(n, wgu_v[j, lo:hi, :])
                ut = _dot_nt(n, wgu_v[4 + j, lo:hi, :])
                gate_ref[j, :, lo:hi] = gt.astype(BF16)
                up_ref[j, :, lo:hi] = ut.astype(BF16)
                act_ref[j, :, lo:hi] = ((gt * _sigmoid(gt)) * ut).astype(BF16)
            acc = acc + _dot(act_ref[j], wd_v[j])
        ht = xt + 0.5 * acc
        if head is None:
            h_ref[...] = ht
        else:
            gain = gf_ref[...]
            diff = _rms_fwd_tile(ht, gain) - t_ref[...]
            part = 0.5 * jnp.sum(jnp.sum(diff * diff, axis=-1, keepdims=True) / D, axis=0, keepdims=True)
            dx, dg = _rms_bwd_tile(ht, gain, diff / D)
            dh_ref[...] = dx
            _accumulate(loss_ref, step, jnp.broadcast_to(part, (8, 128)))
            _accumulate(dgf_ref, step, dg)

    ff = jax.ShapeDtypeStruct((4, t, FF_PAD), BF16)
    operands, in_specs = (x, g, wgu, wd), [_row_spec(TM, D), _const_spec((1, D)), _ANY, _ANY]
    out_specs = [_row_spec(TM, D)] + [_blk_row_spec(4, TM, FF_PAD)] * 3 + [_row_spec(TM, D)]
    out_shape = [jax.ShapeDtypeStruct((t, D), BF16), ff, ff, ff, jax.ShapeDtypeStruct((t, D), F32)]
    if head is not None:
        operands += tuple(head)
        in_specs += [_const_spec((1, D)), _row_spec(TM, D)]
        out_specs += [_const_spec((8, 128)), _const_spec((1, D))]
        out_shape += [jax.ShapeDtypeStruct((8, 128), F32), jax.ShapeDtypeStruct((1, D), F32)]
    return _call(
        body, operands, grid=(t // TM,), name=name, comm=comm, in_specs=in_specs, out_specs=out_specs,
        out_shape=out_shape,
        scratch_shapes=[pltpu.VMEM((N_DEV, FF_PAD, D), BF16), pltpu.VMEM((4, FF_PAD, D), BF16),
                        pltpu.SemaphoreType.DMA((1 + N_DEV,))])


def _ffn_bwd(dh, xin, g, gate, up, wgu, wd, name, comm=None):
    t = dh.shape[0]

    def body(dh_ref, x_ref, g_ref, gate_ref, up_ref, wgu_hbm, wd_hbm,
             dgu_ref, dhb_ref, dx_ref, dg_ref, wgu_v, wd_v, sems):
        step = pl.program_id(0)
        _zero_down_pad(step, wd_v)
        _load_resident(step, [(wgu_hbm, wgu_v)] + _down_pairs(wd_hbm, wd_v), sems)
        dht = dh_ref[...]
        dhb = (0.5 * dht).astype(BF16)
        dhb_ref[...] = dhb
        dn = jnp.zeros((TM, D), F32)
        for j in range(4):
            for s in range(FF_PAD // FF_SUB):
                lo, hi = s * FF_SUB, (s + 1) * FF_SUB
                da = _dot_nt(dhb, wd_v[j, lo:hi, :])
                gt = gate_ref[j, :, lo:hi].astype(F32)
                ut = up_ref[j, :, lo:hi].astype(F32)
                sg = _sigmoid(gt)
                dgt = (da * ut * (sg * (1.0 + gt * (1.0 - sg)))).astype(BF16)
                dut = (da * (gt * sg)).astype(BF16)
                dgu_ref[j, :, lo:hi] = dgt
                dgu_ref[4 + j, :, lo:hi] = dut
            dn = dn + _dot(dgu_ref[j], wgu_v[j]) + _dot(dgu_ref[4 + j], wgu_v[4 + j])
        dx, dg = _rms_bwd_tile(x_ref[...], g_ref[...], dn)
        dx_ref[...] = dht + dx
        _accumulate(dg_ref, step, dg)

    return _call(
        body, (dh, xin, g, gate, up, wgu, wd), grid=(t // TM,), name=name, comm=comm,
        in_specs=[_row_spec(TM, D), _row_spec(TM, D), _const_spec((1, D)), _blk_row_spec(4, TM, FF_PAD),
                  _blk_row_spec(4, TM, FF_PAD), _ANY, _ANY],
        out_specs=[_blk_row_spec(N_DEV, TM, FF_PAD), _row_spec(TM, D), _row_spec(TM, D), _const_spec((1, D))],
        out_shape=[jax.ShapeDtypeStruct((N_DEV, t, FF_PAD), BF16), jax.ShapeDtypeStruct((t, D), BF16),
                   jax.ShapeDtypeStruct((t, D), F32), jax.ShapeDtypeStruct((1, D), F32)],
        scratch_shapes=[pltpu.VMEM((N_DEV, FF_PAD, D), BF16), pltpu.VMEM((4, FF_PAD, D), BF16),
                        pltpu.SemaphoreType.DMA((1 + N_DEV,))])


WIDE_TILES = (1024, 512, 256, 128)


def _pick_tile(n, options=(512, 256, 128)):
    for o in options:
        if n % o == 0:
            return o
    return n


def _mm_tn_squares(pairs, name):
    k, d = pairs[0][0].shape
    n = pairs[0][1].shape[1]
    tn = _pick_tile(n)
    count = len(pairs)

    def body(*refs):
        m = pl.program_id(0)
        for idx in range(count):
            @pl.when(m == idx)
            def _(idx=idx):
                refs[-1][...] = _dot_tn(refs[idx][...], refs[count + idx][...]).astype(BF16).reshape(
                    N_DEV, d // N_DEV, tn)

    a_specs = [pl.BlockSpec((k, d), lambda m, j: (0, 0)) for _ in pairs]
    b_specs = [pl.BlockSpec((k, tn), lambda m, j, idx=idx: (0, jnp.where(m == idx, j, 0))) for idx in range(count)]
    return pl.pallas_call(
        body, grid=(count, n // tn), name=name, in_specs=a_specs + b_specs,
        out_specs=pl.BlockSpec((N_DEV, d // N_DEV, tn), lambda m, j: (0, m, j)),
        out_shape=jax.ShapeDtypeStruct((N_DEV, count * (d // N_DEV), n), BF16),
        compiler_params=_cparams(2),
    )(*[a for a, _ in pairs], *[b for _, b in pairs])


def _mm_tn_cols_many(a, parts, name):
    k, m = a.shape
    n = parts[0].shape[2]
    tn = _pick_tile(n)
    firsts, total = [], 0
    for p in parts:
        firsts.append(total)
        total += p.shape[0]

    def body(a_ref, *refs):
        j = pl.program_id(0)
        for p, first, ref in zip(parts, firsts, refs):
            @pl.when(jnp.logical_and(j >= first, j < first + p.shape[0]))
            def _(ref=ref):
                refs[-1][0] = _dot_tn(a_ref[...].astype(BF16), ref[0].astype(BF16)).astype(BF16)

    specs = [pl.BlockSpec((1, k, tn), lambda j, i, first=first, last=p.shape[0] - 1:
                          (jnp.clip(j - first, 0, last), 0, jnp.where(jnp.logical_and(j >= first, j <= first + last), i, 0)))
             for p, first in zip(parts, firsts)]
    return pl.pallas_call(
        body, grid=(total, n // tn), name=name,
        in_specs=[pl.BlockSpec((k, m), lambda j, i: (0, 0))] + specs,
        out_specs=pl.BlockSpec((1, m, tn), lambda j, i: (j, 0, i)),
        out_shape=jax.ShapeDtypeStruct((total, m, n), BF16),
        compiler_params=_cparams(2),
    )(a, *parts)


def _mm_tn_cols(a, b, name):
    k, m = a.shape
    nb, _, n = b.shape
    tm = _pick_tile(m, WIDE_TILES)

    def body(a_ref, b_ref, o_ref):
        o_ref[0] = _dot_tn(a_ref[...].astype(BF16), b_ref[0].astype(BF16)).astype(BF16)

    return pl.pallas_call(
        body, grid=(nb, m // tm), name=name,
        in_specs=[pl.BlockSpec((k, tm), lambda j, i: (0, i)), pl.BlockSpec((1, k, n), lambda j, i: (j, 0, 0))],
        out_specs=pl.BlockSpec((1, tm, n), lambda j, i: (j, i, 0)),
        out_shape=jax.ShapeDtypeStruct((nb, m, n), BF16),
        compiler_params=_cparams(2),
    )(a, b)


def _mm_tn_rows(a, b, keep, name, comm=None):
    nb, k, m = a.shape
    _, n = b.shape
    tn = _pick_tile(n, WIDE_TILES)

    def body(a_ref, b_ref, o_ref):
        o_ref[0] = _dot_tn(a_ref[0], b_ref[...])[:keep].astype(BF16)

    (out,), couts = _call(
        body, (a, b), grid=(nb, n // tn), name=name, comm=comm,
        in_specs=[pl.BlockSpec((1, k, m), lambda j, i: (j, 0, 0)), pl.BlockSpec((k, tn), lambda j, i: (0, i))],
        out_specs=[pl.BlockSpec((1, keep, tn), lambda j, i: (j, 0, i))],
        out_shape=[jax.ShapeDtypeStruct((nb, keep, n), BF16)], scratch_shapes=[])
    return out if comm is None else (out, couts)


PCG_W = 5 * D
QKV_W = 3 * D
PROJ_SUB = 512


def _inproj_fwd(h, g, w_in, conv_w, name, comm=None):
    t = h.shape[0]

    def body(h_ref, g_ref, w_hbm, cw_ref, u_ref, pcg_ref, qkv_ref, yc_ref, w_v, tail_v, sems):
        step = pl.program_id(0)
        _load_resident(step, [(w_hbm, w_v)], sems)

        @pl.when(step == 0)
        def _():
            tail_v[...] = jnp.zeros_like(tail_v)

        u = _rms_fwd_tile(h_ref[...], g_ref[...]).astype(BF16)
        u_ref[...] = u
        for blk in range(N_DEV):
            for s in range(D // PROJ_SUB):
                lo, hi = s * PROJ_SUB, (s + 1) * PROJ_SUB
                p = _dot(u, w_v[blk, :, lo:hi])
                if blk < 3:
                    pcg_ref[:, blk * D + lo:blk * D + hi] = p
                elif blk < 6:
                    qkv_ref[:, (blk - 3) * D + lo:(blk - 3) * D + hi] = p.astype(BF16)
                else:
                    pcg_ref[:, (blk - 3) * D + lo:(blk - 3) * D + hi] = p
        xc = pcg_ref[:, D:2 * D] * pcg_ref[:, 2 * D:3 * D]
        ext = jnp.concatenate([tail_v[...], xc], axis=0)
        conv = (cw_ref[0:1, :] * pltpu.roll(ext, 2, 0)[8:] + cw_ref[1:2, :] * pltpu.roll(ext, 1, 0)[8:]
                + cw_ref[2:3, :] * xc)
        yc_ref[...] = (pcg_ref[:, 0:D] * conv).astype(BF16)
        tail_v[...] = xc[TM - 8:]

    return _call(
        body, (h, g, w_in, conv_w), grid=(t // TM,), name=name, comm=comm,
        in_specs=[_row_spec(TM, D), _const_spec((1, D)), _ANY, _const_spec((CONV_K, D))],
        out_specs=[_row_spec(TM, D), _row_spec(TM, PCG_W), _row_spec(TM, QKV_W), _row_spec(TM, D)],
        out_shape=[jax.ShapeDtypeStruct((t, D), BF16), jax.ShapeDtypeStruct((t, PCG_W), F32),
                   jax.ShapeDtypeStruct((t, QKV_W), BF16), jax.ShapeDtypeStruct((t, D), BF16)],
        scratch_shapes=[pltpu.VMEM((N_DEV, D, D), BF16), pltpu.VMEM((8, D), F32), pltpu.SemaphoreType.DMA((1,))])


def _tri2(cond):
    rr = lax.broadcasted_iota(jnp.int32, (2 * TK, TK), 0) & (TK - 1)
    cc = lax.broadcasted_iota(jnp.int32, (2 * TK, TK), 1)
    return cond(rr, cc).astype(BF16)


def _causal(shift, row0=0):
    rr = lax.broadcasted_iota(jnp.int32, (TQ - row0, TK), 0) + row0
    cc = lax.broadcasted_iota(jnp.int32, (TQ - row0, TK), 1)
    return cc + shift < rr


def _cumdot(v, tri2):
    hi = v.astype(BF16)
    lo = (v - hi.astype(F32)).astype(BF16)
    return _dot(jnp.concatenate([hi, lo], axis=1), tri2)


LOG2_E = 1.4426950408889634


def _log_1m_beta(z):
    return -(jnp.maximum(z, 0.0) + jnp.log2(1.0 + jnp.exp2(-jnp.abs(z))))


def _sb_specs(t):
    g = SB_H // SB_HPS
    w = SB_HPS * SB_DH
    q_spec = pl.BlockSpec((TQ, w), lambda h, i: (i, h))
    k_spec = pl.BlockSpec((t, w), lambda h, i: (0, g + h))
    v_spec = pl.BlockSpec((t, w), lambda h, i: (0, 2 * g + h))
    ct_spec = pl.BlockSpec((SB_HPS, TQ, 1), lambda h, i: (h, i, 0))
    return g, w, q_spec, k_spec, v_spec, ct_spec


def _sb_fwd(qkv, name, comm=None):
    t = qkv.shape[0]
    scale = SB_DH ** -0.5
    g, w, q_spec, k_spec, v_spec, ct_spec = _sb_specs(t)

    def body(q_ref, k_ref, v_ref, y_ref, ct_ref):
        i = pl.program_id(1)
        later = _tri2(lambda j, s: j > s)
        n_diag = TQ // TK

        def block(j, carry, shift):
            off = pl.multiple_of(j * TK, TK)
            zs, ms = [], []
            for hd in range(SB_HPS):
                cols = slice(hd * SB_DH, (hd + 1) * SB_DH)
                z = _dot_nt(q_ref[:, cols], k_ref[pl.ds(off, TK), cols]) * (scale * LOG2_E)
                m = _log_1m_beta(z)
                if shift is not None:
                    m = jnp.where(_causal(shift), m, 0.0)
                zs.append(z)
                ms.append(m)
            after = _cumdot(jnp.concatenate(ms, axis=0), later)
            out = []
            for hd in range(SB_HPS):
                acc, c_sum = carry[hd]
                cols = slice(hd * SB_DH, (hd + 1) * SB_DH)
                a = jnp.exp2((ms[hd] + zs[hd]) + (c_sum + after[hd * TQ:(hd + 1) * TQ]))
                if shift is not None:
                    a = jnp.where(_causal(shift), a, 0.0)
                out.append((acc + _dot(a.astype(BF16), v_ref[pl.ds(off, TK), cols]),
                            c_sum + jnp.sum(ms[hd], axis=1, keepdims=True)))
            return tuple(out)

        carry = tuple((jnp.zeros((TQ, SB_DH), F32), jnp.zeros((TQ, 1), F32)) for _ in range(SB_HPS))
        for d in reversed(range(n_diag)):
            carry = block(i * n_diag + d, carry, d * TK)
        carry = lax.fori_loop(0, i * n_diag, lambda jj, c: block(i * n_diag - 1 - jj, c, None), carry)
        for hd in range(SB_HPS):
            y_ref[:, hd * SB_DH:(hd + 1) * SB_DH] = carry[hd][0].astype(BF16)
            ct_ref[hd] = carry[hd][1]

    return _call(
        body, (qkv, qkv, qkv), grid=(g, t // TQ), name=name, comm=comm,
        in_specs=[q_spec, k_spec, v_spec],
        out_specs=[q_spec, ct_spec],
        out_shape=[jax.ShapeDtypeStruct((t, D), BF16), jax.ShapeDtypeStruct((SB_H, t, 1), F32)],
        scratch_shapes=[])


def _sb_bwd(qkv, dy, ctot, after, name, comm=None):
    t = qkv.shape[0]
    scale = SB_DH ** -0.5
    g, w, q_spec, k_spec, v_spec, ct_spec = _sb_specs(t)
    acc_spec = pl.BlockSpec((2, t, w), lambda h, i: (0, 0, h))

    def body(q_ref, k_ref, v_ref, dy_ref, ct_ref, after_ref, dq_ref, dkv_ref):
        i = pl.program_id(1)

        @pl.when(i == 0)
        def _():
            dkv_ref[...] = jnp.zeros_like(dkv_ref)

        upto = _tri2(lambda j, s: j <= s)
        n_diag = TQ // TK

        def block(j, carry, shift):
            off = pl.multiple_of(j * TK, TK)
            r0 = 0 if shift is None else shift
            nr = TQ - r0
            causal = None if shift is None else _causal(shift, r0)

            def grow(old, delta):
                return old + delta if r0 == 0 else jnp.concatenate([old[:r0], old[r0:] + delta], axis=0)

            zs, ms = [], []
            for hd in range(SB_HPS):
                cols = slice(hd * SB_DH, (hd + 1) * SB_DH)
                z = _dot_nt(q_ref[r0:, cols], k_ref[pl.ds(off, TK), cols]) * (scale * LOG2_E)
                m = _log_1m_beta(z)
                if causal is not None:
                    m = jnp.where(causal, m, 0.0)
                zs.append(z)
                ms.append(m)
            m_upto = _cumdot(jnp.concatenate(ms, axis=0), upto)
            ls, a_s, es = [], [], []
            for hd in range(SB_HPS):
                cols = slice(hd * SB_DH, (hd + 1) * SB_DH)
                l = ms[hd] + zs[hd]
                a = jnp.exp2(l + ((ct_ref[hd, r0:] - carry[hd][1][r0:]) - m_upto[hd * nr:(hd + 1) * nr]))
                if causal is not None:
                    a = jnp.where(causal, a, 0.0)
                ls.append(l)
                a_s.append(a)
                es.append(_dot_nt(dy_ref[r0:, cols], v_ref[pl.ds(off, TK), cols]) * a)
            e_upto = _dot(jnp.concatenate(es, axis=0).astype(BF16), upto[:TK])
            out = []
            for hd in range(SB_HPS):
                dq, p_sum, e_sum = carry[hd]
                cols = slice(hd * SB_DH, (hd + 1) * SB_DH)
                e = es[hd]
                dz = e - jnp.exp2(ls[hd]) * (e_sum[r0:] + e_upto[hd * nr:(hd + 1) * nr])
                if causal is not None:
                    dz = jnp.where(causal, dz, 0.0)
                dzs = (dz * scale).astype(BF16)
                dkv_ref[0, pl.ds(off, TK), cols] += _dot_tn(dzs, q_ref[r0:, cols])
                dkv_ref[1, pl.ds(off, TK), cols] += _dot_tn(a_s[hd].astype(BF16), dy_ref[r0:, cols])
                out.append((grow(dq, _dot(dzs, k_ref[pl.ds(off, TK), cols])),
                            grow(p_sum, jnp.sum(ms[hd], axis=1, keepdims=True)),
                            grow(e_sum, jnp.sum(e, axis=1, keepdims=True))))
            return tuple(out)

        zero = jnp.zeros((TQ, 1), F32)
        init = tuple((jnp.zeros((TQ, SB_DH), F32), zero, zero) for _ in range(SB_HPS))
        carry = lax.fori_loop(0, i * n_diag, lambda j, c: block(j, c, None), init)
        for d in range(n_diag):
            carry = block(i * n_diag + d, carry, d * TK)
        for hd in range(SB_HPS):
            dq_ref[:, hd * SB_DH:(hd + 1) * SB_DH] = carry[hd][0].astype(BF16)

    return _call(
        body, (qkv, qkv, qkv, dy, ctot, after), grid=(g, t // TQ), name=name, comm=comm,
        in_specs=[q_spec, k_spec, v_spec, q_spec, ct_spec, pl.BlockSpec(after.shape, lambda h, i: (0, 0))],
        out_specs=[q_spec, acc_spec],
        out_shape=[jax.ShapeDtypeStruct((t, D), BF16), jax.ShapeDtypeStruct((2, t, D), F32)],
        scratch_shapes=[])


def _gate_specs():
    return [pl.BlockSpec((TM, D), lambda i: (i, 3)), pl.BlockSpec((TM, D), lambda i: (i, 4))]


def _mix_pairs(mix_hbm, dsts):
    pairs = []
    for index, dst in enumerate(dsts):
        pairs += _square_pairs(mix_hbm, index, dst)
    return pairs


def _mix_out_fwd(yc, ysb, pcg, b_gate, h, w_mix, name, comm=None):
    t = h.shape[0]

    def body(yc_ref, ysb_ref, gc_ref, gs_ref, b_ref, h_ref, mix_hbm,
             a_ref, b_out_ref, mg_ref, h2_ref, wc_v, wa_v, wo_v, sems):
        _load_resident(pl.program_id(0), _mix_pairs(mix_hbm, (wc_v, wa_v, wo_v)), sems)
        a = _dot(yc_ref[...], wc_v[...])
        b = _dot(ysb_ref[...], wa_v[...])
        merged = (_sigmoid(gc_ref[...] + b_ref[:, :D]) * a + _sigmoid(gs_ref[...] + b_ref[:, D:]) * b).astype(BF16)
        a_ref[...] = a
        b_out_ref[...] = b
        mg_ref[...] = merged
        h2_ref[...] = h_ref[...] + _dot(merged, wo_v[...])

    return _call(
        body, (yc, ysb, pcg, pcg, b_gate, h, w_mix), grid=(t // TM,), name=name, comm=comm,
        in_specs=[_row_spec(TM, D), _row_spec(TM, D)] + _gate_specs()
                 + [_const_spec((1, 2 * D)), _row_spec(TM, D), _ANY],
        out_specs=[_row_spec(TM, D)] * 4,
        out_shape=[jax.ShapeDtypeStruct((t, D), F32), jax.ShapeDtypeStruct((t, D), F32),
                   jax.ShapeDtypeStruct((t, D), BF16), jax.ShapeDtypeStruct((t, D), F32)],
        scratch_shapes=[pltpu.VMEM((D, D), BF16)] * 3 + [pltpu.SemaphoreType.DMA((3 * N_DEV,))])


def _mix_out_bwd(dh2, a, b, pcg, b_gate, conv_w, w_mix, name, comm=None):
    t = dh2.shape[0]
    n_tile = t // TM
    per8 = TM // 8

    def rows(n):
        return pl.BlockSpec((TM, n), lambda i: (n_tile - 1 - i, 0))

    def cols(block):
        return pl.BlockSpec((TM, D), lambda i: (n_tile - 1 - i, block))

    def before(block):
        return pl.BlockSpec((8, D), lambda i: (jnp.maximum((n_tile - 1 - i) * per8 - 1, 0), block))

    def body(dh_ref, a_ref, b_ref, gc_ref, gs_ref, cb_ref, cc_ref, cx_ref, ccp_ref, cxp_ref, bias_ref, cw_ref, mix_hbm,
             dhb_ref, da_ref, db_ref, dgp_ref, dc_ref, dysb_ref, dbias_ref, dcw_ref, wc_v, wa_v, wo_v, head_v, sems):
        step = pl.program_id(0)
        _load_resident(step, _mix_pairs(mix_hbm, (wc_v, wa_v, wo_v)), sems)

        @pl.when(step == 0)
        def _():
            head_v[...] = jnp.zeros_like(head_v)
            dcw_ref[...] = jnp.zeros_like(dcw_ref)

        dhb = dh_ref[...].astype(BF16)
        dhb_ref[...] = dhb
        dm = _dot_nt(dhb, wo_v[...])
        gc = _sigmoid(gc_ref[...] + bias_ref[:, :D])
        gs = _sigmoid(gs_ref[...] + bias_ref[:, D:])
        da = (dm * gc).astype(BF16)
        db = (dm * gs).astype(BF16)
        da_ref[...] = da
        db_ref[...] = db
        dgc = dm * a_ref[...] * (gc * (1.0 - gc))
        dgs = dm * b_ref[...] * (gs * (1.0 - gs))
        dgp_ref[0] = dgc.astype(BF16)
        dgp_ref[1] = dgs.astype(BF16)
        _accumulate(dbias_ref.at[:, :D], step, jnp.sum(dgc, axis=0, keepdims=True))
        _accumulate(dbias_ref.at[:, D:], step, jnp.sum(dgs, axis=0, keepdims=True))
        dysb_ref[...] = _dot_nt(db, wa_v[...]).astype(BF16)
        dyc = _dot_nt(da, wc_v[...])
        cc, cx = cc_ref[...], cx_ref[...]
        xc = cc * cx
        xc_before = jnp.where(step == n_tile - 1, 0.0, ccp_ref[...] * cxp_ref[...])
        ext = jnp.concatenate([xc_before, xc], axis=0)
        x1 = pltpu.roll(ext, 1, 0)[8:]
        x2 = pltpu.roll(ext, 2, 0)[8:]
        w0, w1, w2 = cw_ref[0:1, :], cw_ref[1:2, :], cw_ref[2:3, :]
        dc_ref[0] = (dyc * (w0 * x2 + w1 * x1 + w2 * xc)).astype(BF16)
        dconv = dyc * cb_ref[...]
        dcw_ref[0:1, :] += jnp.sum(dconv * x2, axis=0, keepdims=True)
        dcw_ref[1:2, :] += jnp.sum(dconv * x1, axis=0, keepdims=True)
        dcw_ref[2:3, :] += jnp.sum(dconv * xc, axis=0, keepdims=True)
        after = jnp.concatenate([dconv, head_v[...]], axis=0)
        dxc = w2 * dconv + w1 * pltpu.roll(after, TM + 7, 0)[:TM] + w0 * pltpu.roll(after, TM + 6, 0)[:TM]
        dc_ref[1] = (dxc * cx).astype(BF16)
        dc_ref[2] = (dxc * cc).astype(BF16)
        head_v[...] = dconv[:8]

    return _call(
        body, (dh2, a, b, pcg, pcg, pcg, pcg, pcg, pcg, pcg, b_gate, conv_w, w_mix), grid=(n_tile,), name=name,
        comm=comm,
        in_specs=[rows(D)] * 3 + [cols(3), cols(4), cols(0), cols(1), cols(2), before(1), before(2),
                                  _const_spec((1, 2 * D)), _const_spec((CONV_K, D)), _ANY],
        out_specs=[rows(D)] * 3 + [pl.BlockSpec((2, TM, D), lambda i: (0, n_tile - 1 - i, 0)),
                                   pl.BlockSpec((3, TM, D), lambda i: (0, n_tile - 1 - i, 0)), rows(D),
                                   _const_spec((1, 2 * D)), _const_spec((8, D))],
        out_shape=[jax.ShapeDtypeStruct((t, D), BF16)] * 3
                  + [jax.ShapeDtypeStruct((2, t, D), BF16), jax.ShapeDtypeStruct((3, t, D), BF16),
                     jax.ShapeDtypeStruct((t, D), BF16), jax.ShapeDtypeStruct((1, 2 * D), F32),
                     jax.ShapeDtypeStruct((8, D), F32)],
        scratch_shapes=[pltpu.VMEM((D, D), BF16)] * 3 + [pltpu.VMEM((8, D), F32),
                                                         pltpu.SemaphoreType.DMA((3 * N_DEV,))])


def _inproj_bwd(dconv, dq, dkv, dgp, w_in, h, g, dh_res, name, comm=None):
    t = h.shape[0]

    def body(dc_ref, dq_ref, dkv_ref, dgp_ref, w_hbm, h_ref, g_ref, dres_ref, dh_ref, dg_ref, w_v, sems):
        step = pl.program_id(0)
        _load_resident(step, [(w_hbm, w_v)], sems)
        du = _dot_nt(dq_ref[...], w_v[3])
        for k in range(3):
            du = du + _dot_nt(dc_ref[k], w_v[k])
        for k in range(2):
            du = du + _dot_nt(dkv_ref[k].astype(BF16), w_v[4 + k]) + _dot_nt(dgp_ref[k], w_v[6 + k])
        dx, dg = _rms_bwd_tile(h_ref[...], g_ref[...], du)
        dh_ref[...] = dres_ref[...] + dx
        _accumulate(dg_ref, step, dg)

    return _call(
        body, (dconv, dq, dkv, dgp, w_in, h, g, dh_res), grid=(t // TM,), name=name, comm=comm,
        in_specs=[_blk_row_spec(3, TM, D), _row_spec(TM, D), _blk_row_spec(2, TM, D), _blk_row_spec(2, TM, D), _ANY,
                  _row_spec(TM, D), _const_spec((1, D)), _row_spec(TM, D)],
        out_specs=[_row_spec(TM, D), _const_spec((1, D))],
        out_shape=[jax.ShapeDtypeStruct((t, D), F32), jax.ShapeDtypeStruct((1, D), F32)],
        scratch_shapes=[pltpu.VMEM((N_DEV, D, D), BF16), pltpu.SemaphoreType.DMA((1,))])


def _softmax_rows(s):
    e = jnp.exp(s - jnp.max(s, axis=-1, keepdims=True))
    return e / jnp.sum(e, axis=-1, keepdims=True)


def _cross_pairs(cross_hbm, wq_v, wo_v):
    return _square_pairs(cross_hbm, 0, wq_v) + _square_pairs(cross_hbm, 1, wo_v)


def _cross_fwd(h, g, mem, g_mem, w_ckv, w_cross, name):
    t = h.shape[0]
    m = mem.shape[0]
    scale = X_DH ** -0.5

    def body(h_ref, g_ref, mem_ref, gm_ref, wkv_ref, cross_hbm, hn_ref, qx_ref, o_ref, h3_ref, mn_ref, kv_ref,
             wq_v, wo_v, sems):
        _load_resident(pl.program_id(0), _cross_pairs(cross_hbm, wq_v, wo_v), sems)

        @pl.when(pl.program_id(0) == 0)
        def _():
            mn = _rms_fwd_tile(mem_ref[...], gm_ref[...]).astype(BF16)
            mn_ref[...] = mn
            for j in range(N_DEV):
                kv_ref[j] = _dot(mn, wkv_ref[j]).astype(BF16)

        ht = h_ref[...]
        hn = _rms_fwd_tile(ht, g_ref[...]).astype(BF16)
        hn_ref[...] = hn
        qx = _dot(hn, wq_v[...]).astype(BF16)
        qx_ref[...] = qx
        for hd in range(X_H):
            lo, hi = hd * X_DH, (hd + 1) * X_DH
            p = _softmax_rows(_dot_nt(qx[:, lo:hi], kv_ref[hd]) * scale)
            o_ref[:, lo:hi] = _dot(p.astype(BF16), kv_ref[X_H + hd]).astype(BF16)
        h3_ref[...] = ht + _dot(o_ref[...], wo_v[...])

    return pl.pallas_call(
        body, grid=(t // TM,), name=name,
        in_specs=[_row_spec(TM, D), _const_spec((1, D)), _const_spec((m, D)), _const_spec((1, D)),
                  _const_spec((N_DEV, D, X_DH)), _ANY],
        out_specs=[_row_spec(TM, D)] * 4 + [_const_spec((m, D)), _const_spec((N_DEV, m, X_DH))],
        out_shape=[jax.ShapeDtypeStruct((t, D), BF16)] * 3 + [jax.ShapeDtypeStruct((t, D), F32),
                                                              jax.ShapeDtypeStruct((m, D), BF16),
                                                              jax.ShapeDtypeStruct((N_DEV, m, X_DH), BF16)],
        scratch_shapes=[pltpu.VMEM((D, D), BF16)] * 2 + [pltpu.SemaphoreType.DMA((2 * N_DEV,))],
        compiler_params=_cparams(),
    )(h, g, mem, g_mem, w_ckv, w_cross)


def _cross_bwd(dh3, h, g, qx, kv, mem, g_mem, w_ckv, w_cross, name, comm=None):
    t = h.shape[0]
    m = kv.shape[1]
    scale = X_DH ** -0.5

    def body(dh_ref, h_ref, g_ref, qx_ref, kv_ref, mem_ref, gm_ref, wkv_ref, cross_hbm,
             dhb_ref, dqx_ref, dkv_ref, dh2_ref, dg_ref, dgm_ref, wq_v, wo_v, sems):
        step = pl.program_id(0)
        _load_resident(step, _cross_pairs(cross_hbm, wq_v, wo_v), sems)

        @pl.when(step == 0)
        def _():
            dkv_ref[...] = jnp.zeros_like(dkv_ref)

        dht = dh_ref[...]
        dhb = dht.astype(BF16)
        dhb_ref[...] = dhb
        do = _dot_nt(dhb, wo_v[...]).astype(BF16)
        for hd in range(X_H):
            lo, hi = hd * X_DH, (hd + 1) * X_DH
            qh = qx_ref[:, lo:hi]
            kh = kv_ref[hd]
            p = _softmax_rows(_dot_nt(qh, kh) * scale)
            doh = do[:, lo:hi]
            dp = _dot_nt(doh, kv_ref[X_H + hd])
            ds = (p * (dp - jnp.sum(dp * p, axis=-1, keepdims=True)) * scale).astype(BF16)
            dqx_ref[:, lo:hi] = _dot(ds, kh).astype(BF16)
            dkv_ref[hd] += _dot_tn(ds, qh)
            dkv_ref[X_H + hd] += _dot_tn(p.astype(BF16), doh)
        dhn = _dot_nt(dqx_ref[...], wq_v[...])
        dx, dg = _rms_bwd_tile(h_ref[...], g_ref[...], dhn)
        dh2_ref[...] = dht + dx
        _accumulate(dg_ref, step, dg)

        @pl.when(step == t // TM - 1)
        def _():
            dmn = jnp.zeros((m, D), F32)
            for j in range(N_DEV):
                dmn = dmn + _dot_nt(dkv_ref[j].astype(BF16), wkv_ref[j])
            dgm_ref[...] = _rms_bwd_tile(mem_ref[...], gm_ref[...], dmn)[1]

    return _call(
        body, (dh3, h, g, qx, kv, mem, g_mem, w_ckv, w_cross), grid=(t // TM,), name=name, comm=comm,
        in_specs=[_row_spec(TM, D), _row_spec(TM, D), _const_spec((1, D)), _row_spec(TM, D),
                  _const_spec((N_DEV, m, X_DH)), _const_spec((m, D)), _const_spec((1, D)),
                  _const_spec((N_DEV, D, X_DH)), _ANY],
        out_specs=[_row_spec(TM, D), _row_spec(TM, D), _const_spec((N_DEV, m, X_DH)), _row_spec(TM, D),
                   _const_spec((1, D)), _const_spec((1, D))],
        out_shape=[jax.ShapeDtypeStruct((t, D), BF16), jax.ShapeDtypeStruct((t, D), BF16),
                   jax.ShapeDtypeStruct((N_DEV, m, X_DH), F32), jax.ShapeDtypeStruct((t, D), F32),
                   jax.ShapeDtypeStruct((1, D), F32), jax.ShapeDtypeStruct((1, D), F32)],
        scratch_shapes=[pltpu.VMEM((D, D), BF16)] * 2 + [pltpu.SemaphoreType.DMA((2 * N_DEV,))])


def _adamw_small(small_all, dev, wts, mom1, mom2, name):
    n_slots = small_all.shape[0]
    conv_rows, conv_cols = wts["conv_w"].shape
    shapes = {n: (1, wts[n].size) for n in GAINS + ("b_gate",)}
    shapes["conv_w"] = (conv_rows, conv_cols)
    n_par = len(SMALL)
    gate_row, conv_row = len(GAINS), len(GAINS) + 2

    def body(dev_ref, all_ref, cols_ref, *refs):
        params, loss_ref, outs, acc = refs[:3 * n_par], refs[3 * n_par], refs[3 * n_par + 1:-1], refs[-1]
        total = all_ref[0]
        for k in range(1, n_slots):
            total = total + all_ref[k]
        acc[...] = total
        loss_ref[...] = acc[LOSS_ROW:LOSS_ROW + 1, 0:1]
        for i, n in enumerate(SMALL):
            w_ref, m_ref, v_ref = params[3 * i:3 * i + 3]
            res = outs[4 * i:4 * i + 4]
            if n == "conv_w":
                gt = cols_ref[0, conv_row:conv_row + conv_rows, :]
                for k in range(1, n_slots):
                    gt = gt + cols_ref[k, conv_row:conv_row + conv_rows, :]
                _adamw_apply(gt, w_ref, m_ref, v_ref, *res)
            elif n == "b_gate":
                for half in range(2):
                    cols = slice(half * D, (half + 1) * D)
                    _adamw_apply(acc[gate_row + half:gate_row + half + 1, :], w_ref.at[:, cols], m_ref.at[:, cols],
                                 v_ref.at[:, cols], *[r.at[:, cols] for r in res])
            else:
                _adamw_apply(acc[i:i + 1, :], w_ref, m_ref, v_ref, *res)

    def whole(shape):
        return pl.BlockSpec(shape, lambda i, dev_ref: (0,) * len(shape))

    operands = [a.reshape(shapes[n]) for n in SMALL for a in (wts[n], mom1[n], mom2[n])]
    out_shapes = [shapes[n] for n in SMALL for _ in range(4)]
    outs = pl.pallas_call(
        body, name=name,
        grid_spec=pltpu.PrefetchScalarGridSpec(
            num_scalar_prefetch=1, grid=(1,),
            in_specs=[whole(small_all.shape),
                      pl.BlockSpec((n_slots, SMALL_R, conv_cols), lambda i, dev_ref: (0, 0, dev_ref[0]))]
            + [whole(a.shape) for a in operands],
            out_specs=[whole((1, 1))] + [whole(s) for s in out_shapes],
            scratch_shapes=[pltpu.VMEM((SMALL_R, D), F32)]),
        out_shape=[jax.ShapeDtypeStruct((1, 1), F32)] + [jax.ShapeDtypeStruct(s, F32) for s in out_shapes],
        compiler_params=_cparams(),
    )(dev, small_all, small_all, *operands)
    small = {n: tuple(o.reshape(wts[n].shape) for o in outs[1 + 4 * i:5 + 4 * i]) for i, n in enumerate(SMALL)}
    return outs[0].reshape(()), small


def _adamw_own(w, land, own, chip, m, v, name, row_block=0, token=None):
    r, c = w.shape
    tr = _pick_tile(r, (256, 352, 128))
    off = row_block * (r // tr)

    def body(chip_ref, w_ref, land_ref, own_ref, m_ref, v_ref, *rest):
        mine = own_ref[0].astype(F32)
        gt = jnp.where(chip_ref[0] == 0, mine, land_ref[0].astype(F32))
        for k in range(1, N_CHIP):
            gt = gt + jnp.where(chip_ref[0] == k, mine, land_ref[k].astype(F32))
        if token is None:
            _adamw_apply(gt, w_ref, m_ref, v_ref, *rest)
        else:
            before, *outs, later = rest
            _adamw_apply(gt + before[0:1, 0:1], w_ref, m_ref, v_ref, *outs)
            later[...] = before[...]

    spec = pl.BlockSpec((tr, c), lambda i, chip_ref: (i, 0))
    in_specs = [spec, pl.BlockSpec((N_CHIP, tr, c), lambda i, chip_ref: (0, i + off, 0)),
                pl.BlockSpec((1, tr, c), lambda i, chip_ref: (chip_ref[0], i + off, 0)), spec, spec]
    operands = (chip, w, land, own, m, v)
    out_specs, out_shape = [spec] * 4, [jax.ShapeDtypeStruct((r, c), F32)] * 4
    if token is not None:
        token_spec = pl.BlockSpec(token.shape, lambda i, chip_ref: (0, 0))
        in_specs.append(token_spec)
        operands += (token,)
        out_specs, out_shape = out_specs + [token_spec], out_shape + [jax.ShapeDtypeStruct(token.shape, token.dtype)]
    return pl.pallas_call(
        body, name=name,
        grid_spec=pltpu.PrefetchScalarGridSpec(
            num_scalar_prefetch=1, grid=(r // tr,), in_specs=in_specs, out_specs=out_specs),
        out_shape=out_shape,
        compiler_params=_cparams(),
    )(*operands)


def _adamw_apply(gt, w_ref, m_ref, v_ref, g_ref, d_ref, nm_ref, nv_ref):
    g_ref[...] = gt
    nm = ADAM_B1 * m_ref[...] + (1.0 - ADAM_B1) * gt
    nv = ADAM_B2 * v_ref[...] + (1.0 - ADAM_B2) * jnp.square(gt)
    m_hat = nm / (1.0 - ADAM_B1 ** ADAM_STEP)
    v_hat = nv / (1.0 - ADAM_B2 ** ADAM_STEP)
    d_ref[...] = -ADAM_LR * (m_hat / (jnp.sqrt(v_hat) + ADAM_EPS) + ADAM_WD * w_ref[...])
    nm_ref[...] = nm
    nv_ref[...] = nv


def _mesh_pos():
    return lax.axis_index("x"), lax.axis_index("y"), lax.axis_index("c")


def _both(first, second):
    n_in, n_out, n_sem = len(first.inputs), len(first.out_shapes), len(first.sem_shapes)

    def run(round_name):
        def both(in_refs, out_refs, sems):
            getattr(first, round_name)(in_refs[:n_in], out_refs[:n_out], sems[:n_sem])
            getattr(second, round_name)(in_refs[n_in:], out_refs[n_out:], sems[n_sem:])
        return both

    return types.SimpleNamespace(
        inputs=first.inputs + second.inputs, out_shapes=first.out_shapes + second.out_shapes,
        sem_shapes=first.sem_shapes + second.sem_shapes, start=run("start"), middle=run("middle"),
        finish=run("finish"))


def _no_round(in_refs, out_refs, sems):
    pass


def _after(token):
    return types.SimpleNamespace(inputs=[token], out_shapes=[], sem_shapes=[], start=_no_round, middle=_no_round,
                                 finish=_no_round)


def _run_exchange(comm, name):
    c_in, c_out = len(comm.inputs), len(comm.out_shapes)

    def body(*refs):
        cins, couts, sems = refs[:c_in], refs[c_in:c_in + c_out], refs[c_in + c_out:]
        comm.start(cins, couts, sems)
        comm.middle(cins, couts, sems)
        comm.finish(cins, couts, sems)

    return list(pl.pallas_call(
        body, name=name, out_shape=list(comm.out_shapes),
        in_specs=[_ANY] * c_in, out_specs=[_ANY] * c_out, scratch_shapes=list(comm.sem_shapes),
    )(*comm.inputs))


def _gather_exchange(shards):
    n_arr = len(shards)

    def plan(x_refs, out_refs, sems):
        send_sems, recv_sems, local_sems = sems[:3]
        stage = sems[3:]
        x, y, c = _mesh_pos()
        me, sibling = (x, y, c), (x, y, 1 - c)
        xn, yn, diag = (1 - x, y), (x, 1 - y), (1 - x, 1 - y)

        def slot(a, px, py, pc, half=None):
            ref = out_refs[a].at[4 * px + 2 * py + pc]
            if half is None:
                return ref
            rows = shards[a].shape[0] // 2
            return ref.at[half * rows:(half + 1) * rows]

        def copy(a, k, block, to, half=None, src=None):
            dst = slot(a, *block, half)
            return pltpu.make_async_remote_copy(
                src_ref=dst if src is None else src, dst_ref=dst,
                send_sem=send_sems.at[a, k], recv_sem=recv_sems.at[a, k],
                device_id=to, device_id_type=pl.DeviceIdType.MESH)

        return types.SimpleNamespace(
            me=me, sibling=sibling, xn=xn, yn=yn, diag=diag, c=c, copy=copy,
            mine_in=[pltpu.make_async_copy(x_refs[a], stage[a], local_sems.at[a, 0]) for a in range(n_arr)],
            mine_out=[pltpu.make_async_copy(stage[a], slot(a, *me), local_sems.at[a, 1]) for a in range(n_arr)],
            first=[cp for a in range(n_arr) for cp in (
                copy(a, 0, me, sibling, src=x_refs[a]), copy(a, 1, me, (*xn, c), src=x_refs[a]),
                copy(a, 2, me, (*yn, c), src=x_refs[a]))],
            second=lambda a: (copy(a, 3, (*xn, c), (*yn, c), half=0), copy(a, 5, (*xn, c), sibling),
                              copy(a, 4, (*yn, c), (*xn, c), half=1), copy(a, 6, (*yn, c), sibling)),
            third=lambda a: (copy(a, 7, (*diag, c), sibling, half=0), copy(a, 8, (*diag, c), sibling, half=1)))

    def start(x_refs, out_refs, sems):
        p = plan(x_refs, out_refs, sems)
        for cp in p.first + p.mine_in:
            cp.start()
        for cp_in, cp_out in zip(p.mine_in, p.mine_out):
            cp_in.wait()
            cp_out.start()

    def middle(x_refs, out_refs, sems):
        p = plan(x_refs, out_refs, sems)
        for a in range(n_arr):
            to_yn, x_to_sib, to_xn, y_to_sib = p.second(a)
            p.copy(a, 1, (*p.xn, p.c), p.me).wait_recv()
            to_yn.start()
            x_to_sib.start()
            p.copy(a, 2, (*p.yn, p.c), p.me).wait_recv()
            to_xn.start()
            y_to_sib.start()

    def finish(x_refs, out_refs, sems):
        p = plan(x_refs, out_refs, sems)
        for a in range(n_arr):
            half0_to_sib, half1_to_sib = p.third(a)
            p.copy(a, 3, (*p.diag, p.c), p.me, half=0).wait_recv()
            half0_to_sib.start()
            p.copy(a, 4, (*p.diag, p.c), p.me, half=1).wait_recv()
            half1_to_sib.start()
        other = 1 - p.c
        for a in range(n_arr):
            p.copy(a, 0, p.sibling, p.me).wait_recv()
            p.copy(a, 5, (*p.xn, other), p.me).wait_recv()
            p.copy(a, 6, (*p.yn, other), p.me).wait_recv()
            p.copy(a, 7, (*p.diag, other), p.me, half=0).wait_recv()
            p.copy(a, 8, (*p.diag, other), p.me, half=1).wait_recv()
        for cp in p.first:
            cp.wait_send()
        for a in range(n_arr):
            for cp in p.second(a) + p.third(a):
                cp.wait_send()
        for cp in p.mine_out:
            cp.wait()

    return types.SimpleNamespace(
        inputs=list(shards), start=start, middle=middle, finish=finish,
        out_shapes=[jax.ShapeDtypeStruct((N_DEV,) + s.shape, s.dtype) for s in shards],
        sem_shapes=[pltpu.SemaphoreType.DMA((n_arr, 9)), pltpu.SemaphoreType.DMA((n_arr, 9)),
                    pltpu.SemaphoreType.DMA((n_arr, 2))] + [pltpu.VMEM(s.shape, s.dtype) for s in shards])


def _pair_exchange(grads):
    n_arr = len(grads)

    def plan(g_refs, land_refs, sems):
        send_sems, recv_sems = sems
        x, y, c = _mesh_pos()
        return [pltpu.make_async_remote_copy(
            src_ref=g_refs[a].at[2 * k + 1 - c], dst_ref=land_refs[a].at[k],
            send_sem=send_sems.at[a, k], recv_sem=recv_sems.at[a, k],
            device_id=(x, y, 1 - c), device_id_type=pl.DeviceIdType.MESH)
            for a in range(n_arr) for k in range(N_CHIP)]

    def start(g_refs, land_refs, sems):
        for cp in plan(g_refs, land_refs, sems):
            cp.start()

    def finish(g_refs, land_refs, sems):
        for cp in plan(g_refs, land_refs, sems):
            cp.wait()

    return types.SimpleNamespace(
        inputs=list(grads), start=start, middle=_no_round, finish=finish,
        out_shapes=[jax.ShapeDtypeStruct((N_CHIP,) + g.shape[1:], g.dtype) for g in grads],
        sem_shapes=[pltpu.SemaphoreType.DMA((n_arr, N_CHIP)), pltpu.SemaphoreType.DMA((n_arr, N_CHIP))])


def _direct_gather(shard):
    def plan(x_ref, out_ref, sems):
        send_sems, recv_sems, local_sem = sems
        x, y, c = _mesh_pos()
        mine = out_ref.at[4 * x + 2 * y + c]
        local = pltpu.make_async_copy(x_ref, mine, local_sem.at[0])
        remote = []
        for k in range(N_DEV - 1):
            peer = tuple(1 - pos if (k + 1) >> bit & 1 else pos for pos, bit in ((x, 2), (y, 1), (c, 0)))
            remote.append(pltpu.make_async_remote_copy(
                src_ref=x_ref, dst_ref=mine, send_sem=send_sems.at[k], recv_sem=recv_sems.at[k],
                device_id=peer, device_id_type=pl.DeviceIdType.MESH))
        return local, remote

    def start(x_refs, out_refs, sems):
        local, remote = plan(x_refs[0], out_refs[0], sems)
        for cp in [local] + remote:
            cp.start()

    def finish(x_refs, out_refs, sems):
        local, remote = plan(x_refs[0], out_refs[0], sems)
        for cp in remote:
            cp.wait_recv()
        for cp in remote:
            cp.wait_send()
        local.wait()

    return types.SimpleNamespace(
        inputs=[shard], start=start, middle=_no_round, finish=finish,
        out_shapes=[jax.ShapeDtypeStruct((N_DEV,) + shard.shape, shard.dtype)],
        sem_shapes=[pltpu.SemaphoreType.DMA((N_DEV - 1,)), pltpu.SemaphoreType.DMA((N_DEV - 1,)),
                    pltpu.SemaphoreType.DMA((1,))])


def _chip_exchange(parts):
    n_arr = len(parts)

    def plan(p_refs, land_refs, sems):
        send_sems, recv_sems, local_sems = sems
        x, y, c = _mesh_pos()
        my_chip = 2 * x + y
        chips = [(1 - x, y), (x, 1 - y), (1 - x, 1 - y)]
        local = [pltpu.make_async_copy(p_refs[a].at[my_chip], land_refs[a].at[my_chip], local_sems.at[a])
                 for a in range(n_arr)]

        def copy(a, k, src_slot, dst_slot, px, py):
            return pltpu.make_async_remote_copy(
                src_ref=p_refs[a].at[src_slot], dst_ref=land_refs[a].at[dst_slot],
                send_sem=send_sems.at[a, k], recv_sem=recv_sems.at[a, k],
                device_id=(px, py, c), device_id_type=pl.DeviceIdType.MESH)

        sends = [copy(a, k, 2 * px + py, my_chip, px, py) for a in range(n_arr) for k, (px, py) in enumerate(chips)]
        arrivals = [copy(a, k, my_chip, 2 * px + py, px, py) for a in range(n_arr)
                    for k, (px, py) in enumerate(chips)]
        return local, sends, arrivals

    def start(p_refs, land_refs, sems):
        local, sends, _ = plan(p_refs, land_refs, sems)
        for cp in local + sends:
            cp.start()

    def finish(p_refs, land_refs, sems):
        local, sends, arrivals = plan(p_refs, land_refs, sems)
        for cp in arrivals:
            cp.wait_recv()
        for cp in sends:
            cp.wait_send()
        for cp in local:
            cp.wait()

    return types.SimpleNamespace(
        inputs=list(parts), start=start, middle=_no_round, finish=finish,
        out_shapes=[jax.ShapeDtypeStruct(p.shape, p.dtype) for p in parts],
        sem_shapes=[pltpu.SemaphoreType.DMA((n_arr, 3)), pltpu.SemaphoreType.DMA((n_arr, 3)),
                    pltpu.SemaphoreType.DMA((n_arr,))])


_HBM = pl.BlockSpec(memory_space=pltpu.HBM)
_SEM = pl.BlockSpec(memory_space=pltpu.SEMAPHORE)
_DATAFLOW = pltpu.SideEffectType.DATAFLOW_SIDE_EFFECTING


def _chip_copies(p_refs, land_refs, send_sems, recv_sems):
    x, y, c = _mesh_pos()
    my_chip = 2 * x + y
    chips = [(1 - x, y), (x, 1 - y), (1 - x, 1 - y)]
    return [pltpu.make_async_remote_copy(
        src_ref=p_refs[a].at[2 * px + py], dst_ref=land_refs[a].at[my_chip],
        send_sem=send_sems[3 * a + k], recv_sem=recv_sems[3 * a + k],
        device_id=(px, py, c), device_id_type=pl.DeviceIdType.MESH)
        for a in range(len(p_refs)) for k, (px, py) in enumerate(chips)]


def _chip_exchange_begin(parts, name):
    n_arr = len(parts)
    n_buf, n_copy = 2 * n_arr, 3 * n_arr
    lands = [lax.empty(p.shape, p.dtype) for p in parts]

    def body(*refs):
        p_refs, land_refs = refs[:n_arr], refs[n_arr:n_buf]
        send_sems, recv_sems, token = refs[n_buf:n_buf + n_copy], refs[n_buf + n_copy:n_buf + 2 * n_copy], refs[-1]
        for cp in _chip_copies(p_refs, land_refs, send_sems, recv_sems):
            cp.start()
        token[...] = jnp.zeros_like(token)

    bufs = list(parts) + list(lands)
    outs = pl.pallas_call(
        body, name=name,
        out_shape=(*[pltpu.SemaphoreType.DMA(())] * (2 * n_copy), *[pltpu.HBM(b.shape, b.dtype) for b in bufs],
                   jax.ShapeDtypeStruct((8, 128), F32)),
        in_specs=[_HBM] * n_buf,
        out_specs=(*[_SEM] * (2 * n_copy), *[_HBM] * n_buf, pl.BlockSpec(memory_space=pltpu.VMEM)),
        input_output_aliases={i: 2 * n_copy + i for i in range(n_buf)},
        compiler_params=pltpu.CompilerParams(has_side_effects=_DATAFLOW),
    )(*[pltpu.with_memory_space_constraint(b, pltpu.HBM) for b in bufs])
    sems = list(outs[:2 * n_copy])
    thru = list(outs[2 * n_copy:2 * n_copy + n_buf])
    return types.SimpleNamespace(send_sems=sems[:n_copy], recv_sems=sems[n_copy:], parts=thru[:n_arr],
                                 lands=thru[n_arr:], token=outs[-1])


def _chip_exchange_end(flight, after, name):
    send_sems, recv_sems, parts, lands = flight.send_sems, flight.recv_sems, flight.parts, flight.lands
    n_arr = len(parts)
    n_buf, n_copy = 2 * n_arr, 3 * n_arr

    def body(*refs):
        p_refs, land_refs = refs[:n_arr], refs[n_arr:n_buf]
        sems = refs[n_buf:n_buf + 2 * n_copy]
        for cp in _chip_copies(p_refs, land_refs, sems[:n_copy], sems[n_copy:]):
            cp.wait_send()
            cp.wait_recv()

    bufs = list(parts) + list(lands)
    outs = pl.pallas_call(
        body, name=name, out_shape=tuple(pltpu.HBM(b.shape, b.dtype) for b in bufs),
        in_specs=[_HBM] * n_buf + [_SEM] * (2 * n_copy) + [_ANY], out_specs=tuple([_HBM] * n_buf),
        input_output_aliases={i: i for i in range(n_buf)},
        compiler_params=pltpu.CompilerParams(has_side_effects=_DATAFLOW),
    )(*bufs, *send_sems, *recv_sems, after)
    return list(outs[:n_arr]), list(outs[n_arr:])


def _row_tile(r, cap=640):
    best = None
    for cand in range(16, min(r, cap) + 1, 16):
        if r % cand == 0:
            best = cand
    return best if best is not None else r


def _pair_sum(gs, landeds, core, name):
    tiles = [_row_tile(g.shape[1]) for g in gs]
    counts = [g.shape[1] // tr for g, tr in zip(gs, tiles)]
    n_arr = len(gs)

    def body(core_ref, *refs):
        for a in range(n_arr):
            mine, theirs, out = refs[2 * a], refs[2 * a + 1], refs[2 * n_arr + a]
            out[0] = (mine[0].astype(F32) + theirs[0].astype(F32)).astype(out.dtype)

    in_specs, out_specs, operands = [], [], []
    for g, landed, tr, count in zip(gs, landeds, tiles, counts):
        c_dim = g.shape[2]
        last = count - 1
        in_specs += [pl.BlockSpec((1, tr, c_dim),
                                  lambda k, i, core_ref, last=last: (2 * k + core_ref[0], jnp.minimum(i, last), 0)),
                     pl.BlockSpec((1, tr, c_dim), lambda k, i, core_ref, last=last: (k, jnp.minimum(i, last), 0))]
        out_specs.append(pl.BlockSpec((1, tr, c_dim), lambda k, i, core_ref, last=last: (k, jnp.minimum(i, last), 0)))
        operands += [g, landed]
    return list(pl.pallas_call(
        body, name=name,
        grid_spec=pltpu.PrefetchScalarGridSpec(
            num_scalar_prefetch=1, grid=(N_CHIP, max(counts)), in_specs=in_specs, out_specs=out_specs),
        out_shape=[jax.ShapeDtypeStruct((N_CHIP,) + g.shape[1:], g.dtype) for g in gs],
        compiler_params=_cparams(2),
    )(core, *operands))


GAINS = ("g_ffn1", "g_mix", "g_cross", "g_mem", "g_ffn2", "g_final")
SMALL = GAINS + ("b_gate", "conv_w")
SMALL_R = 16
LOSS_ROW = 11
WEIGHT_ORDER = ("g_ffn1", "w_ffn1_gu", "w_ffn1_down", "g_mix", "w_in", "b_gate", "conv_w", "w_conv_out",
                "w_attn_out", "w_o", "g_cross", "g_mem", "w_cq", "w_ckv", "w_co", "g_ffn2", "w_ffn2_gu",
                "w_ffn2_down", "g_final")
GU_NAMES = ("w_ffn1_gu", "w_ffn2_gu")


def _pack_small(vals, conv_rows):
    rows = [vals[n].reshape(1, D) for n in GAINS] + [vals["b_gate"].reshape(2, D), conv_rows.reshape(CONV_K, D)]
    used = len(GAINS) + 2 + CONV_K
    return jnp.concatenate(rows + [jnp.zeros((SMALL_R - used, D), F32)], axis=0)


def _exchange_shards(wts):
    out = {n: jnp.pad(wts[n].T.astype(BF16), ((0, FF_PAD - FF_BLK), (0, 0))) for n in GU_NAMES}
    for n in ("w_ckv", "w_in", "w_ffn1_down", "w_ffn2_down"):
        out[n] = wts[n].astype(BF16)
    out["mix"] = jnp.concatenate([wts[n].astype(BF16) for n in MIX_MATS], axis=0)
    out["cross"] = jnp.concatenate([wts[n].astype(BF16) for n in CROSS_MATS], axis=0)
    return out


def _reduce_group(grads, landed, core, names):
    return _pair_sum(grads, landed, core, "grads_pair_sum_" + "_".join(names))


def _step(x, mem, target, sh, conv_pad, gains, b_gate, core):
    wg1, wd1, conv_all = _run_exchange(_gather_exchange([sh["w_ffn1_gu"], sh["w_ffn1_down"], conv_pad]), "gather_ffn1")
    conv_w = conv_all[:, :CONV_K, :].transpose(1, 0, 2).reshape(CONV_K, D)
    (n1, gate1, up1, act1, h1), (w_in,) = _ffn_fwd(
        x, gains["g_ffn1"], wg1, wd1, "ffn1_fwd", comm=_gather_exchange([sh["w_in"]]))
    (u, pcg, qkv, yc), (w_mix, wd2) = _inproj_fwd(h1, gains["g_mix"], w_in, conv_w, "inproj_fwd",
                                                  comm=_gather_exchange([sh["mix"], sh["w_ffn2_down"]]))
    (ysb, ctot), (w_cross, w_ckv, wg2) = _sb_fwd(
        qkv, "sb_fwd", comm=_gather_exchange([sh["cross"], sh["w_ckv"], sh["w_ffn2_gu"]]))
    (a_mix, b_mix, merged, h2), _ = _mix_out_fwd(yc, ysb, pcg, b_gate, h1, w_mix, "mix_out_fwd")
    hn, qx, o_x, h3, mn, kv = _cross_fwd(h2, gains["g_cross"], mem, gains["g_mem"], w_ckv, w_cross, "cross_fwd")
    (n4, gate2, up2, act2, dh4, loss, dg_final), _ = _ffn_fwd(h3, gains["g_ffn2"], wg2, wd2, "ffn2_fwd",
                                                              head=(gains["g_final"], target))

    gs = {"g_final": dg_final}
    (dgu2, dh4b, dh3, gs["g_ffn2"]), _ = _ffn_bwd(dh4, h3, gains["g_ffn2"], gate2, up2, wg2, wd2, "ffn2_bwd")
    grads_a = [_mm_tn_rows(dgu2, n4, FF_PAD, "dw_ffn2_gu"),
               _mm_tn_rows(act2, dh4b, FF_BLK, "dw_ffn2_down").reshape(N_DEV, DOWN_ROWS, D)]
    names_a = ["w_ffn2_gu", "w_ffn2_down"]
    (dh3b, dqx, dkv, dh2, gs["g_cross"], gs["g_mem"]), _ = _cross_bwd(
        dh3, h2, gains["g_cross"], qx, kv, mem, gains["g_mem"], w_ckv, w_cross, "cross_bwd")
    grads_b = [_mm_tn_cols(mn, dkv, "dw_ckv"), _mm_tn_squares([(hn, dqx), (o_x, dh3b)], "dw_cross")]
    names_b = ["w_ckv", "cross"]
    (dh2b, da_mix, db_mix, dgp, dconv, dysb, gs["b_gate"], gs["conv_w"]), landed_ab = _mix_out_bwd(
        dh2, a_mix, b_mix, pcg, b_gate, conv_w, w_mix, "mix_out_bwd", comm=_pair_exchange(grads_a + grads_b))
    sums_ab = _reduce_group(grads_a + grads_b, landed_ab, core, names_a + names_b)
    grads_c = [_mm_tn_squares([(yc, da_mix), (ysb, db_mix), (merged, dh2b)], "dw_mix")]
    flight_ab = _chip_exchange_begin(sums_ab, "grads_to_chips_early_begin")
    (dq, dkv_sb), _ = _sb_bwd(qkv, dysb, ctot, flight_ab.token, "sb_bwd")
    grads_d = [_mm_tn_cols_many(u, [dconv, dq[None], dkv_sb, dgp], "dw_in")]
    (dh1, gs["g_mix"]), landed_cd = _inproj_bwd(dconv, dq, dkv_sb, dgp, w_in, h1, gains["g_mix"], dh2, "inproj_bwd",
                                                comm=_pair_exchange(grads_c + grads_d))
    sums_cd = _reduce_group(grads_c + grads_d, landed_cd, core, ["mix", "w_in"])
    flight_d = _chip_exchange_begin(sums_cd, "grads_to_chips_w_in_begin")
    (dgu1, dh1b, dx, gs["g_ffn1"]), _ = _ffn_bwd(dh1, x, gains["g_ffn1"] + flight_d.token[0, 0], gate1, up1, wg1, wd1,
                                                 "ffn1_bwd")
    dw_gu1 = _mm_tn_rows(dgu1, n1, FF_PAD, "dw_ffn1_gu")
    small_mine = _pack_small({n: gs[n] for n in GAINS + ("b_gate",)}, gs["conv_w"][:CONV_K])
    small_mine = small_mine.at[LOSS_ROW, 0].set(loss[0, 0])
    dw_down1, (landed_gu1, small_all) = _mm_tn_rows(
        act1, dh1b, FF_BLK, "dw_ffn1_down", comm=_both(_pair_exchange([dw_gu1]), _direct_gather(small_mine)))
    flight_gu1 = _chip_exchange_begin(_reduce_group([dw_gu1], [landed_gu1], core, ["w_ffn1_gu"]),
                                      "grads_to_chips_ffn1_gu_begin")
    grads_down1 = [dw_down1.reshape(N_DEV, DOWN_ROWS, D)]
    landed_down1 = _run_exchange(_both(_pair_exchange(grads_down1), _after(flight_gu1.token)),
                                 "grads_to_sibling_ffn1_down")
    flight_down1 = _chip_exchange_begin(_reduce_group(grads_down1, landed_down1, core, ["w_ffn1_down"]),
                                        "grads_to_chips_ffn1_down_begin")
    flights = [(names_a + names_b, flight_ab), (["mix", "w_in"], flight_d), (["w_ffn1_gu"], flight_gu1),
               (["w_ffn1_down"], flight_down1)]
    return dx, flights, small_all


def kernel(x, mem, g_ffn1, w_ffn1_gu, w_ffn1_down, g_mix, w_in, b_gate, conv_w, w_conv_out, w_attn_out, w_o, g_cross, g_mem, w_cq, w_ckv, w_co, g_ffn2, w_ffn2_gu, w_ffn2_down, g_final, loss_target, m_g_ffn1, m_w_ffn1_gu, m_w_ffn1_down, m_g_mix, m_w_in, m_b_gate, m_conv_w, m_w_conv_out, m_w_attn_out, m_w_o, m_g_cross, m_g_mem, m_w_cq, m_w_ckv, m_w_co, m_g_ffn2, m_w_ffn2_gu, m_w_ffn2_down, m_g_final, v_g_ffn1, v_w_ffn1_gu, v_w_ffn1_down, v_g_mix, v_w_in, v_b_gate, v_conv_w, v_w_conv_out, v_w_attn_out, v_w_o, v_g_cross, v_g_mem, v_w_cq, v_w_ckv, v_w_co, v_g_ffn2, v_w_ffn2_gu, v_w_ffn2_down, v_g_final):
    args = locals()
    wts = {n: args[n] for n in WEIGHT_ORDER}
    mom1 = {n: args["m_" + n] for n in WEIGHT_ORDER}
    mom2 = {n: args["v_" + n] for n in WEIGHT_ORDER}
    cx, cy, cc = _mesh_pos()
    dev = 4 * cx + 2 * cy + cc
    conv_cols = D // N_DEV

    conv_pad = jnp.concatenate([conv_w, jnp.zeros((SMALL_R - CONV_K, conv_cols), F32)], axis=0)
    gains = {n: wts[n].reshape(1, D) for n in GAINS}
    dx, flights, small_all = _step(x[0], mem[0], loss_target[0], _exchange_shards(wts), conv_pad, gains,
                                 b_gate.reshape(1, 2 * D), cc.reshape(1).astype(jnp.int32))

    grads, delta, new_m, new_v = {}, {}, {}, {}

    def operands(n, transposed):
        trio = (wts[n], mom1[n], mom2[n])
        return tuple(a.T for a in trio) if transposed else trio

    def record(n, res, transposed):
        grads[n], delta[n], new_m[n], new_v[n] = [r.T for r in res] if transposed else res

    early = [("w_ffn2_gu", "w_ffn2_gu", 0, True), ("w_ffn2_down", "w_ffn2_down", 0, False),
             ("w_ckv", "w_ckv", 0, False), ("w_in", "w_in", 0, False)]
    early += [(n, "mix", k, False) for k, n in enumerate(MIX_MATS)]
    early += [(n, "cross", k, False) for k, n in enumerate(CROSS_MATS)]
    chip = (2 * cx + cy).reshape(1).astype(jnp.int32)
    (names_early, flight_early), (names_w_in, flight_w_in), *last_flights = flights
    token = last_flights[-1][1].token
    own, land = {}, {}
    for names, flight, tag in ((names_early, flight_early, "early"), (names_w_in, flight_w_in, "w_in")):
        own_parts, landed = _chip_exchange_end(flight, token, "grads_to_chips_%s_end" % tag)
        own.update(zip(names, own_parts))
        land.update(zip(names, landed))
    for n, buf, row_block, transposed in early:
        w, m1, m2 = operands(n, transposed)
        *res, token = _adamw_own(w, land[buf], own[buf], chip, m1, m2, "adamw_" + n, row_block, token)
        record(n, res, transposed)

    for ((n,), flight), transposed in zip(last_flights, (True, False)):
        (own_n,), (land_n,) = _chip_exchange_end(flight, token, "grads_to_chips_%s_end" % n)
        w, m1, m2 = operands(n, transposed)
        *res, token = _adamw_own(w, land_n, own_n, chip, m1, m2, "adamw_" + n, token=token)
        record(n, res, transposed)

    loss, small = _adamw_small(small_all, dev.reshape(1).astype(jnp.int32), wts, mom1, mom2, "adamw_small")
    for n in SMALL:
        grads[n], delta[n], new_m[n], new_v[n] = small[n]

    return (loss, dx[None], *[grads[n] for n in WEIGHT_ORDER], *[delta[n] for n in WEIGHT_ORDER],
            *[new_m[n] for n in WEIGHT_ORDER], *[new_v[n] for n in WEIGHT_ORDER])
```

```python
import types

import jax
import jax.numpy as jnp
from jax import lax
from jax.experimental import pallas as pl
from jax.experimental.pallas import tpu as pltpu

F32 = jnp.float32
BF16 = jnp.bfloat16

D = 1024
DFF = 2816
SB_H = 8
SB_DH = 128
X_H = 4
X_DH = 256
CONV_K = 3
RMS_EPS = 1e-6
N_DEV = 8
N_CHIP = 4
SQ_ROWS = D // N_DEV

ADAM_LR = 0.001
ADAM_B1 = 0.9
ADAM_B2 = 0.999
ADAM_EPS = 1e-08
ADAM_WD = 0.01
ADAM_STEP = 10

TM = 256
TQ = 512
TK = 256
SB_HPS = 2
VMEM_LIMIT = 56 << 20

FF_BLK = DFF // 4
FF_PAD = 768
FF_SUB = 256
DOWN_ROWS = DFF // N_DEV

MIX_MATS = ("w_conv_out", "w_attn_out", "w_o")
CROSS_MATS = ("w_cq", "w_co")

_ANY = pl.BlockSpec(memory_space=pl.ANY)


def _cparams(n_axes=1):
    return pltpu.CompilerParams(
        dimension_semantics=("arbitrary",) * n_axes, vmem_limit_bytes=VMEM_LIMIT)


def _row_spec(tm, n):
    return pl.BlockSpec((tm, n), lambda i: (i, 0))


def _blk_row_spec(nb, tm, n):
    return pl.BlockSpec((nb, tm, n), lambda i: (0, i, 0))


def _const_spec(shape):
    zeros = (0,) * len(shape)
    return pl.BlockSpec(shape, lambda i: zeros)


def _dot(a, b):
    return jnp.dot(a, b, preferred_element_type=F32)


def _dot_nt(a, b):
    return lax.dot_general(a, b, (((1,), (1,)), ((), ())), preferred_element_type=F32)


def _dot_tn(a, b):
    return lax.dot_general(a, b, (((0,), (0,)), ((), ())), preferred_element_type=F32)


def _sigmoid(x):
    return 1.0 / (1.0 + jnp.exp(-x))


def _call(body, operands, *, grid, in_specs, out_specs, out_shape, scratch_shapes, name, comm=None):
    n_in, n_out, n_sc = len(in_specs), len(out_specs), len(scratch_shapes)
    if comm is None:
        outs = pl.pallas_call(
            body, grid=grid, name=name, in_specs=in_specs, out_specs=out_specs, out_shape=out_shape,
            scratch_shapes=scratch_shapes, compiler_params=_cparams(len(grid)))(*operands)
        return list(outs), []
    c_in, c_out, c_sem = len(comm.inputs), len(comm.out_shapes), len(comm.sem_shapes)

    def hosted(*refs):
        bounds = [0, n_in, c_in, n_out, c_out, n_sc, c_sem]
        parts, pos = [], 0
        for k in bounds[1:]:
            parts.append(refs[pos:pos + k])
            pos += k
        ins, cins, outs, couts, scr, sems = parts
        step, n_steps = pl.program_id(0), grid[0]
        for ax in range(1, len(grid)):
            step, n_steps = step * grid[ax] + pl.program_id(ax), n_steps * grid[ax]

        @pl.when(step == 0)
        def _():
            comm.start(cins, couts, sems)

        @pl.when(step == (2 * n_steps) // 3)
        def _():
            comm.middle(cins, couts, sems)

        body(*ins, *outs, *scr)

        @pl.when(step == n_steps - 1)
        def _():
            comm.finish(cins, couts, sems)

    res = pl.pallas_call(
        hosted, grid=grid, name=name, in_specs=list(in_specs) + [_ANY] * c_in,
        out_specs=list(out_specs) + [_ANY] * c_out, out_shape=list(out_shape) + list(comm.out_shapes),
        scratch_shapes=list(scratch_shapes) + list(comm.sem_shapes),
        compiler_params=_cparams(len(grid)))(*operands, *comm.inputs)
    return list(res[:n_out]), list(res[n_out:])


def _load_resident(step, pairs, sems):
    @pl.when(step == 0)
    def _():
        copies = [pltpu.make_async_copy(src, dst, sems.at[k]) for k, (src, dst) in enumerate(pairs)]
        for cp in copies:
            cp.start()
        for cp in copies:
            cp.wait()


def _square_pairs(buf_hbm, index, dst):
    off = index * SQ_ROWS
    return [(buf_hbm.at[d, off:off + SQ_ROWS, :], dst.at[d * SQ_ROWS:(d + 1) * SQ_ROWS, :]) for d in range(N_DEV)]


def _down_pairs(wd_hbm, dst):
    return [(wd_hbm.at[d], dst.at[d // 2, (d % 2) * DOWN_ROWS:(d % 2 + 1) * DOWN_ROWS, :]) for d in range(N_DEV)]


def _zero_down_pad(step, dst):
    @pl.when(step == 0)
    def _():
        dst[:, FF_BLK:, :] = jnp.zeros((4, FF_PAD - FF_BLK, D), BF16)


def _rms_fwd_tile(xt, g):
    r = lax.rsqrt(jnp.mean(xt * xt, axis=-1, keepdims=True) + RMS_EPS)
    return (xt * r) * g


def _rms_bwd_tile(xt, g, dn):
    r = lax.rsqrt(jnp.mean(xt * xt, axis=-1, keepdims=True) + RMS_EPS)
    xhat = xt * r
    dxhat = dn * g
    dx = r * (dxhat - xhat * jnp.mean(dxhat * xhat, axis=-1, keepdims=True))
    dg = jnp.sum(dn * xhat, axis=0, keepdims=True)
    return dx, dg


def _accumulate(ref, step, value):
    @pl.when(step == 0)
    def _():
        ref[...] = value

    @pl.when(step != 0)
    def _():
        ref[...] = ref[...] + value


def _ffn_fwd(x, g, wgu, wd, name, comm=None, head=None):
    t = x.shape[0]

    def body(x_ref, g_ref, wgu_hbm, wd_hbm, *refs):
        if head is None:
            n_ref, gate_ref, up_ref, act_ref, h_ref, wgu_v, wd_v, sems = refs
        else:
            gf_ref, t_ref, n_ref, gate_ref, up_ref, act_ref, dh_ref, loss_ref, dgf_ref, wgu_v, wd_v, sems = refs
        step = pl.program_id(0)
        _zero_down_pad(step, wd_v)
        _load_resident(step, [(wgu_hbm, wgu_v)] + _down_pairs(wd_hbm, wd_v), sems)
        xt = x_ref[...]
        n = _rms_fwd_tile(xt, g_ref[...]).astype(BF16)
        n_ref[...] = n
        acc = jnp.zeros((TM, D), F32)
        for j in range(4):
            for s in range(FF_PAD // FF_SUB):
                lo, hi = s * FF_SUB, (s + 1) * FF_SUB
                gt = _dot_nt(n, wgu_v[j, lo:hi, :])
                ut = _dot_nt(n, wgu_v[4 + j, lo:hi, :])
                gate_ref[j, :, lo:hi] = gt.astype(BF16)
                up_ref[j, :, lo:hi] = ut.astype(BF16)
                act_ref[j, :, lo:hi] = ((gt * _sigmoid(gt)) * ut).astype(BF16)
            acc = acc + _dot(act_ref[j], wd_v[j])
        ht = xt + 0.5 * acc
        if head is None:
            h_ref[...] = ht
        else:
            gain = gf_ref[...]
            diff = _rms_fwd_tile(ht, gain) - t_ref[...]
            part = 0.5 * jnp.sum(jnp.sum(diff * diff, axis=-1, keepdims=True) / D, axis=0, keepdims=True)
            dx, dg = _rms_bwd_tile(ht, gain, diff / D)
            dh_ref[...] = dx
            _accumulate(loss_ref, step, jnp.broadcast_to(part, (8, 128)))
            _accumulate(dgf_ref, step, dg)

    ff = jax.ShapeDtypeStruct((4, t, FF_PAD), BF16)
    operands, in_specs = (x, g, wgu, wd), [_row_spec(TM, D), _const_spec((1, D)), _ANY, _ANY]
    out_specs = [_row_spec(TM, D)] + [_blk_row_spec(4, TM, FF_PAD)] * 3 + [_row_spec(TM, D)]
    out_shape = [jax.ShapeDtypeStruct((t, D), BF16), ff, ff, ff, jax.ShapeDtypeStruct((t, D), F32)]
    if head is not None:
        operands += tuple(head)
        in_specs += [_const_spec((1, D)), _row_spec(TM, D)]
        out_specs += [_const_spec((8, 128)), _const_spec((1, D))]
        out_shape += [jax.ShapeDtypeStruct((8, 128), F32), jax.ShapeDtypeStruct((1, D), F32)]
    return _call(
        body, operands, grid=(t // TM,), name=name, comm=comm, in_specs=in_specs, out_specs=out_specs,
        out_shape=out_shape,
        scratch_shapes=[pltpu.VMEM((N_DEV, FF_PAD, D), BF16), pltpu.VMEM((4, FF_PAD, D), BF16),
                        pltpu.SemaphoreType.DMA((1 + N_DEV,))])


def _ffn_bwd(dh, xin, g, gate, up, wgu, wd, name, comm=None):
    t = dh.shape[0]

    def body(dh_ref, x_ref, g_ref, gate_ref, up_ref, wgu_hbm, wd_hbm,
             dgu_ref, dhb_ref, dx_ref, dg_ref, wgu_v, wd_v, sems):
        step = pl.program_id(0)
        _zero_down_pad(step, wd_v)
        _load_resident(step, [(wgu_hbm, wgu_v)] + _down_pairs(wd_hbm, wd_v), sems)
        dht = dh_ref[...]
        dhb = (0.5 * dht).astype(BF16)
        dhb_ref[...] = dhb
        dn = jnp.zeros((TM, D), F32)
        for j in range(4):
            for s in range(FF_PAD // FF_SUB):
                lo, hi = s * FF_SUB, (s + 1) * FF_SUB
                da = _dot_nt(dhb, wd_v[j, lo:hi, :])
                gt = gate_ref[j, :, lo:hi].astype(F32)
                ut = up_ref[j, :, lo:hi].astype(F32)
                sg = _sigmoid(gt)
                dgt = (da * ut * (sg * (1.0 + gt * (1.0 - sg)))).astype(BF16)
                dut = (da * (gt * sg)).astype(BF16)
                dgu_ref[j, :, lo:hi] = dgt
                dgu_ref[4 + j, :, lo:hi] = dut
            dn = dn + _dot(dgu_ref[j], wgu_v[j]) + _dot(dgu_ref[4 + j], wgu_v[4 + j])
        dx, dg = _rms_bwd_tile(x_ref[...], g_ref[...], dn)
        dx_ref[...] = dht + dx
        _accumulate(dg_ref, step, dg)

    return _call(
        body, (dh, xin, g, gate, up, wgu, wd), grid=(t // TM,), name=name, comm=comm,
        in_specs=[_row_spec(TM, D), _row_spec(TM, D), _const_spec((1, D)), _blk_row_spec(4, TM, FF_PAD),
                  _blk_row_spec(4, TM, FF_PAD), _ANY, _ANY],
        out_specs=[_blk_row_spec(N_DEV, TM, FF_PAD), _row_spec(TM, D), _row_spec(TM, D), _const_spec((1, D))],
        out_shape=[jax.ShapeDtypeStruct((N_DEV, t, FF_PAD), BF16), jax.ShapeDtypeStruct((t, D), BF16),
                   jax.ShapeDtypeStruct((t, D), F32), jax.ShapeDtypeStruct((1, D), F32)],
        scratch_shapes=[pltpu.VMEM((N_DEV, FF_PAD, D), BF16), pltpu.VMEM((4, FF_PAD, D), BF16),
                        pltpu.SemaphoreType.DMA((1 + N_DEV,))])


WIDE_TILES = (1024, 512, 256, 128)


def _pick_tile(n, options=(512, 256, 128)):
    for o in options:
        if n % o == 0:
            return o
    return n


def _mm_tn_squares(pairs, name):
    k, d = pairs[0][0].shape
    n = pairs[0][1].shape[1]
    tn = _pick_tile(n)
    count = len(pairs)

    def body(*refs):
        m = pl.program_id(0)
        for idx in range(count):
            @pl.when(m == idx)
            def _(idx=idx):
                refs[-1][...] = _dot_tn(refs[idx][...], refs[count + idx][...]).astype(BF16).reshape(
                    N_DEV, d // N_DEV, tn)

    a_specs = [pl.BlockSpec((k, d), lambda m, j: (0, 0)) for _ in pairs]
    b_specs = [pl.BlockSpec((k, tn), lambda m, j, idx=idx: (0, jnp.where(m == idx, j, 0))) for idx in range(count)]
    return pl.pallas_call(
        body, grid=(count, n // tn), name=name, in_specs=a_specs + b_specs,
        out_specs=pl.BlockSpec((N_DEV, d // N_DEV, tn), lambda m, j: (0, m, j)),
        out_shape=jax.ShapeDtypeStruct((N_DEV, count * (d // N_DEV), n), BF16),
        compiler_params=_cparams(2),
    )(*[a for a, _ in pairs], *[b for _, b in pairs])


def _mm_tn_cols_many(a, parts, name):
    k, m = a.shape
    n = parts[0].shape[2]
    tn = _pick_tile(n)
    firsts, total = [], 0
    for p in parts:
        firsts.append(total)
        total += p.shape[0]

    def body(a_ref, *refs):
        j = pl.program_id(0)
        for p, first, ref in zip(parts, firsts, refs):
            @pl.when(jnp.logical_and(j >= first, j < first + p.shape[0]))
            def _(ref=ref):
                refs[-1][0] = _dot_tn(a_ref[...].astype(BF16), ref[0].astype(BF16)).astype(BF16)

    specs = [pl.BlockSpec((1, k, tn), lambda j, i, first=first, last=p.shape[0] - 1:
                          (jnp.clip(j - first, 0, last), 0, jnp.where(jnp.logical_and(j >= first, j <= first + last), i, 0)))
             for p, first in zip(parts, firsts)]
    return pl.pallas_call(
        body, grid=(total, n // tn), name=name,
        in_specs=[pl.BlockSpec((k, m), lambda j, i: (0, 0))] + specs,
        out_specs=pl.BlockSpec((1, m, tn), lambda j, i: (j, 0, i)),
        out_shape=jax.ShapeDtypeStruct((total, m, n), BF16),
        compiler_params=_cparams(2),
    )(a, *parts)


def _mm_tn_cols(a, b, name):
    k, m = a.shape
    nb, _, n = b.shape
    tm = _pick_tile(m, WIDE_TILES)

    def body(a_ref, b_ref, o_ref):
        o_ref[0] = _dot_tn(a_ref[...].astype(BF16), b_ref[0].astype(BF16)).astype(BF16)

    return pl.pallas_call(
        body, grid=(nb, m // tm), name=name,
        in_specs=[pl.BlockSpec((k, tm), lambda j, i: (0, i)), pl.BlockSpec((1, k, n), lambda j, i: (j, 0, 0))],
        out_specs=pl.BlockSpec((1, tm, n), lambda j, i: (j, i, 0)),
        out_shape=jax.ShapeDtypeStruct((nb, m, n), BF16),
        compiler_params=_cparams(2),
    )(a, b)


def _mm_tn_rows(a, b, keep, name, comm=None):
    nb, k, m = a.shape
    _, n = b.shape
    tn = _pick_tile(n, WIDE_TILES)

    def body(a_ref, b_ref, o_ref):
        o_ref[0] = _dot_tn(a_ref[0], b_ref[...])[:keep].astype(BF16)

    (out,), couts = _call(
        body, (a, b), grid=(nb, n // tn), name=name, comm=comm,
        in_specs=[pl.BlockSpec((1, k, m), lambda j, i: (j, 0, 0)), pl.BlockSpec((k, tn), lambda j, i: (0, i))],
        out_specs=[pl.BlockSpec((1, keep, tn), lambda j, i: (j, 0, i))],
        out_shape=[jax.ShapeDtypeStruct((nb, keep, n), BF16)], scratch_shapes=[])
    return out if comm is None else (out, couts)


PCG_W = 5 * D
QKV_W = 3 * D
PROJ_SUB = 512


def _inproj_fwd(h, g, w_in, conv_w, name, comm=None):
    t = h.shape[0]

    def body(h_ref, g_ref, w_hbm, cw_ref, u_ref, pcg_ref, qkv_ref, yc_ref, w_v, tail_v, sems):
        step = pl.program_id(0)
        _load_resident(step, [(w_hbm, w_v)], sems)

        @pl.when(step == 0)
        def _():
            tail_v[...] = jnp.zeros_like(tail_v)

        u = _rms_fwd_tile(h_ref[...], g_ref[...]).astype(BF16)
        u_ref[...] = u
        for blk in range(N_DEV):
            for s in range(D // PROJ_SUB):
                lo, hi = s * PROJ_SUB, (s + 1) * PROJ_SUB
                p = _dot(u, w_v[blk, :, lo:hi])
                if blk < 3:
                    pcg_ref[:, blk * D + lo:blk * D + hi] = p
                elif blk < 6:
                    qkv_ref[:, (blk - 3) * D + lo:(blk - 3) * D + hi] = p.astype(BF16)
                else:
                    pcg_ref[:, (blk - 3) * D + lo:(blk - 3) * D + hi] = p
        xc = pcg_ref[:, D:2 * D] * pcg_ref[:, 2 * D:3 * D]
        ext = jnp.concatenate([tail_v[...], xc], axis=0)
        conv = (cw_ref[0:1, :] * pltpu.roll(ext, 2, 0)[8:] + cw_ref[1:2, :] * pltpu.roll(ext, 1, 0)[8:]
                + cw_ref[2:3, :] * xc)
        yc_ref[...] = (pcg_ref[:, 0:D] * conv).astype(BF16)
        tail_v[...] = xc[TM - 8:]

    return _call(
        body, (h, g, w_in, conv_w), grid=(t // TM,), name=name, comm=comm,
        in_specs=[_row_spec(TM, D), _const_spec((1, D)), _ANY, _const_spec((CONV_K, D))],
        out_specs=[_row_spec(TM, D), _row_spec(TM, PCG_W), _row_spec(TM, QKV_W), _row_spec(TM, D)],
        out_shape=[jax.ShapeDtypeStruct((t, D), BF16), jax.ShapeDtypeStruct((t, PCG_W), F32),
                   jax.ShapeDtypeStruct((t, QKV_W), BF16), jax.ShapeDtypeStruct((t, D), BF16)],
        scratch_shapes=[pltpu.VMEM((N_DEV, D, D), BF16), pltpu.VMEM((8, D), F32), pltpu.SemaphoreType.DMA((1,))])


def _tri2(cond):
    rr = lax.broadcasted_iota(jnp.int32, (2 * TK, TK), 0) & (TK - 1)
    cc = lax.broadcasted_iota(jnp.int32, (2 * TK, TK), 1)
    return cond(rr, cc).astype(BF16)


def _causal(shift, row0=0):
    rr = lax.broadcasted_iota(jnp.int32, (TQ - row0, TK), 0) + row0
    cc = lax.broadcasted_iota(jnp.int32, (TQ - row0, TK), 1)
    return cc + shift < rr


def _cumdot(v, tri2):
    hi = v.astype(BF16)
    lo = (v - hi.astype(F32)).astype(BF16)
    return _dot(jnp.concatenate([hi, lo], axis=1), tri2)


LOG2_E = 1.4426950408889634


def _log_1m_beta(z):
    return -(jnp.maximum(z, 0.0) + jnp.log2(1.0 + jnp.exp2(-jnp.abs(z))))


def _sb_specs(t):
    g = SB_H // SB_HPS
    w = SB_HPS * SB_DH
    q_spec = pl.BlockSpec((TQ, w), lambda h, i: (i, h))
    k_spec = pl.BlockSpec((t, w), lambda h, i: (0, g + h))
    v_spec = pl.BlockSpec((t, w), lambda h, i: (0, 2 * g + h))
    ct_spec = pl.BlockSpec((SB_HPS, TQ, 1), lambda h, i: (h, i, 0))
    return g, w, q_spec, k_spec, v_spec, ct_spec


def _sb_fwd(qkv, name, comm=None):
    t = qkv.shape[0]
    scale = SB_DH ** -0.5
    g, w, q_spec, k_spec, v_spec, ct_spec = _sb_specs(t)

    def body(q_ref, k_ref, v_ref, y_ref, ct_ref):
        i = pl.program_id(1)
        later = _tri2(lambda j, s: j > s)
        n_diag = TQ // TK

        def block(j, carry, shift):
            off = pl.multiple_of(j * TK, TK)
            zs, ms = [], []
            for hd in range(SB_HPS):
                cols = slice(hd * SB_DH, (hd + 1) * SB_DH)
                z = _dot_nt(q_ref[:, cols], k_ref[pl.ds(off, TK), cols]) * (scale * LOG2_E)
                m = _log_1m_beta(z)
                if shift is not None:
                    m = jnp.where(_causal(shift), m, 0.0)
                zs.append(z)
                ms.append(m)
            after = _cumdot(jnp.concatenate(ms, axis=0), later)
            out = []
            for hd in range(SB_HPS):
                acc, c_sum = carry[hd]
                cols = slice(hd * SB_DH, (hd + 1) * SB_DH)
                a = jnp.exp2((ms[hd] + zs[hd]) + (c_sum + after[hd * TQ:(hd + 1) * TQ]))
                if shift is not None:
                    a = jnp.where(_causal(shift), a, 0.0)
                out.append((acc + _dot(a.astype(BF16), v_ref[pl.ds(off, TK), cols]),
                            c_sum + jnp.sum(ms[hd], axis=1, keepdims=True)))
            return tuple(out)

        carry = tuple((jnp.zeros((TQ, SB_DH), F32), jnp.zeros((TQ, 1), F32)) for _ in range(SB_HPS))
        for d in reversed(range(n_diag)):
            carry = block(i * n_diag + d, carry, d * TK)
        carry = lax.fori_loop(0, i * n_diag, lambda jj, c: block(i * n_diag - 1 - jj, c, None), carry)
        for hd in range(SB_HPS):
            y_ref[:, hd * SB_DH:(hd + 1) * SB_DH] = carry[hd][0].astype(BF16)
            ct_ref[hd] = carry[hd][1]

    return _call(
        body, (qkv, qkv, qkv), grid=(g, t // TQ), name=name, comm=comm,
        in_specs=[q_spec, k_spec, v_spec],
        out_specs=[q_spec, ct_spec],
        out_shape=[jax.ShapeDtypeStruct((t, D), BF16), jax.ShapeDtypeStruct((SB_H, t, 1), F32)],
        scratch_shapes=[])


def _sb_bwd(qkv, dy, ctot, after, name, comm=None):
    t = qkv.shape[0]
    scale = SB_DH ** -0.5
    g, w, q_spec, k_spec, v_spec, ct_spec = _sb_specs(t)
    acc_spec = pl.BlockSpec((2, t, w), lambda h, i: (0, 0, h))

    def body(q_ref, k_ref, v_ref, dy_ref, ct_ref, after_ref, dq_ref, dkv_ref):
        i = pl.program_id(1)

        @pl.when(i == 0)
        def _():
            dkv_ref[...] = jnp.zeros_like(dkv_ref)

        upto = _tri2(lambda j, s: j <= s)
        n_diag = TQ // TK

        def block(j, carry, shift):
            off = pl.multiple_of(j * TK, TK)
            r0 = 0 if shift is None else shift
            nr = TQ - r0
            causal = None if shift is None else _causal(shift, r0)

            def grow(old, delta):
                return old + delta if r0 == 0 else jnp.concatenate([old[:r0], old[r0:] + delta], axis=0)

            zs, ms = [], []
            for hd in range(SB_HPS):
                cols = slice(hd * SB_DH, (hd + 1) * SB_DH)
                z = _dot_nt(q_ref[r0:, cols], k_ref[pl.ds(off, TK), cols]) * (scale * LOG2_E)
                m = _log_1m_beta(z)
                if causal is not None:
                    m = jnp.where(causal, m, 0.0)
                zs.append(z)
                ms.append(m)
            m_upto = _cumdot(jnp.concatenate(ms, axis=0), upto)
            ls, a_s, es = [], [], []
            for hd in range(SB_HPS):
                cols = slice(hd * SB_DH, (hd + 1) * SB_DH)
                l = ms[hd] + zs[hd]
                a = jnp.exp2(l + ((ct_ref[hd, r0:] - carry[hd][1][r0:]) - m_upto[hd * nr:(hd + 1) * nr]))
                if causal is not None:
                    a = jnp.where(causal, a, 0.0)
                ls.append(l)
                a_s.append(a)
                es.append(_dot_nt(dy_ref[r0:, cols], v_ref[pl.ds(off, TK), cols]) * a)
            e_upto = _dot(jnp.concatenate(es, axis=0).astype(BF16), upto[:TK])
            out = []
            for hd in range(SB_HPS):
                dq, p_sum, e_sum = carry[hd]
                cols = slice(hd * SB_DH, (hd + 1) * SB_DH)
                e = es[hd]
                dz = e - jnp.exp2(ls[hd]) * (e_sum[r0:] + e_upto[hd * nr:(hd + 1) * nr])
                if causal is not None:
                    dz = jnp.where(causal, dz, 0.0)
                dzs = (dz * scale).astype(BF16)
                dkv_ref[0, pl.ds(off, TK), cols] += _dot_tn(dzs, q_ref[r0:, cols])
                dkv_ref[1, pl.ds(off, TK), cols] += _dot_tn(a_s[hd].astype(BF16), dy_ref[r0:, cols])
                out.append((grow(dq, _dot(dzs, k_ref[pl.ds(off, TK), cols])),
                            grow(p_sum, jnp.sum(ms[hd], axis=1, keepdims=True)),
                            grow(e_sum, jnp.sum(e, axis=1, keepdims=True))))
            return tuple(out)

        zero = jnp.zeros((TQ, 1), F32)
        init = tuple((jnp.zeros((TQ, SB_DH), F32), zero, zero) for _ in range(SB_HPS))
        carry = lax.fori_loop(0, i * n_diag, lambda j, c: block(j, c, None), init)
        for d in range(n_diag):
            carry = block(i * n_diag + d, carry, d * TK)
        for hd in range(SB_HPS):
            dq_ref[:, hd * SB_DH:(hd + 1) * SB_DH] = carry[hd][0].astype(BF16)

    return _call(
        body, (qkv, qkv, qkv, dy, ctot, after), grid=(g, t // TQ), name=name, comm=comm,
        in_specs=[q_spec, k_spec, v_spec, q_spec, ct_spec, pl.BlockSpec(after.shape, lambda h, i: (0, 0))],
        out_specs=[q_spec, acc_spec],
        out_shape=[jax.ShapeDtypeStruct((t, D), BF16), jax.ShapeDtypeStruct((2, t, D), F32)],
        scratch_shapes=[])


def _gate_specs():
    return [pl.BlockSpec((TM, D), lambda i: (i, 3)), pl.BlockSpec((TM, D), lambda i: (i, 4))]


def _mix_pairs(mix_hbm, dsts):
    pairs = []
    for index, dst in enumerate(dsts):
        pairs += _square_pairs(mix_hbm, index, dst)
    return pairs


def _mix_out_fwd(yc, ysb, pcg, b_gate, h, w_mix, name, comm=None):
    t = h.shape[0]

    def body(yc_ref, ysb_ref, gc_ref, gs_ref, b_ref, h_ref, mix_hbm,
             a_ref, b_out_ref, mg_ref, h2_ref, wc_v, wa_v, wo_v, sems):
        _load_resident(pl.program_id(0), _mix_pairs(mix_hbm, (wc_v, wa_v, wo_v)), sems)
        a = _dot(yc_ref[...], wc_v[...])
        b = _dot(ysb_ref[...], wa_v[...])
        merged = (_sigmoid(gc_ref[...] + b_ref[:, :D]) * a + _sigmoid(gs_ref[...] + b_ref[:, D:]) * b).astype(BF16)
        a_ref[...] = a
        b_out_ref[...] = b
        mg_ref[...] = merged
        h2_ref[...] = h_ref[...] + _dot(merged, wo_v[...])

    return _call(
        body, (yc, ysb, pcg, pcg, b_gate, h, w_mix), grid=(t // TM,), name=name, comm=comm,
        in_specs=[_row_spec(TM, D), _row_spec(TM, D)] + _gate_specs()
                 + [_const_spec((1, 2 * D)), _row_spec(TM, D), _ANY],
        out_specs=[_row_spec(TM, D)] * 4,
        out_shape=[jax.ShapeDtypeStruct((t, D), F32), jax.ShapeDtypeStruct((t, D), F32),
                   jax.ShapeDtypeStruct((t, D), BF16), jax.ShapeDtypeStruct((t, D), F32)],
        scratch_shapes=[pltpu.VMEM((D, D), BF16)] * 3 + [pltpu.SemaphoreType.DMA((3 * N_DEV,))])


def _mix_out_bwd(dh2, a, b, pcg, b_gate, conv_w, w_mix, name, comm=None):
    t = dh2.shape[0]
    n_tile = t // TM
    per8 = TM // 8

    def rows(n):
        return pl.BlockSpec((TM, n), lambda i: (n_tile - 1 - i, 0))

    def cols(block):
        return pl.BlockSpec((TM, D), lambda i: (n_tile - 1 - i, block))

    def before(block):
        return pl.BlockSpec((8, D), lambda i: (jnp.maximum((n_tile - 1 - i) * per8 - 1, 0), block))

    def body(dh_ref, a_ref, b_ref, gc_ref, gs_ref, cb_ref, cc_ref, cx_ref, ccp_ref, cxp_ref, bias_ref, cw_ref, mix_hbm,
             dhb_ref, da_ref, db_ref, dgp_ref, dc_ref, dysb_ref, dbias_ref, dcw_ref, wc_v, wa_v, wo_v, head_v, sems):
        step = pl.program_id(0)
        _load_resident(step, _mix_pairs(mix_hbm, (wc_v, wa_v, wo_v)), sems)

        @pl.when(step == 0)
        def _():
            head_v[...] = jnp.zeros_like(head_v)
            dcw_ref[...] = jnp.zeros_like(dcw_ref)

        dhb = dh_ref[...].astype(BF16)
        dhb_ref[...] = dhb
        dm = _dot_nt(dhb, wo_v[...])
        gc = _sigmoid(gc_ref[...] + bias_ref[:, :D])
        gs = _sigmoid(gs_ref[...] + bias_ref[:, D:])
        da = (dm * gc).astype(BF16)
        db = (dm * gs).astype(BF16)
        da_ref[...] = da
        db_ref[...] = db
        dgc = dm * a_ref[...] * (gc * (1.0 - gc))
        dgs = dm * b_ref[...] * (gs * (1.0 - gs))
        dgp_ref[0] = dgc.astype(BF16)
        dgp_ref[1] = dgs.astype(BF16)
        _accumulate(dbias_ref.at[:, :D], step, jnp.sum(dgc, axis=0, keepdims=True))
        _accumulate(dbias_ref.at[:, D:], step, jnp.sum(dgs, axis=0, keepdims=True))
        dysb_ref[...] = _dot_nt(db, wa_v[...]).astype(BF16)
        dyc = _dot_nt(da, wc_v[...])
        cc, cx = cc_ref[...], cx_ref[...]
        xc = cc * cx
        xc_before = jnp.where(step == n_tile - 1, 0.0, ccp_ref[...] * cxp_ref[...])
        ext = jnp.concatenate([xc_before, xc], axis=0)
        x1 = pltpu.roll(ext, 1, 0)[8:]
        x2 = pltpu.roll(ext, 2, 0)[8:]
        w0, w1, w2 = cw_ref[0:1, :], cw_ref[1:2, :], cw_ref[2:3, :]
        dc_ref[0] = (dyc * (w0 * x2 + w1 * x1 + w2 * xc)).astype(BF16)
        dconv = dyc * cb_ref[...]
        dcw_ref[0:1, :] += jnp.sum(dconv * x2, axis=0, keepdims=True)
        dcw_ref[1:2, :] += jnp.sum(dconv * x1, axis=0, keepdims=True)
        dcw_ref[2:3, :] += jnp.sum(dconv * xc, axis=0, keepdims=True)
        after = jnp.concatenate([dconv, head_v[...]], axis=0)
        dxc = w2 * dconv + w1 * pltpu.roll(after, TM + 7, 0)[:TM] + w0 * pltpu.roll(after, TM + 6, 0)[:TM]
        dc_ref[1] = (dxc * cx).astype(BF16)
        dc_ref[2] = (dxc * cc).astype(BF16)
        head_v[...] = dconv[:8]

    return _call(
        body, (dh2, a, b, pcg, pcg, pcg, pcg, pcg, pcg, pcg, b_gate, conv_w, w_mix), grid=(n_tile,), name=name,
        comm=comm,
        in_specs=[rows(D)] * 3 + [cols(3), cols(4), cols(0), cols(1), cols(2), before(1), before(2),
                                  _const_spec((1, 2 * D)), _const_spec((CONV_K, D)), _ANY],
        out_specs=[rows(D)] * 3 + [pl.BlockSpec((2, TM, D), lambda i: (0, n_tile - 1 - i, 0)),
                                   pl.BlockSpec((3, TM, D), lambda i: (0, n_tile - 1 - i, 0)), rows(D),
                                   _const_spec((1, 2 * D)), _const_spec((8, D))],
        out_shape=[jax.ShapeDtypeStruct((t, D), BF16)] * 3
                  + [jax.ShapeDtypeStruct((2, t, D), BF16), jax.ShapeDtypeStruct((3, t, D), BF16),
                     jax.ShapeDtypeStruct((t, D), BF16), jax.ShapeDtypeStruct((1, 2 * D), F32),
                     jax.ShapeDtypeStruct((8, D), F32)],
        scratch_shapes=[pltpu.VMEM((D, D), BF16)] * 3 + [pltpu.VMEM((8, D), F32),
                                                         pltpu.SemaphoreType.DMA((3 * N_DEV,))])


def _inproj_bwd(dconv, dq, dkv, dgp, w_in, h, g, dh_res, name, comm=None):
    t = h.shape[0]

    def body(dc_ref, dq_ref, dkv_ref, dgp_ref, w_hbm, h_ref, g_ref, dres_ref, dh_ref, dg_ref, w_v, sems):
        step = pl.program_id(0)
        _load_resident(step, [(w_hbm, w_v)], sems)
        du = _dot_nt(dq_ref[...], w_v[3])
        for k in range(3):
            du = du + _dot_nt(dc_ref[k], w_v[k])
        for k in range(2):
            du = du + _dot_nt(dkv_ref[k].astype(BF16), w_v[4 + k]) + _dot_nt(dgp_ref[k], w_v[6 + k])
        dx, dg = _rms_bwd_tile(h_ref[...], g_ref[...], du)
        dh_ref[...] = dres_ref[...] + dx
        _accumulate(dg_ref, step, dg)

    return _call(
        body, (dconv, dq, dkv, dgp, w_in, h, g, dh_res), grid=(t // TM,), name=name, comm=comm,
        in_specs=[_blk_row_spec(3, TM, D), _row_spec(TM, D), _blk_row_spec(2, TM, D), _blk_row_spec(2, TM, D), _ANY,
                  _row_spec(TM, D), _const_spec((1, D)), _row_spec(TM, D)],
        out_specs=[_row_spec(TM, D), _const_spec((1, D))],
        out_shape=[jax.ShapeDtypeStruct((t, D), F32), jax.ShapeDtypeStruct((1, D), F32)],
        scratch_shapes=[pltpu.VMEM((N_DEV, D, D), BF16), pltpu.SemaphoreType.DMA((1,))])


def _softmax_rows(s):
    e = jnp.exp(s - jnp.max(s, axis=-1, keepdims=True))
    return e / jnp.sum(e, axis=-1, keepdims=True)


def _cross_pairs(cross_hbm, wq_v, wo_v):
    return _square_pairs(cross_hbm, 0, wq_v) + _square_pairs(cross_hbm, 1, wo_v)


def _cross_fwd(h, g, mem, g_mem, w_ckv, w_cross, name):
    t = h.shape[0]
    m = mem.shape[0]
    scale = X_DH ** -0.5

    def body(h_ref, g_ref, mem_ref, gm_ref, wkv_ref, cross_hbm, hn_ref, qx_ref, o_ref, h3_ref, mn_ref, kv_ref,
             wq_v, wo_v, sems):
        _load_resident(pl.program_id(0), _cross_pairs(cross_hbm, wq_v, wo_v), sems)

        @pl.when(pl.program_id(0) == 0)
        def _():
            mn = _rms_fwd_tile(mem_ref[...], gm_ref[...]).astype(BF16)
            mn_ref[...] = mn
            for j in range(N_DEV):
                kv_ref[j] = _dot(mn, wkv_ref[j]).astype(BF16)

        ht = h_ref[...]
        hn = _rms_fwd_tile(ht, g_ref[...]).astype(BF16)
        hn_ref[...] = hn
        qx = _dot(hn, wq_v[...]).astype(BF16)
        qx_ref[...] = qx
        for hd in range(X_H):
            lo, hi = hd * X_DH, (hd + 1) * X_DH
            p = _softmax_rows(_dot_nt(qx[:, lo:hi], kv_ref[hd]) * scale)
            o_ref[:, lo:hi] = _dot(p.astype(BF16), kv_ref[X_H + hd]).astype(BF16)
        h3_ref[...] = ht + _dot(o_ref[...], wo_v[...])

    return pl.pallas_call(
        body, grid=(t // TM,), name=name,
        in_specs=[_row_spec(TM, D), _const_spec((1, D)), _const_spec((m, D)), _const_spec((1, D)),
                  _const_spec((N_DEV, D, X_DH)), _ANY],
        out_specs=[_row_spec(TM, D)] * 4 + [_const_spec((m, D)), _const_spec((N_DEV, m, X_DH))],
        out_shape=[jax.ShapeDtypeStruct((t, D), BF16)] * 3 + [jax.ShapeDtypeStruct((t, D), F32),
                                                              jax.ShapeDtypeStruct((m, D), BF16),
                                                              jax.ShapeDtypeStruct((N_DEV, m, X_DH), BF16)],
        scratch_shapes=[pltpu.VMEM((D, D), BF16)] * 2 + [pltpu.SemaphoreType.DMA((2 * N_DEV,))],
        compiler_params=_cparams(),
    )(h, g, mem, g_mem, w_ckv, w_cross)


def _cross_bwd(dh3, h, g, qx, kv, mem, g_mem, w_ckv, w_cross, name, comm=None):
    t = h.shape[0]
    m = kv.shape[1]
    scale = X_DH ** -0.5

    def body(dh_ref, h_ref, g_ref, qx_ref, kv_ref, mem_ref, gm_ref, wkv_ref, cross_hbm,
             dhb_ref, dqx_ref, dkv_ref, dh2_ref, dg_ref, dgm_ref, wq_v, wo_v, sems):
        step = pl.program_id(0)
        _load_resident(step, _cross_pairs(cross_hbm, wq_v, wo_v), sems)

        @pl.when(step == 0)
        def _():
            dkv_ref[...] = jnp.zeros_like(dkv_ref)

        dht = dh_ref[...]
        dhb = dht.astype(BF16)
        dhb_ref[...] = dhb
        do = _dot_nt(dhb, wo_v[...]).astype(BF16)
        for hd in range(X_H):
            lo, hi = hd * X_DH, (hd + 1) * X_DH
            qh = qx_ref[:, lo:hi]
            kh = kv_ref[hd]
            p = _softmax_rows(_dot_nt(qh, kh) * scale)
            doh = do[:, lo:hi]
            dp = _dot_nt(doh, kv_ref[X_H + hd])
            ds = (p * (dp - jnp.sum(dp * p, axis=-1, keepdims=True)) * scale).astype(BF16)
            dqx_ref[:, lo:hi] = _dot(ds, kh).astype(BF16)
            dkv_ref[hd] += _dot_tn(ds, qh)
            dkv_ref[X_H + hd] += _dot_tn(p.astype(BF16), doh)
        dhn = _dot_nt(dqx_ref[...], wq_v[...])
        dx, dg = _rms_bwd_tile(h_ref[...], g_ref[...], dhn)
        dh2_ref[...] = dht + dx
        _accumulate(dg_ref, step, dg)

        @pl.when(step == t // TM - 1)
        def _():
            dmn = jnp.zeros((m, D), F32)
            for j in range(N_DEV):
                dmn = dmn + _dot_nt(dkv_ref[j].astype(BF16), wkv_ref[j])
            dgm_ref[...] = _rms_bwd_tile(mem_ref[...], gm_ref[...], dmn)[1]

    return _call(
        body, (dh3, h, g, qx, kv, mem, g_mem, w_ckv, w_cross), grid=(t // TM,), name=name, comm=comm,
        in_specs=[_row_spec(TM, D), _row_spec(TM, D), _const_spec((1, D)), _row_spec(TM, D),
                  _const_spec((N_DEV, m, X_DH)), _const_spec((m, D)), _const_spec((1, D)),
                  _const_spec((N_DEV, D, X_DH)), _ANY],
        out_specs=[_row_spec(TM, D), _row_spec(TM, D), _const_spec((N_DEV, m, X_DH)), _row_spec(TM, D),
                   _const_spec((1, D)), _const_spec((1, D))],
        out_shape=[jax.ShapeDtypeStruct((t, D), BF16), jax.ShapeDtypeStruct((t, D), BF16),
                   jax.ShapeDtypeStruct((N_DEV, m, X_DH), F32), jax.ShapeDtypeStruct((t, D), F32),
                   jax.ShapeDtypeStruct((1, D), F32), jax.ShapeDtypeStruct((1, D), F32)],
        scratch_shapes=[pltpu.VMEM((D, D), BF16)] * 2 + [pltpu.SemaphoreType.DMA((2 * N_DEV,))])


def _adamw_small(small_all, dev, wts, mom1, mom2, name):
    n_slots = small_all.shape[0]
    conv_rows, conv_cols = wts["conv_w"].shape
    shapes = {n: (1, wts[n].size) for n in GAINS + ("b_gate",)}
    shapes["conv_w"] = (conv_rows, conv_cols)
    n_par = len(SMALL)
    gate_row, conv_row = len(GAINS), len(GAINS) + 2

    def body(dev_ref, all_ref, cols_ref, *refs):
        params, loss_ref, outs, acc = refs[:3 * n_par], refs[3 * n_par], refs[3 * n_par + 1:-1], refs[-1]
        total = all_ref[0]
        for k in range(1, n_slots):
            total = total + all_ref[k]
        acc[...] = total
        loss_ref[...] = acc[LOSS_ROW:LOSS_ROW + 1, 0:1]
        for i, n in enumerate(SMALL):
            w_ref, m_ref, v_ref = params[3 * i:3 * i + 3]
            res = outs[4 * i:4 * i + 4]
            if n == "conv_w":
                gt = cols_ref[0, conv_row:conv_row + conv_rows, :]
                for k in range(1, n_slots):
                    gt = gt + cols_ref[k, conv_row:conv_row + conv_rows, :]
                _adamw_apply(gt, w_ref, m_ref, v_ref, *res)
            elif n == "b_gate":
                for half in range(2):
                    cols = slice(half * D, (half + 1) * D)
                    _adamw_apply(acc[gate_row + half:gate_row + half + 1, :], w_ref.at[:, cols], m_ref.at[:, cols],
                                 v_ref.at[:, cols], *[r.at[:, cols] for r in res])
            else:
                _adamw_apply(acc[i:i + 1, :], w_ref, m_ref, v_ref, *res)

    def whole(shape):
        return pl.BlockSpec(shape, lambda i, dev_ref: (0,) * len(shape))

    operands = [a.reshape(shapes[n]) for n in SMALL for a in (wts[n], mom1[n], mom2[n])]
    out_shapes = [shapes[n] for n in SMALL for _ in range(4)]
    outs = pl.pallas_call(
        body, name=name,
        grid_spec=pltpu.PrefetchScalarGridSpec(
            num_scalar_prefetch=1, grid=(1,),
            in_specs=[whole(small_all.shape),
                      pl.BlockSpec((n_slots, SMALL_R, conv_cols), lambda i, dev_ref: (0, 0, dev_ref[0]))]
            + [whole(a.shape) for a in operands],
            out_specs=[whole((1, 1))] + [whole(s) for s in out_shapes],
            scratch_shapes=[pltpu.VMEM((SMALL_R, D), F32)]),
        out_shape=[jax.ShapeDtypeStruct((1, 1), F32)] + [jax.ShapeDtypeStruct(s, F32) for s in out_shapes],
        compiler_params=_cparams(),
    )(dev, small_all, small_all, *operands)
    small = {n: tuple(o.reshape(wts[n].shape) for o in outs[1 + 4 * i:5 + 4 * i]) for i, n in enumerate(SMALL)}
    return outs[0].reshape(()), small


def _adamw_own(w, land, own, chip, m, v, name, row_block=0, token=None):
    r, c = w.shape
    tr = _pick_tile(r, (256, 352, 128))
    off = row_block * (r // tr)

    def body(chip_ref, w_ref, land_ref, own_ref, m_ref, v_ref, *rest):
        mine = own_ref[0].astype(F32)
        gt = jnp.where(chip_ref[0] == 0, mine, land_ref[0].astype(F32))
        for k in range(1, N_CHIP):
            gt = gt + jnp.where(chip_ref[0] == k, mine, land_ref[k].astype(F32))
        if token is None:
            _adamw_apply(gt, w_ref, m_ref, v_ref, *rest)
        else:
            before, *outs, later = rest
            _adamw_apply(gt + before[0:1, 0:1], w_ref, m_ref, v_ref, *outs)
            later[...] = before[...]

    spec = pl.BlockSpec((tr, c), lambda i, chip_ref: (i, 0))
    in_specs = [spec, pl.BlockSpec((N_CHIP, tr, c), lambda i, chip_ref: (0, i + off, 0)),
                pl.BlockSpec((1, tr, c), lambda i, chip_ref: (chip_ref[0], i + off, 0)), spec, spec]
    operands = (chip, w, land, own, m, v)
    out_specs, out_shape = [spec] * 4, [jax.ShapeDtypeStruct((r, c), F32)] * 4
    if token is not None:
        token_spec = pl.BlockSpec(token.shape, lambda i, chip_ref: (0, 0))
        in_specs.append(token_spec)
        operands += (token,)
        out_specs, out_shape = out_specs + [token_spec], out_shape + [jax.ShapeDtypeStruct(token.shape, token.dtype)]
    return pl.pallas_call(
        body, name=name,
        grid_spec=pltpu.PrefetchScalarGridSpec(
            num_scalar_prefetch=1, grid=(r // tr,), in_specs=in_specs, out_specs=out_specs),
        out_shape=out_shape,
        compiler_params=_cparams(),
    )(*operands)


def _adamw_apply(gt, w_ref, m_ref, v_ref, g_ref, d_ref, nm_ref, nv_ref):
    g_ref[...] = gt
    nm = ADAM_B1 * m_ref[...] + (1.0 - ADAM_B1) * gt
    nv = ADAM_B2 * v_ref[...] + (1.0 - ADAM_B2) * jnp.square(gt)
    m_hat = nm / (1.0 - ADAM_B1 ** ADAM_STEP)
    v_hat = nv / (1.0 - ADAM_B2 ** ADAM_STEP)
    d_ref[...] = -ADAM_LR * (m_hat / (jnp.sqrt(v_hat) + ADAM_EPS) + ADAM_WD * w_ref[...])
    nm_ref[...] = nm
    nv_ref[...] = nv


def _mesh_pos():
    return lax.axis_index("x"), lax.axis_index("y"), lax.axis_index("c")


def _both(first, second):
    n_in, n_out, n_sem = len(first.inputs), len(first.out_shapes), len(first.sem_shapes)

    def run(round_name):
        def both(in_refs, out_refs, sems):
            getattr(first, round_name)(in_refs[:n_in], out_refs[:n_out], sems[:n_sem])
            getattr(second, round_name)(in_refs[n_in:], out_refs[n_out:], sems[n_sem:])
        return both

    return types.SimpleNamespace(
        inputs=first.inputs + second.inputs, out_shapes=first.out_shapes + second.out_shapes,
        sem_shapes=first.sem_shapes + second.sem_shapes, start=run("start"), middle=run("middle"),
        finish=run("finish"))


def _no_round(in_refs, out_refs, sems):
    pass


def _after(token):
    return types.SimpleNamespace(inputs=[token], out_shapes=[], sem_shapes=[], start=_no_round, middle=_no_round,
                                 finish=_no_round)


def _run_exchange(comm, name):
    c_in, c_out = len(comm.inputs), len(comm.out_shapes)

    def body(*refs):
        cins, couts, sems = refs[:c_in], refs[c_in:c_in + c_out], refs[c_in + c_out:]
        comm.start(cins, couts, sems)
        comm.middle(cins, couts, sems)
        comm.finish(cins, couts, sems)

    return list(pl.pallas_call(
        body, name=name, out_shape=list(comm.out_shapes),
        in_specs=[_ANY] * c_in, out_specs=[_ANY] * c_out, scratch_shapes=list(comm.sem_shapes),
    )(*comm.inputs))


def _gather_exchange(shards):
    n_arr = len(shards)

    def plan(x_refs, out_refs, sems):
        send_sems, recv_sems, local_sems = sems[:3]
        stage = sems[3:]
        x, y, c = _mesh_pos()
        me, sibling = (x, y, c), (x, y, 1 - c)
        xn, yn, diag = (1 - x, y), (x, 1 - y), (1 - x, 1 - y)

        def slot(a, px, py, pc, half=None):
            ref = out_refs[a].at[4 * px + 2 * py + pc]
            if half is None:
                return ref
            rows = shards[a].shape[0] // 2
            return ref.at[half * rows:(half + 1) * rows]

        def copy(a, k, block, to, half=None, src=None):
            dst = slot(a, *block, half)
            return pltpu.make_async_remote_copy(
                src_ref=dst if src is None else src, dst_ref=dst,
                send_sem=send_sems.at[a, k], recv_sem=recv_sems.at[a, k],
                device_id=to, device_id_type=pl.DeviceIdType.MESH)

        return types.SimpleNamespace(
            me=me, sibling=sibling, xn=xn, yn=yn, diag=diag, c=c, copy=copy,
            mine_in=[pltpu.make_async_copy(x_refs[a], stage[a], local_sems.at[a, 0]) for a in range(n_arr)],
            mine_out=[pltpu.make_async_copy(stage[a], slot(a, *me), local_sems.at[a, 1]) for a in range(n_arr)],
            first=[cp for a in range(n_arr) for cp in (
                copy(a, 0, me, sibling, src=x_refs[a]), copy(a, 1, me, (*xn, c), src=x_refs[a]),
                copy(a, 2, me, (*yn, c), src=x_refs[a]))],
            second=lambda a: (copy(a, 3, (*xn, c), (*yn, c), half=0), copy(a, 5, (*xn, c), sibling),
                              copy(a, 4, (*yn, c), (*xn, c), half=1), copy(a, 6, (*yn, c), sibling)),
            third=lambda a: (copy(a, 7, (*diag, c), sibling, half=0), copy(a, 8, (*diag, c), sibling, half=1)))

    def start(x_refs, out_refs, sems):
        p = plan(x_refs, out_refs, sems)
        for cp in p.first + p.mine_in:
            cp.start()
        for cp_in, cp_out in zip(p.mine_in, p.mine_out):
            cp_in.wait()
            cp_out.start()

    def middle(x_refs, out_refs, sems):
        p = plan(x_refs, out_refs, sems)
        for a in range(n_arr):
            to_yn, x_to_sib, to_xn, y_to_sib = p.second(a)
            p.copy(a, 1, (*p.xn, p.c), p.me).wait_recv()
            to_yn.start()
            x_to_sib.start()
            p.copy(a, 2, (*p.yn, p.c), p.me).wait_recv()
            to_xn.start()
            y_to_sib.start()

    def finish(x_refs, out_refs, sems):
        p = plan(x_refs, out_refs, sems)
        for a in range(n_arr):
            half0_to_sib, half1_to_sib = p.third(a)
            p.copy(a, 3, (*p.diag, p.c), p.me, half=0).wait_recv()
            half0_to_sib.start()
            p.copy(a, 4, (*p.diag, p.c), p.me, half=1).wait_recv()
            half1_to_sib.start()
        other = 1 - p.c
        for a in range(n_arr):
            p.copy(a, 0, p.sibling, p.me).wait_recv()
            p.copy(a, 5, (*p.xn, other), p.me).wait_recv()
            p.copy(a, 6, (*p.yn, other), p.me).wait_recv()
            p.copy(a, 7, (*p.diag, other), p.me, half=0).wait_recv()
            p.copy(a, 8, (*p.diag, other), p.me, half=1).wait_recv()
        for cp in p.first:
            cp.wait_send()
        for a in range(n_arr):
            for cp in p.second(a) + p.third(a):
                cp.wait_send()
        for cp in p.mine_out:
            cp.wait()

    return types.SimpleNamespace(
        inputs=list(shards), start=start, middle=middle, finish=finish,
        out_shapes=[jax.ShapeDtypeStruct((N_DEV,) + s.shape, s.dtype) for s in shards],
        sem_shapes=[pltpu.SemaphoreType.DMA((n_arr, 9)), pltpu.SemaphoreType.DMA((n_arr, 9)),
                    pltpu.SemaphoreType.DMA((n_arr, 2))] + [pltpu.VMEM(s.shape, s.dtype) for s in shards])


def _pair_exchange(grads):
    n_arr = len(grads)

    def plan(g_refs, land_refs, sems):
        send_sems, recv_sems = sems
        x, y, c = _mesh_pos()
        return [pltpu.make_async_remote_copy(
            src_ref=g_refs[a].at[2 * k + 1 - c], dst_ref=land_refs[a].at[k],
            send_sem=send_sems.at[a, k], recv_sem=recv_sems.at[a, k],
            device_id=(x, y, 1 - c), device_id_type=pl.DeviceIdType.MESH)
            for a in range(n_arr) for k in range(N_CHIP)]

    def start(g_refs, land_refs, sems):
        for cp in plan(g_refs, land_refs, sems):
            cp.start()

    def finish(g_refs, land_refs, sems):
        for cp in plan(g_refs, land_refs, sems):
            cp.wait()

    return types.SimpleNamespace(
        inputs=list(grads), start=start, middle=_no_round, finish=finish,
        out_shapes=[jax.ShapeDtypeStruct((N_CHIP,) + g.shape[1:], g.dtype) for g in grads],
        sem_shapes=[pltpu.SemaphoreType.DMA((n_arr, N_CHIP)), pltpu.SemaphoreType.DMA((n_arr, N_CHIP))])


def _direct_gather(shard):
    def plan(x_ref, out_ref, sems):
        send_sems, recv_sems, local_sem = sems
        x, y, c = _mesh_pos()
        mine = out_ref.at[4 * x + 2 * y + c]
        local = pltpu.make_async_copy(x_ref, mine, local_sem.at[0])
        remote = []
        for k in range(N_DEV - 1):
            peer = tuple(1 - pos if (k + 1) >> bit & 1 else pos for pos, bit in ((x, 2), (y, 1), (c, 0)))
            remote.append(pltpu.make_async_remote_copy(
                src_ref=x_ref, dst_ref=mine, send_sem=send_sems.at[k], recv_sem=recv_sems.at[k],
                device_id=peer, device_id_type=pl.DeviceIdType.MESH))
        return local, remote

    def start(x_refs, out_refs, sems):
        local, remote = plan(x_refs[0], out_refs[0], sems)
        for cp in [local] + remote:
            cp.start()

    def finish(x_refs, out_refs, sems):
        local, remote = plan(x_refs[0], out_refs[0], sems)
        for cp in remote:
            cp.wait_recv()
        for cp in remote:
            cp.wait_send()
        local.wait()

    return types.SimpleNamespace(
        inputs=[shard], start=start, middle=_no_round, finish=finish,
        out_shapes=[jax.ShapeDtypeStruct((N_DEV,) + shard.shape, shard.dtype)],
        sem_shapes=[pltpu.SemaphoreType.DMA((N_DEV - 1,)), pltpu.SemaphoreType.DMA((N_DEV - 1,)),
                    pltpu.SemaphoreType.DMA((1,))])


def _chip_exchange(parts):
    n_arr = len(parts)

    def plan(p_refs, land_refs, sems):
        send_sems, recv_sems, local_sems = sems
        x, y, c = _mesh_pos()
        my_chip = 2 * x + y
        chips = [(1 - x, y), (x, 1 - y), (1 - x, 1 - y)]
        local = [pltpu.make_async_copy(p_refs[a].at[my_chip], land_refs[a].at[my_chip], local_sems.at[a])
                 for a in range(n_arr)]

        def copy(a, k, src_slot, dst_slot, px, py):
            return pltpu.make_async_remote_copy(
                src_ref=p_refs[a].at[src_slot], dst_ref=land_refs[a].at[dst_slot],
                send_sem=send_sems.at[a, k], recv_sem=recv_sems.at[a, k],
                device_id=(px, py, c), device_id_type=pl.DeviceIdType.MESH)

        sends = [copy(a, k, 2 * px + py, my_chip, px, py) for a in range(n_arr) for k, (px, py) in enumerate(chips)]
        arrivals = [copy(a, k, my_chip, 2 * px + py, px, py) for a in range(n_arr)
                    for k, (px, py) in enumerate(chips)]
        return local, sends, arrivals

    def start(p_refs, land_refs, sems):
        local, sends, _ = plan(p_refs, land_refs, sems)
        for cp in local + sends:
            cp.start()

    def finish(p_refs, land_refs, sems):
        local, sends, arrivals = plan(p_refs, land_refs, sems)
        for cp in arrivals:
            cp.wait_recv()
        for cp in sends:
            cp.wait_send()
        for cp in local:
            cp.wait()

    return types.SimpleNamespace(
        inputs=list(parts), start=start, middle=_no_round, finish=finish,
        out_shapes=[jax.ShapeDtypeStruct(p.shape, p.dtype) for p in parts],
        sem_shapes=[pltpu.SemaphoreType.DMA((n_arr, 3)), pltpu.SemaphoreType.DMA((n_arr, 3)),
                    pltpu.SemaphoreType.DMA((n_arr,))])


_HBM = pl.BlockSpec(memory_space=pltpu.HBM)
_SEM = pl.BlockSpec(memory_space=pltpu.SEMAPHORE)
_DATAFLOW = pltpu.SideEffectType.DATAFLOW_SIDE_EFFECTING


def _chip_copies(p_refs, land_refs, send_sems, recv_sems):
    x, y, c = _mesh_pos()
    my_chip = 2 * x + y
    chips = [(1 - x, y), (x, 1 - y), (1 - x, 1 - y)]
    return [pltpu.make_async_remote_copy(
        src_ref=p_refs[a].at[2 * px + py], dst_ref=land_refs[a].at[my_chip],
        send_sem=send_sems[3 * a + k], recv_sem=recv_sems[3 * a + k],
        device_id=(px, py, c), device_id_type=pl.DeviceIdType.MESH)
        for a in range(len(p_refs)) for k, (px, py) in enumerate(chips)]


def _chip_exchange_begin(parts, name):
    n_arr = len(parts)
    n_buf, n_copy = 2 * n_arr, 3 * n_arr
    lands = [lax.empty(p.shape, p.dtype) for p in parts]

    def body(*refs):
        p_refs, land_refs = refs[:n_arr], refs[n_arr:n_buf]
        send_sems, recv_sems, token = refs[n_buf:n_buf + n_copy], refs[n_buf + n_copy:n_buf + 2 * n_copy], refs[-1]
        for cp in _chip_copies(p_refs, land_refs, send_sems, recv_sems):
            cp.start()
        token[...] = jnp.zeros_like(token)

    bufs = list(parts) + list(lands)
    outs = pl.pallas_call(
        body, name=name,
        out_shape=(*[pltpu.SemaphoreType.DMA(())] * (2 * n_copy), *[pltpu.HBM(b.shape, b.dtype) for b in bufs],
                   jax.ShapeDtypeStruct((8, 128), F32)),
        in_specs=[_HBM] * n_buf,
        out_specs=(*[_SEM] * (2 * n_copy), *[_HBM] * n_buf, pl.BlockSpec(memory_space=pltpu.VMEM)),
        input_output_aliases={i: 2 * n_copy + i for i in range(n_buf)},
        compiler_params=pltpu.CompilerParams(has_side_effects=_DATAFLOW),
    )(*[pltpu.with_memory_space_constraint(b, pltpu.HBM) for b in bufs])
    sems = list(outs[:2 * n_copy])
    thru = list(outs[2 * n_copy:2 * n_copy + n_buf])
    return types.SimpleNamespace(send_sems=sems[:n_copy], recv_sems=sems[n_copy:], parts=thru[:n_arr],
                                 lands=thru[n_arr:], token=outs[-1])


def _chip_exchange_end(flight, after, name):
    send_sems, recv_sems, parts, lands = flight.send_sems, flight.recv_sems, flight.parts, flight.lands
    n_arr = len(parts)
    n_buf, n_copy = 2 * n_arr, 3 * n_arr

    def body(*refs):
        p_refs, land_refs = refs[:n_arr], refs[n_arr:n_buf]
        sems = refs[n_buf:n_buf + 2 * n_copy]
        for cp in _chip_copies(p_refs, land_refs, sems[:n_copy], sems[n_copy:]):
            cp.wait_send()
            cp.wait_recv()

    bufs = list(parts) + list(lands)
    outs = pl.pallas_call(
        body, name=name, out_shape=tuple(pltpu.HBM(b.shape, b.dtype) for b in bufs),
        in_specs=[_HBM] * n_buf + [_SEM] * (2 * n_copy) + [_ANY], out_specs=tuple([_HBM] * n_buf),
        input_output_aliases={i: i for i in range(n_buf)},
        compiler_params=pltpu.CompilerParams(has_side_effects=_DATAFLOW),
    )(*bufs, *send_sems, *recv_sems, after)
    return list(outs[:n_arr]), list(outs[n_arr:])


def _row_tile(r, cap=640):
    best = None
    for cand in range(16, min(r, cap) + 1, 16):
        if r % cand == 0:
            best = cand
    return best if best is not None else r


def _pair_sum(gs, landeds, core, name):
    tiles = [_row_tile(g.shape[1]) for g in gs]
    counts = [g.shape[1] // tr for g, tr in zip(gs, tiles)]
    n_arr = len(gs)

    def body(core_ref, *refs):
        for a in range(n_arr):
            mine, theirs, out = refs[2 * a], refs[2 * a + 1], refs[2 * n_arr + a]
            out[0] = (mine[0].astype(F32) + theirs[0].astype(F32)).astype(out.dtype)

    in_specs, out_specs, operands = [], [], []
    for g, landed, tr, count in zip(gs, landeds, tiles, counts):
        c_dim = g.shape[2]
        last = count - 1
        in_specs += [pl.BlockSpec((1, tr, c_dim),
                                  lambda k, i, core_ref, last=last: (2 * k + core_ref[0], jnp.minimum(i, last), 0)),
                     pl.BlockSpec((1, tr, c_dim), lambda k, i, core_ref, last=last: (k, jnp.minimum(i, last), 0))]
        out_specs.append(pl.BlockSpec((1, tr, c_dim), lambda k, i, core_ref, last=last: (k, jnp.minimum(i, last), 0)))
        operands += [g, landed]
    return list(pl.pallas_call(
        body, name=name,
        grid_spec=pltpu.PrefetchScalarGridSpec(
            num_scalar_prefetch=1, grid=(N_CHIP, max(counts)), in_specs=in_specs, out_specs=out_specs),
        out_shape=[jax.ShapeDtypeStruct((N_CHIP,) + g.shape[1:], g.dtype) for g in gs],
        compiler_params=_cparams(2),
    )(core, *operands))


GAINS = ("g_ffn1", "g_mix", "g_cross", "g_mem", "g_ffn2", "g_final")
SMALL = GAINS + ("b_gate", "conv_w")
SMALL_R = 16
LOSS_ROW = 11
WEIGHT_ORDER = ("g_ffn1", "w_ffn1_gu", "w_ffn1_down", "g_mix", "w_in", "b_gate", "conv_w", "w_conv_out",
                "w_attn_out", "w_o", "g_cross", "g_mem", "w_cq", "w_ckv", "w_co", "g_ffn2", "w_ffn2_gu",
                "w_ffn2_down", "g_final")
GU_NAMES = ("w_ffn1_gu", "w_ffn2_gu")


def _pack_small(vals, conv_rows):
    rows = [vals[n].reshape(1, D) for n in GAINS] + [vals["b_gate"].reshape(2, D), conv_rows.reshape(CONV_K, D)]
    used = len(GAINS) + 2 + CONV_K
    return jnp.concatenate(rows + [jnp.zeros((SMALL_R - used, D), F32)], axis=0)


def _exchange_shards(wts):
    out = {n: jnp.pad(wts[n].T.astype(BF16), ((0, FF_PAD - FF_BLK), (0, 0))) for n in GU_NAMES}
    for n in ("w_ckv", "w_in", "w_ffn1_down", "w_ffn2_down"):
        out[n] = wts[n].astype(BF16)
    out["mix"] = jnp.concatenate([wts[n].astype(BF16) for n in MIX_MATS], axis=0)
    out["cross"] = jnp.concatenate([wts[n].astype(BF16) for n in CROSS_MATS], axis=0)
    return out


def _reduce_group(grads, landed, core, names):
    return _pair_sum(grads, landed, core, "grads_pair_sum_" + "_".join(names))


def _step(x, mem, target, sh, conv_pad, gains, b_gate, core):
    wg1, wd1, conv_all = _run_exchange(_gather_exchange([sh["w_ffn1_gu"], sh["w_ffn1_down"], conv_pad]), "gather_ffn1")
    conv_w = conv_all[:, :CONV_K, :].transpose(1, 0, 2).reshape(CONV_K, D)
    (n1, gate1, up1, act1, h1), (w_in,) = _ffn_fwd(
        x, gains["g_ffn1"], wg1, wd1, "ffn1_fwd", comm=_gather_exchange([sh["w_in"]]))
    (u, pcg, qkv, yc), (w_mix,) = _inproj_fwd(h1, gains["g_mix"], w_in, conv_w, "inproj_fwd",
                                              comm=_gather_exchange([sh["mix"]]))
    (ysb, ctot), (w_cross, w_ckv, wg2, wd2) = _sb_fwd(
        qkv, "sb_fwd", comm=_gather_exchange([sh["cross"], sh["w_ckv"], sh["w_ffn2_gu"], sh["w_ffn2_down"]]))
    (a_mix, b_mix, merged, h2), _ = _mix_out_fwd(yc, ysb, pcg, b_gate, h1, w_mix, "mix_out_fwd")
    hn, qx, o_x, h3, mn, kv = _cross_fwd(h2, gains["g_cross"], mem, gains["g_mem"], w_ckv, w_cross, "cross_fwd")
    (n4, gate2, up2, act2, dh4, loss, dg_final), _ = _ffn_fwd(h3, gains["g_ffn2"], wg2, wd2, "ffn2_fwd",
                                                              head=(gains["g_final"], target))

    gs = {"g_final": dg_final}
    (dgu2, dh4b, dh3, gs["g_ffn2"]), _ = _ffn_bwd(dh4, h3, gains["g_ffn2"], gate2, up2, wg2, wd2, "ffn2_bwd")
    grads_a = [_mm_tn_rows(dgu2, n4, FF_PAD, "dw_ffn2_gu"),
               _mm_tn_rows(act2, dh4b, FF_BLK, "dw_ffn2_down").reshape(N_DEV, DOWN_ROWS, D)]
    names_a = ["w_ffn2_gu", "w_ffn2_down"]
    (dh3b, dqx, dkv, dh2, gs["g_cross"], gs["g_mem"]), _ = _cross_bwd(
        dh3, h2, gains["g_cross"], qx, kv, mem, gains["g_mem"], w_ckv, w_cross, "cross_bwd")
    grads_b = [_mm_tn_cols(mn, dkv, "dw_ckv"), _mm_tn_squares([(hn, dqx), (o_x, dh3b)], "dw_cross")]
    names_b = ["w_ckv", "cross"]
    (dh2b, da_mix, db_mix, dgp, dconv, dysb, gs["b_gate"], gs["conv_w"]), landed_ab = _mix_out_bwd(
        dh2, a_mix, b_mix, pcg, b_gate, conv_w, w_mix, "mix_out_bwd", comm=_pair_exchange(grads_a + grads_b))
    sums_ab = _reduce_group(grads_a + grads_b, landed_ab, core, names_a + names_b)
    grads_c = [_mm_tn_squares([(yc, da_mix), (ysb, db_mix), (merged, dh2b)], "dw_mix")]
    flight_ab = _chip_exchange_begin(sums_ab, "grads_to_chips_early_begin")
    (dq, dkv_sb), _ = _sb_bwd(qkv, dysb, ctot, flight_ab.token, "sb_bwd")
    grads_d = [_mm_tn_cols_many(u, [dconv, dq[None], dkv_sb, dgp], "dw_in")]
    (dh1, gs["g_mix"]), landed_cd = _inproj_bwd(dconv, dq, dkv_sb, dgp, w_in, h1, gains["g_mix"], dh2, "inproj_bwd",
                                                comm=_pair_exchange(grads_c + grads_d))
    sums_cd = _reduce_group(grads_c + grads_d, landed_cd, core, ["mix", "w_in"])
    flight_d = _chip_exchange_begin(sums_cd, "grads_to_chips_w_in_begin")
    (dgu1, dh1b, dx, gs["g_ffn1"]), _ = _ffn_bwd(dh1, x, gains["g_ffn1"] + flight_d.token[0, 0], gate1, up1, wg1, wd1,
                                                 "ffn1_bwd")
    dw_gu1 = _mm_tn_rows(dgu1, n1, FF_PAD, "dw_ffn1_gu")
    small_mine = _pack_small({n: gs[n] for n in GAINS + ("b_gate",)}, gs["conv_w"][:CONV_K])
    small_mine = small_mine.at[LOSS_ROW, 0].set(loss[0, 0])
    dw_down1, (landed_gu1, small_all) = _mm_tn_rows(
        act1, dh1b, FF_BLK, "dw_ffn1_down", comm=_both(_pair_exchange([dw_gu1]), _direct_gather(small_mine)))
    flight_gu1 = _chip_exchange_begin(_reduce_group([dw_gu1], [landed_gu1], core, ["w_ffn1_gu"]),
                                      "grads_to_chips_ffn1_gu_begin")
    grads_down1 = [dw_down1.reshape(N_DEV, DOWN_ROWS, D)]
    landed_down1 = _run_exchange(_both(_pair_exchange(grads_down1), _after(flight_gu1.token)),
                                 "grads_to_sibling_ffn1_down")
    flight_down1 = _chip_exchange_begin(_reduce_group(grads_down1, landed_down1, core, ["w_ffn1_down"]),
                                        "grads_to_chips_ffn1_down_begin")
    flights = [(names_a + names_b, flight_ab), (["mix", "w_in"], flight_d), (["w_ffn1_gu"], flight_gu1),
               (["w_ffn1_down"], flight_down1)]
    return dx, flights, small_all


def kernel(x, mem, g_ffn1, w_ffn1_gu, w_ffn1_down, g_mix, w_in, b_gate, conv_w, w_conv_out, w_attn_out, w_o, g_cross, g_mem, w_cq, w_ckv, w_co, g_ffn2, w_ffn2_gu, w_ffn2_down, g_final, loss_target, m_g_ffn1, m_w_ffn1_gu, m_w_ffn1_down, m_g_mix, m_w_in, m_b_gate, m_conv_w, m_w_conv_out, m_w_attn_out, m_w_o, m_g_cross, m_g_mem, m_w_cq, m_w_ckv, m_w_co, m_g_ffn2, m_w_ffn2_gu, m_w_ffn2_down, m_g_final, v_g_ffn1, v_w_ffn1_gu, v_w_ffn1_down, v_g_mix, v_w_in, v_b_gate, v_conv_w, v_w_conv_out, v_w_attn_out, v_w_o, v_g_cross, v_g_mem, v_w_cq, v_w_ckv, v_w_co, v_g_ffn2, v_w_ffn2_gu, v_w_ffn2_down, v_g_final):
    args = locals()
    wts = {n: args[n] for n in WEIGHT_ORDER}
    mom1 = {n: args["m_" + n] for n in WEIGHT_ORDER}
    mom2 = {n: args["v_" + n] for n in WEIGHT_ORDER}
    cx, cy, cc = _mesh_pos()
    dev = 4 * cx + 2 * cy + cc
    conv_cols = D // N_DEV

    conv_pad = jnp.concatenate([conv_w, jnp.zeros((SMALL_R - CONV_K, conv_cols), F32)], axis=0)
    gains = {n: wts[n].reshape(1, D) for n in GAINS}
    dx, flights, small_all = _step(x[0], mem[0], loss_target[0], _exchange_shards(wts), conv_pad, gains,
                                 b_gate.reshape(1, 2 * D), cc.reshape(1).astype(jnp.int32))

    grads, delta, new_m, new_v = {}, {}, {}, {}

    def operands(n, transposed):
        trio = (wts[n], mom1[n], mom2[n])
        return tuple(a.T for a in trio) if transposed else trio

    def record(n, res, transposed):
        grads[n], delta[n], new_m[n], new_v[n] = [r.T for r in res] if transposed else res

    early = [("w_ffn2_gu", "w_ffn2_gu", 0, True), ("w_ffn2_down", "w_ffn2_down", 0, False),
             ("w_ckv", "w_ckv", 0, False), ("w_in", "w_in", 0, False)]
    early += [(n, "mix", k, False) for k, n in enumerate(MIX_MATS)]
    early += [(n, "cross", k, False) for k, n in enumerate(CROSS_MATS)]
    chip = (2 * cx + cy).reshape(1).astype(jnp.int32)
    (names_early, flight_early), (names_w_in, flight_w_in), *last_flights = flights
    token = last_flights[-1][1].token
    own, land = {}, {}
    for names, flight, tag in ((names_early, flight_early, "early"), (names_w_in, flight_w_in, "w_in")):
        own_parts, landed = _chip_exchange_end(flight, token, "grads_to_chips_%s_end" % tag)
        own.update(zip(names, own_parts))
        land.update(zip(names, landed))
    for n, buf, row_block, transposed in early:
        w, m1, m2 = operands(n, transposed)
        *res, token = _adamw_own(w, land[buf], own[buf], chip, m1, m2, "adamw_" + n, row_block, token)
        record(n, res, transposed)

    for ((n,), flight), transposed in zip(last_flights, (True, False)):
        (own_n,), (land_n,) = _chip_exchange_end(flight, token, "grads_to_chips_%s_end" % n)
        w, m1, m2 = operands(n, transposed)
        *res, token = _adamw_own(w, land_n, own_n, chip, m1, m2, "adamw_" + n, token=token)
        record(n, res, transposed)

    loss, small = _adamw_small(small_all, dev.reshape(1).astype(jnp.int32), wts, mom1, mom2, "adamw_small")
    for n in SMALL:
        grads[n], delta[n], new_m[n], new_v[n] = small[n]

    return (loss, dx[None], *[grads[n] for n in WEIGHT_ORDER], *[delta[n] for n in WEIGHT_ORDER],
            *[new_m[n] for n in WEIGHT_ORDER], *[new_v[n] for n in WEIGHT_ORDER])
```

```python
import types

import jax
import jax.numpy as jnp
from jax import lax
from jax.experimental import pallas as pl
from jax.experimental.pallas import tpu as pltpu

F32 = jnp.float32
BF16 = jnp.bfloat16

D = 1024
DFF = 2816
SB_H = 8
SB_DH = 128
X_H = 4
X_DH = 256
CONV_K = 3
RMS_EPS = 1e-6
N_DEV = 8
N_CHIP = 4
SQ_ROWS = D // N_DEV

ADAM_LR = 0.001
ADAM_B1 = 0.9
ADAM_B2 = 0.999
ADAM_EPS = 1e-08
ADAM_WD = 0.01
ADAM_STEP = 10

TM = 256
TQ = 512
TK = 256
SB_HPS = 2
VMEM_LIMIT = 56 << 20

FF_BLK = DFF // 4
FF_PAD = 768
FF_SUB = 256
DOWN_ROWS = DFF // N_DEV

MIX_MATS = ("w_conv_out", "w_attn_out", "w_o")
CROSS_MATS = ("w_cq", "w_co")

_ANY = pl.BlockSpec(memory_space=pl.ANY)


def _cparams(n_axes=1):
    return pltpu.CompilerParams(
        dimension_semantics=("arbitrary",) * n_axes, vmem_limit_bytes=VMEM_LIMIT)


def _row_spec(tm, n):
    return pl.BlockSpec((tm, n), lambda i: (i, 0))


def _blk_row_spec(nb, tm, n):
    return pl.BlockSpec((nb, tm, n), lambda i: (0, i, 0))


def _const_spec(shape):
    zeros = (0,) * len(shape)
    return pl.BlockSpec(shape, lambda i: zeros)


def _dot(a, b):
    return jnp.dot(a, b, preferred_element_type=F32)


def _dot_nt(a, b):
    return lax.dot_general(a, b, (((1,), (1,)), ((), ())), preferred_element_type=F32)


def _dot_tn(a, b):
    return lax.dot_general(a, b, (((0,), (0,)), ((), ())), preferred_element_type=F32)


def _sigmoid(x):
    return 1.0 / (1.0 + jnp.exp(-x))


def _call(body, operands, *, grid, in_specs, out_specs, out_shape, scratch_shapes, name, comm=None):
    n_in, n_out, n_sc = len(in_specs), len(out_specs), len(scratch_shapes)
    if comm is None:
        outs = pl.pallas_call(
            body, grid=grid, name=name, in_specs=in_specs, out_specs=out_specs, out_shape=out_shape,
            scratch_shapes=scratch_shapes, compiler_params=_cparams(len(grid)))(*operands)
        return list(outs), []
    c_in, c_out, c_sem = len(comm.inputs), len(comm.out_shapes), len(comm.sem_shapes)

    def hosted(*refs):
        bounds = [0, n_in, c_in, n_out, c_out, n_sc, c_sem]
        parts, pos = [], 0
        for k in bounds[1:]:
            parts.append(refs[pos:pos + k])
            pos += k
        ins, cins, outs, couts, scr, sems = parts
        step, n_steps = pl.program_id(0), grid[0]
        for ax in range(1, len(grid)):
            step, n_steps = step * grid[ax] + pl.program_id(ax), n_steps * grid[ax]

        @pl.when(step == 0)
        def _():
            comm.start(cins, couts, sems)

        @pl.when(step == (2 * n_steps) // 3)
        def _():
            comm.middle(cins, couts, sems)

        body(*ins, *outs, *scr)

        @pl.when(step == n_steps - 1)
        def _():
            comm.finish(cins, couts, sems)

    res = pl.pallas_call(
        hosted, grid=grid, name=name, in_specs=list(in_specs) + [_ANY] * c_in,
        out_specs=list(out_specs) + [_ANY] * c_out, out_shape=list(out_shape) + list(comm.out_shapes),
        scratch_shapes=list(scratch_shapes) + list(comm.sem_shapes),
        compiler_params=_cparams(len(grid)))(*operands, *comm.inputs)
    return list(res[:n_out]), list(res[n_out:])


def _load_resident(step, pairs, sems):
    @pl.when(step == 0)
    def _():
        copies = [pltpu.make_async_copy(src, dst, sems.at[k]) for k, (src, dst) in enumerate(pairs)]
        for cp in copies:
            cp.start()
        for cp in copies:
            cp.wait()


def _square_pairs(buf_hbm, index, dst):
    off = index * SQ_ROWS
    return [(buf_hbm.at[d, off:off + SQ_ROWS, :], dst.at[d * SQ_ROWS:(d + 1) * SQ_ROWS, :]) for d in range(N_DEV)]


def _down_pairs(wd_hbm, dst):
    return [(wd_hbm.at[d], dst.at[d // 2, (d % 2) * DOWN_ROWS:(d % 2 + 1) * DOWN_ROWS, :]) for d in range(N_DEV)]


def _zero_down_pad(step, dst):
    @pl.when(step == 0)
    def _():
        dst[:, FF_BLK:, :] = jnp.zeros((4, FF_PAD - FF_BLK, D), BF16)


def _rms_fwd_tile(xt, g):
    r = lax.rsqrt(jnp.mean(xt * xt, axis=-1, keepdims=True) + RMS_EPS)
    return (xt * r) * g


def _rms_bwd_tile(xt, g, dn):
    r = lax.rsqrt(jnp.mean(xt * xt, axis=-1, keepdims=True) + RMS_EPS)
    xhat = xt * r
    dxhat = dn * g
    dx = r * (dxhat - xhat * jnp.mean(dxhat * xhat, axis=-1, keepdims=True))
    dg = jnp.sum(dn * xhat, axis=0, keepdims=True)
    return dx, dg


def _accumulate(ref, step, value):
    @pl.when(step == 0)
    def _():
        ref[...] = value

    @pl.when(step != 0)
    def _():
        ref[...] = ref[...] + value


def _ffn_fwd(x, g, wgu, wd, name, comm=None, head=None):
    t = x.shape[0]

    def body(x_ref, g_ref, wgu_hbm, wd_hbm, *refs):
        if head is None:
            n_ref, gate_ref, up_ref, act_ref, h_ref, wgu_v, wd_v, sems = refs
        else:
            gf_ref, t_ref, n_ref, gate_ref, up_ref, act_ref, dh_ref, loss_ref, dgf_ref, wgu_v, wd_v, sems = refs
        step = pl.program_id(0)
        _zero_down_pad(step, wd_v)
        _load_resident(step, [(wgu_hbm, wgu_v)] + _down_pairs(wd_hbm, wd_v), sems)
        xt = x_ref[...]
        n = _rms_fwd_tile(xt, g_ref[...]).astype(BF16)
        n_ref[...] = n
        acc = jnp.zeros((TM, D), F32)
        for j in range(4):
            for s in range(FF_PAD // FF_SUB):
                lo, hi = s * FF_SUB, (s + 1) * FF_SUB
                gt = _dot_nt(n, wgu_v[j, lo:hi, :])
                ut = _dot_nt(n, wgu_v[4 + j, lo:hi, :])
                gate_ref[j, :, lo:hi] = gt.astype(BF16)
                up_ref[j, :, lo:hi] = ut.astype(BF16)
                act_ref[j, :, lo:hi] = ((gt * _sigmoid(gt)) * ut).astype(BF16)
            acc = acc + _dot(act_ref[j], wd_v[j])
        ht = xt + 0.5 * acc
        if head is None:
            h_ref[...] = ht
        else:
            gain = gf_ref[...]
            diff = _rms_fwd_tile(ht, gain) - t_ref[...]
            part = 0.5 * jnp.sum(jnp.sum(diff * diff, axis=-1, keepdims=True) / D, axis=0, keepdims=True)
            dx, dg = _rms_bwd_tile(ht, gain, diff / D)
            dh_ref[...] = dx
            _accumulate(loss_ref, step, jnp.broadcast_to(part, (8, 128)))
            _accumulate(dgf_ref, step, dg)

    ff = jax.ShapeDtypeStruct((4, t, FF_PAD), BF16)
    operands, in_specs = (x, g, wgu, wd), [_row_spec(TM, D), _const_spec((1, D)), _ANY, _ANY]
    out_specs = [_row_spec(TM, D)] + [_blk_row_spec(4, TM, FF_PAD)] * 3 + [_row_spec(TM, D)]
    out_shape = [jax.ShapeDtypeStruct((t, D), BF16), ff, ff, ff, jax.ShapeDtypeStruct((t, D), F32)]
    if head is not None:
        operands += tuple(head)
        in_specs += [_const_spec((1, D)), _row_spec(TM, D)]
        out_specs += [_const_spec((8, 128)), _const_spec((1, D))]
        out_shape += [jax.ShapeDtypeStruct((8, 128), F32), jax.ShapeDtypeStruct((1, D), F32)]
    return _call(
        body, operands, grid=(t // TM,), name=name, comm=comm, in_specs=in_specs, out_specs=out_specs,
        out_shape=out_shape,
        scratch_shapes=[pltpu.VMEM((N_DEV, FF_PAD, D), BF16), pltpu.VMEM((4, FF_PAD, D), BF16),
                        pltpu.SemaphoreType.DMA((1 + N_DEV,))])


def _ffn_bwd(dh, xin, g, gate, up, wgu, wd, name, comm=None):
    t = dh.shape[0]

    def body(dh_ref, x_ref, g_ref, gate_ref, up_ref, wgu_hbm, wd_hbm,
             dgu_ref, dhb_ref, dx_ref, dg_ref, wgu_v, wd_v, sems):
        step = pl.program_id(0)
        _zero_down_pad(step, wd_v)
        _load_resident(step, [(wgu_hbm, wgu_v)] + _down_pairs(wd_hbm, wd_v), sems)
        dht = dh_ref[...]
        dhb = (0.5 * dht).astype(BF16)
        dhb_ref[...] = dhb
        dn = jnp.zeros((TM, D), F32)
        for j in range(4):
            for s in range(FF_PAD // FF_SUB):
                lo, hi = s * FF_SUB, (s + 1) * FF_SUB
                da = _dot_nt(dhb, wd_v[j, lo:hi, :])
                gt = gate_ref[j, :, lo:hi].astype(F32)
                ut = up_ref[j, :, lo:hi].astype(F32)
                sg = _sigmoid(gt)
                dgt = (da * ut * (sg * (1.0 + gt * (1.0 - sg)))).astype(BF16)
                dut = (da * (gt * sg)).astype(BF16)
                dgu_ref[j, :, lo:hi] = dgt
                dgu_ref[4 + j, :, lo:hi] = dut
            dn = dn + _dot(dgu_ref[j], wgu_v[j]) + _dot(dgu_ref[4 + j], wgu_v[4 + j])
        dx, dg = _rms_bwd_tile(x_ref[...], g_ref[...], dn)
        dx_ref[...] = dht + dx
        _accumulate(dg_ref, step, dg)

    return _call(
        body, (dh, xin, g, gate, up, wgu, wd), grid=(t // TM,), name=name, comm=comm,
        in_specs=[_row_spec(TM, D), _row_spec(TM, D), _const_spec((1, D)), _blk_row_spec(4, TM, FF_PAD),
                  _blk_row_spec(4, TM, FF_PAD), _ANY, _ANY],
        out_specs=[_blk_row_spec(N_DEV, TM, FF_PAD), _row_spec(TM, D), _row_spec(TM, D), _const_spec((1, D))],
        out_shape=[jax.ShapeDtypeStruct((N_DEV, t, FF_PAD), BF16), jax.ShapeDtypeStruct((t, D), BF16),
                   jax.ShapeDtypeStruct((t, D), F32), jax.ShapeDtypeStruct((1, D), F32)],
        scratch_shapes=[pltpu.VMEM((N_DEV, FF_PAD, D), BF16), pltpu.VMEM((4, FF_PAD, D), BF16),
                        pltpu.SemaphoreType.DMA((1 + N_DEV,))])


WIDE_TILES = (1024, 512, 256, 128)


def _pick_tile(n, options=(512, 256, 128)):
    for o in options:
        if n % o == 0:
            return o
    return n


def _mm_tn_squares(pairs, name):
    k, d = pairs[0][0].shape
    n = pairs[0][1].shape[1]
    tn = _pick_tile(n)
    count = len(pairs)

    def body(*refs):
        m = pl.program_id(0)
        for idx in range(count):
            @pl.when(m == idx)
            def _(idx=idx):
                refs[-1][...] = _dot_tn(refs[idx][...], refs[count + idx][...]).astype(BF16).reshape(
                    N_DEV, d // N_DEV, tn)

    a_specs = [pl.BlockSpec((k, d), lambda m, j: (0, 0)) for _ in pairs]
    b_specs = [pl.BlockSpec((k, tn), lambda m, j, idx=idx: (0, jnp.where(m == idx, j, 0))) for idx in range(count)]
    return pl.pallas_call(
        body, grid=(count, n // tn), name=name, in_specs=a_specs + b_specs,
        out_specs=pl.BlockSpec((N_DEV, d // N_DEV, tn), lambda m, j: (0, m, j)),
        out_shape=jax.ShapeDtypeStruct((N_DEV, count * (d // N_DEV), n), BF16),
        compiler_params=_cparams(2),
    )(*[a for a, _ in pairs], *[b for _, b in pairs])


def _mm_tn_cols_many(a, parts, name):
    k, m = a.shape
    n = parts[0].shape[2]
    tn = _pick_tile(n)
    firsts, total = [], 0
    for p in parts:
        firsts.append(total)
        total += p.shape[0]

    def body(a_ref, *refs):
        j = pl.program_id(0)
        for p, first, ref in zip(parts, firsts, refs):
            @pl.when(jnp.logical_and(j >= first, j < first + p.shape[0]))
            def _(ref=ref):
                refs[-1][0] = _dot_tn(a_ref[...].astype(BF16), ref[0].astype(BF16)).astype(BF16)

    specs = [pl.BlockSpec((1, k, tn), lambda j, i, first=first, last=p.shape[0] - 1:
                          (jnp.clip(j - first, 0, last), 0, jnp.where(jnp.logical_and(j >= first, j <= first + last), i, 0)))
             for p, first in zip(parts, firsts)]
    return pl.pallas_call(
        body, grid=(total, n // tn), name=name,
        in_specs=[pl.BlockSpec((k, m), lambda j, i: (0, 0))] + specs,
        out_specs=pl.BlockSpec((1, m, tn), lambda j, i: (j, 0, i)),
        out_shape=jax.ShapeDtypeStruct((total, m, n), BF16),
        compiler_params=_cparams(2),
    )(a, *parts)


def _mm_tn_cols(a, b, name):
    k, m = a.shape
    nb, _, n = b.shape
    tm = _pick_tile(m, WIDE_TILES)

    def body(a_ref, b_ref, o_ref):
        o_ref[0] = _dot_tn(a_ref[...].astype(BF16), b_ref[0].astype(BF16)).astype(BF16)

    return pl.pallas_call(
        body, grid=(nb, m // tm), name=name,
        in_specs=[pl.BlockSpec((k, tm), lambda j, i: (0, i)), pl.BlockSpec((1, k, n), lambda j, i: (j, 0, 0))],
        out_specs=pl.BlockSpec((1, tm, n), lambda j, i: (j, i, 0)),
        out_shape=jax.ShapeDtypeStruct((nb, m, n), BF16),
        compiler_params=_cparams(2),
    )(a, b)


def _mm_tn_rows(a, b, keep, name, comm=None):
    nb, k, m = a.shape
    _, n = b.shape
    tn = _pick_tile(n, WIDE_TILES)

    def body(a_ref, b_ref, o_ref):
        o_ref[0] = _dot_tn(a_ref[0], b_ref[...])[:keep].astype(BF16)

    (out,), couts = _call(
        body, (a, b), grid=(nb, n // tn), name=name, comm=comm,
        in_specs=[pl.BlockSpec((1, k, m), lambda j, i: (j, 0, 0)), pl.BlockSpec((k, tn), lambda j, i: (0, i))],
        out_specs=[pl.BlockSpec((1, keep, tn), lambda j, i: (j, 0, i))],
        out_shape=[jax.ShapeDtypeStruct((nb, keep, n), BF16)], scratch_shapes=[])
    return out if comm is None else (out, couts)


PCG_W = 5 * D
QKV_W = 3 * D
PROJ_SUB = 512


def _inproj_fwd(h, g, w_in, conv_w, name, comm=None):
    t = h.shape[0]

    def body(h_ref, g_ref, w_hbm, cw_ref, u_ref, pcg_ref, qkv_ref, yc_ref, w_v, tail_v, sems):
        step = pl.program_id(0)
        _load_resident(step, [(w_hbm, w_v)], sems)

        @pl.when(step == 0)
        def _():
            tail_v[...] = jnp.zeros_like(tail_v)

        u = _rms_fwd_tile(h_ref[...], g_ref[...]).astype(BF16)
        u_ref[...] = u
        for blk in range(N_DEV):
            for s in range(D // PROJ_SUB):
                lo, hi = s * PROJ_SUB, (s + 1) * PROJ_SUB
                p = _dot(u, w_v[blk, :, lo:hi])
                if blk < 3:
                    pcg_ref[:, blk * D + lo:blk * D + hi] = p
                elif blk < 6:
                    qkv_ref[:, (blk - 3) * D + lo:(blk - 3) * D + hi] = p.astype(BF16)
                else:
                    pcg_ref[:, (blk - 3) * D + lo:(blk - 3) * D + hi] = p
        xc = pcg_ref[:, D:2 * D] * pcg_ref[:, 2 * D:3 * D]
        ext = jnp.concatenate([tail_v[...], xc], axis=0)
        conv = (cw_ref[0:1, :] * pltpu.roll(ext, 2, 0)[8:] + cw_ref[1:2, :] * pltpu.roll(ext, 1, 0)[8:]
                + cw_ref[2:3, :] * xc)
        yc_ref[...] = (pcg_ref[:, 0:D] * conv).astype(BF16)
        tail_v[...] = xc[TM - 8:]

    return _call(
        body, (h, g, w_in, conv_w), grid=(t // TM,), name=name, comm=comm,
        in_specs=[_row_spec(TM, D), _const_spec((1, D)), _ANY, _const_spec((CONV_K, D))],
        out_specs=[_row_spec(TM, D), _row_spec(TM, PCG_W), _row_spec(TM, QKV_W), _row_spec(TM, D)],
        out_shape=[jax.ShapeDtypeStruct((t, D), BF16), jax.ShapeDtypeStruct((t, PCG_W), F32),
                   jax.ShapeDtypeStruct((t, QKV_W), BF16), jax.ShapeDtypeStruct((t, D), BF16)],
        scratch_shapes=[pltpu.VMEM((N_DEV, D, D), BF16), pltpu.VMEM((8, D), F32), pltpu.SemaphoreType.DMA((1,))])


def _tri2(cond):
    rr = lax.broadcasted_iota(jnp.int32, (2 * TK, TK), 0) & (TK - 1)
    cc = lax.broadcasted_iota(jnp.int32, (2 * TK, TK), 1)
    return cond(rr, cc).astype(BF16)


def _causal(shift, row0=0):
    rr = lax.broadcasted_iota(jnp.int32, (TQ - row0, TK), 0) + row0
    cc = lax.broadcasted_iota(jnp.int32, (TQ - row0, TK), 1)
    return cc + shift < rr


def _cumdot(v, tri2):
    hi = v.astype(BF16)
    lo = (v - hi.astype(F32)).astype(BF16)
    return _dot(jnp.concatenate([hi, lo], axis=1), tri2)


LOG2_E = 1.4426950408889634


def _log_1m_beta(z):
    return -(jnp.maximum(z, 0.0) + jnp.log2(1.0 + jnp.exp2(-jnp.abs(z))))


def _sb_specs(t):
    g = SB_H // SB_HPS
    w = SB_HPS * SB_DH
    q_spec = pl.BlockSpec((TQ, w), lambda h, i: (i, h))
    k_spec = pl.BlockSpec((t, w), lambda h, i: (0, g + h))
    v_spec = pl.BlockSpec((t, w), lambda h, i: (0, 2 * g + h))
    ct_spec = pl.BlockSpec((SB_HPS, TQ, 1), lambda h, i: (h, i, 0))
    return g, w, q_spec, k_spec, v_spec, ct_spec


def _sb_fwd(qkv, name, comm=None):
    t = qkv.shape[0]
    scale = SB_DH ** -0.5
    g, w, q_spec, k_spec, v_spec, ct_spec = _sb_specs(t)

    def body(q_ref, k_ref, v_ref, y_ref, ct_ref):
        i = pl.program_id(1)
        later = _tri2(lambda j, s: j > s)
        n_diag = TQ // TK

        def block(j, carry, shift):
            off = pl.multiple_of(j * TK, TK)
            zs, ms = [], []
            for hd in range(SB_HPS):
                cols = slice(hd * SB_DH, (hd + 1) * SB_DH)
                z = _dot_nt(q_ref[:, cols], k_ref[pl.ds(off, TK), cols]) * (scale * LOG2_E)
                m = _log_1m_beta(z)
                if shift is not None:
                    m = jnp.where(_causal(shift), m, 0.0)
                zs.append(z)
                ms.append(m)
            after = _cumdot(jnp.concatenate(ms, axis=0), later)
            out = []
            for hd in range(SB_HPS):
                acc, c_sum = carry[hd]
                cols = slice(hd * SB_DH, (hd + 1) * SB_DH)
                a = jnp.exp2((ms[hd] + zs[hd]) + (c_sum + after[hd * TQ:(hd + 1) * TQ]))
                if shift is not None:
                    a = jnp.where(_causal(shift), a, 0.0)
                out.append((acc + _dot(a.astype(BF16), v_ref[pl.ds(off, TK), cols]),
                            c_sum + jnp.sum(ms[hd], axis=1, keepdims=True)))
            return tuple(out)

        carry = tuple((jnp.zeros((TQ, SB_DH), F32), jnp.zeros((TQ, 1), F32)) for _ in range(SB_HPS))
        for d in reversed(range(n_diag)):
            carry = block(i * n_diag + d, carry, d * TK)
        carry = lax.fori_loop(0, i * n_diag, lambda jj, c: block(i * n_diag - 1 - jj, c, None), carry)
        for hd in range(SB_HPS):
            y_ref[:, hd * SB_DH:(hd + 1) * SB_DH] = carry[hd][0].astype(BF16)
            ct_ref[hd] = carry[hd][1]

    return _call(
        body, (qkv, qkv, qkv), grid=(g, t // TQ), name=name, comm=comm,
        in_specs=[q_spec, k_spec, v_spec],
        out_specs=[q_spec, ct_spec],
        out_shape=[jax.ShapeDtypeStruct((t, D), BF16), jax.ShapeDtypeStruct((SB_H, t, 1), F32)],
        scratch_shapes=[])


def _sb_bwd(qkv, dy, ctot, after, name, comm=None):
    t = qkv.shape[0]
    scale = SB_DH ** -0.5
    g, w, q_spec, k_spec, v_spec, ct_spec = _sb_specs(t)
    acc_spec = pl.BlockSpec((2, t, w), lambda h, i: (0, 0, h))

    def body(q_ref, k_ref, v_ref, dy_ref, ct_ref, after_ref, dq_ref, dkv_ref):
        i = pl.program_id(1)

        @pl.when(i == 0)
        def _():
            dkv_ref[...] = jnp.zeros_like(dkv_ref)

        upto = _tri2(lambda j, s: j <= s)
        n_diag = TQ // TK

        def block(j, carry, shift):
            off = pl.multiple_of(j * TK, TK)
            r0 = 0 if shift is None else shift
            nr = TQ - r0
            causal = None if shift is None else _causal(shift, r0)

            def grow(old, delta):
                return old + delta if r0 == 0 else jnp.concatenate([old[:r0], old[r0:] + delta], axis=0)

            zs, ms = [], []
            for hd in range(SB_HPS):
                cols = slice(hd * SB_DH, (hd + 1) * SB_DH)
                z = _dot_nt(q_ref[r0:, cols], k_ref[pl.ds(off, TK), cols]) * (scale * LOG2_E)
                m = _log_1m_beta(z)
                if causal is not None:
                    m = jnp.where(causal, m, 0.0)
                zs.append(z)
                ms.append(m)
            m_upto = _cumdot(jnp.concatenate(ms, axis=0), upto)
            ls, a_s, es = [], [], []
            for hd in range(SB_HPS):
                cols = slice(hd * SB_DH, (hd + 1) * SB_DH)
                l = ms[hd] + zs[hd]
                a = jnp.exp2(l + ((ct_ref[hd, r0:] - carry[hd][1][r0:]) - m_upto[hd * nr:(hd + 1) * nr]))
                if causal is not None:
                    a = jnp.where(causal, a, 0.0)
                ls.append(l)
                a_s.append(a)
                es.append(_dot_nt(dy_ref[r0:, cols], v_ref[pl.ds(off, TK), cols]) * a)
            e_upto = _dot(jnp.concatenate(es, axis=0).astype(BF16), upto[:TK])
            out = []
            for hd in range(SB_HPS):
                dq, p_sum, e_sum = carry[hd]
                cols = slice(hd * SB_DH, (hd + 1) * SB_DH)
                e = es[hd]
                dz = e - jnp.exp2(ls[hd]) * (e_sum[r0:] + e_upto[hd * nr:(hd + 1) * nr])
                if causal is not None:
                    dz = jnp.where(causal, dz, 0.0)
                dzs = (dz * scale).astype(BF16)
                dkv_ref[0, pl.ds(off, TK), cols] += _dot_tn(dzs, q_ref[r0:, cols])
                dkv_ref[1, pl.ds(off, TK), cols] += _dot_tn(a_s[hd].astype(BF16), dy_ref[r0:, cols])
                out.append((grow(dq, _dot(dzs, k_ref[pl.ds(off, TK), cols])),
                            grow(p_sum, jnp.sum(ms[hd], axis=1, keepdims=True)),
                            grow(e_sum, jnp.sum(e, axis=1, keepdims=True))))
            return tuple(out)

        zero = jnp.zeros((TQ, 1), F32)
        init = tuple((jnp.zeros((TQ, SB_DH), F32), zero, zero) for _ in range(SB_HPS))
        carry = lax.fori_loop(0, i * n_diag, lambda j, c: block(j, c, None), init)
        for d in range(n_diag):
            carry = block(i * n_diag + d, carry, d * TK)
        for hd in range(SB_HPS):
            dq_ref[:, hd * SB_DH:(hd + 1) * SB_DH] = carry[hd][0].astype(BF16)

    return _call(
        body, (qkv, qkv, qkv, dy, ctot, after), grid=(g, t // TQ), name=name, comm=comm,
        in_specs=[q_spec, k_spec, v_spec, q_spec, ct_spec, pl.BlockSpec(after.shape, lambda h, i: (0, 0))],
        out_specs=[q_spec, acc_spec],
        out_shape=[jax.ShapeDtypeStruct((t, D), BF16), jax.ShapeDtypeStruct((2, t, D), F32)],
        scratch_shapes=[])


def _gate_specs():
    return [pl.BlockSpec((TM, D), lambda i: (i, 3)), pl.BlockSpec((TM, D), lambda i: (i, 4))]


def _mix_pairs(mix_hbm, dsts):
    pairs = []
    for index, dst in enumerate(dsts):
        pairs += _square_pairs(mix_hbm, index, dst)
    return pairs


def _mix_out_fwd(yc, ysb, pcg, b_gate, h, w_mix, name, comm=None):
    t = h.shape[0]

    def body(yc_ref, ysb_ref, gc_ref, gs_ref, b_ref, h_ref, mix_hbm,
             a_ref, b_out_ref, mg_ref, h2_ref, wc_v, wa_v, wo_v, sems):
        _load_resident(pl.program_id(0), _mix_pairs(mix_hbm, (wc_v, wa_v, wo_v)), sems)
        a = _dot(yc_ref[...], wc_v[...])
        b = _dot(ysb_ref[...], wa_v[...])
        merged = (_sigmoid(gc_ref[...] + b_ref[:, :D]) * a + _sigmoid(gs_ref[...] + b_ref[:, D:]) * b).astype(BF16)
        a_ref[...] = a
        b_out_ref[...] = b
        mg_ref[...] = merged
        h2_ref[...] = h_ref[...] + _dot(merged, wo_v[...])

    return _call(
        body, (yc, ysb, pcg, pcg, b_gate, h, w_mix), grid=(t // TM,), name=name, comm=comm,
        in_specs=[_row_spec(TM, D), _row_spec(TM, D)] + _gate_specs()
                 + [_const_spec((1, 2 * D)), _row_spec(TM, D), _ANY],
        out_specs=[_row_spec(TM, D)] * 4,
        out_shape=[jax.ShapeDtypeStruct((t, D), F32), jax.ShapeDtypeStruct((t, D), F32),
                   jax.ShapeDtypeStruct((t, D), BF16), jax.ShapeDtypeStruct((t, D), F32)],
        scratch_shapes=[pltpu.VMEM((D, D), BF16)] * 3 + [pltpu.SemaphoreType.DMA((3 * N_DEV,))])


def _mix_out_bwd(dh2, a, b, pcg, b_gate, conv_w, w_mix, name, comm=None):
    t = dh2.shape[0]
    n_tile = t // TM
    per8 = TM // 8

    def rows(n):
        return pl.BlockSpec((TM, n), lambda i: (n_tile - 1 - i, 0))

    def cols(block):
        return pl.BlockSpec((TM, D), lambda i: (n_tile - 1 - i, block))

    def before(block):
        return pl.BlockSpec((8, D), lambda i: (jnp.maximum((n_tile - 1 - i) * per8 - 1, 0), block))

    def body(dh_ref, a_ref, b_ref, gc_ref, gs_ref, cb_ref, cc_ref, cx_ref, ccp_ref, cxp_ref, bias_ref, cw_ref, mix_hbm,
             dhb_ref, da_ref, db_ref, dgp_ref, dc_ref, dysb_ref, dbias_ref, dcw_ref, wc_v, wa_v, wo_v, head_v, sems):
        step = pl.program_id(0)
        _load_resident(step, _mix_pairs(mix_hbm, (wc_v, wa_v, wo_v)), sems)

        @pl.when(step == 0)
        def _():
            head_v[...] = jnp.zeros_like(head_v)
            dcw_ref[...] = jnp.zeros_like(dcw_ref)

        dhb = dh_ref[...].astype(BF16)
        dhb_ref[...] = dhb
        dm = _dot_nt(dhb, wo_v[...])
        gc = _sigmoid(gc_ref[...] + bias_ref[:, :D])
        gs = _sigmoid(gs_ref[...] + bias_ref[:, D:])
        da = (dm * gc).astype(BF16)
        db = (dm * gs).astype(BF16)
        da_ref[...] = da
        db_ref[...] = db
        dgc = dm * a_ref[...] * (gc * (1.0 - gc))
        dgs = dm * b_ref[...] * (gs * (1.0 - gs))
        dgp_ref[0] = dgc.astype(BF16)
        dgp_ref[1] = dgs.astype(BF16)
        _accumulate(dbias_ref.at[:, :D], step, jnp.sum(dgc, axis=0, keepdims=True))
        _accumulate(dbias_ref.at[:, D:], step, jnp.sum(dgs, axis=0, keepdims=True))
        dysb_ref[...] = _dot_nt(db, wa_v[...]).astype(BF16)
        dyc = _dot_nt(da, wc_v[...])
        cc, cx = cc_ref[...], cx_ref[...]
        xc = cc * cx
        xc_before = jnp.where(step == n_tile - 1, 0.0, ccp_ref[...] * cxp_ref[...])
        ext = jnp.concatenate([xc_before, xc], axis=0)
        x1 = pltpu.roll(ext, 1, 0)[8:]
        x2 = pltpu.roll(ext, 2, 0)[8:]
        w0, w1, w2 = cw_ref[0:1, :], cw_ref[1:2, :], cw_ref[2:3, :]
        dc_ref[0] = (dyc * (w0 * x2 + w1 * x1 + w2 * xc)).astype(BF16)
        dconv = dyc * cb_ref[...]
        dcw_ref[0:1, :] += jnp.sum(dconv * x2, axis=0, keepdims=True)
        dcw_ref[1:2, :] += jnp.sum(dconv * x1, axis=0, keepdims=True)
        dcw_ref[2:3, :] += jnp.sum(dconv * xc, axis=0, keepdims=True)
        after = jnp.concatenate([dconv, head_v[...]], axis=0)
        dxc = w2 * dconv + w1 * pltpu.roll(after, TM + 7, 0)[:TM] + w0 * pltpu.roll(after, TM + 6, 0)[:TM]
        dc_ref[1] = (dxc * cx).astype(BF16)
        dc_ref[2] = (dxc * cc).astype(BF16)
        head_v[...] = dconv[:8]

    return _call(
        body, (dh2, a, b, pcg, pcg, pcg, pcg, pcg, pcg, pcg, b_gate, conv_w, w_mix), grid=(n_tile,), name=name,
        comm=comm,
        in_specs=[rows(D)] * 3 + [cols(3), cols(4), cols(0), cols(1), cols(2), before(1), before(2),
                                  _const_spec((1, 2 * D)), _const_spec((CONV_K, D)), _ANY],
        out_specs=[rows(D)] * 3 + [pl.BlockSpec((2, TM, D), lambda i: (0, n_tile - 1 - i, 0)),
                                   pl.BlockSpec((3, TM, D), lambda i: (0, n_tile - 1 - i, 0)), rows(D),
                                   _const_spec((1, 2 * D)), _const_spec((8, D))],
        out_shape=[jax.ShapeDtypeStruct((t, D), BF16)] * 3
                  + [jax.ShapeDtypeStruct((2, t, D), BF16), jax.ShapeDtypeStruct((3, t, D), BF16),
                     jax.ShapeDtypeStruct((t, D), BF16), jax.ShapeDtypeStruct((1, 2 * D), F32),
                     jax.ShapeDtypeStruct((8, D), F32)],
        scratch_shapes=[pltpu.VMEM((D, D), BF16)] * 3 + [pltpu.VMEM((8, D), F32),
                                                         pltpu.SemaphoreType.DMA((3 * N_DEV,))])


def _inproj_bwd(dconv, dq, dkv, dgp, w_in, h, g, dh_res, name, comm=None):
    t = h.shape[0]

    def body(dc_ref, dq_ref, dkv_ref, dgp_ref, w_hbm, h_ref, g_ref, dres_ref, dh_ref, dg_ref, w_v, sems):
        step = pl.program_id(0)
        _load_resident(step, [(w_hbm, w_v)], sems)
        du = _dot_nt(dq_ref[...], w_v[3])
        for k in range(3):
            du = du + _dot_nt(dc_ref[k], w_v[k])
        for k in range(2):
            du = du + _dot_nt(dkv_ref[k].astype(BF16), w_v[4 + k]) + _dot_nt(dgp_ref[k], w_v[6 + k])
        dx, dg = _rms_bwd_tile(h_ref[...], g_ref[...], du)
        dh_ref[...] = dres_ref[...] + dx
        _accumulate(dg_ref, step, dg)

    return _call(
        body, (dconv, dq, dkv, dgp, w_in, h, g, dh_res), grid=(t // TM,), name=name, comm=comm,
        in_specs=[_blk_row_spec(3, TM, D), _row_spec(TM, D), _blk_row_spec(2, TM, D), _blk_row_spec(2, TM, D), _ANY,
                  _row_spec(TM, D), _const_spec((1, D)), _row_spec(TM, D)],
        out_specs=[_row_spec(TM, D), _const_spec((1, D))],
        out_shape=[jax.ShapeDtypeStruct((t, D), F32), jax.ShapeDtypeStruct((1, D), F32)],
        scratch_shapes=[pltpu.VMEM((N_DEV, D, D), BF16), pltpu.SemaphoreType.DMA((1,))])


def _softmax_rows(s):
    e = jnp.exp(s - jnp.max(s, axis=-1, keepdims=True))
    return e / jnp.sum(e, axis=-1, keepdims=True)


def _cross_pairs(cross_hbm, wq_v, wo_v):
    return _square_pairs(cross_hbm, 0, wq_v) + _square_pairs(cross_hbm, 1, wo_v)


def _cross_fwd(h, g, mem, g_mem, w_ckv, w_cross, name):
    t = h.shape[0]
    m = mem.shape[0]
    scale = X_DH ** -0.5

    def body(h_ref, g_ref, mem_ref, gm_ref, wkv_ref, cross_hbm, hn_ref, qx_ref, o_ref, h3_ref, mn_ref, kv_ref,
             wq_v, wo_v, sems):
        _load_resident(pl.program_id(0), _cross_pairs(cross_hbm, wq_v, wo_v), sems)

        @pl.when(pl.program_id(0) == 0)
        def _():
            mn = _rms_fwd_tile(mem_ref[...], gm_ref[...]).astype(BF16)
            mn_ref[...] = mn
            for j in range(N_DEV):
                kv_ref[j] = _dot(mn, wkv_ref[j]).astype(BF16)

        ht = h_ref[...]
        hn = _rms_fwd_tile(ht, g_ref[...]).astype(BF16)
        hn_ref[...] = hn
        qx = _dot(hn, wq_v[...]).astype(BF16)
        qx_ref[...] = qx
        for hd in range(X_H):
            lo, hi = hd * X_DH, (hd + 1) * X_DH
            p = _softmax_rows(_dot_nt(qx[:, lo:hi], kv_ref[hd]) * scale)
            o_ref[:, lo:hi] = _dot(p.astype(BF16), kv_ref[X_H + hd]).astype(BF16)
        h3_ref[...] = ht + _dot(o_ref[...], wo_v[...])

    return pl.pallas_call(
        body, grid=(t // TM,), name=name,
        in_specs=[_row_spec(TM, D), _const_spec((1, D)), _const_spec((m, D)), _const_spec((1, D)),
                  _const_spec((N_DEV, D, X_DH)), _ANY],
        out_specs=[_row_spec(TM, D)] * 4 + [_const_spec((m, D)), _const_spec((N_DEV, m, X_DH))],
        out_shape=[jax.ShapeDtypeStruct((t, D), BF16)] * 3 + [jax.ShapeDtypeStruct((t, D), F32),
                                                              jax.ShapeDtypeStruct((m, D), BF16),
                                                              jax.ShapeDtypeStruct((N_DEV, m, X_DH), BF16)],
        scratch_shapes=[pltpu.VMEM((D, D), BF16)] * 2 + [pltpu.SemaphoreType.DMA((2 * N_DEV,))],
        compiler_params=_cparams(),
    )(h, g, mem, g_mem, w_ckv, w_cross)


def _cross_bwd(dh3, h, g, qx, kv, mem, g_mem, w_ckv, w_cross, name, comm=None):
    t = h.shape[0]
    m = kv.shape[1]
    scale = X_DH ** -0.5

    def body(dh_ref, h_ref, g_ref, qx_ref, kv_ref, mem_ref, gm_ref, wkv_ref, cross_hbm,
             dhb_ref, dqx_ref, dkv_ref, dh2_ref, dg_ref, dgm_ref, wq_v, wo_v, sems):
        step = pl.program_id(0)
        _load_resident(step, _cross_pairs(cross_hbm, wq_v, wo_v), sems)

        @pl.when(step == 0)
        def _():
            dkv_ref[...] = jnp.zeros_like(dkv_ref)

        dht = dh_ref[...]
        dhb = dht.astype(BF16)
        dhb_ref[...] = dhb
        do = _dot_nt(dhb, wo_v[...]).astype(BF16)
        for hd in range(X_H):
            lo, hi = hd * X_DH, (hd + 1) * X_DH
            qh = qx_ref[:, lo:hi]
            kh = kv_ref[hd]
            p = _softmax_rows(_dot_nt(qh, kh) * scale)
            doh = do[:, lo:hi]
            dp = _dot_nt(doh, kv_ref[X_H + hd])
            ds = (p * (dp - jnp.sum(dp * p, axis=-1, keepdims=True)) * scale).astype(BF16)
            dqx_ref[:, lo:hi] = _dot(ds, kh).astype(BF16)
            dkv_ref[hd] += _dot_tn(ds, qh)
            dkv_ref[X_H + hd] += _dot_tn(p.astype(BF16), doh)
        dhn = _dot_nt(dqx_ref[...], wq_v[...])
        dx, dg = _rms_bwd_tile(h_ref[...], g_ref[...], dhn)
        dh2_ref[...] = dht + dx
        _accumulate(dg_ref, step, dg)

        @pl.when(step == t // TM - 1)
        def _():
            dmn = jnp.zeros((m, D), F32)
            for j in range(N_DEV):
                dmn = dmn + _dot_nt(dkv_ref[j].astype(BF16), wkv_ref[j])
            dgm_ref[...] = _rms_bwd_tile(mem_ref[...], gm_ref[...], dmn)[1]

    return _call(
        body, (dh3, h, g, qx, kv, mem, g_mem, w_ckv, w_cross), grid=(t // TM,), name=name, comm=comm,
        in_specs=[_row_spec(TM, D), _row_spec(TM, D), _const_spec((1, D)), _row_spec(TM, D),
                  _const_spec((N_DEV, m, X_DH)), _const_spec((m, D)), _const_spec((1, D)),
                  _const_spec((N_DEV, D, X_DH)), _ANY],
        out_specs=[_row_spec(TM, D), _row_spec(TM, D), _const_spec((N_DEV, m, X_DH)), _row_spec(TM, D),
                   _const_spec((1, D)), _const_spec((1, D))],
        out_shape=[jax.ShapeDtypeStruct((t, D), BF16), jax.ShapeDtypeStruct((t, D), BF16),
                   jax.ShapeDtypeStruct((N_DEV, m, X_DH), F32), jax.ShapeDtypeStruct((t, D), F32),
                   jax.ShapeDtypeStruct((1, D), F32), jax.ShapeDtypeStruct((1, D), F32)],
        scratch_shapes=[pltpu.VMEM((D, D), BF16)] * 2 + [pltpu.SemaphoreType.DMA((2 * N_DEV,))])


def _adamw_small(small_all, dev, wts, mom1, mom2, name):
    n_slots = small_all.shape[0]
    conv_rows, conv_cols = wts["conv_w"].shape
    shapes = {n: (1, wts[n].size) for n in GAINS + ("b_gate",)}
    shapes["conv_w"] = (conv_rows, conv_cols)
    n_par = len(SMALL)
    gate_row, conv_row = len(GAINS), len(GAINS) + 2

    def body(dev_ref, all_ref, cols_ref, *refs):
        params, loss_ref, outs, acc = refs[:3 * n_par], refs[3 * n_par], refs[3 * n_par + 1:-1], refs[-1]
        total = all_ref[0]
        for k in range(1, n_slots):
            total = total + all_ref[k]
        acc[...] = total
        loss_ref[...] = acc[LOSS_ROW:LOSS_ROW + 1, 0:1]
        for i, n in enumerate(SMALL):
            w_ref, m_ref, v_ref = params[3 * i:3 * i + 3]
            res = outs[4 * i:4 * i + 4]
            if n == "conv_w":
                gt = cols_ref[0, conv_row:conv_row + conv_rows, :]
                for k in range(1, n_slots):
                    gt = gt + cols_ref[k, conv_row:conv_row + conv_rows, :]
                _adamw_apply(gt, w_ref, m_ref, v_ref, *res)
            elif n == "b_gate":
                for half in range(2):
                    cols = slice(half * D, (half + 1) * D)
                    _adamw_apply(acc[gate_row + half:gate_row + half + 1, :], w_ref.at[:, cols], m_ref.at[:, cols],
                                 v_ref.at[:, cols], *[r.at[:, cols] for r in res])
            else:
                _adamw_apply(acc[i:i + 1, :], w_ref, m_ref, v_ref, *res)

    def whole(shape):
        return pl.BlockSpec(shape, lambda i, dev_ref: (0,) * len(shape))

    operands = [a.reshape(shapes[n]) for n in SMALL for a in (wts[n], mom1[n], mom2[n])]
    out_shapes = [shapes[n] for n in SMALL for _ in range(4)]
    outs = pl.pallas_call(
        body, name=name,
        grid_spec=pltpu.PrefetchScalarGridSpec(
            num_scalar_prefetch=1, grid=(1,),
            in_specs=[whole(small_all.shape),
                      pl.BlockSpec((n_slots, SMALL_R, conv_cols), lambda i, dev_ref: (0, 0, dev_ref[0]))]
            + [whole(a.shape) for a in operands],
            out_specs=[whole((1, 1))] + [whole(s) for s in out_shapes],
            scratch_shapes=[pltpu.VMEM((SMALL_R, D), F32)]),
        out_shape=[jax.ShapeDtypeStruct((1, 1), F32)] + [jax.ShapeDtypeStruct(s, F32) for s in out_shapes],
        compiler_params=_cparams(),
    )(dev, small_all, small_all, *operands)
    small = {n: tuple(o.reshape(wts[n].shape) for o in outs[1 + 4 * i:5 + 4 * i]) for i, n in enumerate(SMALL)}
    return outs[0].reshape(()), small


def _adamw_own(w, land, own, chip, m, v, name, row_block=0, token=None):
    r, c = w.shape
    tr = _pick_tile(r, (256, 352, 128))
    off = row_block * (r // tr)

    def body(chip_ref, w_ref, land_ref, own_ref, m_ref, v_ref, *rest):
        mine = own_ref[0].astype(F32)
        gt = jnp.where(chip_ref[0] == 0, mine, land_ref[0].astype(F32))
        for k in range(1, N_CHIP):
            gt = gt + jnp.where(chip_ref[0] == k, mine, land_ref[k].astype(F32))
        if token is None:
            _adamw_apply(gt, w_ref, m_ref, v_ref, *rest)
        else:
            before, *outs, later = rest
            _adamw_apply(gt + before[0:1, 0:1], w_ref, m_ref, v_ref, *outs)
            later[...] = before[...]

    spec = pl.BlockSpec((tr, c), lambda i, chip_ref: (i, 0))
    in_specs = [spec, pl.BlockSpec((N_CHIP, tr, c), lambda i, chip_ref: (0, i + off, 0)),
                pl.BlockSpec((1, tr, c), lambda i, chip_ref: (chip_ref[0], i + off, 0)), spec, spec]
    operands = (chip, w, land, own, m, v)
    out_specs, out_shape = [spec] * 4, [jax.ShapeDtypeStruct((r, c), F32)] * 4
    if token is not None:
        token_spec = pl.BlockSpec(token.shape, lambda i, chip_ref: (0, 0))
        in_specs.append(token_spec)
        operands += (token,)
        out_specs, out_shape = out_specs + [token_spec], out_shape + [jax.ShapeDtypeStruct(token.shape, token.dtype)]
    return pl.pallas_call(
        body, name=name,
        grid_spec=pltpu.PrefetchScalarGridSpec(
            num_scalar_prefetch=1, grid=(r // tr,), in_specs=in_specs, out_specs=out_specs),
        out_shape=out_shape,
        compiler_params=_cparams(),
    )(*operands)


def _adamw_apply(gt, w_ref, m_ref, v_ref, g_ref, d_ref, nm_ref, nv_ref):
    g_ref[...] = gt
    nm = ADAM_B1 * m_ref[...] + (1.0 - ADAM_B1) * gt
    nv = ADAM_B2 * v_ref[...] + (1.0 - ADAM_B2) * jnp.square(gt)
    m_hat = nm / (1.0 - ADAM_B1 ** ADAM_STEP)
    v_hat = nv / (1.0 - ADAM_B2 ** ADAM_STEP)
    d_ref[...] = -ADAM_LR * (m_hat / (jnp.sqrt(v_hat) + ADAM_EPS) + ADAM_WD * w_ref[...])
    nm_ref[...] = nm
    nv_ref[...] = nv


def _mesh_pos():
    return lax.axis_index("x"), lax.axis_index("y"), lax.axis_index("c")


def _both(first, second):
    n_in, n_out, n_sem = len(first.inputs), len(first.out_shapes), len(first.sem_shapes)

    def run(round_name):
        def both(in_refs, out_refs, sems):
            getattr(first, round_name)(in_refs[:n_in], out_refs[:n_out], sems[:n_sem])
            getattr(second, round_name)(in_refs[n_in:], out_refs[n_out:], sems[n_sem:])
        return both

    return types.SimpleNamespace(
        inputs=first.inputs + second.inputs, out_shapes=first.out_shapes + second.out_shapes,
        sem_shapes=first.sem_shapes + second.sem_shapes, start=run("start"), middle=run("middle"),
        finish=run("finish"))


def _no_round(in_refs, out_refs, sems):
    pass


def _after(token):
    return types.SimpleNamespace(inputs=[token], out_shapes=[], sem_shapes=[], start=_no_round, middle=_no_round,
                                 finish=_no_round)


def _run_exchange(comm, name):
    c_in, c_out = len(comm.inputs), len(comm.out_shapes)

    def body(*refs):
        cins, couts, sems = refs[:c_in], refs[c_in:c_in + c_out], refs[c_in + c_out:]
        comm.start(cins, couts, sems)
        comm.middle(cins, couts, sems)
        comm.finish(cins, couts, sems)

    return list(pl.pallas_call(
        body, name=name, out_shape=list(comm.out_shapes),
        in_specs=[_ANY] * c_in, out_specs=[_ANY] * c_out, scratch_shapes=list(comm.sem_shapes),
    )(*comm.inputs))


def _gather_exchange(shards):
    n_arr = len(shards)

    def plan(x_refs, out_refs, sems):
        send_sems, recv_sems, local_sems = sems[:3]
        stage = sems[3:]
        x, y, c = _mesh_pos()
        me, sibling = (x, y, c), (x, y, 1 - c)
        xn, yn, diag = (1 - x, y), (x, 1 - y), (1 - x, 1 - y)

        def slot(a, px, py, pc, half=None):
            ref = out_refs[a].at[4 * px + 2 * py + pc]
            if half is None:
                return ref
            rows = shards[a].shape[0] // 2
            return ref.at[half * rows:(half + 1) * rows]

        def copy(a, k, block, to, half=None, src=None):
            dst = slot(a, *block, half)
            return pltpu.make_async_remote_copy(
                src_ref=dst if src is None else src, dst_ref=dst,
                send_sem=send_sems.at[a, k], recv_sem=recv_sems.at[a, k],
                device_id=to, device_id_type=pl.DeviceIdType.MESH)

        return types.SimpleNamespace(
            me=me, sibling=sibling, xn=xn, yn=yn, diag=diag, c=c, copy=copy,
            mine_in=[pltpu.make_async_copy(x_refs[a], stage[a], local_sems.at[a, 0]) for a in range(n_arr)],
            mine_out=[pltpu.make_async_copy(stage[a], slot(a, *me), local_sems.at[a, 1]) for a in range(n_arr)],
            first=[cp for a in range(n_arr) for cp in (
                copy(a, 0, me, sibling, src=x_refs[a]), copy(a, 1, me, (*xn, c), src=x_refs[a]),
                copy(a, 2, me, (*yn, c), src=x_refs[a]))],
            second=lambda a: (copy(a, 3, (*xn, c), (*yn, c), half=0), copy(a, 5, (*xn, c), sibling),
                              copy(a, 4, (*yn, c), (*xn, c), half=1), copy(a, 6, (*yn, c), sibling)),
            third=lambda a: (copy(a, 7, (*diag, c), sibling, half=0), copy(a, 8, (*diag, c), sibling, half=1)))

    def start(x_refs, out_refs, sems):
        p = plan(x_refs, out_refs, sems)
        for cp in p.first + p.mine_in:
            cp.start()
        for cp_in, cp_out in zip(p.mine_in, p.mine_out):
            cp_in.wait()
            cp_out.start()

    def middle(x_refs, out_refs, sems):
        p = plan(x_refs, out_refs, sems)
        for a in range(n_arr):
            to_yn, x_to_sib, to_xn, y_to_sib = p.second(a)
            p.copy(a, 1, (*p.xn, p.c), p.me).wait_recv()
            to_yn.start()
            x_to_sib.start()
            p.copy(a, 2, (*p.yn, p.c), p.me).wait_recv()
            to_xn.start()
            y_to_sib.start()

    def finish(x_refs, out_refs, sems):
        p = plan(x_refs, out_refs, sems)
        for a in range(n_arr):
            half0_to_sib, half1_to_sib = p.third(a)
            p.copy(a, 3, (*p.diag, p.c), p.me, half=0).wait_recv()
            half0_to_sib.start()
            p.copy(a, 4, (*p.diag, p.c), p.me, half=1).wait_recv()
            half1_to_sib.start()
        other = 1 - p.c
        for a in range(n_arr):
            p.copy(a, 0, p.sibling, p.me).wait_recv()
            p.copy(a, 5, (*p.xn, other), p.me).wait_recv()
            p.copy(a, 6, (*p.yn, other), p.me).wait_recv()
            p.copy(a, 7, (*p.diag, other), p.me, half=0).wait_recv()
            p.copy(a, 8, (*p.diag, other), p.me, half=1).wait_recv()
        for cp in p.first:
            cp.wait_send()
        for a in range(n_arr):
            for cp in p.second(a) + p.third(a):
                cp.wait_send()
        for cp in p.mine_out:
            cp.wait()

    return types.SimpleNamespace(
        inputs=list(shards), start=start, middle=middle, finish=finish,
        out_shapes=[jax.ShapeDtypeStruct((N_DEV,) + s.shape, s.dtype) for s in shards],
        sem_shapes=[pltpu.SemaphoreType.DMA((n_arr, 9)), pltpu.SemaphoreType.DMA((n_arr, 9)),
                    pltpu.SemaphoreType.DMA((n_arr, 2))] + [pltpu.VMEM(s.shape, s.dtype) for s in shards])


def _pair_exchange(grads):
    n_arr = len(grads)

    def plan(g_refs, land_refs, sems):
        send_sems, recv_sems = sems
        x, y, c = _mesh_pos()
        return [pltpu.make_async_remote_copy(
            src_ref=g_refs[a].at[2 * k + 1 - c], dst_ref=land_refs[a].at[k],
            send_sem=send_sems.at[a, k], recv_sem=recv_sems.at[a, k],
            device_id=(x, y, 1 - c), device_id_type=pl.DeviceIdType.MESH)
            for a in range(n_arr) for k in range(N_CHIP)]

    def start(g_refs, land_refs, sems):
        for cp in plan(g_refs, land_refs, sems):
            cp.start()

    def finish(g_refs, land_refs, sems):
        for cp in plan(g_refs, land_refs, sems):
            cp.wait()

    return types.SimpleNamespace(
        inputs=list(grads), start=start, middle=_no_round, finish=finish,
        out_shapes=[jax.ShapeDtypeStruct((N_CHIP,) + g.shape[1:], g.dtype) for g in grads],
        sem_shapes=[pltpu.SemaphoreType.DMA((n_arr, N_CHIP)), pltpu.SemaphoreType.DMA((n_arr, N_CHIP))])


def _direct_gather(shard):
    def plan(x_ref, out_ref, sems):
        send_sems, recv_sems, local_sem = sems
        x, y, c = _mesh_pos()
        mine = out_ref.at[4 * x + 2 * y + c]
        local = pltpu.make_async_copy(x_ref, mine, local_sem.at[0])
        remote = []
        for k in range(N_DEV - 1):
            peer = tuple(1 - pos if (k + 1) >> bit & 1 else pos for pos, bit in ((x, 2), (y, 1), (c, 0)))
            remote.append(pltpu.make_async_remote_copy(
                src_ref=x_ref, dst_ref=mine, send_sem=send_sems.at[k], recv_sem=recv_sems.at[k],
                device_id=peer, device_id_type=pl.DeviceIdType.MESH))
        return local, remote

    def start(x_refs, out_refs, sems):
        local, remote = plan(x_refs[0], out_refs[0], sems)
        for cp in [local] + remote:
            cp.start()

    def finish(x_refs, out_refs, sems):
        local, remote = plan(x_refs[0], out_refs[0], sems)
        for cp in remote:
            cp.wait_recv()
        for cp in remote:
            cp.wait_send()
        local.wait()

    return types.SimpleNamespace(
        inputs=[shard], start=start, middle=_no_round, finish=finish,
        out_shapes=[jax.ShapeDtypeStruct((N_DEV,) + shard.shape, shard.dtype)],
        sem_shapes=[pltpu.SemaphoreType.DMA((N_DEV - 1,)), pltpu.SemaphoreType.DMA((N_DEV - 1,)),
                    pltpu.SemaphoreType.DMA((1,))])


def _chip_exchange(parts):
    n_arr = len(parts)

    def plan(p_refs, land_refs, sems):
        send_sems, recv_sems, local_sems = sems
        x, y, c = _mesh_pos()
        my_chip = 2 * x + y
        chips = [(1 - x, y), (x, 1 - y), (1 - x, 1 - y)]
        local = [pltpu.make_async_copy(p_refs[a].at[my_chip], land_refs[a].at[my_chip], local_sems.at[a])
                 for a in range(n_arr)]

        def copy(a, k, src_slot, dst_slot, px, py):
            return pltpu.make_async_remote_copy(
                src_ref=p_refs[a].at[src_slot], dst_ref=land_refs[a].at[dst_slot],
                send_sem=send_sems.at[a, k], recv_sem=recv_sems.at[a, k],
                device_id=(px, py, c), device_id_type=pl.DeviceIdType.MESH)

        sends = [copy(a, k, 2 * px + py, my_chip, px, py) for a in range(n_arr) for k, (px, py) in enumerate(chips)]
        arrivals = [copy(a, k, my_chip, 2 * px + py, px, py) for a in range(n_arr)
                    for k, (px, py) in enumerate(chips)]
        return local, sends, arrivals

    def start(p_refs, land_refs, sems):
        local, sends, _ = plan(p_refs, land_refs, sems)
        for cp in local + sends:
            cp.start()

    def finish(p_refs, land_refs, sems):
        local, sends, arrivals = plan(p_refs, land_refs, sems)
        for cp in arrivals:
            cp.wait_recv()
        for cp in sends:
            cp.wait_send()
        for cp in local:
            cp.wait()

    return types.SimpleNamespace(
        inputs=list(parts), start=start, middle=_no_round, finish=finish,
        out_shapes=[jax.ShapeDtypeStruct(p.shape, p.dtype) for p in parts],
        sem_shapes=[pltpu.SemaphoreType.DMA((n_arr, 3)), pltpu.SemaphoreType.DMA((n_arr, 3)),
                    pltpu.SemaphoreType.DMA((n_arr,))])


_HBM = pl.BlockSpec(memory_space=pltpu.HBM)
_SEM = pl.BlockSpec(memory_space=pltpu.SEMAPHORE)
_DATAFLOW = pltpu.SideEffectType.DATAFLOW_SIDE_EFFECTING


def _chip_copies(p_refs, land_refs, send_sems, recv_sems):
    x, y, c = _mesh_pos()
    my_chip = 2 * x + y
    chips = [(1 - x, y), (x, 1 - y), (1 - x, 1 - y)]
    return [pltpu.make_async_remote_copy(
        src_ref=p_refs[a].at[2 * px + py], dst_ref=land_refs[a].at[my_chip],
        send_sem=send_sems[3 * a + k], recv_sem=recv_sems[3 * a + k],
        device_id=(px, py, c), device_id_type=pl.DeviceIdType.MESH)
        for a in range(len(p_refs)) for k, (px, py) in enumerate(chips)]


def _chip_exchange_begin(parts, name):
    n_arr = len(parts)
    n_buf, n_copy = 2 * n_arr, 3 * n_arr
    lands = [lax.empty(p.shape, p.dtype) for p in parts]

    def body(*refs):
        p_refs, land_refs = refs[:n_arr], refs[n_arr:n_buf]
        send_sems, recv_sems, token = refs[n_buf:n_buf + n_copy], refs[n_buf + n_copy:n_buf + 2 * n_copy], refs[-1]
        for cp in _chip_copies(p_refs, land_refs, send_sems, recv_sems):
            cp.start()
        token[...] = jnp.zeros_like(token)

    bufs = list(parts) + list(lands)
    outs = pl.pallas_call(
        body, name=name,
        out_shape=(*[pltpu.SemaphoreType.DMA(())] * (2 * n_copy), *[pltpu.HBM(b.shape, b.dtype) for b in bufs],
                   jax.ShapeDtypeStruct((8, 128), F32)),
        in_specs=[_HBM] * n_buf,
        out_specs=(*[_SEM] * (2 * n_copy), *[_HBM] * n_buf, pl.BlockSpec(memory_space=pltpu.VMEM)),
        input_output_aliases={i: 2 * n_copy + i for i in range(n_buf)},
        compiler_params=pltpu.CompilerParams(has_side_effects=_DATAFLOW),
    )(*[pltpu.with_memory_space_constraint(b, pltpu.HBM) for b in bufs])
    sems = list(outs[:2 * n_copy])
    thru = list(outs[2 * n_copy:2 * n_copy + n_buf])
    return types.SimpleNamespace(send_sems=sems[:n_copy], recv_sems=sems[n_copy:], parts=thru[:n_arr],
                                 lands=thru[n_arr:], token=outs[-1])


def _chip_exchange_end(flight, after, name):
    send_sems, recv_sems, parts, lands = flight.send_sems, flight.recv_sems, flight.parts, flight.lands
    n_arr = len(parts)
    n_buf, n_copy = 2 * n_arr, 3 * n_arr

    def body(*refs):
        p_refs, land_refs = refs[:n_arr], refs[n_arr:n_buf]
        sems = refs[n_buf:n_buf + 2 * n_copy]
        for cp in _chip_copies(p_refs, land_refs, sems[:n_copy], sems[n_copy:]):
            cp.wait_send()
            cp.wait_recv()

    bufs = list(parts) + list(lands)
    outs = pl.pallas_call(
        body, name=name, out_shape=tuple(pltpu.HBM(b.shape, b.dtype) for b in bufs),
        in_specs=[_HBM] * n_buf + [_SEM] * (2 * n_copy) + [_ANY], out_specs=tuple([_HBM] * n_buf),
        input_output_aliases={i: i for i in range(n_buf)},
        compiler_params=pltpu.CompilerParams(has_side_effects=_DATAFLOW),
    )(*bufs, *send_sems, *recv_sems, after)
    return list(outs[:n_arr]), list(outs[n_arr:])


def _row_tile(r, cap=640):
    best = None
    for cand in range(16, min(r, cap) + 1, 16):
        if r % cand == 0:
            best = cand
    return best if best is not None else r


def _pair_sum(gs, landeds, core, name):
    tiles = [_row_tile(g.shape[1]) for g in gs]
    counts = [g.shape[1] // tr for g, tr in zip(gs, tiles)]
    n_arr = len(gs)

    def body(core_ref, *refs):
        for a in range(n_arr):
            mine, theirs, out = refs[2 * a], refs[2 * a + 1], refs[2 * n_arr + a]
            out[0] = (mine[0].astype(F32) + theirs[0].astype(F32)).astype(out.dtype)

    in_specs, out_specs, operands = [], [], []
    for g, landed, tr, count in zip(gs, landeds, tiles, counts):
        c_dim = g.shape[2]
        last = count - 1
        in_specs += [pl.BlockSpec((1, tr, c_dim),
                                  lambda k, i, core_ref, last=last: (2 * k + core_ref[0], jnp.minimum(i, last), 0)),
                     pl.BlockSpec((1, tr, c_dim), lambda k, i, core_ref, last=last: (k, jnp.minimum(i, last), 0))]
        out_specs.append(pl.BlockSpec((1, tr, c_dim), lambda k, i, core_ref, last=last: (k, jnp.minimum(i, last), 0)))
        operands += [g, landed]
    return list(pl.pallas_call(
        body, name=name,
        grid_spec=pltpu.PrefetchScalarGridSpec(
            num_scalar_prefetch=1, grid=(N_CHIP, max(counts)), in_specs=in_specs, out_specs=out_specs),
        out_shape=[jax.ShapeDtypeStruct((N_CHIP,) + g.shape[1:], g.dtype) for g in gs],
        compiler_params=_cparams(2),
    )(core, *operands))


GAINS = ("g_ffn1", "g_mix", "g_cross", "g_mem", "g_ffn2", "g_final")
SMALL = GAINS + ("b_gate", "conv_w")
SMALL_R = 16
LOSS_ROW = 11
WEIGHT_ORDER = ("g_ffn1", "w_ffn1_gu", "w_ffn1_down", "g_mix", "w_in", "b_gate", "conv_w", "w_conv_out",
                "w_attn_out", "w_o", "g_cross", "g_mem", "w_cq", "w_ckv", "w_co", "g_ffn2", "w_ffn2_gu",
                "w_ffn2_down", "g_final")
GU_NAMES = ("w_ffn1_gu", "w_ffn2_gu")


def _pack_small(vals, conv_rows):
    rows = [vals[n].reshape(1, D) for n in GAINS] + [vals["b_gate"].reshape(2, D), conv_rows.reshape(CONV_K, D)]
    used = len(GAINS) + 2 + CONV_K
    return jnp.concatenate(rows + [jnp.zeros((SMALL_R - used, D), F32)], axis=0)


def _exchange_shards(wts):
    out = {n: jnp.pad(wts[n].T.astype(BF16), ((0, FF_PAD - FF_BLK), (0, 0))) for n in GU_NAMES}
    for n in ("w_ckv", "w_in", "w_ffn1_down", "w_ffn2_down"):
        out[n] = wts[n].astype(BF16)
    out["mix"] = jnp.concatenate([wts[n].astype(BF16) for n in MIX_MATS], axis=0)
    out["cross"] = jnp.concatenate([wts[n].astype(BF16) for n in CROSS_MATS], axis=0)
    return out


def _reduce_group(grads, landed, core, names):
    return _pair_sum(grads, landed, core, "grads_pair_sum_" + "_".join(names))


def _step(x, mem, target, sh, conv_pad, gains, b_gate, core):
    wg1, wd1, conv_all = _run_exchange(_gather_exchange([sh["w_ffn1_gu"], sh["w_ffn1_down"], conv_pad]), "gather_ffn1")
    conv_w = conv_all[:, :CONV_K, :].transpose(1, 0, 2).reshape(CONV_K, D)
    (n1, gate1, up1, act1, h1), (w_in,) = _ffn_fwd(
        x, gains["g_ffn1"], wg1, wd1, "ffn1_fwd", comm=_gather_exchange([sh["w_in"]]))
    (u, pcg, qkv, yc), (w_mix, wd2) = _inproj_fwd(h1, gains["g_mix"], w_in, conv_w, "inproj_fwd",
                                                  comm=_gather_exchange([sh["mix"], sh["w_ffn2_down"]]))
    (ysb, ctot), (w_cross, w_ckv, wg2) = _sb_fwd(
        qkv, "sb_fwd", comm=_gather_exchange([sh["cross"], sh["w_ckv"], sh["w_ffn2_gu"]]))
    (a_mix, b_mix, merged, h2), _ = _mix_out_fwd(yc, ysb, pcg, b_gate, h1, w_mix, "mix_out_fwd")
    hn, qx, o_x, h3, mn, kv = _cross_fwd(h2, gains["g_cross"], mem, gains["g_mem"], w_ckv, w_cross, "cross_fwd")
    (n4, gate2, up2, act2, dh4, loss, dg_final), _ = _ffn_fwd(h3, gains["g_ffn2"], wg2, wd2, "ffn2_fwd",
                                                              head=(gains["g_final"], target))

    gs = {"g_final": dg_final}
    (dgu2, dh4b, dh3, gs["g_ffn2"]), _ = _ffn_bwd(dh4, h3, gains["g_ffn2"], gate2, up2, wg2, wd2, "ffn2_bwd")
    grads_a = [_mm_tn_rows(dgu2, n4, FF_PAD, "dw_ffn2_gu"),
               _mm_tn_rows(act2, dh4b, FF_BLK, "dw_ffn2_down").reshape(N_DEV, DOWN_ROWS, D)]
    names_a = ["w_ffn2_gu", "w_ffn2_down"]
    (dh3b, dqx, dkv, dh2, gs["g_cross"], gs["g_mem"]), _ = _cross_bwd(
        dh3, h2, gains["g_cross"], qx, kv, mem, gains["g_mem"], w_ckv, w_cross, "cross_bwd")
    grads_b = [_mm_tn_cols(mn, dkv, "dw_ckv"), _mm_tn_squares([(hn, dqx), (o_x, dh3b)], "dw_cross")]
    names_b = ["w_ckv", "cross"]
    (dh2b, da_mix, db_mix, dgp, dconv, dysb, gs["b_gate"], gs["conv_w"]), landed_ab = _mix_out_bwd(
        dh2, a_mix, b_mix, pcg, b_gate, conv_w, w_mix, "mix_out_bwd", comm=_pair_exchange(grads_a + grads_b))
    sums_ab = _reduce_group(grads_a + grads_b, landed_ab, core, names_a + names_b)
    grads_c = [_mm_tn_squares([(yc, da_mix), (ysb, db_mix), (merged, dh2b)], "dw_mix")]
    flight_ab = _chip_exchange_begin(sums_ab, "grads_to_chips_early_begin")
    (dq, dkv_sb), _ = _sb_bwd(qkv, dysb, ctot, flight_ab.token, "sb_bwd")
    grads_d = [_mm_tn_cols_many(u, [dconv, dq[None], dkv_sb, dgp], "dw_in")]
    (dh1, gs["g_mix"]), landed_cd = _inproj_bwd(dconv, dq, dkv_sb, dgp, w_in, h1, gains["g_mix"], dh2, "inproj_bwd",
                                                comm=_pair_exchange(grads_c + grads_d))
    sums_cd = _reduce_group(grads_c + grads_d, landed_cd, core, ["mix", "w_in"])
    flight_d = _chip_exchange_begin(sums_cd, "grads_to_chips_w_in_begin")
    (dgu1, dh1b, dx, gs["g_ffn1"]), _ = _ffn_bwd(dh1, x, gains["g_ffn1"], gate1, up1, wg1, wd1, "ffn1_bwd",
                                                 comm=_after(flight_d.token))
    dw_gu1 = _mm_tn_rows(dgu1, n1, FF_PAD, "dw_ffn1_gu")
    small_mine = _pack_small({n: gs[n] for n in GAINS + ("b_gate",)}, gs["conv_w"][:CONV_K])
    small_mine = small_mine.at[LOSS_ROW, 0].set(loss[0, 0])
    dw_down1, (landed_gu1, small_all) = _mm_tn_rows(
        act1, dh1b, FF_BLK, "dw_ffn1_down", comm=_both(_pair_exchange([dw_gu1]), _direct_gather(small_mine)))
    flight_gu1 = _chip_exchange_begin(_reduce_group([dw_gu1], [landed_gu1], core, ["w_ffn1_gu"]),
                                      "grads_to_chips_ffn1_gu_begin")
    grads_down1 = [dw_down1.reshape(N_DEV, DOWN_ROWS, D)]
    landed_down1 = _run_exchange(_both(_pair_exchange(grads_down1), _after(flight_gu1.token)),
                                 "grads_to_sibling_ffn1_down")
    flight_down1 = _chip_exchange_begin(_reduce_group(grads_down1, landed_down1, core, ["w_ffn1_down"]),
                                        "grads_to_chips_ffn1_down_begin")
    flights = [(names_a + names_b, flight_ab), (["mix", "w_in"], flight_d), (["w_ffn1_gu"], flight_gu1),
               (["w_ffn1_down"], flight_down1)]
    return dx, flights, small_all


def kernel(x, mem, g_ffn1, w_ffn1_gu, w_ffn1_down, g_mix, w_in, b_gate, conv_w, w_conv_out, w_attn_out, w_o, g_cross, g_mem, w_cq, w_ckv, w_co, g_ffn2, w_ffn2_gu, w_ffn2_down, g_final, loss_target, m_g_ffn1, m_w_ffn1_gu, m_w_ffn1_down, m_g_mix, m_w_in, m_b_gate, m_conv_w, m_w_conv_out, m_w_attn_out, m_w_o, m_g_cross, m_g_mem, m_w_cq, m_w_ckv, m_w_co, m_g_ffn2, m_w_ffn2_gu, m_w_ffn2_down, m_g_final, v_g_ffn1, v_w_ffn1_gu, v_w_ffn1_down, v_g_mix, v_w_in, v_b_gate, v_conv_w, v_w_conv_out, v_w_attn_out, v_w_o, v_g_cross, v_g_mem, v_w_cq, v_w_ckv, v_w_co, v_g_ffn2, v_w_ffn2_gu, v_w_ffn2_down, v_g_final):
    args = locals()
    wts = {n: args[n] for n in WEIGHT_ORDER}
    mom1 = {n: args["m_" + n] for n in WEIGHT_ORDER}
    mom2 = {n: args["v_" + n] for n in WEIGHT_ORDER}
    cx, cy, cc = _mesh_pos()
    dev = 4 * cx + 2 * cy + cc
    conv_cols = D // N_DEV

    conv_pad = jnp.concatenate([conv_w, jnp.zeros((SMALL_R - CONV_K, conv_cols), F32)], axis=0)
    gains = {n: wts[n].reshape(1, D) for n in GAINS}
    dx, flights, small_all = _step(x[0], mem[0], loss_target[0], _exchange_shards(wts), conv_pad, gains,
                                 b_gate.reshape(1, 2 * D), cc.reshape(1).astype(jnp.int32))

    grads, delta, new_m, new_v = {}, {}, {}, {}

    def operands(n, transposed):
        trio = (wts[n], mom1[n], mom2[n])
        return tuple(a.T for a in trio) if transposed else trio

    def record(n, res, transposed):
        grads[n], delta[n], new_m[n], new_v[n] = [r.T for r in res] if transposed else res

    early = [("w_ffn2_gu", "w_ffn2_gu", 0, True), ("w_ffn2_down", "w_ffn2_down", 0, False),
             ("w_ckv", "w_ckv", 0, False), ("w_in", "w_in", 0, False)]
    early += [(n, "mix", k, False) for k, n in enumerate(MIX_MATS)]
    early += [(n, "cross", k, False) for k, n in enumerate(CROSS_MATS)]
    chip = (2 * cx + cy).reshape(1).astype(jnp.int32)
    (names_early, flight_early), (names_w_in, flight_w_in), *last_flights = flights
    token = last_flights[-1][1].token
    own, land = {}, {}
    for names, flight, tag in ((names_early, flight_early, "early"), (names_w_in, flight_w_in, "w_in")):
        own_parts, landed = _chip_exchange_end(flight, token, "grads_to_chips_%s_end" % tag)
        own.update(zip(names, own_parts))
        land.update(zip(names, landed))
    for n, buf, row_block, transposed in early:
        w, m1, m2 = operands(n, transposed)
        *res, token = _adamw_own(w, land[buf], own[buf], chip, m1, m2, "adamw_" + n, row_block, token)
        record(n, res, transposed)

    for ((n,), flight), transposed in zip(last_flights, (True, False)):
        (own_n,), (land_n,) = _chip_exchange_end(flight, token, "grads_to_chips_%s_end" % n)
        w, m1, m2 = operands(n, transposed)
        *res, token = _adamw_own(w, land_n, own_n, chip, m1, m2, "adamw_" + n, token=token)
        record(n, res, transposed)

    loss, small = _adamw_small(small_all, dev.reshape(1).astype(jnp.int32), wts, mom1, mom2, "adamw_small")
    for n in SMALL:
        grads[n], delta[n], new_m[n], new_v[n] = small[n]

    return (loss, dx[None], *[grads[n] for n in WEIGHT_ORDER], *[delta[n] for n in WEIGHT_ORDER],
            *[new_m[n] for n in WEIGHT_ORDER], *[new_v[n] for n in WEIGHT_ORDER])
```

```python
import types

import jax
import jax.numpy as jnp
from jax import lax
from jax.experimental import pallas as pl
from jax.experimental.pallas import tpu as pltpu

F32 = jnp.float32
BF16 = jnp.bfloat16

D = 1024
DFF = 2816
SB_H = 8
SB_DH = 128
X_H = 4
X_DH = 256
CONV_K = 3
RMS_EPS = 1e-6
N_DEV = 8
N_CHIP = 4
SQ_ROWS = D // N_DEV

ADAM_LR = 0.001
ADAM_B1 = 0.9
ADAM_B2 = 0.999
ADAM_EPS = 1e-08
ADAM_WD = 0.01
ADAM_STEP = 10

TM = 256
TQ = 512
TK = 256
SB_HPS = 2
VMEM_LIMIT = 56 << 20

FF_BLK = DFF // 4
FF_PAD = 768
FF_SUB = 256
DOWN_ROWS = DFF // N_DEV

MIX_MATS = ("w_conv_out", "w_attn_out", "w_o")
CROSS_MATS = ("w_cq", "w_co")

_ANY = pl.BlockSpec(memory_space=pl.ANY)


def _cparams(n_axes=1):
    return pltpu.CompilerParams(
        dimension_semantics=("arbitrary",) * n_axes, vmem_limit_bytes=VMEM_LIMIT)


def _row_spec(tm, n):
    return pl.BlockSpec((tm, n), lambda i: (i, 0))


def _blk_row_spec(nb, tm, n):
    return pl.BlockSpec((nb, tm, n), lambda i: (0, i, 0))


def _const_spec(shape):
    zeros = (0,) * len(shape)
    return pl.BlockSpec(shape, lambda i: zeros)


def _dot(a, b):
    return jnp.dot(a, b, preferred_element_type=F32)


def _dot_nt(a, b):
    return lax.dot_general(a, b, (((1,), (1,)), ((), ())), preferred_element_type=F32)


def _dot_tn(a, b):
    return lax.dot_general(a, b, (((0,), (0,)), ((), ())), preferred_element_type=F32)


def _sigmoid(x):
    return 1.0 / (1.0 + jnp.exp(-x))


def _call(body, operands, *, grid, in_specs, out_specs, out_shape, scratch_shapes, name, comm=None):
    n_in, n_out, n_sc = len(in_specs), len(out_specs), len(scratch_shapes)
    if comm is None:
        outs = pl.pallas_call(
            body, grid=grid, name=name, in_specs=in_specs, out_specs=out_specs, out_shape=out_shape,
            scratch_shapes=scratch_shapes, compiler_params=_cparams(len(grid)))(*operands)
        return list(outs), []
    c_in, c_out, c_sem = len(comm.inputs), len(comm.out_shapes), len(comm.sem_shapes)

    def hosted(*refs):
        bounds = [0, n_in, c_in, n_out, c_out, n_sc, c_sem]
        parts, pos = [], 0
        for k in bounds[1:]:
            parts.append(refs[pos:pos + k])
            pos += k
        ins, cins, outs, couts, scr, sems = parts
        step, n_steps = pl.program_id(0), grid[0]
        for ax in range(1, len(grid)):
            step, n_steps = step * grid[ax] + pl.program_id(ax), n_steps * grid[ax]

        @pl.when(step == 0)
        def _():
            comm.start(cins, couts, sems)

        @pl.when(step == (2 * n_steps) // 3)
        def _():
            comm.middle(cins, couts, sems)

        body(*ins, *outs, *scr)

        @pl.when(step == n_steps - 1)
        def _():
            comm.finish(cins, couts, sems)

    res = pl.pallas_call(
        hosted, grid=grid, name=name, in_specs=list(in_specs) + [_ANY] * c_in,
        out_specs=list(out_specs) + [_ANY] * c_out, out_shape=list(out_shape) + list(comm.out_shapes),
        scratch_shapes=list(scratch_shapes) + list(comm.sem_shapes),
        compiler_params=_cparams(len(grid)))(*operands, *comm.inputs)
    return list(res[:n_out]), list(res[n_out:])


def _load_resident(step, pairs, sems):
    @pl.when(step == 0)
    def _():
        copies = [pltpu.make_async_copy(src, dst, sems.at[k]) for k, (src, dst) in enumerate(pairs)]
        for cp in copies:
            cp.start()
        for cp in copies:
            cp.wait()


def _square_pairs(buf_hbm, index, dst):
    off = index * SQ_ROWS
    return [(buf_hbm.at[d, off:off + SQ_ROWS, :], dst.at[d * SQ_ROWS:(d + 1) * SQ_ROWS, :]) for d in range(N_DEV)]


def _down_pairs(wd_hbm, dst):
    return [(wd_hbm.at[d], dst.at[d // 2, (d % 2) * DOWN_ROWS:(d % 2 + 1) * DOWN_ROWS, :]) for d in range(N_DEV)]


def _zero_down_pad(step, dst):
    @pl.when(step == 0)
    def _():
        dst[:, FF_BLK:, :] = jnp.zeros((4, FF_PAD - FF_BLK, D), BF16)


def _rms_fwd_tile(xt, g):
    r = lax.rsqrt(jnp.mean(xt * xt, axis=-1, keepdims=True) + RMS_EPS)
    return (xt * r) * g


def _rms_bwd_tile(xt, g, dn):
    r = lax.rsqrt(jnp.mean(xt * xt, axis=-1, keepdims=True) + RMS_EPS)
    xhat = xt * r
    dxhat = dn * g
    dx = r * (dxhat - xhat * jnp.mean(dxhat * xhat, axis=-1, keepdims=True))
    dg = jnp.sum(dn * xhat, axis=0, keepdims=True)
    return dx, dg


def _accumulate(ref, step, value):
    @pl.when(step == 0)
    def _():
        ref[...] = value

    @pl.when(step != 0)
    def _():
        ref[...] = ref[...] + value


def _ffn_fwd(x, g, wgu, wd, name, comm=None, head=None):
    t = x.shape[0]

    def body(x_ref, g_ref, wgu_hbm, wd_hbm, *refs):
        if head is None:
            n_ref, gate_ref, up_ref, act_ref, h_ref, wgu_v, wd_v, sems = refs
        else:
            gf_ref, t_ref, n_ref, gate_ref, up_ref, act_ref, dh_ref, loss_ref, dgf_ref, wgu_v, wd_v, sems = refs
        step = pl.program_id(0)
        _zero_down_pad(step, wd_v)
        _load_resident(step, [(wgu_hbm, wgu_v)] + _down_pairs(wd_hbm, wd_v), sems)
        xt = x_ref[...]
        n = _rms_fwd_tile(xt, g_ref[...]).astype(BF16)
        n_ref[...] = n
        acc = jnp.zeros((TM, D), F32)
        for j in range(4):
            for s in range(FF_PAD // FF_SUB):
                lo, hi = s * FF_SUB, (s + 1) * FF_SUB
                gt = _dot_nt(n, wgu_v[j, lo:hi, :])
                ut = _dot_nt(n, wgu_v[4 + j, lo:hi, :])
                gate_ref[j, :, lo:hi] = gt.astype(BF16)
                up_ref[j, :, lo:hi] = ut.astype(BF16)
                act_ref[j, :, lo:hi] = ((gt * _sigmoid(gt)) * ut).astype(BF16)
            acc = acc + _dot(act_ref[j], wd_v[j])
        ht = xt + 0.5 * acc
        if head is None:
            h_ref[...] = ht
        else:
            gain = gf_ref[...]
            diff = _rms_fwd_tile(ht, gain) - t_ref[...]
            part = 0.5 * jnp.sum(jnp.sum(diff * diff, axis=-1, keepdims=True) / D, axis=0, keepdims=True)
            dx, dg = _rms_bwd_tile(ht, gain, diff / D)
            dh_ref[...] = dx
            _accumulate(loss_ref, step, jnp.broadcast_to(part, (8, 128)))
            _accumulate(dgf_ref, step, dg)

    ff = jax.ShapeDtypeStruct((4, t, FF_PAD), BF16)
    operands, in_specs = (x, g, wgu, wd), [_row_spec(TM, D), _const_spec((1, D)), _ANY, _ANY]
    out_specs = [_row_spec(TM, D)] + [_blk_row_spec(4, TM, FF_PAD)] * 3 + [_row_spec(TM, D)]
    out_shape = [jax.ShapeDtypeStruct((t, D), BF16), ff, ff, ff, jax.ShapeDtypeStruct((t, D), F32)]
    if head is not None:
        operands += tuple(head)
        in_specs += [_const_spec((1, D)), _row_spec(TM, D)]
        out_specs += [_const_spec((8, 128)), _const_spec((1, D))]
        out_shape += [jax.ShapeDtypeStruct((8, 128), F32), jax.ShapeDtypeStruct((1, D), F32)]
    return _call(
        body, operands, grid=(t // TM,), name=name, comm=comm, in_specs=in_specs, out_specs=out_specs,
        out_shape=out_shape,
        scratch_shapes=[pltpu.VMEM((N_DEV, FF_PAD, D), BF16), pltpu.VMEM((4, FF_PAD, D), BF16),
                        pltpu.SemaphoreType.DMA((1 + N_DEV,))])


def _ffn_bwd(dh, xin, g, gate, up, wgu, wd, name, comm=None):
    t = dh.shape[0]

    def body(dh_ref, x_ref, g_ref, gate_ref, up_ref, wgu_hbm, wd_hbm,
             dgu_ref, dhb_ref, dx_ref, dg_ref, wgu_v, wd_v, sems):
        step = pl.program_id(0)
        _zero_down_pad(step, wd_v)
        _load_resident(step, [(wgu_hbm, wgu_v)] + _down_pairs(wd_hbm, wd_v), sems)
        dht = dh_ref[...]
        dhb = (0.5 * dht).astype(BF16)
        dhb_ref[...] = dhb
        dn = jnp.zeros((TM, D), F32)
        for j in range(4):
            for s in range(FF_PAD // FF_SUB):
                lo, hi = s * FF_SUB, (s + 1) * FF_SUB
                da = _dot_nt(dhb, wd_v[j, lo:hi, :])
                gt = gate_ref[j, :, lo:hi].astype(F32)
                ut = up_ref[j, :, lo:hi].astype(F32)
                sg = _sigmoid(gt)
                dgt = (da * ut * (sg * (1.0 + gt * (1.0 - sg)))).astype(BF16)
                dut = (da * (gt * sg)).astype(BF16)
                dgu_ref[j, :, lo:hi] = dgt
                dgu_ref[4 + j, :, lo:hi] = dut
            dn = dn + _dot(dgu_ref[j], wgu_v[j]) + _dot(dgu_ref[4 + j], wgu_v[4 + j])
        dx, dg = _rms_bwd_tile(x_ref[...], g_ref[...], dn)
        dx_ref[...] = dht + dx
        _accumulate(dg_ref, step, dg)

    return _call(
        body, (dh, xin, g, gate, up, wgu, wd), grid=(t // TM,), name=name, comm=comm,
        in_specs=[_row_spec(TM, D), _row_spec(TM, D), _const_spec((1, D)), _blk_row_spec(4, TM, FF_PAD),
                  _blk_row_spec(4, TM, FF_PAD), _ANY, _ANY],
        out_specs=[_blk_row_spec(N_DEV, TM, FF_PAD), _row_spec(TM, D), _row_spec(TM, D), _const_spec((1, D))],
        out_shape=[jax.ShapeDtypeStruct((N_DEV, t, FF_PAD), BF16), jax.ShapeDtypeStruct((t, D), BF16),
                   jax.ShapeDtypeStruct((t, D), F32), jax.ShapeDtypeStruct((1, D), F32)],
        scratch_shapes=[pltpu.VMEM((N_DEV, FF_PAD, D), BF16), pltpu.VMEM((4, FF_PAD, D), BF16),
                        pltpu.SemaphoreType.DMA((1 + N_DEV,))])


WIDE_TILES = (1024, 512, 256, 128)


def _pick_tile(n, options=(512, 256, 128)):
    for o in options:
        if n % o == 0:
            return o
    return n


def _mm_tn_squares(pairs, name):
    k, d = pairs[0][0].shape
    n = pairs[0][1].shape[1]
    tn = _pick_tile(n)
    count = len(pairs)

    def body(*refs):
        m = pl.program_id(0)
        for idx in range(count):
            @pl.when(m == idx)
            def _(idx=idx):
                refs[-1][...] = _dot_tn(refs[idx][...], refs[count + idx][...]).astype(BF16).reshape(
                    N_DEV, d // N_DEV, tn)

    a_specs = [pl.BlockSpec((k, d), lambda m, j: (0, 0)) for _ in pairs]
    b_specs = [pl.BlockSpec((k, tn), lambda m, j, idx=idx: (0, jnp.where(m == idx, j, 0))) for idx in range(count)]
    return pl.pallas_call(
        body, grid=(count, n // tn), name=name, in_specs=a_specs + b_specs,
        out_specs=pl.BlockSpec((N_DEV, d // N_DEV, tn), lambda m, j: (0, m, j)),
        out_shape=jax.ShapeDtypeStruct((N_DEV, count * (d // N_DEV), n), BF16),
        compiler_params=_cparams(2),
    )(*[a for a, _ in pairs], *[b for _, b in pairs])


def _mm_tn_cols_many(a, parts, name):
    k, m = a.shape
    n = parts[0].shape[2]
    tn = _pick_tile(n)
    firsts, total = [], 0
    for p in parts:
        firsts.append(total)
        total += p.shape[0]

    def body(a_ref, *refs):
        j = pl.program_id(0)
        for p, first, ref in zip(parts, firsts, refs):
            @pl.when(jnp.logical_and(j >= first, j < first + p.shape[0]))
            def _(ref=ref):
                refs[-1][0] = _dot_tn(a_ref[...].astype(BF16), ref[0].astype(BF16)).astype(BF16)

    specs = [pl.BlockSpec((1, k, tn), lambda j, i, first=first, last=p.shape[0] - 1:
                          (jnp.clip(j - first, 0, last), 0, jnp.where(jnp.logical_and(j >= first, j <= first + last), i, 0)))
             for p, first in zip(parts, firsts)]
    return pl.pallas_call(
        body, grid=(total, n // tn), name=name,
        in_specs=[pl.BlockSpec((k, m), lambda j, i: (0, 0))] + specs,
        out_specs=pl.BlockSpec((1, m, tn), lambda j, i: (j, 0, i)),
        out_shape=jax.ShapeDtypeStruct((total, m, n), BF16),
        compiler_params=_cparams(2),
    )(a, *parts)


def _mm_tn_cols(a, b, name):
    k, m = a.shape
    nb, _, n = b.shape
    tm = _pick_tile(m, WIDE_TILES)

    def body(a_ref, b_ref, o_ref):
        o_ref[0] = _dot_tn(a_ref[...].astype(BF16), b_ref[0].astype(BF16)).astype(BF16)

    return pl.pallas_call(
        body, grid=(nb, m // tm), name=name,
        in_specs=[pl.BlockSpec((k, tm), lambda j, i: (0, i)), pl.BlockSpec((1, k, n), lambda j, i: (j, 0, 0))],
        out_specs=pl.BlockSpec((1, tm, n), lambda j, i: (j, i, 0)),
        out_shape=jax.ShapeDtypeStruct((nb, m, n), BF16),
        compiler_params=_cparams(2),
    )(a, b)


def _mm_tn_rows(a, b, keep, name, comm=None):
    nb, k, m = a.shape
    _, n = b.shape
    tn = _pick_tile(n, WIDE_TILES)

    def body(a_ref, b_ref, o_ref):
        o_ref[0] = _dot_tn(a_ref[0], b_ref[...])[:keep].astype(BF16)

    (out,), couts = _call(
        body, (a, b), grid=(nb, n // tn), name=name, comm=comm,
        in_specs=[pl.BlockSpec((1, k, m), lambda j, i: (j, 0, 0)), pl.BlockSpec((k, tn), lambda j, i: (0, i))],
        out_specs=[pl.BlockSpec((1, keep, tn), lambda j, i: (j, 0, i))],
        out_shape=[jax.ShapeDtypeStruct((nb, keep, n), BF16)], scratch_shapes=[])
    return out if comm is None else (out, couts)


PCG_W = 5 * D
QKV_W = 3 * D
PROJ_SUB = 512


def _inproj_fwd(h, g, w_in, conv_w, name, comm=None):
    t = h.shape[0]

    def body(h_ref, g_ref, w_hbm, cw_ref, u_ref, pcg_ref, qkv_ref, yc_ref, w_v, tail_v, sems):
        step = pl.program_id(0)
        _load_resident(step, [(w_hbm, w_v)], sems)

        @pl.when(step == 0)
        def _():
            tail_v[...] = jnp.zeros_like(tail_v)

        u = _rms_fwd_tile(h_ref[...], g_ref[...]).astype(BF16)
        u_ref[...] = u
        for blk in range(N_DEV):
            for s in range(D // PROJ_SUB):
                lo, hi = s * PROJ_SUB, (s + 1) * PROJ_SUB
                p = _dot(u, w_v[blk, :, lo:hi])
                if blk < 3:
                    pcg_ref[:, blk * D + lo:blk * D + hi] = p
                elif blk < 6:
                    qkv_ref[:, (blk - 3) * D + lo:(blk - 3) * D + hi] = p.astype(BF16)
                else:
                    pcg_ref[:, (blk - 3) * D + lo:(blk - 3) * D + hi] = p
        xc = pcg_ref[:, D:2 * D] * pcg_ref[:, 2 * D:3 * D]
        ext = jnp.concatenate([tail_v[...], xc], axis=0)
        conv = (cw_ref[0:1, :] * pltpu.roll(ext, 2, 0)[8:] + cw_ref[1:2, :] * pltpu.roll(ext, 1, 0)[8:]
                + cw_ref[2:3, :] * xc)
        yc_ref[...] = (pcg_ref[:, 0:D] * conv).astype(BF16)
        tail_v[...] = xc[TM - 8:]

    return _call(
        body, (h, g, w_in, conv_w), grid=(t // TM,), name=name, comm=comm,
        in_specs=[_row_spec(TM, D), _const_spec((1, D)), _ANY, _const_spec((CONV_K, D))],
        out_specs=[_row_spec(TM, D), _row_spec(TM, PCG_W), _row_spec(TM, QKV_W), _row_spec(TM, D)],
        out_shape=[jax.ShapeDtypeStruct((t, D), BF16), jax.ShapeDtypeStruct((t, PCG_W), F32),
                   jax.ShapeDtypeStruct((t, QKV_W), BF16), jax.ShapeDtypeStruct((t, D), BF16)],
        scratch_shapes=[pltpu.VMEM((N_DEV, D, D), BF16), pltpu.VMEM((8, D), F32), pltpu.SemaphoreType.DMA((1,))])


def _tri2(cond):
    rr = lax.broadcasted_iota(jnp.int32, (2 * TK, TK), 0) & (TK - 1)
    cc = lax.broadcasted_iota(jnp.int32, (2 * TK, TK), 1)
    return cond(rr, cc).astype(BF16)


def _causal(shift, row0=0):
    rr = lax.broadcasted_iota(jnp.int32, (TQ - row0, TK), 0) + row0
    cc = lax.broadcasted_iota(jnp.int32, (TQ - row0, TK), 1)
    return cc + shift < rr


def _cumdot(v, tri2):
    hi = v.astype(BF16)
    lo = (v - hi.astype(F32)).astype(BF16)
    return _dot(jnp.concatenate([hi, lo], axis=1), tri2)


LOG2_E = 1.4426950408889634


def _log_1m_beta(z):
    return -(jnp.maximum(z, 0.0) + jnp.log2(1.0 + jnp.exp2(-jnp.abs(z))))


def _sb_specs(t):
    g = SB_H // SB_HPS
    w = SB_HPS * SB_DH
    q_spec = pl.BlockSpec((TQ, w), lambda h, i: (i, h))
    k_spec = pl.BlockSpec((t, w), lambda h, i: (0, g + h))
    v_spec = pl.BlockSpec((t, w), lambda h, i: (0, 2 * g + h))
    ct_spec = pl.BlockSpec((SB_HPS, TQ, 1), lambda h, i: (h, i, 0))
    return g, w, q_spec, k_spec, v_spec, ct_spec


def _sb_fwd(qkv, name, comm=None):
    t = qkv.shape[0]
    scale = SB_DH ** -0.5
    g, w, q_spec, k_spec, v_spec, ct_spec = _sb_specs(t)

    def body(q_ref, k_ref, v_ref, y_ref, ct_ref):
        i = pl.program_id(1)
        later = _tri2(lambda j, s: j > s)
        n_diag = TQ // TK

        def block(j, carry, shift):
            off = pl.multiple_of(j * TK, TK)
            zs, ms = [], []
            for hd in range(SB_HPS):
                cols = slice(hd * SB_DH, (hd + 1) * SB_DH)
                z = _dot_nt(q_ref[:, cols], k_ref[pl.ds(off, TK), cols]) * (scale * LOG2_E)
                m = _log_1m_beta(z)
                if shift is not None:
                    m = jnp.where(_causal(shift), m, 0.0)
                zs.append(z)
                ms.append(m)
            after = _cumdot(jnp.concatenate(ms, axis=0), later)
            out = []
            for hd in range(SB_HPS):
                acc, c_sum = carry[hd]
                cols = slice(hd * SB_DH, (hd + 1) * SB_DH)
                a = jnp.exp2((ms[hd] + zs[hd]) + (c_sum + after[hd * TQ:(hd + 1) * TQ]))
                if shift is not None:
                    a = jnp.where(_causal(shift), a, 0.0)
                out.append((acc + _dot(a.astype(BF16), v_ref[pl.ds(off, TK), cols]),
                            c_sum + jnp.sum(ms[hd], axis=1, keepdims=True)))
            return tuple(out)

        carry = tuple((jnp.zeros((TQ, SB_DH), F32), jnp.zeros((TQ, 1), F32)) for _ in range(SB_HPS))
        for d in reversed(range(n_diag)):
            carry = block(i * n_diag + d, carry, d * TK)
        carry = lax.fori_loop(0, i * n_diag, lambda jj, c: block(i * n_diag - 1 - jj, c, None), carry)
        for hd in range(SB_HPS):
            y_ref[:, hd * SB_DH:(hd + 1) * SB_DH] = carry[hd][0].astype(BF16)
            ct_ref[hd] = carry[hd][1]

    return _call(
        body, (qkv, qkv, qkv), grid=(g, t // TQ), name=name, comm=comm,
        in_specs=[q_spec, k_spec, v_spec],
        out_specs=[q_spec, ct_spec],
        out_shape=[jax.ShapeDtypeStruct((t, D), BF16), jax.ShapeDtypeStruct((SB_H, t, 1), F32)],
        scratch_shapes=[])


def _sb_bwd(qkv, dy, ctot, after, name, comm=None):
    t = qkv.shape[0]
    scale = SB_DH ** -0.5
    g, w, q_spec, k_spec, v_spec, ct_spec = _sb_specs(t)
    acc_spec = pl.BlockSpec((2, t, w), lambda h, i: (0, 0, h))

    def body(q_ref, k_ref, v_ref, dy_ref, ct_ref, after_ref, dq_ref, dkv_ref):
        i = pl.program_id(1)

        @pl.when(i == 0)
        def _():
            dkv_ref[...] = jnp.zeros_like(dkv_ref)

        upto = _tri2(lambda j, s: j <= s)
        n_diag = TQ // TK

        def block(j, carry, shift):
            off = pl.multiple_of(j * TK, TK)
            r0 = 0 if shift is None else shift
            nr = TQ - r0
            causal = None if shift is None else _causal(shift, r0)

            def grow(old, delta):
                return old + delta if r0 == 0 else jnp.concatenate([old[:r0], old[r0:] + delta], axis=0)

            zs, ms = [], []
            for hd in range(SB_HPS):
                cols = slice(hd * SB_DH, (hd + 1) * SB_DH)
                z = _dot_nt(q_ref[r0:, cols], k_ref[pl.ds(off, TK), cols]) * (scale * LOG2_E)
                m = _log_1m_beta(z)
                if causal is not None:
                    m = jnp.where(causal, m, 0.0)
                zs.append(z)
                ms.append(m)
            m_upto = _cumdot(jnp.concatenate(ms, axis=0), upto)
            ls, a_s, es = [], [], []
            for hd in range(SB_HPS):
                cols = slice(hd * SB_DH, (hd + 1) * SB_DH)
                l = ms[hd] + zs[hd]
                a = jnp.exp2(l + ((ct_ref[hd, r0:] - carry[hd][1][r0:]) - m_upto[hd * nr:(hd + 1) * nr]))
                if causal is not None:
                    a = jnp.where(causal, a, 0.0)
                ls.append(l)
                a_s.append(a)
                es.append(_dot_nt(dy_ref[r0:, cols], v_ref[pl.ds(off, TK), cols]) * a)
            e_upto = _dot(jnp.concatenate(es, axis=0).astype(BF16), upto[:TK])
            out = []
            for hd in range(SB_HPS):
                dq, p_sum, e_sum = carry[hd]
                cols = slice(hd * SB_DH, (hd + 1) * SB_DH)
                e = es[hd]
                dz = e - jnp.exp2(ls[hd]) * (e_sum[r0:] + e_upto[hd * nr:(hd + 1) * nr])
                if causal is not None:
                    dz = jnp.where(causal, dz, 0.0)
                dzs = (dz * scale).astype(BF16)
                dkv_ref[0, pl.ds(off, TK), cols] += _dot_tn(dzs, q_ref[r0:, cols])
                dkv_ref[1, pl.ds(off, TK), cols] += _dot_tn(a_s[hd].astype(BF16), dy_ref[r0:, cols])
                out.append((grow(dq, _dot(dzs, k_ref[pl.ds(off, TK), cols])),
                            grow(p_sum, jnp.sum(ms[hd], axis=1, keepdims=True)),
                            grow(e_sum, jnp.sum(e, axis=1, keepdims=True))))
            return tuple(out)

        zero = jnp.zeros((TQ, 1), F32)
        init = tuple((jnp.zeros((TQ, SB_DH), F32), zero, zero) for _ in range(SB_HPS))
        carry = lax.fori_loop(0, i * n_diag, lambda j, c: block(j, c, None), init)
        for d in range(n_diag):
            carry = block(i * n_diag + d, carry, d * TK)
        for hd in range(SB_HPS):
            dq_ref[:, hd * SB_DH:(hd + 1) * SB_DH] = carry[hd][0].astype(BF16)

    return _call(
        body, (qkv, qkv, qkv, dy, ctot, after), grid=(g, t // TQ), name=name, comm=comm,
        in_specs=[q_spec, k_spec, v_spec, q_spec, ct_spec, pl.BlockSpec(after.shape, lambda h, i: (0, 0))],
        out_specs=[q_spec, acc_spec],
        out_shape=[jax.ShapeDtypeStruct((t, D), BF16), jax.ShapeDtypeStruct((2, t, D), F32)],
        scratch_shapes=[])


def _gate_specs():
    return [pl.BlockSpec((TM, D), lambda i: (i, 3)), pl.BlockSpec((TM, D), lambda i: (i, 4))]


def _mix_pairs(mix_hbm, dsts):
    pairs = []
    for index, dst in enumerate(dsts):
        pairs += _square_pairs(mix_hbm, index, dst)
    return pairs


def _mix_out_fwd(yc, ysb, pcg, b_gate, h, w_mix, name, comm=None):
    t = h.shape[0]

    def body(yc_ref, ysb_ref, gc_ref, gs_ref, b_ref, h_ref, mix_hbm,
             a_ref, b_out_ref, mg_ref, h2_ref, wc_v, wa_v, wo_v, sems):
        _load_resident(pl.program_id(0), _mix_pairs(mix_hbm, (wc_v, wa_v, wo_v)), sems)
        a = _dot(yc_ref[...], wc_v[...])
        b = _dot(ysb_ref[...], wa_v[...])
        merged = (_sigmoid(gc_ref[...] + b_ref[:, :D]) * a + _sigmoid(gs_ref[...] + b_ref[:, D:]) * b).astype(BF16)
        a_ref[...] = a
        b_out_ref[...] = b
        mg_ref[...] = merged
        h2_ref[...] = h_ref[...] + _dot(merged, wo_v[...])

    return _call(
        body, (yc, ysb, pcg, pcg, b_gate, h, w_mix), grid=(t // TM,), name=name, comm=comm,
        in_specs=[_row_spec(TM, D), _row_spec(TM, D)] + _gate_specs()
                 + [_const_spec((1, 2 * D)), _row_spec(TM, D), _ANY],
        out_specs=[_row_spec(TM, D)] * 4,
        out_shape=[jax.ShapeDtypeStruct((t, D), F32), jax.ShapeDtypeStruct((t, D), F32),
                   jax.ShapeDtypeStruct((t, D), BF16), jax.ShapeDtypeStruct((t, D), F32)],
        scratch_shapes=[pltpu.VMEM((D, D), BF16)] * 3 + [pltpu.SemaphoreType.DMA((3 * N_DEV,))])


def _mix_out_bwd(dh2, a, b, pcg, b_gate, conv_w, w_mix, name, comm=None):
    t = dh2.shape[0]
    n_tile = t // TM
    per8 = TM // 8

    def rows(n):
        return pl.BlockSpec((TM, n), lambda i: (n_tile - 1 - i, 0))

    def cols(block):
        return pl.BlockSpec((TM, D), lambda i: (n_tile - 1 - i, block))

    def before(block):
        return pl.BlockSpec((8, D), lambda i: (jnp.maximum((n_tile - 1 - i) * per8 - 1, 0), block))

    def body(dh_ref, a_ref, b_ref, gc_ref, gs_ref, cb_ref, cc_ref, cx_ref, ccp_ref, cxp_ref, bias_ref, cw_ref, mix_hbm,
             dhb_ref, da_ref, db_ref, dgp_ref, dc_ref, dysb_ref, dbias_ref, dcw_ref, wc_v, wa_v, wo_v, head_v, sems):
        step = pl.program_id(0)
        _load_resident(step, _mix_pairs(mix_hbm, (wc_v, wa_v, wo_v)), sems)

        @pl.when(step == 0)
        def _():
            head_v[...] = jnp.zeros_like(head_v)
            dcw_ref[...] = jnp.zeros_like(dcw_ref)

        dhb = dh_ref[...].astype(BF16)
        dhb_ref[...] = dhb
        dm = _dot_nt(dhb, wo_v[...])
        gc = _sigmoid(gc_ref[...] + bias_ref[:, :D])
        gs = _sigmoid(gs_ref[...] + bias_ref[:, D:])
        da = (dm * gc).astype(BF16)
        db = (dm * gs).astype(BF16)
        da_ref[...] = da
        db_ref[...] = db
        dgc = dm * a_ref[...] * (gc * (1.0 - gc))
        dgs = dm * b_ref[...] * (gs * (1.0 - gs))
        dgp_ref[0] = dgc.astype(BF16)
        dgp_ref[1] = dgs.astype(BF16)
        _accumulate(dbias_ref.at[:, :D], step, jnp.sum(dgc, axis=0, keepdims=True))
        _accumulate(dbias_ref.at[:, D:], step, jnp.sum(dgs, axis=0, keepdims=True))
        dysb_ref[...] = _dot_nt(db, wa_v[...]).astype(BF16)
        dyc = _dot_nt(da, wc_v[...])
        cc, cx = cc_ref[...], cx_ref[...]
        xc = cc * cx
        xc_before = jnp.where(step == n_tile - 1, 0.0, ccp_ref[...] * cxp_ref[...])
        ext = jnp.concatenate([xc_before, xc], axis=0)
        x1 = pltpu.roll(ext, 1, 0)[8:]
        x2 = pltpu.roll(ext, 2, 0)[8:]
        w0, w1, w2 = cw_ref[0:1, :], cw_ref[1:2, :], cw_ref[2:3, :]
        dc_ref[0] = (dyc * (w0 * x2 + w1 * x1 + w2 * xc)).astype(BF16)
        dconv = dyc * cb_ref[...]
        dcw_ref[0:1, :] += jnp.sum(dconv * x2, axis=0, keepdims=True)
        dcw_ref[1:2, :] += jnp.sum(dconv * x1, axis=0, keepdims=True)
        dcw_ref[2:3, :] += jnp.sum(dconv * xc, axis=0, keepdims=True)
        after = jnp.concatenate([dconv, head_v[...]], axis=0)
        dxc = w2 * dconv + w1 * pltpu.roll(after, TM + 7, 0)[:TM] + w0 * pltpu.roll(after, TM + 6, 0)[:TM]
        dc_ref[1] = (dxc * cx).astype(BF16)
        dc_ref[2] = (dxc * cc).astype(BF16)
        head_v[...] = dconv[:8]

    return _call(
        body, (dh2, a, b, pcg, pcg, pcg, pcg, pcg, pcg, pcg, b_gate, conv_w, w_mix), grid=(n_tile,), name=name,
        comm=comm,
        in_specs=[rows(D)] * 3 + [cols(3), cols(4), cols(0), cols(1), cols(2), before(1), before(2),
                                  _const_spec((1, 2 * D)), _const_spec((CONV_K, D)), _ANY],
        out_specs=[rows(D)] * 3 + [pl.BlockSpec((2, TM, D), lambda i: (0, n_tile - 1 - i, 0)),
                                   pl.BlockSpec((3, TM, D), lambda i: (0, n_tile - 1 - i, 0)), rows(D),
                                   _const_spec((1, 2 * D)), _const_spec((8, D))],
        out_shape=[jax.ShapeDtypeStruct((t, D), BF16)] * 3
                  + [jax.ShapeDtypeStruct((2, t, D), BF16), jax.ShapeDtypeStruct((3, t, D), BF16),
                     jax.ShapeDtypeStruct((t, D), BF16), jax.ShapeDtypeStruct((1, 2 * D), F32),
                     jax.ShapeDtypeStruct((8, D), F32)],
        scratch_shapes=[pltpu.VMEM((D, D), BF16)] * 3 + [pltpu.VMEM((8, D), F32),
                                                         pltpu.SemaphoreType.DMA((3 * N_DEV,))])


def _inproj_bwd(dconv, dq, dkv, dgp, w_in, h, g, dh_res, name, comm=None):
    t = h.shape[0]

    def body(dc_ref, dq_ref, dkv_ref, dgp_ref, w_hbm, h_ref, g_ref, dres_ref, dh_ref, dg_ref, w_v, sems):
        step = pl.program_id(0)
        _load_resident(step, [(w_hbm, w_v)], sems)
        du = _dot_nt(dq_ref[...], w_v[3])
        for k in range(3):
            du = du + _dot_nt(dc_ref[k], w_v[k])
        for k in range(2):
            du = du + _dot_nt(dkv_ref[k].astype(BF16), w_v[4 + k]) + _dot_nt(dgp_ref[k], w_v[6 + k])
        dx, dg = _rms_bwd_tile(h_ref[...], g_ref[...], du)
        dh_ref[...] = dres_ref[...] + dx
        _accumulate(dg_ref, step, dg)

    return _call(
        body, (dconv, dq, dkv, dgp, w_in, h, g, dh_res), grid=(t // TM,), name=name, comm=comm,
        in_specs=[_blk_row_spec(3, TM, D), _row_spec(TM, D), _blk_row_spec(2, TM, D), _blk_row_spec(2, TM, D), _ANY,
                  _row_spec(TM, D), _const_spec((1, D)), _row_spec(TM, D)],
        out_specs=[_row_spec(TM, D), _const_spec((1, D))],
        out_shape=[jax.ShapeDtypeStruct((t, D), F32), jax.ShapeDtypeStruct((1, D), F32)],
        scratch_shapes=[pltpu.VMEM((N_DEV, D, D), BF16), pltpu.SemaphoreType.DMA((1,))])


def _softmax_rows(s):
    e = jnp.exp(s - jnp.max(s, axis=-1, keepdims=True))
    return e / jnp.sum(e, axis=-1, keepdims=True)


def _cross_pairs(cross_hbm, wq_v, wo_v):
    return _square_pairs(cross_hbm, 0, wq_v) + _square_pairs(cross_hbm, 1, wo_v)


def _cross_fwd(h, g, mem, g_mem, w_ckv, w_cross, name):
    t = h.shape[0]
    m = mem.shape[0]
    scale = X_DH ** -0.5

    def body(h_ref, g_ref, mem_ref, gm_ref, wkv_ref, cross_hbm, hn_ref, qx_ref, o_ref, h3_ref, mn_ref, kv_ref,
             wq_v, wo_v, sems):
        _load_resident(pl.program_id(0), _cross_pairs(cross_hbm, wq_v, wo_v), sems)

        @pl.when(pl.program_id(0) == 0)
        def _():
            mn = _rms_fwd_tile(mem_ref[...], gm_ref[...]).astype(BF16)
            mn_ref[...] = mn
            for j in range(N_DEV):
                kv_ref[j] = _dot(mn, wkv_ref[j]).astype(BF16)

        ht = h_ref[...]
        hn = _rms_fwd_tile(ht, g_ref[...]).astype(BF16)
        hn_ref[...] = hn
        qx = _dot(hn, wq_v[...]).astype(BF16)
        qx_ref[...] = qx
        for hd in range(X_H):
            lo, hi = hd * X_DH, (hd + 1) * X_DH
            p = _softmax_rows(_dot_nt(qx[:, lo:hi], kv_ref[hd]) * scale)
            o_ref[:, lo:hi] = _dot(p.astype(BF16), kv_ref[X_H + hd]).astype(BF16)
        h3_ref[...] = ht + _dot(o_ref[...], wo_v[...])

    return pl.pallas_call(
        body, grid=(t // TM,), name=name,
        in_specs=[_row_spec(TM, D), _const_spec((1, D)), _const_spec((m, D)), _const_spec((1, D)),
                  _const_spec((N_DEV, D, X_DH)), _ANY],
        out_specs=[_row_spec(TM, D)] * 4 + [_const_spec((m, D)), _const_spec((N_DEV, m, X_DH))],
        out_shape=[jax.ShapeDtypeStruct((t, D), BF16)] * 3 + [jax.ShapeDtypeStruct((t, D), F32),
                                                              jax.ShapeDtypeStruct((m, D), BF16),
                                                              jax.ShapeDtypeStruct((N_DEV, m, X_DH), BF16)],
        scratch_shapes=[pltpu.VMEM((D, D), BF16)] * 2 + [pltpu.SemaphoreType.DMA((2 * N_DEV,))],
        compiler_params=_cparams(),
    )(h, g, mem, g_mem, w_ckv, w_cross)


def _cross_bwd(dh3, h, g, qx, kv, mem, g_mem, w_ckv, w_cross, name, comm=None):
    t = h.shape[0]
    m = kv.shape[1]
    scale = X_DH ** -0.5

    def body(dh_ref, h_ref, g_ref, qx_ref, kv_ref, mem_ref, gm_ref, wkv_ref, cross_hbm,
             dhb_ref, dqx_ref, dkv_ref, dh2_ref, dg_ref, dgm_ref, wq_v, wo_v, sems):
        step = pl.program_id(0)
        _load_resident(step, _cross_pairs(cross_hbm, wq_v, wo_v), sems)

        @pl.when(step == 0)
        def _():
            dkv_ref[...] = jnp.zeros_like(dkv_ref)

        dht = dh_ref[...]
        dhb = dht.astype(BF16)
        dhb_ref[...] = dhb
        do = _dot_nt(dhb, wo_v[...]).astype(BF16)
        for hd in range(X_H):
            lo, hi = hd * X_DH, (hd + 1) * X_DH
            qh = qx_ref[:, lo:hi]
            kh = kv_ref[hd]
            p = _softmax_rows(_dot_nt(qh, kh) * scale)
            doh = do[:, lo:hi]
            dp = _dot_nt(doh, kv_ref[X_H + hd])
            ds = (p * (dp - jnp.sum(dp * p, axis=-1, keepdims=True)) * scale).astype(BF16)
            dqx_ref[:, lo:hi] = _dot(ds, kh).astype(BF16)
            dkv_ref[hd] += _dot_tn(ds, qh)
            dkv_ref[X_H + hd] += _dot_tn(p.astype(BF16), doh)
        dhn = _dot_nt(dqx_ref[...], wq_v[...])
        dx, dg = _rms_bwd_tile(h_ref[...], g_ref[...], dhn)
        dh2_ref[...] = dht + dx
        _accumulate(dg_ref, step, dg)

        @pl.when(step == t // TM - 1)
        def _():
            dmn = jnp.zeros((m, D), F32)
            for j in range(N_DEV):
                dmn = dmn + _dot_nt(dkv_ref[j].astype(BF16), wkv_ref[j])
            dgm_ref[...] = _rms_bwd_tile(mem_ref[...], gm_ref[...], dmn)[1]

    return _call(
        body, (dh3, h, g, qx, kv, mem, g_mem, w_ckv, w_cross), grid=(t // TM,), name=name, comm=comm,
        in_specs=[_row_spec(TM, D), _row_spec(TM, D), _const_spec((1, D)), _row_spec(TM, D),
                  _const_spec((N_DEV, m, X_DH)), _const_spec((m, D)), _const_spec((1, D)),
                  _const_spec((N_DEV, D, X_DH)), _ANY],
        out_specs=[_row_spec(TM, D), _row_spec(TM, D), _const_spec((N_DEV, m, X_DH)), _row_spec(TM, D),
                   _const_spec((1, D)), _const_spec((1, D))],
        out_shape=[jax.ShapeDtypeStruct((t, D), BF16), jax.ShapeDtypeStruct((t, D), BF16),
                   jax.ShapeDtypeStruct((N_DEV, m, X_DH), F32), jax.ShapeDtypeStruct((t, D), F32),
                   jax.ShapeDtypeStruct((1, D), F32), jax.ShapeDtypeStruct((1, D), F32)],
        scratch_shapes=[pltpu.VMEM((D, D), BF16)] * 2 + [pltpu.SemaphoreType.DMA((2 * N_DEV,))])


def _adamw_small(land, own, ids, wts, mom1, mom2, name):
    n_slots = land.shape[0]
    conv_rows, conv_cols = wts["conv_w"].shape
    shapes = {n: (1, wts[n].size) for n in GAINS + ("b_gate",)}
    shapes["conv_w"] = (conv_rows, conv_cols)
    n_par = len(SMALL)
    gate_row, conv_row = len(GAINS), len(GAINS) + 2

    def body(ids_ref, land_ref, own_ref, land_cols, own_cols, *refs):
        params, loss_ref, outs, acc = refs[:3 * n_par], refs[3 * n_par], refs[3 * n_par + 1:-1], refs[-1]

        def chips_sum(theirs, mine, rows):
            total = jnp.where(ids_ref[1] == 0, mine[0, rows, :], theirs[0, rows, :])
            for k in range(1, n_slots):
                total = total + jnp.where(ids_ref[1] == k, mine[0, rows, :], theirs[k, rows, :])
            return total

        acc[...] = chips_sum(land_ref, own_ref, slice(0, SMALL_R))
        loss_ref[...] = acc[LOSS_ROW:LOSS_ROW + 1, 0:1]
        for i, n in enumerate(SMALL):
            w_ref, m_ref, v_ref = params[3 * i:3 * i + 3]
            res = outs[4 * i:4 * i + 4]
            if n == "conv_w":
                gt = chips_sum(land_cols, own_cols, slice(conv_row, conv_row + conv_rows))
                _adamw_apply(gt, w_ref, m_ref, v_ref, *res)
            elif n == "b_gate":
                for half in range(2):
                    cols = slice(half * D, (half + 1) * D)
                    _adamw_apply(acc[gate_row + half:gate_row + half + 1, :], w_ref.at[:, cols], m_ref.at[:, cols],
                                 v_ref.at[:, cols], *[r.at[:, cols] for r in res])
            else:
                _adamw_apply(acc[i:i + 1, :], w_ref, m_ref, v_ref, *res)

    def whole(shape):
        return pl.BlockSpec(shape, lambda i, dev_ref: (0,) * len(shape))

    operands = [a.reshape(shapes[n]) for n in SMALL for a in (wts[n], mom1[n], mom2[n])]
    out_shapes = [shapes[n] for n in SMALL for _ in range(4)]
    outs = pl.pallas_call(
        body, name=name,
        grid_spec=pltpu.PrefetchScalarGridSpec(
            num_scalar_prefetch=1, grid=(1,),
            in_specs=[whole(land.shape), whole((1, SMALL_R, D)),
                      pl.BlockSpec((n_slots, SMALL_R, conv_cols), lambda i, ids_ref: (0, 0, ids_ref[0])),
                      pl.BlockSpec((1, SMALL_R, conv_cols), lambda i, ids_ref: (0, 0, ids_ref[0]))]
            + [whole(a.shape) for a in operands],
            out_specs=[whole((1, 1))] + [whole(s) for s in out_shapes],
            scratch_shapes=[pltpu.VMEM((SMALL_R, D), F32)]),
        out_shape=[jax.ShapeDtypeStruct((1, 1), F32)] + [jax.ShapeDtypeStruct(s, F32) for s in out_shapes],
        compiler_params=_cparams(),
    )(ids, land, own, land, own, *operands)
    small = {n: tuple(o.reshape(wts[n].shape) for o in outs[1 + 4 * i:5 + 4 * i]) for i, n in enumerate(SMALL)}
    return outs[0].reshape(()), small


def _adamw_own(w, land, own, chip, m, v, name, row_block=0, token=None):
    r, c = w.shape
    tr = _pick_tile(r, (256, 352, 128))
    off = row_block * (r // tr)

    def body(chip_ref, w_ref, land_ref, own_ref, m_ref, v_ref, *rest):
        mine = own_ref[0].astype(F32)
        gt = jnp.where(chip_ref[0] == 0, mine, land_ref[0].astype(F32))
        for k in range(1, N_CHIP):
            gt = gt + jnp.where(chip_ref[0] == k, mine, land_ref[k].astype(F32))
        if token is None:
            _adamw_apply(gt, w_ref, m_ref, v_ref, *rest)
        else:
            before, *outs, later = rest
            _adamw_apply(gt + before[0:1, 0:1], w_ref, m_ref, v_ref, *outs)
            later[...] = before[...]

    spec = pl.BlockSpec((tr, c), lambda i, chip_ref: (i, 0))
    in_specs = [spec, pl.BlockSpec((N_CHIP, tr, c), lambda i, chip_ref: (0, i + off, 0)),
                pl.BlockSpec((1, tr, c), lambda i, chip_ref: (chip_ref[0], i + off, 0)), spec, spec]
    operands = (chip, w, land, own, m, v)
    out_specs, out_shape = [spec] * 4, [jax.ShapeDtypeStruct((r, c), F32)] * 4
    if token is not None:
        token_spec = pl.BlockSpec(token.shape, lambda i, chip_ref: (0, 0))
        in_specs.append(token_spec)
        operands += (token,)
        out_specs, out_shape = out_specs + [token_spec], out_shape + [jax.ShapeDtypeStruct(token.shape, token.dtype)]
    return pl.pallas_call(
        body, name=name,
        grid_spec=pltpu.PrefetchScalarGridSpec(
            num_scalar_prefetch=1, grid=(r // tr,), in_specs=in_specs, out_specs=out_specs),
        out_shape=out_shape,
        compiler_params=_cparams(),
    )(*operands)


def _adamw_apply(gt, w_ref, m_ref, v_ref, g_ref, d_ref, nm_ref, nv_ref):
    g_ref[...] = gt
    nm = ADAM_B1 * m_ref[...] + (1.0 - ADAM_B1) * gt
    nv = ADAM_B2 * v_ref[...] + (1.0 - ADAM_B2) * jnp.square(gt)
    m_hat = nm / (1.0 - ADAM_B1 ** ADAM_STEP)
    v_hat = nv / (1.0 - ADAM_B2 ** ADAM_STEP)
    d_ref[...] = -ADAM_LR * (m_hat / (jnp.sqrt(v_hat) + ADAM_EPS) + ADAM_WD * w_ref[...])
    nm_ref[...] = nm
    nv_ref[...] = nv


def _mesh_pos():
    return lax.axis_index("x"), lax.axis_index("y"), lax.axis_index("c")


def _both(first, second):
    n_in, n_out, n_sem = len(first.inputs), len(first.out_shapes), len(first.sem_shapes)

    def run(round_name):
        def both(in_refs, out_refs, sems):
            getattr(first, round_name)(in_refs[:n_in], out_refs[:n_out], sems[:n_sem])
            getattr(second, round_name)(in_refs[n_in:], out_refs[n_out:], sems[n_sem:])
        return both

    return types.SimpleNamespace(
        inputs=first.inputs + second.inputs, out_shapes=first.out_shapes + second.out_shapes,
        sem_shapes=first.sem_shapes + second.sem_shapes, start=run("start"), middle=run("middle"),
        finish=run("finish"))


def _no_round(in_refs, out_refs, sems):
    pass


def _after(token):
    return types.SimpleNamespace(inputs=[token], out_shapes=[], sem_shapes=[], start=_no_round, middle=_no_round,
                                 finish=_no_round)


def _run_exchange(comm, name):
    c_in, c_out = len(comm.inputs), len(comm.out_shapes)

    def body(*refs):
        cins, couts, sems = refs[:c_in], refs[c_in:c_in + c_out], refs[c_in + c_out:]
        comm.start(cins, couts, sems)
        comm.middle(cins, couts, sems)
        comm.finish(cins, couts, sems)

    return list(pl.pallas_call(
        body, name=name, out_shape=list(comm.out_shapes),
        in_specs=[_ANY] * c_in, out_specs=[_ANY] * c_out, scratch_shapes=list(comm.sem_shapes),
    )(*comm.inputs))


def _gather_exchange(shards):
    n_arr = len(shards)

    def plan(x_refs, out_refs, sems):
        send_sems, recv_sems, local_sems = sems[:3]
        stage = sems[3:]
        x, y, c = _mesh_pos()
        me, sibling = (x, y, c), (x, y, 1 - c)
        xn, yn, diag = (1 - x, y), (x, 1 - y), (1 - x, 1 - y)

        def slot(a, px, py, pc, half=None):
            ref = out_refs[a].at[4 * px + 2 * py + pc]
            if half is None:
                return ref
            rows = shards[a].shape[0] // 2
            return ref.at[half * rows:(half + 1) * rows]

        def copy(a, k, block, to, half=None, src=None):
            dst = slot(a, *block, half)
            return pltpu.make_async_remote_copy(
                src_ref=dst if src is None else src, dst_ref=dst,
                send_sem=send_sems.at[a, k], recv_sem=recv_sems.at[a, k],
                device_id=to, device_id_type=pl.DeviceIdType.MESH)

        return types.SimpleNamespace(
            me=me, sibling=sibling, xn=xn, yn=yn, diag=diag, c=c, copy=copy,
            mine_in=[pltpu.make_async_copy(x_refs[a], stage[a], local_sems.at[a, 0]) for a in range(n_arr)],
            mine_out=[pltpu.make_async_copy(stage[a], slot(a, *me), local_sems.at[a, 1]) for a in range(n_arr)],
            first=[cp for a in range(n_arr) for cp in (
                copy(a, 0, me, sibling, src=x_refs[a]), copy(a, 1, me, (*xn, c), src=x_refs[a]),
                copy(a, 2, me, (*yn, c), src=x_refs[a]))],
            second=lambda a: (copy(a, 3, (*xn, c), (*yn, c), half=0), copy(a, 5, (*xn, c), sibling),
                              copy(a, 4, (*yn, c), (*xn, c), half=1), copy(a, 6, (*yn, c), sibling)),
            third=lambda a: (copy(a, 7, (*diag, c), sibling, half=0), copy(a, 8, (*diag, c), sibling, half=1)))

    def start(x_refs, out_refs, sems):
        p = plan(x_refs, out_refs, sems)
        for cp in p.first + p.mine_in:
            cp.start()
        for cp_in, cp_out in zip(p.mine_in, p.mine_out):
            cp_in.wait()
            cp_out.start()

    def middle(x_refs, out_refs, sems):
        p = plan(x_refs, out_refs, sems)
        for a in range(n_arr):
            to_yn, x_to_sib, to_xn, y_to_sib = p.second(a)
            p.copy(a, 1, (*p.xn, p.c), p.me).wait_recv()
            to_yn.start()
            x_to_sib.start()
            p.copy(a, 2, (*p.yn, p.c), p.me).wait_recv()
            to_xn.start()
            y_to_sib.start()

    def finish(x_refs, out_refs, sems):
        p = plan(x_refs, out_refs, sems)
        for a in range(n_arr):
            half0_to_sib, half1_to_sib = p.third(a)
            p.copy(a, 3, (*p.diag, p.c), p.me, half=0).wait_recv()
            half0_to_sib.start()
            p.copy(a, 4, (*p.diag, p.c), p.me, half=1).wait_recv()
            half1_to_sib.start()
        other = 1 - p.c
        for a in range(n_arr):
            p.copy(a, 0, p.sibling, p.me).wait_recv()
            p.copy(a, 5, (*p.xn, other), p.me).wait_recv()
            p.copy(a, 6, (*p.yn, other), p.me).wait_recv()
            p.copy(a, 7, (*p.diag, other), p.me, half=0).wait_recv()
            p.copy(a, 8, (*p.diag, other), p.me, half=1).wait_recv()
        for cp in p.first:
            cp.wait_send()
        for a in range(n_arr):
            for cp in p.second(a) + p.third(a):
                cp.wait_send()
        for cp in p.mine_out:
            cp.wait()

    return types.SimpleNamespace(
        inputs=list(shards), start=start, middle=middle, finish=finish,
        out_shapes=[jax.ShapeDtypeStruct((N_DEV,) + s.shape, s.dtype) for s in shards],
        sem_shapes=[pltpu.SemaphoreType.DMA((n_arr, 9)), pltpu.SemaphoreType.DMA((n_arr, 9)),
                    pltpu.SemaphoreType.DMA((n_arr, 2))] + [pltpu.VMEM(s.shape, s.dtype) for s in shards])


def _pair_exchange(grads):
    n_arr = len(grads)

    def plan(g_refs, land_refs, sems):
        send_sems, recv_sems = sems
        x, y, c = _mesh_pos()
        return [pltpu.make_async_remote_copy(
            src_ref=g_refs[a].at[2 * k + 1 - c], dst_ref=land_refs[a].at[k],
            send_sem=send_sems.at[a, k], recv_sem=recv_sems.at[a, k],
            device_id=(x, y, 1 - c), device_id_type=pl.DeviceIdType.MESH)
            for a in range(n_arr) for k in range(N_CHIP)]

    def start(g_refs, land_refs, sems):
        for cp in plan(g_refs, land_refs, sems):
            cp.start()

    def finish(g_refs, land_refs, sems):
        for cp in plan(g_refs, land_refs, sems):
            cp.wait()

    return types.SimpleNamespace(
        inputs=list(grads), start=start, middle=_no_round, finish=finish,
        out_shapes=[jax.ShapeDtypeStruct((N_CHIP,) + g.shape[1:], g.dtype) for g in grads],
        sem_shapes=[pltpu.SemaphoreType.DMA((n_arr, N_CHIP)), pltpu.SemaphoreType.DMA((n_arr, N_CHIP))])


def _sibling_swap(block):
    def plan(x_ref, out_ref, sems):
        send_sem, recv_sem = sems
        x, y, c = _mesh_pos()
        return pltpu.make_async_remote_copy(
            src_ref=x_ref, dst_ref=out_ref, send_sem=send_sem.at[0], recv_sem=recv_sem.at[0],
            device_id=(x, y, 1 - c), device_id_type=pl.DeviceIdType.MESH)

    def start(x_refs, out_refs, sems):
        plan(x_refs[0], out_refs[0], sems).start()

    def finish(x_refs, out_refs, sems):
        plan(x_refs[0], out_refs[0], sems).wait()

    return types.SimpleNamespace(
        inputs=[block], start=start, middle=_no_round, finish=finish,
        out_shapes=[jax.ShapeDtypeStruct(block.shape, block.dtype)],
        sem_shapes=[pltpu.SemaphoreType.DMA((1,)), pltpu.SemaphoreType.DMA((1,))])


def _chip_exchange(parts):
    n_arr = len(parts)

    def plan(p_refs, land_refs, sems):
        send_sems, recv_sems, local_sems = sems
        x, y, c = _mesh_pos()
        my_chip = 2 * x + y
        chips = [(1 - x, y), (x, 1 - y), (1 - x, 1 - y)]
        local = [pltpu.make_async_copy(p_refs[a].at[my_chip], land_refs[a].at[my_chip], local_sems.at[a])
                 for a in range(n_arr)]

        def copy(a, k, src_slot, dst_slot, px, py):
            return pltpu.make_async_remote_copy(
                src_ref=p_refs[a].at[src_slot], dst_ref=land_refs[a].at[dst_slot],
                send_sem=send_sems.at[a, k], recv_sem=recv_sems.at[a, k],
                device_id=(px, py, c), device_id_type=pl.DeviceIdType.MESH)

        sends = [copy(a, k, 2 * px + py, my_chip, px, py) for a in range(n_arr) for k, (px, py) in enumerate(chips)]
        arrivals = [copy(a, k, my_chip, 2 * px + py, px, py) for a in range(n_arr)
                    for k, (px, py) in enumerate(chips)]
        return local, sends, arrivals

    def start(p_refs, land_refs, sems):
        local, sends, _ = plan(p_refs, land_refs, sems)
        for cp in local + sends:
            cp.start()

    def finish(p_refs, land_refs, sems):
        local, sends, arrivals = plan(p_refs, land_refs, sems)
        for cp in arrivals:
            cp.wait_recv()
        for cp in sends:
            cp.wait_send()
        for cp in local:
            cp.wait()

    return types.SimpleNamespace(
        inputs=list(parts), start=start, middle=_no_round, finish=finish,
        out_shapes=[jax.ShapeDtypeStruct(p.shape, p.dtype) for p in parts],
        sem_shapes=[pltpu.SemaphoreType.DMA((n_arr, 3)), pltpu.SemaphoreType.DMA((n_arr, 3)),
                    pltpu.SemaphoreType.DMA((n_arr,))])


_HBM = pl.BlockSpec(memory_space=pltpu.HBM)
_SEM = pl.BlockSpec(memory_space=pltpu.SEMAPHORE)
_DATAFLOW = pltpu.SideEffectType.DATAFLOW_SIDE_EFFECTING


def _chip_copies(p_refs, land_refs, send_sems, recv_sems):
    x, y, c = _mesh_pos()
    my_chip = 2 * x + y
    chips = [(1 - x, y), (x, 1 - y), (1 - x, 1 - y)]
    return [pltpu.make_async_remote_copy(
        src_ref=p_refs[a].at[2 * px + py], dst_ref=land_refs[a].at[my_chip],
        send_sem=send_sems[3 * a + k], recv_sem=recv_sems[3 * a + k],
        device_id=(px, py, c), device_id_type=pl.DeviceIdType.MESH)
        for a in range(len(p_refs)) for k, (px, py) in enumerate(chips)]


def _chip_exchange_begin(parts, name):
    n_arr = len(parts)
    n_buf, n_copy = 2 * n_arr, 3 * n_arr
    lands = [lax.empty(p.shape, p.dtype) for p in parts]

    def body(*refs):
        p_refs, land_refs = refs[:n_arr], refs[n_arr:n_buf]
        send_sems, recv_sems, token = refs[n_buf:n_buf + n_copy], refs[n_buf + n_copy:n_buf + 2 * n_copy], refs[-1]
        for cp in _chip_copies(p_refs, land_refs, send_sems, recv_sems):
            cp.start()
        token[...] = jnp.zeros_like(token)

    bufs = list(parts) + list(lands)
    outs = pl.pallas_call(
        body, name=name,
        out_shape=(*[pltpu.SemaphoreType.DMA(())] * (2 * n_copy), *[pltpu.HBM(b.shape, b.dtype) for b in bufs],
                   jax.ShapeDtypeStruct((8, 128), F32)),
        in_specs=[_HBM] * n_buf,
        out_specs=(*[_SEM] * (2 * n_copy), *[_HBM] * n_buf, pl.BlockSpec(memory_space=pltpu.VMEM)),
        input_output_aliases={i: 2 * n_copy + i for i in range(n_buf)},
        compiler_params=pltpu.CompilerParams(has_side_effects=_DATAFLOW),
    )(*[pltpu.with_memory_space_constraint(b, pltpu.HBM) for b in bufs])
    sems = list(outs[:2 * n_copy])
    thru = list(outs[2 * n_copy:2 * n_copy + n_buf])
    return types.SimpleNamespace(send_sems=sems[:n_copy], recv_sems=sems[n_copy:], parts=thru[:n_arr],
                                 lands=thru[n_arr:], token=outs[-1])


def _chip_exchange_end(flight, after, name):
    send_sems, recv_sems, parts, lands = flight.send_sems, flight.recv_sems, flight.parts, flight.lands
    n_arr = len(parts)
    n_buf, n_copy = 2 * n_arr, 3 * n_arr

    def body(*refs):
        p_refs, land_refs = refs[:n_arr], refs[n_arr:n_buf]
        sems = refs[n_buf:n_buf + 2 * n_copy]
        for cp in _chip_copies(p_refs, land_refs, sems[:n_copy], sems[n_copy:]):
            cp.wait_send()
            cp.wait_recv()

    bufs = list(parts) + list(lands)
    outs = pl.pallas_call(
        body, name=name, out_shape=tuple(pltpu.HBM(b.shape, b.dtype) for b in bufs),
        in_specs=[_HBM] * n_buf + [_SEM] * (2 * n_copy) + [_ANY], out_specs=tuple([_HBM] * n_buf),
        input_output_aliases={i: i for i in range(n_buf)},
        compiler_params=pltpu.CompilerParams(has_side_effects=_DATAFLOW),
    )(*bufs, *send_sems, *recv_sems, after)
    return list(outs[:n_arr]), list(outs[n_arr:])


def _row_tile(r, cap=640):
    best = None
    for cand in range(16, min(r, cap) + 1, 16):
        if r % cand == 0:
            best = cand
    return best if best is not None else r


def _pair_sum(gs, landeds, core, name):
    tiles = [_row_tile(g.shape[1]) for g in gs]
    counts = [g.shape[1] // tr for g, tr in zip(gs, tiles)]
    n_arr = len(gs)

    def body(core_ref, *refs):
        for a in range(n_arr):
            mine, theirs, out = refs[2 * a], refs[2 * a + 1], refs[2 * n_arr + a]
            out[0] = (mine[0].astype(F32) + theirs[0].astype(F32)).astype(out.dtype)

    in_specs, out_specs, operands = [], [], []
    for g, landed, tr, count in zip(gs, landeds, tiles, counts):
        c_dim = g.shape[2]
        last = count - 1
        in_specs += [pl.BlockSpec((1, tr, c_dim),
                                  lambda k, i, core_ref, last=last: (2 * k + core_ref[0], jnp.minimum(i, last), 0)),
                     pl.BlockSpec((1, tr, c_dim), lambda k, i, core_ref, last=last: (k, jnp.minimum(i, last), 0))]
        out_specs.append(pl.BlockSpec((1, tr, c_dim), lambda k, i, core_ref, last=last: (k, jnp.minimum(i, last), 0)))
        operands += [g, landed]
    return list(pl.pallas_call(
        body, name=name,
        grid_spec=pltpu.PrefetchScalarGridSpec(
            num_scalar_prefetch=1, grid=(N_CHIP, max(counts)), in_specs=in_specs, out_specs=out_specs),
        out_shape=[jax.ShapeDtypeStruct((N_CHIP,) + g.shape[1:], g.dtype) for g in gs],
        compiler_params=_cparams(2),
    )(core, *operands))


GAINS = ("g_ffn1", "g_mix", "g_cross", "g_mem", "g_ffn2", "g_final")
SMALL = GAINS + ("b_gate", "conv_w")
SMALL_R = 16
LOSS_ROW = 11
WEIGHT_ORDER = ("g_ffn1", "w_ffn1_gu", "w_ffn1_down", "g_mix", "w_in", "b_gate", "conv_w", "w_conv_out",
                "w_attn_out", "w_o", "g_cross", "g_mem", "w_cq", "w_ckv", "w_co", "g_ffn2", "w_ffn2_gu",
                "w_ffn2_down", "g_final")
GU_NAMES = ("w_ffn1_gu", "w_ffn2_gu")


def _pack_small(vals, conv_rows):
    rows = [vals[n].reshape(1, D) for n in GAINS] + [vals["b_gate"].reshape(2, D), conv_rows.reshape(CONV_K, D)]
    used = len(GAINS) + 2 + CONV_K
    return jnp.concatenate(rows + [jnp.zeros((SMALL_R - used, D), F32)], axis=0)


def _exchange_shards(wts):
    out = {n: jnp.pad(wts[n].T.astype(BF16), ((0, FF_PAD - FF_BLK), (0, 0))) for n in GU_NAMES}
    for n in ("w_ckv", "w_in", "w_ffn1_down", "w_ffn2_down"):
        out[n] = wts[n].astype(BF16)
    out["mix"] = jnp.concatenate([wts[n].astype(BF16) for n in MIX_MATS], axis=0)
    out["cross"] = jnp.concatenate([wts[n].astype(BF16) for n in CROSS_MATS], axis=0)
    return out


def _reduce_group(grads, landed, core, names):
    return _pair_sum(grads, landed, core, "grads_pair_sum_" + "_".join(names))


def _step(x, mem, target, sh, conv_pad, gains, b_gate, core):
    wg1, wd1, conv_all = _run_exchange(_gather_exchange([sh["w_ffn1_gu"], sh["w_ffn1_down"], conv_pad]), "gather_ffn1")
    conv_w = conv_all[:, :CONV_K, :].transpose(1, 0, 2).reshape(CONV_K, D)
    (n1, gate1, up1, act1, h1), (w_in,) = _ffn_fwd(
        x, gains["g_ffn1"], wg1, wd1, "ffn1_fwd", comm=_gather_exchange([sh["w_in"]]))
    (u, pcg, qkv, yc), (w_mix, wd2) = _inproj_fwd(h1, gains["g_mix"], w_in, conv_w, "inproj_fwd",
                                                  comm=_gather_exchange([sh["mix"], sh["w_ffn2_down"]]))
    (ysb, ctot), (w_cross, w_ckv, wg2) = _sb_fwd(
        qkv, "sb_fwd", comm=_gather_exchange([sh["cross"], sh["w_ckv"], sh["w_ffn2_gu"]]))
    (a_mix, b_mix, merged, h2), _ = _mix_out_fwd(yc, ysb, pcg, b_gate, h1, w_mix, "mix_out_fwd")
    hn, qx, o_x, h3, mn, kv = _cross_fwd(h2, gains["g_cross"], mem, gains["g_mem"], w_ckv, w_cross, "cross_fwd")
    (n4, gate2, up2, act2, dh4, loss, dg_final), _ = _ffn_fwd(h3, gains["g_ffn2"], wg2, wd2, "ffn2_fwd",
                                                              head=(gains["g_final"], target))

    gs = {"g_final": dg_final}
    (dgu2, dh4b, dh3, gs["g_ffn2"]), _ = _ffn_bwd(dh4, h3, gains["g_ffn2"], gate2, up2, wg2, wd2, "ffn2_bwd")
    grads_a = [_mm_tn_rows(dgu2, n4, FF_PAD, "dw_ffn2_gu"),
               _mm_tn_rows(act2, dh4b, FF_BLK, "dw_ffn2_down").reshape(N_DEV, DOWN_ROWS, D)]
    names_a = ["w_ffn2_gu", "w_ffn2_down"]
    (dh3b, dqx, dkv, dh2, gs["g_cross"], gs["g_mem"]), _ = _cross_bwd(
        dh3, h2, gains["g_cross"], qx, kv, mem, gains["g_mem"], w_ckv, w_cross, "cross_bwd")
    grads_b = [_mm_tn_cols(mn, dkv, "dw_ckv"), _mm_tn_squares([(hn, dqx), (o_x, dh3b)], "dw_cross")]
    names_b = ["w_ckv", "cross"]
    (dh2b, da_mix, db_mix, dgp, dconv, dysb, gs["b_gate"], gs["conv_w"]), landed_ab = _mix_out_bwd(
        dh2, a_mix, b_mix, pcg, b_gate, conv_w, w_mix, "mix_out_bwd", comm=_pair_exchange(grads_a + grads_b))
    sums_ab = _reduce_group(grads_a + grads_b, landed_ab, core, names_a + names_b)
    grads_c = [_mm_tn_squares([(yc, da_mix), (ysb, db_mix), (merged, dh2b)], "dw_mix")]
    flight_ab = _chip_exchange_begin(sums_ab, "grads_to_chips_early_begin")
    (dq, dkv_sb), _ = _sb_bwd(qkv, dysb, ctot, flight_ab.token, "sb_bwd")
    grads_d = [_mm_tn_cols_many(u, [dconv, dq[None], dkv_sb, dgp], "dw_in")]
    (dh1, gs["g_mix"]), landed_cd = _inproj_bwd(dconv, dq, dkv_sb, dgp, w_in, h1, gains["g_mix"], dh2, "inproj_bwd",
                                                comm=_pair_exchange(grads_c + grads_d))
    sums_cd = _reduce_group(grads_c + grads_d, landed_cd, core, ["mix", "w_in"])
    flight_d = _chip_exchange_begin(sums_cd, "grads_to_chips_w_in_begin")
    (dgu1, dh1b, dx, gs["g_ffn1"]), _ = _ffn_bwd(dh1, x, gains["g_ffn1"], gate1, up1, wg1, wd1, "ffn1_bwd",
                                                 comm=_after(flight_d.token))
    dw_gu1 = _mm_tn_rows(dgu1, n1, FF_PAD, "dw_ffn1_gu")
    small_mine = _pack_small({n: gs[n] for n in GAINS + ("b_gate",)}, gs["conv_w"][:CONV_K])
    small_mine = small_mine.at[LOSS_ROW, 0].set(loss[0, 0])
    dw_down1, (landed_gu1,) = _mm_tn_rows(act1, dh1b, FF_BLK, "dw_ffn1_down", comm=_pair_exchange([dw_gu1]))
    flight_gu1 = _chip_exchange_begin(_reduce_group([dw_gu1], [landed_gu1], core, ["w_ffn1_gu"]),
                                      "grads_to_chips_ffn1_gu_begin")
    grads_down1 = [dw_down1.reshape(N_DEV, DOWN_ROWS, D)]
    landed_down1, small_sibling = _run_exchange(
        _both(_both(_pair_exchange(grads_down1), _sibling_swap(small_mine)), _after(flight_gu1.token)),
        "grads_to_sibling_ffn1_down")
    small_chip = small_mine + small_sibling
    small_parts = jnp.broadcast_to(small_chip[None], (N_CHIP, SMALL_R, D))
    flight_down1 = _chip_exchange_begin(
        _reduce_group(grads_down1, [landed_down1], core, ["w_ffn1_down"]) + [small_parts],
        "grads_to_chips_ffn1_down_begin")
    flights = [(names_a + names_b, flight_ab), (["mix", "w_in"], flight_d), (["w_ffn1_gu"], flight_gu1),
               (["w_ffn1_down", "small"], flight_down1)]
    return dx, flights


def kernel(x, mem, g_ffn1, w_ffn1_gu, w_ffn1_down, g_mix, w_in, b_gate, conv_w, w_conv_out, w_attn_out, w_o, g_cross, g_mem, w_cq, w_ckv, w_co, g_ffn2, w_ffn2_gu, w_ffn2_down, g_final, loss_target, m_g_ffn1, m_w_ffn1_gu, m_w_ffn1_down, m_g_mix, m_w_in, m_b_gate, m_conv_w, m_w_conv_out, m_w_attn_out, m_w_o, m_g_cross, m_g_mem, m_w_cq, m_w_ckv, m_w_co, m_g_ffn2, m_w_ffn2_gu, m_w_ffn2_down, m_g_final, v_g_ffn1, v_w_ffn1_gu, v_w_ffn1_down, v_g_mix, v_w_in, v_b_gate, v_conv_w, v_w_conv_out, v_w_attn_out, v_w_o, v_g_cross, v_g_mem, v_w_cq, v_w_ckv, v_w_co, v_g_ffn2, v_w_ffn2_gu, v_w_ffn2_down, v_g_final):
    args = locals()
    wts = {n: args[n] for n in WEIGHT_ORDER}
    mom1 = {n: args["m_" + n] for n in WEIGHT_ORDER}
    mom2 = {n: args["v_" + n] for n in WEIGHT_ORDER}
    cx, cy, cc = _mesh_pos()
    dev = 4 * cx + 2 * cy + cc
    conv_cols = D // N_DEV

    conv_pad = jnp.concatenate([conv_w, jnp.zeros((SMALL_R - CONV_K, conv_cols), F32)], axis=0)
    gains = {n: wts[n].reshape(1, D) for n in GAINS}
    dx, flights = _step(x[0], mem[0], loss_target[0], _exchange_shards(wts), conv_pad, gains,
                                 b_gate.reshape(1, 2 * D), cc.reshape(1).astype(jnp.int32))

    grads, delta, new_m, new_v = {}, {}, {}, {}

    def operands(n, transposed):
        trio = (wts[n], mom1[n], mom2[n])
        return tuple(a.T for a in trio) if transposed else trio

    def record(n, res, transposed):
        grads[n], delta[n], new_m[n], new_v[n] = [r.T for r in res] if transposed else res

    early = [("w_ffn2_gu", "w_ffn2_gu", 0, True), ("w_ffn2_down", "w_ffn2_down", 0, False),
             ("w_ckv", "w_ckv", 0, False), ("w_in", "w_in", 0, False)]
    early += [(n, "mix", k, False) for k, n in enumerate(MIX_MATS)]
    early += [(n, "cross", k, False) for k, n in enumerate(CROSS_MATS)]
    chip = (2 * cx + cy).reshape(1).astype(jnp.int32)
    (names_early, flight_early), (names_w_in, flight_w_in), *last_flights = flights
    token = last_flights[-1][1].token
    own, land = {}, {}
    for names, flight, tag in ((names_early, flight_early, "early"), (names_w_in, flight_w_in, "w_in")):
        own_parts, landed = _chip_exchange_end(flight, token, "grads_to_chips_%s_end" % tag)
        own.update(zip(names, own_parts))
        land.update(zip(names, landed))
    for n, buf, row_block, transposed in early:
        w, m1, m2 = operands(n, transposed)
        *res, token = _adamw_own(w, land[buf], own[buf], chip, m1, m2, "adamw_" + n, row_block, token)
        record(n, res, transposed)

    for (names, flight), transposed in zip(last_flights, (True, False)):
        n = names[0]
        own_parts, landed = _chip_exchange_end(flight, token, "grads_to_chips_%s_end" % n)
        w, m1, m2 = operands(n, transposed)
        *res, token = _adamw_own(w, landed[0], own_parts[0], chip, m1, m2, "adamw_" + n, token=token)
        record(n, res, transposed)

    ids = jnp.stack([dev, 2 * cx + cy]).astype(jnp.int32)
    loss, small = _adamw_small(landed[1], own_parts[1], ids, wts, mom1, mom2, "adamw_small")
    for n in SMALL:
        grads[n], delta[n], new_m[n], new_v[n] = small[n]

    return (loss, dx[None], *[grads[n] for n in WEIGHT_ORDER], *[delta[n] for n in WEIGHT_ORDER],
            *[new_m[n] for n in WEIGHT_ORDER], *[new_v[n] for n in WEIGHT_ORDER])
```

```python
import types

import jax
import jax.numpy as jnp
from jax import lax
from jax.experimental import pallas as pl
from jax.experimental.pallas import tpu as pltpu

F32 = jnp.float32
BF16 = jnp.bfloat16

D = 1024
DFF = 2816
SB_H = 8
SB_DH = 128
X_H = 4
X_DH = 256
CONV_K = 3
RMS_EPS = 1e-6
N_DEV = 8
N_CHIP = 4
SQ_ROWS = D // N_DEV

ADAM_LR = 0.001
ADAM_B1 = 0.9
ADAM_B2 = 0.999
ADAM_EPS = 1e-08
ADAM_WD = 0.01
ADAM_STEP = 10

TM = 256
TQ = 512
TK = 256
SB_HPS = 2
VMEM_LIMIT = 56 << 20

FF_BLK = DFF // 4
FF_PAD = 768
FF_SUB = 256
DOWN_ROWS = DFF // N_DEV

MIX_MATS = ("w_conv_out", "w_attn_out", "w_o")
CROSS_MATS = ("w_cq", "w_co")

_ANY = pl.BlockSpec(memory_space=pl.ANY)


def _cparams(n_axes=1):
    return pltpu.CompilerParams(
        dimension_semantics=("arbitrary",) * n_axes, vmem_limit_bytes=VMEM_LIMIT)


def _row_spec(tm, n):
    return pl.BlockSpec((tm, n), lambda i: (i, 0))


def _blk_row_spec(nb, tm, n):
    return pl.BlockSpec((nb, tm, n), lambda i: (0, i, 0))


def _const_spec(shape):
    zeros = (0,) * len(shape)
    return pl.BlockSpec(shape, lambda i: zeros)


def _dot(a, b):
    return jnp.dot(a, b, preferred_element_type=F32)


def _dot_nt(a, b):
    return lax.dot_general(a, b, (((1,), (1,)), ((), ())), preferred_element_type=F32)


def _dot_tn(a, b):
    return lax.dot_general(a, b, (((0,), (0,)), ((), ())), preferred_element_type=F32)


def _sigmoid(x):
    return 1.0 / (1.0 + jnp.exp(-x))


def _call(body, operands, *, grid, in_specs, out_specs, out_shape, scratch_shapes, name, comm=None):
    n_in, n_out, n_sc = len(in_specs), len(out_specs), len(scratch_shapes)
    if comm is None:
        outs = pl.pallas_call(
            body, grid=grid, name=name, in_specs=in_specs, out_specs=out_specs, out_shape=out_shape,
            scratch_shapes=scratch_shapes, compiler_params=_cparams(len(grid)))(*operands)
        return list(outs), []
    c_in, c_out, c_sem = len(comm.inputs), len(comm.out_shapes), len(comm.sem_shapes)

    def hosted(*refs):
        bounds = [0, n_in, c_in, n_out, c_out, n_sc, c_sem]
        parts, pos = [], 0
        for k in bounds[1:]:
            parts.append(refs[pos:pos + k])
            pos += k
        ins, cins, outs, couts, scr, sems = parts
        step, n_steps = pl.program_id(0), grid[0]
        for ax in range(1, len(grid)):
            step, n_steps = step * grid[ax] + pl.program_id(ax), n_steps * grid[ax]

        @pl.when(step == 0)
        def _():
            comm.start(cins, couts, sems)

        @pl.when(step == (2 * n_steps) // 3)
        def _():
            comm.middle(cins, couts, sems)

        body(*ins, *outs, *scr)

        @pl.when(step == n_steps - 1)
        def _():
            comm.finish(cins, couts, sems)

    res = pl.pallas_call(
        hosted, grid=grid, name=name, in_specs=list(in_specs) + [_ANY] * c_in,
        out_specs=list(out_specs) + [_ANY] * c_out, out_shape=list(out_shape) + list(comm.out_shapes),
        scratch_shapes=list(scratch_shapes) + list(comm.sem_shapes),
        compiler_params=_cparams(len(grid)))(*operands, *comm.inputs)
    return list(res[:n_out]), list(res[n_out:])


def _load_resident(step, pairs, sems):
    @pl.when(step == 0)
    def _():
        copies = [pltpu.make_async_copy(src, dst, sems.at[k]) for k, (src, dst) in enumerate(pairs)]
        for cp in copies:
            cp.start()
        for cp in copies:
            cp.wait()


def _square_pairs(buf_hbm, index, dst):
    off = index * SQ_ROWS
    return [(buf_hbm.at[d, off:off + SQ_ROWS, :], dst.at[d * SQ_ROWS:(d + 1) * SQ_ROWS, :]) for d in range(N_DEV)]


def _down_pairs(wd_hbm, dst):
    return [(wd_hbm.at[d], dst.at[d // 2, (d % 2) * DOWN_ROWS:(d % 2 + 1) * DOWN_ROWS, :]) for d in range(N_DEV)]


def _zero_down_pad(step, dst):
    @pl.when(step == 0)
    def _():
        dst[:, FF_BLK:, :] = jnp.zeros((4, FF_PAD - FF_BLK, D), BF16)


def _rms_fwd_tile(xt, g):
    r = lax.rsqrt(jnp.mean(xt * xt, axis=-1, keepdims=True) + RMS_EPS)
    return (xt * r) * g


def _rms_bwd_tile(xt, g, dn):
    r = lax.rsqrt(jnp.mean(xt * xt, axis=-1, keepdims=True) + RMS_EPS)
    xhat = xt * r
    dxhat = dn * g
    dx = r * (dxhat - xhat * jnp.mean(dxhat * xhat, axis=-1, keepdims=True))
    dg = jnp.sum(dn * xhat, axis=0, keepdims=True)
    return dx, dg


def _accumulate(ref, step, value):
    @pl.when(step == 0)
    def _():
        ref[...] = value

    @pl.when(step != 0)
    def _():
        ref[...] = ref[...] + value


def _ffn_fwd(x, g, wgu, wd, name, comm=None, head=None):
    t = x.shape[0]

    def body(x_ref, g_ref, wgu_hbm, wd_hbm, *refs):
        if head is None:
            n_ref, gate_ref, up_ref, act_ref, h_ref, wgu_v, wd_v, sems = refs
        else:
            gf_ref, t_ref, n_ref, gate_ref, up_ref, act_ref, dh_ref, loss_ref, dgf_ref, wgu_v, wd_v, sems = refs
        step = pl.program_id(0)
        _zero_down_pad(step, wd_v)
        _load_resident(step, [(wgu_hbm, wgu_v)] + _down_pairs(wd_hbm, wd_v), sems)
        xt = x_ref[...]
        n = _rms_fwd_tile(xt, g_ref[...]).astype(BF16)
        n_ref[...] = n
        acc = jnp.zeros((TM, D), F32)
        for j in range(4):
            for s in range(FF_PAD // FF_SUB):
                lo, hi = s * FF_SUB, (s + 1) * FF_SUB
                gt = _dot_nt(n, wgu_v[j, lo:hi, :])
                ut = _dot_nt(n, wgu_v[4 + j, lo:hi, :])
                gate_ref[j, :, lo:hi] = gt.astype(BF16)
                up_ref[j, :, lo:hi] = ut.astype(BF16)
                act_ref[j, :, lo:hi] = ((gt * _sigmoid(gt)) * ut).astype(BF16)
            acc = acc + _dot(act_ref[j], wd_v[j])
        ht = xt + 0.5 * acc
        if head is None:
            h_ref[...] = ht
        else:
            gain = gf_ref[...]
            diff = _rms_fwd_tile(ht, gain) - t_ref[...]
            part = 0.5 * jnp.sum(jnp.sum(diff * diff, axis=-1, keepdims=True) / D, axis=0, keepdims=True)
            dx, dg = _rms_bwd_tile(ht, gain, diff / D)
            dh_ref[...] = dx
            _accumulate(loss_ref, step, jnp.broadcast_to(part, (8, 128)))
            _accumulate(dgf_ref, step, dg)

    ff = jax.ShapeDtypeStruct((4, t, FF_PAD), BF16)
    operands, in_specs = (x, g, wgu, wd), [_row_spec(TM, D), _const_spec((1, D)), _ANY, _ANY]
    out_specs = [_row_spec(TM, D)] + [_blk_row_spec(4, TM, FF_PAD)] * 3 + [_row_spec(TM, D)]
    out_shape = [jax.ShapeDtypeStruct((t, D), BF16), ff, ff, ff, jax.ShapeDtypeStruct((t, D), F32)]
    if head is not None:
        operands += tuple(head)
        in_specs += [_const_spec((1, D)), _row_spec(TM, D)]
        out_specs += [_const_spec((8, 128)), _const_spec((1, D))]
        out_shape += [jax.ShapeDtypeStruct((8, 128), F32), jax.ShapeDtypeStruct((1, D), F32)]
    return _call(
        body, operands, grid=(t // TM,), name=name, comm=comm, in_specs=in_specs, out_specs=out_specs,
        out_shape=out_shape,
        scratch_shapes=[pltpu.VMEM((N_DEV, FF_PAD, D), BF16), pltpu.VMEM((4, FF_PAD, D), BF16),
                        pltpu.SemaphoreType.DMA((1 + N_DEV,))])


def _ffn_bwd(dh, xin, g, gate, up, wgu, wd, name, comm=None):
    t = dh.shape[0]

    def body(dh_ref, x_ref, g_ref, gate_ref, up_ref, wgu_hbm, wd_hbm,
             dgu_ref, dhb_ref, dx_ref, dg_ref, wgu_v, wd_v, sems):
        step = pl.program_id(0)
        _zero_down_pad(step, wd_v)
        _load_resident(step, [(wgu_hbm, wgu_v)] + _down_pairs(wd_hbm, wd_v), sems)
        dht = dh_ref[...]
        dhb = (0.5 * dht).astype(BF16)
        dhb_ref[...] = dhb
        dn = jnp.zeros((TM, D), F32)
        for j in range(4):
            for s in range(FF_PAD // FF_SUB):
                lo, hi = s * FF_SUB, (s + 1) * FF_SUB
                da = _dot_nt(dhb, wd_v[j, lo:hi, :])
                gt = gate_ref[j, :, lo:hi].astype(F32)
                ut = up_ref[j, :, lo:hi].astype(F32)
                sg = _sigmoid(gt)
                dgt = (da * ut * (sg * (1.0 + gt * (1.0 - sg)))).astype(BF16)
                dut = (da * (gt * sg)).astype(BF16)
                dgu_ref[j, :, lo:hi] = dgt
                dgu_ref[4 + j, :, lo:hi] = dut
            dn = dn + _dot(dgu_ref[j], wgu_v[j]) + _dot(dgu_ref[4 + j], wgu_v[4 + j])
        dx, dg = _rms_bwd_tile(x_ref[...], g_ref[...], dn)
        dx_ref[...] = dht + dx
        _accumulate(dg_ref, step, dg)

    return _call(
        body, (dh, xin, g, gate, up, wgu, wd), grid=(t // TM,), name=name, comm=comm,
        in_specs=[_row_spec(TM, D), _row_spec(TM, D), _const_spec((1, D)), _blk_row_spec(4, TM, FF_PAD),
                  _blk_row_spec(4, TM, FF_PAD), _ANY, _ANY],
        out_specs=[_blk_row_spec(N_DEV, TM, FF_PAD), _row_spec(TM, D), _row_spec(TM, D), _const_spec((1, D))],
        out_shape=[jax.ShapeDtypeStruct((N_DEV, t, FF_PAD), BF16), jax.ShapeDtypeStruct((t, D), BF16),
                   jax.ShapeDtypeStruct((t, D), F32), jax.ShapeDtypeStruct((1, D), F32)],
        scratch_shapes=[pltpu.VMEM((N_DEV, FF_PAD, D), BF16), pltpu.VMEM((4, FF_PAD, D), BF16),
                        pltpu.SemaphoreType.DMA((1 + N_DEV,))])


WIDE_TILES = (1024, 512, 256, 128)


def _pick_tile(n, options=(512, 256, 128)):
    for o in options:
        if n % o == 0:
            return o
    return n


def _mm_tn_squares(pairs, name):
    k, d = pairs[0][0].shape
    n = pairs[0][1].shape[1]
    tn = _pick_tile(n)
    count = len(pairs)

    def body(*refs):
        m = pl.program_id(0)
        for idx in range(count):
            @pl.when(m == idx)
            def _(idx=idx):
                refs[-1][...] = _dot_tn(refs[idx][...], refs[count + idx][...]).astype(BF16).reshape(
                    N_DEV, d // N_DEV, tn)

    a_specs = [pl.BlockSpec((k, d), lambda m, j: (0, 0)) for _ in pairs]
    b_specs = [pl.BlockSpec((k, tn), lambda m, j, idx=idx: (0, jnp.where(m == idx, j, 0))) for idx in range(count)]
    return pl.pallas_call(
        body, grid=(count, n // tn), name=name, in_specs=a_specs + b_specs,
        out_specs=pl.BlockSpec((N_DEV, d // N_DEV, tn), lambda m, j: (0, m, j)),
        out_shape=jax.ShapeDtypeStruct((N_DEV, count * (d // N_DEV), n), BF16),
        compiler_params=_cparams(2),
    )(*[a for a, _ in pairs], *[b for _, b in pairs])


def _mm_tn_cols_many(a, parts, name):
    k, m = a.shape
    n = parts[0].shape[2]
    tn = _pick_tile(n)
    firsts, total = [], 0
    for p in parts:
        firsts.append(total)
        total += p.shape[0]

    def body(a_ref, *refs):
        j = pl.program_id(0)
        for p, first, ref in zip(parts, firsts, refs):
            @pl.when(jnp.logical_and(j >= first, j < first + p.shape[0]))
            def _(ref=ref):
                refs[-1][0] = _dot_tn(a_ref[...].astype(BF16), ref[0].astype(BF16)).astype(BF16)

    specs = [pl.BlockSpec((1, k, tn), lambda j, i, first=first, last=p.shape[0] - 1:
                          (jnp.clip(j - first, 0, last), 0, jnp.where(jnp.logical_and(j >= first, j <= first + last), i, 0)))
             for p, first in zip(parts, firsts)]
    return pl.pallas_call(
        body, grid=(total, n // tn), name=name,
        in_specs=[pl.BlockSpec((k, m), lambda j, i: (0, 0))] + specs,
        out_specs=pl.BlockSpec((1, m, tn), lambda j, i: (j, 0, i)),
        out_shape=jax.ShapeDtypeStruct((total, m, n), BF16),
        compiler_params=_cparams(2),
    )(a, *parts)


def _mm_tn_cols(a, b, name):
    k, m = a.shape
    nb, _, n = b.shape
    tm = _pick_tile(m, WIDE_TILES)

    def body(a_ref, b_ref, o_ref):
        o_ref[0] = _dot_tn(a_ref[...].astype(BF16), b_ref[0].astype(BF16)).astype(BF16)

    return pl.pallas_call(
        body, grid=(nb, m // tm), name=name,
        in_specs=[pl.BlockSpec((k, tm), lambda j, i: (0, i)), pl.BlockSpec((1, k, n), lambda j, i: (j, 0, 0))],
        out_specs=pl.BlockSpec((1, tm, n), lambda j, i: (j, i, 0)),
        out_shape=jax.ShapeDtypeStruct((nb, m, n), BF16),
        compiler_params=_cparams(2),
    )(a, b)


def _mm_tn_rows(a, b, keep, name, comm=None):
    nb, k, m = a.shape
    _, n = b.shape
    tn = _pick_tile(n, WIDE_TILES)

    def body(a_ref, b_ref, o_ref):
        o_ref[0] = _dot_tn(a_ref[0], b_ref[...])[:keep].astype(BF16)

    (out,), couts = _call(
        body, (a, b), grid=(nb, n // tn), name=name, comm=comm,
        in_specs=[pl.BlockSpec((1, k, m), lambda j, i: (j, 0, 0)), pl.BlockSpec((k, tn), lambda j, i: (0, i))],
        out_specs=[pl.BlockSpec((1, keep, tn), lambda j, i: (j, 0, i))],
        out_shape=[jax.ShapeDtypeStruct((nb, keep, n), BF16)], scratch_shapes=[])
    return out if comm is None else (out, couts)


PCG_W = 5 * D
QKV_W = 3 * D
PROJ_SUB = 512


def _inproj_fwd(h, g, w_in, conv_w, name, comm=None):
    t = h.shape[0]

    def body(h_ref, g_ref, w_hbm, cw_ref, u_ref, pcg_ref, qkv_ref, yc_ref, w_v, tail_v, sems):
        step = pl.program_id(0)
        _load_resident(step, [(w_hbm, w_v)], sems)

        @pl.when(step == 0)
        def _():
            tail_v[...] = jnp.zeros_like(tail_v)

        u = _rms_fwd_tile(h_ref[...], g_ref[...]).astype(BF16)
        u_ref[...] = u
        for blk in range(N_DEV):
            for s in range(D // PROJ_SUB):
                lo, hi = s * PROJ_SUB, (s + 1) * PROJ_SUB
                p = _dot(u, w_v[blk, :, lo:hi])
                if blk < 3:
                    pcg_ref[:, blk * D + lo:blk * D + hi] = p
                elif blk < 6:
                    qkv_ref[:, (blk - 3) * D + lo:(blk - 3) * D + hi] = p.astype(BF16)
                else:
                    pcg_ref[:, (blk - 3) * D + lo:(blk - 3) * D + hi] = p
        xc = pcg_ref[:, D:2 * D] * pcg_ref[:, 2 * D:3 * D]
        ext = jnp.concatenate([tail_v[...], xc], axis=0)
        conv = (cw_ref[0:1, :] * pltpu.roll(ext, 2, 0)[8:] + cw_ref[1:2, :] * pltpu.roll(ext, 1, 0)[8:]
                + cw_ref[2:3, :] * xc)
        yc_ref[...] = (pcg_ref[:, 0:D] * conv).astype(BF16)
        tail_v[...] = xc[TM - 8:]

    return _call(
        body, (h, g, w_in, conv_w), grid=(t // TM,), name=name, comm=comm,
        in_specs=[_row_spec(TM, D), _const_spec((1, D)), _ANY, _const_spec((CONV_K, D))],
        out_specs=[_row_spec(TM, D), _row_spec(TM, PCG_W), _row_spec(TM, QKV_W), _row_spec(TM, D)],
        out_shape=[jax.ShapeDtypeStruct((t, D), BF16), jax.ShapeDtypeStruct((t, PCG_W), F32),
                   jax.ShapeDtypeStruct((t, QKV_W), BF16), jax.ShapeDtypeStruct((t, D), BF16)],
        scratch_shapes=[pltpu.VMEM((N_DEV, D, D), BF16), pltpu.VMEM((8, D), F32), pltpu.SemaphoreType.DMA((1,))])


def _tri2(cond):
    rr = lax.broadcasted_iota(jnp.int32, (2 * TK, TK), 0) & (TK - 1)
    cc = lax.broadcasted_iota(jnp.int32, (2 * TK, TK), 1)
    return cond(rr, cc).astype(BF16)


def _causal(shift, row0=0):
    rr = lax.broadcasted_iota(jnp.int32, (TQ - row0, TK), 0) + row0
    cc = lax.broadcasted_iota(jnp.int32, (TQ - row0, TK), 1)
    return cc + shift < rr


def _cumdot(v, tri2):
    hi = v.astype(BF16)
    lo = (v - hi.astype(F32)).astype(BF16)
    return _dot(jnp.concatenate([hi, lo], axis=1), tri2)


LOG2_E = 1.4426950408889634


def _log_1m_beta(z):
    return -(jnp.maximum(z, 0.0) + jnp.log2(1.0 + jnp.exp2(-jnp.abs(z))))


def _sb_specs(t):
    g = SB_H // SB_HPS
    w = SB_HPS * SB_DH
    q_spec = pl.BlockSpec((TQ, w), lambda h, i: (i, h))
    k_spec = pl.BlockSpec((t, w), lambda h, i: (0, g + h))
    v_spec = pl.BlockSpec((t, w), lambda h, i: (0, 2 * g + h))
    ct_spec = pl.BlockSpec((SB_HPS, TQ, 1), lambda h, i: (h, i, 0))
    return g, w, q_spec, k_spec, v_spec, ct_spec


def _sb_fwd(qkv, name, comm=None):
    t = qkv.shape[0]
    scale = SB_DH ** -0.5
    g, w, q_spec, k_spec, v_spec, ct_spec = _sb_specs(t)

    def body(q_ref, k_ref, v_ref, y_ref, ct_ref):
        i = pl.program_id(1)
        later = _tri2(lambda j, s: j > s)
        n_diag = TQ // TK

        def block(j, carry, shift):
            off = pl.multiple_of(j * TK, TK)
            zs, ms = [], []
            for hd in range(SB_HPS):
                cols = slice(hd * SB_DH, (hd + 1) * SB_DH)
                z = _dot_nt(q_ref[:, cols], k_ref[pl.ds(off, TK), cols]) * (scale * LOG2_E)
                m = _log_1m_beta(z)
                if shift is not None:
                    m = jnp.where(_causal(shift), m, 0.0)
                zs.append(z)
                ms.append(m)
            after = _cumdot(jnp.concatenate(ms, axis=0), later)
            out = []
            for hd in range(SB_HPS):
                acc, c_sum = carry[hd]
                cols = slice(hd * SB_DH, (hd + 1) * SB_DH)
                a = jnp.exp2((ms[hd] + zs[hd]) + (c_sum + after[hd * TQ:(hd + 1) * TQ]))
                if shift is not None:
                    a = jnp.where(_causal(shift), a, 0.0)
                out.append((acc + _dot(a.astype(BF16), v_ref[pl.ds(off, TK), cols]),
                            c_sum + jnp.sum(ms[hd], axis=1, keepdims=True)))
            return tuple(out)

        carry = tuple((jnp.zeros((TQ, SB_DH), F32), jnp.zeros((TQ, 1), F32)) for _ in range(SB_HPS))
        for d in reversed(range(n_diag)):
            carry = block(i * n_diag + d, carry, d * TK)
        carry = lax.fori_loop(0, i * n_diag, lambda jj, c: block(i * n_diag - 1 - jj, c, None), carry)
        for hd in range(SB_HPS):
            y_ref[:, hd * SB_DH:(hd + 1) * SB_DH] = carry[hd][0].astype(BF16)
            ct_ref[hd] = carry[hd][1]

    return _call(
        body, (qkv, qkv, qkv), grid=(g, t // TQ), name=name, comm=comm,
        in_specs=[q_spec, k_spec, v_spec],
        out_specs=[q_spec, ct_spec],
        out_shape=[jax.ShapeDtypeStruct((t, D), BF16), jax.ShapeDtypeStruct((SB_H, t, 1), F32)],
        scratch_shapes=[])


def _sb_bwd(qkv, dy, ctot, after, name, comm=None):
    t = qkv.shape[0]
    scale = SB_DH ** -0.5
    g, w, q_spec, k_spec, v_spec, ct_spec = _sb_specs(t)
    acc_spec = pl.BlockSpec((2, t, w), lambda h, i: (0, 0, h))

    def body(q_ref, k_ref, v_ref, dy_ref, ct_ref, after_ref, dq_ref, dkv_ref):
        i = pl.program_id(1)

        @pl.when(i == 0)
        def _():
            dkv_ref[...] = jnp.zeros_like(dkv_ref)

        upto = _tri2(lambda j, s: j <= s)
        n_diag = TQ // TK

        def block(j, carry, shift):
            off = pl.multiple_of(j * TK, TK)
            r0 = 0 if shift is None else shift
            nr = TQ - r0
            causal = None if shift is None else _causal(shift, r0)

            def grow(old, delta):
                return old + delta if r0 == 0 else jnp.concatenate([old[:r0], old[r0:] + delta], axis=0)

            zs, ms = [], []
            for hd in range(SB_HPS):
                cols = slice(hd * SB_DH, (hd + 1) * SB_DH)
                z = _dot_nt(q_ref[r0:, cols], k_ref[pl.ds(off, TK), cols]) * (scale * LOG2_E)
                m = _log_1m_beta(z)
                if causal is not None:
                    m = jnp.where(causal, m, 0.0)
                zs.append(z)
                ms.append(m)
            m_upto = _cumdot(jnp.concatenate(ms, axis=0), upto)
            ls, a_s, es = [], [], []
            for hd in range(SB_HPS):
                cols = slice(hd * SB_DH, (hd + 1) * SB_DH)
                l = ms[hd] + zs[hd]
                a = jnp.exp2(l + ((ct_ref[hd, r0:] - carry[hd][1][r0:]) - m_upto[hd * nr:(hd + 1) * nr]))
                if causal is not None:
                    a = jnp.where(causal, a, 0.0)
                ls.append(l)
                a_s.append(a)
                es.append(_dot_nt(dy_ref[r0:, cols], v_ref[pl.ds(off, TK), cols]) * a)
            e_upto = _dot(jnp.concatenate(es, axis=0).astype(BF16), upto[:TK])
            out = []
            for hd in range(SB_HPS):
                dq, p_sum, e_sum = carry[hd]
                cols = slice(hd * SB_DH, (hd + 1) * SB_DH)
                e = es[hd]
                dz = e - jnp.exp2(ls[hd]) * (e_sum[r0:] + e_upto[hd * nr:(hd + 1) * nr])
                if causal is not None:
                    dz = jnp.where(causal, dz, 0.0)
                dzs = (dz * scale).astype(BF16)
                dkv_ref[0, pl.ds(off, TK), cols] += _dot_tn(dzs, q_ref[r0:, cols])
                dkv_ref[1, pl.ds(off, TK), cols] += _dot_tn(a_s[hd].astype(BF16), dy_ref[r0:, cols])
                out.append((grow(dq, _dot(dzs, k_ref[pl.ds(off, TK), cols])),
                            grow(p_sum, jnp.sum(ms[hd], axis=1, keepdims=True)),
                            grow(e_sum, jnp.sum(e, axis=1, keepdims=True))))
            return tuple(out)

        zero = jnp.zeros((TQ, 1), F32)
        init = tuple((jnp.zeros((TQ, SB_DH), F32), zero, zero) for _ in range(SB_HPS))
        carry = lax.fori_loop(0, i * n_diag, lambda j, c: block(j, c, None), init)
        for d in range(n_diag):
            carry = block(i * n_diag + d, carry, d * TK)
        for hd in range(SB_HPS):
            dq_ref[:, hd * SB_DH:(hd + 1) * SB_DH] = carry[hd][0].astype(BF16)

    return _call(
        body, (qkv, qkv, qkv, dy, ctot, after), grid=(g, t // TQ), name=name, comm=comm,
        in_specs=[q_spec, k_spec, v_spec, q_spec, ct_spec, pl.BlockSpec(after.shape, lambda h, i: (0, 0))],
        out_specs=[q_spec, acc_spec],
        out_shape=[jax.ShapeDtypeStruct((t, D), BF16), jax.ShapeDtypeStruct((2, t, D), F32)],
        scratch_shapes=[])


def _gate_specs():
    return [pl.BlockSpec((TM, D), lambda i: (i, 3)), pl.BlockSpec((TM, D), lambda i: (i, 4))]


def _mix_pairs(mix_hbm, dsts):
    pairs = []
    for index, dst in enumerate(dsts):
        pairs += _square_pairs(mix_hbm, index, dst)
    return pairs


def _mix_out_fwd(yc, ysb, pcg, b_gate, h, w_mix, name, comm=None):
    t = h.shape[0]

    def body(yc_ref, ysb_ref, gc_ref, gs_ref, b_ref, h_ref, mix_hbm,
             a_ref, b_out_ref, mg_ref, h2_ref, wc_v, wa_v, wo_v, sems):
        _load_resident(pl.program_id(0), _mix_pairs(mix_hbm, (wc_v, wa_v, wo_v)), sems)
        a = _dot(yc_ref[...], wc_v[...])
        b = _dot(ysb_ref[...], wa_v[...])
        merged = (_sigmoid(gc_ref[...] + b_ref[:, :D]) * a + _sigmoid(gs_ref[...] + b_ref[:, D:]) * b).astype(BF16)
        a_ref[...] = a
        b_out_ref[...] = b
        mg_ref[...] = merged
        h2_ref[...] = h_ref[...] + _dot(merged, wo_v[...])

    return _call(
        body, (yc, ysb, pcg, pcg, b_gate, h, w_mix), grid=(t // TM,), name=name, comm=comm,
        in_specs=[_row_spec(TM, D), _row_spec(TM, D)] + _gate_specs()
                 + [_const_spec((1, 2 * D)), _row_spec(TM, D), _ANY],
        out_specs=[_row_spec(TM, D)] * 4,
        out_shape=[jax.ShapeDtypeStruct((t, D), F32), jax.ShapeDtypeStruct((t, D), F32),
                   jax.ShapeDtypeStruct((t, D), BF16), jax.ShapeDtypeStruct((t, D), F32)],
        scratch_shapes=[pltpu.VMEM((D, D), BF16)] * 3 + [pltpu.SemaphoreType.DMA((3 * N_DEV,))])


def _mix_out_bwd(dh2, a, b, pcg, b_gate, conv_w, w_mix, name, comm=None):
    t = dh2.shape[0]
    n_tile = t // TM
    per8 = TM // 8

    def rows(n):
        return pl.BlockSpec((TM, n), lambda i: (n_tile - 1 - i, 0))

    def cols(block):
        return pl.BlockSpec((TM, D), lambda i: (n_tile - 1 - i, block))

    def before(block):
        return pl.BlockSpec((8, D), lambda i: (jnp.maximum((n_tile - 1 - i) * per8 - 1, 0), block))

    def body(dh_ref, a_ref, b_ref, gc_ref, gs_ref, cb_ref, cc_ref, cx_ref, ccp_ref, cxp_ref, bias_ref, cw_ref, mix_hbm,
             dhb_ref, da_ref, db_ref, dgp_ref, dc_ref, dysb_ref, dbias_ref, dcw_ref, wc_v, wa_v, wo_v, head_v, sems):
        step = pl.program_id(0)
        _load_resident(step, _mix_pairs(mix_hbm, (wc_v, wa_v, wo_v)), sems)

        @pl.when(step == 0)
        def _():
            head_v[...] = jnp.zeros_like(head_v)
            dcw_ref[...] = jnp.zeros_like(dcw_ref)

        dhb = dh_ref[...].astype(BF16)
        dhb_ref[...] = dhb
        dm = _dot_nt(dhb, wo_v[...])
        gc = _sigmoid(gc_ref[...] + bias_ref[:, :D])
        gs = _sigmoid(gs_ref[...] + bias_ref[:, D:])
        da = (dm * gc).astype(BF16)
        db = (dm * gs).astype(BF16)
        da_ref[...] = da
        db_ref[...] = db
        dgc = dm * a_ref[...] * (gc * (1.0 - gc))
        dgs = dm * b_ref[...] * (gs * (1.0 - gs))
        dgp_ref[0] = dgc.astype(BF16)
        dgp_ref[1] = dgs.astype(BF16)
        _accumulate(dbias_ref.at[:, :D], step, jnp.sum(dgc, axis=0, keepdims=True))
        _accumulate(dbias_ref.at[:, D:], step, jnp.sum(dgs, axis=0, keepdims=True))
        dysb_ref[...] = _dot_nt(db, wa_v[...]).astype(BF16)
        dyc = _dot_nt(da, wc_v[...])
        cc, cx = cc_ref[...], cx_ref[...]
        xc = cc * cx
        xc_before = jnp.where(step == n_tile - 1, 0.0, ccp_ref[...] * cxp_ref[...])
        ext = jnp.concatenate([xc_before, xc], axis=0)
        x1 = pltpu.roll(ext, 1, 0)[8:]
        x2 = pltpu.roll(ext, 2, 0)[8:]
        w0, w1, w2 = cw_ref[0:1, :], cw_ref[1:2, :], cw_ref[2:3, :]
        dc_ref[0] = (dyc * (w0 * x2 + w1 * x1 + w2 * xc)).astype(BF16)
        dconv = dyc * cb_ref[...]
        dcw_ref[0:1, :] += jnp.sum(dconv * x2, axis=0, keepdims=True)
        dcw_ref[1:2, :] += jnp.sum(dconv * x1, axis=0, keepdims=True)
        dcw_ref[2:3, :] += jnp.sum(dconv * xc, axis=0, keepdims=True)
        after = jnp.concatenate([dconv, head_v[...]], axis=0)
        dxc = w2 * dconv + w1 * pltpu.roll(after, TM + 7, 0)[:TM] + w0 * pltpu.roll(after, TM + 6, 0)[:TM]
        dc_ref[1] = (dxc * cx).astype(BF16)
        dc_ref[2] = (dxc * cc).astype(BF16)
        head_v[...] = dconv[:8]

    return _call(
        body, (dh2, a, b, pcg, pcg, pcg, pcg, pcg, pcg, pcg, b_gate, conv_w, w_mix), grid=(n_tile,), name=name,
        comm=comm,
        in_specs=[rows(D)] * 3 + [cols(3), cols(4), cols(0), cols(1), cols(2), before(1), before(2),
                                  _const_spec((1, 2 * D)), _const_spec((CONV_K, D)), _ANY],
        out_specs=[rows(D)] * 3 + [pl.BlockSpec((2, TM, D), lambda i: (0, n_tile - 1 - i, 0)),
                                   pl.BlockSpec((3, TM, D), lambda i: (0, n_tile - 1 - i, 0)), rows(D),
                                   _const_spec((1, 2 * D)), _const_spec((8, D))],
        out_shape=[jax.ShapeDtypeStruct((t, D), BF16)] * 3
                  + [jax.ShapeDtypeStruct((2, t, D), BF16), jax.ShapeDtypeStruct((3, t, D), BF16),
                     jax.ShapeDtypeStruct((t, D), BF16), jax.ShapeDtypeStruct((1, 2 * D), F32),
                     jax.ShapeDtypeStruct((8, D), F32)],
        scratch_shapes=[pltpu.VMEM((D, D), BF16)] * 3 + [pltpu.VMEM((8, D), F32),
                                                         pltpu.SemaphoreType.DMA((3 * N_DEV,))])


def _inproj_bwd(dconv, dq, dkv, dgp, w_in, h, g, dh_res, name, comm=None):
    t = h.shape[0]

    def body(dc_ref, dq_ref, dkv_ref, dgp_ref, w_hbm, h_ref, g_ref, dres_ref, dh_ref, dg_ref, w_v, sems):
        step = pl.program_id(0)
        _load_resident(step, [(w_hbm, w_v)], sems)
        du = _dot_nt(dq_ref[...], w_v[3])
        for k in range(3):
            du = du + _dot_nt(dc_ref[k], w_v[k])
        for k in range(2):
            du = du + _dot_nt(dkv_ref[k].astype(BF16), w_v[4 + k]) + _dot_nt(dgp_ref[k], w_v[6 + k])
        dx, dg = _rms_bwd_tile(h_ref[...], g_ref[...], du)
        dh_ref[...] = dres_ref[...] + dx
        _accumulate(dg_ref, step, dg)

    return _call(
        body, (dconv, dq, dkv, dgp, w_in, h, g, dh_res), grid=(t // TM,), name=name, comm=comm,
        in_specs=[_blk_row_spec(3, TM, D), _row_spec(TM, D), _blk_row_spec(2, TM, D), _blk_row_spec(2, TM, D), _ANY,
                  _row_spec(TM, D), _const_spec((1, D)), _row_spec(TM, D)],
        out_specs=[_row_spec(TM, D), _const_spec((1, D))],
        out_shape=[jax.ShapeDtypeStruct((t, D), F32), jax.ShapeDtypeStruct((1, D), F32)],
        scratch_shapes=[pltpu.VMEM((N_DEV, D, D), BF16), pltpu.SemaphoreType.DMA((1,))])


def _softmax_rows(s):
    e = jnp.exp(s - jnp.max(s, axis=-1, keepdims=True))
    return e / jnp.sum(e, axis=-1, keepdims=True)


def _cross_pairs(cross_hbm, wq_v, wo_v):
    return _square_pairs(cross_hbm, 0, wq_v) + _square_pairs(cross_hbm, 1, wo_v)


def _cross_fwd(h, g, mem, g_mem, w_ckv, w_cross, name):
    t = h.shape[0]
    m = mem.shape[0]
    scale = X_DH ** -0.5

    def body(h_ref, g_ref, mem_ref, gm_ref, wkv_ref, cross_hbm, hn_ref, qx_ref, o_ref, h3_ref, mn_ref, kv_ref,
             wq_v, wo_v, sems):
        _load_resident(pl.program_id(0), _cross_pairs(cross_hbm, wq_v, wo_v), sems)

        @pl.when(pl.program_id(0) == 0)
        def _():
            mn = _rms_fwd_tile(mem_ref[...], gm_ref[...]).astype(BF16)
            mn_ref[...] = mn
            for j in range(N_DEV):
                kv_ref[j] = _dot(mn, wkv_ref[j]).astype(BF16)

        ht = h_ref[...]
        hn = _rms_fwd_tile(ht, g_ref[...]).astype(BF16)
        hn_ref[...] = hn
        qx = _dot(hn, wq_v[...]).astype(BF16)
        qx_ref[...] = qx
        for hd in range(X_H):
            lo, hi = hd * X_DH, (hd + 1) * X_DH
            p = _softmax_rows(_dot_nt(qx[:, lo:hi], kv_ref[hd]) * scale)
            o_ref[:, lo:hi] = _dot(p.astype(BF16), kv_ref[X_H + hd]).astype(BF16)
        h3_ref[...] = ht + _dot(o_ref[...], wo_v[...])

    return pl.pallas_call(
        body, grid=(t // TM,), name=name,
        in_specs=[_row_spec(TM, D), _const_spec((1, D)), _const_spec((m, D)), _const_spec((1, D)),
                  _const_spec((N_DEV, D, X_DH)), _ANY],
        out_specs=[_row_spec(TM, D)] * 4 + [_const_spec((m, D)), _const_spec((N_DEV, m, X_DH))],
        out_shape=[jax.ShapeDtypeStruct((t, D), BF16)] * 3 + [jax.ShapeDtypeStruct((t, D), F32),
                                                              jax.ShapeDtypeStruct((m, D), BF16),
                                                              jax.ShapeDtypeStruct((N_DEV, m, X_DH), BF16)],
        scratch_shapes=[pltpu.VMEM((D, D), BF16)] * 2 + [pltpu.SemaphoreType.DMA((2 * N_DEV,))],
        compiler_params=_cparams(),
    )(h, g, mem, g_mem, w_ckv, w_cross)


def _cross_bwd(dh3, h, g, qx, kv, mem, g_mem, w_ckv, w_cross, name, comm=None):
    t = h.shape[0]
    m = kv.shape[1]
    scale = X_DH ** -0.5

    def body(dh_ref, h_ref, g_ref, qx_ref, kv_ref, mem_ref, gm_ref, wkv_ref, cross_hbm,
             dhb_ref, dqx_ref, dkv_ref, dh2_ref, dg_ref, dgm_ref, wq_v, wo_v, sems):
        step = pl.program_id(0)
        _load_resident(step, _cross_pairs(cross_hbm, wq_v, wo_v), sems)

        @pl.when(step == 0)
        def _():
            dkv_ref[...] = jnp.zeros_like(dkv_ref)

        dht = dh_ref[...]
        dhb = dht.astype(BF16)
        dhb_ref[...] = dhb
        do = _dot_nt(dhb, wo_v[...]).astype(BF16)
        for hd in range(X_H):
            lo, hi = hd * X_DH, (hd + 1) * X_DH
            qh = qx_ref[:, lo:hi]
            kh = kv_ref[hd]
            p = _softmax_rows(_dot_nt(qh, kh) * scale)
            doh = do[:, lo:hi]
            dp = _dot_nt(doh, kv_ref[X_H + hd])
            ds = (p * (dp - jnp.sum(dp * p, axis=-1, keepdims=True)) * scale).astype(BF16)
            dqx_ref[:, lo:hi] = _dot(ds, kh).astype(BF16)
            dkv_ref[hd] += _dot_tn(ds, qh)
            dkv_ref[X_H + hd] += _dot_tn(p.astype(BF16), doh)
        dhn = _dot_nt(dqx_ref[...], wq_v[...])
        dx, dg = _rms_bwd_tile(h_ref[...], g_ref[...], dhn)
        dh2_ref[...] = dht + dx
        _accumulate(dg_ref, step, dg)

        @pl.when(step == t // TM - 1)
        def _():
            dmn = jnp.zeros((m, D), F32)
            for j in range(N_DEV):
                dmn = dmn + _dot_nt(dkv_ref[j].astype(BF16), wkv_ref[j])
            dgm_ref[...] = _rms_bwd_tile(mem_ref[...], gm_ref[...], dmn)[1]

    return _call(
        body, (dh3, h, g, qx, kv, mem, g_mem, w_ckv, w_cross), grid=(t // TM,), name=name, comm=comm,
        in_specs=[_row_spec(TM, D), _row_spec(TM, D), _const_spec((1, D)), _row_spec(TM, D),
                  _const_spec((N_DEV, m, X_DH)), _const_spec((m, D)), _const_spec((1, D)),
                  _const_spec((N_DEV, D, X_DH)), _ANY],
        out_specs=[_row_spec(TM, D), _row_spec(TM, D), _const_spec((N_DEV, m, X_DH)), _row_spec(TM, D),
                   _const_spec((1, D)), _const_spec((1, D))],
        out_shape=[jax.ShapeDtypeStruct((t, D), BF16), jax.ShapeDtypeStruct((t, D), BF16),
                   jax.ShapeDtypeStruct((N_DEV, m, X_DH), F32), jax.ShapeDtypeStruct((t, D), F32),
                   jax.ShapeDtypeStruct((1, D), F32), jax.ShapeDtypeStruct((1, D), F32)],
        scratch_shapes=[pltpu.VMEM((D, D), BF16)] * 2 + [pltpu.SemaphoreType.DMA((2 * N_DEV,))])


def _adamw_small(land, own, ids, wts, mom1, mom2, name):
    n_slots = land.shape[0]
    conv_rows, conv_cols = wts["conv_w"].shape
    shapes = {n: (1, wts[n].size) for n in GAINS + ("b_gate",)}
    shapes["conv_w"] = (conv_rows, conv_cols)
    n_par = len(SMALL)
    gate_row, conv_row = len(GAINS), len(GAINS) + 2

    def body(ids_ref, land_ref, own_ref, land_cols, own_cols, *refs):
        params, loss_ref, outs, acc = refs[:3 * n_par], refs[3 * n_par], refs[3 * n_par + 1:-1], refs[-1]

        def chips_sum(theirs, mine, rows):
            total = jnp.where(ids_ref[1] == 0, mine[0, rows, :], theirs[0, rows, :])
            for k in range(1, n_slots):
                total = total + jnp.where(ids_ref[1] == k, mine[0, rows, :], theirs[k, rows, :])
            return total

        acc[...] = chips_sum(land_ref, own_ref, slice(0, SMALL_R))
        loss_ref[...] = acc[LOSS_ROW:LOSS_ROW + 1, 0:1]
        for i, n in enumerate(SMALL):
            w_ref, m_ref, v_ref = params[3 * i:3 * i + 3]
            res = outs[4 * i:4 * i + 4]
            if n == "conv_w":
                gt = chips_sum(land_cols, own_cols, slice(conv_row, conv_row + conv_rows))
                _adamw_apply(gt, w_ref, m_ref, v_ref, *res)
            elif n == "b_gate":
                for half in range(2):
                    cols = slice(half * D, (half + 1) * D)
                    _adamw_apply(acc[gate_row + half:gate_row + half + 1, :], w_ref.at[:, cols], m_ref.at[:, cols],
                                 v_ref.at[:, cols], *[r.at[:, cols] for r in res])
            else:
                _adamw_apply(acc[i:i + 1, :], w_ref, m_ref, v_ref, *res)

    def whole(shape):
        return pl.BlockSpec(shape, lambda i, dev_ref: (0,) * len(shape))

    operands = [a.reshape(shapes[n]) for n in SMALL for a in (wts[n], mom1[n], mom2[n])]
    out_shapes = [shapes[n] for n in SMALL for _ in range(4)]
    outs = pl.pallas_call(
        body, name=name,
        grid_spec=pltpu.PrefetchScalarGridSpec(
            num_scalar_prefetch=1, grid=(1,),
            in_specs=[whole(land.shape), whole((1, SMALL_R, D)),
                      pl.BlockSpec((n_slots, SMALL_R, conv_cols), lambda i, ids_ref: (0, 0, ids_ref[0])),
                      pl.BlockSpec((1, SMALL_R, conv_cols), lambda i, ids_ref: (0, 0, ids_ref[0]))]
            + [whole(a.shape) for a in operands],
            out_specs=[whole((1, 1))] + [whole(s) for s in out_shapes],
            scratch_shapes=[pltpu.VMEM((SMALL_R, D), F32)]),
        out_shape=[jax.ShapeDtypeStruct((1, 1), F32)] + [jax.ShapeDtypeStruct(s, F32) for s in out_shapes],
        compiler_params=_cparams(),
    )(ids, land, own, land, own, *operands)
    small = {n: tuple(o.reshape(wts[n].shape) for o in outs[1 + 4 * i:5 + 4 * i]) for i, n in enumerate(SMALL)}
    return outs[0].reshape(()), small


def _adamw_own(w, land, own, chip, m, v, name, row_block=0, token=None):
    r, c = w.shape
    tr = _pick_tile(r, (256, 176, 64))
    off = row_block * (r // tr)

    def body(chip_ref, w_ref, land_ref, own_ref, m_ref, v_ref, *rest):
        mine = own_ref[0].astype(F32)
        gt = jnp.where(chip_ref[0] == 0, mine, land_ref[0].astype(F32))
        for k in range(1, N_CHIP):
            gt = gt + jnp.where(chip_ref[0] == k, mine, land_ref[k].astype(F32))
        if token is None:
            _adamw_apply(gt, w_ref, m_ref, v_ref, *rest)
        else:
            before, *outs, later = rest
            _adamw_apply(gt + before[0:1, 0:1], w_ref, m_ref, v_ref, *outs)
            later[...] = before[...]

    spec = pl.BlockSpec((tr, c), lambda i, chip_ref: (i, 0))
    in_specs = [spec, pl.BlockSpec((N_CHIP, tr, c), lambda i, chip_ref: (0, i + off, 0)),
                pl.BlockSpec((1, tr, c), lambda i, chip_ref: (chip_ref[0], i + off, 0)), spec, spec]
    operands = (chip, w, land, own, m, v)
    out_specs, out_shape = [spec] * 4, [jax.ShapeDtypeStruct((r, c), F32)] * 4
    if token is not None:
        token_spec = pl.BlockSpec(token.shape, lambda i, chip_ref: (0, 0))
        in_specs.append(token_spec)
        operands += (token,)
        out_specs, out_shape = out_specs + [token_spec], out_shape + [jax.ShapeDtypeStruct(token.shape, token.dtype)]
    return pl.pallas_call(
        body, name=name,
        grid_spec=pltpu.PrefetchScalarGridSpec(
            num_scalar_prefetch=1, grid=(r // tr,), in_specs=in_specs, out_specs=out_specs),
        out_shape=out_shape,
        compiler_params=_cparams(),
    )(*operands)


def _adamw_apply(gt, w_ref, m_ref, v_ref, g_ref, d_ref, nm_ref, nv_ref):
    g_ref[...] = gt
    nm = ADAM_B1 * m_ref[...] + (1.0 - ADAM_B1) * gt
    nv = ADAM_B2 * v_ref[...] + (1.0 - ADAM_B2) * jnp.square(gt)
    m_hat = nm / (1.0 - ADAM_B1 ** ADAM_STEP)
    v_hat = nv / (1.0 - ADAM_B2 ** ADAM_STEP)
    d_ref[...] = -ADAM_LR * (m_hat / (jnp.sqrt(v_hat) + ADAM_EPS) + ADAM_WD * w_ref[...])
    nm_ref[...] = nm
    nv_ref[...] = nv


def _mesh_pos():
    return lax.axis_index("x"), lax.axis_index("y"), lax.axis_index("c")


def _both(first, second):
    n_in, n_out, n_sem = len(first.inputs), len(first.out_shapes), len(first.sem_shapes)

    def run(round_name):
        def both(in_refs, out_refs, sems):
            getattr(first, round_name)(in_refs[:n_in], out_refs[:n_out], sems[:n_sem])
            getattr(second, round_name)(in_refs[n_in:], out_refs[n_out:], sems[n_sem:])
        return both

    return types.SimpleNamespace(
        inputs=first.inputs + second.inputs, out_shapes=first.out_shapes + second.out_shapes,
        sem_shapes=first.sem_shapes + second.sem_shapes, start=run("start"), middle=run("middle"),
        finish=run("finish"))


def _no_round(in_refs, out_refs, sems):
    pass


def _after(token):
    return types.SimpleNamespace(inputs=[token], out_shapes=[], sem_shapes=[], start=_no_round, middle=_no_round,
                                 finish=_no_round)


def _run_exchange(comm, name):
    c_in, c_out = len(comm.inputs), len(comm.out_shapes)

    def body(*refs):
        cins, couts, sems = refs[:c_in], refs[c_in:c_in + c_out], refs[c_in + c_out:]
        comm.start(cins, couts, sems)
        comm.middle(cins, couts, sems)
        comm.finish(cins, couts, sems)

    return list(pl.pallas_call(
        body, name=name, out_shape=list(comm.out_shapes),
        in_specs=[_ANY] * c_in, out_specs=[_ANY] * c_out, scratch_shapes=list(comm.sem_shapes),
    )(*comm.inputs))


def _gather_exchange(shards):
    n_arr = len(shards)

    def plan(x_refs, out_refs, sems):
        send_sems, recv_sems, local_sems = sems[:3]
        stage = sems[3:]
        x, y, c = _mesh_pos()
        me, sibling = (x, y, c), (x, y, 1 - c)
        xn, yn, diag = (1 - x, y), (x, 1 - y), (1 - x, 1 - y)

        def slot(a, px, py, pc, half=None):
            ref = out_refs[a].at[4 * px + 2 * py + pc]
            if half is None:
                return ref
            rows = shards[a].shape[0] // 2
            return ref.at[half * rows:(half + 1) * rows]

        def copy(a, k, block, to, half=None, src=None):
            dst = slot(a, *block, half)
            return pltpu.make_async_remote_copy(
                src_ref=dst if src is None else src, dst_ref=dst,
                send_sem=send_sems.at[a, k], recv_sem=recv_sems.at[a, k],
                device_id=to, device_id_type=pl.DeviceIdType.MESH)

        return types.SimpleNamespace(
            me=me, sibling=sibling, xn=xn, yn=yn, diag=diag, c=c, copy=copy,
            mine_in=[pltpu.make_async_copy(x_refs[a], stage[a], local_sems.at[a, 0]) for a in range(n_arr)],
            mine_out=[pltpu.make_async_copy(stage[a], slot(a, *me), local_sems.at[a, 1]) for a in range(n_arr)],
            first=[cp for a in range(n_arr) for cp in (
                copy(a, 0, me, sibling, src=x_refs[a]), copy(a, 1, me, (*xn, c), src=x_refs[a]),
                copy(a, 2, me, (*yn, c), src=x_refs[a]))],
            second=lambda a: (copy(a, 3, (*xn, c), (*yn, c), half=0), copy(a, 5, (*xn, c), sibling),
                              copy(a, 4, (*yn, c), (*xn, c), half=1), copy(a, 6, (*yn, c), sibling)),
            third=lambda a: (copy(a, 7, (*diag, c), sibling, half=0), copy(a, 8, (*diag, c), sibling, half=1)))

    def start(x_refs, out_refs, sems):
        p = plan(x_refs, out_refs, sems)
        for cp in p.first + p.mine_in:
            cp.start()
        for cp_in, cp_out in zip(p.mine_in, p.mine_out):
            cp_in.wait()
            cp_out.start()

    def middle(x_refs, out_refs, sems):
        p = plan(x_refs, out_refs, sems)
        for a in range(n_arr):
            to_yn, x_to_sib, to_xn, y_to_sib = p.second(a)
            p.copy(a, 1, (*p.xn, p.c), p.me).wait_recv()
            to_yn.start()
            x_to_sib.start()
            p.copy(a, 2, (*p.yn, p.c), p.me).wait_recv()
            to_xn.start()
            y_to_sib.start()

    def finish(x_refs, out_refs, sems):
        p = plan(x_refs, out_refs, sems)
        for a in range(n_arr):
            half0_to_sib, half1_to_sib = p.third(a)
            p.copy(a, 3, (*p.diag, p.c), p.me, half=0).wait_recv()
            half0_to_sib.start()
            p.copy(a, 4, (*p.diag, p.c), p.me, half=1).wait_recv()
            half1_to_sib.start()
        other = 1 - p.c
        for a in range(n_arr):
            p.copy(a, 0, p.sibling, p.me).wait_recv()
            p.copy(a, 5, (*p.xn, other), p.me).wait_recv()
            p.copy(a, 6, (*p.yn, other), p.me).wait_recv()
            p.copy(a, 7, (*p.diag, other), p.me, half=0).wait_recv()
            p.copy(a, 8, (*p.diag, other), p.me, half=1).wait_recv()
        for cp in p.first:
            cp.wait_send()
        for a in range(n_arr):
            for cp in p.second(a) + p.third(a):
                cp.wait_send()
        for cp in p.mine_out:
            cp.wait()

    return types.SimpleNamespace(
        inputs=list(shards), start=start, middle=middle, finish=finish,
        out_shapes=[jax.ShapeDtypeStruct((N_DEV,) + s.shape, s.dtype) for s in shards],
        sem_shapes=[pltpu.SemaphoreType.DMA((n_arr, 9)), pltpu.SemaphoreType.DMA((n_arr, 9)),
                    pltpu.SemaphoreType.DMA((n_arr, 2))] + [pltpu.VMEM(s.shape, s.dtype) for s in shards])


def _pair_exchange(grads):
    n_arr = len(grads)

    def plan(g_refs, land_refs, sems):
        send_sems, recv_sems = sems
        x, y, c = _mesh_pos()
        return [pltpu.make_async_remote_copy(
            src_ref=g_refs[a].at[2 * k + 1 - c], dst_ref=land_refs[a].at[k],
            send_sem=send_sems.at[a, k], recv_sem=recv_sems.at[a, k],
            device_id=(x, y, 1 - c), device_id_type=pl.DeviceIdType.MESH)
            for a in range(n_arr) for k in range(N_CHIP)]

    def start(g_refs, land_refs, sems):
        for cp in plan(g_refs, land_refs, sems):
            cp.start()

    def finish(g_refs, land_refs, sems):
        for cp in plan(g_refs, land_refs, sems):
            cp.wait()

    return types.SimpleNamespace(
        inputs=list(grads), start=start, middle=_no_round, finish=finish,
        out_shapes=[jax.ShapeDtypeStruct((N_CHIP,) + g.shape[1:], g.dtype) for g in grads],
        sem_shapes=[pltpu.SemaphoreType.DMA((n_arr, N_CHIP)), pltpu.SemaphoreType.DMA((n_arr, N_CHIP))])


def _sibling_swap(block):
    def plan(x_ref, out_ref, sems):
        send_sem, recv_sem = sems
        x, y, c = _mesh_pos()
        return pltpu.make_async_remote_copy(
            src_ref=x_ref, dst_ref=out_ref, send_sem=send_sem.at[0], recv_sem=recv_sem.at[0],
            device_id=(x, y, 1 - c), device_id_type=pl.DeviceIdType.MESH)

    def start(x_refs, out_refs, sems):
        plan(x_refs[0], out_refs[0], sems).start()

    def finish(x_refs, out_refs, sems):
        plan(x_refs[0], out_refs[0], sems).wait()

    return types.SimpleNamespace(
        inputs=[block], start=start, middle=_no_round, finish=finish,
        out_shapes=[jax.ShapeDtypeStruct(block.shape, block.dtype)],
        sem_shapes=[pltpu.SemaphoreType.DMA((1,)), pltpu.SemaphoreType.DMA((1,))])


def _chip_exchange(parts):
    n_arr = len(parts)

    def plan(p_refs, land_refs, sems):
        send_sems, recv_sems, local_sems = sems
        x, y, c = _mesh_pos()
        my_chip = 2 * x + y
        chips = [(1 - x, y), (x, 1 - y), (1 - x, 1 - y)]
        local = [pltpu.make_async_copy(p_refs[a].at[my_chip], land_refs[a].at[my_chip], local_sems.at[a])
                 for a in range(n_arr)]

        def copy(a, k, src_slot, dst_slot, px, py):
            return pltpu.make_async_remote_copy(
                src_ref=p_refs[a].at[src_slot], dst_ref=land_refs[a].at[dst_slot],
                send_sem=send_sems.at[a, k], recv_sem=recv_sems.at[a, k],
                device_id=(px, py, c), device_id_type=pl.DeviceIdType.MESH)

        sends = [copy(a, k, 2 * px + py, my_chip, px, py) for a in range(n_arr) for k, (px, py) in enumerate(chips)]
        arrivals = [copy(a, k, my_chip, 2 * px + py, px, py) for a in range(n_arr)
                    for k, (px, py) in enumerate(chips)]
        return local, sends, arrivals

    def start(p_refs, land_refs, sems):
        local, sends, _ = plan(p_refs, land_refs, sems)
        for cp in local + sends:
            cp.start()

    def finish(p_refs, land_refs, sems):
        local, sends, arrivals = plan(p_refs, land_refs, sems)
        for cp in arrivals:
            cp.wait_recv()
        for cp in sends:
            cp.wait_send()
        for cp in local:
            cp.wait()

    return types.SimpleNamespace(
        inputs=list(parts), start=start, middle=_no_round, finish=finish,
        out_shapes=[jax.ShapeDtypeStruct(p.shape, p.dtype) for p in parts],
        sem_shapes=[pltpu.SemaphoreType.DMA((n_arr, 3)), pltpu.SemaphoreType.DMA((n_arr, 3)),
                    pltpu.SemaphoreType.DMA((n_arr,))])


_HBM = pl.BlockSpec(memory_space=pltpu.HBM)
_SEM = pl.BlockSpec(memory_space=pltpu.SEMAPHORE)
_DATAFLOW = pltpu.SideEffectType.DATAFLOW_SIDE_EFFECTING


def _chip_copies(p_refs, land_refs, send_sems, recv_sems):
    x, y, c = _mesh_pos()
    my_chip = 2 * x + y
    chips = [(1 - x, y), (x, 1 - y), (1 - x, 1 - y)]
    return [pltpu.make_async_remote_copy(
        src_ref=p_refs[a].at[2 * px + py], dst_ref=land_refs[a].at[my_chip],
        send_sem=send_sems[3 * a + k], recv_sem=recv_sems[3 * a + k],
        device_id=(px, py, c), device_id_type=pl.DeviceIdType.MESH)
        for a in range(len(p_refs)) for k, (px, py) in enumerate(chips)]


def _chip_exchange_begin(parts, name):
    n_arr = len(parts)
    n_buf, n_copy = 2 * n_arr, 3 * n_arr
    lands = [lax.empty(p.shape, p.dtype) for p in parts]

    def body(*refs):
        p_refs, land_refs = refs[:n_arr], refs[n_arr:n_buf]
        send_sems, recv_sems, token = refs[n_buf:n_buf + n_copy], refs[n_buf + n_copy:n_buf + 2 * n_copy], refs[-1]
        for cp in _chip_copies(p_refs, land_refs, send_sems, recv_sems):
            cp.start()
        token[...] = jnp.zeros_like(token)

    bufs = list(parts) + list(lands)
    outs = pl.pallas_call(
        body, name=name,
        out_shape=(*[pltpu.SemaphoreType.DMA(())] * (2 * n_copy), *[pltpu.HBM(b.shape, b.dtype) for b in bufs],
                   jax.ShapeDtypeStruct((8, 128), F32)),
        in_specs=[_HBM] * n_buf,
        out_specs=(*[_SEM] * (2 * n_copy), *[_HBM] * n_buf, pl.BlockSpec(memory_space=pltpu.VMEM)),
        input_output_aliases={i: 2 * n_copy + i for i in range(n_buf)},
        compiler_params=pltpu.CompilerParams(has_side_effects=_DATAFLOW),
    )(*[pltpu.with_memory_space_constraint(b, pltpu.HBM) for b in bufs])
    sems = list(outs[:2 * n_copy])
    thru = list(outs[2 * n_copy:2 * n_copy + n_buf])
    return types.SimpleNamespace(send_sems=sems[:n_copy], recv_sems=sems[n_copy:], parts=thru[:n_arr],
                                 lands=thru[n_arr:], token=outs[-1])


def _chip_exchange_end(flight, after, name):
    send_sems, recv_sems, parts, lands = flight.send_sems, flight.recv_sems, flight.parts, flight.lands
    n_arr = len(parts)
    n_buf, n_copy = 2 * n_arr, 3 * n_arr

    def body(*refs):
        p_refs, land_refs = refs[:n_arr], refs[n_arr:n_buf]
        sems = refs[n_buf:n_buf + 2 * n_copy]
        for cp in _chip_copies(p_refs, land_refs, sems[:n_copy], sems[n_copy:]):
            cp.wait_send()
            cp.wait_recv()

    bufs = list(parts) + list(lands)
    outs = pl.pallas_call(
        body, name=name, out_shape=tuple(pltpu.HBM(b.shape, b.dtype) for b in bufs),
        in_specs=[_HBM] * n_buf + [_SEM] * (2 * n_copy) + [_ANY], out_specs=tuple([_HBM] * n_buf),
        input_output_aliases={i: i for i in range(n_buf)},
        compiler_params=pltpu.CompilerParams(has_side_effects=_DATAFLOW),
    )(*bufs, *send_sems, *recv_sems, after)
    return list(outs[:n_arr]), list(outs[n_arr:])


def _row_tile(r, cap=640):
    best = None
    for cand in range(16, min(r, cap) + 1, 16):
        if r % cand == 0:
            best = cand
    return best if best is not None else r


def _pair_sum(gs, landeds, core, name):
    tiles = [_row_tile(g.shape[1]) for g in gs]
    counts = [g.shape[1] // tr for g, tr in zip(gs, tiles)]
    n_arr = len(gs)

    def body(core_ref, *refs):
        for a in range(n_arr):
            mine, theirs, out = refs[2 * a], refs[2 * a + 1], refs[2 * n_arr + a]
            out[0] = (mine[0].astype(F32) + theirs[0].astype(F32)).astype(out.dtype)

    in_specs, out_specs, operands = [], [], []
    for g, landed, tr, count in zip(gs, landeds, tiles, counts):
        c_dim = g.shape[2]
        last = count - 1
        in_specs += [pl.BlockSpec((1, tr, c_dim),
                                  lambda k, i, core_ref, last=last: (2 * k + core_ref[0], jnp.minimum(i, last), 0)),
                     pl.BlockSpec((1, tr, c_dim), lambda k, i, core_ref, last=last: (k, jnp.minimum(i, last), 0))]
        out_specs.append(pl.BlockSpec((1, tr, c_dim), lambda k, i, core_ref, last=last: (k, jnp.minimum(i, last), 0)))
        operands += [g, landed]
    return list(pl.pallas_call(
        body, name=name,
        grid_spec=pltpu.PrefetchScalarGridSpec(
            num_scalar_prefetch=1, grid=(N_CHIP, max(counts)), in_specs=in_specs, out_specs=out_specs),
        out_shape=[jax.ShapeDtypeStruct((N_CHIP,) + g.shape[1:], g.dtype) for g in gs],
        compiler_params=_cparams(2),
    )(core, *operands))


GAINS = ("g_ffn1", "g_mix", "g_cross", "g_mem", "g_ffn2", "g_final")
SMALL = GAINS + ("b_gate", "conv_w")
SMALL_R = 16
LOSS_ROW = 11
WEIGHT_ORDER = ("g_ffn1", "w_ffn1_gu", "w_ffn1_down", "g_mix", "w_in", "b_gate", "conv_w", "w_conv_out",
                "w_attn_out", "w_o", "g_cross", "g_mem", "w_cq", "w_ckv", "w_co", "g_ffn2", "w_ffn2_gu",
                "w_ffn2_down", "g_final")
GU_NAMES = ("w_ffn1_gu", "w_ffn2_gu")


def _pack_small(vals, conv_rows):
    rows = [vals[n].reshape(1, D) for n in GAINS] + [vals["b_gate"].reshape(2, D), conv_rows.reshape(CONV_K, D)]
    used = len(GAINS) + 2 + CONV_K
    return jnp.concatenate(rows + [jnp.zeros((SMALL_R - used, D), F32)], axis=0)


def _exchange_shards(wts):
    out = {n: jnp.pad(wts[n].T.astype(BF16), ((0, FF_PAD - FF_BLK), (0, 0))) for n in GU_NAMES}
    for n in ("w_ckv", "w_in", "w_ffn1_down", "w_ffn2_down"):
        out[n] = wts[n].astype(BF16)
    out["mix"] = jnp.concatenate([wts[n].astype(BF16) for n in MIX_MATS], axis=0)
    out["cross"] = jnp.concatenate([wts[n].astype(BF16) for n in CROSS_MATS], axis=0)
    return out


def _reduce_group(grads, landed, core, names):
    return _pair_sum(grads, landed, core, "grads_pair_sum_" + "_".join(names))


def _step(x, mem, target, sh, conv_pad, gains, b_gate, core):
    wg1, wd1, conv_all = _run_exchange(_gather_exchange([sh["w_ffn1_gu"], sh["w_ffn1_down"], conv_pad]), "gather_ffn1")
    conv_w = conv_all[:, :CONV_K, :].transpose(1, 0, 2).reshape(CONV_K, D)
    (n1, gate1, up1, act1, h1), (w_in,) = _ffn_fwd(
        x, gains["g_ffn1"], wg1, wd1, "ffn1_fwd", comm=_gather_exchange([sh["w_in"]]))
    (u, pcg, qkv, yc), (w_mix, wd2) = _inproj_fwd(h1, gains["g_mix"], w_in, conv_w, "inproj_fwd",
                                                  comm=_gather_exchange([sh["mix"], sh["w_ffn2_down"]]))
    (ysb, ctot), (w_cross, w_ckv, wg2) = _sb_fwd(
        qkv, "sb_fwd", comm=_gather_exchange([sh["cross"], sh["w_ckv"], sh["w_ffn2_gu"]]))
    (a_mix, b_mix, merged, h2), _ = _mix_out_fwd(yc, ysb, pcg, b_gate, h1, w_mix, "mix_out_fwd")
    hn, qx, o_x, h3, mn, kv = _cross_fwd(h2, gains["g_cross"], mem, gains["g_mem"], w_ckv, w_cross, "cross_fwd")
    (n4, gate2, up2, act2, dh4, loss, dg_final), _ = _ffn_fwd(h3, gains["g_ffn2"], wg2, wd2, "ffn2_fwd",
                                                              head=(gains["g_final"], target))

    gs = {"g_final": dg_final}
    (dgu2, dh4b, dh3, gs["g_ffn2"]), _ = _ffn_bwd(dh4, h3, gains["g_ffn2"], gate2, up2, wg2, wd2, "ffn2_bwd")
    grads_a = [_mm_tn_rows(dgu2, n4, FF_PAD, "dw_ffn2_gu"),
               _mm_tn_rows(act2, dh4b, FF_BLK, "dw_ffn2_down").reshape(N_DEV, DOWN_ROWS, D)]
    names_a = ["w_ffn2_gu", "w_ffn2_down"]
    (dh3b, dqx, dkv, dh2, gs["g_cross"], gs["g_mem"]), _ = _cross_bwd(
        dh3, h2, gains["g_cross"], qx, kv, mem, gains["g_mem"], w_ckv, w_cross, "cross_bwd")
    grads_b = [_mm_tn_cols(mn, dkv, "dw_ckv"), _mm_tn_squares([(hn, dqx), (o_x, dh3b)], "dw_cross")]
    names_b = ["w_ckv", "cross"]
    (dh2b, da_mix, db_mix, dgp, dconv, dysb, gs["b_gate"], gs["conv_w"]), landed_ab = _mix_out_bwd(
        dh2, a_mix, b_mix, pcg, b_gate, conv_w, w_mix, "mix_out_bwd", comm=_pair_exchange(grads_a + grads_b))
    sums_ab = _reduce_group(grads_a + grads_b, landed_ab, core, names_a + names_b)
    grads_c = [_mm_tn_squares([(yc, da_mix), (ysb, db_mix), (merged, dh2b)], "dw_mix")]
    flight_ab = _chip_exchange_begin(sums_ab, "grads_to_chips_early_begin")
    (dq, dkv_sb), _ = _sb_bwd(qkv, dysb, ctot, flight_ab.token, "sb_bwd")
    grads_d = [_mm_tn_cols_many(u, [dconv, dq[None], dkv_sb, dgp], "dw_in")]
    (dh1, gs["g_mix"]), landed_cd = _inproj_bwd(dconv, dq, dkv_sb, dgp, w_in, h1, gains["g_mix"], dh2, "inproj_bwd",
                                                comm=_pair_exchange(grads_c + grads_d))
    sums_cd = _reduce_group(grads_c + grads_d, landed_cd, core, ["mix", "w_in"])
    flight_d = _chip_exchange_begin(sums_cd, "grads_to_chips_w_in_begin")
    (dgu1, dh1b, dx, gs["g_ffn1"]), _ = _ffn_bwd(dh1, x, gains["g_ffn1"], gate1, up1, wg1, wd1, "ffn1_bwd",
                                                 comm=_after(flight_d.token))
    dw_gu1 = _mm_tn_rows(dgu1, n1, FF_PAD, "dw_ffn1_gu")
    small_mine = _pack_small({n: gs[n] for n in GAINS + ("b_gate",)}, gs["conv_w"][:CONV_K])
    small_mine = small_mine.at[LOSS_ROW, 0].set(loss[0, 0])
    dw_down1, (landed_gu1,) = _mm_tn_rows(act1, dh1b, FF_BLK, "dw_ffn1_down", comm=_pair_exchange([dw_gu1]))
    flight_gu1 = _chip_exchange_begin(_reduce_group([dw_gu1], [landed_gu1], core, ["w_ffn1_gu"]),
                                      "grads_to_chips_ffn1_gu_begin")
    grads_down1 = [dw_down1.reshape(N_DEV, DOWN_ROWS, D)]
    landed_down1, small_sibling = _run_exchange(
        _both(_both(_pair_exchange(grads_down1), _sibling_swap(small_mine)), _after(flight_gu1.token)),
        "grads_to_sibling_ffn1_down")
    small_chip = small_mine + small_sibling
    small_parts = jnp.broadcast_to(small_chip[None], (N_CHIP, SMALL_R, D))
    flight_down1 = _chip_exchange_begin(
        _reduce_group(grads_down1, [landed_down1], core, ["w_ffn1_down"]) + [small_parts],
        "grads_to_chips_ffn1_down_begin")
    flights = [(names_a + names_b, flight_ab), (["mix", "w_in"], flight_d), (["w_ffn1_gu"], flight_gu1),
               (["w_ffn1_down", "small"], flight_down1)]
    return dx, flights


def kernel(x, mem, g_ffn1, w_ffn1_gu, w_ffn1_down, g_mix, w_in, b_gate, conv_w, w_conv_out, w_attn_out, w_o, g_cross, g_mem, w_cq, w_ckv, w_co, g_ffn2, w_ffn2_gu, w_ffn2_down, g_final, loss_target, m_g_ffn1, m_w_ffn1_gu, m_w_ffn1_down, m_g_mix, m_w_in, m_b_gate, m_conv_w, m_w_conv_out, m_w_attn_out, m_w_o, m_g_cross, m_g_mem, m_w_cq, m_w_ckv, m_w_co, m_g_ffn2, m_w_ffn2_gu, m_w_ffn2_down, m_g_final, v_g_ffn1, v_w_ffn1_gu, v_w_ffn1_down, v_g_mix, v_w_in, v_b_gate, v_conv_w, v_w_conv_out, v_w_attn_out, v_w_o, v_g_cross, v_g_mem, v_w_cq, v_w_ckv, v_w_co, v_g_ffn2, v_w_ffn2_gu, v_w_ffn2_down, v_g_final):
    args = locals()
    wts = {n: args[n] for n in WEIGHT_ORDER}
    mom1 = {n: args["m_" + n] for n in WEIGHT_ORDER}
    mom2 = {n: args["v_" + n] for n in WEIGHT_ORDER}
    cx, cy, cc = _mesh_pos()
    dev = 4 * cx + 2 * cy + cc
    conv_cols = D // N_DEV

    conv_pad = jnp.concatenate([conv_w, jnp.zeros((SMALL_R - CONV_K, conv_cols), F32)], axis=0)
    gains = {n: wts[n].reshape(1, D) for n in GAINS}
    dx, flights = _step(x[0], mem[0], loss_target[0], _exchange_shards(wts), conv_pad, gains,
                                 b_gate.reshape(1, 2 * D), cc.reshape(1).astype(jnp.int32))

    grads, delta, new_m, new_v = {}, {}, {}, {}

    def operands(n, transposed):
        trio = (wts[n], mom1[n], mom2[n])
        return tuple(a.T for a in trio) if transposed else trio

    def record(n, res, transposed):
        grads[n], delta[n], new_m[n], new_v[n] = [r.T for r in res] if transposed else res

    early = [("w_ffn2_gu", "w_ffn2_gu", 0, True), ("w_ffn2_down", "w_ffn2_down", 0, False),
             ("w_ckv", "w_ckv", 0, False), ("w_in", "w_in", 0, False)]
    early += [(n, "mix", k, False) for k, n in enumerate(MIX_MATS)]
    early += [(n, "cross", k, False) for k, n in enumerate(CROSS_MATS)]
    chip = (2 * cx + cy).reshape(1).astype(jnp.int32)
    (names_early, flight_early), (names_w_in, flight_w_in), *last_flights = flights
    token = last_flights[-1][1].token
    own, land = {}, {}
    for names, flight, tag in ((names_early, flight_early, "early"), (names_w_in, flight_w_in, "w_in")):
        own_parts, landed = _chip_exchange_end(flight, token, "grads_to_chips_%s_end" % tag)
        own.update(zip(names, own_parts))
        land.update(zip(names, landed))
    for n, buf, row_block, transposed in early:
        w, m1, m2 = operands(n, transposed)
        *res, token = _adamw_own(w, land[buf], own[buf], chip, m1, m2, "adamw_" + n, row_block, token)
        record(n, res, transposed)

    for (names, flight), transposed in zip(last_flights, (True, False)):
        n = names[0]
        own_parts, landed = _chip_exchange_end(flight, token, "grads_to_chips_%s_end" % n)
        w, m1, m2 = operands(n, transposed)
        *res, token = _adamw_own(w, landed[0], own_parts[0], chip, m1, m2, "adamw_" + n, token=token)
        record(n, res, transposed)

    ids = jnp.stack([dev, 2 * cx + cy]).astype(jnp.int32)
    loss, small = _adamw_small(landed[1], own_parts[1], ids, wts, mom1, mom2, "adamw_small")
    for n in SMALL:
        grads[n], delta[n], new_m[n], new_v[n] = small[n]

    return (loss, dx[None], *[grads[n] for n in WEIGHT_ORDER], *[delta[n] for n in WEIGHT_ORDER],
            *[new_m[n] for n in WEIGHT_ORDER], *[new_v[n] for n in WEIGHT_ORDER])
```
